```python
import math
import jax, jax.numpy as jnp
from jax import lax
import numpy as np

D_MODEL = 1024
BATCH = 8
SEQ = 4096
DEPTH = 1

MIX_WIDTH = D_MODEL
HEAD_DIM = 64
GMLP_GROUPS = 8
GMLP_WIDTH = GMLP_GROUPS * HEAD_DIM
CHUNK = 128
N_Q_HEADS = 8
N_KV_HEADS = 2
GQA_GROUP = N_Q_HEADS // N_KV_HEADS
ATTN_WIDTH = N_Q_HEADS * HEAD_DIM
KV_WIDTH = N_KV_HEADS * HEAD_DIM
WINDOW = 128
Q_BLOCK = 128
ROPE_THETA = 500000.0
ROT_DIM = HEAD_DIM // 4
D_FF = 4 * D_MODEL
IN_PROJ_WIDTH = 2 * GMLP_WIDTH + ATTN_WIDTH + 2 * KV_WIDTH
N_MOD = 6
EPS = 1e-5

kernel_name = "hybrid_gmlp_swa_sink_block"


def rms_norm(x, g):
    xf = x.astype(jnp.float32)
    y = xf * lax.rsqrt(jnp.mean(xf * xf, axis=-1, keepdims=True) + EPS)
    return (y * g.astype(jnp.float32)).astype(x.dtype)


def modulate(h, shift, scale):
    return h * (1 + scale[:, None, :]) + shift[:, None, :]


def partial_rope(t, positions):
    half = ROT_DIM // 2
    inv_freq = ROPE_THETA ** (-jnp.arange(0, ROT_DIM, 2, dtype=jnp.float32) / ROT_DIM)
    ang = positions.astype(jnp.float32)[..., None] * inv_freq
    cos = jnp.cos(ang)[:, :, None, :].astype(t.dtype)
    sin = jnp.sin(ang)[:, :, None, :].astype(t.dtype)
    t1 = t[..., :half]
    t2 = t[..., half:ROT_DIM]
    return jnp.concatenate([t1 * cos - t2 * sin, t2 * cos + t1 * sin, t[..., ROT_DIM:]], axis=-1)


def chunked_sgu(z, w_s, b_s):
    B, S, _ = z.shape
    n_chunks = S // CHUNK
    u, v = jnp.split(z, 2, axis=-1)
    v = v.reshape(B, n_chunks, CHUNK, GMLP_GROUPS, HEAD_DIM)
    causal = jnp.tril(jnp.ones((CHUNK, CHUNK), dtype=w_s.dtype))
    w = w_s * causal[None]
    sv = jnp.einsum('hts,bcshd->bcthd', w, v) + jnp.transpose(b_s)[None, None, :, :, None]
    return u * sv.reshape(B, S, GMLP_WIDTH)


def sliding_window_sink_attention(q, k, v, sinks, positions):
    B, S, _, _ = q.shape
    nb = S // Q_BLOCK
    q = partial_rope(q, positions)
    k = partial_rope(k, positions)
    qb = q.reshape(B, nb, Q_BLOCK, N_KV_HEADS, GQA_GROUP, HEAD_DIM)
    pad = jnp.zeros((B, Q_BLOCK, N_KV_HEADS, HEAD_DIM), k.dtype)
    kp = jnp.concatenate([pad, k], axis=1)
    vp = jnp.concatenate([pad, v], axis=1)
    kb = jnp.concatenate([kp[:, :S].reshape(B, nb, Q_BLOCK, N_KV_HEADS, HEAD_DIM),
                          k.reshape(B, nb, Q_BLOCK, N_KV_HEADS, HEAD_DIM)], axis=2)
    vb = jnp.concatenate([vp[:, :S].reshape(B, nb, Q_BLOCK, N_KV_HEADS, HEAD_DIM),
                          v.reshape(B, nb, Q_BLOCK, N_KV_HEADS, HEAD_DIM)], axis=2)
    scores = jnp.einsum('bnqhgd,bnkhd->bnhgqk', qb, kb).astype(jnp.float32) / math.sqrt(HEAD_DIM)
    qi = jnp.arange(Q_BLOCK)[:, None]
    kj = jnp.arange(2 * Q_BLOCK)[None, :]
    band = (kj > qi) & (kj <= qi + WINDOW)
    valid_first = jnp.arange(nb)[:, None, None] > 0
    mask = band[None] & (valid_first | (kj >= Q_BLOCK)[None])
    mask = mask[None, :, None, None]
    scores = jnp.where(mask, scores, -jnp.inf)
    sink = sinks.astype(jnp.float32).reshape(N_KV_HEADS, GQA_GROUP)[None, None, :, :, None, None]
    m = jnp.maximum(jnp.max(scores, axis=-1, keepdims=True), sink)
    p = jnp.exp(scores - m)
    denom = jnp.sum(p, axis=-1, keepdims=True) + jnp.exp(sink - m)
    probs = (p / denom).astype(v.dtype)
    out = jnp.einsum('bnhgqk,bnkhd->bnqhgd', probs, vb)
    return out.reshape(B, S, ATTN_WIDTH)


def _fwd_setup_inputs(seed: int = 0) -> dict:
    key = jax.random.key(seed)
    ks = jax.random.split(key, 16)
    f32 = jnp.float32
    x = jax.random.normal(ks[0], (BATCH, SEQ, D_MODEL), f32)
    c = jax.random.normal(ks[1], (BATCH, D_MODEL), f32)
    offsets = jax.random.randint(ks[2], (BATCH, 1), 0, 2048, dtype=jnp.int32)
    positions = offsets + jnp.arange(SEQ, dtype=jnp.int32)[None, :]
    w_ada = jax.random.normal(ks[3], (DEPTH, D_MODEL, N_MOD * D_MODEL), f32) * (0.5 * D_MODEL ** -0.5)
    b_ada = jax.random.normal(ks[4], (DEPTH, N_MOD * D_MODEL), f32) * 0.02
    g_mix = 1.0 + 0.02 * jax.random.normal(ks[5], (DEPTH, D_MODEL), f32)
    w_in = jax.random.normal(ks[6], (DEPTH, D_MODEL, IN_PROJ_WIDTH), f32) * D_MODEL ** -0.5
    w_spatial = jax.random.normal(ks[7], (DEPTH, GMLP_GROUPS, CHUNK, CHUNK), f32) * CHUNK ** -0.5
    b_spatial = 1.0 + 0.01 * jax.random.normal(ks[8], (DEPTH, GMLP_GROUPS, CHUNK), f32)
    sinks = jax.random.normal(ks[9], (DEPTH, N_Q_HEADS), f32) * 0.5
    w_out = jax.random.normal(ks[10], (DEPTH, MIX_WIDTH, D_MODEL), f32) * MIX_WIDTH ** -0.5
    g_ffn = 1.0 + 0.02 * jax.random.normal(ks[11], (DEPTH, D_MODEL), f32)
    w_ff1 = jax.random.normal(ks[12], (DEPTH, D_MODEL, D_FF), f32) * D_MODEL ** -0.5
    w_ff2 = jax.random.normal(ks[13], (DEPTH, D_FF, D_MODEL), f32) * D_FF ** -0.5
    g_final = 1.0 + 0.02 * jax.random.normal(ks[14], (D_MODEL,), f32)
    return {"x": x, "c": c, "positions": positions, "w_ada": w_ada, "b_ada": b_ada,
            "g_mix": g_mix, "w_in": w_in, "w_spatial": w_spatial, "b_spatial": b_spatial,
            "sinks": sinks, "w_out": w_out, "g_ffn": g_ffn, "w_ff1": w_ff1, "w_ff2": w_ff2,
            "g_final": g_final}


def _fwd_reference(x, c, positions, w_ada, b_ada, g_mix, w_in, w_spatial, b_spatial, sinks,
              w_out, g_ffn, w_ff1, w_ff2, g_final):
    B, S, _ = x.shape
    c_act = jax.nn.silu(c)
    for l in range(DEPTH):
        mod = c_act @ w_ada[l] + b_ada[l]
        shift1, scale1, gate1, shift2, scale2, gate2 = jnp.split(mod, N_MOD, axis=-1)

        h = modulate(rms_norm(x, g_mix[l]), shift1, scale1)
        proj = h @ w_in[l]
        z_a = proj[..., :2 * GMLP_WIDTH]
        o = 2 * GMLP_WIDTH
        q = proj[..., o:o + ATTN_WIDTH].reshape(B, S, N_Q_HEADS, HEAD_DIM)
        o += ATTN_WIDTH
        k = proj[..., o:o + KV_WIDTH].reshape(B, S, N_KV_HEADS, HEAD_DIM)
        o += KV_WIDTH
        v = proj[..., o:o + KV_WIDTH].reshape(B, S, N_KV_HEADS, HEAD_DIM)

        out_a = chunked_sgu(jax.nn.gelu(z_a), w_spatial[l], b_spatial[l])
        out_b = sliding_window_sink_attention(q, k, v, sinks[l], positions)
        mix = jnp.concatenate([out_a, out_b], axis=-1) @ w_out[l]
        x = x + gate1[:, None, :] * mix

        h2 = modulate(rms_norm(x, g_ffn[l]), shift2, scale2)
        ff = jnp.square(jax.nn.relu(h2 @ w_ff1[l])) @ w_ff2[l]
        x = x + gate2[:, None, :] * ff
    return rms_norm(x, g_final)


import jax as _jax
import jax.numpy as _jnp

TWIN_FORMAT = 'train_step'
FWD_PARAMS = ['x', 'c', 'positions', 'w_ada', 'b_ada', 'g_mix', 'w_in', 'w_spatial', 'b_spatial', 'sinks', 'w_out', 'g_ffn', 'w_ff1', 'w_ff2', 'g_final']
TWIN_WEIGHTS = ['w_ada', 'b_ada', 'g_mix', 'w_in', 'w_spatial', 'b_spatial', 'sinks', 'w_out', 'g_ffn', 'w_ff1', 'w_ff2', 'g_final']
TWIN_DIFF_INPUT = 'x'
TWIN_INPUTS = ['x', 'c', 'positions', 'w_ada', 'b_ada', 'g_mix', 'w_in', 'w_spatial', 'b_spatial', 'sinks', 'w_out', 'g_ffn', 'w_ff1', 'w_ff2', 'g_final', 'loss_target', 'm_w_ada', 'm_b_ada', 'm_g_mix', 'm_w_in', 'm_w_spatial', 'm_b_spatial', 'm_sinks', 'm_w_out', 'm_g_ffn', 'm_w_ff1', 'm_w_ff2', 'm_g_final', 'v_w_ada', 'v_b_ada', 'v_g_mix', 'v_w_in', 'v_w_spatial', 'v_b_spatial', 'v_sinks', 'v_w_out', 'v_g_ffn', 'v_w_ff1', 'v_w_ff2', 'v_g_final']
TWIN_OUTPUTS = ['loss', 'grad_x', 'grad_w_ada', 'grad_b_ada', 'grad_g_mix', 'grad_w_in', 'grad_w_spatial', 'grad_b_spatial', 'grad_sinks', 'grad_w_out', 'grad_g_ffn', 'grad_w_ff1', 'grad_w_ff2', 'grad_g_final', 'delta_w_ada', 'delta_b_ada', 'delta_g_mix', 'delta_w_in', 'delta_w_spatial', 'delta_b_spatial', 'delta_sinks', 'delta_w_out', 'delta_g_ffn', 'delta_w_ff1', 'delta_w_ff2', 'delta_g_final', 'new_m_w_ada', 'new_m_b_ada', 'new_m_g_mix', 'new_m_w_in', 'new_m_w_spatial', 'new_m_b_spatial', 'new_m_sinks', 'new_m_w_out', 'new_m_g_ffn', 'new_m_w_ff1', 'new_m_w_ff2', 'new_m_g_final', 'new_v_w_ada', 'new_v_b_ada', 'new_v_g_mix', 'new_v_w_in', 'new_v_w_spatial', 'new_v_b_spatial', 'new_v_sinks', 'new_v_w_out', 'new_v_g_ffn', 'new_v_w_ff1', 'new_v_w_ff2', 'new_v_g_final']
TWIN_LEAF_KINDS = {'loss': 'loss', 'grad_x': 'grad_x', 'grad_w_ada': 'grad_w', 'grad_b_ada': 'grad_w', 'grad_g_mix': 'grad_w', 'grad_w_in': 'grad_w', 'grad_w_spatial': 'grad_w', 'grad_b_spatial': 'grad_w', 'grad_sinks': 'grad_w', 'grad_w_out': 'grad_w', 'grad_g_ffn': 'grad_w', 'grad_w_ff1': 'grad_w', 'grad_w_ff2': 'grad_w', 'grad_g_final': 'grad_w', 'delta_w_ada': 'delta_w', 'delta_b_ada': 'delta_w', 'delta_g_mix': 'delta_w', 'delta_w_in': 'delta_w', 'delta_w_spatial': 'delta_w', 'delta_b_spatial': 'delta_w', 'delta_sinks': 'delta_w', 'delta_w_out': 'delta_w', 'delta_g_ffn': 'delta_w', 'delta_w_ff1': 'delta_w', 'delta_w_ff2': 'delta_w', 'delta_g_final': 'delta_w', 'new_m_w_ada': 'new_m', 'new_m_b_ada': 'new_m', 'new_m_g_mix': 'new_m', 'new_m_w_in': 'new_m', 'new_m_w_spatial': 'new_m', 'new_m_b_spatial': 'new_m', 'new_m_sinks': 'new_m', 'new_m_w_out': 'new_m', 'new_m_g_ffn': 'new_m', 'new_m_w_ff1': 'new_m', 'new_m_w_ff2': 'new_m', 'new_m_g_final': 'new_m', 'new_v_w_ada': 'new_v', 'new_v_b_ada': 'new_v', 'new_v_g_mix': 'new_v', 'new_v_w_in': 'new_v', 'new_v_w_spatial': 'new_v', 'new_v_b_spatial': 'new_v', 'new_v_sinks': 'new_v', 'new_v_w_out': 'new_v', 'new_v_g_ffn': 'new_v', 'new_v_w_ff1': 'new_v', 'new_v_w_ff2': 'new_v', 'new_v_g_final': 'new_v'}


def _forward(args):
    return _fwd_reference(*[args[k] for k in FWD_PARAMS])


def _output_shape():
    out = _jax.eval_shape(lambda: _forward(_fwd_setup_inputs(0)))
    return out.shape, out.dtype

N_MICROBATCH = 1
ADAM_LR = 0.001
ADAM_B1 = 0.9
ADAM_B2 = 0.999
ADAM_EPS = 1e-08
ADAM_WD = 0.01
ADAM_STEP = 10
PER_EXAMPLE_BATCH_AXIS = {'x': 0, 'c': 0, 'positions': 0, 'loss_target': 0}
SHARED_INPUTS = []
_WEIGHT_DTYPES = {'w_ada': _jnp.float32, 'b_ada': _jnp.float32, 'g_mix': _jnp.float32, 'w_in': _jnp.float32, 'w_spatial': _jnp.float32, 'b_spatial': _jnp.float32, 'sinks': _jnp.float32, 'w_out': _jnp.float32, 'g_ffn': _jnp.float32, 'w_ff1': _jnp.float32, 'w_ff2': _jnp.float32, 'g_final': _jnp.float32}
MOMENT_SCALE = {'w_ada': 8.848899e-02, 'b_ada': 1.541630e-01, 'g_mix': 3.843398e-02, 'w_in': 3.112073e-02, 'w_spatial': 1.459022e-02, 'b_spatial': 2.790407e-02, 'sinks': 8.720712e-03, 'w_out': 3.526196e-02, 'g_ffn': 8.230853e-02, 'w_ff1': 4.260813e-02, 'w_ff2': 7.612007e-02, 'g_final': 3.226397e+01}


def _to_microbatches(a, axis):
    t = _jnp.moveaxis(a, axis, 0)
    t = t.reshape((N_MICROBATCH, t.shape[0] // N_MICROBATCH) + t.shape[1:])
    return _jnp.moveaxis(t, 1, axis + 1)


def setup_inputs(seed: int = 0) -> dict:
    inp = _fwd_setup_inputs(seed)
    key = _jax.random.fold_in(_jax.random.key(seed), 7919)
    shape, _ = _output_shape()
    out = dict(inp)
    out["loss_target"] = _jax.random.normal(_jax.random.fold_in(key, 0), shape, _jnp.float32)
    for i, name in enumerate(TWIN_WEIGHTS):
        w = inp[name].astype(_jnp.float32)
        if MOMENT_SCALE is None:
            s = _jnp.sqrt(_jnp.mean(_jnp.square(w)) + 1e-30)
        else:
            s = MOMENT_SCALE[name]
        km, kv = _jax.random.split(_jax.random.fold_in(key, i + 1))
        out[name] = w
        out["m_" + name] = s * _jax.random.normal(km, w.shape, _jnp.float32)
        out["v_" + name] = (s * s) * _jax.random.uniform(kv, w.shape, _jnp.float32, 0.5, 1.5)
    if N_MICROBATCH > 1:
        for name, axis in PER_EXAMPLE_BATCH_AXIS.items():
            out[name] = _to_microbatches(out[name], axis)
    return {'x': out['x'], 'c': out['c'], 'positions': out['positions'], 'w_ada': out['w_ada'], 'b_ada': out['b_ada'], 'g_mix': out['g_mix'], 'w_in': out['w_in'], 'w_spatial': out['w_spatial'], 'b_spatial': out['b_spatial'], 'sinks': out['sinks'], 'w_out': out['w_out'], 'g_ffn': out['g_ffn'], 'w_ff1': out['w_ff1'], 'w_ff2': out['w_ff2'], 'g_final': out['g_final'], 'loss_target': out['loss_target'], 'm_w_ada': out['m_w_ada'], 'm_b_ada': out['m_b_ada'], 'm_g_mix': out['m_g_mix'], 'm_w_in': out['m_w_in'], 'm_w_spatial': out['m_w_spatial'], 'm_b_spatial': out['m_b_spatial'], 'm_sinks': out['m_sinks'], 'm_w_out': out['m_w_out'], 'm_g_ffn': out['m_g_ffn'], 'm_w_ff1': out['m_w_ff1'], 'm_w_ff2': out['m_w_ff2'], 'm_g_final': out['m_g_final'], 'v_w_ada': out['v_w_ada'], 'v_b_ada': out['v_b_ada'], 'v_g_mix': out['v_g_mix'], 'v_w_in': out['v_w_in'], 'v_w_spatial': out['v_w_spatial'], 'v_b_spatial': out['v_b_spatial'], 'v_sinks': out['v_sinks'], 'v_w_out': out['v_w_out'], 'v_g_ffn': out['v_g_ffn'], 'v_w_ff1': out['v_w_ff1'], 'v_w_ff2': out['v_w_ff2'], 'v_g_final': out['v_g_final']}


def _loss(weights, diff, rest, loss_target):
    with _jax.named_scope("forward"):
        args = {**rest, TWIN_DIFF_INPUT: diff, **{k: w.astype(_WEIGHT_DTYPES[k]) for k, w in weights.items()}}
        y = _forward(args)
    with _jax.named_scope("loss_head"):
        err = _jnp.square(y.astype(_jnp.float32) - loss_target)
        return 0.5 * _jnp.sum(_jnp.mean(err, axis=-1)) if err.ndim else 0.5 * err


def _adamw(w, g, m, v):
    m = ADAM_B1 * m + (1.0 - ADAM_B1) * g
    v = ADAM_B2 * v + (1.0 - ADAM_B2) * _jnp.square(g)
    m_hat = m / (1.0 - ADAM_B1 ** ADAM_STEP)
    v_hat = v / (1.0 - ADAM_B2 ** ADAM_STEP)
    delta = -ADAM_LR * (m_hat / (_jnp.sqrt(v_hat) + ADAM_EPS) + ADAM_WD * w)
    return delta, m, v


def reference(x, c, positions, w_ada, b_ada, g_mix, w_in, w_spatial, b_spatial, sinks, w_out, g_ffn, w_ff1, w_ff2, g_final, loss_target, m_w_ada, m_b_ada, m_g_mix, m_w_in, m_w_spatial, m_b_spatial, m_sinks, m_w_out, m_g_ffn, m_w_ff1, m_w_ff2, m_g_final, v_w_ada, v_b_ada, v_g_mix, v_w_in, v_w_spatial, v_b_spatial, v_sinks, v_w_out, v_g_ffn, v_w_ff1, v_w_ff2, v_g_final):
    given = dict(x=x, c=c, positions=positions, w_ada=w_ada, b_ada=b_ada, g_mix=g_mix, w_in=w_in, w_spatial=w_spatial, b_spatial=b_spatial, sinks=sinks, w_out=w_out, g_ffn=g_ffn, w_ff1=w_ff1, w_ff2=w_ff2, g_final=g_final, loss_target=loss_target, m_w_ada=m_w_ada, m_b_ada=m_b_ada, m_g_mix=m_g_mix, m_w_in=m_w_in, m_w_spatial=m_w_spatial, m_b_spatial=m_b_spatial, m_sinks=m_sinks, m_w_out=m_w_out, m_g_ffn=m_g_ffn, m_w_ff1=m_w_ff1, m_w_ff2=m_w_ff2, m_g_final=m_g_final, v_w_ada=v_w_ada, v_b_ada=v_b_ada, v_g_mix=v_g_mix, v_w_in=v_w_in, v_w_spatial=v_w_spatial, v_b_spatial=v_b_spatial, v_sinks=v_sinks, v_w_out=v_w_out, v_g_ffn=v_g_ffn, v_w_ff1=v_w_ff1, v_w_ff2=v_w_ff2, v_g_final=v_g_final)
    weights = {n: given[n] for n in TWIN_WEIGHTS}
    shared = {n: given[n] for n in SHARED_INPUTS}
    per_example = {n: given[n] for n in ['x', 'c', 'positions']}
    grad_fn = _jax.value_and_grad(_loss, argnums=(0, 1))

    def one_microbatch(ex, loss_target):
        ex = dict(ex)
        diff = ex.pop(TWIN_DIFF_INPUT)
        return grad_fn(weights, diff, {**shared, **ex}, loss_target)

    if N_MICROBATCH == 1:
        loss, (grad_w, grad_x) = one_microbatch(per_example, given["loss_target"])
    else:
        def body(carry, xs):
            loss_sum, grad_sum = carry
            l_k, (gw_k, gx_k) = one_microbatch(xs[0], xs[1])
            with _jax.named_scope("update"):
                return (loss_sum + l_k, _jax.tree.map(_jnp.add, grad_sum, gw_k)), gx_k

        init = (_jnp.zeros((), _jnp.float32), _jax.tree.map(_jnp.zeros_like, weights))
        (loss, grad_w), grad_x = _jax.lax.scan(body, init, (per_example, given["loss_target"]))
    with _jax.named_scope("update"):
        delta_w, new_m, new_v = {}, {}, {}
        for n in TWIN_WEIGHTS:
            delta_w[n], new_m[n], new_v[n] = _adamw(weights[n], grad_w[n], given["m_" + n], given["v_" + n])
    return (loss, grad_x, *[grad_w[n] for n in TWIN_WEIGHTS], *[delta_w[n] for n in TWIN_WEIGHTS],
            *[new_m[n] for n in TWIN_WEIGHTS], *[new_v[n] for n in TWIN_WEIGHTS])
```

```python
import functools
import math

import jax
import jax.numpy as jnp
from jax import lax
from jax.experimental import pallas as pl
from jax.experimental.pallas import tpu as pltpu

F32 = jnp.float32
MXU_DTYPE = jnp.bfloat16
WEIGHT_COMM_DTYPE = jnp.bfloat16
GRAD_COMM_DTYPE = jnp.float32

D_MODEL = 1024
D_FF = 4096
HEAD_DIM = 64
GMLP_GROUPS = 8
GMLP_WIDTH = 512
CHUNK = 128
N_Q_HEADS = 8
N_KV_HEADS = 2
ATTN_WIDTH = 512
KV_WIDTH = 128
ROT_DIM = 16
ROPE_THETA = 500000.0
IN_PROJ_WIDTH = 1792
N_MOD = 6
EPS = 1e-5
N_CHIPS = 4
N_DEV = 8
LANES = 128

ADAM_LR = 0.001
ADAM_B1 = 0.9
ADAM_B2 = 0.999
ADAM_EPS = 1e-08
ADAM_WD = 0.01
ADAM_STEP = 10

VMEM_LIMIT_BYTES = 58 * 1024 * 1024
MESH = pl.DeviceIdType.MESH


def _params(*semantics):
    return pltpu.CompilerParams(dimension_semantics=semantics, vmem_limit_bytes=VMEM_LIMIT_BYTES)


def _dot(a, b):
    return jnp.dot(a.astype(MXU_DTYPE), b.astype(MXU_DTYPE), preferred_element_type=F32)


def _dot_nt(a, b):
    return lax.dot_general(a.astype(MXU_DTYPE), b.astype(MXU_DTYPE), (((1,), (1,)), ((), ())),
                           preferred_element_type=F32)


def _dot_tn(a, b):
    return lax.dot_general(a.astype(MXU_DTYPE), b.astype(MXU_DTYPE), (((0,), (0,)), ((), ())),
                           preferred_element_type=F32)


def _full(shape):
    return pl.BlockSpec(shape, lambda *_: (0,) * len(shape))


def _any():
    return pl.BlockSpec(memory_space=pl.ANY)


def _rowsum(v):
    return jnp.sum(v, axis=0, keepdims=True)


def _mean_last(v):
    return jnp.mean(v, axis=-1, keepdims=True)


def _all_gather8(block, name, split=False):
    blk_shape = block.shape[1:] if split else block.shape

    def body(x_ref, out_ref, send_sems, recv_sems, local_sem):
        x, y, c = lax.axis_index("x"), lax.axis_index("y"), lax.axis_index("c")
        me, sibling = (x, y, c), (x, y, 1 - c)
        chips = [(1 - x, y), (x, 1 - y), (1 - x, 1 - y)]
        src_mine = x_ref.at[c] if split else x_ref

        def slot(px, py, pc):
            return out_ref.at[4 * px + 2 * py + pc]

        def copy(k, blk, to, src=None):
            return pltpu.make_async_remote_copy(
                src_ref=slot(*blk) if src is None else src, dst_ref=slot(*blk),
                send_sem=send_sems.at[k], recv_sem=recv_sems.at[k],
                device_id=to, device_id_type=MESH)

        mine = pltpu.make_async_copy(src_mine, slot(*me), local_sem)
        mine.start()
        first = [copy(0, me, sibling, src=src_mine)]
        first += [copy(1 + j, me, (*chip, c), src=src_mine) for j, chip in enumerate(chips)]
        for cp in first:
            cp.start()
        passed = [copy(4 + j, (*chip, c), sibling) for j, chip in enumerate(chips)]
        for j, chip in enumerate(chips):
            copy(1 + j, (*chip, c), me).wait_recv()
            passed[j].start()
        copy(0, sibling, me).wait_recv()
        for j, chip in enumerate(chips):
            copy(4 + j, (*chip, 1 - c), me).wait_recv()
        for cp in first + passed:
            cp.wait_send()
        mine.wait()

    return pl.pallas_call(
        body, name=name,
        out_shape=jax.ShapeDtypeStruct((N_DEV,) + tuple(blk_shape), block.dtype),
        in_specs=[_any()], out_specs=_any(),
        scratch_shapes=[pltpu.SemaphoreType.DMA((7,)), pltpu.SemaphoreType.DMA((7,)),
                        pltpu.SemaphoreType.DMA],
    )(block)


def _sibling_exchange_halves(parts, name):
    n_arr = len(parts)

    def body(*refs):
        in_refs = refs[:n_arr]
        out_refs = refs[n_arr:2 * n_arr]
        send_sems, recv_sems = refs[2 * n_arr:]
        x, y, c = lax.axis_index("x"), lax.axis_index("y"), lax.axis_index("c")
        copies = []
        for a in range(n_arr):
            for k in range(N_CHIPS):
                copies.append(pltpu.make_async_remote_copy(
                    src_ref=in_refs[a].at[k, 1 - c], dst_ref=out_refs[a].at[k],
                    send_sem=send_sems.at[a * N_CHIPS + k], recv_sem=recv_sems.at[a * N_CHIPS + k],
                    device_id=(x, y, 1 - c), device_id_type=MESH))
        for cp in copies:
            cp.start()
        for cp in copies:
            cp.wait()

    return pl.pallas_call(
        body, name=name,
        out_shape=[jax.ShapeDtypeStruct((N_CHIPS,) + p.shape[2:], p.dtype) for p in parts],
        in_specs=[_any()] * n_arr, out_specs=[_any()] * n_arr,
        scratch_shapes=[pltpu.SemaphoreType.DMA((n_arr * N_CHIPS,)),
                        pltpu.SemaphoreType.DMA((n_arr * N_CHIPS,))],
    )(*parts)


def _chip_scatter(halves, name):
    n_arr = len(halves)

    def body(*refs):
        in_refs = refs[:n_arr]
        out_refs = refs[n_arr:2 * n_arr]
        send_sems, recv_sems, local_sems = refs[2 * n_arr:]
        x, y, c = lax.axis_index("x"), lax.axis_index("y"), lax.axis_index("c")
        chips = [(1 - x, y), (x, 1 - y), (1 - x, 1 - y)]
        copies, local = [], []
        for a in range(n_arr):
            for j, (px, py) in enumerate(chips):
                copies.append(pltpu.make_async_remote_copy(
                    src_ref=in_refs[a].at[2 * px + py], dst_ref=out_refs[a].at[j],
                    send_sem=send_sems.at[a * 3 + j], recv_sem=recv_sems.at[a * 3 + j],
                    device_id=(px, py, c), device_id_type=MESH))
            local.append(pltpu.make_async_copy(in_refs[a].at[2 * x + y], out_refs[a].at[3],
                                               local_sems.at[a]))
        for cp in copies + local:
            cp.start()
        for cp in copies + local:
            cp.wait()

    return pl.pallas_call(
        body, name=name,
        out_shape=[jax.ShapeDtypeStruct(h.shape, h.dtype) for h in halves],
        in_specs=[_any()] * n_arr, out_specs=[_any()] * n_arr,
        scratch_shapes=[pltpu.SemaphoreType.DMA((n_arr * 3,)), pltpu.SemaphoreType.DMA((n_arr * 3,)),
                        pltpu.SemaphoreType.DMA((n_arr,))],
    )(*halves)


def _sibling_share(halves, name):
    n_arr = len(halves)

    def body(*refs):
        in_refs = refs[:n_arr]
        out_refs = refs[n_arr:2 * n_arr]
        send_sems, recv_sems, local_sems = refs[2 * n_arr:]
        x, y, c = lax.axis_index("x"), lax.axis_index("y"), lax.axis_index("c")
        copies = []
        for a in range(n_arr):
            copies.append(pltpu.make_async_remote_copy(
                src_ref=in_refs[a], dst_ref=out_refs[a].at[c],
                send_sem=send_sems.at[a], recv_sem=recv_sems.at[a],
                device_id=(x, y, 1 - c), device_id_type=MESH))
        local = [pltpu.make_async_copy(in_refs[a], out_refs[a].at[c], local_sems.at[a])
                 for a in range(n_arr)]
        for cp in copies + local:
            cp.start()
        for a in range(n_arr):
            pltpu.make_async_remote_copy(
                src_ref=in_refs[a], dst_ref=out_refs[a].at[1 - c],
                send_sem=send_sems.at[a], recv_sem=recv_sems.at[a],
                device_id=(x, y, 1 - c), device_id_type=MESH).wait()
        for cp in local:
            cp.wait()

    return pl.pallas_call(
        body, name=name,
        out_shape=[jax.ShapeDtypeStruct((2,) + h.shape, h.dtype) for h in halves],
        in_specs=[_any()] * n_arr, out_specs=[_any()] * n_arr,
        scratch_shapes=[pltpu.SemaphoreType.DMA((n_arr,)), pltpu.SemaphoreType.DMA((n_arr,)),
                        pltpu.SemaphoreType.DMA((n_arr,))],
    )(*halves)


def _gelu_tanh(z):
    k = math.sqrt(2.0 / math.pi)
    t = jnp.tanh(k * (z + 0.044715 * (z * z * z)))
    return 0.5 * z * (1.0 + t), t


def _gelu_tanh_grad(z, t):
    k = math.sqrt(2.0 / math.pi)
    return 0.5 * (1.0 + t) + 0.5 * z * (1.0 - t * t) * (k * (1.0 + 3.0 * 0.044715 * (z * z)))


def _rope_tables(pos_col, invf_row):
    ang = pos_col.astype(F32) * invf_row
    cos, sin = jnp.cos(ang), jnp.sin(ang)
    d = lax.broadcasted_iota(jnp.int32, ang.shape, 1) & (HEAD_DIM - 1)
    half = ROT_DIM // 2
    c_tab = jnp.where(d < ROT_DIM, cos, 1.0)
    s1 = jnp.where(d < half, -sin, 0.0)
    s2 = jnp.where((d >= half) & (d < ROT_DIM), sin, 0.0)
    return c_tab, s1, s2


def _rope_apply(t, tabs, sign):
    c_tab, s1, s2 = tabs
    reps = t.shape[1] // LANES
    if reps > 1:
        c_tab, s1, s2 = (jnp.tile(v, (1, reps)) for v in (c_tab, s1, s2))
    half = ROT_DIM // 2
    up = pltpu.roll(t, t.shape[1] - half, 1)
    down = pltpu.roll(t, half, 1)
    return t * c_tab + sign * (up * s1 + down * s2)


def _lane_masks(shape):
    lane = lax.broadcasted_iota(jnp.int32, shape, 1)
    return lane < HEAD_DIM, lane >= HEAD_DIM


def _attn_mask(first_block):
    qi = lax.broadcasted_iota(jnp.int32, (CHUNK, 2 * CHUNK), 0)
    kj = lax.broadcasted_iota(jnp.int32, (CHUNK, 2 * CHUNK), 1)
    band = (kj > qi) & (kj <= qi + CHUNK)
    return band & (jnp.logical_not(first_block) | (kj >= CHUNK))


def _head_plan():
    plan = []
    for h in range(N_Q_HEADS):
        pair, hi = h // 2, h % 2
        group = h // (N_Q_HEADS // N_KV_HEADS)
        plan.append((pair, hi, (group == 1) != (hi == 1)))
    return plan


def _softmax_sink(s, sink):
    m = jnp.maximum(jnp.max(s, axis=-1, keepdims=True), sink)
    p = jnp.exp(s - m)
    e_sink = jnp.exp(sink - m)
    den = jnp.sum(p, axis=-1, keepdims=True) + e_sink
    return p / den, e_sink / den


def _sgu_forward_pair(wm, vp, j):
    lo, hi = _lane_masks(vp.shape)
    lhs = jnp.concatenate([wm[2 * j], wm[2 * j + 1]], axis=1)
    rhs = jnp.concatenate([jnp.where(lo, vp, 0.0), jnp.where(hi, vp, 0.0)], axis=0)
    return _dot(lhs, rhs)


def _masked_spatial(w_ref):
    t = lax.broadcasted_iota(jnp.int32, (CHUNK, CHUNK), 0)
    s = lax.broadcasted_iota(jnp.int32, (CHUNK, CHUNK), 1)
    tril = s <= t
    return [jnp.where(tril, w_ref[g], 0.0) for g in range(GMLP_GROUPS)], tril


def _mod_kernel(c_all, w_shard, b_shard):
    n = w_shard.shape[1]
    tn = 512

    def body(c_ref, w_ref, b_ref, mod_ref, act_ref):
        cv = c_ref[...]
        act = cv * (1.0 / (1.0 + jnp.exp(-cv)))
        act_ref[...] = act
        mod_ref[...] = _dot(act, w_ref[...]) + b_ref[...]

    return pl.pallas_call(
        body, name="ada_mod", grid=(n // tn,),
        out_shape=[jax.ShapeDtypeStruct((N_DEV, n), F32), jax.ShapeDtypeStruct((N_DEV, D_MODEL), F32)],
        in_specs=[_full((N_DEV, D_MODEL)), pl.BlockSpec((D_MODEL, tn), lambda i: (0, i)),
                  pl.BlockSpec((1, tn), lambda i: (0, i))],
        out_specs=[pl.BlockSpec((N_DEV, tn), lambda i: (0, i)), _full((N_DEV, D_MODEL))],
        compiler_params=_params("arbitrary"),
    )(c_all, w_shard, b_shard)


def _in_proj_kernel(x, vecs, w_in):
    seq = x.shape[0]
    tm = 512

    def body(x_ref, v_ref, w_ref, proj_ref, h_ref):
        xv = x_ref[...]
        rstd = lax.rsqrt(_mean_last(xv * xv) + EPS)
        n1 = (xv * rstd) * v_ref[0:1, :]
        h = n1 * (1.0 + v_ref[2:3, :]) + v_ref[1:2, :]
        hb = h.astype(MXU_DTYPE)
        h_ref[...] = hb
        proj_ref[...] = _dot(hb, w_ref[...])

    return pl.pallas_call(
        body, name="in_proj", grid=(seq // tm,),
        out_shape=[jax.ShapeDtypeStruct((seq, IN_PROJ_WIDTH), F32),
                   jax.ShapeDtypeStruct((seq, D_MODEL), MXU_DTYPE)],
        in_specs=[pl.BlockSpec((tm, D_MODEL), lambda i: (i, 0)), _full((8, D_MODEL)),
                  _full((D_MODEL, IN_PROJ_WIDTH))],
        out_specs=[pl.BlockSpec((tm, IN_PROJ_WIDTH), lambda i: (i, 0)),
                   pl.BlockSpec((tm, D_MODEL), lambda i: (i, 0))],
        compiler_params=_params("parallel"),
    )(x, vecs, w_in)


def _mixer_fwd_kernel(proj, pos, w_spatial, bias_full, sink_rows, invf):
    seq = proj.shape[0]
    nb = seq // CHUNK
    kv_col = (2 * GMLP_WIDTH + ATTN_WIDTH) // (2 * KV_WIDTH)

    def body(proj_ref, prev_ref, pos_ref, ppos_ref, w_ref, bias_ref, sink_ref, invf_ref, cat_ref):
        i = pl.program_id(0)
        wm, _ = _masked_spatial(w_ref)
        for j in range(GMLP_GROUPS // 2):
            cols = slice(LANES * j, LANES * (j + 1))
            vcols = slice(GMLP_WIDTH + LANES * j, GMLP_WIDTH + LANES * (j + 1))
            u, _ = _gelu_tanh(proj_ref[:, cols])
            vp, _ = _gelu_tanh(proj_ref[:, vcols])
            sv = _sgu_forward_pair(wm, vp, j) + bias_ref[:, cols]
            cat_ref[:, cols] = (u * sv).astype(cat_ref.dtype)
        tabs = _rope_tables(pos_ref[...], invf_ref[...])
        ptabs = _rope_tables(ppos_ref[...], invf_ref[...])
        o = 2 * GMLP_WIDTH
        q_r = _rope_apply(proj_ref[:, o:o + ATTN_WIDTH], tabs, 1.0)
        k_cur = _rope_apply(proj_ref[:, o + ATTN_WIDTH:o + ATTN_WIDTH + KV_WIDTH], tabs, 1.0)
        k_prev = _rope_apply(prev_ref[:, 0:KV_WIDTH], ptabs, 1.0)
        k_a = jnp.concatenate([k_prev, k_cur], axis=0)
        v_a = jnp.concatenate([prev_ref[:, KV_WIDTH:2 * KV_WIDTH],
                               proj_ref[:, o + ATTN_WIDTH + KV_WIDTH:o + ATTN_WIDTH + 2 * KV_WIDTH]], axis=0)
        k_b = pltpu.roll(k_a, HEAD_DIM, 1)
        v_b = pltpu.roll(v_a, HEAD_DIM, 1)
        mask = _attn_mask(i == 0)
        lo, hi = _lane_masks((CHUNK, LANES))
        lo2, hi2 = _lane_masks((2 * CHUNK, LANES))
        outs = [None] * (N_Q_HEADS // 2)
        for h, (pair, is_hi, rolled) in enumerate(_head_plan()):
            qp = q_r[:, LANES * pair:LANES * (pair + 1)]
            qm = jnp.where(hi if is_hi else lo, qp, 0.0)
            s = _dot_nt(qm, k_b if rolled else k_a) * (1.0 / math.sqrt(HEAD_DIM))
            s = jnp.where(mask, s, -jnp.inf)
            probs, _ = _softmax_sink(s, sink_ref[h:h + 1, 0:1])
            vm = jnp.where(hi2 if is_hi else lo2, v_b if rolled else v_a, 0.0)
            oh = _dot(probs, vm)
            outs[pair] = oh if outs[pair] is None else outs[pair] + oh
        for pair in range(N_Q_HEADS // 2):
            cols = slice(GMLP_WIDTH + LANES * pair, GMLP_WIDTH + LANES * (pair + 1))
            cat_ref[:, cols] = outs[pair].astype(cat_ref.dtype)

    return pl.pallas_call(
        body, name="mixer_fwd", grid=(nb,),
        out_shape=jax.ShapeDtypeStruct((seq, D_MODEL), MXU_DTYPE),
        in_specs=[pl.BlockSpec((CHUNK, IN_PROJ_WIDTH), lambda i: (i, 0)),
                  pl.BlockSpec((CHUNK, 2 * KV_WIDTH), lambda i: (jnp.maximum(i - 1, 0), kv_col)),
                  pl.BlockSpec((CHUNK, 1), lambda i: (i, 0)),
                  pl.BlockSpec((CHUNK, 1), lambda i: (jnp.maximum(i - 1, 0), 0)),
                  _full((GMLP_GROUPS, CHUNK, CHUNK)), _full((CHUNK, GMLP_WIDTH)),
                  _full((N_Q_HEADS, LANES)), _full((1, LANES))],
        out_specs=pl.BlockSpec((CHUNK, D_MODEL), lambda i: (i, 0)),
        compiler_params=_params("parallel"),
    )(proj, proj, pos, pos, w_spatial, bias_full, sink_rows, invf)


def _trunk_kernel(x, target, cat, vecs, w_out, w_ff1, w_ff2):
    seq = x.shape[0]
    tm = 256
    nj = D_FF // D_MODEL

    def body(x_ref, t_ref, cat_ref, v_ref, wout_hbm, w1_hbm, w2_hbm,
             dx1_ref, dcat_ref, dmix_ref, h2_ref, r_ref, da_ref, dff_ref, sums_ref,
             wout, w1, w2, a_scr, sem):
        i = pl.program_id(0)

        @pl.when(i == 0)
        def _():
            copies = [pltpu.make_async_copy(wout_hbm, wout, sem.at[0]),
                      pltpu.make_async_copy(w1_hbm, w1, sem.at[1]),
                      pltpu.make_async_copy(w2_hbm, w2, sem.at[2])]
            for cp in copies:
                cp.start()
            for cp in copies:
                cp.wait()
            sums_ref[...] = jnp.zeros_like(sums_ref)

        gate1, shift2, scale2 = v_ref[0:1, :], v_ref[1:2, :], v_ref[2:3, :]
        gate2, g_ffn, g_final = v_ref[3:4, :], v_ref[4:5, :], v_ref[5:6, :]

        mix = _dot(cat_ref[...], wout[...])
        x1 = x_ref[...] + gate1 * mix
        rstd2 = lax.rsqrt(_mean_last(x1 * x1) + EPS)
        xh2 = x1 * rstd2
        n2 = xh2 * g_ffn
        h2b = (n2 * (1.0 + scale2) + shift2).astype(MXU_DTYPE)
        h2_ref[...] = h2b
        ff = jnp.zeros((tm, D_MODEL), F32)
        for j in range(nj):
            a = _dot(h2b, w1[j])
            a_scr[j] = a
            relu = jnp.maximum(a, 0.0)
            rb = (relu * relu).astype(MXU_DTYPE)
            r_ref[:, D_MODEL * j:D_MODEL * (j + 1)] = rb
            ff = ff + _dot(rb, w2[j])
        x2 = x1 + gate2 * ff
        rstd3 = lax.rsqrt(_mean_last(x2 * x2) + EPS)
        xh3 = x2 * rstd3
        err = xh3 * g_final - t_ref[...]
        loss = 0.5 * _rowsum(_mean_last(err * err))
        dy = err * (1.0 / D_MODEL)
        dxh3 = dy * g_final
        dx2 = rstd3 * (dxh3 - xh3 * _mean_last(dxh3 * xh3))
        dffb = (dx2 * gate2).astype(MXU_DTYPE)
        dff_ref[...] = dffb
        dh2 = jnp.zeros((tm, D_MODEL), F32)
        for j in range(nj):
            dr = _dot_nt(dffb, w2[j])
            dab = (dr * (2.0 * jnp.maximum(a_scr[j], 0.0))).astype(MXU_DTYPE)
            da_ref[:, D_MODEL * j:D_MODEL * (j + 1)] = dab
            dh2 = dh2 + _dot_nt(dab, w1[j])
        dn2 = dh2 * (1.0 + scale2)
        dxh2 = dn2 * g_ffn
        dx1 = dx2 + rstd2 * (dxh2 - xh2 * _mean_last(dxh2 * xh2))
        dx1_ref[...] = dx1
        dmixb = (dx1 * gate1).astype(MXU_DTYPE)
        dmix_ref[...] = dmixb
        dcat_ref[...] = _dot_nt(dmixb, wout[...])

        sums_ref[0:1, :] += _rowsum(dh2)
        sums_ref[1:2, :] += _rowsum(dh2 * n2)
        sums_ref[2:3, :] += _rowsum(dx2 * ff)
        sums_ref[3:4, :] += _rowsum(dn2 * xh2)
        sums_ref[4:5, :] += _rowsum(dy * xh3)
        sums_ref[5:6, :] += _rowsum(dx1 * mix)
        sums_ref[6:7, :] += jnp.broadcast_to(loss, (1, D_MODEL))

    tok = lambda w: pl.BlockSpec((tm, w), lambda i: (i, 0))
    return pl.pallas_call(
        body, name="trunk", grid=(seq // tm,),
        out_shape=[jax.ShapeDtypeStruct((seq, D_MODEL), F32), jax.ShapeDtypeStruct((seq, D_MODEL), F32),
                   jax.ShapeDtypeStruct((seq, D_MODEL), MXU_DTYPE), jax.ShapeDtypeStruct((seq, D_MODEL), MXU_DTYPE),
                   jax.ShapeDtypeStruct((seq, D_FF), MXU_DTYPE), jax.ShapeDtypeStruct((seq, D_FF), MXU_DTYPE),
                   jax.ShapeDtypeStruct((seq, D_MODEL), MXU_DTYPE), jax.ShapeDtypeStruct((8, D_MODEL), F32)],
        in_specs=[tok(D_MODEL), tok(D_MODEL), tok(D_MODEL), _full((8, D_MODEL)), _any(), _any(), _any()],
        out_specs=[tok(D_MODEL), tok(D_MODEL), tok(D_MODEL), tok(D_MODEL), tok(D_FF), tok(D_FF), tok(D_MODEL),
                   _full((8, D_MODEL))],
        scratch_shapes=[pltpu.VMEM((D_MODEL, D_MODEL), MXU_DTYPE), pltpu.VMEM((nj, D_MODEL, D_MODEL), MXU_DTYPE),
                        pltpu.VMEM((nj, D_MODEL, D_MODEL), MXU_DTYPE), pltpu.VMEM((nj, tm, D_MODEL), F32),
                        pltpu.SemaphoreType.DMA((3,))],
        compiler_params=_params("arbitrary"),
    )(x, target, cat, vecs, w_out, w_ff1, w_ff2)


def _mixer_bwd_kernel(proj, pos, dcat, w_spatial, bias_full, sink_rows, invf):
    seq = proj.shape[0]
    nb = seq // CHUNK
    kv_col = (2 * GMLP_WIDTH + ATTN_WIDTH) // (2 * KV_WIDTH)

    def body(proj_ref, prev_ref, pos_ref, ppos_ref, dcat_ref, w_ref, bias_ref, sink_ref, invf_ref,
             dproj_ref, dw_ref, db_ref, dsink_ref, carry):
        step = pl.program_id(0)
        blk = nb - 1 - step

        @pl.when(step == 0)
        def _():
            carry[...] = jnp.zeros_like(carry)
            dw_ref[...] = jnp.zeros_like(dw_ref)
            db_ref[...] = jnp.zeros_like(db_ref)
            dsink_ref[...] = jnp.zeros_like(dsink_ref)

        wm, tril = _masked_spatial(w_ref)
        lo, hi = _lane_masks((CHUNK, LANES))
        for j in range(GMLP_GROUPS // 2):
            cols = slice(LANES * j, LANES * (j + 1))
            vcols = slice(GMLP_WIDTH + LANES * j, GMLP_WIDTH + LANES * (j + 1))
            zu, zv = proj_ref[:, cols], proj_ref[:, vcols]
            u, tu = _gelu_tanh(zu)
            vp, tv = _gelu_tanh(zv)
            sv = _sgu_forward_pair(wm, vp, j) + bias_ref[:, cols]
            dout = dcat_ref[:, cols]
            du = dout * sv
            dsv = dout * u
            dsv_lo, dsv_hi = jnp.where(lo, dsv, 0.0), jnp.where(hi, dsv, 0.0)
            lhs_t = jnp.concatenate([wm[2 * j].T, wm[2 * j + 1].T], axis=1)
            dv = _dot(lhs_t, jnp.concatenate([dsv_lo, dsv_hi], axis=0))
            dw_ref[2 * j] += jnp.where(tril, _dot_nt(dsv_lo, vp), 0.0)
            dw_ref[2 * j + 1] += jnp.where(tril, _dot_nt(dsv_hi, vp), 0.0)
            lane = lax.broadcasted_iota(jnp.int32, (CHUNK, LANES), 1)
            db_ref[...] += (jnp.where(lane == 2 * j, jnp.sum(dsv_lo, axis=1, keepdims=True), 0.0)
                            + jnp.where(lane == 2 * j + 1, jnp.sum(dsv_hi, axis=1, keepdims=True), 0.0))
            dproj_ref[:, cols] = (du * _gelu_tanh_grad(zu, tu)).astype(dproj_ref.dtype)
            dproj_ref[:, vcols] = (dv * _gelu_tanh_grad(zv, tv)).astype(dproj_ref.dtype)
        tabs = _rope_tables(pos_ref[...], invf_ref[...])
        ptabs = _rope_tables(ppos_ref[...], invf_ref[...])
        o = 2 * GMLP_WIDTH
        q_r = _rope_apply(proj_ref[:, o:o + ATTN_WIDTH], tabs, 1.0)
        k_cur = _rope_apply(proj_ref[:, o + ATTN_WIDTH:o + ATTN_WIDTH + KV_WIDTH], tabs, 1.0)
        k_prev = _rope_apply(prev_ref[:, 0:KV_WIDTH], ptabs, 1.0)
        k_a = jnp.concatenate([k_prev, k_cur], axis=0)
        v_a = jnp.concatenate([prev_ref[:, KV_WIDTH:2 * KV_WIDTH],
                               proj_ref[:, o + ATTN_WIDTH + KV_WIDTH:o + ATTN_WIDTH + 2 * KV_WIDTH]], axis=0)
        k_b = pltpu.roll(k_a, HEAD_DIM, 1)
        v_b = pltpu.roll(v_a, HEAD_DIM, 1)
        mask = _attn_mask(blk == 0)
        lo2, hi2 = _lane_masks((2 * CHUNK, LANES))
        scale = 1.0 / math.sqrt(HEAD_DIM)
        zero_kv = jnp.zeros((2 * CHUNK, LANES), F32)
        dk_a, dk_b, dv_a, dv_b = zero_kv, zero_kv, zero_kv, zero_kv
        dq_pairs = [None] * (N_Q_HEADS // 2)
        for h, (pair, is_hi, rolled) in enumerate(_head_plan()):
            half, half2 = (hi, hi2) if is_hi else (lo, lo2)
            qm = jnp.where(half, q_r[:, LANES * pair:LANES * (pair + 1)], 0.0)
            k_sel, v_sel = (k_b, v_b) if rolled else (k_a, v_a)
            s = jnp.where(mask, _dot_nt(qm, k_sel) * scale, -jnp.inf)
            probs, p_sink = _softmax_sink(s, sink_ref[h:h + 1, 0:1])
            dom = jnp.where(half, dcat_ref[:, GMLP_WIDTH + LANES * pair:GMLP_WIDTH + LANES * (pair + 1)], 0.0)
            dprobs = _dot_nt(dom, v_sel)
            delta = jnp.sum(probs * dprobs, axis=-1, keepdims=True)
            ds = probs * (dprobs - delta) * scale
            dsink_ref[h:h + 1, :] += jnp.broadcast_to(_rowsum(-p_sink * delta), (1, LANES))
            dqh = _dot(ds, jnp.where(half2, k_sel, 0.0))
            dq_pairs[pair] = dqh if dq_pairs[pair] is None else dq_pairs[pair] + dqh
            dkh = _dot_tn(ds, qm)
            dvh = _dot_tn(probs, dom)
            if rolled:
                dk_b, dv_b = dk_b + dkh, dv_b + dvh
            else:
                dk_a, dv_a = dk_a + dkh, dv_a + dvh
        dk_all = dk_a + pltpu.roll(dk_b, HEAD_DIM, 1)
        dv_all = dv_a + pltpu.roll(dv_b, HEAD_DIM, 1)
        dk_cur = dk_all[CHUNK:, :] + carry[:, 0:KV_WIDTH]
        dv_cur = dv_all[CHUNK:, :] + carry[:, KV_WIDTH:2 * KV_WIDTH]
        carry[:, 0:KV_WIDTH] = dk_all[:CHUNK, :]
        carry[:, KV_WIDTH:2 * KV_WIDTH] = dv_all[:CHUNK, :]
        dq = _rope_apply(jnp.concatenate(dq_pairs, axis=1), tabs, -1.0)
        dproj_ref[:, o:o + ATTN_WIDTH] = dq.astype(dproj_ref.dtype)
        dproj_ref[:, o + ATTN_WIDTH:o + ATTN_WIDTH + KV_WIDTH] = (
            _rope_apply(dk_cur, tabs, -1.0).astype(dproj_ref.dtype))
        dproj_ref[:, o + ATTN_WIDTH + KV_WIDTH:o + ATTN_WIDTH + 2 * KV_WIDTH] = dv_cur.astype(dproj_ref.dtype)

    rev = lambda i: nb - 1 - i
    return pl.pallas_call(
        body, name="mixer_bwd", grid=(nb,),
        out_shape=[jax.ShapeDtypeStruct((seq, IN_PROJ_WIDTH), MXU_DTYPE),
                   jax.ShapeDtypeStruct((GMLP_GROUPS, CHUNK, CHUNK), F32),
                   jax.ShapeDtypeStruct((CHUNK, LANES), F32),
                   jax.ShapeDtypeStruct((N_Q_HEADS, LANES), F32)],
        in_specs=[pl.BlockSpec((CHUNK, IN_PROJ_WIDTH), lambda i: (rev(i), 0)),
                  pl.BlockSpec((CHUNK, 2 * KV_WIDTH), lambda i: (jnp.maximum(rev(i) - 1, 0), kv_col)),
                  pl.BlockSpec((CHUNK, 1), lambda i: (rev(i), 0)),
                  pl.BlockSpec((CHUNK, 1), lambda i: (jnp.maximum(rev(i) - 1, 0), 0)),
                  pl.BlockSpec((CHUNK, D_MODEL), lambda i: (rev(i), 0)),
                  _full((GMLP_GROUPS, CHUNK, CHUNK)), _full((CHUNK, GMLP_WIDTH)),
                  _full((N_Q_HEADS, LANES)), _full((1, LANES))],
        out_specs=[pl.BlockSpec((CHUNK, IN_PROJ_WIDTH), lambda i: (rev(i), 0)),
                   _full((GMLP_GROUPS, CHUNK, CHUNK)), _full((CHUNK, LANES)), _full((N_Q_HEADS, LANES))],
        scratch_shapes=[pltpu.VMEM((CHUNK, 2 * KV_WIDTH), F32)],
        compiler_params=_params("arbitrary"),
    )(proj, proj, pos, pos, dcat, w_spatial, bias_full, sink_rows, invf)


def _in_proj_bwd_kernel(x, dx1, dproj, vecs, w_in):
    seq = x.shape[0]
    tm = 512

    def body(x_ref, dx1_ref, dp_ref, v_ref, w_ref, gx_ref, sums_ref):
        @pl.when(pl.program_id(0) == 0)
        def _():
            sums_ref[...] = jnp.zeros_like(sums_ref)

        g_mix, scale1 = v_ref[0:1, :], v_ref[2:3, :]
        dh = _dot_nt(dp_ref[...], w_ref[...])
        xv = x_ref[...]
        rstd = lax.rsqrt(_mean_last(xv * xv) + EPS)
        xh = xv * rstd
        dn1 = dh * (1.0 + scale1)
        dxh = dn1 * g_mix
        gx_ref[...] = dx1_ref[...] + rstd * (dxh - xh * _mean_last(dxh * xh))
        sums_ref[0:1, :] += _rowsum(dh)
        sums_ref[1:2, :] += _rowsum(dh * (xh * g_mix))
        sums_ref[2:3, :] += _rowsum(dn1 * xh)

    return pl.pallas_call(
        body, name="in_proj_bwd", grid=(seq // tm,),
        out_shape=[jax.ShapeDtypeStruct((seq, D_MODEL), F32), jax.ShapeDtypeStruct((8, D_MODEL), F32)],
        in_specs=[pl.BlockSpec((tm, D_MODEL), lambda i: (i, 0)), pl.BlockSpec((tm, D_MODEL), lambda i: (i, 0)),
                  pl.BlockSpec((tm, IN_PROJ_WIDTH), lambda i: (i, 0)), _full((8, D_MODEL)),
                  _full((D_MODEL, IN_PROJ_WIDTH))],
        out_specs=[pl.BlockSpec((tm, D_MODEL), lambda i: (i, 0)), _full((8, D_MODEL))],
        compiler_params=_params("arbitrary"),
    )(x, dx1, dproj, vecs, w_in)


def _weight_grad_kernel(a, b, name, tm, tn, blocked_cols=False):
    seq, m = a.shape
    n = b.shape[1]
    tk = 512
    nk = seq // tk

    def body(a_ref, b_ref, o_ref, acc):
        k = pl.program_id(2)

        @pl.when(k == 0)
        def _():
            acc[...] = jnp.zeros_like(acc)

        acc[...] += _dot_tn(a_ref[...], b_ref[...])

        @pl.when(k == nk - 1)
        def _():
            o_ref[...] = acc[...]

    if blocked_cols:
        out_shape = jax.ShapeDtypeStruct((n // tn, m, tn), F32)
        out_spec = pl.BlockSpec((None, tm, tn), lambda i, j, k: (j, i, 0))
    else:
        out_shape = jax.ShapeDtypeStruct((m, n), F32)
        out_spec = pl.BlockSpec((tm, tn), lambda i, j, k: (i, j))
    return pl.pallas_call(
        body, name=name, grid=(m // tm, n // tn, nk), out_shape=out_shape,
        in_specs=[pl.BlockSpec((tk, tm), lambda i, j, k: (k, i)), pl.BlockSpec((tk, tn), lambda i, j, k: (k, j))],
        out_specs=out_spec, scratch_shapes=[pltpu.VMEM((tm, tn), F32)],
        compiler_params=_params("parallel", "parallel", "arbitrary"),
    )(a, b)


def _adam_update(w, g, m, v):
    m_new = ADAM_B1 * m + (1.0 - ADAM_B1) * g
    v_new = ADAM_B2 * v + (1.0 - ADAM_B2) * (g * g)
    m_hat = m_new / (1.0 - ADAM_B1 ** ADAM_STEP)
    v_hat = v_new / (1.0 - ADAM_B2 ** ADAM_STEP)
    delta = -ADAM_LR * (m_hat / (jnp.sqrt(v_hat) + ADAM_EPS) + ADAM_WD * w)
    return delta, m_new, v_new


def _add_halves_kernel(part, recv, c_idx, name):
    _, _, r, n = part.shape
    tr = min(r, 256)

    def body(c_ref, p_ref, q_ref, o_ref):
        del c_ref
        o_ref[...] = (p_ref[...] + q_ref[...]).astype(o_ref.dtype)

    return pl.pallas_call(
        body, name=name, out_shape=jax.ShapeDtypeStruct((N_CHIPS, r, n), GRAD_COMM_DTYPE),
        grid_spec=pltpu.PrefetchScalarGridSpec(
            num_scalar_prefetch=1, grid=(N_CHIPS, r // tr),
            in_specs=[pl.BlockSpec((None, None, tr, n), lambda k, i, c: (k, c[0], i, 0)),
                      pl.BlockSpec((None, tr, n), lambda k, i, c: (k, i, 0))],
            out_specs=pl.BlockSpec((None, tr, n), lambda k, i, c: (k, i, 0))),
        compiler_params=_params("parallel", "parallel"),
    )(c_idx, part, recv)


def _sum_chips_kernel(parts, name):
    _, r, n = parts.shape
    tr = min(r, 256)

    def body(p_ref, o_ref):
        acc = p_ref[0].astype(F32)
        for k in range(1, N_CHIPS):
            acc = acc + p_ref[k].astype(F32)
        o_ref[...] = acc

    return pl.pallas_call(
        body, name=name, grid=(r // tr,), out_shape=jax.ShapeDtypeStruct((r, n), F32),
        in_specs=[pl.BlockSpec((N_CHIPS, tr, n), lambda i: (0, i, 0))],
        out_specs=pl.BlockSpec((tr, n), lambda i: (i, 0)),
        compiler_params=_params("parallel"),
    )(parts)


def _adam_kernel(w, g, m, v, name):
    r, n = w.shape
    tr = min(r, 256)

    def body(w_ref, g_ref, m_ref, v_ref, d_ref, mo_ref, vo_ref):
        d_ref[...], mo_ref[...], vo_ref[...] = _adam_update(w_ref[...], g_ref[...], m_ref[...], v_ref[...])

    spec = pl.BlockSpec((tr, n), lambda i: (i, 0))
    return pl.pallas_call(
        body, name=name, grid=(r // tr,), out_shape=[jax.ShapeDtypeStruct((r, n), F32)] * 3,
        in_specs=[spec] * 4, out_specs=[spec] * 3, compiler_params=_params("parallel"),
    )(w, g, m, v)


def _small_update_kernel(gathered, w, m, v):
    _, r, n = gathered.shape

    def body(g_ref, w_ref, m_ref, v_ref, gs_ref, d_ref, mo_ref, vo_ref):
        g = g_ref[0]
        for k in range(1, N_DEV):
            g = g + g_ref[k]
        gs_ref[...] = g
        d_ref[...], mo_ref[...], vo_ref[...] = _adam_update(w_ref[...], g, m_ref[...], v_ref[...])

    return pl.pallas_call(
        body, name="small_update", grid=(1,), out_shape=[jax.ShapeDtypeStruct((r, n), F32)] * 4,
        in_specs=[_full((N_DEV, r, n))] + [_full((r, n))] * 3, out_specs=[_full((r, n))] * 4,
        compiler_params=_params("arbitrary"),
    )(gathered, w, m, v)


def _ada_update_kernel(act_t, dmod, w, m, v):
    r, n = w.shape
    tr = 256

    def body(a_ref, d_ref, w_ref, m_ref, v_ref, g_ref, dl_ref, mo_ref, vo_ref):
        g = _dot(a_ref[...], d_ref[...])
        g_ref[...] = g
        dl_ref[...], mo_ref[...], vo_ref[...] = _adam_update(w_ref[...], g, m_ref[...], v_ref[...])

    spec = pl.BlockSpec((tr, n), lambda i: (i, 0))
    return pl.pallas_call(
        body, name="ada_update", grid=(r // tr,), out_shape=[jax.ShapeDtypeStruct((r, n), F32)] * 4,
        in_specs=[pl.BlockSpec((tr, N_DEV), lambda i: (i, 0)), _full((N_DEV, n)), spec, spec, spec],
        out_specs=[spec] * 4, compiler_params=_params("parallel"),
    )(act_t, dmod, w, m, v)


_SMALL_ROWS = (("b_ada", 48), ("g_mix", 8), ("g_ffn", 8), ("g_final", 8), ("b_spatial", 8),
               ("sinks", 1), ("loss", 1), ("pad", 6), ("w_spatial", 1024))


def _pack_small(parts):
    rows = []
    for name, nrows in _SMALL_ROWS:
        if name in parts:
            flat = parts[name].reshape(-1).astype(F32)
            flat = jnp.pad(flat, (0, nrows * LANES - flat.shape[0]))
            rows.append(flat.reshape(nrows, LANES))
        else:
            rows.append(jnp.zeros((nrows, LANES), F32))
    return jnp.concatenate(rows, axis=0)


def _unpack_small(packed, shapes):
    out, start = {}, 0
    for name, nrows in _SMALL_ROWS:
        if name in shapes:
            size = math.prod(shapes[name])
            out[name] = packed[start:start + nrows].reshape(-1)[:size].reshape(shapes[name])
        start += nrows
    return out


def kernel(x, c, positions, w_ada, b_ada, g_mix, w_in, w_spatial, b_spatial, sinks, w_out, g_ffn, w_ff1, w_ff2, g_final, loss_target, m_w_ada, m_b_ada, m_g_mix, m_w_in, m_w_spatial, m_b_spatial, m_sinks, m_w_out, m_g_ffn, m_w_ff1, m_w_ff2, m_g_final, v_w_ada, v_b_ada, v_g_mix, v_w_in, v_w_spatial, v_b_spatial, v_sinks, v_w_out, v_g_ffn, v_w_ff1, v_w_ff2, v_g_final):
    xi, yi, ci = lax.axis_index("x"), lax.axis_index("y"), lax.axis_index("c")
    chip = 2 * xi + yi
    dev = 2 * chip + ci
    seq = x.shape[1]
    x2, tgt = x[0], loss_target[0]
    pos = positions.reshape(seq, 1)
    ada_cols = w_ada.shape[2]

    c_all = _all_gather8(c, "gather_c").reshape(N_DEV, D_MODEL)
    b_shard = lax.dynamic_slice(b_ada, (0, chip * ada_cols), (1, ada_cols))
    mod_part, act = _mod_kernel(c_all, w_ada[0], b_shard)
    mod_all = _all_gather8(mod_part, "gather_mod")
    mod_me = lax.dynamic_index_in_dim(mod_all[0::2], dev, axis=1, keepdims=False)
    mod_me = mod_me.reshape(N_MOD, D_MODEL)
    shift1, scale1, gate1, shift2, scale2, gate2 = (mod_me[k:k + 1] for k in range(N_MOD))

    def gather_weight(w, name):
        r, n = w.shape[1], w.shape[2]
        halves = w[0].astype(WEIGHT_COMM_DTYPE).reshape(2, r // 2, n)
        return _all_gather8(halves, name, split=True).reshape(N_CHIPS, r, n)

    w_in_full = gather_weight(w_in, "gather_w_in").transpose(1, 0, 2).reshape(D_MODEL, IN_PROJ_WIDTH)
    w_out_full = gather_weight(w_out, "gather_w_out").reshape(D_MODEL, D_MODEL)
    w_ff1_blocks = gather_weight(w_ff1, "gather_w_ff1")
    w_ff2_blocks = gather_weight(w_ff2, "gather_w_ff2")

    zeros_row = jnp.zeros((1, D_MODEL), F32)
    vecs1 = jnp.concatenate([g_mix, shift1, scale1] + [zeros_row] * 5, axis=0)
    vecs2 = jnp.concatenate([gate1, shift2, scale2, gate2, g_ffn, g_final.reshape(1, D_MODEL)]
                            + [zeros_row] * 2, axis=0)
    bias_full = jnp.repeat(b_spatial[0].T, HEAD_DIM, axis=1)
    sink_rows = jnp.broadcast_to(sinks[0][:, None], (N_Q_HEADS, LANES))
    inv_freq = ROPE_THETA ** (-jnp.arange(0, ROT_DIM, 2, dtype=F32) / ROT_DIM)
    lane_d = jnp.arange(LANES) % HEAD_DIM
    invf = jnp.where(lane_d < ROT_DIM, inv_freq[lane_d % (ROT_DIM // 2)], 0.0).reshape(1, LANES)

    proj, hb = _in_proj_kernel(x2, vecs1, w_in_full)
    cat = _mixer_fwd_kernel(proj, pos, w_spatial[0], bias_full, sink_rows, invf)
    dx1, dcat, dmix, h2b, rb, dab, dffb, sums2 = _trunk_kernel(
        x2, tgt, cat, vecs2, w_out_full, w_ff1_blocks, w_ff2_blocks)
    dproj, dw_spatial, db_lanes, dsink_rows = _mixer_bwd_kernel(
        proj, pos, dcat, w_spatial[0], bias_full, sink_rows, invf)
    grad_x, sums1 = _in_proj_bwd_kernel(x2, dx1, dproj, vecs1, w_in_full)

    dw_ff2 = _weight_grad_kernel(rb, dffb, "dw_ff2", 1024, 1024)
    dw_ff1 = _weight_grad_kernel(h2b, dab, "dw_ff1", 1024, 1024, blocked_cols=True)
    dw_out = _weight_grad_kernel(cat, dmix, "dw_out", 1024, 1024)
    dw_in = _weight_grad_kernel(hb, dproj, "dw_in", 1024, 896)
    dw_in = dw_in.reshape(D_MODEL, N_CHIPS, IN_PROJ_WIDTH // N_CHIPS).transpose(1, 0, 2)

    big = {"w_in": (w_in, m_w_in, v_w_in, dw_in), "w_out": (w_out, m_w_out, v_w_out, dw_out),
           "w_ff1": (w_ff1, m_w_ff1, v_w_ff1, dw_ff1), "w_ff2": (w_ff2, m_w_ff2, v_w_ff2, dw_ff2)}
    names = list(big)
    parts = []
    for nm in names:
        r, n = big[nm][0].shape[1], big[nm][0].shape[2]
        parts.append(big[nm][3].reshape(N_CHIPS, 2, r // 2, n))
    c_idx = ci.reshape(1).astype(jnp.int32)
    from_sibling = _sibling_exchange_halves(parts, "grad_to_sibling")
    chip_sums = [_add_halves_kernel(p, q, c_idx, "grad_add_" + nm) for nm, p, q in zip(names, parts, from_sibling)]
    scattered = _chip_scatter(chip_sums, "grad_to_chips")
    totals = [_sum_chips_kernel(s, "grad_sum_" + nm) for nm, s in zip(names, scattered)]
    shared = _sibling_share(totals, "grad_share")
    big_out = {}
    for nm, g in zip(names, shared):
        w, m, v, _ = big[nm]
        g = g.reshape(w.shape[1:])
        d, mn, vn = _adam_kernel(w[0], g, m[0], v[0], "adam_" + nm)
        big_out[nm] = tuple(t[None] for t in (g, d, mn, vn))

    dmod = jnp.concatenate([sums1[0:1], sums1[1:2], sums2[5:6], sums2[0:1], sums2[1:2], sums2[2:3]], axis=1)
    db_spatial = db_lanes[:, 0:GMLP_GROUPS].T
    small_grads = {"b_ada": dmod, "g_mix": sums1[2], "g_ffn": sums2[3], "g_final": sums2[4],
                   "b_spatial": db_spatial, "sinks": dsink_rows[:, 0], "loss": sums2[6, 0:1],
                   "w_spatial": dw_spatial}
    small_w = {"b_ada": b_ada, "g_mix": g_mix, "g_ffn": g_ffn, "g_final": g_final, "b_spatial": b_spatial,
               "sinks": sinks, "w_spatial": w_spatial}
    small_m = {"b_ada": m_b_ada, "g_mix": m_g_mix, "g_ffn": m_g_ffn, "g_final": m_g_final,
               "b_spatial": m_b_spatial, "sinks": m_sinks, "w_spatial": m_w_spatial}
    small_v = {"b_ada": v_b_ada, "g_mix": v_g_mix, "g_ffn": v_g_ffn, "g_final": v_g_final,
               "b_spatial": v_b_spatial, "sinks": v_sinks, "w_spatial": v_w_spatial}
    gathered = _all_gather8(_pack_small(small_grads), "gather_small")
    packed = _small_update_kernel(gathered, _pack_small(small_w), _pack_small(small_m), _pack_small(small_v))
    shapes = {k: a.shape for k, a in small_w.items()}
    sg, sd, sm, sv = (_unpack_small(p, shapes) for p in packed)
    loss = packed[0][sum(nr for nm, nr in _SMALL_ROWS[:6]), 0]

    dmod_all = gathered[:, 0:N_MOD * D_MODEL // LANES, :].reshape(N_DEV, N_MOD * D_MODEL)
    dmod_cols = lax.dynamic_slice(dmod_all, (0, chip * ada_cols), (N_DEV, ada_cols))
    ada = _ada_update_kernel(act.T, dmod_cols, w_ada[0], m_w_ada[0], v_w_ada[0])
    big_out["w_ada"] = tuple(t[None] for t in ada)

    order = ["w_ada", "b_ada", "g_mix", "w_in", "w_spatial", "b_spatial", "sinks", "w_out", "g_ffn",
             "w_ff1", "w_ff2", "g_final"]

    def leaf(nm, k):
        return big_out[nm][k] if nm in big_out else (sg, sd, sm, sv)[k][nm]

    outs = [loss, grad_x[None]]
    for k in range(4):
        outs += [leaf(nm, k) for nm in order]
    return tuple(outs)
```

```python
import functools
import math

import jax
import jax.numpy as jnp
from jax import lax
from jax.experimental import pallas as pl
from jax.experimental.pallas import tpu as pltpu

F32 = jnp.float32
MXU_DTYPE = jnp.bfloat16
WEIGHT_COMM_DTYPE = jnp.bfloat16
GRAD_COMM_DTYPE = jnp.bfloat16

D_MODEL = 1024
D_FF = 4096
HEAD_DIM = 64
GMLP_GROUPS = 8
GMLP_WIDTH = 512
CHUNK = 128
N_Q_HEADS = 8
N_KV_HEADS = 2
ATTN_WIDTH = 512
KV_WIDTH = 128
ROT_DIM = 16
ROPE_THETA = 500000.0
IN_PROJ_WIDTH = 1792
N_MOD = 6
EPS = 1e-5
N_CHIPS = 4
N_DEV = 8
LANES = 128

ADAM_LR = 0.001
ADAM_B1 = 0.9
ADAM_B2 = 0.999
ADAM_EPS = 1e-08
ADAM_WD = 0.01
ADAM_STEP = 10

VMEM_LIMIT_BYTES = 58 * 1024 * 1024
MESH = pl.DeviceIdType.MESH


def _params(*semantics):
    return pltpu.CompilerParams(dimension_semantics=semantics, vmem_limit_bytes=VMEM_LIMIT_BYTES)


def _dot(a, b):
    return jnp.dot(a.astype(MXU_DTYPE), b.astype(MXU_DTYPE), preferred_element_type=F32)


def _dot_nt(a, b):
    return lax.dot_general(a.astype(MXU_DTYPE), b.astype(MXU_DTYPE), (((1,), (1,)), ((), ())),
                           preferred_element_type=F32)


def _dot_tn(a, b):
    return lax.dot_general(a.astype(MXU_DTYPE), b.astype(MXU_DTYPE), (((0,), (0,)), ((), ())),
                           preferred_element_type=F32)


def _full(shape):
    return pl.BlockSpec(shape, lambda *_: (0,) * len(shape))


def _any():
    return pl.BlockSpec(memory_space=pl.ANY)


def _rowsum(v):
    return jnp.sum(v, axis=0, keepdims=True)


def _mean_last(v):
    return jnp.mean(v, axis=-1, keepdims=True)


def _all_gather8(block, name, split=False):
    blk_shape = block.shape[1:] if split else block.shape

    def body(x_ref, out_ref, send_sems, recv_sems, local_sem):
        x, y, c = lax.axis_index("x"), lax.axis_index("y"), lax.axis_index("c")
        me, sibling = (x, y, c), (x, y, 1 - c)
        chips = [(1 - x, y), (x, 1 - y), (1 - x, 1 - y)]
        src_mine = x_ref.at[c] if split else x_ref

        def slot(px, py, pc):
            return out_ref.at[4 * px + 2 * py + pc]

        def copy(k, blk, to, src=None):
            return pltpu.make_async_remote_copy(
                src_ref=slot(*blk) if src is None else src, dst_ref=slot(*blk),
                send_sem=send_sems.at[k], recv_sem=recv_sems.at[k],
                device_id=to, device_id_type=MESH)

        mine = pltpu.make_async_copy(src_mine, slot(*me), local_sem)
        mine.start()
        first = [copy(0, me, sibling, src=src_mine)]
        first += [copy(1 + j, me, (*chip, c), src=src_mine) for j, chip in enumerate(chips)]
        for cp in first:
            cp.start()
        passed = [copy(4 + j, (*chip, c), sibling) for j, chip in enumerate(chips)]
        for j, chip in enumerate(chips):
            copy(1 + j, (*chip, c), me).wait_recv()
            passed[j].start()
        copy(0, sibling, me).wait_recv()
        for j, chip in enumerate(chips):
            copy(4 + j, (*chip, 1 - c), me).wait_recv()
        for cp in first + passed:
            cp.wait_send()
        mine.wait()

    return pl.pallas_call(
        body, name=name,
        out_shape=jax.ShapeDtypeStruct((N_DEV,) + tuple(blk_shape), block.dtype),
        in_specs=[_any()], out_specs=_any(),
        scratch_shapes=[pltpu.SemaphoreType.DMA((7,)), pltpu.SemaphoreType.DMA((7,)),
                        pltpu.SemaphoreType.DMA],
    )(block)


def _sibling_exchange_halves(parts, name):
    n_arr = len(parts)

    def body(*refs):
        in_refs = refs[:n_arr]
        out_refs = refs[n_arr:2 * n_arr]
        send_sems, recv_sems = refs[2 * n_arr:]
        x, y, c = lax.axis_index("x"), lax.axis_index("y"), lax.axis_index("c")
        copies = []
        for a in range(n_arr):
            for k in range(N_CHIPS):
                copies.append(pltpu.make_async_remote_copy(
                    src_ref=in_refs[a].at[k, 1 - c], dst_ref=out_refs[a].at[k],
                    send_sem=send_sems.at[a * N_CHIPS + k], recv_sem=recv_sems.at[a * N_CHIPS + k],
                    device_id=(x, y, 1 - c), device_id_type=MESH))
        for cp in copies:
            cp.start()
        for cp in copies:
            cp.wait()

    return pl.pallas_call(
        body, name=name,
        out_shape=[jax.ShapeDtypeStruct((N_CHIPS,) + p.shape[2:], p.dtype) for p in parts],
        in_specs=[_any()] * n_arr, out_specs=[_any()] * n_arr,
        scratch_shapes=[pltpu.SemaphoreType.DMA((n_arr * N_CHIPS,)),
                        pltpu.SemaphoreType.DMA((n_arr * N_CHIPS,))],
    )(*parts)


def _chip_scatter(halves, name):
    n_arr = len(halves)

    def body(*refs):
        in_refs = refs[:n_arr]
        out_refs = refs[n_arr:2 * n_arr]
        send_sems, recv_sems, local_sems = refs[2 * n_arr:]
        x, y, c = lax.axis_index("x"), lax.axis_index("y"), lax.axis_index("c")
        chips = [(1 - x, y), (x, 1 - y), (1 - x, 1 - y)]
        copies, local = [], []
        for a in range(n_arr):
            for j, (px, py) in enumerate(chips):
                copies.append(pltpu.make_async_remote_copy(
                    src_ref=in_refs[a].at[2 * px + py], dst_ref=out_refs[a].at[j],
                    send_sem=send_sems.at[a * 3 + j], recv_sem=recv_sems.at[a * 3 + j],
                    device_id=(px, py, c), device_id_type=MESH))
            local.append(pltpu.make_async_copy(in_refs[a].at[2 * x + y], out_refs[a].at[3],
                                               local_sems.at[a]))
        for cp in copies + local:
            cp.start()
        for cp in copies + local:
            cp.wait()

    return pl.pallas_call(
        body, name=name,
        out_shape=[jax.ShapeDtypeStruct(h.shape, h.dtype) for h in halves],
        in_specs=[_any()] * n_arr, out_specs=[_any()] * n_arr,
        scratch_shapes=[pltpu.SemaphoreType.DMA((n_arr * 3,)), pltpu.SemaphoreType.DMA((n_arr * 3,)),
                        pltpu.SemaphoreType.DMA((n_arr,))],
    )(*halves)


def _sibling_share(halves, name):
    n_arr = len(halves)

    def body(*refs):
        in_refs = refs[:n_arr]
        out_refs = refs[n_arr:2 * n_arr]
        send_sems, recv_sems, local_sems = refs[2 * n_arr:]
        x, y, c = lax.axis_index("x"), lax.axis_index("y"), lax.axis_index("c")
        copies = []
        for a in range(n_arr):
            copies.append(pltpu.make_async_remote_copy(
                src_ref=in_refs[a], dst_ref=out_refs[a].at[c],
                send_sem=send_sems.at[a], recv_sem=recv_sems.at[a],
                device_id=(x, y, 1 - c), device_id_type=MESH))
        local = [pltpu.make_async_copy(in_refs[a], out_refs[a].at[c], local_sems.at[a])
                 for a in range(n_arr)]
        for cp in copies + local:
            cp.start()
        for a in range(n_arr):
            pltpu.make_async_remote_copy(
                src_ref=in_refs[a], dst_ref=out_refs[a].at[1 - c],
                send_sem=send_sems.at[a], recv_sem=recv_sems.at[a],
                device_id=(x, y, 1 - c), device_id_type=MESH).wait()
        for cp in local:
            cp.wait()

    return pl.pallas_call(
        body, name=name,
        out_shape=[jax.ShapeDtypeStruct((2,) + h.shape, h.dtype) for h in halves],
        in_specs=[_any()] * n_arr, out_specs=[_any()] * n_arr,
        scratch_shapes=[pltpu.SemaphoreType.DMA((n_arr,)), pltpu.SemaphoreType.DMA((n_arr,)),
                        pltpu.SemaphoreType.DMA((n_arr,))],
    )(*halves)


def _gelu_tanh(z):
    k = math.sqrt(2.0 / math.pi)
    t = jnp.tanh(k * (z + 0.044715 * (z * z * z)))
    return 0.5 * z * (1.0 + t), t


def _gelu_tanh_grad(z, t):
    k = math.sqrt(2.0 / math.pi)
    return 0.5 * (1.0 + t) + 0.5 * z * (1.0 - t * t) * (k * (1.0 + 3.0 * 0.044715 * (z * z)))


def _rope_tables(pos_col, invf_row):
    ang = pos_col.astype(F32) * invf_row
    cos, sin = jnp.cos(ang), jnp.sin(ang)
    d = lax.broadcasted_iota(jnp.int32, ang.shape, 1) & (HEAD_DIM - 1)
    half = ROT_DIM // 2
    c_tab = jnp.where(d < ROT_DIM, cos, 1.0)
    s1 = jnp.where(d < half, -sin, 0.0)
    s2 = jnp.where((d >= half) & (d < ROT_DIM), sin, 0.0)
    return c_tab, s1, s2


def _rope_apply(t, tabs, sign):
    c_tab, s1, s2 = tabs
    reps = t.shape[1] // LANES
    if reps > 1:
        c_tab, s1, s2 = (jnp.tile(v, (1, reps)) for v in (c_tab, s1, s2))
    half = ROT_DIM // 2
    up = pltpu.roll(t, t.shape[1] - half, 1)
    down = pltpu.roll(t, half, 1)
    return t * c_tab + sign * (up * s1 + down * s2)


def _lane_masks(shape):
    lane = lax.broadcasted_iota(jnp.int32, shape, 1)
    return lane < HEAD_DIM, lane >= HEAD_DIM


def _attn_mask(first_block):
    qi = lax.broadcasted_iota(jnp.int32, (CHUNK, 2 * CHUNK), 0)
    kj = lax.broadcasted_iota(jnp.int32, (CHUNK, 2 * CHUNK), 1)
    band = (kj > qi) & (kj <= qi + CHUNK)
    return band & (jnp.logical_not(first_block) | (kj >= CHUNK))


def _head_plan():
    plan = []
    for h in range(N_Q_HEADS):
        pair, hi = h // 2, h % 2
        group = h // (N_Q_HEADS // N_KV_HEADS)
        plan.append((pair, hi, (group == 1) != (hi == 1)))
    return plan


def _softmax_sink(s, sink):
    m = jnp.maximum(jnp.max(s, axis=-1, keepdims=True), sink)
    p = jnp.exp(s - m)
    e_sink = jnp.exp(sink - m)
    den = jnp.sum(p, axis=-1, keepdims=True) + e_sink
    return p / den, e_sink / den


def _sgu_forward_pair(wm, vp, j):
    lo, hi = _lane_masks(vp.shape)
    lhs = jnp.concatenate([wm[2 * j], wm[2 * j + 1]], axis=1)
    rhs = jnp.concatenate([jnp.where(lo, vp, 0.0), jnp.where(hi, vp, 0.0)], axis=0)
    return _dot(lhs, rhs)


def _masked_spatial(w_ref):
    t = lax.broadcasted_iota(jnp.int32, (CHUNK, CHUNK), 0)
    s = lax.broadcasted_iota(jnp.int32, (CHUNK, CHUNK), 1)
    tril = s <= t
    return [jnp.where(tril, w_ref[g], 0.0) for g in range(GMLP_GROUPS)], tril


def _mod_kernel(c_all, w_shard, b_shard):
    n = w_shard.shape[1]
    tn = 512

    def body(c_ref, w_ref, b_ref, mod_ref, act_ref):
        cv = c_ref[...]
        act = cv * (1.0 / (1.0 + jnp.exp(-cv)))
        act_ref[...] = act
        mod_ref[...] = _dot(act, w_ref[...]) + b_ref[...]

    return pl.pallas_call(
        body, name="ada_mod", grid=(n // tn,),
        out_shape=[jax.ShapeDtypeStruct((N_DEV, n), F32), jax.ShapeDtypeStruct((N_DEV, D_MODEL), F32)],
        in_specs=[_full((N_DEV, D_MODEL)), pl.BlockSpec((D_MODEL, tn), lambda i: (0, i)),
                  pl.BlockSpec((1, tn), lambda i: (0, i))],
        out_specs=[pl.BlockSpec((N_DEV, tn), lambda i: (0, i)), _full((N_DEV, D_MODEL))],
        compiler_params=_params("arbitrary"),
    )(c_all, w_shard, b_shard)


def _in_proj_kernel(x, vecs, w_in):
    seq = x.shape[0]
    tm = 512

    def body(x_ref, v_ref, w_ref, proj_ref, h_ref):
        xv = x_ref[...]
        rstd = lax.rsqrt(_mean_last(xv * xv) + EPS)
        n1 = (xv * rstd) * v_ref[0:1, :]
        h = n1 * (1.0 + v_ref[2:3, :]) + v_ref[1:2, :]
        hb = h.astype(MXU_DTYPE)
        h_ref[...] = hb
        proj_ref[...] = _dot(hb, w_ref[...])

    return pl.pallas_call(
        body, name="in_proj", grid=(seq // tm,),
        out_shape=[jax.ShapeDtypeStruct((seq, IN_PROJ_WIDTH), F32),
                   jax.ShapeDtypeStruct((seq, D_MODEL), MXU_DTYPE)],
        in_specs=[pl.BlockSpec((tm, D_MODEL), lambda i: (i, 0)), _full((8, D_MODEL)),
                  _full((D_MODEL, IN_PROJ_WIDTH))],
        out_specs=[pl.BlockSpec((tm, IN_PROJ_WIDTH), lambda i: (i, 0)),
                   pl.BlockSpec((tm, D_MODEL), lambda i: (i, 0))],
        compiler_params=_params("parallel"),
    )(x, vecs, w_in)


def _mixer_fwd_kernel(proj, pos, w_spatial, bias_full, sink_rows, invf):
    seq = proj.shape[0]
    nb = seq // CHUNK
    kv_col = (2 * GMLP_WIDTH + ATTN_WIDTH) // (2 * KV_WIDTH)

    def body(proj_ref, prev_ref, pos_ref, ppos_ref, w_ref, bias_ref, sink_ref, invf_ref, cat_ref):
        i = pl.program_id(0)
        wm, _ = _masked_spatial(w_ref)
        for j in range(GMLP_GROUPS // 2):
            cols = slice(LANES * j, LANES * (j + 1))
            vcols = slice(GMLP_WIDTH + LANES * j, GMLP_WIDTH + LANES * (j + 1))
            u, _ = _gelu_tanh(proj_ref[:, cols])
            vp, _ = _gelu_tanh(proj_ref[:, vcols])
            sv = _sgu_forward_pair(wm, vp, j) + bias_ref[:, cols]
            cat_ref[:, cols] = (u * sv).astype(cat_ref.dtype)
        tabs = _rope_tables(pos_ref[...], invf_ref[...])
        ptabs = _rope_tables(ppos_ref[...], invf_ref[...])
        o = 2 * GMLP_WIDTH
        q_r = _rope_apply(proj_ref[:, o:o + ATTN_WIDTH], tabs, 1.0)
        k_cur = _rope_apply(proj_ref[:, o + ATTN_WIDTH:o + ATTN_WIDTH + KV_WIDTH], tabs, 1.0)
        k_prev = _rope_apply(prev_ref[:, 0:KV_WIDTH], ptabs, 1.0)
        k_a = jnp.concatenate([k_prev, k_cur], axis=0)
        v_a = jnp.concatenate([prev_ref[:, KV_WIDTH:2 * KV_WIDTH],
                               proj_ref[:, o + ATTN_WIDTH + KV_WIDTH:o + ATTN_WIDTH + 2 * KV_WIDTH]], axis=0)
        k_b = pltpu.roll(k_a, HEAD_DIM, 1)
        v_b = pltpu.roll(v_a, HEAD_DIM, 1)
        mask = _attn_mask(i == 0)
        lo, hi = _lane_masks((CHUNK, LANES))
        lo2, hi2 = _lane_masks((2 * CHUNK, LANES))
        outs = [None] * (N_Q_HEADS // 2)
        for h, (pair, is_hi, rolled) in enumerate(_head_plan()):
            qp = q_r[:, LANES * pair:LANES * (pair + 1)]
            qm = jnp.where(hi if is_hi else lo, qp, 0.0)
            s = _dot_nt(qm, k_b if rolled else k_a) * (1.0 / math.sqrt(HEAD_DIM))
            s = jnp.where(mask, s, -jnp.inf)
            probs, _ = _softmax_sink(s, sink_ref[h:h + 1, 0:1])
            vm = jnp.where(hi2 if is_hi else lo2, v_b if rolled else v_a, 0.0)
            oh = _dot(probs, vm)
            outs[pair] = oh if outs[pair] is None else outs[pair] + oh
        for pair in range(N_Q_HEADS // 2):
            cols = slice(GMLP_WIDTH + LANES * pair, GMLP_WIDTH + LANES * (pair + 1))
            cat_ref[:, cols] = outs[pair].astype(cat_ref.dtype)

    return pl.pallas_call(
        body, name="mixer_fwd", grid=(nb,),
        out_shape=jax.ShapeDtypeStruct((seq, D_MODEL), MXU_DTYPE),
        in_specs=[pl.BlockSpec((CHUNK, IN_PROJ_WIDTH), lambda i: (i, 0)),
                  pl.BlockSpec((CHUNK, 2 * KV_WIDTH), lambda i: (jnp.maximum(i - 1, 0), kv_col)),
                  pl.BlockSpec((CHUNK, 1), lambda i: (i, 0)),
                  pl.BlockSpec((CHUNK, 1), lambda i: (jnp.maximum(i - 1, 0), 0)),
                  _full((GMLP_GROUPS, CHUNK, CHUNK)), _full((CHUNK, GMLP_WIDTH)),
                  _full((N_Q_HEADS, LANES)), _full((1, LANES))],
        out_specs=pl.BlockSpec((CHUNK, D_MODEL), lambda i: (i, 0)),
        compiler_params=_params("parallel"),
    )(proj, proj, pos, pos, w_spatial, bias_full, sink_rows, invf)


def _trunk_kernel(x, target, cat, vecs, w_out, w_ff1, w_ff2):
    seq = x.shape[0]
    tm = 256
    nj = D_FF // D_MODEL

    def body(x_ref, t_ref, cat_ref, v_ref, wout_hbm, w1_hbm, w2_hbm,
             dx1_ref, dcat_ref, dmix_ref, h2_ref, r_ref, da_ref, dff_ref, sums_ref,
             wout, w1, w2, a_scr, sem):
        i = pl.program_id(0)

        @pl.when(i == 0)
        def _():
            copies = [pltpu.make_async_copy(wout_hbm, wout, sem.at[0]),
                      pltpu.make_async_copy(w1_hbm, w1, sem.at[1]),
                      pltpu.make_async_copy(w2_hbm, w2, sem.at[2])]
            for cp in copies:
                cp.start()
            for cp in copies:
                cp.wait()
            sums_ref[...] = jnp.zeros_like(sums_ref)

        gate1, shift2, scale2 = v_ref[0:1, :], v_ref[1:2, :], v_ref[2:3, :]
        gate2, g_ffn, g_final = v_ref[3:4, :], v_ref[4:5, :], v_ref[5:6, :]

        mix = _dot(cat_ref[...], wout[...])
        x1 = x_ref[...] + gate1 * mix
        rstd2 = lax.rsqrt(_mean_last(x1 * x1) + EPS)
        xh2 = x1 * rstd2
        n2 = xh2 * g_ffn
        h2b = (n2 * (1.0 + scale2) + shift2).astype(MXU_DTYPE)
        h2_ref[...] = h2b
        ff = jnp.zeros((tm, D_MODEL), F32)
        for j in range(nj):
            a = _dot(h2b, w1[j])
            a_scr[j] = a
            relu = jnp.maximum(a, 0.0)
            rb = (relu * relu).astype(MXU_DTYPE)
            r_ref[:, D_MODEL * j:D_MODEL * (j + 1)] = rb
            ff = ff + _dot(rb, w2[j])
        x2 = x1 + gate2 * ff
        rstd3 = lax.rsqrt(_mean_last(x2 * x2) + EPS)
        xh3 = x2 * rstd3
        err = xh3 * g_final - t_ref[...]
        loss = 0.5 * _rowsum(_mean_last(err * err))
        dy = err * (1.0 / D_MODEL)
        dxh3 = dy * g_final
        dx2 = rstd3 * (dxh3 - xh3 * _mean_last(dxh3 * xh3))
        dffb = (dx2 * gate2).astype(MXU_DTYPE)
        dff_ref[...] = dffb
        dh2 = jnp.zeros((tm, D_MODEL), F32)
        for j in range(nj):
            dr = _dot_nt(dffb, w2[j])
            dab = (dr * (2.0 * jnp.maximum(a_scr[j], 0.0))).astype(MXU_DTYPE)
            da_ref[:, D_MODEL * j:D_MODEL * (j + 1)] = dab
            dh2 = dh2 + _dot_nt(dab, w1[j])
        dn2 = dh2 * (1.0 + scale2)
        dxh2 = dn2 * g_ffn
        dx1 = dx2 + rstd2 * (dxh2 - xh2 * _mean_last(dxh2 * xh2))
        dx1_ref[...] = dx1
        dmixb = (dx1 * gate1).astype(MXU_DTYPE)
        dmix_ref[...] = dmixb
        dcat_ref[...] = _dot_nt(dmixb, wout[...])

        sums_ref[0:1, :] += _rowsum(dh2)
        sums_ref[1:2, :] += _rowsum(dh2 * n2)
        sums_ref[2:3, :] += _rowsum(dx2 * ff)
        sums_ref[3:4, :] += _rowsum(dn2 * xh2)
        sums_ref[4:5, :] += _rowsum(dy * xh3)
        sums_ref[5:6, :] += _rowsum(dx1 * mix)
        sums_ref[6:7, :] += jnp.broadcast_to(loss, (1, D_MODEL))

    tok = lambda w: pl.BlockSpec((tm, w), lambda i: (i, 0))
    return pl.pallas_call(
        body, name="trunk", grid=(seq // tm,),
        out_shape=[jax.ShapeDtypeStruct((seq, D_MODEL), F32), jax.ShapeDtypeStruct((seq, D_MODEL), F32),
                   jax.ShapeDtypeStruct((seq, D_MODEL), MXU_DTYPE), jax.ShapeDtypeStruct((seq, D_MODEL), MXU_DTYPE),
                   jax.ShapeDtypeStruct((seq, D_FF), MXU_DTYPE), jax.ShapeDtypeStruct((seq, D_FF), MXU_DTYPE),
                   jax.ShapeDtypeStruct((seq, D_MODEL), MXU_DTYPE), jax.ShapeDtypeStruct((8, D_MODEL), F32)],
        in_specs=[tok(D_MODEL), tok(D_MODEL), tok(D_MODEL), _full((8, D_MODEL)), _any(), _any(), _any()],
        out_specs=[tok(D_MODEL), tok(D_MODEL), tok(D_MODEL), tok(D_MODEL), tok(D_FF), tok(D_FF), tok(D_MODEL),
                   _full((8, D_MODEL))],
        scratch_shapes=[pltpu.VMEM((D_MODEL, D_MODEL), MXU_DTYPE), pltpu.VMEM((nj, D_MODEL, D_MODEL), MXU_DTYPE),
                        pltpu.VMEM((nj, D_MODEL, D_MODEL), MXU_DTYPE), pltpu.VMEM((nj, tm, D_MODEL), F32),
                        pltpu.SemaphoreType.DMA((3,))],
        compiler_params=_params("arbitrary"),
    )(x, target, cat, vecs, w_out, w_ff1, w_ff2)


def _mixer_bwd_kernel(proj, pos, dcat, w_spatial, bias_full, sink_rows, invf):
    seq = proj.shape[0]
    nb = seq // CHUNK
    kv_col = (2 * GMLP_WIDTH + ATTN_WIDTH) // (2 * KV_WIDTH)

    def body(proj_ref, prev_ref, pos_ref, ppos_ref, dcat_ref, w_ref, bias_ref, sink_ref, invf_ref,
             dproj_ref, dw_ref, db_ref, dsink_ref, carry):
        step = pl.program_id(0)
        blk = nb - 1 - step

        @pl.when(step == 0)
        def _():
            carry[...] = jnp.zeros_like(carry)
            dw_ref[...] = jnp.zeros_like(dw_ref)
            db_ref[...] = jnp.zeros_like(db_ref)
            dsink_ref[...] = jnp.zeros_like(dsink_ref)

        wm, tril = _masked_spatial(w_ref)
        lo, hi = _lane_masks((CHUNK, LANES))
        for j in range(GMLP_GROUPS // 2):
            cols = slice(LANES * j, LANES * (j + 1))
            vcols = slice(GMLP_WIDTH + LANES * j, GMLP_WIDTH + LANES * (j + 1))
            zu, zv = proj_ref[:, cols], proj_ref[:, vcols]
            u, tu = _gelu_tanh(zu)
            vp, tv = _gelu_tanh(zv)
            sv = _sgu_forward_pair(wm, vp, j) + bias_ref[:, cols]
            dout = dcat_ref[:, cols]
            du = dout * sv
            dsv = dout * u
            dsv_lo, dsv_hi = jnp.where(lo, dsv, 0.0), jnp.where(hi, dsv, 0.0)
            lhs_t = jnp.concatenate([wm[2 * j].T, wm[2 * j + 1].T], axis=1)
            dv = _dot(lhs_t, jnp.concatenate([dsv_lo, dsv_hi], axis=0))
            dw_ref[2 * j] += jnp.where(tril, _dot_nt(dsv_lo, vp), 0.0)
            dw_ref[2 * j + 1] += jnp.where(tril, _dot_nt(dsv_hi, vp), 0.0)
            lane = lax.broadcasted_iota(jnp.int32, (CHUNK, LANES), 1)
            db_ref[...] += (jnp.where(lane == 2 * j, jnp.sum(dsv_lo, axis=1, keepdims=True), 0.0)
                            + jnp.where(lane == 2 * j + 1, jnp.sum(dsv_hi, axis=1, keepdims=True), 0.0))
            dproj_ref[:, cols] = (du * _gelu_tanh_grad(zu, tu)).astype(dproj_ref.dtype)
            dproj_ref[:, vcols] = (dv * _gelu_tanh_grad(zv, tv)).astype(dproj_ref.dtype)
        tabs = _rope_tables(pos_ref[...], invf_ref[...])
        ptabs = _rope_tables(ppos_ref[...], invf_ref[...])
        o = 2 * GMLP_WIDTH
        q_r = _rope_apply(proj_ref[:, o:o + ATTN_WIDTH], tabs, 1.0)
        k_cur = _rope_apply(proj_ref[:, o + ATTN_WIDTH:o + ATTN_WIDTH + KV_WIDTH], tabs, 1.0)
        k_prev = _rope_apply(prev_ref[:, 0:KV_WIDTH], ptabs, 1.0)
        k_a = jnp.concatenate([k_prev, k_cur], axis=0)
        v_a = jnp.concatenate([prev_ref[:, KV_WIDTH:2 * KV_WIDTH],
                               proj_ref[:, o + ATTN_WIDTH + KV_WIDTH:o + ATTN_WIDTH + 2 * KV_WIDTH]], axis=0)
        k_b = pltpu.roll(k_a, HEAD_DIM, 1)
        v_b = pltpu.roll(v_a, HEAD_DIM, 1)
        mask = _attn_mask(blk == 0)
        lo2, hi2 = _lane_masks((2 * CHUNK, LANES))
        scale = 1.0 / math.sqrt(HEAD_DIM)
        zero_kv = jnp.zeros((2 * CHUNK, LANES), F32)
        dk_a, dk_b, dv_a, dv_b = zero_kv, zero_kv, zero_kv, zero_kv
        dq_pairs = [None] * (N_Q_HEADS // 2)
        for h, (pair, is_hi, rolled) in enumerate(_head_plan()):
            half, half2 = (hi, hi2) if is_hi else (lo, lo2)
            qm = jnp.where(half, q_r[:, LANES * pair:LANES * (pair + 1)], 0.0)
            k_sel, v_sel = (k_b, v_b) if rolled else (k_a, v_a)
            s = jnp.where(mask, _dot_nt(qm, k_sel) * scale, -jnp.inf)
            probs, p_sink = _softmax_sink(s, sink_ref[h:h + 1, 0:1])
            dom = jnp.where(half, dcat_ref[:, GMLP_WIDTH + LANES * pair:GMLP_WIDTH + LANES * (pair + 1)], 0.0)
            dprobs = _dot_nt(dom, v_sel)
            delta = jnp.sum(probs * dprobs, axis=-1, keepdims=True)
            ds = probs * (dprobs - delta) * scale
            dsink_ref[h:h + 1, :] += jnp.broadcast_to(_rowsum(-p_sink * delta), (1, LANES))
            dqh = _dot(ds, jnp.where(half2, k_sel, 0.0))
            dq_pairs[pair] = dqh if dq_pairs[pair] is None else dq_pairs[pair] + dqh
            dkh = _dot_tn(ds, qm)
            dvh = _dot_tn(probs, dom)
            if rolled:
                dk_b, dv_b = dk_b + dkh, dv_b + dvh
            else:
                dk_a, dv_a = dk_a + dkh, dv_a + dvh
        dk_all = dk_a + pltpu.roll(dk_b, HEAD_DIM, 1)
        dv_all = dv_a + pltpu.roll(dv_b, HEAD_DIM, 1)
        dk_cur = dk_all[CHUNK:, :] + carry[:, 0:KV_WIDTH]
        dv_cur = dv_all[CHUNK:, :] + carry[:, KV_WIDTH:2 * KV_WIDTH]
        carry[:, 0:KV_WIDTH] = dk_all[:CHUNK, :]
        carry[:, KV_WIDTH:2 * KV_WIDTH] = dv_all[:CHUNK, :]
        dq = _rope_apply(jnp.concatenate(dq_pairs, axis=1), tabs, -1.0)
        dproj_ref[:, o:o + ATTN_WIDTH] = dq.astype(dproj_ref.dtype)
        dproj_ref[:, o + ATTN_WIDTH:o + ATTN_WIDTH + KV_WIDTH] = (
            _rope_apply(dk_cur, tabs, -1.0).astype(dproj_ref.dtype))
        dproj_ref[:, o + ATTN_WIDTH + KV_WIDTH:o + ATTN_WIDTH + 2 * KV_WIDTH] = dv_cur.astype(dproj_ref.dtype)

    rev = lambda i: nb - 1 - i
    return pl.pallas_call(
        body, name="mixer_bwd", grid=(nb,),
        out_shape=[jax.ShapeDtypeStruct((seq, IN_PROJ_WIDTH), MXU_DTYPE),
                   jax.ShapeDtypeStruct((GMLP_GROUPS, CHUNK, CHUNK), F32),
                   jax.ShapeDtypeStruct((CHUNK, LANES), F32),
                   jax.ShapeDtypeStruct((N_Q_HEADS, LANES), F32)],
        in_specs=[pl.BlockSpec((CHUNK, IN_PROJ_WIDTH), lambda i: (rev(i), 0)),
                  pl.BlockSpec((CHUNK, 2 * KV_WIDTH), lambda i: (jnp.maximum(rev(i) - 1, 0), kv_col)),
                  pl.BlockSpec((CHUNK, 1), lambda i: (rev(i), 0)),
                  pl.BlockSpec((CHUNK, 1), lambda i: (jnp.maximum(rev(i) - 1, 0), 0)),
                  pl.BlockSpec((CHUNK, D_MODEL), lambda i: (rev(i), 0)),
                  _full((GMLP_GROUPS, CHUNK, CHUNK)), _full((CHUNK, GMLP_WIDTH)),
                  _full((N_Q_HEADS, LANES)), _full((1, LANES))],
        out_specs=[pl.BlockSpec((CHUNK, IN_PROJ_WIDTH), lambda i: (rev(i), 0)),
                   _full((GMLP_GROUPS, CHUNK, CHUNK)), _full((CHUNK, LANES)), _full((N_Q_HEADS, LANES))],
        scratch_shapes=[pltpu.VMEM((CHUNK, 2 * KV_WIDTH), F32)],
        compiler_params=_params("arbitrary"),
    )(proj, proj, pos, pos, dcat, w_spatial, bias_full, sink_rows, invf)


def _in_proj_bwd_kernel(x, dx1, dproj, vecs, w_in):
    seq = x.shape[0]
    tm = 512

    def body(x_ref, dx1_ref, dp_ref, v_ref, w_ref, gx_ref, sums_ref):
        @pl.when(pl.program_id(0) == 0)
        def _():
            sums_ref[...] = jnp.zeros_like(sums_ref)

        g_mix, scale1 = v_ref[0:1, :], v_ref[2:3, :]
        dh = _dot_nt(dp_ref[...], w_ref[...])
        xv = x_ref[...]
        rstd = lax.rsqrt(_mean_last(xv * xv) + EPS)
        xh = xv * rstd
        dn1 = dh * (1.0 + scale1)
        dxh = dn1 * g_mix
        gx_ref[...] = dx1_ref[...] + rstd * (dxh - xh * _mean_last(dxh * xh))
        sums_ref[0:1, :] += _rowsum(dh)
        sums_ref[1:2, :] += _rowsum(dh * (xh * g_mix))
        sums_ref[2:3, :] += _rowsum(dn1 * xh)

    return pl.pallas_call(
        body, name="in_proj_bwd", grid=(seq // tm,),
        out_shape=[jax.ShapeDtypeStruct((seq, D_MODEL), F32), jax.ShapeDtypeStruct((8, D_MODEL), F32)],
        in_specs=[pl.BlockSpec((tm, D_MODEL), lambda i: (i, 0)), pl.BlockSpec((tm, D_MODEL), lambda i: (i, 0)),
                  pl.BlockSpec((tm, IN_PROJ_WIDTH), lambda i: (i, 0)), _full((8, D_MODEL)),
                  _full((D_MODEL, IN_PROJ_WIDTH))],
        out_specs=[pl.BlockSpec((tm, D_MODEL), lambda i: (i, 0)), _full((8, D_MODEL))],
        compiler_params=_params("arbitrary"),
    )(x, dx1, dproj, vecs, w_in)


def _weight_grad_kernel(a, b, name, tm, tn, blocked_cols=False):
    seq, m = a.shape
    n = b.shape[1]
    tk = 512
    nk = seq // tk

    def body(a_ref, b_ref, o_ref, acc):
        k = pl.program_id(2)

        @pl.when(k == 0)
        def _():
            acc[...] = jnp.zeros_like(acc)

        acc[...] += _dot_tn(a_ref[...], b_ref[...])

        @pl.when(k == nk - 1)
        def _():
            o_ref[...] = acc[...]

    if blocked_cols:
        out_shape = jax.ShapeDtypeStruct((n // tn, m, tn), F32)
        out_spec = pl.BlockSpec((None, tm, tn), lambda i, j, k: (j, i, 0))
    else:
        out_shape = jax.ShapeDtypeStruct((m, n), F32)
        out_spec = pl.BlockSpec((tm, tn), lambda i, j, k: (i, j))
    return pl.pallas_call(
        body, name=name, grid=(m // tm, n // tn, nk), out_shape=out_shape,
        in_specs=[pl.BlockSpec((tk, tm), lambda i, j, k: (k, i)), pl.BlockSpec((tk, tn), lambda i, j, k: (k, j))],
        out_specs=out_spec, scratch_shapes=[pltpu.VMEM((tm, tn), F32)],
        compiler_params=_params("parallel", "parallel", "arbitrary"),
    )(a, b)


def _adam_update(w, g, m, v):
    m_new = ADAM_B1 * m + (1.0 - ADAM_B1) * g
    v_new = ADAM_B2 * v + (1.0 - ADAM_B2) * (g * g)
    m_hat = m_new / (1.0 - ADAM_B1 ** ADAM_STEP)
    v_hat = v_new / (1.0 - ADAM_B2 ** ADAM_STEP)
    delta = -ADAM_LR * (m_hat / (jnp.sqrt(v_hat) + ADAM_EPS) + ADAM_WD * w)
    return delta, m_new, v_new


def _add_halves_kernel(part, recv, c_idx, name):
    _, _, r, n = part.shape
    tr = min(r, 256)

    def body(c_ref, p_ref, q_ref, o_ref):
        del c_ref
        o_ref[...] = (p_ref[...] + q_ref[...]).astype(o_ref.dtype)

    return pl.pallas_call(
        body, name=name, out_shape=jax.ShapeDtypeStruct((N_CHIPS, r, n), GRAD_COMM_DTYPE),
        grid_spec=pltpu.PrefetchScalarGridSpec(
            num_scalar_prefetch=1, grid=(N_CHIPS, r // tr),
            in_specs=[pl.BlockSpec((None, None, tr, n), lambda k, i, c: (k, c[0], i, 0)),
                      pl.BlockSpec((None, tr, n), lambda k, i, c: (k, i, 0))],
            out_specs=pl.BlockSpec((None, tr, n), lambda k, i, c: (k, i, 0))),
        compiler_params=_params("parallel", "parallel"),
    )(c_idx, part, recv)


def _sum_chips_kernel(parts, name):
    _, r, n = parts.shape
    tr = min(r, 256)

    def body(p_ref, o_ref):
        acc = p_ref[0].astype(F32)
        for k in range(1, N_CHIPS):
            acc = acc + p_ref[k].astype(F32)
        o_ref[...] = acc

    return pl.pallas_call(
        body, name=name, grid=(r // tr,), out_shape=jax.ShapeDtypeStruct((r, n), F32),
        in_specs=[pl.BlockSpec((N_CHIPS, tr, n), lambda i: (0, i, 0))],
        out_specs=pl.BlockSpec((tr, n), lambda i: (i, 0)),
        compiler_params=_params("parallel"),
    )(parts)


def _adam_kernel(w, g, m, v, name):
    r, n = w.shape
    tr = min(r, 256)

    def body(w_ref, g_ref, m_ref, v_ref, d_ref, mo_ref, vo_ref):
        d_ref[...], mo_ref[...], vo_ref[...] = _adam_update(w_ref[...], g_ref[...], m_ref[...], v_ref[...])

    spec = pl.BlockSpec((tr, n), lambda i: (i, 0))
    return pl.pallas_call(
        body, name=name, grid=(r // tr,), out_shape=[jax.ShapeDtypeStruct((r, n), F32)] * 3,
        in_specs=[spec] * 4, out_specs=[spec] * 3, compiler_params=_params("parallel"),
    )(w, g, m, v)


def _small_update_kernel(gathered, w, m, v):
    _, r, n = gathered.shape

    def body(g_ref, w_ref, m_ref, v_ref, gs_ref, d_ref, mo_ref, vo_ref):
        g = g_ref[0]
        for k in range(1, N_DEV):
            g = g + g_ref[k]
        gs_ref[...] = g
        d_ref[...], mo_ref[...], vo_ref[...] = _adam_update(w_ref[...], g, m_ref[...], v_ref[...])

    return pl.pallas_call(
        body, name="small_update", grid=(1,), out_shape=[jax.ShapeDtypeStruct((r, n), F32)] * 4,
        in_specs=[_full((N_DEV, r, n))] + [_full((r, n))] * 3, out_specs=[_full((r, n))] * 4,
        compiler_params=_params("arbitrary"),
    )(gathered, w, m, v)


def _ada_update_kernel(act_t, dmod, w, m, v):
    r, n = w.shape
    tr = 256

    def body(a_ref, d_ref, w_ref, m_ref, v_ref, g_ref, dl_ref, mo_ref, vo_ref):
        g = _dot(a_ref[...], d_ref[...])
        g_ref[...] = g
        dl_ref[...], mo_ref[...], vo_ref[...] = _adam_update(w_ref[...], g, m_ref[...], v_ref[...])

    spec = pl.BlockSpec((tr, n), lambda i: (i, 0))
    return pl.pallas_call(
        body, name="ada_update", grid=(r // tr,), out_shape=[jax.ShapeDtypeStruct((r, n), F32)] * 4,
        in_specs=[pl.BlockSpec((tr, N_DEV), lambda i: (i, 0)), _full((N_DEV, n)), spec, spec, spec],
        out_specs=[spec] * 4, compiler_params=_params("parallel"),
    )(act_t, dmod, w, m, v)


_SMALL_ROWS = (("b_ada", 48), ("g_mix", 8), ("g_ffn", 8), ("g_final", 8), ("b_spatial", 8),
               ("sinks", 1), ("loss", 1), ("pad", 6), ("w_spatial", 1024))


def _pack_small(parts):
    rows = []
    for name, nrows in _SMALL_ROWS:
        if name in parts:
            flat = parts[name].reshape(-1).astype(F32)
            flat = jnp.pad(flat, (0, nrows * LANES - flat.shape[0]))
            rows.append(flat.reshape(nrows, LANES))
        else:
            rows.append(jnp.zeros((nrows, LANES), F32))
    return jnp.concatenate(rows, axis=0)


def _unpack_small(packed, shapes):
    out, start = {}, 0
    for name, nrows in _SMALL_ROWS:
        if name in shapes:
            size = math.prod(shapes[name])
            out[name] = packed[start:start + nrows].reshape(-1)[:size].reshape(shapes[name])
        start += nrows
    return out


def kernel(x, c, positions, w_ada, b_ada, g_mix, w_in, w_spatial, b_spatial, sinks, w_out, g_ffn, w_ff1, w_ff2, g_final, loss_target, m_w_ada, m_b_ada, m_g_mix, m_w_in, m_w_spatial, m_b_spatial, m_sinks, m_w_out, m_g_ffn, m_w_ff1, m_w_ff2, m_g_final, v_w_ada, v_b_ada, v_g_mix, v_w_in, v_w_spatial, v_b_spatial, v_sinks, v_w_out, v_g_ffn, v_w_ff1, v_w_ff2, v_g_final):
    xi, yi, ci = lax.axis_index("x"), lax.axis_index("y"), lax.axis_index("c")
    chip = 2 * xi + yi
    dev = 2 * chip + ci
    seq = x.shape[1]
    x2, tgt = x[0], loss_target[0]
    pos = positions.reshape(seq, 1)
    ada_cols = w_ada.shape[2]

    c_all = _all_gather8(c, "gather_c").reshape(N_DEV, D_MODEL)
    b_shard = lax.dynamic_slice(b_ada, (0, chip * ada_cols), (1, ada_cols))
    mod_part, act = _mod_kernel(c_all, w_ada[0], b_shard)
    mod_all = _all_gather8(mod_part, "gather_mod")
    mod_me = lax.dynamic_index_in_dim(mod_all[0::2], dev, axis=1, keepdims=False)
    mod_me = mod_me.reshape(N_MOD, D_MODEL)
    shift1, scale1, gate1, shift2, scale2, gate2 = (mod_me[k:k + 1] for k in range(N_MOD))

    def gather_weight(w, name):
        r, n = w.shape[1], w.shape[2]
        halves = w[0].astype(WEIGHT_COMM_DTYPE).reshape(2, r // 2, n)
        return _all_gather8(halves, name, split=True).reshape(N_CHIPS, r, n)

    w_in_full = gather_weight(w_in, "gather_w_in").transpose(1, 0, 2).reshape(D_MODEL, IN_PROJ_WIDTH)
    w_out_full = gather_weight(w_out, "gather_w_out").reshape(D_MODEL, D_MODEL)
    w_ff1_blocks = gather_weight(w_ff1, "gather_w_ff1")
    w_ff2_blocks = gather_weight(w_ff2, "gather_w_ff2")

    zeros_row = jnp.zeros((1, D_MODEL), F32)
    vecs1 = jnp.concatenate([g_mix, shift1, scale1] + [zeros_row] * 5, axis=0)
    vecs2 = jnp.concatenate([gate1, shift2, scale2, gate2, g_ffn, g_final.reshape(1, D_MODEL)]
                            + [zeros_row] * 2, axis=0)
    bias_full = jnp.repeat(b_spatial[0].T, HEAD_DIM, axis=1)
    sink_rows = jnp.broadcast_to(sinks[0][:, None], (N_Q_HEADS, LANES))
    inv_freq = ROPE_THETA ** (-jnp.arange(0, ROT_DIM, 2, dtype=F32) / ROT_DIM)
    lane_d = jnp.arange(LANES) % HEAD_DIM
    invf = jnp.where(lane_d < ROT_DIM, inv_freq[lane_d % (ROT_DIM // 2)], 0.0).reshape(1, LANES)

    proj, hb = _in_proj_kernel(x2, vecs1, w_in_full)
    cat = _mixer_fwd_kernel(proj, pos, w_spatial[0], bias_full, sink_rows, invf)
    dx1, dcat, dmix, h2b, rb, dab, dffb, sums2 = _trunk_kernel(
        x2, tgt, cat, vecs2, w_out_full, w_ff1_blocks, w_ff2_blocks)
    dproj, dw_spatial, db_lanes, dsink_rows = _mixer_bwd_kernel(
        proj, pos, dcat, w_spatial[0], bias_full, sink_rows, invf)
    grad_x, sums1 = _in_proj_bwd_kernel(x2, dx1, dproj, vecs1, w_in_full)

    dw_ff2 = _weight_grad_kernel(rb, dffb, "dw_ff2", 1024, 1024)
    dw_ff1 = _weight_grad_kernel(h2b, dab, "dw_ff1", 1024, 1024, blocked_cols=True)
    dw_out = _weight_grad_kernel(cat, dmix, "dw_out", 1024, 1024)
    dw_in = _weight_grad_kernel(hb, dproj, "dw_in", 1024, 896)
    dw_in = dw_in.reshape(D_MODEL, N_CHIPS, IN_PROJ_WIDTH // N_CHIPS).transpose(1, 0, 2)

    big = {"w_in": (w_in, m_w_in, v_w_in, dw_in), "w_out": (w_out, m_w_out, v_w_out, dw_out),
           "w_ff1": (w_ff1, m_w_ff1, v_w_ff1, dw_ff1), "w_ff2": (w_ff2, m_w_ff2, v_w_ff2, dw_ff2)}
    names = list(big)
    parts = []
    for nm in names:
        r, n = big[nm][0].shape[1], big[nm][0].shape[2]
        parts.append(big[nm][3].reshape(N_CHIPS, 2, r // 2, n))
    c_idx = ci.reshape(1).astype(jnp.int32)
    from_sibling = _sibling_exchange_halves(parts, "grad_to_sibling")
    chip_sums = [_add_halves_kernel(p, q, c_idx, "grad_add_" + nm) for nm, p, q in zip(names, parts, from_sibling)]
    scattered = _chip_scatter(chip_sums, "grad_to_chips")
    totals = [_sum_chips_kernel(s, "grad_sum_" + nm) for nm, s in zip(names, scattered)]
    shared = _sibling_share(totals, "grad_share")
    big_out = {}
    for nm, g in zip(names, shared):
        w, m, v, _ = big[nm]
        g = g.reshape(w.shape[1:])
        d, mn, vn = _adam_kernel(w[0], g, m[0], v[0], "adam_" + nm)
        big_out[nm] = tuple(t[None] for t in (g, d, mn, vn))

    dmod = jnp.concatenate([sums1[0:1], sums1[1:2], sums2[5:6], sums2[0:1], sums2[1:2], sums2[2:3]], axis=1)
    db_spatial = db_lanes[:, 0:GMLP_GROUPS].T
    small_grads = {"b_ada": dmod, "g_mix": sums1[2], "g_ffn": sums2[3], "g_final": sums2[4],
                   "b_spatial": db_spatial, "sinks": dsink_rows[:, 0], "loss": sums2[6, 0:1],
                   "w_spatial": dw_spatial}
    small_w = {"b_ada": b_ada, "g_mix": g_mix, "g_ffn": g_ffn, "g_final": g_final, "b_spatial": b_spatial,
               "sinks": sinks, "w_spatial": w_spatial}
    small_m = {"b_ada": m_b_ada, "g_mix": m_g_mix, "g_ffn": m_g_ffn, "g_final": m_g_final,
               "b_spatial": m_b_spatial, "sinks": m_sinks, "w_spatial": m_w_spatial}
    small_v = {"b_ada": v_b_ada, "g_mix": v_g_mix, "g_ffn": v_g_ffn, "g_final": v_g_final,
               "b_spatial": v_b_spatial, "sinks": v_sinks, "w_spatial": v_w_spatial}
    gathered = _all_gather8(_pack_small(small_grads), "gather_small")
    packed = _small_update_kernel(gathered, _pack_small(small_w), _pack_small(small_m), _pack_small(small_v))
    shapes = {k: a.shape for k, a in small_w.items()}
    sg, sd, sm, sv = (_unpack_small(p, shapes) for p in packed)
    loss = packed[0][sum(nr for nm, nr in _SMALL_ROWS[:6]), 0]

    dmod_all = gathered[:, 0:N_MOD * D_MODEL // LANES, :].reshape(N_DEV, N_MOD * D_MODEL)
    dmod_cols = lax.dynamic_slice(dmod_all, (0, chip * ada_cols), (N_DEV, ada_cols))
    ada = _ada_update_kernel(act.T, dmod_cols, w_ada[0], m_w_ada[0], v_w_ada[0])
    big_out["w_ada"] = tuple(t[None] for t in ada)

    order = ["w_ada", "b_ada", "g_mix", "w_in", "w_spatial", "b_spatial", "sinks", "w_out", "g_ffn",
             "w_ff1", "w_ff2", "g_final"]

    def leaf(nm, k):
        return big_out[nm][k] if nm in big_out else (sg, sd, sm, sv)[k][nm]

    outs = [loss, grad_x[None]]
    for k in range(4):
        outs += [leaf(nm, k) for nm in order]
    return tuple(outs)
```

```python
import functools
import math

import jax
import jax.numpy as jnp
from jax import lax
from jax.experimental import pallas as pl
from jax.experimental.pallas import tpu as pltpu

F32 = jnp.float32
MXU_DTYPE = jnp.bfloat16
WEIGHT_COMM_DTYPE = jnp.bfloat16
GRAD_COMM_DTYPE = jnp.bfloat16

D_MODEL = 1024
D_FF = 4096
HEAD_DIM = 64
GMLP_GROUPS = 8
GMLP_WIDTH = 512
CHUNK = 128
N_Q_HEADS = 8
N_KV_HEADS = 2
ATTN_WIDTH = 512
KV_WIDTH = 128
ROT_DIM = 16
ROPE_THETA = 500000.0
IN_PROJ_WIDTH = 1792
N_MOD = 6
EPS = 1e-5
N_CHIPS = 4
N_DEV = 8
LANES = 128

ADAM_LR = 0.001
ADAM_B1 = 0.9
ADAM_B2 = 0.999
ADAM_EPS = 1e-08
ADAM_WD = 0.01
ADAM_STEP = 10

VMEM_LIMIT_BYTES = 58 * 1024 * 1024
MESH = pl.DeviceIdType.MESH


def _params(*semantics):
    return pltpu.CompilerParams(dimension_semantics=semantics, vmem_limit_bytes=VMEM_LIMIT_BYTES)


def _dot(a, b):
    return jnp.dot(a.astype(MXU_DTYPE), b.astype(MXU_DTYPE), preferred_element_type=F32)


def _dot_nt(a, b):
    return lax.dot_general(a.astype(MXU_DTYPE), b.astype(MXU_DTYPE), (((1,), (1,)), ((), ())),
                           preferred_element_type=F32)


def _dot_tn(a, b):
    return lax.dot_general(a.astype(MXU_DTYPE), b.astype(MXU_DTYPE), (((0,), (0,)), ((), ())),
                           preferred_element_type=F32)


def _full(shape):
    return pl.BlockSpec(shape, lambda *_: (0,) * len(shape))


def _any():
    return pl.BlockSpec(memory_space=pl.ANY)


def _rowsum(v):
    return jnp.sum(v, axis=0, keepdims=True)


def _mean_last(v):
    return jnp.mean(v, axis=-1, keepdims=True)


def _all_gather8(block, name, split=False):
    blk_shape = block.shape[1:] if split else block.shape

    def body(x_ref, out_ref, send_sems, recv_sems, local_sem):
        x, y, c = lax.axis_index("x"), lax.axis_index("y"), lax.axis_index("c")
        me, sibling = (x, y, c), (x, y, 1 - c)
        chips = [(1 - x, y), (x, 1 - y), (1 - x, 1 - y)]
        src_mine = x_ref.at[c] if split else x_ref

        def slot(px, py, pc):
            return out_ref.at[4 * px + 2 * py + pc]

        def copy(k, blk, to, src=None):
            return pltpu.make_async_remote_copy(
                src_ref=slot(*blk) if src is None else src, dst_ref=slot(*blk),
                send_sem=send_sems.at[k], recv_sem=recv_sems.at[k],
                device_id=to, device_id_type=MESH)

        mine = pltpu.make_async_copy(src_mine, slot(*me), local_sem)
        mine.start()
        first = [copy(0, me, sibling, src=src_mine)]
        first += [copy(1 + j, me, (*chip, c), src=src_mine) for j, chip in enumerate(chips)]
        for cp in first:
            cp.start()
        passed = [copy(4 + j, (*chip, c), sibling) for j, chip in enumerate(chips)]
        for j, chip in enumerate(chips):
            copy(1 + j, (*chip, c), me).wait_recv()
            passed[j].start()
        copy(0, sibling, me).wait_recv()
        for j, chip in enumerate(chips):
            copy(4 + j, (*chip, 1 - c), me).wait_recv()
        for cp in first + passed:
            cp.wait_send()
        mine.wait()

    return pl.pallas_call(
        body, name=name,
        out_shape=jax.ShapeDtypeStruct((N_DEV,) + tuple(blk_shape), block.dtype),
        in_specs=[_any()], out_specs=_any(),
        scratch_shapes=[pltpu.SemaphoreType.DMA((7,)), pltpu.SemaphoreType.DMA((7,)),
                        pltpu.SemaphoreType.DMA],
    )(block)


def _sibling_exchange_halves(parts, name):
    n_arr = len(parts)

    def body(*refs):
        in_refs = refs[:n_arr]
        out_refs = refs[n_arr:2 * n_arr]
        send_sems, recv_sems = refs[2 * n_arr:]
        x, y, c = lax.axis_index("x"), lax.axis_index("y"), lax.axis_index("c")
        copies = []
        for a in range(n_arr):
            for k in range(N_CHIPS):
                copies.append(pltpu.make_async_remote_copy(
                    src_ref=in_refs[a].at[k, 1 - c], dst_ref=out_refs[a].at[k],
                    send_sem=send_sems.at[a * N_CHIPS + k], recv_sem=recv_sems.at[a * N_CHIPS + k],
                    device_id=(x, y, 1 - c), device_id_type=MESH))
        for cp in copies:
            cp.start()
        for cp in copies:
            cp.wait()

    return pl.pallas_call(
        body, name=name,
        out_shape=[jax.ShapeDtypeStruct((N_CHIPS,) + p.shape[2:], p.dtype) for p in parts],
        in_specs=[_any()] * n_arr, out_specs=[_any()] * n_arr,
        scratch_shapes=[pltpu.SemaphoreType.DMA((n_arr * N_CHIPS,)),
                        pltpu.SemaphoreType.DMA((n_arr * N_CHIPS,))],
    )(*parts)


def _chip_scatter(halves, name):
    n_arr = len(halves)

    def body(*refs):
        in_refs = refs[:n_arr]
        out_refs = refs[n_arr:2 * n_arr]
        send_sems, recv_sems, local_sems = refs[2 * n_arr:]
        x, y, c = lax.axis_index("x"), lax.axis_index("y"), lax.axis_index("c")
        chips = [(1 - x, y), (x, 1 - y), (1 - x, 1 - y)]
        copies, local = [], []
        for a in range(n_arr):
            for j, (px, py) in enumerate(chips):
                copies.append(pltpu.make_async_remote_copy(
                    src_ref=in_refs[a].at[2 * px + py], dst_ref=out_refs[a].at[j],
                    send_sem=send_sems.at[a * 3 + j], recv_sem=recv_sems.at[a * 3 + j],
                    device_id=(px, py, c), device_id_type=MESH))
            local.append(pltpu.make_async_copy(in_refs[a].at[2 * x + y], out_refs[a].at[3],
                                               local_sems.at[a]))
        for cp in copies + local:
            cp.start()
        for cp in copies + local:
            cp.wait()

    return pl.pallas_call(
        body, name=name,
        out_shape=[jax.ShapeDtypeStruct(h.shape, h.dtype) for h in halves],
        in_specs=[_any()] * n_arr, out_specs=[_any()] * n_arr,
        scratch_shapes=[pltpu.SemaphoreType.DMA((n_arr * 3,)), pltpu.SemaphoreType.DMA((n_arr * 3,)),
                        pltpu.SemaphoreType.DMA((n_arr,))],
    )(*halves)


def _sibling_share(bufs, name):
    n_arr = len(bufs)

    def body(*refs):
        out_refs = refs[n_arr:2 * n_arr]
        send_sems, recv_sems = refs[2 * n_arr:]
        x, y, c = lax.axis_index("x"), lax.axis_index("y"), lax.axis_index("c")
        copies = [pltpu.make_async_remote_copy(
            src_ref=out_refs[a].at[c], dst_ref=out_refs[a].at[c],
            send_sem=send_sems.at[a], recv_sem=recv_sems.at[a],
            device_id=(x, y, 1 - c), device_id_type=MESH) for a in range(n_arr)]
        for cp in copies:
            cp.start()
        for a in range(n_arr):
            pltpu.make_async_remote_copy(
                src_ref=out_refs[a].at[c], dst_ref=out_refs[a].at[1 - c],
                send_sem=send_sems.at[a], recv_sem=recv_sems.at[a],
                device_id=(x, y, 1 - c), device_id_type=MESH).wait()

    return pl.pallas_call(
        body, name=name,
        out_shape=[jax.ShapeDtypeStruct(b.shape, b.dtype) for b in bufs],
        in_specs=[_any()] * n_arr, out_specs=[_any()] * n_arr,
        input_output_aliases={a: a for a in range(n_arr)},
        scratch_shapes=[pltpu.SemaphoreType.DMA((n_arr,)), pltpu.SemaphoreType.DMA((n_arr,))],
    )(*bufs)


def _gelu_tanh(z):
    k = math.sqrt(2.0 / math.pi)
    t = jnp.tanh(k * (z + 0.044715 * (z * z * z)))
    return 0.5 * z * (1.0 + t), t


def _gelu_tanh_grad(z, t):
    k = math.sqrt(2.0 / math.pi)
    return 0.5 * (1.0 + t) + 0.5 * z * (1.0 - t * t) * (k * (1.0 + 3.0 * 0.044715 * (z * z)))


def _rope_tables(pos_col, invf_row):
    ang = pos_col.astype(F32) * invf_row
    cos, sin = jnp.cos(ang), jnp.sin(ang)
    d = lax.broadcasted_iota(jnp.int32, ang.shape, 1) & (HEAD_DIM - 1)
    half = ROT_DIM // 2
    c_tab = jnp.where(d < ROT_DIM, cos, 1.0)
    s1 = jnp.where(d < half, -sin, 0.0)
    s2 = jnp.where((d >= half) & (d < ROT_DIM), sin, 0.0)
    return c_tab, s1, s2


def _rope_apply(t, tabs, sign):
    c_tab, s1, s2 = tabs
    reps = t.shape[1] // LANES
    if reps > 1:
        c_tab, s1, s2 = (jnp.tile(v, (1, reps)) for v in (c_tab, s1, s2))
    half = ROT_DIM // 2
    up = pltpu.roll(t, t.shape[1] - half, 1)
    down = pltpu.roll(t, half, 1)
    return t * c_tab + sign * (up * s1 + down * s2)


def _lane_masks(shape):
    lane = lax.broadcasted_iota(jnp.int32, shape, 1)
    return lane < HEAD_DIM, lane >= HEAD_DIM


def _attn_mask(first_block):
    qi = lax.broadcasted_iota(jnp.int32, (CHUNK, 2 * CHUNK), 0)
    kj = lax.broadcasted_iota(jnp.int32, (CHUNK, 2 * CHUNK), 1)
    band = (kj > qi) & (kj <= qi + CHUNK)
    return band & (jnp.logical_not(first_block) | (kj >= CHUNK))


def _head_plan():
    plan = []
    for h in range(N_Q_HEADS):
        pair, hi = h // 2, h % 2
        group = h // (N_Q_HEADS // N_KV_HEADS)
        plan.append((pair, hi, (group == 1) != (hi == 1)))
    return plan


def _softmax_sink(s, sink):
    m = jnp.maximum(jnp.max(s, axis=-1, keepdims=True), sink)
    p = jnp.exp(s - m)
    e_sink = jnp.exp(sink - m)
    den = jnp.sum(p, axis=-1, keepdims=True) + e_sink
    return p / den, e_sink / den


def _sgu_forward_pair(wm, vp, j):
    lo, hi = _lane_masks(vp.shape)
    lhs = jnp.concatenate([wm[2 * j], wm[2 * j + 1]], axis=1)
    rhs = jnp.concatenate([jnp.where(lo, vp, 0.0), jnp.where(hi, vp, 0.0)], axis=0)
    return _dot(lhs, rhs)


def _masked_spatial(w_ref):
    t = lax.broadcasted_iota(jnp.int32, (CHUNK, CHUNK), 0)
    s = lax.broadcasted_iota(jnp.int32, (CHUNK, CHUNK), 1)
    tril = s <= t
    return [jnp.where(tril, w_ref[g], 0.0) for g in range(GMLP_GROUPS)], tril


def _mod_kernel(c_all, w_shard, b_shard):
    n = w_shard.shape[1]
    tn = 512

    def body(c_ref, w_ref, b_ref, mod_ref, act_ref):
        cv = c_ref[...]
        act = cv * (1.0 / (1.0 + jnp.exp(-cv)))
        act_ref[...] = act
        mod_ref[...] = _dot(act, w_ref[...]) + b_ref[...]

    return pl.pallas_call(
        body, name="ada_mod", grid=(n // tn,),
        out_shape=[jax.ShapeDtypeStruct((N_DEV, n), F32), jax.ShapeDtypeStruct((N_DEV, D_MODEL), F32)],
        in_specs=[_full((N_DEV, D_MODEL)), pl.BlockSpec((D_MODEL, tn), lambda i: (0, i)),
                  pl.BlockSpec((1, tn), lambda i: (0, i))],
        out_specs=[pl.BlockSpec((N_DEV, tn), lambda i: (0, i)), _full((N_DEV, D_MODEL))],
        compiler_params=_params("arbitrary"),
    )(c_all, w_shard, b_shard)


def _in_proj_kernel(x, vecs, w_in):
    seq = x.shape[0]
    tm = 512

    def body(x_ref, v_ref, w_ref, proj_ref, h_ref):
        xv = x_ref[...]
        rstd = lax.rsqrt(_mean_last(xv * xv) + EPS)
        n1 = (xv * rstd) * v_ref[0:1, :]
        h = n1 * (1.0 + v_ref[2:3, :]) + v_ref[1:2, :]
        hb = h.astype(MXU_DTYPE)
        h_ref[...] = hb
        proj_ref[...] = _dot(hb, w_ref[...])

    return pl.pallas_call(
        body, name="in_proj", grid=(seq // tm,),
        out_shape=[jax.ShapeDtypeStruct((seq, IN_PROJ_WIDTH), F32),
                   jax.ShapeDtypeStruct((seq, D_MODEL), MXU_DTYPE)],
        in_specs=[pl.BlockSpec((tm, D_MODEL), lambda i: (i, 0)), _full((8, D_MODEL)),
                  _full((D_MODEL, IN_PROJ_WIDTH))],
        out_specs=[pl.BlockSpec((tm, IN_PROJ_WIDTH), lambda i: (i, 0)),
                   pl.BlockSpec((tm, D_MODEL), lambda i: (i, 0))],
        compiler_params=_params("parallel"),
    )(x, vecs, w_in)


def _mixer_fwd_kernel(proj, pos, w_spatial, bias_full, sink_rows, invf):
    seq = proj.shape[0]
    nb = seq // CHUNK
    kv_col = (2 * GMLP_WIDTH + ATTN_WIDTH) // (2 * KV_WIDTH)

    def body(proj_ref, prev_ref, pos_ref, ppos_ref, w_ref, bias_ref, sink_ref, invf_ref, cat_ref):
        i = pl.program_id(0)
        wm, _ = _masked_spatial(w_ref)
        for j in range(GMLP_GROUPS // 2):
            cols = slice(LANES * j, LANES * (j + 1))
            vcols = slice(GMLP_WIDTH + LANES * j, GMLP_WIDTH + LANES * (j + 1))
            u, _ = _gelu_tanh(proj_ref[:, cols])
            vp, _ = _gelu_tanh(proj_ref[:, vcols])
            sv = _sgu_forward_pair(wm, vp, j) + bias_ref[:, cols]
            cat_ref[:, cols] = (u * sv).astype(cat_ref.dtype)
        tabs = _rope_tables(pos_ref[...], invf_ref[...])
        ptabs = _rope_tables(ppos_ref[...], invf_ref[...])
        o = 2 * GMLP_WIDTH
        q_r = _rope_apply(proj_ref[:, o:o + ATTN_WIDTH], tabs, 1.0)
        k_cur = _rope_apply(proj_ref[:, o + ATTN_WIDTH:o + ATTN_WIDTH + KV_WIDTH], tabs, 1.0)
        k_prev = _rope_apply(prev_ref[:, 0:KV_WIDTH], ptabs, 1.0)
        k_a = jnp.concatenate([k_prev, k_cur], axis=0)
        v_a = jnp.concatenate([prev_ref[:, KV_WIDTH:2 * KV_WIDTH],
                               proj_ref[:, o + ATTN_WIDTH + KV_WIDTH:o + ATTN_WIDTH + 2 * KV_WIDTH]], axis=0)
        k_b = pltpu.roll(k_a, HEAD_DIM, 1)
        v_b = pltpu.roll(v_a, HEAD_DIM, 1)
        mask = _attn_mask(i == 0)
        lo, hi = _lane_masks((CHUNK, LANES))
        lo2, hi2 = _lane_masks((2 * CHUNK, LANES))
        outs = [None] * (N_Q_HEADS // 2)
        for h, (pair, is_hi, rolled) in enumerate(_head_plan()):
            qp = q_r[:, LANES * pair:LANES * (pair + 1)]
            qm = jnp.where(hi if is_hi else lo, qp, 0.0)
            s = _dot_nt(qm, k_b if rolled else k_a) * (1.0 / math.sqrt(HEAD_DIM))
            s = jnp.where(mask, s, -jnp.inf)
            probs, _ = _softmax_sink(s, sink_ref[h:h + 1, 0:1])
            vm = jnp.where(hi2 if is_hi else lo2, v_b if rolled else v_a, 0.0)
            oh = _dot(probs, vm)
            outs[pair] = oh if outs[pair] is None else outs[pair] + oh
        for pair in range(N_Q_HEADS // 2):
            cols = slice(GMLP_WIDTH + LANES * pair, GMLP_WIDTH + LANES * (pair + 1))
            cat_ref[:, cols] = outs[pair].astype(cat_ref.dtype)

    return pl.pallas_call(
        body, name="mixer_fwd", grid=(nb,),
        out_shape=jax.ShapeDtypeStruct((seq, D_MODEL), MXU_DTYPE),
        in_specs=[pl.BlockSpec((CHUNK, IN_PROJ_WIDTH), lambda i: (i, 0)),
                  pl.BlockSpec((CHUNK, 2 * KV_WIDTH), lambda i: (jnp.maximum(i - 1, 0), kv_col)),
                  pl.BlockSpec((CHUNK, 1), lambda i: (i, 0)),
                  pl.BlockSpec((CHUNK, 1), lambda i: (jnp.maximum(i - 1, 0), 0)),
                  _full((GMLP_GROUPS, CHUNK, CHUNK)), _full((CHUNK, GMLP_WIDTH)),
                  _full((N_Q_HEADS, LANES)), _full((1, LANES))],
        out_specs=pl.BlockSpec((CHUNK, D_MODEL), lambda i: (i, 0)),
        compiler_params=_params("parallel"),
    )(proj, proj, pos, pos, w_spatial, bias_full, sink_rows, invf)


def _trunk_kernel(x, target, cat, vecs, w_out, w_ff1, w_ff2):
    seq = x.shape[0]
    tm = 256
    nj = D_FF // D_MODEL

    def body(x_ref, t_ref, cat_ref, v_ref, wout_hbm, w1_hbm, w2_hbm,
             dx1_ref, dcat_ref, dmix_ref, h2_ref, r_ref, da_ref, dff_ref, sums_ref,
             wout, w1, w2, a_scr, sem):
        i = pl.program_id(0)

        @pl.when(i == 0)
        def _():
            copies = [pltpu.make_async_copy(wout_hbm, wout, sem.at[0]),
                      pltpu.make_async_copy(w1_hbm, w1, sem.at[1]),
                      pltpu.make_async_copy(w2_hbm, w2, sem.at[2])]
            for cp in copies:
                cp.start()
            for cp in copies:
                cp.wait()
            sums_ref[...] = jnp.zeros_like(sums_ref)

        gate1, shift2, scale2 = v_ref[0:1, :], v_ref[1:2, :], v_ref[2:3, :]
        gate2, g_ffn, g_final = v_ref[3:4, :], v_ref[4:5, :], v_ref[5:6, :]

        mix = _dot(cat_ref[...], wout[...])
        x1 = x_ref[...] + gate1 * mix
        rstd2 = lax.rsqrt(_mean_last(x1 * x1) + EPS)
        xh2 = x1 * rstd2
        n2 = xh2 * g_ffn
        h2b = (n2 * (1.0 + scale2) + shift2).astype(MXU_DTYPE)
        h2_ref[...] = h2b
        ff = jnp.zeros((tm, D_MODEL), F32)
        for j in range(nj):
            a = _dot(h2b, w1[j])
            a_scr[j] = a
            relu = jnp.maximum(a, 0.0)
            rb = (relu * relu).astype(MXU_DTYPE)
            r_ref[:, D_MODEL * j:D_MODEL * (j + 1)] = rb
            ff = ff + _dot(rb, w2[j])
        x2 = x1 + gate2 * ff
        rstd3 = lax.rsqrt(_mean_last(x2 * x2) + EPS)
        xh3 = x2 * rstd3
        err = xh3 * g_final - t_ref[...]
        loss = 0.5 * _rowsum(_mean_last(err * err))
        dy = err * (1.0 / D_MODEL)
        dxh3 = dy * g_final
        dx2 = rstd3 * (dxh3 - xh3 * _mean_last(dxh3 * xh3))
        dffb = (dx2 * gate2).astype(MXU_DTYPE)
        dff_ref[...] = dffb
        dh2 = jnp.zeros((tm, D_MODEL), F32)
        for j in range(nj):
            dr = _dot_nt(dffb, w2[j])
            dab = (dr * (2.0 * jnp.maximum(a_scr[j], 0.0))).astype(MXU_DTYPE)
            da_ref[:, D_MODEL * j:D_MODEL * (j + 1)] = dab
            dh2 = dh2 + _dot_nt(dab, w1[j])
        dn2 = dh2 * (1.0 + scale2)
        dxh2 = dn2 * g_ffn
        dx1 = dx2 + rstd2 * (dxh2 - xh2 * _mean_last(dxh2 * xh2))
        dx1_ref[...] = dx1
        dmixb = (dx1 * gate1).astype(MXU_DTYPE)
        dmix_ref[...] = dmixb
        dcat_ref[...] = _dot_nt(dmixb, wout[...])

        sums_ref[0:1, :] += _rowsum(dh2)
        sums_ref[1:2, :] += _rowsum(dh2 * n2)
        sums_ref[2:3, :] += _rowsum(dx2 * ff)
        sums_ref[3:4, :] += _rowsum(dn2 * xh2)
        sums_ref[4:5, :] += _rowsum(dy * xh3)
        sums_ref[5:6, :] += _rowsum(dx1 * mix)
        sums_ref[6:7, :] += jnp.broadcast_to(loss, (1, D_MODEL))

    tok = lambda w: pl.BlockSpec((tm, w), lambda i: (i, 0))
    return pl.pallas_call(
        body, name="trunk", grid=(seq // tm,),
        out_shape=[jax.ShapeDtypeStruct((seq, D_MODEL), F32), jax.ShapeDtypeStruct((seq, D_MODEL), F32),
                   jax.ShapeDtypeStruct((seq, D_MODEL), MXU_DTYPE), jax.ShapeDtypeStruct((seq, D_MODEL), MXU_DTYPE),
                   jax.ShapeDtypeStruct((seq, D_FF), MXU_DTYPE), jax.ShapeDtypeStruct((seq, D_FF), MXU_DTYPE),
                   jax.ShapeDtypeStruct((seq, D_MODEL), MXU_DTYPE), jax.ShapeDtypeStruct((8, D_MODEL), F32)],
        in_specs=[tok(D_MODEL), tok(D_MODEL), tok(D_MODEL), _full((8, D_MODEL)), _any(), _any(), _any()],
        out_specs=[tok(D_MODEL), tok(D_MODEL), tok(D_MODEL), tok(D_MODEL), tok(D_FF), tok(D_FF), tok(D_MODEL),
                   _full((8, D_MODEL))],
        scratch_shapes=[pltpu.VMEM((D_MODEL, D_MODEL), MXU_DTYPE), pltpu.VMEM((nj, D_MODEL, D_MODEL), MXU_DTYPE),
                        pltpu.VMEM((nj, D_MODEL, D_MODEL), MXU_DTYPE), pltpu.VMEM((nj, tm, D_MODEL), F32),
                        pltpu.SemaphoreType.DMA((3,))],
        compiler_params=_params("arbitrary"),
    )(x, target, cat, vecs, w_out, w_ff1, w_ff2)


def _mixer_bwd_kernel(proj, pos, dcat, w_spatial, bias_full, sink_rows, invf):
    seq = proj.shape[0]
    nb = seq // CHUNK
    kv_col = (2 * GMLP_WIDTH + ATTN_WIDTH) // (2 * KV_WIDTH)

    def body(proj_ref, prev_ref, pos_ref, ppos_ref, dcat_ref, w_ref, bias_ref, sink_ref, invf_ref,
             dproj_ref, dw_ref, db_ref, dsink_ref, carry):
        step = pl.program_id(0)
        blk = nb - 1 - step

        @pl.when(step == 0)
        def _():
            carry[...] = jnp.zeros_like(carry)
            dw_ref[...] = jnp.zeros_like(dw_ref)
            db_ref[...] = jnp.zeros_like(db_ref)
            dsink_ref[...] = jnp.zeros_like(dsink_ref)

        wm, tril = _masked_spatial(w_ref)
        lo, hi = _lane_masks((CHUNK, LANES))
        for j in range(GMLP_GROUPS // 2):
            cols = slice(LANES * j, LANES * (j + 1))
            vcols = slice(GMLP_WIDTH + LANES * j, GMLP_WIDTH + LANES * (j + 1))
            zu, zv = proj_ref[:, cols], proj_ref[:, vcols]
            u, tu = _gelu_tanh(zu)
            vp, tv = _gelu_tanh(zv)
            sv = _sgu_forward_pair(wm, vp, j) + bias_ref[:, cols]
            dout = dcat_ref[:, cols]
            du = dout * sv
            dsv = dout * u
            dsv_lo, dsv_hi = jnp.where(lo, dsv, 0.0), jnp.where(hi, dsv, 0.0)
            lhs_t = jnp.concatenate([wm[2 * j].T, wm[2 * j + 1].T], axis=1)
            dv = _dot(lhs_t, jnp.concatenate([dsv_lo, dsv_hi], axis=0))
            dw_ref[2 * j] += jnp.where(tril, _dot_nt(dsv_lo, vp), 0.0)
            dw_ref[2 * j + 1] += jnp.where(tril, _dot_nt(dsv_hi, vp), 0.0)
            lane = lax.broadcasted_iota(jnp.int32, (CHUNK, LANES), 1)
            db_ref[...] += (jnp.where(lane == 2 * j, jnp.sum(dsv_lo, axis=1, keepdims=True), 0.0)
                            + jnp.where(lane == 2 * j + 1, jnp.sum(dsv_hi, axis=1, keepdims=True), 0.0))
            dproj_ref[:, cols] = (du * _gelu_tanh_grad(zu, tu)).astype(dproj_ref.dtype)
            dproj_ref[:, vcols] = (dv * _gelu_tanh_grad(zv, tv)).astype(dproj_ref.dtype)
        tabs = _rope_tables(pos_ref[...], invf_ref[...])
        ptabs = _rope_tables(ppos_ref[...], invf_ref[...])
        o = 2 * GMLP_WIDTH
        q_r = _rope_apply(proj_ref[:, o:o + ATTN_WIDTH], tabs, 1.0)
        k_cur = _rope_apply(proj_ref[:, o + ATTN_WIDTH:o + ATTN_WIDTH + KV_WIDTH], tabs, 1.0)
        k_prev = _rope_apply(prev_ref[:, 0:KV_WIDTH], ptabs, 1.0)
        k_a = jnp.concatenate([k_prev, k_cur], axis=0)
        v_a = jnp.concatenate([prev_ref[:, KV_WIDTH:2 * KV_WIDTH],
                               proj_ref[:, o + ATTN_WIDTH + KV_WIDTH:o + ATTN_WIDTH + 2 * KV_WIDTH]], axis=0)
        k_b = pltpu.roll(k_a, HEAD_DIM, 1)
        v_b = pltpu.roll(v_a, HEAD_DIM, 1)
        mask = _attn_mask(blk == 0)
        lo2, hi2 = _lane_masks((2 * CHUNK, LANES))
        scale = 1.0 / math.sqrt(HEAD_DIM)
        zero_kv = jnp.zeros((2 * CHUNK, LANES), F32)
        dk_a, dk_b, dv_a, dv_b = zero_kv, zero_kv, zero_kv, zero_kv
        dq_pairs = [None] * (N_Q_HEADS // 2)
        for h, (pair, is_hi, rolled) in enumerate(_head_plan()):
            half, half2 = (hi, hi2) if is_hi else (lo, lo2)
            qm = jnp.where(half, q_r[:, LANES * pair:LANES * (pair + 1)], 0.0)
            k_sel, v_sel = (k_b, v_b) if rolled else (k_a, v_a)
            s = jnp.where(mask, _dot_nt(qm, k_sel) * scale, -jnp.inf)
            probs, p_sink = _softmax_sink(s, sink_ref[h:h + 1, 0:1])
            dom = jnp.where(half, dcat_ref[:, GMLP_WIDTH + LANES * pair:GMLP_WIDTH + LANES * (pair + 1)], 0.0)
            dprobs = _dot_nt(dom, v_sel)
            delta = jnp.sum(probs * dprobs, axis=-1, keepdims=True)
            ds = probs * (dprobs - delta) * scale
            dsink_ref[h:h + 1, :] += jnp.broadcast_to(_rowsum(-p_sink * delta), (1, LANES))
            dqh = _dot(ds, jnp.where(half2, k_sel, 0.0))
            dq_pairs[pair] = dqh if dq_pairs[pair] is None else dq_pairs[pair] + dqh
            dkh = _dot_tn(ds, qm)
            dvh = _dot_tn(probs, dom)
            if rolled:
                dk_b, dv_b = dk_b + dkh, dv_b + dvh
            else:
                dk_a, dv_a = dk_a + dkh, dv_a + dvh
        dk_all = dk_a + pltpu.roll(dk_b, HEAD_DIM, 1)
        dv_all = dv_a + pltpu.roll(dv_b, HEAD_DIM, 1)
        dk_cur = dk_all[CHUNK:, :] + carry[:, 0:KV_WIDTH]
        dv_cur = dv_all[CHUNK:, :] + carry[:, KV_WIDTH:2 * KV_WIDTH]
        carry[:, 0:KV_WIDTH] = dk_all[:CHUNK, :]
        carry[:, KV_WIDTH:2 * KV_WIDTH] = dv_all[:CHUNK, :]
        dq = _rope_apply(jnp.concatenate(dq_pairs, axis=1), tabs, -1.0)
        dproj_ref[:, o:o + ATTN_WIDTH] = dq.astype(dproj_ref.dtype)
        dproj_ref[:, o + ATTN_WIDTH:o + ATTN_WIDTH + KV_WIDTH] = (
            _rope_apply(dk_cur, tabs, -1.0).astype(dproj_ref.dtype))
        dproj_ref[:, o + ATTN_WIDTH + KV_WIDTH:o + ATTN_WIDTH + 2 * KV_WIDTH] = dv_cur.astype(dproj_ref.dtype)

    rev = lambda i: nb - 1 - i
    return pl.pallas_call(
        body, name="mixer_bwd", grid=(nb,),
        out_shape=[jax.ShapeDtypeStruct((seq, IN_PROJ_WIDTH), MXU_DTYPE),
                   jax.ShapeDtypeStruct((GMLP_GROUPS, CHUNK, CHUNK), F32),
                   jax.ShapeDtypeStruct((CHUNK, LANES), F32),
                   jax.ShapeDtypeStruct((N_Q_HEADS, LANES), F32)],
        in_specs=[pl.BlockSpec((CHUNK, IN_PROJ_WIDTH), lambda i: (rev(i), 0)),
                  pl.BlockSpec((CHUNK, 2 * KV_WIDTH), lambda i: (jnp.maximum(rev(i) - 1, 0), kv_col)),
                  pl.BlockSpec((CHUNK, 1), lambda i: (rev(i), 0)),
                  pl.BlockSpec((CHUNK, 1), lambda i: (jnp.maximum(rev(i) - 1, 0), 0)),
                  pl.BlockSpec((CHUNK, D_MODEL), lambda i: (rev(i), 0)),
                  _full((GMLP_GROUPS, CHUNK, CHUNK)), _full((CHUNK, GMLP_WIDTH)),
                  _full((N_Q_HEADS, LANES)), _full((1, LANES))],
        out_specs=[pl.BlockSpec((CHUNK, IN_PROJ_WIDTH), lambda i: (rev(i), 0)),
                   _full((GMLP_GROUPS, CHUNK, CHUNK)), _full((CHUNK, LANES)), _full((N_Q_HEADS, LANES))],
        scratch_shapes=[pltpu.VMEM((CHUNK, 2 * KV_WIDTH), F32)],
        compiler_params=_params("arbitrary"),
    )(proj, proj, pos, pos, dcat, w_spatial, bias_full, sink_rows, invf)


def _in_proj_bwd_kernel(x, dx1, dproj, vecs, w_in):
    seq = x.shape[0]
    tm = 512

    def body(x_ref, dx1_ref, dp_ref, v_ref, w_ref, gx_ref, sums_ref):
        @pl.when(pl.program_id(0) == 0)
        def _():
            sums_ref[...] = jnp.zeros_like(sums_ref)

        g_mix, scale1 = v_ref[0:1, :], v_ref[2:3, :]
        dh = _dot_nt(dp_ref[...], w_ref[...])
        xv = x_ref[...]
        rstd = lax.rsqrt(_mean_last(xv * xv) + EPS)
        xh = xv * rstd
        dn1 = dh * (1.0 + scale1)
        dxh = dn1 * g_mix
        gx_ref[...] = dx1_ref[...] + rstd * (dxh - xh * _mean_last(dxh * xh))
        sums_ref[0:1, :] += _rowsum(dh)
        sums_ref[1:2, :] += _rowsum(dh * (xh * g_mix))
        sums_ref[2:3, :] += _rowsum(dn1 * xh)

    return pl.pallas_call(
        body, name="in_proj_bwd", grid=(seq // tm,),
        out_shape=[jax.ShapeDtypeStruct((seq, D_MODEL), F32), jax.ShapeDtypeStruct((8, D_MODEL), F32)],
        in_specs=[pl.BlockSpec((tm, D_MODEL), lambda i: (i, 0)), pl.BlockSpec((tm, D_MODEL), lambda i: (i, 0)),
                  pl.BlockSpec((tm, IN_PROJ_WIDTH), lambda i: (i, 0)), _full((8, D_MODEL)),
                  _full((D_MODEL, IN_PROJ_WIDTH))],
        out_specs=[pl.BlockSpec((tm, D_MODEL), lambda i: (i, 0)), _full((8, D_MODEL))],
        compiler_params=_params("arbitrary"),
    )(x, dx1, dproj, vecs, w_in)


def _weight_grad_kernel(a, b, name, tm, tn, blocked_cols=False):
    seq, m = a.shape
    n = b.shape[1]
    tk = 512
    nk = seq // tk

    def body(a_ref, b_ref, o_ref, acc):
        k = pl.program_id(2)

        @pl.when(k == 0)
        def _():
            acc[...] = jnp.zeros_like(acc)

        acc[...] += _dot_tn(a_ref[...], b_ref[...])

        @pl.when(k == nk - 1)
        def _():
            o_ref[...] = acc[...]

    if blocked_cols:
        out_shape = jax.ShapeDtypeStruct((n // tn, m, tn), F32)
        out_spec = pl.BlockSpec((None, tm, tn), lambda i, j, k: (j, i, 0))
    else:
        out_shape = jax.ShapeDtypeStruct((m, n), F32)
        out_spec = pl.BlockSpec((tm, tn), lambda i, j, k: (i, j))
    return pl.pallas_call(
        body, name=name, grid=(m // tm, n // tn, nk), out_shape=out_shape,
        in_specs=[pl.BlockSpec((tk, tm), lambda i, j, k: (k, i)), pl.BlockSpec((tk, tn), lambda i, j, k: (k, j))],
        out_specs=out_spec, scratch_shapes=[pltpu.VMEM((tm, tn), F32)],
        compiler_params=_params("parallel", "parallel", "arbitrary"),
    )(a, b)


def _adam_update(w, g, m, v):
    m_new = ADAM_B1 * m + (1.0 - ADAM_B1) * g
    v_new = ADAM_B2 * v + (1.0 - ADAM_B2) * (g * g)
    m_hat = m_new / (1.0 - ADAM_B1 ** ADAM_STEP)
    v_hat = v_new / (1.0 - ADAM_B2 ** ADAM_STEP)
    delta = -ADAM_LR * (m_hat / (jnp.sqrt(v_hat) + ADAM_EPS) + ADAM_WD * w)
    return delta, m_new, v_new


def _add_halves_kernel(part, recv, c_idx, name):
    _, _, r, n = part.shape
    tr = min(r, 256)

    def body(c_ref, p_ref, q_ref, o_ref):
        del c_ref
        o_ref[...] = (p_ref[...] + q_ref[...]).astype(o_ref.dtype)

    return pl.pallas_call(
        body, name=name, out_shape=jax.ShapeDtypeStruct((N_CHIPS, r, n), GRAD_COMM_DTYPE),
        grid_spec=pltpu.PrefetchScalarGridSpec(
            num_scalar_prefetch=1, grid=(N_CHIPS, r // tr),
            in_specs=[pl.BlockSpec((None, None, tr, n), lambda k, i, c: (k, c[0], i, 0)),
                      pl.BlockSpec((None, tr, n), lambda k, i, c: (k, i, 0))],
            out_specs=pl.BlockSpec((None, tr, n), lambda k, i, c: (k, i, 0))),
        compiler_params=_params("parallel", "parallel"),
    )(c_idx, part, recv)


def _sum_chips_kernel(parts, c_idx, name):
    _, r, n = parts.shape
    tr = min(r, 256)

    def body(c_ref, p_ref, o_ref):
        del c_ref
        acc = p_ref[0].astype(F32)
        for k in range(1, N_CHIPS):
            acc = acc + p_ref[k].astype(F32)
        o_ref[...] = acc

    return pl.pallas_call(
        body, name=name, out_shape=jax.ShapeDtypeStruct((2, r, n), F32),
        grid_spec=pltpu.PrefetchScalarGridSpec(
            num_scalar_prefetch=1, grid=(r // tr,),
            in_specs=[pl.BlockSpec((N_CHIPS, tr, n), lambda i, c: (0, i, 0))],
            out_specs=pl.BlockSpec((None, tr, n), lambda i, c: (c[0], i, 0))),
        compiler_params=_params("parallel"),
    )(c_idx, parts)


def _adam_kernel(w, g, m, v, name):
    r, n = w.shape
    tr = min(r, 256)

    def body(w_ref, g_ref, m_ref, v_ref, d_ref, mo_ref, vo_ref):
        d_ref[...], mo_ref[...], vo_ref[...] = _adam_update(w_ref[...], g_ref[...], m_ref[...], v_ref[...])

    spec = pl.BlockSpec((tr, n), lambda i: (i, 0))
    return pl.pallas_call(
        body, name=name, grid=(r // tr,), out_shape=[jax.ShapeDtypeStruct((r, n), F32)] * 3,
        in_specs=[spec] * 4, out_specs=[spec] * 3, compiler_params=_params("parallel"),
    )(w, g, m, v)


def _small_update_kernel(gathered, w, m, v):
    _, r, n = gathered.shape

    def body(g_ref, w_ref, m_ref, v_ref, gs_ref, d_ref, mo_ref, vo_ref):
        g = g_ref[0]
        for k in range(1, N_DEV):
            g = g + g_ref[k]
        gs_ref[...] = g
        d_ref[...], mo_ref[...], vo_ref[...] = _adam_update(w_ref[...], g, m_ref[...], v_ref[...])

    return pl.pallas_call(
        body, name="small_update", grid=(1,), out_shape=[jax.ShapeDtypeStruct((r, n), F32)] * 4,
        in_specs=[_full((N_DEV, r, n))] + [_full((r, n))] * 3, out_specs=[_full((r, n))] * 4,
        compiler_params=_params("arbitrary"),
    )(gathered, w, m, v)


def _ada_update_kernel(act_t, dmod, w, m, v):
    r, n = w.shape
    tr = 256

    def body(a_ref, d_ref, w_ref, m_ref, v_ref, g_ref, dl_ref, mo_ref, vo_ref):
        g = _dot(a_ref[...], d_ref[...])
        g_ref[...] = g
        dl_ref[...], mo_ref[...], vo_ref[...] = _adam_update(w_ref[...], g, m_ref[...], v_ref[...])

    spec = pl.BlockSpec((tr, n), lambda i: (i, 0))
    return pl.pallas_call(
        body, name="ada_update", grid=(r // tr,), out_shape=[jax.ShapeDtypeStruct((r, n), F32)] * 4,
        in_specs=[pl.BlockSpec((tr, N_DEV), lambda i: (i, 0)), _full((N_DEV, n)), spec, spec, spec],
        out_specs=[spec] * 4, compiler_params=_params("parallel"),
    )(act_t, dmod, w, m, v)


_SMALL_ROWS = (("b_ada", 48), ("g_mix", 8), ("g_ffn", 8), ("g_final", 8), ("b_spatial", 8),
               ("sinks", 1), ("loss", 1), ("pad", 6), ("w_spatial", 1024))


def _pack_small(parts):
    rows = []
    for name, nrows in _SMALL_ROWS:
        if name in parts:
            flat = parts[name].reshape(-1).astype(F32)
            flat = jnp.pad(flat, (0, nrows * LANES - flat.shape[0]))
            rows.append(flat.reshape(nrows, LANES))
        else:
            rows.append(jnp.zeros((nrows, LANES), F32))
    return jnp.concatenate(rows, axis=0)


def _unpack_small(packed, shapes):
    out, start = {}, 0
    for name, nrows in _SMALL_ROWS:
        if name in shapes:
            size = math.prod(shapes[name])
            out[name] = packed[start:start + nrows].reshape(-1)[:size].reshape(shapes[name])
        start += nrows
    return out


def kernel(x, c, positions, w_ada, b_ada, g_mix, w_in, w_spatial, b_spatial, sinks, w_out, g_ffn, w_ff1, w_ff2, g_final, loss_target, m_w_ada, m_b_ada, m_g_mix, m_w_in, m_w_spatial, m_b_spatial, m_sinks, m_w_out, m_g_ffn, m_w_ff1, m_w_ff2, m_g_final, v_w_ada, v_b_ada, v_g_mix, v_w_in, v_w_spatial, v_b_spatial, v_sinks, v_w_out, v_g_ffn, v_w_ff1, v_w_ff2, v_g_final):
    xi, yi, ci = lax.axis_index("x"), lax.axis_index("y"), lax.axis_index("c")
    chip = 2 * xi + yi
    dev = 2 * chip + ci
    seq = x.shape[1]
    x2, tgt = x[0], loss_target[0]
    pos = positions.reshape(seq, 1)
    ada_cols = w_ada.shape[2]

    c_all = _all_gather8(c, "gather_c").reshape(N_DEV, D_MODEL)
    b_shard = lax.dynamic_slice(b_ada, (0, chip * ada_cols), (1, ada_cols))
    mod_part, act = _mod_kernel(c_all, w_ada[0], b_shard)
    mod_all = _all_gather8(mod_part, "gather_mod")
    mod_me = lax.dynamic_index_in_dim(mod_all[0::2], dev, axis=1, keepdims=False)
    mod_me = mod_me.reshape(N_MOD, D_MODEL)
    shift1, scale1, gate1, shift2, scale2, gate2 = (mod_me[k:k + 1] for k in range(N_MOD))

    def gather_weight(w, name):
        r, n = w.shape[1], w.shape[2]
        halves = w[0].astype(WEIGHT_COMM_DTYPE).reshape(2, r // 2, n)
        return _all_gather8(halves, name, split=True).reshape(N_CHIPS, r, n)

    w_in_full = gather_weight(w_in, "gather_w_in").transpose(1, 0, 2).reshape(D_MODEL, IN_PROJ_WIDTH)
    w_out_full = gather_weight(w_out, "gather_w_out").reshape(D_MODEL, D_MODEL)
    w_ff1_blocks = gather_weight(w_ff1, "gather_w_ff1")
    w_ff2_blocks = gather_weight(w_ff2, "gather_w_ff2")

    zeros_row = jnp.zeros((1, D_MODEL), F32)
    vecs1 = jnp.concatenate([g_mix, shift1, scale1] + [zeros_row] * 5, axis=0)
    vecs2 = jnp.concatenate([gate1, shift2, scale2, gate2, g_ffn, g_final.reshape(1, D_MODEL)]
                            + [zeros_row] * 2, axis=0)
    bias_full = jnp.repeat(b_spatial[0].T, HEAD_DIM, axis=1)
    sink_rows = jnp.broadcast_to(sinks[0][:, None], (N_Q_HEADS, LANES))
    inv_freq = ROPE_THETA ** (-jnp.arange(0, ROT_DIM, 2, dtype=F32) / ROT_DIM)
    lane_d = jnp.arange(LANES) % HEAD_DIM
    invf = jnp.where(lane_d < ROT_DIM, inv_freq[lane_d % (ROT_DIM // 2)], 0.0).reshape(1, LANES)

    proj, hb = _in_proj_kernel(x2, vecs1, w_in_full)
    cat = _mixer_fwd_kernel(proj, pos, w_spatial[0], bias_full, sink_rows, invf)
    dx1, dcat, dmix, h2b, rb, dab, dffb, sums2 = _trunk_kernel(
        x2, tgt, cat, vecs2, w_out_full, w_ff1_blocks, w_ff2_blocks)
    dproj, dw_spatial, db_lanes, dsink_rows = _mixer_bwd_kernel(
        proj, pos, dcat, w_spatial[0], bias_full, sink_rows, invf)
    grad_x, sums1 = _in_proj_bwd_kernel(x2, dx1, dproj, vecs1, w_in_full)

    dw_ff2 = _weight_grad_kernel(rb, dffb, "dw_ff2", 1024, 1024)
    dw_ff1 = _weight_grad_kernel(h2b, dab, "dw_ff1", 1024, 1024, blocked_cols=True)
    dw_out = _weight_grad_kernel(cat, dmix, "dw_out", 1024, 1024)
    dw_in = _weight_grad_kernel(hb, dproj, "dw_in", 1024, 896)
    dw_in = dw_in.reshape(D_MODEL, N_CHIPS, IN_PROJ_WIDTH // N_CHIPS).transpose(1, 0, 2)

    big = {"w_in": (w_in, m_w_in, v_w_in, dw_in), "w_out": (w_out, m_w_out, v_w_out, dw_out),
           "w_ff1": (w_ff1, m_w_ff1, v_w_ff1, dw_ff1), "w_ff2": (w_ff2, m_w_ff2, v_w_ff2, dw_ff2)}
    names = list(big)
    parts = []
    for nm in names:
        r, n = big[nm][0].shape[1], big[nm][0].shape[2]
        parts.append(big[nm][3].reshape(N_CHIPS, 2, r // 2, n))
    c_idx = ci.reshape(1).astype(jnp.int32)
    from_sibling = _sibling_exchange_halves(parts, "grad_to_sibling")
    chip_sums = [_add_halves_kernel(p, q, c_idx, "grad_add_" + nm) for nm, p, q in zip(names, parts, from_sibling)]
    scattered = _chip_scatter(chip_sums, "grad_to_chips")
    totals = [_sum_chips_kernel(s, c_idx, "grad_sum_" + nm) for nm, s in zip(names, scattered)]
    shared = _sibling_share(totals, "grad_share")
    big_out = {}
    for nm, g in zip(names, shared):
        w, m, v, _ = big[nm]
        g = g.reshape(w.shape[1:])
        d, mn, vn = _adam_kernel(w[0], g, m[0], v[0], "adam_" + nm)
        big_out[nm] = tuple(t[None] for t in (g, d, mn, vn))

    dmod = jnp.concatenate([sums1[0:1], sums1[1:2], sums2[5:6], sums2[0:1], sums2[1:2], sums2[2:3]], axis=1)
    db_spatial = db_lanes[:, 0:GMLP_GROUPS].T
    small_grads = {"b_ada": dmod, "g_mix": sums1[2], "g_ffn": sums2[3], "g_final": sums2[4],
                   "b_spatial": db_spatial, "sinks": dsink_rows[:, 0], "loss": sums2[6, 0:1],
                   "w_spatial": dw_spatial}
    small_w = {"b_ada": b_ada, "g_mix": g_mix, "g_ffn": g_ffn, "g_final": g_final, "b_spatial": b_spatial,
               "sinks": sinks, "w_spatial": w_spatial}
    small_m = {"b_ada": m_b_ada, "g_mix": m_g_mix, "g_ffn": m_g_ffn, "g_final": m_g_final,
               "b_spatial": m_b_spatial, "sinks": m_sinks, "w_spatial": m_w_spatial}
    small_v = {"b_ada": v_b_ada, "g_mix": v_g_mix, "g_ffn": v_g_ffn, "g_final": v_g_final,
               "b_spatial": v_b_spatial, "sinks": v_sinks, "w_spatial": v_w_spatial}
    gathered = _all_gather8(_pack_small(small_grads), "gather_small")
    packed = _small_update_kernel(gathered, _pack_small(small_w), _pack_small(small_m), _pack_small(small_v))
    shapes = {k: a.shape for k, a in small_w.items()}
    sg, sd, sm, sv = (_unpack_small(p, shapes) for p in packed)
    loss = packed[0][sum(nr for nm, nr in _SMALL_ROWS[:6]), 0]

    dmod_all = gathered[:, 0:N_MOD * D_MODEL // LANES, :].reshape(N_DEV, N_MOD * D_MODEL)
    dmod_cols = lax.dynamic_slice(dmod_all, (0, chip * ada_cols), (N_DEV, ada_cols))
    ada = _ada_update_kernel(act.T, dmod_cols, w_ada[0], m_w_ada[0], v_w_ada[0])
    big_out["w_ada"] = tuple(t[None] for t in ada)

    order = ["w_ada", "b_ada", "g_mix", "w_in", "w_spatial", "b_spatial", "sinks", "w_out", "g_ffn",
             "w_ff1", "w_ff2", "g_final"]

    def leaf(nm, k):
        return big_out[nm][k] if nm in big_out else (sg, sd, sm, sv)[k][nm]

    outs = [loss, grad_x[None]]
    for k in range(4):
        outs += [leaf(nm, k) for nm in order]
    return tuple(outs)
```

```python
import functools
import math

import jax
import jax.numpy as jnp
from jax import lax
from jax.experimental import pallas as pl
from jax.experimental.pallas import tpu as pltpu

F32 = jnp.float32
MXU_DTYPE = jnp.bfloat16
WEIGHT_COMM_DTYPE = jnp.bfloat16
GRAD_COMM_DTYPE = jnp.bfloat16

D_MODEL = 1024
D_FF = 4096
HEAD_DIM = 64
GMLP_GROUPS = 8
GMLP_WIDTH = 512
CHUNK = 128
N_Q_HEADS = 8
N_KV_HEADS = 2
ATTN_WIDTH = 512
KV_WIDTH = 128
ROT_DIM = 16
ROPE_THETA = 500000.0
IN_PROJ_WIDTH = 1792
N_MOD = 6
EPS = 1e-5
N_CHIPS = 4
N_DEV = 8
LANES = 128

ADAM_LR = 0.001
ADAM_B1 = 0.9
ADAM_B2 = 0.999
ADAM_EPS = 1e-08
ADAM_WD = 0.01
ADAM_STEP = 10

VMEM_LIMIT_BYTES = 58 * 1024 * 1024
MESH = pl.DeviceIdType.MESH


def _params(*semantics):
    return pltpu.CompilerParams(dimension_semantics=semantics, vmem_limit_bytes=VMEM_LIMIT_BYTES)


def _dot(a, b):
    return jnp.dot(a.astype(MXU_DTYPE), b.astype(MXU_DTYPE), preferred_element_type=F32)


def _dot_nt(a, b):
    return lax.dot_general(a.astype(MXU_DTYPE), b.astype(MXU_DTYPE), (((1,), (1,)), ((), ())),
                           preferred_element_type=F32)


def _dot_tn(a, b):
    return lax.dot_general(a.astype(MXU_DTYPE), b.astype(MXU_DTYPE), (((0,), (0,)), ((), ())),
                           preferred_element_type=F32)


def _full(shape):
    return pl.BlockSpec(shape, lambda *_: (0,) * len(shape))


def _any():
    return pl.BlockSpec(memory_space=pl.ANY)


def _rowsum(v):
    return jnp.sum(v, axis=0, keepdims=True)


def _mean_last(v):
    return jnp.mean(v, axis=-1, keepdims=True)


def _all_gather8(block, name, split=False):
    blk_shape = block.shape[1:] if split else block.shape

    def body(x_ref, out_ref, send_sems, recv_sems, local_sem):
        x, y, c = lax.axis_index("x"), lax.axis_index("y"), lax.axis_index("c")
        me, sibling = (x, y, c), (x, y, 1 - c)
        chips = [(1 - x, y), (x, 1 - y), (1 - x, 1 - y)]
        src_mine = x_ref.at[c] if split else x_ref

        def slot(px, py, pc):
            return out_ref.at[4 * px + 2 * py + pc]

        def copy(k, blk, to, src=None):
            return pltpu.make_async_remote_copy(
                src_ref=slot(*blk) if src is None else src, dst_ref=slot(*blk),
                send_sem=send_sems.at[k], recv_sem=recv_sems.at[k],
                device_id=to, device_id_type=MESH)

        mine = pltpu.make_async_copy(src_mine, slot(*me), local_sem)
        mine.start()
        first = [copy(0, me, sibling, src=src_mine)]
        first += [copy(1 + j, me, (*chip, c), src=src_mine) for j, chip in enumerate(chips)]
        for cp in first:
            cp.start()
        passed = [copy(4 + j, (*chip, c), sibling) for j, chip in enumerate(chips)]
        for j, chip in enumerate(chips):
            copy(1 + j, (*chip, c), me).wait_recv()
            passed[j].start()
        copy(0, sibling, me).wait_recv()
        for j, chip in enumerate(chips):
            copy(4 + j, (*chip, 1 - c), me).wait_recv()
        for cp in first + passed:
            cp.wait_send()
        mine.wait()

    return pl.pallas_call(
        body, name=name,
        out_shape=jax.ShapeDtypeStruct((N_DEV,) + tuple(blk_shape), block.dtype),
        in_specs=[_any()], out_specs=_any(),
        scratch_shapes=[pltpu.SemaphoreType.DMA((7,)), pltpu.SemaphoreType.DMA((7,)),
                        pltpu.SemaphoreType.DMA],
    )(block)


def _sibling_exchange_halves(parts, name):
    n_arr = len(parts)

    def body(*refs):
        in_refs = refs[:n_arr]
        out_refs = refs[n_arr:2 * n_arr]
        send_sems, recv_sems = refs[2 * n_arr:]
        x, y, c = lax.axis_index("x"), lax.axis_index("y"), lax.axis_index("c")
        copies = []
        for a in range(n_arr):
            for k in range(N_CHIPS):
                copies.append(pltpu.make_async_remote_copy(
                    src_ref=in_refs[a].at[k, 1 - c], dst_ref=out_refs[a].at[k],
                    send_sem=send_sems.at[a * N_CHIPS + k], recv_sem=recv_sems.at[a * N_CHIPS + k],
                    device_id=(x, y, 1 - c), device_id_type=MESH))
        for cp in copies:
            cp.start()
        for cp in copies:
            cp.wait()

    return pl.pallas_call(
        body, name=name,
        out_shape=[jax.ShapeDtypeStruct((N_CHIPS,) + p.shape[2:], p.dtype) for p in parts],
        in_specs=[_any()] * n_arr, out_specs=[_any()] * n_arr,
        scratch_shapes=[pltpu.SemaphoreType.DMA((n_arr * N_CHIPS,)),
                        pltpu.SemaphoreType.DMA((n_arr * N_CHIPS,))],
    )(*parts)


def _chip_scatter(halves, name):
    n_arr = len(halves)

    def body(*refs):
        in_refs = refs[:n_arr]
        out_refs = refs[n_arr:2 * n_arr]
        send_sems, recv_sems, local_sems = refs[2 * n_arr:]
        x, y, c = lax.axis_index("x"), lax.axis_index("y"), lax.axis_index("c")
        chips = [(1 - x, y), (x, 1 - y), (1 - x, 1 - y)]
        copies, local = [], []
        for a in range(n_arr):
            for j, (px, py) in enumerate(chips):
                copies.append(pltpu.make_async_remote_copy(
                    src_ref=in_refs[a].at[2 * px + py], dst_ref=out_refs[a].at[j],
                    send_sem=send_sems.at[a * 3 + j], recv_sem=recv_sems.at[a * 3 + j],
                    device_id=(px, py, c), device_id_type=MESH))
            local.append(pltpu.make_async_copy(in_refs[a].at[2 * x + y], out_refs[a].at[3],
                                               local_sems.at[a]))
        for cp in copies + local:
            cp.start()
        for cp in copies + local:
            cp.wait()

    return pl.pallas_call(
        body, name=name,
        out_shape=[jax.ShapeDtypeStruct(h.shape, h.dtype) for h in halves],
        in_specs=[_any()] * n_arr, out_specs=[_any()] * n_arr,
        scratch_shapes=[pltpu.SemaphoreType.DMA((n_arr * 3,)), pltpu.SemaphoreType.DMA((n_arr * 3,)),
                        pltpu.SemaphoreType.DMA((n_arr,))],
    )(*halves)


def _sibling_share(bufs, name):
    n_arr = len(bufs)

    def body(*refs):
        out_refs = refs[n_arr:2 * n_arr]
        send_sems, recv_sems = refs[2 * n_arr:]
        x, y, c = lax.axis_index("x"), lax.axis_index("y"), lax.axis_index("c")
        copies = [pltpu.make_async_remote_copy(
            src_ref=out_refs[a].at[c], dst_ref=out_refs[a].at[c],
            send_sem=send_sems.at[a], recv_sem=recv_sems.at[a],
            device_id=(x, y, 1 - c), device_id_type=MESH) for a in range(n_arr)]
        for cp in copies:
            cp.start()
        for a in range(n_arr):
            pltpu.make_async_remote_copy(
                src_ref=out_refs[a].at[c], dst_ref=out_refs[a].at[1 - c],
                send_sem=send_sems.at[a], recv_sem=recv_sems.at[a],
                device_id=(x, y, 1 - c), device_id_type=MESH).wait()

    return pl.pallas_call(
        body, name=name,
        out_shape=[jax.ShapeDtypeStruct(b.shape, b.dtype) for b in bufs],
        in_specs=[_any()] * n_arr, out_specs=[_any()] * n_arr,
        input_output_aliases={a: a for a in range(n_arr)},
        scratch_shapes=[pltpu.SemaphoreType.DMA((n_arr,)), pltpu.SemaphoreType.DMA((n_arr,))],
    )(*bufs)


def _gelu_tanh(z):
    k = math.sqrt(2.0 / math.pi)
    t = jnp.tanh(k * (z + 0.044715 * (z * z * z)))
    return 0.5 * z * (1.0 + t), t


def _gelu_tanh_grad(z, t):
    k = math.sqrt(2.0 / math.pi)
    return 0.5 * (1.0 + t) + 0.5 * z * (1.0 - t * t) * (k * (1.0 + 3.0 * 0.044715 * (z * z)))


def _rope_angle_kernel(pos_row, invf_col):
    seq = pos_row.shape[1]

    def body(p_ref, f_ref, cos_ref, sin_ref):
        ang = p_ref[...].astype(F32) * f_ref[...]
        cos_ref[...] = jnp.cos(ang)
        sin_ref[...] = jnp.sin(ang)

    return pl.pallas_call(
        body, name="rope_angles", grid=(1,), out_shape=[jax.ShapeDtypeStruct((ROT_DIM // 2, seq), F32)] * 2,
        in_specs=[_full((1, seq)), _full((ROT_DIM // 2, 1))], out_specs=[_full((ROT_DIM // 2, seq))] * 2,
        compiler_params=_params("arbitrary"),
    )(pos_row, invf_col)


def _rope_lane_tables(cos, sin):
    cos_t, sin_t = cos.T, sin.T
    seq, half = cos_t.shape
    ones = jnp.ones((seq, HEAD_DIM - ROT_DIM), F32)
    c64 = jnp.concatenate([cos_t, cos_t, ones], axis=1)
    s1 = jnp.concatenate([sin_t, jnp.zeros((seq, HEAD_DIM - half), F32)], axis=1)
    s2 = jnp.concatenate([jnp.zeros((seq, half), F32), sin_t, jnp.zeros((seq, HEAD_DIM - ROT_DIM), F32)], axis=1)
    return jnp.concatenate([jnp.tile(t, (1, LANES // HEAD_DIM)) for t in (c64, s1, s2)], axis=1)


def _rope_apply(t, tab, sign):
    reps = t.shape[1] // LANES
    c_tab, s1, s2 = (jnp.tile(tab[:, LANES * k:LANES * (k + 1)], (1, reps)) if reps > 1
                     else tab[:, LANES * k:LANES * (k + 1)] for k in range(3))
    half = ROT_DIM // 2
    up = pltpu.roll(t, t.shape[1] - half, 1)
    down = pltpu.roll(t, half, 1)
    return t * c_tab + sign * (down * s2 - up * s1)


def _lane_masks(shape):
    lane = lax.broadcasted_iota(jnp.int32, shape, 1)
    return lane < HEAD_DIM, lane >= HEAD_DIM


HEADS_PER_GROUP = N_Q_HEADS // N_KV_HEADS
ATTN_SCALE = 1.0 / math.sqrt(HEAD_DIM)


def _attn_bias_t(first_block):
    kj = lax.broadcasted_iota(jnp.int32, (2 * CHUNK, CHUNK), 0)
    qi = lax.broadcasted_iota(jnp.int32, (2 * CHUNK, CHUNK), 1)
    ok = (kj > qi) & (kj <= qi + CHUNK) & (jnp.logical_not(first_block) | (kj >= CHUNK))
    return jnp.tile(jnp.where(ok, 0.0, -jnp.inf), (1, HEADS_PER_GROUP))


def _group_rows(x, g, lo, hi):
    rows = []
    for r in range(HEADS_PER_GROUP):
        h = HEADS_PER_GROUP * g + r
        pair = x[:, LANES * (h // 2):LANES * (h // 2 + 1)]
        rows.append(jnp.where(hi if h % 2 else lo, pair, 0.0))
    return jnp.concatenate(rows, axis=0)


def _pairs_from_rows(rows, lo):
    return [jnp.where(lo, rows[2 * CHUNK * k:2 * CHUNK * k + CHUNK], rows[2 * CHUNK * k + CHUNK:2 * CHUNK * (k + 1)])
            for k in range(HEADS_PER_GROUP // 2)]


def _group_dup(a, b, g, lo2):
    return jnp.where(lo2, a, b) if g == 0 else jnp.where(lo2, b, a)


def _sink_row(sink_ref, g):
    return jnp.concatenate([sink_ref[HEADS_PER_GROUP * g + r:HEADS_PER_GROUP * g + r + 1, :]
                            for r in range(HEADS_PER_GROUP)], axis=1)


def _attn_probs_t(k_dup, q_rows, bias_t, sink_row):
    s_t = _dot_nt(k_dup, q_rows) * ATTN_SCALE + bias_t
    m = jnp.maximum(jnp.max(s_t, axis=0, keepdims=True), sink_row)
    p = jnp.exp(s_t - m)
    e_sink = jnp.exp(sink_row - m)
    inv = 1.0 / (jnp.sum(p, axis=0, keepdims=True) + e_sink)
    return p * inv, e_sink * inv


def _sgu_forward_pair(wm, vp, j):
    lo, hi = _lane_masks(vp.shape)
    lhs = jnp.concatenate([wm[2 * j], wm[2 * j + 1]], axis=1)
    rhs = jnp.concatenate([jnp.where(lo, vp, 0.0), jnp.where(hi, vp, 0.0)], axis=0)
    return _dot(lhs, rhs)


def _masked_spatial(w_ref):
    t = lax.broadcasted_iota(jnp.int32, (CHUNK, CHUNK), 0)
    s = lax.broadcasted_iota(jnp.int32, (CHUNK, CHUNK), 1)
    tril = s <= t
    return [jnp.where(tril, w_ref[g], 0.0) for g in range(GMLP_GROUPS)], tril, s >= t


def _mod_kernel(c_all, w_shard, b_shard):
    n = w_shard.shape[1]
    tn = 512

    def body(c_ref, w_ref, b_ref, mod_ref, act_ref):
        cv = c_ref[...]
        act = cv * (1.0 / (1.0 + jnp.exp(-cv)))
        act_ref[...] = act
        mod_ref[...] = _dot(act, w_ref[...]) + b_ref[...]

    return pl.pallas_call(
        body, name="ada_mod", grid=(n // tn,),
        out_shape=[jax.ShapeDtypeStruct((N_DEV, n), F32), jax.ShapeDtypeStruct((N_DEV, D_MODEL), F32)],
        in_specs=[_full((N_DEV, D_MODEL)), pl.BlockSpec((D_MODEL, tn), lambda i: (0, i)),
                  pl.BlockSpec((1, tn), lambda i: (0, i))],
        out_specs=[pl.BlockSpec((N_DEV, tn), lambda i: (0, i)), _full((N_DEV, D_MODEL))],
        compiler_params=_params("arbitrary"),
    )(c_all, w_shard, b_shard)


def _in_proj_kernel(x, vecs, w_in):
    seq = x.shape[0]
    tm = 512

    def body(x_ref, v_ref, w_ref, proj_ref, h_ref):
        xv = x_ref[...]
        rstd = lax.rsqrt(_mean_last(xv * xv) + EPS)
        n1 = (xv * rstd) * v_ref[0:1, :]
        h = n1 * (1.0 + v_ref[2:3, :]) + v_ref[1:2, :]
        hb = h.astype(MXU_DTYPE)
        h_ref[...] = hb
        proj_ref[...] = _dot(hb, w_ref[...])

    return pl.pallas_call(
        body, name="in_proj", grid=(seq // tm,),
        out_shape=[jax.ShapeDtypeStruct((seq, IN_PROJ_WIDTH), F32),
                   jax.ShapeDtypeStruct((seq, D_MODEL), MXU_DTYPE)],
        in_specs=[pl.BlockSpec((tm, D_MODEL), lambda i: (i, 0)), _full((8, D_MODEL)),
                  _full((D_MODEL, IN_PROJ_WIDTH))],
        out_specs=[pl.BlockSpec((tm, IN_PROJ_WIDTH), lambda i: (i, 0)),
                   pl.BlockSpec((tm, D_MODEL), lambda i: (i, 0))],
        compiler_params=_params("parallel"),
    )(x, vecs, w_in)


def _mixer_fwd_kernel(proj, rope_tab, w_spatial, bias_full, sink_rows):
    seq = proj.shape[0]
    nb = seq // CHUNK
    kv_col = (2 * GMLP_WIDTH + ATTN_WIDTH) // (2 * KV_WIDTH)

    def body(proj_ref, prev_ref, tab_ref, ptab_ref, w_ref, bias_ref, sink_ref, cat_ref):
        i = pl.program_id(0)
        wm, _, _ = _masked_spatial(w_ref)
        for j in range(GMLP_GROUPS // 2):
            cols = slice(LANES * j, LANES * (j + 1))
            vcols = slice(GMLP_WIDTH + LANES * j, GMLP_WIDTH + LANES * (j + 1))
            u, _ = _gelu_tanh(proj_ref[:, cols])
            vp, _ = _gelu_tanh(proj_ref[:, vcols])
            sv = _sgu_forward_pair(wm, vp, j) + bias_ref[:, cols]
            cat_ref[:, cols] = (u * sv).astype(cat_ref.dtype)
        o = 2 * GMLP_WIDTH
        tab = tab_ref[...]
        q_r = _rope_apply(proj_ref[:, o:o + ATTN_WIDTH], tab, 1.0)
        k_cur = _rope_apply(proj_ref[:, o + ATTN_WIDTH:o + ATTN_WIDTH + KV_WIDTH], tab, 1.0)
        k_prev = _rope_apply(prev_ref[:, 0:KV_WIDTH], ptab_ref[...], 1.0)
        k_a = jnp.concatenate([k_prev, k_cur], axis=0)
        v_a = jnp.concatenate([prev_ref[:, KV_WIDTH:2 * KV_WIDTH],
                               proj_ref[:, o + ATTN_WIDTH + KV_WIDTH:o + ATTN_WIDTH + 2 * KV_WIDTH]], axis=0)
        k_b = pltpu.roll(k_a, HEAD_DIM, 1)
        v_b = pltpu.roll(v_a, HEAD_DIM, 1)
        bias_t = _attn_bias_t(i == 0)
        lo, hi = _lane_masks((CHUNK, LANES))
        lo2, _ = _lane_masks((2 * CHUNK, LANES))
        for g in range(N_KV_HEADS):
            p_t, _ = _attn_probs_t(_group_dup(k_a, k_b, g, lo2), _group_rows(q_r, g, lo, hi), bias_t,
                                   _sink_row(sink_ref, g))
            o_t = _dot(_group_dup(v_a, v_b, g, lo2).T, p_t)
            for k, pair in enumerate(_pairs_from_rows(o_t.T, lo)):
                c0 = GMLP_WIDTH + LANES * (2 * g + k)
                cat_ref[:, c0:c0 + LANES] = pair.astype(cat_ref.dtype)

    return pl.pallas_call(
        body, name="mixer_fwd", grid=(nb,),
        out_shape=jax.ShapeDtypeStruct((seq, D_MODEL), MXU_DTYPE),
        in_specs=[pl.BlockSpec((CHUNK, IN_PROJ_WIDTH), lambda i: (i, 0)),
                  pl.BlockSpec((CHUNK, 2 * KV_WIDTH), lambda i: (jnp.maximum(i - 1, 0), kv_col)),
                  pl.BlockSpec((CHUNK, 3 * LANES), lambda i: (i, 0)),
                  pl.BlockSpec((CHUNK, 3 * LANES), lambda i: (jnp.maximum(i - 1, 0), 0)),
                  _full((GMLP_GROUPS, CHUNK, CHUNK)), _full((CHUNK, GMLP_WIDTH)),
                  _full((N_Q_HEADS, LANES))],
        out_specs=pl.BlockSpec((CHUNK, D_MODEL), lambda i: (i, 0)),
        compiler_params=_params("parallel"),
    )(proj, proj, rope_tab, rope_tab, w_spatial, bias_full, sink_rows)


def _trunk_kernel(x, target, cat, vecs, w_out, w_ff1, w_ff2):
    seq = x.shape[0]
    tm = 256
    nj = D_FF // D_MODEL

    def body(x_ref, t_ref, cat_ref, v_ref, wout_hbm, w1_hbm, w2_hbm,
             dx1_ref, dcat_ref, dmix_ref, h2_ref, r_ref, da_ref, dff_ref, sums_ref,
             wout, w1, w2, a_scr, sem):
        i = pl.program_id(0)

        @pl.when(i == 0)
        def _():
            copies = [pltpu.make_async_copy(wout_hbm, wout, sem.at[0]),
                      pltpu.make_async_copy(w1_hbm, w1, sem.at[1]),
                      pltpu.make_async_copy(w2_hbm, w2, sem.at[2])]
            for cp in copies:
                cp.start()
            for cp in copies:
                cp.wait()
            sums_ref[...] = jnp.zeros_like(sums_ref)

        gate1, shift2, scale2 = v_ref[0:1, :], v_ref[1:2, :], v_ref[2:3, :]
        gate2, g_ffn, g_final = v_ref[3:4, :], v_ref[4:5, :], v_ref[5:6, :]

        mix = _dot(cat_ref[...], wout[...])
        x1 = x_ref[...] + gate1 * mix
        rstd2 = lax.rsqrt(_mean_last(x1 * x1) + EPS)
        xh2 = x1 * rstd2
        n2 = xh2 * g_ffn
        h2b = (n2 * (1.0 + scale2) + shift2).astype(MXU_DTYPE)
        h2_ref[...] = h2b
        ff = jnp.zeros((tm, D_MODEL), F32)
        for j in range(nj):
            a = _dot(h2b, w1[j])
            a_scr[j] = a
            relu = jnp.maximum(a, 0.0)
            rb = (relu * relu).astype(MXU_DTYPE)
            r_ref[:, D_MODEL * j:D_MODEL * (j + 1)] = rb
            ff = ff + _dot(rb, w2[j])
        x2 = x1 + gate2 * ff
        rstd3 = lax.rsqrt(_mean_last(x2 * x2) + EPS)
        xh3 = x2 * rstd3
        err = xh3 * g_final - t_ref[...]
        loss = 0.5 * _rowsum(_mean_last(err * err))
        dy = err * (1.0 / D_MODEL)
        dxh3 = dy * g_final
        dx2 = rstd3 * (dxh3 - xh3 * _mean_last(dxh3 * xh3))
        dffb = (dx2 * gate2).astype(MXU_DTYPE)
        dff_ref[...] = dffb
        dh2 = jnp.zeros((tm, D_MODEL), F32)
        for j in range(nj):
            dr = _dot_nt(dffb, w2[j])
            dab = (dr * (2.0 * jnp.maximum(a_scr[j], 0.0))).astype(MXU_DTYPE)
            da_ref[:, D_MODEL * j:D_MODEL * (j + 1)] = dab
            dh2 = dh2 + _dot_nt(dab, w1[j])
        dn2 = dh2 * (1.0 + scale2)
        dxh2 = dn2 * g_ffn
        dx1 = dx2 + rstd2 * (dxh2 - xh2 * _mean_last(dxh2 * xh2))
        dx1_ref[...] = dx1
        dmixb = (dx1 * gate1).astype(MXU_DTYPE)
        dmix_ref[...] = dmixb
        dcat_ref[...] = _dot_nt(dmixb, wout[...])

        sums_ref[0:1, :] += _rowsum(dh2)
        sums_ref[1:2, :] += _rowsum(dh2 * n2)
        sums_ref[2:3, :] += _rowsum(dx2 * ff)
        sums_ref[3:4, :] += _rowsum(dn2 * xh2)
        sums_ref[4:5, :] += _rowsum(dy * xh3)
        sums_ref[5:6, :] += _rowsum(dx1 * mix)
        sums_ref[6:7, :] += jnp.broadcast_to(loss, (1, D_MODEL))

    tok = lambda w: pl.BlockSpec((tm, w), lambda i: (i, 0))
    return pl.pallas_call(
        body, name="trunk", grid=(seq // tm,),
        out_shape=[jax.ShapeDtypeStruct((seq, D_MODEL), F32), jax.ShapeDtypeStruct((seq, D_MODEL), F32),
                   jax.ShapeDtypeStruct((seq, D_MODEL), MXU_DTYPE), jax.ShapeDtypeStruct((seq, D_MODEL), MXU_DTYPE),
                   jax.ShapeDtypeStruct((seq, D_FF), MXU_DTYPE), jax.ShapeDtypeStruct((seq, D_FF), MXU_DTYPE),
                   jax.ShapeDtypeStruct((seq, D_MODEL), MXU_DTYPE), jax.ShapeDtypeStruct((8, D_MODEL), F32)],
        in_specs=[tok(D_MODEL), tok(D_MODEL), tok(D_MODEL), _full((8, D_MODEL)), _any(), _any(), _any()],
        out_specs=[tok(D_MODEL), tok(D_MODEL), tok(D_MODEL), tok(D_MODEL), tok(D_FF), tok(D_FF), tok(D_MODEL),
                   _full((8, D_MODEL))],
        scratch_shapes=[pltpu.VMEM((D_MODEL, D_MODEL), MXU_DTYPE), pltpu.VMEM((nj, D_MODEL, D_MODEL), MXU_DTYPE),
                        pltpu.VMEM((nj, D_MODEL, D_MODEL), MXU_DTYPE), pltpu.VMEM((nj, tm, D_MODEL), F32),
                        pltpu.SemaphoreType.DMA((3,))],
        compiler_params=_params("arbitrary"),
    )(x, target, cat, vecs, w_out, w_ff1, w_ff2)


def _mixer_bwd_kernel(proj, rope_tab, dcat, w_spatial, w_spatial_t, bias_full, sink_rows):
    seq = proj.shape[0]
    nb = seq // CHUNK
    kv_col = (2 * GMLP_WIDTH + ATTN_WIDTH) // (2 * KV_WIDTH)

    def body(proj_ref, prev_ref, tab_ref, ptab_ref, dcat_ref, w_ref, wt_ref, bias_ref, sink_ref,
             dproj_ref, dw_ref, db_ref, dsink_ref, carry):
        step = pl.program_id(0)
        blk = nb - 1 - step

        @pl.when(step == 0)
        def _():
            carry[...] = jnp.zeros_like(carry)
            dw_ref[...] = jnp.zeros_like(dw_ref)
            db_ref[...] = jnp.zeros_like(db_ref)
            dsink_ref[...] = jnp.zeros_like(dsink_ref)

        wm, tril, triu = _masked_spatial(w_ref)
        lo, hi = _lane_masks((CHUNK, LANES))
        lane = lax.broadcasted_iota(jnp.int32, (CHUNK, LANES), 1)
        db = jnp.zeros((CHUNK, LANES), F32)
        for j in range(GMLP_GROUPS // 2):
            cols = slice(LANES * j, LANES * (j + 1))
            vcols = slice(GMLP_WIDTH + LANES * j, GMLP_WIDTH + LANES * (j + 1))
            zu, zv = proj_ref[:, cols], proj_ref[:, vcols]
            u, tu = _gelu_tanh(zu)
            vp, tv = _gelu_tanh(zv)
            sv = _sgu_forward_pair(wm, vp, j) + bias_ref[:, cols]
            dout = dcat_ref[:, cols]
            du = dout * sv
            dsv = dout * u
            dsv_lo, dsv_hi = jnp.where(lo, dsv, 0.0), jnp.where(hi, dsv, 0.0)
            lhs_t = jnp.concatenate([jnp.where(triu, wt_ref[2 * j], 0.0),
                                     jnp.where(triu, wt_ref[2 * j + 1], 0.0)], axis=1)
            dv = _dot(lhs_t, jnp.concatenate([dsv_lo, dsv_hi], axis=0))
            dw_ref[2 * j] += jnp.where(tril, _dot_nt(dsv_lo, vp), 0.0)
            dw_ref[2 * j + 1] += jnp.where(tril, _dot_nt(dsv_hi, vp), 0.0)
            db = db + (jnp.where(lane == 2 * j, jnp.sum(dsv_lo, axis=1, keepdims=True), 0.0)
                       + jnp.where(lane == 2 * j + 1, jnp.sum(dsv_hi, axis=1, keepdims=True), 0.0))
            dproj_ref[:, cols] = (du * _gelu_tanh_grad(zu, tu)).astype(dproj_ref.dtype)
            dproj_ref[:, vcols] = (dv * _gelu_tanh_grad(zv, tv)).astype(dproj_ref.dtype)
        db_ref[...] += db
        o = 2 * GMLP_WIDTH
        tab = tab_ref[...]
        q_r = _rope_apply(proj_ref[:, o:o + ATTN_WIDTH], tab, 1.0)
        k_cur = _rope_apply(proj_ref[:, o + ATTN_WIDTH:o + ATTN_WIDTH + KV_WIDTH], tab, 1.0)
        k_prev = _rope_apply(prev_ref[:, 0:KV_WIDTH], ptab_ref[...], 1.0)
        k_a = jnp.concatenate([k_prev, k_cur], axis=0)
        v_a = jnp.concatenate([prev_ref[:, KV_WIDTH:2 * KV_WIDTH],
                               proj_ref[:, o + ATTN_WIDTH + KV_WIDTH:o + ATTN_WIDTH + 2 * KV_WIDTH]], axis=0)
        k_b = pltpu.roll(k_a, HEAD_DIM, 1)
        v_b = pltpu.roll(v_a, HEAD_DIM, 1)
        bias_t = _attn_bias_t(blk == 0)
        lo2, _ = _lane_masks((2 * CHUNK, LANES))
        dout_b = dcat_ref[:, GMLP_WIDTH:GMLP_WIDTH + ATTN_WIDTH]
        dk_tot, dv_tot, dq_pairs = [], [], []
        for g in range(N_KV_HEADS):
            k_dup, v_dup = _group_dup(k_a, k_b, g, lo2), _group_dup(v_a, v_b, g, lo2)
            q_rows = _group_rows(q_r, g, lo, hi)
            do_rows = _group_rows(dout_b, g, lo, hi)
            p_t, p_sink = _attn_probs_t(k_dup, q_rows, bias_t, _sink_row(sink_ref, g))
            dp_t = _dot_nt(v_dup, do_rows)
            delta = jnp.sum(p_t * dp_t, axis=0, keepdims=True)
            ds_t = p_t * (dp_t - delta) * ATTN_SCALE
            dsink = -p_sink * delta
            for r in range(HEADS_PER_GROUP):
                h = HEADS_PER_GROUP * g + r
                dsink_ref[h:h + 1, :] += jnp.broadcast_to(
                    jnp.sum(dsink[:, LANES * r:LANES * (r + 1)], axis=1, keepdims=True), (1, LANES))
            dk_full = _dot(ds_t, q_rows)
            dv_full = _dot(p_t, do_rows)
            dk_tot.append(dk_full + pltpu.roll(dk_full, HEAD_DIM, 1))
            dv_tot.append(dv_full + pltpu.roll(dv_full, HEAD_DIM, 1))
            dq_t = _dot(k_dup.T, ds_t)
            dq_pairs += _pairs_from_rows(dq_t.T, lo)
        dk_all = jnp.where(lo2, dk_tot[0], dk_tot[1])
        dv_all = jnp.where(lo2, dv_tot[0], dv_tot[1])
        dk_cur = dk_all[CHUNK:, :] + carry[:, 0:KV_WIDTH]
        dv_cur = dv_all[CHUNK:, :] + carry[:, KV_WIDTH:2 * KV_WIDTH]
        carry[:, 0:KV_WIDTH] = dk_all[:CHUNK, :]
        carry[:, KV_WIDTH:2 * KV_WIDTH] = dv_all[:CHUNK, :]
        dq = _rope_apply(jnp.concatenate(dq_pairs, axis=1), tab, -1.0)
        dproj_ref[:, o:o + ATTN_WIDTH] = dq.astype(dproj_ref.dtype)
        dproj_ref[:, o + ATTN_WIDTH:o + ATTN_WIDTH + KV_WIDTH] = (
            _rope_apply(dk_cur, tab, -1.0).astype(dproj_ref.dtype))
        dproj_ref[:, o + ATTN_WIDTH + KV_WIDTH:o + ATTN_WIDTH + 2 * KV_WIDTH] = dv_cur.astype(dproj_ref.dtype)

    rev = lambda i: nb - 1 - i
    return pl.pallas_call(
        body, name="mixer_bwd", grid=(nb,),
        out_shape=[jax.ShapeDtypeStruct((seq, IN_PROJ_WIDTH), MXU_DTYPE),
                   jax.ShapeDtypeStruct((GMLP_GROUPS, CHUNK, CHUNK), F32),
                   jax.ShapeDtypeStruct((CHUNK, LANES), F32),
                   jax.ShapeDtypeStruct((N_Q_HEADS, LANES), F32)],
        in_specs=[pl.BlockSpec((CHUNK, IN_PROJ_WIDTH), lambda i: (rev(i), 0)),
                  pl.BlockSpec((CHUNK, 2 * KV_WIDTH), lambda i: (jnp.maximum(rev(i) - 1, 0), kv_col)),
                  pl.BlockSpec((CHUNK, 3 * LANES), lambda i: (rev(i), 0)),
                  pl.BlockSpec((CHUNK, 3 * LANES), lambda i: (jnp.maximum(rev(i) - 1, 0), 0)),
                  pl.BlockSpec((CHUNK, D_MODEL), lambda i: (rev(i), 0)),
                  _full((GMLP_GROUPS, CHUNK, CHUNK)), _full((GMLP_GROUPS, CHUNK, CHUNK)),
                  _full((CHUNK, GMLP_WIDTH)), _full((N_Q_HEADS, LANES))],
        out_specs=[pl.BlockSpec((CHUNK, IN_PROJ_WIDTH), lambda i: (rev(i), 0)),
                   _full((GMLP_GROUPS, CHUNK, CHUNK)), _full((CHUNK, LANES)), _full((N_Q_HEADS, LANES))],
        scratch_shapes=[pltpu.VMEM((CHUNK, 2 * KV_WIDTH), F32)],
        compiler_params=_params("arbitrary"),
    )(proj, proj, rope_tab, rope_tab, dcat, w_spatial, w_spatial_t, bias_full, sink_rows)


def _in_proj_bwd_kernel(x, dx1, dproj, vecs, w_in):
    seq = x.shape[0]
    tm = 512

    def body(x_ref, dx1_ref, dp_ref, v_ref, w_ref, gx_ref, sums_ref):
        @pl.when(pl.program_id(0) == 0)
        def _():
            sums_ref[...] = jnp.zeros_like(sums_ref)

        g_mix, scale1 = v_ref[0:1, :], v_ref[2:3, :]
        dh = _dot_nt(dp_ref[...], w_ref[...])
        xv = x_ref[...]
        rstd = lax.rsqrt(_mean_last(xv * xv) + EPS)
        xh = xv * rstd
        dn1 = dh * (1.0 + scale1)
        dxh = dn1 * g_mix
        gx_ref[...] = dx1_ref[...] + rstd * (dxh - xh * _mean_last(dxh * xh))
        sums_ref[0:1, :] += _rowsum(dh)
        sums_ref[1:2, :] += _rowsum(dh * (xh * g_mix))
        sums_ref[2:3, :] += _rowsum(dn1 * xh)

    return pl.pallas_call(
        body, name="in_proj_bwd", grid=(seq // tm,),
        out_shape=[jax.ShapeDtypeStruct((seq, D_MODEL), F32), jax.ShapeDtypeStruct((8, D_MODEL), F32)],
        in_specs=[pl.BlockSpec((tm, D_MODEL), lambda i: (i, 0)), pl.BlockSpec((tm, D_MODEL), lambda i: (i, 0)),
                  pl.BlockSpec((tm, IN_PROJ_WIDTH), lambda i: (i, 0)), _full((8, D_MODEL)),
                  _full((D_MODEL, IN_PROJ_WIDTH))],
        out_specs=[pl.BlockSpec((tm, D_MODEL), lambda i: (i, 0)), _full((8, D_MODEL))],
        compiler_params=_params("arbitrary"),
    )(x, dx1, dproj, vecs, w_in)


def _weight_grad_kernel(a, b, name, tm, tn, blocked_cols=False):
    seq, m = a.shape
    n = b.shape[1]
    tk = 512
    nk = seq // tk

    def body(a_ref, b_ref, o_ref, acc):
        k = pl.program_id(2)

        @pl.when(k == 0)
        def _():
            acc[...] = jnp.zeros_like(acc)

        acc[...] += _dot_tn(a_ref[...], b_ref[...])

        @pl.when(k == nk - 1)
        def _():
            o_ref[...] = acc[...]

    if blocked_cols:
        out_shape = jax.ShapeDtypeStruct((n // tn, m, tn), F32)
        out_spec = pl.BlockSpec((None, tm, tn), lambda i, j, k: (j, i, 0))
    else:
        out_shape = jax.ShapeDtypeStruct((m, n), F32)
        out_spec = pl.BlockSpec((tm, tn), lambda i, j, k: (i, j))
    return pl.pallas_call(
        body, name=name, grid=(m // tm, n // tn, nk), out_shape=out_shape,
        in_specs=[pl.BlockSpec((tk, tm), lambda i, j, k: (k, i)), pl.BlockSpec((tk, tn), lambda i, j, k: (k, j))],
        out_specs=out_spec, scratch_shapes=[pltpu.VMEM((tm, tn), F32)],
        compiler_params=_params("parallel", "parallel", "arbitrary"),
    )(a, b)


def _adam_update(w, g, m, v):
    m_new = ADAM_B1 * m + (1.0 - ADAM_B1) * g
    v_new = ADAM_B2 * v + (1.0 - ADAM_B2) * (g * g)
    m_hat = m_new / (1.0 - ADAM_B1 ** ADAM_STEP)
    v_hat = v_new / (1.0 - ADAM_B2 ** ADAM_STEP)
    delta = -ADAM_LR * (m_hat / (jnp.sqrt(v_hat) + ADAM_EPS) + ADAM_WD * w)
    return delta, m_new, v_new


def _add_halves_kernel(part, recv, c_idx, name):
    _, _, r, n = part.shape
    tr = min(r, 256)

    def body(c_ref, p_ref, q_ref, o_ref):
        del c_ref
        o_ref[...] = (p_ref[...] + q_ref[...]).astype(o_ref.dtype)

    return pl.pallas_call(
        body, name=name, out_shape=jax.ShapeDtypeStruct((N_CHIPS, r, n), GRAD_COMM_DTYPE),
        grid_spec=pltpu.PrefetchScalarGridSpec(
            num_scalar_prefetch=1, grid=(N_CHIPS, r // tr),
            in_specs=[pl.BlockSpec((None, None, tr, n), lambda k, i, c: (k, c[0], i, 0)),
                      pl.BlockSpec((None, tr, n), lambda k, i, c: (k, i, 0))],
            out_specs=pl.BlockSpec((None, tr, n), lambda k, i, c: (k, i, 0))),
        compiler_params=_params("parallel", "parallel"),
    )(c_idx, part, recv)


def _sum_chips_kernel(parts, c_idx, name):
    _, r, n = parts.shape
    tr = min(r, 256)

    def body(c_ref, p_ref, o_ref):
        del c_ref
        acc = p_ref[0].astype(F32)
        for k in range(1, N_CHIPS):
            acc = acc + p_ref[k].astype(F32)
        o_ref[...] = acc

    return pl.pallas_call(
        body, name=name, out_shape=jax.ShapeDtypeStruct((2, r, n), F32),
        grid_spec=pltpu.PrefetchScalarGridSpec(
            num_scalar_prefetch=1, grid=(r // tr,),
            in_specs=[pl.BlockSpec((N_CHIPS, tr, n), lambda i, c: (0, i, 0))],
            out_specs=pl.BlockSpec((None, tr, n), lambda i, c: (c[0], i, 0))),
        compiler_params=_params("parallel"),
    )(c_idx, parts)


def _adam_kernel(w, g, m, v, name):
    r, n = w.shape
    tr = min(r, 256)

    def body(w_ref, g_ref, m_ref, v_ref, d_ref, mo_ref, vo_ref):
        d_ref[...], mo_ref[...], vo_ref[...] = _adam_update(w_ref[...], g_ref[...], m_ref[...], v_ref[...])

    spec = pl.BlockSpec((tr, n), lambda i: (i, 0))
    return pl.pallas_call(
        body, name=name, grid=(r // tr,), out_shape=[jax.ShapeDtypeStruct((r, n), F32)] * 3,
        in_specs=[spec] * 4, out_specs=[spec] * 3, compiler_params=_params("parallel"),
    )(w, g, m, v)


def _small_update_kernel(gathered, w, m, v):
    _, r, n = gathered.shape

    def body(g_ref, w_ref, m_ref, v_ref, gs_ref, d_ref, mo_ref, vo_ref):
        g = g_ref[0]
        for k in range(1, N_DEV):
            g = g + g_ref[k]
        gs_ref[...] = g
        d_ref[...], mo_ref[...], vo_ref[...] = _adam_update(w_ref[...], g, m_ref[...], v_ref[...])

    return pl.pallas_call(
        body, name="small_update", grid=(1,), out_shape=[jax.ShapeDtypeStruct((r, n), F32)] * 4,
        in_specs=[_full((N_DEV, r, n))] + [_full((r, n))] * 3, out_specs=[_full((r, n))] * 4,
        compiler_params=_params("arbitrary"),
    )(gathered, w, m, v)


def _ada_update_kernel(act_t, dmod, w, m, v):
    r, n = w.shape
    tr = 256

    def body(a_ref, d_ref, w_ref, m_ref, v_ref, g_ref, dl_ref, mo_ref, vo_ref):
        g = _dot(a_ref[...], d_ref[...])
        g_ref[...] = g
        dl_ref[...], mo_ref[...], vo_ref[...] = _adam_update(w_ref[...], g, m_ref[...], v_ref[...])

    spec = pl.BlockSpec((tr, n), lambda i: (i, 0))
    return pl.pallas_call(
        body, name="ada_update", grid=(r // tr,), out_shape=[jax.ShapeDtypeStruct((r, n), F32)] * 4,
        in_specs=[pl.BlockSpec((tr, N_DEV), lambda i: (i, 0)), _full((N_DEV, n)), spec, spec, spec],
        out_specs=[spec] * 4, compiler_params=_params("parallel"),
    )(act_t, dmod, w, m, v)


_SMALL_ROWS = (("b_ada", 48), ("g_mix", 8), ("g_ffn", 8), ("g_final", 8), ("b_spatial", 8),
               ("sinks", 1), ("loss", 1), ("pad", 6), ("w_spatial", 1024))


def _pack_small(parts):
    rows = []
    for name, nrows in _SMALL_ROWS:
        if name in parts:
            flat = parts[name].reshape(-1).astype(F32)
            flat = jnp.pad(flat, (0, nrows * LANES - flat.shape[0]))
            rows.append(flat.reshape(nrows, LANES))
        else:
            rows.append(jnp.zeros((nrows, LANES), F32))
    return jnp.concatenate(rows, axis=0)


def _unpack_small(packed, shapes):
    out, start = {}, 0
    for name, nrows in _SMALL_ROWS:
        if name in shapes:
            size = math.prod(shapes[name])
            out[name] = packed[start:start + nrows].reshape(-1)[:size].reshape(shapes[name])
        start += nrows
    return out


def kernel(x, c, positions, w_ada, b_ada, g_mix, w_in, w_spatial, b_spatial, sinks, w_out, g_ffn, w_ff1, w_ff2, g_final, loss_target, m_w_ada, m_b_ada, m_g_mix, m_w_in, m_w_spatial, m_b_spatial, m_sinks, m_w_out, m_g_ffn, m_w_ff1, m_w_ff2, m_g_final, v_w_ada, v_b_ada, v_g_mix, v_w_in, v_w_spatial, v_b_spatial, v_sinks, v_w_out, v_g_ffn, v_w_ff1, v_w_ff2, v_g_final):
    xi, yi, ci = lax.axis_index("x"), lax.axis_index("y"), lax.axis_index("c")
    chip = 2 * xi + yi
    dev = 2 * chip + ci
    seq = x.shape[1]
    x2, tgt = x[0], loss_target[0]
    ada_cols = w_ada.shape[2]

    c_all = _all_gather8(c, "gather_c").reshape(N_DEV, D_MODEL)
    b_shard = lax.dynamic_slice(b_ada, (0, chip * ada_cols), (1, ada_cols))
    mod_part, act = _mod_kernel(c_all, w_ada[0], b_shard)
    mod_all = _all_gather8(mod_part, "gather_mod")
    mod_me = lax.dynamic_index_in_dim(mod_all[0::2], dev, axis=1, keepdims=False)
    mod_me = mod_me.reshape(N_MOD, D_MODEL)
    shift1, scale1, gate1, shift2, scale2, gate2 = (mod_me[k:k + 1] for k in range(N_MOD))

    def gather_weight(w, name):
        r, n = w.shape[1], w.shape[2]
        halves = w[0].astype(WEIGHT_COMM_DTYPE).reshape(2, r // 2, n)
        return _all_gather8(halves, name, split=True).reshape(N_CHIPS, r, n)

    w_in_full = gather_weight(w_in, "gather_w_in").transpose(1, 0, 2).reshape(D_MODEL, IN_PROJ_WIDTH)
    w_out_full = gather_weight(w_out, "gather_w_out").reshape(D_MODEL, D_MODEL)
    w_ff1_blocks = gather_weight(w_ff1, "gather_w_ff1")
    w_ff2_blocks = gather_weight(w_ff2, "gather_w_ff2")

    zeros_row = jnp.zeros((1, D_MODEL), F32)
    vecs1 = jnp.concatenate([g_mix, shift1, scale1] + [zeros_row] * 5, axis=0)
    vecs2 = jnp.concatenate([gate1, shift2, scale2, gate2, g_ffn, g_final.reshape(1, D_MODEL)]
                            + [zeros_row] * 2, axis=0)
    bias_full = jnp.repeat(b_spatial[0].T, HEAD_DIM, axis=1)
    sink_rows = jnp.broadcast_to(sinks[0][:, None], (N_Q_HEADS, LANES))
    inv_freq = ROPE_THETA ** (-jnp.arange(0, ROT_DIM, 2, dtype=F32) / ROT_DIM)
    rope_tab = _rope_lane_tables(*_rope_angle_kernel(positions, inv_freq.reshape(ROT_DIM // 2, 1)))

    proj, hb = _in_proj_kernel(x2, vecs1, w_in_full)
    cat = _mixer_fwd_kernel(proj, rope_tab, w_spatial[0], bias_full, sink_rows)
    dx1, dcat, dmix, h2b, rb, dab, dffb, sums2 = _trunk_kernel(
        x2, tgt, cat, vecs2, w_out_full, w_ff1_blocks, w_ff2_blocks)
    dproj, dw_spatial, db_lanes, dsink_rows = _mixer_bwd_kernel(
        proj, rope_tab, dcat, w_spatial[0], w_spatial[0].transpose(0, 2, 1), bias_full, sink_rows)
    grad_x, sums1 = _in_proj_bwd_kernel(x2, dx1, dproj, vecs1, w_in_full)

    dw_ff2 = _weight_grad_kernel(rb, dffb, "dw_ff2", 1024, 1024)
    dw_ff1 = _weight_grad_kernel(h2b, dab, "dw_ff1", 1024, 1024, blocked_cols=True)
    dw_out = _weight_grad_kernel(cat, dmix, "dw_out", 1024, 1024)
    dw_in = _weight_grad_kernel(hb, dproj, "dw_in", 1024, 896)
    dw_in = dw_in.reshape(D_MODEL, N_CHIPS, IN_PROJ_WIDTH // N_CHIPS).transpose(1, 0, 2)

    big = {"w_in": (w_in, m_w_in, v_w_in, dw_in), "w_out": (w_out, m_w_out, v_w_out, dw_out),
           "w_ff1": (w_ff1, m_w_ff1, v_w_ff1, dw_ff1), "w_ff2": (w_ff2, m_w_ff2, v_w_ff2, dw_ff2)}
    names = list(big)
    parts = []
    for nm in names:
        r, n = big[nm][0].shape[1], big[nm][0].shape[2]
        parts.append(big[nm][3].reshape(N_CHIPS, 2, r // 2, n))
    c_idx = ci.reshape(1).astype(jnp.int32)
    from_sibling = _sibling_exchange_halves(parts, "grad_to_sibling")
    chip_sums = [_add_halves_kernel(p, q, c_idx, "grad_add_" + nm) for nm, p, q in zip(names, parts, from_sibling)]
    scattered = _chip_scatter(chip_sums, "grad_to_chips")
    totals = [_sum_chips_kernel(s, c_idx, "grad_sum_" + nm) for nm, s in zip(names, scattered)]
    shared = _sibling_share(totals, "grad_share")
    big_out = {}
    for nm, g in zip(names, shared):
        w, m, v, _ = big[nm]
        g = g.reshape(w.shape[1:])
        d, mn, vn = _adam_kernel(w[0], g, m[0], v[0], "adam_" + nm)
        big_out[nm] = tuple(t[None] for t in (g, d, mn, vn))

    dmod = jnp.concatenate([sums1[0:1], sums1[1:2], sums2[5:6], sums2[0:1], sums2[1:2], sums2[2:3]], axis=1)
    db_spatial = db_lanes[:, 0:GMLP_GROUPS].T
    small_grads = {"b_ada": dmod, "g_mix": sums1[2], "g_ffn": sums2[3], "g_final": sums2[4],
                   "b_spatial": db_spatial, "sinks": dsink_rows[:, 0], "loss": sums2[6, 0:1],
                   "w_spatial": dw_spatial}
    small_w = {"b_ada": b_ada, "g_mix": g_mix, "g_ffn": g_ffn, "g_final": g_final, "b_spatial": b_spatial,
               "sinks": sinks, "w_spatial": w_spatial}
    small_m = {"b_ada": m_b_ada, "g_mix": m_g_mix, "g_ffn": m_g_ffn, "g_final": m_g_final,
               "b_spatial": m_b_spatial, "sinks": m_sinks, "w_spatial": m_w_spatial}
    small_v = {"b_ada": v_b_ada, "g_mix": v_g_mix, "g_ffn": v_g_ffn, "g_final": v_g_final,
               "b_spatial": v_b_spatial, "sinks": v_sinks, "w_spatial": v_w_spatial}
    gathered = _all_gather8(_pack_small(small_grads), "gather_small")
    packed = _small_update_kernel(gathered, _pack_small(small_w), _pack_small(small_m), _pack_small(small_v))
    shapes = {k: a.shape for k, a in small_w.items()}
    sg, sd, sm, sv = (_unpack_small(p, shapes) for p in packed)
    loss = packed[0][sum(nr for nm, nr in _SMALL_ROWS[:6]), 0]

    dmod_all = gathered[:, 0:N_MOD * D_MODEL // LANES, :].reshape(N_DEV, N_MOD * D_MODEL)
    dmod_cols = lax.dynamic_slice(dmod_all, (0, chip * ada_cols), (N_DEV, ada_cols))
    ada = _ada_update_kernel(act.T, dmod_cols, w_ada[0], m_w_ada[0], v_w_ada[0])
    big_out["w_ada"] = tuple(t[None] for t in ada)

    order = ["w_ada", "b_ada", "g_mix", "w_in", "w_spatial", "b_spatial", "sinks", "w_out", "g_ffn",
             "w_ff1", "w_ff2", "g_final"]

    def leaf(nm, k):
        return big_out[nm][k] if nm in big_out else (sg, sd, sm, sv)[k][nm]

    outs = [loss, grad_x[None]]
    for k in range(4):
        outs += [leaf(nm, k) for nm in order]
    return tuple(outs)
```

```python
import math
from typing import Callable, NamedTuple

import jax
import jax.numpy as jnp
from jax import lax
from jax.experimental import pallas as pl
from jax.experimental.pallas import tpu as pltpu

F32 = jnp.float32
MXU_DTYPE = jnp.bfloat16
WEIGHT_COMM_DTYPE = jnp.bfloat16
GRAD_COMM_DTYPE = jnp.bfloat16

D_MODEL = 1024
D_FF = 4096
HEAD_DIM = 64
GMLP_GROUPS = 8
GMLP_WIDTH = 512
CHUNK = 128
N_Q_HEADS = 8
N_KV_HEADS = 2
ATTN_WIDTH = 512
KV_WIDTH = 128
ROT_DIM = 16
ROPE_THETA = 500000.0
IN_PROJ_WIDTH = 1792
N_MOD = 6
EPS = 1e-5
N_CHIPS = 4
N_DEV = 8
LANES = 128

ADAM_LR = 0.001
ADAM_B1 = 0.9
ADAM_B2 = 0.999
ADAM_EPS = 1e-08
ADAM_WD = 0.01
ADAM_STEP = 10

VMEM_LIMIT_BYTES = 58 * 1024 * 1024
MESH = pl.DeviceIdType.MESH


def _params(*semantics):
    return pltpu.CompilerParams(dimension_semantics=semantics, vmem_limit_bytes=VMEM_LIMIT_BYTES)


def _dot(a, b):
    return jnp.dot(a.astype(MXU_DTYPE), b.astype(MXU_DTYPE), preferred_element_type=F32)


def _dot_nt(a, b):
    return lax.dot_general(a.astype(MXU_DTYPE), b.astype(MXU_DTYPE), (((1,), (1,)), ((), ())),
                           preferred_element_type=F32)


def _dot_tn(a, b):
    return lax.dot_general(a.astype(MXU_DTYPE), b.astype(MXU_DTYPE), (((0,), (0,)), ((), ())),
                           preferred_element_type=F32)


def _full(shape):
    return pl.BlockSpec(shape, lambda *_: (0,) * len(shape))


def _any():
    return pl.BlockSpec(memory_space=pl.ANY)


def _rowsum(v):
    return jnp.sum(v, axis=0, keepdims=True)


def _mean_last(v):
    return jnp.mean(v, axis=-1, keepdims=True)


class _Comm(NamedTuple):
    operands: tuple
    out_shapes: tuple
    n_sems: int
    make: Callable


def _hosted_call(body, comm, *, name, grid, in_specs, out_shape, out_specs, scratch_shapes=(), semantics):
    if comm is None:
        return pl.pallas_call(body, name=name, grid=grid, in_specs=in_specs, out_shape=out_shape,
                              out_specs=out_specs, scratch_shapes=list(scratch_shapes),
                              compiler_params=_params(*semantics))
    n_in, n_out, n_scr = len(in_specs), len(out_shape), len(scratch_shapes)
    k_in, k_out = len(comm.operands), len(comm.out_shapes)
    last = grid[0] - 1

    def hosted(*refs):
        ins, refs = refs[:n_in], refs[n_in:]
        c_ins, refs = refs[:k_in], refs[k_in:]
        outs, refs = refs[:n_out], refs[n_out:]
        c_outs, refs = refs[:k_out], refs[k_out:]
        scratch, (send_sems, recv_sems) = refs[:n_scr], refs[n_scr:]
        step = pl.program_id(0)

        @pl.when(step == 0)
        def _():
            for cp in comm.make(c_ins, c_outs, send_sems, recv_sems)[0]:
                cp.start()

        body(*ins, *outs, *scratch)

        @pl.when(step == last)
        def _():
            for wait in comm.make(c_ins, c_outs, send_sems, recv_sems)[1]:
                wait()

    call = pl.pallas_call(
        hosted, name=name, grid=grid, in_specs=list(in_specs) + [_any()] * k_in,
        out_shape=list(out_shape) + list(comm.out_shapes), out_specs=list(out_specs) + [_any()] * k_out,
        scratch_shapes=list(scratch_shapes) + [pltpu.SemaphoreType.DMA((comm.n_sems,)),
                                                pltpu.SemaphoreType.DMA((comm.n_sems,))],
        compiler_params=_params(*semantics))
    return lambda *args: call(*args, *comm.operands)


def _mesh_place():
    x, y, c = lax.axis_index("x"), lax.axis_index("y"), lax.axis_index("c")
    return x, y, c, [(1 - x, y), (x, 1 - y), (1 - x, 1 - y)]


def _gather_job(halves):
    per = 5

    def make(ins, outs, send_sems, recv_sems):
        x, y, c, chips = _mesh_place()
        starts, waits = [], []
        for a, (src, out) in enumerate(zip(ins, outs)):
            mine = out.at[4 * x + 2 * y + c]
            local = pltpu.make_async_copy(src.at[c], mine, send_sems.at[per * a + 4])
            to = [(x, y, 1 - c)] + [(px, py, c) for px, py in chips]
            sends = [pltpu.make_async_remote_copy(
                src_ref=src.at[c], dst_ref=mine, send_sem=send_sems.at[per * a + k],
                recv_sem=recv_sems.at[per * a + k], device_id=dev, device_id_type=MESH)
                for k, dev in enumerate(to)]
            recvs = [pltpu.make_async_remote_copy(
                src_ref=src.at[c], dst_ref=out.at[4 * px + 2 * py + pc], send_sem=send_sems.at[per * a + k],
                recv_sem=recv_sems.at[per * a + k], device_id=(px, py, pc), device_id_type=MESH)
                for k, (px, py, pc) in enumerate(to)]
            starts += [local] + sends
            waits += [local.wait] + [s.wait_send for s in sends] + [r.wait_recv for r in recvs]
        return starts, waits

    return _Comm(tuple(halves), tuple(jax.ShapeDtypeStruct((N_DEV,) + h.shape[1:], h.dtype) for h in halves),
                 per * len(halves), make)


def _gather_forward(bufs, name):
    n_arr = len(bufs)

    def body(*refs):
        outs = refs[n_arr:2 * n_arr]
        send_sems, recv_sems = refs[2 * n_arr:]
        x, y, c, chips = _mesh_place()
        sends, recvs = [], []
        for a, buf in enumerate(outs):
            for j, (px, py) in enumerate(chips):
                mine, theirs = buf.at[4 * px + 2 * py + c], buf.at[4 * px + 2 * py + 1 - c]
                sems = dict(send_sem=send_sems.at[3 * a + j], recv_sem=recv_sems.at[3 * a + j],
                            device_id=(x, y, 1 - c), device_id_type=MESH)
                sends.append(pltpu.make_async_remote_copy(src_ref=mine, dst_ref=mine, **sems))
                recvs.append(pltpu.make_async_remote_copy(src_ref=mine, dst_ref=theirs, **sems))
        for cp in sends:
            cp.start()
        for s, r in zip(sends, recvs):
            s.wait_send()
            r.wait_recv()

    return pl.pallas_call(
        body, name=name, out_shape=[jax.ShapeDtypeStruct(b.shape, b.dtype) for b in bufs],
        in_specs=[_any()] * n_arr, out_specs=[_any()] * n_arr,
        input_output_aliases={a: a for a in range(n_arr)},
        scratch_shapes=[pltpu.SemaphoreType.DMA((3 * n_arr,)), pltpu.SemaphoreType.DMA((3 * n_arr,))],
    )(*bufs)


def _scatter_job(chip_sums):
    def make(ins, outs, send_sems, recv_sems):
        x, y, c, chips = _mesh_place()
        copies = [pltpu.make_async_remote_copy(
            src_ref=src.at[2 * px + py], dst_ref=out.at[j], send_sem=send_sems.at[3 * a + j],
            recv_sem=recv_sems.at[3 * a + j], device_id=(px, py, c), device_id_type=MESH)
            for a, (src, out) in enumerate(zip(ins, outs)) for j, (px, py) in enumerate(chips)]
        return copies, [cp.wait for cp in copies]

    return _Comm(tuple(chip_sums), tuple(jax.ShapeDtypeStruct((3,) + s.shape[1:], s.dtype) for s in chip_sums),
                 3 * len(chip_sums), make)


def _run_comm(comm, name):
    def body(token_ref):
        token_ref[...] = jnp.zeros_like(token_ref)

    out = _hosted_call(body, comm, name=name, grid=(1,), in_specs=[],
                       out_shape=[jax.ShapeDtypeStruct((8, LANES), F32)], out_specs=[_full((8, LANES))],
                       semantics=("arbitrary",))()
    return out[1:]


def _all_gather8(block, name, split=False):
    blk_shape = block.shape[1:] if split else block.shape

    def body(x_ref, out_ref, send_sems, recv_sems, local_sem):
        x, y, c = lax.axis_index("x"), lax.axis_index("y"), lax.axis_index("c")
        me, sibling = (x, y, c), (x, y, 1 - c)
        chips = [(1 - x, y), (x, 1 - y), (1 - x, 1 - y)]
        src_mine = x_ref.at[c] if split else x_ref

        def slot(px, py, pc):
            return out_ref.at[4 * px + 2 * py + pc]

        def copy(k, blk, to, src=None):
            return pltpu.make_async_remote_copy(
                src_ref=slot(*blk) if src is None else src, dst_ref=slot(*blk),
                send_sem=send_sems.at[k], recv_sem=recv_sems.at[k],
                device_id=to, device_id_type=MESH)

        mine = pltpu.make_async_copy(src_mine, slot(*me), local_sem)
        mine.start()
        first = [copy(0, me, sibling, src=src_mine)]
        first += [copy(1 + j, me, (*chip, c), src=src_mine) for j, chip in enumerate(chips)]
        for cp in first:
            cp.start()
        passed = [copy(4 + j, (*chip, c), sibling) for j, chip in enumerate(chips)]
        for j, chip in enumerate(chips):
            copy(1 + j, (*chip, c), me).wait_recv()
            passed[j].start()
        copy(0, sibling, me).wait_recv()
        for j, chip in enumerate(chips):
            copy(4 + j, (*chip, 1 - c), me).wait_recv()
        for cp in first + passed:
            cp.wait_send()
        mine.wait()

    return pl.pallas_call(
        body, name=name,
        out_shape=jax.ShapeDtypeStruct((N_DEV,) + tuple(blk_shape), block.dtype),
        in_specs=[_any()], out_specs=_any(),
        scratch_shapes=[pltpu.SemaphoreType.DMA((7,)), pltpu.SemaphoreType.DMA((7,)),
                        pltpu.SemaphoreType.DMA],
    )(block)


def _sibling_exchange_halves(parts, name):
    n_arr = len(parts)

    def body(*refs):
        in_refs = refs[:n_arr]
        out_refs = refs[n_arr:2 * n_arr]
        send_sems, recv_sems = refs[2 * n_arr:]
        x, y, c = lax.axis_index("x"), lax.axis_index("y"), lax.axis_index("c")
        copies = []
        for a in range(n_arr):
            for k in range(N_CHIPS):
                copies.append(pltpu.make_async_remote_copy(
                    src_ref=in_refs[a].at[k, 1 - c], dst_ref=out_refs[a].at[k],
                    send_sem=send_sems.at[a * N_CHIPS + k], recv_sem=recv_sems.at[a * N_CHIPS + k],
                    device_id=(x, y, 1 - c), device_id_type=MESH))
        for cp in copies:
            cp.start()
        for cp in copies:
            cp.wait()

    return pl.pallas_call(
        body, name=name,
        out_shape=[jax.ShapeDtypeStruct((N_CHIPS,) + p.shape[2:], p.dtype) for p in parts],
        in_specs=[_any()] * n_arr, out_specs=[_any()] * n_arr,
        scratch_shapes=[pltpu.SemaphoreType.DMA((n_arr * N_CHIPS,)),
                        pltpu.SemaphoreType.DMA((n_arr * N_CHIPS,))],
    )(*parts)


def _chip_scatter(halves, name):
    n_arr = len(halves)

    def body(*refs):
        in_refs = refs[:n_arr]
        out_refs = refs[n_arr:2 * n_arr]
        send_sems, recv_sems, local_sems = refs[2 * n_arr:]
        x, y, c = lax.axis_index("x"), lax.axis_index("y"), lax.axis_index("c")
        chips = [(1 - x, y), (x, 1 - y), (1 - x, 1 - y)]
        copies, local = [], []
        for a in range(n_arr):
            for j, (px, py) in enumerate(chips):
                copies.append(pltpu.make_async_remote_copy(
                    src_ref=in_refs[a].at[2 * px + py], dst_ref=out_refs[a].at[j],
                    send_sem=send_sems.at[a * 3 + j], recv_sem=recv_sems.at[a * 3 + j],
                    device_id=(px, py, c), device_id_type=MESH))
            local.append(pltpu.make_async_copy(in_refs[a].at[2 * x + y], out_refs[a].at[3],
                                               local_sems.at[a]))
        for cp in copies + local:
            cp.start()
        for cp in copies + local:
            cp.wait()

    return pl.pallas_call(
        body, name=name,
        out_shape=[jax.ShapeDtypeStruct(h.shape, h.dtype) for h in halves],
        in_specs=[_any()] * n_arr, out_specs=[_any()] * n_arr,
        scratch_shapes=[pltpu.SemaphoreType.DMA((n_arr * 3,)), pltpu.SemaphoreType.DMA((n_arr * 3,)),
                        pltpu.SemaphoreType.DMA((n_arr,))],
    )(*halves)


def _sibling_share(bufs, name):
    n_arr = len(bufs)

    def body(*refs):
        out_refs = refs[n_arr:2 * n_arr]
        send_sems, recv_sems = refs[2 * n_arr:]
        x, y, c = lax.axis_index("x"), lax.axis_index("y"), lax.axis_index("c")
        copies = [pltpu.make_async_remote_copy(
            src_ref=out_refs[a].at[c], dst_ref=out_refs[a].at[c],
            send_sem=send_sems.at[a], recv_sem=recv_sems.at[a],
            device_id=(x, y, 1 - c), device_id_type=MESH) for a in range(n_arr)]
        for cp in copies:
            cp.start()
        for a in range(n_arr):
            pltpu.make_async_remote_copy(
                src_ref=out_refs[a].at[c], dst_ref=out_refs[a].at[1 - c],
                send_sem=send_sems.at[a], recv_sem=recv_sems.at[a],
                device_id=(x, y, 1 - c), device_id_type=MESH).wait()

    return pl.pallas_call(
        body, name=name,
        out_shape=[jax.ShapeDtypeStruct(b.shape, b.dtype) for b in bufs],
        in_specs=[_any()] * n_arr, out_specs=[_any()] * n_arr,
        input_output_aliases={a: a for a in range(n_arr)},
        scratch_shapes=[pltpu.SemaphoreType.DMA((n_arr,)), pltpu.SemaphoreType.DMA((n_arr,))],
    )(*bufs)


def _gelu_tanh(z):
    k = math.sqrt(2.0 / math.pi)
    t = jnp.tanh(k * (z + 0.044715 * (z * z * z)))
    return 0.5 * z * (1.0 + t), t


def _gelu_tanh_grad(z, t):
    k = math.sqrt(2.0 / math.pi)
    return 0.5 * (1.0 + t) + 0.5 * z * (1.0 - t * t) * (k * (1.0 + 3.0 * 0.044715 * (z * z)))


def _rope_angle_kernel(pos_row, invf_col):
    seq = pos_row.shape[1]

    def body(p_ref, f_ref, cos_ref, sin_ref):
        ang = p_ref[...].astype(F32) * f_ref[...]
        cos_ref[...] = jnp.cos(ang)
        sin_ref[...] = jnp.sin(ang)

    return pl.pallas_call(
        body, name="rope_angles", grid=(1,), out_shape=[jax.ShapeDtypeStruct((ROT_DIM // 2, seq), F32)] * 2,
        in_specs=[_full((1, seq)), _full((ROT_DIM // 2, 1))], out_specs=[_full((ROT_DIM // 2, seq))] * 2,
        compiler_params=_params("arbitrary"),
    )(pos_row, invf_col)


def _rope_lane_tables(cos, sin):
    cos_t, sin_t = cos.T, sin.T
    seq, half = cos_t.shape
    ones = jnp.ones((seq, HEAD_DIM - ROT_DIM), F32)
    c64 = jnp.concatenate([cos_t, cos_t, ones], axis=1)
    s1 = jnp.concatenate([sin_t, jnp.zeros((seq, HEAD_DIM - half), F32)], axis=1)
    s2 = jnp.concatenate([jnp.zeros((seq, half), F32), sin_t, jnp.zeros((seq, HEAD_DIM - ROT_DIM), F32)], axis=1)
    return jnp.concatenate([jnp.tile(t, (1, LANES // HEAD_DIM)) for t in (c64, s1, s2)], axis=1)


def _rope_apply(t, tab, sign):
    reps = t.shape[1] // LANES
    c_tab, s1, s2 = (jnp.tile(tab[:, LANES * k:LANES * (k + 1)], (1, reps)) if reps > 1
                     else tab[:, LANES * k:LANES * (k + 1)] for k in range(3))
    half = ROT_DIM // 2
    up = pltpu.roll(t, t.shape[1] - half, 1)
    down = pltpu.roll(t, half, 1)
    return t * c_tab + sign * (down * s2 - up * s1)


def _lane_masks(shape):
    lane = lax.broadcasted_iota(jnp.int32, shape, 1)
    return lane < HEAD_DIM, lane >= HEAD_DIM


HEADS_PER_GROUP = N_Q_HEADS // N_KV_HEADS
ATTN_SCALE = 1.0 / math.sqrt(HEAD_DIM)


def _attn_bias_t(first_block):
    kj = lax.broadcasted_iota(jnp.int32, (2 * CHUNK, CHUNK), 0)
    qi = lax.broadcasted_iota(jnp.int32, (2 * CHUNK, CHUNK), 1)
    ok = (kj > qi) & (kj <= qi + CHUNK) & (jnp.logical_not(first_block) | (kj >= CHUNK))
    return jnp.tile(jnp.where(ok, 0.0, -jnp.inf), (1, HEADS_PER_GROUP))


def _group_rows(x, g, lo, hi):
    rows = []
    for r in range(HEADS_PER_GROUP):
        h = HEADS_PER_GROUP * g + r
        pair = x[:, LANES * (h // 2):LANES * (h // 2 + 1)]
        rows.append(jnp.where(hi if h % 2 else lo, pair, 0.0))
    return jnp.concatenate(rows, axis=0)


def _pairs_from_rows(rows, lo):
    return [jnp.where(lo, rows[2 * CHUNK * k:2 * CHUNK * k + CHUNK], rows[2 * CHUNK * k + CHUNK:2 * CHUNK * (k + 1)])
            for k in range(HEADS_PER_GROUP // 2)]


def _group_dup(a, b, g, lo2):
    return jnp.where(lo2, a, b) if g == 0 else jnp.where(lo2, b, a)


def _sink_row(sink_ref, g):
    return jnp.concatenate([sink_ref[HEADS_PER_GROUP * g + r:HEADS_PER_GROUP * g + r + 1, :]
                            for r in range(HEADS_PER_GROUP)], axis=1)


def _attn_probs_t(k_dup, q_rows, bias_t, sink_row):
    s_t = _dot_nt(k_dup, q_rows) * ATTN_SCALE + bias_t
    m = jnp.maximum(jnp.max(s_t, axis=0, keepdims=True), sink_row)
    p = jnp.exp(s_t - m)
    e_sink = jnp.exp(sink_row - m)
    inv = 1.0 / (jnp.sum(p, axis=0, keepdims=True) + e_sink)
    return p * inv, e_sink * inv


def _sgu_forward_pair(wm, vp, j):
    lo, hi = _lane_masks(vp.shape)
    lhs = jnp.concatenate([wm[2 * j], wm[2 * j + 1]], axis=1)
    rhs = jnp.concatenate([jnp.where(lo, vp, 0.0), jnp.where(hi, vp, 0.0)], axis=0)
    return _dot(lhs, rhs)


def _masked_spatial(w_ref):
    t = lax.broadcasted_iota(jnp.int32, (CHUNK, CHUNK), 0)
    s = lax.broadcasted_iota(jnp.int32, (CHUNK, CHUNK), 1)
    tril = s <= t
    return [jnp.where(tril, w_ref[g], 0.0) for g in range(GMLP_GROUPS)], tril, s >= t


def _mod_kernel(c_all, w_shard, b_shard):
    n = w_shard.shape[1]
    tn = 512

    def body(c_ref, w_ref, b_ref, mod_ref, act_ref):
        cv = c_ref[...]
        act = cv * (1.0 / (1.0 + jnp.exp(-cv)))
        act_ref[...] = act
        mod_ref[...] = _dot(act, w_ref[...]) + b_ref[...]

    return pl.pallas_call(
        body, name="ada_mod", grid=(n // tn,),
        out_shape=[jax.ShapeDtypeStruct((N_DEV, n), F32), jax.ShapeDtypeStruct((N_DEV, D_MODEL), F32)],
        in_specs=[_full((N_DEV, D_MODEL)), pl.BlockSpec((D_MODEL, tn), lambda i: (0, i)),
                  pl.BlockSpec((1, tn), lambda i: (0, i))],
        out_specs=[pl.BlockSpec((N_DEV, tn), lambda i: (0, i)), _full((N_DEV, D_MODEL))],
        compiler_params=_params("arbitrary"),
    )(c_all, w_shard, b_shard)


def _in_proj_kernel(x, vecs, w_in, comm=None):
    seq = x.shape[0]
    tm = 512

    def body(x_ref, v_ref, w_ref, proj_ref, h_ref):
        xv = x_ref[...]
        rstd = lax.rsqrt(_mean_last(xv * xv) + EPS)
        n1 = (xv * rstd) * v_ref[0:1, :]
        h = n1 * (1.0 + v_ref[2:3, :]) + v_ref[1:2, :]
        hb = h.astype(MXU_DTYPE)
        h_ref[...] = hb
        proj_ref[...] = _dot(hb, w_ref[...])

    return _hosted_call(
        body, comm, name="in_proj", grid=(seq // tm,),
        out_shape=[jax.ShapeDtypeStruct((seq, IN_PROJ_WIDTH), F32),
                   jax.ShapeDtypeStruct((seq, D_MODEL), MXU_DTYPE)],
        in_specs=[pl.BlockSpec((tm, D_MODEL), lambda i: (i, 0)), _full((8, D_MODEL)),
                  _full((D_MODEL, IN_PROJ_WIDTH))],
        out_specs=[pl.BlockSpec((tm, IN_PROJ_WIDTH), lambda i: (i, 0)),
                   pl.BlockSpec((tm, D_MODEL), lambda i: (i, 0))],
        semantics=("arbitrary",),
    )(x, vecs, w_in)


def _mixer_fwd_kernel(proj, rope_tab, w_spatial, bias_full, sink_rows, comm=None):
    seq = proj.shape[0]
    nb = seq // CHUNK
    kv_col = (2 * GMLP_WIDTH + ATTN_WIDTH) // (2 * KV_WIDTH)

    def body(proj_ref, prev_ref, tab_ref, ptab_ref, w_ref, bias_ref, sink_ref, cat_ref):
        i = pl.program_id(0)
        wm, _, _ = _masked_spatial(w_ref)
        for j in range(GMLP_GROUPS // 2):
            cols = slice(LANES * j, LANES * (j + 1))
            vcols = slice(GMLP_WIDTH + LANES * j, GMLP_WIDTH + LANES * (j + 1))
            u, _ = _gelu_tanh(proj_ref[:, cols])
            vp, _ = _gelu_tanh(proj_ref[:, vcols])
            sv = _sgu_forward_pair(wm, vp, j) + bias_ref[:, cols]
            cat_ref[:, cols] = (u * sv).astype(cat_ref.dtype)
        o = 2 * GMLP_WIDTH
        tab = tab_ref[...]
        q_r = _rope_apply(proj_ref[:, o:o + ATTN_WIDTH], tab, 1.0)
        k_cur = _rope_apply(proj_ref[:, o + ATTN_WIDTH:o + ATTN_WIDTH + KV_WIDTH], tab, 1.0)
        k_prev = _rope_apply(prev_ref[:, 0:KV_WIDTH], ptab_ref[...], 1.0)
        k_a = jnp.concatenate([k_prev, k_cur], axis=0)
        v_a = jnp.concatenate([prev_ref[:, KV_WIDTH:2 * KV_WIDTH],
                               proj_ref[:, o + ATTN_WIDTH + KV_WIDTH:o + ATTN_WIDTH + 2 * KV_WIDTH]], axis=0)
        k_b = pltpu.roll(k_a, HEAD_DIM, 1)
        v_b = pltpu.roll(v_a, HEAD_DIM, 1)
        bias_t = _attn_bias_t(i == 0)
        lo, hi = _lane_masks((CHUNK, LANES))
        lo2, _ = _lane_masks((2 * CHUNK, LANES))
        for g in range(N_KV_HEADS):
            p_t, _ = _attn_probs_t(_group_dup(k_a, k_b, g, lo2), _group_rows(q_r, g, lo, hi), bias_t,
                                   _sink_row(sink_ref, g))
            o_t = _dot(_group_dup(v_a, v_b, g, lo2).T, p_t)
            for k, pair in enumerate(_pairs_from_rows(o_t.T, lo)):
                c0 = GMLP_WIDTH + LANES * (2 * g + k)
                cat_ref[:, c0:c0 + LANES] = pair.astype(cat_ref.dtype)

    return _hosted_call(
        body, comm, name="mixer_fwd", grid=(nb,),
        out_shape=[jax.ShapeDtypeStruct((seq, D_MODEL), MXU_DTYPE)],
        in_specs=[pl.BlockSpec((CHUNK, IN_PROJ_WIDTH), lambda i: (i, 0)),
                  pl.BlockSpec((CHUNK, 2 * KV_WIDTH), lambda i: (jnp.maximum(i - 1, 0), kv_col)),
                  pl.BlockSpec((CHUNK, 3 * LANES), lambda i: (i, 0)),
                  pl.BlockSpec((CHUNK, 3 * LANES), lambda i: (jnp.maximum(i - 1, 0), 0)),
                  _full((GMLP_GROUPS, CHUNK, CHUNK)), _full((CHUNK, GMLP_WIDTH)),
                  _full((N_Q_HEADS, LANES))],
        out_specs=[pl.BlockSpec((CHUNK, D_MODEL), lambda i: (i, 0))],
        semantics=("arbitrary",),
    )(proj, proj, rope_tab, rope_tab, w_spatial, bias_full, sink_rows)


def _trunk_kernel(x, target, cat, vecs, w_out, w_ff1, w_ff2):
    seq = x.shape[0]
    tm = 256
    nj = D_FF // D_MODEL

    def body(x_ref, t_ref, cat_ref, v_ref, wout_hbm, w1_hbm, w2_hbm,
             dx1_ref, dcat_ref, dmix_ref, h2_ref, r_ref, da_ref, dff_ref, sums_ref,
             wout, w1, w2, a_scr, sem):
        i = pl.program_id(0)

        @pl.when(i == 0)
        def _():
            copies = [pltpu.make_async_copy(wout_hbm, wout, sem.at[0]),
                      pltpu.make_async_copy(w1_hbm, w1, sem.at[1]),
                      pltpu.make_async_copy(w2_hbm, w2, sem.at[2])]
            for cp in copies:
                cp.start()
            for cp in copies:
                cp.wait()
            sums_ref[...] = jnp.zeros_like(sums_ref)

        gate1, shift2, scale2 = v_ref[0:1, :], v_ref[1:2, :], v_ref[2:3, :]
        gate2, g_ffn, g_final = v_ref[3:4, :], v_ref[4:5, :], v_ref[5:6, :]

        mix = _dot(cat_ref[...], wout[...])
        x1 = x_ref[...] + gate1 * mix
        rstd2 = lax.rsqrt(_mean_last(x1 * x1) + EPS)
        xh2 = x1 * rstd2
        n2 = xh2 * g_ffn
        h2b = (n2 * (1.0 + scale2) + shift2).astype(MXU_DTYPE)
        h2_ref[...] = h2b
        ff = jnp.zeros((tm, D_MODEL), F32)
        for j in range(nj):
            a = _dot(h2b, w1[j])
            a_scr[j] = a
            relu = jnp.maximum(a, 0.0)
            rb = (relu * relu).astype(MXU_DTYPE)
            r_ref[:, D_MODEL * j:D_MODEL * (j + 1)] = rb
            ff = ff + _dot(rb, w2[j])
        x2 = x1 + gate2 * ff
        rstd3 = lax.rsqrt(_mean_last(x2 * x2) + EPS)
        xh3 = x2 * rstd3
        err = xh3 * g_final - t_ref[...]
        loss = 0.5 * _rowsum(_mean_last(err * err))
        dy = err * (1.0 / D_MODEL)
        dxh3 = dy * g_final
        dx2 = rstd3 * (dxh3 - xh3 * _mean_last(dxh3 * xh3))
        dffb = (dx2 * gate2).astype(MXU_DTYPE)
        dff_ref[...] = dffb
        dh2 = jnp.zeros((tm, D_MODEL), F32)
        for j in range(nj):
            dr = _dot_nt(dffb, w2[j])
            dab = (dr * (2.0 * jnp.maximum(a_scr[j], 0.0))).astype(MXU_DTYPE)
            da_ref[:, D_MODEL * j:D_MODEL * (j + 1)] = dab
            dh2 = dh2 + _dot_nt(dab, w1[j])
        dn2 = dh2 * (1.0 + scale2)
        dxh2 = dn2 * g_ffn
        dx1 = dx2 + rstd2 * (dxh2 - xh2 * _mean_last(dxh2 * xh2))
        dx1_ref[...] = dx1
        dmixb = (dx1 * gate1).astype(MXU_DTYPE)
        dmix_ref[...] = dmixb
        dcat_ref[...] = _dot_nt(dmixb, wout[...])

        sums_ref[0:1, :] += _rowsum(dh2)
        sums_ref[1:2, :] += _rowsum(dh2 * n2)
        sums_ref[2:3, :] += _rowsum(dx2 * ff)
        sums_ref[3:4, :] += _rowsum(dn2 * xh2)
        sums_ref[4:5, :] += _rowsum(dy * xh3)
        sums_ref[5:6, :] += _rowsum(dx1 * mix)
        sums_ref[6:7, :] += jnp.broadcast_to(loss, (1, D_MODEL))

    tok = lambda w: pl.BlockSpec((tm, w), lambda i: (i, 0))
    return pl.pallas_call(
        body, name="trunk", grid=(seq // tm,),
        out_shape=[jax.ShapeDtypeStruct((seq, D_MODEL), F32), jax.ShapeDtypeStruct((seq, D_MODEL), F32),
                   jax.ShapeDtypeStruct((seq, D_MODEL), MXU_DTYPE), jax.ShapeDtypeStruct((seq, D_MODEL), MXU_DTYPE),
                   jax.ShapeDtypeStruct((seq, D_FF), MXU_DTYPE), jax.ShapeDtypeStruct((seq, D_FF), MXU_DTYPE),
                   jax.ShapeDtypeStruct((seq, D_MODEL), MXU_DTYPE), jax.ShapeDtypeStruct((8, D_MODEL), F32)],
        in_specs=[tok(D_MODEL), tok(D_MODEL), tok(D_MODEL), _full((8, D_MODEL)), _any(), _any(), _any()],
        out_specs=[tok(D_MODEL), tok(D_MODEL), tok(D_MODEL), tok(D_MODEL), tok(D_FF), tok(D_FF), tok(D_MODEL),
                   _full((8, D_MODEL))],
        scratch_shapes=[pltpu.VMEM((D_MODEL, D_MODEL), MXU_DTYPE), pltpu.VMEM((nj, D_MODEL, D_MODEL), MXU_DTYPE),
                        pltpu.VMEM((nj, D_MODEL, D_MODEL), MXU_DTYPE), pltpu.VMEM((nj, tm, D_MODEL), F32),
                        pltpu.SemaphoreType.DMA((3,))],
        compiler_params=_params("arbitrary"),
    )(x, target, cat, vecs, w_out, w_ff1, w_ff2)


def _mixer_bwd_kernel(proj, rope_tab, dcat, w_spatial, w_spatial_t, bias_full, sink_rows, comm=None):
    seq = proj.shape[0]
    nb = seq // CHUNK
    kv_col = (2 * GMLP_WIDTH + ATTN_WIDTH) // (2 * KV_WIDTH)

    def body(proj_ref, prev_ref, tab_ref, ptab_ref, dcat_ref, w_ref, wt_ref, bias_ref, sink_ref,
             dproj_ref, dw_ref, db_ref, dsink_ref, carry):
        step = pl.program_id(0)
        blk = nb - 1 - step

        @pl.when(step == 0)
        def _():
            carry[...] = jnp.zeros_like(carry)
            dw_ref[...] = jnp.zeros_like(dw_ref)
            db_ref[...] = jnp.zeros_like(db_ref)
            dsink_ref[...] = jnp.zeros_like(dsink_ref)

        wm, tril, triu = _masked_spatial(w_ref)
        lo, hi = _lane_masks((CHUNK, LANES))
        lane = lax.broadcasted_iota(jnp.int32, (CHUNK, LANES), 1)
        db = jnp.zeros((CHUNK, LANES), F32)
        for j in range(GMLP_GROUPS // 2):
            cols = slice(LANES * j, LANES * (j + 1))
            vcols = slice(GMLP_WIDTH + LANES * j, GMLP_WIDTH + LANES * (j + 1))
            zu, zv = proj_ref[:, cols], proj_ref[:, vcols]
            u, tu = _gelu_tanh(zu)
            vp, tv = _gelu_tanh(zv)
            sv = _sgu_forward_pair(wm, vp, j) + bias_ref[:, cols]
            dout = dcat_ref[:, cols]
            du = dout * sv
            dsv = dout * u
            dsv_lo, dsv_hi = jnp.where(lo, dsv, 0.0), jnp.where(hi, dsv, 0.0)
            lhs_t = jnp.concatenate([jnp.where(triu, wt_ref[2 * j], 0.0),
                                     jnp.where(triu, wt_ref[2 * j + 1], 0.0)], axis=1)
            dv = _dot(lhs_t, jnp.concatenate([dsv_lo, dsv_hi], axis=0))
            dw_ref[2 * j] += jnp.where(tril, _dot_nt(dsv_lo, vp), 0.0)
            dw_ref[2 * j + 1] += jnp.where(tril, _dot_nt(dsv_hi, vp), 0.0)
            db = db + (jnp.where(lane == 2 * j, jnp.sum(dsv_lo, axis=1, keepdims=True), 0.0)
                       + jnp.where(lane == 2 * j + 1, jnp.sum(dsv_hi, axis=1, keepdims=True), 0.0))
            dproj_ref[:, cols] = (du * _gelu_tanh_grad(zu, tu)).astype(dproj_ref.dtype)
            dproj_ref[:, vcols] = (dv * _gelu_tanh_grad(zv, tv)).astype(dproj_ref.dtype)
        db_ref[...] += db
        o = 2 * GMLP_WIDTH
        tab = tab_ref[...]
        q_r = _rope_apply(proj_ref[:, o:o + ATTN_WIDTH], tab, 1.0)
        k_cur = _rope_apply(proj_ref[:, o + ATTN_WIDTH:o + ATTN_WIDTH + KV_WIDTH], tab, 1.0)
        k_prev = _rope_apply(prev_ref[:, 0:KV_WIDTH], ptab_ref[...], 1.0)
        k_a = jnp.concatenate([k_prev, k_cur], axis=0)
        v_a = jnp.concatenate([prev_ref[:, KV_WIDTH:2 * KV_WIDTH],
                               proj_ref[:, o + ATTN_WIDTH + KV_WIDTH:o + ATTN_WIDTH + 2 * KV_WIDTH]], axis=0)
        k_b = pltpu.roll(k_a, HEAD_DIM, 1)
        v_b = pltpu.roll(v_a, HEAD_DIM, 1)
        bias_t = _attn_bias_t(blk == 0)
        lo2, _ = _lane_masks((2 * CHUNK, LANES))
        dout_b = dcat_ref[:, GMLP_WIDTH:GMLP_WIDTH + ATTN_WIDTH]
        dk_tot, dv_tot, dq_pairs = [], [], []
        for g in range(N_KV_HEADS):
            k_dup, v_dup = _group_dup(k_a, k_b, g, lo2), _group_dup(v_a, v_b, g, lo2)
            q_rows = _group_rows(q_r, g, lo, hi)
            do_rows = _group_rows(dout_b, g, lo, hi)
            p_t, p_sink = _attn_probs_t(k_dup, q_rows, bias_t, _sink_row(sink_ref, g))
            dp_t = _dot_nt(v_dup, do_rows)
            delta = jnp.sum(p_t * dp_t, axis=0, keepdims=True)
            ds_t = p_t * (dp_t - delta) * ATTN_SCALE
            dsink = -p_sink * delta
            for r in range(HEADS_PER_GROUP):
                h = HEADS_PER_GROUP * g + r
                dsink_ref[h:h + 1, :] += jnp.broadcast_to(
                    jnp.sum(dsink[:, LANES * r:LANES * (r + 1)], axis=1, keepdims=True), (1, LANES))
            dk_full = _dot(ds_t, q_rows)
            dv_full = _dot(p_t, do_rows)
            dk_tot.append(dk_full + pltpu.roll(dk_full, HEAD_DIM, 1))
            dv_tot.append(dv_full + pltpu.roll(dv_full, HEAD_DIM, 1))
            dq_t = _dot(k_dup.T, ds_t)
            dq_pairs += _pairs_from_rows(dq_t.T, lo)
        dk_all = jnp.where(lo2, dk_tot[0], dk_tot[1])
        dv_all = jnp.where(lo2, dv_tot[0], dv_tot[1])
        dk_cur = dk_all[CHUNK:, :] + carry[:, 0:KV_WIDTH]
        dv_cur = dv_all[CHUNK:, :] + carry[:, KV_WIDTH:2 * KV_WIDTH]
        carry[:, 0:KV_WIDTH] = dk_all[:CHUNK, :]
        carry[:, KV_WIDTH:2 * KV_WIDTH] = dv_all[:CHUNK, :]
        dq = _rope_apply(jnp.concatenate(dq_pairs, axis=1), tab, -1.0)
        dproj_ref[:, o:o + ATTN_WIDTH] = dq.astype(dproj_ref.dtype)
        dproj_ref[:, o + ATTN_WIDTH:o + ATTN_WIDTH + KV_WIDTH] = (
            _rope_apply(dk_cur, tab, -1.0).astype(dproj_ref.dtype))
        dproj_ref[:, o + ATTN_WIDTH + KV_WIDTH:o + ATTN_WIDTH + 2 * KV_WIDTH] = dv_cur.astype(dproj_ref.dtype)

    rev = lambda i: nb - 1 - i
    return _hosted_call(
        body, comm, name="mixer_bwd", grid=(nb,),
        out_shape=[jax.ShapeDtypeStruct((seq, IN_PROJ_WIDTH), MXU_DTYPE),
                   jax.ShapeDtypeStruct((GMLP_GROUPS, CHUNK, CHUNK), F32),
                   jax.ShapeDtypeStruct((CHUNK, LANES), F32),
                   jax.ShapeDtypeStruct((N_Q_HEADS, LANES), F32)],
        in_specs=[pl.BlockSpec((CHUNK, IN_PROJ_WIDTH), lambda i: (rev(i), 0)),
                  pl.BlockSpec((CHUNK, 2 * KV_WIDTH), lambda i: (jnp.maximum(rev(i) - 1, 0), kv_col)),
                  pl.BlockSpec((CHUNK, 3 * LANES), lambda i: (rev(i), 0)),
                  pl.BlockSpec((CHUNK, 3 * LANES), lambda i: (jnp.maximum(rev(i) - 1, 0), 0)),
                  pl.BlockSpec((CHUNK, D_MODEL), lambda i: (rev(i), 0)),
                  _full((GMLP_GROUPS, CHUNK, CHUNK)), _full((GMLP_GROUPS, CHUNK, CHUNK)),
                  _full((CHUNK, GMLP_WIDTH)), _full((N_Q_HEADS, LANES))],
        out_specs=[pl.BlockSpec((CHUNK, IN_PROJ_WIDTH), lambda i: (rev(i), 0)),
                   _full((GMLP_GROUPS, CHUNK, CHUNK)), _full((CHUNK, LANES)), _full((N_Q_HEADS, LANES))],
        scratch_shapes=[pltpu.VMEM((CHUNK, 2 * KV_WIDTH), F32)],
        semantics=("arbitrary",),
    )(proj, proj, rope_tab, rope_tab, dcat, w_spatial, w_spatial_t, bias_full, sink_rows)


def _in_proj_bwd_kernel(x, dx1, dproj, vecs, w_in, comm=None):
    seq = x.shape[0]
    tm = 512

    def body(x_ref, dx1_ref, dp_ref, v_ref, w_ref, gx_ref, sums_ref):
        @pl.when(pl.program_id(0) == 0)
        def _():
            sums_ref[...] = jnp.zeros_like(sums_ref)

        g_mix, scale1 = v_ref[0:1, :], v_ref[2:3, :]
        dh = _dot_nt(dp_ref[...], w_ref[...])
        xv = x_ref[...]
        rstd = lax.rsqrt(_mean_last(xv * xv) + EPS)
        xh = xv * rstd
        dn1 = dh * (1.0 + scale1)
        dxh = dn1 * g_mix
        gx_ref[...] = dx1_ref[...] + rstd * (dxh - xh * _mean_last(dxh * xh))
        sums_ref[0:1, :] += _rowsum(dh)
        sums_ref[1:2, :] += _rowsum(dh * (xh * g_mix))
        sums_ref[2:3, :] += _rowsum(dn1 * xh)

    return _hosted_call(
        body, comm, name="in_proj_bwd", grid=(seq // tm,),
        out_shape=[jax.ShapeDtypeStruct((seq, D_MODEL), F32), jax.ShapeDtypeStruct((8, D_MODEL), F32)],
        in_specs=[pl.BlockSpec((tm, D_MODEL), lambda i: (i, 0)), pl.BlockSpec((tm, D_MODEL), lambda i: (i, 0)),
                  pl.BlockSpec((tm, IN_PROJ_WIDTH), lambda i: (i, 0)), _full((8, D_MODEL)),
                  _full((D_MODEL, IN_PROJ_WIDTH))],
        out_specs=[pl.BlockSpec((tm, D_MODEL), lambda i: (i, 0)), _full((8, D_MODEL))],
        semantics=("arbitrary",),
    )(x, dx1, dproj, vecs, w_in)


def _weight_grad_kernel(a, b, name, tm, tn, blocked_cols=False):
    seq, m = a.shape
    n = b.shape[1]
    tk = 512
    nk = seq // tk

    def body(a_ref, b_ref, o_ref, acc):
        k = pl.program_id(2)

        @pl.when(k == 0)
        def _():
            acc[...] = jnp.zeros_like(acc)

        acc[...] += _dot_tn(a_ref[...], b_ref[...])

        @pl.when(k == nk - 1)
        def _():
            o_ref[...] = acc[...]

    if blocked_cols:
        out_shape = jax.ShapeDtypeStruct((n // tn, m, tn), F32)
        out_spec = pl.BlockSpec((None, tm, tn), lambda i, j, k: (j, i, 0))
    else:
        out_shape = jax.ShapeDtypeStruct((m, n), F32)
        out_spec = pl.BlockSpec((tm, tn), lambda i, j, k: (i, j))
    return pl.pallas_call(
        body, name=name, grid=(m // tm, n // tn, nk), out_shape=out_shape,
        in_specs=[pl.BlockSpec((tk, tm), lambda i, j, k: (k, i)), pl.BlockSpec((tk, tn), lambda i, j, k: (k, j))],
        out_specs=out_spec, scratch_shapes=[pltpu.VMEM((tm, tn), F32)],
        compiler_params=_params("parallel", "parallel", "arbitrary"),
    )(a, b)


def _adam_update(w, g, m, v):
    m_new = ADAM_B1 * m + (1.0 - ADAM_B1) * g
    v_new = ADAM_B2 * v + (1.0 - ADAM_B2) * (g * g)
    m_hat = m_new / (1.0 - ADAM_B1 ** ADAM_STEP)
    v_hat = v_new / (1.0 - ADAM_B2 ** ADAM_STEP)
    delta = -ADAM_LR * (m_hat / (jnp.sqrt(v_hat) + ADAM_EPS) + ADAM_WD * w)
    return delta, m_new, v_new


def _add_halves_kernel(part, recv, c_idx, name):
    _, _, r, n = part.shape
    tr = min(r, 256)

    def body(c_ref, p_ref, q_ref, o_ref):
        del c_ref
        o_ref[...] = (p_ref[...] + q_ref[...]).astype(o_ref.dtype)

    return pl.pallas_call(
        body, name=name, out_shape=jax.ShapeDtypeStruct((N_CHIPS, r, n), GRAD_COMM_DTYPE),
        grid_spec=pltpu.PrefetchScalarGridSpec(
            num_scalar_prefetch=1, grid=(N_CHIPS, r // tr),
            in_specs=[pl.BlockSpec((None, None, tr, n), lambda k, i, c: (k, c[0], i, 0)),
                      pl.BlockSpec((None, tr, n), lambda k, i, c: (k, i, 0))],
            out_specs=pl.BlockSpec((None, tr, n), lambda k, i, c: (k, i, 0))),
        compiler_params=_params("parallel", "parallel"),
    )(c_idx, part, recv)


def _sum_chips_kernel(own, others, place, name):
    _, r, n = own.shape
    tr = min(r, 256)

    def body(place_ref, own_ref, oth_ref, o_ref):
        del place_ref
        acc = own_ref[...].astype(F32)
        for k in range(N_CHIPS - 1):
            acc = acc + oth_ref[k].astype(F32)
        o_ref[...] = acc

    return pl.pallas_call(
        body, name=name, out_shape=jax.ShapeDtypeStruct((2, r, n), F32),
        grid_spec=pltpu.PrefetchScalarGridSpec(
            num_scalar_prefetch=1, grid=(r // tr,),
            in_specs=[pl.BlockSpec((None, tr, n), lambda i, p: (p[0], i, 0)),
                      pl.BlockSpec((N_CHIPS - 1, tr, n), lambda i, p: (0, i, 0))],
            out_specs=pl.BlockSpec((None, tr, n), lambda i, p: (p[1], i, 0))),
        compiler_params=_params("parallel"),
    )(place, own, others)


def _adam_kernel(w, g, m, v, name):
    r, n = w.shape
    tr = min(r, 256)

    def body(w_ref, g_ref, m_ref, v_ref, d_ref, mo_ref, vo_ref):
        d_ref[...], mo_ref[...], vo_ref[...] = _adam_update(w_ref[...], g_ref[...], m_ref[...], v_ref[...])

    spec = pl.BlockSpec((tr, n), lambda i: (i, 0))
    return pl.pallas_call(
        body, name=name, grid=(r // tr,), out_shape=[jax.ShapeDtypeStruct((r, n), F32)] * 3,
        in_specs=[spec] * 4, out_specs=[spec] * 3, compiler_params=_params("parallel"),
    )(w, g, m, v)


def _small_update_kernel(gathered, w, m, v):
    _, r, n = gathered.shape

    def body(g_ref, w_ref, m_ref, v_ref, gs_ref, d_ref, mo_ref, vo_ref):
        g = g_ref[0]
        for k in range(1, N_DEV):
            g = g + g_ref[k]
        gs_ref[...] = g
        d_ref[...], mo_ref[...], vo_ref[...] = _adam_update(w_ref[...], g, m_ref[...], v_ref[...])

    return pl.pallas_call(
        body, name="small_update", grid=(1,), out_shape=[jax.ShapeDtypeStruct((r, n), F32)] * 4,
        in_specs=[_full((N_DEV, r, n))] + [_full((r, n))] * 3, out_specs=[_full((r, n))] * 4,
        compiler_params=_params("arbitrary"),
    )(gathered, w, m, v)


def _ada_update_kernel(act_t, dmod, w, m, v):
    r, n = w.shape
    tr = 256

    def body(a_ref, d_ref, w_ref, m_ref, v_ref, g_ref, dl_ref, mo_ref, vo_ref):
        g = _dot(a_ref[...], d_ref[...])
        g_ref[...] = g
        dl_ref[...], mo_ref[...], vo_ref[...] = _adam_update(w_ref[...], g, m_ref[...], v_ref[...])

    spec = pl.BlockSpec((tr, n), lambda i: (i, 0))
    return pl.pallas_call(
        body, name="ada_update", grid=(r // tr,), out_shape=[jax.ShapeDtypeStruct((r, n), F32)] * 4,
        in_specs=[pl.BlockSpec((tr, N_DEV), lambda i: (i, 0)), _full((N_DEV, n)), spec, spec, spec],
        out_specs=[spec] * 4, compiler_params=_params("parallel"),
    )(act_t, dmod, w, m, v)


_SMALL_ROWS = (("b_ada", 48), ("g_mix", 8), ("g_ffn", 8), ("g_final", 8), ("b_spatial", 8),
               ("sinks", 1), ("loss", 1), ("pad", 6), ("w_spatial", 1024))


def _pack_small(parts):
    rows = []
    for name, nrows in _SMALL_ROWS:
        if name in parts:
            flat = parts[name].reshape(-1).astype(F32)
            flat = jnp.pad(flat, (0, nrows * LANES - flat.shape[0]))
            rows.append(flat.reshape(nrows, LANES))
        else:
            rows.append(jnp.zeros((nrows, LANES), F32))
    return jnp.concatenate(rows, axis=0)


def _unpack_small(packed, shapes):
    out, start = {}, 0
    for name, nrows in _SMALL_ROWS:
        if name in shapes:
            size = math.prod(shapes[name])
            out[name] = packed[start:start + nrows].reshape(-1)[:size].reshape(shapes[name])
        start += nrows
    return out


def kernel(x, c, positions, w_ada, b_ada, g_mix, w_in, w_spatial, b_spatial, sinks, w_out, g_ffn, w_ff1, w_ff2, g_final, loss_target, m_w_ada, m_b_ada, m_g_mix, m_w_in, m_w_spatial, m_b_spatial, m_sinks, m_w_out, m_g_ffn, m_w_ff1, m_w_ff2, m_g_final, v_w_ada, v_b_ada, v_g_mix, v_w_in, v_w_spatial, v_b_spatial, v_sinks, v_w_out, v_g_ffn, v_w_ff1, v_w_ff2, v_g_final):
    xi, yi, ci = lax.axis_index("x"), lax.axis_index("y"), lax.axis_index("c")
    chip = 2 * xi + yi
    dev = 2 * chip + ci
    seq = x.shape[1]
    x2, tgt = x[0], loss_target[0]
    ada_cols = w_ada.shape[2]

    c_all = _all_gather8(c, "gather_c").reshape(N_DEV, D_MODEL)
    b_shard = lax.dynamic_slice(b_ada, (0, chip * ada_cols), (1, ada_cols))
    mod_part, act = _mod_kernel(c_all, w_ada[0], b_shard)
    mod_all = _all_gather8(mod_part, "gather_mod")
    mod_me = lax.dynamic_index_in_dim(mod_all[0::2], dev, axis=1, keepdims=False)
    mod_me = mod_me.reshape(N_MOD, D_MODEL)
    shift1, scale1, gate1, shift2, scale2, gate2 = (mod_me[k:k + 1] for k in range(N_MOD))

    def halves(w):
        r, n = w.shape[1], w.shape[2]
        return w[0].astype(WEIGHT_COMM_DTYPE).reshape(2, r // 2, n)

    w_in_full = _all_gather8(halves(w_in), "gather_w_in", split=True)
    w_in_full = w_in_full.reshape(N_CHIPS, D_MODEL, -1).transpose(1, 0, 2).reshape(D_MODEL, IN_PROJ_WIDTH)

    zeros_row = jnp.zeros((1, D_MODEL), F32)
    vecs1 = jnp.concatenate([g_mix, shift1, scale1] + [zeros_row] * 5, axis=0)
    vecs2 = jnp.concatenate([gate1, shift2, scale2, gate2, g_ffn, g_final.reshape(1, D_MODEL)]
                            + [zeros_row] * 2, axis=0)
    bias_full = jnp.repeat(b_spatial[0].T, HEAD_DIM, axis=1)
    sink_rows = jnp.broadcast_to(sinks[0][:, None], (N_Q_HEADS, LANES))
    inv_freq = ROPE_THETA ** (-jnp.arange(0, ROT_DIM, 2, dtype=F32) / ROT_DIM)
    rope_tab = _rope_lane_tables(*_rope_angle_kernel(positions, inv_freq.reshape(ROT_DIM // 2, 1)))

    proj, hb, g_ff1 = _in_proj_kernel(x2, vecs1, w_in_full, comm=_gather_job([halves(w_ff1)]))
    cat, g_ff2, g_out = _mixer_fwd_kernel(proj, rope_tab, w_spatial[0], bias_full, sink_rows,
                                          comm=_gather_job([halves(w_ff2), halves(w_out)]))
    g_ff1, g_ff2, g_out = _gather_forward([g_ff1, g_ff2, g_out], "gather_forward")
    w_out_full = g_out.reshape(D_MODEL, D_MODEL)
    w_ff1_blocks = g_ff1.reshape(N_CHIPS, D_MODEL, D_MODEL)
    w_ff2_blocks = g_ff2.reshape(N_CHIPS, D_MODEL, D_MODEL)
    dx1, dcat, dmix, h2b, rb, dab, dffb, sums2 = _trunk_kernel(
        x2, tgt, cat, vecs2, w_out_full, w_ff1_blocks, w_ff2_blocks)

    c_idx = ci.reshape(1).astype(jnp.int32)
    place = jnp.stack([chip, ci]).astype(jnp.int32)
    big = {"w_in": (w_in, m_w_in, v_w_in), "w_out": (w_out, m_w_out, v_w_out),
           "w_ff1": (w_ff1, m_w_ff1, v_w_ff1), "w_ff2": (w_ff2, m_w_ff2, v_w_ff2)}

    def chip_sums_of(names, grads):
        parts = [g.reshape(N_CHIPS, 2, big[nm][0].shape[1] // 2, big[nm][0].shape[2]) for nm, g in zip(names, grads)]
        from_sibling = _sibling_exchange_halves(parts, "grad_to_sibling_" + names[0])
        return [_add_halves_kernel(p, q, c_idx, "grad_add_" + nm) for nm, p, q in zip(names, parts, from_sibling)]

    dw_ff2 = _weight_grad_kernel(rb, dffb, "dw_ff2", 1024, 1024)
    dw_ff1 = _weight_grad_kernel(h2b, dab, "dw_ff1", 1024, 1024, blocked_cols=True)
    dw_out = _weight_grad_kernel(cat, dmix, "dw_out", 1024, 1024)
    cs_ff2, cs_ff1, cs_out = chip_sums_of(["w_ff2", "w_ff1", "w_out"], [dw_ff2, dw_ff1, dw_out])
    dproj, dw_spatial, db_lanes, dsink_rows, sc_ff2, sc_out = _mixer_bwd_kernel(
        proj, rope_tab, dcat, w_spatial[0], w_spatial[0].transpose(0, 2, 1), bias_full, sink_rows,
        comm=_scatter_job([cs_ff2, cs_out]))
    grad_x, sums1, sc_ff1 = _in_proj_bwd_kernel(x2, dx1, dproj, vecs1, w_in_full, comm=_scatter_job([cs_ff1]))
    dw_in = _weight_grad_kernel(hb, dproj, "dw_in", 1024, 896)
    dw_in = dw_in.reshape(D_MODEL, N_CHIPS, IN_PROJ_WIDTH // N_CHIPS).transpose(1, 0, 2)
    cs_in, = chip_sums_of(["w_in"], [dw_in])
    sc_in, = _run_comm(_scatter_job([cs_in]), "grad_to_chips_w_in")

    names = ["w_in", "w_out", "w_ff1", "w_ff2"]
    totals = [_sum_chips_kernel(own, oth, place, "grad_sum_" + nm)
              for nm, own, oth in zip(names, [cs_in, cs_out, cs_ff1, cs_ff2], [sc_in, sc_out, sc_ff1, sc_ff2])]
    shared = _sibling_share(totals, "grad_share")
    big_out = {}
    for nm, g in zip(names, shared):
        w, m, v = big[nm]
        g = g.reshape(w.shape[1:])
        d, mn, vn = _adam_kernel(w[0], g, m[0], v[0], "adam_" + nm)
        big_out[nm] = tuple(t[None] for t in (g, d, mn, vn))

    dmod = jnp.concatenate([sums1[0:1], sums1[1:2], sums2[5:6], sums2[0:1], sums2[1:2], sums2[2:3]], axis=1)
    db_spatial = db_lanes[:, 0:GMLP_GROUPS].T
    small_grads = {"b_ada": dmod, "g_mix": sums1[2], "g_ffn": sums2[3], "g_final": sums2[4],
                   "b_spatial": db_spatial, "sinks": dsink_rows[:, 0], "loss": sums2[6, 0:1],
                   "w_spatial": dw_spatial}
    small_w = {"b_ada": b_ada, "g_mix": g_mix, "g_ffn": g_ffn, "g_final": g_final, "b_spatial": b_spatial,
               "sinks": sinks, "w_spatial": w_spatial}
    small_m = {"b_ada": m_b_ada, "g_mix": m_g_mix, "g_ffn": m_g_ffn, "g_final": m_g_final,
               "b_spatial": m_b_spatial, "sinks": m_sinks, "w_spatial": m_w_spatial}
    small_v = {"b_ada": v_b_ada, "g_mix": v_g_mix, "g_ffn": v_g_ffn, "g_final": v_g_final,
               "b_spatial": v_b_spatial, "sinks": v_sinks, "w_spatial": v_w_spatial}
    gathered = _all_gather8(_pack_small(small_grads), "gather_small")
    packed = _small_update_kernel(gathered, _pack_small(small_w), _pack_small(small_m), _pack_small(small_v))
    shapes = {k: a.shape for k, a in small_w.items()}
    sg, sd, sm, sv = (_unpack_small(p, shapes) for p in packed)
    loss = packed[0][sum(nr for nm, nr in _SMALL_ROWS[:6]), 0]

    dmod_all = gathered[:, 0:N_MOD * D_MODEL // LANES, :].reshape(N_DEV, N_MOD * D_MODEL)
    dmod_cols = lax.dynamic_slice(dmod_all, (0, chip * ada_cols), (N_DEV, ada_cols))
    ada = _ada_update_kernel(act.T, dmod_cols, w_ada[0], m_w_ada[0], v_w_ada[0])
    big_out["w_ada"] = tuple(t[None] for t in ada)

    order = ["w_ada", "b_ada", "g_mix", "w_in", "w_spatial", "b_spatial", "sinks", "w_out", "g_ffn",
             "w_ff1", "w_ff2", "g_final"]

    def leaf(nm, k):
        return big_out[nm][k] if nm in big_out else (sg, sd, sm, sv)[k][nm]

    outs = [loss, grad_x[None]]
    for k in range(4):
        outs += [leaf(nm, k) for nm in order]
    return tuple(outs)
```

```python
import math
from typing import Callable, NamedTuple

import jax
import jax.numpy as jnp
from jax import lax
from jax.experimental import pallas as pl
from jax.experimental.pallas import tpu as pltpu

F32 = jnp.float32
MXU_DTYPE = jnp.bfloat16
WEIGHT_COMM_DTYPE = jnp.bfloat16
GRAD_COMM_DTYPE = jnp.bfloat16

D_MODEL = 1024
D_FF = 4096
HEAD_DIM = 64
GMLP_GROUPS = 8
GMLP_WIDTH = 512
CHUNK = 128
N_Q_HEADS = 8
N_KV_HEADS = 2
ATTN_WIDTH = 512
KV_WIDTH = 128
ROT_DIM = 16
ROPE_THETA = 500000.0
IN_PROJ_WIDTH = 1792
N_MOD = 6
EPS = 1e-5
N_CHIPS = 4
N_DEV = 8
LANES = 128

ADAM_LR = 0.001
ADAM_B1 = 0.9
ADAM_B2 = 0.999
ADAM_EPS = 1e-08
ADAM_WD = 0.01
ADAM_STEP = 10

VMEM_LIMIT_BYTES = 58 * 1024 * 1024
MESH = pl.DeviceIdType.MESH


def _params(*semantics):
    return pltpu.CompilerParams(dimension_semantics=semantics, vmem_limit_bytes=VMEM_LIMIT_BYTES)


def _dot(a, b):
    return jnp.dot(a.astype(MXU_DTYPE), b.astype(MXU_DTYPE), preferred_element_type=F32)


def _dot_nt(a, b):
    return lax.dot_general(a.astype(MXU_DTYPE), b.astype(MXU_DTYPE), (((1,), (1,)), ((), ())),
                           preferred_element_type=F32)


def _dot_tn(a, b):
    return lax.dot_general(a.astype(MXU_DTYPE), b.astype(MXU_DTYPE), (((0,), (0,)), ((), ())),
                           preferred_element_type=F32)


def _full(shape):
    return pl.BlockSpec(shape, lambda *_: (0,) * len(shape))


def _any():
    return pl.BlockSpec(memory_space=pl.ANY)


def _rowsum(v):
    return jnp.sum(v, axis=0, keepdims=True)


def _mean_last(v):
    return jnp.mean(v, axis=-1, keepdims=True)


class _Comm(NamedTuple):
    operands: tuple
    out_shapes: tuple
    n_sems: int
    make: Callable


def _hosted_call(body, comm, *, name, grid, in_specs, out_shape, out_specs, scratch_shapes=(), semantics):
    if comm is None:
        return pl.pallas_call(body, name=name, grid=grid, in_specs=in_specs, out_shape=out_shape,
                              out_specs=out_specs, scratch_shapes=list(scratch_shapes),
                              compiler_params=_params(*semantics))
    n_in, n_out, n_scr = len(in_specs), len(out_shape), len(scratch_shapes)
    k_in, k_out = len(comm.operands), len(comm.out_shapes)

    def hosted(*refs):
        ins, refs = refs[:n_in], refs[n_in:]
        c_ins, refs = refs[:k_in], refs[k_in:]
        outs, refs = refs[:n_out], refs[n_out:]
        c_outs, refs = refs[:k_out], refs[k_out:]
        scratch, (send_sems, recv_sems) = refs[:n_scr], refs[n_scr:]
        first, last = None, None
        for d, size in enumerate(grid):
            at_start, at_end = pl.program_id(d) == 0, pl.program_id(d) == size - 1
            first = at_start if first is None else first & at_start
            last = at_end if last is None else last & at_end

        @pl.when(first)
        def _():
            for cp in comm.make(c_ins, c_outs, send_sems, recv_sems)[0]:
                cp.start()

        body(*ins, *outs, *scratch)

        @pl.when(last)
        def _():
            for wait in comm.make(c_ins, c_outs, send_sems, recv_sems)[1]:
                wait()

    call = pl.pallas_call(
        hosted, name=name, grid=grid, in_specs=list(in_specs) + [_any()] * k_in,
        out_shape=list(out_shape) + list(comm.out_shapes), out_specs=list(out_specs) + [_any()] * k_out,
        scratch_shapes=list(scratch_shapes) + [pltpu.SemaphoreType.DMA((comm.n_sems,)),
                                                pltpu.SemaphoreType.DMA((comm.n_sems,))],
        compiler_params=_params(*semantics))
    return lambda *args: call(*args, *comm.operands)


def _mesh_place():
    x, y, c = lax.axis_index("x"), lax.axis_index("y"), lax.axis_index("c")
    return x, y, c, [(1 - x, y), (x, 1 - y), (1 - x, 1 - y)]


def _gather_job(halves):
    per = 5

    def make(ins, outs, send_sems, recv_sems):
        x, y, c, chips = _mesh_place()
        starts, waits = [], []
        for a, (src, out) in enumerate(zip(ins, outs)):
            mine = out.at[4 * x + 2 * y + c]
            local = pltpu.make_async_copy(src.at[c], mine, send_sems.at[per * a + 4])
            to = [(x, y, 1 - c)] + [(px, py, c) for px, py in chips]
            sends = [pltpu.make_async_remote_copy(
                src_ref=src.at[c], dst_ref=mine, send_sem=send_sems.at[per * a + k],
                recv_sem=recv_sems.at[per * a + k], device_id=dev, device_id_type=MESH)
                for k, dev in enumerate(to)]
            recvs = [pltpu.make_async_remote_copy(
                src_ref=src.at[c], dst_ref=out.at[4 * px + 2 * py + pc], send_sem=send_sems.at[per * a + k],
                recv_sem=recv_sems.at[per * a + k], device_id=(px, py, pc), device_id_type=MESH)
                for k, (px, py, pc) in enumerate(to)]
            starts += [local] + sends
            waits += [local.wait] + [s.wait_send for s in sends] + [r.wait_recv for r in recvs]
        return starts, waits

    return _Comm(tuple(halves), tuple(jax.ShapeDtypeStruct((N_DEV,) + h.shape[1:], h.dtype) for h in halves),
                 per * len(halves), make)


def _gather_forward(bufs, name):
    n_arr = len(bufs)

    def body(*refs):
        outs = refs[n_arr:2 * n_arr]
        send_sems, recv_sems = refs[2 * n_arr:]
        x, y, c, chips = _mesh_place()
        sends, recvs = [], []
        for a, buf in enumerate(outs):
            for j, (px, py) in enumerate(chips):
                mine, theirs = buf.at[4 * px + 2 * py + c], buf.at[4 * px + 2 * py + 1 - c]
                sems = dict(send_sem=send_sems.at[3 * a + j], recv_sem=recv_sems.at[3 * a + j],
                            device_id=(x, y, 1 - c), device_id_type=MESH)
                sends.append(pltpu.make_async_remote_copy(src_ref=mine, dst_ref=mine, **sems))
                recvs.append(pltpu.make_async_remote_copy(src_ref=mine, dst_ref=theirs, **sems))
        for cp in sends:
            cp.start()
        for s, r in zip(sends, recvs):
            s.wait_send()
            r.wait_recv()

    return pl.pallas_call(
        body, name=name, out_shape=[jax.ShapeDtypeStruct(b.shape, b.dtype) for b in bufs],
        in_specs=[_any()] * n_arr, out_specs=[_any()] * n_arr,
        input_output_aliases={a: a for a in range(n_arr)},
        scratch_shapes=[pltpu.SemaphoreType.DMA((3 * n_arr,)), pltpu.SemaphoreType.DMA((3 * n_arr,))],
    )(*bufs)


def _sibling_job(parts):
    def make(ins, outs, send_sems, recv_sems):
        x, y, c, _ = _mesh_place()
        copies = [pltpu.make_async_remote_copy(
            src_ref=src.at[k, 1 - c], dst_ref=out.at[k], send_sem=send_sems.at[N_CHIPS * a + k],
            recv_sem=recv_sems.at[N_CHIPS * a + k], device_id=(x, y, 1 - c), device_id_type=MESH)
            for a, (src, out) in enumerate(zip(ins, outs)) for k in range(N_CHIPS)]
        return copies, [cp.wait for cp in copies]

    return _Comm(tuple(parts), tuple(jax.ShapeDtypeStruct((N_CHIPS,) + p.shape[2:], p.dtype) for p in parts),
                 N_CHIPS * len(parts), make)


def _scatter_job(chip_sums):
    def make(ins, outs, send_sems, recv_sems):
        x, y, c, chips = _mesh_place()
        copies = [pltpu.make_async_remote_copy(
            src_ref=src.at[2 * px + py], dst_ref=out.at[j], send_sem=send_sems.at[3 * a + j],
            recv_sem=recv_sems.at[3 * a + j], device_id=(px, py, c), device_id_type=MESH)
            for a, (src, out) in enumerate(zip(ins, outs)) for j, (px, py) in enumerate(chips)]
        return copies, [cp.wait for cp in copies]

    return _Comm(tuple(chip_sums), tuple(jax.ShapeDtypeStruct((3,) + s.shape[1:], s.dtype) for s in chip_sums),
                 3 * len(chip_sums), make)


def _run_comm(comm, name):
    def body(token_ref):
        token_ref[...] = jnp.zeros_like(token_ref)

    out = _hosted_call(body, comm, name=name, grid=(1,), in_specs=[],
                       out_shape=[jax.ShapeDtypeStruct((8, LANES), F32)], out_specs=[_full((8, LANES))],
                       semantics=("arbitrary",))()
    return out[1:]


def _all_gather8(block, name, split=False):
    blk_shape = block.shape[1:] if split else block.shape

    def body(x_ref, out_ref, send_sems, recv_sems, local_sem):
        x, y, c = lax.axis_index("x"), lax.axis_index("y"), lax.axis_index("c")
        me, sibling = (x, y, c), (x, y, 1 - c)
        chips = [(1 - x, y), (x, 1 - y), (1 - x, 1 - y)]
        src_mine = x_ref.at[c] if split else x_ref

        def slot(px, py, pc):
            return out_ref.at[4 * px + 2 * py + pc]

        def copy(k, blk, to, src=None):
            return pltpu.make_async_remote_copy(
                src_ref=slot(*blk) if src is None else src, dst_ref=slot(*blk),
                send_sem=send_sems.at[k], recv_sem=recv_sems.at[k],
                device_id=to, device_id_type=MESH)

        mine = pltpu.make_async_copy(src_mine, slot(*me), local_sem)
        mine.start()
        first = [copy(0, me, sibling, src=src_mine)]
        first += [copy(1 + j, me, (*chip, c), src=src_mine) for j, chip in enumerate(chips)]
        for cp in first:
            cp.start()
        passed = [copy(4 + j, (*chip, c), sibling) for j, chip in enumerate(chips)]
        for j, chip in enumerate(chips):
            copy(1 + j, (*chip, c), me).wait_recv()
            passed[j].start()
        copy(0, sibling, me).wait_recv()
        for j, chip in enumerate(chips):
            copy(4 + j, (*chip, 1 - c), me).wait_recv()
        for cp in first + passed:
            cp.wait_send()
        mine.wait()

    return pl.pallas_call(
        body, name=name,
        out_shape=jax.ShapeDtypeStruct((N_DEV,) + tuple(blk_shape), block.dtype),
        in_specs=[_any()], out_specs=_any(),
        scratch_shapes=[pltpu.SemaphoreType.DMA((7,)), pltpu.SemaphoreType.DMA((7,)),
                        pltpu.SemaphoreType.DMA],
    )(block)


def _sibling_share(bufs, name):
    n_arr = len(bufs)

    def body(*refs):
        out_refs = refs[n_arr:2 * n_arr]
        send_sems, recv_sems = refs[2 * n_arr:]
        x, y, c = lax.axis_index("x"), lax.axis_index("y"), lax.axis_index("c")
        copies = [pltpu.make_async_remote_copy(
            src_ref=out_refs[a].at[c], dst_ref=out_refs[a].at[c],
            send_sem=send_sems.at[a], recv_sem=recv_sems.at[a],
            device_id=(x, y, 1 - c), device_id_type=MESH) for a in range(n_arr)]
        for cp in copies:
            cp.start()
        for a in range(n_arr):
            pltpu.make_async_remote_copy(
                src_ref=out_refs[a].at[c], dst_ref=out_refs[a].at[1 - c],
                send_sem=send_sems.at[a], recv_sem=recv_sems.at[a],
                device_id=(x, y, 1 - c), device_id_type=MESH).wait()

    return pl.pallas_call(
        body, name=name,
        out_shape=[jax.ShapeDtypeStruct(b.shape, b.dtype) for b in bufs],
        in_specs=[_any()] * n_arr, out_specs=[_any()] * n_arr,
        input_output_aliases={a: a for a in range(n_arr)},
        scratch_shapes=[pltpu.SemaphoreType.DMA((n_arr,)), pltpu.SemaphoreType.DMA((n_arr,))],
    )(*bufs)


def _gelu_tanh(z):
    k = math.sqrt(2.0 / math.pi)
    t = jnp.tanh(k * (z + 0.044715 * (z * z * z)))
    return 0.5 * z * (1.0 + t), t


def _gelu_tanh_grad(z, t):
    k = math.sqrt(2.0 / math.pi)
    return 0.5 * (1.0 + t) + 0.5 * z * (1.0 - t * t) * (k * (1.0 + 3.0 * 0.044715 * (z * z)))


def _rope_angle_kernel(pos_row, invf_col):
    seq = pos_row.shape[1]

    def body(p_ref, f_ref, cos_ref, sin_ref):
        ang = p_ref[...].astype(F32) * f_ref[...]
        cos_ref[...] = jnp.cos(ang)
        sin_ref[...] = jnp.sin(ang)

    return pl.pallas_call(
        body, name="rope_angles", grid=(1,), out_shape=[jax.ShapeDtypeStruct((ROT_DIM // 2, seq), F32)] * 2,
        in_specs=[_full((1, seq)), _full((ROT_DIM // 2, 1))], out_specs=[_full((ROT_DIM // 2, seq))] * 2,
        compiler_params=_params("arbitrary"),
    )(pos_row, invf_col)


def _rope_lane_tables(cos, sin):
    cos_t, sin_t = cos.T, sin.T
    seq, half = cos_t.shape
    ones = jnp.ones((seq, HEAD_DIM - ROT_DIM), F32)
    c64 = jnp.concatenate([cos_t, cos_t, ones], axis=1)
    s1 = jnp.concatenate([sin_t, jnp.zeros((seq, HEAD_DIM - half), F32)], axis=1)
    s2 = jnp.concatenate([jnp.zeros((seq, half), F32), sin_t, jnp.zeros((seq, HEAD_DIM - ROT_DIM), F32)], axis=1)
    return jnp.concatenate([jnp.tile(t, (1, LANES // HEAD_DIM)) for t in (c64, s1, s2)], axis=1)


def _rope_apply(t, tab, sign):
    reps = t.shape[1] // LANES
    c_tab, s1, s2 = (jnp.tile(tab[:, LANES * k:LANES * (k + 1)], (1, reps)) if reps > 1
                     else tab[:, LANES * k:LANES * (k + 1)] for k in range(3))
    half = ROT_DIM // 2
    up = pltpu.roll(t, t.shape[1] - half, 1)
    down = pltpu.roll(t, half, 1)
    return t * c_tab + sign * (down * s2 - up * s1)


def _lane_masks(shape):
    lane = lax.broadcasted_iota(jnp.int32, shape, 1)
    return lane < HEAD_DIM, lane >= HEAD_DIM


HEADS_PER_GROUP = N_Q_HEADS // N_KV_HEADS
ATTN_SCALE = 1.0 / math.sqrt(HEAD_DIM)


def _attn_bias_t(first_block):
    kj = lax.broadcasted_iota(jnp.int32, (2 * CHUNK, CHUNK), 0)
    qi = lax.broadcasted_iota(jnp.int32, (2 * CHUNK, CHUNK), 1)
    ok = (kj > qi) & (kj <= qi + CHUNK) & (jnp.logical_not(first_block) | (kj >= CHUNK))
    return jnp.tile(jnp.where(ok, 0.0, -jnp.inf), (1, HEADS_PER_GROUP))


def _group_rows(x, g, lo, hi):
    rows = []
    for r in range(HEADS_PER_GROUP):
        h = HEADS_PER_GROUP * g + r
        pair = x[:, LANES * (h // 2):LANES * (h // 2 + 1)]
        rows.append(jnp.where(hi if h % 2 else lo, pair, 0.0))
    return jnp.concatenate(rows, axis=0)


def _pairs_from_rows(rows, lo):
    return [jnp.where(lo, rows[2 * CHUNK * k:2 * CHUNK * k + CHUNK], rows[2 * CHUNK * k + CHUNK:2 * CHUNK * (k + 1)])
            for k in range(HEADS_PER_GROUP // 2)]


def _group_dup(a, b, g, lo2):
    return jnp.where(lo2, a, b) if g == 0 else jnp.where(lo2, b, a)


def _sink_row(sink_ref, g):
    return jnp.concatenate([sink_ref[HEADS_PER_GROUP * g + r:HEADS_PER_GROUP * g + r + 1, :]
                            for r in range(HEADS_PER_GROUP)], axis=1)


def _attn_probs_t(k_dup, q_rows, bias_t, sink_row):
    s_t = _dot_nt(k_dup, q_rows) * ATTN_SCALE + bias_t
    m = jnp.maximum(jnp.max(s_t, axis=0, keepdims=True), sink_row)
    p = jnp.exp(s_t - m)
    e_sink = jnp.exp(sink_row - m)
    inv = 1.0 / (jnp.sum(p, axis=0, keepdims=True) + e_sink)
    return p * inv, e_sink * inv


def _sgu_forward_pair(wm, vp, j):
    lo, hi = _lane_masks(vp.shape)
    lhs = jnp.concatenate([wm[2 * j], wm[2 * j + 1]], axis=1)
    rhs = jnp.concatenate([jnp.where(lo, vp, 0.0), jnp.where(hi, vp, 0.0)], axis=0)
    return _dot(lhs, rhs)


def _masked_spatial(w_ref):
    t = lax.broadcasted_iota(jnp.int32, (CHUNK, CHUNK), 0)
    s = lax.broadcasted_iota(jnp.int32, (CHUNK, CHUNK), 1)
    tril = s <= t
    return [jnp.where(tril, w_ref[g], 0.0) for g in range(GMLP_GROUPS)], tril, s >= t


def _mod_kernel(c_all, w_shard, b_shard):
    n = w_shard.shape[1]
    tn = 512

    def body(c_ref, w_ref, b_ref, mod_ref, act_ref):
        cv = c_ref[...]
        act = cv * (1.0 / (1.0 + jnp.exp(-cv)))
        act_ref[...] = act
        mod_ref[...] = _dot(act, w_ref[...]) + b_ref[...]

    return pl.pallas_call(
        body, name="ada_mod", grid=(n // tn,),
        out_shape=[jax.ShapeDtypeStruct((N_DEV, n), F32), jax.ShapeDtypeStruct((N_DEV, D_MODEL), F32)],
        in_specs=[_full((N_DEV, D_MODEL)), pl.BlockSpec((D_MODEL, tn), lambda i: (0, i)),
                  pl.BlockSpec((1, tn), lambda i: (0, i))],
        out_specs=[pl.BlockSpec((N_DEV, tn), lambda i: (0, i)), _full((N_DEV, D_MODEL))],
        compiler_params=_params("arbitrary"),
    )(c_all, w_shard, b_shard)


def _in_proj_kernel(x, vecs, w_in, comm=None):
    seq = x.shape[0]
    tm = 512

    def body(x_ref, v_ref, w_ref, proj_ref, h_ref):
        xv = x_ref[...]
        rstd = lax.rsqrt(_mean_last(xv * xv) + EPS)
        n1 = (xv * rstd) * v_ref[0:1, :]
        h = n1 * (1.0 + v_ref[2:3, :]) + v_ref[1:2, :]
        hb = h.astype(MXU_DTYPE)
        h_ref[...] = hb
        proj_ref[...] = _dot(hb, w_ref[...])

    return _hosted_call(
        body, comm, name="in_proj", grid=(seq // tm,),
        out_shape=[jax.ShapeDtypeStruct((seq, IN_PROJ_WIDTH), F32),
                   jax.ShapeDtypeStruct((seq, D_MODEL), MXU_DTYPE)],
        in_specs=[pl.BlockSpec((tm, D_MODEL), lambda i: (i, 0)), _full((8, D_MODEL)),
                  _full((D_MODEL, IN_PROJ_WIDTH))],
        out_specs=[pl.BlockSpec((tm, IN_PROJ_WIDTH), lambda i: (i, 0)),
                   pl.BlockSpec((tm, D_MODEL), lambda i: (i, 0))],
        semantics=("arbitrary",),
    )(x, vecs, w_in)


def _mixer_fwd_kernel(proj, rope_tab, w_spatial, bias_full, sink_rows, comm=None):
    seq = proj.shape[0]
    nb = seq // CHUNK
    kv_col = (2 * GMLP_WIDTH + ATTN_WIDTH) // (2 * KV_WIDTH)

    def body(proj_ref, prev_ref, tab_ref, ptab_ref, w_ref, bias_ref, sink_ref, cat_ref):
        i = pl.program_id(0)
        wm, _, _ = _masked_spatial(w_ref)
        for j in range(GMLP_GROUPS // 2):
            cols = slice(LANES * j, LANES * (j + 1))
            vcols = slice(GMLP_WIDTH + LANES * j, GMLP_WIDTH + LANES * (j + 1))
            u, _ = _gelu_tanh(proj_ref[:, cols])
            vp, _ = _gelu_tanh(proj_ref[:, vcols])
            sv = _sgu_forward_pair(wm, vp, j) + bias_ref[:, cols]
            cat_ref[:, cols] = (u * sv).astype(cat_ref.dtype)
        o = 2 * GMLP_WIDTH
        tab = tab_ref[...]
        q_r = _rope_apply(proj_ref[:, o:o + ATTN_WIDTH], tab, 1.0)
        k_cur = _rope_apply(proj_ref[:, o + ATTN_WIDTH:o + ATTN_WIDTH + KV_WIDTH], tab, 1.0)
        k_prev = _rope_apply(prev_ref[:, 0:KV_WIDTH], ptab_ref[...], 1.0)
        k_a = jnp.concatenate([k_prev, k_cur], axis=0)
        v_a = jnp.concatenate([prev_ref[:, KV_WIDTH:2 * KV_WIDTH],
                               proj_ref[:, o + ATTN_WIDTH + KV_WIDTH:o + ATTN_WIDTH + 2 * KV_WIDTH]], axis=0)
        k_b = pltpu.roll(k_a, HEAD_DIM, 1)
        v_b = pltpu.roll(v_a, HEAD_DIM, 1)
        bias_t = _attn_bias_t(i == 0)
        lo, hi = _lane_masks((CHUNK, LANES))
        lo2, _ = _lane_masks((2 * CHUNK, LANES))
        for g in range(N_KV_HEADS):
            p_t, _ = _attn_probs_t(_group_dup(k_a, k_b, g, lo2), _group_rows(q_r, g, lo, hi), bias_t,
                                   _sink_row(sink_ref, g))
            o_t = _dot(_group_dup(v_a, v_b, g, lo2).T, p_t)
            for k, pair in enumerate(_pairs_from_rows(o_t.T, lo)):
                c0 = GMLP_WIDTH + LANES * (2 * g + k)
                cat_ref[:, c0:c0 + LANES] = pair.astype(cat_ref.dtype)

    return _hosted_call(
        body, comm, name="mixer_fwd", grid=(nb,),
        out_shape=[jax.ShapeDtypeStruct((seq, D_MODEL), MXU_DTYPE)],
        in_specs=[pl.BlockSpec((CHUNK, IN_PROJ_WIDTH), lambda i: (i, 0)),
                  pl.BlockSpec((CHUNK, 2 * KV_WIDTH), lambda i: (jnp.maximum(i - 1, 0), kv_col)),
                  pl.BlockSpec((CHUNK, 3 * LANES), lambda i: (i, 0)),
                  pl.BlockSpec((CHUNK, 3 * LANES), lambda i: (jnp.maximum(i - 1, 0), 0)),
                  _full((GMLP_GROUPS, CHUNK, CHUNK)), _full((CHUNK, GMLP_WIDTH)),
                  _full((N_Q_HEADS, LANES))],
        out_specs=[pl.BlockSpec((CHUNK, D_MODEL), lambda i: (i, 0))],
        semantics=("arbitrary",),
    )(proj, proj, rope_tab, rope_tab, w_spatial, bias_full, sink_rows)


def _trunk_kernel(x, target, cat, vecs, w_out, w_ff1, w_ff2):
    seq = x.shape[0]
    tm = 256
    nj = D_FF // D_MODEL

    def body(x_ref, t_ref, cat_ref, v_ref, wout_hbm, w1_hbm, w2_hbm,
             dx1_ref, dcat_ref, dmix_ref, h2_ref, r_ref, da_ref, dff_ref, sums_ref,
             wout, w1, w2, a_scr, sem):
        i = pl.program_id(0)

        @pl.when(i == 0)
        def _():
            copies = [pltpu.make_async_copy(wout_hbm, wout, sem.at[0]),
                      pltpu.make_async_copy(w1_hbm, w1, sem.at[1]),
                      pltpu.make_async_copy(w2_hbm, w2, sem.at[2])]
            for cp in copies:
                cp.start()
            for cp in copies:
                cp.wait()
            sums_ref[...] = jnp.zeros_like(sums_ref)

        gate1, shift2, scale2 = v_ref[0:1, :], v_ref[1:2, :], v_ref[2:3, :]
        gate2, g_ffn, g_final = v_ref[3:4, :], v_ref[4:5, :], v_ref[5:6, :]

        mix = _dot(cat_ref[...], wout[...])
        x1 = x_ref[...] + gate1 * mix
        rstd2 = lax.rsqrt(_mean_last(x1 * x1) + EPS)
        xh2 = x1 * rstd2
        n2 = xh2 * g_ffn
        h2b = (n2 * (1.0 + scale2) + shift2).astype(MXU_DTYPE)
        h2_ref[...] = h2b
        ff = jnp.zeros((tm, D_MODEL), F32)
        for j in range(nj):
            a = _dot(h2b, w1[j])
            a_scr[j] = a
            relu = jnp.maximum(a, 0.0)
            rb = (relu * relu).astype(MXU_DTYPE)
            r_ref[:, D_MODEL * j:D_MODEL * (j + 1)] = rb
            ff = ff + _dot(rb, w2[j])
        x2 = x1 + gate2 * ff
        rstd3 = lax.rsqrt(_mean_last(x2 * x2) + EPS)
        xh3 = x2 * rstd3
        err = xh3 * g_final - t_ref[...]
        loss = 0.5 * _rowsum(_mean_last(err * err))
        dy = err * (1.0 / D_MODEL)
        dxh3 = dy * g_final
        dx2 = rstd3 * (dxh3 - xh3 * _mean_last(dxh3 * xh3))
        dffb = (dx2 * gate2).astype(MXU_DTYPE)
        dff_ref[...] = dffb
        dh2 = jnp.zeros((tm, D_MODEL), F32)
        for j in range(nj):
            dr = _dot_nt(dffb, w2[j])
            dab = (dr * (2.0 * jnp.maximum(a_scr[j], 0.0))).astype(MXU_DTYPE)
            da_ref[:, D_MODEL * j:D_MODEL * (j + 1)] = dab
            dh2 = dh2 + _dot_nt(dab, w1[j])
        dn2 = dh2 * (1.0 + scale2)
        dxh2 = dn2 * g_ffn
        dx1 = dx2 + rstd2 * (dxh2 - xh2 * _mean_last(dxh2 * xh2))
        dx1_ref[...] = dx1
        dmixb = (dx1 * gate1).astype(MXU_DTYPE)
        dmix_ref[...] = dmixb
        dcat_ref[...] = _dot_nt(dmixb, wout[...])

        sums_ref[0:1, :] += _rowsum(dh2)
        sums_ref[1:2, :] += _rowsum(dh2 * n2)
        sums_ref[2:3, :] += _rowsum(dx2 * ff)
        sums_ref[3:4, :] += _rowsum(dn2 * xh2)
        sums_ref[4:5, :] += _rowsum(dy * xh3)
        sums_ref[5:6, :] += _rowsum(dx1 * mix)
        sums_ref[6:7, :] += jnp.broadcast_to(loss, (1, D_MODEL))

    tok = lambda w: pl.BlockSpec((tm, w), lambda i: (i, 0))
    return pl.pallas_call(
        body, name="trunk", grid=(seq // tm,),
        out_shape=[jax.ShapeDtypeStruct((seq, D_MODEL), F32), jax.ShapeDtypeStruct((seq, D_MODEL), F32),
                   jax.ShapeDtypeStruct((seq, D_MODEL), MXU_DTYPE), jax.ShapeDtypeStruct((seq, D_MODEL), MXU_DTYPE),
                   jax.ShapeDtypeStruct((seq, D_FF), MXU_DTYPE), jax.ShapeDtypeStruct((seq, D_FF), MXU_DTYPE),
                   jax.ShapeDtypeStruct((seq, D_MODEL), MXU_DTYPE), jax.ShapeDtypeStruct((8, D_MODEL), F32)],
        in_specs=[tok(D_MODEL), tok(D_MODEL), tok(D_MODEL), _full((8, D_MODEL)), _any(), _any(), _any()],
        out_specs=[tok(D_MODEL), tok(D_MODEL), tok(D_MODEL), tok(D_MODEL), tok(D_FF), tok(D_FF), tok(D_MODEL),
                   _full((8, D_MODEL))],
        scratch_shapes=[pltpu.VMEM((D_MODEL, D_MODEL), MXU_DTYPE), pltpu.VMEM((nj, D_MODEL, D_MODEL), MXU_DTYPE),
                        pltpu.VMEM((nj, D_MODEL, D_MODEL), MXU_DTYPE), pltpu.VMEM((nj, tm, D_MODEL), F32),
                        pltpu.SemaphoreType.DMA((3,))],
        compiler_params=_params("arbitrary"),
    )(x, target, cat, vecs, w_out, w_ff1, w_ff2)


def _mixer_bwd_kernel(proj, rope_tab, dcat, w_spatial, w_spatial_t, bias_full, sink_rows, comm=None):
    seq = proj.shape[0]
    nb = seq // CHUNK
    kv_col = (2 * GMLP_WIDTH + ATTN_WIDTH) // (2 * KV_WIDTH)

    def body(proj_ref, prev_ref, tab_ref, ptab_ref, dcat_ref, w_ref, wt_ref, bias_ref, sink_ref,
             dproj_ref, dw_ref, db_ref, dsink_ref, carry):
        step = pl.program_id(0)
        blk = nb - 1 - step

        @pl.when(step == 0)
        def _():
            carry[...] = jnp.zeros_like(carry)
            dw_ref[...] = jnp.zeros_like(dw_ref)
            db_ref[...] = jnp.zeros_like(db_ref)
            dsink_ref[...] = jnp.zeros_like(dsink_ref)

        wm, tril, triu = _masked_spatial(w_ref)
        lo, hi = _lane_masks((CHUNK, LANES))
        lane = lax.broadcasted_iota(jnp.int32, (CHUNK, LANES), 1)
        db = jnp.zeros((CHUNK, LANES), F32)
        for j in range(GMLP_GROUPS // 2):
            cols = slice(LANES * j, LANES * (j + 1))
            vcols = slice(GMLP_WIDTH + LANES * j, GMLP_WIDTH + LANES * (j + 1))
            zu, zv = proj_ref[:, cols], proj_ref[:, vcols]
            u, tu = _gelu_tanh(zu)
            vp, tv = _gelu_tanh(zv)
            sv = _sgu_forward_pair(wm, vp, j) + bias_ref[:, cols]
            dout = dcat_ref[:, cols]
            du = dout * sv
            dsv = dout * u
            dsv_lo, dsv_hi = jnp.where(lo, dsv, 0.0), jnp.where(hi, dsv, 0.0)
            lhs_t = jnp.concatenate([jnp.where(triu, wt_ref[2 * j], 0.0),
                                     jnp.where(triu, wt_ref[2 * j + 1], 0.0)], axis=1)
            dv = _dot(lhs_t, jnp.concatenate([dsv_lo, dsv_hi], axis=0))
            dw_ref[2 * j] += jnp.where(tril, _dot_nt(dsv_lo, vp), 0.0)
            dw_ref[2 * j + 1] += jnp.where(tril, _dot_nt(dsv_hi, vp), 0.0)
            db = db + (jnp.where(lane == 2 * j, jnp.sum(dsv_lo, axis=1, keepdims=True), 0.0)
                       + jnp.where(lane == 2 * j + 1, jnp.sum(dsv_hi, axis=1, keepdims=True), 0.0))
            dproj_ref[:, cols] = (du * _gelu_tanh_grad(zu, tu)).astype(dproj_ref.dtype)
            dproj_ref[:, vcols] = (dv * _gelu_tanh_grad(zv, tv)).astype(dproj_ref.dtype)
        db_ref[...] += db
        o = 2 * GMLP_WIDTH
        tab = tab_ref[...]
        q_r = _rope_apply(proj_ref[:, o:o + ATTN_WIDTH], tab, 1.0)
        k_cur = _rope_apply(proj_ref[:, o + ATTN_WIDTH:o + ATTN_WIDTH + KV_WIDTH], tab, 1.0)
        k_prev = _rope_apply(prev_ref[:, 0:KV_WIDTH], ptab_ref[...], 1.0)
        k_a = jnp.concatenate([k_prev, k_cur], axis=0)
        v_a = jnp.concatenate([prev_ref[:, KV_WIDTH:2 * KV_WIDTH],
                               proj_ref[:, o + ATTN_WIDTH + KV_WIDTH:o + ATTN_WIDTH + 2 * KV_WIDTH]], axis=0)
        k_b = pltpu.roll(k_a, HEAD_DIM, 1)
        v_b = pltpu.roll(v_a, HEAD_DIM, 1)
        bias_t = _attn_bias_t(blk == 0)
        lo2, _ = _lane_masks((2 * CHUNK, LANES))
        dout_b = dcat_ref[:, GMLP_WIDTH:GMLP_WIDTH + ATTN_WIDTH]
        dk_tot, dv_tot, dq_pairs = [], [], []
        for g in range(N_KV_HEADS):
            k_dup, v_dup = _group_dup(k_a, k_b, g, lo2), _group_dup(v_a, v_b, g, lo2)
            q_rows = _group_rows(q_r, g, lo, hi)
            do_rows = _group_rows(dout_b, g, lo, hi)
            p_t, p_sink = _attn_probs_t(k_dup, q_rows, bias_t, _sink_row(sink_ref, g))
            dp_t = _dot_nt(v_dup, do_rows)
            delta = jnp.sum(p_t * dp_t, axis=0, keepdims=True)
            ds_t = p_t * (dp_t - delta) * ATTN_SCALE
            dsink = -p_sink * delta
            for r in range(HEADS_PER_GROUP):
                h = HEADS_PER_GROUP * g + r
                dsink_ref[h:h + 1, :] += jnp.broadcast_to(
                    jnp.sum(dsink[:, LANES * r:LANES * (r + 1)], axis=1, keepdims=True), (1, LANES))
            dk_full = _dot(ds_t, q_rows)
            dv_full = _dot(p_t, do_rows)
            dk_tot.append(dk_full + pltpu.roll(dk_full, HEAD_DIM, 1))
            dv_tot.append(dv_full + pltpu.roll(dv_full, HEAD_DIM, 1))
            dq_t = _dot(k_dup.T, ds_t)
            dq_pairs += _pairs_from_rows(dq_t.T, lo)
        dk_all = jnp.where(lo2, dk_tot[0], dk_tot[1])
        dv_all = jnp.where(lo2, dv_tot[0], dv_tot[1])
        dk_cur = dk_all[CHUNK:, :] + carry[:, 0:KV_WIDTH]
        dv_cur = dv_all[CHUNK:, :] + carry[:, KV_WIDTH:2 * KV_WIDTH]
        carry[:, 0:KV_WIDTH] = dk_all[:CHUNK, :]
        carry[:, KV_WIDTH:2 * KV_WIDTH] = dv_all[:CHUNK, :]
        dq = _rope_apply(jnp.concatenate(dq_pairs, axis=1), tab, -1.0)
        dproj_ref[:, o:o + ATTN_WIDTH] = dq.astype(dproj_ref.dtype)
        dproj_ref[:, o + ATTN_WIDTH:o + ATTN_WIDTH + KV_WIDTH] = (
            _rope_apply(dk_cur, tab, -1.0).astype(dproj_ref.dtype))
        dproj_ref[:, o + ATTN_WIDTH + KV_WIDTH:o + ATTN_WIDTH + 2 * KV_WIDTH] = dv_cur.astype(dproj_ref.dtype)

    rev = lambda i: nb - 1 - i
    return _hosted_call(
        body, comm, name="mixer_bwd", grid=(nb,),
        out_shape=[jax.ShapeDtypeStruct((seq, IN_PROJ_WIDTH), MXU_DTYPE),
                   jax.ShapeDtypeStruct((GMLP_GROUPS, CHUNK, CHUNK), F32),
                   jax.ShapeDtypeStruct((CHUNK, LANES), F32),
                   jax.ShapeDtypeStruct((N_Q_HEADS, LANES), F32)],
        in_specs=[pl.BlockSpec((CHUNK, IN_PROJ_WIDTH), lambda i: (rev(i), 0)),
                  pl.BlockSpec((CHUNK, 2 * KV_WIDTH), lambda i: (jnp.maximum(rev(i) - 1, 0), kv_col)),
                  pl.BlockSpec((CHUNK, 3 * LANES), lambda i: (rev(i), 0)),
                  pl.BlockSpec((CHUNK, 3 * LANES), lambda i: (jnp.maximum(rev(i) - 1, 0), 0)),
                  pl.BlockSpec((CHUNK, D_MODEL), lambda i: (rev(i), 0)),
                  _full((GMLP_GROUPS, CHUNK, CHUNK)), _full((GMLP_GROUPS, CHUNK, CHUNK)),
                  _full((CHUNK, GMLP_WIDTH)), _full((N_Q_HEADS, LANES))],
        out_specs=[pl.BlockSpec((CHUNK, IN_PROJ_WIDTH), lambda i: (rev(i), 0)),
                   _full((GMLP_GROUPS, CHUNK, CHUNK)), _full((CHUNK, LANES)), _full((N_Q_HEADS, LANES))],
        scratch_shapes=[pltpu.VMEM((CHUNK, 2 * KV_WIDTH), F32)],
        semantics=("arbitrary",),
    )(proj, proj, rope_tab, rope_tab, dcat, w_spatial, w_spatial_t, bias_full, sink_rows)


def _in_proj_bwd_kernel(x, dx1, dproj, vecs, w_in, comm=None):
    seq = x.shape[0]
    tm = 512

    def body(x_ref, dx1_ref, dp_ref, v_ref, w_ref, gx_ref, sums_ref):
        @pl.when(pl.program_id(0) == 0)
        def _():
            sums_ref[...] = jnp.zeros_like(sums_ref)

        g_mix, scale1 = v_ref[0:1, :], v_ref[2:3, :]
        dh = _dot_nt(dp_ref[...], w_ref[...])
        xv = x_ref[...]
        rstd = lax.rsqrt(_mean_last(xv * xv) + EPS)
        xh = xv * rstd
        dn1 = dh * (1.0 + scale1)
        dxh = dn1 * g_mix
        gx_ref[...] = dx1_ref[...] + rstd * (dxh - xh * _mean_last(dxh * xh))
        sums_ref[0:1, :] += _rowsum(dh)
        sums_ref[1:2, :] += _rowsum(dh * (xh * g_mix))
        sums_ref[2:3, :] += _rowsum(dn1 * xh)

    return _hosted_call(
        body, comm, name="in_proj_bwd", grid=(seq // tm,),
        out_shape=[jax.ShapeDtypeStruct((seq, D_MODEL), F32), jax.ShapeDtypeStruct((8, D_MODEL), F32)],
        in_specs=[pl.BlockSpec((tm, D_MODEL), lambda i: (i, 0)), pl.BlockSpec((tm, D_MODEL), lambda i: (i, 0)),
                  pl.BlockSpec((tm, IN_PROJ_WIDTH), lambda i: (i, 0)), _full((8, D_MODEL)),
                  _full((D_MODEL, IN_PROJ_WIDTH))],
        out_specs=[pl.BlockSpec((tm, D_MODEL), lambda i: (i, 0)), _full((8, D_MODEL))],
        semantics=("arbitrary",),
    )(x, dx1, dproj, vecs, w_in)


def _weight_grad_kernel(a, b, name, tm, tn, blocked_cols=False, comm=None):
    seq, m = a.shape
    n = b.shape[1]
    tk = min(seq, 1024)
    nk = seq // tk

    def body(a_ref, b_ref, o_ref, acc):
        k = pl.program_id(2)

        @pl.when(k == 0)
        def _():
            acc[...] = jnp.zeros_like(acc)

        acc[...] += _dot_tn(a_ref[...], b_ref[...])

        @pl.when(k == nk - 1)
        def _():
            o_ref[...] = acc[...]

    if blocked_cols:
        out_shape = jax.ShapeDtypeStruct((n // tn, m, tn), F32)
        out_spec = pl.BlockSpec((None, tm, tn), lambda i, j, k: (j, i, 0))
    else:
        out_shape = jax.ShapeDtypeStruct((m, n), F32)
        out_spec = pl.BlockSpec((tm, tn), lambda i, j, k: (i, j))
    out = _hosted_call(
        body, comm, name=name, grid=(m // tm, n // tn, nk), out_shape=[out_shape],
        in_specs=[pl.BlockSpec((tk, tm), lambda i, j, k: (k, i)), pl.BlockSpec((tk, tn), lambda i, j, k: (k, j))],
        out_specs=[out_spec], scratch_shapes=[pltpu.VMEM((tm, tn), F32)],
        semantics=("arbitrary", "arbitrary", "arbitrary"),
    )(a, b)
    return out[0] if comm is None else out


def _adam_update(w, g, m, v):
    m_new = ADAM_B1 * m + (1.0 - ADAM_B1) * g
    v_new = ADAM_B2 * v + (1.0 - ADAM_B2) * (g * g)
    m_hat = m_new / (1.0 - ADAM_B1 ** ADAM_STEP)
    v_hat = v_new / (1.0 - ADAM_B2 ** ADAM_STEP)
    delta = -ADAM_LR * (m_hat / (jnp.sqrt(v_hat) + ADAM_EPS) + ADAM_WD * w)
    return delta, m_new, v_new


def _add_halves_kernel(part, recv, c_idx, name):
    _, _, r, n = part.shape
    tr = min(r, 256)

    def body(c_ref, p_ref, q_ref, o_ref):
        del c_ref
        o_ref[...] = (p_ref[...] + q_ref[...]).astype(o_ref.dtype)

    return pl.pallas_call(
        body, name=name, out_shape=jax.ShapeDtypeStruct((N_CHIPS, r, n), GRAD_COMM_DTYPE),
        grid_spec=pltpu.PrefetchScalarGridSpec(
            num_scalar_prefetch=1, grid=(N_CHIPS, r // tr),
            in_specs=[pl.BlockSpec((None, None, tr, n), lambda k, i, c: (k, c[0], i, 0)),
                      pl.BlockSpec((None, tr, n), lambda k, i, c: (k, i, 0))],
            out_specs=pl.BlockSpec((None, tr, n), lambda k, i, c: (k, i, 0))),
        compiler_params=_params("parallel", "parallel"),
    )(c_idx, part, recv)


def _sum_chips_kernel(own, others, place, name):
    _, r, n = own.shape
    tr = min(r, 256)

    def body(place_ref, own_ref, oth_ref, o_ref):
        del place_ref
        acc = own_ref[...].astype(F32)
        for k in range(N_CHIPS - 1):
            acc = acc + oth_ref[k].astype(F32)
        o_ref[...] = acc

    return pl.pallas_call(
        body, name=name, out_shape=jax.ShapeDtypeStruct((2, r, n), F32),
        grid_spec=pltpu.PrefetchScalarGridSpec(
            num_scalar_prefetch=1, grid=(r // tr,),
            in_specs=[pl.BlockSpec((None, tr, n), lambda i, p: (p[0], i, 0)),
                      pl.BlockSpec((N_CHIPS - 1, tr, n), lambda i, p: (0, i, 0))],
            out_specs=pl.BlockSpec((None, tr, n), lambda i, p: (p[1], i, 0))),
        compiler_params=_params("parallel"),
    )(place, own, others)


def _adam_kernel(w, g, m, v, name):
    r, n = w.shape
    tr = min(r, 256)

    def body(w_ref, g_ref, m_ref, v_ref, d_ref, mo_ref, vo_ref):
        d_ref[...], mo_ref[...], vo_ref[...] = _adam_update(w_ref[...], g_ref[...], m_ref[...], v_ref[...])

    spec = pl.BlockSpec((tr, n), lambda i: (i, 0))
    return pl.pallas_call(
        body, name=name, grid=(r // tr,), out_shape=[jax.ShapeDtypeStruct((r, n), F32)] * 3,
        in_specs=[spec] * 4, out_specs=[spec] * 3, compiler_params=_params("parallel"),
    )(w, g, m, v)


def _small_update_kernel(gathered, w, m, v):
    _, r, n = gathered.shape

    def body(g_ref, w_ref, m_ref, v_ref, gs_ref, d_ref, mo_ref, vo_ref):
        g = g_ref[0]
        for k in range(1, N_DEV):
            g = g + g_ref[k]
        gs_ref[...] = g
        d_ref[...], mo_ref[...], vo_ref[...] = _adam_update(w_ref[...], g, m_ref[...], v_ref[...])

    return pl.pallas_call(
        body, name="small_update", grid=(1,), out_shape=[jax.ShapeDtypeStruct((r, n), F32)] * 4,
        in_specs=[_full((N_DEV, r, n))] + [_full((r, n))] * 3, out_specs=[_full((r, n))] * 4,
        compiler_params=_params("arbitrary"),
    )(gathered, w, m, v)


def _ada_update_kernel(act_t, dmod, w, m, v):
    r, n = w.shape
    tr = 256

    def body(a_ref, d_ref, w_ref, m_ref, v_ref, g_ref, dl_ref, mo_ref, vo_ref):
        g = _dot(a_ref[...], d_ref[...])
        g_ref[...] = g
        dl_ref[...], mo_ref[...], vo_ref[...] = _adam_update(w_ref[...], g, m_ref[...], v_ref[...])

    spec = pl.BlockSpec((tr, n), lambda i: (i, 0))
    return pl.pallas_call(
        body, name="ada_update", grid=(r // tr,), out_shape=[jax.ShapeDtypeStruct((r, n), F32)] * 4,
        in_specs=[pl.BlockSpec((tr, N_DEV), lambda i: (i, 0)), _full((N_DEV, n)), spec, spec, spec],
        out_specs=[spec] * 4, compiler_params=_params("parallel"),
    )(act_t, dmod, w, m, v)


_SMALL_ROWS = (("b_ada", 48), ("g_mix", 8), ("g_ffn", 8), ("g_final", 8), ("b_spatial", 8),
               ("sinks", 1), ("loss", 1), ("pad", 6), ("w_spatial", 1024))


def _pack_small(parts):
    rows = []
    for name, nrows in _SMALL_ROWS:
        if name in parts:
            flat = parts[name].reshape(-1).astype(F32)
            flat = jnp.pad(flat, (0, nrows * LANES - flat.shape[0]))
            rows.append(flat.reshape(nrows, LANES))
        else:
            rows.append(jnp.zeros((nrows, LANES), F32))
    return jnp.concatenate(rows, axis=0)


def _unpack_small(packed, shapes):
    out, start = {}, 0
    for name, nrows in _SMALL_ROWS:
        if name in shapes:
            size = math.prod(shapes[name])
            out[name] = packed[start:start + nrows].reshape(-1)[:size].reshape(shapes[name])
        start += nrows
    return out


def kernel(x, c, positions, w_ada, b_ada, g_mix, w_in, w_spatial, b_spatial, sinks, w_out, g_ffn, w_ff1, w_ff2, g_final, loss_target, m_w_ada, m_b_ada, m_g_mix, m_w_in, m_w_spatial, m_b_spatial, m_sinks, m_w_out, m_g_ffn, m_w_ff1, m_w_ff2, m_g_final, v_w_ada, v_b_ada, v_g_mix, v_w_in, v_w_spatial, v_b_spatial, v_sinks, v_w_out, v_g_ffn, v_w_ff1, v_w_ff2, v_g_final):
    xi, yi, ci = lax.axis_index("x"), lax.axis_index("y"), lax.axis_index("c")
    chip = 2 * xi + yi
    dev = 2 * chip + ci
    seq = x.shape[1]
    x2, tgt = x[0], loss_target[0]
    ada_cols = w_ada.shape[2]

    c_all = _all_gather8(c, "gather_c").reshape(N_DEV, D_MODEL)
    b_shard = lax.dynamic_slice(b_ada, (0, chip * ada_cols), (1, ada_cols))
    mod_part, act = _mod_kernel(c_all, w_ada[0], b_shard)
    mod_all = _all_gather8(mod_part, "gather_mod")
    mod_me = lax.dynamic_index_in_dim(mod_all[0::2], dev, axis=1, keepdims=False)
    mod_me = mod_me.reshape(N_MOD, D_MODEL)
    shift1, scale1, gate1, shift2, scale2, gate2 = (mod_me[k:k + 1] for k in range(N_MOD))

    def halves(w):
        r, n = w.shape[1], w.shape[2]
        return w[0].astype(WEIGHT_COMM_DTYPE).reshape(2, r // 2, n)

    w_in_full = _all_gather8(halves(w_in), "gather_w_in", split=True)
    w_in_full = w_in_full.reshape(N_CHIPS, D_MODEL, -1).transpose(1, 0, 2).reshape(D_MODEL, IN_PROJ_WIDTH)

    zeros_row = jnp.zeros((1, D_MODEL), F32)
    vecs1 = jnp.concatenate([g_mix, shift1, scale1] + [zeros_row] * 5, axis=0)
    vecs2 = jnp.concatenate([gate1, shift2, scale2, gate2, g_ffn, g_final.reshape(1, D_MODEL)]
                            + [zeros_row] * 2, axis=0)
    bias_full = jnp.repeat(b_spatial[0].T, HEAD_DIM, axis=1)
    sink_rows = jnp.broadcast_to(sinks[0][:, None], (N_Q_HEADS, LANES))
    inv_freq = ROPE_THETA ** (-jnp.arange(0, ROT_DIM, 2, dtype=F32) / ROT_DIM)
    rope_tab = _rope_lane_tables(*_rope_angle_kernel(positions, inv_freq.reshape(ROT_DIM // 2, 1)))

    proj, hb, g_ff1 = _in_proj_kernel(x2, vecs1, w_in_full, comm=_gather_job([halves(w_ff1)]))
    cat, g_ff2, g_out = _mixer_fwd_kernel(proj, rope_tab, w_spatial[0], bias_full, sink_rows,
                                          comm=_gather_job([halves(w_ff2), halves(w_out)]))
    g_ff1, g_ff2, g_out = _gather_forward([g_ff1, g_ff2, g_out], "gather_forward")
    w_out_full = g_out.reshape(D_MODEL, D_MODEL)
    w_ff1_blocks = g_ff1.reshape(N_CHIPS, D_MODEL, D_MODEL)
    w_ff2_blocks = g_ff2.reshape(N_CHIPS, D_MODEL, D_MODEL)
    dx1, dcat, dmix, h2b, rb, dab, dffb, sums2 = _trunk_kernel(
        x2, tgt, cat, vecs2, w_out_full, w_ff1_blocks, w_ff2_blocks)

    c_idx = ci.reshape(1).astype(jnp.int32)
    place = jnp.stack([chip, ci]).astype(jnp.int32)
    big = {"w_in": (w_in, m_w_in, v_w_in), "w_out": (w_out, m_w_out, v_w_out),
           "w_ff1": (w_ff1, m_w_ff1, v_w_ff1), "w_ff2": (w_ff2, m_w_ff2, v_w_ff2)}

    def halves_of(nm, g):
        return g.reshape(N_CHIPS, 2, big[nm][0].shape[1] // 2, big[nm][0].shape[2])

    p_ff2 = halves_of("w_ff2", _weight_grad_kernel(rb, dffb, "dw_ff2", 1024, 1024))
    dw_ff1, q_ff2 = _weight_grad_kernel(h2b, dab, "dw_ff1", 1024, 1024, blocked_cols=True,
                                        comm=_sibling_job([p_ff2]))
    p_ff1 = halves_of("w_ff1", dw_ff1)
    dw_out, q_ff1 = _weight_grad_kernel(cat, dmix, "dw_out", 1024, 1024, comm=_sibling_job([p_ff1]))
    p_out = halves_of("w_out", dw_out)
    q_out, = _run_comm(_sibling_job([p_out]), "grad_to_sibling_w_out")
    cs_ff2 = _add_halves_kernel(p_ff2, q_ff2, c_idx, "grad_add_w_ff2")
    cs_ff1 = _add_halves_kernel(p_ff1, q_ff1, c_idx, "grad_add_w_ff1")
    cs_out = _add_halves_kernel(p_out, q_out, c_idx, "grad_add_w_out")
    dproj, dw_spatial, db_lanes, dsink_rows, sc_ff2, sc_out = _mixer_bwd_kernel(
        proj, rope_tab, dcat, w_spatial[0], w_spatial[0].transpose(0, 2, 1), bias_full, sink_rows,
        comm=_scatter_job([cs_ff2, cs_out]))
    dw_in, sc_ff1 = _weight_grad_kernel(hb, dproj, "dw_in", 1024, 896, comm=_scatter_job([cs_ff1]))
    grad_x, sums1 = _in_proj_bwd_kernel(x2, dx1, dproj, vecs1, w_in_full)
    p_in = halves_of("w_in", dw_in.reshape(D_MODEL, N_CHIPS, IN_PROJ_WIDTH // N_CHIPS).transpose(1, 0, 2))
    q_in, = _run_comm(_sibling_job([p_in]), "grad_to_sibling_w_in")
    cs_in = _add_halves_kernel(p_in, q_in, c_idx, "grad_add_w_in")
    sc_in, = _run_comm(_scatter_job([cs_in]), "grad_to_chips_w_in")

    names = ["w_in", "w_out", "w_ff1", "w_ff2"]
    totals = [_sum_chips_kernel(own, oth, place, "grad_sum_" + nm)
              for nm, own, oth in zip(names, [cs_in, cs_out, cs_ff1, cs_ff2], [sc_in, sc_out, sc_ff1, sc_ff2])]
    shared = _sibling_share(totals, "grad_share")
    big_out = {}
    for nm, g in zip(names, shared):
        w, m, v = big[nm]
        g = g.reshape(w.shape[1:])
        d, mn, vn = _adam_kernel(w[0], g, m[0], v[0], "adam_" + nm)
        big_out[nm] = tuple(t[None] for t in (g, d, mn, vn))

    dmod = jnp.concatenate([sums1[0:1], sums1[1:2], sums2[5:6], sums2[0:1], sums2[1:2], sums2[2:3]], axis=1)
    db_spatial = db_lanes[:, 0:GMLP_GROUPS].T
    small_grads = {"b_ada": dmod, "g_mix": sums1[2], "g_ffn": sums2[3], "g_final": sums2[4],
                   "b_spatial": db_spatial, "sinks": dsink_rows[:, 0], "loss": sums2[6, 0:1],
                   "w_spatial": dw_spatial}
    small_w = {"b_ada": b_ada, "g_mix": g_mix, "g_ffn": g_ffn, "g_final": g_final, "b_spatial": b_spatial,
               "sinks": sinks, "w_spatial": w_spatial}
    small_m = {"b_ada": m_b_ada, "g_mix": m_g_mix, "g_ffn": m_g_ffn, "g_final": m_g_final,
               "b_spatial": m_b_spatial, "sinks": m_sinks, "w_spatial": m_w_spatial}
    small_v = {"b_ada": v_b_ada, "g_mix": v_g_mix, "g_ffn": v_g_ffn, "g_final": v_g_final,
               "b_spatial": v_b_spatial, "sinks": v_sinks, "w_spatial": v_w_spatial}
    gathered = _all_gather8(_pack_small(small_grads), "gather_small")
    packed = _small_update_kernel(gathered, _pack_small(small_w), _pack_small(small_m), _pack_small(small_v))
    shapes = {k: a.shape for k, a in small_w.items()}
    sg, sd, sm, sv = (_unpack_small(p, shapes) for p in packed)
    loss = packed[0][sum(nr for nm, nr in _SMALL_ROWS[:6]), 0]

    dmod_all = gathered[:, 0:N_MOD * D_MODEL // LANES, :].reshape(N_DEV, N_MOD * D_MODEL)
    dmod_cols = lax.dynamic_slice(dmod_all, (0, chip * ada_cols), (N_DEV, ada_cols))
    ada = _ada_update_kernel(act.T, dmod_cols, w_ada[0], m_w_ada[0], v_w_ada[0])
    big_out["w_ada"] = tuple(t[None] for t in ada)

    order = ["w_ada", "b_ada", "g_mix", "w_in", "w_spatial", "b_spatial", "sinks", "w_out", "g_ffn",
             "w_ff1", "w_ff2", "g_final"]

    def leaf(nm, k):
        return big_out[nm][k] if nm in big_out else (sg, sd, sm, sv)[k][nm]

    outs = [loss, grad_x[None]]
    for k in range(4):
        outs += [leaf(nm, k) for nm in order]
    return tuple(outs)
```

```python
import math
from typing import Callable, NamedTuple

import jax
import jax.numpy as jnp
from jax import lax
from jax.experimental import pallas as pl
from jax.experimental.pallas import tpu as pltpu

F32 = jnp.float32
MXU_DTYPE = jnp.bfloat16
WEIGHT_COMM_DTYPE = jnp.bfloat16
GRAD_COMM_DTYPE = jnp.bfloat16

D_MODEL = 1024
D_FF = 4096
HEAD_DIM = 64
GMLP_GROUPS = 8
GMLP_WIDTH = 512
CHUNK = 128
N_Q_HEADS = 8
N_KV_HEADS = 2
ATTN_WIDTH = 512
KV_WIDTH = 128
ROT_DIM = 16
ROPE_THETA = 500000.0
IN_PROJ_WIDTH = 1792
N_MOD = 6
EPS = 1e-5
N_CHIPS = 4
N_DEV = 8
LANES = 128

ADAM_LR = 0.001
ADAM_B1 = 0.9
ADAM_B2 = 0.999
ADAM_EPS = 1e-08
ADAM_WD = 0.01
ADAM_STEP = 10

VMEM_LIMIT_BYTES = 58 * 1024 * 1024
MESH = pl.DeviceIdType.MESH


def _params(*semantics):
    return pltpu.CompilerParams(dimension_semantics=semantics, vmem_limit_bytes=VMEM_LIMIT_BYTES)


def _dot(a, b):
    return jnp.dot(a.astype(MXU_DTYPE), b.astype(MXU_DTYPE), preferred_element_type=F32)


def _dot_nt(a, b):
    return lax.dot_general(a.astype(MXU_DTYPE), b.astype(MXU_DTYPE), (((1,), (1,)), ((), ())),
                           preferred_element_type=F32)


def _dot_tn(a, b):
    return lax.dot_general(a.astype(MXU_DTYPE), b.astype(MXU_DTYPE), (((0,), (0,)), ((), ())),
                           preferred_element_type=F32)


def _full(shape):
    return pl.BlockSpec(shape, lambda *_: (0,) * len(shape))


def _any():
    return pl.BlockSpec(memory_space=pl.ANY)


def _rowsum(v):
    return jnp.sum(v, axis=0, keepdims=True)


def _mean_last(v):
    return jnp.mean(v, axis=-1, keepdims=True)


class _Comm(NamedTuple):
    operands: tuple
    out_shapes: tuple
    n_sems: int
    make: Callable


def _hosted_call(body, comm, *, name, grid, in_specs, out_shape, out_specs, scratch_shapes=(), semantics):
    if comm is None:
        return pl.pallas_call(body, name=name, grid=grid, in_specs=in_specs, out_shape=out_shape,
                              out_specs=out_specs, scratch_shapes=list(scratch_shapes),
                              compiler_params=_params(*semantics))
    n_in, n_out, n_scr = len(in_specs), len(out_shape), len(scratch_shapes)
    k_in, k_out = len(comm.operands), len(comm.out_shapes)

    def hosted(*refs):
        ins, refs = refs[:n_in], refs[n_in:]
        c_ins, refs = refs[:k_in], refs[k_in:]
        outs, refs = refs[:n_out], refs[n_out:]
        c_outs, refs = refs[:k_out], refs[k_out:]
        scratch, (send_sems, recv_sems) = refs[:n_scr], refs[n_scr:]
        first, last = None, None
        for d, size in enumerate(grid):
            at_start, at_end = pl.program_id(d) == 0, pl.program_id(d) == size - 1
            first = at_start if first is None else first & at_start
            last = at_end if last is None else last & at_end

        @pl.when(first)
        def _():
            for cp in comm.make(c_ins, c_outs, send_sems, recv_sems)[0]:
                cp.start()

        body(*ins, *outs, *scratch)

        @pl.when(last)
        def _():
            for wait in comm.make(c_ins, c_outs, send_sems, recv_sems)[1]:
                wait()

    call = pl.pallas_call(
        hosted, name=name, grid=grid, in_specs=list(in_specs) + [_any()] * k_in,
        out_shape=list(out_shape) + list(comm.out_shapes), out_specs=list(out_specs) + [_any()] * k_out,
        scratch_shapes=list(scratch_shapes) + [pltpu.SemaphoreType.DMA((comm.n_sems,)),
                                                pltpu.SemaphoreType.DMA((comm.n_sems,))],
        compiler_params=_params(*semantics))
    return lambda *args: call(*args, *comm.operands)


def _mesh_place():
    x, y, c = lax.axis_index("x"), lax.axis_index("y"), lax.axis_index("c")
    return x, y, c, [(1 - x, y), (x, 1 - y), (1 - x, 1 - y)]


def _gather_job(halves):
    per = 5

    def make(ins, outs, send_sems, recv_sems):
        x, y, c, chips = _mesh_place()
        starts, waits = [], []
        for a, (src, out) in enumerate(zip(ins, outs)):
            mine = out.at[4 * x + 2 * y + c]
            local = pltpu.make_async_copy(src.at[c], mine, send_sems.at[per * a + 4])
            to = [(x, y, 1 - c)] + [(px, py, c) for px, py in chips]
            sends = [pltpu.make_async_remote_copy(
                src_ref=src.at[c], dst_ref=mine, send_sem=send_sems.at[per * a + k],
                recv_sem=recv_sems.at[per * a + k], device_id=dev, device_id_type=MESH)
                for k, dev in enumerate(to)]
            recvs = [pltpu.make_async_remote_copy(
                src_ref=src.at[c], dst_ref=out.at[4 * px + 2 * py + pc], send_sem=send_sems.at[per * a + k],
                recv_sem=recv_sems.at[per * a + k], device_id=(px, py, pc), device_id_type=MESH)
                for k, (px, py, pc) in enumerate(to)]
            starts += [local] + sends
            waits += [local.wait] + [s.wait_send for s in sends] + [r.wait_recv for r in recvs]
        return starts, waits

    return _Comm(tuple(halves), tuple(jax.ShapeDtypeStruct((N_DEV,) + h.shape[1:], h.dtype) for h in halves),
                 per * len(halves), make)


def _gather_forward(bufs, name):
    n_arr = len(bufs)

    def body(*refs):
        outs = refs[n_arr:2 * n_arr]
        send_sems, recv_sems = refs[2 * n_arr:]
        x, y, c, chips = _mesh_place()
        sends, recvs = [], []
        for a, buf in enumerate(outs):
            for j, (px, py) in enumerate(chips):
                mine, theirs = buf.at[4 * px + 2 * py + c], buf.at[4 * px + 2 * py + 1 - c]
                sems = dict(send_sem=send_sems.at[3 * a + j], recv_sem=recv_sems.at[3 * a + j],
                            device_id=(x, y, 1 - c), device_id_type=MESH)
                sends.append(pltpu.make_async_remote_copy(src_ref=mine, dst_ref=mine, **sems))
                recvs.append(pltpu.make_async_remote_copy(src_ref=mine, dst_ref=theirs, **sems))
        for cp in sends:
            cp.start()
        for s, r in zip(sends, recvs):
            s.wait_send()
            r.wait_recv()

    return pl.pallas_call(
        body, name=name, out_shape=[jax.ShapeDtypeStruct(b.shape, b.dtype) for b in bufs],
        in_specs=[_any()] * n_arr, out_specs=[_any()] * n_arr,
        input_output_aliases={a: a for a in range(n_arr)},
        scratch_shapes=[pltpu.SemaphoreType.DMA((3 * n_arr,)), pltpu.SemaphoreType.DMA((3 * n_arr,))],
    )(*bufs)


def _sibling_job(parts):
    def make(ins, outs, send_sems, recv_sems):
        x, y, c, _ = _mesh_place()
        copies = [pltpu.make_async_remote_copy(
            src_ref=src.at[k, 1 - c], dst_ref=out.at[k], send_sem=send_sems.at[N_CHIPS * a + k],
            recv_sem=recv_sems.at[N_CHIPS * a + k], device_id=(x, y, 1 - c), device_id_type=MESH)
            for a, (src, out) in enumerate(zip(ins, outs)) for k in range(N_CHIPS)]
        return copies, [cp.wait for cp in copies]

    return _Comm(tuple(parts), tuple(jax.ShapeDtypeStruct((N_CHIPS,) + p.shape[2:], p.dtype) for p in parts),
                 N_CHIPS * len(parts), make)


def _scatter_job(chip_sums):
    def make(ins, outs, send_sems, recv_sems):
        x, y, c, chips = _mesh_place()
        copies = [pltpu.make_async_remote_copy(
            src_ref=src.at[2 * px + py], dst_ref=out.at[j], send_sem=send_sems.at[3 * a + j],
            recv_sem=recv_sems.at[3 * a + j], device_id=(px, py, c), device_id_type=MESH)
            for a, (src, out) in enumerate(zip(ins, outs)) for j, (px, py) in enumerate(chips)]
        return copies, [cp.wait for cp in copies]

    return _Comm(tuple(chip_sums), tuple(jax.ShapeDtypeStruct((3,) + s.shape[1:], s.dtype) for s in chip_sums),
                 3 * len(chip_sums), make)


def _run_comm(comm, name):
    def body(token_ref):
        token_ref[...] = jnp.zeros_like(token_ref)

    out = _hosted_call(body, comm, name=name, grid=(1,), in_specs=[],
                       out_shape=[jax.ShapeDtypeStruct((8, LANES), F32)], out_specs=[_full((8, LANES))],
                       semantics=("arbitrary",))()
    return out[1:]


def _all_gather8(blocks, name, split=False):
    n_arr = len(blocks)

    def body(*refs):
        x_refs, out_refs = refs[:n_arr], refs[n_arr:2 * n_arr]
        send_sems, recv_sems, local_sems = refs[2 * n_arr:]
        x, y, c, chips = _mesh_place()
        me, sibling = (x, y, c), (x, y, 1 - c)
        arrays = []
        for a, (x_ref, out_ref) in enumerate(zip(x_refs, out_refs)):
            src_mine = x_ref.at[c] if split else x_ref

            def copy(k, blk, to, src=None, a=a, out_ref=out_ref):
                dst = out_ref.at[4 * blk[0] + 2 * blk[1] + blk[2]]
                return pltpu.make_async_remote_copy(
                    src_ref=dst if src is None else src, dst_ref=dst,
                    send_sem=send_sems.at[7 * a + k], recv_sem=recv_sems.at[7 * a + k],
                    device_id=to, device_id_type=MESH)

            mine = pltpu.make_async_copy(src_mine, out_ref.at[4 * x + 2 * y + c], local_sems.at[a])
            mine.start()
            first = [copy(0, me, sibling, src=src_mine)]
            first += [copy(1 + j, me, (*chip, c), src=src_mine) for j, chip in enumerate(chips)]
            for cp in first:
                cp.start()
            arrays.append((copy, mine, first))
        sent = []
        for copy, mine, first in arrays:
            passed = [copy(4 + j, (*chip, c), sibling) for j, chip in enumerate(chips)]
            for j, chip in enumerate(chips):
                copy(1 + j, (*chip, c), me).wait_recv()
                passed[j].start()
            sent += first + passed
        for copy, mine, first in arrays:
            copy(0, sibling, me).wait_recv()
            for j, chip in enumerate(chips):
                copy(4 + j, (*chip, 1 - c), me).wait_recv()
            mine.wait()
        for cp in sent:
            cp.wait_send()

    return pl.pallas_call(
        body, name=name,
        out_shape=[jax.ShapeDtypeStruct((N_DEV,) + tuple(b.shape[1:] if split else b.shape), b.dtype)
                   for b in blocks],
        in_specs=[_any()] * n_arr, out_specs=[_any()] * n_arr,
        scratch_shapes=[pltpu.SemaphoreType.DMA((7 * n_arr,)), pltpu.SemaphoreType.DMA((7 * n_arr,)),
                        pltpu.SemaphoreType.DMA((n_arr,))],
    )(*blocks)


def _sibling_share(bufs, name):
    n_arr = len(bufs)

    def body(*refs):
        out_refs = refs[n_arr:2 * n_arr]
        send_sems, recv_sems = refs[2 * n_arr:]
        x, y, c = lax.axis_index("x"), lax.axis_index("y"), lax.axis_index("c")
        copies = [pltpu.make_async_remote_copy(
            src_ref=out_refs[a].at[c], dst_ref=out_refs[a].at[c],
            send_sem=send_sems.at[a], recv_sem=recv_sems.at[a],
            device_id=(x, y, 1 - c), device_id_type=MESH) for a in range(n_arr)]
        for cp in copies:
            cp.start()
        for a in range(n_arr):
            pltpu.make_async_remote_copy(
                src_ref=out_refs[a].at[c], dst_ref=out_refs[a].at[1 - c],
                send_sem=send_sems.at[a], recv_sem=recv_sems.at[a],
                device_id=(x, y, 1 - c), device_id_type=MESH).wait()

    return pl.pallas_call(
        body, name=name,
        out_shape=[jax.ShapeDtypeStruct(b.shape, b.dtype) for b in bufs],
        in_specs=[_any()] * n_arr, out_specs=[_any()] * n_arr,
        input_output_aliases={a: a for a in range(n_arr)},
        scratch_shapes=[pltpu.SemaphoreType.DMA((n_arr,)), pltpu.SemaphoreType.DMA((n_arr,))],
    )(*bufs)


def _gelu_tanh(z):
    k = math.sqrt(2.0 / math.pi)
    t = jnp.tanh(k * (z + 0.044715 * (z * z * z)))
    return 0.5 * z * (1.0 + t), t


def _gelu_tanh_grad(z, t):
    k = math.sqrt(2.0 / math.pi)
    return 0.5 * (1.0 + t) + 0.5 * z * (1.0 - t * t) * (k * (1.0 + 3.0 * 0.044715 * (z * z)))


def _rope_angle_kernel(pos_row, invf_col):
    seq = pos_row.shape[1]

    def body(p_ref, f_ref, cos_ref, sin_ref):
        ang = p_ref[...].astype(F32) * f_ref[...]
        cos_ref[...] = jnp.cos(ang)
        sin_ref[...] = jnp.sin(ang)

    return pl.pallas_call(
        body, name="rope_angles", grid=(1,), out_shape=[jax.ShapeDtypeStruct((ROT_DIM // 2, seq), F32)] * 2,
        in_specs=[_full((1, seq)), _full((ROT_DIM // 2, 1))], out_specs=[_full((ROT_DIM // 2, seq))] * 2,
        compiler_params=_params("arbitrary"),
    )(pos_row, invf_col)


def _rope_lane_tables(cos, sin):
    cos_t, sin_t = cos.T, sin.T
    seq, half = cos_t.shape
    ones = jnp.ones((seq, HEAD_DIM - ROT_DIM), F32)
    c64 = jnp.concatenate([cos_t, cos_t, ones], axis=1)
    s1 = jnp.concatenate([sin_t, jnp.zeros((seq, HEAD_DIM - half), F32)], axis=1)
    s2 = jnp.concatenate([jnp.zeros((seq, half), F32), sin_t, jnp.zeros((seq, HEAD_DIM - ROT_DIM), F32)], axis=1)
    return jnp.concatenate([jnp.tile(t, (1, LANES // HEAD_DIM)) for t in (c64, s1, s2)], axis=1)


def _rope_apply(t, tab, sign):
    reps = t.shape[1] // LANES
    c_tab, s1, s2 = (jnp.tile(tab[:, LANES * k:LANES * (k + 1)], (1, reps)) if reps > 1
                     else tab[:, LANES * k:LANES * (k + 1)] for k in range(3))
    half = ROT_DIM // 2
    up = pltpu.roll(t, t.shape[1] - half, 1)
    down = pltpu.roll(t, half, 1)
    return t * c_tab + sign * (down * s2 - up * s1)


def _lane_masks(shape):
    lane = lax.broadcasted_iota(jnp.int32, shape, 1)
    return lane < HEAD_DIM, lane >= HEAD_DIM


HEADS_PER_GROUP = N_Q_HEADS // N_KV_HEADS
ATTN_SCALE = 1.0 / math.sqrt(HEAD_DIM)


def _attn_bias_t(first_block):
    kj = lax.broadcasted_iota(jnp.int32, (2 * CHUNK, CHUNK), 0)
    qi = lax.broadcasted_iota(jnp.int32, (2 * CHUNK, CHUNK), 1)
    ok = (kj > qi) & (kj <= qi + CHUNK) & (jnp.logical_not(first_block) | (kj >= CHUNK))
    return jnp.tile(jnp.where(ok, 0.0, -jnp.inf), (1, HEADS_PER_GROUP))


def _group_rows(x, g, lo, hi):
    rows = []
    for r in range(HEADS_PER_GROUP):
        h = HEADS_PER_GROUP * g + r
        pair = x[:, LANES * (h // 2):LANES * (h // 2 + 1)]
        rows.append(jnp.where(hi if h % 2 else lo, pair, 0.0))
    return jnp.concatenate(rows, axis=0)


def _pairs_from_rows(rows, lo):
    return [jnp.where(lo, rows[2 * CHUNK * k:2 * CHUNK * k + CHUNK], rows[2 * CHUNK * k + CHUNK:2 * CHUNK * (k + 1)])
            for k in range(HEADS_PER_GROUP // 2)]


def _group_dup(a, b, g, lo2):
    return jnp.where(lo2, a, b) if g == 0 else jnp.where(lo2, b, a)


def _sink_row(sink_ref, g):
    return jnp.concatenate([sink_ref[HEADS_PER_GROUP * g + r:HEADS_PER_GROUP * g + r + 1, :]
                            for r in range(HEADS_PER_GROUP)], axis=1)


def _attn_probs_t(k_dup, q_rows, bias_t, sink_row):
    s_t = _dot_nt(k_dup, q_rows) * ATTN_SCALE + bias_t
    m = jnp.maximum(jnp.max(s_t, axis=0, keepdims=True), sink_row)
    p = jnp.exp(s_t - m)
    e_sink = jnp.exp(sink_row - m)
    inv = 1.0 / (jnp.sum(p, axis=0, keepdims=True) + e_sink)
    return p * inv, e_sink * inv


def _sgu_forward_pair(wm, vp, j):
    lo, hi = _lane_masks(vp.shape)
    lhs = jnp.concatenate([wm[2 * j], wm[2 * j + 1]], axis=1)
    rhs = jnp.concatenate([jnp.where(lo, vp, 0.0), jnp.where(hi, vp, 0.0)], axis=0)
    return _dot(lhs, rhs)


def _masked_spatial(w_ref):
    t = lax.broadcasted_iota(jnp.int32, (CHUNK, CHUNK), 0)
    s = lax.broadcasted_iota(jnp.int32, (CHUNK, CHUNK), 1)
    tril = s <= t
    return [jnp.where(tril, w_ref[g], 0.0) for g in range(GMLP_GROUPS)], tril, s >= t


def _mod_kernel(c_all, w_shard, b_shard):
    n = w_shard.shape[1]
    tn = 512

    def body(c_ref, w_ref, b_ref, mod_ref, act_ref):
        cv = c_ref[...]
        act = cv * (1.0 / (1.0 + jnp.exp(-cv)))
        act_ref[...] = act
        mod_ref[...] = _dot(act, w_ref[...]) + b_ref[...]

    return pl.pallas_call(
        body, name="ada_mod", grid=(n // tn,),
        out_shape=[jax.ShapeDtypeStruct((N_DEV, n), F32), jax.ShapeDtypeStruct((N_DEV, D_MODEL), F32)],
        in_specs=[_full((N_DEV, D_MODEL)), pl.BlockSpec((D_MODEL, tn), lambda i: (0, i)),
                  pl.BlockSpec((1, tn), lambda i: (0, i))],
        out_specs=[pl.BlockSpec((N_DEV, tn), lambda i: (0, i)), _full((N_DEV, D_MODEL))],
        compiler_params=_params("arbitrary"),
    )(c_all, w_shard, b_shard)


def _in_proj_kernel(x, vecs, w_in, comm=None):
    seq = x.shape[0]
    tm = 512

    def body(x_ref, v_ref, w_ref, proj_ref, h_ref):
        xv = x_ref[...]
        rstd = lax.rsqrt(_mean_last(xv * xv) + EPS)
        n1 = (xv * rstd) * v_ref[0:1, :]
        h = n1 * (1.0 + v_ref[2:3, :]) + v_ref[1:2, :]
        hb = h.astype(MXU_DTYPE)
        h_ref[...] = hb
        proj_ref[...] = _dot(hb, w_ref[...])

    return _hosted_call(
        body, comm, name="in_proj", grid=(seq // tm,),
        out_shape=[jax.ShapeDtypeStruct((seq, IN_PROJ_WIDTH), F32),
                   jax.ShapeDtypeStruct((seq, D_MODEL), MXU_DTYPE)],
        in_specs=[pl.BlockSpec((tm, D_MODEL), lambda i: (i, 0)), _full((8, D_MODEL)),
                  _full((D_MODEL, IN_PROJ_WIDTH))],
        out_specs=[pl.BlockSpec((tm, IN_PROJ_WIDTH), lambda i: (i, 0)),
                   pl.BlockSpec((tm, D_MODEL), lambda i: (i, 0))],
        semantics=("arbitrary",),
    )(x, vecs, w_in)


def _mixer_fwd_kernel(proj, rope_tab, w_spatial, bias_full, sink_rows, comm=None):
    seq = proj.shape[0]
    nb = seq // CHUNK
    kv_col = (2 * GMLP_WIDTH + ATTN_WIDTH) // (2 * KV_WIDTH)

    def body(proj_ref, prev_ref, tab_ref, ptab_ref, w_ref, bias_ref, sink_ref, cat_ref):
        i = pl.program_id(0)
        wm, _, _ = _masked_spatial(w_ref)
        for j in range(GMLP_GROUPS // 2):
            cols = slice(LANES * j, LANES * (j + 1))
            vcols = slice(GMLP_WIDTH + LANES * j, GMLP_WIDTH + LANES * (j + 1))
            u, _ = _gelu_tanh(proj_ref[:, cols])
            vp, _ = _gelu_tanh(proj_ref[:, vcols])
            sv = _sgu_forward_pair(wm, vp, j) + bias_ref[:, cols]
            cat_ref[:, cols] = (u * sv).astype(cat_ref.dtype)
        o = 2 * GMLP_WIDTH
        tab = tab_ref[...]
        q_r = _rope_apply(proj_ref[:, o:o + ATTN_WIDTH], tab, 1.0)
        k_cur = _rope_apply(proj_ref[:, o + ATTN_WIDTH:o + ATTN_WIDTH + KV_WIDTH], tab, 1.0)
        k_prev = _rope_apply(prev_ref[:, 0:KV_WIDTH], ptab_ref[...], 1.0)
        k_a = jnp.concatenate([k_prev, k_cur], axis=0)
        v_a = jnp.concatenate([prev_ref[:, KV_WIDTH:2 * KV_WIDTH],
                               proj_ref[:, o + ATTN_WIDTH + KV_WIDTH:o + ATTN_WIDTH + 2 * KV_WIDTH]], axis=0)
        k_b = pltpu.roll(k_a, HEAD_DIM, 1)
        v_b = pltpu.roll(v_a, HEAD_DIM, 1)
        bias_t = _attn_bias_t(i == 0)
        lo, hi = _lane_masks((CHUNK, LANES))
        lo2, _ = _lane_masks((2 * CHUNK, LANES))
        for g in range(N_KV_HEADS):
            p_t, _ = _attn_probs_t(_group_dup(k_a, k_b, g, lo2), _group_rows(q_r, g, lo, hi), bias_t,
                                   _sink_row(sink_ref, g))
            o_t = _dot(_group_dup(v_a, v_b, g, lo2).T, p_t)
            for k, pair in enumerate(_pairs_from_rows(o_t.T, lo)):
                c0 = GMLP_WIDTH + LANES * (2 * g + k)
                cat_ref[:, c0:c0 + LANES] = pair.astype(cat_ref.dtype)

    return _hosted_call(
        body, comm, name="mixer_fwd", grid=(nb,),
        out_shape=[jax.ShapeDtypeStruct((seq, D_MODEL), MXU_DTYPE)],
        in_specs=[pl.BlockSpec((CHUNK, IN_PROJ_WIDTH), lambda i: (i, 0)),
                  pl.BlockSpec((CHUNK, 2 * KV_WIDTH), lambda i: (jnp.maximum(i - 1, 0), kv_col)),
                  pl.BlockSpec((CHUNK, 3 * LANES), lambda i: (i, 0)),
                  pl.BlockSpec((CHUNK, 3 * LANES), lambda i: (jnp.maximum(i - 1, 0), 0)),
                  _full((GMLP_GROUPS, CHUNK, CHUNK)), _full((CHUNK, GMLP_WIDTH)),
                  _full((N_Q_HEADS, LANES))],
        out_specs=[pl.BlockSpec((CHUNK, D_MODEL), lambda i: (i, 0))],
        semantics=("arbitrary",),
    )(proj, proj, rope_tab, rope_tab, w_spatial, bias_full, sink_rows)


def _trunk_kernel(x, target, cat, vecs, w_out, w_ff1, w_ff2):
    seq = x.shape[0]
    tm = 256
    nj = D_FF // D_MODEL

    def body(x_ref, t_ref, cat_ref, v_ref, wout_hbm, w1_hbm, w2_hbm,
             dx1_ref, dcat_ref, dmix_ref, h2_ref, r_ref, da_ref, dff_ref, sums_ref,
             wout, w1, w2, a_scr, sem):
        i = pl.program_id(0)

        @pl.when(i == 0)
        def _():
            copies = [pltpu.make_async_copy(wout_hbm, wout, sem.at[0]),
                      pltpu.make_async_copy(w1_hbm, w1, sem.at[1]),
                      pltpu.make_async_copy(w2_hbm, w2, sem.at[2])]
            for cp in copies:
                cp.start()
            for cp in copies:
                cp.wait()
            sums_ref[...] = jnp.zeros_like(sums_ref)

        gate1, shift2, scale2 = v_ref[0:1, :], v_ref[1:2, :], v_ref[2:3, :]
        gate2, g_ffn, g_final = v_ref[3:4, :], v_ref[4:5, :], v_ref[5:6, :]

        mix = _dot(cat_ref[...], wout[...])
        x1 = x_ref[...] + gate1 * mix
        rstd2 = lax.rsqrt(_mean_last(x1 * x1) + EPS)
        xh2 = x1 * rstd2
        n2 = xh2 * g_ffn
        h2b = (n2 * (1.0 + scale2) + shift2).astype(MXU_DTYPE)
        h2_ref[...] = h2b
        ff = jnp.zeros((tm, D_MODEL), F32)
        for j in range(nj):
            a = _dot(h2b, w1[j])
            a_scr[j] = a
            relu = jnp.maximum(a, 0.0)
            rb = (relu * relu).astype(MXU_DTYPE)
            r_ref[:, D_MODEL * j:D_MODEL * (j + 1)] = rb
            ff = ff + _dot(rb, w2[j])
        x2 = x1 + gate2 * ff
        rstd3 = lax.rsqrt(_mean_last(x2 * x2) + EPS)
        xh3 = x2 * rstd3
        err = xh3 * g_final - t_ref[...]
        loss = 0.5 * _rowsum(_mean_last(err * err))
        dy = err * (1.0 / D_MODEL)
        dxh3 = dy * g_final
        dx2 = rstd3 * (dxh3 - xh3 * _mean_last(dxh3 * xh3))
        dffb = (dx2 * gate2).astype(MXU_DTYPE)
        dff_ref[...] = dffb
        dh2 = jnp.zeros((tm, D_MODEL), F32)
        for j in range(nj):
            dr = _dot_nt(dffb, w2[j])
            dab = (dr * (2.0 * jnp.maximum(a_scr[j], 0.0))).astype(MXU_DTYPE)
            da_ref[:, D_MODEL * j:D_MODEL * (j + 1)] = dab
            dh2 = dh2 + _dot_nt(dab, w1[j])
        dn2 = dh2 * (1.0 + scale2)
        dxh2 = dn2 * g_ffn
        dx1 = dx2 + rstd2 * (dxh2 - xh2 * _mean_last(dxh2 * xh2))
        dx1_ref[...] = dx1
        dmixb = (dx1 * gate1).astype(MXU_DTYPE)
        dmix_ref[...] = dmixb
        dcat_ref[...] = _dot_nt(dmixb, wout[...])

        sums_ref[0:1, :] += _rowsum(dh2)
        sums_ref[1:2, :] += _rowsum(dh2 * n2)
        sums_ref[2:3, :] += _rowsum(dx2 * ff)
        sums_ref[3:4, :] += _rowsum(dn2 * xh2)
        sums_ref[4:5, :] += _rowsum(dy * xh3)
        sums_ref[5:6, :] += _rowsum(dx1 * mix)
        sums_ref[6:7, :] += jnp.broadcast_to(loss, (1, D_MODEL))

    tok = lambda w: pl.BlockSpec((tm, w), lambda i: (i, 0))
    return pl.pallas_call(
        body, name="trunk", grid=(seq // tm,),
        out_shape=[jax.ShapeDtypeStruct((seq, D_MODEL), F32), jax.ShapeDtypeStruct((seq, D_MODEL), F32),
                   jax.ShapeDtypeStruct((seq, D_MODEL), MXU_DTYPE), jax.ShapeDtypeStruct((seq, D_MODEL), MXU_DTYPE),
                   jax.ShapeDtypeStruct((seq, D_FF), MXU_DTYPE), jax.ShapeDtypeStruct((seq, D_FF), MXU_DTYPE),
                   jax.ShapeDtypeStruct((seq, D_MODEL), MXU_DTYPE), jax.ShapeDtypeStruct((8, D_MODEL), F32)],
        in_specs=[tok(D_MODEL), tok(D_MODEL), tok(D_MODEL), _full((8, D_MODEL)), _any(), _any(), _any()],
        out_specs=[tok(D_MODEL), tok(D_MODEL), tok(D_MODEL), tok(D_MODEL), tok(D_FF), tok(D_FF), tok(D_MODEL),
                   _full((8, D_MODEL))],
        scratch_shapes=[pltpu.VMEM((D_MODEL, D_MODEL), MXU_DTYPE), pltpu.VMEM((nj, D_MODEL, D_MODEL), MXU_DTYPE),
                        pltpu.VMEM((nj, D_MODEL, D_MODEL), MXU_DTYPE), pltpu.VMEM((nj, tm, D_MODEL), F32),
                        pltpu.SemaphoreType.DMA((3,))],
        compiler_params=_params("arbitrary"),
    )(x, target, cat, vecs, w_out, w_ff1, w_ff2)


def _mixer_bwd_kernel(proj, rope_tab, dcat, w_spatial, w_spatial_t, bias_full, sink_rows, comm=None):
    seq = proj.shape[0]
    nb = seq // CHUNK
    kv_col = (2 * GMLP_WIDTH + ATTN_WIDTH) // (2 * KV_WIDTH)

    def body(proj_ref, prev_ref, tab_ref, ptab_ref, dcat_ref, w_ref, wt_ref, bias_ref, sink_ref,
             dproj_ref, dw_ref, db_ref, dsink_ref, carry):
        step = pl.program_id(0)
        blk = nb - 1 - step

        @pl.when(step == 0)
        def _():
            carry[...] = jnp.zeros_like(carry)
            dw_ref[...] = jnp.zeros_like(dw_ref)
            db_ref[...] = jnp.zeros_like(db_ref)
            dsink_ref[...] = jnp.zeros_like(dsink_ref)

        wm, tril, triu = _masked_spatial(w_ref)
        lo, hi = _lane_masks((CHUNK, LANES))
        lane = lax.broadcasted_iota(jnp.int32, (CHUNK, LANES), 1)
        db = jnp.zeros((CHUNK, LANES), F32)
        for j in range(GMLP_GROUPS // 2):
            cols = slice(LANES * j, LANES * (j + 1))
            vcols = slice(GMLP_WIDTH + LANES * j, GMLP_WIDTH + LANES * (j + 1))
            zu, zv = proj_ref[:, cols], proj_ref[:, vcols]
            u, tu = _gelu_tanh(zu)
            vp, tv = _gelu_tanh(zv)
            sv = _sgu_forward_pair(wm, vp, j) + bias_ref[:, cols]
            dout = dcat_ref[:, cols]
            du = dout * sv
            dsv = dout * u
            dsv_lo, dsv_hi = jnp.where(lo, dsv, 0.0), jnp.where(hi, dsv, 0.0)
            lhs_t = jnp.concatenate([jnp.where(triu, wt_ref[2 * j], 0.0),
                                     jnp.where(triu, wt_ref[2 * j + 1], 0.0)], axis=1)
            dv = _dot(lhs_t, jnp.concatenate([dsv_lo, dsv_hi], axis=0))
            dw_ref[2 * j] += jnp.where(tril, _dot_nt(dsv_lo, vp), 0.0)
            dw_ref[2 * j + 1] += jnp.where(tril, _dot_nt(dsv_hi, vp), 0.0)
            db = db + (jnp.where(lane == 2 * j, jnp.sum(dsv_lo, axis=1, keepdims=True), 0.0)
                       + jnp.where(lane == 2 * j + 1, jnp.sum(dsv_hi, axis=1, keepdims=True), 0.0))
            dproj_ref[:, cols] = (du * _gelu_tanh_grad(zu, tu)).astype(dproj_ref.dtype)
            dproj_ref[:, vcols] = (dv * _gelu_tanh_grad(zv, tv)).astype(dproj_ref.dtype)
        db_ref[...] += db
        o = 2 * GMLP_WIDTH
        tab = tab_ref[...]
        q_r = _rope_apply(proj_ref[:, o:o + ATTN_WIDTH], tab, 1.0)
        k_cur = _rope_apply(proj_ref[:, o + ATTN_WIDTH:o + ATTN_WIDTH + KV_WIDTH], tab, 1.0)
        k_prev = _rope_apply(prev_ref[:, 0:KV_WIDTH], ptab_ref[...], 1.0)
        k_a = jnp.concatenate([k_prev, k_cur], axis=0)
        v_a = jnp.concatenate([prev_ref[:, KV_WIDTH:2 * KV_WIDTH],
                               proj_ref[:, o + ATTN_WIDTH + KV_WIDTH:o + ATTN_WIDTH + 2 * KV_WIDTH]], axis=0)
        k_b = pltpu.roll(k_a, HEAD_DIM, 1)
        v_b = pltpu.roll(v_a, HEAD_DIM, 1)
        bias_t = _attn_bias_t(blk == 0)
        lo2, _ = _lane_masks((2 * CHUNK, LANES))
        dout_b = dcat_ref[:, GMLP_WIDTH:GMLP_WIDTH + ATTN_WIDTH]
        dk_tot, dv_tot, dq_pairs = [], [], []
        for g in range(N_KV_HEADS):
            k_dup, v_dup = _group_dup(k_a, k_b, g, lo2), _group_dup(v_a, v_b, g, lo2)
            q_rows = _group_rows(q_r, g, lo, hi)
            do_rows = _group_rows(dout_b, g, lo, hi)
            p_t, p_sink = _attn_probs_t(k_dup, q_rows, bias_t, _sink_row(sink_ref, g))
            dp_t = _dot_nt(v_dup, do_rows)
            delta = jnp.sum(p_t * dp_t, axis=0, keepdims=True)
            ds_t = p_t * (dp_t - delta) * ATTN_SCALE
            dsink = -p_sink * delta
            for r in range(HEADS_PER_GROUP):
                h = HEADS_PER_GROUP * g + r
                dsink_ref[h:h + 1, :] += jnp.broadcast_to(
                    jnp.sum(dsink[:, LANES * r:LANES * (r + 1)], axis=1, keepdims=True), (1, LANES))
            dk_full = _dot(ds_t, q_rows)
            dv_full = _dot(p_t, do_rows)
            dk_tot.append(dk_full + pltpu.roll(dk_full, HEAD_DIM, 1))
            dv_tot.append(dv_full + pltpu.roll(dv_full, HEAD_DIM, 1))
            dq_t = _dot(k_dup.T, ds_t)
            dq_pairs += _pairs_from_rows(dq_t.T, lo)
        dk_all = jnp.where(lo2, dk_tot[0], dk_tot[1])
        dv_all = jnp.where(lo2, dv_tot[0], dv_tot[1])
        dk_cur = dk_all[CHUNK:, :] + carry[:, 0:KV_WIDTH]
        dv_cur = dv_all[CHUNK:, :] + carry[:, KV_WIDTH:2 * KV_WIDTH]
        carry[:, 0:KV_WIDTH] = dk_all[:CHUNK, :]
        carry[:, KV_WIDTH:2 * KV_WIDTH] = dv_all[:CHUNK, :]
        dq = _rope_apply(jnp.concatenate(dq_pairs, axis=1), tab, -1.0)
        dproj_ref[:, o:o + ATTN_WIDTH] = dq.astype(dproj_ref.dtype)
        dproj_ref[:, o + ATTN_WIDTH:o + ATTN_WIDTH + KV_WIDTH] = (
            _rope_apply(dk_cur, tab, -1.0).astype(dproj_ref.dtype))
        dproj_ref[:, o + ATTN_WIDTH + KV_WIDTH:o + ATTN_WIDTH + 2 * KV_WIDTH] = dv_cur.astype(dproj_ref.dtype)

    rev = lambda i: nb - 1 - i
    return _hosted_call(
        body, comm, name="mixer_bwd", grid=(nb,),
        out_shape=[jax.ShapeDtypeStruct((seq, IN_PROJ_WIDTH), MXU_DTYPE),
                   jax.ShapeDtypeStruct((GMLP_GROUPS, CHUNK, CHUNK), F32),
                   jax.ShapeDtypeStruct((CHUNK, LANES), F32),
                   jax.ShapeDtypeStruct((N_Q_HEADS, LANES), F32)],
        in_specs=[pl.BlockSpec((CHUNK, IN_PROJ_WIDTH), lambda i: (rev(i), 0)),
                  pl.BlockSpec((CHUNK, 2 * KV_WIDTH), lambda i: (jnp.maximum(rev(i) - 1, 0), kv_col)),
                  pl.BlockSpec((CHUNK, 3 * LANES), lambda i: (rev(i), 0)),
                  pl.BlockSpec((CHUNK, 3 * LANES), lambda i: (jnp.maximum(rev(i) - 1, 0), 0)),
                  pl.BlockSpec((CHUNK, D_MODEL), lambda i: (rev(i), 0)),
                  _full((GMLP_GROUPS, CHUNK, CHUNK)), _full((GMLP_GROUPS, CHUNK, CHUNK)),
                  _full((CHUNK, GMLP_WIDTH)), _full((N_Q_HEADS, LANES))],
        out_specs=[pl.BlockSpec((CHUNK, IN_PROJ_WIDTH), lambda i: (rev(i), 0)),
                   _full((GMLP_GROUPS, CHUNK, CHUNK)), _full((CHUNK, LANES)), _full((N_Q_HEADS, LANES))],
        scratch_shapes=[pltpu.VMEM((CHUNK, 2 * KV_WIDTH), F32)],
        semantics=("arbitrary",),
    )(proj, proj, rope_tab, rope_tab, dcat, w_spatial, w_spatial_t, bias_full, sink_rows)


def _in_proj_bwd_kernel(x, dx1, dproj, vecs, w_in, comm=None):
    seq = x.shape[0]
    tm = 512

    def body(x_ref, dx1_ref, dp_ref, v_ref, w_ref, gx_ref, sums_ref):
        @pl.when(pl.program_id(0) == 0)
        def _():
            sums_ref[...] = jnp.zeros_like(sums_ref)

        g_mix, scale1 = v_ref[0:1, :], v_ref[2:3, :]
        dh = _dot_nt(dp_ref[...], w_ref[...])
        xv = x_ref[...]
        rstd = lax.rsqrt(_mean_last(xv * xv) + EPS)
        xh = xv * rstd
        dn1 = dh * (1.0 + scale1)
        dxh = dn1 * g_mix
        gx_ref[...] = dx1_ref[...] + rstd * (dxh - xh * _mean_last(dxh * xh))
        sums_ref[0:1, :] += _rowsum(dh)
        sums_ref[1:2, :] += _rowsum(dh * (xh * g_mix))
        sums_ref[2:3, :] += _rowsum(dn1 * xh)

    return _hosted_call(
        body, comm, name="in_proj_bwd", grid=(seq // tm,),
        out_shape=[jax.ShapeDtypeStruct((seq, D_MODEL), F32), jax.ShapeDtypeStruct((8, D_MODEL), F32)],
        in_specs=[pl.BlockSpec((tm, D_MODEL), lambda i: (i, 0)), pl.BlockSpec((tm, D_MODEL), lambda i: (i, 0)),
                  pl.BlockSpec((tm, IN_PROJ_WIDTH), lambda i: (i, 0)), _full((8, D_MODEL)),
                  _full((D_MODEL, IN_PROJ_WIDTH))],
        out_specs=[pl.BlockSpec((tm, D_MODEL), lambda i: (i, 0)), _full((8, D_MODEL))],
        semantics=("arbitrary",),
    )(x, dx1, dproj, vecs, w_in)


def _weight_grad_kernel(a, b, name, tm, tn, blocked_cols=False, comm=None):
    seq, m = a.shape
    n = b.shape[1]
    tk = min(seq, 1024)
    nk = seq // tk

    def body(a_ref, b_ref, o_ref, acc):
        k = pl.program_id(2)

        @pl.when(k == 0)
        def _():
            acc[...] = jnp.zeros_like(acc)

        acc[...] += _dot_tn(a_ref[...], b_ref[...])

        @pl.when(k == nk - 1)
        def _():
            o_ref[...] = acc[...]

    if blocked_cols:
        out_shape = jax.ShapeDtypeStruct((n // tn, m, tn), F32)
        out_spec = pl.BlockSpec((None, tm, tn), lambda i, j, k: (j, i, 0))
    else:
        out_shape = jax.ShapeDtypeStruct((m, n), F32)
        out_spec = pl.BlockSpec((tm, tn), lambda i, j, k: (i, j))
    out = _hosted_call(
        body, comm, name=name, grid=(m // tm, n // tn, nk), out_shape=[out_shape],
        in_specs=[pl.BlockSpec((tk, tm), lambda i, j, k: (k, i)), pl.BlockSpec((tk, tn), lambda i, j, k: (k, j))],
        out_specs=[out_spec], scratch_shapes=[pltpu.VMEM((tm, tn), F32)],
        semantics=("arbitrary", "arbitrary", "arbitrary"),
    )(a, b)
    return out[0] if comm is None else out


def _adam_update(w, g, m, v):
    m_new = ADAM_B1 * m + (1.0 - ADAM_B1) * g
    v_new = ADAM_B2 * v + (1.0 - ADAM_B2) * (g * g)
    m_hat = m_new / (1.0 - ADAM_B1 ** ADAM_STEP)
    v_hat = v_new / (1.0 - ADAM_B2 ** ADAM_STEP)
    delta = -ADAM_LR * (m_hat / (jnp.sqrt(v_hat) + ADAM_EPS) + ADAM_WD * w)
    return delta, m_new, v_new


def _add_halves_kernel(part, recv, c_idx, name):
    _, _, r, n = part.shape
    tr = min(r, 256)

    def body(c_ref, p_ref, q_ref, o_ref):
        del c_ref
        o_ref[...] = (p_ref[...] + q_ref[...]).astype(o_ref.dtype)

    return pl.pallas_call(
        body, name=name, out_shape=jax.ShapeDtypeStruct((N_CHIPS, r, n), GRAD_COMM_DTYPE),
        grid_spec=pltpu.PrefetchScalarGridSpec(
            num_scalar_prefetch=1, grid=(N_CHIPS, r // tr),
            in_specs=[pl.BlockSpec((None, None, tr, n), lambda k, i, c: (k, c[0], i, 0)),
                      pl.BlockSpec((None, tr, n), lambda k, i, c: (k, i, 0))],
            out_specs=pl.BlockSpec((None, tr, n), lambda k, i, c: (k, i, 0))),
        compiler_params=_params("parallel", "parallel"),
    )(c_idx, part, recv)


def _sum_chips_kernel(own, others, place, name):
    _, r, n = own.shape
    tr = min(r, 256)

    def body(place_ref, own_ref, oth_ref, o_ref):
        del place_ref
        acc = own_ref[...].astype(F32)
        for k in range(N_CHIPS - 1):
            acc = acc + oth_ref[k].astype(F32)
        o_ref[...] = acc

    return pl.pallas_call(
        body, name=name, out_shape=jax.ShapeDtypeStruct((2, r, n), F32),
        grid_spec=pltpu.PrefetchScalarGridSpec(
            num_scalar_prefetch=1, grid=(r // tr,),
            in_specs=[pl.BlockSpec((None, tr, n), lambda i, p: (p[0], i, 0)),
                      pl.BlockSpec((N_CHIPS - 1, tr, n), lambda i, p: (0, i, 0))],
            out_specs=pl.BlockSpec((None, tr, n), lambda i, p: (p[1], i, 0))),
        compiler_params=_params("parallel"),
    )(place, own, others)


def _adam_kernel(w, g, m, v, name):
    r, n = w.shape
    tr = min(r, 256)

    def body(w_ref, g_ref, m_ref, v_ref, d_ref, mo_ref, vo_ref):
        d_ref[...], mo_ref[...], vo_ref[...] = _adam_update(w_ref[...], g_ref[...], m_ref[...], v_ref[...])

    spec = pl.BlockSpec((tr, n), lambda i: (i, 0))
    return pl.pallas_call(
        body, name=name, grid=(r // tr,), out_shape=[jax.ShapeDtypeStruct((r, n), F32)] * 3,
        in_specs=[spec] * 4, out_specs=[spec] * 3, compiler_params=_params("parallel"),
    )(w, g, m, v)


SMALL_PARAMS = ("b_ada", "g_mix", "g_ffn", "g_final", "b_spatial", "sinks", "w_spatial")


def _small_update_kernel(gathered, params):
    shapes = [params[nm][0].shape for nm in SMALL_PARAMS]

    def body(*refs):
        g_refs, refs = refs[:5], refs[5:]
        p_refs, refs = refs[:3 * len(SMALL_PARAMS)], refs[3 * len(SMALL_PARAMS):]
        loss_ref, o_refs = refs[0], refs[1:]

        def total(ref):
            acc = ref[0]
            for k in range(1, N_DEV):
                acc = acc + ref[k]
            return acc

        s1, s2, db, ds, dw = (total(r) for r in g_refs)
        loss_ref[...] = jnp.broadcast_to(s2[6:7, 0:1], loss_ref.shape)
        grads = {"b_ada": [s1[0:1], s1[1:2], s2[5:6], s2[0:1], s2[1:2], s2[2:3]], "g_mix": [s1[2:3]],
                 "g_ffn": [s2[3:4]], "g_final": [s2[4:5]], "b_spatial": [db.T[0:GMLP_GROUPS]],
                 "w_spatial": [dw]}
        lane = lax.broadcasted_iota(jnp.int32, (1, LANES), 1)
        sink_row = jnp.zeros((1, LANES), F32)
        for h in range(N_Q_HEADS):
            sink_row = sink_row + jnp.where(lane == h, ds[h:h + 1, :], 0.0)
        grads["sinks"] = [sink_row[:, 0:N_Q_HEADS]]
        for i, nm in enumerate(SMALL_PARAMS):
            w_ref, m_ref, v_ref = p_refs[3 * i:3 * i + 3]
            outs = o_refs[4 * i:4 * i + 4]
            width = grads[nm][0].shape[1]
            for k, g in enumerate(grads[nm]):
                cols = slice(width * k, width * (k + 1))
                upd = _adam_update(w_ref[:, cols], g, m_ref[:, cols], v_ref[:, cols])
                for o_ref, val in zip(outs, (g,) + upd):
                    o_ref[:, cols] = val

    flat = [a for nm in SMALL_PARAMS for a in params[nm]]
    out_shape = [jax.ShapeDtypeStruct((8, LANES), F32)]
    out_shape += [jax.ShapeDtypeStruct(s, F32) for s in shapes for _ in range(4)]
    outs = pl.pallas_call(
        body, name="small_update", grid=(1,), out_shape=out_shape,
        in_specs=[_full(g.shape) for g in gathered] + [_full(a.shape) for a in flat],
        out_specs=[_full(s.shape) for s in out_shape],
        compiler_params=_params("arbitrary"),
    )(*gathered, *flat)
    return {nm: outs[1 + 4 * i:5 + 4 * i] for i, nm in enumerate(SMALL_PARAMS)}, outs[0]


def _ada_update_kernel(act_t, dmod, w, m, v):
    r, n = w.shape
    tr = 256

    def body(a_ref, d_ref, w_ref, m_ref, v_ref, g_ref, dl_ref, mo_ref, vo_ref):
        g = _dot(a_ref[...], d_ref[...])
        g_ref[...] = g
        dl_ref[...], mo_ref[...], vo_ref[...] = _adam_update(w_ref[...], g, m_ref[...], v_ref[...])

    spec = pl.BlockSpec((tr, n), lambda i: (i, 0))
    return pl.pallas_call(
        body, name="ada_update", grid=(r // tr,), out_shape=[jax.ShapeDtypeStruct((r, n), F32)] * 4,
        in_specs=[pl.BlockSpec((tr, N_DEV), lambda i: (i, 0)), _full((N_DEV, n)), spec, spec, spec],
        out_specs=[spec] * 4, compiler_params=_params("parallel"),
    )(act_t, dmod, w, m, v)


def kernel(x, c, positions, w_ada, b_ada, g_mix, w_in, w_spatial, b_spatial, sinks, w_out, g_ffn, w_ff1, w_ff2, g_final, loss_target, m_w_ada, m_b_ada, m_g_mix, m_w_in, m_w_spatial, m_b_spatial, m_sinks, m_w_out, m_g_ffn, m_w_ff1, m_w_ff2, m_g_final, v_w_ada, v_b_ada, v_g_mix, v_w_in, v_w_spatial, v_b_spatial, v_sinks, v_w_out, v_g_ffn, v_w_ff1, v_w_ff2, v_g_final):
    xi, yi, ci = lax.axis_index("x"), lax.axis_index("y"), lax.axis_index("c")
    chip = 2 * xi + yi
    dev = 2 * chip + ci
    seq = x.shape[1]
    x2, tgt = x[0], loss_target[0]
    ada_cols = w_ada.shape[2]

    c_all = _all_gather8([c], "gather_c")[0].reshape(N_DEV, D_MODEL)
    b_shard = lax.dynamic_slice(b_ada, (0, chip * ada_cols), (1, ada_cols))
    mod_part, act = _mod_kernel(c_all, w_ada[0], b_shard)
    mod_all, = _all_gather8([mod_part], "gather_mod")
    mod_me = lax.dynamic_index_in_dim(mod_all[0::2], dev, axis=1, keepdims=False)
    mod_me = mod_me.reshape(N_MOD, D_MODEL)
    shift1, scale1, gate1, shift2, scale2, gate2 = (mod_me[k:k + 1] for k in range(N_MOD))

    def halves(w):
        r, n = w.shape[1], w.shape[2]
        return w[0].astype(WEIGHT_COMM_DTYPE).reshape(2, r // 2, n)

    w_in_full, = _all_gather8([halves(w_in)], "gather_w_in", split=True)
    w_in_full = w_in_full.reshape(N_CHIPS, D_MODEL, -1).transpose(1, 0, 2).reshape(D_MODEL, IN_PROJ_WIDTH)

    zeros_row = jnp.zeros((1, D_MODEL), F32)
    vecs1 = jnp.concatenate([g_mix, shift1, scale1] + [zeros_row] * 5, axis=0)
    vecs2 = jnp.concatenate([gate1, shift2, scale2, gate2, g_ffn, g_final.reshape(1, D_MODEL)]
                            + [zeros_row] * 2, axis=0)
    bias_full = jnp.repeat(b_spatial[0].T, HEAD_DIM, axis=1)
    sink_rows = jnp.broadcast_to(sinks[0][:, None], (N_Q_HEADS, LANES))
    inv_freq = ROPE_THETA ** (-jnp.arange(0, ROT_DIM, 2, dtype=F32) / ROT_DIM)
    rope_tab = _rope_lane_tables(*_rope_angle_kernel(positions, inv_freq.reshape(ROT_DIM // 2, 1)))

    proj, hb, g_ff1 = _in_proj_kernel(x2, vecs1, w_in_full, comm=_gather_job([halves(w_ff1)]))
    cat, g_ff2, g_out = _mixer_fwd_kernel(proj, rope_tab, w_spatial[0], bias_full, sink_rows,
                                          comm=_gather_job([halves(w_ff2), halves(w_out)]))
    g_ff1, g_ff2, g_out = _gather_forward([g_ff1, g_ff2, g_out], "gather_forward")
    w_out_full = g_out.reshape(D_MODEL, D_MODEL)
    w_ff1_blocks = g_ff1.reshape(N_CHIPS, D_MODEL, D_MODEL)
    w_ff2_blocks = g_ff2.reshape(N_CHIPS, D_MODEL, D_MODEL)
    dx1, dcat, dmix, h2b, rb, dab, dffb, sums2 = _trunk_kernel(
        x2, tgt, cat, vecs2, w_out_full, w_ff1_blocks, w_ff2_blocks)

    c_idx = ci.reshape(1).astype(jnp.int32)
    place = jnp.stack([chip, ci]).astype(jnp.int32)
    big = {"w_in": (w_in, m_w_in, v_w_in), "w_out": (w_out, m_w_out, v_w_out),
           "w_ff1": (w_ff1, m_w_ff1, v_w_ff1), "w_ff2": (w_ff2, m_w_ff2, v_w_ff2)}

    def halves_of(nm, g):
        return g.reshape(N_CHIPS, 2, big[nm][0].shape[1] // 2, big[nm][0].shape[2])

    p_ff2 = halves_of("w_ff2", _weight_grad_kernel(rb, dffb, "dw_ff2", 1024, 1024))
    dw_ff1, q_ff2 = _weight_grad_kernel(h2b, dab, "dw_ff1", 1024, 1024, blocked_cols=True,
                                        comm=_sibling_job([p_ff2]))
    p_ff1 = halves_of("w_ff1", dw_ff1)
    dw_out, q_ff1 = _weight_grad_kernel(cat, dmix, "dw_out", 1024, 1024, comm=_sibling_job([p_ff1]))
    p_out = halves_of("w_out", dw_out)
    q_out, = _run_comm(_sibling_job([p_out]), "grad_to_sibling_w_out")
    cs_ff2 = _add_halves_kernel(p_ff2, q_ff2, c_idx, "grad_add_w_ff2")
    cs_ff1 = _add_halves_kernel(p_ff1, q_ff1, c_idx, "grad_add_w_ff1")
    cs_out = _add_halves_kernel(p_out, q_out, c_idx, "grad_add_w_out")
    dproj, dw_spatial, db_lanes, dsink_rows, sc_ff2, sc_out = _mixer_bwd_kernel(
        proj, rope_tab, dcat, w_spatial[0], w_spatial[0].transpose(0, 2, 1), bias_full, sink_rows,
        comm=_scatter_job([cs_ff2, cs_out]))
    dw_in, sc_ff1 = _weight_grad_kernel(hb, dproj, "dw_in", 1024, 896, comm=_scatter_job([cs_ff1]))
    grad_x, sums1 = _in_proj_bwd_kernel(x2, dx1, dproj, vecs1, w_in_full)
    p_in = halves_of("w_in", dw_in.reshape(D_MODEL, N_CHIPS, IN_PROJ_WIDTH // N_CHIPS).transpose(1, 0, 2))
    q_in, = _run_comm(_sibling_job([p_in]), "grad_to_sibling_w_in")
    cs_in = _add_halves_kernel(p_in, q_in, c_idx, "grad_add_w_in")
    sc_in, = _run_comm(_scatter_job([cs_in]), "grad_to_chips_w_in")

    names = ["w_in", "w_out", "w_ff1", "w_ff2"]
    totals = [_sum_chips_kernel(own, oth, place, "grad_sum_" + nm)
              for nm, own, oth in zip(names, [cs_in, cs_out, cs_ff1, cs_ff2], [sc_in, sc_out, sc_ff1, sc_ff2])]
    shared = _sibling_share(totals, "grad_share")
    big_out = {}
    for nm, g in zip(names, shared):
        w, m, v = big[nm]
        g = g.reshape(w.shape[1:])
        d, mn, vn = _adam_kernel(w[0], g, m[0], v[0], "adam_" + nm)
        big_out[nm] = tuple(t[None] for t in (g, d, mn, vn))

    small = {"b_ada": (b_ada, m_b_ada, v_b_ada), "g_mix": (g_mix, m_g_mix, v_g_mix),
             "g_ffn": (g_ffn, m_g_ffn, v_g_ffn), "g_final": (g_final, m_g_final, v_g_final),
             "b_spatial": (b_spatial, m_b_spatial, v_b_spatial), "sinks": (sinks, m_sinks, v_sinks),
             "w_spatial": (w_spatial, m_w_spatial, v_w_spatial)}
    flat_shape = {"g_final": (1, D_MODEL), "b_spatial": (GMLP_GROUPS, CHUNK), "w_spatial": (GMLP_GROUPS * CHUNK, CHUNK)}
    gathered = _all_gather8([sums1, sums2, db_lanes, dsink_rows, dw_spatial.reshape(GMLP_GROUPS * CHUNK, CHUNK)],
                            "gather_small")
    small_out, loss_tile = _small_update_kernel(
        gathered, {nm: tuple(a.reshape(flat_shape.get(nm, a.shape)) for a in small[nm]) for nm in small})
    small_out = {nm: [o.reshape(small[nm][0].shape) for o in small_out[nm]] for nm in small}
    loss = loss_tile[0, 0]

    g1, g2 = gathered[0], gathered[1]
    dmod_all = jnp.concatenate([g1[:, 0], g1[:, 1], g2[:, 5], g2[:, 0], g2[:, 1], g2[:, 2]], axis=1)
    dmod_cols = lax.dynamic_slice(dmod_all, (0, chip * ada_cols), (N_DEV, ada_cols))
    ada = _ada_update_kernel(act.T, dmod_cols, w_ada[0], m_w_ada[0], v_w_ada[0])
    big_out["w_ada"] = tuple(t[None] for t in ada)

    order = ["w_ada", "b_ada", "g_mix", "w_in", "w_spatial", "b_spatial", "sinks", "w_out", "g_ffn",
             "w_ff1", "w_ff2", "g_final"]

    def leaf(nm, k):
        return big_out[nm][k] if nm in big_out else small_out[nm][k]

    outs = [loss, grad_x[None]]
    for k in range(4):
        outs += [leaf(nm, k) for nm in order]
    return tuple(outs)
```

```python
import math
from typing import Callable, NamedTuple

import jax
import jax.numpy as jnp
from jax import lax
from jax.experimental import pallas as pl
from jax.experimental.pallas import tpu as pltpu

F32 = jnp.float32
MXU_DTYPE = jnp.bfloat16
WEIGHT_COMM_DTYPE = jnp.bfloat16
GRAD_COMM_DTYPE = jnp.bfloat16

D_MODEL = 1024
D_FF = 4096
HEAD_DIM = 64
GMLP_GROUPS = 8
GMLP_WIDTH = 512
CHUNK = 128
N_Q_HEADS = 8
N_KV_HEADS = 2
ATTN_WIDTH = 512
KV_WIDTH = 128
ROT_DIM = 16
ROPE_THETA = 500000.0
IN_PROJ_WIDTH = 1792
N_MOD = 6
EPS = 1e-5
N_CHIPS = 4
N_DEV = 8
LANES = 128
W_IN_BLOCK = IN_PROJ_WIDTH // N_CHIPS
W_IN_PADDED = -(-W_IN_BLOCK // LANES) * LANES

ADAM_LR = 0.001
ADAM_B1 = 0.9
ADAM_B2 = 0.999
ADAM_EPS = 1e-08
ADAM_WD = 0.01
ADAM_STEP = 10

VMEM_LIMIT_BYTES = 58 * 1024 * 1024
MESH = pl.DeviceIdType.MESH


def _params(*semantics):
    return pltpu.CompilerParams(dimension_semantics=semantics, vmem_limit_bytes=VMEM_LIMIT_BYTES)


def _dot(a, b):
    return jnp.dot(a.astype(MXU_DTYPE), b.astype(MXU_DTYPE), preferred_element_type=F32)


def _dot_nt(a, b):
    return lax.dot_general(a.astype(MXU_DTYPE), b.astype(MXU_DTYPE), (((1,), (1,)), ((), ())),
                           preferred_element_type=F32)


def _dot_tn(a, b):
    return lax.dot_general(a.astype(MXU_DTYPE), b.astype(MXU_DTYPE), (((0,), (0,)), ((), ())),
                           preferred_element_type=F32)


def _full(shape):
    return pl.BlockSpec(shape, lambda *_: (0,) * len(shape))


def _any():
    return pl.BlockSpec(memory_space=pl.ANY)


def _rowsum(v):
    return jnp.sum(v, axis=0, keepdims=True)


def _mean_last(v):
    return jnp.mean(v, axis=-1, keepdims=True)


class _Comm(NamedTuple):
    operands: tuple
    out_shapes: tuple
    n_sems: int
    make: Callable


def _hosted_call(body, comm, *, name, grid, in_specs, out_shape, out_specs, scratch_shapes=(), semantics):
    if comm is None:
        return pl.pallas_call(body, name=name, grid=grid, in_specs=in_specs, out_shape=out_shape,
                              out_specs=out_specs, scratch_shapes=list(scratch_shapes),
                              compiler_params=_params(*semantics))
    n_in, n_out, n_scr = len(in_specs), len(out_shape), len(scratch_shapes)
    k_in, k_out = len(comm.operands), len(comm.out_shapes)

    def hosted(*refs):
        ins, refs = refs[:n_in], refs[n_in:]
        c_ins, refs = refs[:k_in], refs[k_in:]
        outs, refs = refs[:n_out], refs[n_out:]
        c_outs, refs = refs[:k_out], refs[k_out:]
        scratch, (send_sems, recv_sems) = refs[:n_scr], refs[n_scr:]
        first, last = None, None
        for d, size in enumerate(grid):
            at_start, at_end = pl.program_id(d) == 0, pl.program_id(d) == size - 1
            first = at_start if first is None else first & at_start
            last = at_end if last is None else last & at_end

        @pl.when(first)
        def _():
            for cp in comm.make(c_ins, c_outs, send_sems, recv_sems)[0]:
                cp.start()

        body(*ins, *outs, *scratch)

        @pl.when(last)
        def _():
            for wait in comm.make(c_ins, c_outs, send_sems, recv_sems)[1]:
                wait()

    call = pl.pallas_call(
        hosted, name=name, grid=grid, in_specs=list(in_specs) + [_any()] * k_in,
        out_shape=list(out_shape) + list(comm.out_shapes), out_specs=list(out_specs) + [_any()] * k_out,
        scratch_shapes=list(scratch_shapes) + [pltpu.SemaphoreType.DMA((comm.n_sems,)),
                                                pltpu.SemaphoreType.DMA((comm.n_sems,))],
        compiler_params=_params(*semantics))
    return lambda *args: call(*args, *comm.operands)


def _mesh_place():
    x, y, c = lax.axis_index("x"), lax.axis_index("y"), lax.axis_index("c")
    return x, y, c, [(1 - x, y), (x, 1 - y), (1 - x, 1 - y)]


def _gather_job(halves):
    per = 5

    def make(ins, outs, send_sems, recv_sems):
        x, y, c, chips = _mesh_place()
        starts, waits = [], []
        for a, (src, out) in enumerate(zip(ins, outs)):
            mine = out.at[4 * x + 2 * y + c]
            local = pltpu.make_async_copy(src.at[c], mine, send_sems.at[per * a + 4])
            to = [(x, y, 1 - c)] + [(px, py, c) for px, py in chips]
            sends = [pltpu.make_async_remote_copy(
                src_ref=src.at[c], dst_ref=mine, send_sem=send_sems.at[per * a + k],
                recv_sem=recv_sems.at[per * a + k], device_id=dev, device_id_type=MESH)
                for k, dev in enumerate(to)]
            recvs = [pltpu.make_async_remote_copy(
                src_ref=src.at[c], dst_ref=out.at[4 * px + 2 * py + pc], send_sem=send_sems.at[per * a + k],
                recv_sem=recv_sems.at[per * a + k], device_id=(px, py, pc), device_id_type=MESH)
                for k, (px, py, pc) in enumerate(to)]
            starts += [local] + sends
            waits += [local.wait] + [s.wait_send for s in sends] + [r.wait_recv for r in recvs]
        return starts, waits

    return _Comm(tuple(halves), tuple(jax.ShapeDtypeStruct((N_DEV,) + h.shape[1:], h.dtype) for h in halves),
                 per * len(halves), make)


def _gather_forward(bufs, name):
    n_arr = len(bufs)

    def body(*refs):
        outs = refs[n_arr:2 * n_arr]
        send_sems, recv_sems = refs[2 * n_arr:]
        x, y, c, chips = _mesh_place()
        sends, recvs = [], []
        for a, buf in enumerate(outs):
            for j, (px, py) in enumerate(chips):
                mine, theirs = buf.at[4 * px + 2 * py + c], buf.at[4 * px + 2 * py + 1 - c]
                sems = dict(send_sem=send_sems.at[3 * a + j], recv_sem=recv_sems.at[3 * a + j],
                            device_id=(x, y, 1 - c), device_id_type=MESH)
                sends.append(pltpu.make_async_remote_copy(src_ref=mine, dst_ref=mine, **sems))
                recvs.append(pltpu.make_async_remote_copy(src_ref=mine, dst_ref=theirs, **sems))
        for cp in sends:
            cp.start()
        for s, r in zip(sends, recvs):
            s.wait_send()
            r.wait_recv()

    return pl.pallas_call(
        body, name=name, out_shape=[jax.ShapeDtypeStruct(b.shape, b.dtype) for b in bufs],
        in_specs=[_any()] * n_arr, out_specs=[_any()] * n_arr,
        input_output_aliases={a: a for a in range(n_arr)},
        scratch_shapes=[pltpu.SemaphoreType.DMA((3 * n_arr,)), pltpu.SemaphoreType.DMA((3 * n_arr,))],
    )(*bufs)


def _sibling_job(parts):
    def make(ins, outs, send_sems, recv_sems):
        x, y, c, _ = _mesh_place()
        copies = [pltpu.make_async_remote_copy(
            src_ref=src.at[k, 1 - c], dst_ref=out.at[k], send_sem=send_sems.at[N_CHIPS * a + k],
            recv_sem=recv_sems.at[N_CHIPS * a + k], device_id=(x, y, 1 - c), device_id_type=MESH)
            for a, (src, out) in enumerate(zip(ins, outs)) for k in range(N_CHIPS)]
        return copies, [cp.wait for cp in copies]

    return _Comm(tuple(parts), tuple(jax.ShapeDtypeStruct((N_CHIPS,) + p.shape[2:], p.dtype) for p in parts),
                 N_CHIPS * len(parts), make)


def _scatter_job(chip_sums):
    def make(ins, outs, send_sems, recv_sems):
        x, y, c, chips = _mesh_place()
        copies = [pltpu.make_async_remote_copy(
            src_ref=src.at[2 * px + py], dst_ref=out.at[j], send_sem=send_sems.at[3 * a + j],
            recv_sem=recv_sems.at[3 * a + j], device_id=(px, py, c), device_id_type=MESH)
            for a, (src, out) in enumerate(zip(ins, outs)) for j, (px, py) in enumerate(chips)]
        return copies, [cp.wait for cp in copies]

    return _Comm(tuple(chip_sums), tuple(jax.ShapeDtypeStruct((3,) + s.shape[1:], s.dtype) for s in chip_sums),
                 3 * len(chip_sums), make)


def _run_comm(comm, name):
    def body(token_ref):
        token_ref[...] = jnp.zeros_like(token_ref)

    out = _hosted_call(body, comm, name=name, grid=(1,), in_specs=[],
                       out_shape=[jax.ShapeDtypeStruct((8, LANES), F32)], out_specs=[_full((8, LANES))],
                       semantics=("arbitrary",))()
    return out[1:]


def _all_gather8(blocks, name, split=False):
    n_arr = len(blocks)

    def body(*refs):
        x_refs, out_refs = refs[:n_arr], refs[n_arr:2 * n_arr]
        send_sems, recv_sems, local_sems = refs[2 * n_arr:]
        x, y, c, chips = _mesh_place()
        me, sibling = (x, y, c), (x, y, 1 - c)
        arrays = []
        for a, (x_ref, out_ref) in enumerate(zip(x_refs, out_refs)):
            src_mine = x_ref.at[c] if split else x_ref

            def copy(k, blk, to, src=None, a=a, out_ref=out_ref):
                dst = out_ref.at[4 * blk[0] + 2 * blk[1] + blk[2]]
                return pltpu.make_async_remote_copy(
                    src_ref=dst if src is None else src, dst_ref=dst,
                    send_sem=send_sems.at[7 * a + k], recv_sem=recv_sems.at[7 * a + k],
                    device_id=to, device_id_type=MESH)

            mine = pltpu.make_async_copy(src_mine, out_ref.at[4 * x + 2 * y + c], local_sems.at[a])
            mine.start()
            first = [copy(0, me, sibling, src=src_mine)]
            first += [copy(1 + j, me, (*chip, c), src=src_mine) for j, chip in enumerate(chips)]
            for cp in first:
                cp.start()
            arrays.append((copy, mine, first))
        sent = []
        for copy, mine, first in arrays:
            passed = [copy(4 + j, (*chip, c), sibling) for j, chip in enumerate(chips)]
            for j, chip in enumerate(chips):
                copy(1 + j, (*chip, c), me).wait_recv()
                passed[j].start()
            sent += first + passed
        for copy, mine, first in arrays:
            copy(0, sibling, me).wait_recv()
            for j, chip in enumerate(chips):
                copy(4 + j, (*chip, 1 - c), me).wait_recv()
            mine.wait()
        for cp in sent:
            cp.wait_send()

    return pl.pallas_call(
        body, name=name,
        out_shape=[jax.ShapeDtypeStruct((N_DEV,) + tuple(b.shape[1:] if split else b.shape), b.dtype)
                   for b in blocks],
        in_specs=[_any()] * n_arr, out_specs=[_any()] * n_arr,
        scratch_shapes=[pltpu.SemaphoreType.DMA((7 * n_arr,)), pltpu.SemaphoreType.DMA((7 * n_arr,)),
                        pltpu.SemaphoreType.DMA((n_arr,))],
    )(*blocks)


def _sibling_share(bufs, name):
    n_arr = len(bufs)

    def body(*refs):
        out_refs = refs[n_arr:2 * n_arr]
        send_sems, recv_sems = refs[2 * n_arr:]
        x, y, c = lax.axis_index("x"), lax.axis_index("y"), lax.axis_index("c")
        copies = [pltpu.make_async_remote_copy(
            src_ref=out_refs[a].at[c], dst_ref=out_refs[a].at[c],
            send_sem=send_sems.at[a], recv_sem=recv_sems.at[a],
            device_id=(x, y, 1 - c), device_id_type=MESH) for a in range(n_arr)]
        for cp in copies:
            cp.start()
        for a in range(n_arr):
            pltpu.make_async_remote_copy(
                src_ref=out_refs[a].at[c], dst_ref=out_refs[a].at[1 - c],
                send_sem=send_sems.at[a], recv_sem=recv_sems.at[a],
                device_id=(x, y, 1 - c), device_id_type=MESH).wait()

    return pl.pallas_call(
        body, name=name,
        out_shape=[jax.ShapeDtypeStruct(b.shape, b.dtype) for b in bufs],
        in_specs=[_any()] * n_arr, out_specs=[_any()] * n_arr,
        input_output_aliases={a: a for a in range(n_arr)},
        scratch_shapes=[pltpu.SemaphoreType.DMA((n_arr,)), pltpu.SemaphoreType.DMA((n_arr,))],
    )(*bufs)


def _gelu_tanh(z):
    k = math.sqrt(2.0 / math.pi)
    t = jnp.tanh(k * (z + 0.044715 * (z * z * z)))
    return 0.5 * z * (1.0 + t), t


def _gelu_tanh_grad(z, t):
    k = math.sqrt(2.0 / math.pi)
    return 0.5 * (1.0 + t) + 0.5 * z * (1.0 - t * t) * (k * (1.0 + 3.0 * 0.044715 * (z * z)))


def _rope_angle_kernel(pos_row, invf_col):
    seq = pos_row.shape[1]

    def body(p_ref, f_ref, cos_ref, sin_ref):
        ang = p_ref[...].astype(F32) * f_ref[...]
        cos_ref[...] = jnp.cos(ang)
        sin_ref[...] = jnp.sin(ang)

    return pl.pallas_call(
        body, name="rope_angles", grid=(1,), out_shape=[jax.ShapeDtypeStruct((ROT_DIM // 2, seq), F32)] * 2,
        in_specs=[_full((1, seq)), _full((ROT_DIM // 2, 1))], out_specs=[_full((ROT_DIM // 2, seq))] * 2,
        compiler_params=_params("arbitrary"),
    )(pos_row, invf_col)


def _rope_lane_tables(cos, sin):
    cos_t, sin_t = cos.T, sin.T
    seq, half = cos_t.shape
    ones = jnp.ones((seq, HEAD_DIM - ROT_DIM), F32)
    c64 = jnp.concatenate([cos_t, cos_t, ones], axis=1)
    s1 = jnp.concatenate([sin_t, jnp.zeros((seq, HEAD_DIM - half), F32)], axis=1)
    s2 = jnp.concatenate([jnp.zeros((seq, half), F32), sin_t, jnp.zeros((seq, HEAD_DIM - ROT_DIM), F32)], axis=1)
    return jnp.concatenate([jnp.tile(t, (1, LANES // HEAD_DIM)) for t in (c64, s1, s2)], axis=1)


def _rope_apply(t, tab, sign):
    reps = t.shape[1] // LANES
    c_tab, s1, s2 = (jnp.tile(tab[:, LANES * k:LANES * (k + 1)], (1, reps)) if reps > 1
                     else tab[:, LANES * k:LANES * (k + 1)] for k in range(3))
    half = ROT_DIM // 2
    up = pltpu.roll(t, t.shape[1] - half, 1)
    down = pltpu.roll(t, half, 1)
    return t * c_tab + sign * (down * s2 - up * s1)


def _lane_masks(shape):
    lane = lax.broadcasted_iota(jnp.int32, shape, 1)
    return lane < HEAD_DIM, lane >= HEAD_DIM


HEADS_PER_GROUP = N_Q_HEADS // N_KV_HEADS
ATTN_SCALE = 1.0 / math.sqrt(HEAD_DIM)


def _attn_bias_t(first_block):
    kj = lax.broadcasted_iota(jnp.int32, (2 * CHUNK, CHUNK), 0)
    qi = lax.broadcasted_iota(jnp.int32, (2 * CHUNK, CHUNK), 1)
    ok = (kj > qi) & (kj <= qi + CHUNK) & (jnp.logical_not(first_block) | (kj >= CHUNK))
    return jnp.tile(jnp.where(ok, 0.0, -jnp.inf), (1, HEADS_PER_GROUP))


def _group_rows(x, g, lo, hi):
    rows = []
    for r in range(HEADS_PER_GROUP):
        h = HEADS_PER_GROUP * g + r
        pair = x[:, LANES * (h // 2):LANES * (h // 2 + 1)]
        rows.append(jnp.where(hi if h % 2 else lo, pair, 0.0))
    return jnp.concatenate(rows, axis=0)


def _pairs_from_rows(rows, lo):
    return [jnp.where(lo, rows[2 * CHUNK * k:2 * CHUNK * k + CHUNK], rows[2 * CHUNK * k + CHUNK:2 * CHUNK * (k + 1)])
            for k in range(HEADS_PER_GROUP // 2)]


def _group_dup(a, b, g, lo2):
    return jnp.where(lo2, a, b) if g == 0 else jnp.where(lo2, b, a)


def _sink_row(sink_ref, g):
    return jnp.concatenate([sink_ref[HEADS_PER_GROUP * g + r:HEADS_PER_GROUP * g + r + 1, :]
                            for r in range(HEADS_PER_GROUP)], axis=1)


def _attn_probs_t(k_dup, q_rows, bias_t, sink_row):
    s_t = _dot_nt(k_dup, q_rows) * ATTN_SCALE + bias_t
    m = jnp.maximum(jnp.max(s_t, axis=0, keepdims=True), sink_row)
    p = jnp.exp(s_t - m)
    e_sink = jnp.exp(sink_row - m)
    inv = 1.0 / (jnp.sum(p, axis=0, keepdims=True) + e_sink)
    return p * inv, e_sink * inv


def _sgu_forward_pair(wm, vp, j):
    lo, hi = _lane_masks(vp.shape)
    lhs = jnp.concatenate([wm[2 * j], wm[2 * j + 1]], axis=1)
    rhs = jnp.concatenate([jnp.where(lo, vp, 0.0), jnp.where(hi, vp, 0.0)], axis=0)
    return _dot(lhs, rhs)


def _masked_spatial(w_ref):
    t = lax.broadcasted_iota(jnp.int32, (CHUNK, CHUNK), 0)
    s = lax.broadcasted_iota(jnp.int32, (CHUNK, CHUNK), 1)
    tril = s <= t
    return [jnp.where(tril, w_ref[g], 0.0) for g in range(GMLP_GROUPS)], tril, s >= t


def _mod_kernel(c_all, w_shard, b_shard):
    n = w_shard.shape[1]
    tn = 512

    def body(c_ref, w_ref, b_ref, mod_ref, act_ref):
        cv = c_ref[...]
        act = cv * (1.0 / (1.0 + jnp.exp(-cv)))
        act_ref[...] = act
        mod_ref[...] = _dot(act, w_ref[...]) + b_ref[...]

    return pl.pallas_call(
        body, name="ada_mod", grid=(n // tn,),
        out_shape=[jax.ShapeDtypeStruct((N_DEV, n), F32), jax.ShapeDtypeStruct((N_DEV, D_MODEL), F32)],
        in_specs=[_full((N_DEV, D_MODEL)), pl.BlockSpec((D_MODEL, tn), lambda i: (0, i)),
                  pl.BlockSpec((1, tn), lambda i: (0, i))],
        out_specs=[pl.BlockSpec((N_DEV, tn), lambda i: (0, i)), _full((N_DEV, D_MODEL))],
        compiler_params=_params("arbitrary"),
    )(c_all, w_shard, b_shard)


def _assemble_w_in(blocks_ref, w_scr):
    pad = W_IN_PADDED - W_IN_BLOCK
    keep = W_IN_PADDED - LANES
    lo = lax.broadcasted_iota(jnp.int32, (D_MODEL, LANES), 1) < pad
    for p in range(N_CHIPS // 2):
        base = 2 * W_IN_BLOCK * p
        first = blocks_ref[2 * p].astype(F32)
        second = pltpu.roll(blocks_ref[2 * p + 1].astype(F32), pad, 1)
        w_scr[:, base:base + keep] = first[:, 0:keep].astype(w_scr.dtype)
        w_scr[:, base + keep:base + W_IN_PADDED] = jnp.where(lo, first[:, keep:], second[:, 0:LANES]).astype(w_scr.dtype)
        w_scr[:, base + W_IN_PADDED:base + 2 * W_IN_BLOCK] = second[:, LANES:].astype(w_scr.dtype)


def _in_proj_kernel(x, vecs, w_in_blocks, comm=None):
    seq = x.shape[0]
    tm = 512

    def body(x_ref, v_ref, wb_ref, proj_ref, h_ref, w_ref):
        @pl.when(pl.program_id(0) == 0)
        def _():
            _assemble_w_in(wb_ref, w_ref)

        xv = x_ref[...]
        rstd = lax.rsqrt(_mean_last(xv * xv) + EPS)
        n1 = (xv * rstd) * v_ref[0:1, :]
        h = n1 * (1.0 + v_ref[2:3, :]) + v_ref[1:2, :]
        hb = h.astype(MXU_DTYPE)
        h_ref[...] = hb
        proj_ref[...] = _dot(hb, w_ref[...])

    return _hosted_call(
        body, comm, name="in_proj", grid=(seq // tm,),
        out_shape=[jax.ShapeDtypeStruct((seq, IN_PROJ_WIDTH), F32),
                   jax.ShapeDtypeStruct((seq, D_MODEL), MXU_DTYPE)],
        in_specs=[pl.BlockSpec((tm, D_MODEL), lambda i: (i, 0)), _full((8, D_MODEL)),
                  _full((N_CHIPS, D_MODEL, W_IN_PADDED))],
        out_specs=[pl.BlockSpec((tm, IN_PROJ_WIDTH), lambda i: (i, 0)),
                   pl.BlockSpec((tm, D_MODEL), lambda i: (i, 0))],
        scratch_shapes=[pltpu.VMEM((D_MODEL, IN_PROJ_WIDTH), MXU_DTYPE)],
        semantics=("arbitrary",),
    )(x, vecs, w_in_blocks)


def _mixer_fwd_kernel(proj, rope_tab, w_spatial, bias_full, sink_rows, comm=None):
    seq = proj.shape[0]
    nb = seq // CHUNK
    kv_col = (2 * GMLP_WIDTH + ATTN_WIDTH) // (2 * KV_WIDTH)

    def body(proj_ref, prev_ref, tab_ref, ptab_ref, w_ref, bias_ref, sink_ref, cat_ref):
        i = pl.program_id(0)
        wm, _, _ = _masked_spatial(w_ref)
        for j in range(GMLP_GROUPS // 2):
            cols = slice(LANES * j, LANES * (j + 1))
            vcols = slice(GMLP_WIDTH + LANES * j, GMLP_WIDTH + LANES * (j + 1))
            u, _ = _gelu_tanh(proj_ref[:, cols])
            vp, _ = _gelu_tanh(proj_ref[:, vcols])
            sv = _sgu_forward_pair(wm, vp, j) + bias_ref[:, cols]
            cat_ref[:, cols] = (u * sv).astype(cat_ref.dtype)
        o = 2 * GMLP_WIDTH
        tab = tab_ref[...]
        q_r = _rope_apply(proj_ref[:, o:o + ATTN_WIDTH], tab, 1.0)
        k_cur = _rope_apply(proj_ref[:, o + ATTN_WIDTH:o + ATTN_WIDTH + KV_WIDTH], tab, 1.0)
        k_prev = _rope_apply(prev_ref[:, 0:KV_WIDTH], ptab_ref[...], 1.0)
        k_a = jnp.concatenate([k_prev, k_cur], axis=0)
        v_a = jnp.concatenate([prev_ref[:, KV_WIDTH:2 * KV_WIDTH],
                               proj_ref[:, o + ATTN_WIDTH + KV_WIDTH:o + ATTN_WIDTH + 2 * KV_WIDTH]], axis=0)
        k_b = pltpu.roll(k_a, HEAD_DIM, 1)
        v_b = pltpu.roll(v_a, HEAD_DIM, 1)
        bias_t = _attn_bias_t(i == 0)
        lo, hi = _lane_masks((CHUNK, LANES))
        lo2, _ = _lane_masks((2 * CHUNK, LANES))
        for g in range(N_KV_HEADS):
            p_t, _ = _attn_probs_t(_group_dup(k_a, k_b, g, lo2), _group_rows(q_r, g, lo, hi), bias_t,
                                   _sink_row(sink_ref, g))
            o_t = _dot(_group_dup(v_a, v_b, g, lo2).T, p_t)
            for k, pair in enumerate(_pairs_from_rows(o_t.T, lo)):
                c0 = GMLP_WIDTH + LANES * (2 * g + k)
                cat_ref[:, c0:c0 + LANES] = pair.astype(cat_ref.dtype)

    return _hosted_call(
        body, comm, name="mixer_fwd", grid=(nb,),
        out_shape=[jax.ShapeDtypeStruct((seq, D_MODEL), MXU_DTYPE)],
        in_specs=[pl.BlockSpec((CHUNK, IN_PROJ_WIDTH), lambda i: (i, 0)),
                  pl.BlockSpec((CHUNK, 2 * KV_WIDTH), lambda i: (jnp.maximum(i - 1, 0), kv_col)),
                  pl.BlockSpec((CHUNK, 3 * LANES), lambda i: (i, 0)),
                  pl.BlockSpec((CHUNK, 3 * LANES), lambda i: (jnp.maximum(i - 1, 0), 0)),
                  _full((GMLP_GROUPS, CHUNK, CHUNK)), _full((CHUNK, GMLP_WIDTH)),
                  _full((N_Q_HEADS, LANES))],
        out_specs=[pl.BlockSpec((CHUNK, D_MODEL), lambda i: (i, 0))],
        semantics=("arbitrary",),
    )(proj, proj, rope_tab, rope_tab, w_spatial, bias_full, sink_rows)


def _trunk_kernel(x, target, cat, vecs, w_out, w_ff1, w_ff2):
    seq = x.shape[0]
    tm = 256
    nj = D_FF // D_MODEL

    def body(x_ref, t_ref, cat_ref, v_ref, wout_hbm, w1_hbm, w2_hbm,
             dx1_ref, dcat_ref, dmix_ref, h2_ref, r_ref, da_ref, dff_ref, sums_ref,
             wout, w1, w2, a_scr, sem):
        i = pl.program_id(0)

        @pl.when(i == 0)
        def _():
            copies = [pltpu.make_async_copy(wout_hbm, wout, sem.at[0]),
                      pltpu.make_async_copy(w1_hbm, w1, sem.at[1]),
                      pltpu.make_async_copy(w2_hbm, w2, sem.at[2])]
            for cp in copies:
                cp.start()
            for cp in copies:
                cp.wait()
            sums_ref[...] = jnp.zeros_like(sums_ref)

        gate1, shift2, scale2 = v_ref[0:1, :], v_ref[1:2, :], v_ref[2:3, :]
        gate2, g_ffn, g_final = v_ref[3:4, :], v_ref[4:5, :], v_ref[5:6, :]

        mix = _dot(cat_ref[...], wout[...])
        x1 = x_ref[...] + gate1 * mix
        rstd2 = lax.rsqrt(_mean_last(x1 * x1) + EPS)
        xh2 = x1 * rstd2
        n2 = xh2 * g_ffn
        h2b = (n2 * (1.0 + scale2) + shift2).astype(MXU_DTYPE)
        h2_ref[...] = h2b
        ff = jnp.zeros((tm, D_MODEL), F32)
        for j in range(nj):
            a = _dot(h2b, w1[j])
            a_scr[j] = a
            relu = jnp.maximum(a, 0.0)
            rb = (relu * relu).astype(MXU_DTYPE)
            r_ref[:, D_MODEL * j:D_MODEL * (j + 1)] = rb
            ff = ff + _dot(rb, w2[j])
        x2 = x1 + gate2 * ff
        rstd3 = lax.rsqrt(_mean_last(x2 * x2) + EPS)
        xh3 = x2 * rstd3
        err = xh3 * g_final - t_ref[...]
        loss = 0.5 * _rowsum(_mean_last(err * err))
        dy = err * (1.0 / D_MODEL)
        dxh3 = dy * g_final
        dx2 = rstd3 * (dxh3 - xh3 * _mean_last(dxh3 * xh3))
        dffb = (dx2 * gate2).astype(MXU_DTYPE)
        dff_ref[...] = dffb
        dh2 = jnp.zeros((tm, D_MODEL), F32)
        for j in range(nj):
            dr = _dot_nt(dffb, w2[j])
            dab = (dr * (2.0 * jnp.maximum(a_scr[j], 0.0))).astype(MXU_DTYPE)
            da_ref[:, D_MODEL * j:D_MODEL * (j + 1)] = dab
            dh2 = dh2 + _dot_nt(dab, w1[j])
        dn2 = dh2 * (1.0 + scale2)
        dxh2 = dn2 * g_ffn
        dx1 = dx2 + rstd2 * (dxh2 - xh2 * _mean_last(dxh2 * xh2))
        dx1_ref[...] = dx1
        dmixb = (dx1 * gate1).astype(MXU_DTYPE)
        dmix_ref[...] = dmixb
        dcat_ref[...] = _dot_nt(dmixb, wout[...])

        sums_ref[0:1, :] += _rowsum(dh2)
        sums_ref[1:2, :] += _rowsum(dh2 * n2)
        sums_ref[2:3, :] += _rowsum(dx2 * ff)
        sums_ref[3:4, :] += _rowsum(dn2 * xh2)
        sums_ref[4:5, :] += _rowsum(dy * xh3)
        sums_ref[5:6, :] += _rowsum(dx1 * mix)
        sums_ref[6:7, :] += jnp.broadcast_to(loss, (1, D_MODEL))

    tok = lambda w: pl.BlockSpec((tm, w), lambda i: (i, 0))
    return pl.pallas_call(
        body, name="trunk", grid=(seq // tm,),
        out_shape=[jax.ShapeDtypeStruct((seq, D_MODEL), F32), jax.ShapeDtypeStruct((seq, D_MODEL), F32),
                   jax.ShapeDtypeStruct((seq, D_MODEL), MXU_DTYPE), jax.ShapeDtypeStruct((seq, D_MODEL), MXU_DTYPE),
                   jax.ShapeDtypeStruct((seq, D_FF), MXU_DTYPE), jax.ShapeDtypeStruct((seq, D_FF), MXU_DTYPE),
                   jax.ShapeDtypeStruct((seq, D_MODEL), MXU_DTYPE), jax.ShapeDtypeStruct((8, D_MODEL), F32)],
        in_specs=[tok(D_MODEL), tok(D_MODEL), tok(D_MODEL), _full((8, D_MODEL)), _any(), _any(), _any()],
        out_specs=[tok(D_MODEL), tok(D_MODEL), tok(D_MODEL), tok(D_MODEL), tok(D_FF), tok(D_FF), tok(D_MODEL),
                   _full((8, D_MODEL))],
        scratch_shapes=[pltpu.VMEM((D_MODEL, D_MODEL), MXU_DTYPE), pltpu.VMEM((nj, D_MODEL, D_MODEL), MXU_DTYPE),
                        pltpu.VMEM((nj, D_MODEL, D_MODEL), MXU_DTYPE), pltpu.VMEM((nj, tm, D_MODEL), F32),
                        pltpu.SemaphoreType.DMA((3,))],
        compiler_params=_params("arbitrary"),
    )(x, target, cat, vecs, w_out, w_ff1, w_ff2)


def _mixer_bwd_kernel(proj, rope_tab, dcat, w_spatial, w_spatial_t, bias_full, sink_rows, comm=None):
    seq = proj.shape[0]
    nb = seq // CHUNK
    kv_col = (2 * GMLP_WIDTH + ATTN_WIDTH) // (2 * KV_WIDTH)

    def body(proj_ref, prev_ref, tab_ref, ptab_ref, dcat_ref, w_ref, wt_ref, bias_ref, sink_ref,
             dproj_ref, dw_ref, db_ref, dsink_ref, carry):
        step = pl.program_id(0)
        blk = nb - 1 - step

        @pl.when(step == 0)
        def _():
            carry[...] = jnp.zeros_like(carry)
            dw_ref[...] = jnp.zeros_like(dw_ref)
            db_ref[...] = jnp.zeros_like(db_ref)
            dsink_ref[...] = jnp.zeros_like(dsink_ref)

        wm, tril, triu = _masked_spatial(w_ref)
        lo, hi = _lane_masks((CHUNK, LANES))
        lane = lax.broadcasted_iota(jnp.int32, (CHUNK, LANES), 1)
        db = jnp.zeros((CHUNK, LANES), F32)
        for j in range(GMLP_GROUPS // 2):
            cols = slice(LANES * j, LANES * (j + 1))
            vcols = slice(GMLP_WIDTH + LANES * j, GMLP_WIDTH + LANES * (j + 1))
            zu, zv = proj_ref[:, cols], proj_ref[:, vcols]
            u, tu = _gelu_tanh(zu)
            vp, tv = _gelu_tanh(zv)
            sv = _sgu_forward_pair(wm, vp, j) + bias_ref[:, cols]
            dout = dcat_ref[:, cols]
            du = dout * sv
            dsv = dout * u
            dsv_lo, dsv_hi = jnp.where(lo, dsv, 0.0), jnp.where(hi, dsv, 0.0)
            lhs_t = jnp.concatenate([jnp.where(triu, wt_ref[2 * j], 0.0),
                                     jnp.where(triu, wt_ref[2 * j + 1], 0.0)], axis=1)
            dv = _dot(lhs_t, jnp.concatenate([dsv_lo, dsv_hi], axis=0))
            dw_ref[2 * j] += jnp.where(tril, _dot_nt(dsv_lo, vp), 0.0)
            dw_ref[2 * j + 1] += jnp.where(tril, _dot_nt(dsv_hi, vp), 0.0)
            db = db + (jnp.where(lane == 2 * j, jnp.sum(dsv_lo, axis=1, keepdims=True), 0.0)
                       + jnp.where(lane == 2 * j + 1, jnp.sum(dsv_hi, axis=1, keepdims=True), 0.0))
            dproj_ref[:, cols] = (du * _gelu_tanh_grad(zu, tu)).astype(dproj_ref.dtype)
            dproj_ref[:, vcols] = (dv * _gelu_tanh_grad(zv, tv)).astype(dproj_ref.dtype)
        db_ref[...] += db
        o = 2 * GMLP_WIDTH
        tab = tab_ref[...]
        q_r = _rope_apply(proj_ref[:, o:o + ATTN_WIDTH], tab, 1.0)
        k_cur = _rope_apply(proj_ref[:, o + ATTN_WIDTH:o + ATTN_WIDTH + KV_WIDTH], tab, 1.0)
        k_prev = _rope_apply(prev_ref[:, 0:KV_WIDTH], ptab_ref[...], 1.0)
        k_a = jnp.concatenate([k_prev, k_cur], axis=0)
        v_a = jnp.concatenate([prev_ref[:, KV_WIDTH:2 * KV_WIDTH],
                               proj_ref[:, o + ATTN_WIDTH + KV_WIDTH:o + ATTN_WIDTH + 2 * KV_WIDTH]], axis=0)
        k_b = pltpu.roll(k_a, HEAD_DIM, 1)
        v_b = pltpu.roll(v_a, HEAD_DIM, 1)
        bias_t = _attn_bias_t(blk == 0)
        lo2, _ = _lane_masks((2 * CHUNK, LANES))
        dout_b = dcat_ref[:, GMLP_WIDTH:GMLP_WIDTH + ATTN_WIDTH]
        dk_tot, dv_tot, dq_pairs = [], [], []
        for g in range(N_KV_HEADS):
            k_dup, v_dup = _group_dup(k_a, k_b, g, lo2), _group_dup(v_a, v_b, g, lo2)
            q_rows = _group_rows(q_r, g, lo, hi)
            do_rows = _group_rows(dout_b, g, lo, hi)
            p_t, p_sink = _attn_probs_t(k_dup, q_rows, bias_t, _sink_row(sink_ref, g))
            dp_t = _dot_nt(v_dup, do_rows)
            delta = jnp.sum(p_t * dp_t, axis=0, keepdims=True)
            ds_t = p_t * (dp_t - delta) * ATTN_SCALE
            dsink = -p_sink * delta
            for r in range(HEADS_PER_GROUP):
                h = HEADS_PER_GROUP * g + r
                dsink_ref[h:h + 1, :] += jnp.broadcast_to(
                    jnp.sum(dsink[:, LANES * r:LANES * (r + 1)], axis=1, keepdims=True), (1, LANES))
            dk_full = _dot(ds_t, q_rows)
            dv_full = _dot(p_t, do_rows)
            dk_tot.append(dk_full + pltpu.roll(dk_full, HEAD_DIM, 1))
            dv_tot.append(dv_full + pltpu.roll(dv_full, HEAD_DIM, 1))
            dq_t = _dot(k_dup.T, ds_t)
            dq_pairs += _pairs_from_rows(dq_t.T, lo)
        dk_all = jnp.where(lo2, dk_tot[0], dk_tot[1])
        dv_all = jnp.where(lo2, dv_tot[0], dv_tot[1])
        dk_cur = dk_all[CHUNK:, :] + carry[:, 0:KV_WIDTH]
        dv_cur = dv_all[CHUNK:, :] + carry[:, KV_WIDTH:2 * KV_WIDTH]
        carry[:, 0:KV_WIDTH] = dk_all[:CHUNK, :]
        carry[:, KV_WIDTH:2 * KV_WIDTH] = dv_all[:CHUNK, :]
        dq = _rope_apply(jnp.concatenate(dq_pairs, axis=1), tab, -1.0)
        dproj_ref[:, o:o + ATTN_WIDTH] = dq.astype(dproj_ref.dtype)
        dproj_ref[:, o + ATTN_WIDTH:o + ATTN_WIDTH + KV_WIDTH] = (
            _rope_apply(dk_cur, tab, -1.0).astype(dproj_ref.dtype))
        dproj_ref[:, o + ATTN_WIDTH + KV_WIDTH:o + ATTN_WIDTH + 2 * KV_WIDTH] = dv_cur.astype(dproj_ref.dtype)

    rev = lambda i: nb - 1 - i
    return _hosted_call(
        body, comm, name="mixer_bwd", grid=(nb,),
        out_shape=[jax.ShapeDtypeStruct((seq, IN_PROJ_WIDTH), MXU_DTYPE),
                   jax.ShapeDtypeStruct((GMLP_GROUPS, CHUNK, CHUNK), F32),
                   jax.ShapeDtypeStruct((CHUNK, LANES), F32),
                   jax.ShapeDtypeStruct((N_Q_HEADS, LANES), F32)],
        in_specs=[pl.BlockSpec((CHUNK, IN_PROJ_WIDTH), lambda i: (rev(i), 0)),
                  pl.BlockSpec((CHUNK, 2 * KV_WIDTH), lambda i: (jnp.maximum(rev(i) - 1, 0), kv_col)),
                  pl.BlockSpec((CHUNK, 3 * LANES), lambda i: (rev(i), 0)),
                  pl.BlockSpec((CHUNK, 3 * LANES), lambda i: (jnp.maximum(rev(i) - 1, 0), 0)),
                  pl.BlockSpec((CHUNK, D_MODEL), lambda i: (rev(i), 0)),
                  _full((GMLP_GROUPS, CHUNK, CHUNK)), _full((GMLP_GROUPS, CHUNK, CHUNK)),
                  _full((CHUNK, GMLP_WIDTH)), _full((N_Q_HEADS, LANES))],
        out_specs=[pl.BlockSpec((CHUNK, IN_PROJ_WIDTH), lambda i: (rev(i), 0)),
                   _full((GMLP_GROUPS, CHUNK, CHUNK)), _full((CHUNK, LANES)), _full((N_Q_HEADS, LANES))],
        scratch_shapes=[pltpu.VMEM((CHUNK, 2 * KV_WIDTH), F32)],
        semantics=("arbitrary",),
    )(proj, proj, rope_tab, rope_tab, dcat, w_spatial, w_spatial_t, bias_full, sink_rows)


def _in_proj_bwd_kernel(x, dx1, dproj, vecs, w_in_blocks, comm=None):
    seq = x.shape[0]
    tm = 512

    def body(x_ref, dx1_ref, dp_ref, v_ref, wb_ref, gx_ref, sums_ref, w_ref):
        @pl.when(pl.program_id(0) == 0)
        def _():
            sums_ref[...] = jnp.zeros_like(sums_ref)
            _assemble_w_in(wb_ref, w_ref)

        g_mix, scale1 = v_ref[0:1, :], v_ref[2:3, :]
        dh = _dot_nt(dp_ref[...], w_ref[...])
        xv = x_ref[...]
        rstd = lax.rsqrt(_mean_last(xv * xv) + EPS)
        xh = xv * rstd
        dn1 = dh * (1.0 + scale1)
        dxh = dn1 * g_mix
        gx_ref[...] = dx1_ref[...] + rstd * (dxh - xh * _mean_last(dxh * xh))
        sums_ref[0:1, :] += _rowsum(dh)
        sums_ref[1:2, :] += _rowsum(dh * (xh * g_mix))
        sums_ref[2:3, :] += _rowsum(dn1 * xh)

    return _hosted_call(
        body, comm, name="in_proj_bwd", grid=(seq // tm,),
        out_shape=[jax.ShapeDtypeStruct((seq, D_MODEL), F32), jax.ShapeDtypeStruct((8, D_MODEL), F32)],
        in_specs=[pl.BlockSpec((tm, D_MODEL), lambda i: (i, 0)), pl.BlockSpec((tm, D_MODEL), lambda i: (i, 0)),
                  pl.BlockSpec((tm, IN_PROJ_WIDTH), lambda i: (i, 0)), _full((8, D_MODEL)),
                  _full((N_CHIPS, D_MODEL, W_IN_PADDED))],
        out_specs=[pl.BlockSpec((tm, D_MODEL), lambda i: (i, 0)), _full((8, D_MODEL))],
        scratch_shapes=[pltpu.VMEM((D_MODEL, IN_PROJ_WIDTH), MXU_DTYPE)],
        semantics=("arbitrary",),
    )(x, dx1, dproj, vecs, w_in_blocks)


def _weight_grad_kernel(a, b, name, tm, tn, layout="plain", comm=None):
    seq, m = a.shape
    n = b.shape[1]
    tk = min(seq, 1024)
    nk = seq // tk

    def body(a_ref, b_ref, o_ref, acc):
        k = pl.program_id(2)

        @pl.when(k == 0)
        def _():
            acc[...] = jnp.zeros_like(acc)

        acc[...] += _dot_tn(a_ref[...], b_ref[...])

        @pl.when(k == nk - 1)
        def _():
            if layout == "w_in":
                lo = lax.broadcasted_iota(jnp.int32, (tm, LANES), 1) < W_IN_PADDED - W_IN_BLOCK
                keep = W_IN_PADDED - LANES
                o_ref[0, :, 0:keep] = acc[:, 0:keep]
                o_ref[0, :, keep:] = jnp.where(lo, acc[:, keep:W_IN_PADDED], 0.0)
                second = pltpu.roll(acc[:, keep:keep + W_IN_PADDED], W_IN_BLOCK, 1)
                o_ref[1, :, 0:keep] = second[:, 0:keep]
                o_ref[1, :, keep:] = jnp.where(lo, second[:, keep:], 0.0)
            else:
                o_ref[...] = acc[...]

    if layout == "w_in":
        assert tn == 2 * W_IN_BLOCK and tm == m
        out_shape = jax.ShapeDtypeStruct((N_CHIPS, m, W_IN_PADDED), F32)
        out_spec = pl.BlockSpec((2, tm, W_IN_PADDED), lambda i, j, k: (j, 0, 0))
    elif layout == "cols":
        out_shape = jax.ShapeDtypeStruct((n // tn, m, tn), F32)
        out_spec = pl.BlockSpec((None, tm, tn), lambda i, j, k: (j, i, 0))
    else:
        out_shape = jax.ShapeDtypeStruct((m, n), F32)
        out_spec = pl.BlockSpec((tm, tn), lambda i, j, k: (i, j))
    out = _hosted_call(
        body, comm, name=name, grid=(m // tm, n // tn, nk), out_shape=[out_shape],
        in_specs=[pl.BlockSpec((tk, tm), lambda i, j, k: (k, i)), pl.BlockSpec((tk, tn), lambda i, j, k: (k, j))],
        out_specs=[out_spec], scratch_shapes=[pltpu.VMEM((tm, tn), F32)],
        semantics=("arbitrary", "arbitrary", "arbitrary"),
    )(a, b)
    return out[0] if comm is None else out


def _adam_update(w, g, m, v):
    m_new = ADAM_B1 * m + (1.0 - ADAM_B1) * g
    v_new = ADAM_B2 * v + (1.0 - ADAM_B2) * (g * g)
    m_hat = m_new / (1.0 - ADAM_B1 ** ADAM_STEP)
    v_hat = v_new / (1.0 - ADAM_B2 ** ADAM_STEP)
    delta = -ADAM_LR * (m_hat / (jnp.sqrt(v_hat) + ADAM_EPS) + ADAM_WD * w)
    return delta, m_new, v_new


def _add_halves_kernel(part, recv, c_idx, name):
    _, _, r, n = part.shape
    tr = min(r, 256)

    def body(c_ref, p_ref, q_ref, o_ref):
        del c_ref
        o_ref[...] = (p_ref[...] + q_ref[...]).astype(o_ref.dtype)

    return pl.pallas_call(
        body, name=name, out_shape=jax.ShapeDtypeStruct((N_CHIPS, r, n), GRAD_COMM_DTYPE),
        grid_spec=pltpu.PrefetchScalarGridSpec(
            num_scalar_prefetch=1, grid=(N_CHIPS, r // tr),
            in_specs=[pl.BlockSpec((None, None, tr, n), lambda k, i, c: (k, c[0], i, 0)),
                      pl.BlockSpec((None, tr, n), lambda k, i, c: (k, i, 0))],
            out_specs=pl.BlockSpec((None, tr, n), lambda k, i, c: (k, i, 0))),
        compiler_params=_params("parallel", "parallel"),
    )(c_idx, part, recv)


def _sum_chips_kernel(own, others, place, name):
    _, r, n = own.shape
    tr = min(r, 256)

    def body(place_ref, own_ref, oth_ref, o_ref):
        del place_ref
        acc = own_ref[...].astype(F32)
        for k in range(N_CHIPS - 1):
            acc = acc + oth_ref[k].astype(F32)
        o_ref[...] = acc

    return pl.pallas_call(
        body, name=name, out_shape=jax.ShapeDtypeStruct((2, r, n), F32),
        grid_spec=pltpu.PrefetchScalarGridSpec(
            num_scalar_prefetch=1, grid=(r // tr,),
            in_specs=[pl.BlockSpec((None, tr, n), lambda i, p: (p[0], i, 0)),
                      pl.BlockSpec((N_CHIPS - 1, tr, n), lambda i, p: (0, i, 0))],
            out_specs=pl.BlockSpec((None, tr, n), lambda i, p: (p[1], i, 0))),
        compiler_params=_params("parallel"),
    )(place, own, others)


def _adam_kernel(w, g, m, v, name):
    r, n = w.shape
    tr = min(r, 256)

    def body(w_ref, g_ref, m_ref, v_ref, g_out, d_ref, mo_ref, vo_ref):
        gv = g_ref[:, 0:n]
        g_out[...] = gv
        d_ref[...], mo_ref[...], vo_ref[...] = _adam_update(w_ref[...], gv, m_ref[...], v_ref[...])

    spec = pl.BlockSpec((tr, n), lambda i: (i, 0))
    return pl.pallas_call(
        body, name=name, grid=(r // tr,), out_shape=[jax.ShapeDtypeStruct((r, n), F32)] * 4,
        in_specs=[spec, pl.BlockSpec((tr, g.shape[1]), lambda i: (i, 0)), spec, spec], out_specs=[spec] * 4,
        compiler_params=_params("parallel"),
    )(w, g, m, v)


SMALL_PARAMS = ("b_ada", "g_mix", "g_ffn", "g_final", "b_spatial", "sinks", "w_spatial")


def _small_update_kernel(gathered, params):
    shapes = [params[nm][0].shape for nm in SMALL_PARAMS]

    def body(*refs):
        g_refs, refs = refs[:5], refs[5:]
        p_refs, refs = refs[:3 * len(SMALL_PARAMS)], refs[3 * len(SMALL_PARAMS):]
        loss_ref, o_refs = refs[0], refs[1:]

        def total(ref):
            acc = ref[0]
            for k in range(1, N_DEV):
                acc = acc + ref[k]
            return acc

        s1, s2, db, ds, dw = (total(r) for r in g_refs)
        loss_ref[...] = jnp.broadcast_to(s2[6:7, 0:1], loss_ref.shape)
        grads = {"b_ada": [s1[0:1], s1[1:2], s2[5:6], s2[0:1], s2[1:2], s2[2:3]], "g_mix": [s1[2:3]],
                 "g_ffn": [s2[3:4]], "g_final": [s2[4:5]], "b_spatial": [db.T[0:GMLP_GROUPS]],
                 "w_spatial": [dw]}
        lane = lax.broadcasted_iota(jnp.int32, (1, LANES), 1)
        sink_row = jnp.zeros((1, LANES), F32)
        for h in range(N_Q_HEADS):
            sink_row = sink_row + jnp.where(lane == h, ds[h:h + 1, :], 0.0)
        grads["sinks"] = [sink_row[:, 0:N_Q_HEADS]]
        for i, nm in enumerate(SMALL_PARAMS):
            w_ref, m_ref, v_ref = p_refs[3 * i:3 * i + 3]
            outs = o_refs[4 * i:4 * i + 4]
            width = grads[nm][0].shape[1]
            for k, g in enumerate(grads[nm]):
                cols = slice(width * k, width * (k + 1))
                upd = _adam_update(w_ref[:, cols], g, m_ref[:, cols], v_ref[:, cols])
                for o_ref, val in zip(outs, (g,) + upd):
                    o_ref[:, cols] = val

    flat = [a for nm in SMALL_PARAMS for a in params[nm]]
    out_shape = [jax.ShapeDtypeStruct((8, LANES), F32)]
    out_shape += [jax.ShapeDtypeStruct(s, F32) for s in shapes for _ in range(4)]
    outs = pl.pallas_call(
        body, name="small_update", grid=(1,), out_shape=out_shape,
        in_specs=[_full(g.shape) for g in gathered] + [_full(a.shape) for a in flat],
        out_specs=[_full(s.shape) for s in out_shape],
        compiler_params=_params("arbitrary"),
    )(*gathered, *flat)
    return {nm: outs[1 + 4 * i:5 + 4 * i] for i, nm in enumerate(SMALL_PARAMS)}, outs[0]


def _ada_update_kernel(act_t, dmod, w, m, v):
    r, n = w.shape
    tr = 256

    def body(a_ref, d_ref, w_ref, m_ref, v_ref, g_ref, dl_ref, mo_ref, vo_ref):
        g = _dot(a_ref[...], d_ref[...])
        g_ref[...] = g
        dl_ref[...], mo_ref[...], vo_ref[...] = _adam_update(w_ref[...], g, m_ref[...], v_ref[...])

    spec = pl.BlockSpec((tr, n), lambda i: (i, 0))
    return pl.pallas_call(
        body, name="ada_update", grid=(r // tr,), out_shape=[jax.ShapeDtypeStruct((r, n), F32)] * 4,
        in_specs=[pl.BlockSpec((tr, N_DEV), lambda i: (i, 0)), _full((N_DEV, n)), spec, spec, spec],
        out_specs=[spec] * 4, compiler_params=_params("parallel"),
    )(act_t, dmod, w, m, v)


def kernel(x, c, positions, w_ada, b_ada, g_mix, w_in, w_spatial, b_spatial, sinks, w_out, g_ffn, w_ff1, w_ff2, g_final, loss_target, m_w_ada, m_b_ada, m_g_mix, m_w_in, m_w_spatial, m_b_spatial, m_sinks, m_w_out, m_g_ffn, m_w_ff1, m_w_ff2, m_g_final, v_w_ada, v_b_ada, v_g_mix, v_w_in, v_w_spatial, v_b_spatial, v_sinks, v_w_out, v_g_ffn, v_w_ff1, v_w_ff2, v_g_final):
    xi, yi, ci = lax.axis_index("x"), lax.axis_index("y"), lax.axis_index("c")
    chip = 2 * xi + yi
    dev = 2 * chip + ci
    seq = x.shape[1]
    x2, tgt = x[0], loss_target[0]
    ada_cols = w_ada.shape[2]

    c_all = _all_gather8([c], "gather_c")[0].reshape(N_DEV, D_MODEL)
    b_shard = lax.dynamic_slice(b_ada, (0, chip * ada_cols), (1, ada_cols))
    mod_part, act = _mod_kernel(c_all, w_ada[0], b_shard)
    mod_all, = _all_gather8([mod_part], "gather_mod")
    mod_me = lax.dynamic_index_in_dim(mod_all[0::2], dev, axis=1, keepdims=False)
    mod_me = mod_me.reshape(N_MOD, D_MODEL)
    shift1, scale1, gate1, shift2, scale2, gate2 = (mod_me[k:k + 1] for k in range(N_MOD))

    def halves(w):
        r, n = w.shape[1], w.shape[2]
        return w[0].astype(WEIGHT_COMM_DTYPE).reshape(2, r // 2, n)

    w_in_padded = jnp.pad(w_in, ((0, 0), (0, 0), (0, W_IN_PADDED - W_IN_BLOCK)))
    w_in_blocks, = _all_gather8([halves(w_in_padded)], "gather_w_in", split=True)
    w_in_blocks = w_in_blocks.reshape(N_CHIPS, D_MODEL, W_IN_PADDED)

    zeros_row = jnp.zeros((1, D_MODEL), F32)
    vecs1 = jnp.concatenate([g_mix, shift1, scale1] + [zeros_row] * 5, axis=0)
    vecs2 = jnp.concatenate([gate1, shift2, scale2, gate2, g_ffn, g_final.reshape(1, D_MODEL)]
                            + [zeros_row] * 2, axis=0)
    bias_full = jnp.repeat(b_spatial[0].T, HEAD_DIM, axis=1)
    sink_rows = jnp.broadcast_to(sinks[0][:, None], (N_Q_HEADS, LANES))
    inv_freq = ROPE_THETA ** (-jnp.arange(0, ROT_DIM, 2, dtype=F32) / ROT_DIM)
    rope_tab = _rope_lane_tables(*_rope_angle_kernel(positions, inv_freq.reshape(ROT_DIM // 2, 1)))

    proj, hb, g_ff1 = _in_proj_kernel(x2, vecs1, w_in_blocks, comm=_gather_job([halves(w_ff1)]))
    cat, g_ff2, g_out = _mixer_fwd_kernel(proj, rope_tab, w_spatial[0], bias_full, sink_rows,
                                          comm=_gather_job([halves(w_ff2), halves(w_out)]))
    g_ff1, g_ff2, g_out = _gather_forward([g_ff1, g_ff2, g_out], "gather_forward")
    w_out_full = g_out.reshape(D_MODEL, D_MODEL)
    w_ff1_blocks = g_ff1.reshape(N_CHIPS, D_MODEL, D_MODEL)
    w_ff2_blocks = g_ff2.reshape(N_CHIPS, D_MODEL, D_MODEL)
    dx1, dcat, dmix, h2b, rb, dab, dffb, sums2 = _trunk_kernel(
        x2, tgt, cat, vecs2, w_out_full, w_ff1_blocks, w_ff2_blocks)

    c_idx = ci.reshape(1).astype(jnp.int32)
    place = jnp.stack([chip, ci]).astype(jnp.int32)
    big = {"w_in": (w_in, m_w_in, v_w_in), "w_out": (w_out, m_w_out, v_w_out),
           "w_ff1": (w_ff1, m_w_ff1, v_w_ff1), "w_ff2": (w_ff2, m_w_ff2, v_w_ff2)}

    def halves_of(nm, g):
        return g.reshape(N_CHIPS, 2, big[nm][0].shape[1] // 2, -1)

    p_ff2 = halves_of("w_ff2", _weight_grad_kernel(rb, dffb, "dw_ff2", 1024, 1024))
    dw_ff1, q_ff2 = _weight_grad_kernel(h2b, dab, "dw_ff1", 1024, 1024, layout="cols",
                                        comm=_sibling_job([p_ff2]))
    p_ff1 = halves_of("w_ff1", dw_ff1)
    dw_out, q_ff1 = _weight_grad_kernel(cat, dmix, "dw_out", 1024, 1024, comm=_sibling_job([p_ff1]))
    p_out = halves_of("w_out", dw_out)
    q_out, = _run_comm(_sibling_job([p_out]), "grad_to_sibling_w_out")
    cs_ff2 = _add_halves_kernel(p_ff2, q_ff2, c_idx, "grad_add_w_ff2")
    cs_ff1 = _add_halves_kernel(p_ff1, q_ff1, c_idx, "grad_add_w_ff1")
    cs_out = _add_halves_kernel(p_out, q_out, c_idx, "grad_add_w_out")
    dproj, dw_spatial, db_lanes, dsink_rows, sc_ff2, sc_out = _mixer_bwd_kernel(
        proj, rope_tab, dcat, w_spatial[0], w_spatial[0].transpose(0, 2, 1), bias_full, sink_rows,
        comm=_scatter_job([cs_ff2, cs_out]))
    dw_in, sc_ff1 = _weight_grad_kernel(hb, dproj, "dw_in", 1024, 2 * W_IN_BLOCK, layout="w_in",
                                        comm=_scatter_job([cs_ff1]))
    grad_x, sums1 = _in_proj_bwd_kernel(x2, dx1, dproj, vecs1, w_in_blocks)
    p_in = halves_of("w_in", dw_in)
    q_in, = _run_comm(_sibling_job([p_in]), "grad_to_sibling_w_in")
    cs_in = _add_halves_kernel(p_in, q_in, c_idx, "grad_add_w_in")
    sc_in, = _run_comm(_scatter_job([cs_in]), "grad_to_chips_w_in")

    names = ["w_in", "w_out", "w_ff1", "w_ff2"]
    totals = [_sum_chips_kernel(own, oth, place, "grad_sum_" + nm)
              for nm, own, oth in zip(names, [cs_in, cs_out, cs_ff1, cs_ff2], [sc_in, sc_out, sc_ff1, sc_ff2])]
    shared = _sibling_share(totals, "grad_share")
    big_out = {}
    for nm, g in zip(names, shared):
        w, m, v = big[nm]
        g = g.reshape(w.shape[1], -1)
        big_out[nm] = tuple(t[None] for t in _adam_kernel(w[0], g, m[0], v[0], "adam_" + nm))

    small = {"b_ada": (b_ada, m_b_ada, v_b_ada), "g_mix": (g_mix, m_g_mix, v_g_mix),
             "g_ffn": (g_ffn, m_g_ffn, v_g_ffn), "g_final": (g_final, m_g_final, v_g_final),
             "b_spatial": (b_spatial, m_b_spatial, v_b_spatial), "sinks": (sinks, m_sinks, v_sinks),
             "w_spatial": (w_spatial, m_w_spatial, v_w_spatial)}
    flat_shape = {"g_final": (1, D_MODEL), "b_spatial": (GMLP_GROUPS, CHUNK), "w_spatial": (GMLP_GROUPS * CHUNK, CHUNK)}
    gathered = _all_gather8([sums1, sums2, db_lanes, dsink_rows, dw_spatial.reshape(GMLP_GROUPS * CHUNK, CHUNK)],
                            "gather_small")
    small_out, loss_tile = _small_update_kernel(
        gathered, {nm: tuple(a.reshape(flat_shape.get(nm, a.shape)) for a in small[nm]) for nm in small})
    small_out = {nm: [o.reshape(small[nm][0].shape) for o in small_out[nm]] for nm in small}
    loss = loss_tile[0, 0]

    g1, g2 = gathered[0], gathered[1]
    dmod_all = jnp.concatenate([g1[:, 0], g1[:, 1], g2[:, 5], g2[:, 0], g2[:, 1], g2[:, 2]], axis=1)
    dmod_cols = lax.dynamic_slice(dmod_all, (0, chip * ada_cols), (N_DEV, ada_cols))
    ada = _ada_update_kernel(act.T, dmod_cols, w_ada[0], m_w_ada[0], v_w_ada[0])
    big_out["w_ada"] = tuple(t[None] for t in ada)

    order = ["w_ada", "b_ada", "g_mix", "w_in", "w_spatial", "b_spatial", "sinks", "w_out", "g_ffn",
             "w_ff1", "w_ff2", "g_final"]

    def leaf(nm, k):
        return big_out[nm][k] if nm in big_out else small_out[nm][k]

    outs = [loss, grad_x[None]]
    for k in range(4):
        outs += [leaf(nm, k) for nm in order]
    return tuple(outs)
```

```python
import math
from typing import Callable, NamedTuple

import jax
import jax.numpy as jnp
from jax import lax
from jax.experimental import pallas as pl
from jax.experimental.pallas import tpu as pltpu

F32 = jnp.float32
MXU_DTYPE = jnp.bfloat16
WEIGHT_COMM_DTYPE = jnp.bfloat16
GRAD_COMM_DTYPE = jnp.bfloat16

D_MODEL = 1024
D_FF = 4096
HEAD_DIM = 64
GMLP_GROUPS = 8
GMLP_WIDTH = 512
CHUNK = 128
N_Q_HEADS = 8
N_KV_HEADS = 2
ATTN_WIDTH = 512
KV_WIDTH = 128
ROT_DIM = 16
ROPE_THETA = 500000.0
IN_PROJ_WIDTH = 1792
N_MOD = 6
EPS = 1e-5
N_CHIPS = 4
N_DEV = 8
LANES = 128
W_IN_BLOCK = IN_PROJ_WIDTH // N_CHIPS

ADAM_LR = 0.001
ADAM_B1 = 0.9
ADAM_B2 = 0.999
ADAM_EPS = 1e-08
ADAM_WD = 0.01
ADAM_STEP = 10

VMEM_LIMIT_BYTES = 58 * 1024 * 1024
MESH = pl.DeviceIdType.MESH


def _params(*semantics):
    return pltpu.CompilerParams(dimension_semantics=semantics, vmem_limit_bytes=VMEM_LIMIT_BYTES)


def _dot(a, b):
    return jnp.dot(a.astype(MXU_DTYPE), b.astype(MXU_DTYPE), preferred_element_type=F32)


def _dot_nt(a, b):
    return lax.dot_general(a.astype(MXU_DTYPE), b.astype(MXU_DTYPE), (((1,), (1,)), ((), ())),
                           preferred_element_type=F32)


def _dot_tn(a, b):
    return lax.dot_general(a.astype(MXU_DTYPE), b.astype(MXU_DTYPE), (((0,), (0,)), ((), ())),
                           preferred_element_type=F32)


def _full(shape):
    return pl.BlockSpec(shape, lambda *_: (0,) * len(shape))


def _any():
    return pl.BlockSpec(memory_space=pl.ANY)


def _rowsum(v):
    return jnp.sum(v, axis=0, keepdims=True)


def _mean_last(v):
    return jnp.mean(v, axis=-1, keepdims=True)


class _Comm(NamedTuple):
    operands: tuple
    out_shapes: tuple
    n_sems: int
    make: Callable


def _hosted_call(body, comm, *, name, grid, in_specs, out_shape, out_specs, scratch_shapes=(), semantics):
    if comm is None:
        return pl.pallas_call(body, name=name, grid=grid, in_specs=in_specs, out_shape=out_shape,
                              out_specs=out_specs, scratch_shapes=list(scratch_shapes),
                              compiler_params=_params(*semantics))
    n_in, n_out, n_scr = len(in_specs), len(out_shape), len(scratch_shapes)
    k_in, k_out = len(comm.operands), len(comm.out_shapes)

    def hosted(*refs):
        ins, refs = refs[:n_in], refs[n_in:]
        c_ins, refs = refs[:k_in], refs[k_in:]
        outs, refs = refs[:n_out], refs[n_out:]
        c_outs, refs = refs[:k_out], refs[k_out:]
        scratch, (send_sems, recv_sems) = refs[:n_scr], refs[n_scr:]
        first, last = None, None
        for d, size in enumerate(grid):
            at_start, at_end = pl.program_id(d) == 0, pl.program_id(d) == size - 1
            first = at_start if first is None else first & at_start
            last = at_end if last is None else last & at_end

        @pl.when(first)
        def _():
            for cp in comm.make(c_ins, c_outs, send_sems, recv_sems)[0]:
                cp.start()

        body(*ins, *outs, *scratch)

        @pl.when(last)
        def _():
            for wait in comm.make(c_ins, c_outs, send_sems, recv_sems)[1]:
                wait()

    call = pl.pallas_call(
        hosted, name=name, grid=grid, in_specs=list(in_specs) + [_any()] * k_in,
        out_shape=list(out_shape) + list(comm.out_shapes), out_specs=list(out_specs) + [_any()] * k_out,
        scratch_shapes=list(scratch_shapes) + [pltpu.SemaphoreType.DMA((comm.n_sems,)),
                                                pltpu.SemaphoreType.DMA((comm.n_sems,))],
        compiler_params=_params(*semantics))
    return lambda *args: call(*args, *comm.operands)


def _mesh_place():
    x, y, c = lax.axis_index("x"), lax.axis_index("y"), lax.axis_index("c")
    return x, y, c, [(1 - x, y), (x, 1 - y), (1 - x, 1 - y)]


def _gather_job(halves):
    per = 5

    def make(ins, outs, send_sems, recv_sems):
        x, y, c, chips = _mesh_place()
        starts, waits = [], []
        for a, (src, out) in enumerate(zip(ins, outs)):
            mine = out.at[4 * x + 2 * y + c]
            local = pltpu.make_async_copy(src.at[c], mine, send_sems.at[per * a + 4])
            to = [(x, y, 1 - c)] + [(px, py, c) for px, py in chips]
            sends = [pltpu.make_async_remote_copy(
                src_ref=src.at[c], dst_ref=mine, send_sem=send_sems.at[per * a + k],
                recv_sem=recv_sems.at[per * a + k], device_id=dev, device_id_type=MESH)
                for k, dev in enumerate(to)]
            recvs = [pltpu.make_async_remote_copy(
                src_ref=src.at[c], dst_ref=out.at[4 * px + 2 * py + pc], send_sem=send_sems.at[per * a + k],
                recv_sem=recv_sems.at[per * a + k], device_id=(px, py, pc), device_id_type=MESH)
                for k, (px, py, pc) in enumerate(to)]
            starts += [local] + sends
            waits += [local.wait] + [s.wait_send for s in sends] + [r.wait_recv for r in recvs]
        return starts, waits

    return _Comm(tuple(halves), tuple(jax.ShapeDtypeStruct((N_DEV,) + h.shape[1:], h.dtype) for h in halves),
                 per * len(halves), make)


def _gather_forward(bufs, name):
    n_arr = len(bufs)

    def body(*refs):
        outs = refs[n_arr:2 * n_arr]
        send_sems, recv_sems = refs[2 * n_arr:]
        x, y, c, chips = _mesh_place()
        sends, recvs = [], []
        for a, buf in enumerate(outs):
            for j, (px, py) in enumerate(chips):
                mine, theirs = buf.at[4 * px + 2 * py + c], buf.at[4 * px + 2 * py + 1 - c]
                sems = dict(send_sem=send_sems.at[3 * a + j], recv_sem=recv_sems.at[3 * a + j],
                            device_id=(x, y, 1 - c), device_id_type=MESH)
                sends.append(pltpu.make_async_remote_copy(src_ref=mine, dst_ref=mine, **sems))
                recvs.append(pltpu.make_async_remote_copy(src_ref=mine, dst_ref=theirs, **sems))
        for cp in sends:
            cp.start()
        for s, r in zip(sends, recvs):
            s.wait_send()
            r.wait_recv()

    return pl.pallas_call(
        body, name=name, out_shape=[jax.ShapeDtypeStruct(b.shape, b.dtype) for b in bufs],
        in_specs=[_any()] * n_arr, out_specs=[_any()] * n_arr,
        input_output_aliases={a: a for a in range(n_arr)},
        scratch_shapes=[pltpu.SemaphoreType.DMA((3 * n_arr,)), pltpu.SemaphoreType.DMA((3 * n_arr,))],
    )(*bufs)


def _sibling_job(parts):
    def make(ins, outs, send_sems, recv_sems):
        x, y, c, _ = _mesh_place()
        copies = [pltpu.make_async_remote_copy(
            src_ref=src.at[k, 1 - c], dst_ref=out.at[k], send_sem=send_sems.at[N_CHIPS * a + k],
            recv_sem=recv_sems.at[N_CHIPS * a + k], device_id=(x, y, 1 - c), device_id_type=MESH)
            for a, (src, out) in enumerate(zip(ins, outs)) for k in range(N_CHIPS)]
        return copies, [cp.wait for cp in copies]

    return _Comm(tuple(parts), tuple(jax.ShapeDtypeStruct((N_CHIPS,) + p.shape[2:], p.dtype) for p in parts),
                 N_CHIPS * len(parts), make)


def _scatter_job(chip_sums):
    def make(ins, outs, send_sems, recv_sems):
        x, y, c, chips = _mesh_place()
        copies = [pltpu.make_async_remote_copy(
            src_ref=src.at[2 * px + py], dst_ref=out.at[j], send_sem=send_sems.at[3 * a + j],
            recv_sem=recv_sems.at[3 * a + j], device_id=(px, py, c), device_id_type=MESH)
            for a, (src, out) in enumerate(zip(ins, outs)) for j, (px, py) in enumerate(chips)]
        return copies, [cp.wait for cp in copies]

    return _Comm(tuple(chip_sums), tuple(jax.ShapeDtypeStruct((3,) + s.shape[1:], s.dtype) for s in chip_sums),
                 3 * len(chip_sums), make)


def _run_comm(comm, name):
    def body(token_ref):
        token_ref[...] = jnp.zeros_like(token_ref)

    out = _hosted_call(body, comm, name=name, grid=(1,), in_specs=[],
                       out_shape=[jax.ShapeDtypeStruct((8, LANES), F32)], out_specs=[_full((8, LANES))],
                       semantics=("arbitrary",))()
    return out[1:]


def _all_gather8(blocks, name, split=False):
    n_arr = len(blocks)

    def body(*refs):
        x_refs, out_refs = refs[:n_arr], refs[n_arr:2 * n_arr]
        send_sems, recv_sems, local_sems = refs[2 * n_arr:]
        x, y, c, chips = _mesh_place()
        me, sibling = (x, y, c), (x, y, 1 - c)
        arrays = []
        for a, (x_ref, out_ref) in enumerate(zip(x_refs, out_refs)):
            src_mine = x_ref.at[c] if split else x_ref

            def copy(k, blk, to, src=None, a=a, out_ref=out_ref):
                dst = out_ref.at[4 * blk[0] + 2 * blk[1] + blk[2]]
                return pltpu.make_async_remote_copy(
                    src_ref=dst if src is None else src, dst_ref=dst,
                    send_sem=send_sems.at[7 * a + k], recv_sem=recv_sems.at[7 * a + k],
                    device_id=to, device_id_type=MESH)

            mine = pltpu.make_async_copy(src_mine, out_ref.at[4 * x + 2 * y + c], local_sems.at[a])
            mine.start()
            first = [copy(0, me, sibling, src=src_mine)]
            first += [copy(1 + j, me, (*chip, c), src=src_mine) for j, chip in enumerate(chips)]
            for cp in first:
                cp.start()
            arrays.append((copy, mine, first))
        sent = []
        for copy, mine, first in arrays:
            passed = [copy(4 + j, (*chip, c), sibling) for j, chip in enumerate(chips)]
            for j, chip in enumerate(chips):
                copy(1 + j, (*chip, c), me).wait_recv()
                passed[j].start()
            sent += first + passed
        for copy, mine, first in arrays:
            copy(0, sibling, me).wait_recv()
            for j, chip in enumerate(chips):
                copy(4 + j, (*chip, 1 - c), me).wait_recv()
            mine.wait()
        for cp in sent:
            cp.wait_send()

    return pl.pallas_call(
        body, name=name,
        out_shape=[jax.ShapeDtypeStruct((N_DEV,) + tuple(b.shape[1:] if split else b.shape), b.dtype)
                   for b in blocks],
        in_specs=[_any()] * n_arr, out_specs=[_any()] * n_arr,
        scratch_shapes=[pltpu.SemaphoreType.DMA((7 * n_arr,)), pltpu.SemaphoreType.DMA((7 * n_arr,)),
                        pltpu.SemaphoreType.DMA((n_arr,))],
    )(*blocks)


def _sibling_share(bufs, name):
    n_arr = len(bufs)

    def body(*refs):
        out_refs = refs[n_arr:2 * n_arr]
        send_sems, recv_sems = refs[2 * n_arr:]
        x, y, c = lax.axis_index("x"), lax.axis_index("y"), lax.axis_index("c")
        copies = [pltpu.make_async_remote_copy(
            src_ref=out_refs[a].at[c], dst_ref=out_refs[a].at[c],
            send_sem=send_sems.at[a], recv_sem=recv_sems.at[a],
            device_id=(x, y, 1 - c), device_id_type=MESH) for a in range(n_arr)]
        for cp in copies:
            cp.start()
        for a in range(n_arr):
            pltpu.make_async_remote_copy(
                src_ref=out_refs[a].at[c], dst_ref=out_refs[a].at[1 - c],
                send_sem=send_sems.at[a], recv_sem=recv_sems.at[a],
                device_id=(x, y, 1 - c), device_id_type=MESH).wait()

    return pl.pallas_call(
        body, name=name,
        out_shape=[jax.ShapeDtypeStruct(b.shape, b.dtype) for b in bufs],
        in_specs=[_any()] * n_arr, out_specs=[_any()] * n_arr,
        input_output_aliases={a: a for a in range(n_arr)},
        scratch_shapes=[pltpu.SemaphoreType.DMA((n_arr,)), pltpu.SemaphoreType.DMA((n_arr,))],
    )(*bufs)


def _gelu_tanh(z):
    k = math.sqrt(2.0 / math.pi)
    t = jnp.tanh(k * (z + 0.044715 * (z * z * z)))
    return 0.5 * z * (1.0 + t), t


def _gelu_tanh_grad(z, t):
    k = math.sqrt(2.0 / math.pi)
    return 0.5 * (1.0 + t) + 0.5 * z * (1.0 - t * t) * (k * (1.0 + 3.0 * 0.044715 * (z * z)))


def _rope_angle_kernel(pos_row, invf_col):
    seq = pos_row.shape[1]

    def body(p_ref, f_ref, cos_ref, sin_ref):
        ang = p_ref[...].astype(F32) * f_ref[...]
        cos_ref[...] = jnp.cos(ang)
        sin_ref[...] = jnp.sin(ang)

    return pl.pallas_call(
        body, name="rope_angles", grid=(1,), out_shape=[jax.ShapeDtypeStruct((ROT_DIM // 2, seq), F32)] * 2,
        in_specs=[_full((1, seq)), _full((ROT_DIM // 2, 1))], out_specs=[_full((ROT_DIM // 2, seq))] * 2,
        compiler_params=_params("arbitrary"),
    )(pos_row, invf_col)


def _rope_lane_tables(cos, sin):
    cos_t, sin_t = cos.T, sin.T
    seq, half = cos_t.shape
    ones = jnp.ones((seq, HEAD_DIM - ROT_DIM), F32)
    c64 = jnp.concatenate([cos_t, cos_t, ones], axis=1)
    s1 = jnp.concatenate([sin_t, jnp.zeros((seq, HEAD_DIM - half), F32)], axis=1)
    s2 = jnp.concatenate([jnp.zeros((seq, half), F32), sin_t, jnp.zeros((seq, HEAD_DIM - ROT_DIM), F32)], axis=1)
    return jnp.concatenate([jnp.tile(t, (1, LANES // HEAD_DIM)) for t in (c64, s1, s2)], axis=1)


def _rope_apply(t, tab, sign):
    reps = t.shape[1] // LANES
    c_tab, s1, s2 = (jnp.tile(tab[:, LANES * k:LANES * (k + 1)], (1, reps)) if reps > 1
                     else tab[:, LANES * k:LANES * (k + 1)] for k in range(3))
    half = ROT_DIM // 2
    up = pltpu.roll(t, t.shape[1] - half, 1)
    down = pltpu.roll(t, half, 1)
    return t * c_tab + sign * (down * s2 - up * s1)


def _lane_masks(shape):
    lane = lax.broadcasted_iota(jnp.int32, shape, 1)
    return lane < HEAD_DIM, lane >= HEAD_DIM


HEADS_PER_GROUP = N_Q_HEADS // N_KV_HEADS
ATTN_SCALE = 1.0 / math.sqrt(HEAD_DIM)


def _attn_bias_t(first_block):
    kj = lax.broadcasted_iota(jnp.int32, (2 * CHUNK, CHUNK), 0)
    qi = lax.broadcasted_iota(jnp.int32, (2 * CHUNK, CHUNK), 1)
    ok = (kj > qi) & (kj <= qi + CHUNK) & (jnp.logical_not(first_block) | (kj >= CHUNK))
    return jnp.tile(jnp.where(ok, 0.0, -jnp.inf), (1, HEADS_PER_GROUP))


def _group_rows(x, g, lo, hi):
    rows = []
    for r in range(HEADS_PER_GROUP):
        h = HEADS_PER_GROUP * g + r
        pair = x[:, LANES * (h // 2):LANES * (h // 2 + 1)]
        rows.append(jnp.where(hi if h % 2 else lo, pair, 0.0))
    return jnp.concatenate(rows, axis=0)


def _pairs_from_rows(rows, lo):
    return [jnp.where(lo, rows[2 * CHUNK * k:2 * CHUNK * k + CHUNK], rows[2 * CHUNK * k + CHUNK:2 * CHUNK * (k + 1)])
            for k in range(HEADS_PER_GROUP // 2)]


def _group_dup(a, b, g, lo2):
    return jnp.where(lo2, a, b) if g == 0 else jnp.where(lo2, b, a)


def _sink_row(sink_ref, g):
    return jnp.concatenate([sink_ref[HEADS_PER_GROUP * g + r:HEADS_PER_GROUP * g + r + 1, :]
                            for r in range(HEADS_PER_GROUP)], axis=1)


def _attn_probs_t(k_dup, q_rows, bias_t, sink_row):
    s_t = _dot_nt(k_dup, q_rows) * ATTN_SCALE + bias_t
    m = jnp.maximum(jnp.max(s_t, axis=0, keepdims=True), sink_row)
    p = jnp.exp(s_t - m)
    e_sink = jnp.exp(sink_row - m)
    inv = 1.0 / (jnp.sum(p, axis=0, keepdims=True) + e_sink)
    return p * inv, e_sink * inv


def _sgu_forward_pair(wm, vp, j):
    lo, hi = _lane_masks(vp.shape)
    lhs = jnp.concatenate([wm[2 * j], wm[2 * j + 1]], axis=1)
    rhs = jnp.concatenate([jnp.where(lo, vp, 0.0), jnp.where(hi, vp, 0.0)], axis=0)
    return _dot(lhs, rhs)


def _masked_spatial(w_ref):
    t = lax.broadcasted_iota(jnp.int32, (CHUNK, CHUNK), 0)
    s = lax.broadcasted_iota(jnp.int32, (CHUNK, CHUNK), 1)
    tril = s <= t
    return [jnp.where(tril, w_ref[g], 0.0) for g in range(GMLP_GROUPS)], tril, s >= t


def _mod_kernel(c_all, w_shard, b_shard):
    n = w_shard.shape[1]
    tn = 512

    def body(c_ref, w_ref, b_ref, mod_ref, act_ref):
        cv = c_ref[...]
        act = cv * (1.0 / (1.0 + jnp.exp(-cv)))
        act_ref[...] = act
        mod_ref[...] = _dot(act, w_ref[...]) + b_ref[...]

    return pl.pallas_call(
        body, name="ada_mod", grid=(n // tn,),
        out_shape=[jax.ShapeDtypeStruct((N_DEV, n), F32), jax.ShapeDtypeStruct((N_DEV, D_MODEL), F32)],
        in_specs=[_full((N_DEV, D_MODEL)), pl.BlockSpec((D_MODEL, tn), lambda i: (0, i)),
                  pl.BlockSpec((1, tn), lambda i: (0, i))],
        out_specs=[pl.BlockSpec((N_DEV, tn), lambda i: (0, i)), _full((N_DEV, D_MODEL))],
        compiler_params=_params("arbitrary"),
    )(c_all, w_shard, b_shard)


def _in_proj_kernel(x, vecs, w_in_t, comm=None):
    seq = x.shape[0]
    tm = 512

    def body(x_ref, v_ref, w_ref, proj_ref, h_ref):
        xv = x_ref[...]
        rstd = lax.rsqrt(_mean_last(xv * xv) + EPS)
        n1 = (xv * rstd) * v_ref[0:1, :]
        h = n1 * (1.0 + v_ref[2:3, :]) + v_ref[1:2, :]
        hb = h.astype(MXU_DTYPE)
        h_ref[...] = hb
        proj_ref[...] = _dot_nt(hb, w_ref[...])

    return _hosted_call(
        body, comm, name="in_proj", grid=(seq // tm,),
        out_shape=[jax.ShapeDtypeStruct((seq, IN_PROJ_WIDTH), F32),
                   jax.ShapeDtypeStruct((seq, D_MODEL), MXU_DTYPE)],
        in_specs=[pl.BlockSpec((tm, D_MODEL), lambda i: (i, 0)), _full((8, D_MODEL)),
                  _full((IN_PROJ_WIDTH, D_MODEL))],
        out_specs=[pl.BlockSpec((tm, IN_PROJ_WIDTH), lambda i: (i, 0)),
                   pl.BlockSpec((tm, D_MODEL), lambda i: (i, 0))],
        semantics=("arbitrary",),
    )(x, vecs, w_in_t)


def _mixer_fwd_kernel(proj, rope_tab, w_spatial, bias_full, sink_rows, comm=None):
    seq = proj.shape[0]
    nb = seq // CHUNK
    kv_col = (2 * GMLP_WIDTH + ATTN_WIDTH) // (2 * KV_WIDTH)

    def body(proj_ref, prev_ref, tab_ref, ptab_ref, w_ref, bias_ref, sink_ref, cat_ref):
        i = pl.program_id(0)
        wm, _, _ = _masked_spatial(w_ref)
        for j in range(GMLP_GROUPS // 2):
            cols = slice(LANES * j, LANES * (j + 1))
            vcols = slice(GMLP_WIDTH + LANES * j, GMLP_WIDTH + LANES * (j + 1))
            u, _ = _gelu_tanh(proj_ref[:, cols])
            vp, _ = _gelu_tanh(proj_ref[:, vcols])
            sv = _sgu_forward_pair(wm, vp, j) + bias_ref[:, cols]
            cat_ref[:, cols] = (u * sv).astype(cat_ref.dtype)
        o = 2 * GMLP_WIDTH
        tab = tab_ref[...]
        q_r = _rope_apply(proj_ref[:, o:o + ATTN_WIDTH], tab, 1.0)
        k_cur = _rope_apply(proj_ref[:, o + ATTN_WIDTH:o + ATTN_WIDTH + KV_WIDTH], tab, 1.0)
        k_prev = _rope_apply(prev_ref[:, 0:KV_WIDTH], ptab_ref[...], 1.0)
        k_a = jnp.concatenate([k_prev, k_cur], axis=0)
        v_a = jnp.concatenate([prev_ref[:, KV_WIDTH:2 * KV_WIDTH],
                               proj_ref[:, o + ATTN_WIDTH + KV_WIDTH:o + ATTN_WIDTH + 2 * KV_WIDTH]], axis=0)
        k_b = pltpu.roll(k_a, HEAD_DIM, 1)
        v_b = pltpu.roll(v_a, HEAD_DIM, 1)
        bias_t = _attn_bias_t(i == 0)
        lo, hi = _lane_masks((CHUNK, LANES))
        lo2, _ = _lane_masks((2 * CHUNK, LANES))
        for g in range(N_KV_HEADS):
            p_t, _ = _attn_probs_t(_group_dup(k_a, k_b, g, lo2), _group_rows(q_r, g, lo, hi), bias_t,
                                   _sink_row(sink_ref, g))
            o_t = _dot(_group_dup(v_a, v_b, g, lo2).T, p_t)
            for k, pair in enumerate(_pairs_from_rows(o_t.T, lo)):
                c0 = GMLP_WIDTH + LANES * (2 * g + k)
                cat_ref[:, c0:c0 + LANES] = pair.astype(cat_ref.dtype)

    return _hosted_call(
        body, comm, name="mixer_fwd", grid=(nb,),
        out_shape=[jax.ShapeDtypeStruct((seq, D_MODEL), MXU_DTYPE)],
        in_specs=[pl.BlockSpec((CHUNK, IN_PROJ_WIDTH), lambda i: (i, 0)),
                  pl.BlockSpec((CHUNK, 2 * KV_WIDTH), lambda i: (jnp.maximum(i - 1, 0), kv_col)),
                  pl.BlockSpec((CHUNK, 3 * LANES), lambda i: (i, 0)),
                  pl.BlockSpec((CHUNK, 3 * LANES), lambda i: (jnp.maximum(i - 1, 0), 0)),
                  _full((GMLP_GROUPS, CHUNK, CHUNK)), _full((CHUNK, GMLP_WIDTH)),
                  _full((N_Q_HEADS, LANES))],
        out_specs=[pl.BlockSpec((CHUNK, D_MODEL), lambda i: (i, 0))],
        semantics=("arbitrary",),
    )(proj, proj, rope_tab, rope_tab, w_spatial, bias_full, sink_rows)


def _trunk_kernel(x, target, cat, vecs, w_out, w_ff1, w_ff2):
    seq = x.shape[0]
    tm = 256
    nj = D_FF // D_MODEL

    def body(x_ref, t_ref, cat_ref, v_ref, wout_hbm, w1_hbm, w2_hbm,
             dx1_ref, dcat_ref, dmix_ref, h2_ref, r_ref, da_ref, dff_ref, sums_ref,
             wout, w1, w2, a_scr, sem):
        i = pl.program_id(0)

        @pl.when(i == 0)
        def _():
            copies = [pltpu.make_async_copy(wout_hbm, wout, sem.at[0]),
                      pltpu.make_async_copy(w1_hbm, w1, sem.at[1]),
                      pltpu.make_async_copy(w2_hbm, w2, sem.at[2])]
            for cp in copies:
                cp.start()
            for cp in copies:
                cp.wait()
            sums_ref[...] = jnp.zeros_like(sums_ref)

        gate1, shift2, scale2 = v_ref[0:1, :], v_ref[1:2, :], v_ref[2:3, :]
        gate2, g_ffn, g_final = v_ref[3:4, :], v_ref[4:5, :], v_ref[5:6, :]

        mix = _dot(cat_ref[...], wout[...])
        x1 = x_ref[...] + gate1 * mix
        rstd2 = lax.rsqrt(_mean_last(x1 * x1) + EPS)
        xh2 = x1 * rstd2
        n2 = xh2 * g_ffn
        h2b = (n2 * (1.0 + scale2) + shift2).astype(MXU_DTYPE)
        h2_ref[...] = h2b
        ff = jnp.zeros((tm, D_MODEL), F32)
        for j in range(nj):
            a = _dot(h2b, w1[j])
            a_scr[j] = a
            relu = jnp.maximum(a, 0.0)
            rb = (relu * relu).astype(MXU_DTYPE)
            r_ref[:, D_MODEL * j:D_MODEL * (j + 1)] = rb
            ff = ff + _dot(rb, w2[j])
        x2 = x1 + gate2 * ff
        rstd3 = lax.rsqrt(_mean_last(x2 * x2) + EPS)
        xh3 = x2 * rstd3
        err = xh3 * g_final - t_ref[...]
        loss = 0.5 * _rowsum(_mean_last(err * err))
        dy = err * (1.0 / D_MODEL)
        dxh3 = dy * g_final
        dx2 = rstd3 * (dxh3 - xh3 * _mean_last(dxh3 * xh3))
        dffb = (dx2 * gate2).astype(MXU_DTYPE)
        dff_ref[...] = dffb
        dh2 = jnp.zeros((tm, D_MODEL), F32)
        for j in range(nj):
            dr = _dot_nt(dffb, w2[j])
            dab = (dr * (2.0 * jnp.maximum(a_scr[j], 0.0))).astype(MXU_DTYPE)
            da_ref[:, D_MODEL * j:D_MODEL * (j + 1)] = dab
            dh2 = dh2 + _dot_nt(dab, w1[j])
        dn2 = dh2 * (1.0 + scale2)
        dxh2 = dn2 * g_ffn
        dx1 = dx2 + rstd2 * (dxh2 - xh2 * _mean_last(dxh2 * xh2))
        dx1_ref[...] = dx1
        dmixb = (dx1 * gate1).astype(MXU_DTYPE)
        dmix_ref[...] = dmixb
        dcat_ref[...] = _dot_nt(dmixb, wout[...])

        sums_ref[0:1, :] += _rowsum(dh2)
        sums_ref[1:2, :] += _rowsum(dh2 * n2)
        sums_ref[2:3, :] += _rowsum(dx2 * ff)
        sums_ref[3:4, :] += _rowsum(dn2 * xh2)
        sums_ref[4:5, :] += _rowsum(dy * xh3)
        sums_ref[5:6, :] += _rowsum(dx1 * mix)
        sums_ref[6:7, :] += jnp.broadcast_to(loss, (1, D_MODEL))

    tok = lambda w: pl.BlockSpec((tm, w), lambda i: (i, 0))
    return pl.pallas_call(
        body, name="trunk", grid=(seq // tm,),
        out_shape=[jax.ShapeDtypeStruct((seq, D_MODEL), F32), jax.ShapeDtypeStruct((seq, D_MODEL), F32),
                   jax.ShapeDtypeStruct((seq, D_MODEL), MXU_DTYPE), jax.ShapeDtypeStruct((seq, D_MODEL), MXU_DTYPE),
                   jax.ShapeDtypeStruct((seq, D_FF), MXU_DTYPE), jax.ShapeDtypeStruct((seq, D_FF), MXU_DTYPE),
                   jax.ShapeDtypeStruct((seq, D_MODEL), MXU_DTYPE), jax.ShapeDtypeStruct((8, D_MODEL), F32)],
        in_specs=[tok(D_MODEL), tok(D_MODEL), tok(D_MODEL), _full((8, D_MODEL)), _any(), _any(), _any()],
        out_specs=[tok(D_MODEL), tok(D_MODEL), tok(D_MODEL), tok(D_MODEL), tok(D_FF), tok(D_FF), tok(D_MODEL),
                   _full((8, D_MODEL))],
        scratch_shapes=[pltpu.VMEM((D_MODEL, D_MODEL), MXU_DTYPE), pltpu.VMEM((nj, D_MODEL, D_MODEL), MXU_DTYPE),
                        pltpu.VMEM((nj, D_MODEL, D_MODEL), MXU_DTYPE), pltpu.VMEM((nj, tm, D_MODEL), F32),
                        pltpu.SemaphoreType.DMA((3,))],
        compiler_params=_params("arbitrary"),
    )(x, target, cat, vecs, w_out, w_ff1, w_ff2)


def _mixer_bwd_kernel(proj, rope_tab, dcat, w_spatial, w_spatial_t, bias_full, sink_rows, comm=None):
    seq = proj.shape[0]
    nb = seq // CHUNK
    kv_col = (2 * GMLP_WIDTH + ATTN_WIDTH) // (2 * KV_WIDTH)

    def body(proj_ref, prev_ref, tab_ref, ptab_ref, dcat_ref, w_ref, wt_ref, bias_ref, sink_ref,
             dproj_ref, dw_ref, db_ref, dsink_ref, carry):
        step = pl.program_id(0)
        blk = nb - 1 - step

        @pl.when(step == 0)
        def _():
            carry[...] = jnp.zeros_like(carry)
            dw_ref[...] = jnp.zeros_like(dw_ref)
            db_ref[...] = jnp.zeros_like(db_ref)
            dsink_ref[...] = jnp.zeros_like(dsink_ref)

        wm, tril, triu = _masked_spatial(w_ref)
        lo, hi = _lane_masks((CHUNK, LANES))
        lane = lax.broadcasted_iota(jnp.int32, (CHUNK, LANES), 1)
        db = jnp.zeros((CHUNK, LANES), F32)
        for j in range(GMLP_GROUPS // 2):
            cols = slice(LANES * j, LANES * (j + 1))
            vcols = slice(GMLP_WIDTH + LANES * j, GMLP_WIDTH + LANES * (j + 1))
            zu, zv = proj_ref[:, cols], proj_ref[:, vcols]
            u, tu = _gelu_tanh(zu)
            vp, tv = _gelu_tanh(zv)
            sv = _sgu_forward_pair(wm, vp, j) + bias_ref[:, cols]
            dout = dcat_ref[:, cols]
            du = dout * sv
            dsv = dout * u
            dsv_lo, dsv_hi = jnp.where(lo, dsv, 0.0), jnp.where(hi, dsv, 0.0)
            lhs_t = jnp.concatenate([jnp.where(triu, wt_ref[2 * j], 0.0),
                                     jnp.where(triu, wt_ref[2 * j + 1], 0.0)], axis=1)
            dv = _dot(lhs_t, jnp.concatenate([dsv_lo, dsv_hi], axis=0))
            dw_ref[2 * j] += jnp.where(tril, _dot_nt(dsv_lo, vp), 0.0)
            dw_ref[2 * j + 1] += jnp.where(tril, _dot_nt(dsv_hi, vp), 0.0)
            db = db + (jnp.where(lane == 2 * j, jnp.sum(dsv_lo, axis=1, keepdims=True), 0.0)
                       + jnp.where(lane == 2 * j + 1, jnp.sum(dsv_hi, axis=1, keepdims=True), 0.0))
            dproj_ref[:, cols] = (du * _gelu_tanh_grad(zu, tu)).astype(dproj_ref.dtype)
            dproj_ref[:, vcols] = (dv * _gelu_tanh_grad(zv, tv)).astype(dproj_ref.dtype)
        db_ref[...] += db
        o = 2 * GMLP_WIDTH
        tab = tab_ref[...]
        q_r = _rope_apply(proj_ref[:, o:o + ATTN_WIDTH], tab, 1.0)
        k_cur = _rope_apply(proj_ref[:, o + ATTN_WIDTH:o + ATTN_WIDTH + KV_WIDTH], tab, 1.0)
        k_prev = _rope_apply(prev_ref[:, 0:KV_WIDTH], ptab_ref[...], 1.0)
        k_a = jnp.concatenate([k_prev, k_cur], axis=0)
        v_a = jnp.concatenate([prev_ref[:, KV_WIDTH:2 * KV_WIDTH],
                               proj_ref[:, o + ATTN_WIDTH + KV_WIDTH:o + ATTN_WIDTH + 2 * KV_WIDTH]], axis=0)
        k_b = pltpu.roll(k_a, HEAD_DIM, 1)
        v_b = pltpu.roll(v_a, HEAD_DIM, 1)
        bias_t = _attn_bias_t(blk == 0)
        lo2, _ = _lane_masks((2 * CHUNK, LANES))
        dout_b = dcat_ref[:, GMLP_WIDTH:GMLP_WIDTH + ATTN_WIDTH]
        dk_tot, dv_tot, dq_pairs = [], [], []
        for g in range(N_KV_HEADS):
            k_dup, v_dup = _group_dup(k_a, k_b, g, lo2), _group_dup(v_a, v_b, g, lo2)
            q_rows = _group_rows(q_r, g, lo, hi)
            do_rows = _group_rows(dout_b, g, lo, hi)
            p_t, p_sink = _attn_probs_t(k_dup, q_rows, bias_t, _sink_row(sink_ref, g))
            dp_t = _dot_nt(v_dup, do_rows)
            delta = jnp.sum(p_t * dp_t, axis=0, keepdims=True)
            ds_t = p_t * (dp_t - delta) * ATTN_SCALE
            dsink = -p_sink * delta
            for r in range(HEADS_PER_GROUP):
                h = HEADS_PER_GROUP * g + r
                dsink_ref[h:h + 1, :] += jnp.broadcast_to(
                    jnp.sum(dsink[:, LANES * r:LANES * (r + 1)], axis=1, keepdims=True), (1, LANES))
            dk_full = _dot(ds_t, q_rows)
            dv_full = _dot(p_t, do_rows)
            dk_tot.append(dk_full + pltpu.roll(dk_full, HEAD_DIM, 1))
            dv_tot.append(dv_full + pltpu.roll(dv_full, HEAD_DIM, 1))
            dq_t = _dot(k_dup.T, ds_t)
            dq_pairs += _pairs_from_rows(dq_t.T, lo)
        dk_all = jnp.where(lo2, dk_tot[0], dk_tot[1])
        dv_all = jnp.where(lo2, dv_tot[0], dv_tot[1])
        dk_cur = dk_all[CHUNK:, :] + carry[:, 0:KV_WIDTH]
        dv_cur = dv_all[CHUNK:, :] + carry[:, KV_WIDTH:2 * KV_WIDTH]
        carry[:, 0:KV_WIDTH] = dk_all[:CHUNK, :]
        carry[:, KV_WIDTH:2 * KV_WIDTH] = dv_all[:CHUNK, :]
        dq = _rope_apply(jnp.concatenate(dq_pairs, axis=1), tab, -1.0)
        dproj_ref[:, o:o + ATTN_WIDTH] = dq.astype(dproj_ref.dtype)
        dproj_ref[:, o + ATTN_WIDTH:o + ATTN_WIDTH + KV_WIDTH] = (
            _rope_apply(dk_cur, tab, -1.0).astype(dproj_ref.dtype))
        dproj_ref[:, o + ATTN_WIDTH + KV_WIDTH:o + ATTN_WIDTH + 2 * KV_WIDTH] = dv_cur.astype(dproj_ref.dtype)

    rev = lambda i: nb - 1 - i
    return _hosted_call(
        body, comm, name="mixer_bwd", grid=(nb,),
        out_shape=[jax.ShapeDtypeStruct((seq, IN_PROJ_WIDTH), MXU_DTYPE),
                   jax.ShapeDtypeStruct((GMLP_GROUPS, CHUNK, CHUNK), F32),
                   jax.ShapeDtypeStruct((CHUNK, LANES), F32),
                   jax.ShapeDtypeStruct((N_Q_HEADS, LANES), F32)],
        in_specs=[pl.BlockSpec((CHUNK, IN_PROJ_WIDTH), lambda i: (rev(i), 0)),
                  pl.BlockSpec((CHUNK, 2 * KV_WIDTH), lambda i: (jnp.maximum(rev(i) - 1, 0), kv_col)),
                  pl.BlockSpec((CHUNK, 3 * LANES), lambda i: (rev(i), 0)),
                  pl.BlockSpec((CHUNK, 3 * LANES), lambda i: (jnp.maximum(rev(i) - 1, 0), 0)),
                  pl.BlockSpec((CHUNK, D_MODEL), lambda i: (rev(i), 0)),
                  _full((GMLP_GROUPS, CHUNK, CHUNK)), _full((GMLP_GROUPS, CHUNK, CHUNK)),
                  _full((CHUNK, GMLP_WIDTH)), _full((N_Q_HEADS, LANES))],
        out_specs=[pl.BlockSpec((CHUNK, IN_PROJ_WIDTH), lambda i: (rev(i), 0)),
                   _full((GMLP_GROUPS, CHUNK, CHUNK)), _full((CHUNK, LANES)), _full((N_Q_HEADS, LANES))],
        scratch_shapes=[pltpu.VMEM((CHUNK, 2 * KV_WIDTH), F32)],
        semantics=("arbitrary",),
    )(proj, proj, rope_tab, rope_tab, dcat, w_spatial, w_spatial_t, bias_full, sink_rows)


def _in_proj_bwd_kernel(x, dx1, dproj, vecs, w_in_t, comm=None):
    seq = x.shape[0]
    tm = 512

    def body(x_ref, dx1_ref, dp_ref, v_ref, w_ref, gx_ref, sums_ref):
        @pl.when(pl.program_id(0) == 0)
        def _():
            sums_ref[...] = jnp.zeros_like(sums_ref)

        g_mix, scale1 = v_ref[0:1, :], v_ref[2:3, :]
        dh = _dot(dp_ref[...], w_ref[...])
        xv = x_ref[...]
        rstd = lax.rsqrt(_mean_last(xv * xv) + EPS)
        xh = xv * rstd
        dn1 = dh * (1.0 + scale1)
        dxh = dn1 * g_mix
        gx_ref[...] = dx1_ref[...] + rstd * (dxh - xh * _mean_last(dxh * xh))
        sums_ref[0:1, :] += _rowsum(dh)
        sums_ref[1:2, :] += _rowsum(dh * (xh * g_mix))
        sums_ref[2:3, :] += _rowsum(dn1 * xh)

    return _hosted_call(
        body, comm, name="in_proj_bwd", grid=(seq // tm,),
        out_shape=[jax.ShapeDtypeStruct((seq, D_MODEL), F32), jax.ShapeDtypeStruct((8, D_MODEL), F32)],
        in_specs=[pl.BlockSpec((tm, D_MODEL), lambda i: (i, 0)), pl.BlockSpec((tm, D_MODEL), lambda i: (i, 0)),
                  pl.BlockSpec((tm, IN_PROJ_WIDTH), lambda i: (i, 0)), _full((8, D_MODEL)),
                  _full((IN_PROJ_WIDTH, D_MODEL))],
        out_specs=[pl.BlockSpec((tm, D_MODEL), lambda i: (i, 0)), _full((8, D_MODEL))],
        semantics=("arbitrary",),
    )(x, dx1, dproj, vecs, w_in_t)


def _weight_grad_kernel(a, b, name, tm, tn, layout="plain", comm=None):
    seq, m = a.shape
    n = b.shape[1]
    tk = min(seq, 1024)
    nk = seq // tk

    def body(a_ref, b_ref, o_ref, acc):
        k = pl.program_id(2)

        @pl.when(k == 0)
        def _():
            acc[...] = jnp.zeros_like(acc)

        acc[...] += _dot_tn(a_ref[...], b_ref[...])

        @pl.when(k == nk - 1)
        def _():
            o_ref[...] = acc[...]

    if layout == "cols":
        out_shape = jax.ShapeDtypeStruct((n // tn, m, tn), F32)
        out_spec = pl.BlockSpec((None, tm, tn), lambda i, j, k: (j, i, 0))
    else:
        out_shape = jax.ShapeDtypeStruct((m, n), F32)
        out_spec = pl.BlockSpec((tm, tn), lambda i, j, k: (i, j))
    out = _hosted_call(
        body, comm, name=name, grid=(m // tm, n // tn, nk), out_shape=[out_shape],
        in_specs=[pl.BlockSpec((tk, tm), lambda i, j, k: (k, i)), pl.BlockSpec((tk, tn), lambda i, j, k: (k, j))],
        out_specs=[out_spec], scratch_shapes=[pltpu.VMEM((tm, tn), F32)],
        semantics=("arbitrary", "arbitrary", "arbitrary"),
    )(a, b)
    return out[0] if comm is None else out


def _row_tile(rows, most=256, sublanes=16):
    return max(t for t in range(sublanes, most + 1, sublanes) if rows % t == 0)


def _adam_update(w, g, m, v):
    m_new = ADAM_B1 * m + (1.0 - ADAM_B1) * g
    v_new = ADAM_B2 * v + (1.0 - ADAM_B2) * (g * g)
    m_hat = m_new / (1.0 - ADAM_B1 ** ADAM_STEP)
    v_hat = v_new / (1.0 - ADAM_B2 ** ADAM_STEP)
    delta = -ADAM_LR * (m_hat / (jnp.sqrt(v_hat) + ADAM_EPS) + ADAM_WD * w)
    return delta, m_new, v_new


def _add_halves_kernel(part, recv, c_idx, name):
    _, _, r, n = part.shape
    tr = _row_tile(r)

    def body(c_ref, p_ref, q_ref, o_ref):
        del c_ref
        o_ref[...] = (p_ref[...] + q_ref[...]).astype(o_ref.dtype)

    return pl.pallas_call(
        body, name=name, out_shape=jax.ShapeDtypeStruct((N_CHIPS, r, n), GRAD_COMM_DTYPE),
        grid_spec=pltpu.PrefetchScalarGridSpec(
            num_scalar_prefetch=1, grid=(N_CHIPS, r // tr),
            in_specs=[pl.BlockSpec((None, None, tr, n), lambda k, i, c: (k, c[0], i, 0)),
                      pl.BlockSpec((None, tr, n), lambda k, i, c: (k, i, 0))],
            out_specs=pl.BlockSpec((None, tr, n), lambda k, i, c: (k, i, 0))),
        compiler_params=_params("parallel", "parallel"),
    )(c_idx, part, recv)


def _sum_chips_kernel(own, others, place, name):
    _, r, n = own.shape
    tr = _row_tile(r)

    def body(place_ref, own_ref, oth_ref, o_ref):
        del place_ref
        acc = own_ref[...].astype(F32)
        for k in range(N_CHIPS - 1):
            acc = acc + oth_ref[k].astype(F32)
        o_ref[...] = acc

    return pl.pallas_call(
        body, name=name, out_shape=jax.ShapeDtypeStruct((2, r, n), F32),
        grid_spec=pltpu.PrefetchScalarGridSpec(
            num_scalar_prefetch=1, grid=(r // tr,),
            in_specs=[pl.BlockSpec((None, tr, n), lambda i, p: (p[0], i, 0)),
                      pl.BlockSpec((N_CHIPS - 1, tr, n), lambda i, p: (0, i, 0))],
            out_specs=pl.BlockSpec((None, tr, n), lambda i, p: (p[1], i, 0))),
        compiler_params=_params("parallel"),
    )(place, own, others)


def _adam_kernel(w, g, m, v, name):
    r, n = w.shape
    tr = _row_tile(r)

    def body(w_ref, g_ref, m_ref, v_ref, g_out, d_ref, mo_ref, vo_ref):
        gv = g_ref[:, 0:n]
        g_out[...] = gv
        d_ref[...], mo_ref[...], vo_ref[...] = _adam_update(w_ref[...], gv, m_ref[...], v_ref[...])

    spec = pl.BlockSpec((tr, n), lambda i: (i, 0))
    return pl.pallas_call(
        body, name=name, grid=(r // tr,), out_shape=[jax.ShapeDtypeStruct((r, n), F32)] * 4,
        in_specs=[spec, pl.BlockSpec((tr, g.shape[1]), lambda i: (i, 0)), spec, spec], out_specs=[spec] * 4,
        compiler_params=_params("parallel"),
    )(w, g, m, v)


SMALL_PARAMS = ("b_ada", "g_mix", "g_ffn", "g_final", "b_spatial", "sinks", "w_spatial")


def _small_update_kernel(gathered, params):
    shapes = [params[nm][0].shape for nm in SMALL_PARAMS]

    def body(*refs):
        g_refs, refs = refs[:5], refs[5:]
        p_refs, refs = refs[:3 * len(SMALL_PARAMS)], refs[3 * len(SMALL_PARAMS):]
        loss_ref, o_refs = refs[0], refs[1:]

        def total(ref):
            acc = ref[0]
            for k in range(1, N_DEV):
                acc = acc + ref[k]
            return acc

        s1, s2, db, ds, dw = (total(r) for r in g_refs)
        loss_ref[...] = jnp.broadcast_to(s2[6:7, 0:1], loss_ref.shape)
        grads = {"b_ada": [s1[0:1], s1[1:2], s2[5:6], s2[0:1], s2[1:2], s2[2:3]], "g_mix": [s1[2:3]],
                 "g_ffn": [s2[3:4]], "g_final": [s2[4:5]], "b_spatial": [db.T[0:GMLP_GROUPS]],
                 "w_spatial": [dw]}
        lane = lax.broadcasted_iota(jnp.int32, (1, LANES), 1)
        sink_row = jnp.zeros((1, LANES), F32)
        for h in range(N_Q_HEADS):
            sink_row = sink_row + jnp.where(lane == h, ds[h:h + 1, :], 0.0)
        grads["sinks"] = [sink_row[:, 0:N_Q_HEADS]]
        for i, nm in enumerate(SMALL_PARAMS):
            w_ref, m_ref, v_ref = p_refs[3 * i:3 * i + 3]
            outs = o_refs[4 * i:4 * i + 4]
            width = grads[nm][0].shape[1]
            for k, g in enumerate(grads[nm]):
                cols = slice(width * k, width * (k + 1))
                upd = _adam_update(w_ref[:, cols], g, m_ref[:, cols], v_ref[:, cols])
                for o_ref, val in zip(outs, (g,) + upd):
                    o_ref[:, cols] = val

    flat = [a for nm in SMALL_PARAMS for a in params[nm]]
    out_shape = [jax.ShapeDtypeStruct((8, LANES), F32)]
    out_shape += [jax.ShapeDtypeStruct(s, F32) for s in shapes for _ in range(4)]
    outs = pl.pallas_call(
        body, name="small_update", grid=(1,), out_shape=out_shape,
        in_specs=[_full(g.shape) for g in gathered] + [_full(a.shape) for a in flat],
        out_specs=[_full(s.shape) for s in out_shape],
        compiler_params=_params("arbitrary"),
    )(*gathered, *flat)
    return {nm: outs[1 + 4 * i:5 + 4 * i] for i, nm in enumerate(SMALL_PARAMS)}, outs[0]


def _ada_update_kernel(act_t, dmod, w, m, v):
    r, n = w.shape
    tr = 256

    def body(a_ref, d_ref, w_ref, m_ref, v_ref, g_ref, dl_ref, mo_ref, vo_ref):
        g = _dot(a_ref[...], d_ref[...])
        g_ref[...] = g
        dl_ref[...], mo_ref[...], vo_ref[...] = _adam_update(w_ref[...], g, m_ref[...], v_ref[...])

    spec = pl.BlockSpec((tr, n), lambda i: (i, 0))
    return pl.pallas_call(
        body, name="ada_update", grid=(r // tr,), out_shape=[jax.ShapeDtypeStruct((r, n), F32)] * 4,
        in_specs=[pl.BlockSpec((tr, N_DEV), lambda i: (i, 0)), _full((N_DEV, n)), spec, spec, spec],
        out_specs=[spec] * 4, compiler_params=_params("parallel"),
    )(act_t, dmod, w, m, v)


def kernel(x, c, positions, w_ada, b_ada, g_mix, w_in, w_spatial, b_spatial, sinks, w_out, g_ffn, w_ff1, w_ff2, g_final, loss_target, m_w_ada, m_b_ada, m_g_mix, m_w_in, m_w_spatial, m_b_spatial, m_sinks, m_w_out, m_g_ffn, m_w_ff1, m_w_ff2, m_g_final, v_w_ada, v_b_ada, v_g_mix, v_w_in, v_w_spatial, v_b_spatial, v_sinks, v_w_out, v_g_ffn, v_w_ff1, v_w_ff2, v_g_final):
    xi, yi, ci = lax.axis_index("x"), lax.axis_index("y"), lax.axis_index("c")
    chip = 2 * xi + yi
    dev = 2 * chip + ci
    seq = x.shape[1]
    x2, tgt = x[0], loss_target[0]
    ada_cols = w_ada.shape[2]

    c_all = _all_gather8([c], "gather_c")[0].reshape(N_DEV, D_MODEL)
    b_shard = lax.dynamic_slice(b_ada, (0, chip * ada_cols), (1, ada_cols))
    mod_part, act = _mod_kernel(c_all, w_ada[0], b_shard)
    mod_all, = _all_gather8([mod_part], "gather_mod")
    mod_me = lax.dynamic_index_in_dim(mod_all[0::2], dev, axis=1, keepdims=False)
    mod_me = mod_me.reshape(N_MOD, D_MODEL)
    shift1, scale1, gate1, shift2, scale2, gate2 = (mod_me[k:k + 1] for k in range(N_MOD))

    big = {"w_in": tuple(a[0].T for a in (w_in, m_w_in, v_w_in)),
           "w_out": (w_out[0], m_w_out[0], v_w_out[0]), "w_ff1": (w_ff1[0], m_w_ff1[0], v_w_ff1[0]),
           "w_ff2": (w_ff2[0], m_w_ff2[0], v_w_ff2[0])}

    def halves(nm):
        r, n = big[nm][0].shape
        return big[nm][0].astype(WEIGHT_COMM_DTYPE).reshape(2, r // 2, n)

    w_in_t, = _all_gather8([halves("w_in")], "gather_w_in", split=True)
    w_in_t = w_in_t.reshape(IN_PROJ_WIDTH, D_MODEL)

    zeros_row = jnp.zeros((1, D_MODEL), F32)
    vecs1 = jnp.concatenate([g_mix, shift1, scale1] + [zeros_row] * 5, axis=0)
    vecs2 = jnp.concatenate([gate1, shift2, scale2, gate2, g_ffn, g_final.reshape(1, D_MODEL)]
                            + [zeros_row] * 2, axis=0)
    bias_full = jnp.repeat(b_spatial[0].T, HEAD_DIM, axis=1)
    sink_rows = jnp.broadcast_to(sinks[0][:, None], (N_Q_HEADS, LANES))
    inv_freq = ROPE_THETA ** (-jnp.arange(0, ROT_DIM, 2, dtype=F32) / ROT_DIM)
    rope_tab = _rope_lane_tables(*_rope_angle_kernel(positions, inv_freq.reshape(ROT_DIM // 2, 1)))

    proj, hb, g_ff1 = _in_proj_kernel(x2, vecs1, w_in_t, comm=_gather_job([halves("w_ff1")]))
    cat, g_ff2, g_out = _mixer_fwd_kernel(proj, rope_tab, w_spatial[0], bias_full, sink_rows,
                                          comm=_gather_job([halves("w_ff2"), halves("w_out")]))
    g_ff1, g_ff2, g_out = _gather_forward([g_ff1, g_ff2, g_out], "gather_forward")
    w_out_full = g_out.reshape(D_MODEL, D_MODEL)
    w_ff1_blocks = g_ff1.reshape(N_CHIPS, D_MODEL, D_MODEL)
    w_ff2_blocks = g_ff2.reshape(N_CHIPS, D_MODEL, D_MODEL)
    dx1, dcat, dmix, h2b, rb, dab, dffb, sums2 = _trunk_kernel(
        x2, tgt, cat, vecs2, w_out_full, w_ff1_blocks, w_ff2_blocks)

    c_idx = ci.reshape(1).astype(jnp.int32)
    place = jnp.stack([chip, ci]).astype(jnp.int32)

    def halves_of(nm, g):
        r, n = big[nm][0].shape
        return g.reshape(N_CHIPS, 2, r // 2, n)

    p_ff2 = halves_of("w_ff2", _weight_grad_kernel(rb, dffb, "dw_ff2", 1024, 1024))
    dw_ff1, q_ff2 = _weight_grad_kernel(h2b, dab, "dw_ff1", 1024, 1024, layout="cols",
                                        comm=_sibling_job([p_ff2]))
    p_ff1 = halves_of("w_ff1", dw_ff1)
    dw_out, q_ff1 = _weight_grad_kernel(cat, dmix, "dw_out", 1024, 1024, comm=_sibling_job([p_ff1]))
    p_out = halves_of("w_out", dw_out)
    q_out, = _run_comm(_sibling_job([p_out]), "grad_to_sibling_w_out")
    cs_ff2 = _add_halves_kernel(p_ff2, q_ff2, c_idx, "grad_add_w_ff2")
    cs_ff1 = _add_halves_kernel(p_ff1, q_ff1, c_idx, "grad_add_w_ff1")
    cs_out = _add_halves_kernel(p_out, q_out, c_idx, "grad_add_w_out")
    dproj, dw_spatial, db_lanes, dsink_rows, sc_ff2, sc_out = _mixer_bwd_kernel(
        proj, rope_tab, dcat, w_spatial[0], w_spatial[0].transpose(0, 2, 1), bias_full, sink_rows,
        comm=_scatter_job([cs_ff2, cs_out]))
    dw_in, sc_ff1 = _weight_grad_kernel(dproj, hb, "dw_in", 2 * W_IN_BLOCK, 1024,
                                        comm=_scatter_job([cs_ff1]))
    grad_x, sums1 = _in_proj_bwd_kernel(x2, dx1, dproj, vecs1, w_in_t)
    p_in = halves_of("w_in", dw_in)
    q_in, = _run_comm(_sibling_job([p_in]), "grad_to_sibling_w_in")
    cs_in = _add_halves_kernel(p_in, q_in, c_idx, "grad_add_w_in")
    sc_in, = _run_comm(_scatter_job([cs_in]), "grad_to_chips_w_in")

    names = ["w_in", "w_out", "w_ff1", "w_ff2"]
    totals = [_sum_chips_kernel(own, oth, place, "grad_sum_" + nm)
              for nm, own, oth in zip(names, [cs_in, cs_out, cs_ff1, cs_ff2], [sc_in, sc_out, sc_ff1, sc_ff2])]
    shared = _sibling_share(totals, "grad_share")
    big_out = {}
    for nm, g in zip(names, shared):
        w, m, v = big[nm]
        outs = _adam_kernel(w, g.reshape(w.shape), m, v, "adam_" + nm)
        big_out[nm] = tuple((t.T if nm == "w_in" else t)[None] for t in outs)

    small = {"b_ada": (b_ada, m_b_ada, v_b_ada), "g_mix": (g_mix, m_g_mix, v_g_mix),
             "g_ffn": (g_ffn, m_g_ffn, v_g_ffn), "g_final": (g_final, m_g_final, v_g_final),
             "b_spatial": (b_spatial, m_b_spatial, v_b_spatial), "sinks": (sinks, m_sinks, v_sinks),
             "w_spatial": (w_spatial, m_w_spatial, v_w_spatial)}
    flat_shape = {"g_final": (1, D_MODEL), "b_spatial": (GMLP_GROUPS, CHUNK), "w_spatial": (GMLP_GROUPS * CHUNK, CHUNK)}
    gathered = _all_gather8([sums1, sums2, db_lanes, dsink_rows, dw_spatial.reshape(GMLP_GROUPS * CHUNK, CHUNK)],
                            "gather_small")
    small_out, loss_tile = _small_update_kernel(
        gathered, {nm: tuple(a.reshape(flat_shape.get(nm, a.shape)) for a in small[nm]) for nm in small})
    small_out = {nm: [o.reshape(small[nm][0].shape) for o in small_out[nm]] for nm in small}
    loss = loss_tile[0, 0]

    g1, g2 = gathered[0], gathered[1]
    dmod_all = jnp.concatenate([g1[:, 0], g1[:, 1], g2[:, 5], g2[:, 0], g2[:, 1], g2[:, 2]], axis=1)
    dmod_cols = lax.dynamic_slice(dmod_all, (0, chip * ada_cols), (N_DEV, ada_cols))
    ada = _ada_update_kernel(act.T, dmod_cols, w_ada[0], m_w_ada[0], v_w_ada[0])
    big_out["w_ada"] = tuple(t[None] for t in ada)

    order = ["w_ada", "b_ada", "g_mix", "w_in", "w_spatial", "b_spatial", "sinks", "w_out", "g_ffn",
             "w_ff1", "w_ff2", "g_final"]

    def leaf(nm, k):
        return big_out[nm][k] if nm in big_out else small_out[nm][k]

    outs = [loss, grad_x[None]]
    for k in range(4):
        outs += [leaf(nm, k) for nm in order]
    return tuple(outs)
```

```python
import math
from typing import Callable, NamedTuple

import jax
import jax.numpy as jnp
from jax import lax
from jax.experimental import pallas as pl
from jax.experimental.pallas import tpu as pltpu

F32 = jnp.float32
MXU_DTYPE = jnp.bfloat16
WEIGHT_COMM_DTYPE = jnp.bfloat16
GRAD_COMM_DTYPE = jnp.bfloat16

D_MODEL = 1024
D_FF = 4096
HEAD_DIM = 64
GMLP_GROUPS = 8
GMLP_WIDTH = 512
CHUNK = 128
N_Q_HEADS = 8
N_KV_HEADS = 2
ATTN_WIDTH = 512
KV_WIDTH = 128
ROT_DIM = 16
ROPE_THETA = 500000.0
IN_PROJ_WIDTH = 1792
N_MOD = 6
EPS = 1e-5
N_CHIPS = 4
N_DEV = 8
LANES = 128
W_IN_BLOCK = IN_PROJ_WIDTH // N_CHIPS

ADAM_LR = 0.001
ADAM_B1 = 0.9
ADAM_B2 = 0.999
ADAM_EPS = 1e-08
ADAM_WD = 0.01
ADAM_STEP = 10

VMEM_LIMIT_BYTES = 58 * 1024 * 1024
MESH = pl.DeviceIdType.MESH


def _params(*semantics):
    return pltpu.CompilerParams(dimension_semantics=semantics, vmem_limit_bytes=VMEM_LIMIT_BYTES)


def _dot(a, b):
    return jnp.dot(a.astype(MXU_DTYPE), b.astype(MXU_DTYPE), preferred_element_type=F32)


def _dot_nt(a, b):
    return lax.dot_general(a.astype(MXU_DTYPE), b.astype(MXU_DTYPE), (((1,), (1,)), ((), ())),
                           preferred_element_type=F32)


def _dot_tn(a, b):
    return lax.dot_general(a.astype(MXU_DTYPE), b.astype(MXU_DTYPE), (((0,), (0,)), ((), ())),
                           preferred_element_type=F32)


def _full(shape):
    return pl.BlockSpec(shape, lambda *_: (0,) * len(shape))


def _any():
    return pl.BlockSpec(memory_space=pl.ANY)


def _rowsum(v):
    return jnp.sum(v, axis=0, keepdims=True)


def _mean_last(v):
    return jnp.mean(v, axis=-1, keepdims=True)


class _Comm(NamedTuple):
    operands: tuple
    out_shapes: tuple
    n_sems: int
    make: Callable


def _hosted_call(body, comm, *, name, grid, in_specs, out_shape, out_specs, scratch_shapes=(), semantics,
                 n_prefetch=0):
    if comm is None:
        return pl.pallas_call(
            body, name=name, out_shape=out_shape, compiler_params=_params(*semantics),
            grid_spec=pltpu.PrefetchScalarGridSpec(
                num_scalar_prefetch=n_prefetch, grid=grid, in_specs=in_specs, out_specs=out_specs,
                scratch_shapes=list(scratch_shapes)))
    n_in, n_out, n_scr = len(in_specs), len(out_shape), len(scratch_shapes)
    k_in, k_out = len(comm.operands), len(comm.out_shapes)

    def hosted(*refs):
        prefetched, refs = refs[:n_prefetch], refs[n_prefetch:]
        ins, refs = refs[:n_in], refs[n_in:]
        c_ins, refs = refs[:k_in], refs[k_in:]
        outs, refs = refs[:n_out], refs[n_out:]
        c_outs, refs = refs[:k_out], refs[k_out:]
        scratch, (send_sems, recv_sems) = refs[:n_scr], refs[n_scr:]
        first, last = None, None
        for d, size in enumerate(grid):
            at_start, at_end = pl.program_id(d) == 0, pl.program_id(d) == size - 1
            first = at_start if first is None else first & at_start
            last = at_end if last is None else last & at_end

        @pl.when(first)
        def _():
            for cp in comm.make(c_ins, c_outs, send_sems, recv_sems)[0]:
                cp.start()

        body(*prefetched, *ins, *outs, *scratch)

        @pl.when(last)
        def _():
            for wait in comm.make(c_ins, c_outs, send_sems, recv_sems)[1]:
                wait()

    call = pl.pallas_call(
        hosted, name=name, out_shape=list(out_shape) + list(comm.out_shapes),
        compiler_params=_params(*semantics),
        grid_spec=pltpu.PrefetchScalarGridSpec(
            num_scalar_prefetch=n_prefetch, grid=grid, in_specs=list(in_specs) + [_any()] * k_in,
            out_specs=list(out_specs) + [_any()] * k_out,
            scratch_shapes=list(scratch_shapes) + [pltpu.SemaphoreType.DMA((comm.n_sems,)),
                                                    pltpu.SemaphoreType.DMA((comm.n_sems,))]))
    return lambda *args: call(*args, *comm.operands)


def _mesh_place():
    x, y, c = lax.axis_index("x"), lax.axis_index("y"), lax.axis_index("c")
    return x, y, c, [(1 - x, y), (x, 1 - y), (1 - x, 1 - y)]


def _gather_job(halves):
    per = 5

    def make(ins, outs, send_sems, recv_sems):
        x, y, c, chips = _mesh_place()
        starts, waits = [], []
        for a, (src, out) in enumerate(zip(ins, outs)):
            mine = out.at[4 * x + 2 * y + c]
            local = pltpu.make_async_copy(src.at[c], mine, send_sems.at[per * a + 4])
            to = [(x, y, 1 - c)] + [(px, py, c) for px, py in chips]
            sends = [pltpu.make_async_remote_copy(
                src_ref=src.at[c], dst_ref=mine, send_sem=send_sems.at[per * a + k],
                recv_sem=recv_sems.at[per * a + k], device_id=dev, device_id_type=MESH)
                for k, dev in enumerate(to)]
            recvs = [pltpu.make_async_remote_copy(
                src_ref=src.at[c], dst_ref=out.at[4 * px + 2 * py + pc], send_sem=send_sems.at[per * a + k],
                recv_sem=recv_sems.at[per * a + k], device_id=(px, py, pc), device_id_type=MESH)
                for k, (px, py, pc) in enumerate(to)]
            starts += [local] + sends
            waits += [local.wait] + [s.wait_send for s in sends] + [r.wait_recv for r in recvs]
        return starts, waits

    return _Comm(tuple(halves), tuple(jax.ShapeDtypeStruct((N_DEV,) + h.shape[1:], h.dtype) for h in halves),
                 per * len(halves), make)


def _gather_forward(bufs, name):
    n_arr = len(bufs)

    def body(*refs):
        outs = refs[n_arr:2 * n_arr]
        send_sems, recv_sems = refs[2 * n_arr:]
        x, y, c, chips = _mesh_place()
        sends, recvs = [], []
        for a, buf in enumerate(outs):
            for j, (px, py) in enumerate(chips):
                mine, theirs = buf.at[4 * px + 2 * py + c], buf.at[4 * px + 2 * py + 1 - c]
                sems = dict(send_sem=send_sems.at[3 * a + j], recv_sem=recv_sems.at[3 * a + j],
                            device_id=(x, y, 1 - c), device_id_type=MESH)
                sends.append(pltpu.make_async_remote_copy(src_ref=mine, dst_ref=mine, **sems))
                recvs.append(pltpu.make_async_remote_copy(src_ref=mine, dst_ref=theirs, **sems))
        for cp in sends:
            cp.start()
        for s, r in zip(sends, recvs):
            s.wait_send()
            r.wait_recv()

    return pl.pallas_call(
        body, name=name, out_shape=[jax.ShapeDtypeStruct(b.shape, b.dtype) for b in bufs],
        in_specs=[_any()] * n_arr, out_specs=[_any()] * n_arr,
        input_output_aliases={a: a for a in range(n_arr)},
        scratch_shapes=[pltpu.SemaphoreType.DMA((3 * n_arr,)), pltpu.SemaphoreType.DMA((3 * n_arr,))],
    )(*bufs)


def _scatter_job(chip_sums):
    def make(ins, outs, send_sems, recv_sems):
        x, y, c, chips = _mesh_place()
        copies = [pltpu.make_async_remote_copy(
            src_ref=src.at[2 * px + py], dst_ref=out.at[j], send_sem=send_sems.at[3 * a + j],
            recv_sem=recv_sems.at[3 * a + j], device_id=(px, py, c), device_id_type=MESH)
            for a, (src, out) in enumerate(zip(ins, outs)) for j, (px, py) in enumerate(chips)]
        return copies, [cp.wait for cp in copies]

    return _Comm(tuple(chip_sums), tuple(jax.ShapeDtypeStruct((3,) + s.shape[1:], s.dtype) for s in chip_sums),
                 3 * len(chip_sums), make)


def _run_comm(comm, name):
    def body(token_ref):
        token_ref[...] = jnp.zeros_like(token_ref)

    out = _hosted_call(body, comm, name=name, grid=(1,), in_specs=[],
                       out_shape=[jax.ShapeDtypeStruct((8, LANES), F32)], out_specs=[_full((8, LANES))],
                       semantics=("arbitrary",))()
    return out[1:]


def _all_gather8(blocks, name, split=False):
    n_arr = len(blocks)

    def body(*refs):
        x_refs, out_refs = refs[:n_arr], refs[n_arr:2 * n_arr]
        send_sems, recv_sems, local_sems = refs[2 * n_arr:]
        x, y, c, chips = _mesh_place()
        me, sibling = (x, y, c), (x, y, 1 - c)
        arrays = []
        for a, (x_ref, out_ref) in enumerate(zip(x_refs, out_refs)):
            src_mine = x_ref.at[c] if split else x_ref

            def copy(k, blk, to, src=None, a=a, out_ref=out_ref):
                dst = out_ref.at[4 * blk[0] + 2 * blk[1] + blk[2]]
                return pltpu.make_async_remote_copy(
                    src_ref=dst if src is None else src, dst_ref=dst,
                    send_sem=send_sems.at[7 * a + k], recv_sem=recv_sems.at[7 * a + k],
                    device_id=to, device_id_type=MESH)

            mine = pltpu.make_async_copy(src_mine, out_ref.at[4 * x + 2 * y + c], local_sems.at[a])
            mine.start()
            first = [copy(0, me, sibling, src=src_mine)]
            first += [copy(1 + j, me, (*chip, c), src=src_mine) for j, chip in enumerate(chips)]
            for cp in first:
                cp.start()
            arrays.append((copy, mine, first))
        sent = []
        for copy, mine, first in arrays:
            passed = [copy(4 + j, (*chip, c), sibling) for j, chip in enumerate(chips)]
            for j, chip in enumerate(chips):
                copy(1 + j, (*chip, c), me).wait_recv()
                passed[j].start()
            sent += first + passed
        for copy, mine, first in arrays:
            copy(0, sibling, me).wait_recv()
            for j, chip in enumerate(chips):
                copy(4 + j, (*chip, 1 - c), me).wait_recv()
            mine.wait()
        for cp in sent:
            cp.wait_send()

    return pl.pallas_call(
        body, name=name,
        out_shape=[jax.ShapeDtypeStruct((N_DEV,) + tuple(b.shape[1:] if split else b.shape), b.dtype)
                   for b in blocks],
        in_specs=[_any()] * n_arr, out_specs=[_any()] * n_arr,
        scratch_shapes=[pltpu.SemaphoreType.DMA((7 * n_arr,)), pltpu.SemaphoreType.DMA((7 * n_arr,)),
                        pltpu.SemaphoreType.DMA((n_arr,))],
    )(*blocks)


def _sibling_share(bufs, name):
    n_arr = len(bufs)

    def body(*refs):
        out_refs = refs[n_arr:2 * n_arr]
        send_sems, recv_sems = refs[2 * n_arr:]
        x, y, c = lax.axis_index("x"), lax.axis_index("y"), lax.axis_index("c")
        copies = [pltpu.make_async_remote_copy(
            src_ref=out_refs[a].at[c], dst_ref=out_refs[a].at[c],
            send_sem=send_sems.at[a], recv_sem=recv_sems.at[a],
            device_id=(x, y, 1 - c), device_id_type=MESH) for a in range(n_arr)]
        for cp in copies:
            cp.start()
        for a in range(n_arr):
            pltpu.make_async_remote_copy(
                src_ref=out_refs[a].at[c], dst_ref=out_refs[a].at[1 - c],
                send_sem=send_sems.at[a], recv_sem=recv_sems.at[a],
                device_id=(x, y, 1 - c), device_id_type=MESH).wait()

    return pl.pallas_call(
        body, name=name,
        out_shape=[jax.ShapeDtypeStruct(b.shape, b.dtype) for b in bufs],
        in_specs=[_any()] * n_arr, out_specs=[_any()] * n_arr,
        input_output_aliases={a: a for a in range(n_arr)},
        scratch_shapes=[pltpu.SemaphoreType.DMA((n_arr,)), pltpu.SemaphoreType.DMA((n_arr,))],
    )(*bufs)


def _gelu_tanh(z):
    k = math.sqrt(2.0 / math.pi)
    t = jnp.tanh(k * (z + 0.044715 * (z * z * z)))
    return 0.5 * z * (1.0 + t), t


def _gelu_tanh_grad(z, t):
    k = math.sqrt(2.0 / math.pi)
    return 0.5 * (1.0 + t) + 0.5 * z * (1.0 - t * t) * (k * (1.0 + 3.0 * 0.044715 * (z * z)))


def _rope_angle_kernel(pos_row, invf_col):
    seq = pos_row.shape[1]

    def body(p_ref, f_ref, cos_ref, sin_ref):
        ang = p_ref[...].astype(F32) * f_ref[...]
        cos_ref[...] = jnp.cos(ang)
        sin_ref[...] = jnp.sin(ang)

    return pl.pallas_call(
        body, name="rope_angles", grid=(1,), out_shape=[jax.ShapeDtypeStruct((ROT_DIM // 2, seq), F32)] * 2,
        in_specs=[_full((1, seq)), _full((ROT_DIM // 2, 1))], out_specs=[_full((ROT_DIM // 2, seq))] * 2,
        compiler_params=_params("arbitrary"),
    )(pos_row, invf_col)


def _rope_lane_tables(cos, sin):
    cos_t, sin_t = cos.T, sin.T
    seq, half = cos_t.shape
    ones = jnp.ones((seq, HEAD_DIM - ROT_DIM), F32)
    c64 = jnp.concatenate([cos_t, cos_t, ones], axis=1)
    s1 = jnp.concatenate([sin_t, jnp.zeros((seq, HEAD_DIM - half), F32)], axis=1)
    s2 = jnp.concatenate([jnp.zeros((seq, half), F32), sin_t, jnp.zeros((seq, HEAD_DIM - ROT_DIM), F32)], axis=1)
    return jnp.concatenate([jnp.tile(t, (1, LANES // HEAD_DIM)) for t in (c64, s1, s2)], axis=1)


def _rope_apply(t, tab, sign):
    reps = t.shape[1] // LANES
    c_tab, s1, s2 = (jnp.tile(tab[:, LANES * k:LANES * (k + 1)], (1, reps)) if reps > 1
                     else tab[:, LANES * k:LANES * (k + 1)] for k in range(3))
    half = ROT_DIM // 2
    up = pltpu.roll(t, t.shape[1] - half, 1)
    down = pltpu.roll(t, half, 1)
    return t * c_tab + sign * (down * s2 - up * s1)


def _lane_masks(shape):
    lane = lax.broadcasted_iota(jnp.int32, shape, 1)
    return lane < HEAD_DIM, lane >= HEAD_DIM


HEADS_PER_GROUP = N_Q_HEADS // N_KV_HEADS
ATTN_SCALE = 1.0 / math.sqrt(HEAD_DIM)


def _attn_bias_t(first_block):
    kj = lax.broadcasted_iota(jnp.int32, (2 * CHUNK, CHUNK), 0)
    qi = lax.broadcasted_iota(jnp.int32, (2 * CHUNK, CHUNK), 1)
    ok = (kj > qi) & (kj <= qi + CHUNK) & (jnp.logical_not(first_block) | (kj >= CHUNK))
    return jnp.tile(jnp.where(ok, 0.0, -jnp.inf), (1, HEADS_PER_GROUP))


def _group_rows(x, g, lo, hi):
    rows = []
    for r in range(HEADS_PER_GROUP):
        h = HEADS_PER_GROUP * g + r
        pair = x[:, LANES * (h // 2):LANES * (h // 2 + 1)]
        rows.append(jnp.where(hi if h % 2 else lo, pair, 0.0))
    return jnp.concatenate(rows, axis=0)


def _pairs_from_rows(rows, lo):
    return [jnp.where(lo, rows[2 * CHUNK * k:2 * CHUNK * k + CHUNK], rows[2 * CHUNK * k + CHUNK:2 * CHUNK * (k + 1)])
            for k in range(HEADS_PER_GROUP // 2)]


def _group_dup(a, b, g, lo2):
    return jnp.where(lo2, a, b) if g == 0 else jnp.where(lo2, b, a)


def _sink_row(sink_ref, g):
    return jnp.concatenate([sink_ref[HEADS_PER_GROUP * g + r:HEADS_PER_GROUP * g + r + 1, :]
                            for r in range(HEADS_PER_GROUP)], axis=1)


def _attn_probs_t(k_dup, q_rows, bias_t, sink_row):
    s_t = _dot_nt(k_dup, q_rows) * ATTN_SCALE + bias_t
    m = jnp.maximum(jnp.max(s_t, axis=0, keepdims=True), sink_row)
    p = jnp.exp(s_t - m)
    e_sink = jnp.exp(sink_row - m)
    inv = 1.0 / (jnp.sum(p, axis=0, keepdims=True) + e_sink)
    return p * inv, e_sink * inv


def _sgu_forward_pair(wm, vp, j):
    lo, hi = _lane_masks(vp.shape)
    lhs = jnp.concatenate([wm[2 * j], wm[2 * j + 1]], axis=1)
    rhs = jnp.concatenate([jnp.where(lo, vp, 0.0), jnp.where(hi, vp, 0.0)], axis=0)
    return _dot(lhs, rhs)


def _masked_spatial(w_ref):
    t = lax.broadcasted_iota(jnp.int32, (CHUNK, CHUNK), 0)
    s = lax.broadcasted_iota(jnp.int32, (CHUNK, CHUNK), 1)
    tril = s <= t
    return [jnp.where(tril, w_ref[g], 0.0) for g in range(GMLP_GROUPS)], tril, s >= t


def _mod_kernel(c_all, w_shard, b_shard):
    n = w_shard.shape[1]
    tn = 512

    def body(c_ref, w_ref, b_ref, mod_ref, act_ref):
        cv = c_ref[...]
        act = cv * (1.0 / (1.0 + jnp.exp(-cv)))
        act_ref[...] = act
        mod_ref[...] = _dot(act, w_ref[...]) + b_ref[...]

    return pl.pallas_call(
        body, name="ada_mod", grid=(n // tn,),
        out_shape=[jax.ShapeDtypeStruct((N_DEV, n), F32), jax.ShapeDtypeStruct((N_DEV, D_MODEL), F32)],
        in_specs=[_full((N_DEV, D_MODEL)), pl.BlockSpec((D_MODEL, tn), lambda i: (0, i)),
                  pl.BlockSpec((1, tn), lambda i: (0, i))],
        out_specs=[pl.BlockSpec((N_DEV, tn), lambda i: (0, i)), _full((N_DEV, D_MODEL))],
        compiler_params=_params("arbitrary"),
    )(c_all, w_shard, b_shard)


def _in_proj_kernel(x, vecs, w_in_t, comm=None):
    seq = x.shape[0]
    tm = 512

    def body(x_ref, v_ref, w_ref, proj_ref, h_ref):
        xv = x_ref[...]
        rstd = lax.rsqrt(_mean_last(xv * xv) + EPS)
        n1 = (xv * rstd) * v_ref[0:1, :]
        h = n1 * (1.0 + v_ref[2:3, :]) + v_ref[1:2, :]
        hb = h.astype(MXU_DTYPE)
        h_ref[...] = hb
        proj_ref[...] = _dot_nt(hb, w_ref[...])

    return _hosted_call(
        body, comm, name="in_proj", grid=(seq // tm,),
        out_shape=[jax.ShapeDtypeStruct((seq, IN_PROJ_WIDTH), F32),
                   jax.ShapeDtypeStruct((seq, D_MODEL), MXU_DTYPE)],
        in_specs=[pl.BlockSpec((tm, D_MODEL), lambda i: (i, 0)), _full((8, D_MODEL)),
                  _full((IN_PROJ_WIDTH, D_MODEL))],
        out_specs=[pl.BlockSpec((tm, IN_PROJ_WIDTH), lambda i: (i, 0)),
                   pl.BlockSpec((tm, D_MODEL), lambda i: (i, 0))],
        semantics=("arbitrary",),
    )(x, vecs, w_in_t)


def _mixer_fwd_kernel(proj, rope_tab, w_spatial, bias_full, sink_rows, comm=None):
    seq = proj.shape[0]
    nb = seq // CHUNK
    kv_col = (2 * GMLP_WIDTH + ATTN_WIDTH) // (2 * KV_WIDTH)

    def body(proj_ref, prev_ref, tab_ref, ptab_ref, w_ref, bias_ref, sink_ref, cat_ref):
        i = pl.program_id(0)
        wm, _, _ = _masked_spatial(w_ref)
        for j in range(GMLP_GROUPS // 2):
            cols = slice(LANES * j, LANES * (j + 1))
            vcols = slice(GMLP_WIDTH + LANES * j, GMLP_WIDTH + LANES * (j + 1))
            u, _ = _gelu_tanh(proj_ref[:, cols])
            vp, _ = _gelu_tanh(proj_ref[:, vcols])
            sv = _sgu_forward_pair(wm, vp, j) + bias_ref[:, cols]
            cat_ref[:, cols] = (u * sv).astype(cat_ref.dtype)
        o = 2 * GMLP_WIDTH
        tab = tab_ref[...]
        q_r = _rope_apply(proj_ref[:, o:o + ATTN_WIDTH], tab, 1.0)
        k_cur = _rope_apply(proj_ref[:, o + ATTN_WIDTH:o + ATTN_WIDTH + KV_WIDTH], tab, 1.0)
        k_prev = _rope_apply(prev_ref[:, 0:KV_WIDTH], ptab_ref[...], 1.0)
        k_a = jnp.concatenate([k_prev, k_cur], axis=0)
        v_a = jnp.concatenate([prev_ref[:, KV_WIDTH:2 * KV_WIDTH],
                               proj_ref[:, o + ATTN_WIDTH + KV_WIDTH:o + ATTN_WIDTH + 2 * KV_WIDTH]], axis=0)
        k_b = pltpu.roll(k_a, HEAD_DIM, 1)
        v_b = pltpu.roll(v_a, HEAD_DIM, 1)
        bias_t = _attn_bias_t(i == 0)
        lo, hi = _lane_masks((CHUNK, LANES))
        lo2, _ = _lane_masks((2 * CHUNK, LANES))
        for g in range(N_KV_HEADS):
            p_t, _ = _attn_probs_t(_group_dup(k_a, k_b, g, lo2), _group_rows(q_r, g, lo, hi), bias_t,
                                   _sink_row(sink_ref, g))
            o_t = _dot(_group_dup(v_a, v_b, g, lo2).T, p_t)
            for k, pair in enumerate(_pairs_from_rows(o_t.T, lo)):
                c0 = GMLP_WIDTH + LANES * (2 * g + k)
                cat_ref[:, c0:c0 + LANES] = pair.astype(cat_ref.dtype)

    return _hosted_call(
        body, comm, name="mixer_fwd", grid=(nb,),
        out_shape=[jax.ShapeDtypeStruct((seq, D_MODEL), MXU_DTYPE)],
        in_specs=[pl.BlockSpec((CHUNK, IN_PROJ_WIDTH), lambda i: (i, 0)),
                  pl.BlockSpec((CHUNK, 2 * KV_WIDTH), lambda i: (jnp.maximum(i - 1, 0), kv_col)),
                  pl.BlockSpec((CHUNK, 3 * LANES), lambda i: (i, 0)),
                  pl.BlockSpec((CHUNK, 3 * LANES), lambda i: (jnp.maximum(i - 1, 0), 0)),
                  _full((GMLP_GROUPS, CHUNK, CHUNK)), _full((CHUNK, GMLP_WIDTH)),
                  _full((N_Q_HEADS, LANES))],
        out_specs=[pl.BlockSpec((CHUNK, D_MODEL), lambda i: (i, 0))],
        semantics=("arbitrary",),
    )(proj, proj, rope_tab, rope_tab, w_spatial, bias_full, sink_rows)


def _trunk_kernel(x, target, cat, vecs, w_out, w_ff1, w_ff2):
    seq = x.shape[0]
    tm = 256
    nj = D_FF // D_MODEL

    def body(x_ref, t_ref, cat_ref, v_ref, wout_hbm, w1_hbm, w2_hbm,
             dx1_ref, dcat_ref, dmix_ref, h2_ref, r_ref, da_ref, dff_ref, sums_ref,
             wout, w1, w2, a_scr, sem):
        i = pl.program_id(0)

        @pl.when(i == 0)
        def _():
            copies = [pltpu.make_async_copy(wout_hbm, wout, sem.at[0]),
                      pltpu.make_async_copy(w1_hbm, w1, sem.at[1]),
                      pltpu.make_async_copy(w2_hbm, w2, sem.at[2])]
            for cp in copies:
                cp.start()
            for cp in copies:
                cp.wait()
            sums_ref[...] = jnp.zeros_like(sums_ref)

        gate1, shift2, scale2 = v_ref[0:1, :], v_ref[1:2, :], v_ref[2:3, :]
        gate2, g_ffn, g_final = v_ref[3:4, :], v_ref[4:5, :], v_ref[5:6, :]

        mix = _dot(cat_ref[...], wout[...])
        x1 = x_ref[...] + gate1 * mix
        rstd2 = lax.rsqrt(_mean_last(x1 * x1) + EPS)
        xh2 = x1 * rstd2
        n2 = xh2 * g_ffn
        h2b = (n2 * (1.0 + scale2) + shift2).astype(MXU_DTYPE)
        h2_ref[...] = h2b
        ff = jnp.zeros((tm, D_MODEL), F32)
        for j in range(nj):
            a = _dot(h2b, w1[j])
            a_scr[j] = a
            relu = jnp.maximum(a, 0.0)
            rb = (relu * relu).astype(MXU_DTYPE)
            r_ref[:, D_MODEL * j:D_MODEL * (j + 1)] = rb
            ff = ff + _dot(rb, w2[j])
        x2 = x1 + gate2 * ff
        rstd3 = lax.rsqrt(_mean_last(x2 * x2) + EPS)
        xh3 = x2 * rstd3
        err = xh3 * g_final - t_ref[...]
        loss = 0.5 * _rowsum(_mean_last(err * err))
        dy = err * (1.0 / D_MODEL)
        dxh3 = dy * g_final
        dx2 = rstd3 * (dxh3 - xh3 * _mean_last(dxh3 * xh3))
        dffb = (dx2 * gate2).astype(MXU_DTYPE)
        dff_ref[...] = dffb
        dh2 = jnp.zeros((tm, D_MODEL), F32)
        for j in range(nj):
            dr = _dot_nt(dffb, w2[j])
            dab = (dr * (2.0 * jnp.maximum(a_scr[j], 0.0))).astype(MXU_DTYPE)
            da_ref[:, D_MODEL * j:D_MODEL * (j + 1)] = dab
            dh2 = dh2 + _dot_nt(dab, w1[j])
        dn2 = dh2 * (1.0 + scale2)
        dxh2 = dn2 * g_ffn
        dx1 = dx2 + rstd2 * (dxh2 - xh2 * _mean_last(dxh2 * xh2))
        dx1_ref[...] = dx1
        dmixb = (dx1 * gate1).astype(MXU_DTYPE)
        dmix_ref[...] = dmixb
        dcat_ref[...] = _dot_nt(dmixb, wout[...])

        sums_ref[0:1, :] += _rowsum(dh2)
        sums_ref[1:2, :] += _rowsum(dh2 * n2)
        sums_ref[2:3, :] += _rowsum(dx2 * ff)
        sums_ref[3:4, :] += _rowsum(dn2 * xh2)
        sums_ref[4:5, :] += _rowsum(dy * xh3)
        sums_ref[5:6, :] += _rowsum(dx1 * mix)
        sums_ref[6:7, :] += jnp.broadcast_to(loss, (1, D_MODEL))

    tok = lambda w: pl.BlockSpec((tm, w), lambda i: (i, 0))
    return pl.pallas_call(
        body, name="trunk", grid=(seq // tm,),
        out_shape=[jax.ShapeDtypeStruct((seq, D_MODEL), F32), jax.ShapeDtypeStruct((seq, D_MODEL), F32),
                   jax.ShapeDtypeStruct((seq, D_MODEL), MXU_DTYPE), jax.ShapeDtypeStruct((seq, D_MODEL), MXU_DTYPE),
                   jax.ShapeDtypeStruct((seq, D_FF), MXU_DTYPE), jax.ShapeDtypeStruct((seq, D_FF), MXU_DTYPE),
                   jax.ShapeDtypeStruct((seq, D_MODEL), MXU_DTYPE), jax.ShapeDtypeStruct((8, D_MODEL), F32)],
        in_specs=[tok(D_MODEL), tok(D_MODEL), tok(D_MODEL), _full((8, D_MODEL)), _any(), _any(), _any()],
        out_specs=[tok(D_MODEL), tok(D_MODEL), tok(D_MODEL), tok(D_MODEL), tok(D_FF), tok(D_FF), tok(D_MODEL),
                   _full((8, D_MODEL))],
        scratch_shapes=[pltpu.VMEM((D_MODEL, D_MODEL), MXU_DTYPE), pltpu.VMEM((nj, D_MODEL, D_MODEL), MXU_DTYPE),
                        pltpu.VMEM((nj, D_MODEL, D_MODEL), MXU_DTYPE), pltpu.VMEM((nj, tm, D_MODEL), F32),
                        pltpu.SemaphoreType.DMA((3,))],
        compiler_params=_params("arbitrary"),
    )(x, target, cat, vecs, w_out, w_ff1, w_ff2)


def _mixer_bwd_kernel(proj, rope_tab, dcat, w_spatial, w_spatial_t, bias_full, sink_rows, comm=None):
    seq = proj.shape[0]
    nb = seq // CHUNK
    kv_col = (2 * GMLP_WIDTH + ATTN_WIDTH) // (2 * KV_WIDTH)

    def body(proj_ref, prev_ref, tab_ref, ptab_ref, dcat_ref, w_ref, wt_ref, bias_ref, sink_ref,
             dproj_ref, dw_ref, db_ref, dsink_ref, carry):
        step = pl.program_id(0)
        blk = nb - 1 - step

        @pl.when(step == 0)
        def _():
            carry[...] = jnp.zeros_like(carry)
            dw_ref[...] = jnp.zeros_like(dw_ref)
            db_ref[...] = jnp.zeros_like(db_ref)
            dsink_ref[...] = jnp.zeros_like(dsink_ref)

        wm, tril, triu = _masked_spatial(w_ref)
        lo, hi = _lane_masks((CHUNK, LANES))
        lane = lax.broadcasted_iota(jnp.int32, (CHUNK, LANES), 1)
        db = jnp.zeros((CHUNK, LANES), F32)
        for j in range(GMLP_GROUPS // 2):
            cols = slice(LANES * j, LANES * (j + 1))
            vcols = slice(GMLP_WIDTH + LANES * j, GMLP_WIDTH + LANES * (j + 1))
            zu, zv = proj_ref[:, cols], proj_ref[:, vcols]
            u, tu = _gelu_tanh(zu)
            vp, tv = _gelu_tanh(zv)
            sv = _sgu_forward_pair(wm, vp, j) + bias_ref[:, cols]
            dout = dcat_ref[:, cols]
            du = dout * sv
            dsv = dout * u
            dsv_lo, dsv_hi = jnp.where(lo, dsv, 0.0), jnp.where(hi, dsv, 0.0)
            lhs_t = jnp.concatenate([jnp.where(triu, wt_ref[2 * j], 0.0),
                                     jnp.where(triu, wt_ref[2 * j + 1], 0.0)], axis=1)
            dv = _dot(lhs_t, jnp.concatenate([dsv_lo, dsv_hi], axis=0))
            dw_ref[2 * j] += jnp.where(tril, _dot_nt(dsv_lo, vp), 0.0)
            dw_ref[2 * j + 1] += jnp.where(tril, _dot_nt(dsv_hi, vp), 0.0)
            db = db + (jnp.where(lane == 2 * j, jnp.sum(dsv_lo, axis=1, keepdims=True), 0.0)
                       + jnp.where(lane == 2 * j + 1, jnp.sum(dsv_hi, axis=1, keepdims=True), 0.0))
            dproj_ref[:, cols] = (du * _gelu_tanh_grad(zu, tu)).astype(dproj_ref.dtype)
            dproj_ref[:, vcols] = (dv * _gelu_tanh_grad(zv, tv)).astype(dproj_ref.dtype)
        db_ref[...] += db
        o = 2 * GMLP_WIDTH
        tab = tab_ref[...]
        q_r = _rope_apply(proj_ref[:, o:o + ATTN_WIDTH], tab, 1.0)
        k_cur = _rope_apply(proj_ref[:, o + ATTN_WIDTH:o + ATTN_WIDTH + KV_WIDTH], tab, 1.0)
        k_prev = _rope_apply(prev_ref[:, 0:KV_WIDTH], ptab_ref[...], 1.0)
        k_a = jnp.concatenate([k_prev, k_cur], axis=0)
        v_a = jnp.concatenate([prev_ref[:, KV_WIDTH:2 * KV_WIDTH],
                               proj_ref[:, o + ATTN_WIDTH + KV_WIDTH:o + ATTN_WIDTH + 2 * KV_WIDTH]], axis=0)
        k_b = pltpu.roll(k_a, HEAD_DIM, 1)
        v_b = pltpu.roll(v_a, HEAD_DIM, 1)
        bias_t = _attn_bias_t(blk == 0)
        lo2, _ = _lane_masks((2 * CHUNK, LANES))
        dout_b = dcat_ref[:, GMLP_WIDTH:GMLP_WIDTH + ATTN_WIDTH]
        dk_tot, dv_tot, dq_pairs = [], [], []
        for g in range(N_KV_HEADS):
            k_dup, v_dup = _group_dup(k_a, k_b, g, lo2), _group_dup(v_a, v_b, g, lo2)
            q_rows = _group_rows(q_r, g, lo, hi)
            do_rows = _group_rows(dout_b, g, lo, hi)
            p_t, p_sink = _attn_probs_t(k_dup, q_rows, bias_t, _sink_row(sink_ref, g))
            dp_t = _dot_nt(v_dup, do_rows)
            delta = jnp.sum(p_t * dp_t, axis=0, keepdims=True)
            ds_t = p_t * (dp_t - delta) * ATTN_SCALE
            dsink = -p_sink * delta
            for r in range(HEADS_PER_GROUP):
                h = HEADS_PER_GROUP * g + r
                dsink_ref[h:h + 1, :] += jnp.broadcast_to(
                    jnp.sum(dsink[:, LANES * r:LANES * (r + 1)], axis=1, keepdims=True), (1, LANES))
            dk_full = _dot(ds_t, q_rows)
            dv_full = _dot(p_t, do_rows)
            dk_tot.append(dk_full + pltpu.roll(dk_full, HEAD_DIM, 1))
            dv_tot.append(dv_full + pltpu.roll(dv_full, HEAD_DIM, 1))
            dq_t = _dot(k_dup.T, ds_t)
            dq_pairs += _pairs_from_rows(dq_t.T, lo)
        dk_all = jnp.where(lo2, dk_tot[0], dk_tot[1])
        dv_all = jnp.where(lo2, dv_tot[0], dv_tot[1])
        dk_cur = dk_all[CHUNK:, :] + carry[:, 0:KV_WIDTH]
        dv_cur = dv_all[CHUNK:, :] + carry[:, KV_WIDTH:2 * KV_WIDTH]
        carry[:, 0:KV_WIDTH] = dk_all[:CHUNK, :]
        carry[:, KV_WIDTH:2 * KV_WIDTH] = dv_all[:CHUNK, :]
        dq = _rope_apply(jnp.concatenate(dq_pairs, axis=1), tab, -1.0)
        dproj_ref[:, o:o + ATTN_WIDTH] = dq.astype(dproj_ref.dtype)
        dproj_ref[:, o + ATTN_WIDTH:o + ATTN_WIDTH + KV_WIDTH] = (
            _rope_apply(dk_cur, tab, -1.0).astype(dproj_ref.dtype))
        dproj_ref[:, o + ATTN_WIDTH + KV_WIDTH:o + ATTN_WIDTH + 2 * KV_WIDTH] = dv_cur.astype(dproj_ref.dtype)

    rev = lambda i: nb - 1 - i
    return _hosted_call(
        body, comm, name="mixer_bwd", grid=(nb,),
        out_shape=[jax.ShapeDtypeStruct((seq, IN_PROJ_WIDTH), MXU_DTYPE),
                   jax.ShapeDtypeStruct((GMLP_GROUPS, CHUNK, CHUNK), F32),
                   jax.ShapeDtypeStruct((CHUNK, LANES), F32),
                   jax.ShapeDtypeStruct((N_Q_HEADS, LANES), F32)],
        in_specs=[pl.BlockSpec((CHUNK, IN_PROJ_WIDTH), lambda i: (rev(i), 0)),
                  pl.BlockSpec((CHUNK, 2 * KV_WIDTH), lambda i: (jnp.maximum(rev(i) - 1, 0), kv_col)),
                  pl.BlockSpec((CHUNK, 3 * LANES), lambda i: (rev(i), 0)),
                  pl.BlockSpec((CHUNK, 3 * LANES), lambda i: (jnp.maximum(rev(i) - 1, 0), 0)),
                  pl.BlockSpec((CHUNK, D_MODEL), lambda i: (rev(i), 0)),
                  _full((GMLP_GROUPS, CHUNK, CHUNK)), _full((GMLP_GROUPS, CHUNK, CHUNK)),
                  _full((CHUNK, GMLP_WIDTH)), _full((N_Q_HEADS, LANES))],
        out_specs=[pl.BlockSpec((CHUNK, IN_PROJ_WIDTH), lambda i: (rev(i), 0)),
                   _full((GMLP_GROUPS, CHUNK, CHUNK)), _full((CHUNK, LANES)), _full((N_Q_HEADS, LANES))],
        scratch_shapes=[pltpu.VMEM((CHUNK, 2 * KV_WIDTH), F32)],
        semantics=("arbitrary",),
    )(proj, proj, rope_tab, rope_tab, dcat, w_spatial, w_spatial_t, bias_full, sink_rows)


def _in_proj_bwd_kernel(x, dx1, dproj, vecs, w_in_t, comm=None):
    seq = x.shape[0]
    tm = 512

    def body(x_ref, dx1_ref, dp_ref, v_ref, w_ref, gx_ref, sums_ref):
        @pl.when(pl.program_id(0) == 0)
        def _():
            sums_ref[...] = jnp.zeros_like(sums_ref)

        g_mix, scale1 = v_ref[0:1, :], v_ref[2:3, :]
        dh = _dot(dp_ref[...], w_ref[...])
        xv = x_ref[...]
        rstd = lax.rsqrt(_mean_last(xv * xv) + EPS)
        xh = xv * rstd
        dn1 = dh * (1.0 + scale1)
        dxh = dn1 * g_mix
        gx_ref[...] = dx1_ref[...] + rstd * (dxh - xh * _mean_last(dxh * xh))
        sums_ref[0:1, :] += _rowsum(dh)
        sums_ref[1:2, :] += _rowsum(dh * (xh * g_mix))
        sums_ref[2:3, :] += _rowsum(dn1 * xh)

    return _hosted_call(
        body, comm, name="in_proj_bwd", grid=(seq // tm,),
        out_shape=[jax.ShapeDtypeStruct((seq, D_MODEL), F32), jax.ShapeDtypeStruct((8, D_MODEL), F32)],
        in_specs=[pl.BlockSpec((tm, D_MODEL), lambda i: (i, 0)), pl.BlockSpec((tm, D_MODEL), lambda i: (i, 0)),
                  pl.BlockSpec((tm, IN_PROJ_WIDTH), lambda i: (i, 0)), _full((8, D_MODEL)),
                  _full((IN_PROJ_WIDTH, D_MODEL))],
        out_specs=[pl.BlockSpec((tm, D_MODEL), lambda i: (i, 0)), _full((8, D_MODEL))],
        semantics=("arbitrary",),
    )(x, dx1, dproj, vecs, w_in_t)


class _GradTiles(NamedTuple):
    tm: int
    tn: int
    n_tiles: int
    chips_per_tile: int
    a_index: Callable
    b_index: Callable


def _weight_grad_kernel(a, b, c_idx, name, tiles, comm=None):
    seq = a.shape[0]
    tk = min(seq, 1024)
    nk = seq // tk
    tm, tn, n_tiles, per = tiles.tm, tiles.tn, tiles.n_tiles, tiles.chips_per_tile
    rows = tm // per

    def half(phase, c):
        return phase * c[0] + (1 - phase) * (1 - c[0])

    def body(c_ref, a_ref, b_ref, o_ref, acc, stage, landed, send_sems, recv_sems):
        del c_ref
        phase, t, kk = pl.program_id(0), pl.program_id(1), pl.program_id(2)
        x, y, c, _ = _mesh_place()

        def copy(tile):
            return pltpu.make_async_remote_copy(
                src_ref=stage.at[tile], dst_ref=landed.at[tile], send_sem=send_sems.at[tile],
                recv_sem=recv_sems.at[tile], device_id=(x, y, 1 - c), device_id_type=MESH)

        @pl.when(kk == 0)
        def _():
            acc[...] = jnp.zeros_like(acc)

        acc[...] += _dot_tn(a_ref[...], b_ref[...])

        @pl.when((kk == nk - 1) & (phase == 0))
        def _():
            stage[t] = acc[...].astype(stage.dtype)
            copy(t).start()

        @pl.when((kk == nk - 1) & (phase == 1))
        def _():
            copy(t).wait_recv()
            total = acc[...] + landed[t].astype(F32)
            for q in range(per):
                o_ref[q] = total[rows * q:rows * (q + 1)].astype(o_ref.dtype)

        @pl.when((kk == nk - 1) & (phase == 1) & (t == n_tiles - 1))
        def _():
            for tile in range(n_tiles):
                copy(tile).wait_send()

    out = _hosted_call(
        body, comm, name=name, grid=(2, n_tiles, nk), n_prefetch=1,
        out_shape=[jax.ShapeDtypeStruct((n_tiles * per, rows, tn), GRAD_COMM_DTYPE)],
        in_specs=[pl.BlockSpec((tk, tm), lambda p, t, k, c: (k, tiles.a_index(t, half(p, c)))),
                  pl.BlockSpec((tk, tn), lambda p, t, k, c: (k, tiles.b_index(t, half(p, c))))],
        out_specs=[pl.BlockSpec((per, rows, tn), lambda p, t, k, c: (p * t, 0, 0))],
        scratch_shapes=[pltpu.VMEM((tm, tn), F32), pltpu.VMEM((n_tiles, tm, tn), GRAD_COMM_DTYPE),
                        pltpu.VMEM((n_tiles, tm, tn), GRAD_COMM_DTYPE),
                        pltpu.SemaphoreType.DMA((n_tiles,)), pltpu.SemaphoreType.DMA((n_tiles,))],
        semantics=("arbitrary", "arbitrary", "arbitrary"),
    )(c_idx, a, b)
    return out[0] if comm is None else out


def _row_tile(rows, most=256, sublanes=16):
    return max(t for t in range(sublanes, most + 1, sublanes) if rows % t == 0)


def _adam_update(w, g, m, v):
    m_new = ADAM_B1 * m + (1.0 - ADAM_B1) * g
    v_new = ADAM_B2 * v + (1.0 - ADAM_B2) * (g * g)
    m_hat = m_new / (1.0 - ADAM_B1 ** ADAM_STEP)
    v_hat = v_new / (1.0 - ADAM_B2 ** ADAM_STEP)
    delta = -ADAM_LR * (m_hat / (jnp.sqrt(v_hat) + ADAM_EPS) + ADAM_WD * w)
    return delta, m_new, v_new


def _sum_chips_kernel(own, others, place, name):
    _, r, n = own.shape
    tr = _row_tile(r)

    def body(place_ref, own_ref, oth_ref, o_ref):
        del place_ref
        acc = own_ref[...].astype(F32)
        for k in range(N_CHIPS - 1):
            acc = acc + oth_ref[k].astype(F32)
        o_ref[...] = acc

    return pl.pallas_call(
        body, name=name, out_shape=jax.ShapeDtypeStruct((2, r, n), F32),
        grid_spec=pltpu.PrefetchScalarGridSpec(
            num_scalar_prefetch=1, grid=(r // tr,),
            in_specs=[pl.BlockSpec((None, tr, n), lambda i, p: (p[0], i, 0)),
                      pl.BlockSpec((N_CHIPS - 1, tr, n), lambda i, p: (0, i, 0))],
            out_specs=pl.BlockSpec((None, tr, n), lambda i, p: (p[1], i, 0))),
        compiler_params=_params("parallel"),
    )(place, own, others)


def _adam_kernel(w, g, m, v, name):
    r, n = w.shape
    by_columns = g.shape[1] == r
    tr, tn = _row_tile(g.shape[1]), g.shape[2]

    def body(w_ref, g_ref, m_ref, v_ref, g_out, d_ref, mo_ref, vo_ref):
        gv = g_ref[...]
        g_out[...] = gv
        d_ref[...], mo_ref[...], vo_ref[...] = _adam_update(w_ref[...], gv, m_ref[...], v_ref[...])

    steps = g.shape[1] // tr
    spec = pl.BlockSpec((tr, tn), (lambda h, i: (i, h)) if by_columns else (lambda h, i: (h * steps + i, 0)))
    return pl.pallas_call(
        body, name=name, grid=(2, steps), out_shape=[jax.ShapeDtypeStruct((r, n), F32)] * 4,
        in_specs=[spec, pl.BlockSpec((None, tr, tn), lambda h, i: (h, i, 0)), spec, spec], out_specs=[spec] * 4,
        compiler_params=_params("parallel", "parallel"),
    )(w, g, m, v)


SMALL_PARAMS = ("b_ada", "g_mix", "g_ffn", "g_final", "b_spatial", "sinks", "w_spatial")


def _small_update_kernel(gathered, params):
    shapes = [params[nm][0].shape for nm in SMALL_PARAMS]

    def body(*refs):
        g_refs, refs = refs[:5], refs[5:]
        p_refs, refs = refs[:3 * len(SMALL_PARAMS)], refs[3 * len(SMALL_PARAMS):]
        loss_ref, o_refs = refs[0], refs[1:]

        def total(ref):
            acc = ref[0]
            for k in range(1, N_DEV):
                acc = acc + ref[k]
            return acc

        s1, s2, db, ds, dw = (total(r) for r in g_refs)
        loss_ref[...] = jnp.broadcast_to(s2[6:7, 0:1], loss_ref.shape)
        grads = {"b_ada": [s1[0:1], s1[1:2], s2[5:6], s2[0:1], s2[1:2], s2[2:3]], "g_mix": [s1[2:3]],
                 "g_ffn": [s2[3:4]], "g_final": [s2[4:5]], "b_spatial": [db.T[0:GMLP_GROUPS]],
                 "w_spatial": [dw]}
        lane = lax.broadcasted_iota(jnp.int32, (1, LANES), 1)
        sink_row = jnp.zeros((1, LANES), F32)
        for h in range(N_Q_HEADS):
            sink_row = sink_row + jnp.where(lane == h, ds[h:h + 1, :], 0.0)
        grads["sinks"] = [sink_row[:, 0:N_Q_HEADS]]
        for i, nm in enumerate(SMALL_PARAMS):
            w_ref, m_ref, v_ref = p_refs[3 * i:3 * i + 3]
            outs = o_refs[4 * i:4 * i + 4]
            width = grads[nm][0].shape[1]
            for k, g in enumerate(grads[nm]):
                cols = slice(width * k, width * (k + 1))
                upd = _adam_update(w_ref[:, cols], g, m_ref[:, cols], v_ref[:, cols])
                for o_ref, val in zip(outs, (g,) + upd):
                    o_ref[:, cols] = val

    flat = [a for nm in SMALL_PARAMS for a in params[nm]]
    out_shape = [jax.ShapeDtypeStruct((8, LANES), F32)]
    out_shape += [jax.ShapeDtypeStruct(s, F32) for s in shapes for _ in range(4)]
    outs = pl.pallas_call(
        body, name="small_update", grid=(1,), out_shape=out_shape,
        in_specs=[_full(g.shape) for g in gathered] + [_full(a.shape) for a in flat],
        out_specs=[_full(s.shape) for s in out_shape],
        compiler_params=_params("arbitrary"),
    )(*gathered, *flat)
    return {nm: outs[1 + 4 * i:5 + 4 * i] for i, nm in enumerate(SMALL_PARAMS)}, outs[0]


def _ada_update_kernel(act_t, dmod, w, m, v):
    r, n = w.shape
    tr = 256

    def body(a_ref, d_ref, w_ref, m_ref, v_ref, g_ref, dl_ref, mo_ref, vo_ref):
        g = _dot(a_ref[...], d_ref[...])
        g_ref[...] = g
        dl_ref[...], mo_ref[...], vo_ref[...] = _adam_update(w_ref[...], g, m_ref[...], v_ref[...])

    spec = pl.BlockSpec((tr, n), lambda i: (i, 0))
    return pl.pallas_call(
        body, name="ada_update", grid=(r // tr,), out_shape=[jax.ShapeDtypeStruct((r, n), F32)] * 4,
        in_specs=[pl.BlockSpec((tr, N_DEV), lambda i: (i, 0)), _full((N_DEV, n)), spec, spec, spec],
        out_specs=[spec] * 4, compiler_params=_params("parallel"),
    )(act_t, dmod, w, m, v)


def kernel(x, c, positions, w_ada, b_ada, g_mix, w_in, w_spatial, b_spatial, sinks, w_out, g_ffn, w_ff1, w_ff2, g_final, loss_target, m_w_ada, m_b_ada, m_g_mix, m_w_in, m_w_spatial, m_b_spatial, m_sinks, m_w_out, m_g_ffn, m_w_ff1, m_w_ff2, m_g_final, v_w_ada, v_b_ada, v_g_mix, v_w_in, v_w_spatial, v_b_spatial, v_sinks, v_w_out, v_g_ffn, v_w_ff1, v_w_ff2, v_g_final):
    xi, yi, ci = lax.axis_index("x"), lax.axis_index("y"), lax.axis_index("c")
    chip = 2 * xi + yi
    dev = 2 * chip + ci
    seq = x.shape[1]
    x2, tgt = x[0], loss_target[0]
    ada_cols = w_ada.shape[2]

    c_all = _all_gather8([c], "gather_c")[0].reshape(N_DEV, D_MODEL)
    b_shard = lax.dynamic_slice(b_ada, (0, chip * ada_cols), (1, ada_cols))
    mod_part, act = _mod_kernel(c_all, w_ada[0], b_shard)
    mod_all, = _all_gather8([mod_part], "gather_mod")
    mod_me = lax.dynamic_index_in_dim(mod_all[0::2], dev, axis=1, keepdims=False)
    mod_me = mod_me.reshape(N_MOD, D_MODEL)
    shift1, scale1, gate1, shift2, scale2, gate2 = (mod_me[k:k + 1] for k in range(N_MOD))

    big = {"w_in": tuple(a[0].T for a in (w_in, m_w_in, v_w_in)),
           "w_out": (w_out[0], m_w_out[0], v_w_out[0]), "w_ff1": (w_ff1[0], m_w_ff1[0], v_w_ff1[0]),
           "w_ff2": (w_ff2[0], m_w_ff2[0], v_w_ff2[0])}

    def halves(nm):
        r, n = big[nm][0].shape
        return big[nm][0].astype(WEIGHT_COMM_DTYPE).reshape(2, r // 2, n)

    w_in_t, = _all_gather8([halves("w_in")], "gather_w_in", split=True)
    w_in_t = w_in_t.reshape(IN_PROJ_WIDTH, D_MODEL)

    zeros_row = jnp.zeros((1, D_MODEL), F32)
    vecs1 = jnp.concatenate([g_mix, shift1, scale1] + [zeros_row] * 5, axis=0)
    vecs2 = jnp.concatenate([gate1, shift2, scale2, gate2, g_ffn, g_final.reshape(1, D_MODEL)]
                            + [zeros_row] * 2, axis=0)
    bias_full = jnp.repeat(b_spatial[0].T, HEAD_DIM, axis=1)
    sink_rows = jnp.broadcast_to(sinks[0][:, None], (N_Q_HEADS, LANES))
    inv_freq = ROPE_THETA ** (-jnp.arange(0, ROT_DIM, 2, dtype=F32) / ROT_DIM)
    rope_tab = _rope_lane_tables(*_rope_angle_kernel(positions, inv_freq.reshape(ROT_DIM // 2, 1)))

    proj, hb, g_ff1 = _in_proj_kernel(x2, vecs1, w_in_t, comm=_gather_job([halves("w_ff1")]))
    cat, g_ff2, g_out = _mixer_fwd_kernel(proj, rope_tab, w_spatial[0], bias_full, sink_rows,
                                          comm=_gather_job([halves("w_ff2"), halves("w_out")]))
    g_ff1, g_ff2, g_out = _gather_forward([g_ff1, g_ff2, g_out], "gather_forward")
    w_out_full = g_out.reshape(D_MODEL, D_MODEL)
    w_ff1_blocks = g_ff1.reshape(N_CHIPS, D_MODEL, D_MODEL)
    w_ff2_blocks = g_ff2.reshape(N_CHIPS, D_MODEL, D_MODEL)
    dx1, dcat, dmix, h2b, rb, dab, dffb, sums2 = _trunk_kernel(
        x2, tgt, cat, vecs2, w_out_full, w_ff1_blocks, w_ff2_blocks)

    c_idx = ci.reshape(1).astype(jnp.int32)
    place = jnp.stack([chip, ci]).astype(jnp.int32)
    half_d = D_MODEL // 2
    cs_ff2 = _weight_grad_kernel(rb, dffb, c_idx, "dw_ff2",
                                 _GradTiles(half_d, D_MODEL, N_CHIPS, 1, lambda t, h: 2 * t + h, lambda t, h: 0))
    cs_ff1 = _weight_grad_kernel(h2b, dab, c_idx, "dw_ff1",
                                 _GradTiles(half_d, D_MODEL, N_CHIPS, 1, lambda t, h: h, lambda t, h: t))
    out_rows = D_MODEL // N_CHIPS // 2
    cs_out = _weight_grad_kernel(cat, dmix, c_idx, "dw_out",
                                 _GradTiles(out_rows, D_MODEL, N_CHIPS, 1, lambda t, h: 2 * t + h, lambda t, h: 0))
    dproj, dw_spatial, db_lanes, dsink_rows, sc_ff2, sc_out = _mixer_bwd_kernel(
        proj, rope_tab, dcat, w_spatial[0], w_spatial[0].transpose(0, 2, 1), bias_full, sink_rows,
        comm=_scatter_job([cs_ff2, cs_out]))
    cs_in, sc_ff1 = _weight_grad_kernel(
        dproj, hb, c_idx, "dw_in",
        _GradTiles(2 * W_IN_BLOCK, half_d, N_CHIPS // 2, 2, lambda t, h: t, lambda t, h: h),
        comm=_scatter_job([cs_ff1]))
    grad_x, sums1 = _in_proj_bwd_kernel(x2, dx1, dproj, vecs1, w_in_t)
    sc_in, = _run_comm(_scatter_job([cs_in]), "grad_to_chips_w_in")

    names = ["w_in", "w_out", "w_ff1", "w_ff2"]
    totals = [_sum_chips_kernel(own, oth, place, "grad_sum_" + nm)
              for nm, own, oth in zip(names, [cs_in, cs_out, cs_ff1, cs_ff2], [sc_in, sc_out, sc_ff1, sc_ff2])]
    shared = _sibling_share(totals, "grad_share")
    big_out = {}
    for nm, g in zip(names, shared):
        w, m, v = big[nm]
        outs = _adam_kernel(w, g, m, v, "adam_" + nm)
        big_out[nm] = tuple((t.T if nm == "w_in" else t)[None] for t in outs)

    small = {"b_ada": (b_ada, m_b_ada, v_b_ada), "g_mix": (g_mix, m_g_mix, v_g_mix),
             "g_ffn": (g_ffn, m_g_ffn, v_g_ffn), "g_final": (g_final, m_g_final, v_g_final),
             "b_spatial": (b_spatial, m_b_spatial, v_b_spatial), "sinks": (sinks, m_sinks, v_sinks),
             "w_spatial": (w_spatial, m_w_spatial, v_w_spatial)}
    flat_shape = {"g_final": (1, D_MODEL), "b_spatial": (GMLP_GROUPS, CHUNK), "w_spatial": (GMLP_GROUPS * CHUNK, CHUNK)}
    gathered = _all_gather8([sums1, sums2, db_lanes, dsink_rows, dw_spatial.reshape(GMLP_GROUPS * CHUNK, CHUNK)],
                            "gather_small")
    small_out, loss_tile = _small_update_kernel(
        gathered, {nm: tuple(a.reshape(flat_shape.get(nm, a.shape)) for a in small[nm]) for nm in small})
    small_out = {nm: [o.reshape(small[nm][0].shape) for o in small_out[nm]] for nm in small}
    loss = loss_tile[0, 0]

    g1, g2 = gathered[0], gathered[1]
    dmod_all = jnp.concatenate([g1[:, 0], g1[:, 1], g2[:, 5], g2[:, 0], g2[:, 1], g2[:, 2]], axis=1)
    dmod_cols = lax.dynamic_slice(dmod_all, (0, chip * ada_cols), (N_DEV, ada_cols))
    ada = _ada_update_kernel(act.T, dmod_cols, w_ada[0], m_w_ada[0], v_w_ada[0])
    big_out["w_ada"] = tuple(t[None] for t in ada)

    order = ["w_ada", "b_ada", "g_mix", "w_in", "w_spatial", "b_spatial", "sinks", "w_out", "g_ffn",
             "w_ff1", "w_ff2", "g_final"]

    def leaf(nm, k):
        return big_out[nm][k] if nm in big_out else small_out[nm][k]

    outs = [loss, grad_x[None]]
    for k in range(4):
        outs += [leaf(nm, k) for nm in order]
    return tuple(outs)
```

```python
import math
from typing import Callable, NamedTuple

import jax
import jax.numpy as jnp
from jax import lax
from jax.experimental import pallas as pl
from jax.experimental.pallas import tpu as pltpu

F32 = jnp.float32
MXU_DTYPE = jnp.bfloat16
WEIGHT_COMM_DTYPE = jnp.bfloat16
GRAD_COMM_DTYPE = jnp.bfloat16

D_MODEL = 1024
D_FF = 4096
HEAD_DIM = 64
GMLP_GROUPS = 8
GMLP_WIDTH = 512
CHUNK = 128
N_Q_HEADS = 8
N_KV_HEADS = 2
ATTN_WIDTH = 512
KV_WIDTH = 128
ROT_DIM = 16
ROPE_THETA = 500000.0
IN_PROJ_WIDTH = 1792
N_MOD = 6
EPS = 1e-5
N_CHIPS = 4
N_DEV = 8
LANES = 128
W_IN_BLOCK = IN_PROJ_WIDTH // N_CHIPS

ADAM_LR = 0.001
ADAM_B1 = 0.9
ADAM_B2 = 0.999
ADAM_EPS = 1e-08
ADAM_WD = 0.01
ADAM_STEP = 10

VMEM_LIMIT_BYTES = 58 * 1024 * 1024
MESH = pl.DeviceIdType.MESH


def _params(*semantics):
    return pltpu.CompilerParams(dimension_semantics=semantics, vmem_limit_bytes=VMEM_LIMIT_BYTES)


def _dot(a, b):
    return jnp.dot(a.astype(MXU_DTYPE), b.astype(MXU_DTYPE), preferred_element_type=F32)


def _dot_nt(a, b):
    return lax.dot_general(a.astype(MXU_DTYPE), b.astype(MXU_DTYPE), (((1,), (1,)), ((), ())),
                           preferred_element_type=F32)


def _dot_tn(a, b):
    return lax.dot_general(a.astype(MXU_DTYPE), b.astype(MXU_DTYPE), (((0,), (0,)), ((), ())),
                           preferred_element_type=F32)


def _full(shape):
    return pl.BlockSpec(shape, lambda *_: (0,) * len(shape))


def _any():
    return pl.BlockSpec(memory_space=pl.ANY)


def _rowsum(v):
    return jnp.sum(v, axis=0, keepdims=True)


def _mean_last(v):
    return jnp.mean(v, axis=-1, keepdims=True)


class _Comm(NamedTuple):
    operands: tuple
    out_shapes: tuple
    n_sems: int
    make: Callable


def _hosted_call(body, comm, *, name, grid, in_specs, out_shape, out_specs, scratch_shapes=(), semantics,
                 n_prefetch=0):
    if comm is None:
        return pl.pallas_call(
            body, name=name, out_shape=out_shape, compiler_params=_params(*semantics),
            grid_spec=pltpu.PrefetchScalarGridSpec(
                num_scalar_prefetch=n_prefetch, grid=grid, in_specs=in_specs, out_specs=out_specs,
                scratch_shapes=list(scratch_shapes)))
    n_in, n_out, n_scr = len(in_specs), len(out_shape), len(scratch_shapes)
    k_in, k_out = len(comm.operands), len(comm.out_shapes)

    def hosted(*refs):
        prefetched, refs = refs[:n_prefetch], refs[n_prefetch:]
        ins, refs = refs[:n_in], refs[n_in:]
        c_ins, refs = refs[:k_in], refs[k_in:]
        outs, refs = refs[:n_out], refs[n_out:]
        c_outs, refs = refs[:k_out], refs[k_out:]
        scratch, (send_sems, recv_sems) = refs[:n_scr], refs[n_scr:]
        first, last = None, None
        for d, size in enumerate(grid):
            at_start, at_end = pl.program_id(d) == 0, pl.program_id(d) == size - 1
            first = at_start if first is None else first & at_start
            last = at_end if last is None else last & at_end

        @pl.when(first)
        def _():
            for cp in comm.make(c_ins, c_outs, send_sems, recv_sems)[0]:
                cp.start()

        body(*prefetched, *ins, *outs, *scratch)

        @pl.when(last)
        def _():
            for wait in comm.make(c_ins, c_outs, send_sems, recv_sems)[1]:
                wait()

    call = pl.pallas_call(
        hosted, name=name, out_shape=list(out_shape) + list(comm.out_shapes),
        compiler_params=_params(*semantics),
        grid_spec=pltpu.PrefetchScalarGridSpec(
            num_scalar_prefetch=n_prefetch, grid=grid, in_specs=list(in_specs) + [_any()] * k_in,
            out_specs=list(out_specs) + [_any()] * k_out,
            scratch_shapes=list(scratch_shapes) + [pltpu.SemaphoreType.DMA((comm.n_sems,)),
                                                    pltpu.SemaphoreType.DMA((comm.n_sems,))]))
    return lambda *args: call(*args, *comm.operands)


def _mesh_place():
    x, y, c = lax.axis_index("x"), lax.axis_index("y"), lax.axis_index("c")
    return x, y, c, [(1 - x, y), (x, 1 - y), (1 - x, 1 - y)]


def _gather_job(halves):
    per = 5

    def make(ins, outs, send_sems, recv_sems):
        x, y, c, chips = _mesh_place()
        starts, waits = [], []
        for a, (src, out) in enumerate(zip(ins, outs)):
            mine = out.at[4 * x + 2 * y + c]
            local = pltpu.make_async_copy(src.at[c], mine, send_sems.at[per * a + 4])
            to = [(x, y, 1 - c)] + [(px, py, c) for px, py in chips]
            sends = [pltpu.make_async_remote_copy(
                src_ref=src.at[c], dst_ref=mine, send_sem=send_sems.at[per * a + k],
                recv_sem=recv_sems.at[per * a + k], device_id=dev, device_id_type=MESH)
                for k, dev in enumerate(to)]
            recvs = [pltpu.make_async_remote_copy(
                src_ref=src.at[c], dst_ref=out.at[4 * px + 2 * py + pc], send_sem=send_sems.at[per * a + k],
                recv_sem=recv_sems.at[per * a + k], device_id=(px, py, pc), device_id_type=MESH)
                for k, (px, py, pc) in enumerate(to)]
            starts += [local] + sends
            waits += [local.wait] + [s.wait_send for s in sends] + [r.wait_recv for r in recvs]
        return starts, waits

    return _Comm(tuple(halves), tuple(jax.ShapeDtypeStruct((N_DEV,) + h.shape[1:], h.dtype) for h in halves),
                 per * len(halves), make)


def _gather_forward(bufs, name):
    n_arr = len(bufs)

    def body(*refs):
        outs = refs[n_arr:2 * n_arr]
        send_sems, recv_sems = refs[2 * n_arr:]
        x, y, c, chips = _mesh_place()
        sends, recvs = [], []
        for a, buf in enumerate(outs):
            for j, (px, py) in enumerate(chips):
                mine, theirs = buf.at[4 * px + 2 * py + c], buf.at[4 * px + 2 * py + 1 - c]
                sems = dict(send_sem=send_sems.at[3 * a + j], recv_sem=recv_sems.at[3 * a + j],
                            device_id=(x, y, 1 - c), device_id_type=MESH)
                sends.append(pltpu.make_async_remote_copy(src_ref=mine, dst_ref=mine, **sems))
                recvs.append(pltpu.make_async_remote_copy(src_ref=mine, dst_ref=theirs, **sems))
        for cp in sends:
            cp.start()
        for s, r in zip(sends, recvs):
            s.wait_send()
            r.wait_recv()

    return pl.pallas_call(
        body, name=name, out_shape=[jax.ShapeDtypeStruct(b.shape, b.dtype) for b in bufs],
        in_specs=[_any()] * n_arr, out_specs=[_any()] * n_arr,
        input_output_aliases={a: a for a in range(n_arr)},
        scratch_shapes=[pltpu.SemaphoreType.DMA((3 * n_arr,)), pltpu.SemaphoreType.DMA((3 * n_arr,))],
    )(*bufs)


def _scatter_job(chip_sums):
    def make(ins, outs, send_sems, recv_sems):
        x, y, c, chips = _mesh_place()
        copies = [pltpu.make_async_remote_copy(
            src_ref=src.at[2 * px + py], dst_ref=out.at[j], send_sem=send_sems.at[3 * a + j],
            recv_sem=recv_sems.at[3 * a + j], device_id=(px, py, c), device_id_type=MESH)
            for a, (src, out) in enumerate(zip(ins, outs)) for j, (px, py) in enumerate(chips)]
        return copies, [cp.wait for cp in copies]

    return _Comm(tuple(chip_sums), tuple(jax.ShapeDtypeStruct((3,) + s.shape[1:], s.dtype) for s in chip_sums),
                 3 * len(chip_sums), make)


def _run_comm(comm, name):
    def body(token_ref):
        token_ref[...] = jnp.zeros_like(token_ref)

    out = _hosted_call(body, comm, name=name, grid=(1,), in_specs=[],
                       out_shape=[jax.ShapeDtypeStruct((8, LANES), F32)], out_specs=[_full((8, LANES))],
                       semantics=("arbitrary",))()
    return out[1:]


def _all_gather8(blocks, name, split=False):
    n_arr = len(blocks)

    def body(*refs):
        x_refs, out_refs = refs[:n_arr], refs[n_arr:2 * n_arr]
        send_sems, recv_sems, local_sems = refs[2 * n_arr:]
        x, y, c, chips = _mesh_place()
        me, sibling = (x, y, c), (x, y, 1 - c)
        arrays = []
        for a, (x_ref, out_ref) in enumerate(zip(x_refs, out_refs)):
            src_mine = x_ref.at[c] if split else x_ref

            def copy(k, blk, to, src=None, a=a, out_ref=out_ref):
                dst = out_ref.at[4 * blk[0] + 2 * blk[1] + blk[2]]
                return pltpu.make_async_remote_copy(
                    src_ref=dst if src is None else src, dst_ref=dst,
                    send_sem=send_sems.at[7 * a + k], recv_sem=recv_sems.at[7 * a + k],
                    device_id=to, device_id_type=MESH)

            mine = pltpu.make_async_copy(src_mine, out_ref.at[4 * x + 2 * y + c], local_sems.at[a])
            mine.start()
            first = [copy(0, me, sibling, src=src_mine)]
            first += [copy(1 + j, me, (*chip, c), src=src_mine) for j, chip in enumerate(chips)]
            for cp in first:
                cp.start()
            arrays.append((copy, mine, first))
        sent = []
        for copy, mine, first in arrays:
            passed = [copy(4 + j, (*chip, c), sibling) for j, chip in enumerate(chips)]
            for j, chip in enumerate(chips):
                copy(1 + j, (*chip, c), me).wait_recv()
                passed[j].start()
            sent += first + passed
        for copy, mine, first in arrays:
            copy(0, sibling, me).wait_recv()
            for j, chip in enumerate(chips):
                copy(4 + j, (*chip, 1 - c), me).wait_recv()
            mine.wait()
        for cp in sent:
            cp.wait_send()

    return pl.pallas_call(
        body, name=name,
        out_shape=[jax.ShapeDtypeStruct((N_DEV,) + tuple(b.shape[1:] if split else b.shape), b.dtype)
                   for b in blocks],
        in_specs=[_any()] * n_arr, out_specs=[_any()] * n_arr,
        scratch_shapes=[pltpu.SemaphoreType.DMA((7 * n_arr,)), pltpu.SemaphoreType.DMA((7 * n_arr,)),
                        pltpu.SemaphoreType.DMA((n_arr,))],
    )(*blocks)


def _sibling_share(bufs, name):
    n_arr = len(bufs)

    def body(*refs):
        out_refs = refs[n_arr:2 * n_arr]
        send_sems, recv_sems = refs[2 * n_arr:]
        x, y, c = lax.axis_index("x"), lax.axis_index("y"), lax.axis_index("c")
        copies = [pltpu.make_async_remote_copy(
            src_ref=out_refs[a].at[c], dst_ref=out_refs[a].at[c],
            send_sem=send_sems.at[a], recv_sem=recv_sems.at[a],
            device_id=(x, y, 1 - c), device_id_type=MESH) for a in range(n_arr)]
        for cp in copies:
            cp.start()
        for a in range(n_arr):
            pltpu.make_async_remote_copy(
                src_ref=out_refs[a].at[c], dst_ref=out_refs[a].at[1 - c],
                send_sem=send_sems.at[a], recv_sem=recv_sems.at[a],
                device_id=(x, y, 1 - c), device_id_type=MESH).wait()

    return pl.pallas_call(
        body, name=name,
        out_shape=[jax.ShapeDtypeStruct(b.shape, b.dtype) for b in bufs],
        in_specs=[_any()] * n_arr, out_specs=[_any()] * n_arr,
        input_output_aliases={a: a for a in range(n_arr)},
        scratch_shapes=[pltpu.SemaphoreType.DMA((n_arr,)), pltpu.SemaphoreType.DMA((n_arr,))],
    )(*bufs)


def _gelu_tanh(z):
    k = math.sqrt(2.0 / math.pi)
    t = jnp.tanh(k * (z + 0.044715 * (z * z * z)))
    return 0.5 * z * (1.0 + t), t


def _gelu_tanh_grad(z, t):
    k = math.sqrt(2.0 / math.pi)
    return 0.5 * (1.0 + t) + 0.5 * z * (1.0 - t * t) * (k * (1.0 + 3.0 * 0.044715 * (z * z)))


def _rope_angle_kernel(pos_row, invf_col):
    seq = pos_row.shape[1]

    def body(p_ref, f_ref, cos_ref, sin_ref):
        ang = p_ref[...].astype(F32) * f_ref[...]
        cos_ref[...] = jnp.cos(ang)
        sin_ref[...] = jnp.sin(ang)

    return pl.pallas_call(
        body, name="rope_angles", grid=(1,), out_shape=[jax.ShapeDtypeStruct((ROT_DIM // 2, seq), F32)] * 2,
        in_specs=[_full((1, seq)), _full((ROT_DIM // 2, 1))], out_specs=[_full((ROT_DIM // 2, seq))] * 2,
        compiler_params=_params("arbitrary"),
    )(pos_row, invf_col)


def _rope_lane_tables(cos, sin):
    cos_t, sin_t = cos.T, sin.T
    seq, half = cos_t.shape
    ones = jnp.ones((seq, HEAD_DIM - ROT_DIM), F32)
    c64 = jnp.concatenate([cos_t, cos_t, ones], axis=1)
    s1 = jnp.concatenate([sin_t, jnp.zeros((seq, HEAD_DIM - half), F32)], axis=1)
    s2 = jnp.concatenate([jnp.zeros((seq, half), F32), sin_t, jnp.zeros((seq, HEAD_DIM - ROT_DIM), F32)], axis=1)
    return jnp.concatenate([jnp.tile(t, (1, LANES // HEAD_DIM)) for t in (c64, s1, s2)], axis=1)


def _rope_apply(t, tab, sign):
    reps = t.shape[1] // LANES
    c_tab, s1, s2 = (jnp.tile(tab[:, LANES * k:LANES * (k + 1)], (1, reps)) if reps > 1
                     else tab[:, LANES * k:LANES * (k + 1)] for k in range(3))
    half = ROT_DIM // 2
    up = pltpu.roll(t, t.shape[1] - half, 1)
    down = pltpu.roll(t, half, 1)
    return t * c_tab + sign * (down * s2 - up * s1)


def _lane_masks(shape):
    lane = lax.broadcasted_iota(jnp.int32, shape, 1)
    return lane < HEAD_DIM, lane >= HEAD_DIM


HEADS_PER_GROUP = N_Q_HEADS // N_KV_HEADS
ATTN_SCALE = 1.0 / math.sqrt(HEAD_DIM)


def _attn_bias_t(first_block):
    kj = lax.broadcasted_iota(jnp.int32, (2 * CHUNK, CHUNK), 0)
    qi = lax.broadcasted_iota(jnp.int32, (2 * CHUNK, CHUNK), 1)
    ok = (kj > qi) & (kj <= qi + CHUNK) & (jnp.logical_not(first_block) | (kj >= CHUNK))
    return jnp.tile(jnp.where(ok, 0.0, -jnp.inf), (1, HEADS_PER_GROUP))


def _group_rows(x, g, lo, hi):
    rows = []
    for r in range(HEADS_PER_GROUP):
        h = HEADS_PER_GROUP * g + r
        pair = x[:, LANES * (h // 2):LANES * (h // 2 + 1)]
        rows.append(jnp.where(hi if h % 2 else lo, pair, 0.0))
    return jnp.concatenate(rows, axis=0)


def _pairs_from_rows(rows, lo):
    return [jnp.where(lo, rows[2 * CHUNK * k:2 * CHUNK * k + CHUNK], rows[2 * CHUNK * k + CHUNK:2 * CHUNK * (k + 1)])
            for k in range(HEADS_PER_GROUP // 2)]


def _group_dup(a, b, g, lo2):
    return jnp.where(lo2, a, b) if g == 0 else jnp.where(lo2, b, a)


def _sink_row(sink_ref, g):
    return jnp.concatenate([sink_ref[HEADS_PER_GROUP * g + r:HEADS_PER_GROUP * g + r + 1, :]
                            for r in range(HEADS_PER_GROUP)], axis=1)


def _attn_probs_t(k_dup, q_rows, bias_t, sink_row):
    s_t = _dot_nt(k_dup, q_rows) * ATTN_SCALE + bias_t
    m = jnp.maximum(jnp.max(s_t, axis=0, keepdims=True), sink_row)
    p = jnp.exp(s_t - m)
    e_sink = jnp.exp(sink_row - m)
    inv = 1.0 / (jnp.sum(p, axis=0, keepdims=True) + e_sink)
    return p * inv, e_sink * inv


def _sgu_forward_pair(wm, vp, j):
    lo, hi = _lane_masks(vp.shape)
    lhs = jnp.concatenate([wm[2 * j], wm[2 * j + 1]], axis=1)
    rhs = jnp.concatenate([jnp.where(lo, vp, 0.0), jnp.where(hi, vp, 0.0)], axis=0)
    return _dot(lhs, rhs)


def _masked_spatial(w_ref):
    t = lax.broadcasted_iota(jnp.int32, (CHUNK, CHUNK), 0)
    s = lax.broadcasted_iota(jnp.int32, (CHUNK, CHUNK), 1)
    tril = s <= t
    return [jnp.where(tril, w_ref[g], 0.0) for g in range(GMLP_GROUPS)], tril, s >= t


def _mod_kernel(c_all, w_shard, b_shard):
    n = w_shard.shape[1]
    tn = 512

    def body(c_ref, w_ref, b_ref, mod_ref, act_ref):
        cv = c_ref[...]
        act = cv * (1.0 / (1.0 + jnp.exp(-cv)))
        act_ref[...] = act
        mod_ref[...] = _dot(act, w_ref[...]) + b_ref[...]

    return pl.pallas_call(
        body, name="ada_mod", grid=(n // tn,),
        out_shape=[jax.ShapeDtypeStruct((N_DEV, n), F32), jax.ShapeDtypeStruct((N_DEV, D_MODEL), F32)],
        in_specs=[_full((N_DEV, D_MODEL)), pl.BlockSpec((D_MODEL, tn), lambda i: (0, i)),
                  pl.BlockSpec((1, tn), lambda i: (0, i))],
        out_specs=[pl.BlockSpec((N_DEV, tn), lambda i: (0, i)), _full((N_DEV, D_MODEL))],
        compiler_params=_params("arbitrary"),
    )(c_all, w_shard, b_shard)


def _in_proj_kernel(x, vecs, w_in_t, comm=None):
    seq = x.shape[0]
    tm = 512

    def body(x_ref, v_ref, w_ref, proj_ref, h_ref):
        xv = x_ref[...]
        rstd = lax.rsqrt(_mean_last(xv * xv) + EPS)
        n1 = (xv * rstd) * v_ref[0:1, :]
        h = n1 * (1.0 + v_ref[2:3, :]) + v_ref[1:2, :]
        hb = h.astype(MXU_DTYPE)
        h_ref[...] = hb
        proj_ref[...] = _dot_nt(hb, w_ref[...])

    return _hosted_call(
        body, comm, name="in_proj", grid=(seq // tm,),
        out_shape=[jax.ShapeDtypeStruct((seq, IN_PROJ_WIDTH), F32),
                   jax.ShapeDtypeStruct((seq, D_MODEL), MXU_DTYPE)],
        in_specs=[pl.BlockSpec((tm, D_MODEL), lambda i: (i, 0)), _full((8, D_MODEL)),
                  _full((IN_PROJ_WIDTH, D_MODEL))],
        out_specs=[pl.BlockSpec((tm, IN_PROJ_WIDTH), lambda i: (i, 0)),
                   pl.BlockSpec((tm, D_MODEL), lambda i: (i, 0))],
        semantics=("arbitrary",),
    )(x, vecs, w_in_t)


def _mixer_fwd_kernel(proj, rope_tab, w_spatial, bias_full, sink_rows, comm=None):
    seq = proj.shape[0]
    nb = seq // CHUNK
    kv_col = (2 * GMLP_WIDTH + ATTN_WIDTH) // (2 * KV_WIDTH)

    def body(proj_ref, prev_ref, tab_ref, ptab_ref, w_ref, bias_ref, sink_ref, cat_ref):
        i = pl.program_id(0)
        wm, _, _ = _masked_spatial(w_ref)
        for j in range(GMLP_GROUPS // 2):
            cols = slice(LANES * j, LANES * (j + 1))
            vcols = slice(GMLP_WIDTH + LANES * j, GMLP_WIDTH + LANES * (j + 1))
            u, _ = _gelu_tanh(proj_ref[:, cols])
            vp, _ = _gelu_tanh(proj_ref[:, vcols])
            sv = _sgu_forward_pair(wm, vp, j) + bias_ref[:, cols]
            cat_ref[:, cols] = (u * sv).astype(cat_ref.dtype)
        o = 2 * GMLP_WIDTH
        tab = tab_ref[...]
        q_r = _rope_apply(proj_ref[:, o:o + ATTN_WIDTH], tab, 1.0)
        k_cur = _rope_apply(proj_ref[:, o + ATTN_WIDTH:o + ATTN_WIDTH + KV_WIDTH], tab, 1.0)
        k_prev = _rope_apply(prev_ref[:, 0:KV_WIDTH], ptab_ref[...], 1.0)
        k_a = jnp.concatenate([k_prev, k_cur], axis=0)
        v_a = jnp.concatenate([prev_ref[:, KV_WIDTH:2 * KV_WIDTH],
                               proj_ref[:, o + ATTN_WIDTH + KV_WIDTH:o + ATTN_WIDTH + 2 * KV_WIDTH]], axis=0)
        k_b = pltpu.roll(k_a, HEAD_DIM, 1)
        v_b = pltpu.roll(v_a, HEAD_DIM, 1)
        bias_t = _attn_bias_t(i == 0)
        lo, hi = _lane_masks((CHUNK, LANES))
        lo2, _ = _lane_masks((2 * CHUNK, LANES))
        for g in range(N_KV_HEADS):
            p_t, _ = _attn_probs_t(_group_dup(k_a, k_b, g, lo2), _group_rows(q_r, g, lo, hi), bias_t,
                                   _sink_row(sink_ref, g))
            o_t = _dot(_group_dup(v_a, v_b, g, lo2).T, p_t)
            for k, pair in enumerate(_pairs_from_rows(o_t.T, lo)):
                c0 = GMLP_WIDTH + LANES * (2 * g + k)
                cat_ref[:, c0:c0 + LANES] = pair.astype(cat_ref.dtype)

    return _hosted_call(
        body, comm, name="mixer_fwd", grid=(nb,),
        out_shape=[jax.ShapeDtypeStruct((seq, D_MODEL), MXU_DTYPE)],
        in_specs=[pl.BlockSpec((CHUNK, IN_PROJ_WIDTH), lambda i: (i, 0)),
                  pl.BlockSpec((CHUNK, 2 * KV_WIDTH), lambda i: (jnp.maximum(i - 1, 0), kv_col)),
                  pl.BlockSpec((CHUNK, 3 * LANES), lambda i: (i, 0)),
                  pl.BlockSpec((CHUNK, 3 * LANES), lambda i: (jnp.maximum(i - 1, 0), 0)),
                  _full((GMLP_GROUPS, CHUNK, CHUNK)), _full((CHUNK, GMLP_WIDTH)),
                  _full((N_Q_HEADS, LANES))],
        out_specs=[pl.BlockSpec((CHUNK, D_MODEL), lambda i: (i, 0))],
        semantics=("arbitrary",),
    )(proj, proj, rope_tab, rope_tab, w_spatial, bias_full, sink_rows)


def _trunk_kernel(x, target, cat, vecs, w_out, w_ff1, w_ff2):
    seq = x.shape[0]
    tm = 256
    nj = D_FF // D_MODEL

    def body(x_ref, t_ref, cat_ref, v_ref, wout_hbm, w1_hbm, w2_hbm,
             dx1_ref, dcat_ref, dmix_ref, h2_ref, r_ref, da_ref, dff_ref, sums_ref,
             wout, w1, w2, a_scr, sem):
        i = pl.program_id(0)

        @pl.when(i == 0)
        def _():
            copies = [pltpu.make_async_copy(wout_hbm, wout, sem.at[0]),
                      pltpu.make_async_copy(w1_hbm, w1, sem.at[1]),
                      pltpu.make_async_copy(w2_hbm, w2, sem.at[2])]
            for cp in copies:
                cp.start()
            for cp in copies:
                cp.wait()
            sums_ref[...] = jnp.zeros_like(sums_ref)

        gate1, shift2, scale2 = v_ref[0:1, :], v_ref[1:2, :], v_ref[2:3, :]
        gate2, g_ffn, g_final = v_ref[3:4, :], v_ref[4:5, :], v_ref[5:6, :]

        mix = _dot(cat_ref[...], wout[...])
        x1 = x_ref[...] + gate1 * mix
        rstd2 = lax.rsqrt(_mean_last(x1 * x1) + EPS)
        xh2 = x1 * rstd2
        n2 = xh2 * g_ffn
        h2b = (n2 * (1.0 + scale2) + shift2).astype(MXU_DTYPE)
        h2_ref[...] = h2b
        ff = jnp.zeros((tm, D_MODEL), F32)
        for j in range(nj):
            a = _dot(h2b, w1[j])
            a_scr[j] = a
            relu = jnp.maximum(a, 0.0)
            rb = (relu * relu).astype(MXU_DTYPE)
            r_ref[:, D_MODEL * j:D_MODEL * (j + 1)] = rb
            ff = ff + _dot(rb, w2[j])
        x2 = x1 + gate2 * ff
        rstd3 = lax.rsqrt(_mean_last(x2 * x2) + EPS)
        xh3 = x2 * rstd3
        err = xh3 * g_final - t_ref[...]
        loss = 0.5 * _rowsum(_mean_last(err * err))
        dy = err * (1.0 / D_MODEL)
        dxh3 = dy * g_final
        dx2 = rstd3 * (dxh3 - xh3 * _mean_last(dxh3 * xh3))
        dffb = (dx2 * gate2).astype(MXU_DTYPE)
        dff_ref[...] = dffb
        dh2 = jnp.zeros((tm, D_MODEL), F32)
        for j in range(nj):
            dr = _dot_nt(dffb, w2[j])
            dab = (dr * (2.0 * jnp.maximum(a_scr[j], 0.0))).astype(MXU_DTYPE)
            da_ref[:, D_MODEL * j:D_MODEL * (j + 1)] = dab
            dh2 = dh2 + _dot_nt(dab, w1[j])
        dn2 = dh2 * (1.0 + scale2)
        dxh2 = dn2 * g_ffn
        dx1 = dx2 + rstd2 * (dxh2 - xh2 * _mean_last(dxh2 * xh2))
        dx1_ref[...] = dx1
        dmixb = (dx1 * gate1).astype(MXU_DTYPE)
        dmix_ref[...] = dmixb
        dcat_ref[...] = _dot_nt(dmixb, wout[...])

        sums_ref[0:1, :] += _rowsum(dh2)
        sums_ref[1:2, :] += _rowsum(dh2 * n2)
        sums_ref[2:3, :] += _rowsum(dx2 * ff)
        sums_ref[3:4, :] += _rowsum(dn2 * xh2)
        sums_ref[4:5, :] += _rowsum(dy * xh3)
        sums_ref[5:6, :] += _rowsum(dx1 * mix)
        sums_ref[6:7, :] += jnp.broadcast_to(loss, (1, D_MODEL))

    tok = lambda w: pl.BlockSpec((tm, w), lambda i: (i, 0))
    return pl.pallas_call(
        body, name="trunk", grid=(seq // tm,),
        out_shape=[jax.ShapeDtypeStruct((seq, D_MODEL), F32), jax.ShapeDtypeStruct((seq, D_MODEL), F32),
                   jax.ShapeDtypeStruct((seq, D_MODEL), MXU_DTYPE), jax.ShapeDtypeStruct((seq, D_MODEL), MXU_DTYPE),
                   jax.ShapeDtypeStruct((seq, D_FF), MXU_DTYPE), jax.ShapeDtypeStruct((seq, D_FF), MXU_DTYPE),
                   jax.ShapeDtypeStruct((seq, D_MODEL), MXU_DTYPE), jax.ShapeDtypeStruct((8, D_MODEL), F32)],
        in_specs=[tok(D_MODEL), tok(D_MODEL), tok(D_MODEL), _full((8, D_MODEL)), _any(), _any(), _any()],
        out_specs=[tok(D_MODEL), tok(D_MODEL), tok(D_MODEL), tok(D_MODEL), tok(D_FF), tok(D_FF), tok(D_MODEL),
                   _full((8, D_MODEL))],
        scratch_shapes=[pltpu.VMEM((D_MODEL, D_MODEL), MXU_DTYPE), pltpu.VMEM((nj, D_MODEL, D_MODEL), MXU_DTYPE),
                        pltpu.VMEM((nj, D_MODEL, D_MODEL), MXU_DTYPE), pltpu.VMEM((nj, tm, D_MODEL), F32),
                        pltpu.SemaphoreType.DMA((3,))],
        compiler_params=_params("arbitrary"),
    )(x, target, cat, vecs, w_out, w_ff1, w_ff2)


def _mixer_bwd_kernel(proj, rope_tab, dcat, w_spatial, w_spatial_t, bias_full, sink_rows, comm=None):
    seq = proj.shape[0]
    nb = seq // CHUNK
    kv_col = (2 * GMLP_WIDTH + ATTN_WIDTH) // (2 * KV_WIDTH)

    def body(proj_ref, prev_ref, tab_ref, ptab_ref, dcat_ref, w_ref, wt_ref, bias_ref, sink_ref,
             dproj_ref, dw_ref, db_ref, dsink_ref, carry):
        step = pl.program_id(0)
        blk = nb - 1 - step

        @pl.when(step == 0)
        def _():
            carry[...] = jnp.zeros_like(carry)
            dw_ref[...] = jnp.zeros_like(dw_ref)
            db_ref[...] = jnp.zeros_like(db_ref)
            dsink_ref[...] = jnp.zeros_like(dsink_ref)

        wm, tril, triu = _masked_spatial(w_ref)
        lo, hi = _lane_masks((CHUNK, LANES))
        lane = lax.broadcasted_iota(jnp.int32, (CHUNK, LANES), 1)
        db = jnp.zeros((CHUNK, LANES), F32)
        for j in range(GMLP_GROUPS // 2):
            cols = slice(LANES * j, LANES * (j + 1))
            vcols = slice(GMLP_WIDTH + LANES * j, GMLP_WIDTH + LANES * (j + 1))
            zu, zv = proj_ref[:, cols], proj_ref[:, vcols]
            u, tu = _gelu_tanh(zu)
            vp, tv = _gelu_tanh(zv)
            sv = _sgu_forward_pair(wm, vp, j) + bias_ref[:, cols]
            dout = dcat_ref[:, cols]
            du = dout * sv
            dsv = dout * u
            dsv_lo, dsv_hi = jnp.where(lo, dsv, 0.0), jnp.where(hi, dsv, 0.0)
            lhs_t = jnp.concatenate([jnp.where(triu, wt_ref[2 * j], 0.0),
                                     jnp.where(triu, wt_ref[2 * j + 1], 0.0)], axis=1)
            dv = _dot(lhs_t, jnp.concatenate([dsv_lo, dsv_hi], axis=0))
            dw_ref[2 * j] += jnp.where(tril, _dot_nt(dsv_lo, vp), 0.0)
            dw_ref[2 * j + 1] += jnp.where(tril, _dot_nt(dsv_hi, vp), 0.0)
            db = db + (jnp.where(lane == 2 * j, jnp.sum(dsv_lo, axis=1, keepdims=True), 0.0)
                       + jnp.where(lane == 2 * j + 1, jnp.sum(dsv_hi, axis=1, keepdims=True), 0.0))
            dproj_ref[:, cols] = (du * _gelu_tanh_grad(zu, tu)).astype(dproj_ref.dtype)
            dproj_ref[:, vcols] = (dv * _gelu_tanh_grad(zv, tv)).astype(dproj_ref.dtype)
        db_ref[...] += db
        o = 2 * GMLP_WIDTH
        tab = tab_ref[...]
        q_r = _rope_apply(proj_ref[:, o:o + ATTN_WIDTH], tab, 1.0)
        k_cur = _rope_apply(proj_ref[:, o + ATTN_WIDTH:o + ATTN_WIDTH + KV_WIDTH], tab, 1.0)
        k_prev = _rope_apply(prev_ref[:, 0:KV_WIDTH], ptab_ref[...], 1.0)
        k_a = jnp.concatenate([k_prev, k_cur], axis=0)
        v_a = jnp.concatenate([prev_ref[:, KV_WIDTH:2 * KV_WIDTH],
                               proj_ref[:, o + ATTN_WIDTH + KV_WIDTH:o + ATTN_WIDTH + 2 * KV_WIDTH]], axis=0)
        k_b = pltpu.roll(k_a, HEAD_DIM, 1)
        v_b = pltpu.roll(v_a, HEAD_DIM, 1)
        bias_t = _attn_bias_t(blk == 0)
        lo2, _ = _lane_masks((2 * CHUNK, LANES))
        dout_b = dcat_ref[:, GMLP_WIDTH:GMLP_WIDTH + ATTN_WIDTH]
        dk_tot, dv_tot, dq_pairs = [], [], []
        for g in range(N_KV_HEADS):
            k_dup, v_dup = _group_dup(k_a, k_b, g, lo2), _group_dup(v_a, v_b, g, lo2)
            q_rows = _group_rows(q_r, g, lo, hi)
            do_rows = _group_rows(dout_b, g, lo, hi)
            p_t, p_sink = _attn_probs_t(k_dup, q_rows, bias_t, _sink_row(sink_ref, g))
            dp_t = _dot_nt(v_dup, do_rows)
            delta = jnp.sum(p_t * dp_t, axis=0, keepdims=True)
            ds_t = p_t * (dp_t - delta) * ATTN_SCALE
            dsink = -p_sink * delta
            for r in range(HEADS_PER_GROUP):
                h = HEADS_PER_GROUP * g + r
                dsink_ref[h:h + 1, :] += jnp.broadcast_to(
                    jnp.sum(dsink[:, LANES * r:LANES * (r + 1)], axis=1, keepdims=True), (1, LANES))
            dk_full = _dot(ds_t, q_rows)
            dv_full = _dot(p_t, do_rows)
            dk_tot.append(dk_full + pltpu.roll(dk_full, HEAD_DIM, 1))
            dv_tot.append(dv_full + pltpu.roll(dv_full, HEAD_DIM, 1))
            dq_t = _dot(k_dup.T, ds_t)
            dq_pairs += _pairs_from_rows(dq_t.T, lo)
        dk_all = jnp.where(lo2, dk_tot[0], dk_tot[1])
        dv_all = jnp.where(lo2, dv_tot[0], dv_tot[1])
        dk_cur = dk_all[CHUNK:, :] + carry[:, 0:KV_WIDTH]
        dv_cur = dv_all[CHUNK:, :] + carry[:, KV_WIDTH:2 * KV_WIDTH]
        carry[:, 0:KV_WIDTH] = dk_all[:CHUNK, :]
        carry[:, KV_WIDTH:2 * KV_WIDTH] = dv_all[:CHUNK, :]
        dq = _rope_apply(jnp.concatenate(dq_pairs, axis=1), tab, -1.0)
        dproj_ref[:, o:o + ATTN_WIDTH] = dq.astype(dproj_ref.dtype)
        dproj_ref[:, o + ATTN_WIDTH:o + ATTN_WIDTH + KV_WIDTH] = (
            _rope_apply(dk_cur, tab, -1.0).astype(dproj_ref.dtype))
        dproj_ref[:, o + ATTN_WIDTH + KV_WIDTH:o + ATTN_WIDTH + 2 * KV_WIDTH] = dv_cur.astype(dproj_ref.dtype)

    rev = lambda i: nb - 1 - i
    return _hosted_call(
        body, comm, name="mixer_bwd", grid=(nb,),
        out_shape=[jax.ShapeDtypeStruct((seq, IN_PROJ_WIDTH), MXU_DTYPE),
                   jax.ShapeDtypeStruct((GMLP_GROUPS, CHUNK, CHUNK), F32),
                   jax.ShapeDtypeStruct((CHUNK, LANES), F32),
                   jax.ShapeDtypeStruct((N_Q_HEADS, LANES), F32)],
        in_specs=[pl.BlockSpec((CHUNK, IN_PROJ_WIDTH), lambda i: (rev(i), 0)),
                  pl.BlockSpec((CHUNK, 2 * KV_WIDTH), lambda i: (jnp.maximum(rev(i) - 1, 0), kv_col)),
                  pl.BlockSpec((CHUNK, 3 * LANES), lambda i: (rev(i), 0)),
                  pl.BlockSpec((CHUNK, 3 * LANES), lambda i: (jnp.maximum(rev(i) - 1, 0), 0)),
                  pl.BlockSpec((CHUNK, D_MODEL), lambda i: (rev(i), 0)),
                  _full((GMLP_GROUPS, CHUNK, CHUNK)), _full((GMLP_GROUPS, CHUNK, CHUNK)),
                  _full((CHUNK, GMLP_WIDTH)), _full((N_Q_HEADS, LANES))],
        out_specs=[pl.BlockSpec((CHUNK, IN_PROJ_WIDTH), lambda i: (rev(i), 0)),
                   _full((GMLP_GROUPS, CHUNK, CHUNK)), _full((CHUNK, LANES)), _full((N_Q_HEADS, LANES))],
        scratch_shapes=[pltpu.VMEM((CHUNK, 2 * KV_WIDTH), F32)],
        semantics=("arbitrary",),
    )(proj, proj, rope_tab, rope_tab, dcat, w_spatial, w_spatial_t, bias_full, sink_rows)


def _in_proj_bwd_kernel(x, dx1, dproj, vecs, w_in_t, comm=None):
    seq = x.shape[0]
    tm = 512

    def body(x_ref, dx1_ref, dp_ref, v_ref, w_ref, gx_ref, sums_ref):
        @pl.when(pl.program_id(0) == 0)
        def _():
            sums_ref[...] = jnp.zeros_like(sums_ref)

        g_mix, scale1 = v_ref[0:1, :], v_ref[2:3, :]
        dh = _dot(dp_ref[...], w_ref[...])
        xv = x_ref[...]
        rstd = lax.rsqrt(_mean_last(xv * xv) + EPS)
        xh = xv * rstd
        dn1 = dh * (1.0 + scale1)
        dxh = dn1 * g_mix
        gx_ref[...] = dx1_ref[...] + rstd * (dxh - xh * _mean_last(dxh * xh))
        sums_ref[0:1, :] += _rowsum(dh)
        sums_ref[1:2, :] += _rowsum(dh * (xh * g_mix))
        sums_ref[2:3, :] += _rowsum(dn1 * xh)

    return _hosted_call(
        body, comm, name="in_proj_bwd", grid=(seq // tm,),
        out_shape=[jax.ShapeDtypeStruct((seq, D_MODEL), F32), jax.ShapeDtypeStruct((8, D_MODEL), F32)],
        in_specs=[pl.BlockSpec((tm, D_MODEL), lambda i: (i, 0)), pl.BlockSpec((tm, D_MODEL), lambda i: (i, 0)),
                  pl.BlockSpec((tm, IN_PROJ_WIDTH), lambda i: (i, 0)), _full((8, D_MODEL)),
                  _full((IN_PROJ_WIDTH, D_MODEL))],
        out_specs=[pl.BlockSpec((tm, D_MODEL), lambda i: (i, 0)), _full((8, D_MODEL))],
        semantics=("arbitrary",),
    )(x, dx1, dproj, vecs, w_in_t)


class _GradTiles(NamedTuple):
    tm: int
    tn: int
    n_tiles: int
    chips_per_tile: int
    a_index: Callable
    b_index: Callable


def _weight_grad_kernel(a, b, c_idx, name, tiles, comm=None):
    seq = a.shape[0]
    tk = min(seq, 1024)
    nk = seq // tk
    tm, tn, n_tiles, per = tiles.tm, tiles.tn, tiles.n_tiles, tiles.chips_per_tile
    rows = tm // per

    def half(phase, c):
        return phase * c[0] + (1 - phase) * (1 - c[0])

    def body(c_ref, a_ref, b_ref, o_ref, acc, stage, landed, send_sems, recv_sems):
        del c_ref
        phase, t, kk = pl.program_id(0), pl.program_id(1), pl.program_id(2)
        x, y, c, _ = _mesh_place()

        def copy(tile):
            return pltpu.make_async_remote_copy(
                src_ref=stage.at[tile], dst_ref=landed.at[tile], send_sem=send_sems.at[tile],
                recv_sem=recv_sems.at[tile], device_id=(x, y, 1 - c), device_id_type=MESH)

        @pl.when(kk == 0)
        def _():
            acc[...] = jnp.zeros_like(acc)

        acc[...] += _dot_tn(a_ref[...], b_ref[...])

        @pl.when((kk == nk - 1) & (phase == 0))
        def _():
            stage[t] = acc[...].astype(stage.dtype)
            copy(t).start()

        @pl.when((kk == nk - 1) & (phase == 1))
        def _():
            copy(t).wait_recv()
            total = acc[...] + landed[t].astype(F32)
            for q in range(per):
                o_ref[q] = total[rows * q:rows * (q + 1)].astype(o_ref.dtype)

        @pl.when((kk == nk - 1) & (phase == 1) & (t == n_tiles - 1))
        def _():
            for tile in range(n_tiles):
                copy(tile).wait_send()

    out = _hosted_call(
        body, comm, name=name, grid=(2, n_tiles, nk), n_prefetch=1,
        out_shape=[jax.ShapeDtypeStruct((n_tiles * per, rows, tn), GRAD_COMM_DTYPE)],
        in_specs=[pl.BlockSpec((tk, tm), lambda p, t, k, c: (k, tiles.a_index(t, half(p, c)))),
                  pl.BlockSpec((tk, tn), lambda p, t, k, c: (k, tiles.b_index(t, half(p, c))))],
        out_specs=[pl.BlockSpec((per, rows, tn), lambda p, t, k, c: (p * t, 0, 0))],
        scratch_shapes=[pltpu.VMEM((tm, tn), F32), pltpu.VMEM((n_tiles, tm, tn), GRAD_COMM_DTYPE),
                        pltpu.VMEM((n_tiles, tm, tn), GRAD_COMM_DTYPE),
                        pltpu.SemaphoreType.DMA((n_tiles,)), pltpu.SemaphoreType.DMA((n_tiles,))],
        semantics=("arbitrary", "arbitrary", "arbitrary"),
    )(c_idx, a, b)
    return out[0] if comm is None else out


def _row_tile(rows, most=256, sublanes=16):
    return max(t for t in range(sublanes, most + 1, sublanes) if rows % t == 0)


def _adam_update(w, g, m, v):
    m_new = ADAM_B1 * m + (1.0 - ADAM_B1) * g
    v_new = ADAM_B2 * v + (1.0 - ADAM_B2) * (g * g)
    m_hat = m_new / (1.0 - ADAM_B1 ** ADAM_STEP)
    v_hat = v_new / (1.0 - ADAM_B2 ** ADAM_STEP)
    delta = -ADAM_LR * (m_hat / (jnp.sqrt(v_hat) + ADAM_EPS) + ADAM_WD * w)
    return delta, m_new, v_new


def _sum_chips_kernel(own, others, place, name):
    _, r, n = own.shape
    tr = _row_tile(r)

    def body(place_ref, own_ref, oth_ref, o_ref):
        del place_ref
        acc = own_ref[...].astype(F32)
        for k in range(N_CHIPS - 1):
            acc = acc + oth_ref[k].astype(F32)
        o_ref[...] = acc

    return pl.pallas_call(
        body, name=name, out_shape=jax.ShapeDtypeStruct((2, r, n), F32),
        grid_spec=pltpu.PrefetchScalarGridSpec(
            num_scalar_prefetch=1, grid=(r // tr,),
            in_specs=[pl.BlockSpec((None, tr, n), lambda i, p: (p[0], i, 0)),
                      pl.BlockSpec((N_CHIPS - 1, tr, n), lambda i, p: (0, i, 0))],
            out_specs=pl.BlockSpec((None, tr, n), lambda i, p: (p[1], i, 0))),
        compiler_params=_params("parallel"),
    )(place, own, others)


def _adam_kernel(w, g, m, v, name):
    r, n = w.shape
    by_columns = g.shape[1] == r
    tr, tn = _row_tile(g.shape[1]), g.shape[2]

    def body(w_ref, g_ref, m_ref, v_ref, g_out, d_ref, mo_ref, vo_ref):
        gv = g_ref[...]
        g_out[...] = gv
        d_ref[...], mo_ref[...], vo_ref[...] = _adam_update(w_ref[...], gv, m_ref[...], v_ref[...])

    steps = g.shape[1] // tr
    spec = pl.BlockSpec((tr, tn), (lambda h, i: (i, h)) if by_columns else (lambda h, i: (h * steps + i, 0)))
    return pl.pallas_call(
        body, name=name, grid=(2, steps), out_shape=[jax.ShapeDtypeStruct((r, n), F32)] * 4,
        in_specs=[spec, pl.BlockSpec((None, tr, tn), lambda h, i: (h, i, 0)), spec, spec], out_specs=[spec] * 4,
        compiler_params=_params("parallel", "parallel"),
    )(w, g, m, v)


SMALL_PARAMS = ("b_ada", "g_mix", "g_ffn", "g_final", "b_spatial", "sinks", "w_spatial")


def _small_update_kernel(gathered, params):
    shapes = [params[nm][0].shape for nm in SMALL_PARAMS]

    def body(*refs):
        g_refs, refs = refs[:5], refs[5:]
        p_refs, refs = refs[:3 * len(SMALL_PARAMS)], refs[3 * len(SMALL_PARAMS):]
        loss_ref, o_refs = refs[0], refs[1:]

        def total(ref):
            acc = ref[0]
            for k in range(1, N_DEV):
                acc = acc + ref[k]
            return acc

        s1, s2, db, ds, dw = (total(r) for r in g_refs)
        loss_ref[...] = jnp.broadcast_to(s2[6:7, 0:1], loss_ref.shape)
        grads = {"b_ada": [s1[0:1], s1[1:2], s2[5:6], s2[0:1], s2[1:2], s2[2:3]], "g_mix": [s1[2:3]],
                 "g_ffn": [s2[3:4]], "g_final": [s2[4:5]], "b_spatial": [db.T[0:GMLP_GROUPS]],
                 "w_spatial": [dw]}
        lane = lax.broadcasted_iota(jnp.int32, (1, LANES), 1)
        sink_row = jnp.zeros((1, LANES), F32)
        for h in range(N_Q_HEADS):
            sink_row = sink_row + jnp.where(lane == h, ds[h:h + 1, :], 0.0)
        grads["sinks"] = [sink_row[:, 0:N_Q_HEADS]]
        for i, nm in enumerate(SMALL_PARAMS):
            w_ref, m_ref, v_ref = p_refs[3 * i:3 * i + 3]
            outs = o_refs[4 * i:4 * i + 4]
            width = grads[nm][0].shape[1]
            for k, g in enumerate(grads[nm]):
                cols = slice(width * k, width * (k + 1))
                upd = _adam_update(w_ref[:, cols], g, m_ref[:, cols], v_ref[:, cols])
                for o_ref, val in zip(outs, (g,) + upd):
                    o_ref[:, cols] = val

    flat = [a for nm in SMALL_PARAMS for a in params[nm]]
    out_shape = [jax.ShapeDtypeStruct((8, LANES), F32)]
    out_shape += [jax.ShapeDtypeStruct(s, F32) for s in shapes for _ in range(4)]
    outs = pl.pallas_call(
        body, name="small_update", grid=(1,), out_shape=out_shape,
        in_specs=[_full(g.shape) for g in gathered] + [_full(a.shape) for a in flat],
        out_specs=[_full(s.shape) for s in out_shape],
        compiler_params=_params("arbitrary"),
    )(*gathered, *flat)
    return {nm: outs[1 + 4 * i:5 + 4 * i] for i, nm in enumerate(SMALL_PARAMS)}, outs[0]


def _ada_update_kernel(act_t, dmod, w, m, v):
    r, n = w.shape
    tr = 256

    def body(a_ref, d_ref, w_ref, m_ref, v_ref, g_ref, dl_ref, mo_ref, vo_ref):
        g = _dot(a_ref[...], d_ref[...])
        g_ref[...] = g
        dl_ref[...], mo_ref[...], vo_ref[...] = _adam_update(w_ref[...], g, m_ref[...], v_ref[...])

    spec = pl.BlockSpec((tr, n), lambda i: (i, 0))
    return pl.pallas_call(
        body, name="ada_update", grid=(r // tr,), out_shape=[jax.ShapeDtypeStruct((r, n), F32)] * 4,
        in_specs=[pl.BlockSpec((tr, N_DEV), lambda i: (i, 0)), _full((N_DEV, n)), spec, spec, spec],
        out_specs=[spec] * 4, compiler_params=_params("parallel"),
    )(act_t, dmod, w, m, v)


def kernel(x, c, positions, w_ada, b_ada, g_mix, w_in, w_spatial, b_spatial, sinks, w_out, g_ffn, w_ff1, w_ff2, g_final, loss_target, m_w_ada, m_b_ada, m_g_mix, m_w_in, m_w_spatial, m_b_spatial, m_sinks, m_w_out, m_g_ffn, m_w_ff1, m_w_ff2, m_g_final, v_w_ada, v_b_ada, v_g_mix, v_w_in, v_w_spatial, v_b_spatial, v_sinks, v_w_out, v_g_ffn, v_w_ff1, v_w_ff2, v_g_final):
    xi, yi, ci = lax.axis_index("x"), lax.axis_index("y"), lax.axis_index("c")
    chip = 2 * xi + yi
    dev = 2 * chip + ci
    seq = x.shape[1]
    x2, tgt = x[0], loss_target[0]
    ada_cols = w_ada.shape[2]

    c_all = _all_gather8([c], "gather_c")[0].reshape(N_DEV, D_MODEL)
    b_shard = lax.dynamic_slice(b_ada, (0, chip * ada_cols), (1, ada_cols))
    mod_part, act = _mod_kernel(c_all, w_ada[0], b_shard)
    mod_all, = _all_gather8([mod_part], "gather_mod")
    mod_me = lax.dynamic_index_in_dim(mod_all[0::2], dev, axis=1, keepdims=False)
    mod_me = mod_me.reshape(N_MOD, D_MODEL)
    shift1, scale1, gate1, shift2, scale2, gate2 = (mod_me[k:k + 1] for k in range(N_MOD))

    big = {"w_in": tuple(a[0].T for a in (w_in, m_w_in, v_w_in)),
           "w_out": (w_out[0], m_w_out[0], v_w_out[0]), "w_ff1": (w_ff1[0], m_w_ff1[0], v_w_ff1[0]),
           "w_ff2": (w_ff2[0], m_w_ff2[0], v_w_ff2[0])}

    def halves(nm):
        r, n = big[nm][0].shape
        return big[nm][0].astype(WEIGHT_COMM_DTYPE).reshape(2, r // 2, n)

    w_in_t, = _all_gather8([halves("w_in")], "gather_w_in", split=True)
    w_in_t = w_in_t.reshape(IN_PROJ_WIDTH, D_MODEL)

    zeros_row = jnp.zeros((1, D_MODEL), F32)
    vecs1 = jnp.concatenate([g_mix, shift1, scale1] + [zeros_row] * 5, axis=0)
    vecs2 = jnp.concatenate([gate1, shift2, scale2, gate2, g_ffn, g_final.reshape(1, D_MODEL)]
                            + [zeros_row] * 2, axis=0)
    bias_full = jnp.repeat(b_spatial[0].T, HEAD_DIM, axis=1)
    sink_rows = jnp.broadcast_to(sinks[0][:, None], (N_Q_HEADS, LANES))
    inv_freq = ROPE_THETA ** (-jnp.arange(0, ROT_DIM, 2, dtype=F32) / ROT_DIM)
    rope_tab = _rope_lane_tables(*_rope_angle_kernel(positions, inv_freq.reshape(ROT_DIM // 2, 1)))

    proj, hb, g_ff1 = _in_proj_kernel(x2, vecs1, w_in_t, comm=_gather_job([halves("w_ff1")]))
    cat, g_ff2, g_out = _mixer_fwd_kernel(proj, rope_tab, w_spatial[0], bias_full, sink_rows,
                                          comm=_gather_job([halves("w_ff2"), halves("w_out")]))
    g_ff1, g_ff2, g_out = _gather_forward([g_ff1, g_ff2, g_out], "gather_forward")
    w_out_full = g_out.reshape(D_MODEL, D_MODEL)
    w_ff1_blocks = g_ff1.reshape(N_CHIPS, D_MODEL, D_MODEL)
    w_ff2_blocks = g_ff2.reshape(N_CHIPS, D_MODEL, D_MODEL)
    dx1, dcat, dmix, h2b, rb, dab, dffb, sums2 = _trunk_kernel(
        x2, tgt, cat, vecs2, w_out_full, w_ff1_blocks, w_ff2_blocks)

    c_idx = ci.reshape(1).astype(jnp.int32)
    place = jnp.stack([chip, ci]).astype(jnp.int32)
    half_d = D_MODEL // 2
    cs_ff2 = _weight_grad_kernel(rb, dffb, c_idx, "dw_ff2",
                                 _GradTiles(D_MODEL, half_d, N_CHIPS, 1, lambda t, h: t, lambda t, h: h))
    cs_ff1 = _weight_grad_kernel(h2b, dab, c_idx, "dw_ff1",
                                 _GradTiles(D_MODEL, half_d, N_CHIPS, 1, lambda t, h: 0, lambda t, h: 2 * t + h))
    cs_out = _weight_grad_kernel(cat, dmix, c_idx, "dw_out",
                                 _GradTiles(D_MODEL, half_d, 1, N_CHIPS, lambda t, h: 0, lambda t, h: h))
    dproj, dw_spatial, db_lanes, dsink_rows, sc_ff2, sc_out = _mixer_bwd_kernel(
        proj, rope_tab, dcat, w_spatial[0], w_spatial[0].transpose(0, 2, 1), bias_full, sink_rows,
        comm=_scatter_job([cs_ff2, cs_out]))
    cs_in, sc_ff1 = _weight_grad_kernel(
        dproj, hb, c_idx, "dw_in",
        _GradTiles(2 * W_IN_BLOCK, half_d, N_CHIPS // 2, 2, lambda t, h: t, lambda t, h: h),
        comm=_scatter_job([cs_ff1]))
    grad_x, sums1 = _in_proj_bwd_kernel(x2, dx1, dproj, vecs1, w_in_t)
    sc_in, = _run_comm(_scatter_job([cs_in]), "grad_to_chips_w_in")

    names = ["w_in", "w_out", "w_ff1", "w_ff2"]
    totals = [_sum_chips_kernel(own, oth, place, "grad_sum_" + nm)
              for nm, own, oth in zip(names, [cs_in, cs_out, cs_ff1, cs_ff2], [sc_in, sc_out, sc_ff1, sc_ff2])]
    shared = _sibling_share(totals, "grad_share")
    big_out = {}
    for nm, g in zip(names, shared):
        w, m, v = big[nm]
        outs = _adam_kernel(w, g, m, v, "adam_" + nm)
        big_out[nm] = tuple((t.T if nm == "w_in" else t)[None] for t in outs)

    small = {"b_ada": (b_ada, m_b_ada, v_b_ada), "g_mix": (g_mix, m_g_mix, v_g_mix),
             "g_ffn": (g_ffn, m_g_ffn, v_g_ffn), "g_final": (g_final, m_g_final, v_g_final),
             "b_spatial": (b_spatial, m_b_spatial, v_b_spatial), "sinks": (sinks, m_sinks, v_sinks),
             "w_spatial": (w_spatial, m_w_spatial, v_w_spatial)}
    flat_shape = {"g_final": (1, D_MODEL), "b_spatial": (GMLP_GROUPS, CHUNK), "w_spatial": (GMLP_GROUPS * CHUNK, CHUNK)}
    gathered = _all_gather8([sums1, sums2, db_lanes, dsink_rows, dw_spatial.reshape(GMLP_GROUPS * CHUNK, CHUNK)],
                            "gather_small")
    small_out, loss_tile = _small_update_kernel(
        gathered, {nm: tuple(a.reshape(flat_shape.get(nm, a.shape)) for a in small[nm]) for nm in small})
    small_out = {nm: [o.reshape(small[nm][0].shape) for o in small_out[nm]] for nm in small}
    loss = loss_tile[0, 0]

    g1, g2 = gathered[0], gathered[1]
    dmod_all = jnp.concatenate([g1[:, 0], g1[:, 1], g2[:, 5], g2[:, 0], g2[:, 1], g2[:, 2]], axis=1)
    dmod_cols = lax.dynamic_slice(dmod_all, (0, chip * ada_cols), (N_DEV, ada_cols))
    ada = _ada_update_kernel(act.T, dmod_cols, w_ada[0], m_w_ada[0], v_w_ada[0])
    big_out["w_ada"] = tuple(t[None] for t in ada)

    order = ["w_ada", "b_ada", "g_mix", "w_in", "w_spatial", "b_spatial", "sinks", "w_out", "g_ffn",
             "w_ff1", "w_ff2", "g_final"]

    def leaf(nm, k):
        return big_out[nm][k] if nm in big_out else small_out[nm][k]

    outs = [loss, grad_x[None]]
    for k in range(4):
        outs += [leaf(nm, k) for nm in order]
    return tuple(outs)
```

```python
import math
from typing import Callable, NamedTuple

import jax
import jax.numpy as jnp
from jax import lax
from jax.experimental import pallas as pl
from jax.experimental.pallas import tpu as pltpu

F32 = jnp.float32
MXU_DTYPE = jnp.bfloat16
WEIGHT_COMM_DTYPE = jnp.bfloat16
GRAD_COMM_DTYPE = jnp.bfloat16

D_MODEL = 1024
D_FF = 4096
HEAD_DIM = 64
GMLP_GROUPS = 8
GMLP_WIDTH = 512
CHUNK = 128
N_Q_HEADS = 8
N_KV_HEADS = 2
ATTN_WIDTH = 512
KV_WIDTH = 128
ROT_DIM = 16
ROPE_THETA = 500000.0
IN_PROJ_WIDTH = 1792
N_MOD = 6
EPS = 1e-5
N_CHIPS = 4
N_DEV = 8
LANES = 128
W_IN_BLOCK = IN_PROJ_WIDTH // N_CHIPS

ADAM_LR = 0.001
ADAM_B1 = 0.9
ADAM_B2 = 0.999
ADAM_EPS = 1e-08
ADAM_WD = 0.01
ADAM_STEP = 10

VMEM_LIMIT_BYTES = 58 * 1024 * 1024
MESH = pl.DeviceIdType.MESH


def _params(*semantics):
    return pltpu.CompilerParams(dimension_semantics=semantics, vmem_limit_bytes=VMEM_LIMIT_BYTES)


def _dot(a, b):
    return jnp.dot(a.astype(MXU_DTYPE), b.astype(MXU_DTYPE), preferred_element_type=F32)


def _dot_nt(a, b):
    return lax.dot_general(a.astype(MXU_DTYPE), b.astype(MXU_DTYPE), (((1,), (1,)), ((), ())),
                           preferred_element_type=F32)


def _dot_tn(a, b):
    return lax.dot_general(a.astype(MXU_DTYPE), b.astype(MXU_DTYPE), (((0,), (0,)), ((), ())),
                           preferred_element_type=F32)


def _full(shape):
    return pl.BlockSpec(shape, lambda *_: (0,) * len(shape))


def _any():
    return pl.BlockSpec(memory_space=pl.ANY)


def _rowsum(v):
    return jnp.sum(v, axis=0, keepdims=True)


def _mean_last(v):
    return jnp.mean(v, axis=-1, keepdims=True)


class _Comm(NamedTuple):
    operands: tuple
    out_shapes: tuple
    n_sems: int
    make: Callable


def _hosted_call(body, comm, *, name, grid, in_specs, out_shape, out_specs, scratch_shapes=(), semantics,
                 n_prefetch=0):
    if comm is None:
        return pl.pallas_call(
            body, name=name, out_shape=out_shape, compiler_params=_params(*semantics),
            grid_spec=pltpu.PrefetchScalarGridSpec(
                num_scalar_prefetch=n_prefetch, grid=grid, in_specs=in_specs, out_specs=out_specs,
                scratch_shapes=list(scratch_shapes)))
    n_in, n_out, n_scr = len(in_specs), len(out_shape), len(scratch_shapes)
    k_in, k_out = len(comm.operands), len(comm.out_shapes)

    def hosted(*refs):
        prefetched, refs = refs[:n_prefetch], refs[n_prefetch:]
        ins, refs = refs[:n_in], refs[n_in:]
        c_ins, refs = refs[:k_in], refs[k_in:]
        outs, refs = refs[:n_out], refs[n_out:]
        c_outs, refs = refs[:k_out], refs[k_out:]
        scratch, (send_sems, recv_sems) = refs[:n_scr], refs[n_scr:]
        first, last = None, None
        for d, size in enumerate(grid):
            at_start, at_end = pl.program_id(d) == 0, pl.program_id(d) == size - 1
            first = at_start if first is None else first & at_start
            last = at_end if last is None else last & at_end

        @pl.when(first)
        def _():
            for cp in comm.make(c_ins, c_outs, send_sems, recv_sems)[0]:
                cp.start()

        body(*prefetched, *ins, *outs, *scratch)

        @pl.when(last)
        def _():
            for wait in comm.make(c_ins, c_outs, send_sems, recv_sems)[1]:
                wait()

    call = pl.pallas_call(
        hosted, name=name, out_shape=list(out_shape) + list(comm.out_shapes),
        compiler_params=_params(*semantics),
        grid_spec=pltpu.PrefetchScalarGridSpec(
            num_scalar_prefetch=n_prefetch, grid=grid, in_specs=list(in_specs) + [_any()] * k_in,
            out_specs=list(out_specs) + [_any()] * k_out,
            scratch_shapes=list(scratch_shapes) + [pltpu.SemaphoreType.DMA((comm.n_sems,)),
                                                    pltpu.SemaphoreType.DMA((comm.n_sems,))]))
    return lambda *args: call(*args, *comm.operands)


def _mesh_place():
    x, y, c = lax.axis_index("x"), lax.axis_index("y"), lax.axis_index("c")
    return x, y, c, [(1 - x, y), (x, 1 - y), (1 - x, 1 - y)]


def _gather_job(halves):
    per = 5

    def make(ins, outs, send_sems, recv_sems):
        x, y, c, chips = _mesh_place()
        starts, waits = [], []
        for a, (src, out) in enumerate(zip(ins, outs)):
            mine = out.at[4 * x + 2 * y + c]
            local = pltpu.make_async_copy(src.at[c], mine, send_sems.at[per * a + 4])
            to = [(x, y, 1 - c)] + [(px, py, c) for px, py in chips]
            sends = [pltpu.make_async_remote_copy(
                src_ref=src.at[c], dst_ref=mine, send_sem=send_sems.at[per * a + k],
                recv_sem=recv_sems.at[per * a + k], device_id=dev, device_id_type=MESH)
                for k, dev in enumerate(to)]
            recvs = [pltpu.make_async_remote_copy(
                src_ref=src.at[c], dst_ref=out.at[4 * px + 2 * py + pc], send_sem=send_sems.at[per * a + k],
                recv_sem=recv_sems.at[per * a + k], device_id=(px, py, pc), device_id_type=MESH)
                for k, (px, py, pc) in enumerate(to)]
            starts += [local] + sends
            waits += [local.wait] + [s.wait_send for s in sends] + [r.wait_recv for r in recvs]
        return starts, waits

    return _Comm(tuple(halves), tuple(jax.ShapeDtypeStruct((N_DEV,) + h.shape[1:], h.dtype) for h in halves),
                 per * len(halves), make)


def _gather_forward(bufs, name):
    n_arr = len(bufs)

    def body(*refs):
        outs = refs[n_arr:2 * n_arr]
        send_sems, recv_sems = refs[2 * n_arr:]
        x, y, c, chips = _mesh_place()
        sends, recvs = [], []
        for a, buf in enumerate(outs):
            for j, (px, py) in enumerate(chips):
                mine, theirs = buf.at[4 * px + 2 * py + c], buf.at[4 * px + 2 * py + 1 - c]
                sems = dict(send_sem=send_sems.at[3 * a + j], recv_sem=recv_sems.at[3 * a + j],
                            device_id=(x, y, 1 - c), device_id_type=MESH)
                sends.append(pltpu.make_async_remote_copy(src_ref=mine, dst_ref=mine, **sems))
                recvs.append(pltpu.make_async_remote_copy(src_ref=mine, dst_ref=theirs, **sems))
        for cp in sends:
            cp.start()
        for s, r in zip(sends, recvs):
            s.wait_send()
            r.wait_recv()

    return pl.pallas_call(
        body, name=name, out_shape=[jax.ShapeDtypeStruct(b.shape, b.dtype) for b in bufs],
        in_specs=[_any()] * n_arr, out_specs=[_any()] * n_arr,
        input_output_aliases={a: a for a in range(n_arr)},
        scratch_shapes=[pltpu.SemaphoreType.DMA((3 * n_arr,)), pltpu.SemaphoreType.DMA((3 * n_arr,))],
    )(*bufs)


def _scatter_job(chip_sums):
    def make(ins, outs, send_sems, recv_sems):
        x, y, c, chips = _mesh_place()
        copies = [pltpu.make_async_remote_copy(
            src_ref=src.at[2 * px + py], dst_ref=out.at[j], send_sem=send_sems.at[3 * a + j],
            recv_sem=recv_sems.at[3 * a + j], device_id=(px, py, c), device_id_type=MESH)
            for a, (src, out) in enumerate(zip(ins, outs)) for j, (px, py) in enumerate(chips)]
        return copies, [cp.wait for cp in copies]

    return _Comm(tuple(chip_sums), tuple(jax.ShapeDtypeStruct((3,) + s.shape[1:], s.dtype) for s in chip_sums),
                 3 * len(chip_sums), make)


def _run_comm(comm, name):
    def body(token_ref):
        token_ref[...] = jnp.zeros_like(token_ref)

    out = _hosted_call(body, comm, name=name, grid=(1,), in_specs=[],
                       out_shape=[jax.ShapeDtypeStruct((8, LANES), F32)], out_specs=[_full((8, LANES))],
                       semantics=("arbitrary",))()
    return out[1:]


def _all_gather8(blocks, name, split=False):
    n_arr = len(blocks)

    def body(*refs):
        x_refs, out_refs = refs[:n_arr], refs[n_arr:2 * n_arr]
        send_sems, recv_sems, local_sems = refs[2 * n_arr:]
        x, y, c, chips = _mesh_place()
        me, sibling = (x, y, c), (x, y, 1 - c)
        arrays = []
        for a, (x_ref, out_ref) in enumerate(zip(x_refs, out_refs)):
            src_mine = x_ref.at[c] if split else x_ref

            def copy(k, blk, to, src=None, a=a, out_ref=out_ref):
                dst = out_ref.at[4 * blk[0] + 2 * blk[1] + blk[2]]
                return pltpu.make_async_remote_copy(
                    src_ref=dst if src is None else src, dst_ref=dst,
                    send_sem=send_sems.at[7 * a + k], recv_sem=recv_sems.at[7 * a + k],
                    device_id=to, device_id_type=MESH)

            mine = pltpu.make_async_copy(src_mine, out_ref.at[4 * x + 2 * y + c], local_sems.at[a])
            mine.start()
            first = [copy(0, me, sibling, src=src_mine)]
            first += [copy(1 + j, me, (*chip, c), src=src_mine) for j, chip in enumerate(chips)]
            for cp in first:
                cp.start()
            arrays.append((copy, mine, first))
        sent = []
        for copy, mine, first in arrays:
            passed = [copy(4 + j, (*chip, c), sibling) for j, chip in enumerate(chips)]
            for j, chip in enumerate(chips):
                copy(1 + j, (*chip, c), me).wait_recv()
                passed[j].start()
            sent += first + passed
        for copy, mine, first in arrays:
            copy(0, sibling, me).wait_recv()
            for j, chip in enumerate(chips):
                copy(4 + j, (*chip, 1 - c), me).wait_recv()
            mine.wait()
        for cp in sent:
            cp.wait_send()

    return pl.pallas_call(
        body, name=name,
        out_shape=[jax.ShapeDtypeStruct((N_DEV,) + tuple(b.shape[1:] if split else b.shape), b.dtype)
                   for b in blocks],
        in_specs=[_any()] * n_arr, out_specs=[_any()] * n_arr,
        scratch_shapes=[pltpu.SemaphoreType.DMA((7 * n_arr,)), pltpu.SemaphoreType.DMA((7 * n_arr,)),
                        pltpu.SemaphoreType.DMA((n_arr,))],
    )(*blocks)


def _sibling_share(bufs, name):
    n_arr = len(bufs)

    def body(*refs):
        out_refs = refs[n_arr:2 * n_arr]
        send_sems, recv_sems = refs[2 * n_arr:]
        x, y, c = lax.axis_index("x"), lax.axis_index("y"), lax.axis_index("c")
        copies = [pltpu.make_async_remote_copy(
            src_ref=out_refs[a].at[c], dst_ref=out_refs[a].at[c],
            send_sem=send_sems.at[a], recv_sem=recv_sems.at[a],
            device_id=(x, y, 1 - c), device_id_type=MESH) for a in range(n_arr)]
        for cp in copies:
            cp.start()
        for a in range(n_arr):
            pltpu.make_async_remote_copy(
                src_ref=out_refs[a].at[c], dst_ref=out_refs[a].at[1 - c],
                send_sem=send_sems.at[a], recv_sem=recv_sems.at[a],
                device_id=(x, y, 1 - c), device_id_type=MESH).wait()

    return pl.pallas_call(
        body, name=name,
        out_shape=[jax.ShapeDtypeStruct(b.shape, b.dtype) for b in bufs],
        in_specs=[_any()] * n_arr, out_specs=[_any()] * n_arr,
        input_output_aliases={a: a for a in range(n_arr)},
        scratch_shapes=[pltpu.SemaphoreType.DMA((n_arr,)), pltpu.SemaphoreType.DMA((n_arr,))],
    )(*bufs)


def _gelu_tanh(z):
    k = math.sqrt(2.0 / math.pi)
    t = jnp.tanh(k * (z + 0.044715 * (z * z * z)))
    return 0.5 * z * (1.0 + t), t


def _gelu_tanh_grad(z, t):
    k = math.sqrt(2.0 / math.pi)
    return 0.5 * (1.0 + t) + 0.5 * z * (1.0 - t * t) * (k * (1.0 + 3.0 * 0.044715 * (z * z)))


def _rope_angle_kernel(pos_row, invf_col):
    seq = pos_row.shape[1]

    def body(p_ref, f_ref, cos_ref, sin_ref):
        ang = p_ref[...].astype(F32) * f_ref[...]
        cos_ref[...] = jnp.cos(ang)
        sin_ref[...] = jnp.sin(ang)

    return pl.pallas_call(
        body, name="rope_angles", grid=(1,), out_shape=[jax.ShapeDtypeStruct((ROT_DIM // 2, seq), F32)] * 2,
        in_specs=[_full((1, seq)), _full((ROT_DIM // 2, 1))], out_specs=[_full((ROT_DIM // 2, seq))] * 2,
        compiler_params=_params("arbitrary"),
    )(pos_row, invf_col)


def _rope_lane_tables(cos, sin):
    cos_t, sin_t = cos.T, sin.T
    seq, half = cos_t.shape
    ones = jnp.ones((seq, HEAD_DIM - ROT_DIM), F32)
    c64 = jnp.concatenate([cos_t, cos_t, ones], axis=1)
    s1 = jnp.concatenate([sin_t, jnp.zeros((seq, HEAD_DIM - half), F32)], axis=1)
    s2 = jnp.concatenate([jnp.zeros((seq, half), F32), sin_t, jnp.zeros((seq, HEAD_DIM - ROT_DIM), F32)], axis=1)
    return jnp.concatenate([jnp.tile(t, (1, LANES // HEAD_DIM)) for t in (c64, s1, s2)], axis=1)


def _rope_apply(t, tab, sign):
    reps = t.shape[1] // LANES
    c_tab, s1, s2 = (jnp.tile(tab[:, LANES * k:LANES * (k + 1)], (1, reps)) if reps > 1
                     else tab[:, LANES * k:LANES * (k + 1)] for k in range(3))
    half = ROT_DIM // 2
    up = pltpu.roll(t, t.shape[1] - half, 1)
    down = pltpu.roll(t, half, 1)
    return t * c_tab + sign * (down * s2 - up * s1)


def _lane_masks(shape):
    lane = lax.broadcasted_iota(jnp.int32, shape, 1)
    return lane < HEAD_DIM, lane >= HEAD_DIM


HEADS_PER_GROUP = N_Q_HEADS // N_KV_HEADS
ATTN_SCALE = 1.0 / math.sqrt(HEAD_DIM)


def _attn_bias_t(first_block):
    kj = lax.broadcasted_iota(jnp.int32, (2 * CHUNK, CHUNK), 0)
    qi = lax.broadcasted_iota(jnp.int32, (2 * CHUNK, CHUNK), 1)
    ok = (kj > qi) & (kj <= qi + CHUNK) & (jnp.logical_not(first_block) | (kj >= CHUNK))
    return jnp.tile(jnp.where(ok, 0.0, -jnp.inf), (1, HEADS_PER_GROUP))


def _group_rows(x, g, lo, hi):
    rows = []
    for r in range(HEADS_PER_GROUP):
        h = HEADS_PER_GROUP * g + r
        pair = x[:, LANES * (h // 2):LANES * (h // 2 + 1)]
        rows.append(jnp.where(hi if h % 2 else lo, pair, 0.0))
    return jnp.concatenate(rows, axis=0)


def _pairs_from_rows(rows, lo):
    return [jnp.where(lo, rows[2 * CHUNK * k:2 * CHUNK * k + CHUNK], rows[2 * CHUNK * k + CHUNK:2 * CHUNK * (k + 1)])
            for k in range(HEADS_PER_GROUP // 2)]


def _group_dup(a, b, g, lo2):
    return jnp.where(lo2, a, b) if g == 0 else jnp.where(lo2, b, a)


def _sink_row(sink_ref, g):
    return jnp.concatenate([sink_ref[HEADS_PER_GROUP * g + r:HEADS_PER_GROUP * g + r + 1, :]
                            for r in range(HEADS_PER_GROUP)], axis=1)


def _attn_probs_t(k_dup, q_rows, bias_t, sink_row):
    s_t = _dot_nt(k_dup, q_rows) * ATTN_SCALE + bias_t
    m = jnp.maximum(jnp.max(s_t, axis=0, keepdims=True), sink_row)
    p = jnp.exp(s_t - m)
    e_sink = jnp.exp(sink_row - m)
    inv = 1.0 / (jnp.sum(p, axis=0, keepdims=True) + e_sink)
    return p * inv, e_sink * inv


def _sgu_forward_pair(wm, vp, j):
    lo, hi = _lane_masks(vp.shape)
    lhs = jnp.concatenate([wm[2 * j], wm[2 * j + 1]], axis=1)
    rhs = jnp.concatenate([jnp.where(lo, vp, 0.0), jnp.where(hi, vp, 0.0)], axis=0)
    return _dot(lhs, rhs)


def _masked_spatial(w_ref):
    t = lax.broadcasted_iota(jnp.int32, (CHUNK, CHUNK), 0)
    s = lax.broadcasted_iota(jnp.int32, (CHUNK, CHUNK), 1)
    tril = s <= t
    return [jnp.where(tril, w_ref[g], 0.0) for g in range(GMLP_GROUPS)], tril, s >= t


def _mod_kernel(c_all, w_shard, b_shard):
    n = w_shard.shape[1]
    tn = 512

    def body(c_ref, w_ref, b_ref, mod_ref, act_ref):
        cv = c_ref[...]
        act = cv * (1.0 / (1.0 + jnp.exp(-cv)))
        act_ref[...] = act
        mod_ref[...] = _dot(act, w_ref[...]) + b_ref[...]

    return pl.pallas_call(
        body, name="ada_mod", grid=(n // tn,),
        out_shape=[jax.ShapeDtypeStruct((N_DEV, n), F32), jax.ShapeDtypeStruct((N_DEV, D_MODEL), F32)],
        in_specs=[_full((N_DEV, D_MODEL)), pl.BlockSpec((D_MODEL, tn), lambda i: (0, i)),
                  pl.BlockSpec((1, tn), lambda i: (0, i))],
        out_specs=[pl.BlockSpec((N_DEV, tn), lambda i: (0, i)), _full((N_DEV, D_MODEL))],
        compiler_params=_params("arbitrary"),
    )(c_all, w_shard, b_shard)


def _in_proj_kernel(x, vecs, w_in_t, comm=None):
    seq = x.shape[0]
    tm = 512

    def body(x_ref, v_ref, w_ref, proj_ref, h_ref):
        xv = x_ref[...]
        rstd = lax.rsqrt(_mean_last(xv * xv) + EPS)
        n1 = (xv * rstd) * v_ref[0:1, :]
        h = n1 * (1.0 + v_ref[2:3, :]) + v_ref[1:2, :]
        hb = h.astype(MXU_DTYPE)
        h_ref[...] = hb
        proj_ref[...] = _dot_nt(hb, w_ref[...])

    return _hosted_call(
        body, comm, name="in_proj", grid=(seq // tm,),
        out_shape=[jax.ShapeDtypeStruct((seq, IN_PROJ_WIDTH), F32),
                   jax.ShapeDtypeStruct((seq, D_MODEL), MXU_DTYPE)],
        in_specs=[pl.BlockSpec((tm, D_MODEL), lambda i: (i, 0)), _full((8, D_MODEL)),
                  _full((IN_PROJ_WIDTH, D_MODEL))],
        out_specs=[pl.BlockSpec((tm, IN_PROJ_WIDTH), lambda i: (i, 0)),
                   pl.BlockSpec((tm, D_MODEL), lambda i: (i, 0))],
        semantics=("arbitrary",),
    )(x, vecs, w_in_t)


def _mixer_fwd_kernel(proj, rope_tab, w_spatial, bias_full, sink_rows, comm=None):
    seq = proj.shape[0]
    nb = seq // CHUNK
    kv_col = (2 * GMLP_WIDTH + ATTN_WIDTH) // (2 * KV_WIDTH)

    def body(proj_ref, prev_ref, tab_ref, ptab_ref, w_ref, bias_ref, sink_ref, cat_ref):
        i = pl.program_id(0)
        wm, _, _ = _masked_spatial(w_ref)
        for j in range(GMLP_GROUPS // 2):
            cols = slice(LANES * j, LANES * (j + 1))
            vcols = slice(GMLP_WIDTH + LANES * j, GMLP_WIDTH + LANES * (j + 1))
            u, _ = _gelu_tanh(proj_ref[:, cols])
            vp, _ = _gelu_tanh(proj_ref[:, vcols])
            sv = _sgu_forward_pair(wm, vp, j) + bias_ref[:, cols]
            cat_ref[:, cols] = (u * sv).astype(cat_ref.dtype)
        o = 2 * GMLP_WIDTH
        tab = tab_ref[...]
        q_r = _rope_apply(proj_ref[:, o:o + ATTN_WIDTH], tab, 1.0)
        k_cur = _rope_apply(proj_ref[:, o + ATTN_WIDTH:o + ATTN_WIDTH + KV_WIDTH], tab, 1.0)
        k_prev = _rope_apply(prev_ref[:, 0:KV_WIDTH], ptab_ref[...], 1.0)
        k_a = jnp.concatenate([k_prev, k_cur], axis=0)
        v_a = jnp.concatenate([prev_ref[:, KV_WIDTH:2 * KV_WIDTH],
                               proj_ref[:, o + ATTN_WIDTH + KV_WIDTH:o + ATTN_WIDTH + 2 * KV_WIDTH]], axis=0)
        k_b = pltpu.roll(k_a, HEAD_DIM, 1)
        v_b = pltpu.roll(v_a, HEAD_DIM, 1)
        bias_t = _attn_bias_t(i == 0)
        lo, hi = _lane_masks((CHUNK, LANES))
        lo2, _ = _lane_masks((2 * CHUNK, LANES))
        for g in range(N_KV_HEADS):
            p_t, _ = _attn_probs_t(_group_dup(k_a, k_b, g, lo2), _group_rows(q_r, g, lo, hi), bias_t,
                                   _sink_row(sink_ref, g))
            o_t = _dot(_group_dup(v_a, v_b, g, lo2).T, p_t)
            for k, pair in enumerate(_pairs_from_rows(o_t.T, lo)):
                c0 = GMLP_WIDTH + LANES * (2 * g + k)
                cat_ref[:, c0:c0 + LANES] = pair.astype(cat_ref.dtype)

    return _hosted_call(
        body, comm, name="mixer_fwd", grid=(nb,),
        out_shape=[jax.ShapeDtypeStruct((seq, D_MODEL), MXU_DTYPE)],
        in_specs=[pl.BlockSpec((CHUNK, IN_PROJ_WIDTH), lambda i: (i, 0)),
                  pl.BlockSpec((CHUNK, 2 * KV_WIDTH), lambda i: (jnp.maximum(i - 1, 0), kv_col)),
                  pl.BlockSpec((CHUNK, 3 * LANES), lambda i: (i, 0)),
                  pl.BlockSpec((CHUNK, 3 * LANES), lambda i: (jnp.maximum(i - 1, 0), 0)),
                  _full((GMLP_GROUPS, CHUNK, CHUNK)), _full((CHUNK, GMLP_WIDTH)),
                  _full((N_Q_HEADS, LANES))],
        out_specs=[pl.BlockSpec((CHUNK, D_MODEL), lambda i: (i, 0))],
        semantics=("arbitrary",),
    )(proj, proj, rope_tab, rope_tab, w_spatial, bias_full, sink_rows)


def _trunk_kernel(x, target, cat, vecs, w_out, w_ff1, w_ff2):
    seq = x.shape[0]
    tm = 256
    nj = D_FF // D_MODEL

    def body(x_ref, t_ref, cat_ref, v_ref, wout_hbm, w1_hbm, w2_hbm,
             dx1_ref, dcat_ref, dmix_ref, h2_ref, r_ref, da_ref, dff_ref, sums_ref,
             wout, w1, w2, a_scr, sem):
        i = pl.program_id(0)

        @pl.when(i == 0)
        def _():
            copies = [pltpu.make_async_copy(wout_hbm, wout, sem.at[0]),
                      pltpu.make_async_copy(w1_hbm, w1, sem.at[1]),
                      pltpu.make_async_copy(w2_hbm, w2, sem.at[2])]
            for cp in copies:
                cp.start()
            for cp in copies:
                cp.wait()
            sums_ref[...] = jnp.zeros_like(sums_ref)

        gate1, shift2, scale2 = v_ref[0:1, :], v_ref[1:2, :], v_ref[2:3, :]
        gate2, g_ffn, g_final = v_ref[3:4, :], v_ref[4:5, :], v_ref[5:6, :]

        mix = _dot(cat_ref[...], wout[...])
        x1 = x_ref[...] + gate1 * mix
        rstd2 = lax.rsqrt(_mean_last(x1 * x1) + EPS)
        xh2 = x1 * rstd2
        n2 = xh2 * g_ffn
        h2b = (n2 * (1.0 + scale2) + shift2).astype(MXU_DTYPE)
        h2_ref[...] = h2b
        ff = jnp.zeros((tm, D_MODEL), F32)
        for j in range(nj):
            a = _dot(h2b, w1[j])
            a_scr[j] = a
            relu = jnp.maximum(a, 0.0)
            rb = (relu * relu).astype(MXU_DTYPE)
            r_ref[:, D_MODEL * j:D_MODEL * (j + 1)] = rb
            ff = ff + _dot(rb, w2[j])
        x2 = x1 + gate2 * ff
        rstd3 = lax.rsqrt(_mean_last(x2 * x2) + EPS)
        xh3 = x2 * rstd3
        err = xh3 * g_final - t_ref[...]
        loss = 0.5 * _rowsum(_mean_last(err * err))
        dy = err * (1.0 / D_MODEL)
        dxh3 = dy * g_final
        dx2 = rstd3 * (dxh3 - xh3 * _mean_last(dxh3 * xh3))
        dffb = (dx2 * gate2).astype(MXU_DTYPE)
        dff_ref[...] = dffb
        dh2 = jnp.zeros((tm, D_MODEL), F32)
        for j in range(nj):
            dr = _dot_nt(dffb, w2[j])
            dab = (dr * (2.0 * jnp.maximum(a_scr[j], 0.0))).astype(MXU_DTYPE)
            da_ref[:, D_MODEL * j:D_MODEL * (j + 1)] = dab
            dh2 = dh2 + _dot_nt(dab, w1[j])
        dn2 = dh2 * (1.0 + scale2)
        dxh2 = dn2 * g_ffn
        dx1 = dx2 + rstd2 * (dxh2 - xh2 * _mean_last(dxh2 * xh2))
        dx1_ref[...] = dx1
        dmixb = (dx1 * gate1).astype(MXU_DTYPE)
        dmix_ref[...] = dmixb
        dcat_ref[...] = _dot_nt(dmixb, wout[...])

        sums_ref[0:1, :] += _rowsum(dh2)
        sums_ref[1:2, :] += _rowsum(dh2 * n2)
        sums_ref[2:3, :] += _rowsum(dx2 * ff)
        sums_ref[3:4, :] += _rowsum(dn2 * xh2)
        sums_ref[4:5, :] += _rowsum(dy * xh3)
        sums_ref[5:6, :] += _rowsum(dx1 * mix)
        sums_ref[6:7, :] += jnp.broadcast_to(loss, (1, D_MODEL))

    tok = lambda w: pl.BlockSpec((tm, w), lambda i: (i, 0))
    return pl.pallas_call(
        body, name="trunk", grid=(seq // tm,),
        out_shape=[jax.ShapeDtypeStruct((seq, D_MODEL), F32), jax.ShapeDtypeStruct((seq, D_MODEL), F32),
                   jax.ShapeDtypeStruct((seq, D_MODEL), MXU_DTYPE), jax.ShapeDtypeStruct((seq, D_MODEL), MXU_DTYPE),
                   jax.ShapeDtypeStruct((seq, D_FF), MXU_DTYPE), jax.ShapeDtypeStruct((seq, D_FF), MXU_DTYPE),
                   jax.ShapeDtypeStruct((seq, D_MODEL), MXU_DTYPE), jax.ShapeDtypeStruct((8, D_MODEL), F32)],
        in_specs=[tok(D_MODEL), tok(D_MODEL), tok(D_MODEL), _full((8, D_MODEL)), _any(), _any(), _any()],
        out_specs=[tok(D_MODEL), tok(D_MODEL), tok(D_MODEL), tok(D_MODEL), tok(D_FF), tok(D_FF), tok(D_MODEL),
                   _full((8, D_MODEL))],
        scratch_shapes=[pltpu.VMEM((D_MODEL, D_MODEL), MXU_DTYPE), pltpu.VMEM((nj, D_MODEL, D_MODEL), MXU_DTYPE),
                        pltpu.VMEM((nj, D_MODEL, D_MODEL), MXU_DTYPE), pltpu.VMEM((nj, tm, D_MODEL), F32),
                        pltpu.SemaphoreType.DMA((3,))],
        compiler_params=_params("arbitrary"),
    )(x, target, cat, vecs, w_out, w_ff1, w_ff2)


def _mixer_bwd_kernel(proj, rope_tab, dcat, w_spatial, w_spatial_t, bias_full, sink_rows, comm=None):
    seq = proj.shape[0]
    nb = seq // CHUNK
    kv_col = (2 * GMLP_WIDTH + ATTN_WIDTH) // (2 * KV_WIDTH)

    def body(proj_ref, prev_ref, tab_ref, ptab_ref, dcat_ref, w_ref, wt_ref, bias_ref, sink_ref,
             dproj_ref, dw_ref, db_ref, dsink_ref, carry):
        step = pl.program_id(0)
        blk = nb - 1 - step

        @pl.when(step == 0)
        def _():
            carry[...] = jnp.zeros_like(carry)
            dw_ref[...] = jnp.zeros_like(dw_ref)
            db_ref[...] = jnp.zeros_like(db_ref)
            dsink_ref[...] = jnp.zeros_like(dsink_ref)

        wm, tril, triu = _masked_spatial(w_ref)
        lo, hi = _lane_masks((CHUNK, LANES))
        lane = lax.broadcasted_iota(jnp.int32, (CHUNK, LANES), 1)
        db = jnp.zeros((CHUNK, LANES), F32)
        for j in range(GMLP_GROUPS // 2):
            cols = slice(LANES * j, LANES * (j + 1))
            vcols = slice(GMLP_WIDTH + LANES * j, GMLP_WIDTH + LANES * (j + 1))
            zu, zv = proj_ref[:, cols], proj_ref[:, vcols]
            u, tu = _gelu_tanh(zu)
            vp, tv = _gelu_tanh(zv)
            sv = _sgu_forward_pair(wm, vp, j) + bias_ref[:, cols]
            dout = dcat_ref[:, cols]
            du = dout * sv
            dsv = dout * u
            dsv_lo, dsv_hi = jnp.where(lo, dsv, 0.0), jnp.where(hi, dsv, 0.0)
            lhs_t = jnp.concatenate([jnp.where(triu, wt_ref[2 * j], 0.0),
                                     jnp.where(triu, wt_ref[2 * j + 1], 0.0)], axis=1)
            dv = _dot(lhs_t, jnp.concatenate([dsv_lo, dsv_hi], axis=0))
            dw_ref[2 * j] += jnp.where(tril, _dot_nt(dsv_lo, vp), 0.0)
            dw_ref[2 * j + 1] += jnp.where(tril, _dot_nt(dsv_hi, vp), 0.0)
            db = db + (jnp.where(lane == 2 * j, jnp.sum(dsv_lo, axis=1, keepdims=True), 0.0)
                       + jnp.where(lane == 2 * j + 1, jnp.sum(dsv_hi, axis=1, keepdims=True), 0.0))
            dproj_ref[:, cols] = (du * _gelu_tanh_grad(zu, tu)).astype(dproj_ref.dtype)
            dproj_ref[:, vcols] = (dv * _gelu_tanh_grad(zv, tv)).astype(dproj_ref.dtype)
        db_ref[...] += db
        o = 2 * GMLP_WIDTH
        tab = tab_ref[...]
        q_r = _rope_apply(proj_ref[:, o:o + ATTN_WIDTH], tab, 1.0)
        k_cur = _rope_apply(proj_ref[:, o + ATTN_WIDTH:o + ATTN_WIDTH + KV_WIDTH], tab, 1.0)
        k_prev = _rope_apply(prev_ref[:, 0:KV_WIDTH], ptab_ref[...], 1.0)
        k_a = jnp.concatenate([k_prev, k_cur], axis=0)
        v_a = jnp.concatenate([prev_ref[:, KV_WIDTH:2 * KV_WIDTH],
                               proj_ref[:, o + ATTN_WIDTH + KV_WIDTH:o + ATTN_WIDTH + 2 * KV_WIDTH]], axis=0)
        k_b = pltpu.roll(k_a, HEAD_DIM, 1)
        v_b = pltpu.roll(v_a, HEAD_DIM, 1)
        bias_t = _attn_bias_t(blk == 0)
        lo2, _ = _lane_masks((2 * CHUNK, LANES))
        dout_b = dcat_ref[:, GMLP_WIDTH:GMLP_WIDTH + ATTN_WIDTH]
        dk_tot, dv_tot, dq_pairs = [], [], []
        for g in range(N_KV_HEADS):
            k_dup, v_dup = _group_dup(k_a, k_b, g, lo2), _group_dup(v_a, v_b, g, lo2)
            q_rows = _group_rows(q_r, g, lo, hi)
            do_rows = _group_rows(dout_b, g, lo, hi)
            p_t, p_sink = _attn_probs_t(k_dup, q_rows, bias_t, _sink_row(sink_ref, g))
            dp_t = _dot_nt(v_dup, do_rows)
            delta = jnp.sum(p_t * dp_t, axis=0, keepdims=True)
            ds_t = p_t * (dp_t - delta) * ATTN_SCALE
            dsink = -p_sink * delta
            for r in range(HEADS_PER_GROUP):
                h = HEADS_PER_GROUP * g + r
                dsink_ref[h:h + 1, :] += jnp.broadcast_to(
                    jnp.sum(dsink[:, LANES * r:LANES * (r + 1)], axis=1, keepdims=True), (1, LANES))
            dk_full = _dot(ds_t, q_rows)
            dv_full = _dot(p_t, do_rows)
            dk_tot.append(dk_full + pltpu.roll(dk_full, HEAD_DIM, 1))
            dv_tot.append(dv_full + pltpu.roll(dv_full, HEAD_DIM, 1))
            dq_t = _dot(k_dup.T, ds_t)
            dq_pairs += _pairs_from_rows(dq_t.T, lo)
        dk_all = jnp.where(lo2, dk_tot[0], dk_tot[1])
        dv_all = jnp.where(lo2, dv_tot[0], dv_tot[1])
        dk_cur = dk_all[CHUNK:, :] + carry[:, 0:KV_WIDTH]
        dv_cur = dv_all[CHUNK:, :] + carry[:, KV_WIDTH:2 * KV_WIDTH]
        carry[:, 0:KV_WIDTH] = dk_all[:CHUNK, :]
        carry[:, KV_WIDTH:2 * KV_WIDTH] = dv_all[:CHUNK, :]
        dq = _rope_apply(jnp.concatenate(dq_pairs, axis=1), tab, -1.0)
        dproj_ref[:, o:o + ATTN_WIDTH] = dq.astype(dproj_ref.dtype)
        dproj_ref[:, o + ATTN_WIDTH:o + ATTN_WIDTH + KV_WIDTH] = (
            _rope_apply(dk_cur, tab, -1.0).astype(dproj_ref.dtype))
        dproj_ref[:, o + ATTN_WIDTH + KV_WIDTH:o + ATTN_WIDTH + 2 * KV_WIDTH] = dv_cur.astype(dproj_ref.dtype)

    rev = lambda i: nb - 1 - i
    return _hosted_call(
        body, comm, name="mixer_bwd", grid=(nb,),
        out_shape=[jax.ShapeDtypeStruct((seq, IN_PROJ_WIDTH), MXU_DTYPE),
                   jax.ShapeDtypeStruct((GMLP_GROUPS, CHUNK, CHUNK), F32),
                   jax.ShapeDtypeStruct((CHUNK, LANES), F32),
                   jax.ShapeDtypeStruct((N_Q_HEADS, LANES), F32)],
        in_specs=[pl.BlockSpec((CHUNK, IN_PROJ_WIDTH), lambda i: (rev(i), 0)),
                  pl.BlockSpec((CHUNK, 2 * KV_WIDTH), lambda i: (jnp.maximum(rev(i) - 1, 0), kv_col)),
                  pl.BlockSpec((CHUNK, 3 * LANES), lambda i: (rev(i), 0)),
                  pl.BlockSpec((CHUNK, 3 * LANES), lambda i: (jnp.maximum(rev(i) - 1, 0), 0)),
                  pl.BlockSpec((CHUNK, D_MODEL), lambda i: (rev(i), 0)),
                  _full((GMLP_GROUPS, CHUNK, CHUNK)), _full((GMLP_GROUPS, CHUNK, CHUNK)),
                  _full((CHUNK, GMLP_WIDTH)), _full((N_Q_HEADS, LANES))],
        out_specs=[pl.BlockSpec((CHUNK, IN_PROJ_WIDTH), lambda i: (rev(i), 0)),
                   _full((GMLP_GROUPS, CHUNK, CHUNK)), _full((CHUNK, LANES)), _full((N_Q_HEADS, LANES))],
        scratch_shapes=[pltpu.VMEM((CHUNK, 2 * KV_WIDTH), F32)],
        semantics=("arbitrary",),
    )(proj, proj, rope_tab, rope_tab, dcat, w_spatial, w_spatial_t, bias_full, sink_rows)


def _in_proj_bwd_kernel(x, dx1, dproj, vecs, w_in_t, comm=None):
    seq = x.shape[0]
    tm = 512

    def body(x_ref, dx1_ref, dp_ref, v_ref, w_ref, gx_ref, sums_ref):
        @pl.when(pl.program_id(0) == 0)
        def _():
            sums_ref[...] = jnp.zeros_like(sums_ref)

        g_mix, scale1 = v_ref[0:1, :], v_ref[2:3, :]
        dh = _dot(dp_ref[...], w_ref[...])
        xv = x_ref[...]
        rstd = lax.rsqrt(_mean_last(xv * xv) + EPS)
        xh = xv * rstd
        dn1 = dh * (1.0 + scale1)
        dxh = dn1 * g_mix
        gx_ref[...] = dx1_ref[...] + rstd * (dxh - xh * _mean_last(dxh * xh))
        sums_ref[0:1, :] += _rowsum(dh)
        sums_ref[1:2, :] += _rowsum(dh * (xh * g_mix))
        sums_ref[2:3, :] += _rowsum(dn1 * xh)

    return _hosted_call(
        body, comm, name="in_proj_bwd", grid=(seq // tm,),
        out_shape=[jax.ShapeDtypeStruct((seq, D_MODEL), F32), jax.ShapeDtypeStruct((8, D_MODEL), F32)],
        in_specs=[pl.BlockSpec((tm, D_MODEL), lambda i: (i, 0)), pl.BlockSpec((tm, D_MODEL), lambda i: (i, 0)),
                  pl.BlockSpec((tm, IN_PROJ_WIDTH), lambda i: (i, 0)), _full((8, D_MODEL)),
                  _full((IN_PROJ_WIDTH, D_MODEL))],
        out_specs=[pl.BlockSpec((tm, D_MODEL), lambda i: (i, 0)), _full((8, D_MODEL))],
        semantics=("arbitrary",),
    )(x, dx1, dproj, vecs, w_in_t)


class _GradTiles(NamedTuple):
    tm: int
    tn: int
    n_tiles: int
    chips_per_tile: int
    a_index: Callable
    b_index: Callable


def _weight_grad_kernel(a, b, c_idx, name, tiles, comm=None):
    seq = a.shape[0]
    tk = min(seq, 4096)
    nk = seq // tk
    tm, tn, n_tiles, per = tiles.tm, tiles.tn, tiles.n_tiles, tiles.chips_per_tile
    rows = tm // per

    def half(phase, c):
        return phase * c[0] + (1 - phase) * (1 - c[0])

    def body(c_ref, a_ref, b_ref, o_ref, acc, stage, landed, send_sems, recv_sems):
        del c_ref
        phase, t, kk = pl.program_id(0), pl.program_id(1), pl.program_id(2)
        x, y, c, _ = _mesh_place()

        def copy(tile):
            return pltpu.make_async_remote_copy(
                src_ref=stage.at[tile], dst_ref=landed.at[tile], send_sem=send_sems.at[tile],
                recv_sem=recv_sems.at[tile], device_id=(x, y, 1 - c), device_id_type=MESH)

        @pl.when(kk == 0)
        def _():
            acc[...] = jnp.zeros_like(acc)

        acc[...] += _dot_tn(a_ref[...], b_ref[...])

        @pl.when((kk == nk - 1) & (phase == 0))
        def _():
            stage[t] = acc[...].astype(stage.dtype)
            copy(t).start()

        @pl.when((kk == nk - 1) & (phase == 1))
        def _():
            copy(t).wait_recv()
            total = acc[...] + landed[t].astype(F32)
            for q in range(per):
                o_ref[q] = total[rows * q:rows * (q + 1)].astype(o_ref.dtype)

        @pl.when((kk == nk - 1) & (phase == 1) & (t == n_tiles - 1))
        def _():
            for tile in range(n_tiles):
                copy(tile).wait_send()

    out = _hosted_call(
        body, comm, name=name, grid=(2, n_tiles, nk), n_prefetch=1,
        out_shape=[jax.ShapeDtypeStruct((n_tiles * per, rows, tn), GRAD_COMM_DTYPE)],
        in_specs=[pl.BlockSpec((tk, tm), lambda p, t, k, c: (k, tiles.a_index(t, half(p, c)))),
                  pl.BlockSpec((tk, tn), lambda p, t, k, c: (k, tiles.b_index(t, half(p, c))))],
        out_specs=[pl.BlockSpec((per, rows, tn), lambda p, t, k, c: (p * t, 0, 0))],
        scratch_shapes=[pltpu.VMEM((tm, tn), F32), pltpu.VMEM((n_tiles, tm, tn), GRAD_COMM_DTYPE),
                        pltpu.VMEM((n_tiles, tm, tn), GRAD_COMM_DTYPE),
                        pltpu.SemaphoreType.DMA((n_tiles,)), pltpu.SemaphoreType.DMA((n_tiles,))],
        semantics=("arbitrary", "arbitrary", "arbitrary"),
    )(c_idx, a, b)
    return out[0] if comm is None else out


def _row_tile(rows, most=256, sublanes=16):
    return max(t for t in range(sublanes, most + 1, sublanes) if rows % t == 0)


def _adam_update(w, g, m, v):
    m_new = ADAM_B1 * m + (1.0 - ADAM_B1) * g
    v_new = ADAM_B2 * v + (1.0 - ADAM_B2) * (g * g)
    m_hat = m_new / (1.0 - ADAM_B1 ** ADAM_STEP)
    v_hat = v_new / (1.0 - ADAM_B2 ** ADAM_STEP)
    delta = -ADAM_LR * (m_hat / (jnp.sqrt(v_hat) + ADAM_EPS) + ADAM_WD * w)
    return delta, m_new, v_new


def _sum_chips_kernel(own, others, place, name):
    _, r, n = own.shape
    tr = _row_tile(r)

    def body(place_ref, own_ref, oth_ref, o_ref):
        del place_ref
        acc = own_ref[...].astype(F32)
        for k in range(N_CHIPS - 1):
            acc = acc + oth_ref[k].astype(F32)
        o_ref[...] = acc

    return pl.pallas_call(
        body, name=name, out_shape=jax.ShapeDtypeStruct((2, r, n), F32),
        grid_spec=pltpu.PrefetchScalarGridSpec(
            num_scalar_prefetch=1, grid=(r // tr,),
            in_specs=[pl.BlockSpec((None, tr, n), lambda i, p: (p[0], i, 0)),
                      pl.BlockSpec((N_CHIPS - 1, tr, n), lambda i, p: (0, i, 0))],
            out_specs=pl.BlockSpec((None, tr, n), lambda i, p: (p[1], i, 0))),
        compiler_params=_params("parallel"),
    )(place, own, others)


def _adam_kernel(w, g, m, v, name):
    r, n = w.shape
    by_columns = g.shape[1] == r
    tr, tn = _row_tile(g.shape[1]), g.shape[2]

    def body(w_ref, g_ref, m_ref, v_ref, g_out, d_ref, mo_ref, vo_ref):
        gv = g_ref[...]
        g_out[...] = gv
        d_ref[...], mo_ref[...], vo_ref[...] = _adam_update(w_ref[...], gv, m_ref[...], v_ref[...])

    steps = g.shape[1] // tr
    spec = pl.BlockSpec((tr, tn), (lambda h, i: (i, h)) if by_columns else (lambda h, i: (h * steps + i, 0)))
    return pl.pallas_call(
        body, name=name, grid=(2, steps), out_shape=[jax.ShapeDtypeStruct((r, n), F32)] * 4,
        in_specs=[spec, pl.BlockSpec((None, tr, tn), lambda h, i: (h, i, 0)), spec, spec], out_specs=[spec] * 4,
        compiler_params=_params("parallel", "parallel"),
    )(w, g, m, v)


SMALL_PARAMS = ("b_ada", "g_mix", "g_ffn", "g_final", "b_spatial", "sinks", "w_spatial")


def _small_update_kernel(gathered, params):
    shapes = [params[nm][0].shape for nm in SMALL_PARAMS]

    def body(*refs):
        g_refs, refs = refs[:5], refs[5:]
        p_refs, refs = refs[:3 * len(SMALL_PARAMS)], refs[3 * len(SMALL_PARAMS):]
        loss_ref, o_refs = refs[0], refs[1:]

        def total(ref):
            acc = ref[0]
            for k in range(1, N_DEV):
                acc = acc + ref[k]
            return acc

        s1, s2, db, ds, dw = (total(r) for r in g_refs)
        loss_ref[...] = jnp.broadcast_to(s2[6:7, 0:1], loss_ref.shape)
        grads = {"b_ada": [s1[0:1], s1[1:2], s2[5:6], s2[0:1], s2[1:2], s2[2:3]], "g_mix": [s1[2:3]],
                 "g_ffn": [s2[3:4]], "g_final": [s2[4:5]], "b_spatial": [db.T[0:GMLP_GROUPS]],
                 "w_spatial": [dw]}
        lane = lax.broadcasted_iota(jnp.int32, (1, LANES), 1)
        sink_row = jnp.zeros((1, LANES), F32)
        for h in range(N_Q_HEADS):
            sink_row = sink_row + jnp.where(lane == h, ds[h:h + 1, :], 0.0)
        grads["sinks"] = [sink_row[:, 0:N_Q_HEADS]]
        for i, nm in enumerate(SMALL_PARAMS):
            w_ref, m_ref, v_ref = p_refs[3 * i:3 * i + 3]
            outs = o_refs[4 * i:4 * i + 4]
            width = grads[nm][0].shape[1]
            for k, g in enumerate(grads[nm]):
                cols = slice(width * k, width * (k + 1))
                upd = _adam_update(w_ref[:, cols], g, m_ref[:, cols], v_ref[:, cols])
                for o_ref, val in zip(outs, (g,) + upd):
                    o_ref[:, cols] = val

    flat = [a for nm in SMALL_PARAMS for a in params[nm]]
    out_shape = [jax.ShapeDtypeStruct((8, LANES), F32)]
    out_shape += [jax.ShapeDtypeStruct(s, F32) for s in shapes for _ in range(4)]
    outs = pl.pallas_call(
        body, name="small_update", grid=(1,), out_shape=out_shape,
        in_specs=[_full(g.shape) for g in gathered] + [_full(a.shape) for a in flat],
        out_specs=[_full(s.shape) for s in out_shape],
        compiler_params=_params("arbitrary"),
    )(*gathered, *flat)
    return {nm: outs[1 + 4 * i:5 + 4 * i] for i, nm in enumerate(SMALL_PARAMS)}, outs[0]


def _ada_update_kernel(act_t, dmod, w, m, v):
    r, n = w.shape
    tr = 256

    def body(a_ref, d_ref, w_ref, m_ref, v_ref, g_ref, dl_ref, mo_ref, vo_ref):
        g = _dot(a_ref[...], d_ref[...])
        g_ref[...] = g
        dl_ref[...], mo_ref[...], vo_ref[...] = _adam_update(w_ref[...], g, m_ref[...], v_ref[...])

    spec = pl.BlockSpec((tr, n), lambda i: (i, 0))
    return pl.pallas_call(
        body, name="ada_update", grid=(r // tr,), out_shape=[jax.ShapeDtypeStruct((r, n), F32)] * 4,
        in_specs=[pl.BlockSpec((tr, N_DEV), lambda i: (i, 0)), _full((N_DEV, n)), spec, spec, spec],
        out_specs=[spec] * 4, compiler_params=_params("parallel"),
    )(act_t, dmod, w, m, v)


def kernel(x, c, positions, w_ada, b_ada, g_mix, w_in, w_spatial, b_spatial, sinks, w_out, g_ffn, w_ff1, w_ff2, g_final, loss_target, m_w_ada, m_b_ada, m_g_mix, m_w_in, m_w_spatial, m_b_spatial, m_sinks, m_w_out, m_g_ffn, m_w_ff1, m_w_ff2, m_g_final, v_w_ada, v_b_ada, v_g_mix, v_w_in, v_w_spatial, v_b_spatial, v_sinks, v_w_out, v_g_ffn, v_w_ff1, v_w_ff2, v_g_final):
    xi, yi, ci = lax.axis_index("x"), lax.axis_index("y"), lax.axis_index("c")
    chip = 2 * xi + yi
    dev = 2 * chip + ci
    seq = x.shape[1]
    x2, tgt = x[0], loss_target[0]
    ada_cols = w_ada.shape[2]

    c_all = _all_gather8([c], "gather_c")[0].reshape(N_DEV, D_MODEL)
    b_shard = lax.dynamic_slice(b_ada, (0, chip * ada_cols), (1, ada_cols))
    mod_part, act = _mod_kernel(c_all, w_ada[0], b_shard)
    mod_all, = _all_gather8([mod_part], "gather_mod")
    mod_me = lax.dynamic_index_in_dim(mod_all[0::2], dev, axis=1, keepdims=False)
    mod_me = mod_me.reshape(N_MOD, D_MODEL)
    shift1, scale1, gate1, shift2, scale2, gate2 = (mod_me[k:k + 1] for k in range(N_MOD))

    big = {"w_in": tuple(a[0].T for a in (w_in, m_w_in, v_w_in)),
           "w_out": (w_out[0], m_w_out[0], v_w_out[0]), "w_ff1": (w_ff1[0], m_w_ff1[0], v_w_ff1[0]),
           "w_ff2": (w_ff2[0], m_w_ff2[0], v_w_ff2[0])}

    def halves(nm):
        r, n = big[nm][0].shape
        return big[nm][0].astype(WEIGHT_COMM_DTYPE).reshape(2, r // 2, n)

    w_in_t, = _all_gather8([halves("w_in")], "gather_w_in", split=True)
    w_in_t = w_in_t.reshape(IN_PROJ_WIDTH, D_MODEL)

    zeros_row = jnp.zeros((1, D_MODEL), F32)
    vecs1 = jnp.concatenate([g_mix, shift1, scale1] + [zeros_row] * 5, axis=0)
    vecs2 = jnp.concatenate([gate1, shift2, scale2, gate2, g_ffn, g_final.reshape(1, D_MODEL)]
                            + [zeros_row] * 2, axis=0)
    bias_full = jnp.repeat(b_spatial[0].T, HEAD_DIM, axis=1)
    sink_rows = jnp.broadcast_to(sinks[0][:, None], (N_Q_HEADS, LANES))
    inv_freq = ROPE_THETA ** (-jnp.arange(0, ROT_DIM, 2, dtype=F32) / ROT_DIM)
    rope_tab = _rope_lane_tables(*_rope_angle_kernel(positions, inv_freq.reshape(ROT_DIM // 2, 1)))

    proj, hb, g_ff1 = _in_proj_kernel(x2, vecs1, w_in_t, comm=_gather_job([halves("w_ff1")]))
    cat, g_ff2, g_out = _mixer_fwd_kernel(proj, rope_tab, w_spatial[0], bias_full, sink_rows,
                                          comm=_gather_job([halves("w_ff2"), halves("w_out")]))
    g_ff1, g_ff2, g_out = _gather_forward([g_ff1, g_ff2, g_out], "gather_forward")
    w_out_full = g_out.reshape(D_MODEL, D_MODEL)
    w_ff1_blocks = g_ff1.reshape(N_CHIPS, D_MODEL, D_MODEL)
    w_ff2_blocks = g_ff2.reshape(N_CHIPS, D_MODEL, D_MODEL)
    dx1, dcat, dmix, h2b, rb, dab, dffb, sums2 = _trunk_kernel(
        x2, tgt, cat, vecs2, w_out_full, w_ff1_blocks, w_ff2_blocks)

    c_idx = ci.reshape(1).astype(jnp.int32)
    place = jnp.stack([chip, ci]).astype(jnp.int32)
    half_d = D_MODEL // 2
    cs_ff2 = _weight_grad_kernel(rb, dffb, c_idx, "dw_ff2",
                                 _GradTiles(D_MODEL, half_d, N_CHIPS, 1, lambda t, h: t, lambda t, h: h))
    cs_ff1 = _weight_grad_kernel(h2b, dab, c_idx, "dw_ff1",
                                 _GradTiles(D_MODEL, half_d, N_CHIPS, 1, lambda t, h: 0, lambda t, h: 2 * t + h))
    cs_out = _weight_grad_kernel(cat, dmix, c_idx, "dw_out",
                                 _GradTiles(D_MODEL, half_d, 1, N_CHIPS, lambda t, h: 0, lambda t, h: h))
    dproj, dw_spatial, db_lanes, dsink_rows, sc_ff2, sc_out = _mixer_bwd_kernel(
        proj, rope_tab, dcat, w_spatial[0], w_spatial[0].transpose(0, 2, 1), bias_full, sink_rows,
        comm=_scatter_job([cs_ff2, cs_out]))
    cs_in, sc_ff1 = _weight_grad_kernel(
        dproj, hb, c_idx, "dw_in",
        _GradTiles(2 * W_IN_BLOCK, half_d, N_CHIPS // 2, 2, lambda t, h: t, lambda t, h: h),
        comm=_scatter_job([cs_ff1]))
    grad_x, sums1 = _in_proj_bwd_kernel(x2, dx1, dproj, vecs1, w_in_t)
    sc_in, = _run_comm(_scatter_job([cs_in]), "grad_to_chips_w_in")

    names = ["w_in", "w_out", "w_ff1", "w_ff2"]
    totals = [_sum_chips_kernel(own, oth, place, "grad_sum_" + nm)
              for nm, own, oth in zip(names, [cs_in, cs_out, cs_ff1, cs_ff2], [sc_in, sc_out, sc_ff1, sc_ff2])]
    shared = _sibling_share(totals, "grad_share")
    big_out = {}
    for nm, g in zip(names, shared):
        w, m, v = big[nm]
        outs = _adam_kernel(w, g, m, v, "adam_" + nm)
        big_out[nm] = tuple((t.T if nm == "w_in" else t)[None] for t in outs)

    small = {"b_ada": (b_ada, m_b_ada, v_b_ada), "g_mix": (g_mix, m_g_mix, v_g_mix),
             "g_ffn": (g_ffn, m_g_ffn, v_g_ffn), "g_final": (g_final, m_g_final, v_g_final),
             "b_spatial": (b_spatial, m_b_spatial, v_b_spatial), "sinks": (sinks, m_sinks, v_sinks),
             "w_spatial": (w_spatial, m_w_spatial, v_w_spatial)}
    flat_shape = {"g_final": (1, D_MODEL), "b_spatial": (GMLP_GROUPS, CHUNK), "w_spatial": (GMLP_GROUPS * CHUNK, CHUNK)}
    gathered = _all_gather8([sums1, sums2, db_lanes, dsink_rows, dw_spatial.reshape(GMLP_GROUPS * CHUNK, CHUNK)],
                            "gather_small")
    small_out, loss_tile = _small_update_kernel(
        gathered, {nm: tuple(a.reshape(flat_shape.get(nm, a.shape)) for a in small[nm]) for nm in small})
    small_out = {nm: [o.reshape(small[nm][0].shape) for o in small_out[nm]] for nm in small}
    loss = loss_tile[0, 0]

    g1, g2 = gathered[0], gathered[1]
    dmod_all = jnp.concatenate([g1[:, 0], g1[:, 1], g2[:, 5], g2[:, 0], g2[:, 1], g2[:, 2]], axis=1)
    dmod_cols = lax.dynamic_slice(dmod_all, (0, chip * ada_cols), (N_DEV, ada_cols))
    ada = _ada_update_kernel(act.T, dmod_cols, w_ada[0], m_w_ada[0], v_w_ada[0])
    big_out["w_ada"] = tuple(t[None] for t in ada)

    order = ["w_ada", "b_ada", "g_mix", "w_in", "w_spatial", "b_spatial", "sinks", "w_out", "g_ffn",
             "w_ff1", "w_ff2", "g_final"]

    def leaf(nm, k):
        return big_out[nm][k] if nm in big_out else small_out[nm][k]

    outs = [loss, grad_x[None]]
    for k in range(4):
        outs += [leaf(nm, k) for nm in order]
    return tuple(outs)
```

```python
import math
from typing import Callable, NamedTuple

import jax
import jax.numpy as jnp
from jax import lax
from jax.experimental import pallas as pl
from jax.experimental.pallas import tpu as pltpu

F32 = jnp.float32
MXU_DTYPE = jnp.bfloat16
WEIGHT_COMM_DTYPE = jnp.bfloat16
GRAD_COMM_DTYPE = jnp.bfloat16

D_MODEL = 1024
D_FF = 4096
HEAD_DIM = 64
GMLP_GROUPS = 8
GMLP_WIDTH = 512
CHUNK = 128
N_Q_HEADS = 8
N_KV_HEADS = 2
ATTN_WIDTH = 512
KV_WIDTH = 128
ROT_DIM = 16
ROPE_THETA = 500000.0
IN_PROJ_WIDTH = 1792
N_MOD = 6
EPS = 1e-5
N_CHIPS = 4
N_DEV = 8
LANES = 128
W_IN_BLOCK = IN_PROJ_WIDTH // N_CHIPS

ADAM_LR = 0.001
ADAM_B1 = 0.9
ADAM_B2 = 0.999
ADAM_EPS = 1e-08
ADAM_WD = 0.01
ADAM_STEP = 10

VMEM_LIMIT_BYTES = 58 * 1024 * 1024
MESH = pl.DeviceIdType.MESH


def _params(*semantics):
    return pltpu.CompilerParams(dimension_semantics=semantics, vmem_limit_bytes=VMEM_LIMIT_BYTES)


def _dot(a, b):
    return jnp.dot(a.astype(MXU_DTYPE), b.astype(MXU_DTYPE), preferred_element_type=F32)


def _dot_nt(a, b):
    return lax.dot_general(a.astype(MXU_DTYPE), b.astype(MXU_DTYPE), (((1,), (1,)), ((), ())),
                           preferred_element_type=F32)


def _dot_tn(a, b):
    return lax.dot_general(a.astype(MXU_DTYPE), b.astype(MXU_DTYPE), (((0,), (0,)), ((), ())),
                           preferred_element_type=F32)


def _full(shape):
    return pl.BlockSpec(shape, lambda *_: (0,) * len(shape))


def _any():
    return pl.BlockSpec(memory_space=pl.ANY)


def _rowsum(v):
    return jnp.sum(v, axis=0, keepdims=True)


def _mean_last(v):
    return jnp.mean(v, axis=-1, keepdims=True)


class _Comm(NamedTuple):
    operands: tuple
    out_shapes: tuple
    n_sems: int
    make: Callable


def _hosted_call(body, comm, *, name, grid, in_specs, out_shape, out_specs, scratch_shapes=(), semantics,
                 n_prefetch=0):
    if comm is None:
        return pl.pallas_call(
            body, name=name, out_shape=out_shape, compiler_params=_params(*semantics),
            grid_spec=pltpu.PrefetchScalarGridSpec(
                num_scalar_prefetch=n_prefetch, grid=grid, in_specs=in_specs, out_specs=out_specs,
                scratch_shapes=list(scratch_shapes)))
    n_in, n_out, n_scr = len(in_specs), len(out_shape), len(scratch_shapes)
    k_in, k_out = len(comm.operands), len(comm.out_shapes)

    def hosted(*refs):
        prefetched, refs = refs[:n_prefetch], refs[n_prefetch:]
        ins, refs = refs[:n_in], refs[n_in:]
        c_ins, refs = refs[:k_in], refs[k_in:]
        outs, refs = refs[:n_out], refs[n_out:]
        c_outs, refs = refs[:k_out], refs[k_out:]
        scratch, (send_sems, recv_sems) = refs[:n_scr], refs[n_scr:]
        first, last = None, None
        for d, size in enumerate(grid):
            at_start, at_end = pl.program_id(d) == 0, pl.program_id(d) == size - 1
            first = at_start if first is None else first & at_start
            last = at_end if last is None else last & at_end

        @pl.when(first)
        def _():
            for cp in comm.make(c_ins, c_outs, send_sems, recv_sems)[0]:
                cp.start()

        body(*prefetched, *ins, *outs, *scratch)

        @pl.when(last)
        def _():
            for wait in comm.make(c_ins, c_outs, send_sems, recv_sems)[1]:
                wait()

    call = pl.pallas_call(
        hosted, name=name, out_shape=list(out_shape) + list(comm.out_shapes),
        compiler_params=_params(*semantics),
        grid_spec=pltpu.PrefetchScalarGridSpec(
            num_scalar_prefetch=n_prefetch, grid=grid, in_specs=list(in_specs) + [_any()] * k_in,
            out_specs=list(out_specs) + [_any()] * k_out,
            scratch_shapes=list(scratch_shapes) + [pltpu.SemaphoreType.DMA((comm.n_sems,)),
                                                    pltpu.SemaphoreType.DMA((comm.n_sems,))]))
    return lambda *args: call(*args, *comm.operands)


def _mesh_place():
    x, y, c = lax.axis_index("x"), lax.axis_index("y"), lax.axis_index("c")
    return x, y, c, [(1 - x, y), (x, 1 - y), (1 - x, 1 - y)]


def _gather_job(blocks, split=True):
    per = 5

    def make(ins, outs, send_sems, recv_sems):
        x, y, c, chips = _mesh_place()
        starts, waits = [], []
        for a, (src, out) in enumerate(zip(ins, outs)):
            src = src.at[c] if split else src
            mine = out.at[4 * x + 2 * y + c]
            local = pltpu.make_async_copy(src, mine, send_sems.at[per * a + 4])
            to = [(x, y, 1 - c)] + [(px, py, c) for px, py in chips]
            sends = [pltpu.make_async_remote_copy(
                src_ref=src, dst_ref=mine, send_sem=send_sems.at[per * a + k],
                recv_sem=recv_sems.at[per * a + k], device_id=dev, device_id_type=MESH)
                for k, dev in enumerate(to)]
            recvs = [pltpu.make_async_remote_copy(
                src_ref=src, dst_ref=out.at[4 * px + 2 * py + pc], send_sem=send_sems.at[per * a + k],
                recv_sem=recv_sems.at[per * a + k], device_id=(px, py, pc), device_id_type=MESH)
                for k, (px, py, pc) in enumerate(to)]
            starts += [local] + sends
            waits += [local.wait] + [s.wait_send for s in sends] + [r.wait_recv for r in recvs]
        return starts, waits

    shapes = tuple(jax.ShapeDtypeStruct((N_DEV,) + tuple(b.shape[1:] if split else b.shape), b.dtype)
                   for b in blocks)
    return _Comm(tuple(blocks), shapes, per * len(blocks), make)


def _gather_forward(bufs, name):
    n_arr = len(bufs)

    def body(*refs):
        outs = refs[n_arr:2 * n_arr]
        send_sems, recv_sems = refs[2 * n_arr:]
        x, y, c, chips = _mesh_place()
        sends, recvs = [], []
        for a, buf in enumerate(outs):
            for j, (px, py) in enumerate(chips):
                mine, theirs = buf.at[4 * px + 2 * py + c], buf.at[4 * px + 2 * py + 1 - c]
                sems = dict(send_sem=send_sems.at[3 * a + j], recv_sem=recv_sems.at[3 * a + j],
                            device_id=(x, y, 1 - c), device_id_type=MESH)
                sends.append(pltpu.make_async_remote_copy(src_ref=mine, dst_ref=mine, **sems))
                recvs.append(pltpu.make_async_remote_copy(src_ref=mine, dst_ref=theirs, **sems))
        for cp in sends:
            cp.start()
        for s, r in zip(sends, recvs):
            s.wait_send()
            r.wait_recv()

    return pl.pallas_call(
        body, name=name, out_shape=[jax.ShapeDtypeStruct(b.shape, b.dtype) for b in bufs],
        in_specs=[_any()] * n_arr, out_specs=[_any()] * n_arr,
        input_output_aliases={a: a for a in range(n_arr)},
        scratch_shapes=[pltpu.SemaphoreType.DMA((3 * n_arr,)), pltpu.SemaphoreType.DMA((3 * n_arr,))],
    )(*bufs)


def _scatter_job(chip_sums):
    def make(ins, outs, send_sems, recv_sems):
        x, y, c, chips = _mesh_place()
        copies = [pltpu.make_async_remote_copy(
            src_ref=src.at[2 * px + py], dst_ref=out.at[j], send_sem=send_sems.at[3 * a + j],
            recv_sem=recv_sems.at[3 * a + j], device_id=(px, py, c), device_id_type=MESH)
            for a, (src, out) in enumerate(zip(ins, outs)) for j, (px, py) in enumerate(chips)]
        return copies, [cp.wait for cp in copies]

    return _Comm(tuple(chip_sums), tuple(jax.ShapeDtypeStruct((3,) + s.shape[1:], s.dtype) for s in chip_sums),
                 3 * len(chip_sums), make)


def _run_comm(comm, name):
    def body(token_ref):
        token_ref[...] = jnp.zeros_like(token_ref)

    out = _hosted_call(body, comm, name=name, grid=(1,), in_specs=[],
                       out_shape=[jax.ShapeDtypeStruct((8, LANES), F32)], out_specs=[_full((8, LANES))],
                       semantics=("arbitrary",))()
    return out[1:]


def _all_gather8(blocks, name, split=False, forward=()):
    n_arr, n_fwd = len(blocks), len(forward)

    def body(*refs):
        x_refs, refs = refs[:n_arr], refs[n_arr + n_fwd:]
        out_refs, refs = refs[:n_arr], refs[n_arr:]
        fwd_refs, (send_sems, recv_sems, local_sems) = refs[:n_fwd], refs[n_fwd:]
        x, y, c, chips = _mesh_place()
        me, sibling = (x, y, c), (x, y, 1 - c)
        passing = []
        for f, buf in enumerate(fwd_refs):
            for j, (px, py) in enumerate(chips):
                mine, theirs = buf.at[4 * px + 2 * py + c], buf.at[4 * px + 2 * py + 1 - c]
                sems = dict(send_sem=send_sems.at[7 * n_arr + 3 * f + j], recv_sem=recv_sems.at[7 * n_arr + 3 * f + j],
                            device_id=sibling, device_id_type=MESH)
                passing.append((pltpu.make_async_remote_copy(src_ref=mine, dst_ref=mine, **sems),
                                pltpu.make_async_remote_copy(src_ref=mine, dst_ref=theirs, **sems)))
        for send, _ in passing:
            send.start()
        arrays = []
        for a, (x_ref, out_ref) in enumerate(zip(x_refs, out_refs)):
            src_mine = x_ref.at[c] if split else x_ref

            def copy(k, blk, to, src=None, a=a, out_ref=out_ref):
                dst = out_ref.at[4 * blk[0] + 2 * blk[1] + blk[2]]
                return pltpu.make_async_remote_copy(
                    src_ref=dst if src is None else src, dst_ref=dst,
                    send_sem=send_sems.at[7 * a + k], recv_sem=recv_sems.at[7 * a + k],
                    device_id=to, device_id_type=MESH)

            mine = pltpu.make_async_copy(src_mine, out_ref.at[4 * x + 2 * y + c], local_sems.at[a])
            mine.start()
            first = [copy(0, me, sibling, src=src_mine)]
            first += [copy(1 + j, me, (*chip, c), src=src_mine) for j, chip in enumerate(chips)]
            for cp in first:
                cp.start()
            arrays.append((copy, mine, first))
        sent = []
        for copy, mine, first in arrays:
            passed = [copy(4 + j, (*chip, c), sibling) for j, chip in enumerate(chips)]
            for j, chip in enumerate(chips):
                copy(1 + j, (*chip, c), me).wait_recv()
                passed[j].start()
            sent += first + passed
        for copy, mine, first in arrays:
            copy(0, sibling, me).wait_recv()
            for j, chip in enumerate(chips):
                copy(4 + j, (*chip, 1 - c), me).wait_recv()
            mine.wait()
        for cp in sent:
            cp.wait_send()
        for send, arrival in passing:
            send.wait_send()
            arrival.wait_recv()

    n_sems = 7 * n_arr + 3 * n_fwd
    return pl.pallas_call(
        body, name=name,
        out_shape=[jax.ShapeDtypeStruct((N_DEV,) + tuple(b.shape[1:] if split else b.shape), b.dtype)
                   for b in blocks] + [jax.ShapeDtypeStruct(f.shape, f.dtype) for f in forward],
        in_specs=[_any()] * (n_arr + n_fwd), out_specs=[_any()] * (n_arr + n_fwd),
        input_output_aliases={n_arr + f: n_arr + f for f in range(n_fwd)},
        scratch_shapes=[pltpu.SemaphoreType.DMA((n_sems,)), pltpu.SemaphoreType.DMA((n_sems,)),
                        pltpu.SemaphoreType.DMA((n_arr,))],
    )(*blocks, *forward)


def _sibling_share(bufs, name):
    n_arr = len(bufs)

    def body(*refs):
        out_refs = refs[n_arr:2 * n_arr]
        send_sems, recv_sems = refs[2 * n_arr:]
        x, y, c = lax.axis_index("x"), lax.axis_index("y"), lax.axis_index("c")
        copies = [pltpu.make_async_remote_copy(
            src_ref=out_refs[a].at[c], dst_ref=out_refs[a].at[c],
            send_sem=send_sems.at[a], recv_sem=recv_sems.at[a],
            device_id=(x, y, 1 - c), device_id_type=MESH) for a in range(n_arr)]
        for cp in copies:
            cp.start()
        for a in range(n_arr):
            pltpu.make_async_remote_copy(
                src_ref=out_refs[a].at[c], dst_ref=out_refs[a].at[1 - c],
                send_sem=send_sems.at[a], recv_sem=recv_sems.at[a],
                device_id=(x, y, 1 - c), device_id_type=MESH).wait()

    return pl.pallas_call(
        body, name=name,
        out_shape=[jax.ShapeDtypeStruct(b.shape, b.dtype) for b in bufs],
        in_specs=[_any()] * n_arr, out_specs=[_any()] * n_arr,
        input_output_aliases={a: a for a in range(n_arr)},
        scratch_shapes=[pltpu.SemaphoreType.DMA((n_arr,)), pltpu.SemaphoreType.DMA((n_arr,))],
    )(*bufs)


def _gelu_tanh(z):
    k = math.sqrt(2.0 / math.pi)
    t = jnp.tanh(k * (z + 0.044715 * (z * z * z)))
    return 0.5 * z * (1.0 + t), t


def _gelu_tanh_grad(z, t):
    k = math.sqrt(2.0 / math.pi)
    return 0.5 * (1.0 + t) + 0.5 * z * (1.0 - t * t) * (k * (1.0 + 3.0 * 0.044715 * (z * z)))


def _rope_angle_kernel(pos_row, invf_col):
    seq = pos_row.shape[1]

    def body(p_ref, f_ref, cos_ref, sin_ref):
        ang = p_ref[...].astype(F32) * f_ref[...]
        cos_ref[...] = jnp.cos(ang)
        sin_ref[...] = jnp.sin(ang)

    return pl.pallas_call(
        body, name="rope_angles", grid=(1,), out_shape=[jax.ShapeDtypeStruct((ROT_DIM // 2, seq), F32)] * 2,
        in_specs=[_full((1, seq)), _full((ROT_DIM // 2, 1))], out_specs=[_full((ROT_DIM // 2, seq))] * 2,
        compiler_params=_params("arbitrary"),
    )(pos_row, invf_col)


def _rope_lane_tables(cos, sin):
    cos_t, sin_t = cos.T, sin.T
    seq, half = cos_t.shape
    ones = jnp.ones((seq, HEAD_DIM - ROT_DIM), F32)
    c64 = jnp.concatenate([cos_t, cos_t, ones], axis=1)
    s1 = jnp.concatenate([sin_t, jnp.zeros((seq, HEAD_DIM - half), F32)], axis=1)
    s2 = jnp.concatenate([jnp.zeros((seq, half), F32), sin_t, jnp.zeros((seq, HEAD_DIM - ROT_DIM), F32)], axis=1)
    return jnp.concatenate([jnp.tile(t, (1, LANES // HEAD_DIM)) for t in (c64, s1, s2)], axis=1)


def _rope_apply(t, tab, sign):
    reps = t.shape[1] // LANES
    c_tab, s1, s2 = (jnp.tile(tab[:, LANES * k:LANES * (k + 1)], (1, reps)) if reps > 1
                     else tab[:, LANES * k:LANES * (k + 1)] for k in range(3))
    half = ROT_DIM // 2
    up = pltpu.roll(t, t.shape[1] - half, 1)
    down = pltpu.roll(t, half, 1)
    return t * c_tab + sign * (down * s2 - up * s1)


def _lane_masks(shape):
    lane = lax.broadcasted_iota(jnp.int32, shape, 1)
    return lane < HEAD_DIM, lane >= HEAD_DIM


HEADS_PER_GROUP = N_Q_HEADS // N_KV_HEADS
ATTN_SCALE = 1.0 / math.sqrt(HEAD_DIM)


def _attn_bias_t(first_block):
    kj = lax.broadcasted_iota(jnp.int32, (2 * CHUNK, CHUNK), 0)
    qi = lax.broadcasted_iota(jnp.int32, (2 * CHUNK, CHUNK), 1)
    ok = (kj > qi) & (kj <= qi + CHUNK) & (jnp.logical_not(first_block) | (kj >= CHUNK))
    return jnp.tile(jnp.where(ok, 0.0, -jnp.inf), (1, HEADS_PER_GROUP))


def _group_rows(x, g, lo, hi):
    rows = []
    for r in range(HEADS_PER_GROUP):
        h = HEADS_PER_GROUP * g + r
        pair = x[:, LANES * (h // 2):LANES * (h // 2 + 1)]
        rows.append(jnp.where(hi if h % 2 else lo, pair, 0.0))
    return jnp.concatenate(rows, axis=0)


def _pairs_from_rows(rows, lo):
    return [jnp.where(lo, rows[2 * CHUNK * k:2 * CHUNK * k + CHUNK], rows[2 * CHUNK * k + CHUNK:2 * CHUNK * (k + 1)])
            for k in range(HEADS_PER_GROUP // 2)]


def _group_dup(a, b, g, lo2):
    return jnp.where(lo2, a, b) if g == 0 else jnp.where(lo2, b, a)


def _sink_row(sink_ref, g):
    return jnp.concatenate([sink_ref[HEADS_PER_GROUP * g + r:HEADS_PER_GROUP * g + r + 1, :]
                            for r in range(HEADS_PER_GROUP)], axis=1)


def _attn_probs_t(k_dup, q_rows, bias_t, sink_row):
    s_t = _dot_nt(k_dup, q_rows) * ATTN_SCALE + bias_t
    m = jnp.maximum(jnp.max(s_t, axis=0, keepdims=True), sink_row)
    p = jnp.exp(s_t - m)
    e_sink = jnp.exp(sink_row - m)
    inv = 1.0 / (jnp.sum(p, axis=0, keepdims=True) + e_sink)
    return p * inv, e_sink * inv


def _sgu_forward_pair(wm, vp, j):
    lo, hi = _lane_masks(vp.shape)
    lhs = jnp.concatenate([wm[2 * j], wm[2 * j + 1]], axis=1)
    rhs = jnp.concatenate([jnp.where(lo, vp, 0.0), jnp.where(hi, vp, 0.0)], axis=0)
    return _dot(lhs, rhs)


def _masked_spatial(w_ref):
    t = lax.broadcasted_iota(jnp.int32, (CHUNK, CHUNK), 0)
    s = lax.broadcasted_iota(jnp.int32, (CHUNK, CHUNK), 1)
    tril = s <= t
    return [jnp.where(tril, w_ref[g], 0.0) for g in range(GMLP_GROUPS)], tril, s >= t


def _mod_kernel(c_all, w_shard, b_shard):
    n = w_shard.shape[1]
    tn = 512

    def body(c_ref, w_ref, b_ref, mod_ref, act_ref):
        cv = c_ref[...]
        act = cv * (1.0 / (1.0 + jnp.exp(-cv)))
        act_ref[...] = act
        mod_ref[...] = _dot(act, w_ref[...]) + b_ref[...]

    return pl.pallas_call(
        body, name="ada_mod", grid=(n // tn,),
        out_shape=[jax.ShapeDtypeStruct((N_DEV, n), F32), jax.ShapeDtypeStruct((N_DEV, D_MODEL), F32)],
        in_specs=[_full((N_DEV, D_MODEL)), pl.BlockSpec((D_MODEL, tn), lambda i: (0, i)),
                  pl.BlockSpec((1, tn), lambda i: (0, i))],
        out_specs=[pl.BlockSpec((N_DEV, tn), lambda i: (0, i)), _full((N_DEV, D_MODEL))],
        compiler_params=_params("arbitrary"),
    )(c_all, w_shard, b_shard)


def _in_proj_kernel(x, vecs, w_in_t, comm=None):
    seq = x.shape[0]
    tm = 512

    def body(x_ref, v_ref, w_ref, proj_ref, h_ref):
        xv = x_ref[...]
        rstd = lax.rsqrt(_mean_last(xv * xv) + EPS)
        n1 = (xv * rstd) * v_ref[0:1, :]
        h = n1 * (1.0 + v_ref[2:3, :]) + v_ref[1:2, :]
        hb = h.astype(MXU_DTYPE)
        h_ref[...] = hb
        proj_ref[...] = _dot_nt(hb, w_ref[...])

    return _hosted_call(
        body, comm, name="in_proj", grid=(seq // tm,),
        out_shape=[jax.ShapeDtypeStruct((seq, IN_PROJ_WIDTH), F32),
                   jax.ShapeDtypeStruct((seq, D_MODEL), MXU_DTYPE)],
        in_specs=[pl.BlockSpec((tm, D_MODEL), lambda i: (i, 0)), _full((8, D_MODEL)),
                  _full((IN_PROJ_WIDTH, D_MODEL))],
        out_specs=[pl.BlockSpec((tm, IN_PROJ_WIDTH), lambda i: (i, 0)),
                   pl.BlockSpec((tm, D_MODEL), lambda i: (i, 0))],
        semantics=("arbitrary",),
    )(x, vecs, w_in_t)


def _mixer_fwd_kernel(proj, rope_tab, w_spatial, bias_full, sink_rows, comm=None):
    seq = proj.shape[0]
    nb = seq // CHUNK
    kv_col = (2 * GMLP_WIDTH + ATTN_WIDTH) // (2 * KV_WIDTH)

    def body(proj_ref, prev_ref, tab_ref, ptab_ref, w_ref, bias_ref, sink_ref, cat_ref):
        i = pl.program_id(0)
        wm, _, _ = _masked_spatial(w_ref)
        for j in range(GMLP_GROUPS // 2):
            cols = slice(LANES * j, LANES * (j + 1))
            vcols = slice(GMLP_WIDTH + LANES * j, GMLP_WIDTH + LANES * (j + 1))
            u, _ = _gelu_tanh(proj_ref[:, cols])
            vp, _ = _gelu_tanh(proj_ref[:, vcols])
            sv = _sgu_forward_pair(wm, vp, j) + bias_ref[:, cols]
            cat_ref[:, cols] = (u * sv).astype(cat_ref.dtype)
        o = 2 * GMLP_WIDTH
        tab = tab_ref[...]
        q_r = _rope_apply(proj_ref[:, o:o + ATTN_WIDTH], tab, 1.0)
        k_cur = _rope_apply(proj_ref[:, o + ATTN_WIDTH:o + ATTN_WIDTH + KV_WIDTH], tab, 1.0)
        k_prev = _rope_apply(prev_ref[:, 0:KV_WIDTH], ptab_ref[...], 1.0)
        k_a = jnp.concatenate([k_prev, k_cur], axis=0)
        v_a = jnp.concatenate([prev_ref[:, KV_WIDTH:2 * KV_WIDTH],
                               proj_ref[:, o + ATTN_WIDTH + KV_WIDTH:o + ATTN_WIDTH + 2 * KV_WIDTH]], axis=0)
        k_b = pltpu.roll(k_a, HEAD_DIM, 1)
        v_b = pltpu.roll(v_a, HEAD_DIM, 1)
        bias_t = _attn_bias_t(i == 0)
        lo, hi = _lane_masks((CHUNK, LANES))
        lo2, _ = _lane_masks((2 * CHUNK, LANES))
        for g in range(N_KV_HEADS):
            p_t, _ = _attn_probs_t(_group_dup(k_a, k_b, g, lo2), _group_rows(q_r, g, lo, hi), bias_t,
                                   _sink_row(sink_ref, g))
            o_t = _dot(_group_dup(v_a, v_b, g, lo2).T, p_t)
            for k, pair in enumerate(_pairs_from_rows(o_t.T, lo)):
                c0 = GMLP_WIDTH + LANES * (2 * g + k)
                cat_ref[:, c0:c0 + LANES] = pair.astype(cat_ref.dtype)

    return _hosted_call(
        body, comm, name="mixer_fwd", grid=(nb,),
        out_shape=[jax.ShapeDtypeStruct((seq, D_MODEL), MXU_DTYPE)],
        in_specs=[pl.BlockSpec((CHUNK, IN_PROJ_WIDTH), lambda i: (i, 0)),
                  pl.BlockSpec((CHUNK, 2 * KV_WIDTH), lambda i: (jnp.maximum(i - 1, 0), kv_col)),
                  pl.BlockSpec((CHUNK, 3 * LANES), lambda i: (i, 0)),
                  pl.BlockSpec((CHUNK, 3 * LANES), lambda i: (jnp.maximum(i - 1, 0), 0)),
                  _full((GMLP_GROUPS, CHUNK, CHUNK)), _full((CHUNK, GMLP_WIDTH)),
                  _full((N_Q_HEADS, LANES))],
        out_specs=[pl.BlockSpec((CHUNK, D_MODEL), lambda i: (i, 0))],
        semantics=("arbitrary",),
    )(proj, proj, rope_tab, rope_tab, w_spatial, bias_full, sink_rows)


def _trunk_kernel(x, target, cat, vecs, w_out, w_ff1, w_ff2):
    seq = x.shape[0]
    tm = 256
    nj = D_FF // D_MODEL

    def body(x_ref, t_ref, cat_ref, v_ref, wout_hbm, w1_hbm, w2_hbm,
             dx1_ref, dcat_ref, dmix_ref, h2_ref, r_ref, da_ref, dff_ref, sums_ref,
             wout, w1, w2, a_scr, sem):
        i = pl.program_id(0)

        @pl.when(i == 0)
        def _():
            copies = [pltpu.make_async_copy(wout_hbm, wout, sem.at[0]),
                      pltpu.make_async_copy(w1_hbm, w1, sem.at[1]),
                      pltpu.make_async_copy(w2_hbm, w2, sem.at[2])]
            for cp in copies:
                cp.start()
            for cp in copies:
                cp.wait()
            sums_ref[...] = jnp.zeros_like(sums_ref)

        gate1, shift2, scale2 = v_ref[0:1, :], v_ref[1:2, :], v_ref[2:3, :]
        gate2, g_ffn, g_final = v_ref[3:4, :], v_ref[4:5, :], v_ref[5:6, :]

        mix = _dot(cat_ref[...], wout[...])
        x1 = x_ref[...] + gate1 * mix
        rstd2 = lax.rsqrt(_mean_last(x1 * x1) + EPS)
        xh2 = x1 * rstd2
        n2 = xh2 * g_ffn
        h2b = (n2 * (1.0 + scale2) + shift2).astype(MXU_DTYPE)
        h2_ref[...] = h2b
        ff = jnp.zeros((tm, D_MODEL), F32)
        for j in range(nj):
            a = _dot(h2b, w1[j])
            a_scr[j] = a
            relu = jnp.maximum(a, 0.0)
            rb = (relu * relu).astype(MXU_DTYPE)
            r_ref[:, D_MODEL * j:D_MODEL * (j + 1)] = rb
            ff = ff + _dot(rb, w2[j])
        x2 = x1 + gate2 * ff
        rstd3 = lax.rsqrt(_mean_last(x2 * x2) + EPS)
        xh3 = x2 * rstd3
        err = xh3 * g_final - t_ref[...]
        loss = 0.5 * _rowsum(_mean_last(err * err))
        dy = err * (1.0 / D_MODEL)
        dxh3 = dy * g_final
        dx2 = rstd3 * (dxh3 - xh3 * _mean_last(dxh3 * xh3))
        dffb = (dx2 * gate2).astype(MXU_DTYPE)
        dff_ref[...] = dffb
        dh2 = jnp.zeros((tm, D_MODEL), F32)
        for j in range(nj):
            dr = _dot_nt(dffb, w2[j])
            dab = (dr * (2.0 * jnp.maximum(a_scr[j], 0.0))).astype(MXU_DTYPE)
            da_ref[:, D_MODEL * j:D_MODEL * (j + 1)] = dab
            dh2 = dh2 + _dot_nt(dab, w1[j])
        dn2 = dh2 * (1.0 + scale2)
        dxh2 = dn2 * g_ffn
        dx1 = dx2 + rstd2 * (dxh2 - xh2 * _mean_last(dxh2 * xh2))
        dx1_ref[...] = dx1
        dmixb = (dx1 * gate1).astype(MXU_DTYPE)
        dmix_ref[...] = dmixb
        dcat_ref[...] = _dot_nt(dmixb, wout[...])

        sums_ref[0:1, :] += _rowsum(dh2)
        sums_ref[1:2, :] += _rowsum(dh2 * n2)
        sums_ref[2:3, :] += _rowsum(dx2 * ff)
        sums_ref[3:4, :] += _rowsum(dn2 * xh2)
        sums_ref[4:5, :] += _rowsum(dy * xh3)
        sums_ref[5:6, :] += _rowsum(dx1 * mix)
        sums_ref[6:7, :] += jnp.broadcast_to(loss, (1, D_MODEL))

    tok = lambda w: pl.BlockSpec((tm, w), lambda i: (i, 0))
    return pl.pallas_call(
        body, name="trunk", grid=(seq // tm,),
        out_shape=[jax.ShapeDtypeStruct((seq, D_MODEL), F32), jax.ShapeDtypeStruct((seq, D_MODEL), F32),
                   jax.ShapeDtypeStruct((seq, D_MODEL), MXU_DTYPE), jax.ShapeDtypeStruct((seq, D_MODEL), MXU_DTYPE),
                   jax.ShapeDtypeStruct((seq, D_FF), MXU_DTYPE), jax.ShapeDtypeStruct((seq, D_FF), MXU_DTYPE),
                   jax.ShapeDtypeStruct((seq, D_MODEL), MXU_DTYPE), jax.ShapeDtypeStruct((8, D_MODEL), F32)],
        in_specs=[tok(D_MODEL), tok(D_MODEL), tok(D_MODEL), _full((8, D_MODEL)), _any(), _any(), _any()],
        out_specs=[tok(D_MODEL), tok(D_MODEL), tok(D_MODEL), tok(D_MODEL), tok(D_FF), tok(D_FF), tok(D_MODEL),
                   _full((8, D_MODEL))],
        scratch_shapes=[pltpu.VMEM((D_MODEL, D_MODEL), MXU_DTYPE), pltpu.VMEM((nj, D_MODEL, D_MODEL), MXU_DTYPE),
                        pltpu.VMEM((nj, D_MODEL, D_MODEL), MXU_DTYPE), pltpu.VMEM((nj, tm, D_MODEL), F32),
                        pltpu.SemaphoreType.DMA((3,))],
        compiler_params=_params("arbitrary"),
    )(x, target, cat, vecs, w_out, w_ff1, w_ff2)


def _mixer_bwd_kernel(proj, rope_tab, dcat, w_spatial, w_spatial_t, bias_full, sink_rows, comm=None):
    seq = proj.shape[0]
    nb = seq // CHUNK
    kv_col = (2 * GMLP_WIDTH + ATTN_WIDTH) // (2 * KV_WIDTH)

    def body(proj_ref, prev_ref, tab_ref, ptab_ref, dcat_ref, w_ref, wt_ref, bias_ref, sink_ref,
             dproj_ref, dw_ref, db_ref, dsink_ref, carry):
        step = pl.program_id(0)
        blk = nb - 1 - step

        @pl.when(step == 0)
        def _():
            carry[...] = jnp.zeros_like(carry)
            dw_ref[...] = jnp.zeros_like(dw_ref)
            db_ref[...] = jnp.zeros_like(db_ref)
            dsink_ref[...] = jnp.zeros_like(dsink_ref)

        wm, tril, triu = _masked_spatial(w_ref)
        lo, hi = _lane_masks((CHUNK, LANES))
        lane = lax.broadcasted_iota(jnp.int32, (CHUNK, LANES), 1)
        db = jnp.zeros((CHUNK, LANES), F32)
        for j in range(GMLP_GROUPS // 2):
            cols = slice(LANES * j, LANES * (j + 1))
            vcols = slice(GMLP_WIDTH + LANES * j, GMLP_WIDTH + LANES * (j + 1))
            zu, zv = proj_ref[:, cols], proj_ref[:, vcols]
            u, tu = _gelu_tanh(zu)
            vp, tv = _gelu_tanh(zv)
            sv = _sgu_forward_pair(wm, vp, j) + bias_ref[:, cols]
            dout = dcat_ref[:, cols]
            du = dout * sv
            dsv = dout * u
            dsv_lo, dsv_hi = jnp.where(lo, dsv, 0.0), jnp.where(hi, dsv, 0.0)
            lhs_t = jnp.concatenate([jnp.where(triu, wt_ref[2 * j], 0.0),
                                     jnp.where(triu, wt_ref[2 * j + 1], 0.0)], axis=1)
            dv = _dot(lhs_t, jnp.concatenate([dsv_lo, dsv_hi], axis=0))
            dw_ref[2 * j] += jnp.where(tril, _dot_nt(dsv_lo, vp), 0.0)
            dw_ref[2 * j + 1] += jnp.where(tril, _dot_nt(dsv_hi, vp), 0.0)
            db = db + (jnp.where(lane == 2 * j, jnp.sum(dsv_lo, axis=1, keepdims=True), 0.0)
                       + jnp.where(lane == 2 * j + 1, jnp.sum(dsv_hi, axis=1, keepdims=True), 0.0))
            dproj_ref[:, cols] = (du * _gelu_tanh_grad(zu, tu)).astype(dproj_ref.dtype)
            dproj_ref[:, vcols] = (dv * _gelu_tanh_grad(zv, tv)).astype(dproj_ref.dtype)
        db_ref[...] += db
        o = 2 * GMLP_WIDTH
        tab = tab_ref[...]
        q_r = _rope_apply(proj_ref[:, o:o + ATTN_WIDTH], tab, 1.0)
        k_cur = _rope_apply(proj_ref[:, o + ATTN_WIDTH:o + ATTN_WIDTH + KV_WIDTH], tab, 1.0)
        k_prev = _rope_apply(prev_ref[:, 0:KV_WIDTH], ptab_ref[...], 1.0)
        k_a = jnp.concatenate([k_prev, k_cur], axis=0)
        v_a = jnp.concatenate([prev_ref[:, KV_WIDTH:2 * KV_WIDTH],
                               proj_ref[:, o + ATTN_WIDTH + KV_WIDTH:o + ATTN_WIDTH + 2 * KV_WIDTH]], axis=0)
        k_b = pltpu.roll(k_a, HEAD_DIM, 1)
        v_b = pltpu.roll(v_a, HEAD_DIM, 1)
        bias_t = _attn_bias_t(blk == 0)
        lo2, _ = _lane_masks((2 * CHUNK, LANES))
        dout_b = dcat_ref[:, GMLP_WIDTH:GMLP_WIDTH + ATTN_WIDTH]
        dk_tot, dv_tot, dq_pairs = [], [], []
        for g in range(N_KV_HEADS):
            k_dup, v_dup = _group_dup(k_a, k_b, g, lo2), _group_dup(v_a, v_b, g, lo2)
            q_rows = _group_rows(q_r, g, lo, hi)
            do_rows = _group_rows(dout_b, g, lo, hi)
            p_t, p_sink = _attn_probs_t(k_dup, q_rows, bias_t, _sink_row(sink_ref, g))
            dp_t = _dot_nt(v_dup, do_rows)
            delta = jnp.sum(p_t * dp_t, axis=0, keepdims=True)
            ds_t = p_t * (dp_t - delta) * ATTN_SCALE
            dsink = -p_sink * delta
            for r in range(HEADS_PER_GROUP):
                h = HEADS_PER_GROUP * g + r
                dsink_ref[h:h + 1, :] += jnp.broadcast_to(
                    jnp.sum(dsink[:, LANES * r:LANES * (r + 1)], axis=1, keepdims=True), (1, LANES))
            dk_full = _dot(ds_t, q_rows)
            dv_full = _dot(p_t, do_rows)
            dk_tot.append(dk_full + pltpu.roll(dk_full, HEAD_DIM, 1))
            dv_tot.append(dv_full + pltpu.roll(dv_full, HEAD_DIM, 1))
            dq_t = _dot(k_dup.T, ds_t)
            dq_pairs += _pairs_from_rows(dq_t.T, lo)
        dk_all = jnp.where(lo2, dk_tot[0], dk_tot[1])
        dv_all = jnp.where(lo2, dv_tot[0], dv_tot[1])
        dk_cur = dk_all[CHUNK:, :] + carry[:, 0:KV_WIDTH]
        dv_cur = dv_all[CHUNK:, :] + carry[:, KV_WIDTH:2 * KV_WIDTH]
        carry[:, 0:KV_WIDTH] = dk_all[:CHUNK, :]
        carry[:, KV_WIDTH:2 * KV_WIDTH] = dv_all[:CHUNK, :]
        dq = _rope_apply(jnp.concatenate(dq_pairs, axis=1), tab, -1.0)
        dproj_ref[:, o:o + ATTN_WIDTH] = dq.astype(dproj_ref.dtype)
        dproj_ref[:, o + ATTN_WIDTH:o + ATTN_WIDTH + KV_WIDTH] = (
            _rope_apply(dk_cur, tab, -1.0).astype(dproj_ref.dtype))
        dproj_ref[:, o + ATTN_WIDTH + KV_WIDTH:o + ATTN_WIDTH + 2 * KV_WIDTH] = dv_cur.astype(dproj_ref.dtype)

    rev = lambda i: nb - 1 - i
    return _hosted_call(
        body, comm, name="mixer_bwd", grid=(nb,),
        out_shape=[jax.ShapeDtypeStruct((seq, IN_PROJ_WIDTH), MXU_DTYPE),
                   jax.ShapeDtypeStruct((GMLP_GROUPS, CHUNK, CHUNK), F32),
                   jax.ShapeDtypeStruct((CHUNK, LANES), F32),
                   jax.ShapeDtypeStruct((N_Q_HEADS, LANES), F32)],
        in_specs=[pl.BlockSpec((CHUNK, IN_PROJ_WIDTH), lambda i: (rev(i), 0)),
                  pl.BlockSpec((CHUNK, 2 * KV_WIDTH), lambda i: (jnp.maximum(rev(i) - 1, 0), kv_col)),
                  pl.BlockSpec((CHUNK, 3 * LANES), lambda i: (rev(i), 0)),
                  pl.BlockSpec((CHUNK, 3 * LANES), lambda i: (jnp.maximum(rev(i) - 1, 0), 0)),
                  pl.BlockSpec((CHUNK, D_MODEL), lambda i: (rev(i), 0)),
                  _full((GMLP_GROUPS, CHUNK, CHUNK)), _full((GMLP_GROUPS, CHUNK, CHUNK)),
                  _full((CHUNK, GMLP_WIDTH)), _full((N_Q_HEADS, LANES))],
        out_specs=[pl.BlockSpec((CHUNK, IN_PROJ_WIDTH), lambda i: (rev(i), 0)),
                   _full((GMLP_GROUPS, CHUNK, CHUNK)), _full((CHUNK, LANES)), _full((N_Q_HEADS, LANES))],
        scratch_shapes=[pltpu.VMEM((CHUNK, 2 * KV_WIDTH), F32)],
        semantics=("arbitrary",),
    )(proj, proj, rope_tab, rope_tab, dcat, w_spatial, w_spatial_t, bias_full, sink_rows)


def _in_proj_bwd_kernel(x, dx1, dproj, vecs, w_in_t, comm=None):
    seq = x.shape[0]
    tm = 512

    def body(x_ref, dx1_ref, dp_ref, v_ref, w_ref, gx_ref, sums_ref):
        @pl.when(pl.program_id(0) == 0)
        def _():
            sums_ref[...] = jnp.zeros_like(sums_ref)

        g_mix, scale1 = v_ref[0:1, :], v_ref[2:3, :]
        dh = _dot(dp_ref[...], w_ref[...])
        xv = x_ref[...]
        rstd = lax.rsqrt(_mean_last(xv * xv) + EPS)
        xh = xv * rstd
        dn1 = dh * (1.0 + scale1)
        dxh = dn1 * g_mix
        gx_ref[...] = dx1_ref[...] + rstd * (dxh - xh * _mean_last(dxh * xh))
        sums_ref[0:1, :] += _rowsum(dh)
        sums_ref[1:2, :] += _rowsum(dh * (xh * g_mix))
        sums_ref[2:3, :] += _rowsum(dn1 * xh)

    return _hosted_call(
        body, comm, name="in_proj_bwd", grid=(seq // tm,),
        out_shape=[jax.ShapeDtypeStruct((seq, D_MODEL), F32), jax.ShapeDtypeStruct((8, D_MODEL), F32)],
        in_specs=[pl.BlockSpec((tm, D_MODEL), lambda i: (i, 0)), pl.BlockSpec((tm, D_MODEL), lambda i: (i, 0)),
                  pl.BlockSpec((tm, IN_PROJ_WIDTH), lambda i: (i, 0)), _full((8, D_MODEL)),
                  _full((IN_PROJ_WIDTH, D_MODEL))],
        out_specs=[pl.BlockSpec((tm, D_MODEL), lambda i: (i, 0)), _full((8, D_MODEL))],
        semantics=("arbitrary",),
    )(x, dx1, dproj, vecs, w_in_t)


class _GradTiles(NamedTuple):
    tm: int
    tn: int
    n_tiles: int
    chips_per_tile: int
    a_index: Callable
    b_index: Callable


def _weight_grad_kernel(a, b, c_idx, name, tiles, comm=None):
    seq = a.shape[0]
    tk = min(seq, 4096)
    nk = seq // tk
    tm, tn, n_tiles, per = tiles.tm, tiles.tn, tiles.n_tiles, tiles.chips_per_tile
    rows = tm // per

    def half(phase, c):
        return phase * c[0] + (1 - phase) * (1 - c[0])

    def body(c_ref, a_ref, b_ref, o_ref, acc, stage, landed, send_sems, recv_sems):
        del c_ref
        phase, t, kk = pl.program_id(0), pl.program_id(1), pl.program_id(2)
        x, y, c, _ = _mesh_place()

        def copy(tile):
            return pltpu.make_async_remote_copy(
                src_ref=stage.at[tile], dst_ref=landed.at[tile], send_sem=send_sems.at[tile],
                recv_sem=recv_sems.at[tile], device_id=(x, y, 1 - c), device_id_type=MESH)

        @pl.when(kk == 0)
        def _():
            acc[...] = jnp.zeros_like(acc)

        acc[...] += _dot_tn(a_ref[...], b_ref[...])

        @pl.when((kk == nk - 1) & (phase == 0))
        def _():
            stage[t] = acc[...].astype(stage.dtype)
            copy(t).start()

        @pl.when((kk == nk - 1) & (phase == 1))
        def _():
            copy(t).wait_recv()
            total = acc[...] + landed[t].astype(F32)
            for q in range(per):
                o_ref[q] = total[rows * q:rows * (q + 1)].astype(o_ref.dtype)

        @pl.when((kk == nk - 1) & (phase == 1) & (t == n_tiles - 1))
        def _():
            for tile in range(n_tiles):
                copy(tile).wait_send()

    out = _hosted_call(
        body, comm, name=name, grid=(2, n_tiles, nk), n_prefetch=1,
        out_shape=[jax.ShapeDtypeStruct((n_tiles * per, rows, tn), GRAD_COMM_DTYPE)],
        in_specs=[pl.BlockSpec((tk, tm), lambda p, t, k, c: (k, tiles.a_index(t, half(p, c)))),
                  pl.BlockSpec((tk, tn), lambda p, t, k, c: (k, tiles.b_index(t, half(p, c))))],
        out_specs=[pl.BlockSpec((per, rows, tn), lambda p, t, k, c: (p * t, 0, 0))],
        scratch_shapes=[pltpu.VMEM((tm, tn), F32), pltpu.VMEM((n_tiles, tm, tn), GRAD_COMM_DTYPE),
                        pltpu.VMEM((n_tiles, tm, tn), GRAD_COMM_DTYPE),
                        pltpu.SemaphoreType.DMA((n_tiles,)), pltpu.SemaphoreType.DMA((n_tiles,))],
        semantics=("arbitrary", "arbitrary", "arbitrary"),
    )(c_idx, a, b)
    return out[0] if comm is None else out


def _row_tile(rows, most=256, sublanes=16):
    return max(t for t in range(sublanes, most + 1, sublanes) if rows % t == 0)


def _adam_update(w, g, m, v):
    m_new = ADAM_B1 * m + (1.0 - ADAM_B1) * g
    v_new = ADAM_B2 * v + (1.0 - ADAM_B2) * (g * g)
    m_hat = m_new / (1.0 - ADAM_B1 ** ADAM_STEP)
    v_hat = v_new / (1.0 - ADAM_B2 ** ADAM_STEP)
    delta = -ADAM_LR * (m_hat / (jnp.sqrt(v_hat) + ADAM_EPS) + ADAM_WD * w)
    return delta, m_new, v_new


def _sum_chips_kernel(own, others, place, name):
    _, r, n = own.shape
    tr = _row_tile(r)

    def body(place_ref, own_ref, oth_ref, o_ref):
        del place_ref
        acc = own_ref[...].astype(F32)
        for k in range(N_CHIPS - 1):
            acc = acc + oth_ref[k].astype(F32)
        o_ref[...] = acc

    return pl.pallas_call(
        body, name=name, out_shape=jax.ShapeDtypeStruct((2, r, n), F32),
        grid_spec=pltpu.PrefetchScalarGridSpec(
            num_scalar_prefetch=1, grid=(r // tr,),
            in_specs=[pl.BlockSpec((None, tr, n), lambda i, p: (p[0], i, 0)),
                      pl.BlockSpec((N_CHIPS - 1, tr, n), lambda i, p: (0, i, 0))],
            out_specs=pl.BlockSpec((None, tr, n), lambda i, p: (p[1], i, 0))),
        compiler_params=_params("parallel"),
    )(place, own, others)


def _adam_kernel(w, g, m, v, name):
    r, n = w.shape
    by_columns = g.shape[1] == r
    tr, tn = _row_tile(g.shape[1]), g.shape[2]

    def body(w_ref, g_ref, m_ref, v_ref, g_out, d_ref, mo_ref, vo_ref):
        gv = g_ref[...]
        g_out[...] = gv
        d_ref[...], mo_ref[...], vo_ref[...] = _adam_update(w_ref[...], gv, m_ref[...], v_ref[...])

    steps = g.shape[1] // tr
    spec = pl.BlockSpec((tr, tn), (lambda h, i: (i, h)) if by_columns else (lambda h, i: (h * steps + i, 0)))
    return pl.pallas_call(
        body, name=name, grid=(2, steps), out_shape=[jax.ShapeDtypeStruct((r, n), F32)] * 4,
        in_specs=[spec, pl.BlockSpec((None, tr, tn), lambda h, i: (h, i, 0)), spec, spec], out_specs=[spec] * 4,
        compiler_params=_params("parallel", "parallel"),
    )(w, g, m, v)


SMALL_PARAMS = ("b_ada", "g_mix", "g_ffn", "g_final", "b_spatial", "sinks", "w_spatial")


def _small_update_kernel(gathered, params):
    shapes = [params[nm][0].shape for nm in SMALL_PARAMS]

    def body(*refs):
        g_refs, refs = refs[:5], refs[5:]
        p_refs, refs = refs[:3 * len(SMALL_PARAMS)], refs[3 * len(SMALL_PARAMS):]
        loss_ref, o_refs = refs[0], refs[1:]

        def total(ref):
            acc = ref[0]
            for k in range(1, N_DEV):
                acc = acc + ref[k]
            return acc

        s1, s2, db, ds, dw = (total(r) for r in g_refs)
        loss_ref[...] = jnp.broadcast_to(s2[6:7, 0:1], loss_ref.shape)
        grads = {"b_ada": [s1[0:1], s1[1:2], s2[5:6], s2[0:1], s2[1:2], s2[2:3]], "g_mix": [s1[2:3]],
                 "g_ffn": [s2[3:4]], "g_final": [s2[4:5]], "b_spatial": [db.T[0:GMLP_GROUPS]],
                 "w_spatial": [dw]}
        lane = lax.broadcasted_iota(jnp.int32, (1, LANES), 1)
        sink_row = jnp.zeros((1, LANES), F32)
        for h in range(N_Q_HEADS):
            sink_row = sink_row + jnp.where(lane == h, ds[h:h + 1, :], 0.0)
        grads["sinks"] = [sink_row[:, 0:N_Q_HEADS]]
        for i, nm in enumerate(SMALL_PARAMS):
            w_ref, m_ref, v_ref = p_refs[3 * i:3 * i + 3]
            outs = o_refs[4 * i:4 * i + 4]
            width = grads[nm][0].shape[1]
            for k, g in enumerate(grads[nm]):
                cols = slice(width * k, width * (k + 1))
                upd = _adam_update(w_ref[:, cols], g, m_ref[:, cols], v_ref[:, cols])
                for o_ref, val in zip(outs, (g,) + upd):
                    o_ref[:, cols] = val

    flat = [a for nm in SMALL_PARAMS for a in params[nm]]
    out_shape = [jax.ShapeDtypeStruct((8, LANES), F32)]
    out_shape += [jax.ShapeDtypeStruct(s, F32) for s in shapes for _ in range(4)]
    outs = pl.pallas_call(
        body, name="small_update", grid=(1,), out_shape=out_shape,
        in_specs=[_full(g.shape) for g in gathered] + [_full(a.shape) for a in flat],
        out_specs=[_full(s.shape) for s in out_shape],
        compiler_params=_params("arbitrary"),
    )(*gathered, *flat)
    return {nm: outs[1 + 4 * i:5 + 4 * i] for i, nm in enumerate(SMALL_PARAMS)}, outs[0]


def _ada_update_kernel(act_t, dmod, w, m, v):
    r, n = w.shape
    tr = 256

    def body(a_ref, d_ref, w_ref, m_ref, v_ref, g_ref, dl_ref, mo_ref, vo_ref):
        g = _dot(a_ref[...], d_ref[...])
        g_ref[...] = g
        dl_ref[...], mo_ref[...], vo_ref[...] = _adam_update(w_ref[...], g, m_ref[...], v_ref[...])

    spec = pl.BlockSpec((tr, n), lambda i: (i, 0))
    return pl.pallas_call(
        body, name="ada_update", grid=(r // tr,), out_shape=[jax.ShapeDtypeStruct((r, n), F32)] * 4,
        in_specs=[pl.BlockSpec((tr, N_DEV), lambda i: (i, 0)), _full((N_DEV, n)), spec, spec, spec],
        out_specs=[spec] * 4, compiler_params=_params("parallel"),
    )(act_t, dmod, w, m, v)


def kernel(x, c, positions, w_ada, b_ada, g_mix, w_in, w_spatial, b_spatial, sinks, w_out, g_ffn, w_ff1, w_ff2, g_final, loss_target, m_w_ada, m_b_ada, m_g_mix, m_w_in, m_w_spatial, m_b_spatial, m_sinks, m_w_out, m_g_ffn, m_w_ff1, m_w_ff2, m_g_final, v_w_ada, v_b_ada, v_g_mix, v_w_in, v_w_spatial, v_b_spatial, v_sinks, v_w_out, v_g_ffn, v_w_ff1, v_w_ff2, v_g_final):
    xi, yi, ci = lax.axis_index("x"), lax.axis_index("y"), lax.axis_index("c")
    chip = 2 * xi + yi
    dev = 2 * chip + ci
    seq = x.shape[1]
    x2, tgt = x[0], loss_target[0]
    ada_cols = w_ada.shape[2]

    c_all = _all_gather8([c], "gather_c")[0].reshape(N_DEV, D_MODEL)
    b_shard = lax.dynamic_slice(b_ada, (0, chip * ada_cols), (1, ada_cols))
    mod_part, act = _mod_kernel(c_all, w_ada[0], b_shard)
    mod_all, = _all_gather8([mod_part], "gather_mod")
    mod_me = lax.dynamic_index_in_dim(mod_all[0::2], dev, axis=1, keepdims=False)
    mod_me = mod_me.reshape(N_MOD, D_MODEL)
    shift1, scale1, gate1, shift2, scale2, gate2 = (mod_me[k:k + 1] for k in range(N_MOD))

    big = {"w_in": tuple(a[0].T for a in (w_in, m_w_in, v_w_in)),
           "w_out": (w_out[0], m_w_out[0], v_w_out[0]), "w_ff1": (w_ff1[0], m_w_ff1[0], v_w_ff1[0]),
           "w_ff2": (w_ff2[0], m_w_ff2[0], v_w_ff2[0])}

    def halves(nm):
        r, n = big[nm][0].shape
        return big[nm][0].astype(WEIGHT_COMM_DTYPE).reshape(2, r // 2, n)

    w_in_t, = _all_gather8([halves("w_in")], "gather_w_in", split=True)
    w_in_t = w_in_t.reshape(IN_PROJ_WIDTH, D_MODEL)

    zeros_row = jnp.zeros((1, D_MODEL), F32)
    vecs1 = jnp.concatenate([g_mix, shift1, scale1] + [zeros_row] * 5, axis=0)
    vecs2 = jnp.concatenate([gate1, shift2, scale2, gate2, g_ffn, g_final.reshape(1, D_MODEL)]
                            + [zeros_row] * 2, axis=0)
    bias_full = jnp.repeat(b_spatial[0].T, HEAD_DIM, axis=1)
    sink_rows = jnp.broadcast_to(sinks[0][:, None], (N_Q_HEADS, LANES))
    inv_freq = ROPE_THETA ** (-jnp.arange(0, ROT_DIM, 2, dtype=F32) / ROT_DIM)
    rope_tab = _rope_lane_tables(*_rope_angle_kernel(positions, inv_freq.reshape(ROT_DIM // 2, 1)))

    proj, hb, g_ff1 = _in_proj_kernel(x2, vecs1, w_in_t, comm=_gather_job([halves("w_ff1")]))
    cat, g_ff2, g_out = _mixer_fwd_kernel(proj, rope_tab, w_spatial[0], bias_full, sink_rows,
                                          comm=_gather_job([halves("w_ff2"), halves("w_out")]))
    g_ff1, g_ff2, g_out = _gather_forward([g_ff1, g_ff2, g_out], "gather_forward")
    w_out_full = g_out.reshape(D_MODEL, D_MODEL)
    w_ff1_blocks = g_ff1.reshape(N_CHIPS, D_MODEL, D_MODEL)
    w_ff2_blocks = g_ff2.reshape(N_CHIPS, D_MODEL, D_MODEL)
    dx1, dcat, dmix, h2b, rb, dab, dffb, sums2 = _trunk_kernel(
        x2, tgt, cat, vecs2, w_out_full, w_ff1_blocks, w_ff2_blocks)

    c_idx = ci.reshape(1).astype(jnp.int32)
    place = jnp.stack([chip, ci]).astype(jnp.int32)
    half_d = D_MODEL // 2
    cs_ff2 = _weight_grad_kernel(rb, dffb, c_idx, "dw_ff2",
                                 _GradTiles(D_MODEL, half_d, N_CHIPS, 1, lambda t, h: t, lambda t, h: h))
    cs_ff1, sc_ff2 = _weight_grad_kernel(
        h2b, dab, c_idx, "dw_ff1",
        _GradTiles(D_MODEL, half_d, N_CHIPS, 1, lambda t, h: 0, lambda t, h: 2 * t + h),
        comm=_scatter_job([cs_ff2]))
    cs_out = _weight_grad_kernel(cat, dmix, c_idx, "dw_out",
                                 _GradTiles(D_MODEL, half_d, 1, N_CHIPS, lambda t, h: 0, lambda t, h: h))
    dproj, dw_spatial, db_lanes, dsink_rows, sc_ff1, sc_out = _mixer_bwd_kernel(
        proj, rope_tab, dcat, w_spatial[0], w_spatial[0].transpose(0, 2, 1), bias_full, sink_rows,
        comm=_scatter_job([cs_ff1, cs_out]))
    cs_in, *small_stage1 = _weight_grad_kernel(
        dproj, hb, c_idx, "dw_in",
        _GradTiles(2 * W_IN_BLOCK, half_d, N_CHIPS // 2, 2, lambda t, h: t, lambda t, h: h),
        comm=_gather_job([db_lanes, dsink_rows, dw_spatial.reshape(GMLP_GROUPS * CHUNK, CHUNK)], split=False))
    grad_x, sums1 = _in_proj_bwd_kernel(x2, dx1, dproj, vecs1, w_in_t)
    sc_in, = _run_comm(_scatter_job([cs_in]), "grad_to_chips_w_in")

    names = ["w_in", "w_out", "w_ff1", "w_ff2"]
    totals = [_sum_chips_kernel(own, oth, place, "grad_sum_" + nm)
              for nm, own, oth in zip(names, [cs_in, cs_out, cs_ff1, cs_ff2], [sc_in, sc_out, sc_ff1, sc_ff2])]
    shared = _sibling_share(totals, "grad_share")
    big_out = {}
    for nm, g in zip(names, shared):
        w, m, v = big[nm]
        outs = _adam_kernel(w, g, m, v, "adam_" + nm)
        big_out[nm] = tuple((t.T if nm == "w_in" else t)[None] for t in outs)

    small = {"b_ada": (b_ada, m_b_ada, v_b_ada), "g_mix": (g_mix, m_g_mix, v_g_mix),
             "g_ffn": (g_ffn, m_g_ffn, v_g_ffn), "g_final": (g_final, m_g_final, v_g_final),
             "b_spatial": (b_spatial, m_b_spatial, v_b_spatial), "sinks": (sinks, m_sinks, v_sinks),
             "w_spatial": (w_spatial, m_w_spatial, v_w_spatial)}
    flat_shape = {"g_final": (1, D_MODEL), "b_spatial": (GMLP_GROUPS, CHUNK), "w_spatial": (GMLP_GROUPS * CHUNK, CHUNK)}
    gathered = _all_gather8([sums1, sums2], "gather_small", forward=small_stage1)
    small_out, loss_tile = _small_update_kernel(
        gathered, {nm: tuple(a.reshape(flat_shape.get(nm, a.shape)) for a in small[nm]) for nm in small})
    small_out = {nm: [o.reshape(small[nm][0].shape) for o in small_out[nm]] for nm in small}
    loss = loss_tile[0, 0]

    g1, g2 = gathered[0], gathered[1]
    dmod_all = jnp.concatenate([g1[:, 0], g1[:, 1], g2[:, 5], g2[:, 0], g2[:, 1], g2[:, 2]], axis=1)
    dmod_cols = lax.dynamic_slice(dmod_all, (0, chip * ada_cols), (N_DEV, ada_cols))
    ada = _ada_update_kernel(act.T, dmod_cols, w_ada[0], m_w_ada[0], v_w_ada[0])
    big_out["w_ada"] = tuple(t[None] for t in ada)

    order = ["w_ada", "b_ada", "g_mix", "w_in", "w_spatial", "b_spatial", "sinks", "w_out", "g_ffn",
             "w_ff1", "w_ff2", "g_final"]

    def leaf(nm, k):
        return big_out[nm][k] if nm in big_out else small_out[nm][k]

    outs = [loss, grad_x[None]]
    for k in range(4):
        outs += [leaf(nm, k) for nm in order]
    return tuple(outs)
```

```python
import math
from typing import Callable, NamedTuple

import jax
import jax.numpy as jnp
from jax import lax
from jax.experimental import pallas as pl
from jax.experimental.pallas import tpu as pltpu

F32 = jnp.float32
MXU_DTYPE = jnp.bfloat16
WEIGHT_COMM_DTYPE = jnp.bfloat16
GRAD_COMM_DTYPE = jnp.bfloat16

D_MODEL = 1024
D_FF = 4096
HEAD_DIM = 64
GMLP_GROUPS = 8
GMLP_WIDTH = 512
CHUNK = 128
N_Q_HEADS = 8
N_KV_HEADS = 2
ATTN_WIDTH = 512
KV_WIDTH = 128
ROT_DIM = 16
ROPE_THETA = 500000.0
IN_PROJ_WIDTH = 1792
N_MOD = 6
EPS = 1e-5
N_CHIPS = 4
N_DEV = 8
LANES = 128
W_IN_BLOCK = IN_PROJ_WIDTH // N_CHIPS

ADAM_LR = 0.001
ADAM_B1 = 0.9
ADAM_B2 = 0.999
ADAM_EPS = 1e-08
ADAM_WD = 0.01
ADAM_STEP = 10

VMEM_LIMIT_BYTES = 58 * 1024 * 1024
MESH = pl.DeviceIdType.MESH


def _params(*semantics):
    return pltpu.CompilerParams(dimension_semantics=semantics, vmem_limit_bytes=VMEM_LIMIT_BYTES)


def _dot(a, b):
    return jnp.dot(a.astype(MXU_DTYPE), b.astype(MXU_DTYPE), preferred_element_type=F32)


def _dot_nt(a, b):
    return lax.dot_general(a.astype(MXU_DTYPE), b.astype(MXU_DTYPE), (((1,), (1,)), ((), ())),
                           preferred_element_type=F32)


def _dot_tn(a, b):
    return lax.dot_general(a.astype(MXU_DTYPE), b.astype(MXU_DTYPE), (((0,), (0,)), ((), ())),
                           preferred_element_type=F32)


def _full(shape):
    return pl.BlockSpec(shape, lambda *_: (0,) * len(shape))


def _any():
    return pl.BlockSpec(memory_space=pl.ANY)


def _rowsum(v):
    return jnp.sum(v, axis=0, keepdims=True)


def _mean_last(v):
    return jnp.mean(v, axis=-1, keepdims=True)


class _Comm(NamedTuple):
    operands: tuple
    out_shapes: tuple
    n_sems: int
    make: Callable
    in_place: bool = False


def _hosted_call(body, comm, *, name, grid, in_specs, out_shape, out_specs, scratch_shapes=(), semantics,
                 n_prefetch=0):
    if comm is None:
        return pl.pallas_call(
            body, name=name, out_shape=out_shape, compiler_params=_params(*semantics),
            grid_spec=pltpu.PrefetchScalarGridSpec(
                num_scalar_prefetch=n_prefetch, grid=grid, in_specs=in_specs, out_specs=out_specs,
                scratch_shapes=list(scratch_shapes)))
    n_in, n_out, n_scr = len(in_specs), len(out_shape), len(scratch_shapes)
    k_in, k_out = len(comm.operands), len(comm.out_shapes)

    def hosted(*refs):
        prefetched, refs = refs[:n_prefetch], refs[n_prefetch:]
        ins, refs = refs[:n_in], refs[n_in:]
        c_ins, refs = refs[:k_in], refs[k_in:]
        outs, refs = refs[:n_out], refs[n_out:]
        c_outs, refs = refs[:k_out], refs[k_out:]
        scratch, (send_sems, recv_sems) = refs[:n_scr], refs[n_scr:]
        first, last = None, None
        for d, size in enumerate(grid):
            at_start, at_end = pl.program_id(d) == 0, pl.program_id(d) == size - 1
            first = at_start if first is None else first & at_start
            last = at_end if last is None else last & at_end

        @pl.when(first)
        def _():
            for cp in comm.make(c_ins, c_outs, send_sems, recv_sems)[0]:
                cp.start()

        body(*prefetched, *ins, *outs, *scratch)

        @pl.when(last)
        def _():
            for wait in comm.make(c_ins, c_outs, send_sems, recv_sems)[1]:
                wait()

    aliases = {n_prefetch + n_in + i: n_out + i for i in range(k_in)} if comm.in_place else {}
    call = pl.pallas_call(
        hosted, name=name, out_shape=list(out_shape) + list(comm.out_shapes),
        compiler_params=_params(*semantics), input_output_aliases=aliases,
        grid_spec=pltpu.PrefetchScalarGridSpec(
            num_scalar_prefetch=n_prefetch, grid=grid, in_specs=list(in_specs) + [_any()] * k_in,
            out_specs=list(out_specs) + [_any()] * k_out,
            scratch_shapes=list(scratch_shapes) + [pltpu.SemaphoreType.DMA((comm.n_sems,)),
                                                    pltpu.SemaphoreType.DMA((comm.n_sems,))]))
    return lambda *args: call(*args, *comm.operands)


def _mesh_place():
    x, y, c = lax.axis_index("x"), lax.axis_index("y"), lax.axis_index("c")
    return x, y, c, [(1 - x, y), (x, 1 - y), (1 - x, 1 - y)]


def _gather_job(blocks, split=True):
    per = 5

    def make(ins, outs, send_sems, recv_sems):
        x, y, c, chips = _mesh_place()
        starts, waits = [], []
        for a, (src, out) in enumerate(zip(ins, outs)):
            src = src.at[c] if split else src
            mine = out.at[4 * x + 2 * y + c]
            local = pltpu.make_async_copy(src, mine, send_sems.at[per * a + 4])
            to = [(x, y, 1 - c)] + [(px, py, c) for px, py in chips]
            sends = [pltpu.make_async_remote_copy(
                src_ref=src, dst_ref=mine, send_sem=send_sems.at[per * a + k],
                recv_sem=recv_sems.at[per * a + k], device_id=dev, device_id_type=MESH)
                for k, dev in enumerate(to)]
            recvs = [pltpu.make_async_remote_copy(
                src_ref=src, dst_ref=out.at[4 * px + 2 * py + pc], send_sem=send_sems.at[per * a + k],
                recv_sem=recv_sems.at[per * a + k], device_id=(px, py, pc), device_id_type=MESH)
                for k, (px, py, pc) in enumerate(to)]
            starts += [local] + sends
            waits += [local.wait] + [s.wait_send for s in sends] + [r.wait_recv for r in recvs]
        return starts, waits

    shapes = tuple(jax.ShapeDtypeStruct((N_DEV,) + tuple(b.shape[1:] if split else b.shape), b.dtype)
                   for b in blocks)
    return _Comm(tuple(blocks), shapes, per * len(blocks), make)


def _slots(x, y, c):
    return 4 * x + 2 * y + c, 4 * (1 - x) + 2 * y + c, 4 * x + 2 * (1 - y) + c, 4 * (1 - x) + 2 * (1 - y) + c


def _gather2d_first(halves):
    per = 4

    def make(ins, outs, send_sems, recv_sems):
        x, y, c, _ = _mesh_place()
        me, xn, yn, _ = _slots(x, y, c)
        starts, waits = [], []
        for a, (src, out) in enumerate(zip(ins, outs)):
            blk = src.at[c]
            rows = blk.shape[0] // 2
            upper, lower = pl.ds(0, rows), pl.ds(rows, rows)

            def copy(k, src_ref, dst_ref, dev, a=a):
                return pltpu.make_async_remote_copy(
                    src_ref=src_ref, dst_ref=dst_ref, send_sem=send_sems.at[per * a + k],
                    recv_sem=recv_sems.at[per * a + k], device_id=dev, device_id_type=MESH)

            local = pltpu.make_async_copy(blk, out.at[me], send_sems.at[per * a + 3])
            sends = [copy(0, blk, out.at[me], (x, y, 1 - c)),
                     copy(1, blk.at[upper], out.at[me, upper], (1 - x, y, c)),
                     copy(2, blk.at[lower], out.at[me, lower], (x, 1 - y, c))]
            recvs = [copy(0, blk, out.at[4 * x + 2 * y + 1 - c], (x, y, 1 - c)),
                     copy(1, blk.at[upper], out.at[xn, upper], (1 - x, y, c)),
                     copy(2, blk.at[lower], out.at[yn, lower], (x, 1 - y, c))]
            starts += [local] + sends
            waits += [local.wait] + [s.wait_send for s in sends] + [r.wait_recv for r in recvs]
        return starts, waits

    shapes = tuple(jax.ShapeDtypeStruct((N_DEV,) + h.shape[1:], h.dtype) for h in halves)
    return _Comm(tuple(halves), shapes, per * len(halves), make)


def _gather2d_second(bufs):
    per = 4

    def make(ins, outs, send_sems, recv_sems):
        del ins
        x, y, c, _ = _mesh_place()
        me, xn, yn, dg = _slots(x, y, c)
        starts, waits = [], []
        for a, buf in enumerate(outs):
            rows = buf.shape[1] // 2
            upper, lower = pl.ds(0, rows), pl.ds(rows, rows)
            plan = [(me, upper, (x, 1 - y, c), yn), (xn, upper, (x, 1 - y, c), dg),
                    (me, lower, (1 - x, y, c), xn), (yn, lower, (1 - x, y, c), dg)]
            for k, (slot, part, dev, landing) in enumerate(plan):
                sems = dict(send_sem=send_sems.at[per * a + k], recv_sem=recv_sems.at[per * a + k],
                            device_id=dev, device_id_type=MESH)
                send = pltpu.make_async_remote_copy(src_ref=buf.at[slot, part], dst_ref=buf.at[slot, part], **sems)
                arrival = pltpu.make_async_remote_copy(src_ref=buf.at[slot, part], dst_ref=buf.at[landing, part], **sems)
                starts.append(send)
                waits += [send.wait_send, arrival.wait_recv]
        return starts, waits

    shapes = tuple(jax.ShapeDtypeStruct(b.shape, b.dtype) for b in bufs)
    return _Comm(tuple(bufs), shapes, per * len(bufs), make, in_place=True)


def _gather_forward(bufs, name):
    n_arr = len(bufs)

    def body(*refs):
        outs = refs[n_arr:2 * n_arr]
        send_sems, recv_sems = refs[2 * n_arr:]
        x, y, c, chips = _mesh_place()
        sends, recvs = [], []
        for a, buf in enumerate(outs):
            for j, (px, py) in enumerate(chips):
                mine, theirs = buf.at[4 * px + 2 * py + c], buf.at[4 * px + 2 * py + 1 - c]
                sems = dict(send_sem=send_sems.at[3 * a + j], recv_sem=recv_sems.at[3 * a + j],
                            device_id=(x, y, 1 - c), device_id_type=MESH)
                sends.append(pltpu.make_async_remote_copy(src_ref=mine, dst_ref=mine, **sems))
                recvs.append(pltpu.make_async_remote_copy(src_ref=mine, dst_ref=theirs, **sems))
        for cp in sends:
            cp.start()
        for s, r in zip(sends, recvs):
            s.wait_send()
            r.wait_recv()

    return pl.pallas_call(
        body, name=name, out_shape=[jax.ShapeDtypeStruct(b.shape, b.dtype) for b in bufs],
        in_specs=[_any()] * n_arr, out_specs=[_any()] * n_arr,
        input_output_aliases={a: a for a in range(n_arr)},
        scratch_shapes=[pltpu.SemaphoreType.DMA((3 * n_arr,)), pltpu.SemaphoreType.DMA((3 * n_arr,))],
    )(*bufs)


def _scatter_job(chip_sums):
    def make(ins, outs, send_sems, recv_sems):
        x, y, c, chips = _mesh_place()
        copies = [pltpu.make_async_remote_copy(
            src_ref=src.at[2 * px + py], dst_ref=out.at[j], send_sem=send_sems.at[3 * a + j],
            recv_sem=recv_sems.at[3 * a + j], device_id=(px, py, c), device_id_type=MESH)
            for a, (src, out) in enumerate(zip(ins, outs)) for j, (px, py) in enumerate(chips)]
        return copies, [cp.wait for cp in copies]

    return _Comm(tuple(chip_sums), tuple(jax.ShapeDtypeStruct((3,) + s.shape[1:], s.dtype) for s in chip_sums),
                 3 * len(chip_sums), make)


def _run_comm(comm, name):
    def body(token_ref):
        token_ref[...] = jnp.zeros_like(token_ref)

    out = _hosted_call(body, comm, name=name, grid=(1,), in_specs=[],
                       out_shape=[jax.ShapeDtypeStruct((8, LANES), F32)], out_specs=[_full((8, LANES))],
                       semantics=("arbitrary",))()
    return out[1:]


def _all_gather8(blocks, name, split=False, forward=()):
    n_arr, n_fwd = len(blocks), len(forward)

    def body(*refs):
        x_refs, refs = refs[:n_arr], refs[n_arr + n_fwd:]
        out_refs, refs = refs[:n_arr], refs[n_arr:]
        fwd_refs, (send_sems, recv_sems, local_sems) = refs[:n_fwd], refs[n_fwd:]
        x, y, c, chips = _mesh_place()
        me, sibling = (x, y, c), (x, y, 1 - c)
        passing = []
        for f, buf in enumerate(fwd_refs):
            for j, (px, py) in enumerate(chips):
                mine, theirs = buf.at[4 * px + 2 * py + c], buf.at[4 * px + 2 * py + 1 - c]
                sems = dict(send_sem=send_sems.at[7 * n_arr + 3 * f + j], recv_sem=recv_sems.at[7 * n_arr + 3 * f + j],
                            device_id=sibling, device_id_type=MESH)
                passing.append((pltpu.make_async_remote_copy(src_ref=mine, dst_ref=mine, **sems),
                                pltpu.make_async_remote_copy(src_ref=mine, dst_ref=theirs, **sems)))
        for send, _ in passing:
            send.start()
        arrays = []
        for a, (x_ref, out_ref) in enumerate(zip(x_refs, out_refs)):
            src_mine = x_ref.at[c] if split else x_ref

            def copy(k, blk, to, src=None, a=a, out_ref=out_ref):
                dst = out_ref.at[4 * blk[0] + 2 * blk[1] + blk[2]]
                return pltpu.make_async_remote_copy(
                    src_ref=dst if src is None else src, dst_ref=dst,
                    send_sem=send_sems.at[7 * a + k], recv_sem=recv_sems.at[7 * a + k],
                    device_id=to, device_id_type=MESH)

            mine = pltpu.make_async_copy(src_mine, out_ref.at[4 * x + 2 * y + c], local_sems.at[a])
            mine.start()
            first = [copy(0, me, sibling, src=src_mine)]
            first += [copy(1 + j, me, (*chip, c), src=src_mine) for j, chip in enumerate(chips)]
            for cp in first:
                cp.start()
            arrays.append((copy, mine, first))
        sent = []
        for copy, mine, first in arrays:
            passed = [copy(4 + j, (*chip, c), sibling) for j, chip in enumerate(chips)]
            for j, chip in enumerate(chips):
                copy(1 + j, (*chip, c), me).wait_recv()
                passed[j].start()
            sent += first + passed
        for copy, mine, first in arrays:
            copy(0, sibling, me).wait_recv()
            for j, chip in enumerate(chips):
                copy(4 + j, (*chip, 1 - c), me).wait_recv()
            mine.wait()
        for cp in sent:
            cp.wait_send()
        for send, arrival in passing:
            send.wait_send()
            arrival.wait_recv()

    n_sems = 7 * n_arr + 3 * n_fwd
    return pl.pallas_call(
        body, name=name,
        out_shape=[jax.ShapeDtypeStruct((N_DEV,) + tuple(b.shape[1:] if split else b.shape), b.dtype)
                   for b in blocks] + [jax.ShapeDtypeStruct(f.shape, f.dtype) for f in forward],
        in_specs=[_any()] * (n_arr + n_fwd), out_specs=[_any()] * (n_arr + n_fwd),
        input_output_aliases={n_arr + f: n_arr + f for f in range(n_fwd)},
        scratch_shapes=[pltpu.SemaphoreType.DMA((n_sems,)), pltpu.SemaphoreType.DMA((n_sems,)),
                        pltpu.SemaphoreType.DMA((n_arr,))],
    )(*blocks, *forward)


def _sibling_share(bufs, name):
    n_arr = len(bufs)

    def body(*refs):
        out_refs = refs[n_arr:2 * n_arr]
        send_sems, recv_sems = refs[2 * n_arr:]
        x, y, c = lax.axis_index("x"), lax.axis_index("y"), lax.axis_index("c")
        copies = [pltpu.make_async_remote_copy(
            src_ref=out_refs[a].at[c], dst_ref=out_refs[a].at[c],
            send_sem=send_sems.at[a], recv_sem=recv_sems.at[a],
            device_id=(x, y, 1 - c), device_id_type=MESH) for a in range(n_arr)]
        for cp in copies:
            cp.start()
        for a in range(n_arr):
            pltpu.make_async_remote_copy(
                src_ref=out_refs[a].at[c], dst_ref=out_refs[a].at[1 - c],
                send_sem=send_sems.at[a], recv_sem=recv_sems.at[a],
                device_id=(x, y, 1 - c), device_id_type=MESH).wait()

    return pl.pallas_call(
        body, name=name,
        out_shape=[jax.ShapeDtypeStruct(b.shape, b.dtype) for b in bufs],
        in_specs=[_any()] * n_arr, out_specs=[_any()] * n_arr,
        input_output_aliases={a: a for a in range(n_arr)},
        scratch_shapes=[pltpu.SemaphoreType.DMA((n_arr,)), pltpu.SemaphoreType.DMA((n_arr,))],
    )(*bufs)


def _gelu_tanh(z):
    k = math.sqrt(2.0 / math.pi)
    t = jnp.tanh(k * (z + 0.044715 * (z * z * z)))
    return 0.5 * z * (1.0 + t), t


def _gelu_tanh_grad(z, t):
    k = math.sqrt(2.0 / math.pi)
    return 0.5 * (1.0 + t) + 0.5 * z * (1.0 - t * t) * (k * (1.0 + 3.0 * 0.044715 * (z * z)))


def _rope_angle_kernel(pos_row, invf_col):
    seq = pos_row.shape[1]

    def body(p_ref, f_ref, cos_ref, sin_ref):
        ang = p_ref[...].astype(F32) * f_ref[...]
        cos_ref[...] = jnp.cos(ang)
        sin_ref[...] = jnp.sin(ang)

    return pl.pallas_call(
        body, name="rope_angles", grid=(1,), out_shape=[jax.ShapeDtypeStruct((ROT_DIM // 2, seq), F32)] * 2,
        in_specs=[_full((1, seq)), _full((ROT_DIM // 2, 1))], out_specs=[_full((ROT_DIM // 2, seq))] * 2,
        compiler_params=_params("arbitrary"),
    )(pos_row, invf_col)


def _rope_lane_tables(cos, sin):
    cos_t, sin_t = cos.T, sin.T
    seq, half = cos_t.shape
    ones = jnp.ones((seq, HEAD_DIM - ROT_DIM), F32)
    c64 = jnp.concatenate([cos_t, cos_t, ones], axis=1)
    s1 = jnp.concatenate([sin_t, jnp.zeros((seq, HEAD_DIM - half), F32)], axis=1)
    s2 = jnp.concatenate([jnp.zeros((seq, half), F32), sin_t, jnp.zeros((seq, HEAD_DIM - ROT_DIM), F32)], axis=1)
    return jnp.concatenate([jnp.tile(t, (1, LANES // HEAD_DIM)) for t in (c64, s1, s2)], axis=1)


def _rope_apply(t, tab, sign):
    reps = t.shape[1] // LANES
    c_tab, s1, s2 = (jnp.tile(tab[:, LANES * k:LANES * (k + 1)], (1, reps)) if reps > 1
                     else tab[:, LANES * k:LANES * (k + 1)] for k in range(3))
    half = ROT_DIM // 2
    up = pltpu.roll(t, t.shape[1] - half, 1)
    down = pltpu.roll(t, half, 1)
    return t * c_tab + sign * (down * s2 - up * s1)


def _lane_masks(shape):
    lane = lax.broadcasted_iota(jnp.int32, shape, 1)
    return lane < HEAD_DIM, lane >= HEAD_DIM


HEADS_PER_GROUP = N_Q_HEADS // N_KV_HEADS
ATTN_SCALE = 1.0 / math.sqrt(HEAD_DIM)


def _attn_bias_t(first_block):
    kj = lax.broadcasted_iota(jnp.int32, (2 * CHUNK, CHUNK), 0)
    qi = lax.broadcasted_iota(jnp.int32, (2 * CHUNK, CHUNK), 1)
    ok = (kj > qi) & (kj <= qi + CHUNK) & (jnp.logical_not(first_block) | (kj >= CHUNK))
    return jnp.tile(jnp.where(ok, 0.0, -jnp.inf), (1, HEADS_PER_GROUP))


def _group_rows(x, g, lo, hi):
    rows = []
    for r in range(HEADS_PER_GROUP):
        h = HEADS_PER_GROUP * g + r
        pair = x[:, LANES * (h // 2):LANES * (h // 2 + 1)]
        rows.append(jnp.where(hi if h % 2 else lo, pair, 0.0))
    return jnp.concatenate(rows, axis=0)


def _pairs_from_rows(rows, lo):
    return [jnp.where(lo, rows[2 * CHUNK * k:2 * CHUNK * k + CHUNK], rows[2 * CHUNK * k + CHUNK:2 * CHUNK * (k + 1)])
            for k in range(HEADS_PER_GROUP // 2)]


def _group_dup(a, b, g, lo2):
    return jnp.where(lo2, a, b) if g == 0 else jnp.where(lo2, b, a)


def _sink_row(sink_ref, g):
    return jnp.concatenate([sink_ref[HEADS_PER_GROUP * g + r:HEADS_PER_GROUP * g + r + 1, :]
                            for r in range(HEADS_PER_GROUP)], axis=1)


def _attn_probs_t(k_dup, q_rows, bias_t, sink_row):
    s_t = _dot_nt(k_dup, q_rows) * ATTN_SCALE + bias_t
    m = jnp.maximum(jnp.max(s_t, axis=0, keepdims=True), sink_row)
    p = jnp.exp(s_t - m)
    e_sink = jnp.exp(sink_row - m)
    inv = 1.0 / (jnp.sum(p, axis=0, keepdims=True) + e_sink)
    return p * inv, e_sink * inv


def _sgu_forward_pair(wm, vp, j):
    lo, hi = _lane_masks(vp.shape)
    lhs = jnp.concatenate([wm[2 * j], wm[2 * j + 1]], axis=1)
    rhs = jnp.concatenate([jnp.where(lo, vp, 0.0), jnp.where(hi, vp, 0.0)], axis=0)
    return _dot(lhs, rhs)


def _masked_spatial(w_ref):
    t = lax.broadcasted_iota(jnp.int32, (CHUNK, CHUNK), 0)
    s = lax.broadcasted_iota(jnp.int32, (CHUNK, CHUNK), 1)
    tril = s <= t
    return [jnp.where(tril, w_ref[g], 0.0) for g in range(GMLP_GROUPS)], tril, s >= t


def _mod_kernel(c_all, w_shard, b_shard):
    n = w_shard.shape[1]
    tn = 512

    def body(c_ref, w_ref, b_ref, mod_ref, act_ref):
        cv = c_ref[...]
        act = cv * (1.0 / (1.0 + jnp.exp(-cv)))
        act_ref[...] = act
        mod_ref[...] = _dot(act, w_ref[...]) + b_ref[...]

    return pl.pallas_call(
        body, name="ada_mod", grid=(n // tn,),
        out_shape=[jax.ShapeDtypeStruct((N_DEV, n), F32), jax.ShapeDtypeStruct((N_DEV, D_MODEL), F32)],
        in_specs=[_full((N_DEV, D_MODEL)), pl.BlockSpec((D_MODEL, tn), lambda i: (0, i)),
                  pl.BlockSpec((1, tn), lambda i: (0, i))],
        out_specs=[pl.BlockSpec((N_DEV, tn), lambda i: (0, i)), _full((N_DEV, D_MODEL))],
        compiler_params=_params("arbitrary"),
    )(c_all, w_shard, b_shard)


def _in_proj_kernel(x, vecs, w_in_t, comm=None):
    seq = x.shape[0]
    tm = 512

    def body(x_ref, v_ref, w_ref, proj_ref, h_ref):
        xv = x_ref[...]
        rstd = lax.rsqrt(_mean_last(xv * xv) + EPS)
        n1 = (xv * rstd) * v_ref[0:1, :]
        h = n1 * (1.0 + v_ref[2:3, :]) + v_ref[1:2, :]
        hb = h.astype(MXU_DTYPE)
        h_ref[...] = hb
        proj_ref[...] = _dot_nt(hb, w_ref[...])

    return _hosted_call(
        body, comm, name="in_proj", grid=(seq // tm,),
        out_shape=[jax.ShapeDtypeStruct((seq, IN_PROJ_WIDTH), F32),
                   jax.ShapeDtypeStruct((seq, D_MODEL), MXU_DTYPE)],
        in_specs=[pl.BlockSpec((tm, D_MODEL), lambda i: (i, 0)), _full((8, D_MODEL)),
                  _full((IN_PROJ_WIDTH, D_MODEL))],
        out_specs=[pl.BlockSpec((tm, IN_PROJ_WIDTH), lambda i: (i, 0)),
                   pl.BlockSpec((tm, D_MODEL), lambda i: (i, 0))],
        semantics=("arbitrary",),
    )(x, vecs, w_in_t)


def _mixer_fwd_kernel(proj, rope_tab, w_spatial, bias_full, sink_rows, comm=None):
    seq = proj.shape[0]
    nb = seq // CHUNK
    kv_col = (2 * GMLP_WIDTH + ATTN_WIDTH) // (2 * KV_WIDTH)

    def body(proj_ref, prev_ref, tab_ref, ptab_ref, w_ref, bias_ref, sink_ref, cat_ref):
        i = pl.program_id(0)
        wm, _, _ = _masked_spatial(w_ref)
        for j in range(GMLP_GROUPS // 2):
            cols = slice(LANES * j, LANES * (j + 1))
            vcols = slice(GMLP_WIDTH + LANES * j, GMLP_WIDTH + LANES * (j + 1))
            u, _ = _gelu_tanh(proj_ref[:, cols])
            vp, _ = _gelu_tanh(proj_ref[:, vcols])
            sv = _sgu_forward_pair(wm, vp, j) + bias_ref[:, cols]
            cat_ref[:, cols] = (u * sv).astype(cat_ref.dtype)
        o = 2 * GMLP_WIDTH
        tab = tab_ref[...]
        q_r = _rope_apply(proj_ref[:, o:o + ATTN_WIDTH], tab, 1.0)
        k_cur = _rope_apply(proj_ref[:, o + ATTN_WIDTH:o + ATTN_WIDTH + KV_WIDTH], tab, 1.0)
        k_prev = _rope_apply(prev_ref[:, 0:KV_WIDTH], ptab_ref[...], 1.0)
        k_a = jnp.concatenate([k_prev, k_cur], axis=0)
        v_a = jnp.concatenate([prev_ref[:, KV_WIDTH:2 * KV_WIDTH],
                               proj_ref[:, o + ATTN_WIDTH + KV_WIDTH:o + ATTN_WIDTH + 2 * KV_WIDTH]], axis=0)
        k_b = pltpu.roll(k_a, HEAD_DIM, 1)
        v_b = pltpu.roll(v_a, HEAD_DIM, 1)
        bias_t = _attn_bias_t(i == 0)
        lo, hi = _lane_masks((CHUNK, LANES))
        lo2, _ = _lane_masks((2 * CHUNK, LANES))
        for g in range(N_KV_HEADS):
            p_t, _ = _attn_probs_t(_group_dup(k_a, k_b, g, lo2), _group_rows(q_r, g, lo, hi), bias_t,
                                   _sink_row(sink_ref, g))
            o_t = _dot(_group_dup(v_a, v_b, g, lo2).T, p_t)
            for k, pair in enumerate(_pairs_from_rows(o_t.T, lo)):
                c0 = GMLP_WIDTH + LANES * (2 * g + k)
                cat_ref[:, c0:c0 + LANES] = pair.astype(cat_ref.dtype)

    return _hosted_call(
        body, comm, name="mixer_fwd", grid=(nb,),
        out_shape=[jax.ShapeDtypeStruct((seq, D_MODEL), MXU_DTYPE)],
        in_specs=[pl.BlockSpec((CHUNK, IN_PROJ_WIDTH), lambda i: (i, 0)),
                  pl.BlockSpec((CHUNK, 2 * KV_WIDTH), lambda i: (jnp.maximum(i - 1, 0), kv_col)),
                  pl.BlockSpec((CHUNK, 3 * LANES), lambda i: (i, 0)),
                  pl.BlockSpec((CHUNK, 3 * LANES), lambda i: (jnp.maximum(i - 1, 0), 0)),
                  _full((GMLP_GROUPS, CHUNK, CHUNK)), _full((CHUNK, GMLP_WIDTH)),
                  _full((N_Q_HEADS, LANES))],
        out_specs=[pl.BlockSpec((CHUNK, D_MODEL), lambda i: (i, 0))],
        semantics=("arbitrary",),
    )(proj, proj, rope_tab, rope_tab, w_spatial, bias_full, sink_rows)


def _trunk_kernel(x, target, cat, vecs, w_out, w_ff1, w_ff2):
    seq = x.shape[0]
    tm = 256
    nj = D_FF // D_MODEL

    def body(x_ref, t_ref, cat_ref, v_ref, wout_hbm, w1_hbm, w2_hbm,
             dx1_ref, dcat_ref, dmix_ref, h2_ref, r_ref, da_ref, dff_ref, sums_ref,
             wout, w1, w2, a_scr, sem):
        i = pl.program_id(0)

        @pl.when(i == 0)
        def _():
            copies = [pltpu.make_async_copy(wout_hbm, wout, sem.at[0]),
                      pltpu.make_async_copy(w1_hbm, w1, sem.at[1]),
                      pltpu.make_async_copy(w2_hbm, w2, sem.at[2])]
            for cp in copies:
                cp.start()
            for cp in copies:
                cp.wait()
            sums_ref[...] = jnp.zeros_like(sums_ref)

        gate1, shift2, scale2 = v_ref[0:1, :], v_ref[1:2, :], v_ref[2:3, :]
        gate2, g_ffn, g_final = v_ref[3:4, :], v_ref[4:5, :], v_ref[5:6, :]

        mix = _dot(cat_ref[...], wout[...])
        x1 = x_ref[...] + gate1 * mix
        rstd2 = lax.rsqrt(_mean_last(x1 * x1) + EPS)
        xh2 = x1 * rstd2
        n2 = xh2 * g_ffn
        h2b = (n2 * (1.0 + scale2) + shift2).astype(MXU_DTYPE)
        h2_ref[...] = h2b
        ff = jnp.zeros((tm, D_MODEL), F32)
        for j in range(nj):
            a = _dot(h2b, w1[j])
            a_scr[j] = a
            relu = jnp.maximum(a, 0.0)
            rb = (relu * relu).astype(MXU_DTYPE)
            r_ref[:, D_MODEL * j:D_MODEL * (j + 1)] = rb
            ff = ff + _dot(rb, w2[j])
        x2 = x1 + gate2 * ff
        rstd3 = lax.rsqrt(_mean_last(x2 * x2) + EPS)
        xh3 = x2 * rstd3
        err = xh3 * g_final - t_ref[...]
        loss = 0.5 * _rowsum(_mean_last(err * err))
        dy = err * (1.0 / D_MODEL)
        dxh3 = dy * g_final
        dx2 = rstd3 * (dxh3 - xh3 * _mean_last(dxh3 * xh3))
        dffb = (dx2 * gate2).astype(MXU_DTYPE)
        dff_ref[...] = dffb
        dh2 = jnp.zeros((tm, D_MODEL), F32)
        for j in range(nj):
            dr = _dot_nt(dffb, w2[j])
            dab = (dr * (2.0 * jnp.maximum(a_scr[j], 0.0))).astype(MXU_DTYPE)
            da_ref[:, D_MODEL * j:D_MODEL * (j + 1)] = dab
            dh2 = dh2 + _dot_nt(dab, w1[j])
        dn2 = dh2 * (1.0 + scale2)
        dxh2 = dn2 * g_ffn
        dx1 = dx2 + rstd2 * (dxh2 - xh2 * _mean_last(dxh2 * xh2))
        dx1_ref[...] = dx1
        dmixb = (dx1 * gate1).astype(MXU_DTYPE)
        dmix_ref[...] = dmixb
        dcat_ref[...] = _dot_nt(dmixb, wout[...])

        sums_ref[0:1, :] += _rowsum(dh2)
        sums_ref[1:2, :] += _rowsum(dh2 * n2)
        sums_ref[2:3, :] += _rowsum(dx2 * ff)
        sums_ref[3:4, :] += _rowsum(dn2 * xh2)
        sums_ref[4:5, :] += _rowsum(dy * xh3)
        sums_ref[5:6, :] += _rowsum(dx1 * mix)
        sums_ref[6:7, :] += jnp.broadcast_to(loss, (1, D_MODEL))

    tok = lambda w: pl.BlockSpec((tm, w), lambda i: (i, 0))
    return pl.pallas_call(
        body, name="trunk", grid=(seq // tm,),
        out_shape=[jax.ShapeDtypeStruct((seq, D_MODEL), F32), jax.ShapeDtypeStruct((seq, D_MODEL), F32),
                   jax.ShapeDtypeStruct((seq, D_MODEL), MXU_DTYPE), jax.ShapeDtypeStruct((seq, D_MODEL), MXU_DTYPE),
                   jax.ShapeDtypeStruct((seq, D_FF), MXU_DTYPE), jax.ShapeDtypeStruct((seq, D_FF), MXU_DTYPE),
                   jax.ShapeDtypeStruct((seq, D_MODEL), MXU_DTYPE), jax.ShapeDtypeStruct((8, D_MODEL), F32)],
        in_specs=[tok(D_MODEL), tok(D_MODEL), tok(D_MODEL), _full((8, D_MODEL)), _any(), _any(), _any()],
        out_specs=[tok(D_MODEL), tok(D_MODEL), tok(D_MODEL), tok(D_MODEL), tok(D_FF), tok(D_FF), tok(D_MODEL),
                   _full((8, D_MODEL))],
        scratch_shapes=[pltpu.VMEM((D_MODEL, D_MODEL), MXU_DTYPE), pltpu.VMEM((nj, D_MODEL, D_MODEL), MXU_DTYPE),
                        pltpu.VMEM((nj, D_MODEL, D_MODEL), MXU_DTYPE), pltpu.VMEM((nj, tm, D_MODEL), F32),
                        pltpu.SemaphoreType.DMA((3,))],
        compiler_params=_params("arbitrary"),
    )(x, target, cat, vecs, w_out, w_ff1, w_ff2)


def _mixer_bwd_kernel(proj, rope_tab, dcat, w_spatial, w_spatial_t, bias_full, sink_rows, comm=None):
    seq = proj.shape[0]
    nb = seq // CHUNK
    kv_col = (2 * GMLP_WIDTH + ATTN_WIDTH) // (2 * KV_WIDTH)

    def body(proj_ref, prev_ref, tab_ref, ptab_ref, dcat_ref, w_ref, wt_ref, bias_ref, sink_ref,
             dproj_ref, dw_ref, db_ref, dsink_ref, carry):
        step = pl.program_id(0)
        blk = nb - 1 - step

        @pl.when(step == 0)
        def _():
            carry[...] = jnp.zeros_like(carry)
            dw_ref[...] = jnp.zeros_like(dw_ref)
            db_ref[...] = jnp.zeros_like(db_ref)
            dsink_ref[...] = jnp.zeros_like(dsink_ref)

        wm, tril, triu = _masked_spatial(w_ref)
        lo, hi = _lane_masks((CHUNK, LANES))
        lane = lax.broadcasted_iota(jnp.int32, (CHUNK, LANES), 1)
        db = jnp.zeros((CHUNK, LANES), F32)
        for j in range(GMLP_GROUPS // 2):
            cols = slice(LANES * j, LANES * (j + 1))
            vcols = slice(GMLP_WIDTH + LANES * j, GMLP_WIDTH + LANES * (j + 1))
            zu, zv = proj_ref[:, cols], proj_ref[:, vcols]
            u, tu = _gelu_tanh(zu)
            vp, tv = _gelu_tanh(zv)
            sv = _sgu_forward_pair(wm, vp, j) + bias_ref[:, cols]
            dout = dcat_ref[:, cols]
            du = dout * sv
            dsv = dout * u
            dsv_lo, dsv_hi = jnp.where(lo, dsv, 0.0), jnp.where(hi, dsv, 0.0)
            lhs_t = jnp.concatenate([jnp.where(triu, wt_ref[2 * j], 0.0),
                                     jnp.where(triu, wt_ref[2 * j + 1], 0.0)], axis=1)
            dv = _dot(lhs_t, jnp.concatenate([dsv_lo, dsv_hi], axis=0))
            dw_ref[2 * j] += jnp.where(tril, _dot_nt(dsv_lo, vp), 0.0)
            dw_ref[2 * j + 1] += jnp.where(tril, _dot_nt(dsv_hi, vp), 0.0)
            db = db + (jnp.where(lane == 2 * j, jnp.sum(dsv_lo, axis=1, keepdims=True), 0.0)
                       + jnp.where(lane == 2 * j + 1, jnp.sum(dsv_hi, axis=1, keepdims=True), 0.0))
            dproj_ref[:, cols] = (du * _gelu_tanh_grad(zu, tu)).astype(dproj_ref.dtype)
            dproj_ref[:, vcols] = (dv * _gelu_tanh_grad(zv, tv)).astype(dproj_ref.dtype)
        db_ref[...] += db
        o = 2 * GMLP_WIDTH
        tab = tab_ref[...]
        q_r = _rope_apply(proj_ref[:, o:o + ATTN_WIDTH], tab, 1.0)
        k_cur = _rope_apply(proj_ref[:, o + ATTN_WIDTH:o + ATTN_WIDTH + KV_WIDTH], tab, 1.0)
        k_prev = _rope_apply(prev_ref[:, 0:KV_WIDTH], ptab_ref[...], 1.0)
        k_a = jnp.concatenate([k_prev, k_cur], axis=0)
        v_a = jnp.concatenate([prev_ref[:, KV_WIDTH:2 * KV_WIDTH],
                               proj_ref[:, o + ATTN_WIDTH + KV_WIDTH:o + ATTN_WIDTH + 2 * KV_WIDTH]], axis=0)
        k_b = pltpu.roll(k_a, HEAD_DIM, 1)
        v_b = pltpu.roll(v_a, HEAD_DIM, 1)
        bias_t = _attn_bias_t(blk == 0)
        lo2, _ = _lane_masks((2 * CHUNK, LANES))
        dout_b = dcat_ref[:, GMLP_WIDTH:GMLP_WIDTH + ATTN_WIDTH]
        dk_tot, dv_tot, dq_pairs = [], [], []
        for g in range(N_KV_HEADS):
            k_dup, v_dup = _group_dup(k_a, k_b, g, lo2), _group_dup(v_a, v_b, g, lo2)
            q_rows = _group_rows(q_r, g, lo, hi)
            do_rows = _group_rows(dout_b, g, lo, hi)
            p_t, p_sink = _attn_probs_t(k_dup, q_rows, bias_t, _sink_row(sink_ref, g))
            dp_t = _dot_nt(v_dup, do_rows)
            delta = jnp.sum(p_t * dp_t, axis=0, keepdims=True)
            ds_t = p_t * (dp_t - delta) * ATTN_SCALE
            dsink = -p_sink * delta
            for r in range(HEADS_PER_GROUP):
                h = HEADS_PER_GROUP * g + r
                dsink_ref[h:h + 1, :] += jnp.broadcast_to(
                    jnp.sum(dsink[:, LANES * r:LANES * (r + 1)], axis=1, keepdims=True), (1, LANES))
            dk_full = _dot(ds_t, q_rows)
            dv_full = _dot(p_t, do_rows)
            dk_tot.append(dk_full + pltpu.roll(dk_full, HEAD_DIM, 1))
            dv_tot.append(dv_full + pltpu.roll(dv_full, HEAD_DIM, 1))
            dq_t = _dot(k_dup.T, ds_t)
            dq_pairs += _pairs_from_rows(dq_t.T, lo)
        dk_all = jnp.where(lo2, dk_tot[0], dk_tot[1])
        dv_all = jnp.where(lo2, dv_tot[0], dv_tot[1])
        dk_cur = dk_all[CHUNK:, :] + carry[:, 0:KV_WIDTH]
        dv_cur = dv_all[CHUNK:, :] + carry[:, KV_WIDTH:2 * KV_WIDTH]
        carry[:, 0:KV_WIDTH] = dk_all[:CHUNK, :]
        carry[:, KV_WIDTH:2 * KV_WIDTH] = dv_all[:CHUNK, :]
        dq = _rope_apply(jnp.concatenate(dq_pairs, axis=1), tab, -1.0)
        dproj_ref[:, o:o + ATTN_WIDTH] = dq.astype(dproj_ref.dtype)
        dproj_ref[:, o + ATTN_WIDTH:o + ATTN_WIDTH + KV_WIDTH] = (
            _rope_apply(dk_cur, tab, -1.0).astype(dproj_ref.dtype))
        dproj_ref[:, o + ATTN_WIDTH + KV_WIDTH:o + ATTN_WIDTH + 2 * KV_WIDTH] = dv_cur.astype(dproj_ref.dtype)

    rev = lambda i: nb - 1 - i
    return _hosted_call(
        body, comm, name="mixer_bwd", grid=(nb,),
        out_shape=[jax.ShapeDtypeStruct((seq, IN_PROJ_WIDTH), MXU_DTYPE),
                   jax.ShapeDtypeStruct((GMLP_GROUPS, CHUNK, CHUNK), F32),
                   jax.ShapeDtypeStruct((CHUNK, LANES), F32),
                   jax.ShapeDtypeStruct((N_Q_HEADS, LANES), F32)],
        in_specs=[pl.BlockSpec((CHUNK, IN_PROJ_WIDTH), lambda i: (rev(i), 0)),
                  pl.BlockSpec((CHUNK, 2 * KV_WIDTH), lambda i: (jnp.maximum(rev(i) - 1, 0), kv_col)),
                  pl.BlockSpec((CHUNK, 3 * LANES), lambda i: (rev(i), 0)),
                  pl.BlockSpec((CHUNK, 3 * LANES), lambda i: (jnp.maximum(rev(i) - 1, 0), 0)),
                  pl.BlockSpec((CHUNK, D_MODEL), lambda i: (rev(i), 0)),
                  _full((GMLP_GROUPS, CHUNK, CHUNK)), _full((GMLP_GROUPS, CHUNK, CHUNK)),
                  _full((CHUNK, GMLP_WIDTH)), _full((N_Q_HEADS, LANES))],
        out_specs=[pl.BlockSpec((CHUNK, IN_PROJ_WIDTH), lambda i: (rev(i), 0)),
                   _full((GMLP_GROUPS, CHUNK, CHUNK)), _full((CHUNK, LANES)), _full((N_Q_HEADS, LANES))],
        scratch_shapes=[pltpu.VMEM((CHUNK, 2 * KV_WIDTH), F32)],
        semantics=("arbitrary",),
    )(proj, proj, rope_tab, rope_tab, dcat, w_spatial, w_spatial_t, bias_full, sink_rows)


def _in_proj_bwd_kernel(x, dx1, dproj, vecs, w_in_t, comm=None):
    seq = x.shape[0]
    tm = 512

    def body(x_ref, dx1_ref, dp_ref, v_ref, w_ref, gx_ref, sums_ref):
        @pl.when(pl.program_id(0) == 0)
        def _():
            sums_ref[...] = jnp.zeros_like(sums_ref)

        g_mix, scale1 = v_ref[0:1, :], v_ref[2:3, :]
        dh = _dot(dp_ref[...], w_ref[...])
        xv = x_ref[...]
        rstd = lax.rsqrt(_mean_last(xv * xv) + EPS)
        xh = xv * rstd
        dn1 = dh * (1.0 + scale1)
        dxh = dn1 * g_mix
        gx_ref[...] = dx1_ref[...] + rstd * (dxh - xh * _mean_last(dxh * xh))
        sums_ref[0:1, :] += _rowsum(dh)
        sums_ref[1:2, :] += _rowsum(dh * (xh * g_mix))
        sums_ref[2:3, :] += _rowsum(dn1 * xh)

    return _hosted_call(
        body, comm, name="in_proj_bwd", grid=(seq // tm,),
        out_shape=[jax.ShapeDtypeStruct((seq, D_MODEL), F32), jax.ShapeDtypeStruct((8, D_MODEL), F32)],
        in_specs=[pl.BlockSpec((tm, D_MODEL), lambda i: (i, 0)), pl.BlockSpec((tm, D_MODEL), lambda i: (i, 0)),
                  pl.BlockSpec((tm, IN_PROJ_WIDTH), lambda i: (i, 0)), _full((8, D_MODEL)),
                  _full((IN_PROJ_WIDTH, D_MODEL))],
        out_specs=[pl.BlockSpec((tm, D_MODEL), lambda i: (i, 0)), _full((8, D_MODEL))],
        semantics=("arbitrary",),
    )(x, dx1, dproj, vecs, w_in_t)


class _GradTiles(NamedTuple):
    tm: int
    tn: int
    n_tiles: int
    chips_per_tile: int
    a_index: Callable
    b_index: Callable


def _weight_grad_kernel(a, b, c_idx, name, tiles, comm=None):
    seq = a.shape[0]
    tk = min(seq, 4096)
    nk = seq // tk
    tm, tn, n_tiles, per = tiles.tm, tiles.tn, tiles.n_tiles, tiles.chips_per_tile
    rows = tm // per

    def half(phase, c):
        return phase * c[0] + (1 - phase) * (1 - c[0])

    def body(c_ref, a_ref, b_ref, o_ref, acc, stage, landed, send_sems, recv_sems):
        del c_ref
        phase, t, kk = pl.program_id(0), pl.program_id(1), pl.program_id(2)
        x, y, c, _ = _mesh_place()

        def copy(tile):
            return pltpu.make_async_remote_copy(
                src_ref=stage.at[tile], dst_ref=landed.at[tile], send_sem=send_sems.at[tile],
                recv_sem=recv_sems.at[tile], device_id=(x, y, 1 - c), device_id_type=MESH)

        @pl.when(kk == 0)
        def _():
            acc[...] = jnp.zeros_like(acc)

        acc[...] += _dot_tn(a_ref[...], b_ref[...])

        @pl.when((kk == nk - 1) & (phase == 0))
        def _():
            stage[t] = acc[...].astype(stage.dtype)
            copy(t).start()

        @pl.when((kk == nk - 1) & (phase == 1))
        def _():
            copy(t).wait_recv()
            total = acc[...] + landed[t].astype(F32)
            for q in range(per):
                o_ref[q] = total[rows * q:rows * (q + 1)].astype(o_ref.dtype)

        @pl.when((kk == nk - 1) & (phase == 1) & (t == n_tiles - 1))
        def _():
            for tile in range(n_tiles):
                copy(tile).wait_send()

    out = _hosted_call(
        body, comm, name=name, grid=(2, n_tiles, nk), n_prefetch=1,
        out_shape=[jax.ShapeDtypeStruct((n_tiles * per, rows, tn), GRAD_COMM_DTYPE)],
        in_specs=[pl.BlockSpec((tk, tm), lambda p, t, k, c: (k, tiles.a_index(t, half(p, c)))),
                  pl.BlockSpec((tk, tn), lambda p, t, k, c: (k, tiles.b_index(t, half(p, c))))],
        out_specs=[pl.BlockSpec((per, rows, tn), lambda p, t, k, c: (p * t, 0, 0))],
        scratch_shapes=[pltpu.VMEM((tm, tn), F32), pltpu.VMEM((n_tiles, tm, tn), GRAD_COMM_DTYPE),
                        pltpu.VMEM((n_tiles, tm, tn), GRAD_COMM_DTYPE),
                        pltpu.SemaphoreType.DMA((n_tiles,)), pltpu.SemaphoreType.DMA((n_tiles,))],
        semantics=("arbitrary", "arbitrary", "arbitrary"),
    )(c_idx, a, b)
    return out[0] if comm is None else out


def _row_tile(rows, most=256, sublanes=16):
    return max(t for t in range(sublanes, most + 1, sublanes) if rows % t == 0)


def _adam_update(w, g, m, v):
    m_new = ADAM_B1 * m + (1.0 - ADAM_B1) * g
    v_new = ADAM_B2 * v + (1.0 - ADAM_B2) * (g * g)
    m_hat = m_new / (1.0 - ADAM_B1 ** ADAM_STEP)
    v_hat = v_new / (1.0 - ADAM_B2 ** ADAM_STEP)
    delta = -ADAM_LR * (m_hat / (jnp.sqrt(v_hat) + ADAM_EPS) + ADAM_WD * w)
    return delta, m_new, v_new


def _sum_chips_kernel(own, others, place, name):
    _, r, n = own.shape
    tr = _row_tile(r)

    def body(place_ref, own_ref, oth_ref, o_ref):
        del place_ref
        acc = own_ref[...].astype(F32)
        for k in range(N_CHIPS - 1):
            acc = acc + oth_ref[k].astype(F32)
        o_ref[...] = acc

    return pl.pallas_call(
        body, name=name, out_shape=jax.ShapeDtypeStruct((2, r, n), F32),
        grid_spec=pltpu.PrefetchScalarGridSpec(
            num_scalar_prefetch=1, grid=(r // tr,),
            in_specs=[pl.BlockSpec((None, tr, n), lambda i, p: (p[0], i, 0)),
                      pl.BlockSpec((N_CHIPS - 1, tr, n), lambda i, p: (0, i, 0))],
            out_specs=pl.BlockSpec((None, tr, n), lambda i, p: (p[1], i, 0))),
        compiler_params=_params("parallel"),
    )(place, own, others)


def _adam_kernel(w, g, m, v, name):
    r, n = w.shape
    by_columns = g.shape[1] == r
    tr, tn = _row_tile(g.shape[1]), g.shape[2]

    def body(w_ref, g_ref, m_ref, v_ref, g_out, d_ref, mo_ref, vo_ref):
        gv = g_ref[...]
        g_out[...] = gv
        d_ref[...], mo_ref[...], vo_ref[...] = _adam_update(w_ref[...], gv, m_ref[...], v_ref[...])

    steps = g.shape[1] // tr
    spec = pl.BlockSpec((tr, tn), (lambda h, i: (i, h)) if by_columns else (lambda h, i: (h * steps + i, 0)))
    return pl.pallas_call(
        body, name=name, grid=(2, steps), out_shape=[jax.ShapeDtypeStruct((r, n), F32)] * 4,
        in_specs=[spec, pl.BlockSpec((None, tr, tn), lambda h, i: (h, i, 0)), spec, spec], out_specs=[spec] * 4,
        compiler_params=_params("parallel", "parallel"),
    )(w, g, m, v)


SMALL_PARAMS = ("b_ada", "g_mix", "g_ffn", "g_final", "b_spatial", "sinks", "w_spatial")


def _small_update_kernel(gathered, params):
    shapes = [params[nm][0].shape for nm in SMALL_PARAMS]

    def body(*refs):
        g_refs, refs = refs[:5], refs[5:]
        p_refs, refs = refs[:3 * len(SMALL_PARAMS)], refs[3 * len(SMALL_PARAMS):]
        loss_ref, o_refs = refs[0], refs[1:]

        def total(ref):
            acc = ref[0]
            for k in range(1, N_DEV):
                acc = acc + ref[k]
            return acc

        s1, s2, db, ds, dw = (total(r) for r in g_refs)
        loss_ref[...] = jnp.broadcast_to(s2[6:7, 0:1], loss_ref.shape)
        grads = {"b_ada": [s1[0:1], s1[1:2], s2[5:6], s2[0:1], s2[1:2], s2[2:3]], "g_mix": [s1[2:3]],
                 "g_ffn": [s2[3:4]], "g_final": [s2[4:5]], "b_spatial": [db.T[0:GMLP_GROUPS]],
                 "w_spatial": [dw]}
        lane = lax.broadcasted_iota(jnp.int32, (1, LANES), 1)
        sink_row = jnp.zeros((1, LANES), F32)
        for h in range(N_Q_HEADS):
            sink_row = sink_row + jnp.where(lane == h, ds[h:h + 1, :], 0.0)
        grads["sinks"] = [sink_row[:, 0:N_Q_HEADS]]
        for i, nm in enumerate(SMALL_PARAMS):
            w_ref, m_ref, v_ref = p_refs[3 * i:3 * i + 3]
            outs = o_refs[4 * i:4 * i + 4]
            width = grads[nm][0].shape[1]
            for k, g in enumerate(grads[nm]):
                cols = slice(width * k, width * (k + 1))
                upd = _adam_update(w_ref[:, cols], g, m_ref[:, cols], v_ref[:, cols])
                for o_ref, val in zip(outs, (g,) + upd):
                    o_ref[:, cols] = val

    flat = [a for nm in SMALL_PARAMS for a in params[nm]]
    out_shape = [jax.ShapeDtypeStruct((8, LANES), F32)]
    out_shape += [jax.ShapeDtypeStruct(s, F32) for s in shapes for _ in range(4)]
    outs = pl.pallas_call(
        body, name="small_update", grid=(1,), out_shape=out_shape,
        in_specs=[_full(g.shape) for g in gathered] + [_full(a.shape) for a in flat],
        out_specs=[_full(s.shape) for s in out_shape],
        compiler_params=_params("arbitrary"),
    )(*gathered, *flat)
    return {nm: outs[1 + 4 * i:5 + 4 * i] for i, nm in enumerate(SMALL_PARAMS)}, outs[0]


def _ada_update_kernel(act_t, dmod, w, m, v):
    r, n = w.shape
    tr = 256

    def body(a_ref, d_ref, w_ref, m_ref, v_ref, g_ref, dl_ref, mo_ref, vo_ref):
        g = _dot(a_ref[...], d_ref[...])
        g_ref[...] = g
        dl_ref[...], mo_ref[...], vo_ref[...] = _adam_update(w_ref[...], g, m_ref[...], v_ref[...])

    spec = pl.BlockSpec((tr, n), lambda i: (i, 0))
    return pl.pallas_call(
        body, name="ada_update", grid=(r // tr,), out_shape=[jax.ShapeDtypeStruct((r, n), F32)] * 4,
        in_specs=[pl.BlockSpec((tr, N_DEV), lambda i: (i, 0)), _full((N_DEV, n)), spec, spec, spec],
        out_specs=[spec] * 4, compiler_params=_params("parallel"),
    )(act_t, dmod, w, m, v)


def kernel(x, c, positions, w_ada, b_ada, g_mix, w_in, w_spatial, b_spatial, sinks, w_out, g_ffn, w_ff1, w_ff2, g_final, loss_target, m_w_ada, m_b_ada, m_g_mix, m_w_in, m_w_spatial, m_b_spatial, m_sinks, m_w_out, m_g_ffn, m_w_ff1, m_w_ff2, m_g_final, v_w_ada, v_b_ada, v_g_mix, v_w_in, v_w_spatial, v_b_spatial, v_sinks, v_w_out, v_g_ffn, v_w_ff1, v_w_ff2, v_g_final):
    xi, yi, ci = lax.axis_index("x"), lax.axis_index("y"), lax.axis_index("c")
    chip = 2 * xi + yi
    dev = 2 * chip + ci
    seq = x.shape[1]
    x2, tgt = x[0], loss_target[0]
    ada_cols = w_ada.shape[2]

    c_all = _all_gather8([c], "gather_c")[0].reshape(N_DEV, D_MODEL)
    b_shard = lax.dynamic_slice(b_ada, (0, chip * ada_cols), (1, ada_cols))
    mod_part, act = _mod_kernel(c_all, w_ada[0], b_shard)
    mod_all, = _all_gather8([mod_part], "gather_mod")
    mod_me = lax.dynamic_index_in_dim(mod_all[0::2], dev, axis=1, keepdims=False)
    mod_me = mod_me.reshape(N_MOD, D_MODEL)
    shift1, scale1, gate1, shift2, scale2, gate2 = (mod_me[k:k + 1] for k in range(N_MOD))

    big = {"w_in": tuple(a[0].T for a in (w_in, m_w_in, v_w_in)),
           "w_out": (w_out[0], m_w_out[0], v_w_out[0]), "w_ff1": (w_ff1[0], m_w_ff1[0], v_w_ff1[0]),
           "w_ff2": (w_ff2[0], m_w_ff2[0], v_w_ff2[0])}

    def halves(nm):
        r, n = big[nm][0].shape
        return big[nm][0].astype(WEIGHT_COMM_DTYPE).reshape(2, r // 2, n)

    w_in_t, = _all_gather8([halves("w_in")], "gather_w_in", split=True)
    w_in_t = w_in_t.reshape(IN_PROJ_WIDTH, D_MODEL)

    zeros_row = jnp.zeros((1, D_MODEL), F32)
    vecs1 = jnp.concatenate([g_mix, shift1, scale1] + [zeros_row] * 5, axis=0)
    vecs2 = jnp.concatenate([gate1, shift2, scale2, gate2, g_ffn, g_final.reshape(1, D_MODEL)]
                            + [zeros_row] * 2, axis=0)
    bias_full = jnp.repeat(b_spatial[0].T, HEAD_DIM, axis=1)
    sink_rows = jnp.broadcast_to(sinks[0][:, None], (N_Q_HEADS, LANES))
    inv_freq = ROPE_THETA ** (-jnp.arange(0, ROT_DIM, 2, dtype=F32) / ROT_DIM)
    rope_tab = _rope_lane_tables(*_rope_angle_kernel(positions, inv_freq.reshape(ROT_DIM // 2, 1)))

    proj, hb, *staged = _in_proj_kernel(
        x2, vecs1, w_in_t, comm=_gather2d_first([halves("w_ff1"), halves("w_ff2"), halves("w_out")]))
    cat, *staged = _mixer_fwd_kernel(proj, rope_tab, w_spatial[0], bias_full, sink_rows,
                                     comm=_gather2d_second(staged))
    g_ff1, g_ff2, g_out = _gather_forward(staged, "gather_forward")
    w_out_full = g_out.reshape(D_MODEL, D_MODEL)
    w_ff1_blocks = g_ff1.reshape(N_CHIPS, D_MODEL, D_MODEL)
    w_ff2_blocks = g_ff2.reshape(N_CHIPS, D_MODEL, D_MODEL)
    dx1, dcat, dmix, h2b, rb, dab, dffb, sums2 = _trunk_kernel(
        x2, tgt, cat, vecs2, w_out_full, w_ff1_blocks, w_ff2_blocks)

    c_idx = ci.reshape(1).astype(jnp.int32)
    place = jnp.stack([chip, ci]).astype(jnp.int32)
    half_d = D_MODEL // 2
    cs_ff2 = _weight_grad_kernel(rb, dffb, c_idx, "dw_ff2",
                                 _GradTiles(D_MODEL, half_d, N_CHIPS, 1, lambda t, h: t, lambda t, h: h))
    cs_ff1, sc_ff2 = _weight_grad_kernel(
        h2b, dab, c_idx, "dw_ff1",
        _GradTiles(D_MODEL, half_d, N_CHIPS, 1, lambda t, h: 0, lambda t, h: 2 * t + h),
        comm=_scatter_job([cs_ff2]))
    cs_out = _weight_grad_kernel(cat, dmix, c_idx, "dw_out",
                                 _GradTiles(D_MODEL, half_d, 1, N_CHIPS, lambda t, h: 0, lambda t, h: h))
    dproj, dw_spatial, db_lanes, dsink_rows, sc_ff1, sc_out = _mixer_bwd_kernel(
        proj, rope_tab, dcat, w_spatial[0], w_spatial[0].transpose(0, 2, 1), bias_full, sink_rows,
        comm=_scatter_job([cs_ff1, cs_out]))
    cs_in, *small_stage1 = _weight_grad_kernel(
        dproj, hb, c_idx, "dw_in",
        _GradTiles(2 * W_IN_BLOCK, half_d, N_CHIPS // 2, 2, lambda t, h: t, lambda t, h: h),
        comm=_gather_job([db_lanes, dsink_rows, dw_spatial.reshape(GMLP_GROUPS * CHUNK, CHUNK)], split=False))
    grad_x, sums1 = _in_proj_bwd_kernel(x2, dx1, dproj, vecs1, w_in_t)
    sc_in, = _run_comm(_scatter_job([cs_in]), "grad_to_chips_w_in")

    names = ["w_in", "w_out", "w_ff1", "w_ff2"]
    totals = [_sum_chips_kernel(own, oth, place, "grad_sum_" + nm)
              for nm, own, oth in zip(names, [cs_in, cs_out, cs_ff1, cs_ff2], [sc_in, sc_out, sc_ff1, sc_ff2])]
    shared = _sibling_share(totals, "grad_share")
    big_out = {}
    for nm, g in zip(names, shared):
        w, m, v = big[nm]
        outs = _adam_kernel(w, g, m, v, "adam_" + nm)
        big_out[nm] = tuple((t.T if nm == "w_in" else t)[None] for t in outs)

    small = {"b_ada": (b_ada, m_b_ada, v_b_ada), "g_mix": (g_mix, m_g_mix, v_g_mix),
             "g_ffn": (g_ffn, m_g_ffn, v_g_ffn), "g_final": (g_final, m_g_final, v_g_final),
             "b_spatial": (b_spatial, m_b_spatial, v_b_spatial), "sinks": (sinks, m_sinks, v_sinks),
             "w_spatial": (w_spatial, m_w_spatial, v_w_spatial)}
    flat_shape = {"g_final": (1, D_MODEL), "b_spatial": (GMLP_GROUPS, CHUNK), "w_spatial": (GMLP_GROUPS * CHUNK, CHUNK)}
    gathered = _all_gather8([sums1, sums2], "gather_small", forward=small_stage1)
    small_out, loss_tile = _small_update_kernel(
        gathered, {nm: tuple(a.reshape(flat_shape.get(nm, a.shape)) for a in small[nm]) for nm in small})
    small_out = {nm: [o.reshape(small[nm][0].shape) for o in small_out[nm]] for nm in small}
    loss = loss_tile[0, 0]

    g1, g2 = gathered[0], gathered[1]
    dmod_all = jnp.concatenate([g1[:, 0], g1[:, 1], g2[:, 5], g2[:, 0], g2[:, 1], g2[:, 2]], axis=1)
    dmod_cols = lax.dynamic_slice(dmod_all, (0, chip * ada_cols), (N_DEV, ada_cols))
    ada = _ada_update_kernel(act.T, dmod_cols, w_ada[0], m_w_ada[0], v_w_ada[0])
    big_out["w_ada"] = tuple(t[None] for t in ada)

    order = ["w_ada", "b_ada", "g_mix", "w_in", "w_spatial", "b_spatial", "sinks", "w_out", "g_ffn",
             "w_ff1", "w_ff2", "g_final"]

    def leaf(nm, k):
        return big_out[nm][k] if nm in big_out else small_out[nm][k]

    outs = [loss, grad_x[None]]
    for k in range(4):
        outs += [leaf(nm, k) for nm in order]
    return tuple(outs)
```

```python
import math
from typing import Callable, NamedTuple

import jax
import jax.numpy as jnp
from jax import lax
from jax.experimental import pallas as pl
from jax.experimental.pallas import tpu as pltpu

F32 = jnp.float32
MXU_DTYPE = jnp.bfloat16
WEIGHT_COMM_DTYPE = jnp.bfloat16
GRAD_COMM_DTYPE = jnp.bfloat16

D_MODEL = 1024
D_FF = 4096
HEAD_DIM = 64
GMLP_GROUPS = 8
GMLP_WIDTH = 512
CHUNK = 128
N_Q_HEADS = 8
N_KV_HEADS = 2
ATTN_WIDTH = 512
KV_WIDTH = 128
ROT_DIM = 16
ROPE_THETA = 500000.0
IN_PROJ_WIDTH = 1792
N_MOD = 6
EPS = 1e-5
N_CHIPS = 4
N_DEV = 8
LANES = 128
W_IN_BLOCK = IN_PROJ_WIDTH // N_CHIPS

ADAM_LR = 0.001
ADAM_B1 = 0.9
ADAM_B2 = 0.999
ADAM_EPS = 1e-08
ADAM_WD = 0.01
ADAM_STEP = 10

VMEM_LIMIT_BYTES = 58 * 1024 * 1024
MESH = pl.DeviceIdType.MESH


def _params(*semantics):
    return pltpu.CompilerParams(dimension_semantics=semantics, vmem_limit_bytes=VMEM_LIMIT_BYTES)


def _dot(a, b):
    return jnp.dot(a.astype(MXU_DTYPE), b.astype(MXU_DTYPE), preferred_element_type=F32)


def _dot_nt(a, b):
    return lax.dot_general(a.astype(MXU_DTYPE), b.astype(MXU_DTYPE), (((1,), (1,)), ((), ())),
                           preferred_element_type=F32)


def _dot_tn(a, b):
    return lax.dot_general(a.astype(MXU_DTYPE), b.astype(MXU_DTYPE), (((0,), (0,)), ((), ())),
                           preferred_element_type=F32)


def _full(shape):
    return pl.BlockSpec(shape, lambda *_: (0,) * len(shape))


def _any():
    return pl.BlockSpec(memory_space=pl.ANY)


def _rowsum(v):
    return jnp.sum(v, axis=0, keepdims=True)


def _mean_last(v):
    return jnp.mean(v, axis=-1, keepdims=True)


class _Comm(NamedTuple):
    operands: tuple
    out_shapes: tuple
    n_sems: int
    make: Callable
    in_place: int = 0


def _hosted_call(body, comm, *, name, grid, in_specs, out_shape, out_specs, scratch_shapes=(), semantics,
                 n_prefetch=0):
    if comm is None:
        return pl.pallas_call(
            body, name=name, out_shape=out_shape, compiler_params=_params(*semantics),
            grid_spec=pltpu.PrefetchScalarGridSpec(
                num_scalar_prefetch=n_prefetch, grid=grid, in_specs=in_specs, out_specs=out_specs,
                scratch_shapes=list(scratch_shapes)))
    n_in, n_out, n_scr = len(in_specs), len(out_shape), len(scratch_shapes)
    k_in, k_out = len(comm.operands), len(comm.out_shapes)

    def hosted(*refs):
        prefetched, refs = refs[:n_prefetch], refs[n_prefetch:]
        ins, refs = refs[:n_in], refs[n_in:]
        c_ins, refs = refs[:k_in], refs[k_in:]
        outs, refs = refs[:n_out], refs[n_out:]
        c_outs, refs = refs[:k_out], refs[k_out:]
        scratch, (send_sems, recv_sems) = refs[:n_scr], refs[n_scr:]
        first, last = None, None
        for d, size in enumerate(grid):
            at_start, at_end = pl.program_id(d) == 0, pl.program_id(d) == size - 1
            first = at_start if first is None else first & at_start
            last = at_end if last is None else last & at_end

        @pl.when(first)
        def _():
            for cp in comm.make(c_ins, c_outs, send_sems, recv_sems)[0]:
                cp.start()

        body(*prefetched, *ins, *outs, *scratch)

        @pl.when(last)
        def _():
            for wait in comm.make(c_ins, c_outs, send_sems, recv_sems)[1]:
                wait()

    aliases = {n_prefetch + n_in + i: n_out + i for i in range(comm.in_place)}
    call = pl.pallas_call(
        hosted, name=name, out_shape=list(out_shape) + list(comm.out_shapes),
        compiler_params=_params(*semantics), input_output_aliases=aliases,
        grid_spec=pltpu.PrefetchScalarGridSpec(
            num_scalar_prefetch=n_prefetch, grid=grid, in_specs=list(in_specs) + [_any()] * k_in,
            out_specs=list(out_specs) + [_any()] * k_out,
            scratch_shapes=list(scratch_shapes) + [pltpu.SemaphoreType.DMA((comm.n_sems,)),
                                                    pltpu.SemaphoreType.DMA((comm.n_sems,))]))
    return lambda *args: call(*args, *comm.operands)


def _mesh_place():
    x, y, c = lax.axis_index("x"), lax.axis_index("y"), lax.axis_index("c")
    return x, y, c, [(1 - x, y), (x, 1 - y), (1 - x, 1 - y)]


def _gather_job(blocks, split=True):
    per = 5

    def make(ins, outs, send_sems, recv_sems):
        x, y, c, chips = _mesh_place()
        starts, waits = [], []
        for a, (src, out) in enumerate(zip(ins, outs)):
            src = src.at[c] if split else src
            mine = out.at[4 * x + 2 * y + c]
            local = pltpu.make_async_copy(src, mine, send_sems.at[per * a + 4])
            to = [(x, y, 1 - c)] + [(px, py, c) for px, py in chips]
            sends = [pltpu.make_async_remote_copy(
                src_ref=src, dst_ref=mine, send_sem=send_sems.at[per * a + k],
                recv_sem=recv_sems.at[per * a + k], device_id=dev, device_id_type=MESH)
                for k, dev in enumerate(to)]
            recvs = [pltpu.make_async_remote_copy(
                src_ref=src, dst_ref=out.at[4 * px + 2 * py + pc], send_sem=send_sems.at[per * a + k],
                recv_sem=recv_sems.at[per * a + k], device_id=(px, py, pc), device_id_type=MESH)
                for k, (px, py, pc) in enumerate(to)]
            starts += [local] + sends
            waits += [local.wait] + [s.wait_send for s in sends] + [r.wait_recv for r in recvs]
        return starts, waits

    shapes = tuple(jax.ShapeDtypeStruct((N_DEV,) + tuple(b.shape[1:] if split else b.shape), b.dtype)
                   for b in blocks)
    return _Comm(tuple(blocks), shapes, per * len(blocks), make)


def _slots(x, y, c):
    return 4 * x + 2 * y + c, 4 * (1 - x) + 2 * y + c, 4 * x + 2 * (1 - y) + c, 4 * (1 - x) + 2 * (1 - y) + c


def _gather2d_first(halves):
    per = 2

    def make(ins, outs, send_sems, recv_sems):
        x, y, c, _ = _mesh_place()
        me, xn, yn, _ = _slots(x, y, c)
        starts, waits = [], []
        for a, (src, out) in enumerate(zip(ins, outs)):
            blk = src.at[c]
            rows = blk.shape[0] // 2
            upper, lower = pl.ds(0, rows), pl.ds(rows, rows)

            def copy(k, src_ref, dst_ref, dev, a=a):
                return pltpu.make_async_remote_copy(
                    src_ref=src_ref, dst_ref=dst_ref, send_sem=send_sems.at[per * a + k],
                    recv_sem=recv_sems.at[per * a + k], device_id=dev, device_id_type=MESH)

            sends = [copy(0, blk.at[upper], out.at[me, upper], (1 - x, y, c)),
                     copy(1, blk.at[lower], out.at[me, lower], (x, 1 - y, c))]
            recvs = [copy(0, blk.at[upper], out.at[xn, upper], (1 - x, y, c)),
                     copy(1, blk.at[lower], out.at[yn, lower], (x, 1 - y, c))]
            starts += sends
            waits += [s.wait_send for s in sends] + [r.wait_recv for r in recvs]
        return starts, waits

    shapes = tuple(jax.ShapeDtypeStruct((N_DEV,) + h.shape[1:], h.dtype) for h in halves)
    return _Comm(tuple(halves), shapes, per * len(halves), make)


def _gather2d_second(bufs, halves):
    per = 4
    n_arr = len(bufs)

    def make(ins, outs, send_sems, recv_sems):
        x, y, c, _ = _mesh_place()
        me, xn, yn, dg = _slots(x, y, c)
        starts, waits = [], []
        for a, buf in enumerate(outs):
            own = ins[n_arr + a].at[c]
            rows = buf.shape[1] // 2
            upper, lower = pl.ds(0, rows), pl.ds(rows, rows)
            plan = [(own.at[upper], me, upper, (x, 1 - y, c), yn), (buf.at[xn, upper], xn, upper, (x, 1 - y, c), dg),
                    (own.at[lower], me, lower, (1 - x, y, c), xn), (buf.at[yn, lower], yn, lower, (1 - x, y, c), dg)]
            for k, (src, slot, part, dev, landing) in enumerate(plan):
                sems = dict(send_sem=send_sems.at[per * a + k], recv_sem=recv_sems.at[per * a + k],
                            device_id=dev, device_id_type=MESH)
                send = pltpu.make_async_remote_copy(src_ref=src, dst_ref=buf.at[slot, part], **sems)
                arrival = pltpu.make_async_remote_copy(src_ref=src, dst_ref=buf.at[landing, part], **sems)
                starts.append(send)
                waits += [send.wait_send, arrival.wait_recv]
        return starts, waits

    shapes = tuple(jax.ShapeDtypeStruct(b.shape, b.dtype) for b in bufs)
    return _Comm(tuple(bufs) + tuple(halves), shapes, per * n_arr, make, in_place=n_arr)


def _gather_forward(bufs, name):
    n_arr = len(bufs)

    def body(*refs):
        outs = refs[n_arr:2 * n_arr]
        send_sems, recv_sems = refs[2 * n_arr:]
        x, y, c, chips = _mesh_place()
        sends, recvs = [], []
        for a, buf in enumerate(outs):
            for j, (px, py) in enumerate(chips):
                mine, theirs = buf.at[4 * px + 2 * py + c], buf.at[4 * px + 2 * py + 1 - c]
                sems = dict(send_sem=send_sems.at[3 * a + j], recv_sem=recv_sems.at[3 * a + j],
                            device_id=(x, y, 1 - c), device_id_type=MESH)
                sends.append(pltpu.make_async_remote_copy(src_ref=mine, dst_ref=mine, **sems))
                recvs.append(pltpu.make_async_remote_copy(src_ref=mine, dst_ref=theirs, **sems))
        for cp in sends:
            cp.start()
        for s, r in zip(sends, recvs):
            s.wait_send()
            r.wait_recv()

    return pl.pallas_call(
        body, name=name, out_shape=[jax.ShapeDtypeStruct(b.shape, b.dtype) for b in bufs],
        in_specs=[_any()] * n_arr, out_specs=[_any()] * n_arr,
        input_output_aliases={a: a for a in range(n_arr)},
        scratch_shapes=[pltpu.SemaphoreType.DMA((3 * n_arr,)), pltpu.SemaphoreType.DMA((3 * n_arr,))],
    )(*bufs)


def _scatter_job(chip_sums):
    def make(ins, outs, send_sems, recv_sems):
        x, y, c, chips = _mesh_place()
        copies = [pltpu.make_async_remote_copy(
            src_ref=src.at[2 * px + py], dst_ref=out.at[j], send_sem=send_sems.at[3 * a + j],
            recv_sem=recv_sems.at[3 * a + j], device_id=(px, py, c), device_id_type=MESH)
            for a, (src, out) in enumerate(zip(ins, outs)) for j, (px, py) in enumerate(chips)]
        return copies, [cp.wait for cp in copies]

    return _Comm(tuple(chip_sums), tuple(jax.ShapeDtypeStruct((3,) + s.shape[1:], s.dtype) for s in chip_sums),
                 3 * len(chip_sums), make)


def _run_comm(comm, name):
    def body(token_ref):
        token_ref[...] = jnp.zeros_like(token_ref)

    out = _hosted_call(body, comm, name=name, grid=(1,), in_specs=[],
                       out_shape=[jax.ShapeDtypeStruct((8, LANES), F32)], out_specs=[_full((8, LANES))],
                       semantics=("arbitrary",))()
    return out[1:]


def _all_gather8(blocks, name, split=False, forward=()):
    n_arr, n_fwd = len(blocks), len(forward)

    def body(*refs):
        x_refs, refs = refs[:n_arr], refs[n_arr + n_fwd:]
        out_refs, refs = refs[:n_arr], refs[n_arr:]
        fwd_refs, (send_sems, recv_sems, local_sems) = refs[:n_fwd], refs[n_fwd:]
        x, y, c, chips = _mesh_place()
        me, sibling = (x, y, c), (x, y, 1 - c)
        passing = []
        for f, buf in enumerate(fwd_refs):
            for j, (px, py) in enumerate(chips):
                mine, theirs = buf.at[4 * px + 2 * py + c], buf.at[4 * px + 2 * py + 1 - c]
                sems = dict(send_sem=send_sems.at[7 * n_arr + 3 * f + j], recv_sem=recv_sems.at[7 * n_arr + 3 * f + j],
                            device_id=sibling, device_id_type=MESH)
                passing.append((pltpu.make_async_remote_copy(src_ref=mine, dst_ref=mine, **sems),
                                pltpu.make_async_remote_copy(src_ref=mine, dst_ref=theirs, **sems)))
        for send, _ in passing:
            send.start()
        arrays = []
        for a, (x_ref, out_ref) in enumerate(zip(x_refs, out_refs)):
            src_mine = x_ref.at[c] if split else x_ref

            def copy(k, blk, to, src=None, a=a, out_ref=out_ref):
                dst = out_ref.at[4 * blk[0] + 2 * blk[1] + blk[2]]
                return pltpu.make_async_remote_copy(
                    src_ref=dst if src is None else src, dst_ref=dst,
                    send_sem=send_sems.at[7 * a + k], recv_sem=recv_sems.at[7 * a + k],
                    device_id=to, device_id_type=MESH)

            mine = pltpu.make_async_copy(src_mine, out_ref.at[4 * x + 2 * y + c], local_sems.at[a])
            mine.start()
            first = [copy(0, me, sibling, src=src_mine)]
            first += [copy(1 + j, me, (*chip, c), src=src_mine) for j, chip in enumerate(chips)]
            for cp in first:
                cp.start()
            arrays.append((copy, mine, first))
        sent = []
        for copy, mine, first in arrays:
            passed = [copy(4 + j, (*chip, c), sibling) for j, chip in enumerate(chips)]
            for j, chip in enumerate(chips):
                copy(1 + j, (*chip, c), me).wait_recv()
                passed[j].start()
            sent += first + passed
        for copy, mine, first in arrays:
            copy(0, sibling, me).wait_recv()
            for j, chip in enumerate(chips):
                copy(4 + j, (*chip, 1 - c), me).wait_recv()
            mine.wait()
        for cp in sent:
            cp.wait_send()
        for send, arrival in passing:
            send.wait_send()
            arrival.wait_recv()

    n_sems = 7 * n_arr + 3 * n_fwd
    return pl.pallas_call(
        body, name=name,
        out_shape=[jax.ShapeDtypeStruct((N_DEV,) + tuple(b.shape[1:] if split else b.shape), b.dtype)
                   for b in blocks] + [jax.ShapeDtypeStruct(f.shape, f.dtype) for f in forward],
        in_specs=[_any()] * (n_arr + n_fwd), out_specs=[_any()] * (n_arr + n_fwd),
        input_output_aliases={n_arr + f: n_arr + f for f in range(n_fwd)},
        scratch_shapes=[pltpu.SemaphoreType.DMA((n_sems,)), pltpu.SemaphoreType.DMA((n_sems,)),
                        pltpu.SemaphoreType.DMA((n_arr,))],
    )(*blocks, *forward)


def _sibling_share(bufs, name):
    n_arr = len(bufs)

    def body(*refs):
        out_refs = refs[n_arr:2 * n_arr]
        send_sems, recv_sems = refs[2 * n_arr:]
        x, y, c = lax.axis_index("x"), lax.axis_index("y"), lax.axis_index("c")
        copies = [pltpu.make_async_remote_copy(
            src_ref=out_refs[a].at[c], dst_ref=out_refs[a].at[c],
            send_sem=send_sems.at[a], recv_sem=recv_sems.at[a],
            device_id=(x, y, 1 - c), device_id_type=MESH) for a in range(n_arr)]
        for cp in copies:
            cp.start()
        for a in range(n_arr):
            pltpu.make_async_remote_copy(
                src_ref=out_refs[a].at[c], dst_ref=out_refs[a].at[1 - c],
                send_sem=send_sems.at[a], recv_sem=recv_sems.at[a],
                device_id=(x, y, 1 - c), device_id_type=MESH).wait()

    return pl.pallas_call(
        body, name=name,
        out_shape=[jax.ShapeDtypeStruct(b.shape, b.dtype) for b in bufs],
        in_specs=[_any()] * n_arr, out_specs=[_any()] * n_arr,
        input_output_aliases={a: a for a in range(n_arr)},
        scratch_shapes=[pltpu.SemaphoreType.DMA((n_arr,)), pltpu.SemaphoreType.DMA((n_arr,))],
    )(*bufs)


def _gelu_tanh(z):
    k = math.sqrt(2.0 / math.pi)
    t = jnp.tanh(k * (z + 0.044715 * (z * z * z)))
    return 0.5 * z * (1.0 + t), t


def _gelu_tanh_grad(z, t):
    k = math.sqrt(2.0 / math.pi)
    return 0.5 * (1.0 + t) + 0.5 * z * (1.0 - t * t) * (k * (1.0 + 3.0 * 0.044715 * (z * z)))


def _rope_angle_kernel(pos_row, invf_col):
    seq = pos_row.shape[1]

    def body(p_ref, f_ref, cos_ref, sin_ref):
        ang = p_ref[...].astype(F32) * f_ref[...]
        cos_ref[...] = jnp.cos(ang)
        sin_ref[...] = jnp.sin(ang)

    return pl.pallas_call(
        body, name="rope_angles", grid=(1,), out_shape=[jax.ShapeDtypeStruct((ROT_DIM // 2, seq), F32)] * 2,
        in_specs=[_full((1, seq)), _full((ROT_DIM // 2, 1))], out_specs=[_full((ROT_DIM // 2, seq))] * 2,
        compiler_params=_params("arbitrary"),
    )(pos_row, invf_col)


def _rope_lane_tables(cos, sin):
    cos_t, sin_t = cos.T, sin.T
    seq, half = cos_t.shape
    ones = jnp.ones((seq, HEAD_DIM - ROT_DIM), F32)
    c64 = jnp.concatenate([cos_t, cos_t, ones], axis=1)
    s1 = jnp.concatenate([sin_t, jnp.zeros((seq, HEAD_DIM - half), F32)], axis=1)
    s2 = jnp.concatenate([jnp.zeros((seq, half), F32), sin_t, jnp.zeros((seq, HEAD_DIM - ROT_DIM), F32)], axis=1)
    return jnp.concatenate([jnp.tile(t, (1, LANES // HEAD_DIM)) for t in (c64, s1, s2)], axis=1)


def _rope_apply(t, tab, sign):
    reps = t.shape[1] // LANES
    c_tab, s1, s2 = (jnp.tile(tab[:, LANES * k:LANES * (k + 1)], (1, reps)) if reps > 1
                     else tab[:, LANES * k:LANES * (k + 1)] for k in range(3))
    half = ROT_DIM // 2
    up = pltpu.roll(t, t.shape[1] - half, 1)
    down = pltpu.roll(t, half, 1)
    return t * c_tab + sign * (down * s2 - up * s1)


def _lane_masks(shape):
    lane = lax.broadcasted_iota(jnp.int32, shape, 1)
    return lane < HEAD_DIM, lane >= HEAD_DIM


HEADS_PER_GROUP = N_Q_HEADS // N_KV_HEADS
ATTN_SCALE = 1.0 / math.sqrt(HEAD_DIM)


def _attn_bias_t(first_block):
    kj = lax.broadcasted_iota(jnp.int32, (2 * CHUNK, CHUNK), 0)
    qi = lax.broadcasted_iota(jnp.int32, (2 * CHUNK, CHUNK), 1)
    ok = (kj > qi) & (kj <= qi + CHUNK) & (jnp.logical_not(first_block) | (kj >= CHUNK))
    return jnp.tile(jnp.where(ok, 0.0, -jnp.inf), (1, HEADS_PER_GROUP))


def _group_rows(x, g, lo, hi):
    rows = []
    for r in range(HEADS_PER_GROUP):
        h = HEADS_PER_GROUP * g + r
        pair = x[:, LANES * (h // 2):LANES * (h // 2 + 1)]
        rows.append(jnp.where(hi if h % 2 else lo, pair, 0.0))
    return jnp.concatenate(rows, axis=0)


def _pairs_from_rows(rows, lo):
    return [jnp.where(lo, rows[2 * CHUNK * k:2 * CHUNK * k + CHUNK], rows[2 * CHUNK * k + CHUNK:2 * CHUNK * (k + 1)])
            for k in range(HEADS_PER_GROUP // 2)]


def _group_dup(a, b, g, lo2):
    return jnp.where(lo2, a, b) if g == 0 else jnp.where(lo2, b, a)


def _sink_row(sink_ref, g):
    return jnp.concatenate([sink_ref[HEADS_PER_GROUP * g + r:HEADS_PER_GROUP * g + r + 1, :]
                            for r in range(HEADS_PER_GROUP)], axis=1)


def _attn_probs_t(k_dup, q_rows, bias_t, sink_row):
    s_t = _dot_nt(k_dup, q_rows) * ATTN_SCALE + bias_t
    m = jnp.maximum(jnp.max(s_t, axis=0, keepdims=True), sink_row)
    p = jnp.exp(s_t - m)
    e_sink = jnp.exp(sink_row - m)
    inv = 1.0 / (jnp.sum(p, axis=0, keepdims=True) + e_sink)
    return p * inv, e_sink * inv


def _sgu_forward_pair(wm, vp, j):
    lo, hi = _lane_masks(vp.shape)
    lhs = jnp.concatenate([wm[2 * j], wm[2 * j + 1]], axis=1)
    rhs = jnp.concatenate([jnp.where(lo, vp, 0.0), jnp.where(hi, vp, 0.0)], axis=0)
    return _dot(lhs, rhs)


def _masked_spatial(w_ref):
    t = lax.broadcasted_iota(jnp.int32, (CHUNK, CHUNK), 0)
    s = lax.broadcasted_iota(jnp.int32, (CHUNK, CHUNK), 1)
    tril = s <= t
    return [jnp.where(tril, w_ref[g], 0.0) for g in range(GMLP_GROUPS)], tril, s >= t


def _mod_kernel(c_all, w_shard, b_shard):
    n = w_shard.shape[1]
    tn = 512

    def body(c_ref, w_ref, b_ref, mod_ref, act_ref):
        cv = c_ref[...]
        act = cv * (1.0 / (1.0 + jnp.exp(-cv)))
        act_ref[...] = act
        mod_ref[...] = _dot(act, w_ref[...]) + b_ref[...]

    return pl.pallas_call(
        body, name="ada_mod", grid=(n // tn,),
        out_shape=[jax.ShapeDtypeStruct((N_DEV, n), F32), jax.ShapeDtypeStruct((N_DEV, D_MODEL), F32)],
        in_specs=[_full((N_DEV, D_MODEL)), pl.BlockSpec((D_MODEL, tn), lambda i: (0, i)),
                  pl.BlockSpec((1, tn), lambda i: (0, i))],
        out_specs=[pl.BlockSpec((N_DEV, tn), lambda i: (0, i)), _full((N_DEV, D_MODEL))],
        compiler_params=_params("arbitrary"),
    )(c_all, w_shard, b_shard)


def _in_proj_kernel(x, vecs, w_in_t, comm=None):
    seq = x.shape[0]
    tm = 512

    def body(x_ref, v_ref, w_ref, proj_ref, h_ref):
        xv = x_ref[...]
        rstd = lax.rsqrt(_mean_last(xv * xv) + EPS)
        n1 = (xv * rstd) * v_ref[0:1, :]
        h = n1 * (1.0 + v_ref[2:3, :]) + v_ref[1:2, :]
        hb = h.astype(MXU_DTYPE)
        h_ref[...] = hb
        proj_ref[...] = _dot_nt(hb, w_ref[...])

    return _hosted_call(
        body, comm, name="in_proj", grid=(seq // tm,),
        out_shape=[jax.ShapeDtypeStruct((seq, IN_PROJ_WIDTH), F32),
                   jax.ShapeDtypeStruct((seq, D_MODEL), MXU_DTYPE)],
        in_specs=[pl.BlockSpec((tm, D_MODEL), lambda i: (i, 0)), _full((8, D_MODEL)),
                  _full((IN_PROJ_WIDTH, D_MODEL))],
        out_specs=[pl.BlockSpec((tm, IN_PROJ_WIDTH), lambda i: (i, 0)),
                   pl.BlockSpec((tm, D_MODEL), lambda i: (i, 0))],
        semantics=("arbitrary",),
    )(x, vecs, w_in_t)


def _mixer_fwd_kernel(proj, rope_tab, w_spatial, bias_full, sink_rows, comm=None):
    seq = proj.shape[0]
    nb = seq // CHUNK
    kv_col = (2 * GMLP_WIDTH + ATTN_WIDTH) // (2 * KV_WIDTH)

    def body(proj_ref, prev_ref, tab_ref, ptab_ref, w_ref, bias_ref, sink_ref, cat_ref):
        i = pl.program_id(0)
        wm, _, _ = _masked_spatial(w_ref)
        for j in range(GMLP_GROUPS // 2):
            cols = slice(LANES * j, LANES * (j + 1))
            vcols = slice(GMLP_WIDTH + LANES * j, GMLP_WIDTH + LANES * (j + 1))
            u, _ = _gelu_tanh(proj_ref[:, cols])
            vp, _ = _gelu_tanh(proj_ref[:, vcols])
            sv = _sgu_forward_pair(wm, vp, j) + bias_ref[:, cols]
            cat_ref[:, cols] = (u * sv).astype(cat_ref.dtype)
        o = 2 * GMLP_WIDTH
        tab = tab_ref[...]
        q_r = _rope_apply(proj_ref[:, o:o + ATTN_WIDTH], tab, 1.0)
        k_cur = _rope_apply(proj_ref[:, o + ATTN_WIDTH:o + ATTN_WIDTH + KV_WIDTH], tab, 1.0)
        k_prev = _rope_apply(prev_ref[:, 0:KV_WIDTH], ptab_ref[...], 1.0)
        k_a = jnp.concatenate([k_prev, k_cur], axis=0)
        v_a = jnp.concatenate([prev_ref[:, KV_WIDTH:2 * KV_WIDTH],
                               proj_ref[:, o + ATTN_WIDTH + KV_WIDTH:o + ATTN_WIDTH + 2 * KV_WIDTH]], axis=0)
        k_b = pltpu.roll(k_a, HEAD_DIM, 1)
        v_b = pltpu.roll(v_a, HEAD_DIM, 1)
        bias_t = _attn_bias_t(i == 0)
        lo, hi = _lane_masks((CHUNK, LANES))
        lo2, _ = _lane_masks((2 * CHUNK, LANES))
        for g in range(N_KV_HEADS):
            p_t, _ = _attn_probs_t(_group_dup(k_a, k_b, g, lo2), _group_rows(q_r, g, lo, hi), bias_t,
                                   _sink_row(sink_ref, g))
            o_t = _dot(_group_dup(v_a, v_b, g, lo2).T, p_t)
            for k, pair in enumerate(_pairs_from_rows(o_t.T, lo)):
                c0 = GMLP_WIDTH + LANES * (2 * g + k)
                cat_ref[:, c0:c0 + LANES] = pair.astype(cat_ref.dtype)

    return _hosted_call(
        body, comm, name="mixer_fwd", grid=(nb,),
        out_shape=[jax.ShapeDtypeStruct((seq, D_MODEL), MXU_DTYPE)],
        in_specs=[pl.BlockSpec((CHUNK, IN_PROJ_WIDTH), lambda i: (i, 0)),
                  pl.BlockSpec((CHUNK, 2 * KV_WIDTH), lambda i: (jnp.maximum(i - 1, 0), kv_col)),
                  pl.BlockSpec((CHUNK, 3 * LANES), lambda i: (i, 0)),
                  pl.BlockSpec((CHUNK, 3 * LANES), lambda i: (jnp.maximum(i - 1, 0), 0)),
                  _full((GMLP_GROUPS, CHUNK, CHUNK)), _full((CHUNK, GMLP_WIDTH)),
                  _full((N_Q_HEADS, LANES))],
        out_specs=[pl.BlockSpec((CHUNK, D_MODEL), lambda i: (i, 0))],
        semantics=("arbitrary",),
    )(proj, proj, rope_tab, rope_tab, w_spatial, bias_full, sink_rows)


def _trunk_kernel(x, target, cat, vecs, chip_idx, gathered, local):
    seq = x.shape[0]
    tm = 256
    nj = D_FF // D_MODEL
    out_rows = D_MODEL // N_CHIPS

    def body(chip_ref, x_ref, t_ref, cat_ref, v_ref, g_out, g_w1, g_w2, l_out, l_w1, l_w2,
             dx1_ref, dcat_ref, dmix_ref, h2_ref, r_ref, da_ref, dff_ref, sums_ref,
             wout, w1, w2, a_scr, sem):
        i = pl.program_id(0)

        @pl.when(i == 0)
        def _():
            dsts = ([wout.at[pl.ds(out_rows * k, out_rows)] for k in range(N_CHIPS)]
                    + [w1.at[k] for k in range(N_CHIPS)] + [w2.at[k] for k in range(N_CHIPS)])
            srcs = [(g, l) for g, l in ((g_out, l_out), (g_w1, l_w1), (g_w2, l_w2)) for _ in range(N_CHIPS)]
            for n, (dst, (g, l)) in enumerate(zip(dsts, srcs)):
                k = n % N_CHIPS

                @pl.when(chip_ref[0] == k)
                def _():
                    pltpu.make_async_copy(l, dst, sem.at[n]).start()

                @pl.when(chip_ref[0] != k)
                def _():
                    pltpu.make_async_copy(g.at[k], dst, sem.at[n]).start()
            for n, (dst, (g, l)) in enumerate(zip(dsts, srcs)):
                pltpu.make_async_copy(l, dst, sem.at[n]).wait()
            sums_ref[...] = jnp.zeros_like(sums_ref)

        gate1, shift2, scale2 = v_ref[0:1, :], v_ref[1:2, :], v_ref[2:3, :]
        gate2, g_ffn, g_final = v_ref[3:4, :], v_ref[4:5, :], v_ref[5:6, :]

        mix = _dot(cat_ref[...], wout[...])
        x1 = x_ref[...] + gate1 * mix
        rstd2 = lax.rsqrt(_mean_last(x1 * x1) + EPS)
        xh2 = x1 * rstd2
        n2 = xh2 * g_ffn
        h2b = (n2 * (1.0 + scale2) + shift2).astype(MXU_DTYPE)
        h2_ref[...] = h2b
        ff = jnp.zeros((tm, D_MODEL), F32)
        for j in range(nj):
            a = _dot(h2b, w1[j])
            a_scr[j] = a
            relu = jnp.maximum(a, 0.0)
            rb = (relu * relu).astype(MXU_DTYPE)
            r_ref[:, D_MODEL * j:D_MODEL * (j + 1)] = rb
            ff = ff + _dot(rb, w2[j])
        x2 = x1 + gate2 * ff
        rstd3 = lax.rsqrt(_mean_last(x2 * x2) + EPS)
        xh3 = x2 * rstd3
        err = xh3 * g_final - t_ref[...]
        loss = 0.5 * _rowsum(_mean_last(err * err))
        dy = err * (1.0 / D_MODEL)
        dxh3 = dy * g_final
        dx2 = rstd3 * (dxh3 - xh3 * _mean_last(dxh3 * xh3))
        dffb = (dx2 * gate2).astype(MXU_DTYPE)
        dff_ref[...] = dffb
        dh2 = jnp.zeros((tm, D_MODEL), F32)
        for j in range(nj):
            dr = _dot_nt(dffb, w2[j])
            dab = (dr * (2.0 * jnp.maximum(a_scr[j], 0.0))).astype(MXU_DTYPE)
            da_ref[:, D_MODEL * j:D_MODEL * (j + 1)] = dab
            dh2 = dh2 + _dot_nt(dab, w1[j])
        dn2 = dh2 * (1.0 + scale2)
        dxh2 = dn2 * g_ffn
        dx1 = dx2 + rstd2 * (dxh2 - xh2 * _mean_last(dxh2 * xh2))
        dx1_ref[...] = dx1
        dmixb = (dx1 * gate1).astype(MXU_DTYPE)
        dmix_ref[...] = dmixb
        dcat_ref[...] = _dot_nt(dmixb, wout[...])

        sums_ref[0:1, :] += _rowsum(dh2)
        sums_ref[1:2, :] += _rowsum(dh2 * n2)
        sums_ref[2:3, :] += _rowsum(dx2 * ff)
        sums_ref[3:4, :] += _rowsum(dn2 * xh2)
        sums_ref[4:5, :] += _rowsum(dy * xh3)
        sums_ref[5:6, :] += _rowsum(dx1 * mix)
        sums_ref[6:7, :] += jnp.broadcast_to(loss, (1, D_MODEL))

    tok = lambda w: pl.BlockSpec((tm, w), lambda i, chip: (i, 0))
    return _hosted_call(
        body, None, name="trunk", grid=(seq // tm,), n_prefetch=1,
        out_shape=[jax.ShapeDtypeStruct((seq, D_MODEL), F32), jax.ShapeDtypeStruct((seq, D_MODEL), F32),
                   jax.ShapeDtypeStruct((seq, D_MODEL), MXU_DTYPE), jax.ShapeDtypeStruct((seq, D_MODEL), MXU_DTYPE),
                   jax.ShapeDtypeStruct((seq, D_FF), MXU_DTYPE), jax.ShapeDtypeStruct((seq, D_FF), MXU_DTYPE),
                   jax.ShapeDtypeStruct((seq, D_MODEL), MXU_DTYPE), jax.ShapeDtypeStruct((8, D_MODEL), F32)],
        in_specs=[tok(D_MODEL), tok(D_MODEL), tok(D_MODEL), _full((8, D_MODEL))] + [_any()] * 6,
        out_specs=[tok(D_MODEL), tok(D_MODEL), tok(D_MODEL), tok(D_MODEL), tok(D_FF), tok(D_FF), tok(D_MODEL),
                   _full((8, D_MODEL))],
        scratch_shapes=[pltpu.VMEM((D_MODEL, D_MODEL), MXU_DTYPE), pltpu.VMEM((nj, D_MODEL, D_MODEL), MXU_DTYPE),
                        pltpu.VMEM((nj, D_MODEL, D_MODEL), MXU_DTYPE), pltpu.VMEM((nj, tm, D_MODEL), F32),
                        pltpu.SemaphoreType.DMA((3 * N_CHIPS,))],
        semantics=("arbitrary",),
    )(chip_idx, x, target, cat, vecs, *gathered, *local)


def _mixer_bwd_kernel(proj, rope_tab, dcat, w_spatial, w_spatial_t, bias_full, sink_rows, comm=None):
    seq = proj.shape[0]
    nb = seq // CHUNK
    kv_col = (2 * GMLP_WIDTH + ATTN_WIDTH) // (2 * KV_WIDTH)

    def body(proj_ref, prev_ref, tab_ref, ptab_ref, dcat_ref, w_ref, wt_ref, bias_ref, sink_ref,
             dproj_ref, dw_ref, db_ref, dsink_ref, carry):
        step = pl.program_id(0)
        blk = nb - 1 - step

        @pl.when(step == 0)
        def _():
            carry[...] = jnp.zeros_like(carry)
            dw_ref[...] = jnp.zeros_like(dw_ref)
            db_ref[...] = jnp.zeros_like(db_ref)
            dsink_ref[...] = jnp.zeros_like(dsink_ref)

        wm, tril, triu = _masked_spatial(w_ref)
        lo, hi = _lane_masks((CHUNK, LANES))
        lane = lax.broadcasted_iota(jnp.int32, (CHUNK, LANES), 1)
        db = jnp.zeros((CHUNK, LANES), F32)
        for j in range(GMLP_GROUPS // 2):
            cols = slice(LANES * j, LANES * (j + 1))
            vcols = slice(GMLP_WIDTH + LANES * j, GMLP_WIDTH + LANES * (j + 1))
            zu, zv = proj_ref[:, cols], proj_ref[:, vcols]
            u, tu = _gelu_tanh(zu)
            vp, tv = _gelu_tanh(zv)
            sv = _sgu_forward_pair(wm, vp, j) + bias_ref[:, cols]
            dout = dcat_ref[:, cols]
            du = dout * sv
            dsv = dout * u
            dsv_lo, dsv_hi = jnp.where(lo, dsv, 0.0), jnp.where(hi, dsv, 0.0)
            lhs_t = jnp.concatenate([jnp.where(triu, wt_ref[2 * j], 0.0),
                                     jnp.where(triu, wt_ref[2 * j + 1], 0.0)], axis=1)
            dv = _dot(lhs_t, jnp.concatenate([dsv_lo, dsv_hi], axis=0))
            dw_ref[2 * j] += jnp.where(tril, _dot_nt(dsv_lo, vp), 0.0)
            dw_ref[2 * j + 1] += jnp.where(tril, _dot_nt(dsv_hi, vp), 0.0)
            db = db + (jnp.where(lane == 2 * j, jnp.sum(dsv_lo, axis=1, keepdims=True), 0.0)
                       + jnp.where(lane == 2 * j + 1, jnp.sum(dsv_hi, axis=1, keepdims=True), 0.0))
            dproj_ref[:, cols] = (du * _gelu_tanh_grad(zu, tu)).astype(dproj_ref.dtype)
            dproj_ref[:, vcols] = (dv * _gelu_tanh_grad(zv, tv)).astype(dproj_ref.dtype)
        db_ref[...] += db
        o = 2 * GMLP_WIDTH
        tab = tab_ref[...]
        q_r = _rope_apply(proj_ref[:, o:o + ATTN_WIDTH], tab, 1.0)
        k_cur = _rope_apply(proj_ref[:, o + ATTN_WIDTH:o + ATTN_WIDTH + KV_WIDTH], tab, 1.0)
        k_prev = _rope_apply(prev_ref[:, 0:KV_WIDTH], ptab_ref[...], 1.0)
        k_a = jnp.concatenate([k_prev, k_cur], axis=0)
        v_a = jnp.concatenate([prev_ref[:, KV_WIDTH:2 * KV_WIDTH],
                               proj_ref[:, o + ATTN_WIDTH + KV_WIDTH:o + ATTN_WIDTH + 2 * KV_WIDTH]], axis=0)
        k_b = pltpu.roll(k_a, HEAD_DIM, 1)
        v_b = pltpu.roll(v_a, HEAD_DIM, 1)
        bias_t = _attn_bias_t(blk == 0)
        lo2, _ = _lane_masks((2 * CHUNK, LANES))
        dout_b = dcat_ref[:, GMLP_WIDTH:GMLP_WIDTH + ATTN_WIDTH]
        dk_tot, dv_tot, dq_pairs = [], [], []
        for g in range(N_KV_HEADS):
            k_dup, v_dup = _group_dup(k_a, k_b, g, lo2), _group_dup(v_a, v_b, g, lo2)
            q_rows = _group_rows(q_r, g, lo, hi)
            do_rows = _group_rows(dout_b, g, lo, hi)
            p_t, p_sink = _attn_probs_t(k_dup, q_rows, bias_t, _sink_row(sink_ref, g))
            dp_t = _dot_nt(v_dup, do_rows)
            delta = jnp.sum(p_t * dp_t, axis=0, keepdims=True)
            ds_t = p_t * (dp_t - delta) * ATTN_SCALE
            dsink = -p_sink * delta
            for r in range(HEADS_PER_GROUP):
                h = HEADS_PER_GROUP * g + r
                dsink_ref[h:h + 1, :] += jnp.broadcast_to(
                    jnp.sum(dsink[:, LANES * r:LANES * (r + 1)], axis=1, keepdims=True), (1, LANES))
            dk_full = _dot(ds_t, q_rows)
            dv_full = _dot(p_t, do_rows)
            dk_tot.append(dk_full + pltpu.roll(dk_full, HEAD_DIM, 1))
            dv_tot.append(dv_full + pltpu.roll(dv_full, HEAD_DIM, 1))
            dq_t = _dot(k_dup.T, ds_t)
            dq_pairs += _pairs_from_rows(dq_t.T, lo)
        dk_all = jnp.where(lo2, dk_tot[0], dk_tot[1])
        dv_all = jnp.where(lo2, dv_tot[0], dv_tot[1])
        dk_cur = dk_all[CHUNK:, :] + carry[:, 0:KV_WIDTH]
        dv_cur = dv_all[CHUNK:, :] + carry[:, KV_WIDTH:2 * KV_WIDTH]
        carry[:, 0:KV_WIDTH] = dk_all[:CHUNK, :]
        carry[:, KV_WIDTH:2 * KV_WIDTH] = dv_all[:CHUNK, :]
        dq = _rope_apply(jnp.concatenate(dq_pairs, axis=1), tab, -1.0)
        dproj_ref[:, o:o + ATTN_WIDTH] = dq.astype(dproj_ref.dtype)
        dproj_ref[:, o + ATTN_WIDTH:o + ATTN_WIDTH + KV_WIDTH] = (
            _rope_apply(dk_cur, tab, -1.0).astype(dproj_ref.dtype))
        dproj_ref[:, o + ATTN_WIDTH + KV_WIDTH:o + ATTN_WIDTH + 2 * KV_WIDTH] = dv_cur.astype(dproj_ref.dtype)

    rev = lambda i: nb - 1 - i
    return _hosted_call(
        body, comm, name="mixer_bwd", grid=(nb,),
        out_shape=[jax.ShapeDtypeStruct((seq, IN_PROJ_WIDTH), MXU_DTYPE),
                   jax.ShapeDtypeStruct((GMLP_GROUPS, CHUNK, CHUNK), F32),
                   jax.ShapeDtypeStruct((CHUNK, LANES), F32),
                   jax.ShapeDtypeStruct((N_Q_HEADS, LANES), F32)],
        in_specs=[pl.BlockSpec((CHUNK, IN_PROJ_WIDTH), lambda i: (rev(i), 0)),
                  pl.BlockSpec((CHUNK, 2 * KV_WIDTH), lambda i: (jnp.maximum(rev(i) - 1, 0), kv_col)),
                  pl.BlockSpec((CHUNK, 3 * LANES), lambda i: (rev(i), 0)),
                  pl.BlockSpec((CHUNK, 3 * LANES), lambda i: (jnp.maximum(rev(i) - 1, 0), 0)),
                  pl.BlockSpec((CHUNK, D_MODEL), lambda i: (rev(i), 0)),
                  _full((GMLP_GROUPS, CHUNK, CHUNK)), _full((GMLP_GROUPS, CHUNK, CHUNK)),
                  _full((CHUNK, GMLP_WIDTH)), _full((N_Q_HEADS, LANES))],
        out_specs=[pl.BlockSpec((CHUNK, IN_PROJ_WIDTH), lambda i: (rev(i), 0)),
                   _full((GMLP_GROUPS, CHUNK, CHUNK)), _full((CHUNK, LANES)), _full((N_Q_HEADS, LANES))],
        scratch_shapes=[pltpu.VMEM((CHUNK, 2 * KV_WIDTH), F32)],
        semantics=("arbitrary",),
    )(proj, proj, rope_tab, rope_tab, dcat, w_spatial, w_spatial_t, bias_full, sink_rows)


def _in_proj_bwd_kernel(x, dx1, dproj, vecs, w_in_t, comm=None):
    seq = x.shape[0]
    tm = 512

    def body(x_ref, dx1_ref, dp_ref, v_ref, w_ref, gx_ref, sums_ref):
        @pl.when(pl.program_id(0) == 0)
        def _():
            sums_ref[...] = jnp.zeros_like(sums_ref)

        g_mix, scale1 = v_ref[0:1, :], v_ref[2:3, :]
        dh = _dot(dp_ref[...], w_ref[...])
        xv = x_ref[...]
        rstd = lax.rsqrt(_mean_last(xv * xv) + EPS)
        xh = xv * rstd
        dn1 = dh * (1.0 + scale1)
        dxh = dn1 * g_mix
        gx_ref[...] = dx1_ref[...] + rstd * (dxh - xh * _mean_last(dxh * xh))
        sums_ref[0:1, :] += _rowsum(dh)
        sums_ref[1:2, :] += _rowsum(dh * (xh * g_mix))
        sums_ref[2:3, :] += _rowsum(dn1 * xh)

    return _hosted_call(
        body, comm, name="in_proj_bwd", grid=(seq // tm,),
        out_shape=[jax.ShapeDtypeStruct((seq, D_MODEL), F32), jax.ShapeDtypeStruct((8, D_MODEL), F32)],
        in_specs=[pl.BlockSpec((tm, D_MODEL), lambda i: (i, 0)), pl.BlockSpec((tm, D_MODEL), lambda i: (i, 0)),
                  pl.BlockSpec((tm, IN_PROJ_WIDTH), lambda i: (i, 0)), _full((8, D_MODEL)),
                  _full((IN_PROJ_WIDTH, D_MODEL))],
        out_specs=[pl.BlockSpec((tm, D_MODEL), lambda i: (i, 0)), _full((8, D_MODEL))],
        semantics=("arbitrary",),
    )(x, dx1, dproj, vecs, w_in_t)


class _GradTiles(NamedTuple):
    tm: int
    tn: int
    n_tiles: int
    chips_per_tile: int
    a_index: Callable
    b_index: Callable


def _weight_grad_kernel(a, b, c_idx, name, tiles, comm=None):
    seq = a.shape[0]
    tk = min(seq, 4096)
    nk = seq // tk
    tm, tn, n_tiles, per = tiles.tm, tiles.tn, tiles.n_tiles, tiles.chips_per_tile
    rows = tm // per

    def half(phase, c):
        return phase * c[0] + (1 - phase) * (1 - c[0])

    def body(c_ref, a_ref, b_ref, o_ref, acc, stage, landed, send_sems, recv_sems):
        del c_ref
        phase, t, kk = pl.program_id(0), pl.program_id(1), pl.program_id(2)
        x, y, c, _ = _mesh_place()

        def copy(tile):
            return pltpu.make_async_remote_copy(
                src_ref=stage.at[tile], dst_ref=landed.at[tile], send_sem=send_sems.at[tile],
                recv_sem=recv_sems.at[tile], device_id=(x, y, 1 - c), device_id_type=MESH)

        @pl.when(kk == 0)
        def _():
            acc[...] = jnp.zeros_like(acc)

        acc[...] += _dot_tn(a_ref[...], b_ref[...])

        @pl.when((kk == nk - 1) & (phase == 0))
        def _():
            stage[t] = acc[...].astype(stage.dtype)
            copy(t).start()

        @pl.when((kk == nk - 1) & (phase == 1))
        def _():
            copy(t).wait_recv()
            total = acc[...] + landed[t].astype(F32)
            for q in range(per):
                o_ref[q] = total[rows * q:rows * (q + 1)].astype(o_ref.dtype)

        @pl.when((kk == nk - 1) & (phase == 1) & (t == n_tiles - 1))
        def _():
            for tile in range(n_tiles):
                copy(tile).wait_send()

    out = _hosted_call(
        body, comm, name=name, grid=(2, n_tiles, nk), n_prefetch=1,
        out_shape=[jax.ShapeDtypeStruct((n_tiles * per, rows, tn), GRAD_COMM_DTYPE)],
        in_specs=[pl.BlockSpec((tk, tm), lambda p, t, k, c: (k, tiles.a_index(t, half(p, c)))),
                  pl.BlockSpec((tk, tn), lambda p, t, k, c: (k, tiles.b_index(t, half(p, c))))],
        out_specs=[pl.BlockSpec((per, rows, tn), lambda p, t, k, c: (p * t, 0, 0))],
        scratch_shapes=[pltpu.VMEM((tm, tn), F32), pltpu.VMEM((n_tiles, tm, tn), GRAD_COMM_DTYPE),
                        pltpu.VMEM((n_tiles, tm, tn), GRAD_COMM_DTYPE),
                        pltpu.SemaphoreType.DMA((n_tiles,)), pltpu.SemaphoreType.DMA((n_tiles,))],
        semantics=("arbitrary", "arbitrary", "arbitrary"),
    )(c_idx, a, b)
    return out[0] if comm is None else out


def _row_tile(rows, most=256, sublanes=16):
    return max(t for t in range(sublanes, most + 1, sublanes) if rows % t == 0)


def _adam_update(w, g, m, v):
    m_new = ADAM_B1 * m + (1.0 - ADAM_B1) * g
    v_new = ADAM_B2 * v + (1.0 - ADAM_B2) * (g * g)
    m_hat = m_new / (1.0 - ADAM_B1 ** ADAM_STEP)
    v_hat = v_new / (1.0 - ADAM_B2 ** ADAM_STEP)
    delta = -ADAM_LR * (m_hat / (jnp.sqrt(v_hat) + ADAM_EPS) + ADAM_WD * w)
    return delta, m_new, v_new


def _sum_chips_kernel(own, others, place, name):
    _, r, n = own.shape
    tr = _row_tile(r)

    def body(place_ref, own_ref, oth_ref, o_ref):
        del place_ref
        acc = own_ref[...].astype(F32)
        for k in range(N_CHIPS - 1):
            acc = acc + oth_ref[k].astype(F32)
        o_ref[...] = acc

    return pl.pallas_call(
        body, name=name, out_shape=jax.ShapeDtypeStruct((2, r, n), F32),
        grid_spec=pltpu.PrefetchScalarGridSpec(
            num_scalar_prefetch=1, grid=(r // tr,),
            in_specs=[pl.BlockSpec((None, tr, n), lambda i, p: (p[0], i, 0)),
                      pl.BlockSpec((N_CHIPS - 1, tr, n), lambda i, p: (0, i, 0))],
            out_specs=pl.BlockSpec((None, tr, n), lambda i, p: (p[1], i, 0))),
        compiler_params=_params("parallel"),
    )(place, own, others)


def _adam_kernel(w, g, m, v, name):
    r, n = w.shape
    by_columns = g.shape[1] == r
    tr, tn = _row_tile(g.shape[1]), g.shape[2]

    def body(w_ref, g_ref, m_ref, v_ref, g_out, d_ref, mo_ref, vo_ref):
        gv = g_ref[...]
        g_out[...] = gv
        d_ref[...], mo_ref[...], vo_ref[...] = _adam_update(w_ref[...], gv, m_ref[...], v_ref[...])

    steps = g.shape[1] // tr
    spec = pl.BlockSpec((tr, tn), (lambda h, i: (i, h)) if by_columns else (lambda h, i: (h * steps + i, 0)))
    return pl.pallas_call(
        body, name=name, grid=(2, steps), out_shape=[jax.ShapeDtypeStruct((r, n), F32)] * 4,
        in_specs=[spec, pl.BlockSpec((None, tr, tn), lambda h, i: (h, i, 0)), spec, spec], out_specs=[spec] * 4,
        compiler_params=_params("parallel", "parallel"),
    )(w, g, m, v)


SMALL_PARAMS = ("b_ada", "g_mix", "g_ffn", "g_final", "b_spatial", "sinks", "w_spatial")


def _small_update_kernel(gathered, params):
    shapes = [params[nm][0].shape for nm in SMALL_PARAMS]

    def body(*refs):
        g_refs, refs = refs[:5], refs[5:]
        p_refs, refs = refs[:3 * len(SMALL_PARAMS)], refs[3 * len(SMALL_PARAMS):]
        loss_ref, o_refs = refs[0], refs[1:]

        def total(ref):
            acc = ref[0]
            for k in range(1, N_DEV):
                acc = acc + ref[k]
            return acc

        s1, s2, db, ds, dw = (total(r) for r in g_refs)
        loss_ref[...] = jnp.broadcast_to(s2[6:7, 0:1], loss_ref.shape)
        grads = {"b_ada": [s1[0:1], s1[1:2], s2[5:6], s2[0:1], s2[1:2], s2[2:3]], "g_mix": [s1[2:3]],
                 "g_ffn": [s2[3:4]], "g_final": [s2[4:5]], "b_spatial": [db.T[0:GMLP_GROUPS]],
                 "w_spatial": [dw]}
        lane = lax.broadcasted_iota(jnp.int32, (1, LANES), 1)
        sink_row = jnp.zeros((1, LANES), F32)
        for h in range(N_Q_HEADS):
            sink_row = sink_row + jnp.where(lane == h, ds[h:h + 1, :], 0.0)
        grads["sinks"] = [sink_row[:, 0:N_Q_HEADS]]
        for i, nm in enumerate(SMALL_PARAMS):
            w_ref, m_ref, v_ref = p_refs[3 * i:3 * i + 3]
            outs = o_refs[4 * i:4 * i + 4]
            width = grads[nm][0].shape[1]
            for k, g in enumerate(grads[nm]):
                cols = slice(width * k, width * (k + 1))
                upd = _adam_update(w_ref[:, cols], g, m_ref[:, cols], v_ref[:, cols])
                for o_ref, val in zip(outs, (g,) + upd):
                    o_ref[:, cols] = val

    flat = [a for nm in SMALL_PARAMS for a in params[nm]]
    out_shape = [jax.ShapeDtypeStruct((8, LANES), F32)]
    out_shape += [jax.ShapeDtypeStruct(s, F32) for s in shapes for _ in range(4)]
    outs = pl.pallas_call(
        body, name="small_update", grid=(1,), out_shape=out_shape,
        in_specs=[_full(g.shape) for g in gathered] + [_full(a.shape) for a in flat],
        out_specs=[_full(s.shape) for s in out_shape],
        compiler_params=_params("arbitrary"),
    )(*gathered, *flat)
    return {nm: outs[1 + 4 * i:5 + 4 * i] for i, nm in enumerate(SMALL_PARAMS)}, outs[0]


def _ada_update_kernel(act_t, dmod, w, m, v):
    r, n = w.shape
    tr = 256

    def body(a_ref, d_ref, w_ref, m_ref, v_ref, g_ref, dl_ref, mo_ref, vo_ref):
        g = _dot(a_ref[...], d_ref[...])
        g_ref[...] = g
        dl_ref[...], mo_ref[...], vo_ref[...] = _adam_update(w_ref[...], g, m_ref[...], v_ref[...])

    spec = pl.BlockSpec((tr, n), lambda i: (i, 0))
    return pl.pallas_call(
        body, name="ada_update", grid=(r // tr,), out_shape=[jax.ShapeDtypeStruct((r, n), F32)] * 4,
        in_specs=[pl.BlockSpec((tr, N_DEV), lambda i: (i, 0)), _full((N_DEV, n)), spec, spec, spec],
        out_specs=[spec] * 4, compiler_params=_params("parallel"),
    )(act_t, dmod, w, m, v)


def kernel(x, c, positions, w_ada, b_ada, g_mix, w_in, w_spatial, b_spatial, sinks, w_out, g_ffn, w_ff1, w_ff2, g_final, loss_target, m_w_ada, m_b_ada, m_g_mix, m_w_in, m_w_spatial, m_b_spatial, m_sinks, m_w_out, m_g_ffn, m_w_ff1, m_w_ff2, m_g_final, v_w_ada, v_b_ada, v_g_mix, v_w_in, v_w_spatial, v_b_spatial, v_sinks, v_w_out, v_g_ffn, v_w_ff1, v_w_ff2, v_g_final):
    xi, yi, ci = lax.axis_index("x"), lax.axis_index("y"), lax.axis_index("c")
    chip = 2 * xi + yi
    dev = 2 * chip + ci
    seq = x.shape[1]
    x2, tgt = x[0], loss_target[0]
    ada_cols = w_ada.shape[2]

    c_all = _all_gather8([c], "gather_c")[0].reshape(N_DEV, D_MODEL)
    b_shard = lax.dynamic_slice(b_ada, (0, chip * ada_cols), (1, ada_cols))
    mod_part, act = _mod_kernel(c_all, w_ada[0], b_shard)
    mod_all, = _all_gather8([mod_part], "gather_mod")
    mod_me = lax.dynamic_index_in_dim(mod_all[0::2], dev, axis=1, keepdims=False)
    mod_me = mod_me.reshape(N_MOD, D_MODEL)
    shift1, scale1, gate1, shift2, scale2, gate2 = (mod_me[k:k + 1] for k in range(N_MOD))

    big = {"w_in": tuple(a[0].T for a in (w_in, m_w_in, v_w_in)),
           "w_out": (w_out[0], m_w_out[0], v_w_out[0]), "w_ff1": (w_ff1[0], m_w_ff1[0], v_w_ff1[0]),
           "w_ff2": (w_ff2[0], m_w_ff2[0], v_w_ff2[0])}

    def halves(nm):
        r, n = big[nm][0].shape
        return big[nm][0].astype(WEIGHT_COMM_DTYPE).reshape(2, r // 2, n)

    w_in_t, = _all_gather8([halves("w_in")], "gather_w_in", split=True)
    w_in_t = w_in_t.reshape(IN_PROJ_WIDTH, D_MODEL)

    zeros_row = jnp.zeros((1, D_MODEL), F32)
    vecs1 = jnp.concatenate([g_mix, shift1, scale1] + [zeros_row] * 5, axis=0)
    vecs2 = jnp.concatenate([gate1, shift2, scale2, gate2, g_ffn, g_final.reshape(1, D_MODEL)]
                            + [zeros_row] * 2, axis=0)
    bias_full = jnp.repeat(b_spatial[0].T, HEAD_DIM, axis=1)
    sink_rows = jnp.broadcast_to(sinks[0][:, None], (N_Q_HEADS, LANES))
    inv_freq = ROPE_THETA ** (-jnp.arange(0, ROT_DIM, 2, dtype=F32) / ROT_DIM)
    rope_tab = _rope_lane_tables(*_rope_angle_kernel(positions, inv_freq.reshape(ROT_DIM // 2, 1)))

    trunk_weights = ["w_out", "w_ff1", "w_ff2"]
    shards = [halves(nm) for nm in trunk_weights]
    proj, hb, *staged = _in_proj_kernel(x2, vecs1, w_in_t, comm=_gather2d_first(shards))
    cat, *staged = _mixer_fwd_kernel(proj, rope_tab, w_spatial[0], bias_full, sink_rows,
                                     comm=_gather2d_second(staged, shards))
    staged = _gather_forward(staged, "gather_forward")
    dx1, dcat, dmix, h2b, rb, dab, dffb, sums2 = _trunk_kernel(
        x2, tgt, cat, vecs2, chip.reshape(1).astype(jnp.int32),
        [g.reshape((N_CHIPS,) + big[nm][0].shape) for nm, g in zip(trunk_weights, staged)],
        [s.reshape(big[nm][0].shape) for nm, s in zip(trunk_weights, shards)])

    c_idx = ci.reshape(1).astype(jnp.int32)
    place = jnp.stack([chip, ci]).astype(jnp.int32)
    half_d = D_MODEL // 2
    cs_ff2 = _weight_grad_kernel(rb, dffb, c_idx, "dw_ff2",
                                 _GradTiles(D_MODEL, half_d, N_CHIPS, 1, lambda t, h: t, lambda t, h: h))
    cs_ff1, sc_ff2 = _weight_grad_kernel(
        h2b, dab, c_idx, "dw_ff1",
        _GradTiles(D_MODEL, half_d, N_CHIPS, 1, lambda t, h: 0, lambda t, h: 2 * t + h),
        comm=_scatter_job([cs_ff2]))
    cs_out = _weight_grad_kernel(cat, dmix, c_idx, "dw_out",
                                 _GradTiles(D_MODEL, half_d, 1, N_CHIPS, lambda t, h: 0, lambda t, h: h))
    dproj, dw_spatial, db_lanes, dsink_rows, sc_ff1, sc_out = _mixer_bwd_kernel(
        proj, rope_tab, dcat, w_spatial[0], w_spatial[0].transpose(0, 2, 1), bias_full, sink_rows,
        comm=_scatter_job([cs_ff1, cs_out]))
    cs_in, *small_stage1 = _weight_grad_kernel(
        dproj, hb, c_idx, "dw_in",
        _GradTiles(2 * W_IN_BLOCK, half_d, N_CHIPS // 2, 2, lambda t, h: t, lambda t, h: h),
        comm=_gather_job([db_lanes, dsink_rows, dw_spatial.reshape(GMLP_GROUPS * CHUNK, CHUNK)], split=False))
    grad_x, sums1 = _in_proj_bwd_kernel(x2, dx1, dproj, vecs1, w_in_t)
    sc_in, = _run_comm(_scatter_job([cs_in]), "grad_to_chips_w_in")

    names = ["w_in", "w_out", "w_ff1", "w_ff2"]
    totals = [_sum_chips_kernel(own, oth, place, "grad_sum_" + nm)
              for nm, own, oth in zip(names, [cs_in, cs_out, cs_ff1, cs_ff2], [sc_in, sc_out, sc_ff1, sc_ff2])]
    shared = _sibling_share(totals, "grad_share")
    big_out = {}
    for nm, g in zip(names, shared):
        w, m, v = big[nm]
        outs = _adam_kernel(w, g, m, v, "adam_" + nm)
        big_out[nm] = tuple((t.T if nm == "w_in" else t)[None] for t in outs)

    small = {"b_ada": (b_ada, m_b_ada, v_b_ada), "g_mix": (g_mix, m_g_mix, v_g_mix),
             "g_ffn": (g_ffn, m_g_ffn, v_g_ffn), "g_final": (g_final, m_g_final, v_g_final),
             "b_spatial": (b_spatial, m_b_spatial, v_b_spatial), "sinks": (sinks, m_sinks, v_sinks),
             "w_spatial": (w_spatial, m_w_spatial, v_w_spatial)}
    flat_shape = {"g_final": (1, D_MODEL), "b_spatial": (GMLP_GROUPS, CHUNK), "w_spatial": (GMLP_GROUPS * CHUNK, CHUNK)}
    gathered = _all_gather8([sums1, sums2], "gather_small", forward=small_stage1)
    small_out, loss_tile = _small_update_kernel(
        gathered, {nm: tuple(a.reshape(flat_shape.get(nm, a.shape)) for a in small[nm]) for nm in small})
    small_out = {nm: [o.reshape(small[nm][0].shape) for o in small_out[nm]] for nm in small}
    loss = loss_tile[0, 0]

    g1, g2 = gathered[0], gathered[1]
    dmod_all = jnp.concatenate([g1[:, 0], g1[:, 1], g2[:, 5], g2[:, 0], g2[:, 1], g2[:, 2]], axis=1)
    dmod_cols = lax.dynamic_slice(dmod_all, (0, chip * ada_cols), (N_DEV, ada_cols))
    ada = _ada_update_kernel(act.T, dmod_cols, w_ada[0], m_w_ada[0], v_w_ada[0])
    big_out["w_ada"] = tuple(t[None] for t in ada)

    order = ["w_ada", "b_ada", "g_mix", "w_in", "w_spatial", "b_spatial", "sinks", "w_out", "g_ffn",
             "w_ff1", "w_ff2", "g_final"]

    def leaf(nm, k):
        return big_out[nm][k] if nm in big_out else small_out[nm][k]

    outs = [loss, grad_x[None]]
    for k in range(4):
        outs += [leaf(nm, k) for nm in order]
    return tuple(outs)
```

```python
import math
from typing import Callable, NamedTuple

import jax
import jax.numpy as jnp
from jax import lax
from jax.experimental import pallas as pl
from jax.experimental.pallas import tpu as pltpu

F32 = jnp.float32
MXU_DTYPE = jnp.bfloat16
WEIGHT_COMM_DTYPE = jnp.bfloat16
GRAD_COMM_DTYPE = jnp.bfloat16

D_MODEL = 1024
D_FF = 4096
HEAD_DIM = 64
GMLP_GROUPS = 8
GMLP_WIDTH = 512
CHUNK = 128
N_Q_HEADS = 8
N_KV_HEADS = 2
ATTN_WIDTH = 512
KV_WIDTH = 128
ROT_DIM = 16
ROPE_THETA = 500000.0
IN_PROJ_WIDTH = 1792
N_MOD = 6
EPS = 1e-5
N_CHIPS = 4
N_DEV = 8
LANES = 128
W_IN_BLOCK = IN_PROJ_WIDTH // N_CHIPS

ADAM_LR = 0.001
ADAM_B1 = 0.9
ADAM_B2 = 0.999
ADAM_EPS = 1e-08
ADAM_WD = 0.01
ADAM_STEP = 10

VMEM_LIMIT_BYTES = 58 * 1024 * 1024
MESH = pl.DeviceIdType.MESH


def _params(*semantics):
    return pltpu.CompilerParams(dimension_semantics=semantics, vmem_limit_bytes=VMEM_LIMIT_BYTES)


def _dot(a, b):
    return jnp.dot(a.astype(MXU_DTYPE), b.astype(MXU_DTYPE), preferred_element_type=F32)


def _dot_nt(a, b):
    return lax.dot_general(a.astype(MXU_DTYPE), b.astype(MXU_DTYPE), (((1,), (1,)), ((), ())),
                           preferred_element_type=F32)


def _dot_tn(a, b):
    return lax.dot_general(a.astype(MXU_DTYPE), b.astype(MXU_DTYPE), (((0,), (0,)), ((), ())),
                           preferred_element_type=F32)


def _full(shape):
    return pl.BlockSpec(shape, lambda *_: (0,) * len(shape))


def _any():
    return pl.BlockSpec(memory_space=pl.ANY)


def _rowsum(v):
    return jnp.sum(v, axis=0, keepdims=True)


def _mean_last(v):
    return jnp.mean(v, axis=-1, keepdims=True)


class _Comm(NamedTuple):
    operands: tuple
    out_shapes: tuple
    n_sems: int
    make: Callable
    in_place: int = 0


def _hosted_call(body, comm, *, name, grid, in_specs, out_shape, out_specs, scratch_shapes=(), semantics,
                 n_prefetch=0):
    if comm is None:
        return pl.pallas_call(
            body, name=name, out_shape=out_shape, compiler_params=_params(*semantics),
            grid_spec=pltpu.PrefetchScalarGridSpec(
                num_scalar_prefetch=n_prefetch, grid=grid, in_specs=in_specs, out_specs=out_specs,
                scratch_shapes=list(scratch_shapes)))
    n_in, n_out, n_scr = len(in_specs), len(out_shape), len(scratch_shapes)
    k_in, k_out = len(comm.operands), len(comm.out_shapes)

    def hosted(*refs):
        prefetched, refs = refs[:n_prefetch], refs[n_prefetch:]
        ins, refs = refs[:n_in], refs[n_in:]
        c_ins, refs = refs[:k_in], refs[k_in:]
        outs, refs = refs[:n_out], refs[n_out:]
        c_outs, refs = refs[:k_out], refs[k_out:]
        scratch, (send_sems, recv_sems) = refs[:n_scr], refs[n_scr:]
        first, last = None, None
        for d, size in enumerate(grid):
            at_start, at_end = pl.program_id(d) == 0, pl.program_id(d) == size - 1
            first = at_start if first is None else first & at_start
            last = at_end if last is None else last & at_end

        @pl.when(first)
        def _():
            for cp in comm.make(c_ins, c_outs, send_sems, recv_sems)[0]:
                cp.start()

        body(*prefetched, *ins, *outs, *scratch)

        @pl.when(last)
        def _():
            for wait in comm.make(c_ins, c_outs, send_sems, recv_sems)[1]:
                wait()

    aliases = {n_prefetch + n_in + i: n_out + i for i in range(comm.in_place)}
    call = pl.pallas_call(
        hosted, name=name, out_shape=list(out_shape) + list(comm.out_shapes),
        compiler_params=_params(*semantics), input_output_aliases=aliases,
        grid_spec=pltpu.PrefetchScalarGridSpec(
            num_scalar_prefetch=n_prefetch, grid=grid, in_specs=list(in_specs) + [_any()] * k_in,
            out_specs=list(out_specs) + [_any()] * k_out,
            scratch_shapes=list(scratch_shapes) + [pltpu.SemaphoreType.DMA((comm.n_sems,)),
                                                    pltpu.SemaphoreType.DMA((comm.n_sems,))]))
    return lambda *args: call(*args, *comm.operands)


def _mesh_place():
    x, y, c = lax.axis_index("x"), lax.axis_index("y"), lax.axis_index("c")
    return x, y, c, [(1 - x, y), (x, 1 - y), (1 - x, 1 - y)]


def _gather_job(bufs):
    per = 4

    def make(ins, outs, send_sems, recv_sems):
        del ins
        x, y, c, chips = _mesh_place()
        starts, waits = [], []
        for a, out in enumerate(outs):
            mine = src = out.at[4 * x + 2 * y + c]
            to = [(x, y, 1 - c)] + [(px, py, c) for px, py in chips]
            sends = [pltpu.make_async_remote_copy(
                src_ref=src, dst_ref=mine, send_sem=send_sems.at[per * a + k],
                recv_sem=recv_sems.at[per * a + k], device_id=dev, device_id_type=MESH)
                for k, dev in enumerate(to)]
            recvs = [pltpu.make_async_remote_copy(
                src_ref=src, dst_ref=out.at[4 * px + 2 * py + pc], send_sem=send_sems.at[per * a + k],
                recv_sem=recv_sems.at[per * a + k], device_id=(px, py, pc), device_id_type=MESH)
                for k, (px, py, pc) in enumerate(to)]
            starts += sends
            waits += [s.wait_send for s in sends] + [r.wait_recv for r in recvs]
        return starts, waits

    shapes = tuple(jax.ShapeDtypeStruct(b.shape, b.dtype) for b in bufs)
    return _Comm(tuple(bufs), shapes, per * len(bufs), make, in_place=len(bufs))


def _slots(x, y, c):
    return 4 * x + 2 * y + c, 4 * (1 - x) + 2 * y + c, 4 * x + 2 * (1 - y) + c, 4 * (1 - x) + 2 * (1 - y) + c


def _gather2d_first(halves):
    per = 2

    def make(ins, outs, send_sems, recv_sems):
        x, y, c, _ = _mesh_place()
        me, xn, yn, _ = _slots(x, y, c)
        starts, waits = [], []
        for a, (src, out) in enumerate(zip(ins, outs)):
            blk = src.at[c]
            rows = blk.shape[0] // 2
            upper, lower = pl.ds(0, rows), pl.ds(rows, rows)

            def copy(k, src_ref, dst_ref, dev, a=a):
                return pltpu.make_async_remote_copy(
                    src_ref=src_ref, dst_ref=dst_ref, send_sem=send_sems.at[per * a + k],
                    recv_sem=recv_sems.at[per * a + k], device_id=dev, device_id_type=MESH)

            sends = [copy(0, blk.at[upper], out.at[me, upper], (1 - x, y, c)),
                     copy(1, blk.at[lower], out.at[me, lower], (x, 1 - y, c))]
            recvs = [copy(0, blk.at[upper], out.at[xn, upper], (1 - x, y, c)),
                     copy(1, blk.at[lower], out.at[yn, lower], (x, 1 - y, c))]
            starts += sends
            waits += [s.wait_send for s in sends] + [r.wait_recv for r in recvs]
        return starts, waits

    shapes = tuple(jax.ShapeDtypeStruct((N_DEV,) + h.shape[1:], h.dtype) for h in halves)
    return _Comm(tuple(halves), shapes, per * len(halves), make)


def _gather2d_second(bufs, halves):
    per = 4
    n_arr = len(bufs)

    def make(ins, outs, send_sems, recv_sems):
        x, y, c, _ = _mesh_place()
        me, xn, yn, dg = _slots(x, y, c)
        starts, waits = [], []
        for a, buf in enumerate(outs):
            own = ins[n_arr + a].at[c]
            rows = buf.shape[1] // 2
            upper, lower = pl.ds(0, rows), pl.ds(rows, rows)
            plan = [(own.at[upper], me, upper, (x, 1 - y, c), yn), (buf.at[xn, upper], xn, upper, (x, 1 - y, c), dg),
                    (own.at[lower], me, lower, (1 - x, y, c), xn), (buf.at[yn, lower], yn, lower, (1 - x, y, c), dg)]
            for k, (src, slot, part, dev, landing) in enumerate(plan):
                sems = dict(send_sem=send_sems.at[per * a + k], recv_sem=recv_sems.at[per * a + k],
                            device_id=dev, device_id_type=MESH)
                send = pltpu.make_async_remote_copy(src_ref=src, dst_ref=buf.at[slot, part], **sems)
                arrival = pltpu.make_async_remote_copy(src_ref=src, dst_ref=buf.at[landing, part], **sems)
                starts.append(send)
                waits += [send.wait_send, arrival.wait_recv]
        return starts, waits

    shapes = tuple(jax.ShapeDtypeStruct(b.shape, b.dtype) for b in bufs)
    return _Comm(tuple(bufs) + tuple(halves), shapes, per * n_arr, make, in_place=n_arr)


def _gather_forward(bufs, name):
    n_arr = len(bufs)

    def body(*refs):
        outs = refs[n_arr:2 * n_arr]
        send_sems, recv_sems = refs[2 * n_arr:]
        x, y, c, chips = _mesh_place()
        sends, recvs = [], []
        for a, buf in enumerate(outs):
            for j, (px, py) in enumerate(chips):
                mine, theirs = buf.at[4 * px + 2 * py + c], buf.at[4 * px + 2 * py + 1 - c]
                sems = dict(send_sem=send_sems.at[3 * a + j], recv_sem=recv_sems.at[3 * a + j],
                            device_id=(x, y, 1 - c), device_id_type=MESH)
                sends.append(pltpu.make_async_remote_copy(src_ref=mine, dst_ref=mine, **sems))
                recvs.append(pltpu.make_async_remote_copy(src_ref=mine, dst_ref=theirs, **sems))
        for cp in sends:
            cp.start()
        for s, r in zip(sends, recvs):
            s.wait_send()
            r.wait_recv()

    return pl.pallas_call(
        body, name=name, out_shape=[jax.ShapeDtypeStruct(b.shape, b.dtype) for b in bufs],
        in_specs=[_any()] * n_arr, out_specs=[_any()] * n_arr,
        input_output_aliases={a: a for a in range(n_arr)},
        scratch_shapes=[pltpu.SemaphoreType.DMA((3 * n_arr,)), pltpu.SemaphoreType.DMA((3 * n_arr,))],
    )(*bufs)


def _scatter_job(chip_sums):
    def make(ins, outs, send_sems, recv_sems):
        x, y, c, chips = _mesh_place()
        copies = [pltpu.make_async_remote_copy(
            src_ref=src.at[2 * px + py], dst_ref=out.at[j], send_sem=send_sems.at[3 * a + j],
            recv_sem=recv_sems.at[3 * a + j], device_id=(px, py, c), device_id_type=MESH)
            for a, (src, out) in enumerate(zip(ins, outs)) for j, (px, py) in enumerate(chips)]
        return copies, [cp.wait for cp in copies]

    return _Comm(tuple(chip_sums), tuple(jax.ShapeDtypeStruct((3,) + s.shape[1:], s.dtype) for s in chip_sums),
                 3 * len(chip_sums), make)


def _run_comm(comm, name):
    def body(token_ref):
        token_ref[...] = jnp.zeros_like(token_ref)

    out = _hosted_call(body, comm, name=name, grid=(1,), in_specs=[],
                       out_shape=[jax.ShapeDtypeStruct((8, LANES), F32)], out_specs=[_full((8, LANES))],
                       semantics=("arbitrary",))()
    return out[1:]


def _all_gather8(blocks, name, split=False, forward=()):
    n_arr, n_fwd = len(blocks), len(forward)

    def body(*refs):
        x_refs, refs = refs[:n_arr], refs[n_arr + n_fwd:]
        out_refs, refs = refs[:n_arr], refs[n_arr:]
        fwd_refs, (send_sems, recv_sems, local_sems) = refs[:n_fwd], refs[n_fwd:]
        x, y, c, chips = _mesh_place()
        me, sibling = (x, y, c), (x, y, 1 - c)
        passing = []
        for f, buf in enumerate(fwd_refs):
            for j, (px, py) in enumerate(chips):
                mine, theirs = buf.at[4 * px + 2 * py + c], buf.at[4 * px + 2 * py + 1 - c]
                sems = dict(send_sem=send_sems.at[7 * n_arr + 3 * f + j], recv_sem=recv_sems.at[7 * n_arr + 3 * f + j],
                            device_id=sibling, device_id_type=MESH)
                passing.append((pltpu.make_async_remote_copy(src_ref=mine, dst_ref=mine, **sems),
                                pltpu.make_async_remote_copy(src_ref=mine, dst_ref=theirs, **sems)))
        for send, _ in passing:
            send.start()
        arrays = []
        for a, (x_ref, out_ref) in enumerate(zip(x_refs, out_refs)):
            src_mine = x_ref.at[c] if split else x_ref

            def copy(k, blk, to, src=None, a=a, out_ref=out_ref):
                dst = out_ref.at[4 * blk[0] + 2 * blk[1] + blk[2]]
                return pltpu.make_async_remote_copy(
                    src_ref=dst if src is None else src, dst_ref=dst,
                    send_sem=send_sems.at[7 * a + k], recv_sem=recv_sems.at[7 * a + k],
                    device_id=to, device_id_type=MESH)

            mine = pltpu.make_async_copy(src_mine, out_ref.at[4 * x + 2 * y + c], local_sems.at[a])
            mine.start()
            first = [copy(0, me, sibling, src=src_mine)]
            first += [copy(1 + j, me, (*chip, c), src=src_mine) for j, chip in enumerate(chips)]
            for cp in first:
                cp.start()
            arrays.append((copy, mine, first))
        sent = []
        for copy, mine, first in arrays:
            passed = [copy(4 + j, (*chip, c), sibling) for j, chip in enumerate(chips)]
            for j, chip in enumerate(chips):
                copy(1 + j, (*chip, c), me).wait_recv()
                passed[j].start()
            sent += first + passed
        for copy, mine, first in arrays:
            copy(0, sibling, me).wait_recv()
            for j, chip in enumerate(chips):
                copy(4 + j, (*chip, 1 - c), me).wait_recv()
            mine.wait()
        for cp in sent:
            cp.wait_send()
        for send, arrival in passing:
            send.wait_send()
            arrival.wait_recv()

    n_sems = 7 * n_arr + 3 * n_fwd
    return pl.pallas_call(
        body, name=name,
        out_shape=[jax.ShapeDtypeStruct((N_DEV,) + tuple(b.shape[1:] if split else b.shape), b.dtype)
                   for b in blocks] + [jax.ShapeDtypeStruct(f.shape, f.dtype) for f in forward],
        in_specs=[_any()] * (n_arr + n_fwd), out_specs=[_any()] * (n_arr + n_fwd),
        input_output_aliases={n_arr + f: n_arr + f for f in range(n_fwd)},
        scratch_shapes=[pltpu.SemaphoreType.DMA((n_sems,)), pltpu.SemaphoreType.DMA((n_sems,)),
                        pltpu.SemaphoreType.DMA((n_arr,))],
    )(*blocks, *forward)


def _sibling_share(bufs, name):
    n_arr = len(bufs)

    def body(*refs):
        out_refs = refs[n_arr:2 * n_arr]
        send_sems, recv_sems = refs[2 * n_arr:]
        x, y, c = lax.axis_index("x"), lax.axis_index("y"), lax.axis_index("c")
        copies = [pltpu.make_async_remote_copy(
            src_ref=out_refs[a].at[c], dst_ref=out_refs[a].at[c],
            send_sem=send_sems.at[a], recv_sem=recv_sems.at[a],
            device_id=(x, y, 1 - c), device_id_type=MESH) for a in range(n_arr)]
        for cp in copies:
            cp.start()
        for a in range(n_arr):
            pltpu.make_async_remote_copy(
                src_ref=out_refs[a].at[c], dst_ref=out_refs[a].at[1 - c],
                send_sem=send_sems.at[a], recv_sem=recv_sems.at[a],
                device_id=(x, y, 1 - c), device_id_type=MESH).wait()

    return pl.pallas_call(
        body, name=name,
        out_shape=[jax.ShapeDtypeStruct(b.shape, b.dtype) for b in bufs],
        in_specs=[_any()] * n_arr, out_specs=[_any()] * n_arr,
        input_output_aliases={a: a for a in range(n_arr)},
        scratch_shapes=[pltpu.SemaphoreType.DMA((n_arr,)), pltpu.SemaphoreType.DMA((n_arr,))],
    )(*bufs)


def _gelu_tanh(z):
    k = math.sqrt(2.0 / math.pi)
    t = jnp.tanh(k * (z + 0.044715 * (z * z * z)))
    return 0.5 * z * (1.0 + t), t


def _gelu_tanh_grad(z, t):
    k = math.sqrt(2.0 / math.pi)
    return 0.5 * (1.0 + t) + 0.5 * z * (1.0 - t * t) * (k * (1.0 + 3.0 * 0.044715 * (z * z)))


def _rope_angle_kernel(pos_row, invf_col):
    seq = pos_row.shape[1]

    def body(p_ref, f_ref, cos_ref, sin_ref):
        ang = p_ref[...].astype(F32) * f_ref[...]
        cos_ref[...] = jnp.cos(ang)
        sin_ref[...] = jnp.sin(ang)

    return pl.pallas_call(
        body, name="rope_angles", grid=(1,), out_shape=[jax.ShapeDtypeStruct((ROT_DIM // 2, seq), F32)] * 2,
        in_specs=[_full((1, seq)), _full((ROT_DIM // 2, 1))], out_specs=[_full((ROT_DIM // 2, seq))] * 2,
        compiler_params=_params("arbitrary"),
    )(pos_row, invf_col)


def _rope_lane_tables(cos, sin):
    cos_t, sin_t = cos.T, sin.T
    seq, half = cos_t.shape
    ones = jnp.ones((seq, HEAD_DIM - ROT_DIM), F32)
    c64 = jnp.concatenate([cos_t, cos_t, ones], axis=1)
    s1 = jnp.concatenate([sin_t, jnp.zeros((seq, HEAD_DIM - half), F32)], axis=1)
    s2 = jnp.concatenate([jnp.zeros((seq, half), F32), sin_t, jnp.zeros((seq, HEAD_DIM - ROT_DIM), F32)], axis=1)
    return jnp.concatenate([jnp.tile(t, (1, LANES // HEAD_DIM)) for t in (c64, s1, s2)], axis=1)


def _rope_apply(t, tab, sign):
    reps = t.shape[1] // LANES
    c_tab, s1, s2 = (jnp.tile(tab[:, LANES * k:LANES * (k + 1)], (1, reps)) if reps > 1
                     else tab[:, LANES * k:LANES * (k + 1)] for k in range(3))
    half = ROT_DIM // 2
    up = pltpu.roll(t, t.shape[1] - half, 1)
    down = pltpu.roll(t, half, 1)
    return t * c_tab + sign * (down * s2 - up * s1)


def _lane_masks(shape):
    lane = lax.broadcasted_iota(jnp.int32, shape, 1)
    return lane < HEAD_DIM, lane >= HEAD_DIM


HEADS_PER_GROUP = N_Q_HEADS // N_KV_HEADS
ATTN_SCALE = 1.0 / math.sqrt(HEAD_DIM)


def _attn_bias_t(first_block):
    kj = lax.broadcasted_iota(jnp.int32, (2 * CHUNK, CHUNK), 0)
    qi = lax.broadcasted_iota(jnp.int32, (2 * CHUNK, CHUNK), 1)
    ok = (kj > qi) & (kj <= qi + CHUNK) & (jnp.logical_not(first_block) | (kj >= CHUNK))
    return jnp.tile(jnp.where(ok, 0.0, -jnp.inf), (1, HEADS_PER_GROUP))


def _group_rows(x, g, lo, hi):
    rows = []
    for r in range(HEADS_PER_GROUP):
        h = HEADS_PER_GROUP * g + r
        pair = x[:, LANES * (h // 2):LANES * (h // 2 + 1)]
        rows.append(jnp.where(hi if h % 2 else lo, pair, 0.0))
    return jnp.concatenate(rows, axis=0)


def _pairs_from_rows(rows, lo):
    return [jnp.where(lo, rows[2 * CHUNK * k:2 * CHUNK * k + CHUNK], rows[2 * CHUNK * k + CHUNK:2 * CHUNK * (k + 1)])
            for k in range(HEADS_PER_GROUP // 2)]


def _group_dup(a, b, g, lo2):
    return jnp.where(lo2, a, b) if g == 0 else jnp.where(lo2, b, a)


def _sink_row(sink_ref, g):
    return jnp.concatenate([sink_ref[HEADS_PER_GROUP * g + r:HEADS_PER_GROUP * g + r + 1, :]
                            for r in range(HEADS_PER_GROUP)], axis=1)


def _attn_probs_t(k_dup, q_rows, bias_t, sink_row):
    s_t = _dot_nt(k_dup, q_rows) * ATTN_SCALE + bias_t
    m = jnp.maximum(jnp.max(s_t, axis=0, keepdims=True), sink_row)
    p = jnp.exp(s_t - m)
    e_sink = jnp.exp(sink_row - m)
    inv = 1.0 / (jnp.sum(p, axis=0, keepdims=True) + e_sink)
    return p * inv, e_sink * inv


def _sgu_forward_pair(wm, vp, j):
    lo, hi = _lane_masks(vp.shape)
    lhs = jnp.concatenate([wm[2 * j], wm[2 * j + 1]], axis=1)
    rhs = jnp.concatenate([jnp.where(lo, vp, 0.0), jnp.where(hi, vp, 0.0)], axis=0)
    return _dot(lhs, rhs)


def _masked_spatial(w_ref):
    t = lax.broadcasted_iota(jnp.int32, (CHUNK, CHUNK), 0)
    s = lax.broadcasted_iota(jnp.int32, (CHUNK, CHUNK), 1)
    tril = s <= t
    return [jnp.where(tril, w_ref[g], 0.0) for g in range(GMLP_GROUPS)], tril, s >= t


def _mod_kernel(c_all, w_shard, b_shard):
    n = w_shard.shape[1]
    tn = 512

    def body(c_ref, w_ref, b_ref, mod_ref, act_ref):
        cv = c_ref[...]
        act = cv * (1.0 / (1.0 + jnp.exp(-cv)))
        act_ref[...] = act
        mod_ref[...] = _dot(act, w_ref[...]) + b_ref[...]

    return pl.pallas_call(
        body, name="ada_mod", grid=(n // tn,),
        out_shape=[jax.ShapeDtypeStruct((N_DEV, n), F32), jax.ShapeDtypeStruct((N_DEV, D_MODEL), F32)],
        in_specs=[_full((N_DEV, D_MODEL)), pl.BlockSpec((D_MODEL, tn), lambda i: (0, i)),
                  pl.BlockSpec((1, tn), lambda i: (0, i))],
        out_specs=[pl.BlockSpec((N_DEV, tn), lambda i: (0, i)), _full((N_DEV, D_MODEL))],
        compiler_params=_params("arbitrary"),
    )(c_all, w_shard, b_shard)


def _in_proj_kernel(x, vecs, w_in_t, comm=None):
    seq = x.shape[0]
    tm = 512

    def body(x_ref, v_ref, w_ref, proj_ref, h_ref):
        xv = x_ref[...]
        rstd = lax.rsqrt(_mean_last(xv * xv) + EPS)
        n1 = (xv * rstd) * v_ref[0:1, :]
        h = n1 * (1.0 + v_ref[2:3, :]) + v_ref[1:2, :]
        hb = h.astype(MXU_DTYPE)
        h_ref[...] = hb
        proj_ref[...] = _dot_nt(hb, w_ref[...])

    return _hosted_call(
        body, comm, name="in_proj", grid=(seq // tm,),
        out_shape=[jax.ShapeDtypeStruct((seq, IN_PROJ_WIDTH), F32),
                   jax.ShapeDtypeStruct((seq, D_MODEL), MXU_DTYPE)],
        in_specs=[pl.BlockSpec((tm, D_MODEL), lambda i: (i, 0)), _full((8, D_MODEL)),
                  _full((IN_PROJ_WIDTH, D_MODEL))],
        out_specs=[pl.BlockSpec((tm, IN_PROJ_WIDTH), lambda i: (i, 0)),
                   pl.BlockSpec((tm, D_MODEL), lambda i: (i, 0))],
        semantics=("arbitrary",),
    )(x, vecs, w_in_t)


def _mixer_fwd_kernel(proj, rope_tab, w_spatial, bias_full, sink_rows, comm=None):
    seq = proj.shape[0]
    nb = seq // CHUNK
    kv_col = (2 * GMLP_WIDTH + ATTN_WIDTH) // (2 * KV_WIDTH)

    def body(proj_ref, prev_ref, tab_ref, ptab_ref, w_ref, bias_ref, sink_ref, cat_ref):
        i = pl.program_id(0)
        wm, _, _ = _masked_spatial(w_ref)
        for j in range(GMLP_GROUPS // 2):
            cols = slice(LANES * j, LANES * (j + 1))
            vcols = slice(GMLP_WIDTH + LANES * j, GMLP_WIDTH + LANES * (j + 1))
            u, _ = _gelu_tanh(proj_ref[:, cols])
            vp, _ = _gelu_tanh(proj_ref[:, vcols])
            sv = _sgu_forward_pair(wm, vp, j) + bias_ref[:, cols]
            cat_ref[:, cols] = (u * sv).astype(cat_ref.dtype)
        o = 2 * GMLP_WIDTH
        tab = tab_ref[...]
        q_r = _rope_apply(proj_ref[:, o:o + ATTN_WIDTH], tab, 1.0)
        k_cur = _rope_apply(proj_ref[:, o + ATTN_WIDTH:o + ATTN_WIDTH + KV_WIDTH], tab, 1.0)
        k_prev = _rope_apply(prev_ref[:, 0:KV_WIDTH], ptab_ref[...], 1.0)
        k_a = jnp.concatenate([k_prev, k_cur], axis=0)
        v_a = jnp.concatenate([prev_ref[:, KV_WIDTH:2 * KV_WIDTH],
                               proj_ref[:, o + ATTN_WIDTH + KV_WIDTH:o + ATTN_WIDTH + 2 * KV_WIDTH]], axis=0)
        k_b = pltpu.roll(k_a, HEAD_DIM, 1)
        v_b = pltpu.roll(v_a, HEAD_DIM, 1)
        bias_t = _attn_bias_t(i == 0)
        lo, hi = _lane_masks((CHUNK, LANES))
        lo2, _ = _lane_masks((2 * CHUNK, LANES))
        for g in range(N_KV_HEADS):
            p_t, _ = _attn_probs_t(_group_dup(k_a, k_b, g, lo2), _group_rows(q_r, g, lo, hi), bias_t,
                                   _sink_row(sink_ref, g))
            o_t = _dot(_group_dup(v_a, v_b, g, lo2).T, p_t)
            for k, pair in enumerate(_pairs_from_rows(o_t.T, lo)):
                c0 = GMLP_WIDTH + LANES * (2 * g + k)
                cat_ref[:, c0:c0 + LANES] = pair.astype(cat_ref.dtype)

    return _hosted_call(
        body, comm, name="mixer_fwd", grid=(nb,),
        out_shape=[jax.ShapeDtypeStruct((seq, D_MODEL), MXU_DTYPE)],
        in_specs=[pl.BlockSpec((CHUNK, IN_PROJ_WIDTH), lambda i: (i, 0)),
                  pl.BlockSpec((CHUNK, 2 * KV_WIDTH), lambda i: (jnp.maximum(i - 1, 0), kv_col)),
                  pl.BlockSpec((CHUNK, 3 * LANES), lambda i: (i, 0)),
                  pl.BlockSpec((CHUNK, 3 * LANES), lambda i: (jnp.maximum(i - 1, 0), 0)),
                  _full((GMLP_GROUPS, CHUNK, CHUNK)), _full((CHUNK, GMLP_WIDTH)),
                  _full((N_Q_HEADS, LANES))],
        out_specs=[pl.BlockSpec((CHUNK, D_MODEL), lambda i: (i, 0))],
        semantics=("arbitrary",),
    )(proj, proj, rope_tab, rope_tab, w_spatial, bias_full, sink_rows)


def _trunk_kernel(x, target, cat, vecs, chip_idx, gathered, local):
    seq = x.shape[0]
    tm = 256
    nj = D_FF // D_MODEL
    out_rows = D_MODEL // N_CHIPS

    def body(chip_ref, x_ref, t_ref, cat_ref, v_ref, g_out, g_w1, g_w2, l_out, l_w1, l_w2,
             dx1_ref, dcat_ref, dmix_ref, h2_ref, r_ref, da_ref, dff_ref, sums_ref,
             wout, w1, w2, a_scr, sem):
        i = pl.program_id(0)

        @pl.when(i == 0)
        def _():
            dsts = ([wout.at[pl.ds(out_rows * k, out_rows)] for k in range(N_CHIPS)]
                    + [w1.at[k] for k in range(N_CHIPS)] + [w2.at[k] for k in range(N_CHIPS)])
            srcs = [(g, l) for g, l in ((g_out, l_out), (g_w1, l_w1), (g_w2, l_w2)) for _ in range(N_CHIPS)]
            for n, (dst, (g, l)) in enumerate(zip(dsts, srcs)):
                k = n % N_CHIPS

                @pl.when(chip_ref[0] == k)
                def _():
                    pltpu.make_async_copy(l, dst, sem.at[n]).start()

                @pl.when(chip_ref[0] != k)
                def _():
                    pltpu.make_async_copy(g.at[k], dst, sem.at[n]).start()
            for n, (dst, (g, l)) in enumerate(zip(dsts, srcs)):
                pltpu.make_async_copy(l, dst, sem.at[n]).wait()
            sums_ref[...] = jnp.zeros_like(sums_ref)

        gate1, shift2, scale2 = v_ref[0:1, :], v_ref[1:2, :], v_ref[2:3, :]
        gate2, g_ffn, g_final = v_ref[3:4, :], v_ref[4:5, :], v_ref[5:6, :]

        mix = _dot(cat_ref[...], wout[...])
        x1 = x_ref[...] + gate1 * mix
        rstd2 = lax.rsqrt(_mean_last(x1 * x1) + EPS)
        xh2 = x1 * rstd2
        n2 = xh2 * g_ffn
        h2b = (n2 * (1.0 + scale2) + shift2).astype(MXU_DTYPE)
        h2_ref[...] = h2b
        ff = jnp.zeros((tm, D_MODEL), F32)
        for j in range(nj):
            a = _dot(h2b, w1[j])
            a_scr[j] = a
            relu = jnp.maximum(a, 0.0)
            rb = (relu * relu).astype(MXU_DTYPE)
            r_ref[:, D_MODEL * j:D_MODEL * (j + 1)] = rb
            ff = ff + _dot(rb, w2[j])
        x2 = x1 + gate2 * ff
        rstd3 = lax.rsqrt(_mean_last(x2 * x2) + EPS)
        xh3 = x2 * rstd3
        err = xh3 * g_final - t_ref[...]
        loss = 0.5 * _rowsum(_mean_last(err * err))
        dy = err * (1.0 / D_MODEL)
        dxh3 = dy * g_final
        dx2 = rstd3 * (dxh3 - xh3 * _mean_last(dxh3 * xh3))
        dffb = (dx2 * gate2).astype(MXU_DTYPE)
        dff_ref[...] = dffb
        dh2 = jnp.zeros((tm, D_MODEL), F32)
        for j in range(nj):
            dr = _dot_nt(dffb, w2[j])
            dab = (dr * (2.0 * jnp.maximum(a_scr[j], 0.0))).astype(MXU_DTYPE)
            da_ref[:, D_MODEL * j:D_MODEL * (j + 1)] = dab
            dh2 = dh2 + _dot_nt(dab, w1[j])
        dn2 = dh2 * (1.0 + scale2)
        dxh2 = dn2 * g_ffn
        dx1 = dx2 + rstd2 * (dxh2 - xh2 * _mean_last(dxh2 * xh2))
        dx1_ref[...] = dx1
        dmixb = (dx1 * gate1).astype(MXU_DTYPE)
        dmix_ref[...] = dmixb
        dcat_ref[...] = _dot_nt(dmixb, wout[...])

        sums_ref[0:1, :] += _rowsum(dh2)
        sums_ref[1:2, :] += _rowsum(dh2 * n2)
        sums_ref[2:3, :] += _rowsum(dx2 * ff)
        sums_ref[3:4, :] += _rowsum(dn2 * xh2)
        sums_ref[4:5, :] += _rowsum(dy * xh3)
        sums_ref[5:6, :] += _rowsum(dx1 * mix)
        sums_ref[6:7, :] += jnp.broadcast_to(loss, (1, D_MODEL))

    tok = lambda w: pl.BlockSpec((tm, w), lambda i, chip: (i, 0))
    return _hosted_call(
        body, None, name="trunk", grid=(seq // tm,), n_prefetch=1,
        out_shape=[jax.ShapeDtypeStruct((seq, D_MODEL), F32), jax.ShapeDtypeStruct((seq, D_MODEL), F32),
                   jax.ShapeDtypeStruct((seq, D_MODEL), MXU_DTYPE), jax.ShapeDtypeStruct((seq, D_MODEL), MXU_DTYPE),
                   jax.ShapeDtypeStruct((seq, D_FF), MXU_DTYPE), jax.ShapeDtypeStruct((seq, D_FF), MXU_DTYPE),
                   jax.ShapeDtypeStruct((seq, D_MODEL), MXU_DTYPE), jax.ShapeDtypeStruct((8, D_MODEL), F32)],
        in_specs=[tok(D_MODEL), tok(D_MODEL), tok(D_MODEL), _full((8, D_MODEL))] + [_any()] * 6,
        out_specs=[tok(D_MODEL), tok(D_MODEL), tok(D_MODEL), tok(D_MODEL), tok(D_FF), tok(D_FF), tok(D_MODEL),
                   _full((8, D_MODEL))],
        scratch_shapes=[pltpu.VMEM((D_MODEL, D_MODEL), MXU_DTYPE), pltpu.VMEM((nj, D_MODEL, D_MODEL), MXU_DTYPE),
                        pltpu.VMEM((nj, D_MODEL, D_MODEL), MXU_DTYPE), pltpu.VMEM((nj, tm, D_MODEL), F32),
                        pltpu.SemaphoreType.DMA((3 * N_CHIPS,))],
        semantics=("arbitrary",),
    )(chip_idx, x, target, cat, vecs, *gathered, *local)


def _mixer_bwd_kernel(proj, rope_tab, dcat, w_spatial, w_spatial_t, bias_full, sink_rows, dev_idx, comm=None):
    seq = proj.shape[0]
    nb = seq // CHUNK
    kv_col = (2 * GMLP_WIDTH + ATTN_WIDTH) // (2 * KV_WIDTH)

    def body(dev_ref, proj_ref, prev_ref, tab_ref, ptab_ref, dcat_ref, w_ref, wt_ref, bias_ref, sink_ref,
             dproj_ref, dw_ref, db_ref, dsink_ref, carry):
        del dev_ref
        step = pl.program_id(0)
        blk = nb - 1 - step

        @pl.when(step == 0)
        def _():
            carry[...] = jnp.zeros_like(carry)
            dw_ref[...] = jnp.zeros_like(dw_ref)
            db_ref[...] = jnp.zeros_like(db_ref)
            dsink_ref[...] = jnp.zeros_like(dsink_ref)

        wm, tril, triu = _masked_spatial(w_ref)
        lo, hi = _lane_masks((CHUNK, LANES))
        lane = lax.broadcasted_iota(jnp.int32, (CHUNK, LANES), 1)
        db = jnp.zeros((CHUNK, LANES), F32)
        for j in range(GMLP_GROUPS // 2):
            cols = slice(LANES * j, LANES * (j + 1))
            vcols = slice(GMLP_WIDTH + LANES * j, GMLP_WIDTH + LANES * (j + 1))
            zu, zv = proj_ref[:, cols], proj_ref[:, vcols]
            u, tu = _gelu_tanh(zu)
            vp, tv = _gelu_tanh(zv)
            sv = _sgu_forward_pair(wm, vp, j) + bias_ref[:, cols]
            dout = dcat_ref[:, cols]
            du = dout * sv
            dsv = dout * u
            dsv_lo, dsv_hi = jnp.where(lo, dsv, 0.0), jnp.where(hi, dsv, 0.0)
            lhs_t = jnp.concatenate([jnp.where(triu, wt_ref[2 * j], 0.0),
                                     jnp.where(triu, wt_ref[2 * j + 1], 0.0)], axis=1)
            dv = _dot(lhs_t, jnp.concatenate([dsv_lo, dsv_hi], axis=0))
            dw_ref[2 * j] += jnp.where(tril, _dot_nt(dsv_lo, vp), 0.0)
            dw_ref[2 * j + 1] += jnp.where(tril, _dot_nt(dsv_hi, vp), 0.0)
            db = db + (jnp.where(lane == 2 * j, jnp.sum(dsv_lo, axis=1, keepdims=True), 0.0)
                       + jnp.where(lane == 2 * j + 1, jnp.sum(dsv_hi, axis=1, keepdims=True), 0.0))
            dproj_ref[:, cols] = (du * _gelu_tanh_grad(zu, tu)).astype(dproj_ref.dtype)
            dproj_ref[:, vcols] = (dv * _gelu_tanh_grad(zv, tv)).astype(dproj_ref.dtype)
        db_ref[...] += db
        o = 2 * GMLP_WIDTH
        tab = tab_ref[...]
        q_r = _rope_apply(proj_ref[:, o:o + ATTN_WIDTH], tab, 1.0)
        k_cur = _rope_apply(proj_ref[:, o + ATTN_WIDTH:o + ATTN_WIDTH + KV_WIDTH], tab, 1.0)
        k_prev = _rope_apply(prev_ref[:, 0:KV_WIDTH], ptab_ref[...], 1.0)
        k_a = jnp.concatenate([k_prev, k_cur], axis=0)
        v_a = jnp.concatenate([prev_ref[:, KV_WIDTH:2 * KV_WIDTH],
                               proj_ref[:, o + ATTN_WIDTH + KV_WIDTH:o + ATTN_WIDTH + 2 * KV_WIDTH]], axis=0)
        k_b = pltpu.roll(k_a, HEAD_DIM, 1)
        v_b = pltpu.roll(v_a, HEAD_DIM, 1)
        bias_t = _attn_bias_t(blk == 0)
        lo2, _ = _lane_masks((2 * CHUNK, LANES))
        dout_b = dcat_ref[:, GMLP_WIDTH:GMLP_WIDTH + ATTN_WIDTH]
        dk_tot, dv_tot, dq_pairs = [], [], []
        for g in range(N_KV_HEADS):
            k_dup, v_dup = _group_dup(k_a, k_b, g, lo2), _group_dup(v_a, v_b, g, lo2)
            q_rows = _group_rows(q_r, g, lo, hi)
            do_rows = _group_rows(dout_b, g, lo, hi)
            p_t, p_sink = _attn_probs_t(k_dup, q_rows, bias_t, _sink_row(sink_ref, g))
            dp_t = _dot_nt(v_dup, do_rows)
            delta = jnp.sum(p_t * dp_t, axis=0, keepdims=True)
            ds_t = p_t * (dp_t - delta) * ATTN_SCALE
            dsink = -p_sink * delta
            for r in range(HEADS_PER_GROUP):
                h = HEADS_PER_GROUP * g + r
                dsink_ref[h:h + 1, :] += jnp.broadcast_to(
                    jnp.sum(dsink[:, LANES * r:LANES * (r + 1)], axis=1, keepdims=True), (1, LANES))
            dk_full = _dot(ds_t, q_rows)
            dv_full = _dot(p_t, do_rows)
            dk_tot.append(dk_full + pltpu.roll(dk_full, HEAD_DIM, 1))
            dv_tot.append(dv_full + pltpu.roll(dv_full, HEAD_DIM, 1))
            dq_t = _dot(k_dup.T, ds_t)
            dq_pairs += _pairs_from_rows(dq_t.T, lo)
        dk_all = jnp.where(lo2, dk_tot[0], dk_tot[1])
        dv_all = jnp.where(lo2, dv_tot[0], dv_tot[1])
        dk_cur = dk_all[CHUNK:, :] + carry[:, 0:KV_WIDTH]
        dv_cur = dv_all[CHUNK:, :] + carry[:, KV_WIDTH:2 * KV_WIDTH]
        carry[:, 0:KV_WIDTH] = dk_all[:CHUNK, :]
        carry[:, KV_WIDTH:2 * KV_WIDTH] = dv_all[:CHUNK, :]
        dq = _rope_apply(jnp.concatenate(dq_pairs, axis=1), tab, -1.0)
        dproj_ref[:, o:o + ATTN_WIDTH] = dq.astype(dproj_ref.dtype)
        dproj_ref[:, o + ATTN_WIDTH:o + ATTN_WIDTH + KV_WIDTH] = (
            _rope_apply(dk_cur, tab, -1.0).astype(dproj_ref.dtype))
        dproj_ref[:, o + ATTN_WIDTH + KV_WIDTH:o + ATTN_WIDTH + 2 * KV_WIDTH] = dv_cur.astype(dproj_ref.dtype)

    rev = lambda i: nb - 1 - i
    slot = lambda shape: pl.BlockSpec((None,) + shape, lambda i, d: (d[0],) + (0,) * len(shape))
    return _hosted_call(
        body, comm, name="mixer_bwd", grid=(nb,), n_prefetch=1,
        out_shape=[jax.ShapeDtypeStruct((seq, IN_PROJ_WIDTH), MXU_DTYPE),
                   jax.ShapeDtypeStruct((N_DEV, GMLP_GROUPS, CHUNK, CHUNK), F32),
                   jax.ShapeDtypeStruct((N_DEV, CHUNK, LANES), F32),
                   jax.ShapeDtypeStruct((N_DEV, N_Q_HEADS, LANES), F32)],
        in_specs=[pl.BlockSpec((CHUNK, IN_PROJ_WIDTH), lambda i, d: (rev(i), 0)),
                  pl.BlockSpec((CHUNK, 2 * KV_WIDTH), lambda i, d: (jnp.maximum(rev(i) - 1, 0), kv_col)),
                  pl.BlockSpec((CHUNK, 3 * LANES), lambda i, d: (rev(i), 0)),
                  pl.BlockSpec((CHUNK, 3 * LANES), lambda i, d: (jnp.maximum(rev(i) - 1, 0), 0)),
                  pl.BlockSpec((CHUNK, D_MODEL), lambda i, d: (rev(i), 0)),
                  _full((GMLP_GROUPS, CHUNK, CHUNK)), _full((GMLP_GROUPS, CHUNK, CHUNK)),
                  _full((CHUNK, GMLP_WIDTH)), _full((N_Q_HEADS, LANES))],
        out_specs=[pl.BlockSpec((CHUNK, IN_PROJ_WIDTH), lambda i, d: (rev(i), 0)),
                   slot((GMLP_GROUPS, CHUNK, CHUNK)), slot((CHUNK, LANES)), slot((N_Q_HEADS, LANES))],
        scratch_shapes=[pltpu.VMEM((CHUNK, 2 * KV_WIDTH), F32)],
        semantics=("arbitrary",),
    )(dev_idx, proj, proj, rope_tab, rope_tab, dcat, w_spatial, w_spatial_t, bias_full, sink_rows)


def _in_proj_bwd_kernel(x, dx1, dproj, vecs, w_in_t, comm=None):
    seq = x.shape[0]
    tm = 512

    def body(x_ref, dx1_ref, dp_ref, v_ref, w_ref, gx_ref, sums_ref):
        @pl.when(pl.program_id(0) == 0)
        def _():
            sums_ref[...] = jnp.zeros_like(sums_ref)

        g_mix, scale1 = v_ref[0:1, :], v_ref[2:3, :]
        dh = _dot(dp_ref[...], w_ref[...])
        xv = x_ref[...]
        rstd = lax.rsqrt(_mean_last(xv * xv) + EPS)
        xh = xv * rstd
        dn1 = dh * (1.0 + scale1)
        dxh = dn1 * g_mix
        gx_ref[...] = dx1_ref[...] + rstd * (dxh - xh * _mean_last(dxh * xh))
        sums_ref[0:1, :] += _rowsum(dh)
        sums_ref[1:2, :] += _rowsum(dh * (xh * g_mix))
        sums_ref[2:3, :] += _rowsum(dn1 * xh)

    return _hosted_call(
        body, comm, name="in_proj_bwd", grid=(seq // tm,),
        out_shape=[jax.ShapeDtypeStruct((seq, D_MODEL), F32), jax.ShapeDtypeStruct((8, D_MODEL), F32)],
        in_specs=[pl.BlockSpec((tm, D_MODEL), lambda i: (i, 0)), pl.BlockSpec((tm, D_MODEL), lambda i: (i, 0)),
                  pl.BlockSpec((tm, IN_PROJ_WIDTH), lambda i: (i, 0)), _full((8, D_MODEL)),
                  _full((IN_PROJ_WIDTH, D_MODEL))],
        out_specs=[pl.BlockSpec((tm, D_MODEL), lambda i: (i, 0)), _full((8, D_MODEL))],
        semantics=("arbitrary",),
    )(x, dx1, dproj, vecs, w_in_t)


class _GradTiles(NamedTuple):
    tm: int
    tn: int
    n_tiles: int
    chips_per_tile: int
    a_index: Callable
    b_index: Callable


def _weight_grad_kernel(a, b, c_idx, name, tiles, comm=None):
    seq = a.shape[0]
    tk = min(seq, 4096)
    nk = seq // tk
    tm, tn, n_tiles, per = tiles.tm, tiles.tn, tiles.n_tiles, tiles.chips_per_tile
    rows = tm // per

    def half(phase, c):
        return phase * c[0] + (1 - phase) * (1 - c[0])

    def body(c_ref, a_ref, b_ref, o_ref, acc, stage, landed, send_sems, recv_sems):
        del c_ref
        phase, t, kk = pl.program_id(0), pl.program_id(1), pl.program_id(2)
        x, y, c, _ = _mesh_place()

        def copy(tile):
            return pltpu.make_async_remote_copy(
                src_ref=stage.at[tile], dst_ref=landed.at[tile], send_sem=send_sems.at[tile],
                recv_sem=recv_sems.at[tile], device_id=(x, y, 1 - c), device_id_type=MESH)

        @pl.when(kk == 0)
        def _():
            acc[...] = jnp.zeros_like(acc)

        acc[...] += _dot_tn(a_ref[...], b_ref[...])

        @pl.when((kk == nk - 1) & (phase == 0))
        def _():
            stage[t] = acc[...].astype(stage.dtype)
            copy(t).start()

        @pl.when((kk == nk - 1) & (phase == 1))
        def _():
            copy(t).wait_recv()
            total = acc[...] + landed[t].astype(F32)
            for q in range(per):
                o_ref[q] = total[rows * q:rows * (q + 1)].astype(o_ref.dtype)

        @pl.when((kk == nk - 1) & (phase == 1) & (t == n_tiles - 1))
        def _():
            for tile in range(n_tiles):
                copy(tile).wait_send()

    out = _hosted_call(
        body, comm, name=name, grid=(2, n_tiles, nk), n_prefetch=1,
        out_shape=[jax.ShapeDtypeStruct((n_tiles * per, rows, tn), GRAD_COMM_DTYPE)],
        in_specs=[pl.BlockSpec((tk, tm), lambda p, t, k, c: (k, tiles.a_index(t, half(p, c)))),
                  pl.BlockSpec((tk, tn), lambda p, t, k, c: (k, tiles.b_index(t, half(p, c))))],
        out_specs=[pl.BlockSpec((per, rows, tn), lambda p, t, k, c: (p * t, 0, 0))],
        scratch_shapes=[pltpu.VMEM((tm, tn), F32), pltpu.VMEM((n_tiles, tm, tn), GRAD_COMM_DTYPE),
                        pltpu.VMEM((n_tiles, tm, tn), GRAD_COMM_DTYPE),
                        pltpu.SemaphoreType.DMA((n_tiles,)), pltpu.SemaphoreType.DMA((n_tiles,))],
        semantics=("arbitrary", "arbitrary", "arbitrary"),
    )(c_idx, a, b)
    return out[0] if comm is None else out


def _row_tile(rows, most=256, sublanes=16):
    return max(t for t in range(sublanes, most + 1, sublanes) if rows % t == 0)


def _adam_update(w, g, m, v):
    m_new = ADAM_B1 * m + (1.0 - ADAM_B1) * g
    v_new = ADAM_B2 * v + (1.0 - ADAM_B2) * (g * g)
    m_hat = m_new / (1.0 - ADAM_B1 ** ADAM_STEP)
    v_hat = v_new / (1.0 - ADAM_B2 ** ADAM_STEP)
    delta = -ADAM_LR * (m_hat / (jnp.sqrt(v_hat) + ADAM_EPS) + ADAM_WD * w)
    return delta, m_new, v_new


def _sum_chips_kernel(own, others, place, name):
    _, r, n = own.shape
    tr = _row_tile(r)

    def body(place_ref, own_ref, oth_ref, o_ref):
        del place_ref
        acc = own_ref[...].astype(F32)
        for k in range(N_CHIPS - 1):
            acc = acc + oth_ref[k].astype(F32)
        o_ref[...] = acc

    return pl.pallas_call(
        body, name=name, out_shape=jax.ShapeDtypeStruct((2, r, n), F32),
        grid_spec=pltpu.PrefetchScalarGridSpec(
            num_scalar_prefetch=1, grid=(r // tr,),
            in_specs=[pl.BlockSpec((None, tr, n), lambda i, p: (p[0], i, 0)),
                      pl.BlockSpec((N_CHIPS - 1, tr, n), lambda i, p: (0, i, 0))],
            out_specs=pl.BlockSpec((None, tr, n), lambda i, p: (p[1], i, 0))),
        compiler_params=_params("parallel"),
    )(place, own, others)


def _adam_kernel(w, g, m, v, name):
    r, n = w.shape
    by_columns = g.shape[1] == r
    tr, tn = _row_tile(g.shape[1]), g.shape[2]

    def body(w_ref, g_ref, m_ref, v_ref, g_out, d_ref, mo_ref, vo_ref):
        gv = g_ref[...]
        g_out[...] = gv
        d_ref[...], mo_ref[...], vo_ref[...] = _adam_update(w_ref[...], gv, m_ref[...], v_ref[...])

    steps = g.shape[1] // tr
    spec = pl.BlockSpec((tr, tn), (lambda h, i: (i, h)) if by_columns else (lambda h, i: (h * steps + i, 0)))
    return pl.pallas_call(
        body, name=name, grid=(2, steps), out_shape=[jax.ShapeDtypeStruct((r, n), F32)] * 4,
        in_specs=[spec, pl.BlockSpec((None, tr, tn), lambda h, i: (h, i, 0)), spec, spec], out_specs=[spec] * 4,
        compiler_params=_params("parallel", "parallel"),
    )(w, g, m, v)


SMALL_PARAMS = ("b_ada", "g_mix", "g_ffn", "g_final", "b_spatial", "sinks", "w_spatial")


def _small_update_kernel(gathered, params):
    shapes = [params[nm][0].shape for nm in SMALL_PARAMS]

    def body(*refs):
        g_refs, refs = refs[:5], refs[5:]
        p_refs, refs = refs[:3 * len(SMALL_PARAMS)], refs[3 * len(SMALL_PARAMS):]
        loss_ref, o_refs = refs[0], refs[1:]

        def total(ref):
            acc = ref[0]
            for k in range(1, N_DEV):
                acc = acc + ref[k]
            return acc

        s1, s2, db, ds, dw = (total(r) for r in g_refs)
        loss_ref[...] = jnp.broadcast_to(s2[6:7, 0:1], loss_ref.shape)
        grads = {"b_ada": [s1[0:1], s1[1:2], s2[5:6], s2[0:1], s2[1:2], s2[2:3]], "g_mix": [s1[2:3]],
                 "g_ffn": [s2[3:4]], "g_final": [s2[4:5]], "b_spatial": [db.T[0:GMLP_GROUPS]],
                 "w_spatial": [dw]}
        lane = lax.broadcasted_iota(jnp.int32, (1, LANES), 1)
        sink_row = jnp.zeros((1, LANES), F32)
        for h in range(N_Q_HEADS):
            sink_row = sink_row + jnp.where(lane == h, ds[h:h + 1, :], 0.0)
        grads["sinks"] = [sink_row[:, 0:N_Q_HEADS]]
        for i, nm in enumerate(SMALL_PARAMS):
            w_ref, m_ref, v_ref = p_refs[3 * i:3 * i + 3]
            outs = o_refs[4 * i:4 * i + 4]
            width = grads[nm][0].shape[1]
            for k, g in enumerate(grads[nm]):
                cols = slice(width * k, width * (k + 1))
                upd = _adam_update(w_ref[:, cols], g, m_ref[:, cols], v_ref[:, cols])
                for o_ref, val in zip(outs, (g,) + upd):
                    o_ref[:, cols] = val

    flat = [a for nm in SMALL_PARAMS for a in params[nm]]
    out_shape = [jax.ShapeDtypeStruct((8, LANES), F32)]
    out_shape += [jax.ShapeDtypeStruct(s, F32) for s in shapes for _ in range(4)]
    outs = pl.pallas_call(
        body, name="small_update", grid=(1,), out_shape=out_shape,
        in_specs=[_full(g.shape) for g in gathered] + [_full(a.shape) for a in flat],
        out_specs=[_full(s.shape) for s in out_shape],
        compiler_params=_params("arbitrary"),
    )(*gathered, *flat)
    return {nm: outs[1 + 4 * i:5 + 4 * i] for i, nm in enumerate(SMALL_PARAMS)}, outs[0]


def _ada_update_kernel(act_t, dmod, w, m, v):
    r, n = w.shape
    tr = 256

    def body(a_ref, d_ref, w_ref, m_ref, v_ref, g_ref, dl_ref, mo_ref, vo_ref):
        g = _dot(a_ref[...], d_ref[...])
        g_ref[...] = g
        dl_ref[...], mo_ref[...], vo_ref[...] = _adam_update(w_ref[...], g, m_ref[...], v_ref[...])

    spec = pl.BlockSpec((tr, n), lambda i: (i, 0))
    return pl.pallas_call(
        body, name="ada_update", grid=(r // tr,), out_shape=[jax.ShapeDtypeStruct((r, n), F32)] * 4,
        in_specs=[pl.BlockSpec((tr, N_DEV), lambda i: (i, 0)), _full((N_DEV, n)), spec, spec, spec],
        out_specs=[spec] * 4, compiler_params=_params("parallel"),
    )(act_t, dmod, w, m, v)


def kernel(x, c, positions, w_ada, b_ada, g_mix, w_in, w_spatial, b_spatial, sinks, w_out, g_ffn, w_ff1, w_ff2, g_final, loss_target, m_w_ada, m_b_ada, m_g_mix, m_w_in, m_w_spatial, m_b_spatial, m_sinks, m_w_out, m_g_ffn, m_w_ff1, m_w_ff2, m_g_final, v_w_ada, v_b_ada, v_g_mix, v_w_in, v_w_spatial, v_b_spatial, v_sinks, v_w_out, v_g_ffn, v_w_ff1, v_w_ff2, v_g_final):
    xi, yi, ci = lax.axis_index("x"), lax.axis_index("y"), lax.axis_index("c")
    chip = 2 * xi + yi
    dev = 2 * chip + ci
    seq = x.shape[1]
    x2, tgt = x[0], loss_target[0]
    ada_cols = w_ada.shape[2]

    c_all = _all_gather8([c], "gather_c")[0].reshape(N_DEV, D_MODEL)
    b_shard = lax.dynamic_slice(b_ada, (0, chip * ada_cols), (1, ada_cols))
    mod_part, act = _mod_kernel(c_all, w_ada[0], b_shard)
    mod_all, = _all_gather8([mod_part], "gather_mod")
    mod_me = lax.dynamic_index_in_dim(mod_all[0::2], dev, axis=1, keepdims=False)
    mod_me = mod_me.reshape(N_MOD, D_MODEL)
    shift1, scale1, gate1, shift2, scale2, gate2 = (mod_me[k:k + 1] for k in range(N_MOD))

    big = {"w_in": tuple(a[0].T for a in (w_in, m_w_in, v_w_in)),
           "w_out": (w_out[0], m_w_out[0], v_w_out[0]), "w_ff1": (w_ff1[0], m_w_ff1[0], v_w_ff1[0]),
           "w_ff2": (w_ff2[0], m_w_ff2[0], v_w_ff2[0])}

    def halves(nm):
        r, n = big[nm][0].shape
        return big[nm][0].astype(WEIGHT_COMM_DTYPE).reshape(2, r // 2, n)

    w_in_t, = _all_gather8([halves("w_in")], "gather_w_in", split=True)
    w_in_t = w_in_t.reshape(IN_PROJ_WIDTH, D_MODEL)

    zeros_row = jnp.zeros((1, D_MODEL), F32)
    vecs1 = jnp.concatenate([g_mix, shift1, scale1] + [zeros_row] * 5, axis=0)
    vecs2 = jnp.concatenate([gate1, shift2, scale2, gate2, g_ffn, g_final.reshape(1, D_MODEL)]
                            + [zeros_row] * 2, axis=0)
    bias_full = jnp.repeat(b_spatial[0].T, HEAD_DIM, axis=1)
    sink_rows = jnp.broadcast_to(sinks[0][:, None], (N_Q_HEADS, LANES))
    inv_freq = ROPE_THETA ** (-jnp.arange(0, ROT_DIM, 2, dtype=F32) / ROT_DIM)
    rope_tab = _rope_lane_tables(*_rope_angle_kernel(positions, inv_freq.reshape(ROT_DIM // 2, 1)))

    trunk_weights = ["w_out", "w_ff1", "w_ff2"]
    shards = [halves(nm) for nm in trunk_weights]
    proj, hb, *staged = _in_proj_kernel(x2, vecs1, w_in_t, comm=_gather2d_first(shards))
    cat, *staged = _mixer_fwd_kernel(proj, rope_tab, w_spatial[0], bias_full, sink_rows,
                                     comm=_gather2d_second(staged, shards))
    staged = _gather_forward(staged, "gather_forward")
    dx1, dcat, dmix, h2b, rb, dab, dffb, sums2 = _trunk_kernel(
        x2, tgt, cat, vecs2, chip.reshape(1).astype(jnp.int32),
        [g.reshape((N_CHIPS,) + big[nm][0].shape) for nm, g in zip(trunk_weights, staged)],
        [s.reshape(big[nm][0].shape) for nm, s in zip(trunk_weights, shards)])

    c_idx = ci.reshape(1).astype(jnp.int32)
    place = jnp.stack([chip, ci]).astype(jnp.int32)
    half_d = D_MODEL // 2
    cs_ff2 = _weight_grad_kernel(rb, dffb, c_idx, "dw_ff2",
                                 _GradTiles(D_MODEL, half_d, N_CHIPS, 1, lambda t, h: t, lambda t, h: h))
    cs_ff1, sc_ff2 = _weight_grad_kernel(
        h2b, dab, c_idx, "dw_ff1",
        _GradTiles(D_MODEL, half_d, N_CHIPS, 1, lambda t, h: 0, lambda t, h: 2 * t + h),
        comm=_scatter_job([cs_ff2]))
    cs_out = _weight_grad_kernel(cat, dmix, c_idx, "dw_out",
                                 _GradTiles(D_MODEL, half_d, 1, N_CHIPS, lambda t, h: 0, lambda t, h: h))
    dproj, dw_spatial, db_lanes, dsink_rows, sc_ff1, sc_out = _mixer_bwd_kernel(
        proj, rope_tab, dcat, w_spatial[0], w_spatial[0].transpose(0, 2, 1), bias_full, sink_rows,
        dev.reshape(1).astype(jnp.int32), comm=_scatter_job([cs_ff1, cs_out]))
    cs_in, *small_stage1 = _weight_grad_kernel(
        dproj, hb, c_idx, "dw_in",
        _GradTiles(2 * W_IN_BLOCK, half_d, N_CHIPS // 2, 2, lambda t, h: t, lambda t, h: h),
        comm=_gather_job([db_lanes, dsink_rows, dw_spatial.reshape(N_DEV, GMLP_GROUPS * CHUNK, CHUNK)]))
    grad_x, sums1 = _in_proj_bwd_kernel(x2, dx1, dproj, vecs1, w_in_t)
    sc_in, = _run_comm(_scatter_job([cs_in]), "grad_to_chips_w_in")

    names = ["w_in", "w_out", "w_ff1", "w_ff2"]
    totals = [_sum_chips_kernel(own, oth, place, "grad_sum_" + nm)
              for nm, own, oth in zip(names, [cs_in, cs_out, cs_ff1, cs_ff2], [sc_in, sc_out, sc_ff1, sc_ff2])]
    shared = _sibling_share(totals, "grad_share")
    big_out = {}
    for nm, g in zip(names, shared):
        w, m, v = big[nm]
        outs = _adam_kernel(w, g, m, v, "adam_" + nm)
        big_out[nm] = tuple((t.T if nm == "w_in" else t)[None] for t in outs)

    small = {"b_ada": (b_ada, m_b_ada, v_b_ada), "g_mix": (g_mix, m_g_mix, v_g_mix),
             "g_ffn": (g_ffn, m_g_ffn, v_g_ffn), "g_final": (g_final, m_g_final, v_g_final),
             "b_spatial": (b_spatial, m_b_spatial, v_b_spatial), "sinks": (sinks, m_sinks, v_sinks),
             "w_spatial": (w_spatial, m_w_spatial, v_w_spatial)}
    flat_shape = {"g_final": (1, D_MODEL), "b_spatial": (GMLP_GROUPS, CHUNK), "w_spatial": (GMLP_GROUPS * CHUNK, CHUNK)}
    gathered = _all_gather8([sums1, sums2], "gather_small", forward=small_stage1)
    small_out, loss_tile = _small_update_kernel(
        gathered, {nm: tuple(a.reshape(flat_shape.get(nm, a.shape)) for a in small[nm]) for nm in small})
    small_out = {nm: [o.reshape(small[nm][0].shape) for o in small_out[nm]] for nm in small}
    loss = loss_tile[0, 0]

    g1, g2 = gathered[0], gathered[1]
    dmod_all = jnp.concatenate([g1[:, 0], g1[:, 1], g2[:, 5], g2[:, 0], g2[:, 1], g2[:, 2]], axis=1)
    dmod_cols = lax.dynamic_slice(dmod_all, (0, chip * ada_cols), (N_DEV, ada_cols))
    ada = _ada_update_kernel(act.T, dmod_cols, w_ada[0], m_w_ada[0], v_w_ada[0])
    big_out["w_ada"] = tuple(t[None] for t in ada)

    order = ["w_ada", "b_ada", "g_mix", "w_in", "w_spatial", "b_spatial", "sinks", "w_out", "g_ffn",
             "w_ff1", "w_ff2", "g_final"]

    def leaf(nm, k):
        return big_out[nm][k] if nm in big_out else small_out[nm][k]

    outs = [loss, grad_x[None]]
    for k in range(4):
        outs += [leaf(nm, k) for nm in order]
    return tuple(outs)
```

```python
import math
from typing import Callable, NamedTuple

import jax
import jax.numpy as jnp
from jax import lax
from jax.experimental import pallas as pl
from jax.experimental.pallas import tpu as pltpu

F32 = jnp.float32
MXU_DTYPE = jnp.bfloat16
WEIGHT_COMM_DTYPE = jnp.bfloat16
GRAD_COMM_DTYPE = jnp.bfloat16

D_MODEL = 1024
D_FF = 4096
HEAD_DIM = 64
GMLP_GROUPS = 8
GMLP_WIDTH = 512
CHUNK = 128
N_Q_HEADS = 8
N_KV_HEADS = 2
ATTN_WIDTH = 512
KV_WIDTH = 128
ROT_DIM = 16
ROPE_THETA = 500000.0
IN_PROJ_WIDTH = 1792
N_MOD = 6
EPS = 1e-5
N_CHIPS = 4
N_DEV = 8
LANES = 128
W_IN_BLOCK = IN_PROJ_WIDTH // N_CHIPS

ADAM_LR = 0.001
ADAM_B1 = 0.9
ADAM_B2 = 0.999
ADAM_EPS = 1e-08
ADAM_WD = 0.01
ADAM_STEP = 10

VMEM_LIMIT_BYTES = 58 * 1024 * 1024
MESH = pl.DeviceIdType.MESH


def _params(*semantics):
    return pltpu.CompilerParams(dimension_semantics=semantics, vmem_limit_bytes=VMEM_LIMIT_BYTES)


def _dot(a, b):
    return jnp.dot(a.astype(MXU_DTYPE), b.astype(MXU_DTYPE), preferred_element_type=F32)


def _dot_nt(a, b):
    return lax.dot_general(a.astype(MXU_DTYPE), b.astype(MXU_DTYPE), (((1,), (1,)), ((), ())),
                           preferred_element_type=F32)


def _dot_tn(a, b):
    return lax.dot_general(a.astype(MXU_DTYPE), b.astype(MXU_DTYPE), (((0,), (0,)), ((), ())),
                           preferred_element_type=F32)


def _full(shape):
    return pl.BlockSpec(shape, lambda *_: (0,) * len(shape))


def _any():
    return pl.BlockSpec(memory_space=pl.ANY)


def _rowsum(v):
    return jnp.sum(v, axis=0, keepdims=True)


def _mean_last(v):
    return jnp.mean(v, axis=-1, keepdims=True)


class _Comm(NamedTuple):
    operands: tuple
    out_shapes: tuple
    n_sems: int
    make: Callable
    in_place: int = 0


def _hosted_call(body, comm, *, name, grid, in_specs, out_shape, out_specs, scratch_shapes=(), semantics,
                 n_prefetch=0):
    if comm is None:
        return pl.pallas_call(
            body, name=name, out_shape=out_shape, compiler_params=_params(*semantics),
            grid_spec=pltpu.PrefetchScalarGridSpec(
                num_scalar_prefetch=n_prefetch, grid=grid, in_specs=in_specs, out_specs=out_specs,
                scratch_shapes=list(scratch_shapes)))
    n_in, n_out, n_scr = len(in_specs), len(out_shape), len(scratch_shapes)
    k_in, k_out = len(comm.operands), len(comm.out_shapes)

    def hosted(*refs):
        prefetched, refs = refs[:n_prefetch], refs[n_prefetch:]
        ins, refs = refs[:n_in], refs[n_in:]
        c_ins, refs = refs[:k_in], refs[k_in:]
        outs, refs = refs[:n_out], refs[n_out:]
        c_outs, refs = refs[:k_out], refs[k_out:]
        scratch, (send_sems, recv_sems) = refs[:n_scr], refs[n_scr:]
        first, last = None, None
        for d, size in enumerate(grid):
            at_start, at_end = pl.program_id(d) == 0, pl.program_id(d) == size - 1
            first = at_start if first is None else first & at_start
            last = at_end if last is None else last & at_end

        @pl.when(first)
        def _():
            for cp in comm.make(c_ins, c_outs, send_sems, recv_sems)[0]:
                cp.start()

        body(*prefetched, *ins, *outs, *scratch)

        @pl.when(last)
        def _():
            for wait in comm.make(c_ins, c_outs, send_sems, recv_sems)[1]:
                wait()

    aliases = {n_prefetch + n_in + i: n_out + i for i in range(comm.in_place)}
    call = pl.pallas_call(
        hosted, name=name, out_shape=list(out_shape) + list(comm.out_shapes),
        compiler_params=_params(*semantics), input_output_aliases=aliases,
        grid_spec=pltpu.PrefetchScalarGridSpec(
            num_scalar_prefetch=n_prefetch, grid=grid, in_specs=list(in_specs) + [_any()] * k_in,
            out_specs=list(out_specs) + [_any()] * k_out,
            scratch_shapes=list(scratch_shapes) + [pltpu.SemaphoreType.DMA((comm.n_sems,)),
                                                    pltpu.SemaphoreType.DMA((comm.n_sems,))]))
    return lambda *args: call(*args, *comm.operands)


def _mesh_place():
    x, y, c = lax.axis_index("x"), lax.axis_index("y"), lax.axis_index("c")
    return x, y, c, [(1 - x, y), (x, 1 - y), (1 - x, 1 - y)]


def _gather_job(bufs):
    per = 4

    def make(ins, outs, send_sems, recv_sems):
        del ins
        x, y, c, chips = _mesh_place()
        starts, waits = [], []
        for a, out in enumerate(outs):
            mine = src = out.at[4 * x + 2 * y + c]
            to = [(x, y, 1 - c)] + [(px, py, c) for px, py in chips]
            sends = [pltpu.make_async_remote_copy(
                src_ref=src, dst_ref=mine, send_sem=send_sems.at[per * a + k],
                recv_sem=recv_sems.at[per * a + k], device_id=dev, device_id_type=MESH)
                for k, dev in enumerate(to)]
            recvs = [pltpu.make_async_remote_copy(
                src_ref=src, dst_ref=out.at[4 * px + 2 * py + pc], send_sem=send_sems.at[per * a + k],
                recv_sem=recv_sems.at[per * a + k], device_id=(px, py, pc), device_id_type=MESH)
                for k, (px, py, pc) in enumerate(to)]
            starts += sends
            waits += [s.wait_send for s in sends] + [r.wait_recv for r in recvs]
        return starts, waits

    shapes = tuple(jax.ShapeDtypeStruct(b.shape, b.dtype) for b in bufs)
    return _Comm(tuple(bufs), shapes, per * len(bufs), make, in_place=len(bufs))


def _slots(x, y, c):
    return 4 * x + 2 * y + c, 4 * (1 - x) + 2 * y + c, 4 * x + 2 * (1 - y) + c, 4 * (1 - x) + 2 * (1 - y) + c


def _gather2d_first(halves):
    per = 2

    def make(ins, outs, send_sems, recv_sems):
        x, y, c, _ = _mesh_place()
        me, xn, yn, _ = _slots(x, y, c)
        starts, waits = [], []
        for a, (src, out) in enumerate(zip(ins, outs)):
            blk = src.at[c]
            rows = blk.shape[0] // 2
            upper, lower = pl.ds(0, rows), pl.ds(rows, rows)

            def copy(k, src_ref, dst_ref, dev, a=a):
                return pltpu.make_async_remote_copy(
                    src_ref=src_ref, dst_ref=dst_ref, send_sem=send_sems.at[per * a + k],
                    recv_sem=recv_sems.at[per * a + k], device_id=dev, device_id_type=MESH)

            sends = [copy(0, blk.at[upper], out.at[me, upper], (1 - x, y, c)),
                     copy(1, blk.at[lower], out.at[me, lower], (x, 1 - y, c))]
            recvs = [copy(0, blk.at[upper], out.at[xn, upper], (1 - x, y, c)),
                     copy(1, blk.at[lower], out.at[yn, lower], (x, 1 - y, c))]
            starts += sends
            waits += [s.wait_send for s in sends] + [r.wait_recv for r in recvs]
        return starts, waits

    shapes = tuple(jax.ShapeDtypeStruct((N_DEV,) + h.shape[1:], h.dtype) for h in halves)
    return _Comm(tuple(halves), shapes, per * len(halves), make)


def _gather2d_second(bufs, halves):
    per = 4
    n_arr = len(bufs)

    def make(ins, outs, send_sems, recv_sems):
        x, y, c, _ = _mesh_place()
        me, xn, yn, dg = _slots(x, y, c)
        starts, waits = [], []
        for a, buf in enumerate(outs):
            own = ins[n_arr + a].at[c]
            rows = buf.shape[1] // 2
            upper, lower = pl.ds(0, rows), pl.ds(rows, rows)
            plan = [(own.at[upper], me, upper, (x, 1 - y, c), yn), (buf.at[xn, upper], xn, upper, (x, 1 - y, c), dg),
                    (own.at[lower], me, lower, (1 - x, y, c), xn), (buf.at[yn, lower], yn, lower, (1 - x, y, c), dg)]
            for k, (src, slot, part, dev, landing) in enumerate(plan):
                sems = dict(send_sem=send_sems.at[per * a + k], recv_sem=recv_sems.at[per * a + k],
                            device_id=dev, device_id_type=MESH)
                send = pltpu.make_async_remote_copy(src_ref=src, dst_ref=buf.at[slot, part], **sems)
                arrival = pltpu.make_async_remote_copy(src_ref=src, dst_ref=buf.at[landing, part], **sems)
                starts.append(send)
                waits += [send.wait_send, arrival.wait_recv]
        return starts, waits

    shapes = tuple(jax.ShapeDtypeStruct(b.shape, b.dtype) for b in bufs)
    return _Comm(tuple(bufs) + tuple(halves), shapes, per * n_arr, make, in_place=n_arr)


def _gather_forward(bufs, name):
    n_arr = len(bufs)

    def body(*refs):
        outs = refs[n_arr:2 * n_arr]
        send_sems, recv_sems = refs[2 * n_arr:]
        x, y, c, chips = _mesh_place()
        sends, recvs = [], []
        for a, buf in enumerate(outs):
            for j, (px, py) in enumerate(chips):
                mine, theirs = buf.at[4 * px + 2 * py + c], buf.at[4 * px + 2 * py + 1 - c]
                sems = dict(send_sem=send_sems.at[3 * a + j], recv_sem=recv_sems.at[3 * a + j],
                            device_id=(x, y, 1 - c), device_id_type=MESH)
                sends.append(pltpu.make_async_remote_copy(src_ref=mine, dst_ref=mine, **sems))
                recvs.append(pltpu.make_async_remote_copy(src_ref=mine, dst_ref=theirs, **sems))
        for cp in sends:
            cp.start()
        for s, r in zip(sends, recvs):
            s.wait_send()
            r.wait_recv()

    return pl.pallas_call(
        body, name=name, out_shape=[jax.ShapeDtypeStruct(b.shape, b.dtype) for b in bufs],
        in_specs=[_any()] * n_arr, out_specs=[_any()] * n_arr,
        input_output_aliases={a: a for a in range(n_arr)},
        scratch_shapes=[pltpu.SemaphoreType.DMA((3 * n_arr,)), pltpu.SemaphoreType.DMA((3 * n_arr,))],
    )(*bufs)


def _scatter_job(chip_sums):
    def make(ins, outs, send_sems, recv_sems):
        x, y, c, chips = _mesh_place()
        copies = [pltpu.make_async_remote_copy(
            src_ref=src.at[2 * px + py], dst_ref=out.at[j], send_sem=send_sems.at[3 * a + j],
            recv_sem=recv_sems.at[3 * a + j], device_id=(px, py, c), device_id_type=MESH)
            for a, (src, out) in enumerate(zip(ins, outs)) for j, (px, py) in enumerate(chips)]
        return copies, [cp.wait for cp in copies]

    return _Comm(tuple(chip_sums), tuple(jax.ShapeDtypeStruct((3,) + s.shape[1:], s.dtype) for s in chip_sums),
                 3 * len(chip_sums), make)


def _all_gather8(blocks, name, split=False, forward=(), riders=()):
    n_arr, n_fwd = len(blocks), len(forward)
    splits = list(split) if isinstance(split, (list, tuple)) else [split] * n_arr
    rider_in = sum(len(r.operands) for r in riders)
    rider_out = sum(len(r.out_shapes) for r in riders)

    def body(*refs):
        x_refs, refs = refs[:n_arr], refs[n_arr + n_fwd:]
        r_ins, refs = refs[:rider_in], refs[rider_in:]
        out_refs, refs = refs[:n_arr], refs[n_arr:]
        fwd_refs, refs = refs[:n_fwd], refs[n_fwd:]
        r_outs, refs = refs[:rider_out], refs[rider_out:]
        (send_sems, recv_sems, local_sems), rider_sems = refs[:3], refs[3:]
        x, y, c, chips = _mesh_place()
        me, sibling = (x, y, c), (x, y, 1 - c)
        rider_waits, i0, o0 = [], 0, 0
        for n, job in enumerate(riders):
            k_in, k_out = len(job.operands), len(job.out_shapes)
            starts, waits = job.make(r_ins[i0:i0 + k_in], r_outs[o0:o0 + k_out],
                                     rider_sems[2 * n], rider_sems[2 * n + 1])
            for cp in starts:
                cp.start()
            rider_waits += waits
            i0, o0 = i0 + k_in, o0 + k_out
        passing = []
        for f, buf in enumerate(fwd_refs):
            for j, (px, py) in enumerate(chips):
                mine, theirs = buf.at[4 * px + 2 * py + c], buf.at[4 * px + 2 * py + 1 - c]
                sems = dict(send_sem=send_sems.at[7 * n_arr + 3 * f + j], recv_sem=recv_sems.at[7 * n_arr + 3 * f + j],
                            device_id=sibling, device_id_type=MESH)
                passing.append((pltpu.make_async_remote_copy(src_ref=mine, dst_ref=mine, **sems),
                                pltpu.make_async_remote_copy(src_ref=mine, dst_ref=theirs, **sems)))
        for send, _ in passing:
            send.start()
        arrays = []
        for a, (x_ref, out_ref) in enumerate(zip(x_refs, out_refs)):
            src_mine = x_ref.at[c] if splits[a] else x_ref

            def copy(k, blk, to, src=None, a=a, out_ref=out_ref):
                dst = out_ref.at[4 * blk[0] + 2 * blk[1] + blk[2]]
                return pltpu.make_async_remote_copy(
                    src_ref=dst if src is None else src, dst_ref=dst,
                    send_sem=send_sems.at[7 * a + k], recv_sem=recv_sems.at[7 * a + k],
                    device_id=to, device_id_type=MESH)

            mine = pltpu.make_async_copy(src_mine, out_ref.at[4 * x + 2 * y + c], local_sems.at[a])
            mine.start()
            first = [copy(0, me, sibling, src=src_mine)]
            first += [copy(1 + j, me, (*chip, c), src=src_mine) for j, chip in enumerate(chips)]
            for cp in first:
                cp.start()
            arrays.append((copy, mine, first))
        sent = []
        for copy, mine, first in arrays:
            passed = [copy(4 + j, (*chip, c), sibling) for j, chip in enumerate(chips)]
            for j, chip in enumerate(chips):
                copy(1 + j, (*chip, c), me).wait_recv()
                passed[j].start()
            sent += first + passed
        for copy, mine, first in arrays:
            copy(0, sibling, me).wait_recv()
            for j, chip in enumerate(chips):
                copy(4 + j, (*chip, 1 - c), me).wait_recv()
            mine.wait()
        for cp in sent:
            cp.wait_send()
        for send, arrival in passing:
            send.wait_send()
            arrival.wait_recv()
        for wait in rider_waits:
            wait()

    n_sems = 7 * n_arr + 3 * n_fwd
    rider_operands = [a for r in riders for a in r.operands]
    rider_shapes = [s for r in riders for s in r.out_shapes]
    return pl.pallas_call(
        body, name=name,
        out_shape=[jax.ShapeDtypeStruct((N_DEV,) + tuple(b.shape[1:] if s else b.shape), b.dtype)
                   for b, s in zip(blocks, splits)]
        + [jax.ShapeDtypeStruct(f.shape, f.dtype) for f in forward] + rider_shapes,
        in_specs=[_any()] * (n_arr + n_fwd + rider_in), out_specs=[_any()] * (n_arr + n_fwd + rider_out),
        input_output_aliases={n_arr + f: n_arr + f for f in range(n_fwd)},
        scratch_shapes=[pltpu.SemaphoreType.DMA((n_sems,)), pltpu.SemaphoreType.DMA((n_sems,)),
                        pltpu.SemaphoreType.DMA((n_arr,))]
        + [pltpu.SemaphoreType.DMA((r.n_sems,)) for r in riders for _ in range(2)],
    )(*blocks, *forward, *rider_operands)


def _sibling_share(bufs, name):
    n_arr = len(bufs)

    def body(*refs):
        out_refs = refs[n_arr:2 * n_arr]
        send_sems, recv_sems = refs[2 * n_arr:]
        x, y, c = lax.axis_index("x"), lax.axis_index("y"), lax.axis_index("c")
        copies = [pltpu.make_async_remote_copy(
            src_ref=out_refs[a].at[c], dst_ref=out_refs[a].at[c],
            send_sem=send_sems.at[a], recv_sem=recv_sems.at[a],
            device_id=(x, y, 1 - c), device_id_type=MESH) for a in range(n_arr)]
        for cp in copies:
            cp.start()
        for a in range(n_arr):
            pltpu.make_async_remote_copy(
                src_ref=out_refs[a].at[c], dst_ref=out_refs[a].at[1 - c],
                send_sem=send_sems.at[a], recv_sem=recv_sems.at[a],
                device_id=(x, y, 1 - c), device_id_type=MESH).wait()

    return pl.pallas_call(
        body, name=name,
        out_shape=[jax.ShapeDtypeStruct(b.shape, b.dtype) for b in bufs],
        in_specs=[_any()] * n_arr, out_specs=[_any()] * n_arr,
        input_output_aliases={a: a for a in range(n_arr)},
        scratch_shapes=[pltpu.SemaphoreType.DMA((n_arr,)), pltpu.SemaphoreType.DMA((n_arr,))],
    )(*bufs)


def _gelu_tanh(z):
    k = math.sqrt(2.0 / math.pi)
    t = jnp.tanh(k * (z + 0.044715 * (z * z * z)))
    return 0.5 * z * (1.0 + t), t


def _gelu_tanh_grad(z, t):
    k = math.sqrt(2.0 / math.pi)
    return 0.5 * (1.0 + t) + 0.5 * z * (1.0 - t * t) * (k * (1.0 + 3.0 * 0.044715 * (z * z)))


def _rope_angle_kernel(pos_row, invf_col):
    seq = pos_row.shape[1]

    def body(p_ref, f_ref, cos_ref, sin_ref):
        ang = p_ref[...].astype(F32) * f_ref[...]
        cos_ref[...] = jnp.cos(ang)
        sin_ref[...] = jnp.sin(ang)

    return pl.pallas_call(
        body, name="rope_angles", grid=(1,), out_shape=[jax.ShapeDtypeStruct((ROT_DIM // 2, seq), F32)] * 2,
        in_specs=[_full((1, seq)), _full((ROT_DIM // 2, 1))], out_specs=[_full((ROT_DIM // 2, seq))] * 2,
        compiler_params=_params("arbitrary"),
    )(pos_row, invf_col)


def _rope_lane_tables(cos, sin):
    cos_t, sin_t = cos.T, sin.T
    seq, half = cos_t.shape
    ones = jnp.ones((seq, HEAD_DIM - ROT_DIM), F32)
    c64 = jnp.concatenate([cos_t, cos_t, ones], axis=1)
    s1 = jnp.concatenate([sin_t, jnp.zeros((seq, HEAD_DIM - half), F32)], axis=1)
    s2 = jnp.concatenate([jnp.zeros((seq, half), F32), sin_t, jnp.zeros((seq, HEAD_DIM - ROT_DIM), F32)], axis=1)
    return jnp.concatenate([jnp.tile(t, (1, LANES // HEAD_DIM)) for t in (c64, s1, s2)], axis=1)


def _rope_apply(t, tab, sign):
    reps = t.shape[1] // LANES
    c_tab, s1, s2 = (jnp.tile(tab[:, LANES * k:LANES * (k + 1)], (1, reps)) if reps > 1
                     else tab[:, LANES * k:LANES * (k + 1)] for k in range(3))
    half = ROT_DIM // 2
    up = pltpu.roll(t, t.shape[1] - half, 1)
    down = pltpu.roll(t, half, 1)
    return t * c_tab + sign * (down * s2 - up * s1)


def _lane_masks(shape):
    lane = lax.broadcasted_iota(jnp.int32, shape, 1)
    return lane < HEAD_DIM, lane >= HEAD_DIM


HEADS_PER_GROUP = N_Q_HEADS // N_KV_HEADS
ATTN_SCALE = 1.0 / math.sqrt(HEAD_DIM)


def _attn_bias_t(first_block):
    kj = lax.broadcasted_iota(jnp.int32, (2 * CHUNK, CHUNK), 0)
    qi = lax.broadcasted_iota(jnp.int32, (2 * CHUNK, CHUNK), 1)
    ok = (kj > qi) & (kj <= qi + CHUNK) & (jnp.logical_not(first_block) | (kj >= CHUNK))
    return jnp.tile(jnp.where(ok, 0.0, -jnp.inf), (1, HEADS_PER_GROUP))


def _group_rows(x, g, lo, hi):
    rows = []
    for r in range(HEADS_PER_GROUP):
        h = HEADS_PER_GROUP * g + r
        pair = x[:, LANES * (h // 2):LANES * (h // 2 + 1)]
        rows.append(jnp.where(hi if h % 2 else lo, pair, 0.0))
    return jnp.concatenate(rows, axis=0)


def _pairs_from_rows(rows, lo):
    return [jnp.where(lo, rows[2 * CHUNK * k:2 * CHUNK * k + CHUNK], rows[2 * CHUNK * k + CHUNK:2 * CHUNK * (k + 1)])
            for k in range(HEADS_PER_GROUP // 2)]


def _group_dup(a, b, g, lo2):
    return jnp.where(lo2, a, b) if g == 0 else jnp.where(lo2, b, a)


def _sink_row(sink_ref, g):
    return jnp.concatenate([sink_ref[HEADS_PER_GROUP * g + r:HEADS_PER_GROUP * g + r + 1, :]
                            for r in range(HEADS_PER_GROUP)], axis=1)


def _attn_probs_t(k_dup, q_rows, bias_t, sink_row):
    s_t = _dot_nt(k_dup, q_rows) * ATTN_SCALE + bias_t
    m = jnp.maximum(jnp.max(s_t, axis=0, keepdims=True), sink_row)
    p = jnp.exp(s_t - m)
    e_sink = jnp.exp(sink_row - m)
    inv = 1.0 / (jnp.sum(p, axis=0, keepdims=True) + e_sink)
    return p * inv, e_sink * inv


def _sgu_forward_pair(wm, vp, j):
    lo, hi = _lane_masks(vp.shape)
    lhs = jnp.concatenate([wm[2 * j], wm[2 * j + 1]], axis=1)
    rhs = jnp.concatenate([jnp.where(lo, vp, 0.0), jnp.where(hi, vp, 0.0)], axis=0)
    return _dot(lhs, rhs)


def _masked_spatial(w_ref):
    t = lax.broadcasted_iota(jnp.int32, (CHUNK, CHUNK), 0)
    s = lax.broadcasted_iota(jnp.int32, (CHUNK, CHUNK), 1)
    tril = s <= t
    return [jnp.where(tril, w_ref[g], 0.0) for g in range(GMLP_GROUPS)], tril, s >= t


def _mod_kernel(c_all, w_shard, b_shard):
    n = w_shard.shape[1]
    tn = 512

    def body(c_ref, w_ref, b_ref, mod_ref, act_ref):
        cv = c_ref[...]
        act = cv * (1.0 / (1.0 + jnp.exp(-cv)))
        act_ref[...] = act
        mod_ref[...] = _dot(act, w_ref[...]) + b_ref[...]

    return pl.pallas_call(
        body, name="ada_mod", grid=(n // tn,),
        out_shape=[jax.ShapeDtypeStruct((N_DEV, n), F32), jax.ShapeDtypeStruct((N_DEV, D_MODEL), F32)],
        in_specs=[_full((N_DEV, D_MODEL)), pl.BlockSpec((D_MODEL, tn), lambda i: (0, i)),
                  pl.BlockSpec((1, tn), lambda i: (0, i))],
        out_specs=[pl.BlockSpec((N_DEV, tn), lambda i: (0, i)), _full((N_DEV, D_MODEL))],
        compiler_params=_params("arbitrary"),
    )(c_all, w_shard, b_shard)


def _in_proj_kernel(x, vecs, w_in_t, comm=None):
    seq = x.shape[0]
    tm = 512

    def body(x_ref, v_ref, w_ref, proj_ref, h_ref):
        xv = x_ref[...]
        rstd = lax.rsqrt(_mean_last(xv * xv) + EPS)
        n1 = (xv * rstd) * v_ref[0:1, :]
        h = n1 * (1.0 + v_ref[2:3, :]) + v_ref[1:2, :]
        hb = h.astype(MXU_DTYPE)
        h_ref[...] = hb
        proj_ref[...] = _dot_nt(hb, w_ref[...])

    return _hosted_call(
        body, comm, name="in_proj", grid=(seq // tm,),
        out_shape=[jax.ShapeDtypeStruct((seq, IN_PROJ_WIDTH), F32),
                   jax.ShapeDtypeStruct((seq, D_MODEL), MXU_DTYPE)],
        in_specs=[pl.BlockSpec((tm, D_MODEL), lambda i: (i, 0)), _full((8, D_MODEL)),
                  _full((IN_PROJ_WIDTH, D_MODEL))],
        out_specs=[pl.BlockSpec((tm, IN_PROJ_WIDTH), lambda i: (i, 0)),
                   pl.BlockSpec((tm, D_MODEL), lambda i: (i, 0))],
        semantics=("arbitrary",),
    )(x, vecs, w_in_t)


def _mixer_fwd_kernel(proj, rope_tab, w_spatial, bias_full, sink_rows, comm=None):
    seq = proj.shape[0]
    nb = seq // CHUNK
    kv_col = (2 * GMLP_WIDTH + ATTN_WIDTH) // (2 * KV_WIDTH)

    def body(proj_ref, prev_ref, tab_ref, ptab_ref, w_ref, bias_ref, sink_ref, cat_ref):
        i = pl.program_id(0)
        wm, _, _ = _masked_spatial(w_ref)
        for j in range(GMLP_GROUPS // 2):
            cols = slice(LANES * j, LANES * (j + 1))
            vcols = slice(GMLP_WIDTH + LANES * j, GMLP_WIDTH + LANES * (j + 1))
            u, _ = _gelu_tanh(proj_ref[:, cols])
            vp, _ = _gelu_tanh(proj_ref[:, vcols])
            sv = _sgu_forward_pair(wm, vp, j) + bias_ref[:, cols]
            cat_ref[:, cols] = (u * sv).astype(cat_ref.dtype)
        o = 2 * GMLP_WIDTH
        tab = tab_ref[...]
        q_r = _rope_apply(proj_ref[:, o:o + ATTN_WIDTH], tab, 1.0)
        k_cur = _rope_apply(proj_ref[:, o + ATTN_WIDTH:o + ATTN_WIDTH + KV_WIDTH], tab, 1.0)
        k_prev = _rope_apply(prev_ref[:, 0:KV_WIDTH], ptab_ref[...], 1.0)
        k_a = jnp.concatenate([k_prev, k_cur], axis=0)
        v_a = jnp.concatenate([prev_ref[:, KV_WIDTH:2 * KV_WIDTH],
                               proj_ref[:, o + ATTN_WIDTH + KV_WIDTH:o + ATTN_WIDTH + 2 * KV_WIDTH]], axis=0)
        k_b = pltpu.roll(k_a, HEAD_DIM, 1)
        v_b = pltpu.roll(v_a, HEAD_DIM, 1)
        bias_t = _attn_bias_t(i == 0)
        lo, hi = _lane_masks((CHUNK, LANES))
        lo2, _ = _lane_masks((2 * CHUNK, LANES))
        for g in range(N_KV_HEADS):
            p_t, _ = _attn_probs_t(_group_dup(k_a, k_b, g, lo2), _group_rows(q_r, g, lo, hi), bias_t,
                                   _sink_row(sink_ref, g))
            o_t = _dot(_group_dup(v_a, v_b, g, lo2).T, p_t)
            for k, pair in enumerate(_pairs_from_rows(o_t.T, lo)):
                c0 = GMLP_WIDTH + LANES * (2 * g + k)
                cat_ref[:, c0:c0 + LANES] = pair.astype(cat_ref.dtype)

    return _hosted_call(
        body, comm, name="mixer_fwd", grid=(nb,),
        out_shape=[jax.ShapeDtypeStruct((seq, D_MODEL), MXU_DTYPE)],
        in_specs=[pl.BlockSpec((CHUNK, IN_PROJ_WIDTH), lambda i: (i, 0)),
                  pl.BlockSpec((CHUNK, 2 * KV_WIDTH), lambda i: (jnp.maximum(i - 1, 0), kv_col)),
                  pl.BlockSpec((CHUNK, 3 * LANES), lambda i: (i, 0)),
                  pl.BlockSpec((CHUNK, 3 * LANES), lambda i: (jnp.maximum(i - 1, 0), 0)),
                  _full((GMLP_GROUPS, CHUNK, CHUNK)), _full((CHUNK, GMLP_WIDTH)),
                  _full((N_Q_HEADS, LANES))],
        out_specs=[pl.BlockSpec((CHUNK, D_MODEL), lambda i: (i, 0))],
        semantics=("arbitrary",),
    )(proj, proj, rope_tab, rope_tab, w_spatial, bias_full, sink_rows)


def _trunk_kernel(x, target, cat, vecs, chip_idx, gathered, local):
    seq = x.shape[0]
    tm = 256
    nj = D_FF // D_MODEL
    out_rows = D_MODEL // N_CHIPS

    def body(chip_ref, x_ref, t_ref, cat_ref, v_ref, g_out, g_w1, g_w2, l_out, l_w1, l_w2,
             dx1_ref, dcat_ref, dmix_ref, h2_ref, r_ref, da_ref, dff_ref, sums_ref,
             wout, w1, w2, a_scr, sem):
        i = pl.program_id(0)

        @pl.when(i == 0)
        def _():
            dsts = ([wout.at[pl.ds(out_rows * k, out_rows)] for k in range(N_CHIPS)]
                    + [w1.at[k] for k in range(N_CHIPS)] + [w2.at[k] for k in range(N_CHIPS)])
            srcs = [(g, l) for g, l in ((g_out, l_out), (g_w1, l_w1), (g_w2, l_w2)) for _ in range(N_CHIPS)]
            for n, (dst, (g, l)) in enumerate(zip(dsts, srcs)):
                k = n % N_CHIPS

                @pl.when(chip_ref[0] == k)
                def _():
                    pltpu.make_async_copy(l, dst, sem.at[n]).start()

                @pl.when(chip_ref[0] != k)
                def _():
                    pltpu.make_async_copy(g.at[k], dst, sem.at[n]).start()
            for n, (dst, (g, l)) in enumerate(zip(dsts, srcs)):
                pltpu.make_async_copy(l, dst, sem.at[n]).wait()
            sums_ref[...] = jnp.zeros_like(sums_ref)

        gate1, shift2, scale2 = v_ref[0:1, :], v_ref[1:2, :], v_ref[2:3, :]
        gate2, g_ffn, g_final = v_ref[3:4, :], v_ref[4:5, :], v_ref[5:6, :]

        mix = _dot(cat_ref[...], wout[...])
        x1 = x_ref[...] + gate1 * mix
        rstd2 = lax.rsqrt(_mean_last(x1 * x1) + EPS)
        xh2 = x1 * rstd2
        n2 = xh2 * g_ffn
        h2b = (n2 * (1.0 + scale2) + shift2).astype(MXU_DTYPE)
        h2_ref[...] = h2b
        ff = jnp.zeros((tm, D_MODEL), F32)
        for j in range(nj):
            a = _dot(h2b, w1[j])
            a_scr[j] = a
            relu = jnp.maximum(a, 0.0)
            rb = (relu * relu).astype(MXU_DTYPE)
            r_ref[:, D_MODEL * j:D_MODEL * (j + 1)] = rb
            ff = ff + _dot(rb, w2[j])
        x2 = x1 + gate2 * ff
        rstd3 = lax.rsqrt(_mean_last(x2 * x2) + EPS)
        xh3 = x2 * rstd3
        err = xh3 * g_final - t_ref[...]
        loss = 0.5 * _rowsum(_mean_last(err * err))
        dy = err * (1.0 / D_MODEL)
        dxh3 = dy * g_final
        dx2 = rstd3 * (dxh3 - xh3 * _mean_last(dxh3 * xh3))
        dffb = (dx2 * gate2).astype(MXU_DTYPE)
        dff_ref[...] = dffb
        dh2 = jnp.zeros((tm, D_MODEL), F32)
        for j in range(nj):
            dr = _dot_nt(dffb, w2[j])
            dab = (dr * (2.0 * jnp.maximum(a_scr[j], 0.0))).astype(MXU_DTYPE)
            da_ref[:, D_MODEL * j:D_MODEL * (j + 1)] = dab
            dh2 = dh2 + _dot_nt(dab, w1[j])
        dn2 = dh2 * (1.0 + scale2)
        dxh2 = dn2 * g_ffn
        dx1 = dx2 + rstd2 * (dxh2 - xh2 * _mean_last(dxh2 * xh2))
        dx1_ref[...] = dx1
        dmixb = (dx1 * gate1).astype(MXU_DTYPE)
        dmix_ref[...] = dmixb
        dcat_ref[...] = _dot_nt(dmixb, wout[...])

        sums_ref[0:1, :] += _rowsum(dh2)
        sums_ref[1:2, :] += _rowsum(dh2 * n2)
        sums_ref[2:3, :] += _rowsum(dx2 * ff)
        sums_ref[3:4, :] += _rowsum(dn2 * xh2)
        sums_ref[4:5, :] += _rowsum(dy * xh3)
        sums_ref[5:6, :] += _rowsum(dx1 * mix)
        sums_ref[6:7, :] += jnp.broadcast_to(loss, (1, D_MODEL))

    tok = lambda w: pl.BlockSpec((tm, w), lambda i, chip: (i, 0))
    return _hosted_call(
        body, None, name="trunk", grid=(seq // tm,), n_prefetch=1,
        out_shape=[jax.ShapeDtypeStruct((seq, D_MODEL), F32), jax.ShapeDtypeStruct((seq, D_MODEL), F32),
                   jax.ShapeDtypeStruct((seq, D_MODEL), MXU_DTYPE), jax.ShapeDtypeStruct((seq, D_MODEL), MXU_DTYPE),
                   jax.ShapeDtypeStruct((seq, D_FF), MXU_DTYPE), jax.ShapeDtypeStruct((seq, D_FF), MXU_DTYPE),
                   jax.ShapeDtypeStruct((seq, D_MODEL), MXU_DTYPE), jax.ShapeDtypeStruct((8, D_MODEL), F32)],
        in_specs=[tok(D_MODEL), tok(D_MODEL), tok(D_MODEL), _full((8, D_MODEL))] + [_any()] * 6,
        out_specs=[tok(D_MODEL), tok(D_MODEL), tok(D_MODEL), tok(D_MODEL), tok(D_FF), tok(D_FF), tok(D_MODEL),
                   _full((8, D_MODEL))],
        scratch_shapes=[pltpu.VMEM((D_MODEL, D_MODEL), MXU_DTYPE), pltpu.VMEM((nj, D_MODEL, D_MODEL), MXU_DTYPE),
                        pltpu.VMEM((nj, D_MODEL, D_MODEL), MXU_DTYPE), pltpu.VMEM((nj, tm, D_MODEL), F32),
                        pltpu.SemaphoreType.DMA((3 * N_CHIPS,))],
        semantics=("arbitrary",),
    )(chip_idx, x, target, cat, vecs, *gathered, *local)


def _mixer_bwd_kernel(proj, rope_tab, dcat, w_spatial, w_spatial_t, bias_full, sink_rows, dev_idx, comm=None):
    seq = proj.shape[0]
    nb = seq // CHUNK
    kv_col = (2 * GMLP_WIDTH + ATTN_WIDTH) // (2 * KV_WIDTH)

    def body(dev_ref, proj_ref, prev_ref, tab_ref, ptab_ref, dcat_ref, w_ref, wt_ref, bias_ref, sink_ref,
             dproj_ref, dw_ref, db_ref, dsink_ref, carry):
        del dev_ref
        step = pl.program_id(0)
        blk = nb - 1 - step

        @pl.when(step == 0)
        def _():
            carry[...] = jnp.zeros_like(carry)
            dw_ref[...] = jnp.zeros_like(dw_ref)
            db_ref[...] = jnp.zeros_like(db_ref)
            dsink_ref[...] = jnp.zeros_like(dsink_ref)

        wm, tril, triu = _masked_spatial(w_ref)
        lo, hi = _lane_masks((CHUNK, LANES))
        lane = lax.broadcasted_iota(jnp.int32, (CHUNK, LANES), 1)
        db = jnp.zeros((CHUNK, LANES), F32)
        for j in range(GMLP_GROUPS // 2):
            cols = slice(LANES * j, LANES * (j + 1))
            vcols = slice(GMLP_WIDTH + LANES * j, GMLP_WIDTH + LANES * (j + 1))
            zu, zv = proj_ref[:, cols], proj_ref[:, vcols]
            u, tu = _gelu_tanh(zu)
            vp, tv = _gelu_tanh(zv)
            sv = _sgu_forward_pair(wm, vp, j) + bias_ref[:, cols]
            dout = dcat_ref[:, cols]
            du = dout * sv
            dsv = dout * u
            dsv_lo, dsv_hi = jnp.where(lo, dsv, 0.0), jnp.where(hi, dsv, 0.0)
            lhs_t = jnp.concatenate([jnp.where(triu, wt_ref[2 * j], 0.0),
                                     jnp.where(triu, wt_ref[2 * j + 1], 0.0)], axis=1)
            dv = _dot(lhs_t, jnp.concatenate([dsv_lo, dsv_hi], axis=0))
            dw_ref[2 * j] += jnp.where(tril, _dot_nt(dsv_lo, vp), 0.0)
            dw_ref[2 * j + 1] += jnp.where(tril, _dot_nt(dsv_hi, vp), 0.0)
            db = db + (jnp.where(lane == 2 * j, jnp.sum(dsv_lo, axis=1, keepdims=True), 0.0)
                       + jnp.where(lane == 2 * j + 1, jnp.sum(dsv_hi, axis=1, keepdims=True), 0.0))
            dproj_ref[:, cols] = (du * _gelu_tanh_grad(zu, tu)).astype(dproj_ref.dtype)
            dproj_ref[:, vcols] = (dv * _gelu_tanh_grad(zv, tv)).astype(dproj_ref.dtype)
        db_ref[...] += db
        o = 2 * GMLP_WIDTH
        tab = tab_ref[...]
        q_r = _rope_apply(proj_ref[:, o:o + ATTN_WIDTH], tab, 1.0)
        k_cur = _rope_apply(proj_ref[:, o + ATTN_WIDTH:o + ATTN_WIDTH + KV_WIDTH], tab, 1.0)
        k_prev = _rope_apply(prev_ref[:, 0:KV_WIDTH], ptab_ref[...], 1.0)
        k_a = jnp.concatenate([k_prev, k_cur], axis=0)
        v_a = jnp.concatenate([prev_ref[:, KV_WIDTH:2 * KV_WIDTH],
                               proj_ref[:, o + ATTN_WIDTH + KV_WIDTH:o + ATTN_WIDTH + 2 * KV_WIDTH]], axis=0)
        k_b = pltpu.roll(k_a, HEAD_DIM, 1)
        v_b = pltpu.roll(v_a, HEAD_DIM, 1)
        bias_t = _attn_bias_t(blk == 0)
        lo2, _ = _lane_masks((2 * CHUNK, LANES))
        dout_b = dcat_ref[:, GMLP_WIDTH:GMLP_WIDTH + ATTN_WIDTH]
        dk_tot, dv_tot, dq_pairs = [], [], []
        for g in range(N_KV_HEADS):
            k_dup, v_dup = _group_dup(k_a, k_b, g, lo2), _group_dup(v_a, v_b, g, lo2)
            q_rows = _group_rows(q_r, g, lo, hi)
            do_rows = _group_rows(dout_b, g, lo, hi)
            p_t, p_sink = _attn_probs_t(k_dup, q_rows, bias_t, _sink_row(sink_ref, g))
            dp_t = _dot_nt(v_dup, do_rows)
            delta = jnp.sum(p_t * dp_t, axis=0, keepdims=True)
            ds_t = p_t * (dp_t - delta) * ATTN_SCALE
            dsink = -p_sink * delta
            for r in range(HEADS_PER_GROUP):
                h = HEADS_PER_GROUP * g + r
                dsink_ref[h:h + 1, :] += jnp.broadcast_to(
                    jnp.sum(dsink[:, LANES * r:LANES * (r + 1)], axis=1, keepdims=True), (1, LANES))
            dk_full = _dot(ds_t, q_rows)
            dv_full = _dot(p_t, do_rows)
            dk_tot.append(dk_full + pltpu.roll(dk_full, HEAD_DIM, 1))
            dv_tot.append(dv_full + pltpu.roll(dv_full, HEAD_DIM, 1))
            dq_t = _dot(k_dup.T, ds_t)
            dq_pairs += _pairs_from_rows(dq_t.T, lo)
        dk_all = jnp.where(lo2, dk_tot[0], dk_tot[1])
        dv_all = jnp.where(lo2, dv_tot[0], dv_tot[1])
        dk_cur = dk_all[CHUNK:, :] + carry[:, 0:KV_WIDTH]
        dv_cur = dv_all[CHUNK:, :] + carry[:, KV_WIDTH:2 * KV_WIDTH]
        carry[:, 0:KV_WIDTH] = dk_all[:CHUNK, :]
        carry[:, KV_WIDTH:2 * KV_WIDTH] = dv_all[:CHUNK, :]
        dq = _rope_apply(jnp.concatenate(dq_pairs, axis=1), tab, -1.0)
        dproj_ref[:, o:o + ATTN_WIDTH] = dq.astype(dproj_ref.dtype)
        dproj_ref[:, o + ATTN_WIDTH:o + ATTN_WIDTH + KV_WIDTH] = (
            _rope_apply(dk_cur, tab, -1.0).astype(dproj_ref.dtype))
        dproj_ref[:, o + ATTN_WIDTH + KV_WIDTH:o + ATTN_WIDTH + 2 * KV_WIDTH] = dv_cur.astype(dproj_ref.dtype)

    rev = lambda i: nb - 1 - i
    slot = lambda shape: pl.BlockSpec((None,) + shape, lambda i, d: (d[0],) + (0,) * len(shape))
    return _hosted_call(
        body, comm, name="mixer_bwd", grid=(nb,), n_prefetch=1,
        out_shape=[jax.ShapeDtypeStruct((seq, IN_PROJ_WIDTH), MXU_DTYPE),
                   jax.ShapeDtypeStruct((N_DEV, GMLP_GROUPS, CHUNK, CHUNK), F32),
                   jax.ShapeDtypeStruct((N_DEV, CHUNK, LANES), F32),
                   jax.ShapeDtypeStruct((N_DEV, N_Q_HEADS, LANES), F32)],
        in_specs=[pl.BlockSpec((CHUNK, IN_PROJ_WIDTH), lambda i, d: (rev(i), 0)),
                  pl.BlockSpec((CHUNK, 2 * KV_WIDTH), lambda i, d: (jnp.maximum(rev(i) - 1, 0), kv_col)),
                  pl.BlockSpec((CHUNK, 3 * LANES), lambda i, d: (rev(i), 0)),
                  pl.BlockSpec((CHUNK, 3 * LANES), lambda i, d: (jnp.maximum(rev(i) - 1, 0), 0)),
                  pl.BlockSpec((CHUNK, D_MODEL), lambda i, d: (rev(i), 0)),
                  _full((GMLP_GROUPS, CHUNK, CHUNK)), _full((GMLP_GROUPS, CHUNK, CHUNK)),
                  _full((CHUNK, GMLP_WIDTH)), _full((N_Q_HEADS, LANES))],
        out_specs=[pl.BlockSpec((CHUNK, IN_PROJ_WIDTH), lambda i, d: (rev(i), 0)),
                   slot((GMLP_GROUPS, CHUNK, CHUNK)), slot((CHUNK, LANES)), slot((N_Q_HEADS, LANES))],
        scratch_shapes=[pltpu.VMEM((CHUNK, 2 * KV_WIDTH), F32)],
        semantics=("arbitrary",),
    )(dev_idx, proj, proj, rope_tab, rope_tab, dcat, w_spatial, w_spatial_t, bias_full, sink_rows)


def _in_proj_bwd_kernel(x, dx1, dproj, vecs, w_in_t, comm=None):
    seq = x.shape[0]
    tm = 512

    def body(x_ref, dx1_ref, dp_ref, v_ref, w_ref, gx_ref, sums_ref):
        @pl.when(pl.program_id(0) == 0)
        def _():
            sums_ref[...] = jnp.zeros_like(sums_ref)

        g_mix, scale1 = v_ref[0:1, :], v_ref[2:3, :]
        dh = _dot(dp_ref[...], w_ref[...])
        xv = x_ref[...]
        rstd = lax.rsqrt(_mean_last(xv * xv) + EPS)
        xh = xv * rstd
        dn1 = dh * (1.0 + scale1)
        dxh = dn1 * g_mix
        gx_ref[...] = dx1_ref[...] + rstd * (dxh - xh * _mean_last(dxh * xh))
        sums_ref[0:1, :] += _rowsum(dh)
        sums_ref[1:2, :] += _rowsum(dh * (xh * g_mix))
        sums_ref[2:3, :] += _rowsum(dn1 * xh)

    return _hosted_call(
        body, comm, name="in_proj_bwd", grid=(seq // tm,),
        out_shape=[jax.ShapeDtypeStruct((seq, D_MODEL), F32), jax.ShapeDtypeStruct((8, D_MODEL), F32)],
        in_specs=[pl.BlockSpec((tm, D_MODEL), lambda i: (i, 0)), pl.BlockSpec((tm, D_MODEL), lambda i: (i, 0)),
                  pl.BlockSpec((tm, IN_PROJ_WIDTH), lambda i: (i, 0)), _full((8, D_MODEL)),
                  _full((IN_PROJ_WIDTH, D_MODEL))],
        out_specs=[pl.BlockSpec((tm, D_MODEL), lambda i: (i, 0)), _full((8, D_MODEL))],
        semantics=("arbitrary",),
    )(x, dx1, dproj, vecs, w_in_t)


class _GradTiles(NamedTuple):
    tm: int
    tn: int
    n_tiles: int
    chips_per_tile: int
    a_index: Callable
    b_index: Callable


def _weight_grad_kernel(a, b, c_idx, name, tiles, comm=None):
    seq = a.shape[0]
    tk = min(seq, 4096)
    nk = seq // tk
    tm, tn, n_tiles, per = tiles.tm, tiles.tn, tiles.n_tiles, tiles.chips_per_tile
    rows = tm // per

    def half(phase, c):
        return phase * c[0] + (1 - phase) * (1 - c[0])

    def body(c_ref, a_ref, b_ref, o_ref, acc, stage, landed, send_sems, recv_sems):
        del c_ref
        phase, t, kk = pl.program_id(0), pl.program_id(1), pl.program_id(2)
        x, y, c, _ = _mesh_place()

        def copy(tile):
            return pltpu.make_async_remote_copy(
                src_ref=stage.at[tile], dst_ref=landed.at[tile], send_sem=send_sems.at[tile],
                recv_sem=recv_sems.at[tile], device_id=(x, y, 1 - c), device_id_type=MESH)

        @pl.when(kk == 0)
        def _():
            acc[...] = jnp.zeros_like(acc)

        acc[...] += _dot_tn(a_ref[...], b_ref[...])

        @pl.when((kk == nk - 1) & (phase == 0))
        def _():
            stage[t] = acc[...].astype(stage.dtype)
            copy(t).start()

        @pl.when((kk == nk - 1) & (phase == 1))
        def _():
            copy(t).wait_recv()
            total = acc[...] + landed[t].astype(F32)
            for q in range(per):
                o_ref[q] = total[rows * q:rows * (q + 1)].astype(o_ref.dtype)

        @pl.when((kk == nk - 1) & (phase == 1) & (t == n_tiles - 1))
        def _():
            for tile in range(n_tiles):
                copy(tile).wait_send()

    out = _hosted_call(
        body, comm, name=name, grid=(2, n_tiles, nk), n_prefetch=1,
        out_shape=[jax.ShapeDtypeStruct((n_tiles * per, rows, tn), GRAD_COMM_DTYPE)],
        in_specs=[pl.BlockSpec((tk, tm), lambda p, t, k, c: (k, tiles.a_index(t, half(p, c)))),
                  pl.BlockSpec((tk, tn), lambda p, t, k, c: (k, tiles.b_index(t, half(p, c))))],
        out_specs=[pl.BlockSpec((per, rows, tn), lambda p, t, k, c: (p * t, 0, 0))],
        scratch_shapes=[pltpu.VMEM((tm, tn), F32), pltpu.VMEM((n_tiles, tm, tn), GRAD_COMM_DTYPE),
                        pltpu.VMEM((n_tiles, tm, tn), GRAD_COMM_DTYPE),
                        pltpu.SemaphoreType.DMA((n_tiles,)), pltpu.SemaphoreType.DMA((n_tiles,))],
        semantics=("arbitrary", "arbitrary", "arbitrary"),
    )(c_idx, a, b)
    return out[0] if comm is None else out


def _row_tile(rows, most=256, sublanes=16):
    return max(t for t in range(sublanes, most + 1, sublanes) if rows % t == 0)


def _adam_update(w, g, m, v):
    m_new = ADAM_B1 * m + (1.0 - ADAM_B1) * g
    v_new = ADAM_B2 * v + (1.0 - ADAM_B2) * (g * g)
    m_hat = m_new / (1.0 - ADAM_B1 ** ADAM_STEP)
    v_hat = v_new / (1.0 - ADAM_B2 ** ADAM_STEP)
    delta = -ADAM_LR * (m_hat / (jnp.sqrt(v_hat) + ADAM_EPS) + ADAM_WD * w)
    return delta, m_new, v_new


def _sum_chips_kernel(own, others, place, name):
    _, r, n = own.shape
    tr = _row_tile(r)

    def body(place_ref, own_ref, oth_ref, o_ref):
        del place_ref
        acc = own_ref[...].astype(F32)
        for k in range(N_CHIPS - 1):
            acc = acc + oth_ref[k].astype(F32)
        o_ref[...] = acc

    return pl.pallas_call(
        body, name=name, out_shape=jax.ShapeDtypeStruct((2, r, n), F32),
        grid_spec=pltpu.PrefetchScalarGridSpec(
            num_scalar_prefetch=1, grid=(r // tr,),
            in_specs=[pl.BlockSpec((None, tr, n), lambda i, p: (p[0], i, 0)),
                      pl.BlockSpec((N_CHIPS - 1, tr, n), lambda i, p: (0, i, 0))],
            out_specs=pl.BlockSpec((None, tr, n), lambda i, p: (p[1], i, 0))),
        compiler_params=_params("parallel"),
    )(place, own, others)


def _adam_kernel(w, g, m, v, name):
    r, n = w.shape
    by_columns = g.shape[1] == r
    tr, tn = _row_tile(g.shape[1], most=512), g.shape[2]

    def body(w_ref, g_ref, m_ref, v_ref, g_out, d_ref, mo_ref, vo_ref):
        gv = g_ref[...]
        g_out[...] = gv
        d_ref[...], mo_ref[...], vo_ref[...] = _adam_update(w_ref[...], gv, m_ref[...], v_ref[...])

    steps = g.shape[1] // tr
    spec = pl.BlockSpec((tr, tn), (lambda h, i: (i, h)) if by_columns else (lambda h, i: (h * steps + i, 0)))
    return pl.pallas_call(
        body, name=name, grid=(2, steps), out_shape=[jax.ShapeDtypeStruct((r, n), F32)] * 4,
        in_specs=[spec, pl.BlockSpec((None, tr, tn), lambda h, i: (h, i, 0)), spec, spec], out_specs=[spec] * 4,
        compiler_params=_params("parallel", "parallel"),
    )(w, g, m, v)


SMALL_PARAMS = ("b_ada", "g_mix", "g_ffn", "g_final", "b_spatial", "sinks", "w_spatial")


def _small_update_kernel(gathered, params):
    shapes = [params[nm][0].shape for nm in SMALL_PARAMS]

    def body(*refs):
        g_refs, refs = refs[:5], refs[5:]
        p_refs, refs = refs[:3 * len(SMALL_PARAMS)], refs[3 * len(SMALL_PARAMS):]
        loss_ref, o_refs = refs[0], refs[1:]

        def total(ref):
            acc = ref[0]
            for k in range(1, N_DEV):
                acc = acc + ref[k]
            return acc

        s1, s2, db, ds, dw = (total(r) for r in g_refs)
        loss_ref[...] = jnp.broadcast_to(s2[6:7, 0:1], loss_ref.shape)
        grads = {"b_ada": [s1[0:1], s1[1:2], s2[5:6], s2[0:1], s2[1:2], s2[2:3]], "g_mix": [s1[2:3]],
                 "g_ffn": [s2[3:4]], "g_final": [s2[4:5]], "b_spatial": [db.T[0:GMLP_GROUPS]],
                 "w_spatial": [dw]}
        lane = lax.broadcasted_iota(jnp.int32, (1, LANES), 1)
        sink_row = jnp.zeros((1, LANES), F32)
        for h in range(N_Q_HEADS):
            sink_row = sink_row + jnp.where(lane == h, ds[h:h + 1, :], 0.0)
        grads["sinks"] = [sink_row[:, 0:N_Q_HEADS]]
        for i, nm in enumerate(SMALL_PARAMS):
            w_ref, m_ref, v_ref = p_refs[3 * i:3 * i + 3]
            outs = o_refs[4 * i:4 * i + 4]
            width = grads[nm][0].shape[1]
            for k, g in enumerate(grads[nm]):
                cols = slice(width * k, width * (k + 1))
                upd = _adam_update(w_ref[:, cols], g, m_ref[:, cols], v_ref[:, cols])
                for o_ref, val in zip(outs, (g,) + upd):
                    o_ref[:, cols] = val

    flat = [a for nm in SMALL_PARAMS for a in params[nm]]
    out_shape = [jax.ShapeDtypeStruct((8, LANES), F32)]
    out_shape += [jax.ShapeDtypeStruct(s, F32) for s in shapes for _ in range(4)]
    outs = pl.pallas_call(
        body, name="small_update", grid=(1,), out_shape=out_shape,
        in_specs=[_full(g.shape) for g in gathered] + [_full(a.shape) for a in flat],
        out_specs=[_full(s.shape) for s in out_shape],
        compiler_params=_params("arbitrary"),
    )(*gathered, *flat)
    return {nm: outs[1 + 4 * i:5 + 4 * i] for i, nm in enumerate(SMALL_PARAMS)}, outs[0]


def _ada_update_kernel(act_t, dmod, w, m, v):
    r, n = w.shape
    tr = 256

    def body(a_ref, d_ref, w_ref, m_ref, v_ref, g_ref, dl_ref, mo_ref, vo_ref):
        g = _dot(a_ref[...], d_ref[...])
        g_ref[...] = g
        dl_ref[...], mo_ref[...], vo_ref[...] = _adam_update(w_ref[...], g, m_ref[...], v_ref[...])

    spec = pl.BlockSpec((tr, n), lambda i: (i, 0))
    return pl.pallas_call(
        body, name="ada_update", grid=(r // tr,), out_shape=[jax.ShapeDtypeStruct((r, n), F32)] * 4,
        in_specs=[pl.BlockSpec((tr, N_DEV), lambda i: (i, 0)), _full((N_DEV, n)), spec, spec, spec],
        out_specs=[spec] * 4, compiler_params=_params("parallel"),
    )(act_t, dmod, w, m, v)


def kernel(x, c, positions, w_ada, b_ada, g_mix, w_in, w_spatial, b_spatial, sinks, w_out, g_ffn, w_ff1, w_ff2, g_final, loss_target, m_w_ada, m_b_ada, m_g_mix, m_w_in, m_w_spatial, m_b_spatial, m_sinks, m_w_out, m_g_ffn, m_w_ff1, m_w_ff2, m_g_final, v_w_ada, v_b_ada, v_g_mix, v_w_in, v_w_spatial, v_b_spatial, v_sinks, v_w_out, v_g_ffn, v_w_ff1, v_w_ff2, v_g_final):
    xi, yi, ci = lax.axis_index("x"), lax.axis_index("y"), lax.axis_index("c")
    chip = 2 * xi + yi
    dev = 2 * chip + ci
    seq = x.shape[1]
    x2, tgt = x[0], loss_target[0]
    ada_cols = w_ada.shape[2]

    big = {"w_in": tuple(a[0].T for a in (w_in, m_w_in, v_w_in)),
           "w_out": (w_out[0], m_w_out[0], v_w_out[0]), "w_ff1": (w_ff1[0], m_w_ff1[0], v_w_ff1[0]),
           "w_ff2": (w_ff2[0], m_w_ff2[0], v_w_ff2[0])}

    def halves(nm):
        r, n = big[nm][0].shape
        return big[nm][0].astype(WEIGHT_COMM_DTYPE).reshape(2, r // 2, n)

    c_all, w_in_t = _all_gather8([c, halves("w_in")], "gather_c_w_in", split=[False, True])
    c_all, w_in_t = c_all.reshape(N_DEV, D_MODEL), w_in_t.reshape(IN_PROJ_WIDTH, D_MODEL)
    b_shard = lax.dynamic_slice(b_ada, (0, chip * ada_cols), (1, ada_cols))
    mod_part, act = _mod_kernel(c_all, w_ada[0], b_shard)
    mod_all, = _all_gather8([mod_part], "gather_mod")
    mod_me = lax.dynamic_index_in_dim(mod_all[0::2], dev, axis=1, keepdims=False)
    mod_me = mod_me.reshape(N_MOD, D_MODEL)
    shift1, scale1, gate1, shift2, scale2, gate2 = (mod_me[k:k + 1] for k in range(N_MOD))

    zeros_row = jnp.zeros((1, D_MODEL), F32)
    vecs1 = jnp.concatenate([g_mix, shift1, scale1] + [zeros_row] * 5, axis=0)
    vecs2 = jnp.concatenate([gate1, shift2, scale2, gate2, g_ffn, g_final.reshape(1, D_MODEL)]
                            + [zeros_row] * 2, axis=0)
    bias_full = jnp.repeat(b_spatial[0].T, HEAD_DIM, axis=1)
    sink_rows = jnp.broadcast_to(sinks[0][:, None], (N_Q_HEADS, LANES))
    inv_freq = ROPE_THETA ** (-jnp.arange(0, ROT_DIM, 2, dtype=F32) / ROT_DIM)
    rope_tab = _rope_lane_tables(*_rope_angle_kernel(positions, inv_freq.reshape(ROT_DIM // 2, 1)))

    trunk_weights = ["w_out", "w_ff1", "w_ff2"]
    shards = [halves(nm) for nm in trunk_weights]
    proj, hb, *staged = _in_proj_kernel(x2, vecs1, w_in_t, comm=_gather2d_first(shards))
    cat, *staged = _mixer_fwd_kernel(proj, rope_tab, w_spatial[0], bias_full, sink_rows,
                                     comm=_gather2d_second(staged, shards))
    staged = _gather_forward(staged, "gather_forward")
    dx1, dcat, dmix, h2b, rb, dab, dffb, sums2 = _trunk_kernel(
        x2, tgt, cat, vecs2, chip.reshape(1).astype(jnp.int32),
        [g.reshape((N_CHIPS,) + big[nm][0].shape) for nm, g in zip(trunk_weights, staged)],
        [s.reshape(big[nm][0].shape) for nm, s in zip(trunk_weights, shards)])

    c_idx = ci.reshape(1).astype(jnp.int32)
    place = jnp.stack([chip, ci]).astype(jnp.int32)
    half_d = D_MODEL // 2
    cs_ff2 = _weight_grad_kernel(rb, dffb, c_idx, "dw_ff2",
                                 _GradTiles(D_MODEL, half_d, N_CHIPS, 1, lambda t, h: t, lambda t, h: h))
    cs_ff1, sc_ff2 = _weight_grad_kernel(
        h2b, dab, c_idx, "dw_ff1",
        _GradTiles(D_MODEL, half_d, N_CHIPS, 1, lambda t, h: 0, lambda t, h: 2 * t + h),
        comm=_scatter_job([cs_ff2]))
    cs_out = _weight_grad_kernel(cat, dmix, c_idx, "dw_out",
                                 _GradTiles(D_MODEL, half_d, 1, N_CHIPS, lambda t, h: 0, lambda t, h: h))
    dproj, dw_spatial, db_lanes, dsink_rows, sc_ff1, sc_out = _mixer_bwd_kernel(
        proj, rope_tab, dcat, w_spatial[0], w_spatial[0].transpose(0, 2, 1), bias_full, sink_rows,
        dev.reshape(1).astype(jnp.int32), comm=_scatter_job([cs_ff1, cs_out]))
    cs_in, *small_stage1 = _weight_grad_kernel(
        dproj, hb, c_idx, "dw_in",
        _GradTiles(2 * W_IN_BLOCK, half_d, N_CHIPS // 2, 2, lambda t, h: t, lambda t, h: h),
        comm=_gather_job([db_lanes, dsink_rows, dw_spatial.reshape(N_DEV, GMLP_GROUPS * CHUNK, CHUNK)]))
    grad_x, sums1 = _in_proj_bwd_kernel(x2, dx1, dproj, vecs1, w_in_t)
    *gathered, sc_in = _all_gather8([sums1, sums2], "gather_small", forward=small_stage1,
                                    riders=[_scatter_job([cs_in])])

    names = ["w_in", "w_out", "w_ff1", "w_ff2"]
    totals = [_sum_chips_kernel(own, oth, place, "grad_sum_" + nm)
              for nm, own, oth in zip(names, [cs_in, cs_out, cs_ff1, cs_ff2], [sc_in, sc_out, sc_ff1, sc_ff2])]
    shared = _sibling_share(totals, "grad_share")
    big_out = {}
    for nm, g in zip(names, shared):
        w, m, v = big[nm]
        outs = _adam_kernel(w, g, m, v, "adam_" + nm)
        big_out[nm] = tuple((t.T if nm == "w_in" else t)[None] for t in outs)

    small = {"b_ada": (b_ada, m_b_ada, v_b_ada), "g_mix": (g_mix, m_g_mix, v_g_mix),
             "g_ffn": (g_ffn, m_g_ffn, v_g_ffn), "g_final": (g_final, m_g_final, v_g_final),
             "b_spatial": (b_spatial, m_b_spatial, v_b_spatial), "sinks": (sinks, m_sinks, v_sinks),
             "w_spatial": (w_spatial, m_w_spatial, v_w_spatial)}
    flat_shape = {"g_final": (1, D_MODEL), "b_spatial": (GMLP_GROUPS, CHUNK), "w_spatial": (GMLP_GROUPS * CHUNK, CHUNK)}
    small_out, loss_tile = _small_update_kernel(
        gathered, {nm: tuple(a.reshape(flat_shape.get(nm, a.shape)) for a in small[nm]) for nm in small})
    small_out = {nm: [o.reshape(small[nm][0].shape) for o in small_out[nm]] for nm in small}
    loss = loss_tile[0, 0]

    g1, g2 = gathered[0], gathered[1]
    dmod_all = jnp.concatenate([g1[:, 0], g1[:, 1], g2[:, 5], g2[:, 0], g2[:, 1], g2[:, 2]], axis=1)
    dmod_cols = lax.dynamic_slice(dmod_all, (0, chip * ada_cols), (N_DEV, ada_cols))
    ada = _ada_update_kernel(act.T, dmod_cols, w_ada[0], m_w_ada[0], v_w_ada[0])
    big_out["w_ada"] = tuple(t[None] for t in ada)

    order = ["w_ada", "b_ada", "g_mix", "w_in", "w_spatial", "b_spatial", "sinks", "w_out", "g_ffn",
             "w_ff1", "w_ff2", "g_final"]

    def leaf(nm, k):
        return big_out[nm][k] if nm in big_out else small_out[nm][k]

    outs = [loss, grad_x[None]]
    for k in range(4):
        outs += [leaf(nm, k) for nm in order]
    return tuple(outs)
```

```python
import math
from typing import Callable, NamedTuple

import jax
import jax.numpy as jnp
from jax import lax
from jax.experimental import pallas as pl
from jax.experimental.pallas import tpu as pltpu

F32 = jnp.float32
MXU_DTYPE = jnp.bfloat16
WEIGHT_COMM_DTYPE = jnp.bfloat16
GRAD_COMM_DTYPE = jnp.bfloat16

D_MODEL = 1024
D_FF = 4096
HEAD_DIM = 64
GMLP_GROUPS = 8
GMLP_WIDTH = 512
CHUNK = 128
N_Q_HEADS = 8
N_KV_HEADS = 2
ATTN_WIDTH = 512
KV_WIDTH = 128
ROT_DIM = 16
ROPE_THETA = 500000.0
IN_PROJ_WIDTH = 1792
N_MOD = 6
EPS = 1e-5
N_CHIPS = 4
N_DEV = 8
LANES = 128
W_IN_BLOCK = IN_PROJ_WIDTH // N_CHIPS

ADAM_LR = 0.001
ADAM_B1 = 0.9
ADAM_B2 = 0.999
ADAM_EPS = 1e-08
ADAM_WD = 0.01
ADAM_STEP = 10

VMEM_LIMIT_BYTES = 58 * 1024 * 1024
MESH = pl.DeviceIdType.MESH


def _params(*semantics):
    return pltpu.CompilerParams(dimension_semantics=semantics, vmem_limit_bytes=VMEM_LIMIT_BYTES)


def _dot(a, b):
    return jnp.dot(a.astype(MXU_DTYPE), b.astype(MXU_DTYPE), preferred_element_type=F32)


def _dot_nt(a, b):
    return lax.dot_general(a.astype(MXU_DTYPE), b.astype(MXU_DTYPE), (((1,), (1,)), ((), ())),
                           preferred_element_type=F32)


def _dot_tn(a, b):
    return lax.dot_general(a.astype(MXU_DTYPE), b.astype(MXU_DTYPE), (((0,), (0,)), ((), ())),
                           preferred_element_type=F32)


def _full(shape):
    return pl.BlockSpec(shape, lambda *_: (0,) * len(shape))


def _any():
    return pl.BlockSpec(memory_space=pl.ANY)


def _rowsum(v):
    return jnp.sum(v, axis=0, keepdims=True)


def _mean_last(v):
    return jnp.mean(v, axis=-1, keepdims=True)


class _Comm(NamedTuple):
    operands: tuple
    out_shapes: tuple
    n_sems: int
    make: Callable
    in_place: int = 0


def _hosted_call(body, comm, *, name, grid, in_specs, out_shape, out_specs, scratch_shapes=(), semantics,
                 n_prefetch=0):
    if comm is None:
        return pl.pallas_call(
            body, name=name, out_shape=out_shape, compiler_params=_params(*semantics),
            grid_spec=pltpu.PrefetchScalarGridSpec(
                num_scalar_prefetch=n_prefetch, grid=grid, in_specs=in_specs, out_specs=out_specs,
                scratch_shapes=list(scratch_shapes)))
    n_in, n_out, n_scr = len(in_specs), len(out_shape), len(scratch_shapes)
    k_in, k_out = len(comm.operands), len(comm.out_shapes)

    def hosted(*refs):
        prefetched, refs = refs[:n_prefetch], refs[n_prefetch:]
        ins, refs = refs[:n_in], refs[n_in:]
        c_ins, refs = refs[:k_in], refs[k_in:]
        outs, refs = refs[:n_out], refs[n_out:]
        c_outs, refs = refs[:k_out], refs[k_out:]
        scratch, (send_sems, recv_sems) = refs[:n_scr], refs[n_scr:]
        first, last = None, None
        for d, size in enumerate(grid):
            at_start, at_end = pl.program_id(d) == 0, pl.program_id(d) == size - 1
            first = at_start if first is None else first & at_start
            last = at_end if last is None else last & at_end

        @pl.when(first)
        def _():
            for cp in comm.make(c_ins, c_outs, send_sems, recv_sems)[0]:
                cp.start()

        body(*prefetched, *ins, *outs, *scratch)

        @pl.when(last)
        def _():
            for wait in comm.make(c_ins, c_outs, send_sems, recv_sems)[1]:
                wait()

    aliases = {n_prefetch + n_in + i: n_out + i for i in range(comm.in_place)}
    call = pl.pallas_call(
        hosted, name=name, out_shape=list(out_shape) + list(comm.out_shapes),
        compiler_params=_params(*semantics), input_output_aliases=aliases,
        grid_spec=pltpu.PrefetchScalarGridSpec(
            num_scalar_prefetch=n_prefetch, grid=grid, in_specs=list(in_specs) + [_any()] * k_in,
            out_specs=list(out_specs) + [_any()] * k_out,
            scratch_shapes=list(scratch_shapes) + [pltpu.SemaphoreType.DMA((comm.n_sems,)),
                                                    pltpu.SemaphoreType.DMA((comm.n_sems,))]))
    return lambda *args: call(*args, *comm.operands)


def _mesh_place():
    x, y, c = lax.axis_index("x"), lax.axis_index("y"), lax.axis_index("c")
    return x, y, c, [(1 - x, y), (x, 1 - y), (1 - x, 1 - y)]


def _gather_job(bufs):
    per = 4

    def make(ins, outs, send_sems, recv_sems):
        del ins
        x, y, c, chips = _mesh_place()
        starts, waits = [], []
        for a, out in enumerate(outs):
            mine = src = out.at[4 * x + 2 * y + c]
            to = [(x, y, 1 - c)] + [(px, py, c) for px, py in chips]
            sends = [pltpu.make_async_remote_copy(
                src_ref=src, dst_ref=mine, send_sem=send_sems.at[per * a + k],
                recv_sem=recv_sems.at[per * a + k], device_id=dev, device_id_type=MESH)
                for k, dev in enumerate(to)]
            recvs = [pltpu.make_async_remote_copy(
                src_ref=src, dst_ref=out.at[4 * px + 2 * py + pc], send_sem=send_sems.at[per * a + k],
                recv_sem=recv_sems.at[per * a + k], device_id=(px, py, pc), device_id_type=MESH)
                for k, (px, py, pc) in enumerate(to)]
            starts += sends
            waits += [s.wait_send for s in sends] + [r.wait_recv for r in recvs]
        return starts, waits

    shapes = tuple(jax.ShapeDtypeStruct(b.shape, b.dtype) for b in bufs)
    return _Comm(tuple(bufs), shapes, per * len(bufs), make, in_place=len(bufs))


def _slots(x, y, c):
    return 4 * x + 2 * y + c, 4 * (1 - x) + 2 * y + c, 4 * x + 2 * (1 - y) + c, 4 * (1 - x) + 2 * (1 - y) + c


def _gather2d_first(halves):
    per = 2

    def make(ins, outs, send_sems, recv_sems):
        x, y, c, _ = _mesh_place()
        me, xn, yn, _ = _slots(x, y, c)
        starts, waits = [], []
        for a, (src, out) in enumerate(zip(ins, outs)):
            blk = src.at[c]
            rows = blk.shape[0] // 2
            upper, lower = pl.ds(0, rows), pl.ds(rows, rows)

            def copy(k, src_ref, dst_ref, dev, a=a):
                return pltpu.make_async_remote_copy(
                    src_ref=src_ref, dst_ref=dst_ref, send_sem=send_sems.at[per * a + k],
                    recv_sem=recv_sems.at[per * a + k], device_id=dev, device_id_type=MESH)

            sends = [copy(0, blk.at[upper], out.at[me, upper], (1 - x, y, c)),
                     copy(1, blk.at[lower], out.at[me, lower], (x, 1 - y, c))]
            recvs = [copy(0, blk.at[upper], out.at[xn, upper], (1 - x, y, c)),
                     copy(1, blk.at[lower], out.at[yn, lower], (x, 1 - y, c))]
            starts += sends
            waits += [s.wait_send for s in sends] + [r.wait_recv for r in recvs]
        return starts, waits

    shapes = tuple(jax.ShapeDtypeStruct((N_DEV,) + h.shape[1:], h.dtype) for h in halves)
    return _Comm(tuple(halves), shapes, per * len(halves), make)


def _gather2d_second(bufs, halves):
    per = 4
    n_arr = len(bufs)

    def make(ins, outs, send_sems, recv_sems):
        x, y, c, _ = _mesh_place()
        me, xn, yn, dg = _slots(x, y, c)
        starts, waits = [], []
        for a, buf in enumerate(outs):
            own = ins[n_arr + a].at[c]
            rows = buf.shape[1] // 2
            upper, lower = pl.ds(0, rows), pl.ds(rows, rows)
            plan = [(own.at[upper], me, upper, (x, 1 - y, c), yn), (buf.at[xn, upper], xn, upper, (x, 1 - y, c), dg),
                    (own.at[lower], me, lower, (1 - x, y, c), xn), (buf.at[yn, lower], yn, lower, (1 - x, y, c), dg)]
            for k, (src, slot, part, dev, landing) in enumerate(plan):
                sems = dict(send_sem=send_sems.at[per * a + k], recv_sem=recv_sems.at[per * a + k],
                            device_id=dev, device_id_type=MESH)
                send = pltpu.make_async_remote_copy(src_ref=src, dst_ref=buf.at[slot, part], **sems)
                arrival = pltpu.make_async_remote_copy(src_ref=src, dst_ref=buf.at[landing, part], **sems)
                starts.append(send)
                waits += [send.wait_send, arrival.wait_recv]
        return starts, waits

    shapes = tuple(jax.ShapeDtypeStruct(b.shape, b.dtype) for b in bufs)
    return _Comm(tuple(bufs) + tuple(halves), shapes, per * n_arr, make, in_place=n_arr)


def _gather_forward(bufs, name):
    n_arr = len(bufs)

    def body(*refs):
        outs = refs[n_arr:2 * n_arr]
        send_sems, recv_sems = refs[2 * n_arr:]
        x, y, c, chips = _mesh_place()
        sends, recvs = [], []
        for a, buf in enumerate(outs):
            for j, (px, py) in enumerate(chips):
                mine, theirs = buf.at[4 * px + 2 * py + c], buf.at[4 * px + 2 * py + 1 - c]
                sems = dict(send_sem=send_sems.at[3 * a + j], recv_sem=recv_sems.at[3 * a + j],
                            device_id=(x, y, 1 - c), device_id_type=MESH)
                sends.append(pltpu.make_async_remote_copy(src_ref=mine, dst_ref=mine, **sems))
                recvs.append(pltpu.make_async_remote_copy(src_ref=mine, dst_ref=theirs, **sems))
        for cp in sends:
            cp.start()
        for s, r in zip(sends, recvs):
            s.wait_send()
            r.wait_recv()

    return pl.pallas_call(
        body, name=name, out_shape=[jax.ShapeDtypeStruct(b.shape, b.dtype) for b in bufs],
        in_specs=[_any()] * n_arr, out_specs=[_any()] * n_arr,
        input_output_aliases={a: a for a in range(n_arr)},
        scratch_shapes=[pltpu.SemaphoreType.DMA((3 * n_arr,)), pltpu.SemaphoreType.DMA((3 * n_arr,))],
    )(*bufs)


def _scatter_job(chip_sums):
    def make(ins, outs, send_sems, recv_sems):
        x, y, c, chips = _mesh_place()
        copies = [pltpu.make_async_remote_copy(
            src_ref=src.at[2 * px + py], dst_ref=out.at[j], send_sem=send_sems.at[3 * a + j],
            recv_sem=recv_sems.at[3 * a + j], device_id=(px, py, c), device_id_type=MESH)
            for a, (src, out) in enumerate(zip(ins, outs)) for j, (px, py) in enumerate(chips)]
        return copies, [cp.wait for cp in copies]

    return _Comm(tuple(chip_sums), tuple(jax.ShapeDtypeStruct((3,) + s.shape[1:], s.dtype) for s in chip_sums),
                 3 * len(chip_sums), make)


def _all_gather8(blocks, name, split=False, forward=(), riders=()):
    n_arr, n_fwd = len(blocks), len(forward)
    splits = list(split) if isinstance(split, (list, tuple)) else [split] * n_arr
    rider_in = sum(len(r.operands) for r in riders)
    rider_out = sum(len(r.out_shapes) for r in riders)

    def body(*refs):
        x_refs, refs = refs[:n_arr], refs[n_arr + n_fwd:]
        r_ins, refs = refs[:rider_in], refs[rider_in:]
        out_refs, refs = refs[:n_arr], refs[n_arr:]
        fwd_refs, refs = refs[:n_fwd], refs[n_fwd:]
        r_outs, refs = refs[:rider_out], refs[rider_out:]
        (send_sems, recv_sems, local_sems), rider_sems = refs[:3], refs[3:]
        x, y, c, chips = _mesh_place()
        me, sibling = (x, y, c), (x, y, 1 - c)
        rider_waits, i0, o0 = [], 0, 0
        for n, job in enumerate(riders):
            k_in, k_out = len(job.operands), len(job.out_shapes)
            starts, waits = job.make(r_ins[i0:i0 + k_in], r_outs[o0:o0 + k_out],
                                     rider_sems[2 * n], rider_sems[2 * n + 1])
            for cp in starts:
                cp.start()
            rider_waits += waits
            i0, o0 = i0 + k_in, o0 + k_out
        passing = []
        for f, buf in enumerate(fwd_refs):
            for j, (px, py) in enumerate(chips):
                mine, theirs = buf.at[4 * px + 2 * py + c], buf.at[4 * px + 2 * py + 1 - c]
                sems = dict(send_sem=send_sems.at[7 * n_arr + 3 * f + j], recv_sem=recv_sems.at[7 * n_arr + 3 * f + j],
                            device_id=sibling, device_id_type=MESH)
                passing.append((pltpu.make_async_remote_copy(src_ref=mine, dst_ref=mine, **sems),
                                pltpu.make_async_remote_copy(src_ref=mine, dst_ref=theirs, **sems)))
        for send, _ in passing:
            send.start()
        arrays = []
        for a, (x_ref, out_ref) in enumerate(zip(x_refs, out_refs)):
            src_mine = x_ref.at[c] if splits[a] else x_ref

            def copy(k, blk, to, src=None, a=a, out_ref=out_ref):
                dst = out_ref.at[4 * blk[0] + 2 * blk[1] + blk[2]]
                return pltpu.make_async_remote_copy(
                    src_ref=dst if src is None else src, dst_ref=dst,
                    send_sem=send_sems.at[7 * a + k], recv_sem=recv_sems.at[7 * a + k],
                    device_id=to, device_id_type=MESH)

            mine = pltpu.make_async_copy(src_mine, out_ref.at[4 * x + 2 * y + c], local_sems.at[a])
            mine.start()
            first = [copy(0, me, sibling, src=src_mine)]
            first += [copy(1 + j, me, (*chip, c), src=src_mine) for j, chip in enumerate(chips)]
            for cp in first:
                cp.start()
            arrays.append((copy, mine, first))
        sent = []
        for copy, mine, first in arrays:
            passed = [copy(4 + j, (*chip, c), sibling) for j, chip in enumerate(chips)]
            for j, chip in enumerate(chips):
                copy(1 + j, (*chip, c), me).wait_recv()
                passed[j].start()
            sent += first + passed
        for copy, mine, first in arrays:
            copy(0, sibling, me).wait_recv()
            for j, chip in enumerate(chips):
                copy(4 + j, (*chip, 1 - c), me).wait_recv()
            mine.wait()
        for cp in sent:
            cp.wait_send()
        for send, arrival in passing:
            send.wait_send()
            arrival.wait_recv()
        for wait in rider_waits:
            wait()

    n_sems = 7 * n_arr + 3 * n_fwd
    rider_operands = [a for r in riders for a in r.operands]
    rider_shapes = [s for r in riders for s in r.out_shapes]
    return pl.pallas_call(
        body, name=name,
        out_shape=[jax.ShapeDtypeStruct((N_DEV,) + tuple(b.shape[1:] if s else b.shape), b.dtype)
                   for b, s in zip(blocks, splits)]
        + [jax.ShapeDtypeStruct(f.shape, f.dtype) for f in forward] + rider_shapes,
        in_specs=[_any()] * (n_arr + n_fwd + rider_in), out_specs=[_any()] * (n_arr + n_fwd + rider_out),
        input_output_aliases={n_arr + f: n_arr + f for f in range(n_fwd)},
        scratch_shapes=[pltpu.SemaphoreType.DMA((n_sems,)), pltpu.SemaphoreType.DMA((n_sems,)),
                        pltpu.SemaphoreType.DMA((n_arr,))]
        + [pltpu.SemaphoreType.DMA((r.n_sems,)) for r in riders for _ in range(2)],
    )(*blocks, *forward, *rider_operands)


def _sibling_share(bufs, name):
    n_arr = len(bufs)

    def body(*refs):
        out_refs = refs[n_arr:2 * n_arr]
        send_sems, recv_sems = refs[2 * n_arr:]
        x, y, c = lax.axis_index("x"), lax.axis_index("y"), lax.axis_index("c")
        copies = [pltpu.make_async_remote_copy(
            src_ref=out_refs[a].at[c], dst_ref=out_refs[a].at[c],
            send_sem=send_sems.at[a], recv_sem=recv_sems.at[a],
            device_id=(x, y, 1 - c), device_id_type=MESH) for a in range(n_arr)]
        for cp in copies:
            cp.start()
        for a in range(n_arr):
            pltpu.make_async_remote_copy(
                src_ref=out_refs[a].at[c], dst_ref=out_refs[a].at[1 - c],
                send_sem=send_sems.at[a], recv_sem=recv_sems.at[a],
                device_id=(x, y, 1 - c), device_id_type=MESH).wait()

    return pl.pallas_call(
        body, name=name,
        out_shape=[jax.ShapeDtypeStruct(b.shape, b.dtype) for b in bufs],
        in_specs=[_any()] * n_arr, out_specs=[_any()] * n_arr,
        input_output_aliases={a: a for a in range(n_arr)},
        scratch_shapes=[pltpu.SemaphoreType.DMA((n_arr,)), pltpu.SemaphoreType.DMA((n_arr,))],
    )(*bufs)


def _gelu_tanh(z):
    k = math.sqrt(2.0 / math.pi)
    t = jnp.tanh(k * (z + 0.044715 * (z * z * z)))
    return 0.5 * z * (1.0 + t), t


def _gelu_tanh_grad(z, t):
    k = math.sqrt(2.0 / math.pi)
    return 0.5 * (1.0 + t) + 0.5 * z * (1.0 - t * t) * (k * (1.0 + 3.0 * 0.044715 * (z * z)))


def _rope_angle_kernel(pos_row, invf_col):
    seq = pos_row.shape[1]

    def body(p_ref, f_ref, cos_ref, sin_ref):
        ang = p_ref[...].astype(F32) * f_ref[...]
        cos_ref[...] = jnp.cos(ang)
        sin_ref[...] = jnp.sin(ang)

    return pl.pallas_call(
        body, name="rope_angles", grid=(1,), out_shape=[jax.ShapeDtypeStruct((ROT_DIM // 2, seq), F32)] * 2,
        in_specs=[_full((1, seq)), _full((ROT_DIM // 2, 1))], out_specs=[_full((ROT_DIM // 2, seq))] * 2,
        compiler_params=_params("arbitrary"),
    )(pos_row, invf_col)


def _rope_lane_tables(cos, sin):
    cos_t, sin_t = cos.T, sin.T
    seq, half = cos_t.shape
    ones = jnp.ones((seq, HEAD_DIM - ROT_DIM), F32)
    c64 = jnp.concatenate([cos_t, cos_t, ones], axis=1)
    s1 = jnp.concatenate([sin_t, jnp.zeros((seq, HEAD_DIM - half), F32)], axis=1)
    s2 = jnp.concatenate([jnp.zeros((seq, half), F32), sin_t, jnp.zeros((seq, HEAD_DIM - ROT_DIM), F32)], axis=1)
    return jnp.concatenate([jnp.tile(t, (1, LANES // HEAD_DIM)) for t in (c64, s1, s2)], axis=1)


def _rope_apply(t, tab, sign):
    reps = t.shape[1] // LANES
    c_tab, s1, s2 = (jnp.tile(tab[:, LANES * k:LANES * (k + 1)], (1, reps)) if reps > 1
                     else tab[:, LANES * k:LANES * (k + 1)] for k in range(3))
    half = ROT_DIM // 2
    up = pltpu.roll(t, t.shape[1] - half, 1)
    down = pltpu.roll(t, half, 1)
    return t * c_tab + sign * (down * s2 - up * s1)


def _lane_masks(shape):
    lane = lax.broadcasted_iota(jnp.int32, shape, 1)
    return lane < HEAD_DIM, lane >= HEAD_DIM


HEADS_PER_GROUP = N_Q_HEADS // N_KV_HEADS
ATTN_SCALE = 1.0 / math.sqrt(HEAD_DIM)


def _attn_bias_t(first_block):
    kj = lax.broadcasted_iota(jnp.int32, (2 * CHUNK, CHUNK), 0)
    qi = lax.broadcasted_iota(jnp.int32, (2 * CHUNK, CHUNK), 1)
    ok = (kj > qi) & (kj <= qi + CHUNK)
    if first_block is not None:
        ok = ok & (jnp.logical_not(first_block) | (kj >= CHUNK))
    return jnp.tile(jnp.where(ok, 0.0, -jnp.inf), (1, HEADS_PER_GROUP))


def _group_rows(x, g, lo, hi):
    rows = []
    for r in range(HEADS_PER_GROUP):
        h = HEADS_PER_GROUP * g + r
        pair = x[:, LANES * (h // 2):LANES * (h // 2 + 1)]
        rows.append(jnp.where(hi if h % 2 else lo, pair, 0.0))
    return jnp.concatenate(rows, axis=0)


def _pairs_from_rows(rows, lo):
    return [jnp.where(lo, rows[2 * CHUNK * k:2 * CHUNK * k + CHUNK], rows[2 * CHUNK * k + CHUNK:2 * CHUNK * (k + 1)])
            for k in range(HEADS_PER_GROUP // 2)]


def _group_dup(a, b, g, lo2):
    return jnp.where(lo2, a, b) if g == 0 else jnp.where(lo2, b, a)


def _sink_row(sink_ref, g):
    return jnp.concatenate([sink_ref[HEADS_PER_GROUP * g + r:HEADS_PER_GROUP * g + r + 1, :]
                            for r in range(HEADS_PER_GROUP)], axis=1)


def _attn_probs_t(k_dup, q_rows, bias_t, sink_row):
    s_t = _dot_nt(k_dup, q_rows) * ATTN_SCALE + bias_t
    m = jnp.maximum(jnp.max(s_t, axis=0, keepdims=True), sink_row)
    p = jnp.exp(s_t - m)
    e_sink = jnp.exp(sink_row - m)
    inv = 1.0 / (jnp.sum(p, axis=0, keepdims=True) + e_sink)
    return p * inv, e_sink * inv


def _sgu_forward_pair(wm, vp, j):
    lo, hi = _lane_masks(vp.shape)
    lhs = jnp.concatenate([wm[2 * j], wm[2 * j + 1]], axis=1)
    rhs = jnp.concatenate([jnp.where(lo, vp, 0.0), jnp.where(hi, vp, 0.0)], axis=0)
    return _dot(lhs, rhs)


def _masked_spatial(w_ref):
    t = lax.broadcasted_iota(jnp.int32, (CHUNK, CHUNK), 0)
    s = lax.broadcasted_iota(jnp.int32, (CHUNK, CHUNK), 1)
    tril = s <= t
    return [jnp.where(tril, w_ref[g], 0.0) for g in range(GMLP_GROUPS)], tril, s >= t


def _mod_kernel(c_all, w_shard, b_shard):
    n = w_shard.shape[1]
    tn = 512

    def body(c_ref, w_ref, b_ref, mod_ref, act_ref):
        cv = c_ref[...]
        act = cv * (1.0 / (1.0 + jnp.exp(-cv)))
        act_ref[...] = act
        mod_ref[...] = _dot(act, w_ref[...]) + b_ref[...]

    return pl.pallas_call(
        body, name="ada_mod", grid=(n // tn,),
        out_shape=[jax.ShapeDtypeStruct((N_DEV, n), F32), jax.ShapeDtypeStruct((N_DEV, D_MODEL), F32)],
        in_specs=[_full((N_DEV, D_MODEL)), pl.BlockSpec((D_MODEL, tn), lambda i: (0, i)),
                  pl.BlockSpec((1, tn), lambda i: (0, i))],
        out_specs=[pl.BlockSpec((N_DEV, tn), lambda i: (0, i)), _full((N_DEV, D_MODEL))],
        compiler_params=_params("arbitrary"),
    )(c_all, w_shard, b_shard)


def _in_proj_kernel(x, vecs, w_in_t, comm=None):
    seq = x.shape[0]
    tm = 512

    def body(x_ref, v_ref, w_ref, proj_ref, h_ref):
        xv = x_ref[...]
        rstd = lax.rsqrt(_mean_last(xv * xv) + EPS)
        n1 = (xv * rstd) * v_ref[0:1, :]
        h = n1 * (1.0 + v_ref[2:3, :]) + v_ref[1:2, :]
        hb = h.astype(MXU_DTYPE)
        h_ref[...] = hb
        proj_ref[...] = _dot_nt(hb, w_ref[...])

    return _hosted_call(
        body, comm, name="in_proj", grid=(seq // tm,),
        out_shape=[jax.ShapeDtypeStruct((seq, IN_PROJ_WIDTH), F32),
                   jax.ShapeDtypeStruct((seq, D_MODEL), MXU_DTYPE)],
        in_specs=[pl.BlockSpec((tm, D_MODEL), lambda i: (i, 0)), _full((8, D_MODEL)),
                  _full((IN_PROJ_WIDTH, D_MODEL))],
        out_specs=[pl.BlockSpec((tm, IN_PROJ_WIDTH), lambda i: (i, 0)),
                   pl.BlockSpec((tm, D_MODEL), lambda i: (i, 0))],
        semantics=("arbitrary",),
    )(x, vecs, w_in_t)


MIXER_BLOCKS_PER_STEP = 2
KV_START = 2 * GMLP_WIDTH + ATTN_WIDTH


def _mixer_fwd_kernel(proj, rope_tab, w_spatial, bias_full, sink_rows, comm=None):
    seq = proj.shape[0]
    per = MIXER_BLOCKS_PER_STEP
    steps = seq // (CHUNK * per)
    kv_col = KV_START // (2 * KV_WIDTH)

    def body(proj_ref, prev_ref, tab_ref, ptab_ref, w_ref, bias_ref, sink_ref, cat_ref):
        i = pl.program_id(0)
        wm, _, _ = _masked_spatial(w_ref)
        lo, hi = _lane_masks((CHUNK, LANES))
        lo2, _ = _lane_masks((2 * CHUNK, LANES))
        o = 2 * GMLP_WIDTH
        for s in range(per):
            rows, before = slice(CHUNK * s, CHUNK * (s + 1)), slice(CHUNK * (s - 1), CHUNK * s)
            for j in range(GMLP_GROUPS // 2):
                cols = slice(LANES * j, LANES * (j + 1))
                vcols = slice(GMLP_WIDTH + LANES * j, GMLP_WIDTH + LANES * (j + 1))
                u, _ = _gelu_tanh(proj_ref[rows, cols])
                vp, _ = _gelu_tanh(proj_ref[rows, vcols])
                sv = _sgu_forward_pair(wm, vp, j) + bias_ref[:, cols]
                cat_ref[rows, cols] = (u * sv).astype(cat_ref.dtype)
            tab = tab_ref[rows, :]
            if s == 0:
                prev_kv, prev_tab, first = prev_ref[...], ptab_ref[...], i == 0
            else:
                prev_kv, prev_tab, first = proj_ref[before, KV_START:KV_START + 2 * KV_WIDTH], tab_ref[before, :], None
            q_r = _rope_apply(proj_ref[rows, o:o + ATTN_WIDTH], tab, 1.0)
            k_cur = _rope_apply(proj_ref[rows, KV_START:KV_START + KV_WIDTH], tab, 1.0)
            k_prev = _rope_apply(prev_kv[:, 0:KV_WIDTH], prev_tab, 1.0)
            k_a = jnp.concatenate([k_prev, k_cur], axis=0)
            v_a = jnp.concatenate([prev_kv[:, KV_WIDTH:2 * KV_WIDTH],
                                   proj_ref[rows, KV_START + KV_WIDTH:KV_START + 2 * KV_WIDTH]], axis=0)
            k_b = pltpu.roll(k_a, HEAD_DIM, 1)
            v_b = pltpu.roll(v_a, HEAD_DIM, 1)
            bias_t = _attn_bias_t(first)
            for g in range(N_KV_HEADS):
                p_t, _ = _attn_probs_t(_group_dup(k_a, k_b, g, lo2), _group_rows(q_r, g, lo, hi), bias_t,
                                       _sink_row(sink_ref, g))
                o_t = _dot(_group_dup(v_a, v_b, g, lo2).T, p_t)
                for k, pair in enumerate(_pairs_from_rows(o_t.T, lo)):
                    c0 = GMLP_WIDTH + LANES * (2 * g + k)
                    cat_ref[rows, c0:c0 + LANES] = pair.astype(cat_ref.dtype)

    return _hosted_call(
        body, comm, name="mixer_fwd", grid=(steps,),
        out_shape=[jax.ShapeDtypeStruct((seq, D_MODEL), MXU_DTYPE)],
        in_specs=[pl.BlockSpec((CHUNK * per, IN_PROJ_WIDTH), lambda i: (i, 0)),
                  pl.BlockSpec((CHUNK, 2 * KV_WIDTH), lambda i: (jnp.maximum(per * i - 1, 0), kv_col)),
                  pl.BlockSpec((CHUNK * per, 3 * LANES), lambda i: (i, 0)),
                  pl.BlockSpec((CHUNK, 3 * LANES), lambda i: (jnp.maximum(per * i - 1, 0), 0)),
                  _full((GMLP_GROUPS, CHUNK, CHUNK)), _full((CHUNK, GMLP_WIDTH)),
                  _full((N_Q_HEADS, LANES))],
        out_specs=[pl.BlockSpec((CHUNK * per, D_MODEL), lambda i: (i, 0))],
        semantics=("arbitrary",),
    )(proj, proj, rope_tab, rope_tab, w_spatial, bias_full, sink_rows)


def _trunk_kernel(x, target, cat, vecs, chip_idx, gathered, local):
    seq = x.shape[0]
    tm = 256
    nj = D_FF // D_MODEL
    out_rows = D_MODEL // N_CHIPS

    def body(chip_ref, x_ref, t_ref, cat_ref, v_ref, g_out, g_w1, g_w2, l_out, l_w1, l_w2,
             dx1_ref, dcat_ref, dmix_ref, h2_ref, r_ref, da_ref, dff_ref, sums_ref,
             wout, w1, w2, a_scr, sem):
        i = pl.program_id(0)

        @pl.when(i == 0)
        def _():
            dsts = ([wout.at[pl.ds(out_rows * k, out_rows)] for k in range(N_CHIPS)]
                    + [w1.at[k] for k in range(N_CHIPS)] + [w2.at[k] for k in range(N_CHIPS)])
            srcs = [(g, l) for g, l in ((g_out, l_out), (g_w1, l_w1), (g_w2, l_w2)) for _ in range(N_CHIPS)]
            for n, (dst, (g, l)) in enumerate(zip(dsts, srcs)):
                k = n % N_CHIPS

                @pl.when(chip_ref[0] == k)
                def _():
                    pltpu.make_async_copy(l, dst, sem.at[n]).start()

                @pl.when(chip_ref[0] != k)
                def _():
                    pltpu.make_async_copy(g.at[k], dst, sem.at[n]).start()
            for n, (dst, (g, l)) in enumerate(zip(dsts, srcs)):
                pltpu.make_async_copy(l, dst, sem.at[n]).wait()
            sums_ref[...] = jnp.zeros_like(sums_ref)

        gate1, shift2, scale2 = v_ref[0:1, :], v_ref[1:2, :], v_ref[2:3, :]
        gate2, g_ffn, g_final = v_ref[3:4, :], v_ref[4:5, :], v_ref[5:6, :]

        mix = _dot(cat_ref[...], wout[...])
        x1 = x_ref[...] + gate1 * mix
        rstd2 = lax.rsqrt(_mean_last(x1 * x1) + EPS)
        xh2 = x1 * rstd2
        n2 = xh2 * g_ffn
        h2b = (n2 * (1.0 + scale2) + shift2).astype(MXU_DTYPE)
        h2_ref[...] = h2b
        ff = jnp.zeros((tm, D_MODEL), F32)
        for j in range(nj):
            a = _dot(h2b, w1[j])
            a_scr[j] = a
            relu = jnp.maximum(a, 0.0)
            rb = (relu * relu).astype(MXU_DTYPE)
            r_ref[:, D_MODEL * j:D_MODEL * (j + 1)] = rb
            ff = ff + _dot(rb, w2[j])
        x2 = x1 + gate2 * ff
        rstd3 = lax.rsqrt(_mean_last(x2 * x2) + EPS)
        xh3 = x2 * rstd3
        err = xh3 * g_final - t_ref[...]
        loss = 0.5 * _rowsum(_mean_last(err * err))
        dy = err * (1.0 / D_MODEL)
        dxh3 = dy * g_final
        dx2 = rstd3 * (dxh3 - xh3 * _mean_last(dxh3 * xh3))
        dffb = (dx2 * gate2).astype(MXU_DTYPE)
        dff_ref[...] = dffb
        dh2 = jnp.zeros((tm, D_MODEL), F32)
        for j in range(nj):
            dr = _dot_nt(dffb, w2[j])
            dab = (dr * (2.0 * jnp.maximum(a_scr[j], 0.0))).astype(MXU_DTYPE)
            da_ref[:, D_MODEL * j:D_MODEL * (j + 1)] = dab
            dh2 = dh2 + _dot_nt(dab, w1[j])
        dn2 = dh2 * (1.0 + scale2)
        dxh2 = dn2 * g_ffn
        dx1 = dx2 + rstd2 * (dxh2 - xh2 * _mean_last(dxh2 * xh2))
        dx1_ref[...] = dx1
        dmixb = (dx1 * gate1).astype(MXU_DTYPE)
        dmix_ref[...] = dmixb
        dcat_ref[...] = _dot_nt(dmixb, wout[...])

        sums_ref[0:1, :] += _rowsum(dh2)
        sums_ref[1:2, :] += _rowsum(dh2 * n2)
        sums_ref[2:3, :] += _rowsum(dx2 * ff)
        sums_ref[3:4, :] += _rowsum(dn2 * xh2)
        sums_ref[4:5, :] += _rowsum(dy * xh3)
        sums_ref[5:6, :] += _rowsum(dx1 * mix)
        sums_ref[6:7, :] += jnp.broadcast_to(loss, (1, D_MODEL))

    tok = lambda w: pl.BlockSpec((tm, w), lambda i, chip: (i, 0))
    return _hosted_call(
        body, None, name="trunk", grid=(seq // tm,), n_prefetch=1,
        out_shape=[jax.ShapeDtypeStruct((seq, D_MODEL), F32), jax.ShapeDtypeStruct((seq, D_MODEL), F32),
                   jax.ShapeDtypeStruct((seq, D_MODEL), MXU_DTYPE), jax.ShapeDtypeStruct((seq, D_MODEL), MXU_DTYPE),
                   jax.ShapeDtypeStruct((seq, D_FF), MXU_DTYPE), jax.ShapeDtypeStruct((seq, D_FF), MXU_DTYPE),
                   jax.ShapeDtypeStruct((seq, D_MODEL), MXU_DTYPE), jax.ShapeDtypeStruct((8, D_MODEL), F32)],
        in_specs=[tok(D_MODEL), tok(D_MODEL), tok(D_MODEL), _full((8, D_MODEL))] + [_any()] * 6,
        out_specs=[tok(D_MODEL), tok(D_MODEL), tok(D_MODEL), tok(D_MODEL), tok(D_FF), tok(D_FF), tok(D_MODEL),
                   _full((8, D_MODEL))],
        scratch_shapes=[pltpu.VMEM((D_MODEL, D_MODEL), MXU_DTYPE), pltpu.VMEM((nj, D_MODEL, D_MODEL), MXU_DTYPE),
                        pltpu.VMEM((nj, D_MODEL, D_MODEL), MXU_DTYPE), pltpu.VMEM((nj, tm, D_MODEL), F32),
                        pltpu.SemaphoreType.DMA((3 * N_CHIPS,))],
        semantics=("arbitrary",),
    )(chip_idx, x, target, cat, vecs, *gathered, *local)


def _mixer_bwd_kernel(proj, rope_tab, dcat, w_spatial, w_spatial_t, bias_full, sink_rows, dev_idx, comm=None):
    seq = proj.shape[0]
    per = MIXER_BLOCKS_PER_STEP
    steps = seq // (CHUNK * per)
    kv_col = KV_START // (2 * KV_WIDTH)

    def body(dev_ref, proj_ref, prev_ref, tab_ref, ptab_ref, dcat_ref, w_ref, wt_ref, bias_ref, sink_ref,
             dproj_ref, dw_ref, db_ref, dsink_ref, carry):
        del dev_ref
        step = pl.program_id(0)

        @pl.when(step == 0)
        def _():
            carry[...] = jnp.zeros_like(carry)
            dw_ref[...] = jnp.zeros_like(dw_ref)
            db_ref[...] = jnp.zeros_like(db_ref)
            dsink_ref[...] = jnp.zeros_like(dsink_ref)

        for s in reversed(range(per)):
            rows = pl.ds(CHUNK * s, CHUNK)
            if s == 0:
                before, before_tab, first = prev_ref, ptab_ref, step == steps - 1
            else:
                before = proj_ref.at[pl.ds(CHUNK * (s - 1), CHUNK), pl.ds(KV_START, 2 * KV_WIDTH)]
                before_tab, first = tab_ref.at[pl.ds(CHUNK * (s - 1), CHUNK)], None
            one_block(proj_ref.at[rows], before, tab_ref.at[rows], before_tab, dcat_ref.at[rows], w_ref, wt_ref,
                      bias_ref, sink_ref, dproj_ref.at[rows], dw_ref, db_ref, dsink_ref, carry, first)

    def one_block(proj_ref, prev_ref, tab_ref, ptab_ref, dcat_ref, w_ref, wt_ref, bias_ref, sink_ref,
                  dproj_ref, dw_ref, db_ref, dsink_ref, carry, first):
        wm, tril, triu = _masked_spatial(w_ref)
        lo, hi = _lane_masks((CHUNK, LANES))
        lane = lax.broadcasted_iota(jnp.int32, (CHUNK, LANES), 1)
        db = jnp.zeros((CHUNK, LANES), F32)
        for j in range(GMLP_GROUPS // 2):
            cols = slice(LANES * j, LANES * (j + 1))
            vcols = slice(GMLP_WIDTH + LANES * j, GMLP_WIDTH + LANES * (j + 1))
            zu, zv = proj_ref[:, cols], proj_ref[:, vcols]
            u, tu = _gelu_tanh(zu)
            vp, tv = _gelu_tanh(zv)
            sv = _sgu_forward_pair(wm, vp, j) + bias_ref[:, cols]
            dout = dcat_ref[:, cols]
            du = dout * sv
            dsv = dout * u
            dsv_lo, dsv_hi = jnp.where(lo, dsv, 0.0), jnp.where(hi, dsv, 0.0)
            lhs_t = jnp.concatenate([jnp.where(triu, wt_ref[2 * j], 0.0),
                                     jnp.where(triu, wt_ref[2 * j + 1], 0.0)], axis=1)
            dv = _dot(lhs_t, jnp.concatenate([dsv_lo, dsv_hi], axis=0))
            dw_ref[2 * j] += jnp.where(tril, _dot_nt(dsv_lo, vp), 0.0)
            dw_ref[2 * j + 1] += jnp.where(tril, _dot_nt(dsv_hi, vp), 0.0)
            db = db + (jnp.where(lane == 2 * j, jnp.sum(dsv_lo, axis=1, keepdims=True), 0.0)
                       + jnp.where(lane == 2 * j + 1, jnp.sum(dsv_hi, axis=1, keepdims=True), 0.0))
            dproj_ref[:, cols] = (du * _gelu_tanh_grad(zu, tu)).astype(dproj_ref.dtype)
            dproj_ref[:, vcols] = (dv * _gelu_tanh_grad(zv, tv)).astype(dproj_ref.dtype)
        db_ref[...] += db
        o = 2 * GMLP_WIDTH
        tab = tab_ref[...]
        q_r = _rope_apply(proj_ref[:, o:o + ATTN_WIDTH], tab, 1.0)
        k_cur = _rope_apply(proj_ref[:, o + ATTN_WIDTH:o + ATTN_WIDTH + KV_WIDTH], tab, 1.0)
        k_prev = _rope_apply(prev_ref[:, 0:KV_WIDTH], ptab_ref[...], 1.0)
        k_a = jnp.concatenate([k_prev, k_cur], axis=0)
        v_a = jnp.concatenate([prev_ref[:, KV_WIDTH:2 * KV_WIDTH],
                               proj_ref[:, o + ATTN_WIDTH + KV_WIDTH:o + ATTN_WIDTH + 2 * KV_WIDTH]], axis=0)
        k_b = pltpu.roll(k_a, HEAD_DIM, 1)
        v_b = pltpu.roll(v_a, HEAD_DIM, 1)
        bias_t = _attn_bias_t(first)
        lo2, _ = _lane_masks((2 * CHUNK, LANES))
        dout_b = dcat_ref[:, GMLP_WIDTH:GMLP_WIDTH + ATTN_WIDTH]
        dk_tot, dv_tot, dq_pairs = [], [], []
        for g in range(N_KV_HEADS):
            k_dup, v_dup = _group_dup(k_a, k_b, g, lo2), _group_dup(v_a, v_b, g, lo2)
            q_rows = _group_rows(q_r, g, lo, hi)
            do_rows = _group_rows(dout_b, g, lo, hi)
            p_t, p_sink = _attn_probs_t(k_dup, q_rows, bias_t, _sink_row(sink_ref, g))
            dp_t = _dot_nt(v_dup, do_rows)
            delta = jnp.sum(p_t * dp_t, axis=0, keepdims=True)
            ds_t = p_t * (dp_t - delta) * ATTN_SCALE
            dsink = -p_sink * delta
            for r in range(HEADS_PER_GROUP):
                h = HEADS_PER_GROUP * g + r
                dsink_ref[h:h + 1, :] += jnp.broadcast_to(
                    jnp.sum(dsink[:, LANES * r:LANES * (r + 1)], axis=1, keepdims=True), (1, LANES))
            dk_full = _dot(ds_t, q_rows)
            dv_full = _dot(p_t, do_rows)
            dk_tot.append(dk_full + pltpu.roll(dk_full, HEAD_DIM, 1))
            dv_tot.append(dv_full + pltpu.roll(dv_full, HEAD_DIM, 1))
            dq_t = _dot(k_dup.T, ds_t)
            dq_pairs += _pairs_from_rows(dq_t.T, lo)
        dk_all = jnp.where(lo2, dk_tot[0], dk_tot[1])
        dv_all = jnp.where(lo2, dv_tot[0], dv_tot[1])
        dk_cur = dk_all[CHUNK:, :] + carry[:, 0:KV_WIDTH]
        dv_cur = dv_all[CHUNK:, :] + carry[:, KV_WIDTH:2 * KV_WIDTH]
        carry[:, 0:KV_WIDTH] = dk_all[:CHUNK, :]
        carry[:, KV_WIDTH:2 * KV_WIDTH] = dv_all[:CHUNK, :]
        dq = _rope_apply(jnp.concatenate(dq_pairs, axis=1), tab, -1.0)
        dproj_ref[:, o:o + ATTN_WIDTH] = dq.astype(dproj_ref.dtype)
        dproj_ref[:, o + ATTN_WIDTH:o + ATTN_WIDTH + KV_WIDTH] = (
            _rope_apply(dk_cur, tab, -1.0).astype(dproj_ref.dtype))
        dproj_ref[:, o + ATTN_WIDTH + KV_WIDTH:o + ATTN_WIDTH + 2 * KV_WIDTH] = dv_cur.astype(dproj_ref.dtype)

    rev = lambda i: steps - 1 - i
    before = lambda i: jnp.maximum(per * rev(i) - 1, 0)
    slot = lambda shape: pl.BlockSpec((None,) + shape, lambda i, d: (d[0],) + (0,) * len(shape))
    return _hosted_call(
        body, comm, name="mixer_bwd", grid=(steps,), n_prefetch=1,
        out_shape=[jax.ShapeDtypeStruct((seq, IN_PROJ_WIDTH), MXU_DTYPE),
                   jax.ShapeDtypeStruct((N_DEV, GMLP_GROUPS, CHUNK, CHUNK), F32),
                   jax.ShapeDtypeStruct((N_DEV, CHUNK, LANES), F32),
                   jax.ShapeDtypeStruct((N_DEV, N_Q_HEADS, LANES), F32)],
        in_specs=[pl.BlockSpec((CHUNK * per, IN_PROJ_WIDTH), lambda i, d: (rev(i), 0)),
                  pl.BlockSpec((CHUNK, 2 * KV_WIDTH), lambda i, d: (before(i), kv_col)),
                  pl.BlockSpec((CHUNK * per, 3 * LANES), lambda i, d: (rev(i), 0)),
                  pl.BlockSpec((CHUNK, 3 * LANES), lambda i, d: (before(i), 0)),
                  pl.BlockSpec((CHUNK * per, D_MODEL), lambda i, d: (rev(i), 0)),
                  _full((GMLP_GROUPS, CHUNK, CHUNK)), _full((GMLP_GROUPS, CHUNK, CHUNK)),
                  _full((CHUNK, GMLP_WIDTH)), _full((N_Q_HEADS, LANES))],
        out_specs=[pl.BlockSpec((CHUNK * per, IN_PROJ_WIDTH), lambda i, d: (rev(i), 0)),
                   slot((GMLP_GROUPS, CHUNK, CHUNK)), slot((CHUNK, LANES)), slot((N_Q_HEADS, LANES))],
        scratch_shapes=[pltpu.VMEM((CHUNK, 2 * KV_WIDTH), F32)],
        semantics=("arbitrary",),
    )(dev_idx, proj, proj, rope_tab, rope_tab, dcat, w_spatial, w_spatial_t, bias_full, sink_rows)


def _in_proj_bwd_kernel(x, dx1, dproj, vecs, w_in_t, comm=None):
    seq = x.shape[0]
    tm = 512

    def body(x_ref, dx1_ref, dp_ref, v_ref, w_ref, gx_ref, sums_ref):
        @pl.when(pl.program_id(0) == 0)
        def _():
            sums_ref[...] = jnp.zeros_like(sums_ref)

        g_mix, scale1 = v_ref[0:1, :], v_ref[2:3, :]
        dh = _dot(dp_ref[...], w_ref[...])
        xv = x_ref[...]
        rstd = lax.rsqrt(_mean_last(xv * xv) + EPS)
        xh = xv * rstd
        dn1 = dh * (1.0 + scale1)
        dxh = dn1 * g_mix
        gx_ref[...] = dx1_ref[...] + rstd * (dxh - xh * _mean_last(dxh * xh))
        sums_ref[0:1, :] += _rowsum(dh)
        sums_ref[1:2, :] += _rowsum(dh * (xh * g_mix))
        sums_ref[2:3, :] += _rowsum(dn1 * xh)

    return _hosted_call(
        body, comm, name="in_proj_bwd", grid=(seq // tm,),
        out_shape=[jax.ShapeDtypeStruct((seq, D_MODEL), F32), jax.ShapeDtypeStruct((8, D_MODEL), F32)],
        in_specs=[pl.BlockSpec((tm, D_MODEL), lambda i: (i, 0)), pl.BlockSpec((tm, D_MODEL), lambda i: (i, 0)),
                  pl.BlockSpec((tm, IN_PROJ_WIDTH), lambda i: (i, 0)), _full((8, D_MODEL)),
                  _full((IN_PROJ_WIDTH, D_MODEL))],
        out_specs=[pl.BlockSpec((tm, D_MODEL), lambda i: (i, 0)), _full((8, D_MODEL))],
        semantics=("arbitrary",),
    )(x, dx1, dproj, vecs, w_in_t)


class _GradTiles(NamedTuple):
    tm: int
    tn: int
    n_tiles: int
    chips_per_tile: int
    a_index: Callable
    b_index: Callable


def _weight_grad_kernel(a, b, c_idx, name, tiles, comm=None):
    seq = a.shape[0]
    tk = min(seq, 4096)
    nk = seq // tk
    tm, tn, n_tiles, per = tiles.tm, tiles.tn, tiles.n_tiles, tiles.chips_per_tile
    rows = tm // per

    def half(phase, c):
        return phase * c[0] + (1 - phase) * (1 - c[0])

    def body(c_ref, a_ref, b_ref, o_ref, acc, stage, landed, send_sems, recv_sems):
        del c_ref
        phase, t, kk = pl.program_id(0), pl.program_id(1), pl.program_id(2)
        x, y, c, _ = _mesh_place()

        def copy(tile):
            return pltpu.make_async_remote_copy(
                src_ref=stage.at[tile], dst_ref=landed.at[tile], send_sem=send_sems.at[tile],
                recv_sem=recv_sems.at[tile], device_id=(x, y, 1 - c), device_id_type=MESH)

        @pl.when(kk == 0)
        def _():
            acc[...] = jnp.zeros_like(acc)

        acc[...] += _dot_tn(a_ref[...], b_ref[...])

        @pl.when((kk == nk - 1) & (phase == 0))
        def _():
            stage[t] = acc[...].astype(stage.dtype)
            copy(t).start()

        @pl.when((kk == nk - 1) & (phase == 1))
        def _():
            copy(t).wait_recv()
            total = acc[...] + landed[t].astype(F32)
            for q in range(per):
                o_ref[q] = total[rows * q:rows * (q + 1)].astype(o_ref.dtype)

        @pl.when((kk == nk - 1) & (phase == 1) & (t == n_tiles - 1))
        def _():
            for tile in range(n_tiles):
                copy(tile).wait_send()

    out = _hosted_call(
        body, comm, name=name, grid=(2, n_tiles, nk), n_prefetch=1,
        out_shape=[jax.ShapeDtypeStruct((n_tiles * per, rows, tn), GRAD_COMM_DTYPE)],
        in_specs=[pl.BlockSpec((tk, tm), lambda p, t, k, c: (k, tiles.a_index(t, half(p, c)))),
                  pl.BlockSpec((tk, tn), lambda p, t, k, c: (k, tiles.b_index(t, half(p, c))))],
        out_specs=[pl.BlockSpec((per, rows, tn), lambda p, t, k, c: (p * t, 0, 0))],
        scratch_shapes=[pltpu.VMEM((tm, tn), F32), pltpu.VMEM((n_tiles, tm, tn), GRAD_COMM_DTYPE),
                        pltpu.VMEM((n_tiles, tm, tn), GRAD_COMM_DTYPE),
                        pltpu.SemaphoreType.DMA((n_tiles,)), pltpu.SemaphoreType.DMA((n_tiles,))],
        semantics=("arbitrary", "arbitrary", "arbitrary"),
    )(c_idx, a, b)
    return out[0] if comm is None else out


def _row_tile(rows, most=256, sublanes=16):
    return max(t for t in range(sublanes, most + 1, sublanes) if rows % t == 0)


def _adam_update(w, g, m, v):
    m_new = ADAM_B1 * m + (1.0 - ADAM_B1) * g
    v_new = ADAM_B2 * v + (1.0 - ADAM_B2) * (g * g)
    m_hat = m_new / (1.0 - ADAM_B1 ** ADAM_STEP)
    v_hat = v_new / (1.0 - ADAM_B2 ** ADAM_STEP)
    delta = -ADAM_LR * (m_hat / (jnp.sqrt(v_hat) + ADAM_EPS) + ADAM_WD * w)
    return delta, m_new, v_new


def _sum_chips_kernel(own, others, place, name):
    _, r, n = own.shape
    tr = _row_tile(r)

    def body(place_ref, own_ref, oth_ref, o_ref):
        del place_ref
        acc = own_ref[...].astype(F32)
        for k in range(N_CHIPS - 1):
            acc = acc + oth_ref[k].astype(F32)
        o_ref[...] = acc

    return pl.pallas_call(
        body, name=name, out_shape=jax.ShapeDtypeStruct((2, r, n), F32),
        grid_spec=pltpu.PrefetchScalarGridSpec(
            num_scalar_prefetch=1, grid=(r // tr,),
            in_specs=[pl.BlockSpec((None, tr, n), lambda i, p: (p[0], i, 0)),
                      pl.BlockSpec((N_CHIPS - 1, tr, n), lambda i, p: (0, i, 0))],
            out_specs=pl.BlockSpec((None, tr, n), lambda i, p: (p[1], i, 0))),
        compiler_params=_params("parallel"),
    )(place, own, others)


def _adam_kernel(w, g, m, v, name):
    r, n = w.shape
    by_columns = g.shape[1] == r
    tr, tn = _row_tile(g.shape[1], most=512), g.shape[2]

    def body(w_ref, g_ref, m_ref, v_ref, g_out, d_ref, mo_ref, vo_ref):
        gv = g_ref[...]
        g_out[...] = gv
        d_ref[...], mo_ref[...], vo_ref[...] = _adam_update(w_ref[...], gv, m_ref[...], v_ref[...])

    steps = g.shape[1] // tr
    spec = pl.BlockSpec((tr, tn), (lambda h, i: (i, h)) if by_columns else (lambda h, i: (h * steps + i, 0)))
    return pl.pallas_call(
        body, name=name, grid=(2, steps), out_shape=[jax.ShapeDtypeStruct((r, n), F32)] * 4,
        in_specs=[spec, pl.BlockSpec((None, tr, tn), lambda h, i: (h, i, 0)), spec, spec], out_specs=[spec] * 4,
        compiler_params=_params("parallel", "parallel"),
    )(w, g, m, v)


SMALL_PARAMS = ("b_ada", "g_mix", "g_ffn", "g_final", "b_spatial", "sinks", "w_spatial")


def _small_update_kernel(gathered, params):
    shapes = [params[nm][0].shape for nm in SMALL_PARAMS]

    def body(*refs):
        g_refs, refs = refs[:5], refs[5:]
        p_refs, refs = refs[:3 * len(SMALL_PARAMS)], refs[3 * len(SMALL_PARAMS):]
        loss_ref, o_refs = refs[0], refs[1:]

        def total(ref):
            acc = ref[0]
            for k in range(1, N_DEV):
                acc = acc + ref[k]
            return acc

        s1, s2, db, ds, dw = (total(r) for r in g_refs)
        loss_ref[...] = jnp.broadcast_to(s2[6:7, 0:1], loss_ref.shape)
        grads = {"b_ada": [s1[0:1], s1[1:2], s2[5:6], s2[0:1], s2[1:2], s2[2:3]], "g_mix": [s1[2:3]],
                 "g_ffn": [s2[3:4]], "g_final": [s2[4:5]], "b_spatial": [db.T[0:GMLP_GROUPS]],
                 "w_spatial": [dw]}
        lane = lax.broadcasted_iota(jnp.int32, (1, LANES), 1)
        sink_row = jnp.zeros((1, LANES), F32)
        for h in range(N_Q_HEADS):
            sink_row = sink_row + jnp.where(lane == h, ds[h:h + 1, :], 0.0)
        grads["sinks"] = [sink_row[:, 0:N_Q_HEADS]]
        for i, nm in enumerate(SMALL_PARAMS):
            w_ref, m_ref, v_ref = p_refs[3 * i:3 * i + 3]
            outs = o_refs[4 * i:4 * i + 4]
            width = grads[nm][0].shape[1]
            for k, g in enumerate(grads[nm]):
                cols = slice(width * k, width * (k + 1))
                upd = _adam_update(w_ref[:, cols], g, m_ref[:, cols], v_ref[:, cols])
                for o_ref, val in zip(outs, (g,) + upd):
                    o_ref[:, cols] = val

    flat = [a for nm in SMALL_PARAMS for a in params[nm]]
    out_shape = [jax.ShapeDtypeStruct((8, LANES), F32)]
    out_shape += [jax.ShapeDtypeStruct(s, F32) for s in shapes for _ in range(4)]
    outs = pl.pallas_call(
        body, name="small_update", grid=(1,), out_shape=out_shape,
        in_specs=[_full(g.shape) for g in gathered] + [_full(a.shape) for a in flat],
        out_specs=[_full(s.shape) for s in out_shape],
        compiler_params=_params("arbitrary"),
    )(*gathered, *flat)
    return {nm: outs[1 + 4 * i:5 + 4 * i] for i, nm in enumerate(SMALL_PARAMS)}, outs[0]


def _ada_update_kernel(act_t, dmod, w, m, v):
    r, n = w.shape
    tr = 256

    def body(a_ref, d_ref, w_ref, m_ref, v_ref, g_ref, dl_ref, mo_ref, vo_ref):
        g = _dot(a_ref[...], d_ref[...])
        g_ref[...] = g
        dl_ref[...], mo_ref[...], vo_ref[...] = _adam_update(w_ref[...], g, m_ref[...], v_ref[...])

    spec = pl.BlockSpec((tr, n), lambda i: (i, 0))
    return pl.pallas_call(
        body, name="ada_update", grid=(r // tr,), out_shape=[jax.ShapeDtypeStruct((r, n), F32)] * 4,
        in_specs=[pl.BlockSpec((tr, N_DEV), lambda i: (i, 0)), _full((N_DEV, n)), spec, spec, spec],
        out_specs=[spec] * 4, compiler_params=_params("parallel"),
    )(act_t, dmod, w, m, v)


def kernel(x, c, positions, w_ada, b_ada, g_mix, w_in, w_spatial, b_spatial, sinks, w_out, g_ffn, w_ff1, w_ff2, g_final, loss_target, m_w_ada, m_b_ada, m_g_mix, m_w_in, m_w_spatial, m_b_spatial, m_sinks, m_w_out, m_g_ffn, m_w_ff1, m_w_ff2, m_g_final, v_w_ada, v_b_ada, v_g_mix, v_w_in, v_w_spatial, v_b_spatial, v_sinks, v_w_out, v_g_ffn, v_w_ff1, v_w_ff2, v_g_final):
    xi, yi, ci = lax.axis_index("x"), lax.axis_index("y"), lax.axis_index("c")
    chip = 2 * xi + yi
    dev = 2 * chip + ci
    seq = x.shape[1]
    x2, tgt = x[0], loss_target[0]
    ada_cols = w_ada.shape[2]

    big = {"w_in": tuple(a[0].T for a in (w_in, m_w_in, v_w_in)),
           "w_out": (w_out[0], m_w_out[0], v_w_out[0]), "w_ff1": (w_ff1[0], m_w_ff1[0], v_w_ff1[0]),
           "w_ff2": (w_ff2[0], m_w_ff2[0], v_w_ff2[0])}

    def halves(nm):
        r, n = big[nm][0].shape
        return big[nm][0].astype(WEIGHT_COMM_DTYPE).reshape(2, r // 2, n)

    c_all, w_in_t = _all_gather8([c, halves("w_in")], "gather_c_w_in", split=[False, True])
    c_all, w_in_t = c_all.reshape(N_DEV, D_MODEL), w_in_t.reshape(IN_PROJ_WIDTH, D_MODEL)
    b_shard = lax.dynamic_slice(b_ada, (0, chip * ada_cols), (1, ada_cols))
    mod_part, act = _mod_kernel(c_all, w_ada[0], b_shard)
    mod_all, = _all_gather8([mod_part], "gather_mod")
    mod_me = lax.dynamic_index_in_dim(mod_all[0::2], dev, axis=1, keepdims=False)
    mod_me = mod_me.reshape(N_MOD, D_MODEL)
    shift1, scale1, gate1, shift2, scale2, gate2 = (mod_me[k:k + 1] for k in range(N_MOD))

    zeros_row = jnp.zeros((1, D_MODEL), F32)
    vecs1 = jnp.concatenate([g_mix, shift1, scale1] + [zeros_row] * 5, axis=0)
    vecs2 = jnp.concatenate([gate1, shift2, scale2, gate2, g_ffn, g_final.reshape(1, D_MODEL)]
                            + [zeros_row] * 2, axis=0)
    bias_full = jnp.repeat(b_spatial[0].T, HEAD_DIM, axis=1)
    sink_rows = jnp.broadcast_to(sinks[0][:, None], (N_Q_HEADS, LANES))
    inv_freq = ROPE_THETA ** (-jnp.arange(0, ROT_DIM, 2, dtype=F32) / ROT_DIM)
    rope_tab = _rope_lane_tables(*_rope_angle_kernel(positions, inv_freq.reshape(ROT_DIM // 2, 1)))

    trunk_weights = ["w_out", "w_ff1", "w_ff2"]
    shards = [halves(nm) for nm in trunk_weights]
    proj, hb, *staged = _in_proj_kernel(x2, vecs1, w_in_t, comm=_gather2d_first(shards))
    cat, *staged = _mixer_fwd_kernel(proj, rope_tab, w_spatial[0], bias_full, sink_rows,
                                     comm=_gather2d_second(staged, shards))
    staged = _gather_forward(staged, "gather_forward")
    dx1, dcat, dmix, h2b, rb, dab, dffb, sums2 = _trunk_kernel(
        x2, tgt, cat, vecs2, chip.reshape(1).astype(jnp.int32),
        [g.reshape((N_CHIPS,) + big[nm][0].shape) for nm, g in zip(trunk_weights, staged)],
        [s.reshape(big[nm][0].shape) for nm, s in zip(trunk_weights, shards)])

    c_idx = ci.reshape(1).astype(jnp.int32)
    place = jnp.stack([chip, ci]).astype(jnp.int32)
    half_d = D_MODEL // 2
    cs_ff2 = _weight_grad_kernel(rb, dffb, c_idx, "dw_ff2",
                                 _GradTiles(D_MODEL, half_d, N_CHIPS, 1, lambda t, h: t, lambda t, h: h))
    cs_ff1, sc_ff2 = _weight_grad_kernel(
        h2b, dab, c_idx, "dw_ff1",
        _GradTiles(D_MODEL, half_d, N_CHIPS, 1, lambda t, h: 0, lambda t, h: 2 * t + h),
        comm=_scatter_job([cs_ff2]))
    cs_out = _weight_grad_kernel(cat, dmix, c_idx, "dw_out",
                                 _GradTiles(D_MODEL, half_d, 1, N_CHIPS, lambda t, h: 0, lambda t, h: h))
    dproj, dw_spatial, db_lanes, dsink_rows, sc_ff1, sc_out = _mixer_bwd_kernel(
        proj, rope_tab, dcat, w_spatial[0], w_spatial[0].transpose(0, 2, 1), bias_full, sink_rows,
        dev.reshape(1).astype(jnp.int32), comm=_scatter_job([cs_ff1, cs_out]))
    cs_in, *small_stage1 = _weight_grad_kernel(
        dproj, hb, c_idx, "dw_in",
        _GradTiles(2 * W_IN_BLOCK, half_d, N_CHIPS // 2, 2, lambda t, h: t, lambda t, h: h),
        comm=_gather_job([db_lanes, dsink_rows, dw_spatial.reshape(N_DEV, GMLP_GROUPS * CHUNK, CHUNK)]))
    grad_x, sums1 = _in_proj_bwd_kernel(x2, dx1, dproj, vecs1, w_in_t)
    *gathered, sc_in = _all_gather8([sums1, sums2], "gather_small", forward=small_stage1,
                                    riders=[_scatter_job([cs_in])])

    names = ["w_in", "w_out", "w_ff1", "w_ff2"]
    totals = [_sum_chips_kernel(own, oth, place, "grad_sum_" + nm)
              for nm, own, oth in zip(names, [cs_in, cs_out, cs_ff1, cs_ff2], [sc_in, sc_out, sc_ff1, sc_ff2])]
    shared = _sibling_share(totals, "grad_share")
    big_out = {}
    for nm, g in zip(names, shared):
        w, m, v = big[nm]
        outs = _adam_kernel(w, g, m, v, "adam_" + nm)
        big_out[nm] = tuple((t.T if nm == "w_in" else t)[None] for t in outs)

    small = {"b_ada": (b_ada, m_b_ada, v_b_ada), "g_mix": (g_mix, m_g_mix, v_g_mix),
             "g_ffn": (g_ffn, m_g_ffn, v_g_ffn), "g_final": (g_final, m_g_final, v_g_final),
             "b_spatial": (b_spatial, m_b_spatial, v_b_spatial), "sinks": (sinks, m_sinks, v_sinks),
             "w_spatial": (w_spatial, m_w_spatial, v_w_spatial)}
    flat_shape = {"g_final": (1, D_MODEL), "b_spatial": (GMLP_GROUPS, CHUNK), "w_spatial": (GMLP_GROUPS * CHUNK, CHUNK)}
    small_out, loss_tile = _small_update_kernel(
        gathered, {nm: tuple(a.reshape(flat_shape.get(nm, a.shape)) for a in small[nm]) for nm in small})
    small_out = {nm: [o.reshape(small[nm][0].shape) for o in small_out[nm]] for nm in small}
    loss = loss_tile[0, 0]

    g1, g2 = gathered[0], gathered[1]
    dmod_all = jnp.concatenate([g1[:, 0], g1[:, 1], g2[:, 5], g2[:, 0], g2[:, 1], g2[:, 2]], axis=1)
    dmod_cols = lax.dynamic_slice(dmod_all, (0, chip * ada_cols), (N_DEV, ada_cols))
    ada = _ada_update_kernel(act.T, dmod_cols, w_ada[0], m_w_ada[0], v_w_ada[0])
    big_out["w_ada"] = tuple(t[None] for t in ada)

    order = ["w_ada", "b_ada", "g_mix", "w_in", "w_spatial", "b_spatial", "sinks", "w_out", "g_ffn",
             "w_ff1", "w_ff2", "g_final"]

    def leaf(nm, k):
        return big_out[nm][k] if nm in big_out else small_out[nm][k]

    outs = [loss, grad_x[None]]
    for k in range(4):
        outs += [leaf(nm, k) for nm in order]
    return tuple(outs)
```

```python
import math
from typing import Callable, NamedTuple

import jax
import jax.numpy as jnp
from jax import lax
from jax.experimental import pallas as pl
from jax.experimental.pallas import tpu as pltpu

F32 = jnp.float32
MXU_DTYPE = jnp.bfloat16
WEIGHT_COMM_DTYPE = jnp.bfloat16
GRAD_COMM_DTYPE = jnp.bfloat16

D_MODEL = 1024
D_FF = 4096
HEAD_DIM = 64
GMLP_GROUPS = 8
GMLP_WIDTH = 512
CHUNK = 128
N_Q_HEADS = 8
N_KV_HEADS = 2
ATTN_WIDTH = 512
KV_WIDTH = 128
ROT_DIM = 16
ROPE_THETA = 500000.0
IN_PROJ_WIDTH = 1792
N_MOD = 6
EPS = 1e-5
N_CHIPS = 4
N_DEV = 8
LANES = 128
W_IN_BLOCK = IN_PROJ_WIDTH // N_CHIPS

ADAM_LR = 0.001
ADAM_B1 = 0.9
ADAM_B2 = 0.999
ADAM_EPS = 1e-08
ADAM_WD = 0.01
ADAM_STEP = 10

VMEM_LIMIT_BYTES = 58 * 1024 * 1024
MESH = pl.DeviceIdType.MESH


def _params(*semantics):
    return pltpu.CompilerParams(dimension_semantics=semantics, vmem_limit_bytes=VMEM_LIMIT_BYTES)


def _dot(a, b):
    return jnp.dot(a.astype(MXU_DTYPE), b.astype(MXU_DTYPE), preferred_element_type=F32)


def _dot_nt(a, b):
    return lax.dot_general(a.astype(MXU_DTYPE), b.astype(MXU_DTYPE), (((1,), (1,)), ((), ())),
                           preferred_element_type=F32)


def _dot_tn(a, b):
    return lax.dot_general(a.astype(MXU_DTYPE), b.astype(MXU_DTYPE), (((0,), (0,)), ((), ())),
                           preferred_element_type=F32)


def _full(shape):
    return pl.BlockSpec(shape, lambda *_: (0,) * len(shape))


def _any():
    return pl.BlockSpec(memory_space=pl.ANY)


def _rowsum(v):
    return jnp.sum(v, axis=0, keepdims=True)


def _mean_last(v):
    return jnp.mean(v, axis=-1, keepdims=True)


class _Comm(NamedTuple):
    operands: tuple
    out_shapes: tuple
    n_sems: int
    make: Callable
    in_place: int = 0


def _hosted_call(body, comm, *, name, grid, in_specs, out_shape, out_specs, scratch_shapes=(), semantics,
                 n_prefetch=0):
    if comm is None:
        return pl.pallas_call(
            body, name=name, out_shape=out_shape, compiler_params=_params(*semantics),
            grid_spec=pltpu.PrefetchScalarGridSpec(
                num_scalar_prefetch=n_prefetch, grid=grid, in_specs=in_specs, out_specs=out_specs,
                scratch_shapes=list(scratch_shapes)))
    n_in, n_out, n_scr = len(in_specs), len(out_shape), len(scratch_shapes)
    k_in, k_out = len(comm.operands), len(comm.out_shapes)

    def hosted(*refs):
        prefetched, refs = refs[:n_prefetch], refs[n_prefetch:]
        ins, refs = refs[:n_in], refs[n_in:]
        c_ins, refs = refs[:k_in], refs[k_in:]
        outs, refs = refs[:n_out], refs[n_out:]
        c_outs, refs = refs[:k_out], refs[k_out:]
        scratch, (send_sems, recv_sems) = refs[:n_scr], refs[n_scr:]
        first, last = None, None
        for d, size in enumerate(grid):
            at_start, at_end = pl.program_id(d) == 0, pl.program_id(d) == size - 1
            first = at_start if first is None else first & at_start
            last = at_end if last is None else last & at_end

        @pl.when(first)
        def _():
            for cp in comm.make(c_ins, c_outs, send_sems, recv_sems)[0]:
                cp.start()

        body(*prefetched, *ins, *outs, *scratch)

        @pl.when(last)
        def _():
            for wait in comm.make(c_ins, c_outs, send_sems, recv_sems)[1]:
                wait()

    aliases = {n_prefetch + n_in + i: n_out + i for i in range(comm.in_place)}
    call = pl.pallas_call(
        hosted, name=name, out_shape=list(out_shape) + list(comm.out_shapes),
        compiler_params=_params(*semantics), input_output_aliases=aliases,
        grid_spec=pltpu.PrefetchScalarGridSpec(
            num_scalar_prefetch=n_prefetch, grid=grid, in_specs=list(in_specs) + [_any()] * k_in,
            out_specs=list(out_specs) + [_any()] * k_out,
            scratch_shapes=list(scratch_shapes) + [pltpu.SemaphoreType.DMA((comm.n_sems,)),
                                                    pltpu.SemaphoreType.DMA((comm.n_sems,))]))
    return lambda *args: call(*args, *comm.operands)


class _Shifted:
    def __init__(self, base, offset):
        self.base, self.offset = base, offset

    @property
    def at(self):
        return self

    def __getitem__(self, k):
        return self.base.at[self.offset + k]


def _merge_in_place(*jobs):
    assert all(j.in_place == len(j.operands) == len(j.out_shapes) for j in jobs)

    def make(ins, outs, send_sems, recv_sems):
        starts, waits, at, sem = [], [], 0, 0
        for j in jobs:
            n = len(j.operands)
            s, w = j.make(ins[at:at + n], outs[at:at + n], _Shifted(send_sems, sem), _Shifted(recv_sems, sem))
            starts, waits, at, sem = starts + s, waits + w, at + n, sem + j.n_sems
        return starts, waits

    operands = tuple(a for j in jobs for a in j.operands)
    return _Comm(operands, tuple(s for j in jobs for s in j.out_shapes), sum(j.n_sems for j in jobs), make,
                 in_place=len(operands))


def _mesh_place():
    x, y, c = lax.axis_index("x"), lax.axis_index("y"), lax.axis_index("c")
    return x, y, c, [(1 - x, y), (x, 1 - y), (1 - x, 1 - y)]


def _gather_job(bufs):
    per = 4

    def make(ins, outs, send_sems, recv_sems):
        del ins
        x, y, c, chips = _mesh_place()
        starts, waits = [], []
        for a, out in enumerate(outs):
            mine = src = out.at[4 * x + 2 * y + c]
            to = [(x, y, 1 - c)] + [(px, py, c) for px, py in chips]
            sends = [pltpu.make_async_remote_copy(
                src_ref=src, dst_ref=mine, send_sem=send_sems.at[per * a + k],
                recv_sem=recv_sems.at[per * a + k], device_id=dev, device_id_type=MESH)
                for k, dev in enumerate(to)]
            recvs = [pltpu.make_async_remote_copy(
                src_ref=src, dst_ref=out.at[4 * px + 2 * py + pc], send_sem=send_sems.at[per * a + k],
                recv_sem=recv_sems.at[per * a + k], device_id=(px, py, pc), device_id_type=MESH)
                for k, (px, py, pc) in enumerate(to)]
            starts += sends
            waits += [s.wait_send for s in sends] + [r.wait_recv for r in recvs]
        return starts, waits

    shapes = tuple(jax.ShapeDtypeStruct(b.shape, b.dtype) for b in bufs)
    return _Comm(tuple(bufs), shapes, per * len(bufs), make, in_place=len(bufs))


def _slots(x, y, c):
    return 4 * x + 2 * y + c, 4 * (1 - x) + 2 * y + c, 4 * x + 2 * (1 - y) + c, 4 * (1 - x) + 2 * (1 - y) + c


def _gather2d_first(halves):
    per = 2

    def make(ins, outs, send_sems, recv_sems):
        x, y, c, _ = _mesh_place()
        me, xn, yn, _ = _slots(x, y, c)
        starts, waits = [], []
        for a, (src, out) in enumerate(zip(ins, outs)):
            blk = src.at[c]
            rows = blk.shape[0] // 2
            upper, lower = pl.ds(0, rows), pl.ds(rows, rows)

            def copy(k, src_ref, dst_ref, dev, a=a):
                return pltpu.make_async_remote_copy(
                    src_ref=src_ref, dst_ref=dst_ref, send_sem=send_sems.at[per * a + k],
                    recv_sem=recv_sems.at[per * a + k], device_id=dev, device_id_type=MESH)

            sends = [copy(0, blk.at[upper], out.at[me, upper], (1 - x, y, c)),
                     copy(1, blk.at[lower], out.at[me, lower], (x, 1 - y, c))]
            recvs = [copy(0, blk.at[upper], out.at[xn, upper], (1 - x, y, c)),
                     copy(1, blk.at[lower], out.at[yn, lower], (x, 1 - y, c))]
            starts += sends
            waits += [s.wait_send for s in sends] + [r.wait_recv for r in recvs]
        return starts, waits

    shapes = tuple(jax.ShapeDtypeStruct((N_DEV,) + h.shape[1:], h.dtype) for h in halves)
    return _Comm(tuple(halves), shapes, per * len(halves), make)


def _gather2d_second(bufs, halves):
    per = 4
    n_arr = len(bufs)

    def make(ins, outs, send_sems, recv_sems):
        x, y, c, _ = _mesh_place()
        me, xn, yn, dg = _slots(x, y, c)
        starts, waits = [], []
        for a, buf in enumerate(outs):
            own = ins[n_arr + a].at[c]
            rows = buf.shape[1] // 2
            upper, lower = pl.ds(0, rows), pl.ds(rows, rows)
            plan = [(own.at[upper], me, upper, (x, 1 - y, c), yn), (buf.at[xn, upper], xn, upper, (x, 1 - y, c), dg),
                    (own.at[lower], me, lower, (1 - x, y, c), xn), (buf.at[yn, lower], yn, lower, (1 - x, y, c), dg)]
            for k, (src, slot, part, dev, landing) in enumerate(plan):
                sems = dict(send_sem=send_sems.at[per * a + k], recv_sem=recv_sems.at[per * a + k],
                            device_id=dev, device_id_type=MESH)
                send = pltpu.make_async_remote_copy(src_ref=src, dst_ref=buf.at[slot, part], **sems)
                arrival = pltpu.make_async_remote_copy(src_ref=src, dst_ref=buf.at[landing, part], **sems)
                starts.append(send)
                waits += [send.wait_send, arrival.wait_recv]
        return starts, waits

    shapes = tuple(jax.ShapeDtypeStruct(b.shape, b.dtype) for b in bufs)
    return _Comm(tuple(bufs) + tuple(halves), shapes, per * n_arr, make, in_place=n_arr)


def _gather_forward(bufs, name):
    n_arr = len(bufs)

    def body(*refs):
        outs = refs[n_arr:2 * n_arr]
        send_sems, recv_sems = refs[2 * n_arr:]
        x, y, c, chips = _mesh_place()
        sends, recvs = [], []
        for a, buf in enumerate(outs):
            for j, (px, py) in enumerate(chips):
                mine, theirs = buf.at[4 * px + 2 * py + c], buf.at[4 * px + 2 * py + 1 - c]
                sems = dict(send_sem=send_sems.at[3 * a + j], recv_sem=recv_sems.at[3 * a + j],
                            device_id=(x, y, 1 - c), device_id_type=MESH)
                sends.append(pltpu.make_async_remote_copy(src_ref=mine, dst_ref=mine, **sems))
                recvs.append(pltpu.make_async_remote_copy(src_ref=mine, dst_ref=theirs, **sems))
        for cp in sends:
            cp.start()
        for s, r in zip(sends, recvs):
            s.wait_send()
            r.wait_recv()

    return pl.pallas_call(
        body, name=name, out_shape=[jax.ShapeDtypeStruct(b.shape, b.dtype) for b in bufs],
        in_specs=[_any()] * n_arr, out_specs=[_any()] * n_arr,
        input_output_aliases={a: a for a in range(n_arr)},
        scratch_shapes=[pltpu.SemaphoreType.DMA((3 * n_arr,)), pltpu.SemaphoreType.DMA((3 * n_arr,))],
    )(*bufs)


def _scatter_job(chip_sums):
    def make(ins, outs, send_sems, recv_sems):
        x, y, c, chips = _mesh_place()
        copies = [pltpu.make_async_remote_copy(
            src_ref=src.at[2 * px + py], dst_ref=out.at[j], send_sem=send_sems.at[3 * a + j],
            recv_sem=recv_sems.at[3 * a + j], device_id=(px, py, c), device_id_type=MESH)
            for a, (src, out) in enumerate(zip(ins, outs)) for j, (px, py) in enumerate(chips)]
        return copies, [cp.wait for cp in copies]

    return _Comm(tuple(chip_sums), tuple(jax.ShapeDtypeStruct((3,) + s.shape[1:], s.dtype) for s in chip_sums),
                 3 * len(chip_sums), make)


def _all_gather8(blocks, name, split=False, forward=(), riders=()):
    n_arr, n_fwd = len(blocks), len(forward)
    splits = list(split) if isinstance(split, (list, tuple)) else [split] * n_arr
    rider_in = sum(len(r.operands) for r in riders)
    rider_out = sum(len(r.out_shapes) for r in riders)

    def body(*refs):
        x_refs, refs = refs[:n_arr], refs[n_arr + n_fwd:]
        r_ins, refs = refs[:rider_in], refs[rider_in:]
        out_refs, refs = refs[:n_arr], refs[n_arr:]
        fwd_refs, refs = refs[:n_fwd], refs[n_fwd:]
        r_outs, refs = refs[:rider_out], refs[rider_out:]
        (send_sems, recv_sems, local_sems), rider_sems = refs[:3], refs[3:]
        x, y, c, chips = _mesh_place()
        me, sibling = (x, y, c), (x, y, 1 - c)
        rider_waits, i0, o0 = [], 0, 0
        for n, job in enumerate(riders):
            k_in, k_out = len(job.operands), len(job.out_shapes)
            starts, waits = job.make(r_ins[i0:i0 + k_in], r_outs[o0:o0 + k_out],
                                     rider_sems[2 * n], rider_sems[2 * n + 1])
            for cp in starts:
                cp.start()
            rider_waits += waits
            i0, o0 = i0 + k_in, o0 + k_out
        passing = []
        for f, buf in enumerate(fwd_refs):
            for j, (px, py) in enumerate(chips):
                mine, theirs = buf.at[4 * px + 2 * py + c], buf.at[4 * px + 2 * py + 1 - c]
                sems = dict(send_sem=send_sems.at[7 * n_arr + 3 * f + j], recv_sem=recv_sems.at[7 * n_arr + 3 * f + j],
                            device_id=sibling, device_id_type=MESH)
                passing.append((pltpu.make_async_remote_copy(src_ref=mine, dst_ref=mine, **sems),
                                pltpu.make_async_remote_copy(src_ref=mine, dst_ref=theirs, **sems)))
        for send, _ in passing:
            send.start()
        arrays = []
        for a, (x_ref, out_ref) in enumerate(zip(x_refs, out_refs)):
            src_mine = x_ref.at[c] if splits[a] else x_ref

            def copy(k, blk, to, src=None, a=a, out_ref=out_ref):
                dst = out_ref.at[4 * blk[0] + 2 * blk[1] + blk[2]]
                return pltpu.make_async_remote_copy(
                    src_ref=dst if src is None else src, dst_ref=dst,
                    send_sem=send_sems.at[7 * a + k], recv_sem=recv_sems.at[7 * a + k],
                    device_id=to, device_id_type=MESH)

            mine = pltpu.make_async_copy(src_mine, out_ref.at[4 * x + 2 * y + c], local_sems.at[a])
            mine.start()
            first = [copy(0, me, sibling, src=src_mine)]
            first += [copy(1 + j, me, (*chip, c), src=src_mine) for j, chip in enumerate(chips)]
            for cp in first:
                cp.start()
            arrays.append((copy, mine, first))
        sent = []
        for copy, mine, first in arrays:
            passed = [copy(4 + j, (*chip, c), sibling) for j, chip in enumerate(chips)]
            for j, chip in enumerate(chips):
                copy(1 + j, (*chip, c), me).wait_recv()
                passed[j].start()
            sent += first + passed
        for copy, mine, first in arrays:
            copy(0, sibling, me).wait_recv()
            for j, chip in enumerate(chips):
                copy(4 + j, (*chip, 1 - c), me).wait_recv()
            mine.wait()
        for cp in sent:
            cp.wait_send()
        for send, arrival in passing:
            send.wait_send()
            arrival.wait_recv()
        for wait in rider_waits:
            wait()

    n_sems = 7 * n_arr + 3 * n_fwd
    rider_operands = [a for r in riders for a in r.operands]
    rider_shapes = [s for r in riders for s in r.out_shapes]
    return pl.pallas_call(
        body, name=name,
        out_shape=[jax.ShapeDtypeStruct((N_DEV,) + tuple(b.shape[1:] if s else b.shape), b.dtype)
                   for b, s in zip(blocks, splits)]
        + [jax.ShapeDtypeStruct(f.shape, f.dtype) for f in forward] + rider_shapes,
        in_specs=[_any()] * (n_arr + n_fwd + rider_in), out_specs=[_any()] * (n_arr + n_fwd + rider_out),
        input_output_aliases={n_arr + f: n_arr + f for f in range(n_fwd)},
        scratch_shapes=[pltpu.SemaphoreType.DMA((n_sems,)), pltpu.SemaphoreType.DMA((n_sems,)),
                        pltpu.SemaphoreType.DMA((n_arr,))]
        + [pltpu.SemaphoreType.DMA((r.n_sems,)) for r in riders for _ in range(2)],
    )(*blocks, *forward, *rider_operands)


def _share_job(bufs):
    def make(ins, outs, send_sems, recv_sems):
        del ins
        x, y, c, _ = _mesh_place()
        sems = lambda a: dict(send_sem=send_sems.at[a], recv_sem=recv_sems.at[a],
                              device_id=(x, y, 1 - c), device_id_type=MESH)
        sends = [pltpu.make_async_remote_copy(src_ref=o.at[c], dst_ref=o.at[c], **sems(a)) for a, o in enumerate(outs)]
        arrivals = [pltpu.make_async_remote_copy(src_ref=o.at[c], dst_ref=o.at[1 - c], **sems(a))
                    for a, o in enumerate(outs)]
        return sends, [s.wait_send for s in sends] + [r.wait_recv for r in arrivals]

    shapes = tuple(jax.ShapeDtypeStruct(b.shape, b.dtype) for b in bufs)
    return _Comm(tuple(bufs), shapes, len(bufs), make, in_place=len(bufs))


def _sibling_share(bufs, name):
    n_arr = len(bufs)

    def body(*refs):
        out_refs = refs[n_arr:2 * n_arr]
        send_sems, recv_sems = refs[2 * n_arr:]
        x, y, c = lax.axis_index("x"), lax.axis_index("y"), lax.axis_index("c")
        copies = [pltpu.make_async_remote_copy(
            src_ref=out_refs[a].at[c], dst_ref=out_refs[a].at[c],
            send_sem=send_sems.at[a], recv_sem=recv_sems.at[a],
            device_id=(x, y, 1 - c), device_id_type=MESH) for a in range(n_arr)]
        for cp in copies:
            cp.start()
        for a in range(n_arr):
            pltpu.make_async_remote_copy(
                src_ref=out_refs[a].at[c], dst_ref=out_refs[a].at[1 - c],
                send_sem=send_sems.at[a], recv_sem=recv_sems.at[a],
                device_id=(x, y, 1 - c), device_id_type=MESH).wait()

    return pl.pallas_call(
        body, name=name,
        out_shape=[jax.ShapeDtypeStruct(b.shape, b.dtype) for b in bufs],
        in_specs=[_any()] * n_arr, out_specs=[_any()] * n_arr,
        input_output_aliases={a: a for a in range(n_arr)},
        scratch_shapes=[pltpu.SemaphoreType.DMA((n_arr,)), pltpu.SemaphoreType.DMA((n_arr,))],
    )(*bufs)


def _gelu_tanh(z):
    k = math.sqrt(2.0 / math.pi)
    t = jnp.tanh(k * (z + 0.044715 * (z * z * z)))
    return 0.5 * z * (1.0 + t), t


def _gelu_tanh_grad(z, t):
    k = math.sqrt(2.0 / math.pi)
    return 0.5 * (1.0 + t) + 0.5 * z * (1.0 - t * t) * (k * (1.0 + 3.0 * 0.044715 * (z * z)))


def _rope_angle_kernel(pos_row, invf_col):
    seq = pos_row.shape[1]

    def body(p_ref, f_ref, cos_ref, sin_ref):
        ang = p_ref[...].astype(F32) * f_ref[...]
        cos_ref[...] = jnp.cos(ang)
        sin_ref[...] = jnp.sin(ang)

    return pl.pallas_call(
        body, name="rope_angles", grid=(1,), out_shape=[jax.ShapeDtypeStruct((ROT_DIM // 2, seq), F32)] * 2,
        in_specs=[_full((1, seq)), _full((ROT_DIM // 2, 1))], out_specs=[_full((ROT_DIM // 2, seq))] * 2,
        compiler_params=_params("arbitrary"),
    )(pos_row, invf_col)


def _rope_lane_tables(cos, sin):
    cos_t, sin_t = cos.T, sin.T
    seq, half = cos_t.shape
    ones = jnp.ones((seq, HEAD_DIM - ROT_DIM), F32)
    c64 = jnp.concatenate([cos_t, cos_t, ones], axis=1)
    s1 = jnp.concatenate([sin_t, jnp.zeros((seq, HEAD_DIM - half), F32)], axis=1)
    s2 = jnp.concatenate([jnp.zeros((seq, half), F32), sin_t, jnp.zeros((seq, HEAD_DIM - ROT_DIM), F32)], axis=1)
    return jnp.concatenate([jnp.tile(t, (1, LANES // HEAD_DIM)) for t in (c64, s1, s2)], axis=1)


def _rope_apply(t, tab, sign):
    reps = t.shape[1] // LANES
    c_tab, s1, s2 = (jnp.tile(tab[:, LANES * k:LANES * (k + 1)], (1, reps)) if reps > 1
                     else tab[:, LANES * k:LANES * (k + 1)] for k in range(3))
    half = ROT_DIM // 2
    up = pltpu.roll(t, t.shape[1] - half, 1)
    down = pltpu.roll(t, half, 1)
    return t * c_tab + sign * (down * s2 - up * s1)


def _lane_masks(shape):
    lane = lax.broadcasted_iota(jnp.int32, shape, 1)
    return lane < HEAD_DIM, lane >= HEAD_DIM


HEADS_PER_GROUP = N_Q_HEADS // N_KV_HEADS
ATTN_SCALE = 1.0 / math.sqrt(HEAD_DIM)


def _attn_bias_t(first_block):
    kj = lax.broadcasted_iota(jnp.int32, (2 * CHUNK, CHUNK), 0)
    qi = lax.broadcasted_iota(jnp.int32, (2 * CHUNK, CHUNK), 1)
    ok = (kj > qi) & (kj <= qi + CHUNK)
    if first_block is not None:
        ok = ok & (jnp.logical_not(first_block) | (kj >= CHUNK))
    return jnp.tile(jnp.where(ok, 0.0, -jnp.inf), (1, HEADS_PER_GROUP))


def _group_rows(x, g, lo, hi):
    rows = []
    for r in range(HEADS_PER_GROUP):
        h = HEADS_PER_GROUP * g + r
        pair = x[:, LANES * (h // 2):LANES * (h // 2 + 1)]
        rows.append(jnp.where(hi if h % 2 else lo, pair, 0.0))
    return jnp.concatenate(rows, axis=0)


def _pairs_from_rows(rows, lo):
    return [jnp.where(lo, rows[2 * CHUNK * k:2 * CHUNK * k + CHUNK], rows[2 * CHUNK * k + CHUNK:2 * CHUNK * (k + 1)])
            for k in range(HEADS_PER_GROUP // 2)]


def _group_dup(a, b, g, lo2):
    return jnp.where(lo2, a, b) if g == 0 else jnp.where(lo2, b, a)


def _sink_row(sink_ref, g):
    return jnp.concatenate([sink_ref[HEADS_PER_GROUP * g + r:HEADS_PER_GROUP * g + r + 1, :]
                            for r in range(HEADS_PER_GROUP)], axis=1)


def _attn_probs_t(k_dup, q_rows, bias_t, sink_row):
    s_t = _dot_nt(k_dup, q_rows) * ATTN_SCALE + bias_t
    m = jnp.maximum(jnp.max(s_t, axis=0, keepdims=True), sink_row)
    p = jnp.exp(s_t - m)
    e_sink = jnp.exp(sink_row - m)
    inv = 1.0 / (jnp.sum(p, axis=0, keepdims=True) + e_sink)
    return p * inv, e_sink * inv


def _sgu_forward_pair(wm, vp, j):
    lo, hi = _lane_masks(vp.shape)
    lhs = jnp.concatenate([wm[2 * j], wm[2 * j + 1]], axis=1)
    rhs = jnp.concatenate([jnp.where(lo, vp, 0.0), jnp.where(hi, vp, 0.0)], axis=0)
    return _dot(lhs, rhs)


def _masked_spatial(w_ref):
    t = lax.broadcasted_iota(jnp.int32, (CHUNK, CHUNK), 0)
    s = lax.broadcasted_iota(jnp.int32, (CHUNK, CHUNK), 1)
    tril = s <= t
    return [jnp.where(tril, w_ref[g], 0.0) for g in range(GMLP_GROUPS)], tril, s >= t


def _mod_kernel(c_all, w_shard, b_shard):
    n = w_shard.shape[1]
    tn = 512

    def body(c_ref, w_ref, b_ref, mod_ref, act_ref):
        cv = c_ref[...]
        act = cv * (1.0 / (1.0 + jnp.exp(-cv)))
        act_ref[...] = act
        mod_ref[...] = _dot(act, w_ref[...]) + b_ref[...]

    return pl.pallas_call(
        body, name="ada_mod", grid=(n // tn,),
        out_shape=[jax.ShapeDtypeStruct((N_DEV, n), F32), jax.ShapeDtypeStruct((N_DEV, D_MODEL), F32)],
        in_specs=[_full((N_DEV, D_MODEL)), pl.BlockSpec((D_MODEL, tn), lambda i: (0, i)),
                  pl.BlockSpec((1, tn), lambda i: (0, i))],
        out_specs=[pl.BlockSpec((N_DEV, tn), lambda i: (0, i)), _full((N_DEV, D_MODEL))],
        compiler_params=_params("arbitrary"),
    )(c_all, w_shard, b_shard)


def _in_proj_kernel(x, vecs, w_in_t, comm=None):
    seq = x.shape[0]
    tm = 512

    def body(x_ref, v_ref, w_ref, proj_ref, h_ref):
        xv = x_ref[...]
        rstd = lax.rsqrt(_mean_last(xv * xv) + EPS)
        n1 = (xv * rstd) * v_ref[0:1, :]
        h = n1 * (1.0 + v_ref[2:3, :]) + v_ref[1:2, :]
        hb = h.astype(MXU_DTYPE)
        h_ref[...] = hb
        proj_ref[...] = _dot_nt(hb, w_ref[...])

    return _hosted_call(
        body, comm, name="in_proj", grid=(seq // tm,),
        out_shape=[jax.ShapeDtypeStruct((seq, IN_PROJ_WIDTH), F32),
                   jax.ShapeDtypeStruct((seq, D_MODEL), MXU_DTYPE)],
        in_specs=[pl.BlockSpec((tm, D_MODEL), lambda i: (i, 0)), _full((8, D_MODEL)),
                  _full((IN_PROJ_WIDTH, D_MODEL))],
        out_specs=[pl.BlockSpec((tm, IN_PROJ_WIDTH), lambda i: (i, 0)),
                   pl.BlockSpec((tm, D_MODEL), lambda i: (i, 0))],
        semantics=("arbitrary",),
    )(x, vecs, w_in_t)


MIXER_BLOCKS_PER_STEP = 2
KV_START = 2 * GMLP_WIDTH + ATTN_WIDTH


def _mixer_fwd_kernel(proj, rope_tab, w_spatial, bias_full, sink_rows, comm=None):
    seq = proj.shape[0]
    per = MIXER_BLOCKS_PER_STEP
    steps = seq // (CHUNK * per)
    kv_col = KV_START // (2 * KV_WIDTH)

    def body(proj_ref, prev_ref, tab_ref, ptab_ref, w_ref, bias_ref, sink_ref, cat_ref):
        i = pl.program_id(0)
        wm, _, _ = _masked_spatial(w_ref)
        lo, hi = _lane_masks((CHUNK, LANES))
        lo2, _ = _lane_masks((2 * CHUNK, LANES))
        o = 2 * GMLP_WIDTH
        for s in range(per):
            rows, before = slice(CHUNK * s, CHUNK * (s + 1)), slice(CHUNK * (s - 1), CHUNK * s)
            for j in range(GMLP_GROUPS // 2):
                cols = slice(LANES * j, LANES * (j + 1))
                vcols = slice(GMLP_WIDTH + LANES * j, GMLP_WIDTH + LANES * (j + 1))
                u, _ = _gelu_tanh(proj_ref[rows, cols])
                vp, _ = _gelu_tanh(proj_ref[rows, vcols])
                sv = _sgu_forward_pair(wm, vp, j) + bias_ref[:, cols]
                cat_ref[rows, cols] = (u * sv).astype(cat_ref.dtype)
            tab = tab_ref[rows, :]
            if s == 0:
                prev_kv, prev_tab, first = prev_ref[...], ptab_ref[...], i == 0
            else:
                prev_kv, prev_tab, first = proj_ref[before, KV_START:KV_START + 2 * KV_WIDTH], tab_ref[before, :], None
            q_r = _rope_apply(proj_ref[rows, o:o + ATTN_WIDTH], tab, 1.0)
            k_cur = _rope_apply(proj_ref[rows, KV_START:KV_START + KV_WIDTH], tab, 1.0)
            k_prev = _rope_apply(prev_kv[:, 0:KV_WIDTH], prev_tab, 1.0)
            k_a = jnp.concatenate([k_prev, k_cur], axis=0)
            v_a = jnp.concatenate([prev_kv[:, KV_WIDTH:2 * KV_WIDTH],
                                   proj_ref[rows, KV_START + KV_WIDTH:KV_START + 2 * KV_WIDTH]], axis=0)
            k_b = pltpu.roll(k_a, HEAD_DIM, 1)
            v_b = pltpu.roll(v_a, HEAD_DIM, 1)
            bias_t = _attn_bias_t(first)
            for g in range(N_KV_HEADS):
                p_t, _ = _attn_probs_t(_group_dup(k_a, k_b, g, lo2), _group_rows(q_r, g, lo, hi), bias_t,
                                       _sink_row(sink_ref, g))
                o_t = _dot(_group_dup(v_a, v_b, g, lo2).T, p_t)
                for k, pair in enumerate(_pairs_from_rows(o_t.T, lo)):
                    c0 = GMLP_WIDTH + LANES * (2 * g + k)
                    cat_ref[rows, c0:c0 + LANES] = pair.astype(cat_ref.dtype)

    return _hosted_call(
        body, comm, name="mixer_fwd", grid=(steps,),
        out_shape=[jax.ShapeDtypeStruct((seq, D_MODEL), MXU_DTYPE)],
        in_specs=[pl.BlockSpec((CHUNK * per, IN_PROJ_WIDTH), lambda i: (i, 0)),
                  pl.BlockSpec((CHUNK, 2 * KV_WIDTH), lambda i: (jnp.maximum(per * i - 1, 0), kv_col)),
                  pl.BlockSpec((CHUNK * per, 3 * LANES), lambda i: (i, 0)),
                  pl.BlockSpec((CHUNK, 3 * LANES), lambda i: (jnp.maximum(per * i - 1, 0), 0)),
                  _full((GMLP_GROUPS, CHUNK, CHUNK)), _full((CHUNK, GMLP_WIDTH)),
                  _full((N_Q_HEADS, LANES))],
        out_specs=[pl.BlockSpec((CHUNK * per, D_MODEL), lambda i: (i, 0))],
        semantics=("arbitrary",),
    )(proj, proj, rope_tab, rope_tab, w_spatial, bias_full, sink_rows)


def _trunk_kernel(x, target, cat, vecs, chip_idx, gathered, local):
    seq = x.shape[0]
    tm = 256
    nj = D_FF // D_MODEL
    out_rows = D_MODEL // N_CHIPS

    def body(chip_ref, x_ref, t_ref, cat_ref, v_ref, g_out, g_w1, g_w2, l_out, l_w1, l_w2,
             dx1_ref, dcat_ref, dmix_ref, h2_ref, r_ref, da_ref, dff_ref, sums_ref,
             wout, w1, w2, a_scr, sem):
        i = pl.program_id(0)

        @pl.when(i == 0)
        def _():
            dsts = ([wout.at[pl.ds(out_rows * k, out_rows)] for k in range(N_CHIPS)]
                    + [w1.at[k] for k in range(N_CHIPS)] + [w2.at[k] for k in range(N_CHIPS)])
            srcs = [(g, l) for g, l in ((g_out, l_out), (g_w1, l_w1), (g_w2, l_w2)) for _ in range(N_CHIPS)]
            for n, (dst, (g, l)) in enumerate(zip(dsts, srcs)):
                k = n % N_CHIPS

                @pl.when(chip_ref[0] == k)
                def _():
                    pltpu.make_async_copy(l, dst, sem.at[n]).start()

                @pl.when(chip_ref[0] != k)
                def _():
                    pltpu.make_async_copy(g.at[k], dst, sem.at[n]).start()
            for n, (dst, (g, l)) in enumerate(zip(dsts, srcs)):
                pltpu.make_async_copy(l, dst, sem.at[n]).wait()
            sums_ref[...] = jnp.zeros_like(sums_ref)

        gate1, shift2, scale2 = v_ref[0:1, :], v_ref[1:2, :], v_ref[2:3, :]
        gate2, g_ffn, g_final = v_ref[3:4, :], v_ref[4:5, :], v_ref[5:6, :]

        mix = _dot(cat_ref[...], wout[...])
        x1 = x_ref[...] + gate1 * mix
        rstd2 = lax.rsqrt(_mean_last(x1 * x1) + EPS)
        xh2 = x1 * rstd2
        n2 = xh2 * g_ffn
        h2b = (n2 * (1.0 + scale2) + shift2).astype(MXU_DTYPE)
        h2_ref[...] = h2b
        ff = jnp.zeros((tm, D_MODEL), F32)
        for j in range(nj):
            a = _dot(h2b, w1[j])
            a_scr[j] = a
            relu = jnp.maximum(a, 0.0)
            rb = (relu * relu).astype(MXU_DTYPE)
            r_ref[:, D_MODEL * j:D_MODEL * (j + 1)] = rb
            ff = ff + _dot(rb, w2[j])
        x2 = x1 + gate2 * ff
        rstd3 = lax.rsqrt(_mean_last(x2 * x2) + EPS)
        xh3 = x2 * rstd3
        err = xh3 * g_final - t_ref[...]
        loss = 0.5 * _rowsum(_mean_last(err * err))
        dy = err * (1.0 / D_MODEL)
        dxh3 = dy * g_final
        dx2 = rstd3 * (dxh3 - xh3 * _mean_last(dxh3 * xh3))
        dffb = (dx2 * gate2).astype(MXU_DTYPE)
        dff_ref[...] = dffb
        dh2 = jnp.zeros((tm, D_MODEL), F32)
        for j in range(nj):
            dr = _dot_nt(dffb, w2[j])
            dab = (dr * (2.0 * jnp.maximum(a_scr[j], 0.0))).astype(MXU_DTYPE)
            da_ref[:, D_MODEL * j:D_MODEL * (j + 1)] = dab
            dh2 = dh2 + _dot_nt(dab, w1[j])
        dn2 = dh2 * (1.0 + scale2)
        dxh2 = dn2 * g_ffn
        dx1 = dx2 + rstd2 * (dxh2 - xh2 * _mean_last(dxh2 * xh2))
        dx1_ref[...] = dx1
        dmixb = (dx1 * gate1).astype(MXU_DTYPE)
        dmix_ref[...] = dmixb
        dcat_ref[...] = _dot_nt(dmixb, wout[...])

        sums_ref[0:1, :] += _rowsum(dh2)
        sums_ref[1:2, :] += _rowsum(dh2 * n2)
        sums_ref[2:3, :] += _rowsum(dx2 * ff)
        sums_ref[3:4, :] += _rowsum(dn2 * xh2)
        sums_ref[4:5, :] += _rowsum(dy * xh3)
        sums_ref[5:6, :] += _rowsum(dx1 * mix)
        sums_ref[6:7, :] += jnp.broadcast_to(loss, (1, D_MODEL))

    tok = lambda w: pl.BlockSpec((tm, w), lambda i, chip: (i, 0))
    return _hosted_call(
        body, None, name="trunk", grid=(seq // tm,), n_prefetch=1,
        out_shape=[jax.ShapeDtypeStruct((seq, D_MODEL), F32), jax.ShapeDtypeStruct((seq, D_MODEL), F32),
                   jax.ShapeDtypeStruct((seq, D_MODEL), MXU_DTYPE), jax.ShapeDtypeStruct((seq, D_MODEL), MXU_DTYPE),
                   jax.ShapeDtypeStruct((seq, D_FF), MXU_DTYPE), jax.ShapeDtypeStruct((seq, D_FF), MXU_DTYPE),
                   jax.ShapeDtypeStruct((seq, D_MODEL), MXU_DTYPE), jax.ShapeDtypeStruct((8, D_MODEL), F32)],
        in_specs=[tok(D_MODEL), tok(D_MODEL), tok(D_MODEL), _full((8, D_MODEL))] + [_any()] * 6,
        out_specs=[tok(D_MODEL), tok(D_MODEL), tok(D_MODEL), tok(D_MODEL), tok(D_FF), tok(D_FF), tok(D_MODEL),
                   _full((8, D_MODEL))],
        scratch_shapes=[pltpu.VMEM((D_MODEL, D_MODEL), MXU_DTYPE), pltpu.VMEM((nj, D_MODEL, D_MODEL), MXU_DTYPE),
                        pltpu.VMEM((nj, D_MODEL, D_MODEL), MXU_DTYPE), pltpu.VMEM((nj, tm, D_MODEL), F32),
                        pltpu.SemaphoreType.DMA((3 * N_CHIPS,))],
        semantics=("arbitrary",),
    )(chip_idx, x, target, cat, vecs, *gathered, *local)


def _mixer_bwd_kernel(proj, rope_tab, dcat, w_spatial, w_spatial_t, bias_full, sink_rows, dev_idx, comm=None):
    seq = proj.shape[0]
    per = MIXER_BLOCKS_PER_STEP
    steps = seq // (CHUNK * per)
    kv_col = KV_START // (2 * KV_WIDTH)

    def body(dev_ref, proj_ref, prev_ref, tab_ref, ptab_ref, dcat_ref, w_ref, wt_ref, bias_ref, sink_ref,
             dproj_ref, dw_ref, db_ref, dsink_ref, carry):
        del dev_ref
        step = pl.program_id(0)

        @pl.when(step == 0)
        def _():
            carry[...] = jnp.zeros_like(carry)
            dw_ref[...] = jnp.zeros_like(dw_ref)
            db_ref[...] = jnp.zeros_like(db_ref)
            dsink_ref[...] = jnp.zeros_like(dsink_ref)

        for s in reversed(range(per)):
            rows = pl.ds(CHUNK * s, CHUNK)
            if s == 0:
                before, before_tab, first = prev_ref, ptab_ref, step == steps - 1
            else:
                before = proj_ref.at[pl.ds(CHUNK * (s - 1), CHUNK), pl.ds(KV_START, 2 * KV_WIDTH)]
                before_tab, first = tab_ref.at[pl.ds(CHUNK * (s - 1), CHUNK)], None
            one_block(proj_ref.at[rows], before, tab_ref.at[rows], before_tab, dcat_ref.at[rows], w_ref, wt_ref,
                      bias_ref, sink_ref, dproj_ref.at[rows], dw_ref, db_ref, dsink_ref, carry, first)

    def one_block(proj_ref, prev_ref, tab_ref, ptab_ref, dcat_ref, w_ref, wt_ref, bias_ref, sink_ref,
                  dproj_ref, dw_ref, db_ref, dsink_ref, carry, first):
        wm, tril, triu = _masked_spatial(w_ref)
        lo, hi = _lane_masks((CHUNK, LANES))
        lane = lax.broadcasted_iota(jnp.int32, (CHUNK, LANES), 1)
        db = jnp.zeros((CHUNK, LANES), F32)
        for j in range(GMLP_GROUPS // 2):
            cols = slice(LANES * j, LANES * (j + 1))
            vcols = slice(GMLP_WIDTH + LANES * j, GMLP_WIDTH + LANES * (j + 1))
            zu, zv = proj_ref[:, cols], proj_ref[:, vcols]
            u, tu = _gelu_tanh(zu)
            vp, tv = _gelu_tanh(zv)
            sv = _sgu_forward_pair(wm, vp, j) + bias_ref[:, cols]
            dout = dcat_ref[:, cols]
            du = dout * sv
            dsv = dout * u
            dsv_lo, dsv_hi = jnp.where(lo, dsv, 0.0), jnp.where(hi, dsv, 0.0)
            lhs_t = jnp.concatenate([jnp.where(triu, wt_ref[2 * j], 0.0),
                                     jnp.where(triu, wt_ref[2 * j + 1], 0.0)], axis=1)
            dv = _dot(lhs_t, jnp.concatenate([dsv_lo, dsv_hi], axis=0))
            dw_ref[2 * j] += jnp.where(tril, _dot_nt(dsv_lo, vp), 0.0)
            dw_ref[2 * j + 1] += jnp.where(tril, _dot_nt(dsv_hi, vp), 0.0)
            db = db + (jnp.where(lane == 2 * j, jnp.sum(dsv_lo, axis=1, keepdims=True), 0.0)
                       + jnp.where(lane == 2 * j + 1, jnp.sum(dsv_hi, axis=1, keepdims=True), 0.0))
            dproj_ref[:, cols] = (du * _gelu_tanh_grad(zu, tu)).astype(dproj_ref.dtype)
            dproj_ref[:, vcols] = (dv * _gelu_tanh_grad(zv, tv)).astype(dproj_ref.dtype)
        db_ref[...] += db
        o = 2 * GMLP_WIDTH
        tab = tab_ref[...]
        q_r = _rope_apply(proj_ref[:, o:o + ATTN_WIDTH], tab, 1.0)
        k_cur = _rope_apply(proj_ref[:, o + ATTN_WIDTH:o + ATTN_WIDTH + KV_WIDTH], tab, 1.0)
        k_prev = _rope_apply(prev_ref[:, 0:KV_WIDTH], ptab_ref[...], 1.0)
        k_a = jnp.concatenate([k_prev, k_cur], axis=0)
        v_a = jnp.concatenate([prev_ref[:, KV_WIDTH:2 * KV_WIDTH],
                               proj_ref[:, o + ATTN_WIDTH + KV_WIDTH:o + ATTN_WIDTH + 2 * KV_WIDTH]], axis=0)
        k_b = pltpu.roll(k_a, HEAD_DIM, 1)
        v_b = pltpu.roll(v_a, HEAD_DIM, 1)
        bias_t = _attn_bias_t(first)
        lo2, _ = _lane_masks((2 * CHUNK, LANES))
        dout_b = dcat_ref[:, GMLP_WIDTH:GMLP_WIDTH + ATTN_WIDTH]
        dk_tot, dv_tot, dq_pairs = [], [], []
        for g in range(N_KV_HEADS):
            k_dup, v_dup = _group_dup(k_a, k_b, g, lo2), _group_dup(v_a, v_b, g, lo2)
            q_rows = _group_rows(q_r, g, lo, hi)
            do_rows = _group_rows(dout_b, g, lo, hi)
            p_t, p_sink = _attn_probs_t(k_dup, q_rows, bias_t, _sink_row(sink_ref, g))
            dp_t = _dot_nt(v_dup, do_rows)
            delta = jnp.sum(p_t * dp_t, axis=0, keepdims=True)
            ds_t = p_t * (dp_t - delta) * ATTN_SCALE
            dsink = -p_sink * delta
            for r in range(HEADS_PER_GROUP):
                h = HEADS_PER_GROUP * g + r
                dsink_ref[h:h + 1, :] += jnp.broadcast_to(
                    jnp.sum(dsink[:, LANES * r:LANES * (r + 1)], axis=1, keepdims=True), (1, LANES))
            dk_full = _dot(ds_t, q_rows)
            dv_full = _dot(p_t, do_rows)
            dk_tot.append(dk_full + pltpu.roll(dk_full, HEAD_DIM, 1))
            dv_tot.append(dv_full + pltpu.roll(dv_full, HEAD_DIM, 1))
            dq_t = _dot(k_dup.T, ds_t)
            dq_pairs += _pairs_from_rows(dq_t.T, lo)
        dk_all = jnp.where(lo2, dk_tot[0], dk_tot[1])
        dv_all = jnp.where(lo2, dv_tot[0], dv_tot[1])
        dk_cur = dk_all[CHUNK:, :] + carry[:, 0:KV_WIDTH]
        dv_cur = dv_all[CHUNK:, :] + carry[:, KV_WIDTH:2 * KV_WIDTH]
        carry[:, 0:KV_WIDTH] = dk_all[:CHUNK, :]
        carry[:, KV_WIDTH:2 * KV_WIDTH] = dv_all[:CHUNK, :]
        dq = _rope_apply(jnp.concatenate(dq_pairs, axis=1), tab, -1.0)
        dproj_ref[:, o:o + ATTN_WIDTH] = dq.astype(dproj_ref.dtype)
        dproj_ref[:, o + ATTN_WIDTH:o + ATTN_WIDTH + KV_WIDTH] = (
            _rope_apply(dk_cur, tab, -1.0).astype(dproj_ref.dtype))
        dproj_ref[:, o + ATTN_WIDTH + KV_WIDTH:o + ATTN_WIDTH + 2 * KV_WIDTH] = dv_cur.astype(dproj_ref.dtype)

    rev = lambda i: steps - 1 - i
    before = lambda i: jnp.maximum(per * rev(i) - 1, 0)
    slot = lambda shape: pl.BlockSpec((None,) + shape, lambda i, d: (d[0],) + (0,) * len(shape))
    return _hosted_call(
        body, comm, name="mixer_bwd", grid=(steps,), n_prefetch=1,
        out_shape=[jax.ShapeDtypeStruct((seq, IN_PROJ_WIDTH), MXU_DTYPE),
                   jax.ShapeDtypeStruct((N_DEV, GMLP_GROUPS, CHUNK, CHUNK), F32),
                   jax.ShapeDtypeStruct((N_DEV, CHUNK, LANES), F32),
                   jax.ShapeDtypeStruct((N_DEV, N_Q_HEADS, LANES), F32)],
        in_specs=[pl.BlockSpec((CHUNK * per, IN_PROJ_WIDTH), lambda i, d: (rev(i), 0)),
                  pl.BlockSpec((CHUNK, 2 * KV_WIDTH), lambda i, d: (before(i), kv_col)),
                  pl.BlockSpec((CHUNK * per, 3 * LANES), lambda i, d: (rev(i), 0)),
                  pl.BlockSpec((CHUNK, 3 * LANES), lambda i, d: (before(i), 0)),
                  pl.BlockSpec((CHUNK * per, D_MODEL), lambda i, d: (rev(i), 0)),
                  _full((GMLP_GROUPS, CHUNK, CHUNK)), _full((GMLP_GROUPS, CHUNK, CHUNK)),
                  _full((CHUNK, GMLP_WIDTH)), _full((N_Q_HEADS, LANES))],
        out_specs=[pl.BlockSpec((CHUNK * per, IN_PROJ_WIDTH), lambda i, d: (rev(i), 0)),
                   slot((GMLP_GROUPS, CHUNK, CHUNK)), slot((CHUNK, LANES)), slot((N_Q_HEADS, LANES))],
        scratch_shapes=[pltpu.VMEM((CHUNK, 2 * KV_WIDTH), F32)],
        semantics=("arbitrary",),
    )(dev_idx, proj, proj, rope_tab, rope_tab, dcat, w_spatial, w_spatial_t, bias_full, sink_rows)


def _in_proj_bwd_kernel(x, dx1, dproj, vecs, w_in_t, comm=None):
    seq = x.shape[0]
    tm = 512

    def body(x_ref, dx1_ref, dp_ref, v_ref, w_ref, gx_ref, sums_ref):
        @pl.when(pl.program_id(0) == 0)
        def _():
            sums_ref[...] = jnp.zeros_like(sums_ref)

        g_mix, scale1 = v_ref[0:1, :], v_ref[2:3, :]
        dh = _dot(dp_ref[...], w_ref[...])
        xv = x_ref[...]
        rstd = lax.rsqrt(_mean_last(xv * xv) + EPS)
        xh = xv * rstd
        dn1 = dh * (1.0 + scale1)
        dxh = dn1 * g_mix
        gx_ref[...] = dx1_ref[...] + rstd * (dxh - xh * _mean_last(dxh * xh))
        sums_ref[0:1, :] += _rowsum(dh)
        sums_ref[1:2, :] += _rowsum(dh * (xh * g_mix))
        sums_ref[2:3, :] += _rowsum(dn1 * xh)

    return _hosted_call(
        body, comm, name="in_proj_bwd", grid=(seq // tm,),
        out_shape=[jax.ShapeDtypeStruct((seq, D_MODEL), F32), jax.ShapeDtypeStruct((8, D_MODEL), F32)],
        in_specs=[pl.BlockSpec((tm, D_MODEL), lambda i: (i, 0)), pl.BlockSpec((tm, D_MODEL), lambda i: (i, 0)),
                  pl.BlockSpec((tm, IN_PROJ_WIDTH), lambda i: (i, 0)), _full((8, D_MODEL)),
                  _full((IN_PROJ_WIDTH, D_MODEL))],
        out_specs=[pl.BlockSpec((tm, D_MODEL), lambda i: (i, 0)), _full((8, D_MODEL))],
        semantics=("arbitrary",),
    )(x, dx1, dproj, vecs, w_in_t)


class _GradTiles(NamedTuple):
    tm: int
    tn: int
    n_tiles: int
    chips_per_tile: int
    a_index: Callable
    b_index: Callable


def _weight_grad_kernel(a, b, c_idx, name, tiles, comm=None):
    seq = a.shape[0]
    tk = min(seq, 4096)
    nk = seq // tk
    tm, tn, n_tiles, per = tiles.tm, tiles.tn, tiles.n_tiles, tiles.chips_per_tile
    rows = tm // per

    def half(phase, c):
        return phase * c[0] + (1 - phase) * (1 - c[0])

    def body(c_ref, a_ref, b_ref, o_ref, acc, stage, landed, send_sems, recv_sems):
        del c_ref
        phase, t, kk = pl.program_id(0), pl.program_id(1), pl.program_id(2)
        x, y, c, _ = _mesh_place()

        def copy(tile):
            return pltpu.make_async_remote_copy(
                src_ref=stage.at[tile], dst_ref=landed.at[tile], send_sem=send_sems.at[tile],
                recv_sem=recv_sems.at[tile], device_id=(x, y, 1 - c), device_id_type=MESH)

        @pl.when(kk == 0)
        def _():
            acc[...] = jnp.zeros_like(acc)

        acc[...] += _dot_tn(a_ref[...], b_ref[...])

        @pl.when((kk == nk - 1) & (phase == 0))
        def _():
            stage[t] = acc[...].astype(stage.dtype)
            copy(t).start()

        @pl.when((kk == nk - 1) & (phase == 1))
        def _():
            copy(t).wait_recv()
            total = acc[...] + landed[t].astype(F32)
            for q in range(per):
                o_ref[q] = total[rows * q:rows * (q + 1)].astype(o_ref.dtype)

        @pl.when((kk == nk - 1) & (phase == 1) & (t == n_tiles - 1))
        def _():
            for tile in range(n_tiles):
                copy(tile).wait_send()

    out = _hosted_call(
        body, comm, name=name, grid=(2, n_tiles, nk), n_prefetch=1,
        out_shape=[jax.ShapeDtypeStruct((n_tiles * per, rows, tn), GRAD_COMM_DTYPE)],
        in_specs=[pl.BlockSpec((tk, tm), lambda p, t, k, c: (k, tiles.a_index(t, half(p, c)))),
                  pl.BlockSpec((tk, tn), lambda p, t, k, c: (k, tiles.b_index(t, half(p, c))))],
        out_specs=[pl.BlockSpec((per, rows, tn), lambda p, t, k, c: (p * t, 0, 0))],
        scratch_shapes=[pltpu.VMEM((tm, tn), F32), pltpu.VMEM((n_tiles, tm, tn), GRAD_COMM_DTYPE),
                        pltpu.VMEM((n_tiles, tm, tn), GRAD_COMM_DTYPE),
                        pltpu.SemaphoreType.DMA((n_tiles,)), pltpu.SemaphoreType.DMA((n_tiles,))],
        semantics=("arbitrary", "arbitrary", "arbitrary"),
    )(c_idx, a, b)
    return out[0] if comm is None else out


def _row_tile(rows, most=256, sublanes=16):
    return max(t for t in range(sublanes, most + 1, sublanes) if rows % t == 0)


def _adam_update(w, g, m, v):
    m_new = ADAM_B1 * m + (1.0 - ADAM_B1) * g
    v_new = ADAM_B2 * v + (1.0 - ADAM_B2) * (g * g)
    m_hat = m_new / (1.0 - ADAM_B1 ** ADAM_STEP)
    v_hat = v_new / (1.0 - ADAM_B2 ** ADAM_STEP)
    delta = -ADAM_LR * (m_hat / (jnp.sqrt(v_hat) + ADAM_EPS) + ADAM_WD * w)
    return delta, m_new, v_new


def _sum_chips_kernel(own, others, place, name):
    _, r, n = own.shape
    tr = _row_tile(r)

    def body(place_ref, own_ref, oth_ref, o_ref):
        del place_ref
        acc = own_ref[...].astype(F32)
        for k in range(N_CHIPS - 1):
            acc = acc + oth_ref[k].astype(F32)
        o_ref[...] = acc

    return pl.pallas_call(
        body, name=name, out_shape=jax.ShapeDtypeStruct((2, r, n), F32),
        grid_spec=pltpu.PrefetchScalarGridSpec(
            num_scalar_prefetch=1, grid=(r // tr,),
            in_specs=[pl.BlockSpec((None, tr, n), lambda i, p: (p[0], i, 0)),
                      pl.BlockSpec((N_CHIPS - 1, tr, n), lambda i, p: (0, i, 0))],
            out_specs=pl.BlockSpec((None, tr, n), lambda i, p: (p[1], i, 0))),
        compiler_params=_params("parallel"),
    )(place, own, others)


def _adam_kernel(w, g, m, v, name):
    r, n = w.shape
    by_columns = g.shape[1] == r
    tr, tn = _row_tile(g.shape[1], most=512), g.shape[2]

    def body(w_ref, g_ref, m_ref, v_ref, g_out, d_ref, mo_ref, vo_ref):
        gv = g_ref[...]
        g_out[...] = gv
        d_ref[...], mo_ref[...], vo_ref[...] = _adam_update(w_ref[...], gv, m_ref[...], v_ref[...])

    steps = g.shape[1] // tr
    spec = pl.BlockSpec((tr, tn), (lambda h, i: (i, h)) if by_columns else (lambda h, i: (h * steps + i, 0)))
    return pl.pallas_call(
        body, name=name, grid=(2, steps), out_shape=[jax.ShapeDtypeStruct((r, n), F32)] * 4,
        in_specs=[spec, pl.BlockSpec((None, tr, tn), lambda h, i: (h, i, 0)), spec, spec], out_specs=[spec] * 4,
        compiler_params=_params("parallel", "parallel"),
    )(w, g, m, v)


SMALL_PARAMS = ("b_ada", "g_mix", "g_ffn", "g_final", "b_spatial", "sinks", "w_spatial")


def _small_update_kernel(gathered, params):
    shapes = [params[nm][0].shape for nm in SMALL_PARAMS]

    def body(*refs):
        g_refs, refs = refs[:5], refs[5:]
        p_refs, refs = refs[:3 * len(SMALL_PARAMS)], refs[3 * len(SMALL_PARAMS):]
        loss_ref, o_refs = refs[0], refs[1:]

        def total(ref):
            acc = ref[0]
            for k in range(1, N_DEV):
                acc = acc + ref[k]
            return acc

        s1, s2, db, ds, dw = (total(r) for r in g_refs)
        loss_ref[...] = jnp.broadcast_to(s2[6:7, 0:1], loss_ref.shape)
        grads = {"b_ada": [s1[0:1], s1[1:2], s2[5:6], s2[0:1], s2[1:2], s2[2:3]], "g_mix": [s1[2:3]],
                 "g_ffn": [s2[3:4]], "g_final": [s2[4:5]], "b_spatial": [db.T[0:GMLP_GROUPS]],
                 "w_spatial": [dw]}
        lane = lax.broadcasted_iota(jnp.int32, (1, LANES), 1)
        sink_row = jnp.zeros((1, LANES), F32)
        for h in range(N_Q_HEADS):
            sink_row = sink_row + jnp.where(lane == h, ds[h:h + 1, :], 0.0)
        grads["sinks"] = [sink_row[:, 0:N_Q_HEADS]]
        for i, nm in enumerate(SMALL_PARAMS):
            w_ref, m_ref, v_ref = p_refs[3 * i:3 * i + 3]
            outs = o_refs[4 * i:4 * i + 4]
            width = grads[nm][0].shape[1]
            for k, g in enumerate(grads[nm]):
                cols = slice(width * k, width * (k + 1))
                upd = _adam_update(w_ref[:, cols], g, m_ref[:, cols], v_ref[:, cols])
                for o_ref, val in zip(outs, (g,) + upd):
                    o_ref[:, cols] = val

    flat = [a for nm in SMALL_PARAMS for a in params[nm]]
    out_shape = [jax.ShapeDtypeStruct((8, LANES), F32)]
    out_shape += [jax.ShapeDtypeStruct(s, F32) for s in shapes for _ in range(4)]
    outs = pl.pallas_call(
        body, name="small_update", grid=(1,), out_shape=out_shape,
        in_specs=[_full(g.shape) for g in gathered] + [_full(a.shape) for a in flat],
        out_specs=[_full(s.shape) for s in out_shape],
        compiler_params=_params("arbitrary"),
    )(*gathered, *flat)
    return {nm: outs[1 + 4 * i:5 + 4 * i] for i, nm in enumerate(SMALL_PARAMS)}, outs[0]


def _ada_update_kernel(act_t, dmod, w, m, v):
    r, n = w.shape
    tr = 256

    def body(a_ref, d_ref, w_ref, m_ref, v_ref, g_ref, dl_ref, mo_ref, vo_ref):
        g = _dot(a_ref[...], d_ref[...])
        g_ref[...] = g
        dl_ref[...], mo_ref[...], vo_ref[...] = _adam_update(w_ref[...], g, m_ref[...], v_ref[...])

    spec = pl.BlockSpec((tr, n), lambda i: (i, 0))
    return pl.pallas_call(
        body, name="ada_update", grid=(r // tr,), out_shape=[jax.ShapeDtypeStruct((r, n), F32)] * 4,
        in_specs=[pl.BlockSpec((tr, N_DEV), lambda i: (i, 0)), _full((N_DEV, n)), spec, spec, spec],
        out_specs=[spec] * 4, compiler_params=_params("parallel"),
    )(act_t, dmod, w, m, v)


def kernel(x, c, positions, w_ada, b_ada, g_mix, w_in, w_spatial, b_spatial, sinks, w_out, g_ffn, w_ff1, w_ff2, g_final, loss_target, m_w_ada, m_b_ada, m_g_mix, m_w_in, m_w_spatial, m_b_spatial, m_sinks, m_w_out, m_g_ffn, m_w_ff1, m_w_ff2, m_g_final, v_w_ada, v_b_ada, v_g_mix, v_w_in, v_w_spatial, v_b_spatial, v_sinks, v_w_out, v_g_ffn, v_w_ff1, v_w_ff2, v_g_final):
    xi, yi, ci = lax.axis_index("x"), lax.axis_index("y"), lax.axis_index("c")
    chip = 2 * xi + yi
    dev = 2 * chip + ci
    seq = x.shape[1]
    x2, tgt = x[0], loss_target[0]
    ada_cols = w_ada.shape[2]

    big = {"w_in": tuple(a[0].T for a in (w_in, m_w_in, v_w_in)),
           "w_out": (w_out[0], m_w_out[0], v_w_out[0]), "w_ff1": (w_ff1[0], m_w_ff1[0], v_w_ff1[0]),
           "w_ff2": (w_ff2[0], m_w_ff2[0], v_w_ff2[0])}

    def halves(nm):
        r, n = big[nm][0].shape
        return big[nm][0].astype(WEIGHT_COMM_DTYPE).reshape(2, r // 2, n)

    c_all, w_in_t, g_out = _all_gather8([c, halves("w_in"), halves("w_out")], "gather_first",
                                        split=[False, True, True])
    c_all, w_in_t = c_all.reshape(N_DEV, D_MODEL), w_in_t.reshape(IN_PROJ_WIDTH, D_MODEL)
    b_shard = lax.dynamic_slice(b_ada, (0, chip * ada_cols), (1, ada_cols))
    mod_part, act = _mod_kernel(c_all, w_ada[0], b_shard)
    mod_all, = _all_gather8([mod_part], "gather_mod")
    mod_me = lax.dynamic_index_in_dim(mod_all[0::2], dev, axis=1, keepdims=False)
    mod_me = mod_me.reshape(N_MOD, D_MODEL)
    shift1, scale1, gate1, shift2, scale2, gate2 = (mod_me[k:k + 1] for k in range(N_MOD))

    zeros_row = jnp.zeros((1, D_MODEL), F32)
    vecs1 = jnp.concatenate([g_mix, shift1, scale1] + [zeros_row] * 5, axis=0)
    vecs2 = jnp.concatenate([gate1, shift2, scale2, gate2, g_ffn, g_final.reshape(1, D_MODEL)]
                            + [zeros_row] * 2, axis=0)
    bias_full = jnp.repeat(b_spatial[0].T, HEAD_DIM, axis=1)
    sink_rows = jnp.broadcast_to(sinks[0][:, None], (N_Q_HEADS, LANES))
    inv_freq = ROPE_THETA ** (-jnp.arange(0, ROT_DIM, 2, dtype=F32) / ROT_DIM)
    rope_tab = _rope_lane_tables(*_rope_angle_kernel(positions, inv_freq.reshape(ROT_DIM // 2, 1)))

    trunk_weights = ["w_out", "w_ff1", "w_ff2"]
    shards = [halves(nm) for nm in trunk_weights]
    proj, hb, *staged = _in_proj_kernel(x2, vecs1, w_in_t, comm=_gather2d_first(shards[1:]))
    cat, *staged = _mixer_fwd_kernel(proj, rope_tab, w_spatial[0], bias_full, sink_rows,
                                     comm=_gather2d_second(staged, shards[1:]))
    staged = [g_out] + list(_gather_forward(staged, "gather_forward"))
    dx1, dcat, dmix, h2b, rb, dab, dffb, sums2 = _trunk_kernel(
        x2, tgt, cat, vecs2, chip.reshape(1).astype(jnp.int32),
        [g.reshape((N_CHIPS,) + big[nm][0].shape) for nm, g in zip(trunk_weights, staged)],
        [s.reshape(big[nm][0].shape) for nm, s in zip(trunk_weights, shards)])

    c_idx = ci.reshape(1).astype(jnp.int32)
    place = jnp.stack([chip, ci]).astype(jnp.int32)
    half_d = D_MODEL // 2
    cs_ff2 = _weight_grad_kernel(rb, dffb, c_idx, "dw_ff2",
                                 _GradTiles(D_MODEL, half_d, N_CHIPS, 1, lambda t, h: t, lambda t, h: h))
    cs_ff1, sc_ff2 = _weight_grad_kernel(
        h2b, dab, c_idx, "dw_ff1",
        _GradTiles(D_MODEL, half_d, N_CHIPS, 1, lambda t, h: 0, lambda t, h: 2 * t + h),
        comm=_scatter_job([cs_ff2]))
    cs_out = _weight_grad_kernel(cat, dmix, c_idx, "dw_out",
                                 _GradTiles(D_MODEL, half_d, 1, N_CHIPS, lambda t, h: 0, lambda t, h: h))
    dproj, dw_spatial, db_lanes, dsink_rows, sc_ff1, sc_out = _mixer_bwd_kernel(
        proj, rope_tab, dcat, w_spatial[0], w_spatial[0].transpose(0, 2, 1), bias_full, sink_rows,
        dev.reshape(1).astype(jnp.int32), comm=_scatter_job([cs_ff1, cs_out]))
    totals = [_sum_chips_kernel(own, oth, place, "grad_sum_" + nm)
              for nm, own, oth in (("w_out", cs_out, sc_out), ("w_ff1", cs_ff1, sc_ff1), ("w_ff2", cs_ff2, sc_ff2))]
    small_slots = [db_lanes, dsink_rows, dw_spatial.reshape(N_DEV, GMLP_GROUPS * CHUNK, CHUNK)]
    cs_in, *rode = _weight_grad_kernel(
        dproj, hb, c_idx, "dw_in",
        _GradTiles(2 * W_IN_BLOCK, half_d, N_CHIPS // 2, 2, lambda t, h: t, lambda t, h: h),
        comm=_merge_in_place(_gather_job(small_slots), _share_job(totals)))
    small_stage1, shared = rode[:len(small_slots)], rode[len(small_slots):]
    grad_x, sums1 = _in_proj_bwd_kernel(x2, dx1, dproj, vecs1, w_in_t)
    *gathered, sc_in = _all_gather8([sums1, sums2], "gather_small", forward=small_stage1,
                                    riders=[_scatter_job([cs_in])])
    total_in = _sum_chips_kernel(cs_in, sc_in, place, "grad_sum_w_in")
    shared = list(_sibling_share([total_in], "grad_share_w_in")) + list(shared)
    names = ["w_in", "w_out", "w_ff1", "w_ff2"]
    big_out = {}
    for nm, g in zip(names, shared):
        w, m, v = big[nm]
        outs = _adam_kernel(w, g, m, v, "adam_" + nm)
        big_out[nm] = tuple((t.T if nm == "w_in" else t)[None] for t in outs)

    small = {"b_ada": (b_ada, m_b_ada, v_b_ada), "g_mix": (g_mix, m_g_mix, v_g_mix),
             "g_ffn": (g_ffn, m_g_ffn, v_g_ffn), "g_final": (g_final, m_g_final, v_g_final),
             "b_spatial": (b_spatial, m_b_spatial, v_b_spatial), "sinks": (sinks, m_sinks, v_sinks),
             "w_spatial": (w_spatial, m_w_spatial, v_w_spatial)}
    flat_shape = {"g_final": (1, D_MODEL), "b_spatial": (GMLP_GROUPS, CHUNK), "w_spatial": (GMLP_GROUPS * CHUNK, CHUNK)}
    small_out, loss_tile = _small_update_kernel(
        gathered, {nm: tuple(a.reshape(flat_shape.get(nm, a.shape)) for a in small[nm]) for nm in small})
    small_out = {nm: [o.reshape(small[nm][0].shape) for o in small_out[nm]] for nm in small}
    loss = loss_tile[0, 0]

    g1, g2 = gathered[0], gathered[1]
    dmod_all = jnp.concatenate([g1[:, 0], g1[:, 1], g2[:, 5], g2[:, 0], g2[:, 1], g2[:, 2]], axis=1)
    dmod_cols = lax.dynamic_slice(dmod_all, (0, chip * ada_cols), (N_DEV, ada_cols))
    ada = _ada_update_kernel(act.T, dmod_cols, w_ada[0], m_w_ada[0], v_w_ada[0])
    big_out["w_ada"] = tuple(t[None] for t in ada)

    order = ["w_ada", "b_ada", "g_mix", "w_in", "w_spatial", "b_spatial", "sinks", "w_out", "g_ffn",
             "w_ff1", "w_ff2", "g_final"]

    def leaf(nm, k):
        return big_out[nm][k] if nm in big_out else small_out[nm][k]

    outs = [loss, grad_x[None]]
    for k in range(4):
        outs += [leaf(nm, k) for nm in order]
    return tuple(outs)
```

```python
import math
from typing import Callable, NamedTuple

import jax
import jax.numpy as jnp
from jax import lax
from jax.experimental import pallas as pl
from jax.experimental.pallas import tpu as pltpu

F32 = jnp.float32
MXU_DTYPE = jnp.bfloat16
WEIGHT_COMM_DTYPE = jnp.bfloat16
GRAD_COMM_DTYPE = jnp.bfloat16

D_MODEL = 1024
D_FF = 4096
HEAD_DIM = 64
GMLP_GROUPS = 8
GMLP_WIDTH = 512
CHUNK = 128
N_Q_HEADS = 8
N_KV_HEADS = 2
ATTN_WIDTH = 512
KV_WIDTH = 128
ROT_DIM = 16
ROPE_THETA = 500000.0
IN_PROJ_WIDTH = 1792
N_MOD = 6
EPS = 1e-5
N_CHIPS = 4
N_DEV = 8
LANES = 128
W_IN_BLOCK = IN_PROJ_WIDTH // N_CHIPS

ADAM_LR = 0.001
ADAM_B1 = 0.9
ADAM_B2 = 0.999
ADAM_EPS = 1e-08
ADAM_WD = 0.01
ADAM_STEP = 10

VMEM_LIMIT_BYTES = 58 * 1024 * 1024
MESH = pl.DeviceIdType.MESH


def _params(*semantics):
    return pltpu.CompilerParams(dimension_semantics=semantics, vmem_limit_bytes=VMEM_LIMIT_BYTES)


def _dot(a, b):
    return jnp.dot(a.astype(MXU_DTYPE), b.astype(MXU_DTYPE), preferred_element_type=F32)


def _dot_nt(a, b):
    return lax.dot_general(a.astype(MXU_DTYPE), b.astype(MXU_DTYPE), (((1,), (1,)), ((), ())),
                           preferred_element_type=F32)


def _dot_tn(a, b):
    return lax.dot_general(a.astype(MXU_DTYPE), b.astype(MXU_DTYPE), (((0,), (0,)), ((), ())),
                           preferred_element_type=F32)


def _full(shape):
    return pl.BlockSpec(shape, lambda *_: (0,) * len(shape))


def _any():
    return pl.BlockSpec(memory_space=pl.ANY)


def _rowsum(v):
    return jnp.sum(v, axis=0, keepdims=True)


def _mean_last(v):
    return jnp.mean(v, axis=-1, keepdims=True)


class _Comm(NamedTuple):
    operands: tuple
    out_shapes: tuple
    n_sems: int
    make: Callable
    in_place: int = 0


def _hosted_call(body, comm, *, name, grid, in_specs, out_shape, out_specs, scratch_shapes=(), semantics,
                 n_prefetch=0):
    if comm is None:
        return pl.pallas_call(
            body, name=name, out_shape=out_shape, compiler_params=_params(*semantics),
            grid_spec=pltpu.PrefetchScalarGridSpec(
                num_scalar_prefetch=n_prefetch, grid=grid, in_specs=in_specs, out_specs=out_specs,
                scratch_shapes=list(scratch_shapes)))
    n_in, n_out, n_scr = len(in_specs), len(out_shape), len(scratch_shapes)
    k_in, k_out = len(comm.operands), len(comm.out_shapes)

    def hosted(*refs):
        prefetched, refs = refs[:n_prefetch], refs[n_prefetch:]
        ins, refs = refs[:n_in], refs[n_in:]
        c_ins, refs = refs[:k_in], refs[k_in:]
        outs, refs = refs[:n_out], refs[n_out:]
        c_outs, refs = refs[:k_out], refs[k_out:]
        scratch, (send_sems, recv_sems) = refs[:n_scr], refs[n_scr:]
        first, last = None, None
        for d, size in enumerate(grid):
            at_start, at_end = pl.program_id(d) == 0, pl.program_id(d) == size - 1
            first = at_start if first is None else first & at_start
            last = at_end if last is None else last & at_end

        @pl.when(first)
        def _():
            for cp in comm.make(c_ins, c_outs, send_sems, recv_sems)[0]:
                cp.start()

        body(*prefetched, *ins, *outs, *scratch)

        @pl.when(last)
        def _():
            for wait in comm.make(c_ins, c_outs, send_sems, recv_sems)[1]:
                wait()

    aliases = {n_prefetch + n_in + i: n_out + i for i in range(comm.in_place)}
    call = pl.pallas_call(
        hosted, name=name, out_shape=list(out_shape) + list(comm.out_shapes),
        compiler_params=_params(*semantics), input_output_aliases=aliases,
        grid_spec=pltpu.PrefetchScalarGridSpec(
            num_scalar_prefetch=n_prefetch, grid=grid, in_specs=list(in_specs) + [_any()] * k_in,
            out_specs=list(out_specs) + [_any()] * k_out,
            scratch_shapes=list(scratch_shapes) + [pltpu.SemaphoreType.DMA((comm.n_sems,)),
                                                    pltpu.SemaphoreType.DMA((comm.n_sems,))]))
    return lambda *args: call(*args, *comm.operands)


class _Shifted:
    def __init__(self, base, offset):
        self.base, self.offset = base, offset

    @property
    def at(self):
        return self

    def __getitem__(self, k):
        return self.base.at[self.offset + k]


def _merge_in_place(*jobs):
    assert all(j.in_place == len(j.operands) == len(j.out_shapes) for j in jobs)

    def make(ins, outs, send_sems, recv_sems):
        starts, waits, at, sem = [], [], 0, 0
        for j in jobs:
            n = len(j.operands)
            s, w = j.make(ins[at:at + n], outs[at:at + n], _Shifted(send_sems, sem), _Shifted(recv_sems, sem))
            starts, waits, at, sem = starts + s, waits + w, at + n, sem + j.n_sems
        return starts, waits

    operands = tuple(a for j in jobs for a in j.operands)
    return _Comm(operands, tuple(s for j in jobs for s in j.out_shapes), sum(j.n_sems for j in jobs), make,
                 in_place=len(operands))


def _mesh_place():
    x, y, c = lax.axis_index("x"), lax.axis_index("y"), lax.axis_index("c")
    return x, y, c, [(1 - x, y), (x, 1 - y), (1 - x, 1 - y)]


def _gather_job(bufs):
    per = 4

    def make(ins, outs, send_sems, recv_sems):
        del ins
        x, y, c, chips = _mesh_place()
        starts, waits = [], []
        for a, out in enumerate(outs):
            mine = src = out.at[4 * x + 2 * y + c]
            to = [(x, y, 1 - c)] + [(px, py, c) for px, py in chips]
            sends = [pltpu.make_async_remote_copy(
                src_ref=src, dst_ref=mine, send_sem=send_sems.at[per * a + k],
                recv_sem=recv_sems.at[per * a + k], device_id=dev, device_id_type=MESH)
                for k, dev in enumerate(to)]
            recvs = [pltpu.make_async_remote_copy(
                src_ref=src, dst_ref=out.at[4 * px + 2 * py + pc], send_sem=send_sems.at[per * a + k],
                recv_sem=recv_sems.at[per * a + k], device_id=(px, py, pc), device_id_type=MESH)
                for k, (px, py, pc) in enumerate(to)]
            starts += sends
            waits += [s.wait_send for s in sends] + [r.wait_recv for r in recvs]
        return starts, waits

    shapes = tuple(jax.ShapeDtypeStruct(b.shape, b.dtype) for b in bufs)
    return _Comm(tuple(bufs), shapes, per * len(bufs), make, in_place=len(bufs))


def _slots(x, y, c):
    return 4 * x + 2 * y + c, 4 * (1 - x) + 2 * y + c, 4 * x + 2 * (1 - y) + c, 4 * (1 - x) + 2 * (1 - y) + c


def _gather2d_first(halves):
    per = 2

    def make(ins, outs, send_sems, recv_sems):
        x, y, c, _ = _mesh_place()
        me, xn, yn, _ = _slots(x, y, c)
        starts, waits = [], []
        for a, (src, out) in enumerate(zip(ins, outs)):
            blk = src.at[c]
            rows = blk.shape[0] // 2
            upper, lower = pl.ds(0, rows), pl.ds(rows, rows)

            def copy(k, src_ref, dst_ref, dev, a=a):
                return pltpu.make_async_remote_copy(
                    src_ref=src_ref, dst_ref=dst_ref, send_sem=send_sems.at[per * a + k],
                    recv_sem=recv_sems.at[per * a + k], device_id=dev, device_id_type=MESH)

            sends = [copy(0, blk.at[upper], out.at[me, upper], (1 - x, y, c)),
                     copy(1, blk.at[lower], out.at[me, lower], (x, 1 - y, c))]
            recvs = [copy(0, blk.at[upper], out.at[xn, upper], (1 - x, y, c)),
                     copy(1, blk.at[lower], out.at[yn, lower], (x, 1 - y, c))]
            starts += sends
            waits += [s.wait_send for s in sends] + [r.wait_recv for r in recvs]
        return starts, waits

    shapes = tuple(jax.ShapeDtypeStruct((N_DEV,) + h.shape[1:], h.dtype) for h in halves)
    return _Comm(tuple(halves), shapes, per * len(halves), make)


def _gather2d_second(bufs, halves):
    per = 4
    n_arr = len(bufs)

    def make(ins, outs, send_sems, recv_sems):
        x, y, c, _ = _mesh_place()
        me, xn, yn, dg = _slots(x, y, c)
        starts, waits = [], []
        for a, buf in enumerate(outs):
            own = ins[n_arr + a].at[c]
            rows = buf.shape[1] // 2
            upper, lower = pl.ds(0, rows), pl.ds(rows, rows)
            plan = [(own.at[upper], me, upper, (x, 1 - y, c), yn), (buf.at[xn, upper], xn, upper, (x, 1 - y, c), dg),
                    (own.at[lower], me, lower, (1 - x, y, c), xn), (buf.at[yn, lower], yn, lower, (1 - x, y, c), dg)]
            for k, (src, slot, part, dev, landing) in enumerate(plan):
                sems = dict(send_sem=send_sems.at[per * a + k], recv_sem=recv_sems.at[per * a + k],
                            device_id=dev, device_id_type=MESH)
                send = pltpu.make_async_remote_copy(src_ref=src, dst_ref=buf.at[slot, part], **sems)
                arrival = pltpu.make_async_remote_copy(src_ref=src, dst_ref=buf.at[landing, part], **sems)
                starts.append(send)
                waits += [send.wait_send, arrival.wait_recv]
        return starts, waits

    shapes = tuple(jax.ShapeDtypeStruct(b.shape, b.dtype) for b in bufs)
    return _Comm(tuple(bufs) + tuple(halves), shapes, per * n_arr, make, in_place=n_arr)


def _gather_forward(bufs, name):
    n_arr = len(bufs)

    def body(*refs):
        outs = refs[n_arr:2 * n_arr]
        send_sems, recv_sems = refs[2 * n_arr:]
        x, y, c, chips = _mesh_place()
        sends, recvs = [], []
        for a, buf in enumerate(outs):
            for j, (px, py) in enumerate(chips):
                mine, theirs = buf.at[4 * px + 2 * py + c], buf.at[4 * px + 2 * py + 1 - c]
                sems = dict(send_sem=send_sems.at[3 * a + j], recv_sem=recv_sems.at[3 * a + j],
                            device_id=(x, y, 1 - c), device_id_type=MESH)
                sends.append(pltpu.make_async_remote_copy(src_ref=mine, dst_ref=mine, **sems))
                recvs.append(pltpu.make_async_remote_copy(src_ref=mine, dst_ref=theirs, **sems))
        for cp in sends:
            cp.start()
        for s, r in zip(sends, recvs):
            s.wait_send()
            r.wait_recv()

    return pl.pallas_call(
        body, name=name, out_shape=[jax.ShapeDtypeStruct(b.shape, b.dtype) for b in bufs],
        in_specs=[_any()] * n_arr, out_specs=[_any()] * n_arr,
        input_output_aliases={a: a for a in range(n_arr)},
        scratch_shapes=[pltpu.SemaphoreType.DMA((3 * n_arr,)), pltpu.SemaphoreType.DMA((3 * n_arr,))],
    )(*bufs)


def _scatter_job(chip_sums):
    def make(ins, outs, send_sems, recv_sems):
        x, y, c, chips = _mesh_place()
        copies = [pltpu.make_async_remote_copy(
            src_ref=src.at[2 * px + py], dst_ref=out.at[j], send_sem=send_sems.at[3 * a + j],
            recv_sem=recv_sems.at[3 * a + j], device_id=(px, py, c), device_id_type=MESH)
            for a, (src, out) in enumerate(zip(ins, outs)) for j, (px, py) in enumerate(chips)]
        return copies, [cp.wait for cp in copies]

    return _Comm(tuple(chip_sums), tuple(jax.ShapeDtypeStruct((3,) + s.shape[1:], s.dtype) for s in chip_sums),
                 3 * len(chip_sums), make)


def _all_gather8(blocks, name, split=False, forward=(), riders=(), skip_own=()):
    n_arr, n_fwd = len(blocks), len(forward)
    splits = list(split) if isinstance(split, (list, tuple)) else [split] * n_arr
    own_slots = [a not in skip_own for a in range(n_arr)]
    rider_in = sum(len(r.operands) for r in riders)
    rider_out = sum(len(r.out_shapes) for r in riders)

    def body(*refs):
        x_refs, refs = refs[:n_arr], refs[n_arr + n_fwd:]
        r_ins, refs = refs[:rider_in], refs[rider_in:]
        out_refs, refs = refs[:n_arr], refs[n_arr:]
        fwd_refs, refs = refs[:n_fwd], refs[n_fwd:]
        r_outs, refs = refs[:rider_out], refs[rider_out:]
        (send_sems, recv_sems, local_sems), rider_sems = refs[:3], refs[3:]
        x, y, c, chips = _mesh_place()
        me, sibling = (x, y, c), (x, y, 1 - c)
        rider_waits, i0, o0 = [], 0, 0
        for n, job in enumerate(riders):
            k_in, k_out = len(job.operands), len(job.out_shapes)
            starts, waits = job.make(r_ins[i0:i0 + k_in], r_outs[o0:o0 + k_out],
                                     rider_sems[2 * n], rider_sems[2 * n + 1])
            for cp in starts:
                cp.start()
            rider_waits += waits
            i0, o0 = i0 + k_in, o0 + k_out
        passing = []
        for f, buf in enumerate(fwd_refs):
            for j, (px, py) in enumerate(chips):
                mine, theirs = buf.at[4 * px + 2 * py + c], buf.at[4 * px + 2 * py + 1 - c]
                sems = dict(send_sem=send_sems.at[7 * n_arr + 3 * f + j], recv_sem=recv_sems.at[7 * n_arr + 3 * f + j],
                            device_id=sibling, device_id_type=MESH)
                passing.append((pltpu.make_async_remote_copy(src_ref=mine, dst_ref=mine, **sems),
                                pltpu.make_async_remote_copy(src_ref=mine, dst_ref=theirs, **sems)))
        for send, _ in passing:
            send.start()
        arrays = []
        for a, (x_ref, out_ref) in enumerate(zip(x_refs, out_refs)):
            src_mine = x_ref.at[c] if splits[a] else x_ref

            def copy(k, blk, to, src=None, a=a, out_ref=out_ref):
                dst = out_ref.at[4 * blk[0] + 2 * blk[1] + blk[2]]
                return pltpu.make_async_remote_copy(
                    src_ref=dst if src is None else src, dst_ref=dst,
                    send_sem=send_sems.at[7 * a + k], recv_sem=recv_sems.at[7 * a + k],
                    device_id=to, device_id_type=MESH)

            mine = pltpu.make_async_copy(src_mine, out_ref.at[4 * x + 2 * y + c], local_sems.at[a])
            first = [copy(0, me, sibling, src=src_mine)] if own_slots[a] else []
            first += [copy(1 + j, me, (*chip, c), src=src_mine) for j, chip in enumerate(chips)]
            for cp in first + ([mine] if own_slots[a] else []):
                cp.start()
            arrays.append((copy, mine, first, own_slots[a]))
        sent = []
        for copy, mine, first, own in arrays:
            passed = [copy(4 + j, (*chip, c), sibling) for j, chip in enumerate(chips)]
            for j, chip in enumerate(chips):
                copy(1 + j, (*chip, c), me).wait_recv()
                passed[j].start()
            sent += first + passed
        for copy, mine, first, own in arrays:
            if own:
                copy(0, sibling, me).wait_recv()
                mine.wait()
            for j, chip in enumerate(chips):
                copy(4 + j, (*chip, 1 - c), me).wait_recv()
        for cp in sent:
            cp.wait_send()
        for send, arrival in passing:
            send.wait_send()
            arrival.wait_recv()
        for wait in rider_waits:
            wait()

    n_sems = 7 * n_arr + 3 * n_fwd
    rider_operands = [a for r in riders for a in r.operands]
    rider_shapes = [s for r in riders for s in r.out_shapes]
    return pl.pallas_call(
        body, name=name,
        out_shape=[jax.ShapeDtypeStruct((N_DEV,) + tuple(b.shape[1:] if s else b.shape), b.dtype)
                   for b, s in zip(blocks, splits)]
        + [jax.ShapeDtypeStruct(f.shape, f.dtype) for f in forward] + rider_shapes,
        in_specs=[_any()] * (n_arr + n_fwd + rider_in), out_specs=[_any()] * (n_arr + n_fwd + rider_out),
        input_output_aliases={n_arr + f: n_arr + f for f in range(n_fwd)},
        scratch_shapes=[pltpu.SemaphoreType.DMA((n_sems,)), pltpu.SemaphoreType.DMA((n_sems,)),
                        pltpu.SemaphoreType.DMA((n_arr,))]
        + [pltpu.SemaphoreType.DMA((r.n_sems,)) for r in riders for _ in range(2)],
    )(*blocks, *forward, *rider_operands)


def _share_job(bufs):
    def make(ins, outs, send_sems, recv_sems):
        del ins
        x, y, c, _ = _mesh_place()
        sems = lambda a: dict(send_sem=send_sems.at[a], recv_sem=recv_sems.at[a],
                              device_id=(x, y, 1 - c), device_id_type=MESH)
        sends = [pltpu.make_async_remote_copy(src_ref=o.at[c], dst_ref=o.at[c], **sems(a)) for a, o in enumerate(outs)]
        arrivals = [pltpu.make_async_remote_copy(src_ref=o.at[c], dst_ref=o.at[1 - c], **sems(a))
                    for a, o in enumerate(outs)]
        return sends, [s.wait_send for s in sends] + [r.wait_recv for r in arrivals]

    shapes = tuple(jax.ShapeDtypeStruct(b.shape, b.dtype) for b in bufs)
    return _Comm(tuple(bufs), shapes, len(bufs), make, in_place=len(bufs))


def _sibling_share(bufs, name):
    n_arr = len(bufs)

    def body(*refs):
        out_refs = refs[n_arr:2 * n_arr]
        send_sems, recv_sems = refs[2 * n_arr:]
        x, y, c = lax.axis_index("x"), lax.axis_index("y"), lax.axis_index("c")
        copies = [pltpu.make_async_remote_copy(
            src_ref=out_refs[a].at[c], dst_ref=out_refs[a].at[c],
            send_sem=send_sems.at[a], recv_sem=recv_sems.at[a],
            device_id=(x, y, 1 - c), device_id_type=MESH) for a in range(n_arr)]
        for cp in copies:
            cp.start()
        for a in range(n_arr):
            pltpu.make_async_remote_copy(
                src_ref=out_refs[a].at[c], dst_ref=out_refs[a].at[1 - c],
                send_sem=send_sems.at[a], recv_sem=recv_sems.at[a],
                device_id=(x, y, 1 - c), device_id_type=MESH).wait()

    return pl.pallas_call(
        body, name=name,
        out_shape=[jax.ShapeDtypeStruct(b.shape, b.dtype) for b in bufs],
        in_specs=[_any()] * n_arr, out_specs=[_any()] * n_arr,
        input_output_aliases={a: a for a in range(n_arr)},
        scratch_shapes=[pltpu.SemaphoreType.DMA((n_arr,)), pltpu.SemaphoreType.DMA((n_arr,))],
    )(*bufs)


def _gelu_tanh(z):
    k = math.sqrt(2.0 / math.pi)
    t = jnp.tanh(k * (z + 0.044715 * (z * z * z)))
    return 0.5 * z * (1.0 + t), t


def _gelu_tanh_grad(z, t):
    k = math.sqrt(2.0 / math.pi)
    return 0.5 * (1.0 + t) + 0.5 * z * (1.0 - t * t) * (k * (1.0 + 3.0 * 0.044715 * (z * z)))


def _rope_angle_kernel(pos_row, invf_col):
    seq = pos_row.shape[1]

    def body(p_ref, f_ref, cos_ref, sin_ref):
        ang = p_ref[...].astype(F32) * f_ref[...]
        cos_ref[...] = jnp.cos(ang)
        sin_ref[...] = jnp.sin(ang)

    return pl.pallas_call(
        body, name="rope_angles", grid=(1,), out_shape=[jax.ShapeDtypeStruct((ROT_DIM // 2, seq), F32)] * 2,
        in_specs=[_full((1, seq)), _full((ROT_DIM // 2, 1))], out_specs=[_full((ROT_DIM // 2, seq))] * 2,
        compiler_params=_params("arbitrary"),
    )(pos_row, invf_col)


def _rope_lane_tables(cos, sin):
    cos_t, sin_t = cos.T, sin.T
    seq, half = cos_t.shape
    ones = jnp.ones((seq, HEAD_DIM - ROT_DIM), F32)
    c64 = jnp.concatenate([cos_t, cos_t, ones], axis=1)
    s1 = jnp.concatenate([sin_t, jnp.zeros((seq, HEAD_DIM - half), F32)], axis=1)
    s2 = jnp.concatenate([jnp.zeros((seq, half), F32), sin_t, jnp.zeros((seq, HEAD_DIM - ROT_DIM), F32)], axis=1)
    return jnp.concatenate([jnp.tile(t, (1, LANES // HEAD_DIM)) for t in (c64, s1, s2)], axis=1)


def _rope_apply(t, tab, sign):
    reps = t.shape[1] // LANES
    c_tab, s1, s2 = (jnp.tile(tab[:, LANES * k:LANES * (k + 1)], (1, reps)) if reps > 1
                     else tab[:, LANES * k:LANES * (k + 1)] for k in range(3))
    half = ROT_DIM // 2
    up = pltpu.roll(t, t.shape[1] - half, 1)
    down = pltpu.roll(t, half, 1)
    return t * c_tab + sign * (down * s2 - up * s1)


def _lane_masks(shape):
    lane = lax.broadcasted_iota(jnp.int32, shape, 1)
    return lane < HEAD_DIM, lane >= HEAD_DIM


HEADS_PER_GROUP = N_Q_HEADS // N_KV_HEADS
ATTN_SCALE = 1.0 / math.sqrt(HEAD_DIM)


def _attn_bias_t(first_block):
    kj = lax.broadcasted_iota(jnp.int32, (2 * CHUNK, CHUNK), 0)
    qi = lax.broadcasted_iota(jnp.int32, (2 * CHUNK, CHUNK), 1)
    ok = (kj > qi) & (kj <= qi + CHUNK)
    if first_block is not None:
        ok = ok & (jnp.logical_not(first_block) | (kj >= CHUNK))
    return jnp.tile(jnp.where(ok, 0.0, -jnp.inf), (1, HEADS_PER_GROUP))


def _group_rows(x, g, lo, hi):
    rows = []
    for r in range(HEADS_PER_GROUP):
        h = HEADS_PER_GROUP * g + r
        pair = x[:, LANES * (h // 2):LANES * (h // 2 + 1)]
        rows.append(jnp.where(hi if h % 2 else lo, pair, 0.0))
    return jnp.concatenate(rows, axis=0)


def _pairs_from_rows(rows, lo):
    return [jnp.where(lo, rows[2 * CHUNK * k:2 * CHUNK * k + CHUNK], rows[2 * CHUNK * k + CHUNK:2 * CHUNK * (k + 1)])
            for k in range(HEADS_PER_GROUP // 2)]


def _group_dup(a, b, g, lo2):
    return jnp.where(lo2, a, b) if g == 0 else jnp.where(lo2, b, a)


def _sink_row(sink_ref, g):
    return jnp.concatenate([sink_ref[HEADS_PER_GROUP * g + r:HEADS_PER_GROUP * g + r + 1, :]
                            for r in range(HEADS_PER_GROUP)], axis=1)


def _attn_probs_t(k_dup, q_rows, bias_t, sink_row):
    s_t = _dot_nt(k_dup, q_rows) * ATTN_SCALE + bias_t
    m = jnp.maximum(jnp.max(s_t, axis=0, keepdims=True), sink_row)
    p = jnp.exp(s_t - m)
    e_sink = jnp.exp(sink_row - m)
    inv = 1.0 / (jnp.sum(p, axis=0, keepdims=True) + e_sink)
    return p * inv, e_sink * inv


def _sgu_forward_pair(wm, vp, j):
    lo, hi = _lane_masks(vp.shape)
    lhs = jnp.concatenate([wm[2 * j], wm[2 * j + 1]], axis=1)
    rhs = jnp.concatenate([jnp.where(lo, vp, 0.0), jnp.where(hi, vp, 0.0)], axis=0)
    return _dot(lhs, rhs)


def _masked_spatial(w_ref):
    t = lax.broadcasted_iota(jnp.int32, (CHUNK, CHUNK), 0)
    s = lax.broadcasted_iota(jnp.int32, (CHUNK, CHUNK), 1)
    tril = s <= t
    return [jnp.where(tril, w_ref[g], 0.0) for g in range(GMLP_GROUPS)], tril, s >= t


def _mod_kernel(c_all, w_shard, b_shard):
    n = w_shard.shape[1]
    tn = 512

    def body(c_ref, w_ref, b_ref, mod_ref, act_ref):
        cv = c_ref[...]
        act = cv * (1.0 / (1.0 + jnp.exp(-cv)))
        act_ref[...] = act
        mod_ref[...] = _dot(act, w_ref[...]) + b_ref[...]

    return pl.pallas_call(
        body, name="ada_mod", grid=(n // tn,),
        out_shape=[jax.ShapeDtypeStruct((N_DEV, n), F32), jax.ShapeDtypeStruct((N_DEV, D_MODEL), F32)],
        in_specs=[_full((N_DEV, D_MODEL)), pl.BlockSpec((D_MODEL, tn), lambda i: (0, i)),
                  pl.BlockSpec((1, tn), lambda i: (0, i))],
        out_specs=[pl.BlockSpec((N_DEV, tn), lambda i: (0, i)), _full((N_DEV, D_MODEL))],
        compiler_params=_params("arbitrary"),
    )(c_all, w_shard, b_shard)


def _load_chip_blocks(chip_ref, gathered, local, dsts, sems, first_sem=0):
    for k, dst in enumerate(dsts):
        @pl.when(chip_ref[0] == k)
        def _():
            pltpu.make_async_copy(local, dst, sems.at[first_sem + k]).start()

        @pl.when(chip_ref[0] != k)
        def _():
            pltpu.make_async_copy(gathered.at[k], dst, sems.at[first_sem + k]).start()
    return [pltpu.make_async_copy(local, dst, sems.at[first_sem + k]).wait for k, dst in enumerate(dsts)]


def _w_in_rows(w_scr):
    return [w_scr.at[pl.ds(W_IN_BLOCK * k, W_IN_BLOCK)] for k in range(N_CHIPS)]


def _in_proj_kernel(x, vecs, chip_idx, w_in_gathered, w_in_local, comm=None):
    seq = x.shape[0]
    tm = 512

    def body(chip_ref, x_ref, v_ref, wg_ref, wl_ref, proj_ref, h_ref, w_ref, sems):
        @pl.when(pl.program_id(0) == 0)
        def _():
            for wait in _load_chip_blocks(chip_ref, wg_ref, wl_ref, _w_in_rows(w_ref), sems):
                wait()

        xv = x_ref[...]
        rstd = lax.rsqrt(_mean_last(xv * xv) + EPS)
        n1 = (xv * rstd) * v_ref[0:1, :]
        h = n1 * (1.0 + v_ref[2:3, :]) + v_ref[1:2, :]
        hb = h.astype(MXU_DTYPE)
        h_ref[...] = hb
        proj_ref[...] = _dot_nt(hb, w_ref[...])

    return _hosted_call(
        body, comm, name="in_proj", grid=(seq // tm,), n_prefetch=1,
        out_shape=[jax.ShapeDtypeStruct((seq, IN_PROJ_WIDTH), F32),
                   jax.ShapeDtypeStruct((seq, D_MODEL), MXU_DTYPE)],
        in_specs=[pl.BlockSpec((tm, D_MODEL), lambda i, chip: (i, 0)), _full((8, D_MODEL)), _any(), _any()],
        out_specs=[pl.BlockSpec((tm, IN_PROJ_WIDTH), lambda i, chip: (i, 0)),
                   pl.BlockSpec((tm, D_MODEL), lambda i, chip: (i, 0))],
        scratch_shapes=[pltpu.VMEM((IN_PROJ_WIDTH, D_MODEL), MXU_DTYPE), pltpu.SemaphoreType.DMA((N_CHIPS,))],
        semantics=("arbitrary",),
    )(chip_idx, x, vecs, w_in_gathered, w_in_local)


MIXER_BLOCKS_PER_STEP = 2
KV_START = 2 * GMLP_WIDTH + ATTN_WIDTH


def _mixer_fwd_kernel(proj, rope_tab, w_spatial, bias_full, sink_rows, comm=None):
    seq = proj.shape[0]
    per = MIXER_BLOCKS_PER_STEP
    steps = seq // (CHUNK * per)
    kv_col = KV_START // (2 * KV_WIDTH)

    def body(proj_ref, prev_ref, tab_ref, ptab_ref, w_ref, bias_ref, sink_ref, cat_ref):
        i = pl.program_id(0)
        wm, _, _ = _masked_spatial(w_ref)
        lo, hi = _lane_masks((CHUNK, LANES))
        lo2, _ = _lane_masks((2 * CHUNK, LANES))
        o = 2 * GMLP_WIDTH
        for s in range(per):
            rows, before = slice(CHUNK * s, CHUNK * (s + 1)), slice(CHUNK * (s - 1), CHUNK * s)
            for j in range(GMLP_GROUPS // 2):
                cols = slice(LANES * j, LANES * (j + 1))
                vcols = slice(GMLP_WIDTH + LANES * j, GMLP_WIDTH + LANES * (j + 1))
                u, _ = _gelu_tanh(proj_ref[rows, cols])
                vp, _ = _gelu_tanh(proj_ref[rows, vcols])
                sv = _sgu_forward_pair(wm, vp, j) + bias_ref[:, cols]
                cat_ref[rows, cols] = (u * sv).astype(cat_ref.dtype)
            tab = tab_ref[rows, :]
            if s == 0:
                prev_kv, prev_tab, first = prev_ref[...], ptab_ref[...], i == 0
            else:
                prev_kv, prev_tab, first = proj_ref[before, KV_START:KV_START + 2 * KV_WIDTH], tab_ref[before, :], None
            q_r = _rope_apply(proj_ref[rows, o:o + ATTN_WIDTH], tab, 1.0)
            k_cur = _rope_apply(proj_ref[rows, KV_START:KV_START + KV_WIDTH], tab, 1.0)
            k_prev = _rope_apply(prev_kv[:, 0:KV_WIDTH], prev_tab, 1.0)
            k_a = jnp.concatenate([k_prev, k_cur], axis=0)
            v_a = jnp.concatenate([prev_kv[:, KV_WIDTH:2 * KV_WIDTH],
                                   proj_ref[rows, KV_START + KV_WIDTH:KV_START + 2 * KV_WIDTH]], axis=0)
            k_b = pltpu.roll(k_a, HEAD_DIM, 1)
            v_b = pltpu.roll(v_a, HEAD_DIM, 1)
            bias_t = _attn_bias_t(first)
            for g in range(N_KV_HEADS):
                p_t, _ = _attn_probs_t(_group_dup(k_a, k_b, g, lo2), _group_rows(q_r, g, lo, hi), bias_t,
                                       _sink_row(sink_ref, g))
                o_t = _dot(_group_dup(v_a, v_b, g, lo2).T, p_t)
                for k, pair in enumerate(_pairs_from_rows(o_t.T, lo)):
                    c0 = GMLP_WIDTH + LANES * (2 * g + k)
                    cat_ref[rows, c0:c0 + LANES] = pair.astype(cat_ref.dtype)

    return _hosted_call(
        body, comm, name="mixer_fwd", grid=(steps,),
        out_shape=[jax.ShapeDtypeStruct((seq, D_MODEL), MXU_DTYPE)],
        in_specs=[pl.BlockSpec((CHUNK * per, IN_PROJ_WIDTH), lambda i: (i, 0)),
                  pl.BlockSpec((CHUNK, 2 * KV_WIDTH), lambda i: (jnp.maximum(per * i - 1, 0), kv_col)),
                  pl.BlockSpec((CHUNK * per, 3 * LANES), lambda i: (i, 0)),
                  pl.BlockSpec((CHUNK, 3 * LANES), lambda i: (jnp.maximum(per * i - 1, 0), 0)),
                  _full((GMLP_GROUPS, CHUNK, CHUNK)), _full((CHUNK, GMLP_WIDTH)),
                  _full((N_Q_HEADS, LANES))],
        out_specs=[pl.BlockSpec((CHUNK * per, D_MODEL), lambda i: (i, 0))],
        semantics=("arbitrary",),
    )(proj, proj, rope_tab, rope_tab, w_spatial, bias_full, sink_rows)


def _trunk_kernel(x, target, cat, vecs, chip_idx, gathered, local):
    seq = x.shape[0]
    tm = 256
    nj = D_FF // D_MODEL
    out_rows = D_MODEL // N_CHIPS

    def body(chip_ref, x_ref, t_ref, cat_ref, v_ref, g_out, g_w1, g_w2, l_out, l_w1, l_w2,
             dx1_ref, dcat_ref, dmix_ref, h2_ref, r_ref, da_ref, dff_ref, sums_ref,
             wout, w1, w2, a_scr, sem):
        i = pl.program_id(0)

        @pl.when(i == 0)
        def _():
            waits = _load_chip_blocks(chip_ref, g_out, l_out,
                                      [wout.at[pl.ds(out_rows * k, out_rows)] for k in range(N_CHIPS)], sem)
            waits += _load_chip_blocks(chip_ref, g_w1, l_w1, [w1.at[k] for k in range(N_CHIPS)], sem, N_CHIPS)
            waits += _load_chip_blocks(chip_ref, g_w2, l_w2, [w2.at[k] for k in range(N_CHIPS)], sem, 2 * N_CHIPS)
            for wait in waits:
                wait()
            sums_ref[...] = jnp.zeros_like(sums_ref)

        gate1, shift2, scale2 = v_ref[0:1, :], v_ref[1:2, :], v_ref[2:3, :]
        gate2, g_ffn, g_final = v_ref[3:4, :], v_ref[4:5, :], v_ref[5:6, :]

        mix = _dot(cat_ref[...], wout[...])
        x1 = x_ref[...] + gate1 * mix
        rstd2 = lax.rsqrt(_mean_last(x1 * x1) + EPS)
        xh2 = x1 * rstd2
        n2 = xh2 * g_ffn
        h2b = (n2 * (1.0 + scale2) + shift2).astype(MXU_DTYPE)
        h2_ref[...] = h2b
        ff = jnp.zeros((tm, D_MODEL), F32)
        for j in range(nj):
            a = _dot(h2b, w1[j])
            a_scr[j] = a
            relu = jnp.maximum(a, 0.0)
            rb = (relu * relu).astype(MXU_DTYPE)
            r_ref[:, D_MODEL * j:D_MODEL * (j + 1)] = rb
            ff = ff + _dot(rb, w2[j])
        x2 = x1 + gate2 * ff
        rstd3 = lax.rsqrt(_mean_last(x2 * x2) + EPS)
        xh3 = x2 * rstd3
        err = xh3 * g_final - t_ref[...]
        loss = 0.5 * _rowsum(_mean_last(err * err))
        dy = err * (1.0 / D_MODEL)
        dxh3 = dy * g_final
        dx2 = rstd3 * (dxh3 - xh3 * _mean_last(dxh3 * xh3))
        dffb = (dx2 * gate2).astype(MXU_DTYPE)
        dff_ref[...] = dffb
        dh2 = jnp.zeros((tm, D_MODEL), F32)
        for j in range(nj):
            dr = _dot_nt(dffb, w2[j])
            dab = (dr * (2.0 * jnp.maximum(a_scr[j], 0.0))).astype(MXU_DTYPE)
            da_ref[:, D_MODEL * j:D_MODEL * (j + 1)] = dab
            dh2 = dh2 + _dot_nt(dab, w1[j])
        dn2 = dh2 * (1.0 + scale2)
        dxh2 = dn2 * g_ffn
        dx1 = dx2 + rstd2 * (dxh2 - xh2 * _mean_last(dxh2 * xh2))
        dx1_ref[...] = dx1
        dmixb = (dx1 * gate1).astype(MXU_DTYPE)
        dmix_ref[...] = dmixb
        dcat_ref[...] = _dot_nt(dmixb, wout[...])

        sums_ref[0:1, :] += _rowsum(dh2)
        sums_ref[1:2, :] += _rowsum(dh2 * n2)
        sums_ref[2:3, :] += _rowsum(dx2 * ff)
        sums_ref[3:4, :] += _rowsum(dn2 * xh2)
        sums_ref[4:5, :] += _rowsum(dy * xh3)
        sums_ref[5:6, :] += _rowsum(dx1 * mix)
        sums_ref[6:7, :] += jnp.broadcast_to(loss, (1, D_MODEL))

    tok = lambda w: pl.BlockSpec((tm, w), lambda i, chip: (i, 0))
    return _hosted_call(
        body, None, name="trunk", grid=(seq // tm,), n_prefetch=1,
        out_shape=[jax.ShapeDtypeStruct((seq, D_MODEL), F32), jax.ShapeDtypeStruct((seq, D_MODEL), F32),
                   jax.ShapeDtypeStruct((seq, D_MODEL), MXU_DTYPE), jax.ShapeDtypeStruct((seq, D_MODEL), MXU_DTYPE),
                   jax.ShapeDtypeStruct((seq, D_FF), MXU_DTYPE), jax.ShapeDtypeStruct((seq, D_FF), MXU_DTYPE),
                   jax.ShapeDtypeStruct((seq, D_MODEL), MXU_DTYPE), jax.ShapeDtypeStruct((8, D_MODEL), F32)],
        in_specs=[tok(D_MODEL), tok(D_MODEL), tok(D_MODEL), _full((8, D_MODEL))] + [_any()] * 6,
        out_specs=[tok(D_MODEL), tok(D_MODEL), tok(D_MODEL), tok(D_MODEL), tok(D_FF), tok(D_FF), tok(D_MODEL),
                   _full((8, D_MODEL))],
        scratch_shapes=[pltpu.VMEM((D_MODEL, D_MODEL), MXU_DTYPE), pltpu.VMEM((nj, D_MODEL, D_MODEL), MXU_DTYPE),
                        pltpu.VMEM((nj, D_MODEL, D_MODEL), MXU_DTYPE), pltpu.VMEM((nj, tm, D_MODEL), F32),
                        pltpu.SemaphoreType.DMA((3 * N_CHIPS,))],
        semantics=("arbitrary",),
    )(chip_idx, x, target, cat, vecs, *gathered, *local)


def _mixer_bwd_kernel(proj, rope_tab, dcat, w_spatial, w_spatial_t, bias_full, sink_rows, dev_idx, comm=None):
    seq = proj.shape[0]
    per = MIXER_BLOCKS_PER_STEP
    steps = seq // (CHUNK * per)
    kv_col = KV_START // (2 * KV_WIDTH)

    def body(dev_ref, proj_ref, prev_ref, tab_ref, ptab_ref, dcat_ref, w_ref, wt_ref, bias_ref, sink_ref,
             dproj_ref, dw_ref, db_ref, dsink_ref, carry):
        del dev_ref
        step = pl.program_id(0)

        @pl.when(step == 0)
        def _():
            carry[...] = jnp.zeros_like(carry)
            dw_ref[...] = jnp.zeros_like(dw_ref)
            db_ref[...] = jnp.zeros_like(db_ref)
            dsink_ref[...] = jnp.zeros_like(dsink_ref)

        for s in reversed(range(per)):
            rows = pl.ds(CHUNK * s, CHUNK)
            if s == 0:
                before, before_tab, first = prev_ref, ptab_ref, step == steps - 1
            else:
                before = proj_ref.at[pl.ds(CHUNK * (s - 1), CHUNK), pl.ds(KV_START, 2 * KV_WIDTH)]
                before_tab, first = tab_ref.at[pl.ds(CHUNK * (s - 1), CHUNK)], None
            one_block(proj_ref.at[rows], before, tab_ref.at[rows], before_tab, dcat_ref.at[rows], w_ref, wt_ref,
                      bias_ref, sink_ref, dproj_ref.at[rows], dw_ref, db_ref, dsink_ref, carry, first)

    def one_block(proj_ref, prev_ref, tab_ref, ptab_ref, dcat_ref, w_ref, wt_ref, bias_ref, sink_ref,
                  dproj_ref, dw_ref, db_ref, dsink_ref, carry, first):
        wm, tril, triu = _masked_spatial(w_ref)
        lo, hi = _lane_masks((CHUNK, LANES))
        lane = lax.broadcasted_iota(jnp.int32, (CHUNK, LANES), 1)
        db = jnp.zeros((CHUNK, LANES), F32)
        for j in range(GMLP_GROUPS // 2):
            cols = slice(LANES * j, LANES * (j + 1))
            vcols = slice(GMLP_WIDTH + LANES * j, GMLP_WIDTH + LANES * (j + 1))
            zu, zv = proj_ref[:, cols], proj_ref[:, vcols]
            u, tu = _gelu_tanh(zu)
            vp, tv = _gelu_tanh(zv)
            sv = _sgu_forward_pair(wm, vp, j) + bias_ref[:, cols]
            dout = dcat_ref[:, cols]
            du = dout * sv
            dsv = dout * u
            dsv_lo, dsv_hi = jnp.where(lo, dsv, 0.0), jnp.where(hi, dsv, 0.0)
            lhs_t = jnp.concatenate([jnp.where(triu, wt_ref[2 * j], 0.0),
                                     jnp.where(triu, wt_ref[2 * j + 1], 0.0)], axis=1)
            dv = _dot(lhs_t, jnp.concatenate([dsv_lo, dsv_hi], axis=0))
            dw_ref[2 * j] += jnp.where(tril, _dot_nt(dsv_lo, vp), 0.0)
            dw_ref[2 * j + 1] += jnp.where(tril, _dot_nt(dsv_hi, vp), 0.0)
            db = db + (jnp.where(lane == 2 * j, jnp.sum(dsv_lo, axis=1, keepdims=True), 0.0)
                       + jnp.where(lane == 2 * j + 1, jnp.sum(dsv_hi, axis=1, keepdims=True), 0.0))
            dproj_ref[:, cols] = (du * _gelu_tanh_grad(zu, tu)).astype(dproj_ref.dtype)
            dproj_ref[:, vcols] = (dv * _gelu_tanh_grad(zv, tv)).astype(dproj_ref.dtype)
        db_ref[...] += db
        o = 2 * GMLP_WIDTH
        tab = tab_ref[...]
        q_r = _rope_apply(proj_ref[:, o:o + ATTN_WIDTH], tab, 1.0)
        k_cur = _rope_apply(proj_ref[:, o + ATTN_WIDTH:o + ATTN_WIDTH + KV_WIDTH], tab, 1.0)
        k_prev = _rope_apply(prev_ref[:, 0:KV_WIDTH], ptab_ref[...], 1.0)
        k_a = jnp.concatenate([k_prev, k_cur], axis=0)
        v_a = jnp.concatenate([prev_ref[:, KV_WIDTH:2 * KV_WIDTH],
                               proj_ref[:, o + ATTN_WIDTH + KV_WIDTH:o + ATTN_WIDTH + 2 * KV_WIDTH]], axis=0)
        k_b = pltpu.roll(k_a, HEAD_DIM, 1)
        v_b = pltpu.roll(v_a, HEAD_DIM, 1)
        bias_t = _attn_bias_t(first)
        lo2, _ = _lane_masks((2 * CHUNK, LANES))
        dout_b = dcat_ref[:, GMLP_WIDTH:GMLP_WIDTH + ATTN_WIDTH]
        dk_tot, dv_tot, dq_pairs = [], [], []
        for g in range(N_KV_HEADS):
            k_dup, v_dup = _group_dup(k_a, k_b, g, lo2), _group_dup(v_a, v_b, g, lo2)
            q_rows = _group_rows(q_r, g, lo, hi)
            do_rows = _group_rows(dout_b, g, lo, hi)
            p_t, p_sink = _attn_probs_t(k_dup, q_rows, bias_t, _sink_row(sink_ref, g))
            dp_t = _dot_nt(v_dup, do_rows)
            delta = jnp.sum(p_t * dp_t, axis=0, keepdims=True)
            ds_t = p_t * (dp_t - delta) * ATTN_SCALE
            dsink = -p_sink * delta
            for r in range(HEADS_PER_GROUP):
                h = HEADS_PER_GROUP * g + r
                dsink_ref[h:h + 1, :] += jnp.broadcast_to(
                    jnp.sum(dsink[:, LANES * r:LANES * (r + 1)], axis=1, keepdims=True), (1, LANES))
            dk_full = _dot(ds_t, q_rows)
            dv_full = _dot(p_t, do_rows)
            dk_tot.append(dk_full + pltpu.roll(dk_full, HEAD_DIM, 1))
            dv_tot.append(dv_full + pltpu.roll(dv_full, HEAD_DIM, 1))
            dq_t = _dot(k_dup.T, ds_t)
            dq_pairs += _pairs_from_rows(dq_t.T, lo)
        dk_all = jnp.where(lo2, dk_tot[0], dk_tot[1])
        dv_all = jnp.where(lo2, dv_tot[0], dv_tot[1])
        dk_cur = dk_all[CHUNK:, :] + carry[:, 0:KV_WIDTH]
        dv_cur = dv_all[CHUNK:, :] + carry[:, KV_WIDTH:2 * KV_WIDTH]
        carry[:, 0:KV_WIDTH] = dk_all[:CHUNK, :]
        carry[:, KV_WIDTH:2 * KV_WIDTH] = dv_all[:CHUNK, :]
        dq = _rope_apply(jnp.concatenate(dq_pairs, axis=1), tab, -1.0)
        dproj_ref[:, o:o + ATTN_WIDTH] = dq.astype(dproj_ref.dtype)
        dproj_ref[:, o + ATTN_WIDTH:o + ATTN_WIDTH + KV_WIDTH] = (
            _rope_apply(dk_cur, tab, -1.0).astype(dproj_ref.dtype))
        dproj_ref[:, o + ATTN_WIDTH + KV_WIDTH:o + ATTN_WIDTH + 2 * KV_WIDTH] = dv_cur.astype(dproj_ref.dtype)

    rev = lambda i: steps - 1 - i
    before = lambda i: jnp.maximum(per * rev(i) - 1, 0)
    slot = lambda shape: pl.BlockSpec((None,) + shape, lambda i, d: (d[0],) + (0,) * len(shape))
    return _hosted_call(
        body, comm, name="mixer_bwd", grid=(steps,), n_prefetch=1,
        out_shape=[jax.ShapeDtypeStruct((seq, IN_PROJ_WIDTH), MXU_DTYPE),
                   jax.ShapeDtypeStruct((N_DEV, GMLP_GROUPS, CHUNK, CHUNK), F32),
                   jax.ShapeDtypeStruct((N_DEV, CHUNK, LANES), F32),
                   jax.ShapeDtypeStruct((N_DEV, N_Q_HEADS, LANES), F32)],
        in_specs=[pl.BlockSpec((CHUNK * per, IN_PROJ_WIDTH), lambda i, d: (rev(i), 0)),
                  pl.BlockSpec((CHUNK, 2 * KV_WIDTH), lambda i, d: (before(i), kv_col)),
                  pl.BlockSpec((CHUNK * per, 3 * LANES), lambda i, d: (rev(i), 0)),
                  pl.BlockSpec((CHUNK, 3 * LANES), lambda i, d: (before(i), 0)),
                  pl.BlockSpec((CHUNK * per, D_MODEL), lambda i, d: (rev(i), 0)),
                  _full((GMLP_GROUPS, CHUNK, CHUNK)), _full((GMLP_GROUPS, CHUNK, CHUNK)),
                  _full((CHUNK, GMLP_WIDTH)), _full((N_Q_HEADS, LANES))],
        out_specs=[pl.BlockSpec((CHUNK * per, IN_PROJ_WIDTH), lambda i, d: (rev(i), 0)),
                   slot((GMLP_GROUPS, CHUNK, CHUNK)), slot((CHUNK, LANES)), slot((N_Q_HEADS, LANES))],
        scratch_shapes=[pltpu.VMEM((CHUNK, 2 * KV_WIDTH), F32)],
        semantics=("arbitrary",),
    )(dev_idx, proj, proj, rope_tab, rope_tab, dcat, w_spatial, w_spatial_t, bias_full, sink_rows)


def _in_proj_bwd_kernel(x, dx1, dproj, vecs, chip_idx, w_in_gathered, w_in_local, comm=None):
    seq = x.shape[0]
    tm = 512

    def body(chip_ref, x_ref, dx1_ref, dp_ref, v_ref, wg_ref, wl_ref, gx_ref, sums_ref, w_ref, sems):
        @pl.when(pl.program_id(0) == 0)
        def _():
            for wait in _load_chip_blocks(chip_ref, wg_ref, wl_ref, _w_in_rows(w_ref), sems):
                wait()
            sums_ref[...] = jnp.zeros_like(sums_ref)

        g_mix, scale1 = v_ref[0:1, :], v_ref[2:3, :]
        dh = _dot(dp_ref[...], w_ref[...])
        xv = x_ref[...]
        rstd = lax.rsqrt(_mean_last(xv * xv) + EPS)
        xh = xv * rstd
        dn1 = dh * (1.0 + scale1)
        dxh = dn1 * g_mix
        gx_ref[...] = dx1_ref[...] + rstd * (dxh - xh * _mean_last(dxh * xh))
        sums_ref[0:1, :] += _rowsum(dh)
        sums_ref[1:2, :] += _rowsum(dh * (xh * g_mix))
        sums_ref[2:3, :] += _rowsum(dn1 * xh)

    tok = lambda w: pl.BlockSpec((tm, w), lambda i, chip: (i, 0))
    return _hosted_call(
        body, comm, name="in_proj_bwd", grid=(seq // tm,), n_prefetch=1,
        out_shape=[jax.ShapeDtypeStruct((seq, D_MODEL), F32), jax.ShapeDtypeStruct((8, D_MODEL), F32)],
        in_specs=[tok(D_MODEL), tok(D_MODEL), tok(IN_PROJ_WIDTH), _full((8, D_MODEL)), _any(), _any()],
        out_specs=[tok(D_MODEL), _full((8, D_MODEL))],
        scratch_shapes=[pltpu.VMEM((IN_PROJ_WIDTH, D_MODEL), MXU_DTYPE), pltpu.SemaphoreType.DMA((N_CHIPS,))],
        semantics=("arbitrary",),
    )(chip_idx, x, dx1, dproj, vecs, w_in_gathered, w_in_local)


class _GradTiles(NamedTuple):
    tm: int
    tn: int
    n_tiles: int
    chips_per_tile: int
    a_index: Callable
    b_index: Callable


def _weight_grad_kernel(a, b, c_idx, name, tiles, comm=None):
    seq = a.shape[0]
    tk = min(seq, 4096)
    nk = seq // tk
    tm, tn, n_tiles, per = tiles.tm, tiles.tn, tiles.n_tiles, tiles.chips_per_tile
    rows = tm // per

    def half(phase, c):
        return phase * c[0] + (1 - phase) * (1 - c[0])

    def body(c_ref, a_ref, b_ref, o_ref, acc, stage, landed, send_sems, recv_sems):
        del c_ref
        phase, t, kk = pl.program_id(0), pl.program_id(1), pl.program_id(2)
        x, y, c, _ = _mesh_place()

        def copy(tile):
            return pltpu.make_async_remote_copy(
                src_ref=stage.at[tile], dst_ref=landed.at[tile], send_sem=send_sems.at[tile],
                recv_sem=recv_sems.at[tile], device_id=(x, y, 1 - c), device_id_type=MESH)

        @pl.when(kk == 0)
        def _():
            acc[...] = jnp.zeros_like(acc)

        acc[...] += _dot_tn(a_ref[...], b_ref[...])

        @pl.when((kk == nk - 1) & (phase == 0))
        def _():
            stage[t] = acc[...].astype(stage.dtype)
            copy(t).start()

        @pl.when((kk == nk - 1) & (phase == 1))
        def _():
            copy(t).wait_recv()
            total = acc[...] + landed[t].astype(F32)
            for q in range(per):
                o_ref[q] = total[rows * q:rows * (q + 1)].astype(o_ref.dtype)

        @pl.when((kk == nk - 1) & (phase == 1) & (t == n_tiles - 1))
        def _():
            for tile in range(n_tiles):
                copy(tile).wait_send()

    out = _hosted_call(
        body, comm, name=name, grid=(2, n_tiles, nk), n_prefetch=1,
        out_shape=[jax.ShapeDtypeStruct((n_tiles * per, rows, tn), GRAD_COMM_DTYPE)],
        in_specs=[pl.BlockSpec((tk, tm), lambda p, t, k, c: (k, tiles.a_index(t, half(p, c)))),
                  pl.BlockSpec((tk, tn), lambda p, t, k, c: (k, tiles.b_index(t, half(p, c))))],
        out_specs=[pl.BlockSpec((per, rows, tn), lambda p, t, k, c: (p * t, 0, 0))],
        scratch_shapes=[pltpu.VMEM((tm, tn), F32), pltpu.VMEM((n_tiles, tm, tn), GRAD_COMM_DTYPE),
                        pltpu.VMEM((n_tiles, tm, tn), GRAD_COMM_DTYPE),
                        pltpu.SemaphoreType.DMA((n_tiles,)), pltpu.SemaphoreType.DMA((n_tiles,))],
        semantics=("arbitrary", "arbitrary", "arbitrary"),
    )(c_idx, a, b)
    return out[0] if comm is None else out


def _row_tile(rows, most=256, sublanes=16):
    return max(t for t in range(sublanes, most + 1, sublanes) if rows % t == 0)


def _adam_update(w, g, m, v):
    m_new = ADAM_B1 * m + (1.0 - ADAM_B1) * g
    v_new = ADAM_B2 * v + (1.0 - ADAM_B2) * (g * g)
    m_hat = m_new / (1.0 - ADAM_B1 ** ADAM_STEP)
    v_hat = v_new / (1.0 - ADAM_B2 ** ADAM_STEP)
    delta = -ADAM_LR * (m_hat / (jnp.sqrt(v_hat) + ADAM_EPS) + ADAM_WD * w)
    return delta, m_new, v_new


def _sum_chips_kernel(own, others, place, name):
    _, r, n = own.shape
    tr = _row_tile(r)

    def body(place_ref, own_ref, oth_ref, o_ref):
        del place_ref
        acc = own_ref[...].astype(F32)
        for k in range(N_CHIPS - 1):
            acc = acc + oth_ref[k].astype(F32)
        o_ref[...] = acc

    return pl.pallas_call(
        body, name=name, out_shape=jax.ShapeDtypeStruct((2, r, n), F32),
        grid_spec=pltpu.PrefetchScalarGridSpec(
            num_scalar_prefetch=1, grid=(r // tr,),
            in_specs=[pl.BlockSpec((None, tr, n), lambda i, p: (p[0], i, 0)),
                      pl.BlockSpec((N_CHIPS - 1, tr, n), lambda i, p: (0, i, 0))],
            out_specs=pl.BlockSpec((None, tr, n), lambda i, p: (p[1], i, 0))),
        compiler_params=_params("parallel"),
    )(place, own, others)


def _adam_kernel(w, g, m, v, name):
    r, n = w.shape
    by_columns = g.shape[1] == r
    tr, tn = _row_tile(g.shape[1], most=512), g.shape[2]

    def body(w_ref, g_ref, m_ref, v_ref, g_out, d_ref, mo_ref, vo_ref):
        gv = g_ref[...]
        g_out[...] = gv
        d_ref[...], mo_ref[...], vo_ref[...] = _adam_update(w_ref[...], gv, m_ref[...], v_ref[...])

    steps = g.shape[1] // tr
    spec = pl.BlockSpec((tr, tn), (lambda h, i: (i, h)) if by_columns else (lambda h, i: (h * steps + i, 0)))
    return pl.pallas_call(
        body, name=name, grid=(2, steps), out_shape=[jax.ShapeDtypeStruct((r, n), F32)] * 4,
        in_specs=[spec, pl.BlockSpec((None, tr, tn), lambda h, i: (h, i, 0)), spec, spec], out_specs=[spec] * 4,
        compiler_params=_params("parallel", "parallel"),
    )(w, g, m, v)


SMALL_PARAMS = ("b_ada", "g_mix", "g_ffn", "g_final", "b_spatial", "sinks", "w_spatial")


def _small_update_kernel(gathered, params):
    shapes = [params[nm][0].shape for nm in SMALL_PARAMS]

    def body(*refs):
        g_refs, refs = refs[:5], refs[5:]
        p_refs, refs = refs[:3 * len(SMALL_PARAMS)], refs[3 * len(SMALL_PARAMS):]
        loss_ref, o_refs = refs[0], refs[1:]

        def total(ref):
            acc = ref[0]
            for k in range(1, N_DEV):
                acc = acc + ref[k]
            return acc

        s1, s2, db, ds, dw = (total(r) for r in g_refs)
        loss_ref[...] = jnp.broadcast_to(s2[6:7, 0:1], loss_ref.shape)
        grads = {"b_ada": [s1[0:1], s1[1:2], s2[5:6], s2[0:1], s2[1:2], s2[2:3]], "g_mix": [s1[2:3]],
                 "g_ffn": [s2[3:4]], "g_final": [s2[4:5]], "b_spatial": [db.T[0:GMLP_GROUPS]],
                 "w_spatial": [dw]}
        lane = lax.broadcasted_iota(jnp.int32, (1, LANES), 1)
        sink_row = jnp.zeros((1, LANES), F32)
        for h in range(N_Q_HEADS):
            sink_row = sink_row + jnp.where(lane == h, ds[h:h + 1, :], 0.0)
        grads["sinks"] = [sink_row[:, 0:N_Q_HEADS]]
        for i, nm in enumerate(SMALL_PARAMS):
            w_ref, m_ref, v_ref = p_refs[3 * i:3 * i + 3]
            outs = o_refs[4 * i:4 * i + 4]
            width = grads[nm][0].shape[1]
            for k, g in enumerate(grads[nm]):
                cols = slice(width * k, width * (k + 1))
                upd = _adam_update(w_ref[:, cols], g, m_ref[:, cols], v_ref[:, cols])
                for o_ref, val in zip(outs, (g,) + upd):
                    o_ref[:, cols] = val

    flat = [a for nm in SMALL_PARAMS for a in params[nm]]
    out_shape = [jax.ShapeDtypeStruct((8, LANES), F32)]
    out_shape += [jax.ShapeDtypeStruct(s, F32) for s in shapes for _ in range(4)]
    outs = pl.pallas_call(
        body, name="small_update", grid=(1,), out_shape=out_shape,
        in_specs=[_full(g.shape) for g in gathered] + [_full(a.shape) for a in flat],
        out_specs=[_full(s.shape) for s in out_shape],
        compiler_params=_params("arbitrary"),
    )(*gathered, *flat)
    return {nm: outs[1 + 4 * i:5 + 4 * i] for i, nm in enumerate(SMALL_PARAMS)}, outs[0]


def _ada_update_kernel(act_t, dmod, w, m, v):
    r, n = w.shape
    tr = 256

    def body(a_ref, d_ref, w_ref, m_ref, v_ref, g_ref, dl_ref, mo_ref, vo_ref):
        g = _dot(a_ref[...], d_ref[...])
        g_ref[...] = g
        dl_ref[...], mo_ref[...], vo_ref[...] = _adam_update(w_ref[...], g, m_ref[...], v_ref[...])

    spec = pl.BlockSpec((tr, n), lambda i: (i, 0))
    return pl.pallas_call(
        body, name="ada_update", grid=(r // tr,), out_shape=[jax.ShapeDtypeStruct((r, n), F32)] * 4,
        in_specs=[pl.BlockSpec((tr, N_DEV), lambda i: (i, 0)), _full((N_DEV, n)), spec, spec, spec],
        out_specs=[spec] * 4, compiler_params=_params("parallel"),
    )(act_t, dmod, w, m, v)


def kernel(x, c, positions, w_ada, b_ada, g_mix, w_in, w_spatial, b_spatial, sinks, w_out, g_ffn, w_ff1, w_ff2, g_final, loss_target, m_w_ada, m_b_ada, m_g_mix, m_w_in, m_w_spatial, m_b_spatial, m_sinks, m_w_out, m_g_ffn, m_w_ff1, m_w_ff2, m_g_final, v_w_ada, v_b_ada, v_g_mix, v_w_in, v_w_spatial, v_b_spatial, v_sinks, v_w_out, v_g_ffn, v_w_ff1, v_w_ff2, v_g_final):
    xi, yi, ci = lax.axis_index("x"), lax.axis_index("y"), lax.axis_index("c")
    chip = 2 * xi + yi
    dev = 2 * chip + ci
    seq = x.shape[1]
    x2, tgt = x[0], loss_target[0]
    ada_cols = w_ada.shape[2]

    big = {"w_in": tuple(a[0].T for a in (w_in, m_w_in, v_w_in)),
           "w_out": (w_out[0], m_w_out[0], v_w_out[0]), "w_ff1": (w_ff1[0], m_w_ff1[0], v_w_ff1[0]),
           "w_ff2": (w_ff2[0], m_w_ff2[0], v_w_ff2[0])}

    def halves(nm):
        r, n = big[nm][0].shape
        return big[nm][0].astype(WEIGHT_COMM_DTYPE).reshape(2, r // 2, n)

    chip_idx = chip.reshape(1).astype(jnp.int32)
    w_in_local = halves("w_in")
    c_all, w_in_g, g_out = _all_gather8([c, w_in_local, halves("w_out")], "gather_first",
                                        split=[False, True, True], skip_own=(1, 2))
    c_all = c_all.reshape(N_DEV, D_MODEL)
    w_in_g, w_in_local = w_in_g.reshape(N_CHIPS, W_IN_BLOCK, D_MODEL), w_in_local.reshape(W_IN_BLOCK, D_MODEL)
    b_shard = lax.dynamic_slice(b_ada, (0, chip * ada_cols), (1, ada_cols))
    mod_part, act = _mod_kernel(c_all, w_ada[0], b_shard)
    mod_all, = _all_gather8([mod_part], "gather_mod")
    mod_me = lax.dynamic_index_in_dim(mod_all[0::2], dev, axis=1, keepdims=False)
    mod_me = mod_me.reshape(N_MOD, D_MODEL)
    shift1, scale1, gate1, shift2, scale2, gate2 = (mod_me[k:k + 1] for k in range(N_MOD))

    zeros_row = jnp.zeros((1, D_MODEL), F32)
    vecs1 = jnp.concatenate([g_mix, shift1, scale1] + [zeros_row] * 5, axis=0)
    vecs2 = jnp.concatenate([gate1, shift2, scale2, gate2, g_ffn, g_final.reshape(1, D_MODEL)]
                            + [zeros_row] * 2, axis=0)
    bias_full = jnp.repeat(b_spatial[0].T, HEAD_DIM, axis=1)
    sink_rows = jnp.broadcast_to(sinks[0][:, None], (N_Q_HEADS, LANES))
    inv_freq = ROPE_THETA ** (-jnp.arange(0, ROT_DIM, 2, dtype=F32) / ROT_DIM)
    rope_tab = _rope_lane_tables(*_rope_angle_kernel(positions, inv_freq.reshape(ROT_DIM // 2, 1)))

    trunk_weights = ["w_out", "w_ff1", "w_ff2"]
    shards = [halves(nm) for nm in trunk_weights]
    proj, hb, *staged = _in_proj_kernel(x2, vecs1, chip_idx, w_in_g, w_in_local,
                                        comm=_gather2d_first(shards[1:]))
    cat, *staged = _mixer_fwd_kernel(proj, rope_tab, w_spatial[0], bias_full, sink_rows,
                                     comm=_gather2d_second(staged, shards[1:]))
    staged = [g_out] + list(_gather_forward(staged, "gather_forward"))
    dx1, dcat, dmix, h2b, rb, dab, dffb, sums2 = _trunk_kernel(
        x2, tgt, cat, vecs2, chip_idx,
        [g.reshape((N_CHIPS,) + big[nm][0].shape) for nm, g in zip(trunk_weights, staged)],
        [s.reshape(big[nm][0].shape) for nm, s in zip(trunk_weights, shards)])

    c_idx = ci.reshape(1).astype(jnp.int32)
    place = jnp.stack([chip, ci]).astype(jnp.int32)
    half_d = D_MODEL // 2
    cs_ff2 = _weight_grad_kernel(rb, dffb, c_idx, "dw_ff2",
                                 _GradTiles(D_MODEL, half_d, N_CHIPS, 1, lambda t, h: t, lambda t, h: h))
    cs_ff1, sc_ff2 = _weight_grad_kernel(
        h2b, dab, c_idx, "dw_ff1",
        _GradTiles(D_MODEL, half_d, N_CHIPS, 1, lambda t, h: 0, lambda t, h: 2 * t + h),
        comm=_scatter_job([cs_ff2]))
    cs_out = _weight_grad_kernel(cat, dmix, c_idx, "dw_out",
                                 _GradTiles(D_MODEL, half_d, 1, N_CHIPS, lambda t, h: 0, lambda t, h: h))
    dproj, dw_spatial, db_lanes, dsink_rows, sc_ff1, sc_out = _mixer_bwd_kernel(
        proj, rope_tab, dcat, w_spatial[0], w_spatial[0].transpose(0, 2, 1), bias_full, sink_rows,
        dev.reshape(1).astype(jnp.int32), comm=_scatter_job([cs_ff1, cs_out]))
    totals = [_sum_chips_kernel(own, oth, place, "grad_sum_" + nm)
              for nm, own, oth in (("w_out", cs_out, sc_out), ("w_ff1", cs_ff1, sc_ff1), ("w_ff2", cs_ff2, sc_ff2))]
    small_slots = [db_lanes, dsink_rows, dw_spatial.reshape(N_DEV, GMLP_GROUPS * CHUNK, CHUNK)]
    cs_in, *rode = _weight_grad_kernel(
        dproj, hb, c_idx, "dw_in",
        _GradTiles(2 * W_IN_BLOCK, half_d, N_CHIPS // 2, 2, lambda t, h: t, lambda t, h: h),
        comm=_merge_in_place(_gather_job(small_slots), _share_job(totals)))
    small_stage1, shared = rode[:len(small_slots)], rode[len(small_slots):]
    grad_x, sums1 = _in_proj_bwd_kernel(x2, dx1, dproj, vecs1, chip_idx, w_in_g, w_in_local)
    *gathered, sc_in = _all_gather8([sums1, sums2], "gather_small", forward=small_stage1,
                                    riders=[_scatter_job([cs_in])])
    total_in = _sum_chips_kernel(cs_in, sc_in, place, "grad_sum_w_in")
    shared = list(_sibling_share([total_in], "grad_share_w_in")) + list(shared)
    names = ["w_in", "w_out", "w_ff1", "w_ff2"]
    big_out = {}
    for nm, g in zip(names, shared):
        w, m, v = big[nm]
        outs = _adam_kernel(w, g, m, v, "adam_" + nm)
        big_out[nm] = tuple((t.T if nm == "w_in" else t)[None] for t in outs)

    small = {"b_ada": (b_ada, m_b_ada, v_b_ada), "g_mix": (g_mix, m_g_mix, v_g_mix),
             "g_ffn": (g_ffn, m_g_ffn, v_g_ffn), "g_final": (g_final, m_g_final, v_g_final),
             "b_spatial": (b_spatial, m_b_spatial, v_b_spatial), "sinks": (sinks, m_sinks, v_sinks),
             "w_spatial": (w_spatial, m_w_spatial, v_w_spatial)}
    flat_shape = {"g_final": (1, D_MODEL), "b_spatial": (GMLP_GROUPS, CHUNK), "w_spatial": (GMLP_GROUPS * CHUNK, CHUNK)}
    small_out, loss_tile = _small_update_kernel(
        gathered, {nm: tuple(a.reshape(flat_shape.get(nm, a.shape)) for a in small[nm]) for nm in small})
    small_out = {nm: [o.reshape(small[nm][0].shape) for o in small_out[nm]] for nm in small}
    loss = loss_tile[0, 0]

    g1, g2 = gathered[0], gathered[1]
    dmod_all = jnp.concatenate([g1[:, 0], g1[:, 1], g2[:, 5], g2[:, 0], g2[:, 1], g2[:, 2]], axis=1)
    dmod_cols = lax.dynamic_slice(dmod_all, (0, chip * ada_cols), (N_DEV, ada_cols))
    ada = _ada_update_kernel(act.T, dmod_cols, w_ada[0], m_w_ada[0], v_w_ada[0])
    big_out["w_ada"] = tuple(t[None] for t in ada)

    order = ["w_ada", "b_ada", "g_mix", "w_in", "w_spatial", "b_spatial", "sinks", "w_out", "g_ffn",
             "w_ff1", "w_ff2", "g_final"]

    def leaf(nm, k):
        return big_out[nm][k] if nm in big_out else small_out[nm][k]

    outs = [loss, grad_x[None]]
    for k in range(4):
        outs += [leaf(nm, k) for nm in order]
    return tuple(outs)
```

```python
import math
from typing import Callable, NamedTuple

import jax
import jax.numpy as jnp
from jax import lax
from jax.experimental import pallas as pl
from jax.experimental.pallas import tpu as pltpu

F32 = jnp.float32
MXU_DTYPE = jnp.bfloat16
WEIGHT_COMM_DTYPE = jnp.bfloat16
GRAD_COMM_DTYPE = jnp.bfloat16

D_MODEL = 1024
D_FF = 4096
HEAD_DIM = 64
GMLP_GROUPS = 8
GMLP_WIDTH = 512
CHUNK = 128
N_Q_HEADS = 8
N_KV_HEADS = 2
ATTN_WIDTH = 512
KV_WIDTH = 128
ROT_DIM = 16
ROPE_THETA = 500000.0
IN_PROJ_WIDTH = 1792
N_MOD = 6
EPS = 1e-5
N_CHIPS = 4
N_DEV = 8
LANES = 128
W_IN_BLOCK = IN_PROJ_WIDTH // N_CHIPS

ADAM_LR = 0.001
ADAM_B1 = 0.9
ADAM_B2 = 0.999
ADAM_EPS = 1e-08
ADAM_WD = 0.01
ADAM_STEP = 10

VMEM_LIMIT_BYTES = 58 * 1024 * 1024
MESH = pl.DeviceIdType.MESH


def _params(*semantics):
    return pltpu.CompilerParams(dimension_semantics=semantics, vmem_limit_bytes=VMEM_LIMIT_BYTES)


def _dot(a, b):
    return jnp.dot(a.astype(MXU_DTYPE), b.astype(MXU_DTYPE), preferred_element_type=F32)


def _dot_nt(a, b):
    return lax.dot_general(a.astype(MXU_DTYPE), b.astype(MXU_DTYPE), (((1,), (1,)), ((), ())),
                           preferred_element_type=F32)


def _dot_tn(a, b):
    return lax.dot_general(a.astype(MXU_DTYPE), b.astype(MXU_DTYPE), (((0,), (0,)), ((), ())),
                           preferred_element_type=F32)


def _full(shape):
    return pl.BlockSpec(shape, lambda *_: (0,) * len(shape))


def _any():
    return pl.BlockSpec(memory_space=pl.ANY)


def _rowsum(v):
    return jnp.sum(v, axis=0, keepdims=True)


def _mean_last(v):
    return jnp.mean(v, axis=-1, keepdims=True)


class _Comm(NamedTuple):
    operands: tuple
    out_shapes: tuple
    n_sems: int
    make: Callable
    in_place: int = 0


def _hosted_call(body, comm, *, name, grid, in_specs, out_shape, out_specs, scratch_shapes=(), semantics,
                 n_prefetch=0):
    if comm is None:
        return pl.pallas_call(
            body, name=name, out_shape=out_shape, compiler_params=_params(*semantics),
            grid_spec=pltpu.PrefetchScalarGridSpec(
                num_scalar_prefetch=n_prefetch, grid=grid, in_specs=in_specs, out_specs=out_specs,
                scratch_shapes=list(scratch_shapes)))
    n_in, n_out, n_scr = len(in_specs), len(out_shape), len(scratch_shapes)
    k_in, k_out = len(comm.operands), len(comm.out_shapes)

    def hosted(*refs):
        prefetched, refs = refs[:n_prefetch], refs[n_prefetch:]
        ins, refs = refs[:n_in], refs[n_in:]
        c_ins, refs = refs[:k_in], refs[k_in:]
        outs, refs = refs[:n_out], refs[n_out:]
        c_outs, refs = refs[:k_out], refs[k_out:]
        scratch, (send_sems, recv_sems) = refs[:n_scr], refs[n_scr:]
        first, last = None, None
        for d, size in enumerate(grid):
            at_start, at_end = pl.program_id(d) == 0, pl.program_id(d) == size - 1
            first = at_start if first is None else first & at_start
            last = at_end if last is None else last & at_end

        @pl.when(first)
        def _():
            for cp in comm.make(c_ins, c_outs, send_sems, recv_sems)[0]:
                cp.start()

        body(*prefetched, *ins, *outs, *scratch)

        @pl.when(last)
        def _():
            for wait in comm.make(c_ins, c_outs, send_sems, recv_sems)[1]:
                wait()

    aliases = {n_prefetch + n_in + i: n_out + i for i in range(comm.in_place)}
    call = pl.pallas_call(
        hosted, name=name, out_shape=list(out_shape) + list(comm.out_shapes),
        compiler_params=_params(*semantics), input_output_aliases=aliases,
        grid_spec=pltpu.PrefetchScalarGridSpec(
            num_scalar_prefetch=n_prefetch, grid=grid, in_specs=list(in_specs) + [_any()] * k_in,
            out_specs=list(out_specs) + [_any()] * k_out,
            scratch_shapes=list(scratch_shapes) + [pltpu.SemaphoreType.DMA((comm.n_sems,)),
                                                    pltpu.SemaphoreType.DMA((comm.n_sems,))]))
    return lambda *args: call(*args, *comm.operands)


class _Shifted:
    def __init__(self, base, offset):
        self.base, self.offset = base, offset

    @property
    def at(self):
        return self

    def __getitem__(self, k):
        return self.base.at[self.offset + k]


def _merge_in_place(*jobs):
    assert all(j.in_place == len(j.operands) == len(j.out_shapes) for j in jobs)

    def make(ins, outs, send_sems, recv_sems):
        starts, waits, at, sem = [], [], 0, 0
        for j in jobs:
            n = len(j.operands)
            s, w = j.make(ins[at:at + n], outs[at:at + n], _Shifted(send_sems, sem), _Shifted(recv_sems, sem))
            starts, waits, at, sem = starts + s, waits + w, at + n, sem + j.n_sems
        return starts, waits

    operands = tuple(a for j in jobs for a in j.operands)
    return _Comm(operands, tuple(s for j in jobs for s in j.out_shapes), sum(j.n_sems for j in jobs), make,
                 in_place=len(operands))


def _mesh_place():
    x, y, c = lax.axis_index("x"), lax.axis_index("y"), lax.axis_index("c")
    return x, y, c, [(1 - x, y), (x, 1 - y), (1 - x, 1 - y)]


def _gather_job(bufs):
    per = 4

    def make(ins, outs, send_sems, recv_sems):
        del ins
        x, y, c, chips = _mesh_place()
        starts, waits = [], []
        for a, out in enumerate(outs):
            mine = src = out.at[4 * x + 2 * y + c]
            to = [(x, y, 1 - c)] + [(px, py, c) for px, py in chips]
            sends = [pltpu.make_async_remote_copy(
                src_ref=src, dst_ref=mine, send_sem=send_sems.at[per * a + k],
                recv_sem=recv_sems.at[per * a + k], device_id=dev, device_id_type=MESH)
                for k, dev in enumerate(to)]
            recvs = [pltpu.make_async_remote_copy(
                src_ref=src, dst_ref=out.at[4 * px + 2 * py + pc], send_sem=send_sems.at[per * a + k],
                recv_sem=recv_sems.at[per * a + k], device_id=(px, py, pc), device_id_type=MESH)
                for k, (px, py, pc) in enumerate(to)]
            starts += sends
            waits += [s.wait_send for s in sends] + [r.wait_recv for r in recvs]
        return starts, waits

    shapes = tuple(jax.ShapeDtypeStruct(b.shape, b.dtype) for b in bufs)
    return _Comm(tuple(bufs), shapes, per * len(bufs), make, in_place=len(bufs))


def _gather_split_job(halves):
    per = 3

    def make(ins, outs, send_sems, recv_sems):
        x, y, c, chips = _mesh_place()
        starts, waits = [], []
        for a, (src, out) in enumerate(zip(ins, outs)):
            for k, (px, py) in enumerate(chips):
                sems = dict(send_sem=send_sems.at[per * a + k], recv_sem=recv_sems.at[per * a + k],
                            device_id=(px, py, c), device_id_type=MESH)
                send = pltpu.make_async_remote_copy(src_ref=src.at[c], dst_ref=out.at[4 * x + 2 * y + c], **sems)
                arrival = pltpu.make_async_remote_copy(src_ref=src.at[c], dst_ref=out.at[4 * px + 2 * py + c], **sems)
                starts.append(send)
                waits += [send.wait_send, arrival.wait_recv]
        return starts, waits

    shapes = tuple(jax.ShapeDtypeStruct((N_DEV,) + h.shape[1:], h.dtype) for h in halves)
    return _Comm(tuple(halves), shapes, per * len(halves), make)


def _slots(x, y, c):
    return 4 * x + 2 * y + c, 4 * (1 - x) + 2 * y + c, 4 * x + 2 * (1 - y) + c, 4 * (1 - x) + 2 * (1 - y) + c


def _gather2d_first(halves):
    per = 2

    def make(ins, outs, send_sems, recv_sems):
        x, y, c, _ = _mesh_place()
        me, xn, yn, _ = _slots(x, y, c)
        starts, waits = [], []
        for a, (src, out) in enumerate(zip(ins, outs)):
            blk = src.at[c]
            rows = blk.shape[0] // 2
            upper, lower = pl.ds(0, rows), pl.ds(rows, rows)

            def copy(k, src_ref, dst_ref, dev, a=a):
                return pltpu.make_async_remote_copy(
                    src_ref=src_ref, dst_ref=dst_ref, send_sem=send_sems.at[per * a + k],
                    recv_sem=recv_sems.at[per * a + k], device_id=dev, device_id_type=MESH)

            sends = [copy(0, blk.at[upper], out.at[me, upper], (1 - x, y, c)),
                     copy(1, blk.at[lower], out.at[me, lower], (x, 1 - y, c))]
            recvs = [copy(0, blk.at[upper], out.at[xn, upper], (1 - x, y, c)),
                     copy(1, blk.at[lower], out.at[yn, lower], (x, 1 - y, c))]
            starts += sends
            waits += [s.wait_send for s in sends] + [r.wait_recv for r in recvs]
        return starts, waits

    shapes = tuple(jax.ShapeDtypeStruct((N_DEV,) + h.shape[1:], h.dtype) for h in halves)
    return _Comm(tuple(halves), shapes, per * len(halves), make)


def _gather2d_second(bufs, halves):
    per = 4
    n_arr = len(bufs)

    def make(ins, outs, send_sems, recv_sems):
        x, y, c, _ = _mesh_place()
        me, xn, yn, dg = _slots(x, y, c)
        starts, waits = [], []
        for a, buf in enumerate(outs):
            own = ins[n_arr + a].at[c]
            rows = buf.shape[1] // 2
            upper, lower = pl.ds(0, rows), pl.ds(rows, rows)
            plan = [(own.at[upper], me, upper, (x, 1 - y, c), yn), (buf.at[xn, upper], xn, upper, (x, 1 - y, c), dg),
                    (own.at[lower], me, lower, (1 - x, y, c), xn), (buf.at[yn, lower], yn, lower, (1 - x, y, c), dg)]
            for k, (src, slot, part, dev, landing) in enumerate(plan):
                sems = dict(send_sem=send_sems.at[per * a + k], recv_sem=recv_sems.at[per * a + k],
                            device_id=dev, device_id_type=MESH)
                send = pltpu.make_async_remote_copy(src_ref=src, dst_ref=buf.at[slot, part], **sems)
                arrival = pltpu.make_async_remote_copy(src_ref=src, dst_ref=buf.at[landing, part], **sems)
                starts.append(send)
                waits += [send.wait_send, arrival.wait_recv]
        return starts, waits

    shapes = tuple(jax.ShapeDtypeStruct(b.shape, b.dtype) for b in bufs)
    return _Comm(tuple(bufs) + tuple(halves), shapes, per * n_arr, make, in_place=n_arr)


def _gather_forward(bufs, name):
    n_arr = len(bufs)

    def body(*refs):
        outs = refs[n_arr:2 * n_arr]
        send_sems, recv_sems = refs[2 * n_arr:]
        x, y, c, chips = _mesh_place()
        sends, recvs = [], []
        for a, buf in enumerate(outs):
            for j, (px, py) in enumerate(chips):
                mine, theirs = buf.at[4 * px + 2 * py + c], buf.at[4 * px + 2 * py + 1 - c]
                sems = dict(send_sem=send_sems.at[3 * a + j], recv_sem=recv_sems.at[3 * a + j],
                            device_id=(x, y, 1 - c), device_id_type=MESH)
                sends.append(pltpu.make_async_remote_copy(src_ref=mine, dst_ref=mine, **sems))
                recvs.append(pltpu.make_async_remote_copy(src_ref=mine, dst_ref=theirs, **sems))
        for cp in sends:
            cp.start()
        for s, r in zip(sends, recvs):
            s.wait_send()
            r.wait_recv()

    return pl.pallas_call(
        body, name=name, out_shape=[jax.ShapeDtypeStruct(b.shape, b.dtype) for b in bufs],
        in_specs=[_any()] * n_arr, out_specs=[_any()] * n_arr,
        input_output_aliases={a: a for a in range(n_arr)},
        scratch_shapes=[pltpu.SemaphoreType.DMA((3 * n_arr,)), pltpu.SemaphoreType.DMA((3 * n_arr,))],
    )(*bufs)


def _scatter_job(chip_sums):
    def make(ins, outs, send_sems, recv_sems):
        x, y, c, chips = _mesh_place()
        copies = [pltpu.make_async_remote_copy(
            src_ref=src.at[2 * px + py], dst_ref=out.at[j], send_sem=send_sems.at[3 * a + j],
            recv_sem=recv_sems.at[3 * a + j], device_id=(px, py, c), device_id_type=MESH)
            for a, (src, out) in enumerate(zip(ins, outs)) for j, (px, py) in enumerate(chips)]
        return copies, [cp.wait for cp in copies]

    return _Comm(tuple(chip_sums), tuple(jax.ShapeDtypeStruct((3,) + s.shape[1:], s.dtype) for s in chip_sums),
                 3 * len(chip_sums), make)


def _all_gather8(blocks, name, split=False, forward=(), riders=(), skip_own=()):
    n_arr, n_fwd = len(blocks), len(forward)
    splits = list(split) if isinstance(split, (list, tuple)) else [split] * n_arr
    own_slots = [a not in skip_own for a in range(n_arr)]
    rider_in = sum(len(r.operands) for r in riders)
    rider_out = sum(len(r.out_shapes) for r in riders)

    def body(*refs):
        x_refs, refs = refs[:n_arr], refs[n_arr + n_fwd:]
        r_ins, refs = refs[:rider_in], refs[rider_in:]
        out_refs, refs = refs[:n_arr], refs[n_arr:]
        fwd_refs, refs = refs[:n_fwd], refs[n_fwd:]
        r_outs, refs = refs[:rider_out], refs[rider_out:]
        (send_sems, recv_sems, local_sems), rider_sems = refs[:3], refs[3:]
        x, y, c, chips = _mesh_place()
        me, sibling = (x, y, c), (x, y, 1 - c)
        rider_waits, i0, o0 = [], 0, 0
        for n, job in enumerate(riders):
            k_in, k_out = len(job.operands), len(job.out_shapes)
            starts, waits = job.make(r_ins[i0:i0 + k_in], r_outs[o0:o0 + k_out],
                                     rider_sems[2 * n], rider_sems[2 * n + 1])
            for cp in starts:
                cp.start()
            rider_waits += waits
            i0, o0 = i0 + k_in, o0 + k_out
        passing = []
        for f, buf in enumerate(fwd_refs):
            for j, (px, py) in enumerate(chips):
                mine, theirs = buf.at[4 * px + 2 * py + c], buf.at[4 * px + 2 * py + 1 - c]
                sems = dict(send_sem=send_sems.at[7 * n_arr + 3 * f + j], recv_sem=recv_sems.at[7 * n_arr + 3 * f + j],
                            device_id=sibling, device_id_type=MESH)
                passing.append((pltpu.make_async_remote_copy(src_ref=mine, dst_ref=mine, **sems),
                                pltpu.make_async_remote_copy(src_ref=mine, dst_ref=theirs, **sems)))
        for send, _ in passing:
            send.start()
        arrays = []
        for a, (x_ref, out_ref) in enumerate(zip(x_refs, out_refs)):
            src_mine = x_ref.at[c] if splits[a] else x_ref

            def copy(k, blk, to, src=None, a=a, out_ref=out_ref):
                dst = out_ref.at[4 * blk[0] + 2 * blk[1] + blk[2]]
                return pltpu.make_async_remote_copy(
                    src_ref=dst if src is None else src, dst_ref=dst,
                    send_sem=send_sems.at[7 * a + k], recv_sem=recv_sems.at[7 * a + k],
                    device_id=to, device_id_type=MESH)

            mine = pltpu.make_async_copy(src_mine, out_ref.at[4 * x + 2 * y + c], local_sems.at[a])
            first = [copy(0, me, sibling, src=src_mine)] if own_slots[a] else []
            first += [copy(1 + j, me, (*chip, c), src=src_mine) for j, chip in enumerate(chips)]
            for cp in first + ([mine] if own_slots[a] else []):
                cp.start()
            arrays.append((copy, mine, first, own_slots[a]))
        sent = []
        for copy, mine, first, own in arrays:
            passed = [copy(4 + j, (*chip, c), sibling) for j, chip in enumerate(chips)]
            for j, chip in enumerate(chips):
                copy(1 + j, (*chip, c), me).wait_recv()
                passed[j].start()
            sent += first + passed
        for copy, mine, first, own in arrays:
            if own:
                copy(0, sibling, me).wait_recv()
                mine.wait()
            for j, chip in enumerate(chips):
                copy(4 + j, (*chip, 1 - c), me).wait_recv()
        for cp in sent:
            cp.wait_send()
        for send, arrival in passing:
            send.wait_send()
            arrival.wait_recv()
        for wait in rider_waits:
            wait()

    n_sems = 7 * n_arr + 3 * n_fwd
    rider_operands = [a for r in riders for a in r.operands]
    rider_shapes = [s for r in riders for s in r.out_shapes]
    return pl.pallas_call(
        body, name=name,
        out_shape=[jax.ShapeDtypeStruct((N_DEV,) + tuple(b.shape[1:] if s else b.shape), b.dtype)
                   for b, s in zip(blocks, splits)]
        + [jax.ShapeDtypeStruct(f.shape, f.dtype) for f in forward] + rider_shapes,
        in_specs=[_any()] * (n_arr + n_fwd + rider_in), out_specs=[_any()] * (n_arr + n_fwd + rider_out),
        input_output_aliases={n_arr + f: n_arr + f for f in range(n_fwd)},
        scratch_shapes=[pltpu.SemaphoreType.DMA((n_sems,)), pltpu.SemaphoreType.DMA((n_sems,)),
                        pltpu.SemaphoreType.DMA((n_arr,))]
        + [pltpu.SemaphoreType.DMA((r.n_sems,)) for r in riders for _ in range(2)],
    )(*blocks, *forward, *rider_operands)


def _share_job(bufs):
    def make(ins, outs, send_sems, recv_sems):
        del ins
        x, y, c, _ = _mesh_place()
        sems = lambda a: dict(send_sem=send_sems.at[a], recv_sem=recv_sems.at[a],
                              device_id=(x, y, 1 - c), device_id_type=MESH)
        sends = [pltpu.make_async_remote_copy(src_ref=o.at[c], dst_ref=o.at[c], **sems(a)) for a, o in enumerate(outs)]
        arrivals = [pltpu.make_async_remote_copy(src_ref=o.at[c], dst_ref=o.at[1 - c], **sems(a))
                    for a, o in enumerate(outs)]
        return sends, [s.wait_send for s in sends] + [r.wait_recv for r in arrivals]

    shapes = tuple(jax.ShapeDtypeStruct(b.shape, b.dtype) for b in bufs)
    return _Comm(tuple(bufs), shapes, len(bufs), make, in_place=len(bufs))


def _sibling_share(bufs, name):
    n_arr = len(bufs)

    def body(*refs):
        out_refs = refs[n_arr:2 * n_arr]
        send_sems, recv_sems = refs[2 * n_arr:]
        x, y, c = lax.axis_index("x"), lax.axis_index("y"), lax.axis_index("c")
        copies = [pltpu.make_async_remote_copy(
            src_ref=out_refs[a].at[c], dst_ref=out_refs[a].at[c],
            send_sem=send_sems.at[a], recv_sem=recv_sems.at[a],
            device_id=(x, y, 1 - c), device_id_type=MESH) for a in range(n_arr)]
        for cp in copies:
            cp.start()
        for a in range(n_arr):
            pltpu.make_async_remote_copy(
                src_ref=out_refs[a].at[c], dst_ref=out_refs[a].at[1 - c],
                send_sem=send_sems.at[a], recv_sem=recv_sems.at[a],
                device_id=(x, y, 1 - c), device_id_type=MESH).wait()

    return pl.pallas_call(
        body, name=name,
        out_shape=[jax.ShapeDtypeStruct(b.shape, b.dtype) for b in bufs],
        in_specs=[_any()] * n_arr, out_specs=[_any()] * n_arr,
        input_output_aliases={a: a for a in range(n_arr)},
        scratch_shapes=[pltpu.SemaphoreType.DMA((n_arr,)), pltpu.SemaphoreType.DMA((n_arr,))],
    )(*bufs)


def _gelu_tanh(z):
    k = math.sqrt(2.0 / math.pi)
    t = jnp.tanh(k * (z + 0.044715 * (z * z * z)))
    return 0.5 * z * (1.0 + t), t


def _gelu_tanh_grad(z, t):
    k = math.sqrt(2.0 / math.pi)
    return 0.5 * (1.0 + t) + 0.5 * z * (1.0 - t * t) * (k * (1.0 + 3.0 * 0.044715 * (z * z)))


def _rope_angle_kernel(pos_row, invf_col):
    seq = pos_row.shape[1]

    def body(p_ref, f_ref, cos_ref, sin_ref):
        ang = p_ref[...].astype(F32) * f_ref[...]
        cos_ref[...] = jnp.cos(ang)
        sin_ref[...] = jnp.sin(ang)

    return pl.pallas_call(
        body, name="rope_angles", grid=(1,), out_shape=[jax.ShapeDtypeStruct((ROT_DIM // 2, seq), F32)] * 2,
        in_specs=[_full((1, seq)), _full((ROT_DIM // 2, 1))], out_specs=[_full((ROT_DIM // 2, seq))] * 2,
        compiler_params=_params("arbitrary"),
    )(pos_row, invf_col)


def _rope_lane_tables(cos, sin):
    cos_t, sin_t = cos.T, sin.T
    seq, half = cos_t.shape
    ones = jnp.ones((seq, HEAD_DIM - ROT_DIM), F32)
    c64 = jnp.concatenate([cos_t, cos_t, ones], axis=1)
    s1 = jnp.concatenate([sin_t, jnp.zeros((seq, HEAD_DIM - half), F32)], axis=1)
    s2 = jnp.concatenate([jnp.zeros((seq, half), F32), sin_t, jnp.zeros((seq, HEAD_DIM - ROT_DIM), F32)], axis=1)
    return jnp.concatenate([jnp.tile(t, (1, LANES // HEAD_DIM)) for t in (c64, s1, s2)], axis=1)


def _rope_apply(t, tab, sign):
    reps = t.shape[1] // LANES
    c_tab, s1, s2 = (jnp.tile(tab[:, LANES * k:LANES * (k + 1)], (1, reps)) if reps > 1
                     else tab[:, LANES * k:LANES * (k + 1)] for k in range(3))
    half = ROT_DIM // 2
    up = pltpu.roll(t, t.shape[1] - half, 1)
    down = pltpu.roll(t, half, 1)
    return t * c_tab + sign * (down * s2 - up * s1)


def _lane_masks(shape):
    lane = lax.broadcasted_iota(jnp.int32, shape, 1)
    return lane < HEAD_DIM, lane >= HEAD_DIM


HEADS_PER_GROUP = N_Q_HEADS // N_KV_HEADS
ATTN_SCALE = 1.0 / math.sqrt(HEAD_DIM)


def _attn_bias_t(first_block):
    kj = lax.broadcasted_iota(jnp.int32, (2 * CHUNK, CHUNK), 0)
    qi = lax.broadcasted_iota(jnp.int32, (2 * CHUNK, CHUNK), 1)
    ok = (kj > qi) & (kj <= qi + CHUNK)
    if first_block is not None:
        ok = ok & (jnp.logical_not(first_block) | (kj >= CHUNK))
    return jnp.tile(jnp.where(ok, 0.0, -jnp.inf), (1, HEADS_PER_GROUP))


def _group_rows(x, g, lo, hi):
    rows = []
    for r in range(HEADS_PER_GROUP):
        h = HEADS_PER_GROUP * g + r
        pair = x[:, LANES * (h // 2):LANES * (h // 2 + 1)]
        rows.append(jnp.where(hi if h % 2 else lo, pair, 0.0))
    return jnp.concatenate(rows, axis=0)


def _pairs_from_rows(rows, lo):
    return [jnp.where(lo, rows[2 * CHUNK * k:2 * CHUNK * k + CHUNK], rows[2 * CHUNK * k + CHUNK:2 * CHUNK * (k + 1)])
            for k in range(HEADS_PER_GROUP // 2)]


def _group_dup(a, b, g, lo2):
    return jnp.where(lo2, a, b) if g == 0 else jnp.where(lo2, b, a)


def _sink_row(sink_ref, g):
    return jnp.concatenate([sink_ref[HEADS_PER_GROUP * g + r:HEADS_PER_GROUP * g + r + 1, :]
                            for r in range(HEADS_PER_GROUP)], axis=1)


def _attn_probs_t(k_dup, q_rows, bias_t, sink_row):
    s_t = _dot_nt(k_dup, q_rows) * ATTN_SCALE + bias_t
    m = jnp.maximum(jnp.max(s_t, axis=0, keepdims=True), sink_row)
    p = jnp.exp(s_t - m)
    e_sink = jnp.exp(sink_row - m)
    inv = 1.0 / (jnp.sum(p, axis=0, keepdims=True) + e_sink)
    return p * inv, e_sink * inv


def _sgu_forward_pair(wm, vp, j):
    lo, hi = _lane_masks(vp.shape)
    lhs = jnp.concatenate([wm[2 * j], wm[2 * j + 1]], axis=1)
    rhs = jnp.concatenate([jnp.where(lo, vp, 0.0), jnp.where(hi, vp, 0.0)], axis=0)
    return _dot(lhs, rhs)


def _masked_spatial(w_ref):
    t = lax.broadcasted_iota(jnp.int32, (CHUNK, CHUNK), 0)
    s = lax.broadcasted_iota(jnp.int32, (CHUNK, CHUNK), 1)
    tril = s <= t
    return [jnp.where(tril, w_ref[g], 0.0) for g in range(GMLP_GROUPS)], tril, s >= t


def _mod_kernel(c_all, w_shard, b_shard, comm=None):
    n = w_shard.shape[1]
    tn = 512

    def body(c_ref, w_ref, b_ref, mod_ref, act_ref):
        cv = c_ref[...]
        act = cv * (1.0 / (1.0 + jnp.exp(-cv)))
        act_ref[...] = act
        mod_ref[...] = _dot(act, w_ref[...]) + b_ref[...]

    return _hosted_call(
        body, comm, name="ada_mod", grid=(n // tn,),
        out_shape=[jax.ShapeDtypeStruct((N_DEV, n), F32), jax.ShapeDtypeStruct((N_DEV, D_MODEL), F32)],
        in_specs=[_full((N_DEV, D_MODEL)), pl.BlockSpec((D_MODEL, tn), lambda i: (0, i)),
                  pl.BlockSpec((1, tn), lambda i: (0, i))],
        out_specs=[pl.BlockSpec((N_DEV, tn), lambda i: (0, i)), _full((N_DEV, D_MODEL))],
        semantics=("arbitrary",),
    )(c_all, w_shard, b_shard)


def _load_chip_blocks(chip_ref, gathered, local, dsts, sems, first_sem=0):
    for k, dst in enumerate(dsts):
        @pl.when(chip_ref[0] == k)
        def _():
            pltpu.make_async_copy(local, dst, sems.at[first_sem + k]).start()

        @pl.when(chip_ref[0] != k)
        def _():
            pltpu.make_async_copy(gathered.at[k], dst, sems.at[first_sem + k]).start()
    return [pltpu.make_async_copy(local, dst, sems.at[first_sem + k]).wait for k, dst in enumerate(dsts)]


def _w_in_rows(w_scr):
    return [w_scr.at[pl.ds(W_IN_BLOCK * k, W_IN_BLOCK)] for k in range(N_CHIPS)]


def _in_proj_kernel(x, vecs, chip_idx, w_in_gathered, w_in_local, comm=None):
    seq = x.shape[0]
    tm = 512

    def body(chip_ref, x_ref, v_ref, wg_ref, wl_ref, proj_ref, h_ref, w_ref, sems):
        @pl.when(pl.program_id(0) == 0)
        def _():
            for wait in _load_chip_blocks(chip_ref, wg_ref, wl_ref, _w_in_rows(w_ref), sems):
                wait()

        xv = x_ref[...]
        rstd = lax.rsqrt(_mean_last(xv * xv) + EPS)
        n1 = (xv * rstd) * v_ref[0:1, :]
        h = n1 * (1.0 + v_ref[2:3, :]) + v_ref[1:2, :]
        hb = h.astype(MXU_DTYPE)
        h_ref[...] = hb
        proj_ref[...] = _dot_nt(hb, w_ref[...])

    return _hosted_call(
        body, comm, name="in_proj", grid=(seq // tm,), n_prefetch=1,
        out_shape=[jax.ShapeDtypeStruct((seq, IN_PROJ_WIDTH), F32),
                   jax.ShapeDtypeStruct((seq, D_MODEL), MXU_DTYPE)],
        in_specs=[pl.BlockSpec((tm, D_MODEL), lambda i, chip: (i, 0)), _full((8, D_MODEL)), _any(), _any()],
        out_specs=[pl.BlockSpec((tm, IN_PROJ_WIDTH), lambda i, chip: (i, 0)),
                   pl.BlockSpec((tm, D_MODEL), lambda i, chip: (i, 0))],
        scratch_shapes=[pltpu.VMEM((IN_PROJ_WIDTH, D_MODEL), MXU_DTYPE), pltpu.SemaphoreType.DMA((N_CHIPS,))],
        semantics=("arbitrary",),
    )(chip_idx, x, vecs, w_in_gathered, w_in_local)


MIXER_BLOCKS_PER_STEP = 2
KV_START = 2 * GMLP_WIDTH + ATTN_WIDTH


def _mixer_fwd_kernel(proj, rope_tab, w_spatial, bias_full, sink_rows, comm=None):
    seq = proj.shape[0]
    per = MIXER_BLOCKS_PER_STEP
    steps = seq // (CHUNK * per)
    kv_col = KV_START // (2 * KV_WIDTH)

    def body(proj_ref, prev_ref, tab_ref, ptab_ref, w_ref, bias_ref, sink_ref, cat_ref):
        i = pl.program_id(0)
        wm, _, _ = _masked_spatial(w_ref)
        lo, hi = _lane_masks((CHUNK, LANES))
        lo2, _ = _lane_masks((2 * CHUNK, LANES))
        o = 2 * GMLP_WIDTH
        for s in range(per):
            rows, before = slice(CHUNK * s, CHUNK * (s + 1)), slice(CHUNK * (s - 1), CHUNK * s)
            for j in range(GMLP_GROUPS // 2):
                cols = slice(LANES * j, LANES * (j + 1))
                vcols = slice(GMLP_WIDTH + LANES * j, GMLP_WIDTH + LANES * (j + 1))
                u, _ = _gelu_tanh(proj_ref[rows, cols])
                vp, _ = _gelu_tanh(proj_ref[rows, vcols])
                sv = _sgu_forward_pair(wm, vp, j) + bias_ref[:, cols]
                cat_ref[rows, cols] = (u * sv).astype(cat_ref.dtype)
            tab = tab_ref[rows, :]
            if s == 0:
                prev_kv, prev_tab, first = prev_ref[...], ptab_ref[...], i == 0
            else:
                prev_kv, prev_tab, first = proj_ref[before, KV_START:KV_START + 2 * KV_WIDTH], tab_ref[before, :], None
            q_r = _rope_apply(proj_ref[rows, o:o + ATTN_WIDTH], tab, 1.0)
            k_cur = _rope_apply(proj_ref[rows, KV_START:KV_START + KV_WIDTH], tab, 1.0)
            k_prev = _rope_apply(prev_kv[:, 0:KV_WIDTH], prev_tab, 1.0)
            k_a = jnp.concatenate([k_prev, k_cur], axis=0)
            v_a = jnp.concatenate([prev_kv[:, KV_WIDTH:2 * KV_WIDTH],
                                   proj_ref[rows, KV_START + KV_WIDTH:KV_START + 2 * KV_WIDTH]], axis=0)
            k_b = pltpu.roll(k_a, HEAD_DIM, 1)
            v_b = pltpu.roll(v_a, HEAD_DIM, 1)
            bias_t = _attn_bias_t(first)
            for g in range(N_KV_HEADS):
                p_t, _ = _attn_probs_t(_group_dup(k_a, k_b, g, lo2), _group_rows(q_r, g, lo, hi), bias_t,
                                       _sink_row(sink_ref, g))
                o_t = _dot(_group_dup(v_a, v_b, g, lo2).T, p_t)
                for k, pair in enumerate(_pairs_from_rows(o_t.T, lo)):
                    c0 = GMLP_WIDTH + LANES * (2 * g + k)
                    cat_ref[rows, c0:c0 + LANES] = pair.astype(cat_ref.dtype)

    return _hosted_call(
        body, comm, name="mixer_fwd", grid=(steps,),
        out_shape=[jax.ShapeDtypeStruct((seq, D_MODEL), MXU_DTYPE)],
        in_specs=[pl.BlockSpec((CHUNK * per, IN_PROJ_WIDTH), lambda i: (i, 0)),
                  pl.BlockSpec((CHUNK, 2 * KV_WIDTH), lambda i: (jnp.maximum(per * i - 1, 0), kv_col)),
                  pl.BlockSpec((CHUNK * per, 3 * LANES), lambda i: (i, 0)),
                  pl.BlockSpec((CHUNK, 3 * LANES), lambda i: (jnp.maximum(per * i - 1, 0), 0)),
                  _full((GMLP_GROUPS, CHUNK, CHUNK)), _full((CHUNK, GMLP_WIDTH)),
                  _full((N_Q_HEADS, LANES))],
        out_specs=[pl.BlockSpec((CHUNK * per, D_MODEL), lambda i: (i, 0))],
        semantics=("arbitrary",),
    )(proj, proj, rope_tab, rope_tab, w_spatial, bias_full, sink_rows)


def _trunk_kernel(x, target, cat, vecs, chip_idx, gathered, local):
    seq = x.shape[0]
    tm = 256
    nj = D_FF // D_MODEL
    out_rows = D_MODEL // N_CHIPS

    def body(chip_ref, x_ref, t_ref, cat_ref, v_ref, g_out, g_w1, g_w2, l_out, l_w1, l_w2,
             dx1_ref, dcat_ref, dmix_ref, h2_ref, r_ref, da_ref, dff_ref, sums_ref,
             wout, w1, w2, a_scr, sem):
        i = pl.program_id(0)

        @pl.when(i == 0)
        def _():
            waits = _load_chip_blocks(chip_ref, g_out, l_out,
                                      [wout.at[pl.ds(out_rows * k, out_rows)] for k in range(N_CHIPS)], sem)
            waits += _load_chip_blocks(chip_ref, g_w1, l_w1, [w1.at[k] for k in range(N_CHIPS)], sem, N_CHIPS)
            waits += _load_chip_blocks(chip_ref, g_w2, l_w2, [w2.at[k] for k in range(N_CHIPS)], sem, 2 * N_CHIPS)
            for wait in waits:
                wait()
            sums_ref[...] = jnp.zeros_like(sums_ref)

        gate1, shift2, scale2 = v_ref[0:1, :], v_ref[1:2, :], v_ref[2:3, :]
        gate2, g_ffn, g_final = v_ref[3:4, :], v_ref[4:5, :], v_ref[5:6, :]

        mix = _dot(cat_ref[...], wout[...])
        x1 = x_ref[...] + gate1 * mix
        rstd2 = lax.rsqrt(_mean_last(x1 * x1) + EPS)
        xh2 = x1 * rstd2
        n2 = xh2 * g_ffn
        h2b = (n2 * (1.0 + scale2) + shift2).astype(MXU_DTYPE)
        h2_ref[...] = h2b
        ff = jnp.zeros((tm, D_MODEL), F32)
        for j in range(nj):
            a = _dot(h2b, w1[j])
            a_scr[j] = a
            relu = jnp.maximum(a, 0.0)
            rb = (relu * relu).astype(MXU_DTYPE)
            r_ref[:, D_MODEL * j:D_MODEL * (j + 1)] = rb
            ff = ff + _dot(rb, w2[j])
        x2 = x1 + gate2 * ff
        rstd3 = lax.rsqrt(_mean_last(x2 * x2) + EPS)
        xh3 = x2 * rstd3
        err = xh3 * g_final - t_ref[...]
        loss = 0.5 * _rowsum(_mean_last(err * err))
        dy = err * (1.0 / D_MODEL)
        dxh3 = dy * g_final
        dx2 = rstd3 * (dxh3 - xh3 * _mean_last(dxh3 * xh3))
        dffb = (dx2 * gate2).astype(MXU_DTYPE)
        dff_ref[...] = dffb
        dh2 = jnp.zeros((tm, D_MODEL), F32)
        for j in range(nj):
            dr = _dot_nt(dffb, w2[j])
            dab = (dr * (2.0 * jnp.maximum(a_scr[j], 0.0))).astype(MXU_DTYPE)
            da_ref[:, D_MODEL * j:D_MODEL * (j + 1)] = dab
            dh2 = dh2 + _dot_nt(dab, w1[j])
        dn2 = dh2 * (1.0 + scale2)
        dxh2 = dn2 * g_ffn
        dx1 = dx2 + rstd2 * (dxh2 - xh2 * _mean_last(dxh2 * xh2))
        dx1_ref[...] = dx1
        dmixb = (dx1 * gate1).astype(MXU_DTYPE)
        dmix_ref[...] = dmixb
        dcat_ref[...] = _dot_nt(dmixb, wout[...])

        sums_ref[0:1, :] += _rowsum(dh2)
        sums_ref[1:2, :] += _rowsum(dh2 * n2)
        sums_ref[2:3, :] += _rowsum(dx2 * ff)
        sums_ref[3:4, :] += _rowsum(dn2 * xh2)
        sums_ref[4:5, :] += _rowsum(dy * xh3)
        sums_ref[5:6, :] += _rowsum(dx1 * mix)
        sums_ref[6:7, :] += jnp.broadcast_to(loss, (1, D_MODEL))

    tok = lambda w: pl.BlockSpec((tm, w), lambda i, chip: (i, 0))
    return _hosted_call(
        body, None, name="trunk", grid=(seq // tm,), n_prefetch=1,
        out_shape=[jax.ShapeDtypeStruct((seq, D_MODEL), F32), jax.ShapeDtypeStruct((seq, D_MODEL), F32),
                   jax.ShapeDtypeStruct((seq, D_MODEL), MXU_DTYPE), jax.ShapeDtypeStruct((seq, D_MODEL), MXU_DTYPE),
                   jax.ShapeDtypeStruct((seq, D_FF), MXU_DTYPE), jax.ShapeDtypeStruct((seq, D_FF), MXU_DTYPE),
                   jax.ShapeDtypeStruct((seq, D_MODEL), MXU_DTYPE), jax.ShapeDtypeStruct((8, D_MODEL), F32)],
        in_specs=[tok(D_MODEL), tok(D_MODEL), tok(D_MODEL), _full((8, D_MODEL))] + [_any()] * 6,
        out_specs=[tok(D_MODEL), tok(D_MODEL), tok(D_MODEL), tok(D_MODEL), tok(D_FF), tok(D_FF), tok(D_MODEL),
                   _full((8, D_MODEL))],
        scratch_shapes=[pltpu.VMEM((D_MODEL, D_MODEL), MXU_DTYPE), pltpu.VMEM((nj, D_MODEL, D_MODEL), MXU_DTYPE),
                        pltpu.VMEM((nj, D_MODEL, D_MODEL), MXU_DTYPE), pltpu.VMEM((nj, tm, D_MODEL), F32),
                        pltpu.SemaphoreType.DMA((3 * N_CHIPS,))],
        semantics=("arbitrary",),
    )(chip_idx, x, target, cat, vecs, *gathered, *local)


def _mixer_bwd_kernel(proj, rope_tab, dcat, w_spatial, w_spatial_t, bias_full, sink_rows, dev_idx, comm=None):
    seq = proj.shape[0]
    per = MIXER_BLOCKS_PER_STEP
    steps = seq // (CHUNK * per)
    kv_col = KV_START // (2 * KV_WIDTH)

    def body(dev_ref, proj_ref, prev_ref, tab_ref, ptab_ref, dcat_ref, w_ref, wt_ref, bias_ref, sink_ref,
             dproj_ref, dw_ref, db_ref, dsink_ref, carry):
        del dev_ref
        step = pl.program_id(0)

        @pl.when(step == 0)
        def _():
            carry[...] = jnp.zeros_like(carry)
            dw_ref[...] = jnp.zeros_like(dw_ref)
            db_ref[...] = jnp.zeros_like(db_ref)
            dsink_ref[...] = jnp.zeros_like(dsink_ref)

        for s in reversed(range(per)):
            rows = pl.ds(CHUNK * s, CHUNK)
            if s == 0:
                before, before_tab, first = prev_ref, ptab_ref, step == steps - 1
            else:
                before = proj_ref.at[pl.ds(CHUNK * (s - 1), CHUNK), pl.ds(KV_START, 2 * KV_WIDTH)]
                before_tab, first = tab_ref.at[pl.ds(CHUNK * (s - 1), CHUNK)], None
            one_block(proj_ref.at[rows], before, tab_ref.at[rows], before_tab, dcat_ref.at[rows], w_ref, wt_ref,
                      bias_ref, sink_ref, dproj_ref.at[rows], dw_ref, db_ref, dsink_ref, carry, first)

    def one_block(proj_ref, prev_ref, tab_ref, ptab_ref, dcat_ref, w_ref, wt_ref, bias_ref, sink_ref,
                  dproj_ref, dw_ref, db_ref, dsink_ref, carry, first):
        wm, tril, triu = _masked_spatial(w_ref)
        lo, hi = _lane_masks((CHUNK, LANES))
        lane = lax.broadcasted_iota(jnp.int32, (CHUNK, LANES), 1)
        db = jnp.zeros((CHUNK, LANES), F32)
        for j in range(GMLP_GROUPS // 2):
            cols = slice(LANES * j, LANES * (j + 1))
            vcols = slice(GMLP_WIDTH + LANES * j, GMLP_WIDTH + LANES * (j + 1))
            zu, zv = proj_ref[:, cols], proj_ref[:, vcols]
            u, tu = _gelu_tanh(zu)
            vp, tv = _gelu_tanh(zv)
            sv = _sgu_forward_pair(wm, vp, j) + bias_ref[:, cols]
            dout = dcat_ref[:, cols]
            du = dout * sv
            dsv = dout * u
            dsv_lo, dsv_hi = jnp.where(lo, dsv, 0.0), jnp.where(hi, dsv, 0.0)
            lhs_t = jnp.concatenate([jnp.where(triu, wt_ref[2 * j], 0.0),
                                     jnp.where(triu, wt_ref[2 * j + 1], 0.0)], axis=1)
            dv = _dot(lhs_t, jnp.concatenate([dsv_lo, dsv_hi], axis=0))
            dw_ref[2 * j] += jnp.where(tril, _dot_nt(dsv_lo, vp), 0.0)
            dw_ref[2 * j + 1] += jnp.where(tril, _dot_nt(dsv_hi, vp), 0.0)
            db = db + (jnp.where(lane == 2 * j, jnp.sum(dsv_lo, axis=1, keepdims=True), 0.0)
                       + jnp.where(lane == 2 * j + 1, jnp.sum(dsv_hi, axis=1, keepdims=True), 0.0))
            dproj_ref[:, cols] = (du * _gelu_tanh_grad(zu, tu)).astype(dproj_ref.dtype)
            dproj_ref[:, vcols] = (dv * _gelu_tanh_grad(zv, tv)).astype(dproj_ref.dtype)
        db_ref[...] += db
        o = 2 * GMLP_WIDTH
        tab = tab_ref[...]
        q_r = _rope_apply(proj_ref[:, o:o + ATTN_WIDTH], tab, 1.0)
        k_cur = _rope_apply(proj_ref[:, o + ATTN_WIDTH:o + ATTN_WIDTH + KV_WIDTH], tab, 1.0)
        k_prev = _rope_apply(prev_ref[:, 0:KV_WIDTH], ptab_ref[...], 1.0)
        k_a = jnp.concatenate([k_prev, k_cur], axis=0)
        v_a = jnp.concatenate([prev_ref[:, KV_WIDTH:2 * KV_WIDTH],
                               proj_ref[:, o + ATTN_WIDTH + KV_WIDTH:o + ATTN_WIDTH + 2 * KV_WIDTH]], axis=0)
        k_b = pltpu.roll(k_a, HEAD_DIM, 1)
        v_b = pltpu.roll(v_a, HEAD_DIM, 1)
        bias_t = _attn_bias_t(first)
        lo2, _ = _lane_masks((2 * CHUNK, LANES))
        dout_b = dcat_ref[:, GMLP_WIDTH:GMLP_WIDTH + ATTN_WIDTH]
        dk_tot, dv_tot, dq_pairs = [], [], []
        for g in range(N_KV_HEADS):
            k_dup, v_dup = _group_dup(k_a, k_b, g, lo2), _group_dup(v_a, v_b, g, lo2)
            q_rows = _group_rows(q_r, g, lo, hi)
            do_rows = _group_rows(dout_b, g, lo, hi)
            p_t, p_sink = _attn_probs_t(k_dup, q_rows, bias_t, _sink_row(sink_ref, g))
            dp_t = _dot_nt(v_dup, do_rows)
            delta = jnp.sum(p_t * dp_t, axis=0, keepdims=True)
            ds_t = p_t * (dp_t - delta) * ATTN_SCALE
            dsink = -p_sink * delta
            for r in range(HEADS_PER_GROUP):
                h = HEADS_PER_GROUP * g + r
                dsink_ref[h:h + 1, :] += jnp.broadcast_to(
                    jnp.sum(dsink[:, LANES * r:LANES * (r + 1)], axis=1, keepdims=True), (1, LANES))
            dk_full = _dot(ds_t, q_rows)
            dv_full = _dot(p_t, do_rows)
            dk_tot.append(dk_full + pltpu.roll(dk_full, HEAD_DIM, 1))
            dv_tot.append(dv_full + pltpu.roll(dv_full, HEAD_DIM, 1))
            dq_t = _dot(k_dup.T, ds_t)
            dq_pairs += _pairs_from_rows(dq_t.T, lo)
        dk_all = jnp.where(lo2, dk_tot[0], dk_tot[1])
        dv_all = jnp.where(lo2, dv_tot[0], dv_tot[1])
        dk_cur = dk_all[CHUNK:, :] + carry[:, 0:KV_WIDTH]
        dv_cur = dv_all[CHUNK:, :] + carry[:, KV_WIDTH:2 * KV_WIDTH]
        carry[:, 0:KV_WIDTH] = dk_all[:CHUNK, :]
        carry[:, KV_WIDTH:2 * KV_WIDTH] = dv_all[:CHUNK, :]
        dq = _rope_apply(jnp.concatenate(dq_pairs, axis=1), tab, -1.0)
        dproj_ref[:, o:o + ATTN_WIDTH] = dq.astype(dproj_ref.dtype)
        dproj_ref[:, o + ATTN_WIDTH:o + ATTN_WIDTH + KV_WIDTH] = (
            _rope_apply(dk_cur, tab, -1.0).astype(dproj_ref.dtype))
        dproj_ref[:, o + ATTN_WIDTH + KV_WIDTH:o + ATTN_WIDTH + 2 * KV_WIDTH] = dv_cur.astype(dproj_ref.dtype)

    rev = lambda i: steps - 1 - i
    before = lambda i: jnp.maximum(per * rev(i) - 1, 0)
    slot = lambda shape: pl.BlockSpec((None,) + shape, lambda i, d: (d[0],) + (0,) * len(shape))
    return _hosted_call(
        body, comm, name="mixer_bwd", grid=(steps,), n_prefetch=1,
        out_shape=[jax.ShapeDtypeStruct((seq, IN_PROJ_WIDTH), MXU_DTYPE),
                   jax.ShapeDtypeStruct((N_DEV, GMLP_GROUPS, CHUNK, CHUNK), F32),
                   jax.ShapeDtypeStruct((N_DEV, CHUNK, LANES), F32),
                   jax.ShapeDtypeStruct((N_DEV, N_Q_HEADS, LANES), F32)],
        in_specs=[pl.BlockSpec((CHUNK * per, IN_PROJ_WIDTH), lambda i, d: (rev(i), 0)),
                  pl.BlockSpec((CHUNK, 2 * KV_WIDTH), lambda i, d: (before(i), kv_col)),
                  pl.BlockSpec((CHUNK * per, 3 * LANES), lambda i, d: (rev(i), 0)),
                  pl.BlockSpec((CHUNK, 3 * LANES), lambda i, d: (before(i), 0)),
                  pl.BlockSpec((CHUNK * per, D_MODEL), lambda i, d: (rev(i), 0)),
                  _full((GMLP_GROUPS, CHUNK, CHUNK)), _full((GMLP_GROUPS, CHUNK, CHUNK)),
                  _full((CHUNK, GMLP_WIDTH)), _full((N_Q_HEADS, LANES))],
        out_specs=[pl.BlockSpec((CHUNK * per, IN_PROJ_WIDTH), lambda i, d: (rev(i), 0)),
                   slot((GMLP_GROUPS, CHUNK, CHUNK)), slot((CHUNK, LANES)), slot((N_Q_HEADS, LANES))],
        scratch_shapes=[pltpu.VMEM((CHUNK, 2 * KV_WIDTH), F32)],
        semantics=("arbitrary",),
    )(dev_idx, proj, proj, rope_tab, rope_tab, dcat, w_spatial, w_spatial_t, bias_full, sink_rows)


def _in_proj_bwd_kernel(x, dx1, dproj, vecs, chip_idx, w_in_gathered, w_in_local, comm=None):
    seq = x.shape[0]
    tm = 512

    def body(chip_ref, x_ref, dx1_ref, dp_ref, v_ref, wg_ref, wl_ref, gx_ref, sums_ref, w_ref, sems):
        @pl.when(pl.program_id(0) == 0)
        def _():
            for wait in _load_chip_blocks(chip_ref, wg_ref, wl_ref, _w_in_rows(w_ref), sems):
                wait()
            sums_ref[...] = jnp.zeros_like(sums_ref)

        g_mix, scale1 = v_ref[0:1, :], v_ref[2:3, :]
        dh = _dot(dp_ref[...], w_ref[...])
        xv = x_ref[...]
        rstd = lax.rsqrt(_mean_last(xv * xv) + EPS)
        xh = xv * rstd
        dn1 = dh * (1.0 + scale1)
        dxh = dn1 * g_mix
        gx_ref[...] = dx1_ref[...] + rstd * (dxh - xh * _mean_last(dxh * xh))
        sums_ref[0:1, :] += _rowsum(dh)
        sums_ref[1:2, :] += _rowsum(dh * (xh * g_mix))
        sums_ref[2:3, :] += _rowsum(dn1 * xh)

    tok = lambda w: pl.BlockSpec((tm, w), lambda i, chip: (i, 0))
    return _hosted_call(
        body, comm, name="in_proj_bwd", grid=(seq // tm,), n_prefetch=1,
        out_shape=[jax.ShapeDtypeStruct((seq, D_MODEL), F32), jax.ShapeDtypeStruct((8, D_MODEL), F32)],
        in_specs=[tok(D_MODEL), tok(D_MODEL), tok(IN_PROJ_WIDTH), _full((8, D_MODEL)), _any(), _any()],
        out_specs=[tok(D_MODEL), _full((8, D_MODEL))],
        scratch_shapes=[pltpu.VMEM((IN_PROJ_WIDTH, D_MODEL), MXU_DTYPE), pltpu.SemaphoreType.DMA((N_CHIPS,))],
        semantics=("arbitrary",),
    )(chip_idx, x, dx1, dproj, vecs, w_in_gathered, w_in_local)


class _GradTiles(NamedTuple):
    tm: int
    tn: int
    n_tiles: int
    chips_per_tile: int
    a_index: Callable
    b_index: Callable


def _weight_grad_kernel(a, b, c_idx, name, tiles, comm=None):
    seq = a.shape[0]
    tk = min(seq, 4096)
    nk = seq // tk
    tm, tn, n_tiles, per = tiles.tm, tiles.tn, tiles.n_tiles, tiles.chips_per_tile
    rows = tm // per

    def half(phase, c):
        return phase * c[0] + (1 - phase) * (1 - c[0])

    def body(c_ref, a_ref, b_ref, o_ref, acc, stage, landed, send_sems, recv_sems):
        del c_ref
        phase, t, kk = pl.program_id(0), pl.program_id(1), pl.program_id(2)
        x, y, c, _ = _mesh_place()

        def copy(tile):
            return pltpu.make_async_remote_copy(
                src_ref=stage.at[tile], dst_ref=landed.at[tile], send_sem=send_sems.at[tile],
                recv_sem=recv_sems.at[tile], device_id=(x, y, 1 - c), device_id_type=MESH)

        @pl.when(kk == 0)
        def _():
            acc[...] = jnp.zeros_like(acc)

        acc[...] += _dot_tn(a_ref[...], b_ref[...])

        @pl.when((kk == nk - 1) & (phase == 0))
        def _():
            stage[t] = acc[...].astype(stage.dtype)
            copy(t).start()

        @pl.when((kk == nk - 1) & (phase == 1))
        def _():
            copy(t).wait_recv()
            total = acc[...] + landed[t].astype(F32)
            for q in range(per):
                o_ref[q] = total[rows * q:rows * (q + 1)].astype(o_ref.dtype)

        @pl.when((kk == nk - 1) & (phase == 1) & (t == n_tiles - 1))
        def _():
            for tile in range(n_tiles):
                copy(tile).wait_send()

    out = _hosted_call(
        body, comm, name=name, grid=(2, n_tiles, nk), n_prefetch=1,
        out_shape=[jax.ShapeDtypeStruct((n_tiles * per, rows, tn), GRAD_COMM_DTYPE)],
        in_specs=[pl.BlockSpec((tk, tm), lambda p, t, k, c: (k, tiles.a_index(t, half(p, c)))),
                  pl.BlockSpec((tk, tn), lambda p, t, k, c: (k, tiles.b_index(t, half(p, c))))],
        out_specs=[pl.BlockSpec((per, rows, tn), lambda p, t, k, c: (p * t, 0, 0))],
        scratch_shapes=[pltpu.VMEM((tm, tn), F32), pltpu.VMEM((n_tiles, tm, tn), GRAD_COMM_DTYPE),
                        pltpu.VMEM((n_tiles, tm, tn), GRAD_COMM_DTYPE),
                        pltpu.SemaphoreType.DMA((n_tiles,)), pltpu.SemaphoreType.DMA((n_tiles,))],
        semantics=("arbitrary", "arbitrary", "arbitrary"),
    )(c_idx, a, b)
    return out[0] if comm is None else out


def _row_tile(rows, most=256, sublanes=16):
    return max(t for t in range(sublanes, most + 1, sublanes) if rows % t == 0)


def _adam_update(w, g, m, v):
    m_new = ADAM_B1 * m + (1.0 - ADAM_B1) * g
    v_new = ADAM_B2 * v + (1.0 - ADAM_B2) * (g * g)
    m_hat = m_new / (1.0 - ADAM_B1 ** ADAM_STEP)
    v_hat = v_new / (1.0 - ADAM_B2 ** ADAM_STEP)
    delta = -ADAM_LR * (m_hat / (jnp.sqrt(v_hat) + ADAM_EPS) + ADAM_WD * w)
    return delta, m_new, v_new


def _sum_chips_kernel(own, others, place, name):
    _, r, n = own.shape
    tr = _row_tile(r)

    def body(place_ref, own_ref, oth_ref, o_ref):
        del place_ref
        acc = own_ref[...].astype(F32)
        for k in range(N_CHIPS - 1):
            acc = acc + oth_ref[k].astype(F32)
        o_ref[...] = acc

    return pl.pallas_call(
        body, name=name, out_shape=jax.ShapeDtypeStruct((2, r, n), F32),
        grid_spec=pltpu.PrefetchScalarGridSpec(
            num_scalar_prefetch=1, grid=(r // tr,),
            in_specs=[pl.BlockSpec((None, tr, n), lambda i, p: (p[0], i, 0)),
                      pl.BlockSpec((N_CHIPS - 1, tr, n), lambda i, p: (0, i, 0))],
            out_specs=pl.BlockSpec((None, tr, n), lambda i, p: (p[1], i, 0))),
        compiler_params=_params("parallel"),
    )(place, own, others)


def _adam_kernel(w, g, m, v, name):
    r, n = w.shape
    by_columns = g.shape[1] == r
    tr, tn = _row_tile(g.shape[1], most=512), g.shape[2]

    def body(w_ref, g_ref, m_ref, v_ref, g_out, d_ref, mo_ref, vo_ref):
        gv = g_ref[...]
        g_out[...] = gv
        d_ref[...], mo_ref[...], vo_ref[...] = _adam_update(w_ref[...], gv, m_ref[...], v_ref[...])

    steps = g.shape[1] // tr
    spec = pl.BlockSpec((tr, tn), (lambda h, i: (i, h)) if by_columns else (lambda h, i: (h * steps + i, 0)))
    return pl.pallas_call(
        body, name=name, grid=(2, steps), out_shape=[jax.ShapeDtypeStruct((r, n), F32)] * 4,
        in_specs=[spec, pl.BlockSpec((None, tr, tn), lambda h, i: (h, i, 0)), spec, spec], out_specs=[spec] * 4,
        compiler_params=_params("parallel", "parallel"),
    )(w, g, m, v)


SMALL_PARAMS = ("b_ada", "g_mix", "g_ffn", "g_final", "b_spatial", "sinks", "w_spatial")


def _small_update_kernel(gathered, params):
    shapes = [params[nm][0].shape for nm in SMALL_PARAMS]

    def body(*refs):
        g_refs, refs = refs[:5], refs[5:]
        p_refs, refs = refs[:3 * len(SMALL_PARAMS)], refs[3 * len(SMALL_PARAMS):]
        loss_ref, o_refs = refs[0], refs[1:]

        def total(ref):
            acc = ref[0]
            for k in range(1, N_DEV):
                acc = acc + ref[k]
            return acc

        s1, s2, db, ds, dw = (total(r) for r in g_refs)
        loss_ref[...] = jnp.broadcast_to(s2[6:7, 0:1], loss_ref.shape)
        grads = {"b_ada": [s1[0:1], s1[1:2], s2[5:6], s2[0:1], s2[1:2], s2[2:3]], "g_mix": [s1[2:3]],
                 "g_ffn": [s2[3:4]], "g_final": [s2[4:5]], "b_spatial": [db.T[0:GMLP_GROUPS]],
                 "w_spatial": [dw]}
        lane = lax.broadcasted_iota(jnp.int32, (1, LANES), 1)
        sink_row = jnp.zeros((1, LANES), F32)
        for h in range(N_Q_HEADS):
            sink_row = sink_row + jnp.where(lane == h, ds[h:h + 1, :], 0.0)
        grads["sinks"] = [sink_row[:, 0:N_Q_HEADS]]
        for i, nm in enumerate(SMALL_PARAMS):
            w_ref, m_ref, v_ref = p_refs[3 * i:3 * i + 3]
            outs = o_refs[4 * i:4 * i + 4]
            width = grads[nm][0].shape[1]
            for k, g in enumerate(grads[nm]):
                cols = slice(width * k, width * (k + 1))
                upd = _adam_update(w_ref[:, cols], g, m_ref[:, cols], v_ref[:, cols])
                for o_ref, val in zip(outs, (g,) + upd):
                    o_ref[:, cols] = val

    flat = [a for nm in SMALL_PARAMS for a in params[nm]]
    out_shape = [jax.ShapeDtypeStruct((8, LANES), F32)]
    out_shape += [jax.ShapeDtypeStruct(s, F32) for s in shapes for _ in range(4)]
    outs = pl.pallas_call(
        body, name="small_update", grid=(1,), out_shape=out_shape,
        in_specs=[_full(g.shape) for g in gathered] + [_full(a.shape) for a in flat],
        out_specs=[_full(s.shape) for s in out_shape],
        compiler_params=_params("arbitrary"),
    )(*gathered, *flat)
    return {nm: outs[1 + 4 * i:5 + 4 * i] for i, nm in enumerate(SMALL_PARAMS)}, outs[0]


def _ada_update_kernel(act_t, dmod, w, m, v):
    r, n = w.shape
    tr = 256

    def body(a_ref, d_ref, w_ref, m_ref, v_ref, g_ref, dl_ref, mo_ref, vo_ref):
        g = _dot(a_ref[...], d_ref[...])
        g_ref[...] = g
        dl_ref[...], mo_ref[...], vo_ref[...] = _adam_update(w_ref[...], g, m_ref[...], v_ref[...])

    spec = pl.BlockSpec((tr, n), lambda i: (i, 0))
    return pl.pallas_call(
        body, name="ada_update", grid=(r // tr,), out_shape=[jax.ShapeDtypeStruct((r, n), F32)] * 4,
        in_specs=[pl.BlockSpec((tr, N_DEV), lambda i: (i, 0)), _full((N_DEV, n)), spec, spec, spec],
        out_specs=[spec] * 4, compiler_params=_params("parallel"),
    )(act_t, dmod, w, m, v)


def kernel(x, c, positions, w_ada, b_ada, g_mix, w_in, w_spatial, b_spatial, sinks, w_out, g_ffn, w_ff1, w_ff2, g_final, loss_target, m_w_ada, m_b_ada, m_g_mix, m_w_in, m_w_spatial, m_b_spatial, m_sinks, m_w_out, m_g_ffn, m_w_ff1, m_w_ff2, m_g_final, v_w_ada, v_b_ada, v_g_mix, v_w_in, v_w_spatial, v_b_spatial, v_sinks, v_w_out, v_g_ffn, v_w_ff1, v_w_ff2, v_g_final):
    xi, yi, ci = lax.axis_index("x"), lax.axis_index("y"), lax.axis_index("c")
    chip = 2 * xi + yi
    dev = 2 * chip + ci
    seq = x.shape[1]
    x2, tgt = x[0], loss_target[0]
    ada_cols = w_ada.shape[2]

    big = {"w_in": tuple(a[0].T for a in (w_in, m_w_in, v_w_in)),
           "w_out": (w_out[0], m_w_out[0], v_w_out[0]), "w_ff1": (w_ff1[0], m_w_ff1[0], v_w_ff1[0]),
           "w_ff2": (w_ff2[0], m_w_ff2[0], v_w_ff2[0])}

    def halves(nm):
        r, n = big[nm][0].shape
        return big[nm][0].astype(WEIGHT_COMM_DTYPE).reshape(2, r // 2, n)

    chip_idx = chip.reshape(1).astype(jnp.int32)
    w_in_local = halves("w_in")
    c_all = _all_gather8([c], "gather_c")[0].reshape(N_DEV, D_MODEL)
    b_shard = lax.dynamic_slice(b_ada, (0, chip * ada_cols), (1, ada_cols))
    mod_part, act, w_in_g, g_out = _mod_kernel(c_all, w_ada[0], b_shard,
                                               comm=_gather_split_job([w_in_local, halves("w_out")]))
    mod_all, w_in_g, g_out = _all_gather8([mod_part], "gather_mod", forward=[w_in_g, g_out])
    w_in_g, w_in_local = w_in_g.reshape(N_CHIPS, W_IN_BLOCK, D_MODEL), w_in_local.reshape(W_IN_BLOCK, D_MODEL)
    mod_me = lax.dynamic_index_in_dim(mod_all[0::2], dev, axis=1, keepdims=False)
    mod_me = mod_me.reshape(N_MOD, D_MODEL)
    shift1, scale1, gate1, shift2, scale2, gate2 = (mod_me[k:k + 1] for k in range(N_MOD))

    zeros_row = jnp.zeros((1, D_MODEL), F32)
    vecs1 = jnp.concatenate([g_mix, shift1, scale1] + [zeros_row] * 5, axis=0)
    vecs2 = jnp.concatenate([gate1, shift2, scale2, gate2, g_ffn, g_final.reshape(1, D_MODEL)]
                            + [zeros_row] * 2, axis=0)
    bias_full = jnp.repeat(b_spatial[0].T, HEAD_DIM, axis=1)
    sink_rows = jnp.broadcast_to(sinks[0][:, None], (N_Q_HEADS, LANES))
    inv_freq = ROPE_THETA ** (-jnp.arange(0, ROT_DIM, 2, dtype=F32) / ROT_DIM)
    rope_tab = _rope_lane_tables(*_rope_angle_kernel(positions, inv_freq.reshape(ROT_DIM // 2, 1)))

    trunk_weights = ["w_out", "w_ff1", "w_ff2"]
    shards = [halves(nm) for nm in trunk_weights]
    proj, hb, *staged = _in_proj_kernel(x2, vecs1, chip_idx, w_in_g, w_in_local,
                                        comm=_gather2d_first(shards[1:]))
    cat, *staged = _mixer_fwd_kernel(proj, rope_tab, w_spatial[0], bias_full, sink_rows,
                                     comm=_gather2d_second(staged, shards[1:]))
    staged = [g_out] + list(_gather_forward(staged, "gather_forward"))
    dx1, dcat, dmix, h2b, rb, dab, dffb, sums2 = _trunk_kernel(
        x2, tgt, cat, vecs2, chip_idx,
        [g.reshape((N_CHIPS,) + big[nm][0].shape) for nm, g in zip(trunk_weights, staged)],
        [s.reshape(big[nm][0].shape) for nm, s in zip(trunk_weights, shards)])

    c_idx = ci.reshape(1).astype(jnp.int32)
    place = jnp.stack([chip, ci]).astype(jnp.int32)
    half_d = D_MODEL // 2
    cs_ff2 = _weight_grad_kernel(rb, dffb, c_idx, "dw_ff2",
                                 _GradTiles(D_MODEL, half_d, N_CHIPS, 1, lambda t, h: t, lambda t, h: h))
    cs_ff1, sc_ff2 = _weight_grad_kernel(
        h2b, dab, c_idx, "dw_ff1",
        _GradTiles(D_MODEL, half_d, N_CHIPS, 1, lambda t, h: 0, lambda t, h: 2 * t + h),
        comm=_scatter_job([cs_ff2]))
    cs_out = _weight_grad_kernel(cat, dmix, c_idx, "dw_out",
                                 _GradTiles(D_MODEL, half_d, 1, N_CHIPS, lambda t, h: 0, lambda t, h: h))
    dproj, dw_spatial, db_lanes, dsink_rows, sc_ff1, sc_out = _mixer_bwd_kernel(
        proj, rope_tab, dcat, w_spatial[0], w_spatial[0].transpose(0, 2, 1), bias_full, sink_rows,
        dev.reshape(1).astype(jnp.int32), comm=_scatter_job([cs_ff1, cs_out]))
    totals = [_sum_chips_kernel(own, oth, place, "grad_sum_" + nm)
              for nm, own, oth in (("w_out", cs_out, sc_out), ("w_ff1", cs_ff1, sc_ff1), ("w_ff2", cs_ff2, sc_ff2))]
    small_slots = [db_lanes, dsink_rows, dw_spatial.reshape(N_DEV, GMLP_GROUPS * CHUNK, CHUNK)]
    cs_in, *rode = _weight_grad_kernel(
        dproj, hb, c_idx, "dw_in",
        _GradTiles(2 * W_IN_BLOCK, half_d, N_CHIPS // 2, 2, lambda t, h: t, lambda t, h: h),
        comm=_merge_in_place(_gather_job(small_slots), _share_job(totals)))
    small_stage1, shared = rode[:len(small_slots)], rode[len(small_slots):]
    grad_x, sums1 = _in_proj_bwd_kernel(x2, dx1, dproj, vecs1, chip_idx, w_in_g, w_in_local)
    *gathered, sc_in = _all_gather8([sums1, sums2], "gather_small", forward=small_stage1,
                                    riders=[_scatter_job([cs_in])])
    total_in = _sum_chips_kernel(cs_in, sc_in, place, "grad_sum_w_in")
    shared = list(_sibling_share([total_in], "grad_share_w_in")) + list(shared)
    names = ["w_in", "w_out", "w_ff1", "w_ff2"]
    big_out = {}
    for nm, g in zip(names, shared):
        w, m, v = big[nm]
        outs = _adam_kernel(w, g, m, v, "adam_" + nm)
        big_out[nm] = tuple((t.T if nm == "w_in" else t)[None] for t in outs)

    small = {"b_ada": (b_ada, m_b_ada, v_b_ada), "g_mix": (g_mix, m_g_mix, v_g_mix),
             "g_ffn": (g_ffn, m_g_ffn, v_g_ffn), "g_final": (g_final, m_g_final, v_g_final),
             "b_spatial": (b_spatial, m_b_spatial, v_b_spatial), "sinks": (sinks, m_sinks, v_sinks),
             "w_spatial": (w_spatial, m_w_spatial, v_w_spatial)}
    flat_shape = {"g_final": (1, D_MODEL), "b_spatial": (GMLP_GROUPS, CHUNK), "w_spatial": (GMLP_GROUPS * CHUNK, CHUNK)}
    small_out, loss_tile = _small_update_kernel(
        gathered, {nm: tuple(a.reshape(flat_shape.get(nm, a.shape)) for a in small[nm]) for nm in small})
    small_out = {nm: [o.reshape(small[nm][0].shape) for o in small_out[nm]] for nm in small}
    loss = loss_tile[0, 0]

    g1, g2 = gathered[0], gathered[1]
    dmod_all = jnp.concatenate([g1[:, 0], g1[:, 1], g2[:, 5], g2[:, 0], g2[:, 1], g2[:, 2]], axis=1)
    dmod_cols = lax.dynamic_slice(dmod_all, (0, chip * ada_cols), (N_DEV, ada_cols))
    ada = _ada_update_kernel(act.T, dmod_cols, w_ada[0], m_w_ada[0], v_w_ada[0])
    big_out["w_ada"] = tuple(t[None] for t in ada)

    order = ["w_ada", "b_ada", "g_mix", "w_in", "w_spatial", "b_spatial", "sinks", "w_out", "g_ffn",
             "w_ff1", "w_ff2", "g_final"]

    def leaf(nm, k):
        return big_out[nm][k] if nm in big_out else small_out[nm][k]

    outs = [loss, grad_x[None]]
    for k in range(4):
        outs += [leaf(nm, k) for nm in order]
    return tuple(outs)
```

```python
import math
from typing import Callable, NamedTuple

import jax
import jax.numpy as jnp
from jax import lax
from jax.experimental import pallas as pl
from jax.experimental.pallas import tpu as pltpu

F32 = jnp.float32
MXU_DTYPE = jnp.bfloat16
WEIGHT_COMM_DTYPE = jnp.bfloat16
GRAD_COMM_DTYPE = jnp.bfloat16

D_MODEL = 1024
D_FF = 4096
HEAD_DIM = 64
GMLP_GROUPS = 8
GMLP_WIDTH = 512
CHUNK = 128
N_Q_HEADS = 8
N_KV_HEADS = 2
ATTN_WIDTH = 512
KV_WIDTH = 128
ROT_DIM = 16
ROPE_THETA = 500000.0
IN_PROJ_WIDTH = 1792
N_MOD = 6
EPS = 1e-5
N_CHIPS = 4
N_DEV = 8
LANES = 128
W_IN_BLOCK = IN_PROJ_WIDTH // N_CHIPS

ADAM_LR = 0.001
ADAM_B1 = 0.9
ADAM_B2 = 0.999
ADAM_EPS = 1e-08
ADAM_WD = 0.01
ADAM_STEP = 10

VMEM_LIMIT_BYTES = 58 * 1024 * 1024
MESH = pl.DeviceIdType.MESH


def _params(*semantics):
    return pltpu.CompilerParams(dimension_semantics=semantics, vmem_limit_bytes=VMEM_LIMIT_BYTES)


def _dot(a, b):
    return jnp.dot(a.astype(MXU_DTYPE), b.astype(MXU_DTYPE), preferred_element_type=F32)


def _dot_nt(a, b):
    return lax.dot_general(a.astype(MXU_DTYPE), b.astype(MXU_DTYPE), (((1,), (1,)), ((), ())),
                           preferred_element_type=F32)


def _dot_tn(a, b):
    return lax.dot_general(a.astype(MXU_DTYPE), b.astype(MXU_DTYPE), (((0,), (0,)), ((), ())),
                           preferred_element_type=F32)


def _full(shape):
    return pl.BlockSpec(shape, lambda *_: (0,) * len(shape))


def _any():
    return pl.BlockSpec(memory_space=pl.ANY)


def _rowsum(v):
    return jnp.sum(v, axis=0, keepdims=True)


def _mean_last(v):
    return jnp.mean(v, axis=-1, keepdims=True)


class _Comm(NamedTuple):
    operands: tuple
    out_shapes: tuple
    n_sems: int
    make: Callable
    in_place: int = 0


def _hosted_call(body, comm, *, name, grid, in_specs, out_shape, out_specs, scratch_shapes=(), semantics,
                 n_prefetch=0):
    if comm is None:
        return pl.pallas_call(
            body, name=name, out_shape=out_shape, compiler_params=_params(*semantics),
            grid_spec=pltpu.PrefetchScalarGridSpec(
                num_scalar_prefetch=n_prefetch, grid=grid, in_specs=in_specs, out_specs=out_specs,
                scratch_shapes=list(scratch_shapes)))
    n_in, n_out, n_scr = len(in_specs), len(out_shape), len(scratch_shapes)
    k_in, k_out = len(comm.operands), len(comm.out_shapes)

    def hosted(*refs):
        prefetched, refs = refs[:n_prefetch], refs[n_prefetch:]
        ins, refs = refs[:n_in], refs[n_in:]
        c_ins, refs = refs[:k_in], refs[k_in:]
        outs, refs = refs[:n_out], refs[n_out:]
        c_outs, refs = refs[:k_out], refs[k_out:]
        scratch, (send_sems, recv_sems) = refs[:n_scr], refs[n_scr:]
        first, last = None, None
        for d, size in enumerate(grid):
            at_start, at_end = pl.program_id(d) == 0, pl.program_id(d) == size - 1
            first = at_start if first is None else first & at_start
            last = at_end if last is None else last & at_end

        @pl.when(first)
        def _():
            for cp in comm.make(c_ins, c_outs, send_sems, recv_sems)[0]:
                cp.start()

        body(*prefetched, *ins, *outs, *scratch)

        @pl.when(last)
        def _():
            for wait in comm.make(c_ins, c_outs, send_sems, recv_sems)[1]:
                wait()

    aliases = {n_prefetch + n_in + i: n_out + i for i in range(comm.in_place)}
    call = pl.pallas_call(
        hosted, name=name, out_shape=list(out_shape) + list(comm.out_shapes),
        compiler_params=_params(*semantics), input_output_aliases=aliases,
        grid_spec=pltpu.PrefetchScalarGridSpec(
            num_scalar_prefetch=n_prefetch, grid=grid, in_specs=list(in_specs) + [_any()] * k_in,
            out_specs=list(out_specs) + [_any()] * k_out,
            scratch_shapes=list(scratch_shapes) + [pltpu.SemaphoreType.DMA((comm.n_sems,)),
                                                    pltpu.SemaphoreType.DMA((comm.n_sems,))]))
    return lambda *args: call(*args, *comm.operands)


class _Shifted:
    def __init__(self, base, offset):
        self.base, self.offset = base, offset

    @property
    def at(self):
        return self

    def __getitem__(self, k):
        return self.base.at[self.offset + k]


def _merge_in_place(*jobs):
    assert all(j.in_place == len(j.operands) == len(j.out_shapes) for j in jobs)

    def make(ins, outs, send_sems, recv_sems):
        starts, waits, at, sem = [], [], 0, 0
        for j in jobs:
            n = len(j.operands)
            s, w = j.make(ins[at:at + n], outs[at:at + n], _Shifted(send_sems, sem), _Shifted(recv_sems, sem))
            starts, waits, at, sem = starts + s, waits + w, at + n, sem + j.n_sems
        return starts, waits

    operands = tuple(a for j in jobs for a in j.operands)
    return _Comm(operands, tuple(s for j in jobs for s in j.out_shapes), sum(j.n_sems for j in jobs), make,
                 in_place=len(operands))


def _mesh_place():
    x, y, c = lax.axis_index("x"), lax.axis_index("y"), lax.axis_index("c")
    return x, y, c, [(1 - x, y), (x, 1 - y), (1 - x, 1 - y)]


def _gather_job(bufs):
    per = 4

    def make(ins, outs, send_sems, recv_sems):
        del ins
        x, y, c, chips = _mesh_place()
        starts, waits = [], []
        for a, out in enumerate(outs):
            mine = src = out.at[4 * x + 2 * y + c]
            to = [(x, y, 1 - c)] + [(px, py, c) for px, py in chips]
            sends = [pltpu.make_async_remote_copy(
                src_ref=src, dst_ref=mine, send_sem=send_sems.at[per * a + k],
                recv_sem=recv_sems.at[per * a + k], device_id=dev, device_id_type=MESH)
                for k, dev in enumerate(to)]
            recvs = [pltpu.make_async_remote_copy(
                src_ref=src, dst_ref=out.at[4 * px + 2 * py + pc], send_sem=send_sems.at[per * a + k],
                recv_sem=recv_sems.at[per * a + k], device_id=(px, py, pc), device_id_type=MESH)
                for k, (px, py, pc) in enumerate(to)]
            starts += sends
            waits += [s.wait_send for s in sends] + [r.wait_recv for r in recvs]
        return starts, waits

    shapes = tuple(jax.ShapeDtypeStruct(b.shape, b.dtype) for b in bufs)
    return _Comm(tuple(bufs), shapes, per * len(bufs), make, in_place=len(bufs))


def _slots(x, y, c):
    return 4 * x + 2 * y + c, 4 * (1 - x) + 2 * y + c, 4 * x + 2 * (1 - y) + c, 4 * (1 - x) + 2 * (1 - y) + c


def _gather2d_first(halves):
    per = 2

    def make(ins, outs, send_sems, recv_sems):
        x, y, c, _ = _mesh_place()
        me, xn, yn, _ = _slots(x, y, c)
        starts, waits = [], []
        for a, (src, out) in enumerate(zip(ins, outs)):
            blk = src.at[c]
            rows = blk.shape[0] // 2
            upper, lower = pl.ds(0, rows), pl.ds(rows, rows)

            def copy(k, src_ref, dst_ref, dev, a=a):
                return pltpu.make_async_remote_copy(
                    src_ref=src_ref, dst_ref=dst_ref, send_sem=send_sems.at[per * a + k],
                    recv_sem=recv_sems.at[per * a + k], device_id=dev, device_id_type=MESH)

            sends = [copy(0, blk.at[upper], out.at[me, upper], (1 - x, y, c)),
                     copy(1, blk.at[lower], out.at[me, lower], (x, 1 - y, c))]
            recvs = [copy(0, blk.at[upper], out.at[xn, upper], (1 - x, y, c)),
                     copy(1, blk.at[lower], out.at[yn, lower], (x, 1 - y, c))]
            starts += sends
            waits += [s.wait_send for s in sends] + [r.wait_recv for r in recvs]
        return starts, waits

    shapes = tuple(jax.ShapeDtypeStruct((N_DEV,) + h.shape[1:], h.dtype) for h in halves)
    return _Comm(tuple(halves), shapes, per * len(halves), make)


def _gather2d_second(bufs, halves):
    per = 4
    n_arr = len(bufs)

    def make(ins, outs, send_sems, recv_sems):
        x, y, c, _ = _mesh_place()
        me, xn, yn, dg = _slots(x, y, c)
        starts, waits = [], []
        for a, buf in enumerate(outs):
            own = ins[n_arr + a].at[c]
            rows = buf.shape[1] // 2
            upper, lower = pl.ds(0, rows), pl.ds(rows, rows)
            plan = [(own.at[upper], me, upper, (x, 1 - y, c), yn), (buf.at[xn, upper], xn, upper, (x, 1 - y, c), dg),
                    (own.at[lower], me, lower, (1 - x, y, c), xn), (buf.at[yn, lower], yn, lower, (1 - x, y, c), dg)]
            for k, (src, slot, part, dev, landing) in enumerate(plan):
                sems = dict(send_sem=send_sems.at[per * a + k], recv_sem=recv_sems.at[per * a + k],
                            device_id=dev, device_id_type=MESH)
                send = pltpu.make_async_remote_copy(src_ref=src, dst_ref=buf.at[slot, part], **sems)
                arrival = pltpu.make_async_remote_copy(src_ref=src, dst_ref=buf.at[landing, part], **sems)
                starts.append(send)
                waits += [send.wait_send, arrival.wait_recv]
        return starts, waits

    shapes = tuple(jax.ShapeDtypeStruct(b.shape, b.dtype) for b in bufs)
    return _Comm(tuple(bufs) + tuple(halves), shapes, per * n_arr, make, in_place=n_arr)


def _gather_forward(bufs, name):
    n_arr = len(bufs)

    def body(*refs):
        outs = refs[n_arr:2 * n_arr]
        send_sems, recv_sems = refs[2 * n_arr:]
        x, y, c, chips = _mesh_place()
        sends, recvs = [], []
        for a, buf in enumerate(outs):
            for j, (px, py) in enumerate(chips):
                mine, theirs = buf.at[4 * px + 2 * py + c], buf.at[4 * px + 2 * py + 1 - c]
                sems = dict(send_sem=send_sems.at[3 * a + j], recv_sem=recv_sems.at[3 * a + j],
                            device_id=(x, y, 1 - c), device_id_type=MESH)
                sends.append(pltpu.make_async_remote_copy(src_ref=mine, dst_ref=mine, **sems))
                recvs.append(pltpu.make_async_remote_copy(src_ref=mine, dst_ref=theirs, **sems))
        for cp in sends:
            cp.start()
        for s, r in zip(sends, recvs):
            s.wait_send()
            r.wait_recv()

    return pl.pallas_call(
        body, name=name, out_shape=[jax.ShapeDtypeStruct(b.shape, b.dtype) for b in bufs],
        in_specs=[_any()] * n_arr, out_specs=[_any()] * n_arr,
        input_output_aliases={a: a for a in range(n_arr)},
        scratch_shapes=[pltpu.SemaphoreType.DMA((3 * n_arr,)), pltpu.SemaphoreType.DMA((3 * n_arr,))],
    )(*bufs)


def _scatter_job(chip_sums):
    def make(ins, outs, send_sems, recv_sems):
        x, y, c, chips = _mesh_place()
        copies = [pltpu.make_async_remote_copy(
            src_ref=src.at[2 * px + py], dst_ref=out.at[j], send_sem=send_sems.at[3 * a + j],
            recv_sem=recv_sems.at[3 * a + j], device_id=(px, py, c), device_id_type=MESH)
            for a, (src, out) in enumerate(zip(ins, outs)) for j, (px, py) in enumerate(chips)]
        return copies, [cp.wait for cp in copies]

    return _Comm(tuple(chip_sums), tuple(jax.ShapeDtypeStruct((3,) + s.shape[1:], s.dtype) for s in chip_sums),
                 3 * len(chip_sums), make)


def _all_gather8(blocks, name, split=False, forward=(), riders=(), skip_own=()):
    n_arr, n_fwd = len(blocks), len(forward)
    splits = list(split) if isinstance(split, (list, tuple)) else [split] * n_arr
    own_slots = [a not in skip_own for a in range(n_arr)]
    rider_in = sum(len(r.operands) for r in riders)
    rider_out = sum(len(r.out_shapes) for r in riders)

    def body(*refs):
        x_refs, refs = refs[:n_arr], refs[n_arr + n_fwd:]
        r_ins, refs = refs[:rider_in], refs[rider_in:]
        out_refs, refs = refs[:n_arr], refs[n_arr:]
        fwd_refs, refs = refs[:n_fwd], refs[n_fwd:]
        r_outs, refs = refs[:rider_out], refs[rider_out:]
        (send_sems, recv_sems, local_sems), rider_sems = refs[:3], refs[3:]
        x, y, c, chips = _mesh_place()
        me, sibling = (x, y, c), (x, y, 1 - c)
        rider_waits, i0, o0 = [], 0, 0
        for n, job in enumerate(riders):
            k_in, k_out = len(job.operands), len(job.out_shapes)
            starts, waits = job.make(r_ins[i0:i0 + k_in], r_outs[o0:o0 + k_out],
                                     rider_sems[2 * n], rider_sems[2 * n + 1])
            for cp in starts:
                cp.start()
            rider_waits += waits
            i0, o0 = i0 + k_in, o0 + k_out
        passing = []
        for f, buf in enumerate(fwd_refs):
            for j, (px, py) in enumerate(chips):
                mine, theirs = buf.at[4 * px + 2 * py + c], buf.at[4 * px + 2 * py + 1 - c]
                sems = dict(send_sem=send_sems.at[7 * n_arr + 3 * f + j], recv_sem=recv_sems.at[7 * n_arr + 3 * f + j],
                            device_id=sibling, device_id_type=MESH)
                passing.append((pltpu.make_async_remote_copy(src_ref=mine, dst_ref=mine, **sems),
                                pltpu.make_async_remote_copy(src_ref=mine, dst_ref=theirs, **sems)))
        for send, _ in passing:
            send.start()
        arrays = []
        for a, (x_ref, out_ref) in enumerate(zip(x_refs, out_refs)):
            src_mine = x_ref.at[c] if splits[a] else x_ref

            def copy(k, blk, to, src=None, a=a, out_ref=out_ref):
                dst = out_ref.at[4 * blk[0] + 2 * blk[1] + blk[2]]
                return pltpu.make_async_remote_copy(
                    src_ref=dst if src is None else src, dst_ref=dst,
                    send_sem=send_sems.at[7 * a + k], recv_sem=recv_sems.at[7 * a + k],
                    device_id=to, device_id_type=MESH)

            mine = pltpu.make_async_copy(src_mine, out_ref.at[4 * x + 2 * y + c], local_sems.at[a])
            first = [copy(0, me, sibling, src=src_mine)] if own_slots[a] else []
            first += [copy(1 + j, me, (*chip, c), src=src_mine) for j, chip in enumerate(chips)]
            for cp in first + ([mine] if own_slots[a] else []):
                cp.start()
            arrays.append((copy, mine, first, own_slots[a]))
        sent = []
        for copy, mine, first, own in arrays:
            passed = [copy(4 + j, (*chip, c), sibling) for j, chip in enumerate(chips)]
            for j, chip in enumerate(chips):
                copy(1 + j, (*chip, c), me).wait_recv()
                passed[j].start()
            sent += first + passed
        for copy, mine, first, own in arrays:
            if own:
                copy(0, sibling, me).wait_recv()
                mine.wait()
            for j, chip in enumerate(chips):
                copy(4 + j, (*chip, 1 - c), me).wait_recv()
        for cp in sent:
            cp.wait_send()
        for send, arrival in passing:
            send.wait_send()
            arrival.wait_recv()
        for wait in rider_waits:
            wait()

    n_sems = 7 * n_arr + 3 * n_fwd
    rider_operands = [a for r in riders for a in r.operands]
    rider_shapes = [s for r in riders for s in r.out_shapes]
    return pl.pallas_call(
        body, name=name,
        out_shape=[jax.ShapeDtypeStruct((N_DEV,) + tuple(b.shape[1:] if s else b.shape), b.dtype)
                   for b, s in zip(blocks, splits)]
        + [jax.ShapeDtypeStruct(f.shape, f.dtype) for f in forward] + rider_shapes,
        in_specs=[_any()] * (n_arr + n_fwd + rider_in), out_specs=[_any()] * (n_arr + n_fwd + rider_out),
        input_output_aliases={n_arr + f: n_arr + f for f in range(n_fwd)},
        scratch_shapes=[pltpu.SemaphoreType.DMA((n_sems,)), pltpu.SemaphoreType.DMA((n_sems,)),
                        pltpu.SemaphoreType.DMA((n_arr,))]
        + [pltpu.SemaphoreType.DMA((r.n_sems,)) for r in riders for _ in range(2)],
    )(*blocks, *forward, *rider_operands)


def _share_job(bufs):
    def make(ins, outs, send_sems, recv_sems):
        del ins
        x, y, c, _ = _mesh_place()
        sems = lambda a: dict(send_sem=send_sems.at[a], recv_sem=recv_sems.at[a],
                              device_id=(x, y, 1 - c), device_id_type=MESH)
        sends = [pltpu.make_async_remote_copy(src_ref=o.at[c], dst_ref=o.at[c], **sems(a)) for a, o in enumerate(outs)]
        arrivals = [pltpu.make_async_remote_copy(src_ref=o.at[c], dst_ref=o.at[1 - c], **sems(a))
                    for a, o in enumerate(outs)]
        return sends, [s.wait_send for s in sends] + [r.wait_recv for r in arrivals]

    shapes = tuple(jax.ShapeDtypeStruct(b.shape, b.dtype) for b in bufs)
    return _Comm(tuple(bufs), shapes, len(bufs), make, in_place=len(bufs))


def _sibling_share(bufs, name):
    n_arr = len(bufs)

    def body(*refs):
        out_refs = refs[n_arr:2 * n_arr]
        send_sems, recv_sems = refs[2 * n_arr:]
        x, y, c = lax.axis_index("x"), lax.axis_index("y"), lax.axis_index("c")
        copies = [pltpu.make_async_remote_copy(
            src_ref=out_refs[a].at[c], dst_ref=out_refs[a].at[c],
            send_sem=send_sems.at[a], recv_sem=recv_sems.at[a],
            device_id=(x, y, 1 - c), device_id_type=MESH) for a in range(n_arr)]
        for cp in copies:
            cp.start()
        for a in range(n_arr):
            pltpu.make_async_remote_copy(
                src_ref=out_refs[a].at[c], dst_ref=out_refs[a].at[1 - c],
                send_sem=send_sems.at[a], recv_sem=recv_sems.at[a],
                device_id=(x, y, 1 - c), device_id_type=MESH).wait()

    return pl.pallas_call(
        body, name=name,
        out_shape=[jax.ShapeDtypeStruct(b.shape, b.dtype) for b in bufs],
        in_specs=[_any()] * n_arr, out_specs=[_any()] * n_arr,
        input_output_aliases={a: a for a in range(n_arr)},
        scratch_shapes=[pltpu.SemaphoreType.DMA((n_arr,)), pltpu.SemaphoreType.DMA((n_arr,))],
    )(*bufs)


def _gelu_tanh(z):
    k = math.sqrt(2.0 / math.pi)
    t = jnp.tanh(k * (z + 0.044715 * (z * z * z)))
    return 0.5 * z * (1.0 + t), t


def _gelu_tanh_grad(z, t):
    k = math.sqrt(2.0 / math.pi)
    return 0.5 * (1.0 + t) + 0.5 * z * (1.0 - t * t) * (k * (1.0 + 3.0 * 0.044715 * (z * z)))


def _rope_angle_kernel(pos_row, invf_col):
    seq = pos_row.shape[1]

    def body(p_ref, f_ref, cos_ref, sin_ref):
        ang = p_ref[...].astype(F32) * f_ref[...]
        cos_ref[...] = jnp.cos(ang)
        sin_ref[...] = jnp.sin(ang)

    return pl.pallas_call(
        body, name="rope_angles", grid=(1,), out_shape=[jax.ShapeDtypeStruct((ROT_DIM // 2, seq), F32)] * 2,
        in_specs=[_full((1, seq)), _full((ROT_DIM // 2, 1))], out_specs=[_full((ROT_DIM // 2, seq))] * 2,
        compiler_params=_params("arbitrary"),
    )(pos_row, invf_col)


def _rope_lane_tables(cos, sin):
    cos_t, sin_t = cos.T, sin.T
    seq, half = cos_t.shape
    ones = jnp.ones((seq, HEAD_DIM - ROT_DIM), F32)
    c64 = jnp.concatenate([cos_t, cos_t, ones], axis=1)
    s1 = jnp.concatenate([sin_t, jnp.zeros((seq, HEAD_DIM - half), F32)], axis=1)
    s2 = jnp.concatenate([jnp.zeros((seq, half), F32), sin_t, jnp.zeros((seq, HEAD_DIM - ROT_DIM), F32)], axis=1)
    return jnp.concatenate([jnp.tile(t, (1, LANES // HEAD_DIM)) for t in (c64, s1, s2)], axis=1)


def _rope_apply(t, tab, sign):
    reps = t.shape[1] // LANES
    c_tab, s1, s2 = (jnp.tile(tab[:, LANES * k:LANES * (k + 1)], (1, reps)) if reps > 1
                     else tab[:, LANES * k:LANES * (k + 1)] for k in range(3))
    half = ROT_DIM // 2
    up = pltpu.roll(t, t.shape[1] - half, 1)
    down = pltpu.roll(t, half, 1)
    return t * c_tab + sign * (down * s2 - up * s1)


def _lane_masks(shape):
    lane = lax.broadcasted_iota(jnp.int32, shape, 1)
    return lane < HEAD_DIM, lane >= HEAD_DIM


HEADS_PER_GROUP = N_Q_HEADS // N_KV_HEADS
ATTN_SCALE = 1.0 / math.sqrt(HEAD_DIM)


def _attn_bias_t(first_block):
    kj = lax.broadcasted_iota(jnp.int32, (2 * CHUNK, CHUNK), 0)
    qi = lax.broadcasted_iota(jnp.int32, (2 * CHUNK, CHUNK), 1)
    ok = (kj > qi) & (kj <= qi + CHUNK)
    if first_block is not None:
        ok = ok & (jnp.logical_not(first_block) | (kj >= CHUNK))
    return jnp.tile(jnp.where(ok, 0.0, -jnp.inf), (1, HEADS_PER_GROUP))


def _group_rows(x, g, lo, hi):
    rows = []
    for r in range(HEADS_PER_GROUP):
        h = HEADS_PER_GROUP * g + r
        pair = x[:, LANES * (h // 2):LANES * (h // 2 + 1)]
        rows.append(jnp.where(hi if h % 2 else lo, pair, 0.0))
    return jnp.concatenate(rows, axis=0)


def _pairs_from_rows(rows, lo):
    return [jnp.where(lo, rows[2 * CHUNK * k:2 * CHUNK * k + CHUNK], rows[2 * CHUNK * k + CHUNK:2 * CHUNK * (k + 1)])
            for k in range(HEADS_PER_GROUP // 2)]


def _group_dup(a, b, g, lo2):
    return jnp.where(lo2, a, b) if g == 0 else jnp.where(lo2, b, a)


def _sink_row(sink_ref, g):
    return jnp.concatenate([sink_ref[HEADS_PER_GROUP * g + r:HEADS_PER_GROUP * g + r + 1, :]
                            for r in range(HEADS_PER_GROUP)], axis=1)


def _attn_probs_t(k_dup, q_rows, bias_t, sink_row):
    s_t = _dot_nt(k_dup, q_rows) * ATTN_SCALE + bias_t
    m = jnp.maximum(jnp.max(s_t, axis=0, keepdims=True), sink_row)
    p = jnp.exp(s_t - m)
    e_sink = jnp.exp(sink_row - m)
    inv = 1.0 / (jnp.sum(p, axis=0, keepdims=True) + e_sink)
    return p * inv, e_sink * inv


def _sgu_forward_pair(wm, vp, j):
    lo, hi = _lane_masks(vp.shape)
    lhs = jnp.concatenate([wm[2 * j], wm[2 * j + 1]], axis=1)
    rhs = jnp.concatenate([jnp.where(lo, vp, 0.0), jnp.where(hi, vp, 0.0)], axis=0)
    return _dot(lhs, rhs)


def _masked_spatial(w_ref):
    t = lax.broadcasted_iota(jnp.int32, (CHUNK, CHUNK), 0)
    s = lax.broadcasted_iota(jnp.int32, (CHUNK, CHUNK), 1)
    tril = s <= t
    return [jnp.where(tril, w_ref[g], 0.0) for g in range(GMLP_GROUPS)], tril, s >= t


def _mod_kernel(c_all, w_shard, b_shard, comm=None):
    n = w_shard.shape[1]
    tn = 512

    def body(c_ref, w_ref, b_ref, mod_ref, act_ref):
        cv = c_ref[...]
        act = cv * (1.0 / (1.0 + jnp.exp(-cv)))
        act_ref[...] = act
        mod_ref[...] = _dot(act, w_ref[...]) + b_ref[...]

    return _hosted_call(
        body, comm, name="ada_mod", grid=(n // tn,),
        out_shape=[jax.ShapeDtypeStruct((N_DEV, n), F32), jax.ShapeDtypeStruct((N_DEV, D_MODEL), F32)],
        in_specs=[_full((N_DEV, D_MODEL)), pl.BlockSpec((D_MODEL, tn), lambda i: (0, i)),
                  pl.BlockSpec((1, tn), lambda i: (0, i))],
        out_specs=[pl.BlockSpec((N_DEV, tn), lambda i: (0, i)), _full((N_DEV, D_MODEL))],
        semantics=("arbitrary",),
    )(c_all, w_shard, b_shard)


def _load_chip_blocks(chip_ref, gathered, local, dsts, sems, first_sem=0):
    for k, dst in enumerate(dsts):
        @pl.when(chip_ref[0] == k)
        def _():
            pltpu.make_async_copy(local, dst, sems.at[first_sem + k]).start()

        @pl.when(chip_ref[0] != k)
        def _():
            pltpu.make_async_copy(gathered.at[k], dst, sems.at[first_sem + k]).start()
    return [pltpu.make_async_copy(local, dst, sems.at[first_sem + k]).wait for k, dst in enumerate(dsts)]


def _w_in_rows(w_scr):
    return [w_scr.at[pl.ds(W_IN_BLOCK * k, W_IN_BLOCK)] for k in range(N_CHIPS)]


def _in_proj_kernel(x, vecs, chip_idx, w_in_gathered, w_in_local, comm=None):
    seq = x.shape[0]
    tm = min(seq, 1024)

    def body(chip_ref, x_ref, v_ref, wg_ref, wl_ref, proj_ref, h_ref, w_ref, sems):
        @pl.when(pl.program_id(0) == 0)
        def _():
            for wait in _load_chip_blocks(chip_ref, wg_ref, wl_ref, _w_in_rows(w_ref), sems):
                wait()

        xv = x_ref[...]
        rstd = lax.rsqrt(_mean_last(xv * xv) + EPS)
        n1 = (xv * rstd) * v_ref[0:1, :]
        h = n1 * (1.0 + v_ref[2:3, :]) + v_ref[1:2, :]
        hb = h.astype(MXU_DTYPE)
        h_ref[...] = hb
        proj_ref[...] = _dot_nt(hb, w_ref[...])

    return _hosted_call(
        body, comm, name="in_proj", grid=(seq // tm,), n_prefetch=1,
        out_shape=[jax.ShapeDtypeStruct((seq, IN_PROJ_WIDTH), F32),
                   jax.ShapeDtypeStruct((seq, D_MODEL), MXU_DTYPE)],
        in_specs=[pl.BlockSpec((tm, D_MODEL), lambda i, chip: (i, 0)), _full((8, D_MODEL)), _any(), _any()],
        out_specs=[pl.BlockSpec((tm, IN_PROJ_WIDTH), lambda i, chip: (i, 0)),
                   pl.BlockSpec((tm, D_MODEL), lambda i, chip: (i, 0))],
        scratch_shapes=[pltpu.VMEM((IN_PROJ_WIDTH, D_MODEL), MXU_DTYPE), pltpu.SemaphoreType.DMA((N_CHIPS,))],
        semantics=("arbitrary",),
    )(chip_idx, x, vecs, w_in_gathered, w_in_local)


MIXER_BLOCKS_PER_STEP = 2
KV_START = 2 * GMLP_WIDTH + ATTN_WIDTH


def _mixer_fwd_kernel(proj, rope_tab, w_spatial, bias_full, sink_rows, comm=None):
    seq = proj.shape[0]
    per = MIXER_BLOCKS_PER_STEP
    steps = seq // (CHUNK * per)
    kv_col = KV_START // (2 * KV_WIDTH)

    def body(proj_ref, prev_ref, tab_ref, ptab_ref, w_ref, bias_ref, sink_ref, cat_ref):
        i = pl.program_id(0)
        wm, _, _ = _masked_spatial(w_ref)
        lo, hi = _lane_masks((CHUNK, LANES))
        lo2, _ = _lane_masks((2 * CHUNK, LANES))
        o = 2 * GMLP_WIDTH
        for s in range(per):
            rows, before = slice(CHUNK * s, CHUNK * (s + 1)), slice(CHUNK * (s - 1), CHUNK * s)
            for j in range(GMLP_GROUPS // 2):
                cols = slice(LANES * j, LANES * (j + 1))
                vcols = slice(GMLP_WIDTH + LANES * j, GMLP_WIDTH + LANES * (j + 1))
                u, _ = _gelu_tanh(proj_ref[rows, cols])
                vp, _ = _gelu_tanh(proj_ref[rows, vcols])
                sv = _sgu_forward_pair(wm, vp, j) + bias_ref[:, cols]
                cat_ref[rows, cols] = (u * sv).astype(cat_ref.dtype)
            tab = tab_ref[rows, :]
            if s == 0:
                prev_kv, prev_tab, first = prev_ref[...], ptab_ref[...], i == 0
            else:
                prev_kv, prev_tab, first = proj_ref[before, KV_START:KV_START + 2 * KV_WIDTH], tab_ref[before, :], None
            q_r = _rope_apply(proj_ref[rows, o:o + ATTN_WIDTH], tab, 1.0)
            k_cur = _rope_apply(proj_ref[rows, KV_START:KV_START + KV_WIDTH], tab, 1.0)
            k_prev = _rope_apply(prev_kv[:, 0:KV_WIDTH], prev_tab, 1.0)
            k_a = jnp.concatenate([k_prev, k_cur], axis=0)
            v_a = jnp.concatenate([prev_kv[:, KV_WIDTH:2 * KV_WIDTH],
                                   proj_ref[rows, KV_START + KV_WIDTH:KV_START + 2 * KV_WIDTH]], axis=0)
            k_b = pltpu.roll(k_a, HEAD_DIM, 1)
            v_b = pltpu.roll(v_a, HEAD_DIM, 1)
            bias_t = _attn_bias_t(first)
            for g in range(N_KV_HEADS):
                p_t, _ = _attn_probs_t(_group_dup(k_a, k_b, g, lo2), _group_rows(q_r, g, lo, hi), bias_t,
                                       _sink_row(sink_ref, g))
                o_t = _dot(_group_dup(v_a, v_b, g, lo2).T, p_t)
                for k, pair in enumerate(_pairs_from_rows(o_t.T, lo)):
                    c0 = GMLP_WIDTH + LANES * (2 * g + k)
                    cat_ref[rows, c0:c0 + LANES] = pair.astype(cat_ref.dtype)

    return _hosted_call(
        body, comm, name="mixer_fwd", grid=(steps,),
        out_shape=[jax.ShapeDtypeStruct((seq, D_MODEL), MXU_DTYPE)],
        in_specs=[pl.BlockSpec((CHUNK * per, IN_PROJ_WIDTH), lambda i: (i, 0)),
                  pl.BlockSpec((CHUNK, 2 * KV_WIDTH), lambda i: (jnp.maximum(per * i - 1, 0), kv_col)),
                  pl.BlockSpec((CHUNK * per, 3 * LANES), lambda i: (i, 0)),
                  pl.BlockSpec((CHUNK, 3 * LANES), lambda i: (jnp.maximum(per * i - 1, 0), 0)),
                  _full((GMLP_GROUPS, CHUNK, CHUNK)), _full((CHUNK, GMLP_WIDTH)),
                  _full((N_Q_HEADS, LANES))],
        out_specs=[pl.BlockSpec((CHUNK * per, D_MODEL), lambda i: (i, 0))],
        semantics=("arbitrary",),
    )(proj, proj, rope_tab, rope_tab, w_spatial, bias_full, sink_rows)


def _trunk_kernel(x, target, cat, vecs, chip_idx, gathered, local):
    seq = x.shape[0]
    tm = 256
    nj = D_FF // D_MODEL
    out_rows = D_MODEL // N_CHIPS

    def body(chip_ref, x_ref, t_ref, cat_ref, v_ref, g_out, g_w1, g_w2, l_out, l_w1, l_w2,
             dx1_ref, dcat_ref, dmix_ref, h2_ref, r_ref, da_ref, dff_ref, sums_ref,
             wout, w1, w2, a_scr, sem):
        i = pl.program_id(0)

        @pl.when(i == 0)
        def _():
            waits = _load_chip_blocks(chip_ref, g_out, l_out,
                                      [wout.at[pl.ds(out_rows * k, out_rows)] for k in range(N_CHIPS)], sem)
            waits += _load_chip_blocks(chip_ref, g_w1, l_w1, [w1.at[k] for k in range(N_CHIPS)], sem, N_CHIPS)
            waits += _load_chip_blocks(chip_ref, g_w2, l_w2, [w2.at[k] for k in range(N_CHIPS)], sem, 2 * N_CHIPS)
            for wait in waits:
                wait()
            sums_ref[...] = jnp.zeros_like(sums_ref)

        gate1, shift2, scale2 = v_ref[0:1, :], v_ref[1:2, :], v_ref[2:3, :]
        gate2, g_ffn, g_final = v_ref[3:4, :], v_ref[4:5, :], v_ref[5:6, :]

        mix = _dot(cat_ref[...], wout[...])
        x1 = x_ref[...] + gate1 * mix
        rstd2 = lax.rsqrt(_mean_last(x1 * x1) + EPS)
        xh2 = x1 * rstd2
        n2 = xh2 * g_ffn
        h2b = (n2 * (1.0 + scale2) + shift2).astype(MXU_DTYPE)
        h2_ref[...] = h2b
        ff = jnp.zeros((tm, D_MODEL), F32)
        for j in range(nj):
            a = _dot(h2b, w1[j])
            a_scr[j] = a
            relu = jnp.maximum(a, 0.0)
            rb = (relu * relu).astype(MXU_DTYPE)
            r_ref[:, D_MODEL * j:D_MODEL * (j + 1)] = rb
            ff = ff + _dot(rb, w2[j])
        x2 = x1 + gate2 * ff
        rstd3 = lax.rsqrt(_mean_last(x2 * x2) + EPS)
        xh3 = x2 * rstd3
        err = xh3 * g_final - t_ref[...]
        loss = 0.5 * _rowsum(_mean_last(err * err))
        dy = err * (1.0 / D_MODEL)
        dxh3 = dy * g_final
        dx2 = rstd3 * (dxh3 - xh3 * _mean_last(dxh3 * xh3))
        dffb = (dx2 * gate2).astype(MXU_DTYPE)
        dff_ref[...] = dffb
        dh2 = jnp.zeros((tm, D_MODEL), F32)
        for j in range(nj):
            dr = _dot_nt(dffb, w2[j])
            dab = (dr * (2.0 * jnp.maximum(a_scr[j], 0.0))).astype(MXU_DTYPE)
            da_ref[:, D_MODEL * j:D_MODEL * (j + 1)] = dab
            dh2 = dh2 + _dot_nt(dab, w1[j])
        dn2 = dh2 * (1.0 + scale2)
        dxh2 = dn2 * g_ffn
        dx1 = dx2 + rstd2 * (dxh2 - xh2 * _mean_last(dxh2 * xh2))
        dx1_ref[...] = dx1
        dmixb = (dx1 * gate1).astype(MXU_DTYPE)
        dmix_ref[...] = dmixb
        dcat_ref[...] = _dot_nt(dmixb, wout[...])

        sums_ref[0:1, :] += _rowsum(dh2)
        sums_ref[1:2, :] += _rowsum(dh2 * n2)
        sums_ref[2:3, :] += _rowsum(dx2 * ff)
        sums_ref[3:4, :] += _rowsum(dn2 * xh2)
        sums_ref[4:5, :] += _rowsum(dy * xh3)
        sums_ref[5:6, :] += _rowsum(dx1 * mix)
        sums_ref[6:7, :] += jnp.broadcast_to(loss, (1, D_MODEL))

    tok = lambda w: pl.BlockSpec((tm, w), lambda i, chip: (i, 0))
    return _hosted_call(
        body, None, name="trunk", grid=(seq // tm,), n_prefetch=1,
        out_shape=[jax.ShapeDtypeStruct((seq, D_MODEL), F32), jax.ShapeDtypeStruct((seq, D_MODEL), F32),
                   jax.ShapeDtypeStruct((seq, D_MODEL), MXU_DTYPE), jax.ShapeDtypeStruct((seq, D_MODEL), MXU_DTYPE),
                   jax.ShapeDtypeStruct((seq, D_FF), MXU_DTYPE), jax.ShapeDtypeStruct((seq, D_FF), MXU_DTYPE),
                   jax.ShapeDtypeStruct((seq, D_MODEL), MXU_DTYPE), jax.ShapeDtypeStruct((8, D_MODEL), F32)],
        in_specs=[tok(D_MODEL), tok(D_MODEL), tok(D_MODEL), _full((8, D_MODEL))] + [_any()] * 6,
        out_specs=[tok(D_MODEL), tok(D_MODEL), tok(D_MODEL), tok(D_MODEL), tok(D_FF), tok(D_FF), tok(D_MODEL),
                   _full((8, D_MODEL))],
        scratch_shapes=[pltpu.VMEM((D_MODEL, D_MODEL), MXU_DTYPE), pltpu.VMEM((nj, D_MODEL, D_MODEL), MXU_DTYPE),
                        pltpu.VMEM((nj, D_MODEL, D_MODEL), MXU_DTYPE), pltpu.VMEM((nj, tm, D_MODEL), F32),
                        pltpu.SemaphoreType.DMA((3 * N_CHIPS,))],
        semantics=("arbitrary",),
    )(chip_idx, x, target, cat, vecs, *gathered, *local)


def _mixer_bwd_kernel(proj, rope_tab, dcat, w_spatial, w_spatial_t, bias_full, sink_rows, dev_idx, comm=None):
    seq = proj.shape[0]
    per = MIXER_BLOCKS_PER_STEP
    steps = seq // (CHUNK * per)
    kv_col = KV_START // (2 * KV_WIDTH)

    def body(dev_ref, proj_ref, prev_ref, tab_ref, ptab_ref, dcat_ref, w_ref, wt_ref, bias_ref, sink_ref,
             dproj_ref, dw_ref, db_ref, dsink_ref, carry):
        del dev_ref
        step = pl.program_id(0)

        @pl.when(step == 0)
        def _():
            carry[...] = jnp.zeros_like(carry)
            dw_ref[...] = jnp.zeros_like(dw_ref)
            db_ref[...] = jnp.zeros_like(db_ref)
            dsink_ref[...] = jnp.zeros_like(dsink_ref)

        for s in reversed(range(per)):
            rows = pl.ds(CHUNK * s, CHUNK)
            if s == 0:
                before, before_tab, first = prev_ref, ptab_ref, step == steps - 1
            else:
                before = proj_ref.at[pl.ds(CHUNK * (s - 1), CHUNK), pl.ds(KV_START, 2 * KV_WIDTH)]
                before_tab, first = tab_ref.at[pl.ds(CHUNK * (s - 1), CHUNK)], None
            one_block(proj_ref.at[rows], before, tab_ref.at[rows], before_tab, dcat_ref.at[rows], w_ref, wt_ref,
                      bias_ref, sink_ref, dproj_ref.at[rows], dw_ref, db_ref, dsink_ref, carry, first)

    def one_block(proj_ref, prev_ref, tab_ref, ptab_ref, dcat_ref, w_ref, wt_ref, bias_ref, sink_ref,
                  dproj_ref, dw_ref, db_ref, dsink_ref, carry, first):
        wm, tril, triu = _masked_spatial(w_ref)
        lo, hi = _lane_masks((CHUNK, LANES))
        lane = lax.broadcasted_iota(jnp.int32, (CHUNK, LANES), 1)
        db = jnp.zeros((CHUNK, LANES), F32)
        for j in range(GMLP_GROUPS // 2):
            cols = slice(LANES * j, LANES * (j + 1))
            vcols = slice(GMLP_WIDTH + LANES * j, GMLP_WIDTH + LANES * (j + 1))
            zu, zv = proj_ref[:, cols], proj_ref[:, vcols]
            u, tu = _gelu_tanh(zu)
            vp, tv = _gelu_tanh(zv)
            sv = _sgu_forward_pair(wm, vp, j) + bias_ref[:, cols]
            dout = dcat_ref[:, cols]
            du = dout * sv
            dsv = dout * u
            dsv_lo, dsv_hi = jnp.where(lo, dsv, 0.0), jnp.where(hi, dsv, 0.0)
            lhs_t = jnp.concatenate([jnp.where(triu, wt_ref[2 * j], 0.0),
                                     jnp.where(triu, wt_ref[2 * j + 1], 0.0)], axis=1)
            dv = _dot(lhs_t, jnp.concatenate([dsv_lo, dsv_hi], axis=0))
            dw_ref[2 * j] += jnp.where(tril, _dot_nt(dsv_lo, vp), 0.0)
            dw_ref[2 * j + 1] += jnp.where(tril, _dot_nt(dsv_hi, vp), 0.0)
            db = db + (jnp.where(lane == 2 * j, jnp.sum(dsv_lo, axis=1, keepdims=True), 0.0)
                       + jnp.where(lane == 2 * j + 1, jnp.sum(dsv_hi, axis=1, keepdims=True), 0.0))
            dproj_ref[:, cols] = (du * _gelu_tanh_grad(zu, tu)).astype(dproj_ref.dtype)
            dproj_ref[:, vcols] = (dv * _gelu_tanh_grad(zv, tv)).astype(dproj_ref.dtype)
        db_ref[...] += db
        o = 2 * GMLP_WIDTH
        tab = tab_ref[...]
        q_r = _rope_apply(proj_ref[:, o:o + ATTN_WIDTH], tab, 1.0)
        k_cur = _rope_apply(proj_ref[:, o + ATTN_WIDTH:o + ATTN_WIDTH + KV_WIDTH], tab, 1.0)
        k_prev = _rope_apply(prev_ref[:, 0:KV_WIDTH], ptab_ref[...], 1.0)
        k_a = jnp.concatenate([k_prev, k_cur], axis=0)
        v_a = jnp.concatenate([prev_ref[:, KV_WIDTH:2 * KV_WIDTH],
                               proj_ref[:, o + ATTN_WIDTH + KV_WIDTH:o + ATTN_WIDTH + 2 * KV_WIDTH]], axis=0)
        k_b = pltpu.roll(k_a, HEAD_DIM, 1)
        v_b = pltpu.roll(v_a, HEAD_DIM, 1)
        bias_t = _attn_bias_t(first)
        lo2, _ = _lane_masks((2 * CHUNK, LANES))
        dout_b = dcat_ref[:, GMLP_WIDTH:GMLP_WIDTH + ATTN_WIDTH]
        dk_tot, dv_tot, dq_pairs = [], [], []
        for g in range(N_KV_HEADS):
            k_dup, v_dup = _group_dup(k_a, k_b, g, lo2), _group_dup(v_a, v_b, g, lo2)
            q_rows = _group_rows(q_r, g, lo, hi)
            do_rows = _group_rows(dout_b, g, lo, hi)
            p_t, p_sink = _attn_probs_t(k_dup, q_rows, bias_t, _sink_row(sink_ref, g))
            dp_t = _dot_nt(v_dup, do_rows)
            delta = jnp.sum(p_t * dp_t, axis=0, keepdims=True)
            ds_t = p_t * (dp_t - delta) * ATTN_SCALE
            dsink = -p_sink * delta
            for r in range(HEADS_PER_GROUP):
                h = HEADS_PER_GROUP * g + r
                dsink_ref[h:h + 1, :] += jnp.broadcast_to(
                    jnp.sum(dsink[:, LANES * r:LANES * (r + 1)], axis=1, keepdims=True), (1, LANES))
            dk_full = _dot(ds_t, q_rows)
            dv_full = _dot(p_t, do_rows)
            dk_tot.append(dk_full + pltpu.roll(dk_full, HEAD_DIM, 1))
            dv_tot.append(dv_full + pltpu.roll(dv_full, HEAD_DIM, 1))
            dq_t = _dot(k_dup.T, ds_t)
            dq_pairs += _pairs_from_rows(dq_t.T, lo)
        dk_all = jnp.where(lo2, dk_tot[0], dk_tot[1])
        dv_all = jnp.where(lo2, dv_tot[0], dv_tot[1])
        dk_cur = dk_all[CHUNK:, :] + carry[:, 0:KV_WIDTH]
        dv_cur = dv_all[CHUNK:, :] + carry[:, KV_WIDTH:2 * KV_WIDTH]
        carry[:, 0:KV_WIDTH] = dk_all[:CHUNK, :]
        carry[:, KV_WIDTH:2 * KV_WIDTH] = dv_all[:CHUNK, :]
        dq = _rope_apply(jnp.concatenate(dq_pairs, axis=1), tab, -1.0)
        dproj_ref[:, o:o + ATTN_WIDTH] = dq.astype(dproj_ref.dtype)
        dproj_ref[:, o + ATTN_WIDTH:o + ATTN_WIDTH + KV_WIDTH] = (
            _rope_apply(dk_cur, tab, -1.0).astype(dproj_ref.dtype))
        dproj_ref[:, o + ATTN_WIDTH + KV_WIDTH:o + ATTN_WIDTH + 2 * KV_WIDTH] = dv_cur.astype(dproj_ref.dtype)

    rev = lambda i: steps - 1 - i
    before = lambda i: jnp.maximum(per * rev(i) - 1, 0)
    slot = lambda shape: pl.BlockSpec((None,) + shape, lambda i, d: (d[0],) + (0,) * len(shape))
    return _hosted_call(
        body, comm, name="mixer_bwd", grid=(steps,), n_prefetch=1,
        out_shape=[jax.ShapeDtypeStruct((seq, IN_PROJ_WIDTH), MXU_DTYPE),
                   jax.ShapeDtypeStruct((N_DEV, GMLP_GROUPS, CHUNK, CHUNK), F32),
                   jax.ShapeDtypeStruct((N_DEV, CHUNK, LANES), F32),
                   jax.ShapeDtypeStruct((N_DEV, N_Q_HEADS, LANES), F32)],
        in_specs=[pl.BlockSpec((CHUNK * per, IN_PROJ_WIDTH), lambda i, d: (rev(i), 0)),
                  pl.BlockSpec((CHUNK, 2 * KV_WIDTH), lambda i, d: (before(i), kv_col)),
                  pl.BlockSpec((CHUNK * per, 3 * LANES), lambda i, d: (rev(i), 0)),
                  pl.BlockSpec((CHUNK, 3 * LANES), lambda i, d: (before(i), 0)),
                  pl.BlockSpec((CHUNK * per, D_MODEL), lambda i, d: (rev(i), 0)),
                  _full((GMLP_GROUPS, CHUNK, CHUNK)), _full((GMLP_GROUPS, CHUNK, CHUNK)),
                  _full((CHUNK, GMLP_WIDTH)), _full((N_Q_HEADS, LANES))],
        out_specs=[pl.BlockSpec((CHUNK * per, IN_PROJ_WIDTH), lambda i, d: (rev(i), 0)),
                   slot((GMLP_GROUPS, CHUNK, CHUNK)), slot((CHUNK, LANES)), slot((N_Q_HEADS, LANES))],
        scratch_shapes=[pltpu.VMEM((CHUNK, 2 * KV_WIDTH), F32)],
        semantics=("arbitrary",),
    )(dev_idx, proj, proj, rope_tab, rope_tab, dcat, w_spatial, w_spatial_t, bias_full, sink_rows)


def _in_proj_bwd_kernel(x, dx1, dproj, vecs, chip_idx, w_in_gathered, w_in_local, comm=None):
    seq = x.shape[0]
    tm = min(seq, 1024)

    def body(chip_ref, x_ref, dx1_ref, dp_ref, v_ref, wg_ref, wl_ref, gx_ref, sums_ref, w_ref, sems):
        @pl.when(pl.program_id(0) == 0)
        def _():
            for wait in _load_chip_blocks(chip_ref, wg_ref, wl_ref, _w_in_rows(w_ref), sems):
                wait()
            sums_ref[...] = jnp.zeros_like(sums_ref)

        g_mix, scale1 = v_ref[0:1, :], v_ref[2:3, :]
        dh = _dot(dp_ref[...], w_ref[...])
        xv = x_ref[...]
        rstd = lax.rsqrt(_mean_last(xv * xv) + EPS)
        xh = xv * rstd
        dn1 = dh * (1.0 + scale1)
        dxh = dn1 * g_mix
        gx_ref[...] = dx1_ref[...] + rstd * (dxh - xh * _mean_last(dxh * xh))
        sums_ref[0:1, :] += _rowsum(dh)
        sums_ref[1:2, :] += _rowsum(dh * (xh * g_mix))
        sums_ref[2:3, :] += _rowsum(dn1 * xh)

    tok = lambda w: pl.BlockSpec((tm, w), lambda i, chip: (i, 0))
    return _hosted_call(
        body, comm, name="in_proj_bwd", grid=(seq // tm,), n_prefetch=1,
        out_shape=[jax.ShapeDtypeStruct((seq, D_MODEL), F32), jax.ShapeDtypeStruct((8, D_MODEL), F32)],
        in_specs=[tok(D_MODEL), tok(D_MODEL), tok(IN_PROJ_WIDTH), _full((8, D_MODEL)), _any(), _any()],
        out_specs=[tok(D_MODEL), _full((8, D_MODEL))],
        scratch_shapes=[pltpu.VMEM((IN_PROJ_WIDTH, D_MODEL), MXU_DTYPE), pltpu.SemaphoreType.DMA((N_CHIPS,))],
        semantics=("arbitrary",),
    )(chip_idx, x, dx1, dproj, vecs, w_in_gathered, w_in_local)


class _GradTiles(NamedTuple):
    tm: int
    tn: int
    n_tiles: int
    chips_per_tile: int
    a_index: Callable
    b_index: Callable


def _weight_grad_kernel(a, b, c_idx, name, tiles, comm=None):
    seq = a.shape[0]
    tk = min(seq, 4096)
    nk = seq // tk
    tm, tn, n_tiles, per = tiles.tm, tiles.tn, tiles.n_tiles, tiles.chips_per_tile
    rows = tm // per

    def half(phase, c):
        return phase * c[0] + (1 - phase) * (1 - c[0])

    def body(c_ref, a_ref, b_ref, o_ref, acc, stage, landed, send_sems, recv_sems):
        del c_ref
        phase, t, kk = pl.program_id(0), pl.program_id(1), pl.program_id(2)
        x, y, c, _ = _mesh_place()

        def copy(tile):
            return pltpu.make_async_remote_copy(
                src_ref=stage.at[tile], dst_ref=landed.at[tile], send_sem=send_sems.at[tile],
                recv_sem=recv_sems.at[tile], device_id=(x, y, 1 - c), device_id_type=MESH)

        @pl.when(kk == 0)
        def _():
            acc[...] = jnp.zeros_like(acc)

        acc[...] += _dot_tn(a_ref[...], b_ref[...])

        @pl.when((kk == nk - 1) & (phase == 0))
        def _():
            stage[t] = acc[...].astype(stage.dtype)
            copy(t).start()

        @pl.when((kk == nk - 1) & (phase == 1))
        def _():
            copy(t).wait_recv()
            total = acc[...] + landed[t].astype(F32)
            for q in range(per):
                o_ref[q] = total[rows * q:rows * (q + 1)].astype(o_ref.dtype)

        @pl.when((kk == nk - 1) & (phase == 1) & (t == n_tiles - 1))
        def _():
            for tile in range(n_tiles):
                copy(tile).wait_send()

    out = _hosted_call(
        body, comm, name=name, grid=(2, n_tiles, nk), n_prefetch=1,
        out_shape=[jax.ShapeDtypeStruct((n_tiles * per, rows, tn), GRAD_COMM_DTYPE)],
        in_specs=[pl.BlockSpec((tk, tm), lambda p, t, k, c: (k, tiles.a_index(t, half(p, c)))),
                  pl.BlockSpec((tk, tn), lambda p, t, k, c: (k, tiles.b_index(t, half(p, c))))],
        out_specs=[pl.BlockSpec((per, rows, tn), lambda p, t, k, c: (p * t, 0, 0))],
        scratch_shapes=[pltpu.VMEM((tm, tn), F32), pltpu.VMEM((n_tiles, tm, tn), GRAD_COMM_DTYPE),
                        pltpu.VMEM((n_tiles, tm, tn), GRAD_COMM_DTYPE),
                        pltpu.SemaphoreType.DMA((n_tiles,)), pltpu.SemaphoreType.DMA((n_tiles,))],
        semantics=("arbitrary", "arbitrary", "arbitrary"),
    )(c_idx, a, b)
    return out[0] if comm is None else out


def _row_tile(rows, most=256, sublanes=16):
    return max(t for t in range(sublanes, most + 1, sublanes) if rows % t == 0)


def _adam_update(w, g, m, v):
    m_new = ADAM_B1 * m + (1.0 - ADAM_B1) * g
    v_new = ADAM_B2 * v + (1.0 - ADAM_B2) * (g * g)
    m_hat = m_new / (1.0 - ADAM_B1 ** ADAM_STEP)
    v_hat = v_new / (1.0 - ADAM_B2 ** ADAM_STEP)
    delta = -ADAM_LR * (m_hat / (jnp.sqrt(v_hat) + ADAM_EPS) + ADAM_WD * w)
    return delta, m_new, v_new


def _sum_chips_kernel(own, others, place, name):
    _, r, n = own.shape
    tr = _row_tile(r)

    def body(place_ref, own_ref, oth_ref, o_ref):
        del place_ref
        acc = own_ref[...].astype(F32)
        for k in range(N_CHIPS - 1):
            acc = acc + oth_ref[k].astype(F32)
        o_ref[...] = acc

    return pl.pallas_call(
        body, name=name, out_shape=jax.ShapeDtypeStruct((2, r, n), F32),
        grid_spec=pltpu.PrefetchScalarGridSpec(
            num_scalar_prefetch=1, grid=(r // tr,),
            in_specs=[pl.BlockSpec((None, tr, n), lambda i, p: (p[0], i, 0)),
                      pl.BlockSpec((N_CHIPS - 1, tr, n), lambda i, p: (0, i, 0))],
            out_specs=pl.BlockSpec((None, tr, n), lambda i, p: (p[1], i, 0))),
        compiler_params=_params("parallel"),
    )(place, own, others)


def _adam_kernel(w, g, m, v, name):
    r, n = w.shape
    by_columns = g.shape[1] == r
    tr, tn = _row_tile(g.shape[1], most=512), g.shape[2]

    def body(w_ref, g_ref, m_ref, v_ref, g_out, d_ref, mo_ref, vo_ref):
        gv = g_ref[...]
        g_out[...] = gv
        d_ref[...], mo_ref[...], vo_ref[...] = _adam_update(w_ref[...], gv, m_ref[...], v_ref[...])

    steps = g.shape[1] // tr
    spec = pl.BlockSpec((tr, tn), (lambda h, i: (i, h)) if by_columns else (lambda h, i: (h * steps + i, 0)))
    return pl.pallas_call(
        body, name=name, grid=(2, steps), out_shape=[jax.ShapeDtypeStruct((r, n), F32)] * 4,
        in_specs=[spec, pl.BlockSpec((None, tr, tn), lambda h, i: (h, i, 0)), spec, spec], out_specs=[spec] * 4,
        compiler_params=_params("parallel", "parallel"),
    )(w, g, m, v)


SMALL_PARAMS = ("b_ada", "g_mix", "g_ffn", "g_final", "b_spatial", "sinks", "w_spatial")


def _small_update_kernel(gathered, params):
    shapes = [params[nm][0].shape for nm in SMALL_PARAMS]

    def body(*refs):
        g_refs, refs = refs[:5], refs[5:]
        p_refs, refs = refs[:3 * len(SMALL_PARAMS)], refs[3 * len(SMALL_PARAMS):]
        loss_ref, o_refs = refs[0], refs[1:]

        def total(ref):
            acc = ref[0]
            for k in range(1, N_DEV):
                acc = acc + ref[k]
            return acc

        s1, s2, db, ds, dw = (total(r) for r in g_refs)
        loss_ref[...] = jnp.broadcast_to(s2[6:7, 0:1], loss_ref.shape)
        grads = {"b_ada": [s1[0:1], s1[1:2], s2[5:6], s2[0:1], s2[1:2], s2[2:3]], "g_mix": [s1[2:3]],
                 "g_ffn": [s2[3:4]], "g_final": [s2[4:5]], "b_spatial": [db.T[0:GMLP_GROUPS]],
                 "w_spatial": [dw]}
        lane = lax.broadcasted_iota(jnp.int32, (1, LANES), 1)
        sink_row = jnp.zeros((1, LANES), F32)
        for h in range(N_Q_HEADS):
            sink_row = sink_row + jnp.where(lane == h, ds[h:h + 1, :], 0.0)
        grads["sinks"] = [sink_row[:, 0:N_Q_HEADS]]
        for i, nm in enumerate(SMALL_PARAMS):
            w_ref, m_ref, v_ref = p_refs[3 * i:3 * i + 3]
            outs = o_refs[4 * i:4 * i + 4]
            width = grads[nm][0].shape[1]
            for k, g in enumerate(grads[nm]):
                cols = slice(width * k, width * (k + 1))
                upd = _adam_update(w_ref[:, cols], g, m_ref[:, cols], v_ref[:, cols])
                for o_ref, val in zip(outs, (g,) + upd):
                    o_ref[:, cols] = val

    flat = [a for nm in SMALL_PARAMS for a in params[nm]]
    out_shape = [jax.ShapeDtypeStruct((8, LANES), F32)]
    out_shape += [jax.ShapeDtypeStruct(s, F32) for s in shapes for _ in range(4)]
    outs = pl.pallas_call(
        body, name="small_update", grid=(1,), out_shape=out_shape,
        in_specs=[_full(g.shape) for g in gathered] + [_full(a.shape) for a in flat],
        out_specs=[_full(s.shape) for s in out_shape],
        compiler_params=_params("arbitrary"),
    )(*gathered, *flat)
    return {nm: outs[1 + 4 * i:5 + 4 * i] for i, nm in enumerate(SMALL_PARAMS)}, outs[0]


def _ada_update_kernel(act_t, dmod, w, m, v):
    r, n = w.shape
    tr = 256

    def body(a_ref, d_ref, w_ref, m_ref, v_ref, g_ref, dl_ref, mo_ref, vo_ref):
        g = _dot(a_ref[...], d_ref[...])
        g_ref[...] = g
        dl_ref[...], mo_ref[...], vo_ref[...] = _adam_update(w_ref[...], g, m_ref[...], v_ref[...])

    spec = pl.BlockSpec((tr, n), lambda i: (i, 0))
    return pl.pallas_call(
        body, name="ada_update", grid=(r // tr,), out_shape=[jax.ShapeDtypeStruct((r, n), F32)] * 4,
        in_specs=[pl.BlockSpec((tr, N_DEV), lambda i: (i, 0)), _full((N_DEV, n)), spec, spec, spec],
        out_specs=[spec] * 4, compiler_params=_params("parallel"),
    )(act_t, dmod, w, m, v)


def kernel(x, c, positions, w_ada, b_ada, g_mix, w_in, w_spatial, b_spatial, sinks, w_out, g_ffn, w_ff1, w_ff2, g_final, loss_target, m_w_ada, m_b_ada, m_g_mix, m_w_in, m_w_spatial, m_b_spatial, m_sinks, m_w_out, m_g_ffn, m_w_ff1, m_w_ff2, m_g_final, v_w_ada, v_b_ada, v_g_mix, v_w_in, v_w_spatial, v_b_spatial, v_sinks, v_w_out, v_g_ffn, v_w_ff1, v_w_ff2, v_g_final):
    xi, yi, ci = lax.axis_index("x"), lax.axis_index("y"), lax.axis_index("c")
    chip = 2 * xi + yi
    dev = 2 * chip + ci
    seq = x.shape[1]
    x2, tgt = x[0], loss_target[0]
    ada_cols = w_ada.shape[2]

    big = {"w_in": tuple(a[0].T for a in (w_in, m_w_in, v_w_in)),
           "w_out": (w_out[0], m_w_out[0], v_w_out[0]), "w_ff1": (w_ff1[0], m_w_ff1[0], v_w_ff1[0]),
           "w_ff2": (w_ff2[0], m_w_ff2[0], v_w_ff2[0])}

    def halves(nm):
        r, n = big[nm][0].shape
        return big[nm][0].astype(WEIGHT_COMM_DTYPE).reshape(2, r // 2, n)

    chip_idx = chip.reshape(1).astype(jnp.int32)
    w_in_local = halves("w_in")
    c_all, w_in_g, g_out = _all_gather8([c, w_in_local, halves("w_out")], "gather_first",
                                        split=[False, True, True], skip_own=(1, 2))
    c_all = c_all.reshape(N_DEV, D_MODEL)
    w_in_g, w_in_local = w_in_g.reshape(N_CHIPS, W_IN_BLOCK, D_MODEL), w_in_local.reshape(W_IN_BLOCK, D_MODEL)
    b_shard = lax.dynamic_slice(b_ada, (0, chip * ada_cols), (1, ada_cols))
    mod_part, act = _mod_kernel(c_all, w_ada[0], b_shard)
    mod_all, = _all_gather8([mod_part], "gather_mod")
    mod_me = lax.dynamic_index_in_dim(mod_all[0::2], dev, axis=1, keepdims=False)
    mod_me = mod_me.reshape(N_MOD, D_MODEL)
    shift1, scale1, gate1, shift2, scale2, gate2 = (mod_me[k:k + 1] for k in range(N_MOD))

    zeros_row = jnp.zeros((1, D_MODEL), F32)
    vecs1 = jnp.concatenate([g_mix, shift1, scale1] + [zeros_row] * 5, axis=0)
    vecs2 = jnp.concatenate([gate1, shift2, scale2, gate2, g_ffn, g_final.reshape(1, D_MODEL)]
                            + [zeros_row] * 2, axis=0)
    bias_full = jnp.repeat(b_spatial[0].T, HEAD_DIM, axis=1)
    sink_rows = jnp.broadcast_to(sinks[0][:, None], (N_Q_HEADS, LANES))
    inv_freq = ROPE_THETA ** (-jnp.arange(0, ROT_DIM, 2, dtype=F32) / ROT_DIM)
    rope_tab = _rope_lane_tables(*_rope_angle_kernel(positions, inv_freq.reshape(ROT_DIM // 2, 1)))

    trunk_weights = ["w_out", "w_ff1", "w_ff2"]
    shards = [halves(nm) for nm in trunk_weights]
    proj, hb, *staged = _in_proj_kernel(x2, vecs1, chip_idx, w_in_g, w_in_local,
                                        comm=_gather2d_first(shards[1:]))
    cat, *staged = _mixer_fwd_kernel(proj, rope_tab, w_spatial[0], bias_full, sink_rows,
                                     comm=_gather2d_second(staged, shards[1:]))
    staged = [g_out] + list(_gather_forward(staged, "gather_forward"))
    dx1, dcat, dmix, h2b, rb, dab, dffb, sums2 = _trunk_kernel(
        x2, tgt, cat, vecs2, chip_idx,
        [g.reshape((N_CHIPS,) + big[nm][0].shape) for nm, g in zip(trunk_weights, staged)],
        [s.reshape(big[nm][0].shape) for nm, s in zip(trunk_weights, shards)])

    c_idx = ci.reshape(1).astype(jnp.int32)
    place = jnp.stack([chip, ci]).astype(jnp.int32)
    half_d = D_MODEL // 2
    cs_ff2 = _weight_grad_kernel(rb, dffb, c_idx, "dw_ff2",
                                 _GradTiles(D_MODEL, half_d, N_CHIPS, 1, lambda t, h: t, lambda t, h: h))
    cs_ff1, sc_ff2 = _weight_grad_kernel(
        h2b, dab, c_idx, "dw_ff1",
        _GradTiles(D_MODEL, half_d, N_CHIPS, 1, lambda t, h: 0, lambda t, h: 2 * t + h),
        comm=_scatter_job([cs_ff2]))
    cs_out = _weight_grad_kernel(cat, dmix, c_idx, "dw_out",
                                 _GradTiles(D_MODEL, half_d, 1, N_CHIPS, lambda t, h: 0, lambda t, h: h))
    dproj, dw_spatial, db_lanes, dsink_rows, sc_ff1, sc_out = _mixer_bwd_kernel(
        proj, rope_tab, dcat, w_spatial[0], w_spatial[0].transpose(0, 2, 1), bias_full, sink_rows,
        dev.reshape(1).astype(jnp.int32), comm=_scatter_job([cs_ff1, cs_out]))
    totals = [_sum_chips_kernel(own, oth, place, "grad_sum_" + nm)
              for nm, own, oth in (("w_out", cs_out, sc_out), ("w_ff1", cs_ff1, sc_ff1), ("w_ff2", cs_ff2, sc_ff2))]
    small_slots = [db_lanes, dsink_rows, dw_spatial.reshape(N_DEV, GMLP_GROUPS * CHUNK, CHUNK)]
    cs_in, *rode = _weight_grad_kernel(
        dproj, hb, c_idx, "dw_in",
        _GradTiles(2 * W_IN_BLOCK, half_d, N_CHIPS // 2, 2, lambda t, h: t, lambda t, h: h),
        comm=_merge_in_place(_gather_job(small_slots), _share_job(totals)))
    small_stage1, shared = rode[:len(small_slots)], rode[len(small_slots):]
    grad_x, sums1 = _in_proj_bwd_kernel(x2, dx1, dproj, vecs1, chip_idx, w_in_g, w_in_local)
    *gathered, sc_in = _all_gather8([sums1, sums2], "gather_small", forward=small_stage1,
                                    riders=[_scatter_job([cs_in])])
    total_in = _sum_chips_kernel(cs_in, sc_in, place, "grad_sum_w_in")
    shared = list(_sibling_share([total_in], "grad_share_w_in")) + list(shared)
    names = ["w_in", "w_out", "w_ff1", "w_ff2"]
    big_out = {}
    for nm, g in zip(names, shared):
        w, m, v = big[nm]
        outs = _adam_kernel(w, g, m, v, "adam_" + nm)
        big_out[nm] = tuple((t.T if nm == "w_in" else t)[None] for t in outs)

    small = {"b_ada": (b_ada, m_b_ada, v_b_ada), "g_mix": (g_mix, m_g_mix, v_g_mix),
             "g_ffn": (g_ffn, m_g_ffn, v_g_ffn), "g_final": (g_final, m_g_final, v_g_final),
             "b_spatial": (b_spatial, m_b_spatial, v_b_spatial), "sinks": (sinks, m_sinks, v_sinks),
             "w_spatial": (w_spatial, m_w_spatial, v_w_spatial)}
    flat_shape = {"g_final": (1, D_MODEL), "b_spatial": (GMLP_GROUPS, CHUNK), "w_spatial": (GMLP_GROUPS * CHUNK, CHUNK)}
    small_out, loss_tile = _small_update_kernel(
        gathered, {nm: tuple(a.reshape(flat_shape.get(nm, a.shape)) for a in small[nm]) for nm in small})
    small_out = {nm: [o.reshape(small[nm][0].shape) for o in small_out[nm]] for nm in small}
    loss = loss_tile[0, 0]

    g1, g2 = gathered[0], gathered[1]
    dmod_all = jnp.concatenate([g1[:, 0], g1[:, 1], g2[:, 5], g2[:, 0], g2[:, 1], g2[:, 2]], axis=1)
    dmod_cols = lax.dynamic_slice(dmod_all, (0, chip * ada_cols), (N_DEV, ada_cols))
    ada = _ada_update_kernel(act.T, dmod_cols, w_ada[0], m_w_ada[0], v_w_ada[0])
    big_out["w_ada"] = tuple(t[None] for t in ada)

    order = ["w_ada", "b_ada", "g_mix", "w_in", "w_spatial", "b_spatial", "sinks", "w_out", "g_ffn",
             "w_ff1", "w_ff2", "g_final"]

    def leaf(nm, k):
        return big_out[nm][k] if nm in big_out else small_out[nm][k]

    outs = [loss, grad_x[None]]
    for k in range(4):
        outs += [leaf(nm, k) for nm in order]
    return tuple(outs)
```

```python
import math
from typing import Callable, NamedTuple

import jax
import jax.numpy as jnp
from jax import lax
from jax.experimental import pallas as pl
from jax.experimental.pallas import tpu as pltpu

F32 = jnp.float32
MXU_DTYPE = jnp.bfloat16
WEIGHT_COMM_DTYPE = jnp.bfloat16
GRAD_COMM_DTYPE = jnp.bfloat16

D_MODEL = 1024
D_FF = 4096
HEAD_DIM = 64
GMLP_GROUPS = 8
GMLP_WIDTH = 512
CHUNK = 128
N_Q_HEADS = 8
N_KV_HEADS = 2
ATTN_WIDTH = 512
KV_WIDTH = 128
ROT_DIM = 16
ROPE_THETA = 500000.0
IN_PROJ_WIDTH = 1792
N_MOD = 6
EPS = 1e-5
N_CHIPS = 4
N_DEV = 8
LANES = 128
W_IN_BLOCK = IN_PROJ_WIDTH // N_CHIPS

ADAM_LR = 0.001
ADAM_B1 = 0.9
ADAM_B2 = 0.999
ADAM_EPS = 1e-08
ADAM_WD = 0.01
ADAM_STEP = 10

VMEM_LIMIT_BYTES = 58 * 1024 * 1024
MESH = pl.DeviceIdType.MESH


def _params(*semantics):
    return pltpu.CompilerParams(dimension_semantics=semantics, vmem_limit_bytes=VMEM_LIMIT_BYTES)


def _dot(a, b):
    return jnp.dot(a.astype(MXU_DTYPE), b.astype(MXU_DTYPE), preferred_element_type=F32)


def _dot_nt(a, b):
    return lax.dot_general(a.astype(MXU_DTYPE), b.astype(MXU_DTYPE), (((1,), (1,)), ((), ())),
                           preferred_element_type=F32)


def _dot_tn(a, b):
    return lax.dot_general(a.astype(MXU_DTYPE), b.astype(MXU_DTYPE), (((0,), (0,)), ((), ())),
                           preferred_element_type=F32)


def _full(shape):
    return pl.BlockSpec(shape, lambda *_: (0,) * len(shape))


def _any():
    return pl.BlockSpec(memory_space=pl.ANY)


def _rowsum(v):
    return jnp.sum(v, axis=0, keepdims=True)


def _mean_last(v):
    return jnp.mean(v, axis=-1, keepdims=True)


class _Comm(NamedTuple):
    operands: tuple
    out_shapes: tuple
    n_sems: int
    make: Callable
    in_place: int = 0


def _hosted_call(body, comm, *, name, grid, in_specs, out_shape, out_specs, scratch_shapes=(), semantics,
                 n_prefetch=0):
    if comm is None:
        return pl.pallas_call(
            body, name=name, out_shape=out_shape, compiler_params=_params(*semantics),
            grid_spec=pltpu.PrefetchScalarGridSpec(
                num_scalar_prefetch=n_prefetch, grid=grid, in_specs=in_specs, out_specs=out_specs,
                scratch_shapes=list(scratch_shapes)))
    n_in, n_out, n_scr = len(in_specs), len(out_shape), len(scratch_shapes)
    k_in, k_out = len(comm.operands), len(comm.out_shapes)

    def hosted(*refs):
        prefetched, refs = refs[:n_prefetch], refs[n_prefetch:]
        ins, refs = refs[:n_in], refs[n_in:]
        c_ins, refs = refs[:k_in], refs[k_in:]
        outs, refs = refs[:n_out], refs[n_out:]
        c_outs, refs = refs[:k_out], refs[k_out:]
        scratch, (send_sems, recv_sems) = refs[:n_scr], refs[n_scr:]
        first, last = None, None
        for d, size in enumerate(grid):
            at_start, at_end = pl.program_id(d) == 0, pl.program_id(d) == size - 1
            first = at_start if first is None else first & at_start
            last = at_end if last is None else last & at_end

        @pl.when(first)
        def _():
            for cp in comm.make(c_ins, c_outs, send_sems, recv_sems)[0]:
                cp.start()

        body(*prefetched, *ins, *outs, *scratch)

        @pl.when(last)
        def _():
            for wait in comm.make(c_ins, c_outs, send_sems, recv_sems)[1]:
                wait()

    aliases = {n_prefetch + n_in + i: n_out + i for i in range(comm.in_place)}
    call = pl.pallas_call(
        hosted, name=name, out_shape=list(out_shape) + list(comm.out_shapes),
        compiler_params=_params(*semantics), input_output_aliases=aliases,
        grid_spec=pltpu.PrefetchScalarGridSpec(
            num_scalar_prefetch=n_prefetch, grid=grid, in_specs=list(in_specs) + [_any()] * k_in,
            out_specs=list(out_specs) + [_any()] * k_out,
            scratch_shapes=list(scratch_shapes) + [pltpu.SemaphoreType.DMA((comm.n_sems,)),
                                                    pltpu.SemaphoreType.DMA((comm.n_sems,))]))
    return lambda *args: call(*args, *comm.operands)


class _Shifted:
    def __init__(self, base, offset):
        self.base, self.offset = base, offset

    @property
    def at(self):
        return self

    def __getitem__(self, k):
        return self.base.at[self.offset + k]


def _merge_in_place(*jobs):
    assert all(j.in_place == len(j.operands) == len(j.out_shapes) for j in jobs)

    def make(ins, outs, send_sems, recv_sems):
        starts, waits, at, sem = [], [], 0, 0
        for j in jobs:
            n = len(j.operands)
            s, w = j.make(ins[at:at + n], outs[at:at + n], _Shifted(send_sems, sem), _Shifted(recv_sems, sem))
            starts, waits, at, sem = starts + s, waits + w, at + n, sem + j.n_sems
        return starts, waits

    operands = tuple(a for j in jobs for a in j.operands)
    return _Comm(operands, tuple(s for j in jobs for s in j.out_shapes), sum(j.n_sems for j in jobs), make,
                 in_place=len(operands))


def _mesh_place():
    x, y, c = lax.axis_index("x"), lax.axis_index("y"), lax.axis_index("c")
    return x, y, c, [(1 - x, y), (x, 1 - y), (1 - x, 1 - y)]


def _gather_job(bufs):
    per = 4

    def make(ins, outs, send_sems, recv_sems):
        del ins
        x, y, c, chips = _mesh_place()
        starts, waits = [], []
        for a, out in enumerate(outs):
            mine = src = out.at[4 * x + 2 * y + c]
            to = [(x, y, 1 - c)] + [(px, py, c) for px, py in chips]
            sends = [pltpu.make_async_remote_copy(
                src_ref=src, dst_ref=mine, send_sem=send_sems.at[per * a + k],
                recv_sem=recv_sems.at[per * a + k], device_id=dev, device_id_type=MESH)
                for k, dev in enumerate(to)]
            recvs = [pltpu.make_async_remote_copy(
                src_ref=src, dst_ref=out.at[4 * px + 2 * py + pc], send_sem=send_sems.at[per * a + k],
                recv_sem=recv_sems.at[per * a + k], device_id=(px, py, pc), device_id_type=MESH)
                for k, (px, py, pc) in enumerate(to)]
            starts += sends
            waits += [s.wait_send for s in sends] + [r.wait_recv for r in recvs]
        return starts, waits

    shapes = tuple(jax.ShapeDtypeStruct(b.shape, b.dtype) for b in bufs)
    return _Comm(tuple(bufs), shapes, per * len(bufs), make, in_place=len(bufs))


def _split_copies(srcs, lands, send_sems, recv_sems):
    x, y, c, chips = _mesh_place()
    pairs = []
    for a, (src, land) in enumerate(zip(srcs, lands)):
        for k, (px, py) in enumerate(chips):
            sems = dict(send_sem=send_sems.at[3 * a + k], recv_sem=recv_sems.at[3 * a + k],
                        device_id=(px, py, c), device_id_type=MESH)
            pairs.append((pltpu.make_async_remote_copy(src_ref=src.at[c], dst_ref=land.at[4 * x + 2 * y + c], **sems),
                          pltpu.make_async_remote_copy(src_ref=src.at[c], dst_ref=land.at[4 * px + 2 * py + c], **sems)))
    return pairs


def _gather_start(halves, name):
    n = len(halves)
    hbm, sem = pl.BlockSpec(memory_space=pltpu.HBM), pl.BlockSpec(memory_space=pltpu.SEMAPHORE)

    def body(*refs):
        srcs, lands, send_sems, recv_sems = refs[:n], refs[n:2 * n], refs[2 * n], refs[2 * n + 1]
        for send, _ in _split_copies(srcs, lands, send_sems, recv_sems):
            send.start()

    lands = [lax.empty((N_DEV,) + h.shape[1:], h.dtype) for h in halves]
    operands = [pltpu.with_memory_space_constraint(a, pltpu.HBM) for a in list(halves) + lands]
    return pl.pallas_call(
        body, name=name,
        out_shape=[pltpu.SemaphoreType.DMA((3 * n,)), pltpu.SemaphoreType.DMA((3 * n,))]
        + [pltpu.HBM(a.shape, a.dtype) for a in operands],
        in_specs=[hbm] * (2 * n), out_specs=[sem, sem] + [hbm] * (2 * n),
        input_output_aliases={i: 2 + i for i in range(2 * n)},
        compiler_params=pltpu.CompilerParams(has_side_effects=pltpu.SideEffectType.DATAFLOW_SIDE_EFFECTING),
    )(*operands)


def _gather_wait(started, after, name):
    send_sems, recv_sems, *bufs = started
    n = len(bufs) // 2
    hbm, sem = pl.BlockSpec(memory_space=pltpu.HBM), pl.BlockSpec(memory_space=pltpu.SEMAPHORE)

    def body(*refs):
        srcs, lands, send_sems, recv_sems = refs[:n], refs[n:2 * n], refs[2 * n], refs[2 * n + 1]
        for send, arrival in _split_copies(srcs, lands, send_sems, recv_sems):
            send.wait_send()
            arrival.wait_recv()

    out = pl.pallas_call(
        body, name=name, out_shape=[pltpu.HBM(b.shape, b.dtype) for b in bufs],
        in_specs=[hbm] * (2 * n) + [sem, sem, _any()], out_specs=[hbm] * (2 * n),
        input_output_aliases={i: i for i in range(2 * n)},
        compiler_params=pltpu.CompilerParams(has_side_effects=pltpu.SideEffectType.DATAFLOW_SIDE_EFFECTING),
    )(*bufs, send_sems, recv_sems, after)
    return out[n:]


def _slots(x, y, c):
    return 4 * x + 2 * y + c, 4 * (1 - x) + 2 * y + c, 4 * x + 2 * (1 - y) + c, 4 * (1 - x) + 2 * (1 - y) + c


def _gather2d_first(halves):
    per = 2

    def make(ins, outs, send_sems, recv_sems):
        x, y, c, _ = _mesh_place()
        me, xn, yn, _ = _slots(x, y, c)
        starts, waits = [], []
        for a, (src, out) in enumerate(zip(ins, outs)):
            blk = src.at[c]
            rows = blk.shape[0] // 2
            upper, lower = pl.ds(0, rows), pl.ds(rows, rows)

            def copy(k, src_ref, dst_ref, dev, a=a):
                return pltpu.make_async_remote_copy(
                    src_ref=src_ref, dst_ref=dst_ref, send_sem=send_sems.at[per * a + k],
                    recv_sem=recv_sems.at[per * a + k], device_id=dev, device_id_type=MESH)

            sends = [copy(0, blk.at[upper], out.at[me, upper], (1 - x, y, c)),
                     copy(1, blk.at[lower], out.at[me, lower], (x, 1 - y, c))]
            recvs = [copy(0, blk.at[upper], out.at[xn, upper], (1 - x, y, c)),
                     copy(1, blk.at[lower], out.at[yn, lower], (x, 1 - y, c))]
            starts += sends
            waits += [s.wait_send for s in sends] + [r.wait_recv for r in recvs]
        return starts, waits

    shapes = tuple(jax.ShapeDtypeStruct((N_DEV,) + h.shape[1:], h.dtype) for h in halves)
    return _Comm(tuple(halves), shapes, per * len(halves), make)


def _gather2d_second(bufs, halves):
    per = 4
    n_arr = len(bufs)

    def make(ins, outs, send_sems, recv_sems):
        x, y, c, _ = _mesh_place()
        me, xn, yn, dg = _slots(x, y, c)
        starts, waits = [], []
        for a, buf in enumerate(outs):
            own = ins[n_arr + a].at[c]
            rows = buf.shape[1] // 2
            upper, lower = pl.ds(0, rows), pl.ds(rows, rows)
            plan = [(own.at[upper], me, upper, (x, 1 - y, c), yn), (buf.at[xn, upper], xn, upper, (x, 1 - y, c), dg),
                    (own.at[lower], me, lower, (1 - x, y, c), xn), (buf.at[yn, lower], yn, lower, (1 - x, y, c), dg)]
            for k, (src, slot, part, dev, landing) in enumerate(plan):
                sems = dict(send_sem=send_sems.at[per * a + k], recv_sem=recv_sems.at[per * a + k],
                            device_id=dev, device_id_type=MESH)
                send = pltpu.make_async_remote_copy(src_ref=src, dst_ref=buf.at[slot, part], **sems)
                arrival = pltpu.make_async_remote_copy(src_ref=src, dst_ref=buf.at[landing, part], **sems)
                starts.append(send)
                waits += [send.wait_send, arrival.wait_recv]
        return starts, waits

    shapes = tuple(jax.ShapeDtypeStruct(b.shape, b.dtype) for b in bufs)
    return _Comm(tuple(bufs) + tuple(halves), shapes, per * n_arr, make, in_place=n_arr)


def _gather_forward(bufs, name):
    n_arr = len(bufs)

    def body(*refs):
        outs = refs[n_arr:2 * n_arr]
        send_sems, recv_sems = refs[2 * n_arr:]
        x, y, c, chips = _mesh_place()
        sends, recvs = [], []
        for a, buf in enumerate(outs):
            for j, (px, py) in enumerate(chips):
                mine, theirs = buf.at[4 * px + 2 * py + c], buf.at[4 * px + 2 * py + 1 - c]
                sems = dict(send_sem=send_sems.at[3 * a + j], recv_sem=recv_sems.at[3 * a + j],
                            device_id=(x, y, 1 - c), device_id_type=MESH)
                sends.append(pltpu.make_async_remote_copy(src_ref=mine, dst_ref=mine, **sems))
                recvs.append(pltpu.make_async_remote_copy(src_ref=mine, dst_ref=theirs, **sems))
        for cp in sends:
            cp.start()
        for s, r in zip(sends, recvs):
            s.wait_send()
            r.wait_recv()

    return pl.pallas_call(
        body, name=name, out_shape=[jax.ShapeDtypeStruct(b.shape, b.dtype) for b in bufs],
        in_specs=[_any()] * n_arr, out_specs=[_any()] * n_arr,
        input_output_aliases={a: a for a in range(n_arr)},
        scratch_shapes=[pltpu.SemaphoreType.DMA((3 * n_arr,)), pltpu.SemaphoreType.DMA((3 * n_arr,))],
    )(*bufs)


def _scatter_job(chip_sums):
    def make(ins, outs, send_sems, recv_sems):
        x, y, c, chips = _mesh_place()
        copies = [pltpu.make_async_remote_copy(
            src_ref=src.at[2 * px + py], dst_ref=out.at[j], send_sem=send_sems.at[3 * a + j],
            recv_sem=recv_sems.at[3 * a + j], device_id=(px, py, c), device_id_type=MESH)
            for a, (src, out) in enumerate(zip(ins, outs)) for j, (px, py) in enumerate(chips)]
        return copies, [cp.wait for cp in copies]

    return _Comm(tuple(chip_sums), tuple(jax.ShapeDtypeStruct((3,) + s.shape[1:], s.dtype) for s in chip_sums),
                 3 * len(chip_sums), make)


def _all_gather8(blocks, name, split=False, forward=(), riders=(), skip_own=()):
    n_arr, n_fwd = len(blocks), len(forward)
    splits = list(split) if isinstance(split, (list, tuple)) else [split] * n_arr
    own_slots = [a not in skip_own for a in range(n_arr)]
    rider_in = sum(len(r.operands) for r in riders)
    rider_out = sum(len(r.out_shapes) for r in riders)

    def body(*refs):
        x_refs, refs = refs[:n_arr], refs[n_arr + n_fwd:]
        r_ins, refs = refs[:rider_in], refs[rider_in:]
        out_refs, refs = refs[:n_arr], refs[n_arr:]
        fwd_refs, refs = refs[:n_fwd], refs[n_fwd:]
        r_outs, refs = refs[:rider_out], refs[rider_out:]
        (send_sems, recv_sems, local_sems), rider_sems = refs[:3], refs[3:]
        x, y, c, chips = _mesh_place()
        me, sibling = (x, y, c), (x, y, 1 - c)
        rider_waits, i0, o0 = [], 0, 0
        for n, job in enumerate(riders):
            k_in, k_out = len(job.operands), len(job.out_shapes)
            starts, waits = job.make(r_ins[i0:i0 + k_in], r_outs[o0:o0 + k_out],
                                     rider_sems[2 * n], rider_sems[2 * n + 1])
            for cp in starts:
                cp.start()
            rider_waits += waits
            i0, o0 = i0 + k_in, o0 + k_out
        passing = []
        for f, buf in enumerate(fwd_refs):
            for j, (px, py) in enumerate(chips):
                mine, theirs = buf.at[4 * px + 2 * py + c], buf.at[4 * px + 2 * py + 1 - c]
                sems = dict(send_sem=send_sems.at[7 * n_arr + 3 * f + j], recv_sem=recv_sems.at[7 * n_arr + 3 * f + j],
                            device_id=sibling, device_id_type=MESH)
                passing.append((pltpu.make_async_remote_copy(src_ref=mine, dst_ref=mine, **sems),
                                pltpu.make_async_remote_copy(src_ref=mine, dst_ref=theirs, **sems)))
        for send, _ in passing:
            send.start()
        arrays = []
        for a, (x_ref, out_ref) in enumerate(zip(x_refs, out_refs)):
            src_mine = x_ref.at[c] if splits[a] else x_ref

            def copy(k, blk, to, src=None, a=a, out_ref=out_ref):
                dst = out_ref.at[4 * blk[0] + 2 * blk[1] + blk[2]]
                return pltpu.make_async_remote_copy(
                    src_ref=dst if src is None else src, dst_ref=dst,
                    send_sem=send_sems.at[7 * a + k], recv_sem=recv_sems.at[7 * a + k],
                    device_id=to, device_id_type=MESH)

            mine = pltpu.make_async_copy(src_mine, out_ref.at[4 * x + 2 * y + c], local_sems.at[a])
            first = [copy(0, me, sibling, src=src_mine)] if own_slots[a] else []
            first += [copy(1 + j, me, (*chip, c), src=src_mine) for j, chip in enumerate(chips)]
            for cp in first + ([mine] if own_slots[a] else []):
                cp.start()
            arrays.append((copy, mine, first, own_slots[a]))
        sent = []
        for copy, mine, first, own in arrays:
            passed = [copy(4 + j, (*chip, c), sibling) for j, chip in enumerate(chips)]
            for j, chip in enumerate(chips):
                copy(1 + j, (*chip, c), me).wait_recv()
                passed[j].start()
            sent += first + passed
        for copy, mine, first, own in arrays:
            if own:
                copy(0, sibling, me).wait_recv()
                mine.wait()
            for j, chip in enumerate(chips):
                copy(4 + j, (*chip, 1 - c), me).wait_recv()
        for cp in sent:
            cp.wait_send()
        for send, arrival in passing:
            send.wait_send()
            arrival.wait_recv()
        for wait in rider_waits:
            wait()

    n_sems = 7 * n_arr + 3 * n_fwd
    rider_operands = [a for r in riders for a in r.operands]
    rider_shapes = [s for r in riders for s in r.out_shapes]
    return pl.pallas_call(
        body, name=name,
        out_shape=[jax.ShapeDtypeStruct((N_DEV,) + tuple(b.shape[1:] if s else b.shape), b.dtype)
                   for b, s in zip(blocks, splits)]
        + [jax.ShapeDtypeStruct(f.shape, f.dtype) for f in forward] + rider_shapes,
        in_specs=[_any()] * (n_arr + n_fwd + rider_in), out_specs=[_any()] * (n_arr + n_fwd + rider_out),
        input_output_aliases={n_arr + f: n_arr + f for f in range(n_fwd)},
        scratch_shapes=[pltpu.SemaphoreType.DMA((n_sems,)), pltpu.SemaphoreType.DMA((n_sems,)),
                        pltpu.SemaphoreType.DMA((n_arr,))]
        + [pltpu.SemaphoreType.DMA((r.n_sems,)) for r in riders for _ in range(2)],
    )(*blocks, *forward, *rider_operands)


def _share_job(bufs):
    def make(ins, outs, send_sems, recv_sems):
        del ins
        x, y, c, _ = _mesh_place()
        sems = lambda a: dict(send_sem=send_sems.at[a], recv_sem=recv_sems.at[a],
                              device_id=(x, y, 1 - c), device_id_type=MESH)
        sends = [pltpu.make_async_remote_copy(src_ref=o.at[c], dst_ref=o.at[c], **sems(a)) for a, o in enumerate(outs)]
        arrivals = [pltpu.make_async_remote_copy(src_ref=o.at[c], dst_ref=o.at[1 - c], **sems(a))
                    for a, o in enumerate(outs)]
        return sends, [s.wait_send for s in sends] + [r.wait_recv for r in arrivals]

    shapes = tuple(jax.ShapeDtypeStruct(b.shape, b.dtype) for b in bufs)
    return _Comm(tuple(bufs), shapes, len(bufs), make, in_place=len(bufs))


def _sibling_share(bufs, name):
    n_arr = len(bufs)

    def body(*refs):
        out_refs = refs[n_arr:2 * n_arr]
        send_sems, recv_sems = refs[2 * n_arr:]
        x, y, c = lax.axis_index("x"), lax.axis_index("y"), lax.axis_index("c")
        copies = [pltpu.make_async_remote_copy(
            src_ref=out_refs[a].at[c], dst_ref=out_refs[a].at[c],
            send_sem=send_sems.at[a], recv_sem=recv_sems.at[a],
            device_id=(x, y, 1 - c), device_id_type=MESH) for a in range(n_arr)]
        for cp in copies:
            cp.start()
        for a in range(n_arr):
            pltpu.make_async_remote_copy(
                src_ref=out_refs[a].at[c], dst_ref=out_refs[a].at[1 - c],
                send_sem=send_sems.at[a], recv_sem=recv_sems.at[a],
                device_id=(x, y, 1 - c), device_id_type=MESH).wait()

    return pl.pallas_call(
        body, name=name,
        out_shape=[jax.ShapeDtypeStruct(b.shape, b.dtype) for b in bufs],
        in_specs=[_any()] * n_arr, out_specs=[_any()] * n_arr,
        input_output_aliases={a: a for a in range(n_arr)},
        scratch_shapes=[pltpu.SemaphoreType.DMA((n_arr,)), pltpu.SemaphoreType.DMA((n_arr,))],
    )(*bufs)


def _gelu_tanh(z):
    k = math.sqrt(2.0 / math.pi)
    t = jnp.tanh(k * (z + 0.044715 * (z * z * z)))
    return 0.5 * z * (1.0 + t), t


def _gelu_tanh_grad(z, t):
    k = math.sqrt(2.0 / math.pi)
    return 0.5 * (1.0 + t) + 0.5 * z * (1.0 - t * t) * (k * (1.0 + 3.0 * 0.044715 * (z * z)))


def _rope_angle_kernel(pos_row, invf_col):
    seq = pos_row.shape[1]

    def body(p_ref, f_ref, cos_ref, sin_ref):
        ang = p_ref[...].astype(F32) * f_ref[...]
        cos_ref[...] = jnp.cos(ang)
        sin_ref[...] = jnp.sin(ang)

    return pl.pallas_call(
        body, name="rope_angles", grid=(1,), out_shape=[jax.ShapeDtypeStruct((ROT_DIM // 2, seq), F32)] * 2,
        in_specs=[_full((1, seq)), _full((ROT_DIM // 2, 1))], out_specs=[_full((ROT_DIM // 2, seq))] * 2,
        compiler_params=_params("arbitrary"),
    )(pos_row, invf_col)


def _rope_lane_tables(cos, sin):
    cos_t, sin_t = cos.T, sin.T
    seq, half = cos_t.shape
    ones = jnp.ones((seq, HEAD_DIM - ROT_DIM), F32)
    c64 = jnp.concatenate([cos_t, cos_t, ones], axis=1)
    s1 = jnp.concatenate([sin_t, jnp.zeros((seq, HEAD_DIM - half), F32)], axis=1)
    s2 = jnp.concatenate([jnp.zeros((seq, half), F32), sin_t, jnp.zeros((seq, HEAD_DIM - ROT_DIM), F32)], axis=1)
    return jnp.concatenate([jnp.tile(t, (1, LANES // HEAD_DIM)) for t in (c64, s1, s2)], axis=1)


def _rope_apply(t, tab, sign):
    reps = t.shape[1] // LANES
    c_tab, s1, s2 = (jnp.tile(tab[:, LANES * k:LANES * (k + 1)], (1, reps)) if reps > 1
                     else tab[:, LANES * k:LANES * (k + 1)] for k in range(3))
    half = ROT_DIM // 2
    up = pltpu.roll(t, t.shape[1] - half, 1)
    down = pltpu.roll(t, half, 1)
    return t * c_tab + sign * (down * s2 - up * s1)


def _lane_masks(shape):
    lane = lax.broadcasted_iota(jnp.int32, shape, 1)
    return lane < HEAD_DIM, lane >= HEAD_DIM


HEADS_PER_GROUP = N_Q_HEADS // N_KV_HEADS
ATTN_SCALE = 1.0 / math.sqrt(HEAD_DIM)


def _attn_bias_t(first_block):
    kj = lax.broadcasted_iota(jnp.int32, (2 * CHUNK, CHUNK), 0)
    qi = lax.broadcasted_iota(jnp.int32, (2 * CHUNK, CHUNK), 1)
    ok = (kj > qi) & (kj <= qi + CHUNK)
    if first_block is not None:
        ok = ok & (jnp.logical_not(first_block) | (kj >= CHUNK))
    return jnp.tile(jnp.where(ok, 0.0, -jnp.inf), (1, HEADS_PER_GROUP))


def _group_rows(x, g, lo, hi):
    rows = []
    for r in range(HEADS_PER_GROUP):
        h = HEADS_PER_GROUP * g + r
        pair = x[:, LANES * (h // 2):LANES * (h // 2 + 1)]
        rows.append(jnp.where(hi if h % 2 else lo, pair, 0.0))
    return jnp.concatenate(rows, axis=0)


def _pairs_from_rows(rows, lo):
    return [jnp.where(lo, rows[2 * CHUNK * k:2 * CHUNK * k + CHUNK], rows[2 * CHUNK * k + CHUNK:2 * CHUNK * (k + 1)])
            for k in range(HEADS_PER_GROUP // 2)]


def _group_dup(a, b, g, lo2):
    return jnp.where(lo2, a, b) if g == 0 else jnp.where(lo2, b, a)


def _sink_row(sink_ref, g):
    return jnp.concatenate([sink_ref[HEADS_PER_GROUP * g + r:HEADS_PER_GROUP * g + r + 1, :]
                            for r in range(HEADS_PER_GROUP)], axis=1)


def _attn_probs_t(k_dup, q_rows, bias_t, sink_row):
    s_t = _dot_nt(k_dup, q_rows) * ATTN_SCALE + bias_t
    m = jnp.maximum(jnp.max(s_t, axis=0, keepdims=True), sink_row)
    p = jnp.exp(s_t - m)
    e_sink = jnp.exp(sink_row - m)
    inv = 1.0 / (jnp.sum(p, axis=0, keepdims=True) + e_sink)
    return p * inv, e_sink * inv


def _sgu_forward_pair(wm, vp, j):
    lo, hi = _lane_masks(vp.shape)
    lhs = jnp.concatenate([wm[2 * j], wm[2 * j + 1]], axis=1)
    rhs = jnp.concatenate([jnp.where(lo, vp, 0.0), jnp.where(hi, vp, 0.0)], axis=0)
    return _dot(lhs, rhs)


def _masked_spatial(w_ref):
    t = lax.broadcasted_iota(jnp.int32, (CHUNK, CHUNK), 0)
    s = lax.broadcasted_iota(jnp.int32, (CHUNK, CHUNK), 1)
    tril = s <= t
    return [jnp.where(tril, w_ref[g], 0.0) for g in range(GMLP_GROUPS)], tril, s >= t


def _mod_kernel(c_all, w_shard, b_shard, comm=None):
    n = w_shard.shape[1]
    tn = 512

    def body(c_ref, w_ref, b_ref, mod_ref, act_ref):
        cv = c_ref[...]
        act = cv * (1.0 / (1.0 + jnp.exp(-cv)))
        act_ref[...] = act
        mod_ref[...] = _dot(act, w_ref[...]) + b_ref[...]

    return _hosted_call(
        body, comm, name="ada_mod", grid=(n // tn,),
        out_shape=[jax.ShapeDtypeStruct((N_DEV, n), F32), jax.ShapeDtypeStruct((N_DEV, D_MODEL), F32)],
        in_specs=[_full((N_DEV, D_MODEL)), pl.BlockSpec((D_MODEL, tn), lambda i: (0, i)),
                  pl.BlockSpec((1, tn), lambda i: (0, i))],
        out_specs=[pl.BlockSpec((N_DEV, tn), lambda i: (0, i)), _full((N_DEV, D_MODEL))],
        semantics=("arbitrary",),
    )(c_all, w_shard, b_shard)


def _load_chip_blocks(chip_ref, gathered, local, dsts, sems, first_sem=0):
    for k, dst in enumerate(dsts):
        @pl.when(chip_ref[0] == k)
        def _():
            pltpu.make_async_copy(local, dst, sems.at[first_sem + k]).start()

        @pl.when(chip_ref[0] != k)
        def _():
            pltpu.make_async_copy(gathered.at[k], dst, sems.at[first_sem + k]).start()
    return [pltpu.make_async_copy(local, dst, sems.at[first_sem + k]).wait for k, dst in enumerate(dsts)]


def _w_in_rows(w_scr):
    return [w_scr.at[pl.ds(W_IN_BLOCK * k, W_IN_BLOCK)] for k in range(N_CHIPS)]


def _in_proj_kernel(x, vecs, chip_idx, w_in_gathered, w_in_local, comm=None):
    seq = x.shape[0]
    tm = 512

    def body(chip_ref, x_ref, v_ref, wg_ref, wl_ref, proj_ref, h_ref, w_ref, sems):
        @pl.when(pl.program_id(0) == 0)
        def _():
            for wait in _load_chip_blocks(chip_ref, wg_ref, wl_ref, _w_in_rows(w_ref), sems):
                wait()

        xv = x_ref[...]
        rstd = lax.rsqrt(_mean_last(xv * xv) + EPS)
        n1 = (xv * rstd) * v_ref[0:1, :]
        h = n1 * (1.0 + v_ref[2:3, :]) + v_ref[1:2, :]
        hb = h.astype(MXU_DTYPE)
        h_ref[...] = hb
        proj_ref[...] = _dot_nt(hb, w_ref[...])

    return _hosted_call(
        body, comm, name="in_proj", grid=(seq // tm,), n_prefetch=1,
        out_shape=[jax.ShapeDtypeStruct((seq, IN_PROJ_WIDTH), F32),
                   jax.ShapeDtypeStruct((seq, D_MODEL), MXU_DTYPE)],
        in_specs=[pl.BlockSpec((tm, D_MODEL), lambda i, chip: (i, 0)), _full((8, D_MODEL)), _any(), _any()],
        out_specs=[pl.BlockSpec((tm, IN_PROJ_WIDTH), lambda i, chip: (i, 0)),
                   pl.BlockSpec((tm, D_MODEL), lambda i, chip: (i, 0))],
        scratch_shapes=[pltpu.VMEM((IN_PROJ_WIDTH, D_MODEL), MXU_DTYPE), pltpu.SemaphoreType.DMA((N_CHIPS,))],
        semantics=("arbitrary",),
    )(chip_idx, x, vecs, w_in_gathered, w_in_local)


MIXER_BLOCKS_PER_STEP = 2
KV_START = 2 * GMLP_WIDTH + ATTN_WIDTH


def _mixer_fwd_kernel(proj, rope_tab, w_spatial, bias_full, sink_rows, comm=None):
    seq = proj.shape[0]
    per = MIXER_BLOCKS_PER_STEP
    steps = seq // (CHUNK * per)
    kv_col = KV_START // (2 * KV_WIDTH)

    def body(proj_ref, prev_ref, tab_ref, ptab_ref, w_ref, bias_ref, sink_ref, cat_ref):
        i = pl.program_id(0)
        wm, _, _ = _masked_spatial(w_ref)
        lo, hi = _lane_masks((CHUNK, LANES))
        lo2, _ = _lane_masks((2 * CHUNK, LANES))
        o = 2 * GMLP_WIDTH
        for s in range(per):
            rows, before = slice(CHUNK * s, CHUNK * (s + 1)), slice(CHUNK * (s - 1), CHUNK * s)
            for j in range(GMLP_GROUPS // 2):
                cols = slice(LANES * j, LANES * (j + 1))
                vcols = slice(GMLP_WIDTH + LANES * j, GMLP_WIDTH + LANES * (j + 1))
                u, _ = _gelu_tanh(proj_ref[rows, cols])
                vp, _ = _gelu_tanh(proj_ref[rows, vcols])
                sv = _sgu_forward_pair(wm, vp, j) + bias_ref[:, cols]
                cat_ref[rows, cols] = (u * sv).astype(cat_ref.dtype)
            tab = tab_ref[rows, :]
            if s == 0:
                prev_kv, prev_tab, first = prev_ref[...], ptab_ref[...], i == 0
            else:
                prev_kv, prev_tab, first = proj_ref[before, KV_START:KV_START + 2 * KV_WIDTH], tab_ref[before, :], None
            q_r = _rope_apply(proj_ref[rows, o:o + ATTN_WIDTH], tab, 1.0)
            k_cur = _rope_apply(proj_ref[rows, KV_START:KV_START + KV_WIDTH], tab, 1.0)
            k_prev = _rope_apply(prev_kv[:, 0:KV_WIDTH], prev_tab, 1.0)
            k_a = jnp.concatenate([k_prev, k_cur], axis=0)
            v_a = jnp.concatenate([prev_kv[:, KV_WIDTH:2 * KV_WIDTH],
                                   proj_ref[rows, KV_START + KV_WIDTH:KV_START + 2 * KV_WIDTH]], axis=0)
            k_b = pltpu.roll(k_a, HEAD_DIM, 1)
            v_b = pltpu.roll(v_a, HEAD_DIM, 1)
            bias_t = _attn_bias_t(first)
            for g in range(N_KV_HEADS):
                p_t, _ = _attn_probs_t(_group_dup(k_a, k_b, g, lo2), _group_rows(q_r, g, lo, hi), bias_t,
                                       _sink_row(sink_ref, g))
                o_t = _dot(_group_dup(v_a, v_b, g, lo2).T, p_t)
                for k, pair in enumerate(_pairs_from_rows(o_t.T, lo)):
                    c0 = GMLP_WIDTH + LANES * (2 * g + k)
                    cat_ref[rows, c0:c0 + LANES] = pair.astype(cat_ref.dtype)

    return _hosted_call(
        body, comm, name="mixer_fwd", grid=(steps,),
        out_shape=[jax.ShapeDtypeStruct((seq, D_MODEL), MXU_DTYPE)],
        in_specs=[pl.BlockSpec((CHUNK * per, IN_PROJ_WIDTH), lambda i: (i, 0)),
                  pl.BlockSpec((CHUNK, 2 * KV_WIDTH), lambda i: (jnp.maximum(per * i - 1, 0), kv_col)),
                  pl.BlockSpec((CHUNK * per, 3 * LANES), lambda i: (i, 0)),
                  pl.BlockSpec((CHUNK, 3 * LANES), lambda i: (jnp.maximum(per * i - 1, 0), 0)),
                  _full((GMLP_GROUPS, CHUNK, CHUNK)), _full((CHUNK, GMLP_WIDTH)),
                  _full((N_Q_HEADS, LANES))],
        out_specs=[pl.BlockSpec((CHUNK * per, D_MODEL), lambda i: (i, 0))],
        semantics=("arbitrary",),
    )(proj, proj, rope_tab, rope_tab, w_spatial, bias_full, sink_rows)


def _trunk_kernel(x, target, cat, vecs, chip_idx, gathered, local):
    seq = x.shape[0]
    tm = 256
    nj = D_FF // D_MODEL
    out_rows = D_MODEL // N_CHIPS

    def body(chip_ref, x_ref, t_ref, cat_ref, v_ref, g_out, g_w1, g_w2, l_out, l_w1, l_w2,
             dx1_ref, dcat_ref, dmix_ref, h2_ref, r_ref, da_ref, dff_ref, sums_ref,
             wout, w1, w2, a_scr, sem):
        i = pl.program_id(0)

        @pl.when(i == 0)
        def _():
            waits = _load_chip_blocks(chip_ref, g_out, l_out,
                                      [wout.at[pl.ds(out_rows * k, out_rows)] for k in range(N_CHIPS)], sem)
            waits += _load_chip_blocks(chip_ref, g_w1, l_w1, [w1.at[k] for k in range(N_CHIPS)], sem, N_CHIPS)
            waits += _load_chip_blocks(chip_ref, g_w2, l_w2, [w2.at[k] for k in range(N_CHIPS)], sem, 2 * N_CHIPS)
            for wait in waits:
                wait()
            sums_ref[...] = jnp.zeros_like(sums_ref)

        gate1, shift2, scale2 = v_ref[0:1, :], v_ref[1:2, :], v_ref[2:3, :]
        gate2, g_ffn, g_final = v_ref[3:4, :], v_ref[4:5, :], v_ref[5:6, :]

        mix = _dot(cat_ref[...], wout[...])
        x1 = x_ref[...] + gate1 * mix
        rstd2 = lax.rsqrt(_mean_last(x1 * x1) + EPS)
        xh2 = x1 * rstd2
        n2 = xh2 * g_ffn
        h2b = (n2 * (1.0 + scale2) + shift2).astype(MXU_DTYPE)
        h2_ref[...] = h2b
        ff = jnp.zeros((tm, D_MODEL), F32)
        for j in range(nj):
            a = _dot(h2b, w1[j])
            a_scr[j] = a
            relu = jnp.maximum(a, 0.0)
            rb = (relu * relu).astype(MXU_DTYPE)
            r_ref[:, D_MODEL * j:D_MODEL * (j + 1)] = rb
            ff = ff + _dot(rb, w2[j])
        x2 = x1 + gate2 * ff
        rstd3 = lax.rsqrt(_mean_last(x2 * x2) + EPS)
        xh3 = x2 * rstd3
        err = xh3 * g_final - t_ref[...]
        loss = 0.5 * _rowsum(_mean_last(err * err))
        dy = err * (1.0 / D_MODEL)
        dxh3 = dy * g_final
        dx2 = rstd3 * (dxh3 - xh3 * _mean_last(dxh3 * xh3))
        dffb = (dx2 * gate2).astype(MXU_DTYPE)
        dff_ref[...] = dffb
        dh2 = jnp.zeros((tm, D_MODEL), F32)
        for j in range(nj):
            dr = _dot_nt(dffb, w2[j])
            dab = (dr * (2.0 * jnp.maximum(a_scr[j], 0.0))).astype(MXU_DTYPE)
            da_ref[:, D_MODEL * j:D_MODEL * (j + 1)] = dab
            dh2 = dh2 + _dot_nt(dab, w1[j])
        dn2 = dh2 * (1.0 + scale2)
        dxh2 = dn2 * g_ffn
        dx1 = dx2 + rstd2 * (dxh2 - xh2 * _mean_last(dxh2 * xh2))
        dx1_ref[...] = dx1
        dmixb = (dx1 * gate1).astype(MXU_DTYPE)
        dmix_ref[...] = dmixb
        dcat_ref[...] = _dot_nt(dmixb, wout[...])

        sums_ref[0:1, :] += _rowsum(dh2)
        sums_ref[1:2, :] += _rowsum(dh2 * n2)
        sums_ref[2:3, :] += _rowsum(dx2 * ff)
        sums_ref[3:4, :] += _rowsum(dn2 * xh2)
        sums_ref[4:5, :] += _rowsum(dy * xh3)
        sums_ref[5:6, :] += _rowsum(dx1 * mix)
        sums_ref[6:7, :] += jnp.broadcast_to(loss, (1, D_MODEL))

    tok = lambda w: pl.BlockSpec((tm, w), lambda i, chip: (i, 0))
    return _hosted_call(
        body, None, name="trunk", grid=(seq // tm,), n_prefetch=1,
        out_shape=[jax.ShapeDtypeStruct((seq, D_MODEL), F32), jax.ShapeDtypeStruct((seq, D_MODEL), F32),
                   jax.ShapeDtypeStruct((seq, D_MODEL), MXU_DTYPE), jax.ShapeDtypeStruct((seq, D_MODEL), MXU_DTYPE),
                   jax.ShapeDtypeStruct((seq, D_FF), MXU_DTYPE), jax.ShapeDtypeStruct((seq, D_FF), MXU_DTYPE),
                   jax.ShapeDtypeStruct((seq, D_MODEL), MXU_DTYPE), jax.ShapeDtypeStruct((8, D_MODEL), F32)],
        in_specs=[tok(D_MODEL), tok(D_MODEL), tok(D_MODEL), _full((8, D_MODEL))] + [_any()] * 6,
        out_specs=[tok(D_MODEL), tok(D_MODEL), tok(D_MODEL), tok(D_MODEL), tok(D_FF), tok(D_FF), tok(D_MODEL),
                   _full((8, D_MODEL))],
        scratch_shapes=[pltpu.VMEM((D_MODEL, D_MODEL), MXU_DTYPE), pltpu.VMEM((nj, D_MODEL, D_MODEL), MXU_DTYPE),
                        pltpu.VMEM((nj, D_MODEL, D_MODEL), MXU_DTYPE), pltpu.VMEM((nj, tm, D_MODEL), F32),
                        pltpu.SemaphoreType.DMA((3 * N_CHIPS,))],
        semantics=("arbitrary",),
    )(chip_idx, x, target, cat, vecs, *gathered, *local)


def _mixer_bwd_kernel(proj, rope_tab, dcat, w_spatial, w_spatial_t, bias_full, sink_rows, dev_idx, comm=None):
    seq = proj.shape[0]
    per = MIXER_BLOCKS_PER_STEP
    steps = seq // (CHUNK * per)
    kv_col = KV_START // (2 * KV_WIDTH)

    def body(dev_ref, proj_ref, prev_ref, tab_ref, ptab_ref, dcat_ref, w_ref, wt_ref, bias_ref, sink_ref,
             dproj_ref, dw_ref, db_ref, dsink_ref, carry):
        del dev_ref
        step = pl.program_id(0)

        @pl.when(step == 0)
        def _():
            carry[...] = jnp.zeros_like(carry)
            dw_ref[...] = jnp.zeros_like(dw_ref)
            db_ref[...] = jnp.zeros_like(db_ref)
            dsink_ref[...] = jnp.zeros_like(dsink_ref)

        for s in reversed(range(per)):
            rows = pl.ds(CHUNK * s, CHUNK)
            if s == 0:
                before, before_tab, first = prev_ref, ptab_ref, step == steps - 1
            else:
                before = proj_ref.at[pl.ds(CHUNK * (s - 1), CHUNK), pl.ds(KV_START, 2 * KV_WIDTH)]
                before_tab, first = tab_ref.at[pl.ds(CHUNK * (s - 1), CHUNK)], None
            one_block(proj_ref.at[rows], before, tab_ref.at[rows], before_tab, dcat_ref.at[rows], w_ref, wt_ref,
                      bias_ref, sink_ref, dproj_ref.at[rows], dw_ref, db_ref, dsink_ref, carry, first)

    def one_block(proj_ref, prev_ref, tab_ref, ptab_ref, dcat_ref, w_ref, wt_ref, bias_ref, sink_ref,
                  dproj_ref, dw_ref, db_ref, dsink_ref, carry, first):
        wm, tril, triu = _masked_spatial(w_ref)
        lo, hi = _lane_masks((CHUNK, LANES))
        lane = lax.broadcasted_iota(jnp.int32, (CHUNK, LANES), 1)
        db = jnp.zeros((CHUNK, LANES), F32)
        for j in range(GMLP_GROUPS // 2):
            cols = slice(LANES * j, LANES * (j + 1))
            vcols = slice(GMLP_WIDTH + LANES * j, GMLP_WIDTH + LANES * (j + 1))
            zu, zv = proj_ref[:, cols], proj_ref[:, vcols]
            u, tu = _gelu_tanh(zu)
            vp, tv = _gelu_tanh(zv)
            sv = _sgu_forward_pair(wm, vp, j) + bias_ref[:, cols]
            dout = dcat_ref[:, cols]
            du = dout * sv
            dsv = dout * u
            dsv_lo, dsv_hi = jnp.where(lo, dsv, 0.0), jnp.where(hi, dsv, 0.0)
            lhs_t = jnp.concatenate([jnp.where(triu, wt_ref[2 * j], 0.0),
                                     jnp.where(triu, wt_ref[2 * j + 1], 0.0)], axis=1)
            dv = _dot(lhs_t, jnp.concatenate([dsv_lo, dsv_hi], axis=0))
            dw_ref[2 * j] += jnp.where(tril, _dot_nt(dsv_lo, vp), 0.0)
            dw_ref[2 * j + 1] += jnp.where(tril, _dot_nt(dsv_hi, vp), 0.0)
            db = db + (jnp.where(lane == 2 * j, jnp.sum(dsv_lo, axis=1, keepdims=True), 0.0)
                       + jnp.where(lane == 2 * j + 1, jnp.sum(dsv_hi, axis=1, keepdims=True), 0.0))
            dproj_ref[:, cols] = (du * _gelu_tanh_grad(zu, tu)).astype(dproj_ref.dtype)
            dproj_ref[:, vcols] = (dv * _gelu_tanh_grad(zv, tv)).astype(dproj_ref.dtype)
        db_ref[...] += db
        o = 2 * GMLP_WIDTH
        tab = tab_ref[...]
        q_r = _rope_apply(proj_ref[:, o:o + ATTN_WIDTH], tab, 1.0)
        k_cur = _rope_apply(proj_ref[:, o + ATTN_WIDTH:o + ATTN_WIDTH + KV_WIDTH], tab, 1.0)
        k_prev = _rope_apply(prev_ref[:, 0:KV_WIDTH], ptab_ref[...], 1.0)
        k_a = jnp.concatenate([k_prev, k_cur], axis=0)
        v_a = jnp.concatenate([prev_ref[:, KV_WIDTH:2 * KV_WIDTH],
                               proj_ref[:, o + ATTN_WIDTH + KV_WIDTH:o + ATTN_WIDTH + 2 * KV_WIDTH]], axis=0)
        k_b = pltpu.roll(k_a, HEAD_DIM, 1)
        v_b = pltpu.roll(v_a, HEAD_DIM, 1)
        bias_t = _attn_bias_t(first)
        lo2, _ = _lane_masks((2 * CHUNK, LANES))
        dout_b = dcat_ref[:, GMLP_WIDTH:GMLP_WIDTH + ATTN_WIDTH]
        dk_tot, dv_tot, dq_pairs = [], [], []
        for g in range(N_KV_HEADS):
            k_dup, v_dup = _group_dup(k_a, k_b, g, lo2), _group_dup(v_a, v_b, g, lo2)
            q_rows = _group_rows(q_r, g, lo, hi)
            do_rows = _group_rows(dout_b, g, lo, hi)
            p_t, p_sink = _attn_probs_t(k_dup, q_rows, bias_t, _sink_row(sink_ref, g))
            dp_t = _dot_nt(v_dup, do_rows)
            delta = jnp.sum(p_t * dp_t, axis=0, keepdims=True)
            ds_t = p_t * (dp_t - delta) * ATTN_SCALE
            dsink = -p_sink * delta
            for r in range(HEADS_PER_GROUP):
                h = HEADS_PER_GROUP * g + r
                dsink_ref[h:h + 1, :] += jnp.broadcast_to(
                    jnp.sum(dsink[:, LANES * r:LANES * (r + 1)], axis=1, keepdims=True), (1, LANES))
            dk_full = _dot(ds_t, q_rows)
            dv_full = _dot(p_t, do_rows)
            dk_tot.append(dk_full + pltpu.roll(dk_full, HEAD_DIM, 1))
            dv_tot.append(dv_full + pltpu.roll(dv_full, HEAD_DIM, 1))
            dq_t = _dot(k_dup.T, ds_t)
            dq_pairs += _pairs_from_rows(dq_t.T, lo)
        dk_all = jnp.where(lo2, dk_tot[0], dk_tot[1])
        dv_all = jnp.where(lo2, dv_tot[0], dv_tot[1])
        dk_cur = dk_all[CHUNK:, :] + carry[:, 0:KV_WIDTH]
        dv_cur = dv_all[CHUNK:, :] + carry[:, KV_WIDTH:2 * KV_WIDTH]
        carry[:, 0:KV_WIDTH] = dk_all[:CHUNK, :]
        carry[:, KV_WIDTH:2 * KV_WIDTH] = dv_all[:CHUNK, :]
        dq = _rope_apply(jnp.concatenate(dq_pairs, axis=1), tab, -1.0)
        dproj_ref[:, o:o + ATTN_WIDTH] = dq.astype(dproj_ref.dtype)
        dproj_ref[:, o + ATTN_WIDTH:o + ATTN_WIDTH + KV_WIDTH] = (
            _rope_apply(dk_cur, tab, -1.0).astype(dproj_ref.dtype))
        dproj_ref[:, o + ATTN_WIDTH + KV_WIDTH:o + ATTN_WIDTH + 2 * KV_WIDTH] = dv_cur.astype(dproj_ref.dtype)

    rev = lambda i: steps - 1 - i
    before = lambda i: jnp.maximum(per * rev(i) - 1, 0)
    slot = lambda shape: pl.BlockSpec((None,) + shape, lambda i, d: (d[0],) + (0,) * len(shape))
    return _hosted_call(
        body, comm, name="mixer_bwd", grid=(steps,), n_prefetch=1,
        out_shape=[jax.ShapeDtypeStruct((seq, IN_PROJ_WIDTH), MXU_DTYPE),
                   jax.ShapeDtypeStruct((N_DEV, GMLP_GROUPS, CHUNK, CHUNK), F32),
                   jax.ShapeDtypeStruct((N_DEV, CHUNK, LANES), F32),
                   jax.ShapeDtypeStruct((N_DEV, N_Q_HEADS, LANES), F32)],
        in_specs=[pl.BlockSpec((CHUNK * per, IN_PROJ_WIDTH), lambda i, d: (rev(i), 0)),
                  pl.BlockSpec((CHUNK, 2 * KV_WIDTH), lambda i, d: (before(i), kv_col)),
                  pl.BlockSpec((CHUNK * per, 3 * LANES), lambda i, d: (rev(i), 0)),
                  pl.BlockSpec((CHUNK, 3 * LANES), lambda i, d: (before(i), 0)),
                  pl.BlockSpec((CHUNK * per, D_MODEL), lambda i, d: (rev(i), 0)),
                  _full((GMLP_GROUPS, CHUNK, CHUNK)), _full((GMLP_GROUPS, CHUNK, CHUNK)),
                  _full((CHUNK, GMLP_WIDTH)), _full((N_Q_HEADS, LANES))],
        out_specs=[pl.BlockSpec((CHUNK * per, IN_PROJ_WIDTH), lambda i, d: (rev(i), 0)),
                   slot((GMLP_GROUPS, CHUNK, CHUNK)), slot((CHUNK, LANES)), slot((N_Q_HEADS, LANES))],
        scratch_shapes=[pltpu.VMEM((CHUNK, 2 * KV_WIDTH), F32)],
        semantics=("arbitrary",),
    )(dev_idx, proj, proj, rope_tab, rope_tab, dcat, w_spatial, w_spatial_t, bias_full, sink_rows)


def _in_proj_bwd_kernel(x, dx1, dproj, vecs, chip_idx, w_in_gathered, w_in_local, comm=None):
    seq = x.shape[0]
    tm = 512

    def body(chip_ref, x_ref, dx1_ref, dp_ref, v_ref, wg_ref, wl_ref, gx_ref, sums_ref, w_ref, sems):
        @pl.when(pl.program_id(0) == 0)
        def _():
            for wait in _load_chip_blocks(chip_ref, wg_ref, wl_ref, _w_in_rows(w_ref), sems):
                wait()
            sums_ref[...] = jnp.zeros_like(sums_ref)

        g_mix, scale1 = v_ref[0:1, :], v_ref[2:3, :]
        dh = _dot(dp_ref[...], w_ref[...])
        xv = x_ref[...]
        rstd = lax.rsqrt(_mean_last(xv * xv) + EPS)
        xh = xv * rstd
        dn1 = dh * (1.0 + scale1)
        dxh = dn1 * g_mix
        gx_ref[...] = dx1_ref[...] + rstd * (dxh - xh * _mean_last(dxh * xh))
        sums_ref[0:1, :] += _rowsum(dh)
        sums_ref[1:2, :] += _rowsum(dh * (xh * g_mix))
        sums_ref[2:3, :] += _rowsum(dn1 * xh)

    tok = lambda w: pl.BlockSpec((tm, w), lambda i, chip: (i, 0))
    return _hosted_call(
        body, comm, name="in_proj_bwd", grid=(seq // tm,), n_prefetch=1,
        out_shape=[jax.ShapeDtypeStruct((seq, D_MODEL), F32), jax.ShapeDtypeStruct((8, D_MODEL), F32)],
        in_specs=[tok(D_MODEL), tok(D_MODEL), tok(IN_PROJ_WIDTH), _full((8, D_MODEL)), _any(), _any()],
        out_specs=[tok(D_MODEL), _full((8, D_MODEL))],
        scratch_shapes=[pltpu.VMEM((IN_PROJ_WIDTH, D_MODEL), MXU_DTYPE), pltpu.SemaphoreType.DMA((N_CHIPS,))],
        semantics=("arbitrary",),
    )(chip_idx, x, dx1, dproj, vecs, w_in_gathered, w_in_local)


class _GradTiles(NamedTuple):
    tm: int
    tn: int
    n_tiles: int
    chips_per_tile: int
    a_index: Callable
    b_index: Callable


def _weight_grad_kernel(a, b, c_idx, name, tiles, comm=None):
    seq = a.shape[0]
    tk = min(seq, 4096)
    nk = seq // tk
    tm, tn, n_tiles, per = tiles.tm, tiles.tn, tiles.n_tiles, tiles.chips_per_tile
    rows = tm // per

    def half(phase, c):
        return phase * c[0] + (1 - phase) * (1 - c[0])

    def body(c_ref, a_ref, b_ref, o_ref, acc, stage, landed, send_sems, recv_sems):
        del c_ref
        phase, t, kk = pl.program_id(0), pl.program_id(1), pl.program_id(2)
        x, y, c, _ = _mesh_place()

        def copy(tile):
            return pltpu.make_async_remote_copy(
                src_ref=stage.at[tile], dst_ref=landed.at[tile], send_sem=send_sems.at[tile],
                recv_sem=recv_sems.at[tile], device_id=(x, y, 1 - c), device_id_type=MESH)

        @pl.when(kk == 0)
        def _():
            acc[...] = jnp.zeros_like(acc)

        acc[...] += _dot_tn(a_ref[...], b_ref[...])

        @pl.when((kk == nk - 1) & (phase == 0))
        def _():
            stage[t] = acc[...].astype(stage.dtype)
            copy(t).start()

        @pl.when((kk == nk - 1) & (phase == 1))
        def _():
            copy(t).wait_recv()
            total = acc[...] + landed[t].astype(F32)
            for q in range(per):
                o_ref[q] = total[rows * q:rows * (q + 1)].astype(o_ref.dtype)

        @pl.when((kk == nk - 1) & (phase == 1) & (t == n_tiles - 1))
        def _():
            for tile in range(n_tiles):
                copy(tile).wait_send()

    out = _hosted_call(
        body, comm, name=name, grid=(2, n_tiles, nk), n_prefetch=1,
        out_shape=[jax.ShapeDtypeStruct((n_tiles * per, rows, tn), GRAD_COMM_DTYPE)],
        in_specs=[pl.BlockSpec((tk, tm), lambda p, t, k, c: (k, tiles.a_index(t, half(p, c)))),
                  pl.BlockSpec((tk, tn), lambda p, t, k, c: (k, tiles.b_index(t, half(p, c))))],
        out_specs=[pl.BlockSpec((per, rows, tn), lambda p, t, k, c: (p * t, 0, 0))],
        scratch_shapes=[pltpu.VMEM((tm, tn), F32), pltpu.VMEM((n_tiles, tm, tn), GRAD_COMM_DTYPE),
                        pltpu.VMEM((n_tiles, tm, tn), GRAD_COMM_DTYPE),
                        pltpu.SemaphoreType.DMA((n_tiles,)), pltpu.SemaphoreType.DMA((n_tiles,))],
        semantics=("arbitrary", "arbitrary", "arbitrary"),
    )(c_idx, a, b)
    return out[0] if comm is None else out


def _row_tile(rows, most=256, sublanes=16):
    return max(t for t in range(sublanes, most + 1, sublanes) if rows % t == 0)


def _adam_update(w, g, m, v):
    m_new = ADAM_B1 * m + (1.0 - ADAM_B1) * g
    v_new = ADAM_B2 * v + (1.0 - ADAM_B2) * (g * g)
    m_hat = m_new / (1.0 - ADAM_B1 ** ADAM_STEP)
    v_hat = v_new / (1.0 - ADAM_B2 ** ADAM_STEP)
    delta = -ADAM_LR * (m_hat / (jnp.sqrt(v_hat) + ADAM_EPS) + ADAM_WD * w)
    return delta, m_new, v_new


def _sum_chips_kernel(own, others, place, name):
    _, r, n = own.shape
    tr = _row_tile(r)

    def body(place_ref, own_ref, oth_ref, o_ref):
        del place_ref
        acc = own_ref[...].astype(F32)
        for k in range(N_CHIPS - 1):
            acc = acc + oth_ref[k].astype(F32)
        o_ref[...] = acc

    return pl.pallas_call(
        body, name=name, out_shape=jax.ShapeDtypeStruct((2, r, n), F32),
        grid_spec=pltpu.PrefetchScalarGridSpec(
            num_scalar_prefetch=1, grid=(r // tr,),
            in_specs=[pl.BlockSpec((None, tr, n), lambda i, p: (p[0], i, 0)),
                      pl.BlockSpec((N_CHIPS - 1, tr, n), lambda i, p: (0, i, 0))],
            out_specs=pl.BlockSpec((None, tr, n), lambda i, p: (p[1], i, 0))),
        compiler_params=_params("parallel"),
    )(place, own, others)


def _adam_kernel(w, g, m, v, name):
    r, n = w.shape
    by_columns = g.shape[1] == r
    tr, tn = _row_tile(g.shape[1], most=512), g.shape[2]

    def body(w_ref, g_ref, m_ref, v_ref, g_out, d_ref, mo_ref, vo_ref):
        gv = g_ref[...]
        g_out[...] = gv
        d_ref[...], mo_ref[...], vo_ref[...] = _adam_update(w_ref[...], gv, m_ref[...], v_ref[...])

    steps = g.shape[1] // tr
    spec = pl.BlockSpec((tr, tn), (lambda h, i: (i, h)) if by_columns else (lambda h, i: (h * steps + i, 0)))
    return pl.pallas_call(
        body, name=name, grid=(2, steps), out_shape=[jax.ShapeDtypeStruct((r, n), F32)] * 4,
        in_specs=[spec, pl.BlockSpec((None, tr, tn), lambda h, i: (h, i, 0)), spec, spec], out_specs=[spec] * 4,
        compiler_params=_params("parallel", "parallel"),
    )(w, g, m, v)


SMALL_PARAMS = ("b_ada", "g_mix", "g_ffn", "g_final", "b_spatial", "sinks", "w_spatial")


def _small_update_kernel(gathered, params):
    shapes = [params[nm][0].shape for nm in SMALL_PARAMS]

    def body(*refs):
        g_refs, refs = refs[:5], refs[5:]
        p_refs, refs = refs[:3 * len(SMALL_PARAMS)], refs[3 * len(SMALL_PARAMS):]
        loss_ref, o_refs = refs[0], refs[1:]

        def total(ref):
            acc = ref[0]
            for k in range(1, N_DEV):
                acc = acc + ref[k]
            return acc

        s1, s2, db, ds, dw = (total(r) for r in g_refs)
        loss_ref[...] = jnp.broadcast_to(s2[6:7, 0:1], loss_ref.shape)
        grads = {"b_ada": [s1[0:1], s1[1:2], s2[5:6], s2[0:1], s2[1:2], s2[2:3]], "g_mix": [s1[2:3]],
                 "g_ffn": [s2[3:4]], "g_final": [s2[4:5]], "b_spatial": [db.T[0:GMLP_GROUPS]],
                 "w_spatial": [dw]}
        lane = lax.broadcasted_iota(jnp.int32, (1, LANES), 1)
        sink_row = jnp.zeros((1, LANES), F32)
        for h in range(N_Q_HEADS):
            sink_row = sink_row + jnp.where(lane == h, ds[h:h + 1, :], 0.0)
        grads["sinks"] = [sink_row[:, 0:N_Q_HEADS]]
        for i, nm in enumerate(SMALL_PARAMS):
            w_ref, m_ref, v_ref = p_refs[3 * i:3 * i + 3]
            outs = o_refs[4 * i:4 * i + 4]
            width = grads[nm][0].shape[1]
            for k, g in enumerate(grads[nm]):
                cols = slice(width * k, width * (k + 1))
                upd = _adam_update(w_ref[:, cols], g, m_ref[:, cols], v_ref[:, cols])
                for o_ref, val in zip(outs, (g,) + upd):
                    o_ref[:, cols] = val

    flat = [a for nm in SMALL_PARAMS for a in params[nm]]
    out_shape = [jax.ShapeDtypeStruct((8, LANES), F32)]
    out_shape += [jax.ShapeDtypeStruct(s, F32) for s in shapes for _ in range(4)]
    outs = pl.pallas_call(
        body, name="small_update", grid=(1,), out_shape=out_shape,
        in_specs=[_full(g.shape) for g in gathered] + [_full(a.shape) for a in flat],
        out_specs=[_full(s.shape) for s in out_shape],
        compiler_params=_params("arbitrary"),
    )(*gathered, *flat)
    return {nm: outs[1 + 4 * i:5 + 4 * i] for i, nm in enumerate(SMALL_PARAMS)}, outs[0]


def _ada_update_kernel(act_t, dmod, w, m, v):
    r, n = w.shape
    tr = 256

    def body(a_ref, d_ref, w_ref, m_ref, v_ref, g_ref, dl_ref, mo_ref, vo_ref):
        g = _dot(a_ref[...], d_ref[...])
        g_ref[...] = g
        dl_ref[...], mo_ref[...], vo_ref[...] = _adam_update(w_ref[...], g, m_ref[...], v_ref[...])

    spec = pl.BlockSpec((tr, n), lambda i: (i, 0))
    return pl.pallas_call(
        body, name="ada_update", grid=(r // tr,), out_shape=[jax.ShapeDtypeStruct((r, n), F32)] * 4,
        in_specs=[pl.BlockSpec((tr, N_DEV), lambda i: (i, 0)), _full((N_DEV, n)), spec, spec, spec],
        out_specs=[spec] * 4, compiler_params=_params("parallel"),
    )(act_t, dmod, w, m, v)


def kernel(x, c, positions, w_ada, b_ada, g_mix, w_in, w_spatial, b_spatial, sinks, w_out, g_ffn, w_ff1, w_ff2, g_final, loss_target, m_w_ada, m_b_ada, m_g_mix, m_w_in, m_w_spatial, m_b_spatial, m_sinks, m_w_out, m_g_ffn, m_w_ff1, m_w_ff2, m_g_final, v_w_ada, v_b_ada, v_g_mix, v_w_in, v_w_spatial, v_b_spatial, v_sinks, v_w_out, v_g_ffn, v_w_ff1, v_w_ff2, v_g_final):
    xi, yi, ci = lax.axis_index("x"), lax.axis_index("y"), lax.axis_index("c")
    chip = 2 * xi + yi
    dev = 2 * chip + ci
    seq = x.shape[1]
    x2, tgt = x[0], loss_target[0]
    ada_cols = w_ada.shape[2]

    big = {"w_in": tuple(a[0].T for a in (w_in, m_w_in, v_w_in)),
           "w_out": (w_out[0], m_w_out[0], v_w_out[0]), "w_ff1": (w_ff1[0], m_w_ff1[0], v_w_ff1[0]),
           "w_ff2": (w_ff2[0], m_w_ff2[0], v_w_ff2[0])}

    def halves(nm):
        r, n = big[nm][0].shape
        return big[nm][0].astype(WEIGHT_COMM_DTYPE).reshape(2, r // 2, n)

    chip_idx = chip.reshape(1).astype(jnp.int32)
    w_in_local = halves("w_in")
    trunk_weights = ["w_out", "w_ff1", "w_ff2"]
    shards = [halves(nm) for nm in trunk_weights]
    w_in_started = _gather_start([w_in_local], "gather_w_in_start")
    trunk_started = _gather_start(shards, "gather_trunk_weights_start")

    c_all = _all_gather8([c], "gather_c")[0].reshape(N_DEV, D_MODEL)
    b_shard = lax.dynamic_slice(b_ada, (0, chip * ada_cols), (1, ada_cols))
    mod_part, act = _mod_kernel(c_all, w_ada[0], b_shard)
    w_in_g, = _gather_wait(w_in_started, mod_part, "gather_w_in_wait")
    mod_all, w_in_g = _all_gather8([mod_part], "gather_mod", forward=[w_in_g])
    w_in_g, w_in_local = w_in_g.reshape(N_CHIPS, W_IN_BLOCK, D_MODEL), w_in_local.reshape(W_IN_BLOCK, D_MODEL)
    mod_me = lax.dynamic_index_in_dim(mod_all[0::2], dev, axis=1, keepdims=False)
    mod_me = mod_me.reshape(N_MOD, D_MODEL)
    shift1, scale1, gate1, shift2, scale2, gate2 = (mod_me[k:k + 1] for k in range(N_MOD))

    zeros_row = jnp.zeros((1, D_MODEL), F32)
    vecs1 = jnp.concatenate([g_mix, shift1, scale1] + [zeros_row] * 5, axis=0)
    vecs2 = jnp.concatenate([gate1, shift2, scale2, gate2, g_ffn, g_final.reshape(1, D_MODEL)]
                            + [zeros_row] * 2, axis=0)
    bias_full = jnp.repeat(b_spatial[0].T, HEAD_DIM, axis=1)
    sink_rows = jnp.broadcast_to(sinks[0][:, None], (N_Q_HEADS, LANES))
    inv_freq = ROPE_THETA ** (-jnp.arange(0, ROT_DIM, 2, dtype=F32) / ROT_DIM)
    rope_tab = _rope_lane_tables(*_rope_angle_kernel(positions, inv_freq.reshape(ROT_DIM // 2, 1)))

    proj, hb = _in_proj_kernel(x2, vecs1, chip_idx, w_in_g, w_in_local)
    cat, = _mixer_fwd_kernel(proj, rope_tab, w_spatial[0], bias_full, sink_rows)
    staged = _gather_forward(_gather_wait(trunk_started, cat, "gather_trunk_weights_wait"), "gather_forward")
    dx1, dcat, dmix, h2b, rb, dab, dffb, sums2 = _trunk_kernel(
        x2, tgt, cat, vecs2, chip_idx,
        [g.reshape((N_CHIPS,) + big[nm][0].shape) for nm, g in zip(trunk_weights, staged)],
        [s.reshape(big[nm][0].shape) for nm, s in zip(trunk_weights, shards)])

    c_idx = ci.reshape(1).astype(jnp.int32)
    place = jnp.stack([chip, ci]).astype(jnp.int32)
    half_d = D_MODEL // 2
    cs_ff2 = _weight_grad_kernel(rb, dffb, c_idx, "dw_ff2",
                                 _GradTiles(D_MODEL, half_d, N_CHIPS, 1, lambda t, h: t, lambda t, h: h))
    cs_ff1, sc_ff2 = _weight_grad_kernel(
        h2b, dab, c_idx, "dw_ff1",
        _GradTiles(D_MODEL, half_d, N_CHIPS, 1, lambda t, h: 0, lambda t, h: 2 * t + h),
        comm=_scatter_job([cs_ff2]))
    cs_out = _weight_grad_kernel(cat, dmix, c_idx, "dw_out",
                                 _GradTiles(D_MODEL, half_d, 1, N_CHIPS, lambda t, h: 0, lambda t, h: h))
    dproj, dw_spatial, db_lanes, dsink_rows, sc_ff1, sc_out = _mixer_bwd_kernel(
        proj, rope_tab, dcat, w_spatial[0], w_spatial[0].transpose(0, 2, 1), bias_full, sink_rows,
        dev.reshape(1).astype(jnp.int32), comm=_scatter_job([cs_ff1, cs_out]))
    totals = [_sum_chips_kernel(own, oth, place, "grad_sum_" + nm)
              for nm, own, oth in (("w_out", cs_out, sc_out), ("w_ff1", cs_ff1, sc_ff1), ("w_ff2", cs_ff2, sc_ff2))]
    small_slots = [db_lanes, dsink_rows, dw_spatial.reshape(N_DEV, GMLP_GROUPS * CHUNK, CHUNK)]
    cs_in, *rode = _weight_grad_kernel(
        dproj, hb, c_idx, "dw_in",
        _GradTiles(2 * W_IN_BLOCK, half_d, N_CHIPS // 2, 2, lambda t, h: t, lambda t, h: h),
        comm=_merge_in_place(_gather_job(small_slots), _share_job(totals)))
    small_stage1, shared = rode[:len(small_slots)], rode[len(small_slots):]
    grad_x, sums1 = _in_proj_bwd_kernel(x2, dx1, dproj, vecs1, chip_idx, w_in_g, w_in_local)
    *gathered, sc_in = _all_gather8([sums1, sums2], "gather_small", forward=small_stage1,
                                    riders=[_scatter_job([cs_in])])
    total_in = _sum_chips_kernel(cs_in, sc_in, place, "grad_sum_w_in")
    shared = list(_sibling_share([total_in], "grad_share_w_in")) + list(shared)
    names = ["w_in", "w_out", "w_ff1", "w_ff2"]
    big_out = {}
    for nm, g in zip(names, shared):
        w, m, v = big[nm]
        outs = _adam_kernel(w, g, m, v, "adam_" + nm)
        big_out[nm] = tuple((t.T if nm == "w_in" else t)[None] for t in outs)

    small = {"b_ada": (b_ada, m_b_ada, v_b_ada), "g_mix": (g_mix, m_g_mix, v_g_mix),
             "g_ffn": (g_ffn, m_g_ffn, v_g_ffn), "g_final": (g_final, m_g_final, v_g_final),
             "b_spatial": (b_spatial, m_b_spatial, v_b_spatial), "sinks": (sinks, m_sinks, v_sinks),
             "w_spatial": (w_spatial, m_w_spatial, v_w_spatial)}
    flat_shape = {"g_final": (1, D_MODEL), "b_spatial": (GMLP_GROUPS, CHUNK), "w_spatial": (GMLP_GROUPS * CHUNK, CHUNK)}
    small_out, loss_tile = _small_update_kernel(
        gathered, {nm: tuple(a.reshape(flat_shape.get(nm, a.shape)) for a in small[nm]) for nm in small})
    small_out = {nm: [o.reshape(small[nm][0].shape) for o in small_out[nm]] for nm in small}
    loss = loss_tile[0, 0]

    g1, g2 = gathered[0], gathered[1]
    dmod_all = jnp.concatenate([g1[:, 0], g1[:, 1], g2[:, 5], g2[:, 0], g2[:, 1], g2[:, 2]], axis=1)
    dmod_cols = lax.dynamic_slice(dmod_all, (0, chip * ada_cols), (N_DEV, ada_cols))
    ada = _ada_update_kernel(act.T, dmod_cols, w_ada[0], m_w_ada[0], v_w_ada[0])
    big_out["w_ada"] = tuple(t[None] for t in ada)

    order = ["w_ada", "b_ada", "g_mix", "w_in", "w_spatial", "b_spatial", "sinks", "w_out", "g_ffn",
             "w_ff1", "w_ff2", "g_final"]

    def leaf(nm, k):
        return big_out[nm][k] if nm in big_out else small_out[nm][k]

    outs = [loss, grad_x[None]]
    for k in range(4):
        outs += [leaf(nm, k) for nm in order]
    return tuple(outs)
```

```python
import math
from typing import Callable, NamedTuple

import jax
import jax.numpy as jnp
from jax import lax
from jax.experimental import pallas as pl
from jax.experimental.pallas import tpu as pltpu

F32 = jnp.float32
MXU_DTYPE = jnp.bfloat16
WEIGHT_COMM_DTYPE = jnp.bfloat16
GRAD_COMM_DTYPE = jnp.bfloat16

D_MODEL = 1024
D_FF = 4096
HEAD_DIM = 64
GMLP_GROUPS = 8
GMLP_WIDTH = 512
CHUNK = 128
N_Q_HEADS = 8
N_KV_HEADS = 2
ATTN_WIDTH = 512
KV_WIDTH = 128
ROT_DIM = 16
ROPE_THETA = 500000.0
IN_PROJ_WIDTH = 1792
N_MOD = 6
EPS = 1e-5
N_CHIPS = 4
N_DEV = 8
LANES = 128
W_IN_BLOCK = IN_PROJ_WIDTH // N_CHIPS

ADAM_LR = 0.001
ADAM_B1 = 0.9
ADAM_B2 = 0.999
ADAM_EPS = 1e-08
ADAM_WD = 0.01
ADAM_STEP = 10

VMEM_LIMIT_BYTES = 58 * 1024 * 1024
MESH = pl.DeviceIdType.MESH


def _params(*semantics):
    return pltpu.CompilerParams(dimension_semantics=semantics, vmem_limit_bytes=VMEM_LIMIT_BYTES)


def _dot(a, b):
    return jnp.dot(a.astype(MXU_DTYPE), b.astype(MXU_DTYPE), preferred_element_type=F32)


def _dot_nt(a, b):
    return lax.dot_general(a.astype(MXU_DTYPE), b.astype(MXU_DTYPE), (((1,), (1,)), ((), ())),
                           preferred_element_type=F32)


def _dot_tn(a, b):
    return lax.dot_general(a.astype(MXU_DTYPE), b.astype(MXU_DTYPE), (((0,), (0,)), ((), ())),
                           preferred_element_type=F32)


def _full(shape):
    return pl.BlockSpec(shape, lambda *_: (0,) * len(shape))


def _any():
    return pl.BlockSpec(memory_space=pl.ANY)


def _rowsum(v):
    return jnp.sum(v, axis=0, keepdims=True)


def _mean_last(v):
    return jnp.mean(v, axis=-1, keepdims=True)


class _Comm(NamedTuple):
    operands: tuple
    out_shapes: tuple
    n_sems: int
    make: Callable
    in_place: int = 0


def _hosted_call(body, comm, *, name, grid, in_specs, out_shape, out_specs, scratch_shapes=(), semantics,
                 n_prefetch=0):
    if comm is None:
        return pl.pallas_call(
            body, name=name, out_shape=out_shape, compiler_params=_params(*semantics),
            grid_spec=pltpu.PrefetchScalarGridSpec(
                num_scalar_prefetch=n_prefetch, grid=grid, in_specs=in_specs, out_specs=out_specs,
                scratch_shapes=list(scratch_shapes)))
    n_in, n_out, n_scr = len(in_specs), len(out_shape), len(scratch_shapes)
    k_in, k_out = len(comm.operands), len(comm.out_shapes)

    def hosted(*refs):
        prefetched, refs = refs[:n_prefetch], refs[n_prefetch:]
        ins, refs = refs[:n_in], refs[n_in:]
        c_ins, refs = refs[:k_in], refs[k_in:]
        outs, refs = refs[:n_out], refs[n_out:]
        c_outs, refs = refs[:k_out], refs[k_out:]
        scratch, (send_sems, recv_sems) = refs[:n_scr], refs[n_scr:]
        first, last = None, None
        for d, size in enumerate(grid):
            at_start, at_end = pl.program_id(d) == 0, pl.program_id(d) == size - 1
            first = at_start if first is None else first & at_start
            last = at_end if last is None else last & at_end

        @pl.when(first)
        def _():
            for cp in comm.make(c_ins, c_outs, send_sems, recv_sems)[0]:
                cp.start()

        body(*prefetched, *ins, *outs, *scratch)

        @pl.when(last)
        def _():
            for wait in comm.make(c_ins, c_outs, send_sems, recv_sems)[1]:
                wait()

    aliases = {n_prefetch + n_in + i: n_out + i for i in range(comm.in_place)}
    call = pl.pallas_call(
        hosted, name=name, out_shape=list(out_shape) + list(comm.out_shapes),
        compiler_params=_params(*semantics), input_output_aliases=aliases,
        grid_spec=pltpu.PrefetchScalarGridSpec(
            num_scalar_prefetch=n_prefetch, grid=grid, in_specs=list(in_specs) + [_any()] * k_in,
            out_specs=list(out_specs) + [_any()] * k_out,
            scratch_shapes=list(scratch_shapes) + [pltpu.SemaphoreType.DMA((comm.n_sems,)),
                                                    pltpu.SemaphoreType.DMA((comm.n_sems,))]))
    return lambda *args: call(*args, *comm.operands)


class _Shifted:
    def __init__(self, base, offset):
        self.base, self.offset = base, offset

    @property
    def at(self):
        return self

    def __getitem__(self, k):
        return self.base.at[self.offset + k]


def _merge_in_place(*jobs):
    assert all(j.in_place == len(j.operands) == len(j.out_shapes) for j in jobs)

    def make(ins, outs, send_sems, recv_sems):
        starts, waits, at, sem = [], [], 0, 0
        for j in jobs:
            n = len(j.operands)
            s, w = j.make(ins[at:at + n], outs[at:at + n], _Shifted(send_sems, sem), _Shifted(recv_sems, sem))
            starts, waits, at, sem = starts + s, waits + w, at + n, sem + j.n_sems
        return starts, waits

    operands = tuple(a for j in jobs for a in j.operands)
    return _Comm(operands, tuple(s for j in jobs for s in j.out_shapes), sum(j.n_sems for j in jobs), make,
                 in_place=len(operands))


def _mesh_place():
    x, y, c = lax.axis_index("x"), lax.axis_index("y"), lax.axis_index("c")
    return x, y, c, [(1 - x, y), (x, 1 - y), (1 - x, 1 - y)]


def _gather_job(bufs):
    per = 4

    def make(ins, outs, send_sems, recv_sems):
        del ins
        x, y, c, chips = _mesh_place()
        starts, waits = [], []
        for a, out in enumerate(outs):
            mine = src = out.at[4 * x + 2 * y + c]
            to = [(x, y, 1 - c)] + [(px, py, c) for px, py in chips]
            sends = [pltpu.make_async_remote_copy(
                src_ref=src, dst_ref=mine, send_sem=send_sems.at[per * a + k],
                recv_sem=recv_sems.at[per * a + k], device_id=dev, device_id_type=MESH)
                for k, dev in enumerate(to)]
            recvs = [pltpu.make_async_remote_copy(
                src_ref=src, dst_ref=out.at[4 * px + 2 * py + pc], send_sem=send_sems.at[per * a + k],
                recv_sem=recv_sems.at[per * a + k], device_id=(px, py, pc), device_id_type=MESH)
                for k, (px, py, pc) in enumerate(to)]
            starts += sends
            waits += [s.wait_send for s in sends] + [r.wait_recv for r in recvs]
        return starts, waits

    shapes = tuple(jax.ShapeDtypeStruct(b.shape, b.dtype) for b in bufs)
    return _Comm(tuple(bufs), shapes, per * len(bufs), make, in_place=len(bufs))


def _split_copies(srcs, lands, send_sems, recv_sems):
    x, y, c, chips = _mesh_place()
    pairs = []
    for a, (src, land) in enumerate(zip(srcs, lands)):
        for k, (px, py) in enumerate(chips):
            sems = dict(send_sem=send_sems.at[3 * a + k], recv_sem=recv_sems.at[3 * a + k],
                        device_id=(px, py, c), device_id_type=MESH)
            pairs.append((pltpu.make_async_remote_copy(src_ref=src.at[c], dst_ref=land.at[4 * x + 2 * y + c], **sems),
                          pltpu.make_async_remote_copy(src_ref=src.at[c], dst_ref=land.at[4 * px + 2 * py + c], **sems)))
    return pairs


def _gather_start(halves, after, name):
    n = len(halves)
    hbm, sem = pl.BlockSpec(memory_space=pltpu.HBM), pl.BlockSpec(memory_space=pltpu.SEMAPHORE)

    def body(*refs):
        srcs, lands, send_sems, recv_sems = refs[:n], refs[n:2 * n], refs[2 * n + 1], refs[2 * n + 2]
        for send, _ in _split_copies(srcs, lands, send_sems, recv_sems):
            send.start()
        token = refs[-1]
        token[...] = jnp.zeros_like(token)

    lands = [lax.empty((N_DEV,) + h.shape[1:], h.dtype) for h in halves]
    operands = [pltpu.with_memory_space_constraint(a, pltpu.HBM) for a in list(halves) + lands]
    return pl.pallas_call(
        body, name=name,
        out_shape=[pltpu.SemaphoreType.DMA((3 * n,)), pltpu.SemaphoreType.DMA((3 * n,))]
        + [pltpu.HBM(a.shape, a.dtype) for a in operands] + [jax.ShapeDtypeStruct((8, LANES), F32)],
        in_specs=[hbm] * (2 * n) + [_any()],
        out_specs=[sem, sem] + [hbm] * (2 * n) + [pl.BlockSpec(memory_space=pltpu.VMEM)],
        input_output_aliases={i: 2 + i for i in range(2 * n)},
        compiler_params=pltpu.CompilerParams(has_side_effects=pltpu.SideEffectType.DATAFLOW_SIDE_EFFECTING),
    )(*operands, after)


def _gather_wait(started, after, name):
    send_sems, recv_sems, *bufs, _ = started
    n = len(bufs) // 2
    hbm, sem = pl.BlockSpec(memory_space=pltpu.HBM), pl.BlockSpec(memory_space=pltpu.SEMAPHORE)

    def body(*refs):
        srcs, lands, send_sems, recv_sems = refs[:n], refs[n:2 * n], refs[2 * n], refs[2 * n + 1]
        for send, arrival in _split_copies(srcs, lands, send_sems, recv_sems):
            send.wait_send()
            arrival.wait_recv()

    out = pl.pallas_call(
        body, name=name, out_shape=[pltpu.HBM(b.shape, b.dtype) for b in bufs],
        in_specs=[hbm] * (2 * n) + [sem, sem, _any()], out_specs=[hbm] * (2 * n),
        input_output_aliases={i: i for i in range(2 * n)},
        compiler_params=pltpu.CompilerParams(has_side_effects=pltpu.SideEffectType.DATAFLOW_SIDE_EFFECTING),
    )(*bufs, send_sems, recv_sems, after)
    return out[n:]


def _slots(x, y, c):
    return 4 * x + 2 * y + c, 4 * (1 - x) + 2 * y + c, 4 * x + 2 * (1 - y) + c, 4 * (1 - x) + 2 * (1 - y) + c


def _gather2d_first(halves):
    per = 2

    def make(ins, outs, send_sems, recv_sems):
        x, y, c, _ = _mesh_place()
        me, xn, yn, _ = _slots(x, y, c)
        starts, waits = [], []
        for a, (src, out) in enumerate(zip(ins, outs)):
            blk = src.at[c]
            rows = blk.shape[0] // 2
            upper, lower = pl.ds(0, rows), pl.ds(rows, rows)

            def copy(k, src_ref, dst_ref, dev, a=a):
                return pltpu.make_async_remote_copy(
                    src_ref=src_ref, dst_ref=dst_ref, send_sem=send_sems.at[per * a + k],
                    recv_sem=recv_sems.at[per * a + k], device_id=dev, device_id_type=MESH)

            sends = [copy(0, blk.at[upper], out.at[me, upper], (1 - x, y, c)),
                     copy(1, blk.at[lower], out.at[me, lower], (x, 1 - y, c))]
            recvs = [copy(0, blk.at[upper], out.at[xn, upper], (1 - x, y, c)),
                     copy(1, blk.at[lower], out.at[yn, lower], (x, 1 - y, c))]
            starts += sends
            waits += [s.wait_send for s in sends] + [r.wait_recv for r in recvs]
        return starts, waits

    shapes = tuple(jax.ShapeDtypeStruct((N_DEV,) + h.shape[1:], h.dtype) for h in halves)
    return _Comm(tuple(halves), shapes, per * len(halves), make)


def _gather2d_second(bufs, halves):
    per = 4
    n_arr = len(bufs)

    def make(ins, outs, send_sems, recv_sems):
        x, y, c, _ = _mesh_place()
        me, xn, yn, dg = _slots(x, y, c)
        starts, waits = [], []
        for a, buf in enumerate(outs):
            own = ins[n_arr + a].at[c]
            rows = buf.shape[1] // 2
            upper, lower = pl.ds(0, rows), pl.ds(rows, rows)
            plan = [(own.at[upper], me, upper, (x, 1 - y, c), yn), (buf.at[xn, upper], xn, upper, (x, 1 - y, c), dg),
                    (own.at[lower], me, lower, (1 - x, y, c), xn), (buf.at[yn, lower], yn, lower, (1 - x, y, c), dg)]
            for k, (src, slot, part, dev, landing) in enumerate(plan):
                sems = dict(send_sem=send_sems.at[per * a + k], recv_sem=recv_sems.at[per * a + k],
                            device_id=dev, device_id_type=MESH)
                send = pltpu.make_async_remote_copy(src_ref=src, dst_ref=buf.at[slot, part], **sems)
                arrival = pltpu.make_async_remote_copy(src_ref=src, dst_ref=buf.at[landing, part], **sems)
                starts.append(send)
                waits += [send.wait_send, arrival.wait_recv]
        return starts, waits

    shapes = tuple(jax.ShapeDtypeStruct(b.shape, b.dtype) for b in bufs)
    return _Comm(tuple(bufs) + tuple(halves), shapes, per * n_arr, make, in_place=n_arr)


def _gather_forward(bufs, name):
    n_arr = len(bufs)

    def body(*refs):
        outs = refs[n_arr:2 * n_arr]
        send_sems, recv_sems = refs[2 * n_arr:]
        x, y, c, chips = _mesh_place()
        sends, recvs = [], []
        for a, buf in enumerate(outs):
            for j, (px, py) in enumerate(chips):
                mine, theirs = buf.at[4 * px + 2 * py + c], buf.at[4 * px + 2 * py + 1 - c]
                sems = dict(send_sem=send_sems.at[3 * a + j], recv_sem=recv_sems.at[3 * a + j],
                            device_id=(x, y, 1 - c), device_id_type=MESH)
                sends.append(pltpu.make_async_remote_copy(src_ref=mine, dst_ref=mine, **sems))
                recvs.append(pltpu.make_async_remote_copy(src_ref=mine, dst_ref=theirs, **sems))
        for cp in sends:
            cp.start()
        for s, r in zip(sends, recvs):
            s.wait_send()
            r.wait_recv()

    return pl.pallas_call(
        body, name=name, out_shape=[jax.ShapeDtypeStruct(b.shape, b.dtype) for b in bufs],
        in_specs=[_any()] * n_arr, out_specs=[_any()] * n_arr,
        input_output_aliases={a: a for a in range(n_arr)},
        scratch_shapes=[pltpu.SemaphoreType.DMA((3 * n_arr,)), pltpu.SemaphoreType.DMA((3 * n_arr,))],
    )(*bufs)


def _scatter_job(chip_sums):
    def make(ins, outs, send_sems, recv_sems):
        x, y, c, chips = _mesh_place()
        copies = [pltpu.make_async_remote_copy(
            src_ref=src.at[2 * px + py], dst_ref=out.at[j], send_sem=send_sems.at[3 * a + j],
            recv_sem=recv_sems.at[3 * a + j], device_id=(px, py, c), device_id_type=MESH)
            for a, (src, out) in enumerate(zip(ins, outs)) for j, (px, py) in enumerate(chips)]
        return copies, [cp.wait for cp in copies]

    return _Comm(tuple(chip_sums), tuple(jax.ShapeDtypeStruct((3,) + s.shape[1:], s.dtype) for s in chip_sums),
                 3 * len(chip_sums), make)


def _all_gather8(blocks, name, split=False, forward=(), riders=(), skip_own=()):
    n_arr, n_fwd = len(blocks), len(forward)
    splits = list(split) if isinstance(split, (list, tuple)) else [split] * n_arr
    own_slots = [a not in skip_own for a in range(n_arr)]
    rider_in = sum(len(r.operands) for r in riders)
    rider_out = sum(len(r.out_shapes) for r in riders)

    def body(*refs):
        x_refs, refs = refs[:n_arr], refs[n_arr + n_fwd:]
        r_ins, refs = refs[:rider_in], refs[rider_in:]
        out_refs, refs = refs[:n_arr], refs[n_arr:]
        fwd_refs, refs = refs[:n_fwd], refs[n_fwd:]
        r_outs, refs = refs[:rider_out], refs[rider_out:]
        (send_sems, recv_sems, local_sems), rider_sems = refs[:3], refs[3:]
        x, y, c, chips = _mesh_place()
        me, sibling = (x, y, c), (x, y, 1 - c)
        rider_waits, i0, o0 = [], 0, 0
        for n, job in enumerate(riders):
            k_in, k_out = len(job.operands), len(job.out_shapes)
            starts, waits = job.make(r_ins[i0:i0 + k_in], r_outs[o0:o0 + k_out],
                                     rider_sems[2 * n], rider_sems[2 * n + 1])
            for cp in starts:
                cp.start()
            rider_waits += waits
            i0, o0 = i0 + k_in, o0 + k_out
        passing = []
        for f, buf in enumerate(fwd_refs):
            for j, (px, py) in enumerate(chips):
                mine, theirs = buf.at[4 * px + 2 * py + c], buf.at[4 * px + 2 * py + 1 - c]
                sems = dict(send_sem=send_sems.at[7 * n_arr + 3 * f + j], recv_sem=recv_sems.at[7 * n_arr + 3 * f + j],
                            device_id=sibling, device_id_type=MESH)
                passing.append((pltpu.make_async_remote_copy(src_ref=mine, dst_ref=mine, **sems),
                                pltpu.make_async_remote_copy(src_ref=mine, dst_ref=theirs, **sems)))
        for send, _ in passing:
            send.start()
        arrays = []
        for a, (x_ref, out_ref) in enumerate(zip(x_refs, out_refs)):
            src_mine = x_ref.at[c] if splits[a] else x_ref

            def copy(k, blk, to, src=None, a=a, out_ref=out_ref):
                dst = out_ref.at[4 * blk[0] + 2 * blk[1] + blk[2]]
                return pltpu.make_async_remote_copy(
                    src_ref=dst if src is None else src, dst_ref=dst,
                    send_sem=send_sems.at[7 * a + k], recv_sem=recv_sems.at[7 * a + k],
                    device_id=to, device_id_type=MESH)

            mine = pltpu.make_async_copy(src_mine, out_ref.at[4 * x + 2 * y + c], local_sems.at[a])
            first = [copy(0, me, sibling, src=src_mine)] if own_slots[a] else []
            first += [copy(1 + j, me, (*chip, c), src=src_mine) for j, chip in enumerate(chips)]
            for cp in first + ([mine] if own_slots[a] else []):
                cp.start()
            arrays.append((copy, mine, first, own_slots[a]))
        sent = []
        for copy, mine, first, own in arrays:
            passed = [copy(4 + j, (*chip, c), sibling) for j, chip in enumerate(chips)]
            for j, chip in enumerate(chips):
                copy(1 + j, (*chip, c), me).wait_recv()
                passed[j].start()
            sent += first + passed
        for copy, mine, first, own in arrays:
            if own:
                copy(0, sibling, me).wait_recv()
                mine.wait()
            for j, chip in enumerate(chips):
                copy(4 + j, (*chip, 1 - c), me).wait_recv()
        for cp in sent:
            cp.wait_send()
        for send, arrival in passing:
            send.wait_send()
            arrival.wait_recv()
        for wait in rider_waits:
            wait()

    n_sems = 7 * n_arr + 3 * n_fwd
    rider_operands = [a for r in riders for a in r.operands]
    rider_shapes = [s for r in riders for s in r.out_shapes]
    return pl.pallas_call(
        body, name=name,
        out_shape=[jax.ShapeDtypeStruct((N_DEV,) + tuple(b.shape[1:] if s else b.shape), b.dtype)
                   for b, s in zip(blocks, splits)]
        + [jax.ShapeDtypeStruct(f.shape, f.dtype) for f in forward] + rider_shapes,
        in_specs=[_any()] * (n_arr + n_fwd + rider_in), out_specs=[_any()] * (n_arr + n_fwd + rider_out),
        input_output_aliases={n_arr + f: n_arr + f for f in range(n_fwd)},
        scratch_shapes=[pltpu.SemaphoreType.DMA((n_sems,)), pltpu.SemaphoreType.DMA((n_sems,)),
                        pltpu.SemaphoreType.DMA((n_arr,))]
        + [pltpu.SemaphoreType.DMA((r.n_sems,)) for r in riders for _ in range(2)],
    )(*blocks, *forward, *rider_operands)


def _share_job(bufs):
    def make(ins, outs, send_sems, recv_sems):
        del ins
        x, y, c, _ = _mesh_place()
        sems = lambda a: dict(send_sem=send_sems.at[a], recv_sem=recv_sems.at[a],
                              device_id=(x, y, 1 - c), device_id_type=MESH)
        sends = [pltpu.make_async_remote_copy(src_ref=o.at[c], dst_ref=o.at[c], **sems(a)) for a, o in enumerate(outs)]
        arrivals = [pltpu.make_async_remote_copy(src_ref=o.at[c], dst_ref=o.at[1 - c], **sems(a))
                    for a, o in enumerate(outs)]
        return sends, [s.wait_send for s in sends] + [r.wait_recv for r in arrivals]

    shapes = tuple(jax.ShapeDtypeStruct(b.shape, b.dtype) for b in bufs)
    return _Comm(tuple(bufs), shapes, len(bufs), make, in_place=len(bufs))


def _sibling_share(bufs, name):
    n_arr = len(bufs)

    def body(*refs):
        out_refs = refs[n_arr:2 * n_arr]
        send_sems, recv_sems = refs[2 * n_arr:]
        x, y, c = lax.axis_index("x"), lax.axis_index("y"), lax.axis_index("c")
        copies = [pltpu.make_async_remote_copy(
            src_ref=out_refs[a].at[c], dst_ref=out_refs[a].at[c],
            send_sem=send_sems.at[a], recv_sem=recv_sems.at[a],
            device_id=(x, y, 1 - c), device_id_type=MESH) for a in range(n_arr)]
        for cp in copies:
            cp.start()
        for a in range(n_arr):
            pltpu.make_async_remote_copy(
                src_ref=out_refs[a].at[c], dst_ref=out_refs[a].at[1 - c],
                send_sem=send_sems.at[a], recv_sem=recv_sems.at[a],
                device_id=(x, y, 1 - c), device_id_type=MESH).wait()

    return pl.pallas_call(
        body, name=name,
        out_shape=[jax.ShapeDtypeStruct(b.shape, b.dtype) for b in bufs],
        in_specs=[_any()] * n_arr, out_specs=[_any()] * n_arr,
        input_output_aliases={a: a for a in range(n_arr)},
        scratch_shapes=[pltpu.SemaphoreType.DMA((n_arr,)), pltpu.SemaphoreType.DMA((n_arr,))],
    )(*bufs)


def _gelu_tanh(z):
    k = math.sqrt(2.0 / math.pi)
    t = jnp.tanh(k * (z + 0.044715 * (z * z * z)))
    return 0.5 * z * (1.0 + t), t


def _gelu_tanh_grad(z, t):
    k = math.sqrt(2.0 / math.pi)
    return 0.5 * (1.0 + t) + 0.5 * z * (1.0 - t * t) * (k * (1.0 + 3.0 * 0.044715 * (z * z)))


def _rope_angle_kernel(pos_row, invf_col):
    seq = pos_row.shape[1]

    def body(p_ref, f_ref, cos_ref, sin_ref):
        ang = p_ref[...].astype(F32) * f_ref[...]
        cos_ref[...] = jnp.cos(ang)
        sin_ref[...] = jnp.sin(ang)

    return pl.pallas_call(
        body, name="rope_angles", grid=(1,), out_shape=[jax.ShapeDtypeStruct((ROT_DIM // 2, seq), F32)] * 2,
        in_specs=[_full((1, seq)), _full((ROT_DIM // 2, 1))], out_specs=[_full((ROT_DIM // 2, seq))] * 2,
        compiler_params=_params("arbitrary"),
    )(pos_row, invf_col)


def _rope_lane_tables(cos, sin):
    cos_t, sin_t = cos.T, sin.T
    seq, half = cos_t.shape
    ones = jnp.ones((seq, HEAD_DIM - ROT_DIM), F32)
    c64 = jnp.concatenate([cos_t, cos_t, ones], axis=1)
    s1 = jnp.concatenate([sin_t, jnp.zeros((seq, HEAD_DIM - half), F32)], axis=1)
    s2 = jnp.concatenate([jnp.zeros((seq, half), F32), sin_t, jnp.zeros((seq, HEAD_DIM - ROT_DIM), F32)], axis=1)
    return jnp.concatenate([jnp.tile(t, (1, LANES // HEAD_DIM)) for t in (c64, s1, s2)], axis=1)


def _rope_apply(t, tab, sign):
    reps = t.shape[1] // LANES
    c_tab, s1, s2 = (jnp.tile(tab[:, LANES * k:LANES * (k + 1)], (1, reps)) if reps > 1
                     else tab[:, LANES * k:LANES * (k + 1)] for k in range(3))
    half = ROT_DIM // 2
    up = pltpu.roll(t, t.shape[1] - half, 1)
    down = pltpu.roll(t, half, 1)
    return t * c_tab + sign * (down * s2 - up * s1)


def _lane_masks(shape):
    lane = lax.broadcasted_iota(jnp.int32, shape, 1)
    return lane < HEAD_DIM, lane >= HEAD_DIM


HEADS_PER_GROUP = N_Q_HEADS // N_KV_HEADS
ATTN_SCALE = 1.0 / math.sqrt(HEAD_DIM)


def _attn_bias_t(first_block):
    kj = lax.broadcasted_iota(jnp.int32, (2 * CHUNK, CHUNK), 0)
    qi = lax.broadcasted_iota(jnp.int32, (2 * CHUNK, CHUNK), 1)
    ok = (kj > qi) & (kj <= qi + CHUNK)
    if first_block is not None:
        ok = ok & (jnp.logical_not(first_block) | (kj >= CHUNK))
    return jnp.tile(jnp.where(ok, 0.0, -jnp.inf), (1, HEADS_PER_GROUP))


def _group_rows(x, g, lo, hi):
    rows = []
    for r in range(HEADS_PER_GROUP):
        h = HEADS_PER_GROUP * g + r
        pair = x[:, LANES * (h // 2):LANES * (h // 2 + 1)]
        rows.append(jnp.where(hi if h % 2 else lo, pair, 0.0))
    return jnp.concatenate(rows, axis=0)


def _pairs_from_rows(rows, lo):
    return [jnp.where(lo, rows[2 * CHUNK * k:2 * CHUNK * k + CHUNK], rows[2 * CHUNK * k + CHUNK:2 * CHUNK * (k + 1)])
            for k in range(HEADS_PER_GROUP // 2)]


def _group_dup(a, b, g, lo2):
    return jnp.where(lo2, a, b) if g == 0 else jnp.where(lo2, b, a)


def _sink_row(sink_ref, g):
    return jnp.concatenate([sink_ref[HEADS_PER_GROUP * g + r:HEADS_PER_GROUP * g + r + 1, :]
                            for r in range(HEADS_PER_GROUP)], axis=1)


def _attn_probs_t(k_dup, q_rows, bias_t, sink_row):
    s_t = _dot_nt(k_dup, q_rows) * ATTN_SCALE + bias_t
    m = jnp.maximum(jnp.max(s_t, axis=0, keepdims=True), sink_row)
    p = jnp.exp(s_t - m)
    e_sink = jnp.exp(sink_row - m)
    inv = 1.0 / (jnp.sum(p, axis=0, keepdims=True) + e_sink)
    return p * inv, e_sink * inv


def _sgu_forward_pair(wm, vp, j):
    lo, hi = _lane_masks(vp.shape)
    lhs = jnp.concatenate([wm[2 * j], wm[2 * j + 1]], axis=1)
    rhs = jnp.concatenate([jnp.where(lo, vp, 0.0), jnp.where(hi, vp, 0.0)], axis=0)
    return _dot(lhs, rhs)


def _masked_spatial(w_ref):
    t = lax.broadcasted_iota(jnp.int32, (CHUNK, CHUNK), 0)
    s = lax.broadcasted_iota(jnp.int32, (CHUNK, CHUNK), 1)
    tril = s <= t
    return [jnp.where(tril, w_ref[g], 0.0) for g in range(GMLP_GROUPS)], tril, s >= t


def _mod_kernel(c_all, w_shard, b_shard, comm=None):
    n = w_shard.shape[1]
    tn = 512

    def body(c_ref, w_ref, b_ref, mod_ref, act_ref):
        cv = c_ref[...]
        act = cv * (1.0 / (1.0 + jnp.exp(-cv)))
        act_ref[...] = act
        mod_ref[...] = _dot(act, w_ref[...]) + b_ref[...]

    return _hosted_call(
        body, comm, name="ada_mod", grid=(n // tn,),
        out_shape=[jax.ShapeDtypeStruct((N_DEV, n), F32), jax.ShapeDtypeStruct((N_DEV, D_MODEL), F32)],
        in_specs=[_full((N_DEV, D_MODEL)), pl.BlockSpec((D_MODEL, tn), lambda i: (0, i)),
                  pl.BlockSpec((1, tn), lambda i: (0, i))],
        out_specs=[pl.BlockSpec((N_DEV, tn), lambda i: (0, i)), _full((N_DEV, D_MODEL))],
        semantics=("arbitrary",),
    )(c_all, w_shard, b_shard)


def _load_chip_blocks(chip_ref, gathered, local, dsts, sems, first_sem=0):
    for k, dst in enumerate(dsts):
        @pl.when(chip_ref[0] == k)
        def _():
            pltpu.make_async_copy(local, dst, sems.at[first_sem + k]).start()

        @pl.when(chip_ref[0] != k)
        def _():
            pltpu.make_async_copy(gathered.at[k], dst, sems.at[first_sem + k]).start()
    return [pltpu.make_async_copy(local, dst, sems.at[first_sem + k]).wait for k, dst in enumerate(dsts)]


def _w_in_rows(w_scr):
    return [w_scr.at[pl.ds(W_IN_BLOCK * k, W_IN_BLOCK)] for k in range(N_CHIPS)]


def _in_proj_kernel(x, vecs, chip_idx, w_in_gathered, w_in_local, comm=None):
    seq = x.shape[0]
    tm = 512

    def body(chip_ref, x_ref, v_ref, wg_ref, wl_ref, proj_ref, h_ref, w_ref, sems):
        @pl.when(pl.program_id(0) == 0)
        def _():
            for wait in _load_chip_blocks(chip_ref, wg_ref, wl_ref, _w_in_rows(w_ref), sems):
                wait()

        xv = x_ref[...]
        rstd = lax.rsqrt(_mean_last(xv * xv) + EPS)
        n1 = (xv * rstd) * v_ref[0:1, :]
        h = n1 * (1.0 + v_ref[2:3, :]) + v_ref[1:2, :]
        hb = h.astype(MXU_DTYPE)
        h_ref[...] = hb
        proj_ref[...] = _dot_nt(hb, w_ref[...])

    return _hosted_call(
        body, comm, name="in_proj", grid=(seq // tm,), n_prefetch=1,
        out_shape=[jax.ShapeDtypeStruct((seq, IN_PROJ_WIDTH), F32),
                   jax.ShapeDtypeStruct((seq, D_MODEL), MXU_DTYPE)],
        in_specs=[pl.BlockSpec((tm, D_MODEL), lambda i, chip: (i, 0)), _full((8, D_MODEL)), _any(), _any()],
        out_specs=[pl.BlockSpec((tm, IN_PROJ_WIDTH), lambda i, chip: (i, 0)),
                   pl.BlockSpec((tm, D_MODEL), lambda i, chip: (i, 0))],
        scratch_shapes=[pltpu.VMEM((IN_PROJ_WIDTH, D_MODEL), MXU_DTYPE), pltpu.SemaphoreType.DMA((N_CHIPS,))],
        semantics=("arbitrary",),
    )(chip_idx, x, vecs, w_in_gathered, w_in_local)


MIXER_BLOCKS_PER_STEP = 2
KV_START = 2 * GMLP_WIDTH + ATTN_WIDTH


def _mixer_fwd_kernel(proj, rope_tab, w_spatial, bias_full, sink_rows, comm=None):
    seq = proj.shape[0]
    per = MIXER_BLOCKS_PER_STEP
    steps = seq // (CHUNK * per)
    kv_col = KV_START // (2 * KV_WIDTH)

    def body(proj_ref, prev_ref, tab_ref, ptab_ref, w_ref, bias_ref, sink_ref, cat_ref):
        i = pl.program_id(0)
        wm, _, _ = _masked_spatial(w_ref)
        lo, hi = _lane_masks((CHUNK, LANES))
        lo2, _ = _lane_masks((2 * CHUNK, LANES))
        o = 2 * GMLP_WIDTH
        for s in range(per):
            rows, before = slice(CHUNK * s, CHUNK * (s + 1)), slice(CHUNK * (s - 1), CHUNK * s)
            for j in range(GMLP_GROUPS // 2):
                cols = slice(LANES * j, LANES * (j + 1))
                vcols = slice(GMLP_WIDTH + LANES * j, GMLP_WIDTH + LANES * (j + 1))
                u, _ = _gelu_tanh(proj_ref[rows, cols])
                vp, _ = _gelu_tanh(proj_ref[rows, vcols])
                sv = _sgu_forward_pair(wm, vp, j) + bias_ref[:, cols]
                cat_ref[rows, cols] = (u * sv).astype(cat_ref.dtype)
            tab = tab_ref[rows, :]
            if s == 0:
                prev_kv, prev_tab, first = prev_ref[...], ptab_ref[...], i == 0
            else:
                prev_kv, prev_tab, first = proj_ref[before, KV_START:KV_START + 2 * KV_WIDTH], tab_ref[before, :], None
            q_r = _rope_apply(proj_ref[rows, o:o + ATTN_WIDTH], tab, 1.0)
            k_cur = _rope_apply(proj_ref[rows, KV_START:KV_START + KV_WIDTH], tab, 1.0)
            k_prev = _rope_apply(prev_kv[:, 0:KV_WIDTH], prev_tab, 1.0)
            k_a = jnp.concatenate([k_prev, k_cur], axis=0)
            v_a = jnp.concatenate([prev_kv[:, KV_WIDTH:2 * KV_WIDTH],
                                   proj_ref[rows, KV_START + KV_WIDTH:KV_START + 2 * KV_WIDTH]], axis=0)
            k_b = pltpu.roll(k_a, HEAD_DIM, 1)
            v_b = pltpu.roll(v_a, HEAD_DIM, 1)
            bias_t = _attn_bias_t(first)
            for g in range(N_KV_HEADS):
                p_t, _ = _attn_probs_t(_group_dup(k_a, k_b, g, lo2), _group_rows(q_r, g, lo, hi), bias_t,
                                       _sink_row(sink_ref, g))
                o_t = _dot(_group_dup(v_a, v_b, g, lo2).T, p_t)
                for k, pair in enumerate(_pairs_from_rows(o_t.T, lo)):
                    c0 = GMLP_WIDTH + LANES * (2 * g + k)
                    cat_ref[rows, c0:c0 + LANES] = pair.astype(cat_ref.dtype)

    return _hosted_call(
        body, comm, name="mixer_fwd", grid=(steps,),
        out_shape=[jax.ShapeDtypeStruct((seq, D_MODEL), MXU_DTYPE)],
        in_specs=[pl.BlockSpec((CHUNK * per, IN_PROJ_WIDTH), lambda i: (i, 0)),
                  pl.BlockSpec((CHUNK, 2 * KV_WIDTH), lambda i: (jnp.maximum(per * i - 1, 0), kv_col)),
                  pl.BlockSpec((CHUNK * per, 3 * LANES), lambda i: (i, 0)),
                  pl.BlockSpec((CHUNK, 3 * LANES), lambda i: (jnp.maximum(per * i - 1, 0), 0)),
                  _full((GMLP_GROUPS, CHUNK, CHUNK)), _full((CHUNK, GMLP_WIDTH)),
                  _full((N_Q_HEADS, LANES))],
        out_specs=[pl.BlockSpec((CHUNK * per, D_MODEL), lambda i: (i, 0))],
        semantics=("arbitrary",),
    )(proj, proj, rope_tab, rope_tab, w_spatial, bias_full, sink_rows)


def _trunk_kernel(x, target, cat, vecs, chip_idx, gathered, local):
    seq = x.shape[0]
    tm = 256
    nj = D_FF // D_MODEL
    out_rows = D_MODEL // N_CHIPS

    def body(chip_ref, x_ref, t_ref, cat_ref, v_ref, g_out, g_w1, g_w2, l_out, l_w1, l_w2,
             dx1_ref, dcat_ref, dmix_ref, h2_ref, r_ref, da_ref, dff_ref, sums_ref,
             wout, w1, w2, a_scr, sem):
        i = pl.program_id(0)

        @pl.when(i == 0)
        def _():
            waits = _load_chip_blocks(chip_ref, g_out, l_out,
                                      [wout.at[pl.ds(out_rows * k, out_rows)] for k in range(N_CHIPS)], sem)
            waits += _load_chip_blocks(chip_ref, g_w1, l_w1, [w1.at[k] for k in range(N_CHIPS)], sem, N_CHIPS)
            waits += _load_chip_blocks(chip_ref, g_w2, l_w2, [w2.at[k] for k in range(N_CHIPS)], sem, 2 * N_CHIPS)
            for wait in waits:
                wait()
            sums_ref[...] = jnp.zeros_like(sums_ref)

        gate1, shift2, scale2 = v_ref[0:1, :], v_ref[1:2, :], v_ref[2:3, :]
        gate2, g_ffn, g_final = v_ref[3:4, :], v_ref[4:5, :], v_ref[5:6, :]

        mix = _dot(cat_ref[...], wout[...])
        x1 = x_ref[...] + gate1 * mix
        rstd2 = lax.rsqrt(_mean_last(x1 * x1) + EPS)
        xh2 = x1 * rstd2
        n2 = xh2 * g_ffn
        h2b = (n2 * (1.0 + scale2) + shift2).astype(MXU_DTYPE)
        h2_ref[...] = h2b
        ff = jnp.zeros((tm, D_MODEL), F32)
        for j in range(nj):
            a = _dot(h2b, w1[j])
            a_scr[j] = a
            relu = jnp.maximum(a, 0.0)
            rb = (relu * relu).astype(MXU_DTYPE)
            r_ref[:, D_MODEL * j:D_MODEL * (j + 1)] = rb
            ff = ff + _dot(rb, w2[j])
        x2 = x1 + gate2 * ff
        rstd3 = lax.rsqrt(_mean_last(x2 * x2) + EPS)
        xh3 = x2 * rstd3
        err = xh3 * g_final - t_ref[...]
        loss = 0.5 * _rowsum(_mean_last(err * err))
        dy = err * (1.0 / D_MODEL)
        dxh3 = dy * g_final
        dx2 = rstd3 * (dxh3 - xh3 * _mean_last(dxh3 * xh3))
        dffb = (dx2 * gate2).astype(MXU_DTYPE)
        dff_ref[...] = dffb
        dh2 = jnp.zeros((tm, D_MODEL), F32)
        for j in range(nj):
            dr = _dot_nt(dffb, w2[j])
            dab = (dr * (2.0 * jnp.maximum(a_scr[j], 0.0))).astype(MXU_DTYPE)
            da_ref[:, D_MODEL * j:D_MODEL * (j + 1)] = dab
            dh2 = dh2 + _dot_nt(dab, w1[j])
        dn2 = dh2 * (1.0 + scale2)
        dxh2 = dn2 * g_ffn
        dx1 = dx2 + rstd2 * (dxh2 - xh2 * _mean_last(dxh2 * xh2))
        dx1_ref[...] = dx1
        dmixb = (dx1 * gate1).astype(MXU_DTYPE)
        dmix_ref[...] = dmixb
        dcat_ref[...] = _dot_nt(dmixb, wout[...])

        sums_ref[0:1, :] += _rowsum(dh2)
        sums_ref[1:2, :] += _rowsum(dh2 * n2)
        sums_ref[2:3, :] += _rowsum(dx2 * ff)
        sums_ref[3:4, :] += _rowsum(dn2 * xh2)
        sums_ref[4:5, :] += _rowsum(dy * xh3)
        sums_ref[5:6, :] += _rowsum(dx1 * mix)
        sums_ref[6:7, :] += jnp.broadcast_to(loss, (1, D_MODEL))

    tok = lambda w: pl.BlockSpec((tm, w), lambda i, chip: (i, 0))
    return _hosted_call(
        body, None, name="trunk", grid=(seq // tm,), n_prefetch=1,
        out_shape=[jax.ShapeDtypeStruct((seq, D_MODEL), F32), jax.ShapeDtypeStruct((seq, D_MODEL), F32),
                   jax.ShapeDtypeStruct((seq, D_MODEL), MXU_DTYPE), jax.ShapeDtypeStruct((seq, D_MODEL), MXU_DTYPE),
                   jax.ShapeDtypeStruct((seq, D_FF), MXU_DTYPE), jax.ShapeDtypeStruct((seq, D_FF), MXU_DTYPE),
                   jax.ShapeDtypeStruct((seq, D_MODEL), MXU_DTYPE), jax.ShapeDtypeStruct((8, D_MODEL), F32)],
        in_specs=[tok(D_MODEL), tok(D_MODEL), tok(D_MODEL), _full((8, D_MODEL))] + [_any()] * 6,
        out_specs=[tok(D_MODEL), tok(D_MODEL), tok(D_MODEL), tok(D_MODEL), tok(D_FF), tok(D_FF), tok(D_MODEL),
                   _full((8, D_MODEL))],
        scratch_shapes=[pltpu.VMEM((D_MODEL, D_MODEL), MXU_DTYPE), pltpu.VMEM((nj, D_MODEL, D_MODEL), MXU_DTYPE),
                        pltpu.VMEM((nj, D_MODEL, D_MODEL), MXU_DTYPE), pltpu.VMEM((nj, tm, D_MODEL), F32),
                        pltpu.SemaphoreType.DMA((3 * N_CHIPS,))],
        semantics=("arbitrary",),
    )(chip_idx, x, target, cat, vecs, *gathered, *local)


def _mixer_bwd_kernel(proj, rope_tab, dcat, w_spatial, w_spatial_t, bias_full, sink_rows, dev_idx, comm=None):
    seq = proj.shape[0]
    per = MIXER_BLOCKS_PER_STEP
    steps = seq // (CHUNK * per)
    kv_col = KV_START // (2 * KV_WIDTH)

    def body(dev_ref, proj_ref, prev_ref, tab_ref, ptab_ref, dcat_ref, w_ref, wt_ref, bias_ref, sink_ref,
             dproj_ref, dw_ref, db_ref, dsink_ref, carry):
        del dev_ref
        step = pl.program_id(0)

        @pl.when(step == 0)
        def _():
            carry[...] = jnp.zeros_like(carry)
            dw_ref[...] = jnp.zeros_like(dw_ref)
            db_ref[...] = jnp.zeros_like(db_ref)
            dsink_ref[...] = jnp.zeros_like(dsink_ref)

        for s in reversed(range(per)):
            rows = pl.ds(CHUNK * s, CHUNK)
            if s == 0:
                before, before_tab, first = prev_ref, ptab_ref, step == steps - 1
            else:
                before = proj_ref.at[pl.ds(CHUNK * (s - 1), CHUNK), pl.ds(KV_START, 2 * KV_WIDTH)]
                before_tab, first = tab_ref.at[pl.ds(CHUNK * (s - 1), CHUNK)], None
            one_block(proj_ref.at[rows], before, tab_ref.at[rows], before_tab, dcat_ref.at[rows], w_ref, wt_ref,
                      bias_ref, sink_ref, dproj_ref.at[rows], dw_ref, db_ref, dsink_ref, carry, first)

    def one_block(proj_ref, prev_ref, tab_ref, ptab_ref, dcat_ref, w_ref, wt_ref, bias_ref, sink_ref,
                  dproj_ref, dw_ref, db_ref, dsink_ref, carry, first):
        wm, tril, triu = _masked_spatial(w_ref)
        lo, hi = _lane_masks((CHUNK, LANES))
        lane = lax.broadcasted_iota(jnp.int32, (CHUNK, LANES), 1)
        db = jnp.zeros((CHUNK, LANES), F32)
        for j in range(GMLP_GROUPS // 2):
            cols = slice(LANES * j, LANES * (j + 1))
            vcols = slice(GMLP_WIDTH + LANES * j, GMLP_WIDTH + LANES * (j + 1))
            zu, zv = proj_ref[:, cols], proj_ref[:, vcols]
            u, tu = _gelu_tanh(zu)
            vp, tv = _gelu_tanh(zv)
            sv = _sgu_forward_pair(wm, vp, j) + bias_ref[:, cols]
            dout = dcat_ref[:, cols]
            du = dout * sv
            dsv = dout * u
            dsv_lo, dsv_hi = jnp.where(lo, dsv, 0.0), jnp.where(hi, dsv, 0.0)
            lhs_t = jnp.concatenate([jnp.where(triu, wt_ref[2 * j], 0.0),
                                     jnp.where(triu, wt_ref[2 * j + 1], 0.0)], axis=1)
            dv = _dot(lhs_t, jnp.concatenate([dsv_lo, dsv_hi], axis=0))
            dw_ref[2 * j] += jnp.where(tril, _dot_nt(dsv_lo, vp), 0.0)
            dw_ref[2 * j + 1] += jnp.where(tril, _dot_nt(dsv_hi, vp), 0.0)
            db = db + (jnp.where(lane == 2 * j, jnp.sum(dsv_lo, axis=1, keepdims=True), 0.0)
                       + jnp.where(lane == 2 * j + 1, jnp.sum(dsv_hi, axis=1, keepdims=True), 0.0))
            dproj_ref[:, cols] = (du * _gelu_tanh_grad(zu, tu)).astype(dproj_ref.dtype)
            dproj_ref[:, vcols] = (dv * _gelu_tanh_grad(zv, tv)).astype(dproj_ref.dtype)
        db_ref[...] += db
        o = 2 * GMLP_WIDTH
        tab = tab_ref[...]
        q_r = _rope_apply(proj_ref[:, o:o + ATTN_WIDTH], tab, 1.0)
        k_cur = _rope_apply(proj_ref[:, o + ATTN_WIDTH:o + ATTN_WIDTH + KV_WIDTH], tab, 1.0)
        k_prev = _rope_apply(prev_ref[:, 0:KV_WIDTH], ptab_ref[...], 1.0)
        k_a = jnp.concatenate([k_prev, k_cur], axis=0)
        v_a = jnp.concatenate([prev_ref[:, KV_WIDTH:2 * KV_WIDTH],
                               proj_ref[:, o + ATTN_WIDTH + KV_WIDTH:o + ATTN_WIDTH + 2 * KV_WIDTH]], axis=0)
        k_b = pltpu.roll(k_a, HEAD_DIM, 1)
        v_b = pltpu.roll(v_a, HEAD_DIM, 1)
        bias_t = _attn_bias_t(first)
        lo2, _ = _lane_masks((2 * CHUNK, LANES))
        dout_b = dcat_ref[:, GMLP_WIDTH:GMLP_WIDTH + ATTN_WIDTH]
        dk_tot, dv_tot, dq_pairs = [], [], []
        for g in range(N_KV_HEADS):
            k_dup, v_dup = _group_dup(k_a, k_b, g, lo2), _group_dup(v_a, v_b, g, lo2)
            q_rows = _group_rows(q_r, g, lo, hi)
            do_rows = _group_rows(dout_b, g, lo, hi)
            p_t, p_sink = _attn_probs_t(k_dup, q_rows, bias_t, _sink_row(sink_ref, g))
            dp_t = _dot_nt(v_dup, do_rows)
            delta = jnp.sum(p_t * dp_t, axis=0, keepdims=True)
            ds_t = p_t * (dp_t - delta) * ATTN_SCALE
            dsink = -p_sink * delta
            for r in range(HEADS_PER_GROUP):
                h = HEADS_PER_GROUP * g + r
                dsink_ref[h:h + 1, :] += jnp.broadcast_to(
                    jnp.sum(dsink[:, LANES * r:LANES * (r + 1)], axis=1, keepdims=True), (1, LANES))
            dk_full = _dot(ds_t, q_rows)
            dv_full = _dot(p_t, do_rows)
            dk_tot.append(dk_full + pltpu.roll(dk_full, HEAD_DIM, 1))
            dv_tot.append(dv_full + pltpu.roll(dv_full, HEAD_DIM, 1))
            dq_t = _dot(k_dup.T, ds_t)
            dq_pairs += _pairs_from_rows(dq_t.T, lo)
        dk_all = jnp.where(lo2, dk_tot[0], dk_tot[1])
        dv_all = jnp.where(lo2, dv_tot[0], dv_tot[1])
        dk_cur = dk_all[CHUNK:, :] + carry[:, 0:KV_WIDTH]
        dv_cur = dv_all[CHUNK:, :] + carry[:, KV_WIDTH:2 * KV_WIDTH]
        carry[:, 0:KV_WIDTH] = dk_all[:CHUNK, :]
        carry[:, KV_WIDTH:2 * KV_WIDTH] = dv_all[:CHUNK, :]
        dq = _rope_apply(jnp.concatenate(dq_pairs, axis=1), tab, -1.0)
        dproj_ref[:, o:o + ATTN_WIDTH] = dq.astype(dproj_ref.dtype)
        dproj_ref[:, o + ATTN_WIDTH:o + ATTN_WIDTH + KV_WIDTH] = (
            _rope_apply(dk_cur, tab, -1.0).astype(dproj_ref.dtype))
        dproj_ref[:, o + ATTN_WIDTH + KV_WIDTH:o + ATTN_WIDTH + 2 * KV_WIDTH] = dv_cur.astype(dproj_ref.dtype)

    rev = lambda i: steps - 1 - i
    before = lambda i: jnp.maximum(per * rev(i) - 1, 0)
    slot = lambda shape: pl.BlockSpec((None,) + shape, lambda i, d: (d[0],) + (0,) * len(shape))
    return _hosted_call(
        body, comm, name="mixer_bwd", grid=(steps,), n_prefetch=1,
        out_shape=[jax.ShapeDtypeStruct((seq, IN_PROJ_WIDTH), MXU_DTYPE),
                   jax.ShapeDtypeStruct((N_DEV, GMLP_GROUPS, CHUNK, CHUNK), F32),
                   jax.ShapeDtypeStruct((N_DEV, CHUNK, LANES), F32),
                   jax.ShapeDtypeStruct((N_DEV, N_Q_HEADS, LANES), F32)],
        in_specs=[pl.BlockSpec((CHUNK * per, IN_PROJ_WIDTH), lambda i, d: (rev(i), 0)),
                  pl.BlockSpec((CHUNK, 2 * KV_WIDTH), lambda i, d: (before(i), kv_col)),
                  pl.BlockSpec((CHUNK * per, 3 * LANES), lambda i, d: (rev(i), 0)),
                  pl.BlockSpec((CHUNK, 3 * LANES), lambda i, d: (before(i), 0)),
                  pl.BlockSpec((CHUNK * per, D_MODEL), lambda i, d: (rev(i), 0)),
                  _full((GMLP_GROUPS, CHUNK, CHUNK)), _full((GMLP_GROUPS, CHUNK, CHUNK)),
                  _full((CHUNK, GMLP_WIDTH)), _full((N_Q_HEADS, LANES))],
        out_specs=[pl.BlockSpec((CHUNK * per, IN_PROJ_WIDTH), lambda i, d: (rev(i), 0)),
                   slot((GMLP_GROUPS, CHUNK, CHUNK)), slot((CHUNK, LANES)), slot((N_Q_HEADS, LANES))],
        scratch_shapes=[pltpu.VMEM((CHUNK, 2 * KV_WIDTH), F32)],
        semantics=("arbitrary",),
    )(dev_idx, proj, proj, rope_tab, rope_tab, dcat, w_spatial, w_spatial_t, bias_full, sink_rows)


def _in_proj_bwd_kernel(x, dx1, dproj, vecs, chip_idx, w_in_gathered, w_in_local, comm=None):
    seq = x.shape[0]
    tm = 512

    def body(chip_ref, x_ref, dx1_ref, dp_ref, v_ref, wg_ref, wl_ref, gx_ref, sums_ref, w_ref, sems):
        @pl.when(pl.program_id(0) == 0)
        def _():
            for wait in _load_chip_blocks(chip_ref, wg_ref, wl_ref, _w_in_rows(w_ref), sems):
                wait()
            sums_ref[...] = jnp.zeros_like(sums_ref)

        g_mix, scale1 = v_ref[0:1, :], v_ref[2:3, :]
        dh = _dot(dp_ref[...], w_ref[...])
        xv = x_ref[...]
        rstd = lax.rsqrt(_mean_last(xv * xv) + EPS)
        xh = xv * rstd
        dn1 = dh * (1.0 + scale1)
        dxh = dn1 * g_mix
        gx_ref[...] = dx1_ref[...] + rstd * (dxh - xh * _mean_last(dxh * xh))
        sums_ref[0:1, :] += _rowsum(dh)
        sums_ref[1:2, :] += _rowsum(dh * (xh * g_mix))
        sums_ref[2:3, :] += _rowsum(dn1 * xh)

    tok = lambda w: pl.BlockSpec((tm, w), lambda i, chip: (i, 0))
    return _hosted_call(
        body, comm, name="in_proj_bwd", grid=(seq // tm,), n_prefetch=1,
        out_shape=[jax.ShapeDtypeStruct((seq, D_MODEL), F32), jax.ShapeDtypeStruct((8, D_MODEL), F32)],
        in_specs=[tok(D_MODEL), tok(D_MODEL), tok(IN_PROJ_WIDTH), _full((8, D_MODEL)), _any(), _any()],
        out_specs=[tok(D_MODEL), _full((8, D_MODEL))],
        scratch_shapes=[pltpu.VMEM((IN_PROJ_WIDTH, D_MODEL), MXU_DTYPE), pltpu.SemaphoreType.DMA((N_CHIPS,))],
        semantics=("arbitrary",),
    )(chip_idx, x, dx1, dproj, vecs, w_in_gathered, w_in_local)


class _GradTiles(NamedTuple):
    tm: int
    tn: int
    n_tiles: int
    chips_per_tile: int
    a_index: Callable
    b_index: Callable


def _weight_grad_kernel(a, b, c_idx, name, tiles, comm=None):
    seq = a.shape[0]
    tk = min(seq, 4096)
    nk = seq // tk
    tm, tn, n_tiles, per = tiles.tm, tiles.tn, tiles.n_tiles, tiles.chips_per_tile
    rows = tm // per

    def half(phase, c):
        return phase * c[0] + (1 - phase) * (1 - c[0])

    def body(c_ref, a_ref, b_ref, o_ref, acc, stage, landed, send_sems, recv_sems):
        del c_ref
        phase, t, kk = pl.program_id(0), pl.program_id(1), pl.program_id(2)
        x, y, c, _ = _mesh_place()

        def copy(tile):
            return pltpu.make_async_remote_copy(
                src_ref=stage.at[tile], dst_ref=landed.at[tile], send_sem=send_sems.at[tile],
                recv_sem=recv_sems.at[tile], device_id=(x, y, 1 - c), device_id_type=MESH)

        @pl.when(kk == 0)
        def _():
            acc[...] = jnp.zeros_like(acc)

        acc[...] += _dot_tn(a_ref[...], b_ref[...])

        @pl.when((kk == nk - 1) & (phase == 0))
        def _():
            stage[t] = acc[...].astype(stage.dtype)
            copy(t).start()

        @pl.when((kk == nk - 1) & (phase == 1))
        def _():
            copy(t).wait_recv()
            total = acc[...] + landed[t].astype(F32)
            for q in range(per):
                o_ref[q] = total[rows * q:rows * (q + 1)].astype(o_ref.dtype)

        @pl.when((kk == nk - 1) & (phase == 1) & (t == n_tiles - 1))
        def _():
            for tile in range(n_tiles):
                copy(tile).wait_send()

    out = _hosted_call(
        body, comm, name=name, grid=(2, n_tiles, nk), n_prefetch=1,
        out_shape=[jax.ShapeDtypeStruct((n_tiles * per, rows, tn), GRAD_COMM_DTYPE)],
        in_specs=[pl.BlockSpec((tk, tm), lambda p, t, k, c: (k, tiles.a_index(t, half(p, c)))),
                  pl.BlockSpec((tk, tn), lambda p, t, k, c: (k, tiles.b_index(t, half(p, c))))],
        out_specs=[pl.BlockSpec((per, rows, tn), lambda p, t, k, c: (p * t, 0, 0))],
        scratch_shapes=[pltpu.VMEM((tm, tn), F32), pltpu.VMEM((n_tiles, tm, tn), GRAD_COMM_DTYPE),
                        pltpu.VMEM((n_tiles, tm, tn), GRAD_COMM_DTYPE),
                        pltpu.SemaphoreType.DMA((n_tiles,)), pltpu.SemaphoreType.DMA((n_tiles,))],
        semantics=("arbitrary", "arbitrary", "arbitrary"),
    )(c_idx, a, b)
    return out[0] if comm is None else out


def _row_tile(rows, most=256, sublanes=16):
    return max(t for t in range(sublanes, most + 1, sublanes) if rows % t == 0)


def _adam_update(w, g, m, v):
    m_new = ADAM_B1 * m + (1.0 - ADAM_B1) * g
    v_new = ADAM_B2 * v + (1.0 - ADAM_B2) * (g * g)
    m_hat = m_new / (1.0 - ADAM_B1 ** ADAM_STEP)
    v_hat = v_new / (1.0 - ADAM_B2 ** ADAM_STEP)
    delta = -ADAM_LR * (m_hat / (jnp.sqrt(v_hat) + ADAM_EPS) + ADAM_WD * w)
    return delta, m_new, v_new


def _sum_chips_kernel(own, others, place, name):
    _, r, n = own.shape
    tr = _row_tile(r)

    def body(place_ref, own_ref, oth_ref, o_ref):
        del place_ref
        acc = own_ref[...].astype(F32)
        for k in range(N_CHIPS - 1):
            acc = acc + oth_ref[k].astype(F32)
        o_ref[...] = acc

    return pl.pallas_call(
        body, name=name, out_shape=jax.ShapeDtypeStruct((2, r, n), F32),
        grid_spec=pltpu.PrefetchScalarGridSpec(
            num_scalar_prefetch=1, grid=(r // tr,),
            in_specs=[pl.BlockSpec((None, tr, n), lambda i, p: (p[0], i, 0)),
                      pl.BlockSpec((N_CHIPS - 1, tr, n), lambda i, p: (0, i, 0))],
            out_specs=pl.BlockSpec((None, tr, n), lambda i, p: (p[1], i, 0))),
        compiler_params=_params("parallel"),
    )(place, own, others)


def _adam_kernel(w, g, m, v, name):
    r, n = w.shape
    by_columns = g.shape[1] == r
    tr, tn = _row_tile(g.shape[1], most=512), g.shape[2]

    def body(w_ref, g_ref, m_ref, v_ref, g_out, d_ref, mo_ref, vo_ref):
        gv = g_ref[...]
        g_out[...] = gv
        d_ref[...], mo_ref[...], vo_ref[...] = _adam_update(w_ref[...], gv, m_ref[...], v_ref[...])

    steps = g.shape[1] // tr
    spec = pl.BlockSpec((tr, tn), (lambda h, i: (i, h)) if by_columns else (lambda h, i: (h * steps + i, 0)))
    return pl.pallas_call(
        body, name=name, grid=(2, steps), out_shape=[jax.ShapeDtypeStruct((r, n), F32)] * 4,
        in_specs=[spec, pl.BlockSpec((None, tr, tn), lambda h, i: (h, i, 0)), spec, spec], out_specs=[spec] * 4,
        compiler_params=_params("parallel", "parallel"),
    )(w, g, m, v)


SMALL_PARAMS = ("b_ada", "g_mix", "g_ffn", "g_final", "b_spatial", "sinks", "w_spatial")


def _small_update_kernel(gathered, params):
    shapes = [params[nm][0].shape for nm in SMALL_PARAMS]

    def body(*refs):
        g_refs, refs = refs[:5], refs[5:]
        p_refs, refs = refs[:3 * len(SMALL_PARAMS)], refs[3 * len(SMALL_PARAMS):]
        loss_ref, o_refs = refs[0], refs[1:]

        def total(ref):
            acc = ref[0]
            for k in range(1, N_DEV):
                acc = acc + ref[k]
            return acc

        s1, s2, db, ds, dw = (total(r) for r in g_refs)
        loss_ref[...] = jnp.broadcast_to(s2[6:7, 0:1], loss_ref.shape)
        grads = {"b_ada": [s1[0:1], s1[1:2], s2[5:6], s2[0:1], s2[1:2], s2[2:3]], "g_mix": [s1[2:3]],
                 "g_ffn": [s2[3:4]], "g_final": [s2[4:5]], "b_spatial": [db.T[0:GMLP_GROUPS]],
                 "w_spatial": [dw]}
        lane = lax.broadcasted_iota(jnp.int32, (1, LANES), 1)
        sink_row = jnp.zeros((1, LANES), F32)
        for h in range(N_Q_HEADS):
            sink_row = sink_row + jnp.where(lane == h, ds[h:h + 1, :], 0.0)
        grads["sinks"] = [sink_row[:, 0:N_Q_HEADS]]
        for i, nm in enumerate(SMALL_PARAMS):
            w_ref, m_ref, v_ref = p_refs[3 * i:3 * i + 3]
            outs = o_refs[4 * i:4 * i + 4]
            width = grads[nm][0].shape[1]
            for k, g in enumerate(grads[nm]):
                cols = slice(width * k, width * (k + 1))
                upd = _adam_update(w_ref[:, cols], g, m_ref[:, cols], v_ref[:, cols])
                for o_ref, val in zip(outs, (g,) + upd):
                    o_ref[:, cols] = val

    flat = [a for nm in SMALL_PARAMS for a in params[nm]]
    out_shape = [jax.ShapeDtypeStruct((8, LANES), F32)]
    out_shape += [jax.ShapeDtypeStruct(s, F32) for s in shapes for _ in range(4)]
    outs = pl.pallas_call(
        body, name="small_update", grid=(1,), out_shape=out_shape,
        in_specs=[_full(g.shape) for g in gathered] + [_full(a.shape) for a in flat],
        out_specs=[_full(s.shape) for s in out_shape],
        compiler_params=_params("arbitrary"),
    )(*gathered, *flat)
    return {nm: outs[1 + 4 * i:5 + 4 * i] for i, nm in enumerate(SMALL_PARAMS)}, outs[0]


def _ada_update_kernel(act_t, dmod, w, m, v):
    r, n = w.shape
    tr = 256

    def body(a_ref, d_ref, w_ref, m_ref, v_ref, g_ref, dl_ref, mo_ref, vo_ref):
        g = _dot(a_ref[...], d_ref[...])
        g_ref[...] = g
        dl_ref[...], mo_ref[...], vo_ref[...] = _adam_update(w_ref[...], g, m_ref[...], v_ref[...])

    spec = pl.BlockSpec((tr, n), lambda i: (i, 0))
    return pl.pallas_call(
        body, name="ada_update", grid=(r // tr,), out_shape=[jax.ShapeDtypeStruct((r, n), F32)] * 4,
        in_specs=[pl.BlockSpec((tr, N_DEV), lambda i: (i, 0)), _full((N_DEV, n)), spec, spec, spec],
        out_specs=[spec] * 4, compiler_params=_params("parallel"),
    )(act_t, dmod, w, m, v)


def kernel(x, c, positions, w_ada, b_ada, g_mix, w_in, w_spatial, b_spatial, sinks, w_out, g_ffn, w_ff1, w_ff2, g_final, loss_target, m_w_ada, m_b_ada, m_g_mix, m_w_in, m_w_spatial, m_b_spatial, m_sinks, m_w_out, m_g_ffn, m_w_ff1, m_w_ff2, m_g_final, v_w_ada, v_b_ada, v_g_mix, v_w_in, v_w_spatial, v_b_spatial, v_sinks, v_w_out, v_g_ffn, v_w_ff1, v_w_ff2, v_g_final):
    xi, yi, ci = lax.axis_index("x"), lax.axis_index("y"), lax.axis_index("c")
    chip = 2 * xi + yi
    dev = 2 * chip + ci
    seq = x.shape[1]
    x2, tgt = x[0], loss_target[0]
    ada_cols = w_ada.shape[2]

    big = {"w_in": tuple(a[0].T for a in (w_in, m_w_in, v_w_in)),
           "w_out": (w_out[0], m_w_out[0], v_w_out[0]), "w_ff1": (w_ff1[0], m_w_ff1[0], v_w_ff1[0]),
           "w_ff2": (w_ff2[0], m_w_ff2[0], v_w_ff2[0])}

    def halves(nm):
        r, n = big[nm][0].shape
        return big[nm][0].astype(WEIGHT_COMM_DTYPE).reshape(2, r // 2, n)

    chip_idx = chip.reshape(1).astype(jnp.int32)
    w_in_local = halves("w_in")
    trunk_weights = ["w_out", "w_ff1", "w_ff2"]
    shards = [halves(nm) for nm in trunk_weights]
    w_in_started = _gather_start([w_in_local], c, "gather_w_in_start")
    trunk_started = _gather_start(shards, w_in_started[-1], "gather_trunk_weights_start")

    c_all = _all_gather8([c + trunk_started[-1][0:1, 0:1]], "gather_c")[0].reshape(N_DEV, D_MODEL)
    b_shard = lax.dynamic_slice(b_ada, (0, chip * ada_cols), (1, ada_cols))
    mod_part, act = _mod_kernel(c_all, w_ada[0], b_shard)
    w_in_g, = _gather_wait(w_in_started, mod_part, "gather_w_in_wait")
    mod_all, w_in_g = _all_gather8([mod_part], "gather_mod", forward=[w_in_g])
    w_in_g, w_in_local = w_in_g.reshape(N_CHIPS, W_IN_BLOCK, D_MODEL), w_in_local.reshape(W_IN_BLOCK, D_MODEL)
    mod_me = lax.dynamic_index_in_dim(mod_all[0::2], dev, axis=1, keepdims=False)
    mod_me = mod_me.reshape(N_MOD, D_MODEL)
    shift1, scale1, gate1, shift2, scale2, gate2 = (mod_me[k:k + 1] for k in range(N_MOD))

    zeros_row = jnp.zeros((1, D_MODEL), F32)
    vecs1 = jnp.concatenate([g_mix, shift1, scale1] + [zeros_row] * 5, axis=0)
    vecs2 = jnp.concatenate([gate1, shift2, scale2, gate2, g_ffn, g_final.reshape(1, D_MODEL)]
                            + [zeros_row] * 2, axis=0)
    bias_full = jnp.repeat(b_spatial[0].T, HEAD_DIM, axis=1)
    sink_rows = jnp.broadcast_to(sinks[0][:, None], (N_Q_HEADS, LANES))
    inv_freq = ROPE_THETA ** (-jnp.arange(0, ROT_DIM, 2, dtype=F32) / ROT_DIM)
    rope_tab = _rope_lane_tables(*_rope_angle_kernel(positions, inv_freq.reshape(ROT_DIM // 2, 1)))

    proj, hb = _in_proj_kernel(x2, vecs1, chip_idx, w_in_g, w_in_local)
    cat, = _mixer_fwd_kernel(proj, rope_tab, w_spatial[0], bias_full, sink_rows)
    staged = _gather_forward(_gather_wait(trunk_started, cat, "gather_trunk_weights_wait"), "gather_forward")
    dx1, dcat, dmix, h2b, rb, dab, dffb, sums2 = _trunk_kernel(
        x2, tgt, cat, vecs2, chip_idx,
        [g.reshape((N_CHIPS,) + big[nm][0].shape) for nm, g in zip(trunk_weights, staged)],
        [s.reshape(big[nm][0].shape) for nm, s in zip(trunk_weights, shards)])

    c_idx = ci.reshape(1).astype(jnp.int32)
    place = jnp.stack([chip, ci]).astype(jnp.int32)
    half_d = D_MODEL // 2
    cs_ff2 = _weight_grad_kernel(rb, dffb, c_idx, "dw_ff2",
                                 _GradTiles(D_MODEL, half_d, N_CHIPS, 1, lambda t, h: t, lambda t, h: h))
    cs_ff1, sc_ff2 = _weight_grad_kernel(
        h2b, dab, c_idx, "dw_ff1",
        _GradTiles(D_MODEL, half_d, N_CHIPS, 1, lambda t, h: 0, lambda t, h: 2 * t + h),
        comm=_scatter_job([cs_ff2]))
    cs_out = _weight_grad_kernel(cat, dmix, c_idx, "dw_out",
                                 _GradTiles(D_MODEL, half_d, 1, N_CHIPS, lambda t, h: 0, lambda t, h: h))
    dproj, dw_spatial, db_lanes, dsink_rows, sc_ff1, sc_out = _mixer_bwd_kernel(
        proj, rope_tab, dcat, w_spatial[0], w_spatial[0].transpose(0, 2, 1), bias_full, sink_rows,
        dev.reshape(1).astype(jnp.int32), comm=_scatter_job([cs_ff1, cs_out]))
    totals = [_sum_chips_kernel(own, oth, place, "grad_sum_" + nm)
              for nm, own, oth in (("w_out", cs_out, sc_out), ("w_ff1", cs_ff1, sc_ff1), ("w_ff2", cs_ff2, sc_ff2))]
    small_slots = [db_lanes, dsink_rows, dw_spatial.reshape(N_DEV, GMLP_GROUPS * CHUNK, CHUNK)]
    cs_in, *rode = _weight_grad_kernel(
        dproj, hb, c_idx, "dw_in",
        _GradTiles(2 * W_IN_BLOCK, half_d, N_CHIPS // 2, 2, lambda t, h: t, lambda t, h: h),
        comm=_merge_in_place(_gather_job(small_slots), _share_job(totals)))
    small_stage1, shared = rode[:len(small_slots)], rode[len(small_slots):]
    grad_x, sums1 = _in_proj_bwd_kernel(x2, dx1, dproj, vecs1, chip_idx, w_in_g, w_in_local)
    *gathered, sc_in = _all_gather8([sums1, sums2], "gather_small", forward=small_stage1,
                                    riders=[_scatter_job([cs_in])])
    total_in = _sum_chips_kernel(cs_in, sc_in, place, "grad_sum_w_in")
    shared = list(_sibling_share([total_in], "grad_share_w_in")) + list(shared)
    names = ["w_in", "w_out", "w_ff1", "w_ff2"]
    big_out = {}
    for nm, g in zip(names, shared):
        w, m, v = big[nm]
        outs = _adam_kernel(w, g, m, v, "adam_" + nm)
        big_out[nm] = tuple((t.T if nm == "w_in" else t)[None] for t in outs)

    small = {"b_ada": (b_ada, m_b_ada, v_b_ada), "g_mix": (g_mix, m_g_mix, v_g_mix),
             "g_ffn": (g_ffn, m_g_ffn, v_g_ffn), "g_final": (g_final, m_g_final, v_g_final),
             "b_spatial": (b_spatial, m_b_spatial, v_b_spatial), "sinks": (sinks, m_sinks, v_sinks),
             "w_spatial": (w_spatial, m_w_spatial, v_w_spatial)}
    flat_shape = {"g_final": (1, D_MODEL), "b_spatial": (GMLP_GROUPS, CHUNK), "w_spatial": (GMLP_GROUPS * CHUNK, CHUNK)}
    small_out, loss_tile = _small_update_kernel(
        gathered, {nm: tuple(a.reshape(flat_shape.get(nm, a.shape)) for a in small[nm]) for nm in small})
    small_out = {nm: [o.reshape(small[nm][0].shape) for o in small_out[nm]] for nm in small}
    loss = loss_tile[0, 0]

    g1, g2 = gathered[0], gathered[1]
    dmod_all = jnp.concatenate([g1[:, 0], g1[:, 1], g2[:, 5], g2[:, 0], g2[:, 1], g2[:, 2]], axis=1)
    dmod_cols = lax.dynamic_slice(dmod_all, (0, chip * ada_cols), (N_DEV, ada_cols))
    ada = _ada_update_kernel(act.T, dmod_cols, w_ada[0], m_w_ada[0], v_w_ada[0])
    big_out["w_ada"] = tuple(t[None] for t in ada)

    order = ["w_ada", "b_ada", "g_mix", "w_in", "w_spatial", "b_spatial", "sinks", "w_out", "g_ffn",
             "w_ff1", "w_ff2", "g_final"]

    def leaf(nm, k):
        return big_out[nm][k] if nm in big_out else small_out[nm][k]

    outs = [loss, grad_x[None]]
    for k in range(4):
        outs += [leaf(nm, k) for nm in order]
    return tuple(outs)
```

```python
import math
from typing import Callable, NamedTuple

import jax
import jax.numpy as jnp
from jax import lax
from jax.experimental import pallas as pl
from jax.experimental.pallas import tpu as pltpu

F32 = jnp.float32
MXU_DTYPE = jnp.bfloat16
WEIGHT_COMM_DTYPE = jnp.bfloat16
GRAD_COMM_DTYPE = jnp.bfloat16

D_MODEL = 1024
D_FF = 4096
HEAD_DIM = 64
GMLP_GROUPS = 8
GMLP_WIDTH = 512
CHUNK = 128
N_Q_HEADS = 8
N_KV_HEADS = 2
ATTN_WIDTH = 512
KV_WIDTH = 128
ROT_DIM = 16
ROPE_THETA = 500000.0
IN_PROJ_WIDTH = 1792
N_MOD = 6
EPS = 1e-5
N_CHIPS = 4
N_DEV = 8
LANES = 128
W_IN_BLOCK = IN_PROJ_WIDTH // N_CHIPS

ADAM_LR = 0.001
ADAM_B1 = 0.9
ADAM_B2 = 0.999
ADAM_EPS = 1e-08
ADAM_WD = 0.01
ADAM_STEP = 10

VMEM_LIMIT_BYTES = 58 * 1024 * 1024
MESH = pl.DeviceIdType.MESH


def _params(*semantics):
    return pltpu.CompilerParams(dimension_semantics=semantics, vmem_limit_bytes=VMEM_LIMIT_BYTES)


def _dot(a, b):
    return jnp.dot(a.astype(MXU_DTYPE), b.astype(MXU_DTYPE), preferred_element_type=F32)


def _dot_nt(a, b):
    return lax.dot_general(a.astype(MXU_DTYPE), b.astype(MXU_DTYPE), (((1,), (1,)), ((), ())),
                           preferred_element_type=F32)


def _dot_tn(a, b):
    return lax.dot_general(a.astype(MXU_DTYPE), b.astype(MXU_DTYPE), (((0,), (0,)), ((), ())),
                           preferred_element_type=F32)


def _full(shape):
    return pl.BlockSpec(shape, lambda *_: (0,) * len(shape))


def _any():
    return pl.BlockSpec(memory_space=pl.ANY)


def _rowsum(v):
    return jnp.sum(v, axis=0, keepdims=True)


def _mean_last(v):
    return jnp.mean(v, axis=-1, keepdims=True)


class _Comm(NamedTuple):
    operands: tuple
    out_shapes: tuple
    n_sems: int
    make: Callable
    in_place: int = 0


def _hosted_call(body, comm, *, name, grid, in_specs, out_shape, out_specs, scratch_shapes=(), semantics,
                 n_prefetch=0):
    if comm is None:
        return pl.pallas_call(
            body, name=name, out_shape=out_shape, compiler_params=_params(*semantics),
            grid_spec=pltpu.PrefetchScalarGridSpec(
                num_scalar_prefetch=n_prefetch, grid=grid, in_specs=in_specs, out_specs=out_specs,
                scratch_shapes=list(scratch_shapes)))
    n_in, n_out, n_scr = len(in_specs), len(out_shape), len(scratch_shapes)
    k_in, k_out = len(comm.operands), len(comm.out_shapes)

    def hosted(*refs):
        prefetched, refs = refs[:n_prefetch], refs[n_prefetch:]
        ins, refs = refs[:n_in], refs[n_in:]
        c_ins, refs = refs[:k_in], refs[k_in:]
        outs, refs = refs[:n_out], refs[n_out:]
        c_outs, refs = refs[:k_out], refs[k_out:]
        scratch, (send_sems, recv_sems) = refs[:n_scr], refs[n_scr:]
        first, last = None, None
        for d, size in enumerate(grid):
            at_start, at_end = pl.program_id(d) == 0, pl.program_id(d) == size - 1
            first = at_start if first is None else first & at_start
            last = at_end if last is None else last & at_end

        @pl.when(first)
        def _():
            for cp in comm.make(c_ins, c_outs, send_sems, recv_sems)[0]:
                cp.start()

        body(*prefetched, *ins, *outs, *scratch)

        @pl.when(last)
        def _():
            for wait in comm.make(c_ins, c_outs, send_sems, recv_sems)[1]:
                wait()

    aliases = {n_prefetch + n_in + i: n_out + i for i in range(comm.in_place)}
    call = pl.pallas_call(
        hosted, name=name, out_shape=list(out_shape) + list(comm.out_shapes),
        compiler_params=_params(*semantics), input_output_aliases=aliases,
        grid_spec=pltpu.PrefetchScalarGridSpec(
            num_scalar_prefetch=n_prefetch, grid=grid, in_specs=list(in_specs) + [_any()] * k_in,
            out_specs=list(out_specs) + [_any()] * k_out,
            scratch_shapes=list(scratch_shapes) + [pltpu.SemaphoreType.DMA((comm.n_sems,)),
                                                    pltpu.SemaphoreType.DMA((comm.n_sems,))]))
    return lambda *args: call(*args, *comm.operands)


class _Shifted:
    def __init__(self, base, offset):
        self.base, self.offset = base, offset

    @property
    def at(self):
        return self

    def __getitem__(self, k):
        return self.base.at[self.offset + k]


def _merge_in_place(*jobs):
    assert all(j.in_place == len(j.operands) == len(j.out_shapes) for j in jobs)

    def make(ins, outs, send_sems, recv_sems):
        starts, waits, at, sem = [], [], 0, 0
        for j in jobs:
            n = len(j.operands)
            s, w = j.make(ins[at:at + n], outs[at:at + n], _Shifted(send_sems, sem), _Shifted(recv_sems, sem))
            starts, waits, at, sem = starts + s, waits + w, at + n, sem + j.n_sems
        return starts, waits

    operands = tuple(a for j in jobs for a in j.operands)
    return _Comm(operands, tuple(s for j in jobs for s in j.out_shapes), sum(j.n_sems for j in jobs), make,
                 in_place=len(operands))


def _mesh_place():
    x, y, c = lax.axis_index("x"), lax.axis_index("y"), lax.axis_index("c")
    return x, y, c, [(1 - x, y), (x, 1 - y), (1 - x, 1 - y)]


def _gather_job(bufs):
    per = 4

    def make(ins, outs, send_sems, recv_sems):
        del ins
        x, y, c, chips = _mesh_place()
        starts, waits = [], []
        for a, out in enumerate(outs):
            mine = src = out.at[4 * x + 2 * y + c]
            to = [(x, y, 1 - c)] + [(px, py, c) for px, py in chips]
            sends = [pltpu.make_async_remote_copy(
                src_ref=src, dst_ref=mine, send_sem=send_sems.at[per * a + k],
                recv_sem=recv_sems.at[per * a + k], device_id=dev, device_id_type=MESH)
                for k, dev in enumerate(to)]
            recvs = [pltpu.make_async_remote_copy(
                src_ref=src, dst_ref=out.at[4 * px + 2 * py + pc], send_sem=send_sems.at[per * a + k],
                recv_sem=recv_sems.at[per * a + k], device_id=(px, py, pc), device_id_type=MESH)
                for k, (px, py, pc) in enumerate(to)]
            starts += sends
            waits += [s.wait_send for s in sends] + [r.wait_recv for r in recvs]
        return starts, waits

    shapes = tuple(jax.ShapeDtypeStruct(b.shape, b.dtype) for b in bufs)
    return _Comm(tuple(bufs), shapes, per * len(bufs), make, in_place=len(bufs))


def _slots(x, y, c):
    return 4 * x + 2 * y + c, 4 * (1 - x) + 2 * y + c, 4 * x + 2 * (1 - y) + c, 4 * (1 - x) + 2 * (1 - y) + c


def _gather2d_first(halves):
    per = 2

    def make(ins, outs, send_sems, recv_sems):
        x, y, c, _ = _mesh_place()
        me, xn, yn, _ = _slots(x, y, c)
        starts, waits = [], []
        for a, (src, out) in enumerate(zip(ins, outs)):
            blk = src.at[c]
            rows = blk.shape[0] // 2
            upper, lower = pl.ds(0, rows), pl.ds(rows, rows)

            def copy(k, src_ref, dst_ref, dev, a=a):
                return pltpu.make_async_remote_copy(
                    src_ref=src_ref, dst_ref=dst_ref, send_sem=send_sems.at[per * a + k],
                    recv_sem=recv_sems.at[per * a + k], device_id=dev, device_id_type=MESH)

            sends = [copy(0, blk.at[upper], out.at[me, upper], (1 - x, y, c)),
                     copy(1, blk.at[lower], out.at[me, lower], (x, 1 - y, c))]
            recvs = [copy(0, blk.at[upper], out.at[xn, upper], (1 - x, y, c)),
                     copy(1, blk.at[lower], out.at[yn, lower], (x, 1 - y, c))]
            starts += sends
            waits += [s.wait_send for s in sends] + [r.wait_recv for r in recvs]
        return starts, waits

    shapes = tuple(jax.ShapeDtypeStruct((N_DEV,) + h.shape[1:], h.dtype) for h in halves)
    return _Comm(tuple(halves), shapes, per * len(halves), make)


def _gather2d_second(bufs, halves):
    per = 4
    n_arr = len(bufs)

    def make(ins, outs, send_sems, recv_sems):
        x, y, c, _ = _mesh_place()
        me, xn, yn, dg = _slots(x, y, c)
        starts, waits = [], []
        for a, buf in enumerate(outs):
            own = ins[n_arr + a].at[c]
            rows = buf.shape[1] // 2
            upper, lower = pl.ds(0, rows), pl.ds(rows, rows)
            plan = [(own.at[upper], me, upper, (x, 1 - y, c), yn), (buf.at[xn, upper], xn, upper, (x, 1 - y, c), dg),
                    (own.at[lower], me, lower, (1 - x, y, c), xn), (buf.at[yn, lower], yn, lower, (1 - x, y, c), dg)]
            for k, (src, slot, part, dev, landing) in enumerate(plan):
                sems = dict(send_sem=send_sems.at[per * a + k], recv_sem=recv_sems.at[per * a + k],
                            device_id=dev, device_id_type=MESH)
                send = pltpu.make_async_remote_copy(src_ref=src, dst_ref=buf.at[slot, part], **sems)
                arrival = pltpu.make_async_remote_copy(src_ref=src, dst_ref=buf.at[landing, part], **sems)
                starts.append(send)
                waits += [send.wait_send, arrival.wait_recv]
        return starts, waits

    shapes = tuple(jax.ShapeDtypeStruct(b.shape, b.dtype) for b in bufs)
    return _Comm(tuple(bufs) + tuple(halves), shapes, per * n_arr, make, in_place=n_arr)


def _gather_forward(bufs, name):
    n_arr = len(bufs)

    def body(*refs):
        outs = refs[n_arr:2 * n_arr]
        send_sems, recv_sems = refs[2 * n_arr:]
        x, y, c, chips = _mesh_place()
        sends, recvs = [], []
        for a, buf in enumerate(outs):
            for j, (px, py) in enumerate(chips):
                mine, theirs = buf.at[4 * px + 2 * py + c], buf.at[4 * px + 2 * py + 1 - c]
                sems = dict(send_sem=send_sems.at[3 * a + j], recv_sem=recv_sems.at[3 * a + j],
                            device_id=(x, y, 1 - c), device_id_type=MESH)
                sends.append(pltpu.make_async_remote_copy(src_ref=mine, dst_ref=mine, **sems))
                recvs.append(pltpu.make_async_remote_copy(src_ref=mine, dst_ref=theirs, **sems))
        for cp in sends:
            cp.start()
        for s, r in zip(sends, recvs):
            s.wait_send()
            r.wait_recv()

    return pl.pallas_call(
        body, name=name, out_shape=[jax.ShapeDtypeStruct(b.shape, b.dtype) for b in bufs],
        in_specs=[_any()] * n_arr, out_specs=[_any()] * n_arr,
        input_output_aliases={a: a for a in range(n_arr)},
        scratch_shapes=[pltpu.SemaphoreType.DMA((3 * n_arr,)), pltpu.SemaphoreType.DMA((3 * n_arr,))],
    )(*bufs)


def _scatter_job(chip_sums):
    def make(ins, outs, send_sems, recv_sems):
        x, y, c, chips = _mesh_place()
        copies = [pltpu.make_async_remote_copy(
            src_ref=src.at[2 * px + py], dst_ref=out.at[j], send_sem=send_sems.at[3 * a + j],
            recv_sem=recv_sems.at[3 * a + j], device_id=(px, py, c), device_id_type=MESH)
            for a, (src, out) in enumerate(zip(ins, outs)) for j, (px, py) in enumerate(chips)]
        return copies, [cp.wait for cp in copies]

    return _Comm(tuple(chip_sums), tuple(jax.ShapeDtypeStruct((3,) + s.shape[1:], s.dtype) for s in chip_sums),
                 3 * len(chip_sums), make)


def _all_gather8(blocks, name, split=False, forward=(), riders=(), skip_own=()):
    n_arr, n_fwd = len(blocks), len(forward)
    splits = list(split) if isinstance(split, (list, tuple)) else [split] * n_arr
    own_slots = [a not in skip_own for a in range(n_arr)]
    rider_in = sum(len(r.operands) for r in riders)
    rider_out = sum(len(r.out_shapes) for r in riders)

    def body(*refs):
        x_refs, refs = refs[:n_arr], refs[n_arr + n_fwd:]
        r_ins, refs = refs[:rider_in], refs[rider_in:]
        out_refs, refs = refs[:n_arr], refs[n_arr:]
        fwd_refs, refs = refs[:n_fwd], refs[n_fwd:]
        r_outs, refs = refs[:rider_out], refs[rider_out:]
        (send_sems, recv_sems, local_sems), rider_sems = refs[:3], refs[3:]
        x, y, c, chips = _mesh_place()
        me, sibling = (x, y, c), (x, y, 1 - c)
        rider_waits, i0, o0 = [], 0, 0
        for n, job in enumerate(riders):
            k_in, k_out = len(job.operands), len(job.out_shapes)
            starts, waits = job.make(r_ins[i0:i0 + k_in], r_outs[o0:o0 + k_out],
                                     rider_sems[2 * n], rider_sems[2 * n + 1])
            for cp in starts:
                cp.start()
            rider_waits += waits
            i0, o0 = i0 + k_in, o0 + k_out
        passing = []
        for f, buf in enumerate(fwd_refs):
            for j, (px, py) in enumerate(chips):
                mine, theirs = buf.at[4 * px + 2 * py + c], buf.at[4 * px + 2 * py + 1 - c]
                sems = dict(send_sem=send_sems.at[7 * n_arr + 3 * f + j], recv_sem=recv_sems.at[7 * n_arr + 3 * f + j],
                            device_id=sibling, device_id_type=MESH)
                passing.append((pltpu.make_async_remote_copy(src_ref=mine, dst_ref=mine, **sems),
                                pltpu.make_async_remote_copy(src_ref=mine, dst_ref=theirs, **sems)))
        for send, _ in passing:
            send.start()
        arrays = []
        for a, (x_ref, out_ref) in enumerate(zip(x_refs, out_refs)):
            src_mine = x_ref.at[c] if splits[a] else x_ref

            def copy(k, blk, to, src=None, a=a, out_ref=out_ref):
                dst = out_ref.at[4 * blk[0] + 2 * blk[1] + blk[2]]
                return pltpu.make_async_remote_copy(
                    src_ref=dst if src is None else src, dst_ref=dst,
                    send_sem=send_sems.at[7 * a + k], recv_sem=recv_sems.at[7 * a + k],
                    device_id=to, device_id_type=MESH)

            mine = pltpu.make_async_copy(src_mine, out_ref.at[4 * x + 2 * y + c], local_sems.at[a])
            first = [copy(0, me, sibling, src=src_mine)] if own_slots[a] else []
            first += [copy(1 + j, me, (*chip, c), src=src_mine) for j, chip in enumerate(chips)]
            for cp in first + ([mine] if own_slots[a] else []):
                cp.start()
            arrays.append((copy, mine, first, own_slots[a]))
        sent = []
        for copy, mine, first, own in arrays:
            passed = [copy(4 + j, (*chip, c), sibling) for j, chip in enumerate(chips)]
            for j, chip in enumerate(chips):
                copy(1 + j, (*chip, c), me).wait_recv()
                passed[j].start()
            sent += first + passed
        for copy, mine, first, own in arrays:
            if own:
                copy(0, sibling, me).wait_recv()
                mine.wait()
            for j, chip in enumerate(chips):
                copy(4 + j, (*chip, 1 - c), me).wait_recv()
        for cp in sent:
            cp.wait_send()
        for send, arrival in passing:
            send.wait_send()
            arrival.wait_recv()
        for wait in rider_waits:
            wait()

    n_sems = 7 * n_arr + 3 * n_fwd
    rider_operands = [a for r in riders for a in r.operands]
    rider_shapes = [s for r in riders for s in r.out_shapes]
    return pl.pallas_call(
        body, name=name,
        out_shape=[jax.ShapeDtypeStruct((N_DEV,) + tuple(b.shape[1:] if s else b.shape), b.dtype)
                   for b, s in zip(blocks, splits)]
        + [jax.ShapeDtypeStruct(f.shape, f.dtype) for f in forward] + rider_shapes,
        in_specs=[_any()] * (n_arr + n_fwd + rider_in), out_specs=[_any()] * (n_arr + n_fwd + rider_out),
        input_output_aliases={n_arr + f: n_arr + f for f in range(n_fwd)},
        scratch_shapes=[pltpu.SemaphoreType.DMA((n_sems,)), pltpu.SemaphoreType.DMA((n_sems,)),
                        pltpu.SemaphoreType.DMA((n_arr,))]
        + [pltpu.SemaphoreType.DMA((r.n_sems,)) for r in riders for _ in range(2)],
    )(*blocks, *forward, *rider_operands)


def _share_job(bufs):
    def make(ins, outs, send_sems, recv_sems):
        del ins
        x, y, c, _ = _mesh_place()
        sems = lambda a: dict(send_sem=send_sems.at[a], recv_sem=recv_sems.at[a],
                              device_id=(x, y, 1 - c), device_id_type=MESH)
        sends = [pltpu.make_async_remote_copy(src_ref=o.at[c], dst_ref=o.at[c], **sems(a)) for a, o in enumerate(outs)]
        arrivals = [pltpu.make_async_remote_copy(src_ref=o.at[c], dst_ref=o.at[1 - c], **sems(a))
                    for a, o in enumerate(outs)]
        return sends, [s.wait_send for s in sends] + [r.wait_recv for r in arrivals]

    shapes = tuple(jax.ShapeDtypeStruct(b.shape, b.dtype) for b in bufs)
    return _Comm(tuple(bufs), shapes, len(bufs), make, in_place=len(bufs))


def _sibling_share(bufs, name):
    n_arr = len(bufs)

    def body(*refs):
        out_refs = refs[n_arr:2 * n_arr]
        send_sems, recv_sems = refs[2 * n_arr:]
        x, y, c = lax.axis_index("x"), lax.axis_index("y"), lax.axis_index("c")
        copies = [pltpu.make_async_remote_copy(
            src_ref=out_refs[a].at[c], dst_ref=out_refs[a].at[c],
            send_sem=send_sems.at[a], recv_sem=recv_sems.at[a],
            device_id=(x, y, 1 - c), device_id_type=MESH) for a in range(n_arr)]
        for cp in copies:
            cp.start()
        for a in range(n_arr):
            pltpu.make_async_remote_copy(
                src_ref=out_refs[a].at[c], dst_ref=out_refs[a].at[1 - c],
                send_sem=send_sems.at[a], recv_sem=recv_sems.at[a],
                device_id=(x, y, 1 - c), device_id_type=MESH).wait()

    return pl.pallas_call(
        body, name=name,
        out_shape=[jax.ShapeDtypeStruct(b.shape, b.dtype) for b in bufs],
        in_specs=[_any()] * n_arr, out_specs=[_any()] * n_arr,
        input_output_aliases={a: a for a in range(n_arr)},
        scratch_shapes=[pltpu.SemaphoreType.DMA((n_arr,)), pltpu.SemaphoreType.DMA((n_arr,))],
    )(*bufs)


def _gelu_tanh(z):
    k = math.sqrt(2.0 / math.pi)
    t = jnp.tanh(k * (z + 0.044715 * (z * z * z)))
    return 0.5 * z * (1.0 + t), t


def _gelu_tanh_grad(z, t):
    k = math.sqrt(2.0 / math.pi)
    return 0.5 * (1.0 + t) + 0.5 * z * (1.0 - t * t) * (k * (1.0 + 3.0 * 0.044715 * (z * z)))


def _rope_angle_kernel(pos_row, invf_col):
    seq = pos_row.shape[1]

    def body(p_ref, f_ref, cos_ref, sin_ref):
        ang = p_ref[...].astype(F32) * f_ref[...]
        cos_ref[...] = jnp.cos(ang)
        sin_ref[...] = jnp.sin(ang)

    return pl.pallas_call(
        body, name="rope_angles", grid=(1,), out_shape=[jax.ShapeDtypeStruct((ROT_DIM // 2, seq), F32)] * 2,
        in_specs=[_full((1, seq)), _full((ROT_DIM // 2, 1))], out_specs=[_full((ROT_DIM // 2, seq))] * 2,
        compiler_params=_params("arbitrary"),
    )(pos_row, invf_col)


def _rope_lane_tables(cos, sin):
    cos_t, sin_t = cos.T, sin.T
    seq, half = cos_t.shape
    ones = jnp.ones((seq, HEAD_DIM - ROT_DIM), F32)
    c64 = jnp.concatenate([cos_t, cos_t, ones], axis=1)
    s1 = jnp.concatenate([sin_t, jnp.zeros((seq, HEAD_DIM - half), F32)], axis=1)
    s2 = jnp.concatenate([jnp.zeros((seq, half), F32), sin_t, jnp.zeros((seq, HEAD_DIM - ROT_DIM), F32)], axis=1)
    return jnp.concatenate([jnp.tile(t, (1, LANES // HEAD_DIM)) for t in (c64, s1, s2)], axis=1)


def _rope_apply(t, tab, sign):
    reps = t.shape[1] // LANES
    c_tab, s1, s2 = (jnp.tile(tab[:, LANES * k:LANES * (k + 1)], (1, reps)) if reps > 1
                     else tab[:, LANES * k:LANES * (k + 1)] for k in range(3))
    half = ROT_DIM // 2
    up = pltpu.roll(t, t.shape[1] - half, 1)
    down = pltpu.roll(t, half, 1)
    return t * c_tab + sign * (down * s2 - up * s1)


def _lane_masks(shape):
    lane = lax.broadcasted_iota(jnp.int32, shape, 1)
    return lane < HEAD_DIM, lane >= HEAD_DIM


HEADS_PER_GROUP = N_Q_HEADS // N_KV_HEADS
ATTN_SCALE = 1.0 / math.sqrt(HEAD_DIM)


def _attn_bias_t(first_block):
    kj = lax.broadcasted_iota(jnp.int32, (2 * CHUNK, CHUNK), 0)
    qi = lax.broadcasted_iota(jnp.int32, (2 * CHUNK, CHUNK), 1)
    ok = (kj > qi) & (kj <= qi + CHUNK)
    if first_block is not None:
        ok = ok & (jnp.logical_not(first_block) | (kj >= CHUNK))
    return jnp.tile(jnp.where(ok, 0.0, -jnp.inf), (1, HEADS_PER_GROUP))


def _group_rows(x, g, lo, hi):
    rows = []
    for r in range(HEADS_PER_GROUP):
        h = HEADS_PER_GROUP * g + r
        pair = x[:, LANES * (h // 2):LANES * (h // 2 + 1)]
        rows.append(jnp.where(hi if h % 2 else lo, pair, 0.0))
    return jnp.concatenate(rows, axis=0)


def _pairs_from_rows(rows, lo):
    return [jnp.where(lo, rows[2 * CHUNK * k:2 * CHUNK * k + CHUNK], rows[2 * CHUNK * k + CHUNK:2 * CHUNK * (k + 1)])
            for k in range(HEADS_PER_GROUP // 2)]


def _group_dup(a, b, g, lo2):
    return jnp.where(lo2, a, b) if g == 0 else jnp.where(lo2, b, a)


def _sink_row(sink_ref, g):
    return jnp.concatenate([sink_ref[HEADS_PER_GROUP * g + r:HEADS_PER_GROUP * g + r + 1, :]
                            for r in range(HEADS_PER_GROUP)], axis=1)


def _attn_probs_t(k_dup, q_rows, bias_t, sink_row):
    s_t = _dot_nt(k_dup, q_rows) * ATTN_SCALE + bias_t
    m = jnp.maximum(jnp.max(s_t, axis=0, keepdims=True), sink_row)
    p = jnp.exp(s_t - m)
    e_sink = jnp.exp(sink_row - m)
    inv = 1.0 / (jnp.sum(p, axis=0, keepdims=True) + e_sink)
    return p * inv, e_sink * inv


def _sgu_forward_pair(wm, vp, j):
    lo, hi = _lane_masks(vp.shape)
    lhs = jnp.concatenate([wm[2 * j], wm[2 * j + 1]], axis=1)
    rhs = jnp.concatenate([jnp.where(lo, vp, 0.0), jnp.where(hi, vp, 0.0)], axis=0)
    return _dot(lhs, rhs)


def _masked_spatial(w_ref):
    t = lax.broadcasted_iota(jnp.int32, (CHUNK, CHUNK), 0)
    s = lax.broadcasted_iota(jnp.int32, (CHUNK, CHUNK), 1)
    tril = s <= t
    return [jnp.where(tril, w_ref[g], 0.0) for g in range(GMLP_GROUPS)], tril, s >= t


def _mod_kernel(c_all, w_shard, b_shard, comm=None):
    n = w_shard.shape[1]
    tn = 512

    def body(c_ref, w_ref, b_ref, mod_ref, act_ref):
        cv = c_ref[...]
        act = cv * (1.0 / (1.0 + jnp.exp(-cv)))
        act_ref[...] = act
        mod_ref[...] = _dot(act, w_ref[...]) + b_ref[...]

    return _hosted_call(
        body, comm, name="ada_mod", grid=(n // tn,),
        out_shape=[jax.ShapeDtypeStruct((N_DEV, n), F32), jax.ShapeDtypeStruct((N_DEV, D_MODEL), F32)],
        in_specs=[_full((N_DEV, D_MODEL)), pl.BlockSpec((D_MODEL, tn), lambda i: (0, i)),
                  pl.BlockSpec((1, tn), lambda i: (0, i))],
        out_specs=[pl.BlockSpec((N_DEV, tn), lambda i: (0, i)), _full((N_DEV, D_MODEL))],
        semantics=("arbitrary",),
    )(c_all, w_shard, b_shard)


def _load_chip_blocks(chip_ref, gathered, local, dsts, sems, first_sem=0):
    for k, dst in enumerate(dsts):
        @pl.when(chip_ref[0] == k)
        def _():
            pltpu.make_async_copy(local, dst, sems.at[first_sem + k]).start()

        @pl.when(chip_ref[0] != k)
        def _():
            pltpu.make_async_copy(gathered.at[k], dst, sems.at[first_sem + k]).start()
    return [pltpu.make_async_copy(local, dst, sems.at[first_sem + k]).wait for k, dst in enumerate(dsts)]


def _in_proj_kernel(x, vecs, w_in_t, comm=None):
    seq = x.shape[0]
    tm = 512

    def body(x_ref, v_ref, w_ref, proj_ref, h_ref):
        xv = x_ref[...]
        rstd = lax.rsqrt(_mean_last(xv * xv) + EPS)
        n1 = (xv * rstd) * v_ref[0:1, :]
        h = n1 * (1.0 + v_ref[2:3, :]) + v_ref[1:2, :]
        hb = h.astype(MXU_DTYPE)
        h_ref[...] = hb
        proj_ref[...] = _dot_nt(hb, w_ref[...])

    return _hosted_call(
        body, comm, name="in_proj", grid=(seq // tm,),
        out_shape=[jax.ShapeDtypeStruct((seq, IN_PROJ_WIDTH), F32),
                   jax.ShapeDtypeStruct((seq, D_MODEL), MXU_DTYPE)],
        in_specs=[pl.BlockSpec((tm, D_MODEL), lambda i: (i, 0)), _full((8, D_MODEL)),
                  _full((IN_PROJ_WIDTH, D_MODEL))],
        out_specs=[pl.BlockSpec((tm, IN_PROJ_WIDTH), lambda i: (i, 0)),
                   pl.BlockSpec((tm, D_MODEL), lambda i: (i, 0))],
        semantics=("arbitrary",),
    )(x, vecs, w_in_t)


MIXER_BLOCKS_PER_STEP = 2
KV_START = 2 * GMLP_WIDTH + ATTN_WIDTH


def _mixer_fwd_kernel(proj, rope_tab, w_spatial, bias_full, sink_rows, comm=None):
    seq = proj.shape[0]
    per = MIXER_BLOCKS_PER_STEP
    steps = seq // (CHUNK * per)
    kv_col = KV_START // (2 * KV_WIDTH)

    def body(proj_ref, prev_ref, tab_ref, ptab_ref, w_ref, bias_ref, sink_ref, cat_ref):
        i = pl.program_id(0)
        wm, _, _ = _masked_spatial(w_ref)
        lo, hi = _lane_masks((CHUNK, LANES))
        lo2, _ = _lane_masks((2 * CHUNK, LANES))
        o = 2 * GMLP_WIDTH
        for s in range(per):
            rows, before = slice(CHUNK * s, CHUNK * (s + 1)), slice(CHUNK * (s - 1), CHUNK * s)
            for j in range(GMLP_GROUPS // 2):
                cols = slice(LANES * j, LANES * (j + 1))
                vcols = slice(GMLP_WIDTH + LANES * j, GMLP_WIDTH + LANES * (j + 1))
                u, _ = _gelu_tanh(proj_ref[rows, cols])
                vp, _ = _gelu_tanh(proj_ref[rows, vcols])
                sv = _sgu_forward_pair(wm, vp, j) + bias_ref[:, cols]
                cat_ref[rows, cols] = (u * sv).astype(cat_ref.dtype)
            tab = tab_ref[rows, :]
            if s == 0:
                prev_kv, prev_tab, first = prev_ref[...], ptab_ref[...], i == 0
            else:
                prev_kv, prev_tab, first = proj_ref[before, KV_START:KV_START + 2 * KV_WIDTH], tab_ref[before, :], None
            q_r = _rope_apply(proj_ref[rows, o:o + ATTN_WIDTH], tab, 1.0)
            k_cur = _rope_apply(proj_ref[rows, KV_START:KV_START + KV_WIDTH], tab, 1.0)
            k_prev = _rope_apply(prev_kv[:, 0:KV_WIDTH], prev_tab, 1.0)
            k_a = jnp.concatenate([k_prev, k_cur], axis=0)
            v_a = jnp.concatenate([prev_kv[:, KV_WIDTH:2 * KV_WIDTH],
                                   proj_ref[rows, KV_START + KV_WIDTH:KV_START + 2 * KV_WIDTH]], axis=0)
            k_b = pltpu.roll(k_a, HEAD_DIM, 1)
            v_b = pltpu.roll(v_a, HEAD_DIM, 1)
            bias_t = _attn_bias_t(first)
            for g in range(N_KV_HEADS):
                p_t, _ = _attn_probs_t(_group_dup(k_a, k_b, g, lo2), _group_rows(q_r, g, lo, hi), bias_t,
                                       _sink_row(sink_ref, g))
                o_t = _dot(_group_dup(v_a, v_b, g, lo2).T, p_t)
                for k, pair in enumerate(_pairs_from_rows(o_t.T, lo)):
                    c0 = GMLP_WIDTH + LANES * (2 * g + k)
                    cat_ref[rows, c0:c0 + LANES] = pair.astype(cat_ref.dtype)

    return _hosted_call(
        body, comm, name="mixer_fwd", grid=(steps,),
        out_shape=[jax.ShapeDtypeStruct((seq, D_MODEL), MXU_DTYPE)],
        in_specs=[pl.BlockSpec((CHUNK * per, IN_PROJ_WIDTH), lambda i: (i, 0)),
                  pl.BlockSpec((CHUNK, 2 * KV_WIDTH), lambda i: (jnp.maximum(per * i - 1, 0), kv_col)),
                  pl.BlockSpec((CHUNK * per, 3 * LANES), lambda i: (i, 0)),
                  pl.BlockSpec((CHUNK, 3 * LANES), lambda i: (jnp.maximum(per * i - 1, 0), 0)),
                  _full((GMLP_GROUPS, CHUNK, CHUNK)), _full((CHUNK, GMLP_WIDTH)),
                  _full((N_Q_HEADS, LANES))],
        out_specs=[pl.BlockSpec((CHUNK * per, D_MODEL), lambda i: (i, 0))],
        semantics=("arbitrary",),
    )(proj, proj, rope_tab, rope_tab, w_spatial, bias_full, sink_rows)


def _trunk_kernel(x, target, cat, vecs, chip_idx, gathered, local):
    seq = x.shape[0]
    tm = 256
    nj = D_FF // D_MODEL
    out_rows = D_MODEL // N_CHIPS

    def body(chip_ref, x_ref, t_ref, cat_ref, v_ref, g_out, g_w1, g_w2, l_out, l_w1, l_w2,
             dx1_ref, dcat_ref, dmix_ref, h2_ref, r_ref, da_ref, dff_ref, sums_ref,
             wout, w1, w2, a_scr, sem):
        i = pl.program_id(0)

        @pl.when(i == 0)
        def _():
            waits = _load_chip_blocks(chip_ref, g_out, l_out,
                                      [wout.at[pl.ds(out_rows * k, out_rows)] for k in range(N_CHIPS)], sem)
            waits += _load_chip_blocks(chip_ref, g_w1, l_w1, [w1.at[k] for k in range(N_CHIPS)], sem, N_CHIPS)
            waits += _load_chip_blocks(chip_ref, g_w2, l_w2, [w2.at[k] for k in range(N_CHIPS)], sem, 2 * N_CHIPS)
            for wait in waits:
                wait()
            sums_ref[...] = jnp.zeros_like(sums_ref)

        gate1, shift2, scale2 = v_ref[0:1, :], v_ref[1:2, :], v_ref[2:3, :]
        gate2, g_ffn, g_final = v_ref[3:4, :], v_ref[4:5, :], v_ref[5:6, :]

        mix = _dot(cat_ref[...], wout[...])
        x1 = x_ref[...] + gate1 * mix
        rstd2 = lax.rsqrt(_mean_last(x1 * x1) + EPS)
        xh2 = x1 * rstd2
        n2 = xh2 * g_ffn
        h2b = (n2 * (1.0 + scale2) + shift2).astype(MXU_DTYPE)
        h2_ref[...] = h2b
        ff = jnp.zeros((tm, D_MODEL), F32)
        for j in range(nj):
            a = _dot(h2b, w1[j])
            a_scr[j] = a
            relu = jnp.maximum(a, 0.0)
            rb = (relu * relu).astype(MXU_DTYPE)
            r_ref[:, D_MODEL * j:D_MODEL * (j + 1)] = rb
            ff = ff + _dot(rb, w2[j])
        x2 = x1 + gate2 * ff
        rstd3 = lax.rsqrt(_mean_last(x2 * x2) + EPS)
        xh3 = x2 * rstd3
        err = xh3 * g_final - t_ref[...]
        loss = 0.5 * _rowsum(_mean_last(err * err))
        dy = err * (1.0 / D_MODEL)
        dxh3 = dy * g_final
        dx2 = rstd3 * (dxh3 - xh3 * _mean_last(dxh3 * xh3))
        dffb = (dx2 * gate2).astype(MXU_DTYPE)
        dff_ref[...] = dffb
        dh2 = jnp.zeros((tm, D_MODEL), F32)
        for j in range(nj):
            dr = _dot_nt(dffb, w2[j])
            dab = (dr * (2.0 * jnp.maximum(a_scr[j], 0.0))).astype(MXU_DTYPE)
            da_ref[:, D_MODEL * j:D_MODEL * (j + 1)] = dab
            dh2 = dh2 + _dot_nt(dab, w1[j])
        dn2 = dh2 * (1.0 + scale2)
        dxh2 = dn2 * g_ffn
        dx1 = dx2 + rstd2 * (dxh2 - xh2 * _mean_last(dxh2 * xh2))
        dx1_ref[...] = dx1
        dmixb = (dx1 * gate1).astype(MXU_DTYPE)
        dmix_ref[...] = dmixb
        dcat_ref[...] = _dot_nt(dmixb, wout[...])

        sums_ref[0:1, :] += _rowsum(dh2)
        sums_ref[1:2, :] += _rowsum(dh2 * n2)
        sums_ref[2:3, :] += _rowsum(dx2 * ff)
        sums_ref[3:4, :] += _rowsum(dn2 * xh2)
        sums_ref[4:5, :] += _rowsum(dy * xh3)
        sums_ref[5:6, :] += _rowsum(dx1 * mix)
        sums_ref[6:7, :] += jnp.broadcast_to(loss, (1, D_MODEL))

    tok = lambda w: pl.BlockSpec((tm, w), lambda i, chip: (i, 0))
    return _hosted_call(
        body, None, name="trunk", grid=(seq // tm,), n_prefetch=1,
        out_shape=[jax.ShapeDtypeStruct((seq, D_MODEL), F32), jax.ShapeDtypeStruct((seq, D_MODEL), F32),
                   jax.ShapeDtypeStruct((seq, D_MODEL), MXU_DTYPE), jax.ShapeDtypeStruct((seq, D_MODEL), MXU_DTYPE),
                   jax.ShapeDtypeStruct((seq, D_FF), MXU_DTYPE), jax.ShapeDtypeStruct((seq, D_FF), MXU_DTYPE),
                   jax.ShapeDtypeStruct((seq, D_MODEL), MXU_DTYPE), jax.ShapeDtypeStruct((8, D_MODEL), F32)],
        in_specs=[tok(D_MODEL), tok(D_MODEL), tok(D_MODEL), _full((8, D_MODEL))] + [_any()] * 6,
        out_specs=[tok(D_MODEL), tok(D_MODEL), tok(D_MODEL), tok(D_MODEL), tok(D_FF), tok(D_FF), tok(D_MODEL),
                   _full((8, D_MODEL))],
        scratch_shapes=[pltpu.VMEM((D_MODEL, D_MODEL), MXU_DTYPE), pltpu.VMEM((nj, D_MODEL, D_MODEL), MXU_DTYPE),
                        pltpu.VMEM((nj, D_MODEL, D_MODEL), MXU_DTYPE), pltpu.VMEM((nj, tm, D_MODEL), F32),
                        pltpu.SemaphoreType.DMA((3 * N_CHIPS,))],
        semantics=("arbitrary",),
    )(chip_idx, x, target, cat, vecs, *gathered, *local)


def _mixer_bwd_kernel(proj, rope_tab, dcat, w_spatial, w_spatial_t, bias_full, sink_rows, dev_idx, comm=None):
    seq = proj.shape[0]
    per = MIXER_BLOCKS_PER_STEP
    steps = seq // (CHUNK * per)
    kv_col = KV_START // (2 * KV_WIDTH)

    def body(dev_ref, proj_ref, prev_ref, tab_ref, ptab_ref, dcat_ref, w_ref, wt_ref, bias_ref, sink_ref,
             dproj_ref, dw_ref, db_ref, dsink_ref, carry):
        del dev_ref
        step = pl.program_id(0)

        @pl.when(step == 0)
        def _():
            carry[...] = jnp.zeros_like(carry)
            dw_ref[...] = jnp.zeros_like(dw_ref)
            db_ref[...] = jnp.zeros_like(db_ref)
            dsink_ref[...] = jnp.zeros_like(dsink_ref)

        for s in reversed(range(per)):
            rows = pl.ds(CHUNK * s, CHUNK)
            if s == 0:
                before, before_tab, first = prev_ref, ptab_ref, step == steps - 1
            else:
                before = proj_ref.at[pl.ds(CHUNK * (s - 1), CHUNK), pl.ds(KV_START, 2 * KV_WIDTH)]
                before_tab, first = tab_ref.at[pl.ds(CHUNK * (s - 1), CHUNK)], None
            one_block(proj_ref.at[rows], before, tab_ref.at[rows], before_tab, dcat_ref.at[rows], w_ref, wt_ref,
                      bias_ref, sink_ref, dproj_ref.at[rows], dw_ref, db_ref, dsink_ref, carry, first)

    def one_block(proj_ref, prev_ref, tab_ref, ptab_ref, dcat_ref, w_ref, wt_ref, bias_ref, sink_ref,
                  dproj_ref, dw_ref, db_ref, dsink_ref, carry, first):
        wm, tril, triu = _masked_spatial(w_ref)
        lo, hi = _lane_masks((CHUNK, LANES))
        lane = lax.broadcasted_iota(jnp.int32, (CHUNK, LANES), 1)
        db = jnp.zeros((CHUNK, LANES), F32)
        for j in range(GMLP_GROUPS // 2):
            cols = slice(LANES * j, LANES * (j + 1))
            vcols = slice(GMLP_WIDTH + LANES * j, GMLP_WIDTH + LANES * (j + 1))
            zu, zv = proj_ref[:, cols], proj_ref[:, vcols]
            u, tu = _gelu_tanh(zu)
            vp, tv = _gelu_tanh(zv)
            sv = _sgu_forward_pair(wm, vp, j) + bias_ref[:, cols]
            dout = dcat_ref[:, cols]
            du = dout * sv
            dsv = dout * u
            dsv_lo, dsv_hi = jnp.where(lo, dsv, 0.0), jnp.where(hi, dsv, 0.0)
            lhs_t = jnp.concatenate([jnp.where(triu, wt_ref[2 * j], 0.0),
                                     jnp.where(triu, wt_ref[2 * j + 1], 0.0)], axis=1)
            dv = _dot(lhs_t, jnp.concatenate([dsv_lo, dsv_hi], axis=0))
            dw_ref[2 * j] += jnp.where(tril, _dot_nt(dsv_lo, vp), 0.0)
            dw_ref[2 * j + 1] += jnp.where(tril, _dot_nt(dsv_hi, vp), 0.0)
            db = db + (jnp.where(lane == 2 * j, jnp.sum(dsv_lo, axis=1, keepdims=True), 0.0)
                       + jnp.where(lane == 2 * j + 1, jnp.sum(dsv_hi, axis=1, keepdims=True), 0.0))
            dproj_ref[:, cols] = (du * _gelu_tanh_grad(zu, tu)).astype(dproj_ref.dtype)
            dproj_ref[:, vcols] = (dv * _gelu_tanh_grad(zv, tv)).astype(dproj_ref.dtype)
        db_ref[...] += db
        o = 2 * GMLP_WIDTH
        tab = tab_ref[...]
        q_r = _rope_apply(proj_ref[:, o:o + ATTN_WIDTH], tab, 1.0)
        k_cur = _rope_apply(proj_ref[:, o + ATTN_WIDTH:o + ATTN_WIDTH + KV_WIDTH], tab, 1.0)
        k_prev = _rope_apply(prev_ref[:, 0:KV_WIDTH], ptab_ref[...], 1.0)
        k_a = jnp.concatenate([k_prev, k_cur], axis=0)
        v_a = jnp.concatenate([prev_ref[:, KV_WIDTH:2 * KV_WIDTH],
                               proj_ref[:, o + ATTN_WIDTH + KV_WIDTH:o + ATTN_WIDTH + 2 * KV_WIDTH]], axis=0)
        k_b = pltpu.roll(k_a, HEAD_DIM, 1)
        v_b = pltpu.roll(v_a, HEAD_DIM, 1)
        bias_t = _attn_bias_t(first)
        lo2, _ = _lane_masks((2 * CHUNK, LANES))
        dout_b = dcat_ref[:, GMLP_WIDTH:GMLP_WIDTH + ATTN_WIDTH]
        dk_tot, dv_tot, dq_pairs = [], [], []
        for g in range(N_KV_HEADS):
            k_dup, v_dup = _group_dup(k_a, k_b, g, lo2), _group_dup(v_a, v_b, g, lo2)
            q_rows = _group_rows(q_r, g, lo, hi)
            do_rows = _group_rows(dout_b, g, lo, hi)
            p_t, p_sink = _attn_probs_t(k_dup, q_rows, bias_t, _sink_row(sink_ref, g))
            dp_t = _dot_nt(v_dup, do_rows)
            delta = jnp.sum(p_t * dp_t, axis=0, keepdims=True)
            ds_t = p_t * (dp_t - delta) * ATTN_SCALE
            dsink = -p_sink * delta
            for r in range(HEADS_PER_GROUP):
                h = HEADS_PER_GROUP * g + r
                dsink_ref[h:h + 1, :] += jnp.broadcast_to(
                    jnp.sum(dsink[:, LANES * r:LANES * (r + 1)], axis=1, keepdims=True), (1, LANES))
            dk_full = _dot(ds_t, q_rows)
            dv_full = _dot(p_t, do_rows)
            dk_tot.append(dk_full + pltpu.roll(dk_full, HEAD_DIM, 1))
            dv_tot.append(dv_full + pltpu.roll(dv_full, HEAD_DIM, 1))
            dq_t = _dot(k_dup.T, ds_t)
            dq_pairs += _pairs_from_rows(dq_t.T, lo)
        dk_all = jnp.where(lo2, dk_tot[0], dk_tot[1])
        dv_all = jnp.where(lo2, dv_tot[0], dv_tot[1])
        dk_cur = dk_all[CHUNK:, :] + carry[:, 0:KV_WIDTH]
        dv_cur = dv_all[CHUNK:, :] + carry[:, KV_WIDTH:2 * KV_WIDTH]
        carry[:, 0:KV_WIDTH] = dk_all[:CHUNK, :]
        carry[:, KV_WIDTH:2 * KV_WIDTH] = dv_all[:CHUNK, :]
        dq = _rope_apply(jnp.concatenate(dq_pairs, axis=1), tab, -1.0)
        dproj_ref[:, o:o + ATTN_WIDTH] = dq.astype(dproj_ref.dtype)
        dproj_ref[:, o + ATTN_WIDTH:o + ATTN_WIDTH + KV_WIDTH] = (
            _rope_apply(dk_cur, tab, -1.0).astype(dproj_ref.dtype))
        dproj_ref[:, o + ATTN_WIDTH + KV_WIDTH:o + ATTN_WIDTH + 2 * KV_WIDTH] = dv_cur.astype(dproj_ref.dtype)

    rev = lambda i: steps - 1 - i
    before = lambda i: jnp.maximum(per * rev(i) - 1, 0)
    slot = lambda shape: pl.BlockSpec((None,) + shape, lambda i, d: (d[0],) + (0,) * len(shape))
    return _hosted_call(
        body, comm, name="mixer_bwd", grid=(steps,), n_prefetch=1,
        out_shape=[jax.ShapeDtypeStruct((seq, IN_PROJ_WIDTH), MXU_DTYPE),
                   jax.ShapeDtypeStruct((N_DEV, GMLP_GROUPS, CHUNK, CHUNK), F32),
                   jax.ShapeDtypeStruct((N_DEV, CHUNK, LANES), F32),
                   jax.ShapeDtypeStruct((N_DEV, N_Q_HEADS, LANES), F32)],
        in_specs=[pl.BlockSpec((CHUNK * per, IN_PROJ_WIDTH), lambda i, d: (rev(i), 0)),
                  pl.BlockSpec((CHUNK, 2 * KV_WIDTH), lambda i, d: (before(i), kv_col)),
                  pl.BlockSpec((CHUNK * per, 3 * LANES), lambda i, d: (rev(i), 0)),
                  pl.BlockSpec((CHUNK, 3 * LANES), lambda i, d: (before(i), 0)),
                  pl.BlockSpec((CHUNK * per, D_MODEL), lambda i, d: (rev(i), 0)),
                  _full((GMLP_GROUPS, CHUNK, CHUNK)), _full((GMLP_GROUPS, CHUNK, CHUNK)),
                  _full((CHUNK, GMLP_WIDTH)), _full((N_Q_HEADS, LANES))],
        out_specs=[pl.BlockSpec((CHUNK * per, IN_PROJ_WIDTH), lambda i, d: (rev(i), 0)),
                   slot((GMLP_GROUPS, CHUNK, CHUNK)), slot((CHUNK, LANES)), slot((N_Q_HEADS, LANES))],
        scratch_shapes=[pltpu.VMEM((CHUNK, 2 * KV_WIDTH), F32)],
        semantics=("arbitrary",),
    )(dev_idx, proj, proj, rope_tab, rope_tab, dcat, w_spatial, w_spatial_t, bias_full, sink_rows)


def _in_proj_bwd_kernel(x, dx1, dproj, vecs, w_in_t, comm=None):
    seq = x.shape[0]
    tm = 512

    def body(x_ref, dx1_ref, dp_ref, v_ref, w_ref, gx_ref, sums_ref):
        @pl.when(pl.program_id(0) == 0)
        def _():
            sums_ref[...] = jnp.zeros_like(sums_ref)

        g_mix, scale1 = v_ref[0:1, :], v_ref[2:3, :]
        dh = _dot(dp_ref[...], w_ref[...])
        xv = x_ref[...]
        rstd = lax.rsqrt(_mean_last(xv * xv) + EPS)
        xh = xv * rstd
        dn1 = dh * (1.0 + scale1)
        dxh = dn1 * g_mix
        gx_ref[...] = dx1_ref[...] + rstd * (dxh - xh * _mean_last(dxh * xh))
        sums_ref[0:1, :] += _rowsum(dh)
        sums_ref[1:2, :] += _rowsum(dh * (xh * g_mix))
        sums_ref[2:3, :] += _rowsum(dn1 * xh)

    tok = lambda w: pl.BlockSpec((tm, w), lambda i: (i, 0))
    return _hosted_call(
        body, comm, name="in_proj_bwd", grid=(seq // tm,),
        out_shape=[jax.ShapeDtypeStruct((seq, D_MODEL), F32), jax.ShapeDtypeStruct((8, D_MODEL), F32)],
        in_specs=[tok(D_MODEL), tok(D_MODEL), tok(IN_PROJ_WIDTH), _full((8, D_MODEL)),
                  _full((IN_PROJ_WIDTH, D_MODEL))],
        out_specs=[tok(D_MODEL), _full((8, D_MODEL))],
        semantics=("arbitrary",),
    )(x, dx1, dproj, vecs, w_in_t)


class _GradTiles(NamedTuple):
    tm: int
    tn: int
    n_tiles: int
    chips_per_tile: int
    a_index: Callable
    b_index: Callable


def _weight_grad_kernel(a, b, c_idx, name, tiles, comm=None):
    seq = a.shape[0]
    tk = min(seq, 4096)
    nk = seq // tk
    tm, tn, n_tiles, per = tiles.tm, tiles.tn, tiles.n_tiles, tiles.chips_per_tile
    rows = tm // per

    def half(phase, c):
        return phase * c[0] + (1 - phase) * (1 - c[0])

    def body(c_ref, a_ref, b_ref, o_ref, acc, stage, landed, send_sems, recv_sems):
        del c_ref
        phase, t, kk = pl.program_id(0), pl.program_id(1), pl.program_id(2)
        x, y, c, _ = _mesh_place()

        def copy(tile):
            return pltpu.make_async_remote_copy(
                src_ref=stage.at[tile], dst_ref=landed.at[tile], send_sem=send_sems.at[tile],
                recv_sem=recv_sems.at[tile], device_id=(x, y, 1 - c), device_id_type=MESH)

        @pl.when(kk == 0)
        def _():
            acc[...] = jnp.zeros_like(acc)

        acc[...] += _dot_tn(a_ref[...], b_ref[...])

        @pl.when((kk == nk - 1) & (phase == 0))
        def _():
            stage[t] = acc[...].astype(stage.dtype)
            copy(t).start()

        @pl.when((kk == nk - 1) & (phase == 1))
        def _():
            copy(t).wait_recv()
            total = acc[...] + landed[t].astype(F32)
            for q in range(per):
                o_ref[q] = total[rows * q:rows * (q + 1)].astype(o_ref.dtype)

        @pl.when((kk == nk - 1) & (phase == 1) & (t == n_tiles - 1))
        def _():
            for tile in range(n_tiles):
                copy(tile).wait_send()

    out = _hosted_call(
        body, comm, name=name, grid=(2, n_tiles, nk), n_prefetch=1,
        out_shape=[jax.ShapeDtypeStruct((n_tiles * per, rows, tn), GRAD_COMM_DTYPE)],
        in_specs=[pl.BlockSpec((tk, tm), lambda p, t, k, c: (k, tiles.a_index(t, half(p, c)))),
                  pl.BlockSpec((tk, tn), lambda p, t, k, c: (k, tiles.b_index(t, half(p, c))))],
        out_specs=[pl.BlockSpec((per, rows, tn), lambda p, t, k, c: (p * t, 0, 0))],
        scratch_shapes=[pltpu.VMEM((tm, tn), F32), pltpu.VMEM((n_tiles, tm, tn), GRAD_COMM_DTYPE),
                        pltpu.VMEM((n_tiles, tm, tn), GRAD_COMM_DTYPE),
                        pltpu.SemaphoreType.DMA((n_tiles,)), pltpu.SemaphoreType.DMA((n_tiles,))],
        semantics=("arbitrary", "arbitrary", "arbitrary"),
    )(c_idx, a, b)
    return out[0] if comm is None else out


def _row_tile(rows, most=256, sublanes=16):
    return max(t for t in range(sublanes, most + 1, sublanes) if rows % t == 0)


def _adam_update(w, g, m, v):
    m_new = ADAM_B1 * m + (1.0 - ADAM_B1) * g
    v_new = ADAM_B2 * v + (1.0 - ADAM_B2) * (g * g)
    m_hat = m_new / (1.0 - ADAM_B1 ** ADAM_STEP)
    v_hat = v_new / (1.0 - ADAM_B2 ** ADAM_STEP)
    delta = -ADAM_LR * (m_hat / (jnp.sqrt(v_hat) + ADAM_EPS) + ADAM_WD * w)
    return delta, m_new, v_new


def _sum_chips_kernel(own, others, place, name):
    _, r, n = own.shape
    tr = _row_tile(r)

    def body(place_ref, own_ref, oth_ref, o_ref):
        del place_ref
        acc = own_ref[...].astype(F32)
        for k in range(N_CHIPS - 1):
            acc = acc + oth_ref[k].astype(F32)
        o_ref[...] = acc

    return pl.pallas_call(
        body, name=name, out_shape=jax.ShapeDtypeStruct((2, r, n), F32),
        grid_spec=pltpu.PrefetchScalarGridSpec(
            num_scalar_prefetch=1, grid=(r // tr,),
            in_specs=[pl.BlockSpec((None, tr, n), lambda i, p: (p[0], i, 0)),
                      pl.BlockSpec((N_CHIPS - 1, tr, n), lambda i, p: (0, i, 0))],
            out_specs=pl.BlockSpec((None, tr, n), lambda i, p: (p[1], i, 0))),
        compiler_params=_params("parallel"),
    )(place, own, others)


def _adam_kernel(w, g, m, v, name):
    r, n = w.shape
    by_columns = g.shape[1] == r
    tr, tn = _row_tile(g.shape[1], most=512), g.shape[2]

    def body(w_ref, g_ref, m_ref, v_ref, g_out, d_ref, mo_ref, vo_ref):
        gv = g_ref[...]
        g_out[...] = gv
        d_ref[...], mo_ref[...], vo_ref[...] = _adam_update(w_ref[...], gv, m_ref[...], v_ref[...])

    steps = g.shape[1] // tr
    spec = pl.BlockSpec((tr, tn), (lambda h, i: (i, h)) if by_columns else (lambda h, i: (h * steps + i, 0)))
    return pl.pallas_call(
        body, name=name, grid=(2, steps), out_shape=[jax.ShapeDtypeStruct((r, n), F32)] * 4,
        in_specs=[spec, pl.BlockSpec((None, tr, tn), lambda h, i: (h, i, 0)), spec, spec], out_specs=[spec] * 4,
        compiler_params=_params("parallel", "parallel"),
    )(w, g, m, v)


SMALL_PARAMS = ("b_ada", "g_mix", "g_ffn", "g_final", "b_spatial", "sinks", "w_spatial")


def _small_update_kernel(gathered, params):
    shapes = [params[nm][0].shape for nm in SMALL_PARAMS]

    def body(*refs):
        g_refs, refs = refs[:5], refs[5:]
        p_refs, refs = refs[:3 * len(SMALL_PARAMS)], refs[3 * len(SMALL_PARAMS):]
        loss_ref, o_refs = refs[0], refs[1:]

        def total(ref):
            acc = ref[0]
            for k in range(1, N_DEV):
                acc = acc + ref[k]
            return acc

        s1, s2, db, ds, dw = (total(r) for r in g_refs)
        loss_ref[...] = jnp.broadcast_to(s2[6:7, 0:1], loss_ref.shape)
        grads = {"b_ada": [s1[0:1], s1[1:2], s2[5:6], s2[0:1], s2[1:2], s2[2:3]], "g_mix": [s1[2:3]],
                 "g_ffn": [s2[3:4]], "g_final": [s2[4:5]], "b_spatial": [db.T[0:GMLP_GROUPS]],
                 "w_spatial": [dw]}
        lane = lax.broadcasted_iota(jnp.int32, (1, LANES), 1)
        sink_row = jnp.zeros((1, LANES), F32)
        for h in range(N_Q_HEADS):
            sink_row = sink_row + jnp.where(lane == h, ds[h:h + 1, :], 0.0)
        grads["sinks"] = [sink_row[:, 0:N_Q_HEADS]]
        for i, nm in enumerate(SMALL_PARAMS):
            w_ref, m_ref, v_ref = p_refs[3 * i:3 * i + 3]
            outs = o_refs[4 * i:4 * i + 4]
            width = grads[nm][0].shape[1]
            for k, g in enumerate(grads[nm]):
                cols = slice(width * k, width * (k + 1))
                upd = _adam_update(w_ref[:, cols], g, m_ref[:, cols], v_ref[:, cols])
                for o_ref, val in zip(outs, (g,) + upd):
                    o_ref[:, cols] = val

    flat = [a for nm in SMALL_PARAMS for a in params[nm]]
    out_shape = [jax.ShapeDtypeStruct((8, LANES), F32)]
    out_shape += [jax.ShapeDtypeStruct(s, F32) for s in shapes for _ in range(4)]
    outs = pl.pallas_call(
        body, name="small_update", grid=(1,), out_shape=out_shape,
        in_specs=[_full(g.shape) for g in gathered] + [_full(a.shape) for a in flat],
        out_specs=[_full(s.shape) for s in out_shape],
        compiler_params=_params("arbitrary"),
    )(*gathered, *flat)
    return {nm: outs[1 + 4 * i:5 + 4 * i] for i, nm in enumerate(SMALL_PARAMS)}, outs[0]


def _ada_update_kernel(act_t, dmod, w, m, v):
    r, n = w.shape
    tr = 256

    def body(a_ref, d_ref, w_ref, m_ref, v_ref, g_ref, dl_ref, mo_ref, vo_ref):
        g = _dot(a_ref[...], d_ref[...])
        g_ref[...] = g
        dl_ref[...], mo_ref[...], vo_ref[...] = _adam_update(w_ref[...], g, m_ref[...], v_ref[...])

    spec = pl.BlockSpec((tr, n), lambda i: (i, 0))
    return pl.pallas_call(
        body, name="ada_update", grid=(r // tr,), out_shape=[jax.ShapeDtypeStruct((r, n), F32)] * 4,
        in_specs=[pl.BlockSpec((tr, N_DEV), lambda i: (i, 0)), _full((N_DEV, n)), spec, spec, spec],
        out_specs=[spec] * 4, compiler_params=_params("parallel"),
    )(act_t, dmod, w, m, v)


def kernel(x, c, positions, w_ada, b_ada, g_mix, w_in, w_spatial, b_spatial, sinks, w_out, g_ffn, w_ff1, w_ff2, g_final, loss_target, m_w_ada, m_b_ada, m_g_mix, m_w_in, m_w_spatial, m_b_spatial, m_sinks, m_w_out, m_g_ffn, m_w_ff1, m_w_ff2, m_g_final, v_w_ada, v_b_ada, v_g_mix, v_w_in, v_w_spatial, v_b_spatial, v_sinks, v_w_out, v_g_ffn, v_w_ff1, v_w_ff2, v_g_final):
    xi, yi, ci = lax.axis_index("x"), lax.axis_index("y"), lax.axis_index("c")
    chip = 2 * xi + yi
    dev = 2 * chip + ci
    seq = x.shape[1]
    x2, tgt = x[0], loss_target[0]
    ada_cols = w_ada.shape[2]

    big = {"w_in": tuple(a[0].T for a in (w_in, m_w_in, v_w_in)),
           "w_out": (w_out[0], m_w_out[0], v_w_out[0]), "w_ff1": (w_ff1[0], m_w_ff1[0], v_w_ff1[0]),
           "w_ff2": (w_ff2[0], m_w_ff2[0], v_w_ff2[0])}

    def halves(nm):
        r, n = big[nm][0].shape
        return big[nm][0].astype(WEIGHT_COMM_DTYPE).reshape(2, r // 2, n)

    chip_idx = chip.reshape(1).astype(jnp.int32)
    c_all, w_in_t, g_out = _all_gather8([c, halves("w_in"), halves("w_out")], "gather_first",
                                        split=[False, True, True], skip_own=(2,))
    c_all, w_in_t = c_all.reshape(N_DEV, D_MODEL), w_in_t.reshape(IN_PROJ_WIDTH, D_MODEL)
    b_shard = lax.dynamic_slice(b_ada, (0, chip * ada_cols), (1, ada_cols))
    mod_part, act = _mod_kernel(c_all, w_ada[0], b_shard)
    mod_all, = _all_gather8([mod_part], "gather_mod")
    mod_me = lax.dynamic_index_in_dim(mod_all[0::2], dev, axis=1, keepdims=False)
    mod_me = mod_me.reshape(N_MOD, D_MODEL)
    shift1, scale1, gate1, shift2, scale2, gate2 = (mod_me[k:k + 1] for k in range(N_MOD))

    zeros_row = jnp.zeros((1, D_MODEL), F32)
    vecs1 = jnp.concatenate([g_mix, shift1, scale1] + [zeros_row] * 5, axis=0)
    vecs2 = jnp.concatenate([gate1, shift2, scale2, gate2, g_ffn, g_final.reshape(1, D_MODEL)]
                            + [zeros_row] * 2, axis=0)
    bias_full = jnp.repeat(b_spatial[0].T, HEAD_DIM, axis=1)
    sink_rows = jnp.broadcast_to(sinks[0][:, None], (N_Q_HEADS, LANES))
    inv_freq = ROPE_THETA ** (-jnp.arange(0, ROT_DIM, 2, dtype=F32) / ROT_DIM)
    rope_tab = _rope_lane_tables(*_rope_angle_kernel(positions, inv_freq.reshape(ROT_DIM // 2, 1)))

    trunk_weights = ["w_out", "w_ff1", "w_ff2"]
    shards = [halves(nm) for nm in trunk_weights]
    proj, hb, *staged = _in_proj_kernel(x2, vecs1, w_in_t, comm=_gather2d_first(shards[1:]))
    cat, *staged = _mixer_fwd_kernel(proj, rope_tab, w_spatial[0], bias_full, sink_rows,
                                     comm=_gather2d_second(staged, shards[1:]))
    staged = [g_out] + list(_gather_forward(staged, "gather_forward"))
    dx1, dcat, dmix, h2b, rb, dab, dffb, sums2 = _trunk_kernel(
        x2, tgt, cat, vecs2, chip_idx,
        [g.reshape((N_CHIPS,) + big[nm][0].shape) for nm, g in zip(trunk_weights, staged)],
        [s.reshape(big[nm][0].shape) for nm, s in zip(trunk_weights, shards)])

    c_idx = ci.reshape(1).astype(jnp.int32)
    place = jnp.stack([chip, ci]).astype(jnp.int32)
    half_d = D_MODEL // 2
    cs_ff2 = _weight_grad_kernel(rb, dffb, c_idx, "dw_ff2",
                                 _GradTiles(D_MODEL, half_d, N_CHIPS, 1, lambda t, h: t, lambda t, h: h))
    cs_ff1, sc_ff2 = _weight_grad_kernel(
        h2b, dab, c_idx, "dw_ff1",
        _GradTiles(D_MODEL, half_d, N_CHIPS, 1, lambda t, h: 0, lambda t, h: 2 * t + h),
        comm=_scatter_job([cs_ff2]))
    cs_out = _weight_grad_kernel(cat, dmix, c_idx, "dw_out",
                                 _GradTiles(D_MODEL, half_d, 1, N_CHIPS, lambda t, h: 0, lambda t, h: h))
    dproj, dw_spatial, db_lanes, dsink_rows, sc_ff1, sc_out = _mixer_bwd_kernel(
        proj, rope_tab, dcat, w_spatial[0], w_spatial[0].transpose(0, 2, 1), bias_full, sink_rows,
        dev.reshape(1).astype(jnp.int32), comm=_scatter_job([cs_ff1, cs_out]))
    totals = [_sum_chips_kernel(own, oth, place, "grad_sum_" + nm)
              for nm, own, oth in (("w_out", cs_out, sc_out), ("w_ff1", cs_ff1, sc_ff1), ("w_ff2", cs_ff2, sc_ff2))]
    small_slots = [db_lanes, dsink_rows, dw_spatial.reshape(N_DEV, GMLP_GROUPS * CHUNK, CHUNK)]
    cs_in, *rode = _weight_grad_kernel(
        dproj, hb, c_idx, "dw_in",
        _GradTiles(2 * W_IN_BLOCK, half_d, N_CHIPS // 2, 2, lambda t, h: t, lambda t, h: h),
        comm=_merge_in_place(_gather_job(small_slots), _share_job(totals)))
    small_stage1, shared = rode[:len(small_slots)], rode[len(small_slots):]
    grad_x, sums1 = _in_proj_bwd_kernel(x2, dx1, dproj, vecs1, w_in_t)
    *gathered, sc_in = _all_gather8([sums1, sums2], "gather_small", forward=small_stage1,
                                    riders=[_scatter_job([cs_in])])
    total_in = _sum_chips_kernel(cs_in, sc_in, place, "grad_sum_w_in")
    shared = list(_sibling_share([total_in], "grad_share_w_in")) + list(shared)
    names = ["w_in", "w_out", "w_ff1", "w_ff2"]
    big_out = {}
    for nm, g in zip(names, shared):
        w, m, v = big[nm]
        outs = _adam_kernel(w, g, m, v, "adam_" + nm)
        big_out[nm] = tuple((t.T if nm == "w_in" else t)[None] for t in outs)

    small = {"b_ada": (b_ada, m_b_ada, v_b_ada), "g_mix": (g_mix, m_g_mix, v_g_mix),
             "g_ffn": (g_ffn, m_g_ffn, v_g_ffn), "g_final": (g_final, m_g_final, v_g_final),
             "b_spatial": (b_spatial, m_b_spatial, v_b_spatial), "sinks": (sinks, m_sinks, v_sinks),
             "w_spatial": (w_spatial, m_w_spatial, v_w_spatial)}
    flat_shape = {"g_final": (1, D_MODEL), "b_spatial": (GMLP_GROUPS, CHUNK), "w_spatial": (GMLP_GROUPS * CHUNK, CHUNK)}
    small_out, loss_tile = _small_update_kernel(
        gathered, {nm: tuple(a.reshape(flat_shape.get(nm, a.shape)) for a in small[nm]) for nm in small})
    small_out = {nm: [o.reshape(small[nm][0].shape) for o in small_out[nm]] for nm in small}
    loss = loss_tile[0, 0]

    g1, g2 = gathered[0], gathered[1]
    dmod_all = jnp.concatenate([g1[:, 0], g1[:, 1], g2[:, 5], g2[:, 0], g2[:, 1], g2[:, 2]], axis=1)
    dmod_cols = lax.dynamic_slice(dmod_all, (0, chip * ada_cols), (N_DEV, ada_cols))
    ada = _ada_update_kernel(act.T, dmod_cols, w_ada[0], m_w_ada[0], v_w_ada[0])
    big_out["w_ada"] = tuple(t[None] for t in ada)

    order = ["w_ada", "b_ada", "g_mix", "w_in", "w_spatial", "b_spatial", "sinks", "w_out", "g_ffn",
             "w_ff1", "w_ff2", "g_final"]

    def leaf(nm, k):
        return big_out[nm][k] if nm in big_out else small_out[nm][k]

    outs = [loss, grad_x[None]]
    for k in range(4):
        outs += [leaf(nm, k) for nm in order]
    return tuple(outs)
```

```python
import math
from typing import Callable, NamedTuple

import jax
import jax.numpy as jnp
from jax import lax
from jax.experimental import pallas as pl
from jax.experimental.pallas import tpu as pltpu

F32 = jnp.float32
MXU_DTYPE = jnp.bfloat16
WEIGHT_COMM_DTYPE = jnp.bfloat16
GRAD_COMM_DTYPE = jnp.bfloat16

D_MODEL = 1024
D_FF = 4096
HEAD_DIM = 64
GMLP_GROUPS = 8
GMLP_WIDTH = 512
CHUNK = 128
N_Q_HEADS = 8
N_KV_HEADS = 2
ATTN_WIDTH = 512
KV_WIDTH = 128
ROT_DIM = 16
ROPE_THETA = 500000.0
IN_PROJ_WIDTH = 1792
N_MOD = 6
EPS = 1e-5
N_CHIPS = 4
N_DEV = 8
LANES = 128
W_IN_BLOCK = IN_PROJ_WIDTH // N_CHIPS

ADAM_LR = 0.001
ADAM_B1 = 0.9
ADAM_B2 = 0.999
ADAM_EPS = 1e-08
ADAM_WD = 0.01
ADAM_STEP = 10

VMEM_LIMIT_BYTES = 58 * 1024 * 1024
MESH = pl.DeviceIdType.MESH


def _params(*semantics):
    return pltpu.CompilerParams(dimension_semantics=semantics, vmem_limit_bytes=VMEM_LIMIT_BYTES)


def _dot(a, b):
    return jnp.dot(a.astype(MXU_DTYPE), b.astype(MXU_DTYPE), preferred_element_type=F32)


def _dot_nt(a, b):
    return lax.dot_general(a.astype(MXU_DTYPE), b.astype(MXU_DTYPE), (((1,), (1,)), ((), ())),
                           preferred_element_type=F32)


def _dot_tn(a, b):
    return lax.dot_general(a.astype(MXU_DTYPE), b.astype(MXU_DTYPE), (((0,), (0,)), ((), ())),
                           preferred_element_type=F32)


def _full(shape):
    return pl.BlockSpec(shape, lambda *_: (0,) * len(shape))


def _any():
    return pl.BlockSpec(memory_space=pl.ANY)


def _rowsum(v):
    return jnp.sum(v, axis=0, keepdims=True)


def _mean_last(v):
    return jnp.mean(v, axis=-1, keepdims=True)


class _Comm(NamedTuple):
    operands: tuple
    out_shapes: tuple
    n_sems: int
    make: Callable
    in_place: int = 0


def _hosted_call(body, comm, *, name, grid, in_specs, out_shape, out_specs, scratch_shapes=(), semantics,
                 n_prefetch=0):
    if comm is None:
        return pl.pallas_call(
            body, name=name, out_shape=out_shape, compiler_params=_params(*semantics),
            grid_spec=pltpu.PrefetchScalarGridSpec(
                num_scalar_prefetch=n_prefetch, grid=grid, in_specs=in_specs, out_specs=out_specs,
                scratch_shapes=list(scratch_shapes)))
    n_in, n_out, n_scr = len(in_specs), len(out_shape), len(scratch_shapes)
    k_in, k_out = len(comm.operands), len(comm.out_shapes)

    def hosted(*refs):
        prefetched, refs = refs[:n_prefetch], refs[n_prefetch:]
        ins, refs = refs[:n_in], refs[n_in:]
        c_ins, refs = refs[:k_in], refs[k_in:]
        outs, refs = refs[:n_out], refs[n_out:]
        c_outs, refs = refs[:k_out], refs[k_out:]
        scratch, (send_sems, recv_sems) = refs[:n_scr], refs[n_scr:]
        first, last = None, None
        for d, size in enumerate(grid):
            at_start, at_end = pl.program_id(d) == 0, pl.program_id(d) == size - 1
            first = at_start if first is None else first & at_start
            last = at_end if last is None else last & at_end

        @pl.when(first)
        def _():
            for cp in comm.make(c_ins, c_outs, send_sems, recv_sems)[0]:
                cp.start()

        body(*prefetched, *ins, *outs, *scratch)

        @pl.when(last)
        def _():
            for wait in comm.make(c_ins, c_outs, send_sems, recv_sems)[1]:
                wait()

    aliases = {n_prefetch + n_in + i: n_out + i for i in range(comm.in_place)}
    call = pl.pallas_call(
        hosted, name=name, out_shape=list(out_shape) + list(comm.out_shapes),
        compiler_params=_params(*semantics), input_output_aliases=aliases,
        grid_spec=pltpu.PrefetchScalarGridSpec(
            num_scalar_prefetch=n_prefetch, grid=grid, in_specs=list(in_specs) + [_any()] * k_in,
            out_specs=list(out_specs) + [_any()] * k_out,
            scratch_shapes=list(scratch_shapes) + [pltpu.SemaphoreType.DMA((comm.n_sems,)),
                                                    pltpu.SemaphoreType.DMA((comm.n_sems,))]))
    return lambda *args: call(*args, *comm.operands)


class _Shifted:
    def __init__(self, base, offset):
        self.base, self.offset = base, offset

    @property
    def at(self):
        return self

    def __getitem__(self, k):
        return self.base.at[self.offset + k]


def _merge_in_place(*jobs):
    assert all(j.in_place == len(j.operands) == len(j.out_shapes) for j in jobs)

    def make(ins, outs, send_sems, recv_sems):
        starts, waits, at, sem = [], [], 0, 0
        for j in jobs:
            n = len(j.operands)
            s, w = j.make(ins[at:at + n], outs[at:at + n], _Shifted(send_sems, sem), _Shifted(recv_sems, sem))
            starts, waits, at, sem = starts + s, waits + w, at + n, sem + j.n_sems
        return starts, waits

    operands = tuple(a for j in jobs for a in j.operands)
    return _Comm(operands, tuple(s for j in jobs for s in j.out_shapes), sum(j.n_sems for j in jobs), make,
                 in_place=len(operands))


def _mesh_place():
    x, y, c = lax.axis_index("x"), lax.axis_index("y"), lax.axis_index("c")
    return x, y, c, [(1 - x, y), (x, 1 - y), (1 - x, 1 - y)]


def _gather_job(bufs):
    per = 4

    def make(ins, outs, send_sems, recv_sems):
        del ins
        x, y, c, chips = _mesh_place()
        starts, waits = [], []
        for a, out in enumerate(outs):
            mine = src = out.at[4 * x + 2 * y + c]
            to = [(x, y, 1 - c)] + [(px, py, c) for px, py in chips]
            sends = [pltpu.make_async_remote_copy(
                src_ref=src, dst_ref=mine, send_sem=send_sems.at[per * a + k],
                recv_sem=recv_sems.at[per * a + k], device_id=dev, device_id_type=MESH)
                for k, dev in enumerate(to)]
            recvs = [pltpu.make_async_remote_copy(
                src_ref=src, dst_ref=out.at[4 * px + 2 * py + pc], send_sem=send_sems.at[per * a + k],
                recv_sem=recv_sems.at[per * a + k], device_id=(px, py, pc), device_id_type=MESH)
                for k, (px, py, pc) in enumerate(to)]
            starts += sends
            waits += [s.wait_send for s in sends] + [r.wait_recv for r in recvs]
        return starts, waits

    shapes = tuple(jax.ShapeDtypeStruct(b.shape, b.dtype) for b in bufs)
    return _Comm(tuple(bufs), shapes, per * len(bufs), make, in_place=len(bufs))


def _slots(x, y, c):
    return 4 * x + 2 * y + c, 4 * (1 - x) + 2 * y + c, 4 * x + 2 * (1 - y) + c, 4 * (1 - x) + 2 * (1 - y) + c


def _gather2d_first(halves):
    per = 2

    def make(ins, outs, send_sems, recv_sems):
        x, y, c, _ = _mesh_place()
        me, xn, yn, _ = _slots(x, y, c)
        starts, waits = [], []
        for a, (src, out) in enumerate(zip(ins, outs)):
            blk = src.at[c]
            rows = blk.shape[0] // 2
            upper, lower = pl.ds(0, rows), pl.ds(rows, rows)

            def copy(k, src_ref, dst_ref, dev, a=a):
                return pltpu.make_async_remote_copy(
                    src_ref=src_ref, dst_ref=dst_ref, send_sem=send_sems.at[per * a + k],
                    recv_sem=recv_sems.at[per * a + k], device_id=dev, device_id_type=MESH)

            sends = [copy(0, blk.at[upper], out.at[me, upper], (1 - x, y, c)),
                     copy(1, blk.at[lower], out.at[me, lower], (x, 1 - y, c))]
            recvs = [copy(0, blk.at[upper], out.at[xn, upper], (1 - x, y, c)),
                     copy(1, blk.at[lower], out.at[yn, lower], (x, 1 - y, c))]
            starts += sends
            waits += [s.wait_send for s in sends] + [r.wait_recv for r in recvs]
        return starts, waits

    shapes = tuple(jax.ShapeDtypeStruct((N_DEV,) + h.shape[1:], h.dtype) for h in halves)
    return _Comm(tuple(halves), shapes, per * len(halves), make)


def _gather2d_second(bufs, halves):
    per = 4
    n_arr = len(bufs)

    def make(ins, outs, send_sems, recv_sems):
        x, y, c, _ = _mesh_place()
        me, xn, yn, dg = _slots(x, y, c)
        starts, waits = [], []
        for a, buf in enumerate(outs):
            own = ins[n_arr + a].at[c]
            rows = buf.shape[1] // 2
            upper, lower = pl.ds(0, rows), pl.ds(rows, rows)
            plan = [(own.at[upper], me, upper, (x, 1 - y, c), yn), (buf.at[xn, upper], xn, upper, (x, 1 - y, c), dg),
                    (own.at[lower], me, lower, (1 - x, y, c), xn), (buf.at[yn, lower], yn, lower, (1 - x, y, c), dg)]
            for k, (src, slot, part, dev, landing) in enumerate(plan):
                sems = dict(send_sem=send_sems.at[per * a + k], recv_sem=recv_sems.at[per * a + k],
                            device_id=dev, device_id_type=MESH)
                send = pltpu.make_async_remote_copy(src_ref=src, dst_ref=buf.at[slot, part], **sems)
                arrival = pltpu.make_async_remote_copy(src_ref=src, dst_ref=buf.at[landing, part], **sems)
                starts.append(send)
                waits += [send.wait_send, arrival.wait_recv]
        return starts, waits

    shapes = tuple(jax.ShapeDtypeStruct(b.shape, b.dtype) for b in bufs)
    return _Comm(tuple(bufs) + tuple(halves), shapes, per * n_arr, make, in_place=n_arr)


def _gather_forward(bufs, name):
    n_arr = len(bufs)

    def body(*refs):
        outs = refs[n_arr:2 * n_arr]
        send_sems, recv_sems = refs[2 * n_arr:]
        x, y, c, chips = _mesh_place()
        sends, recvs = [], []
        for a, buf in enumerate(outs):
            for j, (px, py) in enumerate(chips):
                mine, theirs = buf.at[4 * px + 2 * py + c], buf.at[4 * px + 2 * py + 1 - c]
                sems = dict(send_sem=send_sems.at[3 * a + j], recv_sem=recv_sems.at[3 * a + j],
                            device_id=(x, y, 1 - c), device_id_type=MESH)
                sends.append(pltpu.make_async_remote_copy(src_ref=mine, dst_ref=mine, **sems))
                recvs.append(pltpu.make_async_remote_copy(src_ref=mine, dst_ref=theirs, **sems))
        for cp in sends:
            cp.start()
        for s, r in zip(sends, recvs):
            s.wait_send()
            r.wait_recv()

    return pl.pallas_call(
        body, name=name, out_shape=[jax.ShapeDtypeStruct(b.shape, b.dtype) for b in bufs],
        in_specs=[_any()] * n_arr, out_specs=[_any()] * n_arr,
        input_output_aliases={a: a for a in range(n_arr)},
        scratch_shapes=[pltpu.SemaphoreType.DMA((3 * n_arr,)), pltpu.SemaphoreType.DMA((3 * n_arr,))],
    )(*bufs)


def _scatter_job(chip_sums):
    def make(ins, outs, send_sems, recv_sems):
        x, y, c, chips = _mesh_place()
        copies = [pltpu.make_async_remote_copy(
            src_ref=src.at[2 * px + py], dst_ref=out.at[j], send_sem=send_sems.at[3 * a + j],
            recv_sem=recv_sems.at[3 * a + j], device_id=(px, py, c), device_id_type=MESH)
            for a, (src, out) in enumerate(zip(ins, outs)) for j, (px, py) in enumerate(chips)]
        return copies, [cp.wait for cp in copies]

    return _Comm(tuple(chip_sums), tuple(jax.ShapeDtypeStruct((3,) + s.shape[1:], s.dtype) for s in chip_sums),
                 3 * len(chip_sums), make)


def _all_gather8(blocks, name, split=False, forward=(), riders=(), skip_own=()):
    n_arr, n_fwd = len(blocks), len(forward)
    splits = list(split) if isinstance(split, (list, tuple)) else [split] * n_arr
    own_slots = [a not in skip_own for a in range(n_arr)]
    rider_in = sum(len(r.operands) for r in riders)
    rider_out = sum(len(r.out_shapes) for r in riders)

    def body(*refs):
        x_refs, refs = refs[:n_arr], refs[n_arr + n_fwd:]
        r_ins, refs = refs[:rider_in], refs[rider_in:]
        out_refs, refs = refs[:n_arr], refs[n_arr:]
        fwd_refs, refs = refs[:n_fwd], refs[n_fwd:]
        r_outs, refs = refs[:rider_out], refs[rider_out:]
        (send_sems, recv_sems, local_sems), rider_sems = refs[:3], refs[3:]
        x, y, c, chips = _mesh_place()
        me, sibling = (x, y, c), (x, y, 1 - c)
        rider_waits, i0, o0 = [], 0, 0
        for n, job in enumerate(riders):
            k_in, k_out = len(job.operands), len(job.out_shapes)
            starts, waits = job.make(r_ins[i0:i0 + k_in], r_outs[o0:o0 + k_out],
                                     rider_sems[2 * n], rider_sems[2 * n + 1])
            for cp in starts:
                cp.start()
            rider_waits += waits
            i0, o0 = i0 + k_in, o0 + k_out
        passing = []
        for f, buf in enumerate(fwd_refs):
            for j, (px, py) in enumerate(chips):
                mine, theirs = buf.at[4 * px + 2 * py + c], buf.at[4 * px + 2 * py + 1 - c]
                sems = dict(send_sem=send_sems.at[7 * n_arr + 3 * f + j], recv_sem=recv_sems.at[7 * n_arr + 3 * f + j],
                            device_id=sibling, device_id_type=MESH)
                passing.append((pltpu.make_async_remote_copy(src_ref=mine, dst_ref=mine, **sems),
                                pltpu.make_async_remote_copy(src_ref=mine, dst_ref=theirs, **sems)))
        for send, _ in passing:
            send.start()
        arrays = []
        for a, (x_ref, out_ref) in enumerate(zip(x_refs, out_refs)):
            src_mine = x_ref.at[c] if splits[a] else x_ref

            def copy(k, blk, to, src=None, a=a, out_ref=out_ref):
                dst = out_ref.at[4 * blk[0] + 2 * blk[1] + blk[2]]
                return pltpu.make_async_remote_copy(
                    src_ref=dst if src is None else src, dst_ref=dst,
                    send_sem=send_sems.at[7 * a + k], recv_sem=recv_sems.at[7 * a + k],
                    device_id=to, device_id_type=MESH)

            mine = pltpu.make_async_copy(src_mine, out_ref.at[4 * x + 2 * y + c], local_sems.at[a])
            first = [copy(0, me, sibling, src=src_mine)] if own_slots[a] else []
            first += [copy(1 + j, me, (*chip, c), src=src_mine) for j, chip in enumerate(chips)]
            for cp in first + ([mine] if own_slots[a] else []):
                cp.start()
            arrays.append((copy, mine, first, own_slots[a]))
        sent = []
        for copy, mine, first, own in arrays:
            passed = [copy(4 + j, (*chip, c), sibling) for j, chip in enumerate(chips)]
            for j, chip in enumerate(chips):
                copy(1 + j, (*chip, c), me).wait_recv()
                passed[j].start()
            sent += first + passed
        for copy, mine, first, own in arrays:
            if own:
                copy(0, sibling, me).wait_recv()
                mine.wait()
            for j, chip in enumerate(chips):
                copy(4 + j, (*chip, 1 - c), me).wait_recv()
        for cp in sent:
            cp.wait_send()
        for send, arrival in passing:
            send.wait_send()
            arrival.wait_recv()
        for wait in rider_waits:
            wait()

    n_sems = 7 * n_arr + 3 * n_fwd
    rider_operands = [a for r in riders for a in r.operands]
    rider_shapes = [s for r in riders for s in r.out_shapes]
    return pl.pallas_call(
        body, name=name,
        out_shape=[jax.ShapeDtypeStruct((N_DEV,) + tuple(b.shape[1:] if s else b.shape), b.dtype)
                   for b, s in zip(blocks, splits)]
        + [jax.ShapeDtypeStruct(f.shape, f.dtype) for f in forward] + rider_shapes,
        in_specs=[_any()] * (n_arr + n_fwd + rider_in), out_specs=[_any()] * (n_arr + n_fwd + rider_out),
        input_output_aliases={n_arr + f: n_arr + f for f in range(n_fwd)},
        scratch_shapes=[pltpu.SemaphoreType.DMA((n_sems,)), pltpu.SemaphoreType.DMA((n_sems,)),
                        pltpu.SemaphoreType.DMA((n_arr,))]
        + [pltpu.SemaphoreType.DMA((r.n_sems,)) for r in riders for _ in range(2)],
    )(*blocks, *forward, *rider_operands)


def _share_job(bufs):
    def make(ins, outs, send_sems, recv_sems):
        del ins
        x, y, c, _ = _mesh_place()
        sems = lambda a: dict(send_sem=send_sems.at[a], recv_sem=recv_sems.at[a],
                              device_id=(x, y, 1 - c), device_id_type=MESH)
        sends = [pltpu.make_async_remote_copy(src_ref=o.at[c], dst_ref=o.at[c], **sems(a)) for a, o in enumerate(outs)]
        arrivals = [pltpu.make_async_remote_copy(src_ref=o.at[c], dst_ref=o.at[1 - c], **sems(a))
                    for a, o in enumerate(outs)]
        return sends, [s.wait_send for s in sends] + [r.wait_recv for r in arrivals]

    shapes = tuple(jax.ShapeDtypeStruct(b.shape, b.dtype) for b in bufs)
    return _Comm(tuple(bufs), shapes, len(bufs), make, in_place=len(bufs))


def _sibling_share(bufs, name):
    n_arr = len(bufs)

    def body(*refs):
        out_refs = refs[n_arr:2 * n_arr]
        send_sems, recv_sems = refs[2 * n_arr:]
        x, y, c = lax.axis_index("x"), lax.axis_index("y"), lax.axis_index("c")
        copies = [pltpu.make_async_remote_copy(
            src_ref=out_refs[a].at[c], dst_ref=out_refs[a].at[c],
            send_sem=send_sems.at[a], recv_sem=recv_sems.at[a],
            device_id=(x, y, 1 - c), device_id_type=MESH) for a in range(n_arr)]
        for cp in copies:
            cp.start()
        for a in range(n_arr):
            pltpu.make_async_remote_copy(
                src_ref=out_refs[a].at[c], dst_ref=out_refs[a].at[1 - c],
                send_sem=send_sems.at[a], recv_sem=recv_sems.at[a],
                device_id=(x, y, 1 - c), device_id_type=MESH).wait()

    return pl.pallas_call(
        body, name=name,
        out_shape=[jax.ShapeDtypeStruct(b.shape, b.dtype) for b in bufs],
        in_specs=[_any()] * n_arr, out_specs=[_any()] * n_arr,
        input_output_aliases={a: a for a in range(n_arr)},
        scratch_shapes=[pltpu.SemaphoreType.DMA((n_arr,)), pltpu.SemaphoreType.DMA((n_arr,))],
    )(*bufs)


def _gelu_tanh(z):
    k = math.sqrt(2.0 / math.pi)
    t = jnp.tanh(k * (z + 0.044715 * (z * z * z)))
    return 0.5 * z * (1.0 + t), t


def _gelu_tanh_grad(z, t):
    k = math.sqrt(2.0 / math.pi)
    return 0.5 * (1.0 + t) + 0.5 * z * (1.0 - t * t) * (k * (1.0 + 3.0 * 0.044715 * (z * z)))


def _rope_angle_kernel(pos_row, invf_col):
    seq = pos_row.shape[1]

    def body(p_ref, f_ref, cos_ref, sin_ref):
        ang = p_ref[...].astype(F32) * f_ref[...]
        cos_ref[...] = jnp.cos(ang)
        sin_ref[...] = jnp.sin(ang)

    return pl.pallas_call(
        body, name="rope_angles", grid=(1,), out_shape=[jax.ShapeDtypeStruct((ROT_DIM // 2, seq), F32)] * 2,
        in_specs=[_full((1, seq)), _full((ROT_DIM // 2, 1))], out_specs=[_full((ROT_DIM // 2, seq))] * 2,
        compiler_params=_params("arbitrary"),
    )(pos_row, invf_col)


def _rope_lane_tables(cos, sin):
    cos_t, sin_t = cos.T, sin.T
    seq, half = cos_t.shape
    ones = jnp.ones((seq, HEAD_DIM - ROT_DIM), F32)
    c64 = jnp.concatenate([cos_t, cos_t, ones], axis=1)
    s1 = jnp.concatenate([sin_t, jnp.zeros((seq, HEAD_DIM - half), F32)], axis=1)
    s2 = jnp.concatenate([jnp.zeros((seq, half), F32), sin_t, jnp.zeros((seq, HEAD_DIM - ROT_DIM), F32)], axis=1)
    return jnp.concatenate([jnp.tile(t, (1, LANES // HEAD_DIM)) for t in (c64, s1, s2)], axis=1)


def _rope_apply(t, tab, sign):
    reps = t.shape[1] // LANES
    c_tab, s1, s2 = (jnp.tile(tab[:, LANES * k:LANES * (k + 1)], (1, reps)) if reps > 1
                     else tab[:, LANES * k:LANES * (k + 1)] for k in range(3))
    half = ROT_DIM // 2
    up = pltpu.roll(t, t.shape[1] - half, 1)
    down = pltpu.roll(t, half, 1)
    return t * c_tab + sign * (down * s2 - up * s1)


def _lane_masks(shape):
    lane = lax.broadcasted_iota(jnp.int32, shape, 1)
    return lane < HEAD_DIM, lane >= HEAD_DIM


HEADS_PER_GROUP = N_Q_HEADS // N_KV_HEADS
ATTN_SCALE = 1.0 / math.sqrt(HEAD_DIM)


def _attn_bias_t(first_block):
    kj = lax.broadcasted_iota(jnp.int32, (2 * CHUNK, CHUNK), 0)
    qi = lax.broadcasted_iota(jnp.int32, (2 * CHUNK, CHUNK), 1)
    ok = (kj > qi) & (kj <= qi + CHUNK)
    if first_block is not None:
        ok = ok & (jnp.logical_not(first_block) | (kj >= CHUNK))
    return jnp.tile(jnp.where(ok, 0.0, -jnp.inf), (1, HEADS_PER_GROUP))


def _group_rows(x, g, lo, hi):
    rows = []
    for r in range(HEADS_PER_GROUP):
        h = HEADS_PER_GROUP * g + r
        pair = x[:, LANES * (h // 2):LANES * (h // 2 + 1)]
        rows.append(jnp.where(hi if h % 2 else lo, pair, 0.0))
    return jnp.concatenate(rows, axis=0)


def _pairs_from_rows(rows, lo):
    return [jnp.where(lo, rows[2 * CHUNK * k:2 * CHUNK * k + CHUNK], rows[2 * CHUNK * k + CHUNK:2 * CHUNK * (k + 1)])
            for k in range(HEADS_PER_GROUP // 2)]


def _group_dup(a, b, g, lo2):
    return jnp.where(lo2, a, b) if g == 0 else jnp.where(lo2, b, a)


def _sink_row(sink_ref, g):
    return jnp.concatenate([sink_ref[HEADS_PER_GROUP * g + r:HEADS_PER_GROUP * g + r + 1, :]
                            for r in range(HEADS_PER_GROUP)], axis=1)


def _attn_probs_t(k_dup, q_rows, bias_t, sink_row):
    s_t = _dot_nt(k_dup, q_rows) * ATTN_SCALE + bias_t
    m = jnp.maximum(jnp.max(s_t, axis=0, keepdims=True), sink_row)
    p = jnp.exp(s_t - m)
    e_sink = jnp.exp(sink_row - m)
    inv = 1.0 / (jnp.sum(p, axis=0, keepdims=True) + e_sink)
    return p * inv, e_sink * inv


def _sgu_forward_pair(wm, vp, j):
    lo, hi = _lane_masks(vp.shape)
    lhs = jnp.concatenate([wm[2 * j], wm[2 * j + 1]], axis=1)
    rhs = jnp.concatenate([jnp.where(lo, vp, 0.0), jnp.where(hi, vp, 0.0)], axis=0)
    return _dot(lhs, rhs)


def _masked_spatial(w_ref):
    t = lax.broadcasted_iota(jnp.int32, (CHUNK, CHUNK), 0)
    s = lax.broadcasted_iota(jnp.int32, (CHUNK, CHUNK), 1)
    tril = s <= t
    return [jnp.where(tril, w_ref[g], 0.0) for g in range(GMLP_GROUPS)], tril, s >= t


def _mod_kernel(c_all, w_shard, b_shard, comm=None):
    n = w_shard.shape[1]
    tn = 512

    def body(c_ref, w_ref, b_ref, mod_ref, act_ref):
        cv = c_ref[...]
        act = cv * (1.0 / (1.0 + jnp.exp(-cv)))
        act_ref[...] = act
        mod_ref[...] = _dot(act, w_ref[...]) + b_ref[...]

    return _hosted_call(
        body, comm, name="ada_mod", grid=(n // tn,),
        out_shape=[jax.ShapeDtypeStruct((N_DEV, n), F32), jax.ShapeDtypeStruct((N_DEV, D_MODEL), F32)],
        in_specs=[_full((N_DEV, D_MODEL)), pl.BlockSpec((D_MODEL, tn), lambda i: (0, i)),
                  pl.BlockSpec((1, tn), lambda i: (0, i))],
        out_specs=[pl.BlockSpec((N_DEV, tn), lambda i: (0, i)), _full((N_DEV, D_MODEL))],
        semantics=("arbitrary",),
    )(c_all, w_shard, b_shard)


def _load_chip_blocks(chip_ref, gathered, local, dsts, sems, first_sem=0):
    for k, dst in enumerate(dsts):
        @pl.when(chip_ref[0] == k)
        def _():
            pltpu.make_async_copy(local, dst, sems.at[first_sem + k]).start()

        @pl.when(chip_ref[0] != k)
        def _():
            pltpu.make_async_copy(gathered.at[k], dst, sems.at[first_sem + k]).start()
    return [pltpu.make_async_copy(local, dst, sems.at[first_sem + k]).wait for k, dst in enumerate(dsts)]


def _in_proj_kernel(x, vecs, w_in_t, comm=None):
    seq = x.shape[0]
    tm = 512

    def body(x_ref, v_ref, w_ref, proj_ref, h_ref):
        xv = x_ref[...]
        rstd = lax.rsqrt(_mean_last(xv * xv) + EPS)
        n1 = (xv * rstd) * v_ref[0:1, :]
        h = n1 * (1.0 + v_ref[2:3, :]) + v_ref[1:2, :]
        hb = h.astype(MXU_DTYPE)
        h_ref[...] = hb
        proj_ref[...] = _dot_nt(hb, w_ref[...])

    return _hosted_call(
        body, comm, name="in_proj", grid=(seq // tm,),
        out_shape=[jax.ShapeDtypeStruct((seq, IN_PROJ_WIDTH), F32),
                   jax.ShapeDtypeStruct((seq, D_MODEL), MXU_DTYPE)],
        in_specs=[pl.BlockSpec((tm, D_MODEL), lambda i: (i, 0)), _full((8, D_MODEL)),
                  _full((IN_PROJ_WIDTH, D_MODEL))],
        out_specs=[pl.BlockSpec((tm, IN_PROJ_WIDTH), lambda i: (i, 0)),
                   pl.BlockSpec((tm, D_MODEL), lambda i: (i, 0))],
        semantics=("arbitrary",),
    )(x, vecs, w_in_t)


MIXER_BLOCKS_PER_STEP = 4
KV_START = 2 * GMLP_WIDTH + ATTN_WIDTH


def _mixer_fwd_kernel(proj, rope_tab, w_spatial, bias_full, sink_rows, comm=None):
    seq = proj.shape[0]
    per = MIXER_BLOCKS_PER_STEP
    steps = seq // (CHUNK * per)
    kv_col = KV_START // (2 * KV_WIDTH)

    def body(proj_ref, prev_ref, tab_ref, ptab_ref, w_ref, bias_ref, sink_ref, cat_ref):
        i = pl.program_id(0)
        wm, _, _ = _masked_spatial(w_ref)
        lo, hi = _lane_masks((CHUNK, LANES))
        lo2, _ = _lane_masks((2 * CHUNK, LANES))
        o = 2 * GMLP_WIDTH
        for s in range(per):
            rows, before = slice(CHUNK * s, CHUNK * (s + 1)), slice(CHUNK * (s - 1), CHUNK * s)
            for j in range(GMLP_GROUPS // 2):
                cols = slice(LANES * j, LANES * (j + 1))
                vcols = slice(GMLP_WIDTH + LANES * j, GMLP_WIDTH + LANES * (j + 1))
                u, _ = _gelu_tanh(proj_ref[rows, cols])
                vp, _ = _gelu_tanh(proj_ref[rows, vcols])
                sv = _sgu_forward_pair(wm, vp, j) + bias_ref[:, cols]
                cat_ref[rows, cols] = (u * sv).astype(cat_ref.dtype)
            tab = tab_ref[rows, :]
            if s == 0:
                prev_kv, prev_tab, first = prev_ref[...], ptab_ref[...], i == 0
            else:
                prev_kv, prev_tab, first = proj_ref[before, KV_START:KV_START + 2 * KV_WIDTH], tab_ref[before, :], None
            q_r = _rope_apply(proj_ref[rows, o:o + ATTN_WIDTH], tab, 1.0)
            k_cur = _rope_apply(proj_ref[rows, KV_START:KV_START + KV_WIDTH], tab, 1.0)
            k_prev = _rope_apply(prev_kv[:, 0:KV_WIDTH], prev_tab, 1.0)
            k_a = jnp.concatenate([k_prev, k_cur], axis=0)
            v_a = jnp.concatenate([prev_kv[:, KV_WIDTH:2 * KV_WIDTH],
                                   proj_ref[rows, KV_START + KV_WIDTH:KV_START + 2 * KV_WIDTH]], axis=0)
            k_b = pltpu.roll(k_a, HEAD_DIM, 1)
            v_b = pltpu.roll(v_a, HEAD_DIM, 1)
            bias_t = _attn_bias_t(first)
            for g in range(N_KV_HEADS):
                p_t, _ = _attn_probs_t(_group_dup(k_a, k_b, g, lo2), _group_rows(q_r, g, lo, hi), bias_t,
                                       _sink_row(sink_ref, g))
                o_t = _dot(_group_dup(v_a, v_b, g, lo2).T, p_t)
                for k, pair in enumerate(_pairs_from_rows(o_t.T, lo)):
                    c0 = GMLP_WIDTH + LANES * (2 * g + k)
                    cat_ref[rows, c0:c0 + LANES] = pair.astype(cat_ref.dtype)

    return _hosted_call(
        body, comm, name="mixer_fwd", grid=(steps,),
        out_shape=[jax.ShapeDtypeStruct((seq, D_MODEL), MXU_DTYPE)],
        in_specs=[pl.BlockSpec((CHUNK * per, IN_PROJ_WIDTH), lambda i: (i, 0)),
                  pl.BlockSpec((CHUNK, 2 * KV_WIDTH), lambda i: (jnp.maximum(per * i - 1, 0), kv_col)),
                  pl.BlockSpec((CHUNK * per, 3 * LANES), lambda i: (i, 0)),
                  pl.BlockSpec((CHUNK, 3 * LANES), lambda i: (jnp.maximum(per * i - 1, 0), 0)),
                  _full((GMLP_GROUPS, CHUNK, CHUNK)), _full((CHUNK, GMLP_WIDTH)),
                  _full((N_Q_HEADS, LANES))],
        out_specs=[pl.BlockSpec((CHUNK * per, D_MODEL), lambda i: (i, 0))],
        semantics=("arbitrary",),
    )(proj, proj, rope_tab, rope_tab, w_spatial, bias_full, sink_rows)


def _trunk_kernel(x, target, cat, vecs, chip_idx, gathered, local):
    seq = x.shape[0]
    tm = 256
    nj = D_FF // D_MODEL
    out_rows = D_MODEL // N_CHIPS

    def body(chip_ref, x_ref, t_ref, cat_ref, v_ref, g_out, g_w1, g_w2, l_out, l_w1, l_w2,
             dx1_ref, dcat_ref, dmix_ref, h2_ref, r_ref, da_ref, dff_ref, sums_ref,
             wout, w1, w2, a_scr, sem):
        i = pl.program_id(0)

        @pl.when(i == 0)
        def _():
            waits = _load_chip_blocks(chip_ref, g_out, l_out,
                                      [wout.at[pl.ds(out_rows * k, out_rows)] for k in range(N_CHIPS)], sem)
            waits += _load_chip_blocks(chip_ref, g_w1, l_w1, [w1.at[k] for k in range(N_CHIPS)], sem, N_CHIPS)
            waits += _load_chip_blocks(chip_ref, g_w2, l_w2, [w2.at[k] for k in range(N_CHIPS)], sem, 2 * N_CHIPS)
            for wait in waits:
                wait()
            sums_ref[...] = jnp.zeros_like(sums_ref)

        gate1, shift2, scale2 = v_ref[0:1, :], v_ref[1:2, :], v_ref[2:3, :]
        gate2, g_ffn, g_final = v_ref[3:4, :], v_ref[4:5, :], v_ref[5:6, :]

        mix = _dot(cat_ref[...], wout[...])
        x1 = x_ref[...] + gate1 * mix
        rstd2 = lax.rsqrt(_mean_last(x1 * x1) + EPS)
        xh2 = x1 * rstd2
        n2 = xh2 * g_ffn
        h2b = (n2 * (1.0 + scale2) + shift2).astype(MXU_DTYPE)
        h2_ref[...] = h2b
        ff = jnp.zeros((tm, D_MODEL), F32)
        for j in range(nj):
            a = _dot(h2b, w1[j])
            a_scr[j] = a
            relu = jnp.maximum(a, 0.0)
            rb = (relu * relu).astype(MXU_DTYPE)
            r_ref[:, D_MODEL * j:D_MODEL * (j + 1)] = rb
            ff = ff + _dot(rb, w2[j])
        x2 = x1 + gate2 * ff
        rstd3 = lax.rsqrt(_mean_last(x2 * x2) + EPS)
        xh3 = x2 * rstd3
        err = xh3 * g_final - t_ref[...]
        loss = 0.5 * _rowsum(_mean_last(err * err))
        dy = err * (1.0 / D_MODEL)
        dxh3 = dy * g_final
        dx2 = rstd3 * (dxh3 - xh3 * _mean_last(dxh3 * xh3))
        dffb = (dx2 * gate2).astype(MXU_DTYPE)
        dff_ref[...] = dffb
        dh2 = jnp.zeros((tm, D_MODEL), F32)
        for j in range(nj):
            dr = _dot_nt(dffb, w2[j])
            dab = (dr * (2.0 * jnp.maximum(a_scr[j], 0.0))).astype(MXU_DTYPE)
            da_ref[:, D_MODEL * j:D_MODEL * (j + 1)] = dab
            dh2 = dh2 + _dot_nt(dab, w1[j])
        dn2 = dh2 * (1.0 + scale2)
        dxh2 = dn2 * g_ffn
        dx1 = dx2 + rstd2 * (dxh2 - xh2 * _mean_last(dxh2 * xh2))
        dx1_ref[...] = dx1
        dmixb = (dx1 * gate1).astype(MXU_DTYPE)
        dmix_ref[...] = dmixb
        dcat_ref[...] = _dot_nt(dmixb, wout[...])

        sums_ref[0:1, :] += _rowsum(dh2)
        sums_ref[1:2, :] += _rowsum(dh2 * n2)
        sums_ref[2:3, :] += _rowsum(dx2 * ff)
        sums_ref[3:4, :] += _rowsum(dn2 * xh2)
        sums_ref[4:5, :] += _rowsum(dy * xh3)
        sums_ref[5:6, :] += _rowsum(dx1 * mix)
        sums_ref[6:7, :] += jnp.broadcast_to(loss, (1, D_MODEL))

    tok = lambda w: pl.BlockSpec((tm, w), lambda i, chip: (i, 0))
    return _hosted_call(
        body, None, name="trunk", grid=(seq // tm,), n_prefetch=1,
        out_shape=[jax.ShapeDtypeStruct((seq, D_MODEL), F32), jax.ShapeDtypeStruct((seq, D_MODEL), F32),
                   jax.ShapeDtypeStruct((seq, D_MODEL), MXU_DTYPE), jax.ShapeDtypeStruct((seq, D_MODEL), MXU_DTYPE),
                   jax.ShapeDtypeStruct((seq, D_FF), MXU_DTYPE), jax.ShapeDtypeStruct((seq, D_FF), MXU_DTYPE),
                   jax.ShapeDtypeStruct((seq, D_MODEL), MXU_DTYPE), jax.ShapeDtypeStruct((8, D_MODEL), F32)],
        in_specs=[tok(D_MODEL), tok(D_MODEL), tok(D_MODEL), _full((8, D_MODEL))] + [_any()] * 6,
        out_specs=[tok(D_MODEL), tok(D_MODEL), tok(D_MODEL), tok(D_MODEL), tok(D_FF), tok(D_FF), tok(D_MODEL),
                   _full((8, D_MODEL))],
        scratch_shapes=[pltpu.VMEM((D_MODEL, D_MODEL), MXU_DTYPE), pltpu.VMEM((nj, D_MODEL, D_MODEL), MXU_DTYPE),
                        pltpu.VMEM((nj, D_MODEL, D_MODEL), MXU_DTYPE), pltpu.VMEM((nj, tm, D_MODEL), F32),
                        pltpu.SemaphoreType.DMA((3 * N_CHIPS,))],
        semantics=("arbitrary",),
    )(chip_idx, x, target, cat, vecs, *gathered, *local)


def _mixer_bwd_kernel(proj, rope_tab, dcat, w_spatial, w_spatial_t, bias_full, sink_rows, dev_idx, comm=None):
    seq = proj.shape[0]
    per = MIXER_BLOCKS_PER_STEP
    steps = seq // (CHUNK * per)
    kv_col = KV_START // (2 * KV_WIDTH)

    def body(dev_ref, proj_ref, prev_ref, tab_ref, ptab_ref, dcat_ref, w_ref, wt_ref, bias_ref, sink_ref,
             dproj_ref, dw_out, db_ref, dsink_ref, carry, dw_ref):
        del dev_ref
        step = pl.program_id(0)

        @pl.when(step == 0)
        def _():
            carry[...] = jnp.zeros_like(carry)
            dw_ref[...] = jnp.zeros_like(dw_ref)
            db_ref[...] = jnp.zeros_like(db_ref)
            dsink_ref[...] = jnp.zeros_like(dsink_ref)

        for s in reversed(range(per)):
            rows = pl.ds(CHUNK * s, CHUNK)
            if s == 0:
                before, before_tab, first = prev_ref, ptab_ref, step == steps - 1
            else:
                before = proj_ref.at[pl.ds(CHUNK * (s - 1), CHUNK), pl.ds(KV_START, 2 * KV_WIDTH)]
                before_tab, first = tab_ref.at[pl.ds(CHUNK * (s - 1), CHUNK)], None
            one_block(proj_ref.at[rows], before, tab_ref.at[rows], before_tab, dcat_ref.at[rows], w_ref, wt_ref,
                      bias_ref, sink_ref, dproj_ref.at[rows], dw_ref, db_ref, dsink_ref, carry, first)

        @pl.when(step == steps - 1)
        def _():
            dw_out[...] = dw_ref[...].astype(dw_out.dtype)

    def one_block(proj_ref, prev_ref, tab_ref, ptab_ref, dcat_ref, w_ref, wt_ref, bias_ref, sink_ref,
                  dproj_ref, dw_ref, db_ref, dsink_ref, carry, first):
        wm, tril, triu = _masked_spatial(w_ref)
        lo, hi = _lane_masks((CHUNK, LANES))
        lane = lax.broadcasted_iota(jnp.int32, (CHUNK, LANES), 1)
        db = jnp.zeros((CHUNK, LANES), F32)
        for j in range(GMLP_GROUPS // 2):
            cols = slice(LANES * j, LANES * (j + 1))
            vcols = slice(GMLP_WIDTH + LANES * j, GMLP_WIDTH + LANES * (j + 1))
            zu, zv = proj_ref[:, cols], proj_ref[:, vcols]
            u, tu = _gelu_tanh(zu)
            vp, tv = _gelu_tanh(zv)
            sv = _sgu_forward_pair(wm, vp, j) + bias_ref[:, cols]
            dout = dcat_ref[:, cols]
            du = dout * sv
            dsv = dout * u
            dsv_lo, dsv_hi = jnp.where(lo, dsv, 0.0), jnp.where(hi, dsv, 0.0)
            lhs_t = jnp.concatenate([jnp.where(triu, wt_ref[2 * j], 0.0),
                                     jnp.where(triu, wt_ref[2 * j + 1], 0.0)], axis=1)
            dv = _dot(lhs_t, jnp.concatenate([dsv_lo, dsv_hi], axis=0))
            dw_ref[2 * j] += jnp.where(tril, _dot_nt(dsv_lo, vp), 0.0)
            dw_ref[2 * j + 1] += jnp.where(tril, _dot_nt(dsv_hi, vp), 0.0)
            db = db + (jnp.where(lane == 2 * j, jnp.sum(dsv_lo, axis=1, keepdims=True), 0.0)
                       + jnp.where(lane == 2 * j + 1, jnp.sum(dsv_hi, axis=1, keepdims=True), 0.0))
            dproj_ref[:, cols] = (du * _gelu_tanh_grad(zu, tu)).astype(dproj_ref.dtype)
            dproj_ref[:, vcols] = (dv * _gelu_tanh_grad(zv, tv)).astype(dproj_ref.dtype)
        db_ref[...] += db
        o = 2 * GMLP_WIDTH
        tab = tab_ref[...]
        q_r = _rope_apply(proj_ref[:, o:o + ATTN_WIDTH], tab, 1.0)
        k_cur = _rope_apply(proj_ref[:, o + ATTN_WIDTH:o + ATTN_WIDTH + KV_WIDTH], tab, 1.0)
        k_prev = _rope_apply(prev_ref[:, 0:KV_WIDTH], ptab_ref[...], 1.0)
        k_a = jnp.concatenate([k_prev, k_cur], axis=0)
        v_a = jnp.concatenate([prev_ref[:, KV_WIDTH:2 * KV_WIDTH],
                               proj_ref[:, o + ATTN_WIDTH + KV_WIDTH:o + ATTN_WIDTH + 2 * KV_WIDTH]], axis=0)
        k_b = pltpu.roll(k_a, HEAD_DIM, 1)
        v_b = pltpu.roll(v_a, HEAD_DIM, 1)
        bias_t = _attn_bias_t(first)
        lo2, _ = _lane_masks((2 * CHUNK, LANES))
        dout_b = dcat_ref[:, GMLP_WIDTH:GMLP_WIDTH + ATTN_WIDTH]
        dk_tot, dv_tot, dq_pairs = [], [], []
        for g in range(N_KV_HEADS):
            k_dup, v_dup = _group_dup(k_a, k_b, g, lo2), _group_dup(v_a, v_b, g, lo2)
            q_rows = _group_rows(q_r, g, lo, hi)
            do_rows = _group_rows(dout_b, g, lo, hi)
            p_t, p_sink = _attn_probs_t(k_dup, q_rows, bias_t, _sink_row(sink_ref, g))
            dp_t = _dot_nt(v_dup, do_rows)
            delta = jnp.sum(p_t * dp_t, axis=0, keepdims=True)
            ds_t = p_t * (dp_t - delta) * ATTN_SCALE
            dsink = -p_sink * delta
            for r in range(HEADS_PER_GROUP):
                h = HEADS_PER_GROUP * g + r
                dsink_ref[h:h + 1, :] += jnp.broadcast_to(
                    jnp.sum(dsink[:, LANES * r:LANES * (r + 1)], axis=1, keepdims=True), (1, LANES))
            dk_full = _dot(ds_t, q_rows)
            dv_full = _dot(p_t, do_rows)
            dk_tot.append(dk_full + pltpu.roll(dk_full, HEAD_DIM, 1))
            dv_tot.append(dv_full + pltpu.roll(dv_full, HEAD_DIM, 1))
            dq_t = _dot(k_dup.T, ds_t)
            dq_pairs += _pairs_from_rows(dq_t.T, lo)
        dk_all = jnp.where(lo2, dk_tot[0], dk_tot[1])
        dv_all = jnp.where(lo2, dv_tot[0], dv_tot[1])
        dk_cur = dk_all[CHUNK:, :] + carry[:, 0:KV_WIDTH]
        dv_cur = dv_all[CHUNK:, :] + carry[:, KV_WIDTH:2 * KV_WIDTH]
        carry[:, 0:KV_WIDTH] = dk_all[:CHUNK, :]
        carry[:, KV_WIDTH:2 * KV_WIDTH] = dv_all[:CHUNK, :]
        dq = _rope_apply(jnp.concatenate(dq_pairs, axis=1), tab, -1.0)
        dproj_ref[:, o:o + ATTN_WIDTH] = dq.astype(dproj_ref.dtype)
        dproj_ref[:, o + ATTN_WIDTH:o + ATTN_WIDTH + KV_WIDTH] = (
            _rope_apply(dk_cur, tab, -1.0).astype(dproj_ref.dtype))
        dproj_ref[:, o + ATTN_WIDTH + KV_WIDTH:o + ATTN_WIDTH + 2 * KV_WIDTH] = dv_cur.astype(dproj_ref.dtype)

    rev = lambda i: steps - 1 - i
    before = lambda i: jnp.maximum(per * rev(i) - 1, 0)
    slot = lambda shape: pl.BlockSpec((None,) + shape, lambda i, d: (d[0],) + (0,) * len(shape))
    return _hosted_call(
        body, comm, name="mixer_bwd", grid=(steps,), n_prefetch=1,
        out_shape=[jax.ShapeDtypeStruct((seq, IN_PROJ_WIDTH), MXU_DTYPE),
                   jax.ShapeDtypeStruct((N_DEV, GMLP_GROUPS, CHUNK, CHUNK), GRAD_COMM_DTYPE),
                   jax.ShapeDtypeStruct((N_DEV, CHUNK, LANES), F32),
                   jax.ShapeDtypeStruct((N_DEV, N_Q_HEADS, LANES), F32)],
        in_specs=[pl.BlockSpec((CHUNK * per, IN_PROJ_WIDTH), lambda i, d: (rev(i), 0)),
                  pl.BlockSpec((CHUNK, 2 * KV_WIDTH), lambda i, d: (before(i), kv_col)),
                  pl.BlockSpec((CHUNK * per, 3 * LANES), lambda i, d: (rev(i), 0)),
                  pl.BlockSpec((CHUNK, 3 * LANES), lambda i, d: (before(i), 0)),
                  pl.BlockSpec((CHUNK * per, D_MODEL), lambda i, d: (rev(i), 0)),
                  _full((GMLP_GROUPS, CHUNK, CHUNK)), _full((GMLP_GROUPS, CHUNK, CHUNK)),
                  _full((CHUNK, GMLP_WIDTH)), _full((N_Q_HEADS, LANES))],
        out_specs=[pl.BlockSpec((CHUNK * per, IN_PROJ_WIDTH), lambda i, d: (rev(i), 0)),
                   slot((GMLP_GROUPS, CHUNK, CHUNK)), slot((CHUNK, LANES)), slot((N_Q_HEADS, LANES))],
        scratch_shapes=[pltpu.VMEM((CHUNK, 2 * KV_WIDTH), F32), pltpu.VMEM((GMLP_GROUPS, CHUNK, CHUNK), F32)],
        semantics=("arbitrary",),
    )(dev_idx, proj, proj, rope_tab, rope_tab, dcat, w_spatial, w_spatial_t, bias_full, sink_rows)


def _in_proj_bwd_kernel(x, dx1, dproj, vecs, w_in_t, comm=None):
    seq = x.shape[0]
    tm = 512

    def body(x_ref, dx1_ref, dp_ref, v_ref, w_ref, gx_ref, sums_ref):
        @pl.when(pl.program_id(0) == 0)
        def _():
            sums_ref[...] = jnp.zeros_like(sums_ref)

        g_mix, scale1 = v_ref[0:1, :], v_ref[2:3, :]
        dh = _dot(dp_ref[...], w_ref[...])
        xv = x_ref[...]
        rstd = lax.rsqrt(_mean_last(xv * xv) + EPS)
        xh = xv * rstd
        dn1 = dh * (1.0 + scale1)
        dxh = dn1 * g_mix
        gx_ref[...] = dx1_ref[...] + rstd * (dxh - xh * _mean_last(dxh * xh))
        sums_ref[0:1, :] += _rowsum(dh)
        sums_ref[1:2, :] += _rowsum(dh * (xh * g_mix))
        sums_ref[2:3, :] += _rowsum(dn1 * xh)

    tok = lambda w: pl.BlockSpec((tm, w), lambda i: (i, 0))
    return _hosted_call(
        body, comm, name="in_proj_bwd", grid=(seq // tm,),
        out_shape=[jax.ShapeDtypeStruct((seq, D_MODEL), F32), jax.ShapeDtypeStruct((8, D_MODEL), F32)],
        in_specs=[tok(D_MODEL), tok(D_MODEL), tok(IN_PROJ_WIDTH), _full((8, D_MODEL)),
                  _full((IN_PROJ_WIDTH, D_MODEL))],
        out_specs=[tok(D_MODEL), _full((8, D_MODEL))],
        semantics=("arbitrary",),
    )(x, dx1, dproj, vecs, w_in_t)


class _GradTiles(NamedTuple):
    tm: int
    tn: int
    n_tiles: int
    chips_per_tile: int
    a_index: Callable
    b_index: Callable


def _weight_grad_kernel(a, b, c_idx, name, tiles, comm=None):
    seq = a.shape[0]
    tk = min(seq, 4096)
    nk = seq // tk
    tm, tn, n_tiles, per = tiles.tm, tiles.tn, tiles.n_tiles, tiles.chips_per_tile
    rows = tm // per

    def half(phase, c):
        return phase * c[0] + (1 - phase) * (1 - c[0])

    def body(c_ref, a_ref, b_ref, o_ref, acc, stage, landed, send_sems, recv_sems):
        del c_ref
        phase, t, kk = pl.program_id(0), pl.program_id(1), pl.program_id(2)
        x, y, c, _ = _mesh_place()

        def copy(tile):
            return pltpu.make_async_remote_copy(
                src_ref=stage.at[tile], dst_ref=landed.at[tile], send_sem=send_sems.at[tile],
                recv_sem=recv_sems.at[tile], device_id=(x, y, 1 - c), device_id_type=MESH)

        @pl.when(kk == 0)
        def _():
            acc[...] = jnp.zeros_like(acc)

        acc[...] += _dot_tn(a_ref[...], b_ref[...])

        @pl.when((kk == nk - 1) & (phase == 0))
        def _():
            stage[t] = acc[...].astype(stage.dtype)
            copy(t).start()

        @pl.when((kk == nk - 1) & (phase == 1))
        def _():
            copy(t).wait_recv()
            total = acc[...] + landed[t].astype(F32)
            for q in range(per):
                o_ref[q] = total[rows * q:rows * (q + 1)].astype(o_ref.dtype)

        @pl.when((kk == nk - 1) & (phase == 1) & (t == n_tiles - 1))
        def _():
            for tile in range(n_tiles):
                copy(tile).wait_send()

    out = _hosted_call(
        body, comm, name=name, grid=(2, n_tiles, nk), n_prefetch=1,
        out_shape=[jax.ShapeDtypeStruct((n_tiles * per, rows, tn), GRAD_COMM_DTYPE)],
        in_specs=[pl.BlockSpec((tk, tm), lambda p, t, k, c: (k, tiles.a_index(t, half(p, c)))),
                  pl.BlockSpec((tk, tn), lambda p, t, k, c: (k, tiles.b_index(t, half(p, c))))],
        out_specs=[pl.BlockSpec((per, rows, tn), lambda p, t, k, c: (p * t, 0, 0))],
        scratch_shapes=[pltpu.VMEM((tm, tn), F32), pltpu.VMEM((n_tiles, tm, tn), GRAD_COMM_DTYPE),
                        pltpu.VMEM((n_tiles, tm, tn), GRAD_COMM_DTYPE),
                        pltpu.SemaphoreType.DMA((n_tiles,)), pltpu.SemaphoreType.DMA((n_tiles,))],
        semantics=("arbitrary", "arbitrary", "arbitrary"),
    )(c_idx, a, b)
    return out[0] if comm is None else out


def _row_tile(rows, most=256, sublanes=16):
    return max(t for t in range(sublanes, most + 1, sublanes) if rows % t == 0)


def _adam_update(w, g, m, v):
    m_new = ADAM_B1 * m + (1.0 - ADAM_B1) * g
    v_new = ADAM_B2 * v + (1.0 - ADAM_B2) * (g * g)
    m_hat = m_new / (1.0 - ADAM_B1 ** ADAM_STEP)
    v_hat = v_new / (1.0 - ADAM_B2 ** ADAM_STEP)
    delta = -ADAM_LR * (m_hat / (jnp.sqrt(v_hat) + ADAM_EPS) + ADAM_WD * w)
    return delta, m_new, v_new


def _sum_chips_kernel(own, others, place, name):
    _, r, n = own.shape
    tr = _row_tile(r)

    def body(place_ref, own_ref, oth_ref, o_ref):
        del place_ref
        acc = own_ref[...].astype(F32)
        for k in range(N_CHIPS - 1):
            acc = acc + oth_ref[k].astype(F32)
        o_ref[...] = acc

    return pl.pallas_call(
        body, name=name, out_shape=jax.ShapeDtypeStruct((2, r, n), F32),
        grid_spec=pltpu.PrefetchScalarGridSpec(
            num_scalar_prefetch=1, grid=(r // tr,),
            in_specs=[pl.BlockSpec((None, tr, n), lambda i, p: (p[0], i, 0)),
                      pl.BlockSpec((N_CHIPS - 1, tr, n), lambda i, p: (0, i, 0))],
            out_specs=pl.BlockSpec((None, tr, n), lambda i, p: (p[1], i, 0))),
        compiler_params=_params("parallel"),
    )(place, own, others)


def _adam_kernel(w, g, m, v, name):
    r, n = w.shape
    by_columns = g.shape[1] == r
    tr, tn = _row_tile(g.shape[1], most=512), g.shape[2]

    def body(w_ref, g_ref, m_ref, v_ref, g_out, d_ref, mo_ref, vo_ref):
        gv = g_ref[...]
        g_out[...] = gv
        d_ref[...], mo_ref[...], vo_ref[...] = _adam_update(w_ref[...], gv, m_ref[...], v_ref[...])

    steps = g.shape[1] // tr
    spec = pl.BlockSpec((tr, tn), (lambda h, i: (i, h)) if by_columns else (lambda h, i: (h * steps + i, 0)))
    return pl.pallas_call(
        body, name=name, grid=(2, steps), out_shape=[jax.ShapeDtypeStruct((r, n), F32)] * 4,
        in_specs=[spec, pl.BlockSpec((None, tr, tn), lambda h, i: (h, i, 0)), spec, spec], out_specs=[spec] * 4,
        compiler_params=_params("parallel", "parallel"),
    )(w, g, m, v)


SMALL_PARAMS = ("b_ada", "g_mix", "g_ffn", "g_final", "b_spatial", "sinks", "w_spatial")


def _small_update_kernel(gathered, params):
    shapes = [params[nm][0].shape for nm in SMALL_PARAMS]

    def body(*refs):
        g_refs, refs = refs[:5], refs[5:]
        p_refs, refs = refs[:3 * len(SMALL_PARAMS)], refs[3 * len(SMALL_PARAMS):]
        loss_ref, o_refs = refs[0], refs[1:]

        def total(ref):
            acc = ref[0].astype(F32)
            for k in range(1, N_DEV):
                acc = acc + ref[k].astype(F32)
            return acc

        s1, s2, db, ds, dw = (total(r) for r in g_refs)
        loss_ref[...] = jnp.broadcast_to(s2[6:7, 0:1], loss_ref.shape)
        grads = {"b_ada": [s1[0:1], s1[1:2], s2[5:6], s2[0:1], s2[1:2], s2[2:3]], "g_mix": [s1[2:3]],
                 "g_ffn": [s2[3:4]], "g_final": [s2[4:5]], "b_spatial": [db.T[0:GMLP_GROUPS]],
                 "w_spatial": [dw]}
        lane = lax.broadcasted_iota(jnp.int32, (1, LANES), 1)
        sink_row = jnp.zeros((1, LANES), F32)
        for h in range(N_Q_HEADS):
            sink_row = sink_row + jnp.where(lane == h, ds[h:h + 1, :], 0.0)
        grads["sinks"] = [sink_row[:, 0:N_Q_HEADS]]
        for i, nm in enumerate(SMALL_PARAMS):
            w_ref, m_ref, v_ref = p_refs[3 * i:3 * i + 3]
            outs = o_refs[4 * i:4 * i + 4]
            width = grads[nm][0].shape[1]
            for k, g in enumerate(grads[nm]):
                cols = slice(width * k, width * (k + 1))
                upd = _adam_update(w_ref[:, cols], g, m_ref[:, cols], v_ref[:, cols])
                for o_ref, val in zip(outs, (g,) + upd):
                    o_ref[:, cols] = val

    flat = [a for nm in SMALL_PARAMS for a in params[nm]]
    out_shape = [jax.ShapeDtypeStruct((8, LANES), F32)]
    out_shape += [jax.ShapeDtypeStruct(s, F32) for s in shapes for _ in range(4)]
    outs = pl.pallas_call(
        body, name="small_update", grid=(1,), out_shape=out_shape,
        in_specs=[_full(g.shape) for g in gathered] + [_full(a.shape) for a in flat],
        out_specs=[_full(s.shape) for s in out_shape],
        compiler_params=_params("arbitrary"),
    )(*gathered, *flat)
    return {nm: outs[1 + 4 * i:5 + 4 * i] for i, nm in enumerate(SMALL_PARAMS)}, outs[0]


def _ada_update_kernel(act_t, dmod, w, m, v):
    r, n = w.shape
    tr = 256

    def body(a_ref, d_ref, w_ref, m_ref, v_ref, g_ref, dl_ref, mo_ref, vo_ref):
        g = _dot(a_ref[...], d_ref[...])
        g_ref[...] = g
        dl_ref[...], mo_ref[...], vo_ref[...] = _adam_update(w_ref[...], g, m_ref[...], v_ref[...])

    spec = pl.BlockSpec((tr, n), lambda i: (i, 0))
    return pl.pallas_call(
        body, name="ada_update", grid=(r // tr,), out_shape=[jax.ShapeDtypeStruct((r, n), F32)] * 4,
        in_specs=[pl.BlockSpec((tr, N_DEV), lambda i: (i, 0)), _full((N_DEV, n)), spec, spec, spec],
        out_specs=[spec] * 4, compiler_params=_params("parallel"),
    )(act_t, dmod, w, m, v)


def kernel(x, c, positions, w_ada, b_ada, g_mix, w_in, w_spatial, b_spatial, sinks, w_out, g_ffn, w_ff1, w_ff2, g_final, loss_target, m_w_ada, m_b_ada, m_g_mix, m_w_in, m_w_spatial, m_b_spatial, m_sinks, m_w_out, m_g_ffn, m_w_ff1, m_w_ff2, m_g_final, v_w_ada, v_b_ada, v_g_mix, v_w_in, v_w_spatial, v_b_spatial, v_sinks, v_w_out, v_g_ffn, v_w_ff1, v_w_ff2, v_g_final):
    xi, yi, ci = lax.axis_index("x"), lax.axis_index("y"), lax.axis_index("c")
    chip = 2 * xi + yi
    dev = 2 * chip + ci
    seq = x.shape[1]
    x2, tgt = x[0], loss_target[0]
    ada_cols = w_ada.shape[2]

    big = {"w_in": tuple(a[0].T for a in (w_in, m_w_in, v_w_in)),
           "w_out": (w_out[0], m_w_out[0], v_w_out[0]), "w_ff1": (w_ff1[0], m_w_ff1[0], v_w_ff1[0]),
           "w_ff2": (w_ff2[0], m_w_ff2[0], v_w_ff2[0])}

    def halves(nm):
        r, n = big[nm][0].shape
        return big[nm][0].astype(WEIGHT_COMM_DTYPE).reshape(2, r // 2, n)

    chip_idx = chip.reshape(1).astype(jnp.int32)
    c_all, w_in_t, g_out = _all_gather8([c, halves("w_in"), halves("w_out")], "gather_first",
                                        split=[False, True, True], skip_own=(2,))
    c_all, w_in_t = c_all.reshape(N_DEV, D_MODEL), w_in_t.reshape(IN_PROJ_WIDTH, D_MODEL)
    b_shard = lax.dynamic_slice(b_ada, (0, chip * ada_cols), (1, ada_cols))
    mod_part, act = _mod_kernel(c_all, w_ada[0], b_shard)
    mod_all, = _all_gather8([mod_part], "gather_mod")
    mod_me = lax.dynamic_index_in_dim(mod_all[0::2], dev, axis=1, keepdims=False)
    mod_me = mod_me.reshape(N_MOD, D_MODEL)
    shift1, scale1, gate1, shift2, scale2, gate2 = (mod_me[k:k + 1] for k in range(N_MOD))

    zeros_row = jnp.zeros((1, D_MODEL), F32)
    vecs1 = jnp.concatenate([g_mix, shift1, scale1] + [zeros_row] * 5, axis=0)
    vecs2 = jnp.concatenate([gate1, shift2, scale2, gate2, g_ffn, g_final.reshape(1, D_MODEL)]
                            + [zeros_row] * 2, axis=0)
    bias_full = jnp.repeat(b_spatial[0].T, HEAD_DIM, axis=1)
    sink_rows = jnp.broadcast_to(sinks[0][:, None], (N_Q_HEADS, LANES))
    inv_freq = ROPE_THETA ** (-jnp.arange(0, ROT_DIM, 2, dtype=F32) / ROT_DIM)
    rope_tab = _rope_lane_tables(*_rope_angle_kernel(positions, inv_freq.reshape(ROT_DIM // 2, 1)))

    trunk_weights = ["w_out", "w_ff1", "w_ff2"]
    shards = [halves(nm) for nm in trunk_weights]
    proj, hb, *staged = _in_proj_kernel(x2, vecs1, w_in_t, comm=_gather2d_first(shards[1:]))
    cat, *staged = _mixer_fwd_kernel(proj, rope_tab, w_spatial[0], bias_full, sink_rows,
                                     comm=_gather2d_second(staged, shards[1:]))
    staged = [g_out] + list(_gather_forward(staged, "gather_forward"))
    dx1, dcat, dmix, h2b, rb, dab, dffb, sums2 = _trunk_kernel(
        x2, tgt, cat, vecs2, chip_idx,
        [g.reshape((N_CHIPS,) + big[nm][0].shape) for nm, g in zip(trunk_weights, staged)],
        [s.reshape(big[nm][0].shape) for nm, s in zip(trunk_weights, shards)])

    c_idx = ci.reshape(1).astype(jnp.int32)
    place = jnp.stack([chip, ci]).astype(jnp.int32)
    half_d = D_MODEL // 2
    cs_ff2 = _weight_grad_kernel(rb, dffb, c_idx, "dw_ff2",
                                 _GradTiles(D_MODEL, half_d, N_CHIPS, 1, lambda t, h: t, lambda t, h: h))
    cs_ff1, sc_ff2 = _weight_grad_kernel(
        h2b, dab, c_idx, "dw_ff1",
        _GradTiles(D_MODEL, half_d, N_CHIPS, 1, lambda t, h: 0, lambda t, h: 2 * t + h),
        comm=_scatter_job([cs_ff2]))
    cs_out = _weight_grad_kernel(cat, dmix, c_idx, "dw_out",
                                 _GradTiles(D_MODEL, half_d, 1, N_CHIPS, lambda t, h: 0, lambda t, h: h))
    dproj, dw_spatial, db_lanes, dsink_rows, sc_ff1, sc_out = _mixer_bwd_kernel(
        proj, rope_tab, dcat, w_spatial[0], w_spatial[0].transpose(0, 2, 1), bias_full, sink_rows,
        dev.reshape(1).astype(jnp.int32), comm=_scatter_job([cs_ff1, cs_out]))
    totals = [_sum_chips_kernel(own, oth, place, "grad_sum_" + nm)
              for nm, own, oth in (("w_out", cs_out, sc_out), ("w_ff1", cs_ff1, sc_ff1), ("w_ff2", cs_ff2, sc_ff2))]
    small_slots = [db_lanes, dsink_rows, dw_spatial.reshape(N_DEV, GMLP_GROUPS * CHUNK, CHUNK)]
    cs_in, *rode = _weight_grad_kernel(
        dproj, hb, c_idx, "dw_in",
        _GradTiles(2 * W_IN_BLOCK, half_d, N_CHIPS // 2, 2, lambda t, h: t, lambda t, h: h),
        comm=_merge_in_place(_gather_job(small_slots), _share_job(totals)))
    small_stage1, shared = rode[:len(small_slots)], rode[len(small_slots):]
    grad_x, sums1 = _in_proj_bwd_kernel(x2, dx1, dproj, vecs1, w_in_t)
    *gathered, sc_in = _all_gather8([sums1, sums2], "gather_small", forward=small_stage1,
                                    riders=[_scatter_job([cs_in])])
    total_in = _sum_chips_kernel(cs_in, sc_in, place, "grad_sum_w_in")
    shared = list(_sibling_share([total_in], "grad_share_w_in")) + list(shared)
    names = ["w_in", "w_out", "w_ff1", "w_ff2"]
    big_out = {}
    for nm, g in zip(names, shared):
        w, m, v = big[nm]
        outs = _adam_kernel(w, g, m, v, "adam_" + nm)
        big_out[nm] = tuple((t.T if nm == "w_in" else t)[None] for t in outs)

    small = {"b_ada": (b_ada, m_b_ada, v_b_ada), "g_mix": (g_mix, m_g_mix, v_g_mix),
             "g_ffn": (g_ffn, m_g_ffn, v_g_ffn), "g_final": (g_final, m_g_final, v_g_final),
             "b_spatial": (b_spatial, m_b_spatial, v_b_spatial), "sinks": (sinks, m_sinks, v_sinks),
             "w_spatial": (w_spatial, m_w_spatial, v_w_spatial)}
    flat_shape = {"g_final": (1, D_MODEL), "b_spatial": (GMLP_GROUPS, CHUNK), "w_spatial": (GMLP_GROUPS * CHUNK, CHUNK)}
    small_out, loss_tile = _small_update_kernel(
        gathered, {nm: tuple(a.reshape(flat_shape.get(nm, a.shape)) for a in small[nm]) for nm in small})
    small_out = {nm: [o.reshape(small[nm][0].shape) for o in small_out[nm]] for nm in small}
    loss = loss_tile[0, 0]

    g1, g2 = gathered[0], gathered[1]
    dmod_all = jnp.concatenate([g1[:, 0], g1[:, 1], g2[:, 5], g2[:, 0], g2[:, 1], g2[:, 2]], axis=1)
    dmod_cols = lax.dynamic_slice(dmod_all, (0, chip * ada_cols), (N_DEV, ada_cols))
    ada = _ada_update_kernel(act.T, dmod_cols, w_ada[0], m_w_ada[0], v_w_ada[0])
    big_out["w_ada"] = tuple(t[None] for t in ada)

    order = ["w_ada", "b_ada", "g_mix", "w_in", "w_spatial", "b_spatial", "sinks", "w_out", "g_ffn",
             "w_ff1", "w_ff2", "g_final"]

    def leaf(nm, k):
        return big_out[nm][k] if nm in big_out else small_out[nm][k]

    outs = [loss, grad_x[None]]
    for k in range(4):
        outs += [leaf(nm, k) for nm in order]
    return tuple(outs)
```

```python
import math
from typing import Callable, NamedTuple

import jax
import jax.numpy as jnp
from jax import lax
from jax.experimental import pallas as pl
from jax.experimental.pallas import tpu as pltpu

F32 = jnp.float32
MXU_DTYPE = jnp.bfloat16
WEIGHT_COMM_DTYPE = jnp.bfloat16
GRAD_COMM_DTYPE = jnp.bfloat16

D_MODEL = 1024
D_FF = 4096
HEAD_DIM = 64
GMLP_GROUPS = 8
GMLP_WIDTH = 512
CHUNK = 128
N_Q_HEADS = 8
N_KV_HEADS = 2
ATTN_WIDTH = 512
KV_WIDTH = 128
ROT_DIM = 16
ROPE_THETA = 500000.0
IN_PROJ_WIDTH = 1792
N_MOD = 6
EPS = 1e-5
N_CHIPS = 4
N_DEV = 8
LANES = 128
W_IN_BLOCK = IN_PROJ_WIDTH // N_CHIPS

ADAM_LR = 0.001
ADAM_B1 = 0.9
ADAM_B2 = 0.999
ADAM_EPS = 1e-08
ADAM_WD = 0.01
ADAM_STEP = 10

VMEM_LIMIT_BYTES = 58 * 1024 * 1024
MESH = pl.DeviceIdType.MESH


def _params(*semantics):
    return pltpu.CompilerParams(dimension_semantics=semantics, vmem_limit_bytes=VMEM_LIMIT_BYTES)


def _dot(a, b):
    return jnp.dot(a.astype(MXU_DTYPE), b.astype(MXU_DTYPE), preferred_element_type=F32)


def _dot_nt(a, b):
    return lax.dot_general(a.astype(MXU_DTYPE), b.astype(MXU_DTYPE), (((1,), (1,)), ((), ())),
                           preferred_element_type=F32)


def _dot_tn(a, b):
    return lax.dot_general(a.astype(MXU_DTYPE), b.astype(MXU_DTYPE), (((0,), (0,)), ((), ())),
                           preferred_element_type=F32)


def _full(shape):
    return pl.BlockSpec(shape, lambda *_: (0,) * len(shape))


def _any():
    return pl.BlockSpec(memory_space=pl.ANY)


def _rowsum(v):
    return jnp.sum(v, axis=0, keepdims=True)


def _mean_last(v):
    return jnp.mean(v, axis=-1, keepdims=True)


class _Comm(NamedTuple):
    operands: tuple
    out_shapes: tuple
    n_sems: int
    make: Callable
    in_place: int = 0


def _hosted_call(body, comm, *, name, grid, in_specs, out_shape, out_specs, scratch_shapes=(), semantics,
                 n_prefetch=0):
    if comm is None:
        return pl.pallas_call(
            body, name=name, out_shape=out_shape, compiler_params=_params(*semantics),
            grid_spec=pltpu.PrefetchScalarGridSpec(
                num_scalar_prefetch=n_prefetch, grid=grid, in_specs=in_specs, out_specs=out_specs,
                scratch_shapes=list(scratch_shapes)))
    n_in, n_out, n_scr = len(in_specs), len(out_shape), len(scratch_shapes)
    k_in, k_out = len(comm.operands), len(comm.out_shapes)

    def hosted(*refs):
        prefetched, refs = refs[:n_prefetch], refs[n_prefetch:]
        ins, refs = refs[:n_in], refs[n_in:]
        c_ins, refs = refs[:k_in], refs[k_in:]
        outs, refs = refs[:n_out], refs[n_out:]
        c_outs, refs = refs[:k_out], refs[k_out:]
        scratch, (send_sems, recv_sems) = refs[:n_scr], refs[n_scr:]
        first, last = None, None
        for d, size in enumerate(grid):
            at_start, at_end = pl.program_id(d) == 0, pl.program_id(d) == size - 1
            first = at_start if first is None else first & at_start
            last = at_end if last is None else last & at_end

        @pl.when(first)
        def _():
            for cp in comm.make(c_ins, c_outs, send_sems, recv_sems)[0]:
                cp.start()

        body(*prefetched, *ins, *outs, *scratch)

        @pl.when(last)
        def _():
            for wait in comm.make(c_ins, c_outs, send_sems, recv_sems)[1]:
                wait()

    aliases = {n_prefetch + n_in + i: n_out + i for i in range(comm.in_place)}
    call = pl.pallas_call(
        hosted, name=name, out_shape=list(out_shape) + list(comm.out_shapes),
        compiler_params=_params(*semantics), input_output_aliases=aliases,
        grid_spec=pltpu.PrefetchScalarGridSpec(
            num_scalar_prefetch=n_prefetch, grid=grid, in_specs=list(in_specs) + [_any()] * k_in,
            out_specs=list(out_specs) + [_any()] * k_out,
            scratch_shapes=list(scratch_shapes) + [pltpu.SemaphoreType.DMA((comm.n_sems,)),
                                                    pltpu.SemaphoreType.DMA((comm.n_sems,))]))
    return lambda *args: call(*args, *comm.operands)


class _Shifted:
    def __init__(self, base, offset):
        self.base, self.offset = base, offset

    @property
    def at(self):
        return self

    def __getitem__(self, k):
        return self.base.at[self.offset + k]


def _merge_in_place(*jobs):
    assert all(j.in_place == len(j.operands) == len(j.out_shapes) for j in jobs)

    def make(ins, outs, send_sems, recv_sems):
        starts, waits, at, sem = [], [], 0, 0
        for j in jobs:
            n = len(j.operands)
            s, w = j.make(ins[at:at + n], outs[at:at + n], _Shifted(send_sems, sem), _Shifted(recv_sems, sem))
            starts, waits, at, sem = starts + s, waits + w, at + n, sem + j.n_sems
        return starts, waits

    operands = tuple(a for j in jobs for a in j.operands)
    return _Comm(operands, tuple(s for j in jobs for s in j.out_shapes), sum(j.n_sems for j in jobs), make,
                 in_place=len(operands))


def _mesh_place():
    x, y, c = lax.axis_index("x"), lax.axis_index("y"), lax.axis_index("c")
    return x, y, c, [(1 - x, y), (x, 1 - y), (1 - x, 1 - y)]


def _gather_job(bufs):
    per = 4

    def make(ins, outs, send_sems, recv_sems):
        del ins
        x, y, c, chips = _mesh_place()
        starts, waits = [], []
        for a, out in enumerate(outs):
            mine = src = out.at[4 * x + 2 * y + c]
            to = [(x, y, 1 - c)] + [(px, py, c) for px, py in chips]
            sends = [pltpu.make_async_remote_copy(
                src_ref=src, dst_ref=mine, send_sem=send_sems.at[per * a + k],
                recv_sem=recv_sems.at[per * a + k], device_id=dev, device_id_type=MESH)
                for k, dev in enumerate(to)]
            recvs = [pltpu.make_async_remote_copy(
                src_ref=src, dst_ref=out.at[4 * px + 2 * py + pc], send_sem=send_sems.at[per * a + k],
                recv_sem=recv_sems.at[per * a + k], device_id=(px, py, pc), device_id_type=MESH)
                for k, (px, py, pc) in enumerate(to)]
            starts += sends
            waits += [s.wait_send for s in sends] + [r.wait_recv for r in recvs]
        return starts, waits

    shapes = tuple(jax.ShapeDtypeStruct(b.shape, b.dtype) for b in bufs)
    return _Comm(tuple(bufs), shapes, per * len(bufs), make, in_place=len(bufs))


def _slots(x, y, c):
    return 4 * x + 2 * y + c, 4 * (1 - x) + 2 * y + c, 4 * x + 2 * (1 - y) + c, 4 * (1 - x) + 2 * (1 - y) + c


def _gather2d_first(halves):
    per = 2

    def make(ins, outs, send_sems, recv_sems):
        x, y, c, _ = _mesh_place()
        me, xn, yn, _ = _slots(x, y, c)
        starts, waits = [], []
        for a, (src, out) in enumerate(zip(ins, outs)):
            blk = src.at[c]
            rows = blk.shape[0] // 2
            upper, lower = pl.ds(0, rows), pl.ds(rows, rows)

            def copy(k, src_ref, dst_ref, dev, a=a):
                return pltpu.make_async_remote_copy(
                    src_ref=src_ref, dst_ref=dst_ref, send_sem=send_sems.at[per * a + k],
                    recv_sem=recv_sems.at[per * a + k], device_id=dev, device_id_type=MESH)

            sends = [copy(0, blk.at[upper], out.at[me, upper], (1 - x, y, c)),
                     copy(1, blk.at[lower], out.at[me, lower], (x, 1 - y, c))]
            recvs = [copy(0, blk.at[upper], out.at[xn, upper], (1 - x, y, c)),
                     copy(1, blk.at[lower], out.at[yn, lower], (x, 1 - y, c))]
            starts += sends
            waits += [s.wait_send for s in sends] + [r.wait_recv for r in recvs]
        return starts, waits

    shapes = tuple(jax.ShapeDtypeStruct((N_DEV,) + h.shape[1:], h.dtype) for h in halves)
    return _Comm(tuple(halves), shapes, per * len(halves), make)


def _gather2d_second(bufs, halves):
    per = 4
    n_arr = len(bufs)

    def make(ins, outs, send_sems, recv_sems):
        x, y, c, _ = _mesh_place()
        me, xn, yn, dg = _slots(x, y, c)
        starts, waits = [], []
        for a, buf in enumerate(outs):
            own = ins[n_arr + a].at[c]
            rows = buf.shape[1] // 2
            upper, lower = pl.ds(0, rows), pl.ds(rows, rows)
            plan = [(own.at[upper], me, upper, (x, 1 - y, c), yn), (buf.at[xn, upper], xn, upper, (x, 1 - y, c), dg),
                    (own.at[lower], me, lower, (1 - x, y, c), xn), (buf.at[yn, lower], yn, lower, (1 - x, y, c), dg)]
            for k, (src, slot, part, dev, landing) in enumerate(plan):
                sems = dict(send_sem=send_sems.at[per * a + k], recv_sem=recv_sems.at[per * a + k],
                            device_id=dev, device_id_type=MESH)
                send = pltpu.make_async_remote_copy(src_ref=src, dst_ref=buf.at[slot, part], **sems)
                arrival = pltpu.make_async_remote_copy(src_ref=src, dst_ref=buf.at[landing, part], **sems)
                starts.append(send)
                waits += [send.wait_send, arrival.wait_recv]
        return starts, waits

    shapes = tuple(jax.ShapeDtypeStruct(b.shape, b.dtype) for b in bufs)
    return _Comm(tuple(bufs) + tuple(halves), shapes, per * n_arr, make, in_place=n_arr)


def _gather_forward(bufs, name):
    n_arr = len(bufs)

    def body(*refs):
        outs = refs[n_arr:2 * n_arr]
        send_sems, recv_sems = refs[2 * n_arr:]
        x, y, c, chips = _mesh_place()
        sends, recvs = [], []
        for a, buf in enumerate(outs):
            for j, (px, py) in enumerate(chips):
                mine, theirs = buf.at[4 * px + 2 * py + c], buf.at[4 * px + 2 * py + 1 - c]
                sems = dict(send_sem=send_sems.at[3 * a + j], recv_sem=recv_sems.at[3 * a + j],
                            device_id=(x, y, 1 - c), device_id_type=MESH)
                sends.append(pltpu.make_async_remote_copy(src_ref=mine, dst_ref=mine, **sems))
                recvs.append(pltpu.make_async_remote_copy(src_ref=mine, dst_ref=theirs, **sems))
        for cp in sends:
            cp.start()
        for s, r in zip(sends, recvs):
            s.wait_send()
            r.wait_recv()

    return pl.pallas_call(
        body, name=name, out_shape=[jax.ShapeDtypeStruct(b.shape, b.dtype) for b in bufs],
        in_specs=[_any()] * n_arr, out_specs=[_any()] * n_arr,
        input_output_aliases={a: a for a in range(n_arr)},
        scratch_shapes=[pltpu.SemaphoreType.DMA((3 * n_arr,)), pltpu.SemaphoreType.DMA((3 * n_arr,))],
    )(*bufs)


def _scatter_job(chip_sums, rows=None, into=()):
    n_into = len(into)

    def part(ref):
        return ref if rows is None else ref.at[pl.ds(rows[0], rows[1])]

    def make(ins, outs, send_sems, recv_sems):
        x, y, c, chips = _mesh_place()
        copies = [pltpu.make_async_remote_copy(
            src_ref=part(src.at[2 * px + py]), dst_ref=part(out.at[j]), send_sem=send_sems.at[3 * a + j],
            recv_sem=recv_sems.at[3 * a + j], device_id=(px, py, c), device_id_type=MESH)
            for a, (src, out) in enumerate(zip(ins[n_into:], outs)) for j, (px, py) in enumerate(chips)]
        return copies, [cp.wait for cp in copies]

    shapes = tuple(jax.ShapeDtypeStruct((3,) + s.shape[1:], s.dtype) for s in chip_sums)
    return _Comm(tuple(into) + tuple(chip_sums), shapes, 3 * len(chip_sums), make, in_place=n_into)


def _all_gather8(blocks, name, split=False, forward=(), riders=(), skip_own=()):
    n_arr, n_fwd = len(blocks), len(forward)
    splits = list(split) if isinstance(split, (list, tuple)) else [split] * n_arr
    own_slots = [a not in skip_own for a in range(n_arr)]
    rider_in = sum(len(r.operands) for r in riders)
    rider_out = sum(len(r.out_shapes) for r in riders)

    def body(*refs):
        x_refs, refs = refs[:n_arr], refs[n_arr + n_fwd:]
        r_ins, refs = refs[:rider_in], refs[rider_in:]
        out_refs, refs = refs[:n_arr], refs[n_arr:]
        fwd_refs, refs = refs[:n_fwd], refs[n_fwd:]
        r_outs, refs = refs[:rider_out], refs[rider_out:]
        (send_sems, recv_sems, local_sems), rider_sems = refs[:3], refs[3:]
        x, y, c, chips = _mesh_place()
        me, sibling = (x, y, c), (x, y, 1 - c)
        rider_waits, i0, o0 = [], 0, 0
        for n, job in enumerate(riders):
            k_in, k_out = len(job.operands), len(job.out_shapes)
            starts, waits = job.make(r_ins[i0:i0 + k_in], r_outs[o0:o0 + k_out],
                                     rider_sems[2 * n], rider_sems[2 * n + 1])
            for cp in starts:
                cp.start()
            rider_waits += waits
            i0, o0 = i0 + k_in, o0 + k_out
        passing = []
        for f, buf in enumerate(fwd_refs):
            for j, (px, py) in enumerate(chips):
                mine, theirs = buf.at[4 * px + 2 * py + c], buf.at[4 * px + 2 * py + 1 - c]
                sems = dict(send_sem=send_sems.at[7 * n_arr + 3 * f + j], recv_sem=recv_sems.at[7 * n_arr + 3 * f + j],
                            device_id=sibling, device_id_type=MESH)
                passing.append((pltpu.make_async_remote_copy(src_ref=mine, dst_ref=mine, **sems),
                                pltpu.make_async_remote_copy(src_ref=mine, dst_ref=theirs, **sems)))
        for send, _ in passing:
            send.start()
        arrays = []
        for a, (x_ref, out_ref) in enumerate(zip(x_refs, out_refs)):
            src_mine = x_ref.at[c] if splits[a] else x_ref

            def copy(k, blk, to, src=None, a=a, out_ref=out_ref):
                dst = out_ref.at[4 * blk[0] + 2 * blk[1] + blk[2]]
                return pltpu.make_async_remote_copy(
                    src_ref=dst if src is None else src, dst_ref=dst,
                    send_sem=send_sems.at[7 * a + k], recv_sem=recv_sems.at[7 * a + k],
                    device_id=to, device_id_type=MESH)

            mine = pltpu.make_async_copy(src_mine, out_ref.at[4 * x + 2 * y + c], local_sems.at[a])
            first = [copy(0, me, sibling, src=src_mine)] if own_slots[a] else []
            first += [copy(1 + j, me, (*chip, c), src=src_mine) for j, chip in enumerate(chips)]
            for cp in first + ([mine] if own_slots[a] else []):
                cp.start()
            arrays.append((copy, mine, first, own_slots[a]))
        sent = []
        for copy, mine, first, own in arrays:
            passed = [copy(4 + j, (*chip, c), sibling) for j, chip in enumerate(chips)]
            for j, chip in enumerate(chips):
                copy(1 + j, (*chip, c), me).wait_recv()
                passed[j].start()
            sent += first + passed
        for copy, mine, first, own in arrays:
            if own:
                copy(0, sibling, me).wait_recv()
                mine.wait()
            for j, chip in enumerate(chips):
                copy(4 + j, (*chip, 1 - c), me).wait_recv()
        for cp in sent:
            cp.wait_send()
        for send, arrival in passing:
            send.wait_send()
            arrival.wait_recv()
        for wait in rider_waits:
            wait()

    n_sems = 7 * n_arr + 3 * n_fwd
    rider_operands = [a for r in riders for a in r.operands]
    rider_shapes = [s for r in riders for s in r.out_shapes]
    return pl.pallas_call(
        body, name=name,
        out_shape=[jax.ShapeDtypeStruct((N_DEV,) + tuple(b.shape[1:] if s else b.shape), b.dtype)
                   for b, s in zip(blocks, splits)]
        + [jax.ShapeDtypeStruct(f.shape, f.dtype) for f in forward] + rider_shapes,
        in_specs=[_any()] * (n_arr + n_fwd + rider_in), out_specs=[_any()] * (n_arr + n_fwd + rider_out),
        input_output_aliases={n_arr + f: n_arr + f for f in range(n_fwd)},
        scratch_shapes=[pltpu.SemaphoreType.DMA((n_sems,)), pltpu.SemaphoreType.DMA((n_sems,)),
                        pltpu.SemaphoreType.DMA((n_arr,))]
        + [pltpu.SemaphoreType.DMA((r.n_sems,)) for r in riders for _ in range(2)],
    )(*blocks, *forward, *rider_operands)


def _share_job(bufs):
    def make(ins, outs, send_sems, recv_sems):
        del ins
        x, y, c, _ = _mesh_place()
        sems = lambda a: dict(send_sem=send_sems.at[a], recv_sem=recv_sems.at[a],
                              device_id=(x, y, 1 - c), device_id_type=MESH)
        sends = [pltpu.make_async_remote_copy(src_ref=o.at[c], dst_ref=o.at[c], **sems(a)) for a, o in enumerate(outs)]
        arrivals = [pltpu.make_async_remote_copy(src_ref=o.at[c], dst_ref=o.at[1 - c], **sems(a))
                    for a, o in enumerate(outs)]
        return sends, [s.wait_send for s in sends] + [r.wait_recv for r in arrivals]

    shapes = tuple(jax.ShapeDtypeStruct(b.shape, b.dtype) for b in bufs)
    return _Comm(tuple(bufs), shapes, len(bufs), make, in_place=len(bufs))


def _sibling_share(bufs, name):
    n_arr = len(bufs)

    def body(*refs):
        out_refs = refs[n_arr:2 * n_arr]
        send_sems, recv_sems = refs[2 * n_arr:]
        x, y, c = lax.axis_index("x"), lax.axis_index("y"), lax.axis_index("c")
        copies = [pltpu.make_async_remote_copy(
            src_ref=out_refs[a].at[c], dst_ref=out_refs[a].at[c],
            send_sem=send_sems.at[a], recv_sem=recv_sems.at[a],
            device_id=(x, y, 1 - c), device_id_type=MESH) for a in range(n_arr)]
        for cp in copies:
            cp.start()
        for a in range(n_arr):
            pltpu.make_async_remote_copy(
                src_ref=out_refs[a].at[c], dst_ref=out_refs[a].at[1 - c],
                send_sem=send_sems.at[a], recv_sem=recv_sems.at[a],
                device_id=(x, y, 1 - c), device_id_type=MESH).wait()

    return pl.pallas_call(
        body, name=name,
        out_shape=[jax.ShapeDtypeStruct(b.shape, b.dtype) for b in bufs],
        in_specs=[_any()] * n_arr, out_specs=[_any()] * n_arr,
        input_output_aliases={a: a for a in range(n_arr)},
        scratch_shapes=[pltpu.SemaphoreType.DMA((n_arr,)), pltpu.SemaphoreType.DMA((n_arr,))],
    )(*bufs)


def _gelu_tanh(z):
    k = math.sqrt(2.0 / math.pi)
    t = jnp.tanh(k * (z + 0.044715 * (z * z * z)))
    return 0.5 * z * (1.0 + t), t


def _gelu_tanh_grad(z, t):
    k = math.sqrt(2.0 / math.pi)
    return 0.5 * (1.0 + t) + 0.5 * z * (1.0 - t * t) * (k * (1.0 + 3.0 * 0.044715 * (z * z)))


def _rope_angle_kernel(pos_row, invf_col):
    seq = pos_row.shape[1]

    def body(p_ref, f_ref, cos_ref, sin_ref):
        ang = p_ref[...].astype(F32) * f_ref[...]
        cos_ref[...] = jnp.cos(ang)
        sin_ref[...] = jnp.sin(ang)

    return pl.pallas_call(
        body, name="rope_angles", grid=(1,), out_shape=[jax.ShapeDtypeStruct((ROT_DIM // 2, seq), F32)] * 2,
        in_specs=[_full((1, seq)), _full((ROT_DIM // 2, 1))], out_specs=[_full((ROT_DIM // 2, seq))] * 2,
        compiler_params=_params("arbitrary"),
    )(pos_row, invf_col)


def _rope_lane_tables(cos, sin):
    cos_t, sin_t = cos.T, sin.T
    seq, half = cos_t.shape
    ones = jnp.ones((seq, HEAD_DIM - ROT_DIM), F32)
    c64 = jnp.concatenate([cos_t, cos_t, ones], axis=1)
    s1 = jnp.concatenate([sin_t, jnp.zeros((seq, HEAD_DIM - half), F32)], axis=1)
    s2 = jnp.concatenate([jnp.zeros((seq, half), F32), sin_t, jnp.zeros((seq, HEAD_DIM - ROT_DIM), F32)], axis=1)
    return jnp.concatenate([jnp.tile(t, (1, LANES // HEAD_DIM)) for t in (c64, s1, s2)], axis=1)


def _rope_apply(t, tab, sign):
    reps = t.shape[1] // LANES
    c_tab, s1, s2 = (jnp.tile(tab[:, LANES * k:LANES * (k + 1)], (1, reps)) if reps > 1
                     else tab[:, LANES * k:LANES * (k + 1)] for k in range(3))
    half = ROT_DIM // 2
    up = pltpu.roll(t, t.shape[1] - half, 1)
    down = pltpu.roll(t, half, 1)
    return t * c_tab + sign * (down * s2 - up * s1)


def _lane_masks(shape):
    lane = lax.broadcasted_iota(jnp.int32, shape, 1)
    return lane < HEAD_DIM, lane >= HEAD_DIM


HEADS_PER_GROUP = N_Q_HEADS // N_KV_HEADS
ATTN_SCALE = 1.0 / math.sqrt(HEAD_DIM)


def _attn_bias_t(first_block):
    kj = lax.broadcasted_iota(jnp.int32, (2 * CHUNK, CHUNK), 0)
    qi = lax.broadcasted_iota(jnp.int32, (2 * CHUNK, CHUNK), 1)
    ok = (kj > qi) & (kj <= qi + CHUNK)
    if first_block is not None:
        ok = ok & (jnp.logical_not(first_block) | (kj >= CHUNK))
    return jnp.tile(jnp.where(ok, 0.0, -jnp.inf), (1, HEADS_PER_GROUP))


def _group_rows(x, g, lo, hi):
    rows = []
    for r in range(HEADS_PER_GROUP):
        h = HEADS_PER_GROUP * g + r
        pair = x[:, LANES * (h // 2):LANES * (h // 2 + 1)]
        rows.append(jnp.where(hi if h % 2 else lo, pair, 0.0))
    return jnp.concatenate(rows, axis=0)


def _pairs_from_rows(rows, lo):
    return [jnp.where(lo, rows[2 * CHUNK * k:2 * CHUNK * k + CHUNK], rows[2 * CHUNK * k + CHUNK:2 * CHUNK * (k + 1)])
            for k in range(HEADS_PER_GROUP // 2)]


def _group_dup(a, b, g, lo2):
    return jnp.where(lo2, a, b) if g == 0 else jnp.where(lo2, b, a)


def _sink_row(sink_ref, g):
    return jnp.concatenate([sink_ref[HEADS_PER_GROUP * g + r:HEADS_PER_GROUP * g + r + 1, :]
                            for r in range(HEADS_PER_GROUP)], axis=1)


def _attn_probs_t(k_dup, q_rows, bias_t, sink_row):
    s_t = _dot_nt(k_dup, q_rows) * ATTN_SCALE + bias_t
    m = jnp.maximum(jnp.max(s_t, axis=0, keepdims=True), sink_row)
    p = jnp.exp(s_t - m)
    e_sink = jnp.exp(sink_row - m)
    inv = 1.0 / (jnp.sum(p, axis=0, keepdims=True) + e_sink)
    return p * inv, e_sink * inv


def _sgu_forward_pair(wm, vp, j):
    lo, hi = _lane_masks(vp.shape)
    lhs = jnp.concatenate([wm[2 * j], wm[2 * j + 1]], axis=1)
    rhs = jnp.concatenate([jnp.where(lo, vp, 0.0), jnp.where(hi, vp, 0.0)], axis=0)
    return _dot(lhs, rhs)


def _masked_spatial(w_ref):
    t = lax.broadcasted_iota(jnp.int32, (CHUNK, CHUNK), 0)
    s = lax.broadcasted_iota(jnp.int32, (CHUNK, CHUNK), 1)
    tril = s <= t
    return [jnp.where(tril, w_ref[g], 0.0) for g in range(GMLP_GROUPS)], tril, s >= t


def _mod_kernel(c_all, w_shard, b_shard, comm=None):
    n = w_shard.shape[1]
    tn = 512

    def body(c_ref, w_ref, b_ref, mod_ref, act_ref):
        cv = c_ref[...]
        act = cv * (1.0 / (1.0 + jnp.exp(-cv)))
        act_ref[...] = act
        mod_ref[...] = _dot(act, w_ref[...]) + b_ref[...]

    return _hosted_call(
        body, comm, name="ada_mod", grid=(n // tn,),
        out_shape=[jax.ShapeDtypeStruct((N_DEV, n), F32), jax.ShapeDtypeStruct((N_DEV, D_MODEL), F32)],
        in_specs=[_full((N_DEV, D_MODEL)), pl.BlockSpec((D_MODEL, tn), lambda i: (0, i)),
                  pl.BlockSpec((1, tn), lambda i: (0, i))],
        out_specs=[pl.BlockSpec((N_DEV, tn), lambda i: (0, i)), _full((N_DEV, D_MODEL))],
        semantics=("arbitrary",),
    )(c_all, w_shard, b_shard)


def _load_chip_blocks(chip_ref, gathered, local, dsts, sems, first_sem=0):
    for k, dst in enumerate(dsts):
        @pl.when(chip_ref[0] == k)
        def _():
            pltpu.make_async_copy(local, dst, sems.at[first_sem + k]).start()

        @pl.when(chip_ref[0] != k)
        def _():
            pltpu.make_async_copy(gathered.at[k], dst, sems.at[first_sem + k]).start()
    return [pltpu.make_async_copy(local, dst, sems.at[first_sem + k]).wait for k, dst in enumerate(dsts)]


def _in_proj_kernel(x, vecs, w_in_t, comm=None):
    seq = x.shape[0]
    tm = 512

    def body(x_ref, v_ref, w_ref, proj_ref, h_ref):
        xv = x_ref[...]
        rstd = lax.rsqrt(_mean_last(xv * xv) + EPS)
        n1 = (xv * rstd) * v_ref[0:1, :]
        h = n1 * (1.0 + v_ref[2:3, :]) + v_ref[1:2, :]
        hb = h.astype(MXU_DTYPE)
        h_ref[...] = hb
        proj_ref[...] = _dot_nt(hb, w_ref[...])

    return _hosted_call(
        body, comm, name="in_proj", grid=(seq // tm,),
        out_shape=[jax.ShapeDtypeStruct((seq, IN_PROJ_WIDTH), F32),
                   jax.ShapeDtypeStruct((seq, D_MODEL), MXU_DTYPE)],
        in_specs=[pl.BlockSpec((tm, D_MODEL), lambda i: (i, 0)), _full((8, D_MODEL)),
                  _full((IN_PROJ_WIDTH, D_MODEL))],
        out_specs=[pl.BlockSpec((tm, IN_PROJ_WIDTH), lambda i: (i, 0)),
                   pl.BlockSpec((tm, D_MODEL), lambda i: (i, 0))],
        semantics=("arbitrary",),
    )(x, vecs, w_in_t)


MIXER_BLOCKS_PER_STEP = 4
KV_START = 2 * GMLP_WIDTH + ATTN_WIDTH


def _mixer_fwd_kernel(proj, rope_tab, w_spatial, bias_full, sink_rows, comm=None):
    seq = proj.shape[0]
    per = MIXER_BLOCKS_PER_STEP
    steps = seq // (CHUNK * per)
    kv_col = KV_START // (2 * KV_WIDTH)

    def body(proj_ref, prev_ref, tab_ref, ptab_ref, w_ref, bias_ref, sink_ref, cat_ref):
        i = pl.program_id(0)
        wm, _, _ = _masked_spatial(w_ref)
        lo, hi = _lane_masks((CHUNK, LANES))
        lo2, _ = _lane_masks((2 * CHUNK, LANES))
        o = 2 * GMLP_WIDTH
        for s in range(per):
            rows, before = slice(CHUNK * s, CHUNK * (s + 1)), slice(CHUNK * (s - 1), CHUNK * s)
            for j in range(GMLP_GROUPS // 2):
                cols = slice(LANES * j, LANES * (j + 1))
                vcols = slice(GMLP_WIDTH + LANES * j, GMLP_WIDTH + LANES * (j + 1))
                u, _ = _gelu_tanh(proj_ref[rows, cols])
                vp, _ = _gelu_tanh(proj_ref[rows, vcols])
                sv = _sgu_forward_pair(wm, vp, j) + bias_ref[:, cols]
                cat_ref[rows, cols] = (u * sv).astype(cat_ref.dtype)
            tab = tab_ref[rows, :]
            if s == 0:
                prev_kv, prev_tab, first = prev_ref[...], ptab_ref[...], i == 0
            else:
                prev_kv, prev_tab, first = proj_ref[before, KV_START:KV_START + 2 * KV_WIDTH], tab_ref[before, :], None
            q_r = _rope_apply(proj_ref[rows, o:o + ATTN_WIDTH], tab, 1.0)
            k_cur = _rope_apply(proj_ref[rows, KV_START:KV_START + KV_WIDTH], tab, 1.0)
            k_prev = _rope_apply(prev_kv[:, 0:KV_WIDTH], prev_tab, 1.0)
            k_a = jnp.concatenate([k_prev, k_cur], axis=0)
            v_a = jnp.concatenate([prev_kv[:, KV_WIDTH:2 * KV_WIDTH],
                                   proj_ref[rows, KV_START + KV_WIDTH:KV_START + 2 * KV_WIDTH]], axis=0)
            k_b = pltpu.roll(k_a, HEAD_DIM, 1)
            v_b = pltpu.roll(v_a, HEAD_DIM, 1)
            bias_t = _attn_bias_t(first)
            for g in range(N_KV_HEADS):
                p_t, _ = _attn_probs_t(_group_dup(k_a, k_b, g, lo2), _group_rows(q_r, g, lo, hi), bias_t,
                                       _sink_row(sink_ref, g))
                o_t = _dot(_group_dup(v_a, v_b, g, lo2).T, p_t)
                for k, pair in enumerate(_pairs_from_rows(o_t.T, lo)):
                    c0 = GMLP_WIDTH + LANES * (2 * g + k)
                    cat_ref[rows, c0:c0 + LANES] = pair.astype(cat_ref.dtype)

    return _hosted_call(
        body, comm, name="mixer_fwd", grid=(steps,),
        out_shape=[jax.ShapeDtypeStruct((seq, D_MODEL), MXU_DTYPE)],
        in_specs=[pl.BlockSpec((CHUNK * per, IN_PROJ_WIDTH), lambda i: (i, 0)),
                  pl.BlockSpec((CHUNK, 2 * KV_WIDTH), lambda i: (jnp.maximum(per * i - 1, 0), kv_col)),
                  pl.BlockSpec((CHUNK * per, 3 * LANES), lambda i: (i, 0)),
                  pl.BlockSpec((CHUNK, 3 * LANES), lambda i: (jnp.maximum(per * i - 1, 0), 0)),
                  _full((GMLP_GROUPS, CHUNK, CHUNK)), _full((CHUNK, GMLP_WIDTH)),
                  _full((N_Q_HEADS, LANES))],
        out_specs=[pl.BlockSpec((CHUNK * per, D_MODEL), lambda i: (i, 0))],
        semantics=("arbitrary",),
    )(proj, proj, rope_tab, rope_tab, w_spatial, bias_full, sink_rows)


def _trunk_kernel(x, target, cat, vecs, chip_idx, gathered, local):
    seq = x.shape[0]
    tm = 256
    nj = D_FF // D_MODEL
    out_rows = D_MODEL // N_CHIPS

    def body(chip_ref, x_ref, t_ref, cat_ref, v_ref, g_out, g_w1, g_w2, l_out, l_w1, l_w2,
             dx1_ref, dcat_ref, dmix_ref, h2_ref, r_ref, da_ref, dff_ref, sums_ref,
             wout, w1, w2, a_scr, sem):
        i = pl.program_id(0)

        @pl.when(i == 0)
        def _():
            waits = _load_chip_blocks(chip_ref, g_out, l_out,
                                      [wout.at[pl.ds(out_rows * k, out_rows)] for k in range(N_CHIPS)], sem)
            waits += _load_chip_blocks(chip_ref, g_w1, l_w1, [w1.at[k] for k in range(N_CHIPS)], sem, N_CHIPS)
            waits += _load_chip_blocks(chip_ref, g_w2, l_w2, [w2.at[k] for k in range(N_CHIPS)], sem, 2 * N_CHIPS)
            for wait in waits:
                wait()
            sums_ref[...] = jnp.zeros_like(sums_ref)

        gate1, shift2, scale2 = v_ref[0:1, :], v_ref[1:2, :], v_ref[2:3, :]
        gate2, g_ffn, g_final = v_ref[3:4, :], v_ref[4:5, :], v_ref[5:6, :]

        mix = _dot(cat_ref[...], wout[...])
        x1 = x_ref[...] + gate1 * mix
        rstd2 = lax.rsqrt(_mean_last(x1 * x1) + EPS)
        xh2 = x1 * rstd2
        n2 = xh2 * g_ffn
        h2b = (n2 * (1.0 + scale2) + shift2).astype(MXU_DTYPE)
        h2_ref[...] = h2b
        ff = jnp.zeros((tm, D_MODEL), F32)
        for j in range(nj):
            a = _dot(h2b, w1[j])
            a_scr[j] = a
            relu = jnp.maximum(a, 0.0)
            rb = (relu * relu).astype(MXU_DTYPE)
            r_ref[:, D_MODEL * j:D_MODEL * (j + 1)] = rb
            ff = ff + _dot(rb, w2[j])
        x2 = x1 + gate2 * ff
        rstd3 = lax.rsqrt(_mean_last(x2 * x2) + EPS)
        xh3 = x2 * rstd3
        err = xh3 * g_final - t_ref[...]
        loss = 0.5 * _rowsum(_mean_last(err * err))
        dy = err * (1.0 / D_MODEL)
        dxh3 = dy * g_final
        dx2 = rstd3 * (dxh3 - xh3 * _mean_last(dxh3 * xh3))
        dffb = (dx2 * gate2).astype(MXU_DTYPE)
        dff_ref[...] = dffb
        dh2 = jnp.zeros((tm, D_MODEL), F32)
        for j in range(nj):
            dr = _dot_nt(dffb, w2[j])
            dab = (dr * (2.0 * jnp.maximum(a_scr[j], 0.0))).astype(MXU_DTYPE)
            da_ref[:, D_MODEL * j:D_MODEL * (j + 1)] = dab
            dh2 = dh2 + _dot_nt(dab, w1[j])
        dn2 = dh2 * (1.0 + scale2)
        dxh2 = dn2 * g_ffn
        dx1 = dx2 + rstd2 * (dxh2 - xh2 * _mean_last(dxh2 * xh2))
        dx1_ref[...] = dx1
        dmixb = (dx1 * gate1).astype(MXU_DTYPE)
        dmix_ref[...] = dmixb
        dcat_ref[...] = _dot_nt(dmixb, wout[...])

        sums_ref[0:1, :] += _rowsum(dh2)
        sums_ref[1:2, :] += _rowsum(dh2 * n2)
        sums_ref[2:3, :] += _rowsum(dx2 * ff)
        sums_ref[3:4, :] += _rowsum(dn2 * xh2)
        sums_ref[4:5, :] += _rowsum(dy * xh3)
        sums_ref[5:6, :] += _rowsum(dx1 * mix)
        sums_ref[6:7, :] += jnp.broadcast_to(loss, (1, D_MODEL))

    tok = lambda w: pl.BlockSpec((tm, w), lambda i, chip: (i, 0))
    return _hosted_call(
        body, None, name="trunk", grid=(seq // tm,), n_prefetch=1,
        out_shape=[jax.ShapeDtypeStruct((seq, D_MODEL), F32), jax.ShapeDtypeStruct((seq, D_MODEL), F32),
                   jax.ShapeDtypeStruct((seq, D_MODEL), MXU_DTYPE), jax.ShapeDtypeStruct((seq, D_MODEL), MXU_DTYPE),
                   jax.ShapeDtypeStruct((seq, D_FF), MXU_DTYPE), jax.ShapeDtypeStruct((seq, D_FF), MXU_DTYPE),
                   jax.ShapeDtypeStruct((seq, D_MODEL), MXU_DTYPE), jax.ShapeDtypeStruct((8, D_MODEL), F32)],
        in_specs=[tok(D_MODEL), tok(D_MODEL), tok(D_MODEL), _full((8, D_MODEL))] + [_any()] * 6,
        out_specs=[tok(D_MODEL), tok(D_MODEL), tok(D_MODEL), tok(D_MODEL), tok(D_FF), tok(D_FF), tok(D_MODEL),
                   _full((8, D_MODEL))],
        scratch_shapes=[pltpu.VMEM((D_MODEL, D_MODEL), MXU_DTYPE), pltpu.VMEM((nj, D_MODEL, D_MODEL), MXU_DTYPE),
                        pltpu.VMEM((nj, D_MODEL, D_MODEL), MXU_DTYPE), pltpu.VMEM((nj, tm, D_MODEL), F32),
                        pltpu.SemaphoreType.DMA((3 * N_CHIPS,))],
        semantics=("arbitrary",),
    )(chip_idx, x, target, cat, vecs, *gathered, *local)


def _mixer_bwd_kernel(proj, rope_tab, dcat, w_spatial, w_spatial_t, bias_full, sink_rows, dev_idx, comm=None):
    seq = proj.shape[0]
    per = MIXER_BLOCKS_PER_STEP
    steps = seq // (CHUNK * per)
    kv_col = KV_START // (2 * KV_WIDTH)

    def body(dev_ref, proj_ref, prev_ref, tab_ref, ptab_ref, dcat_ref, w_ref, wt_ref, bias_ref, sink_ref,
             dproj_ref, dw_out, db_ref, dsink_ref, carry, dw_ref):
        del dev_ref
        step = pl.program_id(0)

        @pl.when(step == 0)
        def _():
            carry[...] = jnp.zeros_like(carry)
            dw_ref[...] = jnp.zeros_like(dw_ref)
            db_ref[...] = jnp.zeros_like(db_ref)
            dsink_ref[...] = jnp.zeros_like(dsink_ref)

        for s in reversed(range(per)):
            rows = pl.ds(CHUNK * s, CHUNK)
            if s == 0:
                before, before_tab, first = prev_ref, ptab_ref, step == steps - 1
            else:
                before = proj_ref.at[pl.ds(CHUNK * (s - 1), CHUNK), pl.ds(KV_START, 2 * KV_WIDTH)]
                before_tab, first = tab_ref.at[pl.ds(CHUNK * (s - 1), CHUNK)], None
            one_block(proj_ref.at[rows], before, tab_ref.at[rows], before_tab, dcat_ref.at[rows], w_ref, wt_ref,
                      bias_ref, sink_ref, dproj_ref.at[rows], dw_ref, db_ref, dsink_ref, carry, first)

        @pl.when(step == steps - 1)
        def _():
            dw_out[...] = dw_ref[...].astype(dw_out.dtype)

    def one_block(proj_ref, prev_ref, tab_ref, ptab_ref, dcat_ref, w_ref, wt_ref, bias_ref, sink_ref,
                  dproj_ref, dw_ref, db_ref, dsink_ref, carry, first):
        wm, tril, triu = _masked_spatial(w_ref)
        lo, hi = _lane_masks((CHUNK, LANES))
        lane = lax.broadcasted_iota(jnp.int32, (CHUNK, LANES), 1)
        db = jnp.zeros((CHUNK, LANES), F32)
        for j in range(GMLP_GROUPS // 2):
            cols = slice(LANES * j, LANES * (j + 1))
            vcols = slice(GMLP_WIDTH + LANES * j, GMLP_WIDTH + LANES * (j + 1))
            zu, zv = proj_ref[:, cols], proj_ref[:, vcols]
            u, tu = _gelu_tanh(zu)
            vp, tv = _gelu_tanh(zv)
            sv = _sgu_forward_pair(wm, vp, j) + bias_ref[:, cols]
            dout = dcat_ref[:, cols]
            du = dout * sv
            dsv = dout * u
            dsv_lo, dsv_hi = jnp.where(lo, dsv, 0.0), jnp.where(hi, dsv, 0.0)
            lhs_t = jnp.concatenate([jnp.where(triu, wt_ref[2 * j], 0.0),
                                     jnp.where(triu, wt_ref[2 * j + 1], 0.0)], axis=1)
            dv = _dot(lhs_t, jnp.concatenate([dsv_lo, dsv_hi], axis=0))
            dw_ref[2 * j] += jnp.where(tril, _dot_nt(dsv_lo, vp), 0.0)
            dw_ref[2 * j + 1] += jnp.where(tril, _dot_nt(dsv_hi, vp), 0.0)
            db = db + (jnp.where(lane == 2 * j, jnp.sum(dsv_lo, axis=1, keepdims=True), 0.0)
                       + jnp.where(lane == 2 * j + 1, jnp.sum(dsv_hi, axis=1, keepdims=True), 0.0))
            dproj_ref[:, cols] = (du * _gelu_tanh_grad(zu, tu)).astype(dproj_ref.dtype)
            dproj_ref[:, vcols] = (dv * _gelu_tanh_grad(zv, tv)).astype(dproj_ref.dtype)
        db_ref[...] += db
        o = 2 * GMLP_WIDTH
        tab = tab_ref[...]
        q_r = _rope_apply(proj_ref[:, o:o + ATTN_WIDTH], tab, 1.0)
        k_cur = _rope_apply(proj_ref[:, o + ATTN_WIDTH:o + ATTN_WIDTH + KV_WIDTH], tab, 1.0)
        k_prev = _rope_apply(prev_ref[:, 0:KV_WIDTH], ptab_ref[...], 1.0)
        k_a = jnp.concatenate([k_prev, k_cur], axis=0)
        v_a = jnp.concatenate([prev_ref[:, KV_WIDTH:2 * KV_WIDTH],
                               proj_ref[:, o + ATTN_WIDTH + KV_WIDTH:o + ATTN_WIDTH + 2 * KV_WIDTH]], axis=0)
        k_b = pltpu.roll(k_a, HEAD_DIM, 1)
        v_b = pltpu.roll(v_a, HEAD_DIM, 1)
        bias_t = _attn_bias_t(first)
        lo2, _ = _lane_masks((2 * CHUNK, LANES))
        dout_b = dcat_ref[:, GMLP_WIDTH:GMLP_WIDTH + ATTN_WIDTH]
        dk_tot, dv_tot, dq_pairs = [], [], []
        for g in range(N_KV_HEADS):
            k_dup, v_dup = _group_dup(k_a, k_b, g, lo2), _group_dup(v_a, v_b, g, lo2)
            q_rows = _group_rows(q_r, g, lo, hi)
            do_rows = _group_rows(dout_b, g, lo, hi)
            p_t, p_sink = _attn_probs_t(k_dup, q_rows, bias_t, _sink_row(sink_ref, g))
            dp_t = _dot_nt(v_dup, do_rows)
            delta = jnp.sum(p_t * dp_t, axis=0, keepdims=True)
            ds_t = p_t * (dp_t - delta) * ATTN_SCALE
            dsink = -p_sink * delta
            for r in range(HEADS_PER_GROUP):
                h = HEADS_PER_GROUP * g + r
                dsink_ref[h:h + 1, :] += jnp.broadcast_to(
                    jnp.sum(dsink[:, LANES * r:LANES * (r + 1)], axis=1, keepdims=True), (1, LANES))
            dk_full = _dot(ds_t, q_rows)
            dv_full = _dot(p_t, do_rows)
            dk_tot.append(dk_full + pltpu.roll(dk_full, HEAD_DIM, 1))
            dv_tot.append(dv_full + pltpu.roll(dv_full, HEAD_DIM, 1))
            dq_t = _dot(k_dup.T, ds_t)
            dq_pairs += _pairs_from_rows(dq_t.T, lo)
        dk_all = jnp.where(lo2, dk_tot[0], dk_tot[1])
        dv_all = jnp.where(lo2, dv_tot[0], dv_tot[1])
        dk_cur = dk_all[CHUNK:, :] + carry[:, 0:KV_WIDTH]
        dv_cur = dv_all[CHUNK:, :] + carry[:, KV_WIDTH:2 * KV_WIDTH]
        carry[:, 0:KV_WIDTH] = dk_all[:CHUNK, :]
        carry[:, KV_WIDTH:2 * KV_WIDTH] = dv_all[:CHUNK, :]
        dq = _rope_apply(jnp.concatenate(dq_pairs, axis=1), tab, -1.0)
        dproj_ref[:, o:o + ATTN_WIDTH] = dq.astype(dproj_ref.dtype)
        dproj_ref[:, o + ATTN_WIDTH:o + ATTN_WIDTH + KV_WIDTH] = (
            _rope_apply(dk_cur, tab, -1.0).astype(dproj_ref.dtype))
        dproj_ref[:, o + ATTN_WIDTH + KV_WIDTH:o + ATTN_WIDTH + 2 * KV_WIDTH] = dv_cur.astype(dproj_ref.dtype)

    rev = lambda i: steps - 1 - i
    before = lambda i: jnp.maximum(per * rev(i) - 1, 0)
    slot = lambda shape: pl.BlockSpec((None,) + shape, lambda i, d: (d[0],) + (0,) * len(shape))
    return _hosted_call(
        body, comm, name="mixer_bwd", grid=(steps,), n_prefetch=1,
        out_shape=[jax.ShapeDtypeStruct((seq, IN_PROJ_WIDTH), MXU_DTYPE),
                   jax.ShapeDtypeStruct((N_DEV, GMLP_GROUPS, CHUNK, CHUNK), GRAD_COMM_DTYPE),
                   jax.ShapeDtypeStruct((N_DEV, CHUNK, LANES), F32),
                   jax.ShapeDtypeStruct((N_DEV, N_Q_HEADS, LANES), F32)],
        in_specs=[pl.BlockSpec((CHUNK * per, IN_PROJ_WIDTH), lambda i, d: (rev(i), 0)),
                  pl.BlockSpec((CHUNK, 2 * KV_WIDTH), lambda i, d: (before(i), kv_col)),
                  pl.BlockSpec((CHUNK * per, 3 * LANES), lambda i, d: (rev(i), 0)),
                  pl.BlockSpec((CHUNK, 3 * LANES), lambda i, d: (before(i), 0)),
                  pl.BlockSpec((CHUNK * per, D_MODEL), lambda i, d: (rev(i), 0)),
                  _full((GMLP_GROUPS, CHUNK, CHUNK)), _full((GMLP_GROUPS, CHUNK, CHUNK)),
                  _full((CHUNK, GMLP_WIDTH)), _full((N_Q_HEADS, LANES))],
        out_specs=[pl.BlockSpec((CHUNK * per, IN_PROJ_WIDTH), lambda i, d: (rev(i), 0)),
                   slot((GMLP_GROUPS, CHUNK, CHUNK)), slot((CHUNK, LANES)), slot((N_Q_HEADS, LANES))],
        scratch_shapes=[pltpu.VMEM((CHUNK, 2 * KV_WIDTH), F32), pltpu.VMEM((GMLP_GROUPS, CHUNK, CHUNK), F32)],
        semantics=("arbitrary",),
    )(dev_idx, proj, proj, rope_tab, rope_tab, dcat, w_spatial, w_spatial_t, bias_full, sink_rows)


def _in_proj_bwd_kernel(x, dx1, dproj, vecs, w_in_t, comm=None):
    seq = x.shape[0]
    tm = 512

    def body(x_ref, dx1_ref, dp_ref, v_ref, w_ref, gx_ref, sums_ref):
        @pl.when(pl.program_id(0) == 0)
        def _():
            sums_ref[...] = jnp.zeros_like(sums_ref)

        g_mix, scale1 = v_ref[0:1, :], v_ref[2:3, :]
        dh = _dot(dp_ref[...], w_ref[...])
        xv = x_ref[...]
        rstd = lax.rsqrt(_mean_last(xv * xv) + EPS)
        xh = xv * rstd
        dn1 = dh * (1.0 + scale1)
        dxh = dn1 * g_mix
        gx_ref[...] = dx1_ref[...] + rstd * (dxh - xh * _mean_last(dxh * xh))
        sums_ref[0:1, :] += _rowsum(dh)
        sums_ref[1:2, :] += _rowsum(dh * (xh * g_mix))
        sums_ref[2:3, :] += _rowsum(dn1 * xh)

    tok = lambda w: pl.BlockSpec((tm, w), lambda i: (i, 0))
    return _hosted_call(
        body, comm, name="in_proj_bwd", grid=(seq // tm,),
        out_shape=[jax.ShapeDtypeStruct((seq, D_MODEL), F32), jax.ShapeDtypeStruct((8, D_MODEL), F32)],
        in_specs=[tok(D_MODEL), tok(D_MODEL), tok(IN_PROJ_WIDTH), _full((8, D_MODEL)),
                  _full((IN_PROJ_WIDTH, D_MODEL))],
        out_specs=[tok(D_MODEL), _full((8, D_MODEL))],
        semantics=("arbitrary",),
    )(x, dx1, dproj, vecs, w_in_t)


class _GradTiles(NamedTuple):
    tm: int
    tn: int
    n_tiles: int
    chips_per_tile: int
    a_index: Callable
    b_index: Callable


def _weight_grad_kernel(a, b, c_idx, name, tiles, comm=None):
    seq = a.shape[0]
    tk = min(seq, 4096)
    nk = seq // tk
    tm, tn, n_tiles, per = tiles.tm, tiles.tn, tiles.n_tiles, tiles.chips_per_tile
    rows = tm // per

    def half(phase, c):
        return phase * c[0] + (1 - phase) * (1 - c[0])

    def body(c_ref, a_ref, b_ref, o_ref, acc, stage, landed, send_sems, recv_sems):
        del c_ref
        phase, t, kk = pl.program_id(0), pl.program_id(1), pl.program_id(2)
        x, y, c, _ = _mesh_place()

        def copy(tile):
            return pltpu.make_async_remote_copy(
                src_ref=stage.at[tile], dst_ref=landed.at[tile], send_sem=send_sems.at[tile],
                recv_sem=recv_sems.at[tile], device_id=(x, y, 1 - c), device_id_type=MESH)

        @pl.when(kk == 0)
        def _():
            acc[...] = jnp.zeros_like(acc)

        acc[...] += _dot_tn(a_ref[...], b_ref[...])

        @pl.when((kk == nk - 1) & (phase == 0))
        def _():
            stage[t] = acc[...].astype(stage.dtype)
            copy(t).start()

        @pl.when((kk == nk - 1) & (phase == 1))
        def _():
            copy(t).wait_recv()
            total = acc[...] + landed[t].astype(F32)
            for q in range(per):
                o_ref[q] = total[rows * q:rows * (q + 1)].astype(o_ref.dtype)

        @pl.when((kk == nk - 1) & (phase == 1) & (t == n_tiles - 1))
        def _():
            for tile in range(n_tiles):
                copy(tile).wait_send()

    out = _hosted_call(
        body, comm, name=name, grid=(2, n_tiles, nk), n_prefetch=1,
        out_shape=[jax.ShapeDtypeStruct((n_tiles * per, rows, tn), GRAD_COMM_DTYPE)],
        in_specs=[pl.BlockSpec((tk, tm), lambda p, t, k, c: (k, tiles.a_index(t, half(p, c)))),
                  pl.BlockSpec((tk, tn), lambda p, t, k, c: (k, tiles.b_index(t, half(p, c))))],
        out_specs=[pl.BlockSpec((per, rows, tn), lambda p, t, k, c: (p * t, 0, 0))],
        scratch_shapes=[pltpu.VMEM((tm, tn), F32), pltpu.VMEM((n_tiles, tm, tn), GRAD_COMM_DTYPE),
                        pltpu.VMEM((n_tiles, tm, tn), GRAD_COMM_DTYPE),
                        pltpu.SemaphoreType.DMA((n_tiles,)), pltpu.SemaphoreType.DMA((n_tiles,))],
        semantics=("arbitrary", "arbitrary", "arbitrary"),
    )(c_idx, a, b)
    return out[0] if comm is None else out


def _row_tile(rows, most=256, sublanes=16):
    return max(t for t in range(sublanes, most + 1, sublanes) if rows % t == 0)


def _adam_update(w, g, m, v):
    m_new = ADAM_B1 * m + (1.0 - ADAM_B1) * g
    v_new = ADAM_B2 * v + (1.0 - ADAM_B2) * (g * g)
    m_hat = m_new / (1.0 - ADAM_B1 ** ADAM_STEP)
    v_hat = v_new / (1.0 - ADAM_B2 ** ADAM_STEP)
    delta = -ADAM_LR * (m_hat / (jnp.sqrt(v_hat) + ADAM_EPS) + ADAM_WD * w)
    return delta, m_new, v_new


def _sum_chips_kernel(own, others, place, name):
    _, r, n = own.shape
    tr = _row_tile(r)

    def body(place_ref, own_ref, oth_ref, o_ref):
        del place_ref
        acc = own_ref[...].astype(F32)
        for k in range(N_CHIPS - 1):
            acc = acc + oth_ref[k].astype(F32)
        o_ref[...] = acc

    return pl.pallas_call(
        body, name=name, out_shape=jax.ShapeDtypeStruct((2, r, n), F32),
        grid_spec=pltpu.PrefetchScalarGridSpec(
            num_scalar_prefetch=1, grid=(r // tr,),
            in_specs=[pl.BlockSpec((None, tr, n), lambda i, p: (p[0], i, 0)),
                      pl.BlockSpec((N_CHIPS - 1, tr, n), lambda i, p: (0, i, 0))],
            out_specs=pl.BlockSpec((None, tr, n), lambda i, p: (p[1], i, 0))),
        compiler_params=_params("parallel"),
    )(place, own, others)


def _adam_kernel(w, g, m, v, name):
    r, n = w.shape
    by_columns = g.shape[1] == r
    tr, tn = _row_tile(g.shape[1], most=512), g.shape[2]

    def body(w_ref, g_ref, m_ref, v_ref, g_out, d_ref, mo_ref, vo_ref):
        gv = g_ref[...]
        g_out[...] = gv
        d_ref[...], mo_ref[...], vo_ref[...] = _adam_update(w_ref[...], gv, m_ref[...], v_ref[...])

    steps = g.shape[1] // tr
    spec = pl.BlockSpec((tr, tn), (lambda h, i: (i, h)) if by_columns else (lambda h, i: (h * steps + i, 0)))
    return pl.pallas_call(
        body, name=name, grid=(2, steps), out_shape=[jax.ShapeDtypeStruct((r, n), F32)] * 4,
        in_specs=[spec, pl.BlockSpec((None, tr, tn), lambda h, i: (h, i, 0)), spec, spec], out_specs=[spec] * 4,
        compiler_params=_params("parallel", "parallel"),
    )(w, g, m, v)


SMALL_PARAMS = ("b_ada", "g_mix", "g_ffn", "g_final", "b_spatial", "sinks", "w_spatial")


def _small_update_kernel(gathered, params):
    shapes = [params[nm][0].shape for nm in SMALL_PARAMS]

    def body(*refs):
        g_refs, refs = refs[:5], refs[5:]
        p_refs, refs = refs[:3 * len(SMALL_PARAMS)], refs[3 * len(SMALL_PARAMS):]
        loss_ref, o_refs = refs[0], refs[1:]

        def total(ref):
            acc = ref[0].astype(F32)
            for k in range(1, N_DEV):
                acc = acc + ref[k].astype(F32)
            return acc

        s1, s2, db, ds, dw = (total(r) for r in g_refs)
        loss_ref[...] = jnp.broadcast_to(s2[6:7, 0:1], loss_ref.shape)
        grads = {"b_ada": [s1[0:1], s1[1:2], s2[5:6], s2[0:1], s2[1:2], s2[2:3]], "g_mix": [s1[2:3]],
                 "g_ffn": [s2[3:4]], "g_final": [s2[4:5]], "b_spatial": [db.T[0:GMLP_GROUPS]],
                 "w_spatial": [dw]}
        lane = lax.broadcasted_iota(jnp.int32, (1, LANES), 1)
        sink_row = jnp.zeros((1, LANES), F32)
        for h in range(N_Q_HEADS):
            sink_row = sink_row + jnp.where(lane == h, ds[h:h + 1, :], 0.0)
        grads["sinks"] = [sink_row[:, 0:N_Q_HEADS]]
        for i, nm in enumerate(SMALL_PARAMS):
            w_ref, m_ref, v_ref = p_refs[3 * i:3 * i + 3]
            outs = o_refs[4 * i:4 * i + 4]
            width = grads[nm][0].shape[1]
            for k, g in enumerate(grads[nm]):
                cols = slice(width * k, width * (k + 1))
                upd = _adam_update(w_ref[:, cols], g, m_ref[:, cols], v_ref[:, cols])
                for o_ref, val in zip(outs, (g,) + upd):
                    o_ref[:, cols] = val

    flat = [a for nm in SMALL_PARAMS for a in params[nm]]
    out_shape = [jax.ShapeDtypeStruct((8, LANES), F32)]
    out_shape += [jax.ShapeDtypeStruct(s, F32) for s in shapes for _ in range(4)]
    outs = pl.pallas_call(
        body, name="small_update", grid=(1,), out_shape=out_shape,
        in_specs=[_full(g.shape) for g in gathered] + [_full(a.shape) for a in flat],
        out_specs=[_full(s.shape) for s in out_shape],
        compiler_params=_params("arbitrary"),
    )(*gathered, *flat)
    return {nm: outs[1 + 4 * i:5 + 4 * i] for i, nm in enumerate(SMALL_PARAMS)}, outs[0]


def _ada_update_kernel(act_t, dmod, w, m, v):
    r, n = w.shape
    tr = 256

    def body(a_ref, d_ref, w_ref, m_ref, v_ref, g_ref, dl_ref, mo_ref, vo_ref):
        g = _dot(a_ref[...], d_ref[...])
        g_ref[...] = g
        dl_ref[...], mo_ref[...], vo_ref[...] = _adam_update(w_ref[...], g, m_ref[...], v_ref[...])

    spec = pl.BlockSpec((tr, n), lambda i: (i, 0))
    return pl.pallas_call(
        body, name="ada_update", grid=(r // tr,), out_shape=[jax.ShapeDtypeStruct((r, n), F32)] * 4,
        in_specs=[pl.BlockSpec((tr, N_DEV), lambda i: (i, 0)), _full((N_DEV, n)), spec, spec, spec],
        out_specs=[spec] * 4, compiler_params=_params("parallel"),
    )(act_t, dmod, w, m, v)


def kernel(x, c, positions, w_ada, b_ada, g_mix, w_in, w_spatial, b_spatial, sinks, w_out, g_ffn, w_ff1, w_ff2, g_final, loss_target, m_w_ada, m_b_ada, m_g_mix, m_w_in, m_w_spatial, m_b_spatial, m_sinks, m_w_out, m_g_ffn, m_w_ff1, m_w_ff2, m_g_final, v_w_ada, v_b_ada, v_g_mix, v_w_in, v_w_spatial, v_b_spatial, v_sinks, v_w_out, v_g_ffn, v_w_ff1, v_w_ff2, v_g_final):
    xi, yi, ci = lax.axis_index("x"), lax.axis_index("y"), lax.axis_index("c")
    chip = 2 * xi + yi
    dev = 2 * chip + ci
    seq = x.shape[1]
    x2, tgt = x[0], loss_target[0]
    ada_cols = w_ada.shape[2]

    big = {"w_in": tuple(a[0].T for a in (w_in, m_w_in, v_w_in)),
           "w_out": (w_out[0], m_w_out[0], v_w_out[0]), "w_ff1": (w_ff1[0], m_w_ff1[0], v_w_ff1[0]),
           "w_ff2": (w_ff2[0], m_w_ff2[0], v_w_ff2[0])}

    def halves(nm):
        r, n = big[nm][0].shape
        return big[nm][0].astype(WEIGHT_COMM_DTYPE).reshape(2, r // 2, n)

    chip_idx = chip.reshape(1).astype(jnp.int32)
    c_all, w_in_t, g_out = _all_gather8([c, halves("w_in"), halves("w_out")], "gather_first",
                                        split=[False, True, True], skip_own=(2,))
    c_all, w_in_t = c_all.reshape(N_DEV, D_MODEL), w_in_t.reshape(IN_PROJ_WIDTH, D_MODEL)
    b_shard = lax.dynamic_slice(b_ada, (0, chip * ada_cols), (1, ada_cols))
    mod_part, act = _mod_kernel(c_all, w_ada[0], b_shard)
    mod_all, = _all_gather8([mod_part], "gather_mod")
    mod_me = lax.dynamic_index_in_dim(mod_all[0::2], dev, axis=1, keepdims=False)
    mod_me = mod_me.reshape(N_MOD, D_MODEL)
    shift1, scale1, gate1, shift2, scale2, gate2 = (mod_me[k:k + 1] for k in range(N_MOD))

    zeros_row = jnp.zeros((1, D_MODEL), F32)
    vecs1 = jnp.concatenate([g_mix, shift1, scale1] + [zeros_row] * 5, axis=0)
    vecs2 = jnp.concatenate([gate1, shift2, scale2, gate2, g_ffn, g_final.reshape(1, D_MODEL)]
                            + [zeros_row] * 2, axis=0)
    bias_full = jnp.repeat(b_spatial[0].T, HEAD_DIM, axis=1)
    sink_rows = jnp.broadcast_to(sinks[0][:, None], (N_Q_HEADS, LANES))
    inv_freq = ROPE_THETA ** (-jnp.arange(0, ROT_DIM, 2, dtype=F32) / ROT_DIM)
    rope_tab = _rope_lane_tables(*_rope_angle_kernel(positions, inv_freq.reshape(ROT_DIM // 2, 1)))

    trunk_weights = ["w_out", "w_ff1", "w_ff2"]
    shards = [halves(nm) for nm in trunk_weights]
    proj, hb, *staged = _in_proj_kernel(x2, vecs1, w_in_t, comm=_gather2d_first(shards[1:]))
    cat, *staged = _mixer_fwd_kernel(proj, rope_tab, w_spatial[0], bias_full, sink_rows,
                                     comm=_gather2d_second(staged, shards[1:]))
    staged = [g_out] + list(_gather_forward(staged, "gather_forward"))
    dx1, dcat, dmix, h2b, rb, dab, dffb, sums2 = _trunk_kernel(
        x2, tgt, cat, vecs2, chip_idx,
        [g.reshape((N_CHIPS,) + big[nm][0].shape) for nm, g in zip(trunk_weights, staged)],
        [s.reshape(big[nm][0].shape) for nm, s in zip(trunk_weights, shards)])

    c_idx = ci.reshape(1).astype(jnp.int32)
    place = jnp.stack([chip, ci]).astype(jnp.int32)
    half_d = D_MODEL // 2
    cs_ff2 = _weight_grad_kernel(rb, dffb, c_idx, "dw_ff2",
                                 _GradTiles(D_MODEL, half_d, N_CHIPS, 1, lambda t, h: t, lambda t, h: h))
    split_row = 3 * D_MODEL // 4
    cs_ff1, sc_ff2 = _weight_grad_kernel(
        h2b, dab, c_idx, "dw_ff1",
        _GradTiles(D_MODEL, half_d, N_CHIPS, 1, lambda t, h: 0, lambda t, h: 2 * t + h),
        comm=_scatter_job([cs_ff2], rows=(0, split_row)))
    cs_out, sc_ff2 = _weight_grad_kernel(
        cat, dmix, c_idx, "dw_out", _GradTiles(D_MODEL, half_d, 1, N_CHIPS, lambda t, h: 0, lambda t, h: h),
        comm=_scatter_job([cs_ff2], rows=(split_row, D_MODEL - split_row), into=[sc_ff2]))
    dproj, dw_spatial, db_lanes, dsink_rows, sc_ff1, sc_out = _mixer_bwd_kernel(
        proj, rope_tab, dcat, w_spatial[0], w_spatial[0].transpose(0, 2, 1), bias_full, sink_rows,
        dev.reshape(1).astype(jnp.int32), comm=_scatter_job([cs_ff1, cs_out]))
    totals = [_sum_chips_kernel(own, oth, place, "grad_sum_" + nm)
              for nm, own, oth in (("w_out", cs_out, sc_out), ("w_ff1", cs_ff1, sc_ff1), ("w_ff2", cs_ff2, sc_ff2))]
    small_slots = [db_lanes, dsink_rows, dw_spatial.reshape(N_DEV, GMLP_GROUPS * CHUNK, CHUNK)]
    cs_in, *rode = _weight_grad_kernel(
        dproj, hb, c_idx, "dw_in",
        _GradTiles(2 * W_IN_BLOCK, half_d, N_CHIPS // 2, 2, lambda t, h: t, lambda t, h: h),
        comm=_merge_in_place(_gather_job(small_slots), _share_job(totals)))
    small_stage1, shared = rode[:len(small_slots)], rode[len(small_slots):]
    grad_x, sums1 = _in_proj_bwd_kernel(x2, dx1, dproj, vecs1, w_in_t)
    *gathered, sc_in = _all_gather8([sums1, sums2], "gather_small", forward=small_stage1,
                                    riders=[_scatter_job([cs_in])])
    total_in = _sum_chips_kernel(cs_in, sc_in, place, "grad_sum_w_in")
    shared = list(_sibling_share([total_in], "grad_share_w_in")) + list(shared)
    names = ["w_in", "w_out", "w_ff1", "w_ff2"]
    big_out = {}
    for nm, g in zip(names, shared):
        w, m, v = big[nm]
        outs = _adam_kernel(w, g, m, v, "adam_" + nm)
        big_out[nm] = tuple((t.T if nm == "w_in" else t)[None] for t in outs)

    small = {"b_ada": (b_ada, m_b_ada, v_b_ada), "g_mix": (g_mix, m_g_mix, v_g_mix),
             "g_ffn": (g_ffn, m_g_ffn, v_g_ffn), "g_final": (g_final, m_g_final, v_g_final),
             "b_spatial": (b_spatial, m_b_spatial, v_b_spatial), "sinks": (sinks, m_sinks, v_sinks),
             "w_spatial": (w_spatial, m_w_spatial, v_w_spatial)}
    flat_shape = {"g_final": (1, D_MODEL), "b_spatial": (GMLP_GROUPS, CHUNK), "w_spatial": (GMLP_GROUPS * CHUNK, CHUNK)}
    small_out, loss_tile = _small_update_kernel(
        gathered, {nm: tuple(a.reshape(flat_shape.get(nm, a.shape)) for a in small[nm]) for nm in small})
    small_out = {nm: [o.reshape(small[nm][0].shape) for o in small_out[nm]] for nm in small}
    loss = loss_tile[0, 0]

    g1, g2 = gathered[0], gathered[1]
    dmod_all = jnp.concatenate([g1[:, 0], g1[:, 1], g2[:, 5], g2[:, 0], g2[:, 1], g2[:, 2]], axis=1)
    dmod_cols = lax.dynamic_slice(dmod_all, (0, chip * ada_cols), (N_DEV, ada_cols))
    ada = _ada_update_kernel(act.T, dmod_cols, w_ada[0], m_w_ada[0], v_w_ada[0])
    big_out["w_ada"] = tuple(t[None] for t in ada)

    order = ["w_ada", "b_ada", "g_mix", "w_in", "w_spatial", "b_spatial", "sinks", "w_out", "g_ffn",
             "w_ff1", "w_ff2", "g_final"]

    def leaf(nm, k):
        return big_out[nm][k] if nm in big_out else small_out[nm][k]

    outs = [loss, grad_x[None]]
    for k in range(4):
        outs += [leaf(nm, k) for nm in order]
    return tuple(outs)
```

```python
import math
from typing import Callable, NamedTuple

import jax
import jax.numpy as jnp
from jax import lax
from jax.experimental import pallas as pl
from jax.experimental.pallas import tpu as pltpu

F32 = jnp.float32
MXU_DTYPE = jnp.bfloat16
WEIGHT_COMM_DTYPE = jnp.bfloat16
GRAD_COMM_DTYPE = jnp.bfloat16

D_MODEL = 1024
D_FF = 4096
HEAD_DIM = 64
GMLP_GROUPS = 8
GMLP_WIDTH = 512
CHUNK = 128
N_Q_HEADS = 8
N_KV_HEADS = 2
ATTN_WIDTH = 512
KV_WIDTH = 128
ROT_DIM = 16
ROPE_THETA = 500000.0
IN_PROJ_WIDTH = 1792
N_MOD = 6
EPS = 1e-5
N_CHIPS = 4
N_DEV = 8
LANES = 128
W_IN_BLOCK = IN_PROJ_WIDTH // N_CHIPS

ADAM_LR = 0.001
ADAM_B1 = 0.9
ADAM_B2 = 0.999
ADAM_EPS = 1e-08
ADAM_WD = 0.01
ADAM_STEP = 10

VMEM_LIMIT_BYTES = 58 * 1024 * 1024
MESH = pl.DeviceIdType.MESH


def _params(*semantics):
    return pltpu.CompilerParams(dimension_semantics=semantics, vmem_limit_bytes=VMEM_LIMIT_BYTES)


def _dot(a, b):
    return jnp.dot(a.astype(MXU_DTYPE), b.astype(MXU_DTYPE), preferred_element_type=F32)


def _dot_nt(a, b):
    return lax.dot_general(a.astype(MXU_DTYPE), b.astype(MXU_DTYPE), (((1,), (1,)), ((), ())),
                           preferred_element_type=F32)


def _dot_tn(a, b):
    return lax.dot_general(a.astype(MXU_DTYPE), b.astype(MXU_DTYPE), (((0,), (0,)), ((), ())),
                           preferred_element_type=F32)


def _full(shape):
    return pl.BlockSpec(shape, lambda *_: (0,) * len(shape))


def _any():
    return pl.BlockSpec(memory_space=pl.ANY)


def _rowsum(v):
    return jnp.sum(v, axis=0, keepdims=True)


def _mean_last(v):
    return jnp.mean(v, axis=-1, keepdims=True)


class _Comm(NamedTuple):
    operands: tuple
    out_shapes: tuple
    n_sems: int
    make: Callable
    in_place: int = 0


def _hosted_call(body, comm, *, name, grid, in_specs, out_shape, out_specs, scratch_shapes=(), semantics,
                 n_prefetch=0):
    if comm is None:
        return pl.pallas_call(
            body, name=name, out_shape=out_shape, compiler_params=_params(*semantics),
            grid_spec=pltpu.PrefetchScalarGridSpec(
                num_scalar_prefetch=n_prefetch, grid=grid, in_specs=in_specs, out_specs=out_specs,
                scratch_shapes=list(scratch_shapes)))
    n_in, n_out, n_scr = len(in_specs), len(out_shape), len(scratch_shapes)
    k_in, k_out = len(comm.operands), len(comm.out_shapes)

    def hosted(*refs):
        prefetched, refs = refs[:n_prefetch], refs[n_prefetch:]
        ins, refs = refs[:n_in], refs[n_in:]
        c_ins, refs = refs[:k_in], refs[k_in:]
        outs, refs = refs[:n_out], refs[n_out:]
        c_outs, refs = refs[:k_out], refs[k_out:]
        scratch, (send_sems, recv_sems) = refs[:n_scr], refs[n_scr:]
        first, last = None, None
        for d, size in enumerate(grid):
            at_start, at_end = pl.program_id(d) == 0, pl.program_id(d) == size - 1
            first = at_start if first is None else first & at_start
            last = at_end if last is None else last & at_end

        @pl.when(first)
        def _():
            for cp in comm.make(c_ins, c_outs, send_sems, recv_sems)[0]:
                cp.start()

        body(*prefetched, *ins, *outs, *scratch)

        @pl.when(last)
        def _():
            for wait in comm.make(c_ins, c_outs, send_sems, recv_sems)[1]:
                wait()

    aliases = {n_prefetch + n_in + i: n_out + i for i in range(comm.in_place)}
    call = pl.pallas_call(
        hosted, name=name, out_shape=list(out_shape) + list(comm.out_shapes),
        compiler_params=_params(*semantics), input_output_aliases=aliases,
        grid_spec=pltpu.PrefetchScalarGridSpec(
            num_scalar_prefetch=n_prefetch, grid=grid, in_specs=list(in_specs) + [_any()] * k_in,
            out_specs=list(out_specs) + [_any()] * k_out,
            scratch_shapes=list(scratch_shapes) + [pltpu.SemaphoreType.DMA((comm.n_sems,)),
                                                    pltpu.SemaphoreType.DMA((comm.n_sems,))]))
    return lambda *args: call(*args, *comm.operands)


class _Shifted:
    def __init__(self, base, offset):
        self.base, self.offset = base, offset

    @property
    def at(self):
        return self

    def __getitem__(self, k):
        return self.base.at[self.offset + k]


def _merge_jobs(*jobs):
    def order(count):
        first = [(j, i) for j, job in enumerate(jobs) for i in range(job.in_place)]
        return first + [(j, i) for j, job in enumerate(jobs) for i in range(job.in_place, count(job))]

    op_order, out_order = order(lambda job: len(job.operands)), order(lambda job: len(job.out_shapes))

    def make(ins, outs, send_sems, recv_sems):
        starts, waits, sem = [], [], 0
        for j, job in enumerate(jobs):
            mine_in = [ins[k] for k, (jj, _) in enumerate(op_order) if jj == j]
            mine_out = [outs[k] for k, (jj, _) in enumerate(out_order) if jj == j]
            s, w = job.make(mine_in, mine_out, _Shifted(send_sems, sem), _Shifted(recv_sems, sem))
            starts, waits, sem = starts + s, waits + w, sem + job.n_sems
        return starts, waits

    return _Comm(tuple(jobs[j].operands[i] for j, i in op_order), tuple(jobs[j].out_shapes[i] for j, i in out_order),
                 sum(job.n_sems for job in jobs), make, in_place=sum(job.in_place for job in jobs))


def _mesh_place():
    x, y, c = lax.axis_index("x"), lax.axis_index("y"), lax.axis_index("c")
    return x, y, c, [(1 - x, y), (x, 1 - y), (1 - x, 1 - y)]


def _gather_job(bufs):
    per = 4

    def make(ins, outs, send_sems, recv_sems):
        del ins
        x, y, c, chips = _mesh_place()
        starts, waits = [], []
        for a, out in enumerate(outs):
            mine = src = out.at[4 * x + 2 * y + c]
            to = [(x, y, 1 - c)] + [(px, py, c) for px, py in chips]
            sends = [pltpu.make_async_remote_copy(
                src_ref=src, dst_ref=mine, send_sem=send_sems.at[per * a + k],
                recv_sem=recv_sems.at[per * a + k], device_id=dev, device_id_type=MESH)
                for k, dev in enumerate(to)]
            recvs = [pltpu.make_async_remote_copy(
                src_ref=src, dst_ref=out.at[4 * px + 2 * py + pc], send_sem=send_sems.at[per * a + k],
                recv_sem=recv_sems.at[per * a + k], device_id=(px, py, pc), device_id_type=MESH)
                for k, (px, py, pc) in enumerate(to)]
            starts += sends
            waits += [s.wait_send for s in sends] + [r.wait_recv for r in recvs]
        return starts, waits

    shapes = tuple(jax.ShapeDtypeStruct(b.shape, b.dtype) for b in bufs)
    return _Comm(tuple(bufs), shapes, per * len(bufs), make, in_place=len(bufs))


def _slots(x, y, c):
    return 4 * x + 2 * y + c, 4 * (1 - x) + 2 * y + c, 4 * x + 2 * (1 - y) + c, 4 * (1 - x) + 2 * (1 - y) + c


def _gather2d_first(halves):
    per = 2

    def make(ins, outs, send_sems, recv_sems):
        x, y, c, _ = _mesh_place()
        me, xn, yn, _ = _slots(x, y, c)
        starts, waits = [], []
        for a, (src, out) in enumerate(zip(ins, outs)):
            blk = src.at[c]
            rows = blk.shape[0] // 2
            upper, lower = pl.ds(0, rows), pl.ds(rows, rows)

            def copy(k, src_ref, dst_ref, dev, a=a):
                return pltpu.make_async_remote_copy(
                    src_ref=src_ref, dst_ref=dst_ref, send_sem=send_sems.at[per * a + k],
                    recv_sem=recv_sems.at[per * a + k], device_id=dev, device_id_type=MESH)

            sends = [copy(0, blk.at[upper], out.at[me, upper], (1 - x, y, c)),
                     copy(1, blk.at[lower], out.at[me, lower], (x, 1 - y, c))]
            recvs = [copy(0, blk.at[upper], out.at[xn, upper], (1 - x, y, c)),
                     copy(1, blk.at[lower], out.at[yn, lower], (x, 1 - y, c))]
            starts += sends
            waits += [s.wait_send for s in sends] + [r.wait_recv for r in recvs]
        return starts, waits

    shapes = tuple(jax.ShapeDtypeStruct((N_DEV,) + h.shape[1:], h.dtype) for h in halves)
    return _Comm(tuple(halves), shapes, per * len(halves), make)


def _gather2d_second(bufs, halves):
    per = 4
    n_arr = len(bufs)

    def make(ins, outs, send_sems, recv_sems):
        x, y, c, _ = _mesh_place()
        me, xn, yn, dg = _slots(x, y, c)
        starts, waits = [], []
        for a, buf in enumerate(outs):
            own = ins[n_arr + a].at[c]
            rows = buf.shape[1] // 2
            upper, lower = pl.ds(0, rows), pl.ds(rows, rows)
            plan = [(own.at[upper], me, upper, (x, 1 - y, c), yn), (buf.at[xn, upper], xn, upper, (x, 1 - y, c), dg),
                    (own.at[lower], me, lower, (1 - x, y, c), xn), (buf.at[yn, lower], yn, lower, (1 - x, y, c), dg)]
            for k, (src, slot, part, dev, landing) in enumerate(plan):
                sems = dict(send_sem=send_sems.at[per * a + k], recv_sem=recv_sems.at[per * a + k],
                            device_id=dev, device_id_type=MESH)
                send = pltpu.make_async_remote_copy(src_ref=src, dst_ref=buf.at[slot, part], **sems)
                arrival = pltpu.make_async_remote_copy(src_ref=src, dst_ref=buf.at[landing, part], **sems)
                starts.append(send)
                waits += [send.wait_send, arrival.wait_recv]
        return starts, waits

    shapes = tuple(jax.ShapeDtypeStruct(b.shape, b.dtype) for b in bufs)
    return _Comm(tuple(bufs) + tuple(halves), shapes, per * n_arr, make, in_place=n_arr)


def _gather_forward(bufs, name):
    n_arr = len(bufs)

    def body(*refs):
        outs = refs[n_arr:2 * n_arr]
        send_sems, recv_sems = refs[2 * n_arr:]
        x, y, c, chips = _mesh_place()
        sends, recvs = [], []
        for a, buf in enumerate(outs):
            for j, (px, py) in enumerate(chips):
                mine, theirs = buf.at[4 * px + 2 * py + c], buf.at[4 * px + 2 * py + 1 - c]
                sems = dict(send_sem=send_sems.at[3 * a + j], recv_sem=recv_sems.at[3 * a + j],
                            device_id=(x, y, 1 - c), device_id_type=MESH)
                sends.append(pltpu.make_async_remote_copy(src_ref=mine, dst_ref=mine, **sems))
                recvs.append(pltpu.make_async_remote_copy(src_ref=mine, dst_ref=theirs, **sems))
        for cp in sends:
            cp.start()
        for s, r in zip(sends, recvs):
            s.wait_send()
            r.wait_recv()

    return pl.pallas_call(
        body, name=name, out_shape=[jax.ShapeDtypeStruct(b.shape, b.dtype) for b in bufs],
        in_specs=[_any()] * n_arr, out_specs=[_any()] * n_arr,
        input_output_aliases={a: a for a in range(n_arr)},
        scratch_shapes=[pltpu.SemaphoreType.DMA((3 * n_arr,)), pltpu.SemaphoreType.DMA((3 * n_arr,))],
    )(*bufs)


def _scatter_job(chip_sums, rows=None, into=()):
    n_into = len(into)

    def part(ref):
        return ref if rows is None else ref.at[pl.ds(rows[0], rows[1])]

    def make(ins, outs, send_sems, recv_sems):
        x, y, c, chips = _mesh_place()
        copies = [pltpu.make_async_remote_copy(
            src_ref=part(src.at[2 * px + py]), dst_ref=part(out.at[j]), send_sem=send_sems.at[3 * a + j],
            recv_sem=recv_sems.at[3 * a + j], device_id=(px, py, c), device_id_type=MESH)
            for a, (src, out) in enumerate(zip(ins[n_into:], outs)) for j, (px, py) in enumerate(chips)]
        return copies, [cp.wait for cp in copies]

    shapes = tuple(jax.ShapeDtypeStruct((3,) + s.shape[1:], s.dtype) for s in chip_sums)
    return _Comm(tuple(into) + tuple(chip_sums), shapes, 3 * len(chip_sums), make, in_place=n_into)


def _all_gather8(blocks, name, split=False, forward=(), riders=(), skip_own=()):
    n_arr, n_fwd = len(blocks), len(forward)
    splits = list(split) if isinstance(split, (list, tuple)) else [split] * n_arr
    own_slots = [a not in skip_own for a in range(n_arr)]
    rider_in = sum(len(r.operands) for r in riders)
    rider_out = sum(len(r.out_shapes) for r in riders)

    def body(*refs):
        x_refs, refs = refs[:n_arr], refs[n_arr + n_fwd:]
        r_ins, refs = refs[:rider_in], refs[rider_in:]
        out_refs, refs = refs[:n_arr], refs[n_arr:]
        fwd_refs, refs = refs[:n_fwd], refs[n_fwd:]
        r_outs, refs = refs[:rider_out], refs[rider_out:]
        (send_sems, recv_sems, local_sems), rider_sems = refs[:3], refs[3:]
        x, y, c, chips = _mesh_place()
        me, sibling = (x, y, c), (x, y, 1 - c)
        rider_waits, i0, o0 = [], 0, 0
        for n, job in enumerate(riders):
            k_in, k_out = len(job.operands), len(job.out_shapes)
            starts, waits = job.make(r_ins[i0:i0 + k_in], r_outs[o0:o0 + k_out],
                                     rider_sems[2 * n], rider_sems[2 * n + 1])
            for cp in starts:
                cp.start()
            rider_waits += waits
            i0, o0 = i0 + k_in, o0 + k_out
        passing = []
        for f, buf in enumerate(fwd_refs):
            for j, (px, py) in enumerate(chips):
                mine, theirs = buf.at[4 * px + 2 * py + c], buf.at[4 * px + 2 * py + 1 - c]
                sems = dict(send_sem=send_sems.at[7 * n_arr + 3 * f + j], recv_sem=recv_sems.at[7 * n_arr + 3 * f + j],
                            device_id=sibling, device_id_type=MESH)
                passing.append((pltpu.make_async_remote_copy(src_ref=mine, dst_ref=mine, **sems),
                                pltpu.make_async_remote_copy(src_ref=mine, dst_ref=theirs, **sems)))
        for send, _ in passing:
            send.start()
        arrays = []
        for a, (x_ref, out_ref) in enumerate(zip(x_refs, out_refs)):
            src_mine = x_ref.at[c] if splits[a] else x_ref

            def copy(k, blk, to, src=None, a=a, out_ref=out_ref):
                dst = out_ref.at[4 * blk[0] + 2 * blk[1] + blk[2]]
                return pltpu.make_async_remote_copy(
                    src_ref=dst if src is None else src, dst_ref=dst,
                    send_sem=send_sems.at[7 * a + k], recv_sem=recv_sems.at[7 * a + k],
                    device_id=to, device_id_type=MESH)

            mine = pltpu.make_async_copy(src_mine, out_ref.at[4 * x + 2 * y + c], local_sems.at[a])
            first = [copy(0, me, sibling, src=src_mine)] if own_slots[a] else []
            first += [copy(1 + j, me, (*chip, c), src=src_mine) for j, chip in enumerate(chips)]
            for cp in first + ([mine] if own_slots[a] else []):
                cp.start()
            arrays.append((copy, mine, first, own_slots[a]))
        sent = []
        for copy, mine, first, own in arrays:
            passed = [copy(4 + j, (*chip, c), sibling) for j, chip in enumerate(chips)]
            for j, chip in enumerate(chips):
                copy(1 + j, (*chip, c), me).wait_recv()
                passed[j].start()
            sent += first + passed
        for copy, mine, first, own in arrays:
            if own:
                copy(0, sibling, me).wait_recv()
                mine.wait()
            for j, chip in enumerate(chips):
                copy(4 + j, (*chip, 1 - c), me).wait_recv()
        for cp in sent:
            cp.wait_send()
        for send, arrival in passing:
            send.wait_send()
            arrival.wait_recv()
        for wait in rider_waits:
            wait()

    n_sems = 7 * n_arr + 3 * n_fwd
    rider_operands = [a for r in riders for a in r.operands]
    rider_shapes = [s for r in riders for s in r.out_shapes]
    return pl.pallas_call(
        body, name=name,
        out_shape=[jax.ShapeDtypeStruct((N_DEV,) + tuple(b.shape[1:] if s else b.shape), b.dtype)
                   for b, s in zip(blocks, splits)]
        + [jax.ShapeDtypeStruct(f.shape, f.dtype) for f in forward] + rider_shapes,
        in_specs=[_any()] * (n_arr + n_fwd + rider_in), out_specs=[_any()] * (n_arr + n_fwd + rider_out),
        input_output_aliases={n_arr + f: n_arr + f for f in range(n_fwd)},
        scratch_shapes=[pltpu.SemaphoreType.DMA((n_sems,)), pltpu.SemaphoreType.DMA((n_sems,)),
                        pltpu.SemaphoreType.DMA((n_arr,))]
        + [pltpu.SemaphoreType.DMA((r.n_sems,)) for r in riders for _ in range(2)],
    )(*blocks, *forward, *rider_operands)


def _share_job(bufs):
    def make(ins, outs, send_sems, recv_sems):
        del ins
        x, y, c, _ = _mesh_place()
        sems = lambda a: dict(send_sem=send_sems.at[a], recv_sem=recv_sems.at[a],
                              device_id=(x, y, 1 - c), device_id_type=MESH)
        sends = [pltpu.make_async_remote_copy(src_ref=o.at[c], dst_ref=o.at[c], **sems(a)) for a, o in enumerate(outs)]
        arrivals = [pltpu.make_async_remote_copy(src_ref=o.at[c], dst_ref=o.at[1 - c], **sems(a))
                    for a, o in enumerate(outs)]
        return sends, [s.wait_send for s in sends] + [r.wait_recv for r in arrivals]

    shapes = tuple(jax.ShapeDtypeStruct(b.shape, b.dtype) for b in bufs)
    return _Comm(tuple(bufs), shapes, len(bufs), make, in_place=len(bufs))


def _sibling_share(bufs, name):
    n_arr = len(bufs)

    def body(*refs):
        out_refs = refs[n_arr:2 * n_arr]
        send_sems, recv_sems = refs[2 * n_arr:]
        x, y, c = lax.axis_index("x"), lax.axis_index("y"), lax.axis_index("c")
        copies = [pltpu.make_async_remote_copy(
            src_ref=out_refs[a].at[c], dst_ref=out_refs[a].at[c],
            send_sem=send_sems.at[a], recv_sem=recv_sems.at[a],
            device_id=(x, y, 1 - c), device_id_type=MESH) for a in range(n_arr)]
        for cp in copies:
            cp.start()
        for a in range(n_arr):
            pltpu.make_async_remote_copy(
                src_ref=out_refs[a].at[c], dst_ref=out_refs[a].at[1 - c],
                send_sem=send_sems.at[a], recv_sem=recv_sems.at[a],
                device_id=(x, y, 1 - c), device_id_type=MESH).wait()

    return pl.pallas_call(
        body, name=name,
        out_shape=[jax.ShapeDtypeStruct(b.shape, b.dtype) for b in bufs],
        in_specs=[_any()] * n_arr, out_specs=[_any()] * n_arr,
        input_output_aliases={a: a for a in range(n_arr)},
        scratch_shapes=[pltpu.SemaphoreType.DMA((n_arr,)), pltpu.SemaphoreType.DMA((n_arr,))],
    )(*bufs)


def _gelu_tanh(z):
    k = math.sqrt(2.0 / math.pi)
    t = jnp.tanh(k * (z + 0.044715 * (z * z * z)))
    return 0.5 * z * (1.0 + t), t


def _gelu_tanh_grad(z, t):
    k = math.sqrt(2.0 / math.pi)
    return 0.5 * (1.0 + t) + 0.5 * z * (1.0 - t * t) * (k * (1.0 + 3.0 * 0.044715 * (z * z)))


def _rope_angle_kernel(pos_row, invf_col):
    seq = pos_row.shape[1]

    def body(p_ref, f_ref, cos_ref, sin_ref):
        ang = p_ref[...].astype(F32) * f_ref[...]
        cos_ref[...] = jnp.cos(ang)
        sin_ref[...] = jnp.sin(ang)

    return pl.pallas_call(
        body, name="rope_angles", grid=(1,), out_shape=[jax.ShapeDtypeStruct((ROT_DIM // 2, seq), F32)] * 2,
        in_specs=[_full((1, seq)), _full((ROT_DIM // 2, 1))], out_specs=[_full((ROT_DIM // 2, seq))] * 2,
        compiler_params=_params("arbitrary"),
    )(pos_row, invf_col)


def _rope_lane_tables(cos, sin):
    cos_t, sin_t = cos.T, sin.T
    seq, half = cos_t.shape
    ones = jnp.ones((seq, HEAD_DIM - ROT_DIM), F32)
    c64 = jnp.concatenate([cos_t, cos_t, ones], axis=1)
    s1 = jnp.concatenate([sin_t, jnp.zeros((seq, HEAD_DIM - half), F32)], axis=1)
    s2 = jnp.concatenate([jnp.zeros((seq, half), F32), sin_t, jnp.zeros((seq, HEAD_DIM - ROT_DIM), F32)], axis=1)
    return jnp.concatenate([jnp.tile(t, (1, LANES // HEAD_DIM)) for t in (c64, s1, s2)], axis=1)


def _rope_apply(t, tab, sign):
    reps = t.shape[1] // LANES
    c_tab, s1, s2 = (jnp.tile(tab[:, LANES * k:LANES * (k + 1)], (1, reps)) if reps > 1
                     else tab[:, LANES * k:LANES * (k + 1)] for k in range(3))
    half = ROT_DIM // 2
    up = pltpu.roll(t, t.shape[1] - half, 1)
    down = pltpu.roll(t, half, 1)
    return t * c_tab + sign * (down * s2 - up * s1)


def _lane_masks(shape):
    lane = lax.broadcasted_iota(jnp.int32, shape, 1)
    return lane < HEAD_DIM, lane >= HEAD_DIM


HEADS_PER_GROUP = N_Q_HEADS // N_KV_HEADS
ATTN_SCALE = 1.0 / math.sqrt(HEAD_DIM)


def _attn_bias_t(first_block):
    kj = lax.broadcasted_iota(jnp.int32, (2 * CHUNK, CHUNK), 0)
    qi = lax.broadcasted_iota(jnp.int32, (2 * CHUNK, CHUNK), 1)
    ok = (kj > qi) & (kj <= qi + CHUNK)
    if first_block is not None:
        ok = ok & (jnp.logical_not(first_block) | (kj >= CHUNK))
    return jnp.tile(jnp.where(ok, 0.0, -jnp.inf), (1, HEADS_PER_GROUP))


def _group_rows(x, g, lo, hi):
    rows = []
    for r in range(HEADS_PER_GROUP):
        h = HEADS_PER_GROUP * g + r
        pair = x[:, LANES * (h // 2):LANES * (h // 2 + 1)]
        rows.append(jnp.where(hi if h % 2 else lo, pair, 0.0))
    return jnp.concatenate(rows, axis=0)


def _pairs_from_rows(rows, lo):
    return [jnp.where(lo, rows[2 * CHUNK * k:2 * CHUNK * k + CHUNK], rows[2 * CHUNK * k + CHUNK:2 * CHUNK * (k + 1)])
            for k in range(HEADS_PER_GROUP // 2)]


def _group_dup(a, b, g, lo2):
    return jnp.where(lo2, a, b) if g == 0 else jnp.where(lo2, b, a)


def _sink_row(sink_ref, g):
    return jnp.concatenate([sink_ref[HEADS_PER_GROUP * g + r:HEADS_PER_GROUP * g + r + 1, :]
                            for r in range(HEADS_PER_GROUP)], axis=1)


def _attn_probs_t(k_dup, q_rows, bias_t, sink_row):
    s_t = _dot_nt(k_dup, q_rows) * ATTN_SCALE + bias_t
    m = jnp.maximum(jnp.max(s_t, axis=0, keepdims=True), sink_row)
    p = jnp.exp(s_t - m)
    e_sink = jnp.exp(sink_row - m)
    inv = 1.0 / (jnp.sum(p, axis=0, keepdims=True) + e_sink)
    return p * inv, e_sink * inv


def _sgu_forward_pair(wm, vp, j):
    lo, hi = _lane_masks(vp.shape)
    lhs = jnp.concatenate([wm[2 * j], wm[2 * j + 1]], axis=1)
    rhs = jnp.concatenate([jnp.where(lo, vp, 0.0), jnp.where(hi, vp, 0.0)], axis=0)
    return _dot(lhs, rhs)


def _masked_spatial(w_ref):
    t = lax.broadcasted_iota(jnp.int32, (CHUNK, CHUNK), 0)
    s = lax.broadcasted_iota(jnp.int32, (CHUNK, CHUNK), 1)
    tril = s <= t
    return [jnp.where(tril, w_ref[g], 0.0) for g in range(GMLP_GROUPS)], tril, s >= t


def _mod_kernel(c_all, w_shard, b_shard, comm=None):
    n = w_shard.shape[1]
    tn = 512

    def body(c_ref, w_ref, b_ref, mod_ref, act_ref):
        cv = c_ref[...]
        act = cv * (1.0 / (1.0 + jnp.exp(-cv)))
        act_ref[...] = act
        mod_ref[...] = _dot(act, w_ref[...]) + b_ref[...]

    return _hosted_call(
        body, comm, name="ada_mod", grid=(n // tn,),
        out_shape=[jax.ShapeDtypeStruct((N_DEV, n), F32), jax.ShapeDtypeStruct((N_DEV, D_MODEL), F32)],
        in_specs=[_full((N_DEV, D_MODEL)), pl.BlockSpec((D_MODEL, tn), lambda i: (0, i)),
                  pl.BlockSpec((1, tn), lambda i: (0, i))],
        out_specs=[pl.BlockSpec((N_DEV, tn), lambda i: (0, i)), _full((N_DEV, D_MODEL))],
        semantics=("arbitrary",),
    )(c_all, w_shard, b_shard)


def _load_chip_blocks(chip_ref, gathered, local, dsts, sems, first_sem=0):
    for k, dst in enumerate(dsts):
        @pl.when(chip_ref[0] == k)
        def _():
            pltpu.make_async_copy(local, dst, sems.at[first_sem + k]).start()

        @pl.when(chip_ref[0] != k)
        def _():
            pltpu.make_async_copy(gathered.at[k], dst, sems.at[first_sem + k]).start()
    return [pltpu.make_async_copy(local, dst, sems.at[first_sem + k]).wait for k, dst in enumerate(dsts)]


def _in_proj_kernel(x, vecs, w_in_t, comm=None):
    seq = x.shape[0]
    tm = 512

    def body(x_ref, v_ref, w_ref, proj_ref, h_ref):
        xv = x_ref[...]
        rstd = lax.rsqrt(_mean_last(xv * xv) + EPS)
        n1 = (xv * rstd) * v_ref[0:1, :]
        h = n1 * (1.0 + v_ref[2:3, :]) + v_ref[1:2, :]
        hb = h.astype(MXU_DTYPE)
        h_ref[...] = hb
        proj_ref[...] = _dot_nt(hb, w_ref[...])

    return _hosted_call(
        body, comm, name="in_proj", grid=(seq // tm,),
        out_shape=[jax.ShapeDtypeStruct((seq, IN_PROJ_WIDTH), F32),
                   jax.ShapeDtypeStruct((seq, D_MODEL), MXU_DTYPE)],
        in_specs=[pl.BlockSpec((tm, D_MODEL), lambda i: (i, 0)), _full((8, D_MODEL)),
                  _full((IN_PROJ_WIDTH, D_MODEL))],
        out_specs=[pl.BlockSpec((tm, IN_PROJ_WIDTH), lambda i: (i, 0)),
                   pl.BlockSpec((tm, D_MODEL), lambda i: (i, 0))],
        semantics=("arbitrary",),
    )(x, vecs, w_in_t)


MIXER_BLOCKS_PER_STEP = 4
KV_START = 2 * GMLP_WIDTH + ATTN_WIDTH


def _mixer_fwd_kernel(proj, rope_tab, w_spatial, bias_full, sink_rows, comm=None):
    seq = proj.shape[0]
    per = MIXER_BLOCKS_PER_STEP
    steps = seq // (CHUNK * per)
    kv_col = KV_START // (2 * KV_WIDTH)

    def body(proj_ref, prev_ref, tab_ref, ptab_ref, w_ref, bias_ref, sink_ref, cat_ref):
        i = pl.program_id(0)
        wm, _, _ = _masked_spatial(w_ref)
        lo, hi = _lane_masks((CHUNK, LANES))
        lo2, _ = _lane_masks((2 * CHUNK, LANES))
        o = 2 * GMLP_WIDTH
        for s in range(per):
            rows, before = slice(CHUNK * s, CHUNK * (s + 1)), slice(CHUNK * (s - 1), CHUNK * s)
            for j in range(GMLP_GROUPS // 2):
                cols = slice(LANES * j, LANES * (j + 1))
                vcols = slice(GMLP_WIDTH + LANES * j, GMLP_WIDTH + LANES * (j + 1))
                u, _ = _gelu_tanh(proj_ref[rows, cols])
                vp, _ = _gelu_tanh(proj_ref[rows, vcols])
                sv = _sgu_forward_pair(wm, vp, j) + bias_ref[:, cols]
                cat_ref[rows, cols] = (u * sv).astype(cat_ref.dtype)
            tab = tab_ref[rows, :]
            if s == 0:
                prev_kv, prev_tab, first = prev_ref[...], ptab_ref[...], i == 0
            else:
                prev_kv, prev_tab, first = proj_ref[before, KV_START:KV_START + 2 * KV_WIDTH], tab_ref[before, :], None
            q_r = _rope_apply(proj_ref[rows, o:o + ATTN_WIDTH], tab, 1.0)
            k_cur = _rope_apply(proj_ref[rows, KV_START:KV_START + KV_WIDTH], tab, 1.0)
            k_prev = _rope_apply(prev_kv[:, 0:KV_WIDTH], prev_tab, 1.0)
            k_a = jnp.concatenate([k_prev, k_cur], axis=0)
            v_a = jnp.concatenate([prev_kv[:, KV_WIDTH:2 * KV_WIDTH],
                                   proj_ref[rows, KV_START + KV_WIDTH:KV_START + 2 * KV_WIDTH]], axis=0)
            k_b = pltpu.roll(k_a, HEAD_DIM, 1)
            v_b = pltpu.roll(v_a, HEAD_DIM, 1)
            bias_t = _attn_bias_t(first)
            for g in range(N_KV_HEADS):
                p_t, _ = _attn_probs_t(_group_dup(k_a, k_b, g, lo2), _group_rows(q_r, g, lo, hi), bias_t,
                                       _sink_row(sink_ref, g))
                o_t = _dot(_group_dup(v_a, v_b, g, lo2).T, p_t)
                for k, pair in enumerate(_pairs_from_rows(o_t.T, lo)):
                    c0 = GMLP_WIDTH + LANES * (2 * g + k)
                    cat_ref[rows, c0:c0 + LANES] = pair.astype(cat_ref.dtype)

    return _hosted_call(
        body, comm, name="mixer_fwd", grid=(steps,),
        out_shape=[jax.ShapeDtypeStruct((seq, D_MODEL), MXU_DTYPE)],
        in_specs=[pl.BlockSpec((CHUNK * per, IN_PROJ_WIDTH), lambda i: (i, 0)),
                  pl.BlockSpec((CHUNK, 2 * KV_WIDTH), lambda i: (jnp.maximum(per * i - 1, 0), kv_col)),
                  pl.BlockSpec((CHUNK * per, 3 * LANES), lambda i: (i, 0)),
                  pl.BlockSpec((CHUNK, 3 * LANES), lambda i: (jnp.maximum(per * i - 1, 0), 0)),
                  _full((GMLP_GROUPS, CHUNK, CHUNK)), _full((CHUNK, GMLP_WIDTH)),
                  _full((N_Q_HEADS, LANES))],
        out_specs=[pl.BlockSpec((CHUNK * per, D_MODEL), lambda i: (i, 0))],
        semantics=("arbitrary",),
    )(proj, proj, rope_tab, rope_tab, w_spatial, bias_full, sink_rows)


def _trunk_kernel(x, target, cat, vecs, chip_idx, gathered, local):
    seq = x.shape[0]
    tm = 256
    nj = D_FF // D_MODEL
    out_rows = D_MODEL // N_CHIPS

    def body(chip_ref, x_ref, t_ref, cat_ref, v_ref, g_out, g_w1, g_w2, l_out, l_w1, l_w2,
             dx1_ref, dcat_ref, dmix_ref, h2_ref, r_ref, da_ref, dff_ref, sums_ref,
             wout, w1, w2, a_scr, sem):
        i = pl.program_id(0)

        @pl.when(i == 0)
        def _():
            waits = _load_chip_blocks(chip_ref, g_out, l_out,
                                      [wout.at[pl.ds(out_rows * k, out_rows)] for k in range(N_CHIPS)], sem)
            waits += _load_chip_blocks(chip_ref, g_w1, l_w1, [w1.at[k] for k in range(N_CHIPS)], sem, N_CHIPS)
            waits += _load_chip_blocks(chip_ref, g_w2, l_w2, [w2.at[k] for k in range(N_CHIPS)], sem, 2 * N_CHIPS)
            for wait in waits:
                wait()
            sums_ref[...] = jnp.zeros_like(sums_ref)

        gate1, shift2, scale2 = v_ref[0:1, :], v_ref[1:2, :], v_ref[2:3, :]
        gate2, g_ffn, g_final = v_ref[3:4, :], v_ref[4:5, :], v_ref[5:6, :]

        mix = _dot(cat_ref[...], wout[...])
        x1 = x_ref[...] + gate1 * mix
        rstd2 = lax.rsqrt(_mean_last(x1 * x1) + EPS)
        xh2 = x1 * rstd2
        n2 = xh2 * g_ffn
        h2b = (n2 * (1.0 + scale2) + shift2).astype(MXU_DTYPE)
        h2_ref[...] = h2b
        ff = jnp.zeros((tm, D_MODEL), F32)
        for j in range(nj):
            a = _dot(h2b, w1[j])
            a_scr[j] = a
            relu = jnp.maximum(a, 0.0)
            rb = (relu * relu).astype(MXU_DTYPE)
            r_ref[:, D_MODEL * j:D_MODEL * (j + 1)] = rb
            ff = ff + _dot(rb, w2[j])
        x2 = x1 + gate2 * ff
        rstd3 = lax.rsqrt(_mean_last(x2 * x2) + EPS)
        xh3 = x2 * rstd3
        err = xh3 * g_final - t_ref[...]
        loss = 0.5 * _rowsum(_mean_last(err * err))
        dy = err * (1.0 / D_MODEL)
        dxh3 = dy * g_final
        dx2 = rstd3 * (dxh3 - xh3 * _mean_last(dxh3 * xh3))
        dffb = (dx2 * gate2).astype(MXU_DTYPE)
        dff_ref[...] = dffb
        dh2 = jnp.zeros((tm, D_MODEL), F32)
        for j in range(nj):
            dr = _dot_nt(dffb, w2[j])
            dab = (dr * (2.0 * jnp.maximum(a_scr[j], 0.0))).astype(MXU_DTYPE)
            da_ref[:, D_MODEL * j:D_MODEL * (j + 1)] = dab
            dh2 = dh2 + _dot_nt(dab, w1[j])
        dn2 = dh2 * (1.0 + scale2)
        dxh2 = dn2 * g_ffn
        dx1 = dx2 + rstd2 * (dxh2 - xh2 * _mean_last(dxh2 * xh2))
        dx1_ref[...] = dx1
        dmixb = (dx1 * gate1).astype(MXU_DTYPE)
        dmix_ref[...] = dmixb
        dcat_ref[...] = _dot_nt(dmixb, wout[...])

        sums_ref[0:1, :] += _rowsum(dh2)
        sums_ref[1:2, :] += _rowsum(dh2 * n2)
        sums_ref[2:3, :] += _rowsum(dx2 * ff)
        sums_ref[3:4, :] += _rowsum(dn2 * xh2)
        sums_ref[4:5, :] += _rowsum(dy * xh3)
        sums_ref[5:6, :] += _rowsum(dx1 * mix)
        sums_ref[6:7, :] += jnp.broadcast_to(loss, (1, D_MODEL))

    tok = lambda w: pl.BlockSpec((tm, w), lambda i, chip: (i, 0))
    return _hosted_call(
        body, None, name="trunk", grid=(seq // tm,), n_prefetch=1,
        out_shape=[jax.ShapeDtypeStruct((seq, D_MODEL), F32), jax.ShapeDtypeStruct((seq, D_MODEL), F32),
                   jax.ShapeDtypeStruct((seq, D_MODEL), MXU_DTYPE), jax.ShapeDtypeStruct((seq, D_MODEL), MXU_DTYPE),
                   jax.ShapeDtypeStruct((seq, D_FF), MXU_DTYPE), jax.ShapeDtypeStruct((seq, D_FF), MXU_DTYPE),
                   jax.ShapeDtypeStruct((seq, D_MODEL), MXU_DTYPE), jax.ShapeDtypeStruct((8, D_MODEL), F32)],
        in_specs=[tok(D_MODEL), tok(D_MODEL), tok(D_MODEL), _full((8, D_MODEL))] + [_any()] * 6,
        out_specs=[tok(D_MODEL), tok(D_MODEL), tok(D_MODEL), tok(D_MODEL), tok(D_FF), tok(D_FF), tok(D_MODEL),
                   _full((8, D_MODEL))],
        scratch_shapes=[pltpu.VMEM((D_MODEL, D_MODEL), MXU_DTYPE), pltpu.VMEM((nj, D_MODEL, D_MODEL), MXU_DTYPE),
                        pltpu.VMEM((nj, D_MODEL, D_MODEL), MXU_DTYPE), pltpu.VMEM((nj, tm, D_MODEL), F32),
                        pltpu.SemaphoreType.DMA((3 * N_CHIPS,))],
        semantics=("arbitrary",),
    )(chip_idx, x, target, cat, vecs, *gathered, *local)


def _mixer_bwd_kernel(proj, rope_tab, dcat, w_spatial, w_spatial_t, bias_full, sink_rows, dev_idx, comm=None):
    seq = proj.shape[0]
    per = MIXER_BLOCKS_PER_STEP
    steps = seq // (CHUNK * per)
    kv_col = KV_START // (2 * KV_WIDTH)

    def body(dev_ref, proj_ref, prev_ref, tab_ref, ptab_ref, dcat_ref, w_ref, wt_ref, bias_ref, sink_ref,
             dproj_ref, dw_out, db_ref, dsink_ref, carry, dw_ref):
        del dev_ref
        step = pl.program_id(0)

        @pl.when(step == 0)
        def _():
            carry[...] = jnp.zeros_like(carry)
            dw_ref[...] = jnp.zeros_like(dw_ref)
            db_ref[...] = jnp.zeros_like(db_ref)
            dsink_ref[...] = jnp.zeros_like(dsink_ref)

        for s in reversed(range(per)):
            rows = pl.ds(CHUNK * s, CHUNK)
            if s == 0:
                before, before_tab, first = prev_ref, ptab_ref, step == steps - 1
            else:
                before = proj_ref.at[pl.ds(CHUNK * (s - 1), CHUNK), pl.ds(KV_START, 2 * KV_WIDTH)]
                before_tab, first = tab_ref.at[pl.ds(CHUNK * (s - 1), CHUNK)], None
            one_block(proj_ref.at[rows], before, tab_ref.at[rows], before_tab, dcat_ref.at[rows], w_ref, wt_ref,
                      bias_ref, sink_ref, dproj_ref.at[rows], dw_ref, db_ref, dsink_ref, carry, first)

        @pl.when(step == steps - 1)
        def _():
            dw_out[...] = dw_ref[...].astype(dw_out.dtype)

    def one_block(proj_ref, prev_ref, tab_ref, ptab_ref, dcat_ref, w_ref, wt_ref, bias_ref, sink_ref,
                  dproj_ref, dw_ref, db_ref, dsink_ref, carry, first):
        wm, tril, triu = _masked_spatial(w_ref)
        lo, hi = _lane_masks((CHUNK, LANES))
        lane = lax.broadcasted_iota(jnp.int32, (CHUNK, LANES), 1)
        db = jnp.zeros((CHUNK, LANES), F32)
        for j in range(GMLP_GROUPS // 2):
            cols = slice(LANES * j, LANES * (j + 1))
            vcols = slice(GMLP_WIDTH + LANES * j, GMLP_WIDTH + LANES * (j + 1))
            zu, zv = proj_ref[:, cols], proj_ref[:, vcols]
            u, tu = _gelu_tanh(zu)
            vp, tv = _gelu_tanh(zv)
            sv = _sgu_forward_pair(wm, vp, j) + bias_ref[:, cols]
            dout = dcat_ref[:, cols]
            du = dout * sv
            dsv = dout * u
            dsv_lo, dsv_hi = jnp.where(lo, dsv, 0.0), jnp.where(hi, dsv, 0.0)
            lhs_t = jnp.concatenate([jnp.where(triu, wt_ref[2 * j], 0.0),
                                     jnp.where(triu, wt_ref[2 * j + 1], 0.0)], axis=1)
            dv = _dot(lhs_t, jnp.concatenate([dsv_lo, dsv_hi], axis=0))
            dw_ref[2 * j] += jnp.where(tril, _dot_nt(dsv_lo, vp), 0.0)
            dw_ref[2 * j + 1] += jnp.where(tril, _dot_nt(dsv_hi, vp), 0.0)
            db = db + (jnp.where(lane == 2 * j, jnp.sum(dsv_lo, axis=1, keepdims=True), 0.0)
                       + jnp.where(lane == 2 * j + 1, jnp.sum(dsv_hi, axis=1, keepdims=True), 0.0))
            dproj_ref[:, cols] = (du * _gelu_tanh_grad(zu, tu)).astype(dproj_ref.dtype)
            dproj_ref[:, vcols] = (dv * _gelu_tanh_grad(zv, tv)).astype(dproj_ref.dtype)
        db_ref[...] += db
        o = 2 * GMLP_WIDTH
        tab = tab_ref[...]
        q_r = _rope_apply(proj_ref[:, o:o + ATTN_WIDTH], tab, 1.0)
        k_cur = _rope_apply(proj_ref[:, o + ATTN_WIDTH:o + ATTN_WIDTH + KV_WIDTH], tab, 1.0)
        k_prev = _rope_apply(prev_ref[:, 0:KV_WIDTH], ptab_ref[...], 1.0)
        k_a = jnp.concatenate([k_prev, k_cur], axis=0)
        v_a = jnp.concatenate([prev_ref[:, KV_WIDTH:2 * KV_WIDTH],
                               proj_ref[:, o + ATTN_WIDTH + KV_WIDTH:o + ATTN_WIDTH + 2 * KV_WIDTH]], axis=0)
        k_b = pltpu.roll(k_a, HEAD_DIM, 1)
        v_b = pltpu.roll(v_a, HEAD_DIM, 1)
        bias_t = _attn_bias_t(first)
        lo2, _ = _lane_masks((2 * CHUNK, LANES))
        dout_b = dcat_ref[:, GMLP_WIDTH:GMLP_WIDTH + ATTN_WIDTH]
        dk_tot, dv_tot, dq_pairs = [], [], []
        for g in range(N_KV_HEADS):
            k_dup, v_dup = _group_dup(k_a, k_b, g, lo2), _group_dup(v_a, v_b, g, lo2)
            q_rows = _group_rows(q_r, g, lo, hi)
            do_rows = _group_rows(dout_b, g, lo, hi)
            p_t, p_sink = _attn_probs_t(k_dup, q_rows, bias_t, _sink_row(sink_ref, g))
            dp_t = _dot_nt(v_dup, do_rows)
            delta = jnp.sum(p_t * dp_t, axis=0, keepdims=True)
            ds_t = p_t * (dp_t - delta) * ATTN_SCALE
            dsink = -p_sink * delta
            for r in range(HEADS_PER_GROUP):
                h = HEADS_PER_GROUP * g + r
                dsink_ref[h:h + 1, :] += jnp.broadcast_to(
                    jnp.sum(dsink[:, LANES * r:LANES * (r + 1)], axis=1, keepdims=True), (1, LANES))
            dk_full = _dot(ds_t, q_rows)
            dv_full = _dot(p_t, do_rows)
            dk_tot.append(dk_full + pltpu.roll(dk_full, HEAD_DIM, 1))
            dv_tot.append(dv_full + pltpu.roll(dv_full, HEAD_DIM, 1))
            dq_t = _dot(k_dup.T, ds_t)
            dq_pairs += _pairs_from_rows(dq_t.T, lo)
        dk_all = jnp.where(lo2, dk_tot[0], dk_tot[1])
        dv_all = jnp.where(lo2, dv_tot[0], dv_tot[1])
        dk_cur = dk_all[CHUNK:, :] + carry[:, 0:KV_WIDTH]
        dv_cur = dv_all[CHUNK:, :] + carry[:, KV_WIDTH:2 * KV_WIDTH]
        carry[:, 0:KV_WIDTH] = dk_all[:CHUNK, :]
        carry[:, KV_WIDTH:2 * KV_WIDTH] = dv_all[:CHUNK, :]
        dq = _rope_apply(jnp.concatenate(dq_pairs, axis=1), tab, -1.0)
        dproj_ref[:, o:o + ATTN_WIDTH] = dq.astype(dproj_ref.dtype)
        dproj_ref[:, o + ATTN_WIDTH:o + ATTN_WIDTH + KV_WIDTH] = (
            _rope_apply(dk_cur, tab, -1.0).astype(dproj_ref.dtype))
        dproj_ref[:, o + ATTN_WIDTH + KV_WIDTH:o + ATTN_WIDTH + 2 * KV_WIDTH] = dv_cur.astype(dproj_ref.dtype)

    rev = lambda i: steps - 1 - i
    before = lambda i: jnp.maximum(per * rev(i) - 1, 0)
    slot = lambda shape: pl.BlockSpec((None,) + shape, lambda i, d: (d[0],) + (0,) * len(shape))
    return _hosted_call(
        body, comm, name="mixer_bwd", grid=(steps,), n_prefetch=1,
        out_shape=[jax.ShapeDtypeStruct((seq, IN_PROJ_WIDTH), MXU_DTYPE),
                   jax.ShapeDtypeStruct((N_DEV, GMLP_GROUPS, CHUNK, CHUNK), GRAD_COMM_DTYPE),
                   jax.ShapeDtypeStruct((N_DEV, CHUNK, LANES), F32),
                   jax.ShapeDtypeStruct((N_DEV, N_Q_HEADS, LANES), F32)],
        in_specs=[pl.BlockSpec((CHUNK * per, IN_PROJ_WIDTH), lambda i, d: (rev(i), 0)),
                  pl.BlockSpec((CHUNK, 2 * KV_WIDTH), lambda i, d: (before(i), kv_col)),
                  pl.BlockSpec((CHUNK * per, 3 * LANES), lambda i, d: (rev(i), 0)),
                  pl.BlockSpec((CHUNK, 3 * LANES), lambda i, d: (before(i), 0)),
                  pl.BlockSpec((CHUNK * per, D_MODEL), lambda i, d: (rev(i), 0)),
                  _full((GMLP_GROUPS, CHUNK, CHUNK)), _full((GMLP_GROUPS, CHUNK, CHUNK)),
                  _full((CHUNK, GMLP_WIDTH)), _full((N_Q_HEADS, LANES))],
        out_specs=[pl.BlockSpec((CHUNK * per, IN_PROJ_WIDTH), lambda i, d: (rev(i), 0)),
                   slot((GMLP_GROUPS, CHUNK, CHUNK)), slot((CHUNK, LANES)), slot((N_Q_HEADS, LANES))],
        scratch_shapes=[pltpu.VMEM((CHUNK, 2 * KV_WIDTH), F32), pltpu.VMEM((GMLP_GROUPS, CHUNK, CHUNK), F32)],
        semantics=("arbitrary",),
    )(dev_idx, proj, proj, rope_tab, rope_tab, dcat, w_spatial, w_spatial_t, bias_full, sink_rows)


def _in_proj_bwd_kernel(x, dx1, dproj, vecs, w_in_t, comm=None):
    seq = x.shape[0]
    tm = 512

    def body(x_ref, dx1_ref, dp_ref, v_ref, w_ref, gx_ref, sums_ref):
        @pl.when(pl.program_id(0) == 0)
        def _():
            sums_ref[...] = jnp.zeros_like(sums_ref)

        g_mix, scale1 = v_ref[0:1, :], v_ref[2:3, :]
        dh = _dot(dp_ref[...], w_ref[...])
        xv = x_ref[...]
        rstd = lax.rsqrt(_mean_last(xv * xv) + EPS)
        xh = xv * rstd
        dn1 = dh * (1.0 + scale1)
        dxh = dn1 * g_mix
        gx_ref[...] = dx1_ref[...] + rstd * (dxh - xh * _mean_last(dxh * xh))
        sums_ref[0:1, :] += _rowsum(dh)
        sums_ref[1:2, :] += _rowsum(dh * (xh * g_mix))
        sums_ref[2:3, :] += _rowsum(dn1 * xh)

    tok = lambda w: pl.BlockSpec((tm, w), lambda i: (i, 0))
    return _hosted_call(
        body, comm, name="in_proj_bwd", grid=(seq // tm,),
        out_shape=[jax.ShapeDtypeStruct((seq, D_MODEL), F32), jax.ShapeDtypeStruct((8, D_MODEL), F32)],
        in_specs=[tok(D_MODEL), tok(D_MODEL), tok(IN_PROJ_WIDTH), _full((8, D_MODEL)),
                  _full((IN_PROJ_WIDTH, D_MODEL))],
        out_specs=[tok(D_MODEL), _full((8, D_MODEL))],
        semantics=("arbitrary",),
    )(x, dx1, dproj, vecs, w_in_t)


class _GradTiles(NamedTuple):
    tm: int
    tn: int
    n_tiles: int
    chips_per_tile: int
    a_index: Callable
    b_index: Callable


def _weight_grad_kernel(a, b, c_idx, name, tiles, comm=None):
    seq = a.shape[0]
    tk = min(seq, 4096)
    nk = seq // tk
    tm, tn, n_tiles, per = tiles.tm, tiles.tn, tiles.n_tiles, tiles.chips_per_tile
    rows = tm // per

    def half(phase, c):
        return phase * c[0] + (1 - phase) * (1 - c[0])

    def body(c_ref, a_ref, b_ref, o_ref, acc, stage, landed, send_sems, recv_sems):
        del c_ref
        phase, t, kk = pl.program_id(0), pl.program_id(1), pl.program_id(2)
        x, y, c, _ = _mesh_place()

        def copy(tile):
            return pltpu.make_async_remote_copy(
                src_ref=stage.at[tile], dst_ref=landed.at[tile], send_sem=send_sems.at[tile],
                recv_sem=recv_sems.at[tile], device_id=(x, y, 1 - c), device_id_type=MESH)

        @pl.when(kk == 0)
        def _():
            acc[...] = jnp.zeros_like(acc)

        acc[...] += _dot_tn(a_ref[...], b_ref[...])

        @pl.when((kk == nk - 1) & (phase == 0))
        def _():
            stage[t] = acc[...].astype(stage.dtype)
            copy(t).start()

        @pl.when((kk == nk - 1) & (phase == 1))
        def _():
            copy(t).wait_recv()
            total = acc[...] + landed[t].astype(F32)
            for q in range(per):
                o_ref[q] = total[rows * q:rows * (q + 1)].astype(o_ref.dtype)

        @pl.when((kk == nk - 1) & (phase == 1) & (t == n_tiles - 1))
        def _():
            for tile in range(n_tiles):
                copy(tile).wait_send()

    out = _hosted_call(
        body, comm, name=name, grid=(2, n_tiles, nk), n_prefetch=1,
        out_shape=[jax.ShapeDtypeStruct((n_tiles * per, rows, tn), GRAD_COMM_DTYPE)],
        in_specs=[pl.BlockSpec((tk, tm), lambda p, t, k, c: (k, tiles.a_index(t, half(p, c)))),
                  pl.BlockSpec((tk, tn), lambda p, t, k, c: (k, tiles.b_index(t, half(p, c))))],
        out_specs=[pl.BlockSpec((per, rows, tn), lambda p, t, k, c: (p * t, 0, 0))],
        scratch_shapes=[pltpu.VMEM((tm, tn), F32), pltpu.VMEM((n_tiles, tm, tn), GRAD_COMM_DTYPE),
                        pltpu.VMEM((n_tiles, tm, tn), GRAD_COMM_DTYPE),
                        pltpu.SemaphoreType.DMA((n_tiles,)), pltpu.SemaphoreType.DMA((n_tiles,))],
        semantics=("arbitrary", "arbitrary", "arbitrary"),
    )(c_idx, a, b)
    return out[0] if comm is None else out


def _row_tile(rows, most=256, sublanes=16):
    return max(t for t in range(sublanes, most + 1, sublanes) if rows % t == 0)


def _adam_update(w, g, m, v):
    m_new = ADAM_B1 * m + (1.0 - ADAM_B1) * g
    v_new = ADAM_B2 * v + (1.0 - ADAM_B2) * (g * g)
    m_hat = m_new / (1.0 - ADAM_B1 ** ADAM_STEP)
    v_hat = v_new / (1.0 - ADAM_B2 ** ADAM_STEP)
    delta = -ADAM_LR * (m_hat / (jnp.sqrt(v_hat) + ADAM_EPS) + ADAM_WD * w)
    return delta, m_new, v_new


def _sum_chips_kernel(own, others, place, name):
    _, r, n = own.shape
    tr = _row_tile(r)

    def body(place_ref, own_ref, oth_ref, o_ref):
        del place_ref
        acc = own_ref[...].astype(F32)
        for k in range(N_CHIPS - 1):
            acc = acc + oth_ref[k].astype(F32)
        o_ref[...] = acc

    return pl.pallas_call(
        body, name=name, out_shape=jax.ShapeDtypeStruct((2, r, n), F32),
        grid_spec=pltpu.PrefetchScalarGridSpec(
            num_scalar_prefetch=1, grid=(r // tr,),
            in_specs=[pl.BlockSpec((None, tr, n), lambda i, p: (p[0], i, 0)),
                      pl.BlockSpec((N_CHIPS - 1, tr, n), lambda i, p: (0, i, 0))],
            out_specs=pl.BlockSpec((None, tr, n), lambda i, p: (p[1], i, 0))),
        compiler_params=_params("parallel"),
    )(place, own, others)


def _adam_kernel(w, g, m, v, name):
    r, n = w.shape
    by_columns = g.shape[1] == r
    tr, tn = _row_tile(g.shape[1], most=512), g.shape[2]

    def body(w_ref, g_ref, m_ref, v_ref, g_out, d_ref, mo_ref, vo_ref):
        gv = g_ref[...]
        g_out[...] = gv
        d_ref[...], mo_ref[...], vo_ref[...] = _adam_update(w_ref[...], gv, m_ref[...], v_ref[...])

    steps = g.shape[1] // tr
    spec = pl.BlockSpec((tr, tn), (lambda h, i: (i, h)) if by_columns else (lambda h, i: (h * steps + i, 0)))
    return pl.pallas_call(
        body, name=name, grid=(2, steps), out_shape=[jax.ShapeDtypeStruct((r, n), F32)] * 4,
        in_specs=[spec, pl.BlockSpec((None, tr, tn), lambda h, i: (h, i, 0)), spec, spec], out_specs=[spec] * 4,
        compiler_params=_params("parallel", "parallel"),
    )(w, g, m, v)


SMALL_PARAMS = ("b_ada", "g_mix", "g_ffn", "g_final", "b_spatial", "sinks", "w_spatial")


def _small_update_kernel(gathered, params):
    shapes = [params[nm][0].shape for nm in SMALL_PARAMS]

    def body(*refs):
        g_refs, refs = refs[:5], refs[5:]
        p_refs, refs = refs[:3 * len(SMALL_PARAMS)], refs[3 * len(SMALL_PARAMS):]
        loss_ref, o_refs = refs[0], refs[1:]

        def total(ref):
            acc = ref[0].astype(F32)
            for k in range(1, N_DEV):
                acc = acc + ref[k].astype(F32)
            return acc

        s1, s2, db, ds, dw = (total(r) for r in g_refs)
        loss_ref[...] = jnp.broadcast_to(s2[6:7, 0:1], loss_ref.shape)
        grads = {"b_ada": [s1[0:1], s1[1:2], s2[5:6], s2[0:1], s2[1:2], s2[2:3]], "g_mix": [s1[2:3]],
                 "g_ffn": [s2[3:4]], "g_final": [s2[4:5]], "b_spatial": [db.T[0:GMLP_GROUPS]],
                 "w_spatial": [dw]}
        lane = lax.broadcasted_iota(jnp.int32, (1, LANES), 1)
        sink_row = jnp.zeros((1, LANES), F32)
        for h in range(N_Q_HEADS):
            sink_row = sink_row + jnp.where(lane == h, ds[h:h + 1, :], 0.0)
        grads["sinks"] = [sink_row[:, 0:N_Q_HEADS]]
        for i, nm in enumerate(SMALL_PARAMS):
            w_ref, m_ref, v_ref = p_refs[3 * i:3 * i + 3]
            outs = o_refs[4 * i:4 * i + 4]
            width = grads[nm][0].shape[1]
            for k, g in enumerate(grads[nm]):
                cols = slice(width * k, width * (k + 1))
                upd = _adam_update(w_ref[:, cols], g, m_ref[:, cols], v_ref[:, cols])
                for o_ref, val in zip(outs, (g,) + upd):
                    o_ref[:, cols] = val

    flat = [a for nm in SMALL_PARAMS for a in params[nm]]
    out_shape = [jax.ShapeDtypeStruct((8, LANES), F32)]
    out_shape += [jax.ShapeDtypeStruct(s, F32) for s in shapes for _ in range(4)]
    outs = pl.pallas_call(
        body, name="small_update", grid=(1,), out_shape=out_shape,
        in_specs=[_full(g.shape) for g in gathered] + [_full(a.shape) for a in flat],
        out_specs=[_full(s.shape) for s in out_shape],
        compiler_params=_params("arbitrary"),
    )(*gathered, *flat)
    return {nm: outs[1 + 4 * i:5 + 4 * i] for i, nm in enumerate(SMALL_PARAMS)}, outs[0]


def _ada_update_kernel(act_t, dmod, w, m, v):
    r, n = w.shape
    tr = 256

    def body(a_ref, d_ref, w_ref, m_ref, v_ref, g_ref, dl_ref, mo_ref, vo_ref):
        g = _dot(a_ref[...], d_ref[...])
        g_ref[...] = g
        dl_ref[...], mo_ref[...], vo_ref[...] = _adam_update(w_ref[...], g, m_ref[...], v_ref[...])

    spec = pl.BlockSpec((tr, n), lambda i: (i, 0))
    return pl.pallas_call(
        body, name="ada_update", grid=(r // tr,), out_shape=[jax.ShapeDtypeStruct((r, n), F32)] * 4,
        in_specs=[pl.BlockSpec((tr, N_DEV), lambda i: (i, 0)), _full((N_DEV, n)), spec, spec, spec],
        out_specs=[spec] * 4, compiler_params=_params("parallel"),
    )(act_t, dmod, w, m, v)


def kernel(x, c, positions, w_ada, b_ada, g_mix, w_in, w_spatial, b_spatial, sinks, w_out, g_ffn, w_ff1, w_ff2, g_final, loss_target, m_w_ada, m_b_ada, m_g_mix, m_w_in, m_w_spatial, m_b_spatial, m_sinks, m_w_out, m_g_ffn, m_w_ff1, m_w_ff2, m_g_final, v_w_ada, v_b_ada, v_g_mix, v_w_in, v_w_spatial, v_b_spatial, v_sinks, v_w_out, v_g_ffn, v_w_ff1, v_w_ff2, v_g_final):
    xi, yi, ci = lax.axis_index("x"), lax.axis_index("y"), lax.axis_index("c")
    chip = 2 * xi + yi
    dev = 2 * chip + ci
    seq = x.shape[1]
    x2, tgt = x[0], loss_target[0]
    ada_cols = w_ada.shape[2]

    big = {"w_in": tuple(a[0].T for a in (w_in, m_w_in, v_w_in)),
           "w_out": (w_out[0], m_w_out[0], v_w_out[0]), "w_ff1": (w_ff1[0], m_w_ff1[0], v_w_ff1[0]),
           "w_ff2": (w_ff2[0], m_w_ff2[0], v_w_ff2[0])}

    def halves(nm):
        r, n = big[nm][0].shape
        return big[nm][0].astype(WEIGHT_COMM_DTYPE).reshape(2, r // 2, n)

    chip_idx = chip.reshape(1).astype(jnp.int32)
    c_all, w_in_t, g_out = _all_gather8([c, halves("w_in"), halves("w_out")], "gather_first",
                                        split=[False, True, True], skip_own=(2,))
    c_all, w_in_t = c_all.reshape(N_DEV, D_MODEL), w_in_t.reshape(IN_PROJ_WIDTH, D_MODEL)
    b_shard = lax.dynamic_slice(b_ada, (0, chip * ada_cols), (1, ada_cols))
    mod_part, act = _mod_kernel(c_all, w_ada[0], b_shard)
    mod_all, = _all_gather8([mod_part], "gather_mod")
    mod_me = lax.dynamic_index_in_dim(mod_all[0::2], dev, axis=1, keepdims=False)
    mod_me = mod_me.reshape(N_MOD, D_MODEL)
    shift1, scale1, gate1, shift2, scale2, gate2 = (mod_me[k:k + 1] for k in range(N_MOD))

    zeros_row = jnp.zeros((1, D_MODEL), F32)
    vecs1 = jnp.concatenate([g_mix, shift1, scale1] + [zeros_row] * 5, axis=0)
    vecs2 = jnp.concatenate([gate1, shift2, scale2, gate2, g_ffn, g_final.reshape(1, D_MODEL)]
                            + [zeros_row] * 2, axis=0)
    bias_full = jnp.repeat(b_spatial[0].T, HEAD_DIM, axis=1)
    sink_rows = jnp.broadcast_to(sinks[0][:, None], (N_Q_HEADS, LANES))
    inv_freq = ROPE_THETA ** (-jnp.arange(0, ROT_DIM, 2, dtype=F32) / ROT_DIM)
    rope_tab = _rope_lane_tables(*_rope_angle_kernel(positions, inv_freq.reshape(ROT_DIM // 2, 1)))

    trunk_weights = ["w_out", "w_ff1", "w_ff2"]
    shards = [halves(nm) for nm in trunk_weights]
    proj, hb, *staged = _in_proj_kernel(x2, vecs1, w_in_t, comm=_gather2d_first(shards[1:]))
    cat, *staged = _mixer_fwd_kernel(proj, rope_tab, w_spatial[0], bias_full, sink_rows,
                                     comm=_gather2d_second(staged, shards[1:]))
    staged = [g_out] + list(_gather_forward(staged, "gather_forward"))
    dx1, dcat, dmix, h2b, rb, dab, dffb, sums2 = _trunk_kernel(
        x2, tgt, cat, vecs2, chip_idx,
        [g.reshape((N_CHIPS,) + big[nm][0].shape) for nm, g in zip(trunk_weights, staged)],
        [s.reshape(big[nm][0].shape) for nm, s in zip(trunk_weights, shards)])

    c_idx = ci.reshape(1).astype(jnp.int32)
    place = jnp.stack([chip, ci]).astype(jnp.int32)
    half_d = D_MODEL // 2
    cs_ff2 = _weight_grad_kernel(rb, dffb, c_idx, "dw_ff2",
                                 _GradTiles(D_MODEL, half_d, N_CHIPS, 1, lambda t, h: t, lambda t, h: h))
    split_row = 7 * D_MODEL // 8
    cs_ff1, sc_ff2 = _weight_grad_kernel(
        h2b, dab, c_idx, "dw_ff1",
        _GradTiles(D_MODEL, half_d, N_CHIPS, 1, lambda t, h: 0, lambda t, h: 2 * t + h),
        comm=_scatter_job([cs_ff2], rows=(0, split_row)))
    cs_out = _weight_grad_kernel(cat, dmix, c_idx, "dw_out",
                                 _GradTiles(D_MODEL, half_d, 1, N_CHIPS, lambda t, h: 0, lambda t, h: h))
    dproj, dw_spatial, db_lanes, dsink_rows, sc_ff2, sc_ff1, sc_out = _mixer_bwd_kernel(
        proj, rope_tab, dcat, w_spatial[0], w_spatial[0].transpose(0, 2, 1), bias_full, sink_rows,
        dev.reshape(1).astype(jnp.int32),
        comm=_merge_jobs(_scatter_job([cs_ff1, cs_out]),
                         _scatter_job([cs_ff2], rows=(split_row, D_MODEL - split_row), into=[sc_ff2])))
    totals = [_sum_chips_kernel(own, oth, place, "grad_sum_" + nm)
              for nm, own, oth in (("w_out", cs_out, sc_out), ("w_ff1", cs_ff1, sc_ff1), ("w_ff2", cs_ff2, sc_ff2))]
    small_slots = [db_lanes, dsink_rows, dw_spatial.reshape(N_DEV, GMLP_GROUPS * CHUNK, CHUNK)]
    cs_in, *rode = _weight_grad_kernel(
        dproj, hb, c_idx, "dw_in",
        _GradTiles(2 * W_IN_BLOCK, half_d, N_CHIPS // 2, 2, lambda t, h: t, lambda t, h: h),
        comm=_merge_jobs(_gather_job(small_slots), _share_job(totals)))
    small_stage1, shared = rode[:len(small_slots)], rode[len(small_slots):]
    grad_x, sums1 = _in_proj_bwd_kernel(x2, dx1, dproj, vecs1, w_in_t)
    *gathered, sc_in = _all_gather8([sums1, sums2], "gather_small", forward=small_stage1,
                                    riders=[_scatter_job([cs_in])])
    total_in = _sum_chips_kernel(cs_in, sc_in, place, "grad_sum_w_in")
    shared = list(_sibling_share([total_in], "grad_share_w_in")) + list(shared)
    names = ["w_in", "w_out", "w_ff1", "w_ff2"]
    big_out = {}
    for nm, g in zip(names, shared):
        w, m, v = big[nm]
        outs = _adam_kernel(w, g, m, v, "adam_" + nm)
        big_out[nm] = tuple((t.T if nm == "w_in" else t)[None] for t in outs)

    small = {"b_ada": (b_ada, m_b_ada, v_b_ada), "g_mix": (g_mix, m_g_mix, v_g_mix),
             "g_ffn": (g_ffn, m_g_ffn, v_g_ffn), "g_final": (g_final, m_g_final, v_g_final),
             "b_spatial": (b_spatial, m_b_spatial, v_b_spatial), "sinks": (sinks, m_sinks, v_sinks),
             "w_spatial": (w_spatial, m_w_spatial, v_w_spatial)}
    flat_shape = {"g_final": (1, D_MODEL), "b_spatial": (GMLP_GROUPS, CHUNK), "w_spatial": (GMLP_GROUPS * CHUNK, CHUNK)}
    small_out, loss_tile = _small_update_kernel(
        gathered, {nm: tuple(a.reshape(flat_shape.get(nm, a.shape)) for a in small[nm]) for nm in small})
    small_out = {nm: [o.reshape(small[nm][0].shape) for o in small_out[nm]] for nm in small}
    loss = loss_tile[0, 0]

    g1, g2 = gathered[0], gathered[1]
    dmod_all = jnp.concatenate([g1[:, 0], g1[:, 1], g2[:, 5], g2[:, 0], g2[:, 1], g2[:, 2]], axis=1)
    dmod_cols = lax.dynamic_slice(dmod_all, (0, chip * ada_cols), (N_DEV, ada_cols))
    ada = _ada_update_kernel(act.T, dmod_cols, w_ada[0], m_w_ada[0], v_w_ada[0])
    big_out["w_ada"] = tuple(t[None] for t in ada)

    order = ["w_ada", "b_ada", "g_mix", "w_in", "w_spatial", "b_spatial", "sinks", "w_out", "g_ffn",
             "w_ff1", "w_ff2", "g_final"]

    def leaf(nm, k):
        return big_out[nm][k] if nm in big_out else small_out[nm][k]

    outs = [loss, grad_x[None]]
    for k in range(4):
        outs += [leaf(nm, k) for nm in order]
    return tuple(outs)
```

```python
import math
from typing import Callable, NamedTuple

import jax
import jax.numpy as jnp
from jax import lax
from jax.experimental import pallas as pl
from jax.experimental.pallas import tpu as pltpu

F32 = jnp.float32
MXU_DTYPE = jnp.bfloat16
WEIGHT_COMM_DTYPE = jnp.bfloat16
GRAD_COMM_DTYPE = jnp.bfloat16

D_MODEL = 1024
D_FF = 4096
HEAD_DIM = 64
GMLP_GROUPS = 8
GMLP_WIDTH = 512
CHUNK = 128
N_Q_HEADS = 8
N_KV_HEADS = 2
ATTN_WIDTH = 512
KV_WIDTH = 128
ROT_DIM = 16
ROPE_THETA = 500000.0
IN_PROJ_WIDTH = 1792
N_MOD = 6
EPS = 1e-5
N_CHIPS = 4
N_DEV = 8
LANES = 128
W_IN_BLOCK = IN_PROJ_WIDTH // N_CHIPS

ADAM_LR = 0.001
ADAM_B1 = 0.9
ADAM_B2 = 0.999
ADAM_EPS = 1e-08
ADAM_WD = 0.01
ADAM_STEP = 10

VMEM_LIMIT_BYTES = 58 * 1024 * 1024
MESH = pl.DeviceIdType.MESH


def _params(*semantics):
    return pltpu.CompilerParams(dimension_semantics=semantics, vmem_limit_bytes=VMEM_LIMIT_BYTES)


def _dot(a, b):
    return jnp.dot(a.astype(MXU_DTYPE), b.astype(MXU_DTYPE), preferred_element_type=F32)


def _dot_nt(a, b):
    return lax.dot_general(a.astype(MXU_DTYPE), b.astype(MXU_DTYPE), (((1,), (1,)), ((), ())),
                           preferred_element_type=F32)


def _dot_tn(a, b):
    return lax.dot_general(a.astype(MXU_DTYPE), b.astype(MXU_DTYPE), (((0,), (0,)), ((), ())),
                           preferred_element_type=F32)


def _full(shape):
    return pl.BlockSpec(shape, lambda *_: (0,) * len(shape))


def _any():
    return pl.BlockSpec(memory_space=pl.ANY)


def _rowsum(v):
    return jnp.sum(v, axis=0, keepdims=True)


def _mean_last(v):
    return jnp.mean(v, axis=-1, keepdims=True)


class _Comm(NamedTuple):
    operands: tuple
    out_shapes: tuple
    n_sems: int
    make: Callable
    in_place: int = 0


def _hosted_call(body, comm, *, name, grid, in_specs, out_shape, out_specs, scratch_shapes=(), semantics,
                 n_prefetch=0):
    if comm is None:
        return pl.pallas_call(
            body, name=name, out_shape=out_shape, compiler_params=_params(*semantics),
            grid_spec=pltpu.PrefetchScalarGridSpec(
                num_scalar_prefetch=n_prefetch, grid=grid, in_specs=in_specs, out_specs=out_specs,
                scratch_shapes=list(scratch_shapes)))
    n_in, n_out, n_scr = len(in_specs), len(out_shape), len(scratch_shapes)
    k_in, k_out = len(comm.operands), len(comm.out_shapes)

    def hosted(*refs):
        prefetched, refs = refs[:n_prefetch], refs[n_prefetch:]
        ins, refs = refs[:n_in], refs[n_in:]
        c_ins, refs = refs[:k_in], refs[k_in:]
        outs, refs = refs[:n_out], refs[n_out:]
        c_outs, refs = refs[:k_out], refs[k_out:]
        scratch, (send_sems, recv_sems) = refs[:n_scr], refs[n_scr:]
        first, last = None, None
        for d, size in enumerate(grid):
            at_start, at_end = pl.program_id(d) == 0, pl.program_id(d) == size - 1
            first = at_start if first is None else first & at_start
            last = at_end if last is None else last & at_end

        @pl.when(first)
        def _():
            for cp in comm.make(c_ins, c_outs, send_sems, recv_sems)[0]:
                cp.start()

        body(*prefetched, *ins, *outs, *scratch)

        @pl.when(last)
        def _():
            for wait in comm.make(c_ins, c_outs, send_sems, recv_sems)[1]:
                wait()

    aliases = {n_prefetch + n_in + i: n_out + i for i in range(comm.in_place)}
    call = pl.pallas_call(
        hosted, name=name, out_shape=list(out_shape) + list(comm.out_shapes),
        compiler_params=_params(*semantics), input_output_aliases=aliases,
        grid_spec=pltpu.PrefetchScalarGridSpec(
            num_scalar_prefetch=n_prefetch, grid=grid, in_specs=list(in_specs) + [_any()] * k_in,
            out_specs=list(out_specs) + [_any()] * k_out,
            scratch_shapes=list(scratch_shapes) + [pltpu.SemaphoreType.DMA((comm.n_sems,)),
                                                    pltpu.SemaphoreType.DMA((comm.n_sems,))]))
    return lambda *args: call(*args, *comm.operands)


class _Shifted:
    def __init__(self, base, offset):
        self.base, self.offset = base, offset

    @property
    def at(self):
        return self

    def __getitem__(self, k):
        return self.base.at[self.offset + k]


def _merge_jobs(*jobs):
    def order(count):
        first = [(j, i) for j, job in enumerate(jobs) for i in range(job.in_place)]
        return first + [(j, i) for j, job in enumerate(jobs) for i in range(job.in_place, count(job))]

    op_order, out_order = order(lambda job: len(job.operands)), order(lambda job: len(job.out_shapes))

    def make(ins, outs, send_sems, recv_sems):
        starts, waits, sem = [], [], 0
        for j, job in enumerate(jobs):
            mine_in = [ins[k] for k, (jj, _) in enumerate(op_order) if jj == j]
            mine_out = [outs[k] for k, (jj, _) in enumerate(out_order) if jj == j]
            s, w = job.make(mine_in, mine_out, _Shifted(send_sems, sem), _Shifted(recv_sems, sem))
            starts, waits, sem = starts + s, waits + w, sem + job.n_sems
        return starts, waits

    return _Comm(tuple(jobs[j].operands[i] for j, i in op_order), tuple(jobs[j].out_shapes[i] for j, i in out_order),
                 sum(job.n_sems for job in jobs), make, in_place=sum(job.in_place for job in jobs))


def _mesh_place():
    x, y, c = lax.axis_index("x"), lax.axis_index("y"), lax.axis_index("c")
    return x, y, c, [(1 - x, y), (x, 1 - y), (1 - x, 1 - y)]


def _gather_job(bufs):
    per = 4

    def make(ins, outs, send_sems, recv_sems):
        del ins
        x, y, c, chips = _mesh_place()
        starts, waits = [], []
        for a, out in enumerate(outs):
            mine = src = out.at[4 * x + 2 * y + c]
            to = [(x, y, 1 - c)] + [(px, py, c) for px, py in chips]
            sends = [pltpu.make_async_remote_copy(
                src_ref=src, dst_ref=mine, send_sem=send_sems.at[per * a + k],
                recv_sem=recv_sems.at[per * a + k], device_id=dev, device_id_type=MESH)
                for k, dev in enumerate(to)]
            recvs = [pltpu.make_async_remote_copy(
                src_ref=src, dst_ref=out.at[4 * px + 2 * py + pc], send_sem=send_sems.at[per * a + k],
                recv_sem=recv_sems.at[per * a + k], device_id=(px, py, pc), device_id_type=MESH)
                for k, (px, py, pc) in enumerate(to)]
            starts += sends
            waits += [s.wait_send for s in sends] + [r.wait_recv for r in recvs]
        return starts, waits

    shapes = tuple(jax.ShapeDtypeStruct(b.shape, b.dtype) for b in bufs)
    return _Comm(tuple(bufs), shapes, per * len(bufs), make, in_place=len(bufs))


def _slots(x, y, c):
    return 4 * x + 2 * y + c, 4 * (1 - x) + 2 * y + c, 4 * x + 2 * (1 - y) + c, 4 * (1 - x) + 2 * (1 - y) + c


def _gather2d_first(halves):
    per = 2

    def make(ins, outs, send_sems, recv_sems):
        x, y, c, _ = _mesh_place()
        me, xn, yn, _ = _slots(x, y, c)
        starts, waits = [], []
        for a, (src, out) in enumerate(zip(ins, outs)):
            blk = src.at[c]
            rows = blk.shape[0] // 2
            upper, lower = pl.ds(0, rows), pl.ds(rows, rows)

            def copy(k, src_ref, dst_ref, dev, a=a):
                return pltpu.make_async_remote_copy(
                    src_ref=src_ref, dst_ref=dst_ref, send_sem=send_sems.at[per * a + k],
                    recv_sem=recv_sems.at[per * a + k], device_id=dev, device_id_type=MESH)

            sends = [copy(0, blk.at[upper], out.at[me, upper], (1 - x, y, c)),
                     copy(1, blk.at[lower], out.at[me, lower], (x, 1 - y, c))]
            recvs = [copy(0, blk.at[upper], out.at[xn, upper], (1 - x, y, c)),
                     copy(1, blk.at[lower], out.at[yn, lower], (x, 1 - y, c))]
            starts += sends
            waits += [s.wait_send for s in sends] + [r.wait_recv for r in recvs]
        return starts, waits

    shapes = tuple(jax.ShapeDtypeStruct((N_DEV,) + h.shape[1:], h.dtype) for h in halves)
    return _Comm(tuple(halves), shapes, per * len(halves), make)


def _gather2d_second(bufs, halves):
    per = 4
    n_arr = len(bufs)

    def make(ins, outs, send_sems, recv_sems):
        x, y, c, _ = _mesh_place()
        me, xn, yn, dg = _slots(x, y, c)
        starts, waits = [], []
        for a, buf in enumerate(outs):
            own = ins[n_arr + a].at[c]
            rows = buf.shape[1] // 2
            upper, lower = pl.ds(0, rows), pl.ds(rows, rows)
            plan = [(own.at[upper], me, upper, (x, 1 - y, c), yn), (buf.at[xn, upper], xn, upper, (x, 1 - y, c), dg),
                    (own.at[lower], me, lower, (1 - x, y, c), xn), (buf.at[yn, lower], yn, lower, (1 - x, y, c), dg)]
            for k, (src, slot, part, dev, landing) in enumerate(plan):
                sems = dict(send_sem=send_sems.at[per * a + k], recv_sem=recv_sems.at[per * a + k],
                            device_id=dev, device_id_type=MESH)
                send = pltpu.make_async_remote_copy(src_ref=src, dst_ref=buf.at[slot, part], **sems)
                arrival = pltpu.make_async_remote_copy(src_ref=src, dst_ref=buf.at[landing, part], **sems)
                starts.append(send)
                waits += [send.wait_send, arrival.wait_recv]
        return starts, waits

    shapes = tuple(jax.ShapeDtypeStruct(b.shape, b.dtype) for b in bufs)
    return _Comm(tuple(bufs) + tuple(halves), shapes, per * n_arr, make, in_place=n_arr)


def _gather_forward(bufs, name):
    n_arr = len(bufs)

    def body(*refs):
        outs = refs[n_arr:2 * n_arr]
        send_sems, recv_sems = refs[2 * n_arr:]
        x, y, c, chips = _mesh_place()
        sends, recvs = [], []
        for a, buf in enumerate(outs):
            for j, (px, py) in enumerate(chips):
                mine, theirs = buf.at[4 * px + 2 * py + c], buf.at[4 * px + 2 * py + 1 - c]
                sems = dict(send_sem=send_sems.at[3 * a + j], recv_sem=recv_sems.at[3 * a + j],
                            device_id=(x, y, 1 - c), device_id_type=MESH)
                sends.append(pltpu.make_async_remote_copy(src_ref=mine, dst_ref=mine, **sems))
                recvs.append(pltpu.make_async_remote_copy(src_ref=mine, dst_ref=theirs, **sems))
        for cp in sends:
            cp.start()
        for s, r in zip(sends, recvs):
            s.wait_send()
            r.wait_recv()

    return pl.pallas_call(
        body, name=name, out_shape=[jax.ShapeDtypeStruct(b.shape, b.dtype) for b in bufs],
        in_specs=[_any()] * n_arr, out_specs=[_any()] * n_arr,
        input_output_aliases={a: a for a in range(n_arr)},
        scratch_shapes=[pltpu.SemaphoreType.DMA((3 * n_arr,)), pltpu.SemaphoreType.DMA((3 * n_arr,))],
    )(*bufs)


def _scatter_job(chip_sums, rows=None, into=()):
    n_into = len(into)

    def part(ref):
        return ref if rows is None else ref.at[pl.ds(rows[0], rows[1])]

    def make(ins, outs, send_sems, recv_sems):
        x, y, c, chips = _mesh_place()
        copies = [pltpu.make_async_remote_copy(
            src_ref=part(src.at[2 * px + py]), dst_ref=part(out.at[j]), send_sem=send_sems.at[3 * a + j],
            recv_sem=recv_sems.at[3 * a + j], device_id=(px, py, c), device_id_type=MESH)
            for a, (src, out) in enumerate(zip(ins[n_into:], outs)) for j, (px, py) in enumerate(chips)]
        return copies, [cp.wait for cp in copies]

    shapes = tuple(jax.ShapeDtypeStruct((3,) + s.shape[1:], s.dtype) for s in chip_sums)
    return _Comm(tuple(into) + tuple(chip_sums), shapes, 3 * len(chip_sums), make, in_place=n_into)


def _all_gather8(blocks, name, split=False, forward=(), riders=(), skip_own=()):
    n_arr, n_fwd = len(blocks), len(forward)
    splits = list(split) if isinstance(split, (list, tuple)) else [split] * n_arr
    own_slots = [a not in skip_own for a in range(n_arr)]
    rider_in = sum(len(r.operands) for r in riders)
    rider_out = sum(len(r.out_shapes) for r in riders)

    def body(*refs):
        x_refs, refs = refs[:n_arr], refs[n_arr + n_fwd:]
        r_ins, refs = refs[:rider_in], refs[rider_in:]
        out_refs, refs = refs[:n_arr], refs[n_arr:]
        fwd_refs, refs = refs[:n_fwd], refs[n_fwd:]
        r_outs, refs = refs[:rider_out], refs[rider_out:]
        (send_sems, recv_sems, local_sems), rider_sems = refs[:3], refs[3:]
        x, y, c, chips = _mesh_place()
        me, sibling = (x, y, c), (x, y, 1 - c)
        rider_waits, i0, o0 = [], 0, 0
        for n, job in enumerate(riders):
            k_in, k_out = len(job.operands), len(job.out_shapes)
            starts, waits = job.make(r_ins[i0:i0 + k_in], r_outs[o0:o0 + k_out],
                                     rider_sems[2 * n], rider_sems[2 * n + 1])
            for cp in starts:
                cp.start()
            rider_waits += waits
            i0, o0 = i0 + k_in, o0 + k_out
        passing = []
        for f, buf in enumerate(fwd_refs):
            for j, (px, py) in enumerate(chips):
                mine, theirs = buf.at[4 * px + 2 * py + c], buf.at[4 * px + 2 * py + 1 - c]
                sems = dict(send_sem=send_sems.at[7 * n_arr + 3 * f + j], recv_sem=recv_sems.at[7 * n_arr + 3 * f + j],
                            device_id=sibling, device_id_type=MESH)
                passing.append((pltpu.make_async_remote_copy(src_ref=mine, dst_ref=mine, **sems),
                                pltpu.make_async_remote_copy(src_ref=mine, dst_ref=theirs, **sems)))
        for send, _ in passing:
            send.start()
        arrays = []
        for a, (x_ref, out_ref) in enumerate(zip(x_refs, out_refs)):
            src_mine = x_ref.at[c] if splits[a] else x_ref

            def copy(k, blk, to, src=None, a=a, out_ref=out_ref):
                dst = out_ref.at[4 * blk[0] + 2 * blk[1] + blk[2]]
                return pltpu.make_async_remote_copy(
                    src_ref=dst if src is None else src, dst_ref=dst,
                    send_sem=send_sems.at[7 * a + k], recv_sem=recv_sems.at[7 * a + k],
                    device_id=to, device_id_type=MESH)

            mine = pltpu.make_async_copy(src_mine, out_ref.at[4 * x + 2 * y + c], local_sems.at[a])
            first = [copy(0, me, sibling, src=src_mine)] if own_slots[a] else []
            first += [copy(1 + j, me, (*chip, c), src=src_mine) for j, chip in enumerate(chips)]
            for cp in first + ([mine] if own_slots[a] else []):
                cp.start()
            arrays.append((copy, mine, first, own_slots[a]))
        sent = []
        for copy, mine, first, own in arrays:
            passed = [copy(4 + j, (*chip, c), sibling) for j, chip in enumerate(chips)]
            for j, chip in enumerate(chips):
                copy(1 + j, (*chip, c), me).wait_recv()
                passed[j].start()
            sent += first + passed
        for copy, mine, first, own in arrays:
            if own:
                copy(0, sibling, me).wait_recv()
                mine.wait()
            for j, chip in enumerate(chips):
                copy(4 + j, (*chip, 1 - c), me).wait_recv()
        for cp in sent:
            cp.wait_send()
        for send, arrival in passing:
            send.wait_send()
            arrival.wait_recv()
        for wait in rider_waits:
            wait()

    n_sems = 7 * n_arr + 3 * n_fwd
    rider_operands = [a for r in riders for a in r.operands]
    rider_shapes = [s for r in riders for s in r.out_shapes]
    return pl.pallas_call(
        body, name=name,
        out_shape=[jax.ShapeDtypeStruct((N_DEV,) + tuple(b.shape[1:] if s else b.shape), b.dtype)
                   for b, s in zip(blocks, splits)]
        + [jax.ShapeDtypeStruct(f.shape, f.dtype) for f in forward] + rider_shapes,
        in_specs=[_any()] * (n_arr + n_fwd + rider_in), out_specs=[_any()] * (n_arr + n_fwd + rider_out),
        input_output_aliases={n_arr + f: n_arr + f for f in range(n_fwd)},
        scratch_shapes=[pltpu.SemaphoreType.DMA((n_sems,)), pltpu.SemaphoreType.DMA((n_sems,)),
                        pltpu.SemaphoreType.DMA((n_arr,))]
        + [pltpu.SemaphoreType.DMA((r.n_sems,)) for r in riders for _ in range(2)],
    )(*blocks, *forward, *rider_operands)


def _share_job(bufs):
    def make(ins, outs, send_sems, recv_sems):
        del ins
        x, y, c, _ = _mesh_place()
        sems = lambda a: dict(send_sem=send_sems.at[a], recv_sem=recv_sems.at[a],
                              device_id=(x, y, 1 - c), device_id_type=MESH)
        sends = [pltpu.make_async_remote_copy(src_ref=o.at[c], dst_ref=o.at[c], **sems(a)) for a, o in enumerate(outs)]
        arrivals = [pltpu.make_async_remote_copy(src_ref=o.at[c], dst_ref=o.at[1 - c], **sems(a))
                    for a, o in enumerate(outs)]
        return sends, [s.wait_send for s in sends] + [r.wait_recv for r in arrivals]

    shapes = tuple(jax.ShapeDtypeStruct(b.shape, b.dtype) for b in bufs)
    return _Comm(tuple(bufs), shapes, len(bufs), make, in_place=len(bufs))


def _sibling_share(bufs, name):
    n_arr = len(bufs)

    def body(*refs):
        out_refs = refs[n_arr:2 * n_arr]
        send_sems, recv_sems = refs[2 * n_arr:]
        x, y, c = lax.axis_index("x"), lax.axis_index("y"), lax.axis_index("c")
        copies = [pltpu.make_async_remote_copy(
            src_ref=out_refs[a].at[c], dst_ref=out_refs[a].at[c],
            send_sem=send_sems.at[a], recv_sem=recv_sems.at[a],
            device_id=(x, y, 1 - c), device_id_type=MESH) for a in range(n_arr)]
        for cp in copies:
            cp.start()
        for a in range(n_arr):
            pltpu.make_async_remote_copy(
                src_ref=out_refs[a].at[c], dst_ref=out_refs[a].at[1 - c],
                send_sem=send_sems.at[a], recv_sem=recv_sems.at[a],
                device_id=(x, y, 1 - c), device_id_type=MESH).wait()

    return pl.pallas_call(
        body, name=name,
        out_shape=[jax.ShapeDtypeStruct(b.shape, b.dtype) for b in bufs],
        in_specs=[_any()] * n_arr, out_specs=[_any()] * n_arr,
        input_output_aliases={a: a for a in range(n_arr)},
        scratch_shapes=[pltpu.SemaphoreType.DMA((n_arr,)), pltpu.SemaphoreType.DMA((n_arr,))],
    )(*bufs)


def _gelu_tanh(z):
    k = math.sqrt(2.0 / math.pi)
    t = jnp.tanh(k * (z + 0.044715 * (z * z * z)))
    return 0.5 * z * (1.0 + t), t


def _gelu_tanh_grad(z, t):
    k = math.sqrt(2.0 / math.pi)
    return 0.5 * (1.0 + t) + 0.5 * z * (1.0 - t * t) * (k * (1.0 + 3.0 * 0.044715 * (z * z)))


def _rope_angle_kernel(pos_row, invf_col):
    seq = pos_row.shape[1]

    def body(p_ref, f_ref, cos_ref, sin_ref):
        ang = p_ref[...].astype(F32) * f_ref[...]
        cos_ref[...] = jnp.cos(ang)
        sin_ref[...] = jnp.sin(ang)

    return pl.pallas_call(
        body, name="rope_angles", grid=(1,), out_shape=[jax.ShapeDtypeStruct((ROT_DIM // 2, seq), F32)] * 2,
        in_specs=[_full((1, seq)), _full((ROT_DIM // 2, 1))], out_specs=[_full((ROT_DIM // 2, seq))] * 2,
        compiler_params=_params("arbitrary"),
    )(pos_row, invf_col)


def _rope_lane_tables(cos, sin):
    cos_t, sin_t = cos.T, sin.T
    seq, half = cos_t.shape
    ones = jnp.ones((seq, HEAD_DIM - ROT_DIM), F32)
    c64 = jnp.concatenate([cos_t, cos_t, ones], axis=1)
    s1 = jnp.concatenate([sin_t, jnp.zeros((seq, HEAD_DIM - half), F32)], axis=1)
    s2 = jnp.concatenate([jnp.zeros((seq, half), F32), sin_t, jnp.zeros((seq, HEAD_DIM - ROT_DIM), F32)], axis=1)
    return jnp.concatenate([jnp.tile(t, (1, LANES // HEAD_DIM)) for t in (c64, s1, s2)], axis=1)


def _rope_apply(t, tab, sign):
    reps = t.shape[1] // LANES
    c_tab, s1, s2 = (jnp.tile(tab[:, LANES * k:LANES * (k + 1)], (1, reps)) if reps > 1
                     else tab[:, LANES * k:LANES * (k + 1)] for k in range(3))
    half = ROT_DIM // 2
    up = pltpu.roll(t, t.shape[1] - half, 1)
    down = pltpu.roll(t, half, 1)
    return t * c_tab + sign * (down * s2 - up * s1)


def _lane_masks(shape):
    lane = lax.broadcasted_iota(jnp.int32, shape, 1)
    return lane < HEAD_DIM, lane >= HEAD_DIM


HEADS_PER_GROUP = N_Q_HEADS // N_KV_HEADS
ATTN_SCALE = 1.0 / math.sqrt(HEAD_DIM)


def _attn_bias_t(first_block):
    kj = lax.broadcasted_iota(jnp.int32, (2 * CHUNK, CHUNK), 0)
    qi = lax.broadcasted_iota(jnp.int32, (2 * CHUNK, CHUNK), 1)
    ok = (kj > qi) & (kj <= qi + CHUNK)
    if first_block is not None:
        ok = ok & (jnp.logical_not(first_block) | (kj >= CHUNK))
    return jnp.tile(jnp.where(ok, 0.0, -jnp.inf), (1, HEADS_PER_GROUP))


def _group_rows(x, g, lo, hi):
    rows = []
    for r in range(HEADS_PER_GROUP):
        h = HEADS_PER_GROUP * g + r
        pair = x[:, LANES * (h // 2):LANES * (h // 2 + 1)]
        rows.append(jnp.where(hi if h % 2 else lo, pair, 0.0))
    return jnp.concatenate(rows, axis=0)


def _pairs_from_rows(rows, lo):
    return [jnp.where(lo, rows[2 * CHUNK * k:2 * CHUNK * k + CHUNK], rows[2 * CHUNK * k + CHUNK:2 * CHUNK * (k + 1)])
            for k in range(HEADS_PER_GROUP // 2)]


def _group_dup(a, b, g, lo2):
    return jnp.where(lo2, a, b) if g == 0 else jnp.where(lo2, b, a)


def _sink_row(sink_ref, g):
    return jnp.concatenate([sink_ref[HEADS_PER_GROUP * g + r:HEADS_PER_GROUP * g + r + 1, :]
                            for r in range(HEADS_PER_GROUP)], axis=1)


def _attn_probs_t(k_dup, q_rows, bias_t, sink_row):
    s_t = _dot_nt(k_dup, q_rows) * ATTN_SCALE + bias_t
    m = jnp.maximum(jnp.max(s_t, axis=0, keepdims=True), sink_row)
    p = jnp.exp(s_t - m)
    e_sink = jnp.exp(sink_row - m)
    inv = 1.0 / (jnp.sum(p, axis=0, keepdims=True) + e_sink)
    return p * inv, e_sink * inv


def _sgu_forward_pair(wm, vp, j):
    lo, hi = _lane_masks(vp.shape)
    lhs = jnp.concatenate([wm[2 * j], wm[2 * j + 1]], axis=1)
    rhs = jnp.concatenate([jnp.where(lo, vp, 0.0), jnp.where(hi, vp, 0.0)], axis=0)
    return _dot(lhs, rhs)


def _masked_spatial(w_ref):
    t = lax.broadcasted_iota(jnp.int32, (CHUNK, CHUNK), 0)
    s = lax.broadcasted_iota(jnp.int32, (CHUNK, CHUNK), 1)
    tril = s <= t
    return [jnp.where(tril, w_ref[g], 0.0) for g in range(GMLP_GROUPS)], tril, s >= t


def _mod_kernel(c_all, w_shard, b_shard, comm=None):
    n = w_shard.shape[1]
    tn = 512

    def body(c_ref, w_ref, b_ref, mod_ref, act_ref):
        cv = c_ref[...]
        act = cv * (1.0 / (1.0 + jnp.exp(-cv)))
        act_ref[...] = act
        mod_ref[...] = _dot(act, w_ref[...]) + b_ref[...]

    return _hosted_call(
        body, comm, name="ada_mod", grid=(n // tn,),
        out_shape=[jax.ShapeDtypeStruct((N_DEV, n), F32), jax.ShapeDtypeStruct((N_DEV, D_MODEL), F32)],
        in_specs=[_full((N_DEV, D_MODEL)), pl.BlockSpec((D_MODEL, tn), lambda i: (0, i)),
                  pl.BlockSpec((1, tn), lambda i: (0, i))],
        out_specs=[pl.BlockSpec((N_DEV, tn), lambda i: (0, i)), _full((N_DEV, D_MODEL))],
        semantics=("arbitrary",),
    )(c_all, w_shard, b_shard)


def _load_chip_blocks(chip_ref, gathered, local, dsts, sems, first_sem=0):
    for k, dst in enumerate(dsts):
        @pl.when(chip_ref[0] == k)
        def _():
            pltpu.make_async_copy(local, dst, sems.at[first_sem + k]).start()

        @pl.when(chip_ref[0] != k)
        def _():
            pltpu.make_async_copy(gathered.at[k], dst, sems.at[first_sem + k]).start()
    return [pltpu.make_async_copy(local, dst, sems.at[first_sem + k]).wait for k, dst in enumerate(dsts)]


def _in_proj_kernel(x, vecs, w_in_t, comm=None):
    seq = x.shape[0]
    tm = 512

    def body(x_ref, v_ref, w_ref, proj_ref, h_ref):
        xv = x_ref[...]
        rstd = lax.rsqrt(_mean_last(xv * xv) + EPS)
        n1 = (xv * rstd) * v_ref[0:1, :]
        h = n1 * (1.0 + v_ref[2:3, :]) + v_ref[1:2, :]
        hb = h.astype(MXU_DTYPE)
        h_ref[...] = hb
        proj_ref[...] = _dot_nt(hb, w_ref[...])

    return _hosted_call(
        body, comm, name="in_proj", grid=(seq // tm,),
        out_shape=[jax.ShapeDtypeStruct((seq, IN_PROJ_WIDTH), F32),
                   jax.ShapeDtypeStruct((seq, D_MODEL), MXU_DTYPE)],
        in_specs=[pl.BlockSpec((tm, D_MODEL), lambda i: (i, 0)), _full((8, D_MODEL)),
                  _full((IN_PROJ_WIDTH, D_MODEL))],
        out_specs=[pl.BlockSpec((tm, IN_PROJ_WIDTH), lambda i: (i, 0)),
                   pl.BlockSpec((tm, D_MODEL), lambda i: (i, 0))],
        semantics=("arbitrary",),
    )(x, vecs, w_in_t)


MIXER_BLOCKS_PER_STEP = 4
KV_START = 2 * GMLP_WIDTH + ATTN_WIDTH


def _mixer_fwd_kernel(proj, rope_tab, w_spatial, bias_full, sink_rows, comm=None):
    seq = proj.shape[0]
    per = MIXER_BLOCKS_PER_STEP
    steps = seq // (CHUNK * per)
    kv_col = KV_START // (2 * KV_WIDTH)

    def body(proj_ref, prev_ref, tab_ref, ptab_ref, w_ref, bias_ref, sink_ref, cat_ref):
        i = pl.program_id(0)
        wm, _, _ = _masked_spatial(w_ref)
        lo, hi = _lane_masks((CHUNK, LANES))
        lo2, _ = _lane_masks((2 * CHUNK, LANES))
        o = 2 * GMLP_WIDTH
        for s in range(per):
            rows, before = slice(CHUNK * s, CHUNK * (s + 1)), slice(CHUNK * (s - 1), CHUNK * s)
            for j in range(GMLP_GROUPS // 2):
                cols = slice(LANES * j, LANES * (j + 1))
                vcols = slice(GMLP_WIDTH + LANES * j, GMLP_WIDTH + LANES * (j + 1))
                u, _ = _gelu_tanh(proj_ref[rows, cols])
                vp, _ = _gelu_tanh(proj_ref[rows, vcols])
                sv = _sgu_forward_pair(wm, vp, j) + bias_ref[:, cols]
                cat_ref[rows, cols] = (u * sv).astype(cat_ref.dtype)
            tab = tab_ref[rows, :]
            if s == 0:
                prev_kv, prev_tab, first = prev_ref[...], ptab_ref[...], i == 0
            else:
                prev_kv, prev_tab, first = proj_ref[before, KV_START:KV_START + 2 * KV_WIDTH], tab_ref[before, :], None
            q_r = _rope_apply(proj_ref[rows, o:o + ATTN_WIDTH], tab, 1.0)
            k_cur = _rope_apply(proj_ref[rows, KV_START:KV_START + KV_WIDTH], tab, 1.0)
            k_prev = _rope_apply(prev_kv[:, 0:KV_WIDTH], prev_tab, 1.0)
            k_a = jnp.concatenate([k_prev, k_cur], axis=0)
            v_a = jnp.concatenate([prev_kv[:, KV_WIDTH:2 * KV_WIDTH],
                                   proj_ref[rows, KV_START + KV_WIDTH:KV_START + 2 * KV_WIDTH]], axis=0)
            k_b = pltpu.roll(k_a, HEAD_DIM, 1)
            v_b = pltpu.roll(v_a, HEAD_DIM, 1)
            bias_t = _attn_bias_t(first)
            for g in range(N_KV_HEADS):
                p_t, _ = _attn_probs_t(_group_dup(k_a, k_b, g, lo2), _group_rows(q_r, g, lo, hi), bias_t,
                                       _sink_row(sink_ref, g))
                o_t = _dot(_group_dup(v_a, v_b, g, lo2).T, p_t)
                for k, pair in enumerate(_pairs_from_rows(o_t.T, lo)):
                    c0 = GMLP_WIDTH + LANES * (2 * g + k)
                    cat_ref[rows, c0:c0 + LANES] = pair.astype(cat_ref.dtype)

    return _hosted_call(
        body, comm, name="mixer_fwd", grid=(steps,),
        out_shape=[jax.ShapeDtypeStruct((seq, D_MODEL), MXU_DTYPE)],
        in_specs=[pl.BlockSpec((CHUNK * per, IN_PROJ_WIDTH), lambda i: (i, 0)),
                  pl.BlockSpec((CHUNK, 2 * KV_WIDTH), lambda i: (jnp.maximum(per * i - 1, 0), kv_col)),
                  pl.BlockSpec((CHUNK * per, 3 * LANES), lambda i: (i, 0)),
                  pl.BlockSpec((CHUNK, 3 * LANES), lambda i: (jnp.maximum(per * i - 1, 0), 0)),
                  _full((GMLP_GROUPS, CHUNK, CHUNK)), _full((CHUNK, GMLP_WIDTH)),
                  _full((N_Q_HEADS, LANES))],
        out_specs=[pl.BlockSpec((CHUNK * per, D_MODEL), lambda i: (i, 0))],
        semantics=("arbitrary",),
    )(proj, proj, rope_tab, rope_tab, w_spatial, bias_full, sink_rows)


def _trunk_kernel(x, target, cat, vecs, chip_idx, gathered, local):
    seq = x.shape[0]
    tm = 256
    nj = D_FF // D_MODEL
    out_rows = D_MODEL // N_CHIPS

    def body(chip_ref, x_ref, t_ref, cat_ref, v_ref, g_out, g_w1, g_w2, l_out, l_w1, l_w2,
             dx1_ref, dcat_ref, dmix_ref, h2_ref, r_ref, da_ref, dff_ref, sums_ref,
             wout, w1, w2, a_scr, sem):
        i = pl.program_id(0)

        @pl.when(i == 0)
        def _():
            waits = _load_chip_blocks(chip_ref, g_out, l_out,
                                      [wout.at[pl.ds(out_rows * k, out_rows)] for k in range(N_CHIPS)], sem)
            waits += _load_chip_blocks(chip_ref, g_w1, l_w1, [w1.at[k] for k in range(N_CHIPS)], sem, N_CHIPS)
            waits += _load_chip_blocks(chip_ref, g_w2, l_w2, [w2.at[k] for k in range(N_CHIPS)], sem, 2 * N_CHIPS)
            for wait in waits:
                wait()
            sums_ref[...] = jnp.zeros_like(sums_ref)

        gate1, shift2, scale2 = v_ref[0:1, :], v_ref[1:2, :], v_ref[2:3, :]
        gate2, g_ffn, g_final = v_ref[3:4, :], v_ref[4:5, :], v_ref[5:6, :]

        mix = _dot(cat_ref[...], wout[...])
        x1 = x_ref[...] + gate1 * mix
        rstd2 = lax.rsqrt(_mean_last(x1 * x1) + EPS)
        xh2 = x1 * rstd2
        n2 = xh2 * g_ffn
        h2b = (n2 * (1.0 + scale2) + shift2).astype(MXU_DTYPE)
        h2_ref[...] = h2b
        ff = jnp.zeros((tm, D_MODEL), F32)
        for j in range(nj):
            a = _dot(h2b, w1[j])
            a_scr[j] = a
            relu = jnp.maximum(a, 0.0)
            rb = (relu * relu).astype(MXU_DTYPE)
            r_ref[:, D_MODEL * j:D_MODEL * (j + 1)] = rb
            ff = ff + _dot(rb, w2[j])
        x2 = x1 + gate2 * ff
        rstd3 = lax.rsqrt(_mean_last(x2 * x2) + EPS)
        xh3 = x2 * rstd3
        err = xh3 * g_final - t_ref[...]
        loss = 0.5 * _rowsum(_mean_last(err * err))
        dy = err * (1.0 / D_MODEL)
        dxh3 = dy * g_final
        dx2 = rstd3 * (dxh3 - xh3 * _mean_last(dxh3 * xh3))
        dffb = (dx2 * gate2).astype(MXU_DTYPE)
        dff_ref[...] = dffb
        dh2 = jnp.zeros((tm, D_MODEL), F32)
        for j in range(nj):
            dr = _dot_nt(dffb, w2[j])
            dab = (dr * (2.0 * jnp.maximum(a_scr[j], 0.0))).astype(MXU_DTYPE)
            da_ref[:, D_MODEL * j:D_MODEL * (j + 1)] = dab
            dh2 = dh2 + _dot_nt(dab, w1[j])
        dn2 = dh2 * (1.0 + scale2)
        dxh2 = dn2 * g_ffn
        dx1 = dx2 + rstd2 * (dxh2 - xh2 * _mean_last(dxh2 * xh2))
        dx1_ref[...] = dx1
        dmixb = (dx1 * gate1).astype(MXU_DTYPE)
        dmix_ref[...] = dmixb
        dcat_ref[...] = _dot_nt(dmixb, wout[...])

        sums_ref[0:1, :] += _rowsum(dh2)
        sums_ref[1:2, :] += _rowsum(dh2 * n2)
        sums_ref[2:3, :] += _rowsum(dx2 * ff)
        sums_ref[3:4, :] += _rowsum(dn2 * xh2)
        sums_ref[4:5, :] += _rowsum(dy * xh3)
        sums_ref[5:6, :] += _rowsum(dx1 * mix)
        sums_ref[6:7, :] += jnp.broadcast_to(loss, (1, D_MODEL))

    tok = lambda w: pl.BlockSpec((tm, w), lambda i, chip: (i, 0))
    return _hosted_call(
        body, None, name="trunk", grid=(seq // tm,), n_prefetch=1,
        out_shape=[jax.ShapeDtypeStruct((seq, D_MODEL), F32), jax.ShapeDtypeStruct((seq, D_MODEL), F32),
                   jax.ShapeDtypeStruct((seq, D_MODEL), MXU_DTYPE), jax.ShapeDtypeStruct((seq, D_MODEL), MXU_DTYPE),
                   jax.ShapeDtypeStruct((seq, D_FF), MXU_DTYPE), jax.ShapeDtypeStruct((seq, D_FF), MXU_DTYPE),
                   jax.ShapeDtypeStruct((seq, D_MODEL), MXU_DTYPE), jax.ShapeDtypeStruct((8, D_MODEL), F32)],
        in_specs=[tok(D_MODEL), tok(D_MODEL), tok(D_MODEL), _full((8, D_MODEL))] + [_any()] * 6,
        out_specs=[tok(D_MODEL), tok(D_MODEL), tok(D_MODEL), tok(D_MODEL), tok(D_FF), tok(D_FF), tok(D_MODEL),
                   _full((8, D_MODEL))],
        scratch_shapes=[pltpu.VMEM((D_MODEL, D_MODEL), MXU_DTYPE), pltpu.VMEM((nj, D_MODEL, D_MODEL), MXU_DTYPE),
                        pltpu.VMEM((nj, D_MODEL, D_MODEL), MXU_DTYPE), pltpu.VMEM((nj, tm, D_MODEL), F32),
                        pltpu.SemaphoreType.DMA((3 * N_CHIPS,))],
        semantics=("arbitrary",),
    )(chip_idx, x, target, cat, vecs, *gathered, *local)


def _mixer_bwd_kernel(proj, rope_tab, dcat, w_spatial, w_spatial_t, bias_full, sink_rows, dev_idx, comm=None):
    seq = proj.shape[0]
    per = MIXER_BLOCKS_PER_STEP
    steps = seq // (CHUNK * per)
    kv_col = KV_START // (2 * KV_WIDTH)

    def body(dev_ref, proj_ref, prev_ref, tab_ref, ptab_ref, dcat_ref, w_ref, wt_ref, bias_ref, sink_ref,
             dproj_ref, dw_out, db_ref, dsink_ref, carry, dw_ref):
        del dev_ref
        step = pl.program_id(0)

        @pl.when(step == 0)
        def _():
            carry[...] = jnp.zeros_like(carry)
            dw_ref[...] = jnp.zeros_like(dw_ref)
            db_ref[...] = jnp.zeros_like(db_ref)
            dsink_ref[...] = jnp.zeros_like(dsink_ref)

        for s in reversed(range(per)):
            rows = pl.ds(CHUNK * s, CHUNK)
            if s == 0:
                before, before_tab, first = prev_ref, ptab_ref, step == steps - 1
            else:
                before = proj_ref.at[pl.ds(CHUNK * (s - 1), CHUNK), pl.ds(KV_START, 2 * KV_WIDTH)]
                before_tab, first = tab_ref.at[pl.ds(CHUNK * (s - 1), CHUNK)], None
            one_block(proj_ref.at[rows], before, tab_ref.at[rows], before_tab, dcat_ref.at[rows], w_ref, wt_ref,
                      bias_ref, sink_ref, dproj_ref.at[rows], dw_ref, db_ref, dsink_ref, carry, first)

        @pl.when(step == steps - 1)
        def _():
            dw_out[...] = dw_ref[...].astype(dw_out.dtype)

    def one_block(proj_ref, prev_ref, tab_ref, ptab_ref, dcat_ref, w_ref, wt_ref, bias_ref, sink_ref,
                  dproj_ref, dw_ref, db_ref, dsink_ref, carry, first):
        wm, tril, triu = _masked_spatial(w_ref)
        lo, hi = _lane_masks((CHUNK, LANES))
        lane = lax.broadcasted_iota(jnp.int32, (CHUNK, LANES), 1)
        db = jnp.zeros((CHUNK, LANES), F32)
        for j in range(GMLP_GROUPS // 2):
            cols = slice(LANES * j, LANES * (j + 1))
            vcols = slice(GMLP_WIDTH + LANES * j, GMLP_WIDTH + LANES * (j + 1))
            zu, zv = proj_ref[:, cols], proj_ref[:, vcols]
            u, tu = _gelu_tanh(zu)
            vp, tv = _gelu_tanh(zv)
            sv = _sgu_forward_pair(wm, vp, j) + bias_ref[:, cols]
            dout = dcat_ref[:, cols]
            du = dout * sv
            dsv = dout * u
            dsv_lo, dsv_hi = jnp.where(lo, dsv, 0.0), jnp.where(hi, dsv, 0.0)
            lhs_t = jnp.concatenate([jnp.where(triu, wt_ref[2 * j], 0.0),
                                     jnp.where(triu, wt_ref[2 * j + 1], 0.0)], axis=1)
            dv = _dot(lhs_t, jnp.concatenate([dsv_lo, dsv_hi], axis=0))
            dw_ref[2 * j] += jnp.where(tril, _dot_nt(dsv_lo, vp), 0.0)
            dw_ref[2 * j + 1] += jnp.where(tril, _dot_nt(dsv_hi, vp), 0.0)
            db = db + (jnp.where(lane == 2 * j, jnp.sum(dsv_lo, axis=1, keepdims=True), 0.0)
                       + jnp.where(lane == 2 * j + 1, jnp.sum(dsv_hi, axis=1, keepdims=True), 0.0))
            dproj_ref[:, cols] = (du * _gelu_tanh_grad(zu, tu)).astype(dproj_ref.dtype)
            dproj_ref[:, vcols] = (dv * _gelu_tanh_grad(zv, tv)).astype(dproj_ref.dtype)
        db_ref[...] += db
        o = 2 * GMLP_WIDTH
        tab = tab_ref[...]
        q_r = _rope_apply(proj_ref[:, o:o + ATTN_WIDTH], tab, 1.0)
        k_cur = _rope_apply(proj_ref[:, o + ATTN_WIDTH:o + ATTN_WIDTH + KV_WIDTH], tab, 1.0)
        k_prev = _rope_apply(prev_ref[:, 0:KV_WIDTH], ptab_ref[...], 1.0)
        k_a = jnp.concatenate([k_prev, k_cur], axis=0)
        v_a = jnp.concatenate([prev_ref[:, KV_WIDTH:2 * KV_WIDTH],
                               proj_ref[:, o + ATTN_WIDTH + KV_WIDTH:o + ATTN_WIDTH + 2 * KV_WIDTH]], axis=0)
        k_b = pltpu.roll(k_a, HEAD_DIM, 1)
        v_b = pltpu.roll(v_a, HEAD_DIM, 1)
        bias_t = _attn_bias_t(first)
        lo2, _ = _lane_masks((2 * CHUNK, LANES))
        dout_b = dcat_ref[:, GMLP_WIDTH:GMLP_WIDTH + ATTN_WIDTH]
        dk_tot, dv_tot, dq_pairs = [], [], []
        for g in range(N_KV_HEADS):
            k_dup, v_dup = _group_dup(k_a, k_b, g, lo2), _group_dup(v_a, v_b, g, lo2)
            q_rows = _group_rows(q_r, g, lo, hi)
            do_rows = _group_rows(dout_b, g, lo, hi)
            p_t, p_sink = _attn_probs_t(k_dup, q_rows, bias_t, _sink_row(sink_ref, g))
            dp_t = _dot_nt(v_dup, do_rows)
            delta = jnp.sum(p_t * dp_t, axis=0, keepdims=True)
            ds_t = p_t * (dp_t - delta) * ATTN_SCALE
            dsink = -p_sink * delta
            for r in range(HEADS_PER_GROUP):
                h = HEADS_PER_GROUP * g + r
                dsink_ref[h:h + 1, :] += jnp.broadcast_to(
                    jnp.sum(dsink[:, LANES * r:LANES * (r + 1)], axis=1, keepdims=True), (1, LANES))
            dk_full = _dot(ds_t, q_rows)
            dv_full = _dot(p_t, do_rows)
            dk_tot.append(dk_full + pltpu.roll(dk_full, HEAD_DIM, 1))
            dv_tot.append(dv_full + pltpu.roll(dv_full, HEAD_DIM, 1))
            dq_t = _dot(k_dup.T, ds_t)
            dq_pairs += _pairs_from_rows(dq_t.T, lo)
        dk_all = jnp.where(lo2, dk_tot[0], dk_tot[1])
        dv_all = jnp.where(lo2, dv_tot[0], dv_tot[1])
        dk_cur = dk_all[CHUNK:, :] + carry[:, 0:KV_WIDTH]
        dv_cur = dv_all[CHUNK:, :] + carry[:, KV_WIDTH:2 * KV_WIDTH]
        carry[:, 0:KV_WIDTH] = dk_all[:CHUNK, :]
        carry[:, KV_WIDTH:2 * KV_WIDTH] = dv_all[:CHUNK, :]
        dq = _rope_apply(jnp.concatenate(dq_pairs, axis=1), tab, -1.0)
        dproj_ref[:, o:o + ATTN_WIDTH] = dq.astype(dproj_ref.dtype)
        dproj_ref[:, o + ATTN_WIDTH:o + ATTN_WIDTH + KV_WIDTH] = (
            _rope_apply(dk_cur, tab, -1.0).astype(dproj_ref.dtype))
        dproj_ref[:, o + ATTN_WIDTH + KV_WIDTH:o + ATTN_WIDTH + 2 * KV_WIDTH] = dv_cur.astype(dproj_ref.dtype)

    rev = lambda i: steps - 1 - i
    before = lambda i: jnp.maximum(per * rev(i) - 1, 0)
    slot = lambda shape: pl.BlockSpec((None,) + shape, lambda i, d: (d[0],) + (0,) * len(shape))
    return _hosted_call(
        body, comm, name="mixer_bwd", grid=(steps,), n_prefetch=1,
        out_shape=[jax.ShapeDtypeStruct((seq, IN_PROJ_WIDTH), MXU_DTYPE),
                   jax.ShapeDtypeStruct((N_DEV, GMLP_GROUPS, CHUNK, CHUNK), GRAD_COMM_DTYPE),
                   jax.ShapeDtypeStruct((N_DEV, CHUNK, LANES), F32),
                   jax.ShapeDtypeStruct((N_DEV, N_Q_HEADS, LANES), F32)],
        in_specs=[pl.BlockSpec((CHUNK * per, IN_PROJ_WIDTH), lambda i, d: (rev(i), 0)),
                  pl.BlockSpec((CHUNK, 2 * KV_WIDTH), lambda i, d: (before(i), kv_col)),
                  pl.BlockSpec((CHUNK * per, 3 * LANES), lambda i, d: (rev(i), 0)),
                  pl.BlockSpec((CHUNK, 3 * LANES), lambda i, d: (before(i), 0)),
                  pl.BlockSpec((CHUNK * per, D_MODEL), lambda i, d: (rev(i), 0)),
                  _full((GMLP_GROUPS, CHUNK, CHUNK)), _full((GMLP_GROUPS, CHUNK, CHUNK)),
                  _full((CHUNK, GMLP_WIDTH)), _full((N_Q_HEADS, LANES))],
        out_specs=[pl.BlockSpec((CHUNK * per, IN_PROJ_WIDTH), lambda i, d: (rev(i), 0)),
                   slot((GMLP_GROUPS, CHUNK, CHUNK)), slot((CHUNK, LANES)), slot((N_Q_HEADS, LANES))],
        scratch_shapes=[pltpu.VMEM((CHUNK, 2 * KV_WIDTH), F32), pltpu.VMEM((GMLP_GROUPS, CHUNK, CHUNK), F32)],
        semantics=("arbitrary",),
    )(dev_idx, proj, proj, rope_tab, rope_tab, dcat, w_spatial, w_spatial_t, bias_full, sink_rows)


def _in_proj_bwd_kernel(x, dx1, dproj, vecs, w_in_t, comm=None):
    seq = x.shape[0]
    tm = 512

    def body(x_ref, dx1_ref, dp_ref, v_ref, w_ref, gx_ref, sums_ref):
        @pl.when(pl.program_id(0) == 0)
        def _():
            sums_ref[...] = jnp.zeros_like(sums_ref)

        g_mix, scale1 = v_ref[0:1, :], v_ref[2:3, :]
        dh = _dot(dp_ref[...], w_ref[...])
        xv = x_ref[...]
        rstd = lax.rsqrt(_mean_last(xv * xv) + EPS)
        xh = xv * rstd
        dn1 = dh * (1.0 + scale1)
        dxh = dn1 * g_mix
        gx_ref[...] = dx1_ref[...] + rstd * (dxh - xh * _mean_last(dxh * xh))
        sums_ref[0:1, :] += _rowsum(dh)
        sums_ref[1:2, :] += _rowsum(dh * (xh * g_mix))
        sums_ref[2:3, :] += _rowsum(dn1 * xh)

    tok = lambda w: pl.BlockSpec((tm, w), lambda i: (i, 0))
    return _hosted_call(
        body, comm, name="in_proj_bwd", grid=(seq // tm,),
        out_shape=[jax.ShapeDtypeStruct((seq, D_MODEL), F32), jax.ShapeDtypeStruct((8, D_MODEL), F32)],
        in_specs=[tok(D_MODEL), tok(D_MODEL), tok(IN_PROJ_WIDTH), _full((8, D_MODEL)),
                  _full((IN_PROJ_WIDTH, D_MODEL))],
        out_specs=[tok(D_MODEL), _full((8, D_MODEL))],
        semantics=("arbitrary",),
    )(x, dx1, dproj, vecs, w_in_t)


class _GradTiles(NamedTuple):
    tm: int
    tn: int
    n_tiles: int
    chips_per_tile: int
    a_index: Callable
    b_index: Callable


def _weight_grad_kernel(a, b, c_idx, name, tiles, comm=None):
    seq = a.shape[0]
    tk = min(seq, 4096)
    nk = seq // tk
    tm, tn, n_tiles, per = tiles.tm, tiles.tn, tiles.n_tiles, tiles.chips_per_tile
    rows = tm // per

    def half(phase, c):
        return phase * c[0] + (1 - phase) * (1 - c[0])

    def body(c_ref, a_ref, b_ref, o_ref, acc, stage, landed, send_sems, recv_sems):
        del c_ref
        phase, t, kk = pl.program_id(0), pl.program_id(1), pl.program_id(2)
        x, y, c, _ = _mesh_place()

        def copy(tile):
            return pltpu.make_async_remote_copy(
                src_ref=stage.at[tile], dst_ref=landed.at[tile], send_sem=send_sems.at[tile],
                recv_sem=recv_sems.at[tile], device_id=(x, y, 1 - c), device_id_type=MESH)

        @pl.when(kk == 0)
        def _():
            acc[...] = jnp.zeros_like(acc)

        acc[...] += _dot_tn(a_ref[...], b_ref[...])

        @pl.when((kk == nk - 1) & (phase == 0))
        def _():
            stage[t] = acc[...].astype(stage.dtype)
            copy(t).start()

        @pl.when((kk == nk - 1) & (phase == 1))
        def _():
            copy(t).wait_recv()
            total = acc[...] + landed[t].astype(F32)
            for q in range(per):
                o_ref[q] = total[rows * q:rows * (q + 1)].astype(o_ref.dtype)

        @pl.when((kk == nk - 1) & (phase == 1) & (t == n_tiles - 1))
        def _():
            for tile in range(n_tiles):
                copy(tile).wait_send()

    out = _hosted_call(
        body, comm, name=name, grid=(2, n_tiles, nk), n_prefetch=1,
        out_shape=[jax.ShapeDtypeStruct((n_tiles * per, rows, tn), GRAD_COMM_DTYPE)],
        in_specs=[pl.BlockSpec((tk, tm), lambda p, t, k, c: (k, tiles.a_index(t, half(p, c)))),
                  pl.BlockSpec((tk, tn), lambda p, t, k, c: (k, tiles.b_index(t, half(p, c))))],
        out_specs=[pl.BlockSpec((per, rows, tn), lambda p, t, k, c: (p * t, 0, 0))],
        scratch_shapes=[pltpu.VMEM((tm, tn), F32), pltpu.VMEM((n_tiles, tm, tn), GRAD_COMM_DTYPE),
                        pltpu.VMEM((n_tiles, tm, tn), GRAD_COMM_DTYPE),
                        pltpu.SemaphoreType.DMA((n_tiles,)), pltpu.SemaphoreType.DMA((n_tiles,))],
        semantics=("arbitrary", "arbitrary", "arbitrary"),
    )(c_idx, a, b)
    return out[0] if comm is None else out


def _row_tile(rows, most=256, sublanes=16):
    return max(t for t in range(sublanes, most + 1, sublanes) if rows % t == 0)


def _adam_update(w, g, m, v):
    m_new = ADAM_B1 * m + (1.0 - ADAM_B1) * g
    v_new = ADAM_B2 * v + (1.0 - ADAM_B2) * (g * g)
    m_hat = m_new / (1.0 - ADAM_B1 ** ADAM_STEP)
    v_hat = v_new / (1.0 - ADAM_B2 ** ADAM_STEP)
    delta = -ADAM_LR * (m_hat / (jnp.sqrt(v_hat) + ADAM_EPS) + ADAM_WD * w)
    return delta, m_new, v_new


def _sum_chips_kernel(own, others, place, name):
    _, r, n = own.shape
    tr = _row_tile(r)

    def body(place_ref, own_ref, oth_ref, o_ref):
        del place_ref
        acc = own_ref[...].astype(F32)
        for k in range(N_CHIPS - 1):
            acc = acc + oth_ref[k].astype(F32)
        o_ref[...] = acc

    return pl.pallas_call(
        body, name=name, out_shape=jax.ShapeDtypeStruct((2, r, n), F32),
        grid_spec=pltpu.PrefetchScalarGridSpec(
            num_scalar_prefetch=1, grid=(r // tr,),
            in_specs=[pl.BlockSpec((None, tr, n), lambda i, p: (p[0], i, 0)),
                      pl.BlockSpec((N_CHIPS - 1, tr, n), lambda i, p: (0, i, 0))],
            out_specs=pl.BlockSpec((None, tr, n), lambda i, p: (p[1], i, 0))),
        compiler_params=_params("parallel"),
    )(place, own, others)


def _adam_kernel(w, g, m, v, name):
    r, n = w.shape
    by_columns = g.shape[1] == r
    tr, tn = _row_tile(g.shape[1], most=512), g.shape[2]

    def body(w_ref, g_ref, m_ref, v_ref, g_out, d_ref, mo_ref, vo_ref):
        gv = g_ref[...]
        g_out[...] = gv
        d_ref[...], mo_ref[...], vo_ref[...] = _adam_update(w_ref[...], gv, m_ref[...], v_ref[...])

    steps = g.shape[1] // tr
    spec = pl.BlockSpec((tr, tn), (lambda h, i: (i, h)) if by_columns else (lambda h, i: (h * steps + i, 0)))
    return pl.pallas_call(
        body, name=name, grid=(2, steps), out_shape=[jax.ShapeDtypeStruct((r, n), F32)] * 4,
        in_specs=[spec, pl.BlockSpec((None, tr, tn), lambda h, i: (h, i, 0)), spec, spec], out_specs=[spec] * 4,
        compiler_params=_params("parallel", "parallel"),
    )(w, g, m, v)


SMALL_PARAMS = ("b_ada", "g_mix", "g_ffn", "g_final", "b_spatial", "sinks", "w_spatial")


def _small_update_kernel(gathered, params):
    shapes = [params[nm][0].shape for nm in SMALL_PARAMS]

    def body(*refs):
        g_refs, refs = refs[:5], refs[5:]
        p_refs, refs = refs[:3 * len(SMALL_PARAMS)], refs[3 * len(SMALL_PARAMS):]
        loss_ref, o_refs = refs[0], refs[1:]

        def total(ref):
            acc = ref[0].astype(F32)
            for k in range(1, N_DEV):
                acc = acc + ref[k].astype(F32)
            return acc

        s1, s2, db, ds, dw = (total(r) for r in g_refs)
        loss_ref[...] = jnp.broadcast_to(s2[6:7, 0:1], loss_ref.shape)
        grads = {"b_ada": [s1[0:1], s1[1:2], s2[5:6], s2[0:1], s2[1:2], s2[2:3]], "g_mix": [s1[2:3]],
                 "g_ffn": [s2[3:4]], "g_final": [s2[4:5]], "b_spatial": [db.T[0:GMLP_GROUPS]],
                 "w_spatial": [dw]}
        lane = lax.broadcasted_iota(jnp.int32, (1, LANES), 1)
        sink_row = jnp.zeros((1, LANES), F32)
        for h in range(N_Q_HEADS):
            sink_row = sink_row + jnp.where(lane == h, ds[h:h + 1, :], 0.0)
        grads["sinks"] = [sink_row[:, 0:N_Q_HEADS]]
        for i, nm in enumerate(SMALL_PARAMS):
            w_ref, m_ref, v_ref = p_refs[3 * i:3 * i + 3]
            outs = o_refs[4 * i:4 * i + 4]
            width = grads[nm][0].shape[1]
            for k, g in enumerate(grads[nm]):
                cols = slice(width * k, width * (k + 1))
                upd = _adam_update(w_ref[:, cols], g, m_ref[:, cols], v_ref[:, cols])
                for o_ref, val in zip(outs, (g,) + upd):
                    o_ref[:, cols] = val

    flat = [a for nm in SMALL_PARAMS for a in params[nm]]
    out_shape = [jax.ShapeDtypeStruct((8, LANES), F32)]
    out_shape += [jax.ShapeDtypeStruct(s, F32) for s in shapes for _ in range(4)]
    outs = pl.pallas_call(
        body, name="small_update", grid=(1,), out_shape=out_shape,
        in_specs=[_full(g.shape) for g in gathered] + [_full(a.shape) for a in flat],
        out_specs=[_full(s.shape) for s in out_shape],
        compiler_params=_params("arbitrary"),
    )(*gathered, *flat)
    return {nm: outs[1 + 4 * i:5 + 4 * i] for i, nm in enumerate(SMALL_PARAMS)}, outs[0]


def _ada_update_kernel(act_t, dmod, w, m, v):
    r, n = w.shape
    tr = 256

    def body(a_ref, d_ref, w_ref, m_ref, v_ref, g_ref, dl_ref, mo_ref, vo_ref):
        g = _dot(a_ref[...], d_ref[...])
        g_ref[...] = g
        dl_ref[...], mo_ref[...], vo_ref[...] = _adam_update(w_ref[...], g, m_ref[...], v_ref[...])

    spec = pl.BlockSpec((tr, n), lambda i: (i, 0))
    return pl.pallas_call(
        body, name="ada_update", grid=(r // tr,), out_shape=[jax.ShapeDtypeStruct((r, n), F32)] * 4,
        in_specs=[pl.BlockSpec((tr, N_DEV), lambda i: (i, 0)), _full((N_DEV, n)), spec, spec, spec],
        out_specs=[spec] * 4, compiler_params=_params("parallel"),
    )(act_t, dmod, w, m, v)


def kernel(x, c, positions, w_ada, b_ada, g_mix, w_in, w_spatial, b_spatial, sinks, w_out, g_ffn, w_ff1, w_ff2, g_final, loss_target, m_w_ada, m_b_ada, m_g_mix, m_w_in, m_w_spatial, m_b_spatial, m_sinks, m_w_out, m_g_ffn, m_w_ff1, m_w_ff2, m_g_final, v_w_ada, v_b_ada, v_g_mix, v_w_in, v_w_spatial, v_b_spatial, v_sinks, v_w_out, v_g_ffn, v_w_ff1, v_w_ff2, v_g_final):
    xi, yi, ci = lax.axis_index("x"), lax.axis_index("y"), lax.axis_index("c")
    chip = 2 * xi + yi
    dev = 2 * chip + ci
    seq = x.shape[1]
    x2, tgt = x[0], loss_target[0]
    ada_cols = w_ada.shape[2]

    big = {"w_in": tuple(a[0].T for a in (w_in, m_w_in, v_w_in)),
           "w_out": (w_out[0], m_w_out[0], v_w_out[0]), "w_ff1": (w_ff1[0], m_w_ff1[0], v_w_ff1[0]),
           "w_ff2": (w_ff2[0], m_w_ff2[0], v_w_ff2[0])}

    def halves(nm):
        r, n = big[nm][0].shape
        return big[nm][0].astype(WEIGHT_COMM_DTYPE).reshape(2, r // 2, n)

    chip_idx = chip.reshape(1).astype(jnp.int32)
    c_all, w_in_t, g_out = _all_gather8([c, halves("w_in"), halves("w_out")], "gather_first",
                                        split=[False, True, True], skip_own=(2,))
    c_all, w_in_t = c_all.reshape(N_DEV, D_MODEL), w_in_t.reshape(IN_PROJ_WIDTH, D_MODEL)
    b_shard = lax.dynamic_slice(b_ada, (0, chip * ada_cols), (1, ada_cols))
    mod_part, act = _mod_kernel(c_all, w_ada[0], b_shard)
    mod_all, = _all_gather8([mod_part], "gather_mod")
    mod_me = lax.dynamic_index_in_dim(mod_all[0::2], dev, axis=1, keepdims=False)
    mod_me = mod_me.reshape(N_MOD, D_MODEL)
    shift1, scale1, gate1, shift2, scale2, gate2 = (mod_me[k:k + 1] for k in range(N_MOD))

    zeros_row = jnp.zeros((1, D_MODEL), F32)
    vecs1 = jnp.concatenate([g_mix, shift1, scale1] + [zeros_row] * 5, axis=0)
    vecs2 = jnp.concatenate([gate1, shift2, scale2, gate2, g_ffn, g_final.reshape(1, D_MODEL)]
                            + [zeros_row] * 2, axis=0)
    bias_full = jnp.repeat(b_spatial[0].T, HEAD_DIM, axis=1)
    sink_rows = jnp.broadcast_to(sinks[0][:, None], (N_Q_HEADS, LANES))
    inv_freq = ROPE_THETA ** (-jnp.arange(0, ROT_DIM, 2, dtype=F32) / ROT_DIM)
    rope_tab = _rope_lane_tables(*_rope_angle_kernel(positions, inv_freq.reshape(ROT_DIM // 2, 1)))

    trunk_weights = ["w_out", "w_ff1", "w_ff2"]
    shards = [halves(nm) for nm in trunk_weights]
    proj, hb, *staged = _in_proj_kernel(x2, vecs1, w_in_t, comm=_gather2d_first(shards[1:]))
    cat, *staged = _mixer_fwd_kernel(proj, rope_tab, w_spatial[0], bias_full, sink_rows,
                                     comm=_gather2d_second(staged, shards[1:]))
    staged = [g_out] + list(_gather_forward(staged, "gather_forward"))
    dx1, dcat, dmix, h2b, rb, dab, dffb, sums2 = _trunk_kernel(
        x2, tgt, cat, vecs2, chip_idx,
        [g.reshape((N_CHIPS,) + big[nm][0].shape) for nm, g in zip(trunk_weights, staged)],
        [s.reshape(big[nm][0].shape) for nm, s in zip(trunk_weights, shards)])

    c_idx = ci.reshape(1).astype(jnp.int32)
    place = jnp.stack([chip, ci]).astype(jnp.int32)
    half_d = D_MODEL // 2
    cs_ff2 = _weight_grad_kernel(rb, dffb, c_idx, "dw_ff2",
                                 _GradTiles(D_MODEL, half_d, N_CHIPS, 1, lambda t, h: t, lambda t, h: h))
    eighth = D_MODEL // 8
    cs_ff1, sc_ff2 = _weight_grad_kernel(
        h2b, dab, c_idx, "dw_ff1",
        _GradTiles(D_MODEL, half_d, N_CHIPS, 1, lambda t, h: 0, lambda t, h: 2 * t + h),
        comm=_scatter_job([cs_ff2], rows=(0, 6 * eighth)))
    cs_out, sc_ff2 = _weight_grad_kernel(
        cat, dmix, c_idx, "dw_out", _GradTiles(D_MODEL, half_d, 1, N_CHIPS, lambda t, h: 0, lambda t, h: h),
        comm=_scatter_job([cs_ff2], rows=(6 * eighth, eighth), into=[sc_ff2]))
    dproj, dw_spatial, db_lanes, dsink_rows, sc_ff2, sc_ff1, sc_out = _mixer_bwd_kernel(
        proj, rope_tab, dcat, w_spatial[0], w_spatial[0].transpose(0, 2, 1), bias_full, sink_rows,
        dev.reshape(1).astype(jnp.int32),
        comm=_merge_jobs(_scatter_job([cs_ff1, cs_out]),
                         _scatter_job([cs_ff2], rows=(7 * eighth, eighth), into=[sc_ff2])))
    totals = [_sum_chips_kernel(own, oth, place, "grad_sum_" + nm)
              for nm, own, oth in (("w_out", cs_out, sc_out), ("w_ff1", cs_ff1, sc_ff1), ("w_ff2", cs_ff2, sc_ff2))]
    small_slots = [db_lanes, dsink_rows, dw_spatial.reshape(N_DEV, GMLP_GROUPS * CHUNK, CHUNK)]
    cs_in, *rode = _weight_grad_kernel(
        dproj, hb, c_idx, "dw_in",
        _GradTiles(2 * W_IN_BLOCK, half_d, N_CHIPS // 2, 2, lambda t, h: t, lambda t, h: h),
        comm=_merge_jobs(_gather_job(small_slots), _share_job(totals)))
    small_stage1, shared = rode[:len(small_slots)], rode[len(small_slots):]
    grad_x, sums1 = _in_proj_bwd_kernel(x2, dx1, dproj, vecs1, w_in_t)
    *gathered, sc_in = _all_gather8([sums1, sums2], "gather_small", forward=small_stage1,
                                    riders=[_scatter_job([cs_in])])
    total_in = _sum_chips_kernel(cs_in, sc_in, place, "grad_sum_w_in")
    shared = list(_sibling_share([total_in], "grad_share_w_in")) + list(shared)
    names = ["w_in", "w_out", "w_ff1", "w_ff2"]
    big_out = {}
    for nm, g in zip(names, shared):
        w, m, v = big[nm]
        outs = _adam_kernel(w, g, m, v, "adam_" + nm)
        big_out[nm] = tuple((t.T if nm == "w_in" else t)[None] for t in outs)

    small = {"b_ada": (b_ada, m_b_ada, v_b_ada), "g_mix": (g_mix, m_g_mix, v_g_mix),
             "g_ffn": (g_ffn, m_g_ffn, v_g_ffn), "g_final": (g_final, m_g_final, v_g_final),
             "b_spatial": (b_spatial, m_b_spatial, v_b_spatial), "sinks": (sinks, m_sinks, v_sinks),
             "w_spatial": (w_spatial, m_w_spatial, v_w_spatial)}
    flat_shape = {"g_final": (1, D_MODEL), "b_spatial": (GMLP_GROUPS, CHUNK), "w_spatial": (GMLP_GROUPS * CHUNK, CHUNK)}
    small_out, loss_tile = _small_update_kernel(
        gathered, {nm: tuple(a.reshape(flat_shape.get(nm, a.shape)) for a in small[nm]) for nm in small})
    small_out = {nm: [o.reshape(small[nm][0].shape) for o in small_out[nm]] for nm in small}
    loss = loss_tile[0, 0]

    g1, g2 = gathered[0], gathered[1]
    dmod_all = jnp.concatenate([g1[:, 0], g1[:, 1], g2[:, 5], g2[:, 0], g2[:, 1], g2[:, 2]], axis=1)
    dmod_cols = lax.dynamic_slice(dmod_all, (0, chip * ada_cols), (N_DEV, ada_cols))
    ada = _ada_update_kernel(act.T, dmod_cols, w_ada[0], m_w_ada[0], v_w_ada[0])
    big_out["w_ada"] = tuple(t[None] for t in ada)

    order = ["w_ada", "b_ada", "g_mix", "w_in", "w_spatial", "b_spatial", "sinks", "w_out", "g_ffn",
             "w_ff1", "w_ff2", "g_final"]

    def leaf(nm, k):
        return big_out[nm][k] if nm in big_out else small_out[nm][k]

    outs = [loss, grad_x[None]]
    for k in range(4):
        outs += [leaf(nm, k) for nm in order]
    return tuple(outs)
```

```python
import math
from typing import Callable, NamedTuple

import jax
import jax.numpy as jnp
from jax import lax
from jax.experimental import pallas as pl
from jax.experimental.pallas import tpu as pltpu

F32 = jnp.float32
MXU_DTYPE = jnp.bfloat16
WEIGHT_COMM_DTYPE = jnp.bfloat16
GRAD_COMM_DTYPE = jnp.bfloat16

D_MODEL = 1024
D_FF = 4096
HEAD_DIM = 64
GMLP_GROUPS = 8
GMLP_WIDTH = 512
CHUNK = 128
N_Q_HEADS = 8
N_KV_HEADS = 2
ATTN_WIDTH = 512
KV_WIDTH = 128
ROT_DIM = 16
ROPE_THETA = 500000.0
IN_PROJ_WIDTH = 1792
N_MOD = 6
EPS = 1e-5
N_CHIPS = 4
N_DEV = 8
LANES = 128
W_IN_BLOCK = IN_PROJ_WIDTH // N_CHIPS

ADAM_LR = 0.001
ADAM_B1 = 0.9
ADAM_B2 = 0.999
ADAM_EPS = 1e-08
ADAM_WD = 0.01
ADAM_STEP = 10

VMEM_LIMIT_BYTES = 58 * 1024 * 1024
MESH = pl.DeviceIdType.MESH


def _params(*semantics):
    return pltpu.CompilerParams(dimension_semantics=semantics, vmem_limit_bytes=VMEM_LIMIT_BYTES)


def _dot(a, b):
    return jnp.dot(a.astype(MXU_DTYPE), b.astype(MXU_DTYPE), preferred_element_type=F32)


def _dot_nt(a, b):
    return lax.dot_general(a.astype(MXU_DTYPE), b.astype(MXU_DTYPE), (((1,), (1,)), ((), ())),
                           preferred_element_type=F32)


def _dot_tn(a, b):
    return lax.dot_general(a.astype(MXU_DTYPE), b.astype(MXU_DTYPE), (((0,), (0,)), ((), ())),
                           preferred_element_type=F32)


def _full(shape):
    return pl.BlockSpec(shape, lambda *_: (0,) * len(shape))


def _any():
    return pl.BlockSpec(memory_space=pl.ANY)


def _rowsum(v):
    return jnp.sum(v, axis=0, keepdims=True)


def _mean_last(v):
    return jnp.mean(v, axis=-1, keepdims=True)


class _Comm(NamedTuple):
    operands: tuple
    out_shapes: tuple
    n_sems: int
    make: Callable
    in_place: int = 0


def _hosted_call(body, comm, *, name, grid, in_specs, out_shape, out_specs, scratch_shapes=(), semantics,
                 n_prefetch=0):
    if comm is None:
        return pl.pallas_call(
            body, name=name, out_shape=out_shape, compiler_params=_params(*semantics),
            grid_spec=pltpu.PrefetchScalarGridSpec(
                num_scalar_prefetch=n_prefetch, grid=grid, in_specs=in_specs, out_specs=out_specs,
                scratch_shapes=list(scratch_shapes)))
    n_in, n_out, n_scr = len(in_specs), len(out_shape), len(scratch_shapes)
    k_in, k_out = len(comm.operands), len(comm.out_shapes)

    def hosted(*refs):
        prefetched, refs = refs[:n_prefetch], refs[n_prefetch:]
        ins, refs = refs[:n_in], refs[n_in:]
        c_ins, refs = refs[:k_in], refs[k_in:]
        outs, refs = refs[:n_out], refs[n_out:]
        c_outs, refs = refs[:k_out], refs[k_out:]
        scratch, (send_sems, recv_sems) = refs[:n_scr], refs[n_scr:]
        first, last = None, None
        for d, size in enumerate(grid):
            at_start, at_end = pl.program_id(d) == 0, pl.program_id(d) == size - 1
            first = at_start if first is None else first & at_start
            last = at_end if last is None else last & at_end

        @pl.when(first)
        def _():
            for cp in comm.make(c_ins, c_outs, send_sems, recv_sems)[0]:
                cp.start()

        body(*prefetched, *ins, *outs, *scratch)

        @pl.when(last)
        def _():
            for wait in comm.make(c_ins, c_outs, send_sems, recv_sems)[1]:
                wait()

    aliases = {n_prefetch + n_in + i: n_out + i for i in range(comm.in_place)}
    call = pl.pallas_call(
        hosted, name=name, out_shape=list(out_shape) + list(comm.out_shapes),
        compiler_params=_params(*semantics), input_output_aliases=aliases,
        grid_spec=pltpu.PrefetchScalarGridSpec(
            num_scalar_prefetch=n_prefetch, grid=grid, in_specs=list(in_specs) + [_any()] * k_in,
            out_specs=list(out_specs) + [_any()] * k_out,
            scratch_shapes=list(scratch_shapes) + [pltpu.SemaphoreType.DMA((comm.n_sems,)),
                                                    pltpu.SemaphoreType.DMA((comm.n_sems,))]))
    return lambda *args: call(*args, *comm.operands)


class _Shifted:
    def __init__(self, base, offset):
        self.base, self.offset = base, offset

    @property
    def at(self):
        return self

    def __getitem__(self, k):
        return self.base.at[self.offset + k]


def _merge_jobs(*jobs):
    def order(count):
        first = [(j, i) for j, job in enumerate(jobs) for i in range(job.in_place)]
        return first + [(j, i) for j, job in enumerate(jobs) for i in range(job.in_place, count(job))]

    op_order, out_order = order(lambda job: len(job.operands)), order(lambda job: len(job.out_shapes))

    def make(ins, outs, send_sems, recv_sems):
        starts, waits, sem = [], [], 0
        for j, job in enumerate(jobs):
            mine_in = [ins[k] for k, (jj, _) in enumerate(op_order) if jj == j]
            mine_out = [outs[k] for k, (jj, _) in enumerate(out_order) if jj == j]
            s, w = job.make(mine_in, mine_out, _Shifted(send_sems, sem), _Shifted(recv_sems, sem))
            starts, waits, sem = starts + s, waits + w, sem + job.n_sems
        return starts, waits

    return _Comm(tuple(jobs[j].operands[i] for j, i in op_order), tuple(jobs[j].out_shapes[i] for j, i in out_order),
                 sum(job.n_sems for job in jobs), make, in_place=sum(job.in_place for job in jobs))


def _mesh_place():
    x, y, c = lax.axis_index("x"), lax.axis_index("y"), lax.axis_index("c")
    return x, y, c, [(1 - x, y), (x, 1 - y), (1 - x, 1 - y)]


def _gather_job(bufs):
    per = 4

    def make(ins, outs, send_sems, recv_sems):
        del ins
        x, y, c, chips = _mesh_place()
        starts, waits = [], []
        for a, out in enumerate(outs):
            mine = src = out.at[4 * x + 2 * y + c]
            to = [(x, y, 1 - c)] + [(px, py, c) for px, py in chips]
            sends = [pltpu.make_async_remote_copy(
                src_ref=src, dst_ref=mine, send_sem=send_sems.at[per * a + k],
                recv_sem=recv_sems.at[per * a + k], device_id=dev, device_id_type=MESH)
                for k, dev in enumerate(to)]
            recvs = [pltpu.make_async_remote_copy(
                src_ref=src, dst_ref=out.at[4 * px + 2 * py + pc], send_sem=send_sems.at[per * a + k],
                recv_sem=recv_sems.at[per * a + k], device_id=(px, py, pc), device_id_type=MESH)
                for k, (px, py, pc) in enumerate(to)]
            starts += sends
            waits += [s.wait_send for s in sends] + [r.wait_recv for r in recvs]
        return starts, waits

    shapes = tuple(jax.ShapeDtypeStruct(b.shape, b.dtype) for b in bufs)
    return _Comm(tuple(bufs), shapes, per * len(bufs), make, in_place=len(bufs))


def _slots(x, y, c):
    return 4 * x + 2 * y + c, 4 * (1 - x) + 2 * y + c, 4 * x + 2 * (1 - y) + c, 4 * (1 - x) + 2 * (1 - y) + c


def _gather2d_first(halves):
    per = 2

    def make(ins, outs, send_sems, recv_sems):
        x, y, c, _ = _mesh_place()
        me, xn, yn, _ = _slots(x, y, c)
        starts, waits = [], []
        for a, (src, out) in enumerate(zip(ins, outs)):
            blk = src.at[c]
            rows = blk.shape[0] // 2
            upper, lower = pl.ds(0, rows), pl.ds(rows, rows)

            def copy(k, src_ref, dst_ref, dev, a=a):
                return pltpu.make_async_remote_copy(
                    src_ref=src_ref, dst_ref=dst_ref, send_sem=send_sems.at[per * a + k],
                    recv_sem=recv_sems.at[per * a + k], device_id=dev, device_id_type=MESH)

            sends = [copy(0, blk.at[upper], out.at[me, upper], (1 - x, y, c)),
                     copy(1, blk.at[lower], out.at[me, lower], (x, 1 - y, c))]
            recvs = [copy(0, blk.at[upper], out.at[xn, upper], (1 - x, y, c)),
                     copy(1, blk.at[lower], out.at[yn, lower], (x, 1 - y, c))]
            starts += sends
            waits += [s.wait_send for s in sends] + [r.wait_recv for r in recvs]
        return starts, waits

    shapes = tuple(jax.ShapeDtypeStruct((N_DEV,) + h.shape[1:], h.dtype) for h in halves)
    return _Comm(tuple(halves), shapes, per * len(halves), make)


def _gather2d_second(bufs, halves):
    per = 4
    n_arr = len(bufs)

    def make(ins, outs, send_sems, recv_sems):
        x, y, c, _ = _mesh_place()
        me, xn, yn, dg = _slots(x, y, c)
        starts, waits = [], []
        for a, buf in enumerate(outs):
            own = ins[n_arr + a].at[c]
            rows = buf.shape[1] // 2
            upper, lower = pl.ds(0, rows), pl.ds(rows, rows)
            plan = [(own.at[upper], me, upper, (x, 1 - y, c), yn), (buf.at[xn, upper], xn, upper, (x, 1 - y, c), dg),
                    (own.at[lower], me, lower, (1 - x, y, c), xn), (buf.at[yn, lower], yn, lower, (1 - x, y, c), dg)]
            for k, (src, slot, part, dev, landing) in enumerate(plan):
                sems = dict(send_sem=send_sems.at[per * a + k], recv_sem=recv_sems.at[per * a + k],
                            device_id=dev, device_id_type=MESH)
                send = pltpu.make_async_remote_copy(src_ref=src, dst_ref=buf.at[slot, part], **sems)
                arrival = pltpu.make_async_remote_copy(src_ref=src, dst_ref=buf.at[landing, part], **sems)
                starts.append(send)
                waits += [send.wait_send, arrival.wait_recv]
        return starts, waits

    shapes = tuple(jax.ShapeDtypeStruct(b.shape, b.dtype) for b in bufs)
    return _Comm(tuple(bufs) + tuple(halves), shapes, per * n_arr, make, in_place=n_arr)


def _gather_forward(bufs, name):
    n_arr = len(bufs)

    def body(*refs):
        outs = refs[n_arr:2 * n_arr]
        send_sems, recv_sems = refs[2 * n_arr:]
        x, y, c, chips = _mesh_place()
        sends, recvs = [], []
        for a, buf in enumerate(outs):
            for j, (px, py) in enumerate(chips):
                mine, theirs = buf.at[4 * px + 2 * py + c], buf.at[4 * px + 2 * py + 1 - c]
                sems = dict(send_sem=send_sems.at[3 * a + j], recv_sem=recv_sems.at[3 * a + j],
                            device_id=(x, y, 1 - c), device_id_type=MESH)
                sends.append(pltpu.make_async_remote_copy(src_ref=mine, dst_ref=mine, **sems))
                recvs.append(pltpu.make_async_remote_copy(src_ref=mine, dst_ref=theirs, **sems))
        for cp in sends:
            cp.start()
        for s, r in zip(sends, recvs):
            s.wait_send()
            r.wait_recv()

    return pl.pallas_call(
        body, name=name, out_shape=[jax.ShapeDtypeStruct(b.shape, b.dtype) for b in bufs],
        in_specs=[_any()] * n_arr, out_specs=[_any()] * n_arr,
        input_output_aliases={a: a for a in range(n_arr)},
        scratch_shapes=[pltpu.SemaphoreType.DMA((3 * n_arr,)), pltpu.SemaphoreType.DMA((3 * n_arr,))],
    )(*bufs)


def _scatter_job(chip_sums, rows=None, into=()):
    n_into = len(into)

    def part(ref):
        return ref if rows is None else ref.at[pl.ds(rows[0], rows[1])]

    def make(ins, outs, send_sems, recv_sems):
        x, y, c, chips = _mesh_place()
        copies = [pltpu.make_async_remote_copy(
            src_ref=part(src.at[2 * px + py]), dst_ref=part(out.at[j]), send_sem=send_sems.at[3 * a + j],
            recv_sem=recv_sems.at[3 * a + j], device_id=(px, py, c), device_id_type=MESH)
            for a, (src, out) in enumerate(zip(ins[n_into:], outs)) for j, (px, py) in enumerate(chips)]
        return copies, [cp.wait for cp in copies]

    shapes = tuple(jax.ShapeDtypeStruct((3,) + s.shape[1:], s.dtype) for s in chip_sums)
    return _Comm(tuple(into) + tuple(chip_sums), shapes, 3 * len(chip_sums), make, in_place=n_into)


def _all_gather8(blocks, name, split=False, forward=(), riders=(), skip_own=()):
    n_arr, n_fwd = len(blocks), len(forward)
    splits = list(split) if isinstance(split, (list, tuple)) else [split] * n_arr
    own_slots = [a not in skip_own for a in range(n_arr)]
    rider_in = sum(len(r.operands) for r in riders)
    rider_out = sum(len(r.out_shapes) for r in riders)

    def body(*refs):
        x_refs, refs = refs[:n_arr], refs[n_arr + n_fwd:]
        r_ins, refs = refs[:rider_in], refs[rider_in:]
        out_refs, refs = refs[:n_arr], refs[n_arr:]
        fwd_refs, refs = refs[:n_fwd], refs[n_fwd:]
        r_outs, refs = refs[:rider_out], refs[rider_out:]
        (send_sems, recv_sems, local_sems), rider_sems = refs[:3], refs[3:]
        x, y, c, chips = _mesh_place()
        me, sibling = (x, y, c), (x, y, 1 - c)
        rider_waits, i0, o0 = [], 0, 0
        for n, job in enumerate(riders):
            k_in, k_out = len(job.operands), len(job.out_shapes)
            starts, waits = job.make(r_ins[i0:i0 + k_in], r_outs[o0:o0 + k_out],
                                     rider_sems[2 * n], rider_sems[2 * n + 1])
            for cp in starts:
                cp.start()
            rider_waits += waits
            i0, o0 = i0 + k_in, o0 + k_out
        passing = []
        for f, buf in enumerate(fwd_refs):
            for j, (px, py) in enumerate(chips):
                mine, theirs = buf.at[4 * px + 2 * py + c], buf.at[4 * px + 2 * py + 1 - c]
                sems = dict(send_sem=send_sems.at[7 * n_arr + 3 * f + j], recv_sem=recv_sems.at[7 * n_arr + 3 * f + j],
                            device_id=sibling, device_id_type=MESH)
                passing.append((pltpu.make_async_remote_copy(src_ref=mine, dst_ref=mine, **sems),
                                pltpu.make_async_remote_copy(src_ref=mine, dst_ref=theirs, **sems)))
        for send, _ in passing:
            send.start()
        arrays = []
        for a, (x_ref, out_ref) in enumerate(zip(x_refs, out_refs)):
            src_mine = x_ref.at[c] if splits[a] else x_ref

            def copy(k, blk, to, src=None, a=a, out_ref=out_ref):
                dst = out_ref.at[4 * blk[0] + 2 * blk[1] + blk[2]]
                return pltpu.make_async_remote_copy(
                    src_ref=dst if src is None else src, dst_ref=dst,
                    send_sem=send_sems.at[7 * a + k], recv_sem=recv_sems.at[7 * a + k],
                    device_id=to, device_id_type=MESH)

            mine = pltpu.make_async_copy(src_mine, out_ref.at[4 * x + 2 * y + c], local_sems.at[a])
            first = [copy(0, me, sibling, src=src_mine)] if own_slots[a] else []
            first += [copy(1 + j, me, (*chip, c), src=src_mine) for j, chip in enumerate(chips)]
            for cp in first + ([mine] if own_slots[a] else []):
                cp.start()
            arrays.append((copy, mine, first, own_slots[a]))
        sent = []
        for copy, mine, first, own in arrays:
            passed = [copy(4 + j, (*chip, c), sibling) for j, chip in enumerate(chips)]
            for j, chip in enumerate(chips):
                copy(1 + j, (*chip, c), me).wait_recv()
                passed[j].start()
            sent += first + passed
        for copy, mine, first, own in arrays:
            if own:
                copy(0, sibling, me).wait_recv()
                mine.wait()
            for j, chip in enumerate(chips):
                copy(4 + j, (*chip, 1 - c), me).wait_recv()
        for cp in sent:
            cp.wait_send()
        for send, arrival in passing:
            send.wait_send()
            arrival.wait_recv()
        for wait in rider_waits:
            wait()

    n_sems = 7 * n_arr + 3 * n_fwd
    rider_operands = [a for r in riders for a in r.operands]
    rider_shapes = [s for r in riders for s in r.out_shapes]
    return pl.pallas_call(
        body, name=name,
        out_shape=[jax.ShapeDtypeStruct((N_DEV,) + tuple(b.shape[1:] if s else b.shape), b.dtype)
                   for b, s in zip(blocks, splits)]
        + [jax.ShapeDtypeStruct(f.shape, f.dtype) for f in forward] + rider_shapes,
        in_specs=[_any()] * (n_arr + n_fwd + rider_in), out_specs=[_any()] * (n_arr + n_fwd + rider_out),
        input_output_aliases={n_arr + f: n_arr + f for f in range(n_fwd)},
        scratch_shapes=[pltpu.SemaphoreType.DMA((n_sems,)), pltpu.SemaphoreType.DMA((n_sems,)),
                        pltpu.SemaphoreType.DMA((n_arr,))]
        + [pltpu.SemaphoreType.DMA((r.n_sems,)) for r in riders for _ in range(2)],
    )(*blocks, *forward, *rider_operands)


def _share_job(bufs):
    def make(ins, outs, send_sems, recv_sems):
        del ins
        x, y, c, _ = _mesh_place()
        sems = lambda a: dict(send_sem=send_sems.at[a], recv_sem=recv_sems.at[a],
                              device_id=(x, y, 1 - c), device_id_type=MESH)
        sends = [pltpu.make_async_remote_copy(src_ref=o.at[c], dst_ref=o.at[c], **sems(a)) for a, o in enumerate(outs)]
        arrivals = [pltpu.make_async_remote_copy(src_ref=o.at[c], dst_ref=o.at[1 - c], **sems(a))
                    for a, o in enumerate(outs)]
        return sends, [s.wait_send for s in sends] + [r.wait_recv for r in arrivals]

    shapes = tuple(jax.ShapeDtypeStruct(b.shape, b.dtype) for b in bufs)
    return _Comm(tuple(bufs), shapes, len(bufs), make, in_place=len(bufs))


def _sibling_share(bufs, name):
    n_arr = len(bufs)

    def body(*refs):
        out_refs = refs[n_arr:2 * n_arr]
        send_sems, recv_sems = refs[2 * n_arr:]
        x, y, c = lax.axis_index("x"), lax.axis_index("y"), lax.axis_index("c")
        copies = [pltpu.make_async_remote_copy(
            src_ref=out_refs[a].at[c], dst_ref=out_refs[a].at[c],
            send_sem=send_sems.at[a], recv_sem=recv_sems.at[a],
            device_id=(x, y, 1 - c), device_id_type=MESH) for a in range(n_arr)]
        for cp in copies:
            cp.start()
        for a in range(n_arr):
            pltpu.make_async_remote_copy(
                src_ref=out_refs[a].at[c], dst_ref=out_refs[a].at[1 - c],
                send_sem=send_sems.at[a], recv_sem=recv_sems.at[a],
                device_id=(x, y, 1 - c), device_id_type=MESH).wait()

    return pl.pallas_call(
        body, name=name,
        out_shape=[jax.ShapeDtypeStruct(b.shape, b.dtype) for b in bufs],
        in_specs=[_any()] * n_arr, out_specs=[_any()] * n_arr,
        input_output_aliases={a: a for a in range(n_arr)},
        scratch_shapes=[pltpu.SemaphoreType.DMA((n_arr,)), pltpu.SemaphoreType.DMA((n_arr,))],
    )(*bufs)


def _gelu_tanh(z):
    k = math.sqrt(2.0 / math.pi)
    t = jnp.tanh(k * (z + 0.044715 * (z * z * z)))
    return 0.5 * z * (1.0 + t), t


def _gelu_tanh_grad(z, t):
    k = math.sqrt(2.0 / math.pi)
    return 0.5 * (1.0 + t) + 0.5 * z * (1.0 - t * t) * (k * (1.0 + 3.0 * 0.044715 * (z * z)))


def _rope_angle_kernel(pos_row, invf_col):
    seq = pos_row.shape[1]

    def body(p_ref, f_ref, cos_ref, sin_ref):
        ang = p_ref[...].astype(F32) * f_ref[...]
        cos_ref[...] = jnp.cos(ang)
        sin_ref[...] = jnp.sin(ang)

    return pl.pallas_call(
        body, name="rope_angles", grid=(1,), out_shape=[jax.ShapeDtypeStruct((ROT_DIM // 2, seq), F32)] * 2,
        in_specs=[_full((1, seq)), _full((ROT_DIM // 2, 1))], out_specs=[_full((ROT_DIM // 2, seq))] * 2,
        compiler_params=_params("arbitrary"),
    )(pos_row, invf_col)


def _rope_lane_tables(cos, sin):
    cos_t, sin_t = cos.T, sin.T
    seq, half = cos_t.shape
    ones = jnp.ones((seq, HEAD_DIM - ROT_DIM), F32)
    c64 = jnp.concatenate([cos_t, cos_t, ones], axis=1)
    s1 = jnp.concatenate([sin_t, jnp.zeros((seq, HEAD_DIM - half), F32)], axis=1)
    s2 = jnp.concatenate([jnp.zeros((seq, half), F32), sin_t, jnp.zeros((seq, HEAD_DIM - ROT_DIM), F32)], axis=1)
    return jnp.concatenate([jnp.tile(t, (1, LANES // HEAD_DIM)) for t in (c64, s1, s2)], axis=1)


def _rope_apply(t, tab, sign):
    reps = t.shape[1] // LANES
    c_tab, s1, s2 = (jnp.tile(tab[:, LANES * k:LANES * (k + 1)], (1, reps)) if reps > 1
                     else tab[:, LANES * k:LANES * (k + 1)] for k in range(3))
    half = ROT_DIM // 2
    up = pltpu.roll(t, t.shape[1] - half, 1)
    down = pltpu.roll(t, half, 1)
    return t * c_tab + sign * (down * s2 - up * s1)


def _lane_masks(shape):
    lane = lax.broadcasted_iota(jnp.int32, shape, 1)
    return lane < HEAD_DIM, lane >= HEAD_DIM


HEADS_PER_GROUP = N_Q_HEADS // N_KV_HEADS
ATTN_SCALE = 1.0 / math.sqrt(HEAD_DIM)


def _attn_bias_t(first_block):
    kj = lax.broadcasted_iota(jnp.int32, (2 * CHUNK, CHUNK), 0)
    qi = lax.broadcasted_iota(jnp.int32, (2 * CHUNK, CHUNK), 1)
    ok = (kj > qi) & (kj <= qi + CHUNK)
    if first_block is not None:
        ok = ok & (jnp.logical_not(first_block) | (kj >= CHUNK))
    return jnp.tile(jnp.where(ok, 0.0, -jnp.inf), (1, HEADS_PER_GROUP))


def _group_rows(x, g, lo, hi):
    rows = []
    for r in range(HEADS_PER_GROUP):
        h = HEADS_PER_GROUP * g + r
        pair = x[:, LANES * (h // 2):LANES * (h // 2 + 1)]
        rows.append(jnp.where(hi if h % 2 else lo, pair, 0.0))
    return jnp.concatenate(rows, axis=0)


def _pairs_from_rows(rows, lo):
    return [jnp.where(lo, rows[2 * CHUNK * k:2 * CHUNK * k + CHUNK], rows[2 * CHUNK * k + CHUNK:2 * CHUNK * (k + 1)])
            for k in range(HEADS_PER_GROUP // 2)]


def _group_dup(a, b, g, lo2):
    return jnp.where(lo2, a, b) if g == 0 else jnp.where(lo2, b, a)


def _sink_row(sink_ref, g):
    return jnp.concatenate([sink_ref[HEADS_PER_GROUP * g + r:HEADS_PER_GROUP * g + r + 1, :]
                            for r in range(HEADS_PER_GROUP)], axis=1)


def _attn_probs_t(k_dup, q_rows, bias_t, sink_row):
    s_t = _dot_nt(k_dup, q_rows) * ATTN_SCALE + bias_t
    m = jnp.maximum(jnp.max(s_t, axis=0, keepdims=True), sink_row)
    p = jnp.exp(s_t - m)
    e_sink = jnp.exp(sink_row - m)
    inv = 1.0 / (jnp.sum(p, axis=0, keepdims=True) + e_sink)
    return p * inv, e_sink * inv


def _sgu_forward_pair(wm, vp, j):
    lo, hi = _lane_masks(vp.shape)
    lhs = jnp.concatenate([wm[2 * j], wm[2 * j + 1]], axis=1)
    rhs = jnp.concatenate([jnp.where(lo, vp, 0.0), jnp.where(hi, vp, 0.0)], axis=0)
    return _dot(lhs, rhs)


def _masked_spatial(w_ref):
    t = lax.broadcasted_iota(jnp.int32, (CHUNK, CHUNK), 0)
    s = lax.broadcasted_iota(jnp.int32, (CHUNK, CHUNK), 1)
    tril = s <= t
    return [jnp.where(tril, w_ref[g], 0.0) for g in range(GMLP_GROUPS)], tril, s >= t


def _mod_kernel(c_all, w_shard, b_shard, comm=None):
    n = w_shard.shape[1]
    tn = 512

    def body(c_ref, w_ref, b_ref, mod_ref, act_ref):
        cv = c_ref[...]
        act = cv * (1.0 / (1.0 + jnp.exp(-cv)))
        act_ref[...] = act
        mod_ref[...] = _dot(act, w_ref[...]) + b_ref[...]

    return _hosted_call(
        body, comm, name="ada_mod", grid=(n // tn,),
        out_shape=[jax.ShapeDtypeStruct((N_DEV, n), F32), jax.ShapeDtypeStruct((N_DEV, D_MODEL), F32)],
        in_specs=[_full((N_DEV, D_MODEL)), pl.BlockSpec((D_MODEL, tn), lambda i: (0, i)),
                  pl.BlockSpec((1, tn), lambda i: (0, i))],
        out_specs=[pl.BlockSpec((N_DEV, tn), lambda i: (0, i)), _full((N_DEV, D_MODEL))],
        semantics=("arbitrary",),
    )(c_all, w_shard, b_shard)


def _load_chip_blocks(chip_ref, gathered, local, dsts, sems, first_sem=0):
    for k, dst in enumerate(dsts):
        @pl.when(chip_ref[0] == k)
        def _():
            pltpu.make_async_copy(local, dst, sems.at[first_sem + k]).start()

        @pl.when(chip_ref[0] != k)
        def _():
            pltpu.make_async_copy(gathered.at[k], dst, sems.at[first_sem + k]).start()
    return [pltpu.make_async_copy(local, dst, sems.at[first_sem + k]).wait for k, dst in enumerate(dsts)]


def _in_proj_kernel(x, vecs, w_in_t, comm=None):
    seq = x.shape[0]
    tm = 512

    def body(x_ref, v_ref, w_ref, proj_ref, h_ref):
        xv = x_ref[...]
        rstd = lax.rsqrt(_mean_last(xv * xv) + EPS)
        n1 = (xv * rstd) * v_ref[0:1, :]
        h = n1 * (1.0 + v_ref[2:3, :]) + v_ref[1:2, :]
        hb = h.astype(MXU_DTYPE)
        h_ref[...] = hb
        proj_ref[...] = _dot_nt(hb, w_ref[...])

    return _hosted_call(
        body, comm, name="in_proj", grid=(seq // tm,),
        out_shape=[jax.ShapeDtypeStruct((seq, IN_PROJ_WIDTH), F32),
                   jax.ShapeDtypeStruct((seq, D_MODEL), MXU_DTYPE)],
        in_specs=[pl.BlockSpec((tm, D_MODEL), lambda i: (i, 0)), _full((8, D_MODEL)),
                  _full((IN_PROJ_WIDTH, D_MODEL))],
        out_specs=[pl.BlockSpec((tm, IN_PROJ_WIDTH), lambda i: (i, 0)),
                   pl.BlockSpec((tm, D_MODEL), lambda i: (i, 0))],
        semantics=("arbitrary",),
    )(x, vecs, w_in_t)


MIXER_BLOCKS_PER_STEP = 4
KV_START = 2 * GMLP_WIDTH + ATTN_WIDTH


def _mixer_fwd_kernel(proj, rope_tab, w_spatial, bias_full, sink_rows, comm=None):
    seq = proj.shape[0]
    per = MIXER_BLOCKS_PER_STEP
    steps = seq // (CHUNK * per)
    kv_col = KV_START // (2 * KV_WIDTH)

    def body(proj_ref, prev_ref, tab_ref, ptab_ref, w_ref, bias_ref, sink_ref, cat_ref):
        i = pl.program_id(0)
        wm, _, _ = _masked_spatial(w_ref)
        lo, hi = _lane_masks((CHUNK, LANES))
        lo2, _ = _lane_masks((2 * CHUNK, LANES))
        o = 2 * GMLP_WIDTH
        for s in range(per):
            rows, before = slice(CHUNK * s, CHUNK * (s + 1)), slice(CHUNK * (s - 1), CHUNK * s)
            for j in range(GMLP_GROUPS // 2):
                cols = slice(LANES * j, LANES * (j + 1))
                vcols = slice(GMLP_WIDTH + LANES * j, GMLP_WIDTH + LANES * (j + 1))
                u, _ = _gelu_tanh(proj_ref[rows, cols])
                vp, _ = _gelu_tanh(proj_ref[rows, vcols])
                sv = _sgu_forward_pair(wm, vp, j) + bias_ref[:, cols]
                cat_ref[rows, cols] = (u * sv).astype(cat_ref.dtype)
            tab = tab_ref[rows, :]
            if s == 0:
                prev_kv, prev_tab, first = prev_ref[...], ptab_ref[...], i == 0
            else:
                prev_kv, prev_tab, first = proj_ref[before, KV_START:KV_START + 2 * KV_WIDTH], tab_ref[before, :], None
            q_r = _rope_apply(proj_ref[rows, o:o + ATTN_WIDTH], tab, 1.0)
            k_cur = _rope_apply(proj_ref[rows, KV_START:KV_START + KV_WIDTH], tab, 1.0)
            k_prev = _rope_apply(prev_kv[:, 0:KV_WIDTH], prev_tab, 1.0)
            k_a = jnp.concatenate([k_prev, k_cur], axis=0)
            v_a = jnp.concatenate([prev_kv[:, KV_WIDTH:2 * KV_WIDTH],
                                   proj_ref[rows, KV_START + KV_WIDTH:KV_START + 2 * KV_WIDTH]], axis=0)
            k_b = pltpu.roll(k_a, HEAD_DIM, 1)
            v_b = pltpu.roll(v_a, HEAD_DIM, 1)
            bias_t = _attn_bias_t(first)
            for g in range(N_KV_HEADS):
                p_t, _ = _attn_probs_t(_group_dup(k_a, k_b, g, lo2), _group_rows(q_r, g, lo, hi), bias_t,
                                       _sink_row(sink_ref, g))
                o_t = _dot(_group_dup(v_a, v_b, g, lo2).T, p_t)
                for k, pair in enumerate(_pairs_from_rows(o_t.T, lo)):
                    c0 = GMLP_WIDTH + LANES * (2 * g + k)
                    cat_ref[rows, c0:c0 + LANES] = pair.astype(cat_ref.dtype)

    return _hosted_call(
        body, comm, name="mixer_fwd", grid=(steps,),
        out_shape=[jax.ShapeDtypeStruct((seq, D_MODEL), MXU_DTYPE)],
        in_specs=[pl.BlockSpec((CHUNK * per, IN_PROJ_WIDTH), lambda i: (i, 0)),
                  pl.BlockSpec((CHUNK, 2 * KV_WIDTH), lambda i: (jnp.maximum(per * i - 1, 0), kv_col)),
                  pl.BlockSpec((CHUNK * per, 3 * LANES), lambda i: (i, 0)),
                  pl.BlockSpec((CHUNK, 3 * LANES), lambda i: (jnp.maximum(per * i - 1, 0), 0)),
                  _full((GMLP_GROUPS, CHUNK, CHUNK)), _full((CHUNK, GMLP_WIDTH)),
                  _full((N_Q_HEADS, LANES))],
        out_specs=[pl.BlockSpec((CHUNK * per, D_MODEL), lambda i: (i, 0))],
        semantics=("arbitrary",),
    )(proj, proj, rope_tab, rope_tab, w_spatial, bias_full, sink_rows)


def _trunk_kernel(x, target, cat, vecs, chip_idx, gathered, local):
    seq = x.shape[0]
    tm = 256
    nj = D_FF // D_MODEL
    out_rows = D_MODEL // N_CHIPS

    def body(chip_ref, x_ref, t_ref, cat_ref, v_ref, g_out, g_w1, g_w2, l_out, l_w1, l_w2,
             dx1_ref, dcat_ref, dmix_ref, h2_ref, r_ref, da_ref, dff_ref, sums_ref,
             wout, w1, w2, a_scr, sem):
        i = pl.program_id(0)

        @pl.when(i == 0)
        def _():
            waits = _load_chip_blocks(chip_ref, g_out, l_out,
                                      [wout.at[pl.ds(out_rows * k, out_rows)] for k in range(N_CHIPS)], sem)
            waits += _load_chip_blocks(chip_ref, g_w1, l_w1, [w1.at[k] for k in range(N_CHIPS)], sem, N_CHIPS)
            waits += _load_chip_blocks(chip_ref, g_w2, l_w2, [w2.at[k] for k in range(N_CHIPS)], sem, 2 * N_CHIPS)
            for wait in waits:
                wait()
            sums_ref[...] = jnp.zeros_like(sums_ref)

        gate1, shift2, scale2 = v_ref[0:1, :], v_ref[1:2, :], v_ref[2:3, :]
        gate2, g_ffn, g_final = v_ref[3:4, :], v_ref[4:5, :], v_ref[5:6, :]

        mix = _dot(cat_ref[...], wout[...])
        x1 = x_ref[...] + gate1 * mix
        rstd2 = lax.rsqrt(_mean_last(x1 * x1) + EPS)
        xh2 = x1 * rstd2
        n2 = xh2 * g_ffn
        h2b = (n2 * (1.0 + scale2) + shift2).astype(MXU_DTYPE)
        h2_ref[...] = h2b
        ff = jnp.zeros((tm, D_MODEL), F32)
        for j in range(nj):
            a = _dot(h2b, w1[j])
            a_scr[j] = a
            relu = jnp.maximum(a, 0.0)
            rb = (relu * relu).astype(MXU_DTYPE)
            r_ref[:, D_MODEL * j:D_MODEL * (j + 1)] = rb
            ff = ff + _dot(rb, w2[j])
        x2 = x1 + gate2 * ff
        rstd3 = lax.rsqrt(_mean_last(x2 * x2) + EPS)
        xh3 = x2 * rstd3
        err = xh3 * g_final - t_ref[...]
        loss = 0.5 * _rowsum(_mean_last(err * err))
        dy = err * (1.0 / D_MODEL)
        dxh3 = dy * g_final
        dx2 = rstd3 * (dxh3 - xh3 * _mean_last(dxh3 * xh3))
        dffb = (dx2 * gate2).astype(MXU_DTYPE)
        dff_ref[...] = dffb
        dh2 = jnp.zeros((tm, D_MODEL), F32)
        for j in range(nj):
            dr = _dot_nt(dffb, w2[j])
            dab = (dr * (2.0 * jnp.maximum(a_scr[j], 0.0))).astype(MXU_DTYPE)
            da_ref[:, D_MODEL * j:D_MODEL * (j + 1)] = dab
            dh2 = dh2 + _dot_nt(dab, w1[j])
        dn2 = dh2 * (1.0 + scale2)
        dxh2 = dn2 * g_ffn
        dx1 = dx2 + rstd2 * (dxh2 - xh2 * _mean_last(dxh2 * xh2))
        dx1_ref[...] = dx1
        dmixb = (dx1 * gate1).astype(MXU_DTYPE)
        dmix_ref[...] = dmixb
        dcat_ref[...] = _dot_nt(dmixb, wout[...])

        sums_ref[0:1, :] += _rowsum(dh2)
        sums_ref[1:2, :] += _rowsum(dh2 * n2)
        sums_ref[2:3, :] += _rowsum(dx2 * ff)
        sums_ref[3:4, :] += _rowsum(dn2 * xh2)
        sums_ref[4:5, :] += _rowsum(dy * xh3)
        sums_ref[5:6, :] += _rowsum(dx1 * mix)
        sums_ref[6:7, :] += jnp.broadcast_to(loss, (1, D_MODEL))

    tok = lambda w: pl.BlockSpec((tm, w), lambda i, chip: (i, 0))
    return _hosted_call(
        body, None, name="trunk", grid=(seq // tm,), n_prefetch=1,
        out_shape=[jax.ShapeDtypeStruct((seq, D_MODEL), F32), jax.ShapeDtypeStruct((seq, D_MODEL), F32),
                   jax.ShapeDtypeStruct((seq, D_MODEL), MXU_DTYPE), jax.ShapeDtypeStruct((seq, D_MODEL), MXU_DTYPE),
                   jax.ShapeDtypeStruct((seq, D_FF), MXU_DTYPE), jax.ShapeDtypeStruct((seq, D_FF), MXU_DTYPE),
                   jax.ShapeDtypeStruct((seq, D_MODEL), MXU_DTYPE), jax.ShapeDtypeStruct((8, D_MODEL), F32)],
        in_specs=[tok(D_MODEL), tok(D_MODEL), tok(D_MODEL), _full((8, D_MODEL))] + [_any()] * 6,
        out_specs=[tok(D_MODEL), tok(D_MODEL), tok(D_MODEL), tok(D_MODEL), tok(D_FF), tok(D_FF), tok(D_MODEL),
                   _full((8, D_MODEL))],
        scratch_shapes=[pltpu.VMEM((D_MODEL, D_MODEL), MXU_DTYPE), pltpu.VMEM((nj, D_MODEL, D_MODEL), MXU_DTYPE),
                        pltpu.VMEM((nj, D_MODEL, D_MODEL), MXU_DTYPE), pltpu.VMEM((nj, tm, D_MODEL), F32),
                        pltpu.SemaphoreType.DMA((3 * N_CHIPS,))],
        semantics=("arbitrary",),
    )(chip_idx, x, target, cat, vecs, *gathered, *local)


def _mixer_bwd_kernel(proj, rope_tab, dcat, w_spatial, w_spatial_t, bias_full, sink_rows, dev_idx, comm=None):
    seq = proj.shape[0]
    per = MIXER_BLOCKS_PER_STEP
    steps = seq // (CHUNK * per)
    kv_col = KV_START // (2 * KV_WIDTH)

    def body(dev_ref, proj_ref, prev_ref, tab_ref, ptab_ref, dcat_ref, w_ref, wt_ref, bias_ref, sink_ref,
             dproj_ref, dw_out, db_ref, dsink_ref, carry, dw_ref):
        del dev_ref
        step = pl.program_id(0)

        @pl.when(step == 0)
        def _():
            carry[...] = jnp.zeros_like(carry)
            dw_ref[...] = jnp.zeros_like(dw_ref)
            db_ref[...] = jnp.zeros_like(db_ref)
            dsink_ref[...] = jnp.zeros_like(dsink_ref)

        for s in reversed(range(per)):
            rows = pl.ds(CHUNK * s, CHUNK)
            if s == 0:
                before, before_tab, first = prev_ref, ptab_ref, step == steps - 1
            else:
                before = proj_ref.at[pl.ds(CHUNK * (s - 1), CHUNK), pl.ds(KV_START, 2 * KV_WIDTH)]
                before_tab, first = tab_ref.at[pl.ds(CHUNK * (s - 1), CHUNK)], None
            one_block(proj_ref.at[rows], before, tab_ref.at[rows], before_tab, dcat_ref.at[rows], w_ref, wt_ref,
                      bias_ref, sink_ref, dproj_ref.at[rows], dw_ref, db_ref, dsink_ref, carry, first)

        @pl.when(step == steps - 1)
        def _():
            dw_out[...] = dw_ref[...].astype(dw_out.dtype)

    def one_block(proj_ref, prev_ref, tab_ref, ptab_ref, dcat_ref, w_ref, wt_ref, bias_ref, sink_ref,
                  dproj_ref, dw_ref, db_ref, dsink_ref, carry, first):
        wm, tril, triu = _masked_spatial(w_ref)
        lo, hi = _lane_masks((CHUNK, LANES))
        lane = lax.broadcasted_iota(jnp.int32, (CHUNK, LANES), 1)
        db = jnp.zeros((CHUNK, LANES), F32)
        for j in range(GMLP_GROUPS // 2):
            cols = slice(LANES * j, LANES * (j + 1))
            vcols = slice(GMLP_WIDTH + LANES * j, GMLP_WIDTH + LANES * (j + 1))
            zu, zv = proj_ref[:, cols], proj_ref[:, vcols]
            u, tu = _gelu_tanh(zu)
            vp, tv = _gelu_tanh(zv)
            sv = _sgu_forward_pair(wm, vp, j) + bias_ref[:, cols]
            dout = dcat_ref[:, cols]
            du = dout * sv
            dsv = dout * u
            dsv_lo, dsv_hi = jnp.where(lo, dsv, 0.0), jnp.where(hi, dsv, 0.0)
            lhs_t = jnp.concatenate([jnp.where(triu, wt_ref[2 * j], 0.0),
                                     jnp.where(triu, wt_ref[2 * j + 1], 0.0)], axis=1)
            dv = _dot(lhs_t, jnp.concatenate([dsv_lo, dsv_hi], axis=0))
            dw_ref[2 * j] += jnp.where(tril, _dot_nt(dsv_lo, vp), 0.0)
            dw_ref[2 * j + 1] += jnp.where(tril, _dot_nt(dsv_hi, vp), 0.0)
            db = db + (jnp.where(lane == 2 * j, jnp.sum(dsv_lo, axis=1, keepdims=True), 0.0)
                       + jnp.where(lane == 2 * j + 1, jnp.sum(dsv_hi, axis=1, keepdims=True), 0.0))
            dproj_ref[:, cols] = (du * _gelu_tanh_grad(zu, tu)).astype(dproj_ref.dtype)
            dproj_ref[:, vcols] = (dv * _gelu_tanh_grad(zv, tv)).astype(dproj_ref.dtype)
        db_ref[...] += db
        o = 2 * GMLP_WIDTH
        tab = tab_ref[...]
        q_r = _rope_apply(proj_ref[:, o:o + ATTN_WIDTH], tab, 1.0)
        k_cur = _rope_apply(proj_ref[:, o + ATTN_WIDTH:o + ATTN_WIDTH + KV_WIDTH], tab, 1.0)
        k_prev = _rope_apply(prev_ref[:, 0:KV_WIDTH], ptab_ref[...], 1.0)
        k_a = jnp.concatenate([k_prev, k_cur], axis=0)
        v_a = jnp.concatenate([prev_ref[:, KV_WIDTH:2 * KV_WIDTH],
                               proj_ref[:, o + ATTN_WIDTH + KV_WIDTH:o + ATTN_WIDTH + 2 * KV_WIDTH]], axis=0)
        k_b = pltpu.roll(k_a, HEAD_DIM, 1)
        v_b = pltpu.roll(v_a, HEAD_DIM, 1)
        bias_t = _attn_bias_t(first)
        lo2, _ = _lane_masks((2 * CHUNK, LANES))
        dout_b = dcat_ref[:, GMLP_WIDTH:GMLP_WIDTH + ATTN_WIDTH]
        dk_tot, dv_tot, dq_pairs = [], [], []
        for g in range(N_KV_HEADS):
            k_dup, v_dup = _group_dup(k_a, k_b, g, lo2), _group_dup(v_a, v_b, g, lo2)
            q_rows = _group_rows(q_r, g, lo, hi)
            do_rows = _group_rows(dout_b, g, lo, hi)
            p_t, p_sink = _attn_probs_t(k_dup, q_rows, bias_t, _sink_row(sink_ref, g))
            dp_t = _dot_nt(v_dup, do_rows)
            delta = jnp.sum(p_t * dp_t, axis=0, keepdims=True)
            ds_t = p_t * (dp_t - delta) * ATTN_SCALE
            dsink = -p_sink * delta
            for r in range(HEADS_PER_GROUP):
                h = HEADS_PER_GROUP * g + r
                dsink_ref[h:h + 1, :] += jnp.broadcast_to(
                    jnp.sum(dsink[:, LANES * r:LANES * (r + 1)], axis=1, keepdims=True), (1, LANES))
            dk_full = _dot(ds_t, q_rows)
            dv_full = _dot(p_t, do_rows)
            dk_tot.append(dk_full + pltpu.roll(dk_full, HEAD_DIM, 1))
            dv_tot.append(dv_full + pltpu.roll(dv_full, HEAD_DIM, 1))
            dq_t = _dot(k_dup.T, ds_t)
            dq_pairs += _pairs_from_rows(dq_t.T, lo)
        dk_all = jnp.where(lo2, dk_tot[0], dk_tot[1])
        dv_all = jnp.where(lo2, dv_tot[0], dv_tot[1])
        dk_cur = dk_all[CHUNK:, :] + carry[:, 0:KV_WIDTH]
        dv_cur = dv_all[CHUNK:, :] + carry[:, KV_WIDTH:2 * KV_WIDTH]
        carry[:, 0:KV_WIDTH] = dk_all[:CHUNK, :]
        carry[:, KV_WIDTH:2 * KV_WIDTH] = dv_all[:CHUNK, :]
        dq = _rope_apply(jnp.concatenate(dq_pairs, axis=1), tab, -1.0)
        dproj_ref[:, o:o + ATTN_WIDTH] = dq.astype(dproj_ref.dtype)
        dproj_ref[:, o + ATTN_WIDTH:o + ATTN_WIDTH + KV_WIDTH] = (
            _rope_apply(dk_cur, tab, -1.0).astype(dproj_ref.dtype))
        dproj_ref[:, o + ATTN_WIDTH + KV_WIDTH:o + ATTN_WIDTH + 2 * KV_WIDTH] = dv_cur.astype(dproj_ref.dtype)

    rev = lambda i: steps - 1 - i
    before = lambda i: jnp.maximum(per * rev(i) - 1, 0)
    slot = lambda shape: pl.BlockSpec((None,) + shape, lambda i, d: (d[0],) + (0,) * len(shape))
    return _hosted_call(
        body, comm, name="mixer_bwd", grid=(steps,), n_prefetch=1,
        out_shape=[jax.ShapeDtypeStruct((seq, IN_PROJ_WIDTH), MXU_DTYPE),
                   jax.ShapeDtypeStruct((N_DEV, GMLP_GROUPS, CHUNK, CHUNK), GRAD_COMM_DTYPE),
                   jax.ShapeDtypeStruct((N_DEV, CHUNK, LANES), F32),
                   jax.ShapeDtypeStruct((N_DEV, N_Q_HEADS, LANES), F32)],
        in_specs=[pl.BlockSpec((CHUNK * per, IN_PROJ_WIDTH), lambda i, d: (rev(i), 0)),
                  pl.BlockSpec((CHUNK, 2 * KV_WIDTH), lambda i, d: (before(i), kv_col)),
                  pl.BlockSpec((CHUNK * per, 3 * LANES), lambda i, d: (rev(i), 0)),
                  pl.BlockSpec((CHUNK, 3 * LANES), lambda i, d: (before(i), 0)),
                  pl.BlockSpec((CHUNK * per, D_MODEL), lambda i, d: (rev(i), 0)),
                  _full((GMLP_GROUPS, CHUNK, CHUNK)), _full((GMLP_GROUPS, CHUNK, CHUNK)),
                  _full((CHUNK, GMLP_WIDTH)), _full((N_Q_HEADS, LANES))],
        out_specs=[pl.BlockSpec((CHUNK * per, IN_PROJ_WIDTH), lambda i, d: (rev(i), 0)),
                   slot((GMLP_GROUPS, CHUNK, CHUNK)), slot((CHUNK, LANES)), slot((N_Q_HEADS, LANES))],
        scratch_shapes=[pltpu.VMEM((CHUNK, 2 * KV_WIDTH), F32), pltpu.VMEM((GMLP_GROUPS, CHUNK, CHUNK), F32)],
        semantics=("arbitrary",),
    )(dev_idx, proj, proj, rope_tab, rope_tab, dcat, w_spatial, w_spatial_t, bias_full, sink_rows)


def _in_proj_bwd_kernel(x, dx1, dproj, vecs, w_in_t, comm=None):
    seq = x.shape[0]
    tm = 512

    def body(x_ref, dx1_ref, dp_ref, v_ref, w_ref, gx_ref, sums_ref):
        @pl.when(pl.program_id(0) == 0)
        def _():
            sums_ref[...] = jnp.zeros_like(sums_ref)

        g_mix, scale1 = v_ref[0:1, :], v_ref[2:3, :]
        dh = _dot(dp_ref[...], w_ref[...])
        xv = x_ref[...]
        rstd = lax.rsqrt(_mean_last(xv * xv) + EPS)
        xh = xv * rstd
        dn1 = dh * (1.0 + scale1)
        dxh = dn1 * g_mix
        gx_ref[...] = dx1_ref[...] + rstd * (dxh - xh * _mean_last(dxh * xh))
        sums_ref[0:1, :] += _rowsum(dh)
        sums_ref[1:2, :] += _rowsum(dh * (xh * g_mix))
        sums_ref[2:3, :] += _rowsum(dn1 * xh)

    tok = lambda w: pl.BlockSpec((tm, w), lambda i: (i, 0))
    return _hosted_call(
        body, comm, name="in_proj_bwd", grid=(seq // tm,),
        out_shape=[jax.ShapeDtypeStruct((seq, D_MODEL), F32), jax.ShapeDtypeStruct((8, D_MODEL), F32)],
        in_specs=[tok(D_MODEL), tok(D_MODEL), tok(IN_PROJ_WIDTH), _full((8, D_MODEL)),
                  _full((IN_PROJ_WIDTH, D_MODEL))],
        out_specs=[tok(D_MODEL), _full((8, D_MODEL))],
        semantics=("arbitrary",),
    )(x, dx1, dproj, vecs, w_in_t)


class _GradTiles(NamedTuple):
    tm: int
    tn: int
    n_tiles: int
    chips_per_tile: int
    a_index: Callable
    b_index: Callable


def _weight_grad_kernel(a, b, c_idx, name, tiles, comm=None, scatter=False):
    seq = a.shape[0]
    tk = min(seq, 4096)
    nk = seq // tk
    tm, tn, n_tiles, per = tiles.tm, tiles.tn, tiles.n_tiles, tiles.chips_per_tile
    rows = tm // per

    def half(phase, c):
        return phase * c[0] + (1 - phase) * (1 - c[0])

    def ids():
        if scatter:
            return pl.program_id(1), pl.program_id(0), pl.program_id(2)
        return pl.program_id(0), pl.program_id(1), pl.program_id(2)

    def body(c_ref, a_ref, b_ref, o_ref, *rest):
        del c_ref
        if scatter:
            landing, acc, stage, landed, send_sems, recv_sems, sent, out_sems, in_sems = rest
        else:
            acc, stage, landed, send_sems, recv_sems = rest
        phase, t, kk = ids()
        x, y, c, _ = _mesh_place()

        def to_owner(tile, q, rel, dev):
            return pltpu.make_async_remote_copy(
                src_ref=sent.at[tile, pl.ds(rows * q, rows)], dst_ref=landing.at[rel], send_sem=out_sems.at[rel],
                recv_sem=in_sems.at[rel], device_id=dev, device_id_type=MESH)

        def copy(tile):
            return pltpu.make_async_remote_copy(
                src_ref=stage.at[tile], dst_ref=landed.at[tile], send_sem=send_sems.at[tile],
                recv_sem=recv_sems.at[tile], device_id=(x, y, 1 - c), device_id_type=MESH)

        @pl.when(kk == 0)
        def _():
            acc[...] = jnp.zeros_like(acc)

        acc[...] += _dot_tn(a_ref[...], b_ref[...])

        @pl.when((kk == nk - 1) & (phase == 0))
        def _():
            stage[t] = acc[...].astype(stage.dtype)
            copy(t).start()

        @pl.when((kk == nk - 1) & (phase == 1))
        def _():
            copy(t).wait_recv()
            total = acc[...] + landed[t].astype(F32)
            for q in range(per):
                o_ref[q] = total[rows * q:rows * (q + 1)].astype(o_ref.dtype)
            if scatter:
                sent[t] = total.astype(sent.dtype)
                for q in range(per):
                    owner = t * per + q
                    px, py = owner // 2, owner % 2
                    other_x, other_y = px != x, py != y
                    rel = jnp.where(other_x & other_y, 2, jnp.where(other_y, 1, 0))

                    @pl.when(other_x | other_y)
                    def _(q=q, rel=rel, px=px, py=py):
                        to_owner(t, q, rel, (px, py, c)).start()

        @pl.when((kk == nk - 1) & (phase == 1) & (t == n_tiles - 1))
        def _():
            for tile in range(n_tiles):
                copy(tile).wait_send()
            if scatter:
                for rel in range(N_CHIPS - 1):
                    to_owner(0, 0, rel, (x, y, c)).wait()

    cs_shape = jax.ShapeDtypeStruct((n_tiles * per, rows, tn), GRAD_COMM_DTYPE)
    scratch = [pltpu.VMEM((tm, tn), F32), pltpu.VMEM((n_tiles, tm, tn), GRAD_COMM_DTYPE),
               pltpu.VMEM((n_tiles, tm, tn), GRAD_COMM_DTYPE),
               pltpu.SemaphoreType.DMA((n_tiles,)), pltpu.SemaphoreType.DMA((n_tiles,))]
    if scatter:
        order = lambda f: (lambda t, p, k, c: f(p, t, k, c))
        grid = (n_tiles, 2, nk)
        out_shape = [cs_shape, jax.ShapeDtypeStruct((N_CHIPS - 1, rows, tn), GRAD_COMM_DTYPE)]
        out_specs = [pl.BlockSpec((per, rows, tn), order(lambda p, t, k, c: (t, 0, 0))), _any()]
        scratch += [pltpu.VMEM((n_tiles, tm, tn), GRAD_COMM_DTYPE),
                    pltpu.SemaphoreType.DMA((N_CHIPS - 1,)), pltpu.SemaphoreType.DMA((N_CHIPS - 1,))]
    else:
        order = lambda f: f
        grid = (2, n_tiles, nk)
        out_shape = [cs_shape]
        out_specs = [pl.BlockSpec((per, rows, tn), lambda p, t, k, c: (p * t, 0, 0))]
    out = _hosted_call(
        body, comm, name=name, grid=grid, n_prefetch=1, out_shape=out_shape,
        in_specs=[pl.BlockSpec((tk, tm), order(lambda p, t, k, c: (k, tiles.a_index(t, half(p, c))))),
                  pl.BlockSpec((tk, tn), order(lambda p, t, k, c: (k, tiles.b_index(t, half(p, c)))))],
        out_specs=out_specs, scratch_shapes=scratch, semantics=("arbitrary", "arbitrary", "arbitrary"),
    )(c_idx, a, b)
    return out[0] if comm is None and not scatter else out


def _row_tile(rows, most=256, sublanes=16):
    return max(t for t in range(sublanes, most + 1, sublanes) if rows % t == 0)


def _adam_update(w, g, m, v):
    m_new = ADAM_B1 * m + (1.0 - ADAM_B1) * g
    v_new = ADAM_B2 * v + (1.0 - ADAM_B2) * (g * g)
    m_hat = m_new / (1.0 - ADAM_B1 ** ADAM_STEP)
    v_hat = v_new / (1.0 - ADAM_B2 ** ADAM_STEP)
    delta = -ADAM_LR * (m_hat / (jnp.sqrt(v_hat) + ADAM_EPS) + ADAM_WD * w)
    return delta, m_new, v_new


def _sum_chips_kernel(own, others, place, name):
    _, r, n = own.shape
    tr = _row_tile(r)

    def body(place_ref, own_ref, oth_ref, o_ref):
        del place_ref
        acc = own_ref[...].astype(F32)
        for k in range(N_CHIPS - 1):
            acc = acc + oth_ref[k].astype(F32)
        o_ref[...] = acc

    return pl.pallas_call(
        body, name=name, out_shape=jax.ShapeDtypeStruct((2, r, n), F32),
        grid_spec=pltpu.PrefetchScalarGridSpec(
            num_scalar_prefetch=1, grid=(r // tr,),
            in_specs=[pl.BlockSpec((None, tr, n), lambda i, p: (p[0], i, 0)),
                      pl.BlockSpec((N_CHIPS - 1, tr, n), lambda i, p: (0, i, 0))],
            out_specs=pl.BlockSpec((None, tr, n), lambda i, p: (p[1], i, 0))),
        compiler_params=_params("parallel"),
    )(place, own, others)


def _adam_kernel(w, g, m, v, name):
    r, n = w.shape
    by_columns = g.shape[1] == r
    tr, tn = _row_tile(g.shape[1], most=512), g.shape[2]

    def body(w_ref, g_ref, m_ref, v_ref, g_out, d_ref, mo_ref, vo_ref):
        gv = g_ref[...]
        g_out[...] = gv
        d_ref[...], mo_ref[...], vo_ref[...] = _adam_update(w_ref[...], gv, m_ref[...], v_ref[...])

    steps = g.shape[1] // tr
    spec = pl.BlockSpec((tr, tn), (lambda h, i: (i, h)) if by_columns else (lambda h, i: (h * steps + i, 0)))
    return pl.pallas_call(
        body, name=name, grid=(2, steps), out_shape=[jax.ShapeDtypeStruct((r, n), F32)] * 4,
        in_specs=[spec, pl.BlockSpec((None, tr, tn), lambda h, i: (h, i, 0)), spec, spec], out_specs=[spec] * 4,
        compiler_params=_params("parallel", "parallel"),
    )(w, g, m, v)


SMALL_PARAMS = ("b_ada", "g_mix", "g_ffn", "g_final", "b_spatial", "sinks", "w_spatial")


def _small_update_kernel(gathered, params):
    shapes = [params[nm][0].shape for nm in SMALL_PARAMS]

    def body(*refs):
        g_refs, refs = refs[:5], refs[5:]
        p_refs, refs = refs[:3 * len(SMALL_PARAMS)], refs[3 * len(SMALL_PARAMS):]
        loss_ref, o_refs = refs[0], refs[1:]

        def total(ref):
            acc = ref[0].astype(F32)
            for k in range(1, N_DEV):
                acc = acc + ref[k].astype(F32)
            return acc

        s1, s2, db, ds, dw = (total(r) for r in g_refs)
        loss_ref[...] = jnp.broadcast_to(s2[6:7, 0:1], loss_ref.shape)
        grads = {"b_ada": [s1[0:1], s1[1:2], s2[5:6], s2[0:1], s2[1:2], s2[2:3]], "g_mix": [s1[2:3]],
                 "g_ffn": [s2[3:4]], "g_final": [s2[4:5]], "b_spatial": [db.T[0:GMLP_GROUPS]],
                 "w_spatial": [dw]}
        lane = lax.broadcasted_iota(jnp.int32, (1, LANES), 1)
        sink_row = jnp.zeros((1, LANES), F32)
        for h in range(N_Q_HEADS):
            sink_row = sink_row + jnp.where(lane == h, ds[h:h + 1, :], 0.0)
        grads["sinks"] = [sink_row[:, 0:N_Q_HEADS]]
        for i, nm in enumerate(SMALL_PARAMS):
            w_ref, m_ref, v_ref = p_refs[3 * i:3 * i + 3]
            outs = o_refs[4 * i:4 * i + 4]
            width = grads[nm][0].shape[1]
            for k, g in enumerate(grads[nm]):
                cols = slice(width * k, width * (k + 1))
                upd = _adam_update(w_ref[:, cols], g, m_ref[:, cols], v_ref[:, cols])
                for o_ref, val in zip(outs, (g,) + upd):
                    o_ref[:, cols] = val

    flat = [a for nm in SMALL_PARAMS for a in params[nm]]
    out_shape = [jax.ShapeDtypeStruct((8, LANES), F32)]
    out_shape += [jax.ShapeDtypeStruct(s, F32) for s in shapes for _ in range(4)]
    outs = pl.pallas_call(
        body, name="small_update", grid=(1,), out_shape=out_shape,
        in_specs=[_full(g.shape) for g in gathered] + [_full(a.shape) for a in flat],
        out_specs=[_full(s.shape) for s in out_shape],
        compiler_params=_params("arbitrary"),
    )(*gathered, *flat)
    return {nm: outs[1 + 4 * i:5 + 4 * i] for i, nm in enumerate(SMALL_PARAMS)}, outs[0]


def _ada_update_kernel(act_t, dmod, w, m, v):
    r, n = w.shape
    tr = 256

    def body(a_ref, d_ref, w_ref, m_ref, v_ref, g_ref, dl_ref, mo_ref, vo_ref):
        g = _dot(a_ref[...], d_ref[...])
        g_ref[...] = g
        dl_ref[...], mo_ref[...], vo_ref[...] = _adam_update(w_ref[...], g, m_ref[...], v_ref[...])

    spec = pl.BlockSpec((tr, n), lambda i: (i, 0))
    return pl.pallas_call(
        body, name="ada_update", grid=(r // tr,), out_shape=[jax.ShapeDtypeStruct((r, n), F32)] * 4,
        in_specs=[pl.BlockSpec((tr, N_DEV), lambda i: (i, 0)), _full((N_DEV, n)), spec, spec, spec],
        out_specs=[spec] * 4, compiler_params=_params("parallel"),
    )(act_t, dmod, w, m, v)


def kernel(x, c, positions, w_ada, b_ada, g_mix, w_in, w_spatial, b_spatial, sinks, w_out, g_ffn, w_ff1, w_ff2, g_final, loss_target, m_w_ada, m_b_ada, m_g_mix, m_w_in, m_w_spatial, m_b_spatial, m_sinks, m_w_out, m_g_ffn, m_w_ff1, m_w_ff2, m_g_final, v_w_ada, v_b_ada, v_g_mix, v_w_in, v_w_spatial, v_b_spatial, v_sinks, v_w_out, v_g_ffn, v_w_ff1, v_w_ff2, v_g_final):
    xi, yi, ci = lax.axis_index("x"), lax.axis_index("y"), lax.axis_index("c")
    chip = 2 * xi + yi
    dev = 2 * chip + ci
    seq = x.shape[1]
    x2, tgt = x[0], loss_target[0]
    ada_cols = w_ada.shape[2]

    big = {"w_in": tuple(a[0].T for a in (w_in, m_w_in, v_w_in)),
           "w_out": (w_out[0], m_w_out[0], v_w_out[0]), "w_ff1": (w_ff1[0], m_w_ff1[0], v_w_ff1[0]),
           "w_ff2": (w_ff2[0], m_w_ff2[0], v_w_ff2[0])}

    def halves(nm):
        r, n = big[nm][0].shape
        return big[nm][0].astype(WEIGHT_COMM_DTYPE).reshape(2, r // 2, n)

    chip_idx = chip.reshape(1).astype(jnp.int32)
    c_all, w_in_t, g_out = _all_gather8([c, halves("w_in"), halves("w_out")], "gather_first",
                                        split=[False, True, True], skip_own=(2,))
    c_all, w_in_t = c_all.reshape(N_DEV, D_MODEL), w_in_t.reshape(IN_PROJ_WIDTH, D_MODEL)
    b_shard = lax.dynamic_slice(b_ada, (0, chip * ada_cols), (1, ada_cols))
    mod_part, act = _mod_kernel(c_all, w_ada[0], b_shard)
    mod_all, = _all_gather8([mod_part], "gather_mod")
    mod_me = lax.dynamic_index_in_dim(mod_all[0::2], dev, axis=1, keepdims=False)
    mod_me = mod_me.reshape(N_MOD, D_MODEL)
    shift1, scale1, gate1, shift2, scale2, gate2 = (mod_me[k:k + 1] for k in range(N_MOD))

    zeros_row = jnp.zeros((1, D_MODEL), F32)
    vecs1 = jnp.concatenate([g_mix, shift1, scale1] + [zeros_row] * 5, axis=0)
    vecs2 = jnp.concatenate([gate1, shift2, scale2, gate2, g_ffn, g_final.reshape(1, D_MODEL)]
                            + [zeros_row] * 2, axis=0)
    bias_full = jnp.repeat(b_spatial[0].T, HEAD_DIM, axis=1)
    sink_rows = jnp.broadcast_to(sinks[0][:, None], (N_Q_HEADS, LANES))
    inv_freq = ROPE_THETA ** (-jnp.arange(0, ROT_DIM, 2, dtype=F32) / ROT_DIM)
    rope_tab = _rope_lane_tables(*_rope_angle_kernel(positions, inv_freq.reshape(ROT_DIM // 2, 1)))

    trunk_weights = ["w_out", "w_ff1", "w_ff2"]
    shards = [halves(nm) for nm in trunk_weights]
    proj, hb, *staged = _in_proj_kernel(x2, vecs1, w_in_t, comm=_gather2d_first(shards[1:]))
    cat, *staged = _mixer_fwd_kernel(proj, rope_tab, w_spatial[0], bias_full, sink_rows,
                                     comm=_gather2d_second(staged, shards[1:]))
    staged = [g_out] + list(_gather_forward(staged, "gather_forward"))
    dx1, dcat, dmix, h2b, rb, dab, dffb, sums2 = _trunk_kernel(
        x2, tgt, cat, vecs2, chip_idx,
        [g.reshape((N_CHIPS,) + big[nm][0].shape) for nm, g in zip(trunk_weights, staged)],
        [s.reshape(big[nm][0].shape) for nm, s in zip(trunk_weights, shards)])

    c_idx = ci.reshape(1).astype(jnp.int32)
    place = jnp.stack([chip, ci]).astype(jnp.int32)
    half_d = D_MODEL // 2
    cs_ff2 = _weight_grad_kernel(rb, dffb, c_idx, "dw_ff2",
                                 _GradTiles(D_MODEL, half_d, N_CHIPS, 1, lambda t, h: t, lambda t, h: h))
    eighth = D_MODEL // 8
    cs_ff1, sc_ff2 = _weight_grad_kernel(
        h2b, dab, c_idx, "dw_ff1",
        _GradTiles(D_MODEL, half_d, N_CHIPS, 1, lambda t, h: 0, lambda t, h: 2 * t + h),
        comm=_scatter_job([cs_ff2], rows=(0, 6 * eighth)))
    cs_out, sc_ff2 = _weight_grad_kernel(
        cat, dmix, c_idx, "dw_out", _GradTiles(D_MODEL, half_d, 1, N_CHIPS, lambda t, h: 0, lambda t, h: h),
        comm=_scatter_job([cs_ff2], rows=(6 * eighth, eighth), into=[sc_ff2]))
    dproj, dw_spatial, db_lanes, dsink_rows, sc_ff2, sc_ff1, sc_out = _mixer_bwd_kernel(
        proj, rope_tab, dcat, w_spatial[0], w_spatial[0].transpose(0, 2, 1), bias_full, sink_rows,
        dev.reshape(1).astype(jnp.int32),
        comm=_merge_jobs(_scatter_job([cs_ff1, cs_out]),
                         _scatter_job([cs_ff2], rows=(7 * eighth, eighth), into=[sc_ff2])))
    totals = [_sum_chips_kernel(own, oth, place, "grad_sum_" + nm)
              for nm, own, oth in (("w_out", cs_out, sc_out), ("w_ff1", cs_ff1, sc_ff1), ("w_ff2", cs_ff2, sc_ff2))]
    small_slots = [db_lanes, dsink_rows, dw_spatial.reshape(N_DEV, GMLP_GROUPS * CHUNK, CHUNK)]
    cs_in, sc_in, *rode = _weight_grad_kernel(
        dproj, hb, c_idx, "dw_in",
        _GradTiles(2 * W_IN_BLOCK, half_d, N_CHIPS // 2, 2, lambda t, h: t, lambda t, h: h),
        comm=_merge_jobs(_gather_job(small_slots), _share_job(totals)), scatter=True)
    small_stage1, shared = rode[:len(small_slots)], rode[len(small_slots):]
    grad_x, sums1 = _in_proj_bwd_kernel(x2, dx1, dproj, vecs1, w_in_t)
    gathered = _all_gather8([sums1, sums2], "gather_small", forward=small_stage1)
    total_in = _sum_chips_kernel(cs_in, sc_in, place, "grad_sum_w_in")
    shared = list(_sibling_share([total_in], "grad_share_w_in")) + list(shared)
    names = ["w_in", "w_out", "w_ff1", "w_ff2"]
    big_out = {}
    for nm, g in zip(names, shared):
        w, m, v = big[nm]
        outs = _adam_kernel(w, g, m, v, "adam_" + nm)
        big_out[nm] = tuple((t.T if nm == "w_in" else t)[None] for t in outs)

    small = {"b_ada": (b_ada, m_b_ada, v_b_ada), "g_mix": (g_mix, m_g_mix, v_g_mix),
             "g_ffn": (g_ffn, m_g_ffn, v_g_ffn), "g_final": (g_final, m_g_final, v_g_final),
             "b_spatial": (b_spatial, m_b_spatial, v_b_spatial), "sinks": (sinks, m_sinks, v_sinks),
             "w_spatial": (w_spatial, m_w_spatial, v_w_spatial)}
    flat_shape = {"g_final": (1, D_MODEL), "b_spatial": (GMLP_GROUPS, CHUNK), "w_spatial": (GMLP_GROUPS * CHUNK, CHUNK)}
    small_out, loss_tile = _small_update_kernel(
        gathered, {nm: tuple(a.reshape(flat_shape.get(nm, a.shape)) for a in small[nm]) for nm in small})
    small_out = {nm: [o.reshape(small[nm][0].shape) for o in small_out[nm]] for nm in small}
    loss = loss_tile[0, 0]

    g1, g2 = gathered[0], gathered[1]
    dmod_all = jnp.concatenate([g1[:, 0], g1[:, 1], g2[:, 5], g2[:, 0], g2[:, 1], g2[:, 2]], axis=1)
    dmod_cols = lax.dynamic_slice(dmod_all, (0, chip * ada_cols), (N_DEV, ada_cols))
    ada = _ada_update_kernel(act.T, dmod_cols, w_ada[0], m_w_ada[0], v_w_ada[0])
    big_out["w_ada"] = tuple(t[None] for t in ada)

    order = ["w_ada", "b_ada", "g_mix", "w_in", "w_spatial", "b_spatial", "sinks", "w_out", "g_ffn",
             "w_ff1", "w_ff2", "g_final"]

    def leaf(nm, k):
        return big_out[nm][k] if nm in big_out else small_out[nm][k]

    outs = [loss, grad_x[None]]
    for k in range(4):
        outs += [leaf(nm, k) for nm in order]
    return tuple(outs)
```

```python
import math
from typing import Callable, NamedTuple

import jax
import jax.numpy as jnp
from jax import lax
from jax.experimental import pallas as pl
from jax.experimental.pallas import tpu as pltpu

F32 = jnp.float32
MXU_DTYPE = jnp.bfloat16
WEIGHT_COMM_DTYPE = jnp.bfloat16
GRAD_COMM_DTYPE = jnp.bfloat16

D_MODEL = 1024
D_FF = 4096
HEAD_DIM = 64
GMLP_GROUPS = 8
GMLP_WIDTH = 512
CHUNK = 128
N_Q_HEADS = 8
N_KV_HEADS = 2
ATTN_WIDTH = 512
KV_WIDTH = 128
ROT_DIM = 16
ROPE_THETA = 500000.0
IN_PROJ_WIDTH = 1792
N_MOD = 6
EPS = 1e-5
N_CHIPS = 4
N_DEV = 8
LANES = 128
W_IN_BLOCK = IN_PROJ_WIDTH // N_CHIPS

ADAM_LR = 0.001
ADAM_B1 = 0.9
ADAM_B2 = 0.999
ADAM_EPS = 1e-08
ADAM_WD = 0.01
ADAM_STEP = 10

VMEM_LIMIT_BYTES = 58 * 1024 * 1024
MESH = pl.DeviceIdType.MESH


def _params(*semantics):
    return pltpu.CompilerParams(dimension_semantics=semantics, vmem_limit_bytes=VMEM_LIMIT_BYTES)


def _dot(a, b):
    return jnp.dot(a.astype(MXU_DTYPE), b.astype(MXU_DTYPE), preferred_element_type=F32)


def _dot_nt(a, b):
    return lax.dot_general(a.astype(MXU_DTYPE), b.astype(MXU_DTYPE), (((1,), (1,)), ((), ())),
                           preferred_element_type=F32)


def _dot_tn(a, b):
    return lax.dot_general(a.astype(MXU_DTYPE), b.astype(MXU_DTYPE), (((0,), (0,)), ((), ())),
                           preferred_element_type=F32)


def _full(shape):
    return pl.BlockSpec(shape, lambda *_: (0,) * len(shape))


def _any():
    return pl.BlockSpec(memory_space=pl.ANY)


def _rowsum(v):
    return jnp.sum(v, axis=0, keepdims=True)


def _mean_last(v):
    return jnp.mean(v, axis=-1, keepdims=True)


class _Comm(NamedTuple):
    operands: tuple
    out_shapes: tuple
    n_sems: int
    make: Callable
    in_place: int = 0


def _hosted_call(body, comm, *, name, grid, in_specs, out_shape, out_specs, scratch_shapes=(), semantics,
                 n_prefetch=0):
    if comm is None:
        return pl.pallas_call(
            body, name=name, out_shape=out_shape, compiler_params=_params(*semantics),
            grid_spec=pltpu.PrefetchScalarGridSpec(
                num_scalar_prefetch=n_prefetch, grid=grid, in_specs=in_specs, out_specs=out_specs,
                scratch_shapes=list(scratch_shapes)))
    n_in, n_out, n_scr = len(in_specs), len(out_shape), len(scratch_shapes)
    k_in, k_out = len(comm.operands), len(comm.out_shapes)

    def hosted(*refs):
        prefetched, refs = refs[:n_prefetch], refs[n_prefetch:]
        ins, refs = refs[:n_in], refs[n_in:]
        c_ins, refs = refs[:k_in], refs[k_in:]
        outs, refs = refs[:n_out], refs[n_out:]
        c_outs, refs = refs[:k_out], refs[k_out:]
        scratch, (send_sems, recv_sems) = refs[:n_scr], refs[n_scr:]
        first, last = None, None
        for d, size in enumerate(grid):
            at_start, at_end = pl.program_id(d) == 0, pl.program_id(d) == size - 1
            first = at_start if first is None else first & at_start
            last = at_end if last is None else last & at_end

        @pl.when(first)
        def _():
            for cp in comm.make(c_ins, c_outs, send_sems, recv_sems)[0]:
                cp.start()

        body(*prefetched, *ins, *outs, *scratch)

        @pl.when(last)
        def _():
            for wait in comm.make(c_ins, c_outs, send_sems, recv_sems)[1]:
                wait()

    aliases = {n_prefetch + n_in + i: n_out + i for i in range(comm.in_place)}
    call = pl.pallas_call(
        hosted, name=name, out_shape=list(out_shape) + list(comm.out_shapes),
        compiler_params=_params(*semantics), input_output_aliases=aliases,
        grid_spec=pltpu.PrefetchScalarGridSpec(
            num_scalar_prefetch=n_prefetch, grid=grid, in_specs=list(in_specs) + [_any()] * k_in,
            out_specs=list(out_specs) + [_any()] * k_out,
            scratch_shapes=list(scratch_shapes) + [pltpu.SemaphoreType.DMA((comm.n_sems,)),
                                                    pltpu.SemaphoreType.DMA((comm.n_sems,))]))
    return lambda *args: call(*args, *comm.operands)


class _Shifted:
    def __init__(self, base, offset):
        self.base, self.offset = base, offset

    @property
    def at(self):
        return self

    def __getitem__(self, k):
        return self.base.at[self.offset + k]


def _merge_jobs(*jobs):
    def order(count):
        first = [(j, i) for j, job in enumerate(jobs) for i in range(job.in_place)]
        return first + [(j, i) for j, job in enumerate(jobs) for i in range(job.in_place, count(job))]

    op_order, out_order = order(lambda job: len(job.operands)), order(lambda job: len(job.out_shapes))

    def make(ins, outs, send_sems, recv_sems):
        starts, waits, sem = [], [], 0
        for j, job in enumerate(jobs):
            mine_in = [ins[k] for k, (jj, _) in enumerate(op_order) if jj == j]
            mine_out = [outs[k] for k, (jj, _) in enumerate(out_order) if jj == j]
            s, w = job.make(mine_in, mine_out, _Shifted(send_sems, sem), _Shifted(recv_sems, sem))
            starts, waits, sem = starts + s, waits + w, sem + job.n_sems
        return starts, waits

    return _Comm(tuple(jobs[j].operands[i] for j, i in op_order), tuple(jobs[j].out_shapes[i] for j, i in out_order),
                 sum(job.n_sems for job in jobs), make, in_place=sum(job.in_place for job in jobs))


def _mesh_place():
    x, y, c = lax.axis_index("x"), lax.axis_index("y"), lax.axis_index("c")
    return x, y, c, [(1 - x, y), (x, 1 - y), (1 - x, 1 - y)]


def _gather_job(bufs):
    per = 4

    def make(ins, outs, send_sems, recv_sems):
        del ins
        x, y, c, chips = _mesh_place()
        starts, waits = [], []
        for a, out in enumerate(outs):
            mine = src = out.at[4 * x + 2 * y + c]
            to = [(x, y, 1 - c)] + [(px, py, c) for px, py in chips]
            sends = [pltpu.make_async_remote_copy(
                src_ref=src, dst_ref=mine, send_sem=send_sems.at[per * a + k],
                recv_sem=recv_sems.at[per * a + k], device_id=dev, device_id_type=MESH)
                for k, dev in enumerate(to)]
            recvs = [pltpu.make_async_remote_copy(
                src_ref=src, dst_ref=out.at[4 * px + 2 * py + pc], send_sem=send_sems.at[per * a + k],
                recv_sem=recv_sems.at[per * a + k], device_id=(px, py, pc), device_id_type=MESH)
                for k, (px, py, pc) in enumerate(to)]
            starts += sends
            waits += [s.wait_send for s in sends] + [r.wait_recv for r in recvs]
        return starts, waits

    shapes = tuple(jax.ShapeDtypeStruct(b.shape, b.dtype) for b in bufs)
    return _Comm(tuple(bufs), shapes, per * len(bufs), make, in_place=len(bufs))


def _slots(x, y, c):
    return 4 * x + 2 * y + c, 4 * (1 - x) + 2 * y + c, 4 * x + 2 * (1 - y) + c, 4 * (1 - x) + 2 * (1 - y) + c


def _gather2d_first(halves):
    per = 2

    def make(ins, outs, send_sems, recv_sems):
        x, y, c, _ = _mesh_place()
        me, xn, yn, _ = _slots(x, y, c)
        starts, waits = [], []
        for a, (src, out) in enumerate(zip(ins, outs)):
            blk = src.at[c]
            rows = blk.shape[0] // 2
            upper, lower = pl.ds(0, rows), pl.ds(rows, rows)

            def copy(k, src_ref, dst_ref, dev, a=a):
                return pltpu.make_async_remote_copy(
                    src_ref=src_ref, dst_ref=dst_ref, send_sem=send_sems.at[per * a + k],
                    recv_sem=recv_sems.at[per * a + k], device_id=dev, device_id_type=MESH)

            sends = [copy(0, blk.at[upper], out.at[me, upper], (1 - x, y, c)),
                     copy(1, blk.at[lower], out.at[me, lower], (x, 1 - y, c))]
            recvs = [copy(0, blk.at[upper], out.at[xn, upper], (1 - x, y, c)),
                     copy(1, blk.at[lower], out.at[yn, lower], (x, 1 - y, c))]
            starts += sends
            waits += [s.wait_send for s in sends] + [r.wait_recv for r in recvs]
        return starts, waits

    shapes = tuple(jax.ShapeDtypeStruct((N_DEV,) + h.shape[1:], h.dtype) for h in halves)
    return _Comm(tuple(halves), shapes, per * len(halves), make)


def _gather2d_second(bufs, halves):
    per = 4
    n_arr = len(bufs)

    def make(ins, outs, send_sems, recv_sems):
        x, y, c, _ = _mesh_place()
        me, xn, yn, dg = _slots(x, y, c)
        starts, waits = [], []
        for a, buf in enumerate(outs):
            own = ins[n_arr + a].at[c]
            rows = buf.shape[1] // 2
            upper, lower = pl.ds(0, rows), pl.ds(rows, rows)
            plan = [(own.at[upper], me, upper, (x, 1 - y, c), yn), (buf.at[xn, upper], xn, upper, (x, 1 - y, c), dg),
                    (own.at[lower], me, lower, (1 - x, y, c), xn), (buf.at[yn, lower], yn, lower, (1 - x, y, c), dg)]
            for k, (src, slot, part, dev, landing) in enumerate(plan):
                sems = dict(send_sem=send_sems.at[per * a + k], recv_sem=recv_sems.at[per * a + k],
                            device_id=dev, device_id_type=MESH)
                send = pltpu.make_async_remote_copy(src_ref=src, dst_ref=buf.at[slot, part], **sems)
                arrival = pltpu.make_async_remote_copy(src_ref=src, dst_ref=buf.at[landing, part], **sems)
                starts.append(send)
                waits += [send.wait_send, arrival.wait_recv]
        return starts, waits

    shapes = tuple(jax.ShapeDtypeStruct(b.shape, b.dtype) for b in bufs)
    return _Comm(tuple(bufs) + tuple(halves), shapes, per * n_arr, make, in_place=n_arr)


def _gather_forward(bufs, name):
    n_arr = len(bufs)

    def body(*refs):
        outs = refs[n_arr:2 * n_arr]
        send_sems, recv_sems = refs[2 * n_arr:]
        x, y, c, chips = _mesh_place()
        sends, recvs = [], []
        for a, buf in enumerate(outs):
            for j, (px, py) in enumerate(chips):
                mine, theirs = buf.at[4 * px + 2 * py + c], buf.at[4 * px + 2 * py + 1 - c]
                sems = dict(send_sem=send_sems.at[3 * a + j], recv_sem=recv_sems.at[3 * a + j],
                            device_id=(x, y, 1 - c), device_id_type=MESH)
                sends.append(pltpu.make_async_remote_copy(src_ref=mine, dst_ref=mine, **sems))
                recvs.append(pltpu.make_async_remote_copy(src_ref=mine, dst_ref=theirs, **sems))
        for cp in sends:
            cp.start()
        for s, r in zip(sends, recvs):
            s.wait_send()
            r.wait_recv()

    return pl.pallas_call(
        body, name=name, out_shape=[jax.ShapeDtypeStruct(b.shape, b.dtype) for b in bufs],
        in_specs=[_any()] * n_arr, out_specs=[_any()] * n_arr,
        input_output_aliases={a: a for a in range(n_arr)},
        scratch_shapes=[pltpu.SemaphoreType.DMA((3 * n_arr,)), pltpu.SemaphoreType.DMA((3 * n_arr,))],
    )(*bufs)


def _scatter_job(chip_sums, rows=None, into=()):
    n_into = len(into)

    def part(ref):
        return ref if rows is None else ref.at[pl.ds(rows[0], rows[1])]

    def make(ins, outs, send_sems, recv_sems):
        x, y, c, chips = _mesh_place()
        copies = [pltpu.make_async_remote_copy(
            src_ref=part(src.at[2 * px + py]), dst_ref=part(out.at[j]), send_sem=send_sems.at[3 * a + j],
            recv_sem=recv_sems.at[3 * a + j], device_id=(px, py, c), device_id_type=MESH)
            for a, (src, out) in enumerate(zip(ins[n_into:], outs)) for j, (px, py) in enumerate(chips)]
        return copies, [cp.wait for cp in copies]

    shapes = tuple(jax.ShapeDtypeStruct((3,) + s.shape[1:], s.dtype) for s in chip_sums)
    return _Comm(tuple(into) + tuple(chip_sums), shapes, 3 * len(chip_sums), make, in_place=n_into)


def _all_gather8(blocks, name, split=False, forward=(), riders=(), skip_own=()):
    n_arr, n_fwd = len(blocks), len(forward)
    splits = list(split) if isinstance(split, (list, tuple)) else [split] * n_arr
    own_slots = [a not in skip_own for a in range(n_arr)]
    rider_in = sum(len(r.operands) for r in riders)
    rider_out = sum(len(r.out_shapes) for r in riders)

    def body(*refs):
        x_refs, refs = refs[:n_arr], refs[n_arr + n_fwd:]
        r_ins, refs = refs[:rider_in], refs[rider_in:]
        out_refs, refs = refs[:n_arr], refs[n_arr:]
        fwd_refs, refs = refs[:n_fwd], refs[n_fwd:]
        r_outs, refs = refs[:rider_out], refs[rider_out:]
        (send_sems, recv_sems, local_sems), rider_sems = refs[:3], refs[3:]
        x, y, c, chips = _mesh_place()
        me, sibling = (x, y, c), (x, y, 1 - c)
        passing = []
        for f, buf in enumerate(fwd_refs):
            for j, (px, py) in enumerate(chips):
                mine, theirs = buf.at[4 * px + 2 * py + c], buf.at[4 * px + 2 * py + 1 - c]
                sems = dict(send_sem=send_sems.at[7 * n_arr + 3 * f + j], recv_sem=recv_sems.at[7 * n_arr + 3 * f + j],
                            device_id=sibling, device_id_type=MESH)
                passing.append((pltpu.make_async_remote_copy(src_ref=mine, dst_ref=mine, **sems),
                                pltpu.make_async_remote_copy(src_ref=mine, dst_ref=theirs, **sems)))
        for send, _ in passing:
            send.start()
        arrays = []
        for a, (x_ref, out_ref) in enumerate(zip(x_refs, out_refs)):
            src_mine = x_ref.at[c] if splits[a] else x_ref

            def copy(k, blk, to, src=None, a=a, out_ref=out_ref):
                dst = out_ref.at[4 * blk[0] + 2 * blk[1] + blk[2]]
                return pltpu.make_async_remote_copy(
                    src_ref=dst if src is None else src, dst_ref=dst,
                    send_sem=send_sems.at[7 * a + k], recv_sem=recv_sems.at[7 * a + k],
                    device_id=to, device_id_type=MESH)

            mine = pltpu.make_async_copy(src_mine, out_ref.at[4 * x + 2 * y + c], local_sems.at[a])
            first = [copy(0, me, sibling, src=src_mine)] if own_slots[a] else []
            first += [copy(1 + j, me, (*chip, c), src=src_mine) for j, chip in enumerate(chips)]
            for cp in first + ([mine] if own_slots[a] else []):
                cp.start()
            arrays.append((copy, mine, first, own_slots[a]))
        rider_waits, i0, o0 = [], 0, 0
        for n, job in enumerate(riders):
            k_in, k_out = len(job.operands), len(job.out_shapes)
            starts, waits = job.make(r_ins[i0:i0 + k_in], r_outs[o0:o0 + k_out],
                                     rider_sems[2 * n], rider_sems[2 * n + 1])
            for cp in starts:
                cp.start()
            rider_waits += waits
            i0, o0 = i0 + k_in, o0 + k_out
        sent = []
        for copy, mine, first, own in arrays:
            passed = [copy(4 + j, (*chip, c), sibling) for j, chip in enumerate(chips)]
            for j, chip in enumerate(chips):
                copy(1 + j, (*chip, c), me).wait_recv()
                passed[j].start()
            sent += first + passed
        for copy, mine, first, own in arrays:
            if own:
                copy(0, sibling, me).wait_recv()
                mine.wait()
            for j, chip in enumerate(chips):
                copy(4 + j, (*chip, 1 - c), me).wait_recv()
        for cp in sent:
            cp.wait_send()
        for send, arrival in passing:
            send.wait_send()
            arrival.wait_recv()
        for wait in rider_waits:
            wait()

    n_sems = 7 * n_arr + 3 * n_fwd
    rider_operands = [a for r in riders for a in r.operands]
    rider_shapes = [s for r in riders for s in r.out_shapes]
    return pl.pallas_call(
        body, name=name,
        out_shape=[jax.ShapeDtypeStruct((N_DEV,) + tuple(b.shape[1:] if s else b.shape), b.dtype)
                   for b, s in zip(blocks, splits)]
        + [jax.ShapeDtypeStruct(f.shape, f.dtype) for f in forward] + rider_shapes,
        in_specs=[_any()] * (n_arr + n_fwd + rider_in), out_specs=[_any()] * (n_arr + n_fwd + rider_out),
        input_output_aliases={n_arr + f: n_arr + f for f in range(n_fwd)},
        scratch_shapes=[pltpu.SemaphoreType.DMA((n_sems,)), pltpu.SemaphoreType.DMA((n_sems,)),
                        pltpu.SemaphoreType.DMA((n_arr,))]
        + [pltpu.SemaphoreType.DMA((r.n_sems,)) for r in riders for _ in range(2)],
    )(*blocks, *forward, *rider_operands)


def _share_job(bufs):
    def make(ins, outs, send_sems, recv_sems):
        del ins
        x, y, c, _ = _mesh_place()
        sems = lambda a: dict(send_sem=send_sems.at[a], recv_sem=recv_sems.at[a],
                              device_id=(x, y, 1 - c), device_id_type=MESH)
        sends = [pltpu.make_async_remote_copy(src_ref=o.at[c], dst_ref=o.at[c], **sems(a)) for a, o in enumerate(outs)]
        arrivals = [pltpu.make_async_remote_copy(src_ref=o.at[c], dst_ref=o.at[1 - c], **sems(a))
                    for a, o in enumerate(outs)]
        return sends, [s.wait_send for s in sends] + [r.wait_recv for r in arrivals]

    shapes = tuple(jax.ShapeDtypeStruct(b.shape, b.dtype) for b in bufs)
    return _Comm(tuple(bufs), shapes, len(bufs), make, in_place=len(bufs))


def _sibling_share(bufs, name):
    n_arr = len(bufs)

    def body(*refs):
        out_refs = refs[n_arr:2 * n_arr]
        send_sems, recv_sems = refs[2 * n_arr:]
        x, y, c = lax.axis_index("x"), lax.axis_index("y"), lax.axis_index("c")
        copies = [pltpu.make_async_remote_copy(
            src_ref=out_refs[a].at[c], dst_ref=out_refs[a].at[c],
            send_sem=send_sems.at[a], recv_sem=recv_sems.at[a],
            device_id=(x, y, 1 - c), device_id_type=MESH) for a in range(n_arr)]
        for cp in copies:
            cp.start()
        for a in range(n_arr):
            pltpu.make_async_remote_copy(
                src_ref=out_refs[a].at[c], dst_ref=out_refs[a].at[1 - c],
                send_sem=send_sems.at[a], recv_sem=recv_sems.at[a],
                device_id=(x, y, 1 - c), device_id_type=MESH).wait()

    return pl.pallas_call(
        body, name=name,
        out_shape=[jax.ShapeDtypeStruct(b.shape, b.dtype) for b in bufs],
        in_specs=[_any()] * n_arr, out_specs=[_any()] * n_arr,
        input_output_aliases={a: a for a in range(n_arr)},
        scratch_shapes=[pltpu.SemaphoreType.DMA((n_arr,)), pltpu.SemaphoreType.DMA((n_arr,))],
    )(*bufs)


def _gelu_tanh(z):
    k = math.sqrt(2.0 / math.pi)
    t = jnp.tanh(k * (z + 0.044715 * (z * z * z)))
    return 0.5 * z * (1.0 + t), t


def _gelu_tanh_grad(z, t):
    k = math.sqrt(2.0 / math.pi)
    return 0.5 * (1.0 + t) + 0.5 * z * (1.0 - t * t) * (k * (1.0 + 3.0 * 0.044715 * (z * z)))


def _rope_angle_kernel(pos_row, invf_col):
    seq = pos_row.shape[1]

    def body(p_ref, f_ref, cos_ref, sin_ref):
        ang = p_ref[...].astype(F32) * f_ref[...]
        cos_ref[...] = jnp.cos(ang)
        sin_ref[...] = jnp.sin(ang)

    return pl.pallas_call(
        body, name="rope_angles", grid=(1,), out_shape=[jax.ShapeDtypeStruct((ROT_DIM // 2, seq), F32)] * 2,
        in_specs=[_full((1, seq)), _full((ROT_DIM // 2, 1))], out_specs=[_full((ROT_DIM // 2, seq))] * 2,
        compiler_params=_params("arbitrary"),
    )(pos_row, invf_col)


def _rope_lane_tables(cos, sin):
    cos_t, sin_t = cos.T, sin.T
    seq, half = cos_t.shape
    ones = jnp.ones((seq, HEAD_DIM - ROT_DIM), F32)
    c64 = jnp.concatenate([cos_t, cos_t, ones], axis=1)
    s1 = jnp.concatenate([sin_t, jnp.zeros((seq, HEAD_DIM - half), F32)], axis=1)
    s2 = jnp.concatenate([jnp.zeros((seq, half), F32), sin_t, jnp.zeros((seq, HEAD_DIM - ROT_DIM), F32)], axis=1)
    return jnp.concatenate([jnp.tile(t, (1, LANES // HEAD_DIM)) for t in (c64, s1, s2)], axis=1)


def _rope_apply(t, tab, sign):
    reps = t.shape[1] // LANES
    c_tab, s1, s2 = (jnp.tile(tab[:, LANES * k:LANES * (k + 1)], (1, reps)) if reps > 1
                     else tab[:, LANES * k:LANES * (k + 1)] for k in range(3))
    half = ROT_DIM // 2
    up = pltpu.roll(t, t.shape[1] - half, 1)
    down = pltpu.roll(t, half, 1)
    return t * c_tab + sign * (down * s2 - up * s1)


def _lane_masks(shape):
    lane = lax.broadcasted_iota(jnp.int32, shape, 1)
    return lane < HEAD_DIM, lane >= HEAD_DIM


HEADS_PER_GROUP = N_Q_HEADS // N_KV_HEADS
ATTN_SCALE = 1.0 / math.sqrt(HEAD_DIM)


def _attn_bias_t(first_block):
    kj = lax.broadcasted_iota(jnp.int32, (2 * CHUNK, CHUNK), 0)
    qi = lax.broadcasted_iota(jnp.int32, (2 * CHUNK, CHUNK), 1)
    ok = (kj > qi) & (kj <= qi + CHUNK)
    if first_block is not None:
        ok = ok & (jnp.logical_not(first_block) | (kj >= CHUNK))
    return jnp.tile(jnp.where(ok, 0.0, -jnp.inf), (1, HEADS_PER_GROUP))


def _group_rows(x, g, lo, hi):
    rows = []
    for r in range(HEADS_PER_GROUP):
        h = HEADS_PER_GROUP * g + r
        pair = x[:, LANES * (h // 2):LANES * (h // 2 + 1)]
        rows.append(jnp.where(hi if h % 2 else lo, pair, 0.0))
    return jnp.concatenate(rows, axis=0)


def _pairs_from_rows(rows, lo):
    return [jnp.where(lo, rows[2 * CHUNK * k:2 * CHUNK * k + CHUNK], rows[2 * CHUNK * k + CHUNK:2 * CHUNK * (k + 1)])
            for k in range(HEADS_PER_GROUP // 2)]


def _group_dup(a, b, g, lo2):
    return jnp.where(lo2, a, b) if g == 0 else jnp.where(lo2, b, a)


def _sink_row(sink_ref, g):
    return jnp.concatenate([sink_ref[HEADS_PER_GROUP * g + r:HEADS_PER_GROUP * g + r + 1, :]
                            for r in range(HEADS_PER_GROUP)], axis=1)


def _attn_probs_t(k_dup, q_rows, bias_t, sink_row):
    s_t = _dot_nt(k_dup, q_rows) * ATTN_SCALE + bias_t
    m = jnp.maximum(jnp.max(s_t, axis=0, keepdims=True), sink_row)
    p = jnp.exp(s_t - m)
    e_sink = jnp.exp(sink_row - m)
    inv = 1.0 / (jnp.sum(p, axis=0, keepdims=True) + e_sink)
    return p * inv, e_sink * inv


def _sgu_forward_pair(wm, vp, j):
    lo, hi = _lane_masks(vp.shape)
    lhs = jnp.concatenate([wm[2 * j], wm[2 * j + 1]], axis=1)
    rhs = jnp.concatenate([jnp.where(lo, vp, 0.0), jnp.where(hi, vp, 0.0)], axis=0)
    return _dot(lhs, rhs)


def _masked_spatial(w_ref):
    t = lax.broadcasted_iota(jnp.int32, (CHUNK, CHUNK), 0)
    s = lax.broadcasted_iota(jnp.int32, (CHUNK, CHUNK), 1)
    tril = s <= t
    return [jnp.where(tril, w_ref[g], 0.0) for g in range(GMLP_GROUPS)], tril, s >= t


def _mod_kernel(c_all, w_shard, b_shard, comm=None):
    n = w_shard.shape[1]
    tn = 512

    def body(c_ref, w_ref, b_ref, mod_ref, act_ref):
        cv = c_ref[...]
        act = cv * (1.0 / (1.0 + jnp.exp(-cv)))
        act_ref[...] = act
        mod_ref[...] = _dot(act, w_ref[...]) + b_ref[...]

    return _hosted_call(
        body, comm, name="ada_mod", grid=(n // tn,),
        out_shape=[jax.ShapeDtypeStruct((N_DEV, n), F32), jax.ShapeDtypeStruct((N_DEV, D_MODEL), F32)],
        in_specs=[_full((N_DEV, D_MODEL)), pl.BlockSpec((D_MODEL, tn), lambda i: (0, i)),
                  pl.BlockSpec((1, tn), lambda i: (0, i))],
        out_specs=[pl.BlockSpec((N_DEV, tn), lambda i: (0, i)), _full((N_DEV, D_MODEL))],
        semantics=("arbitrary",),
    )(c_all, w_shard, b_shard)


def _load_chip_blocks(chip_ref, gathered, local, dsts, sems, first_sem=0):
    for k, dst in enumerate(dsts):
        @pl.when(chip_ref[0] == k)
        def _():
            pltpu.make_async_copy(local, dst, sems.at[first_sem + k]).start()

        @pl.when(chip_ref[0] != k)
        def _():
            pltpu.make_async_copy(gathered.at[k], dst, sems.at[first_sem + k]).start()
    return [pltpu.make_async_copy(local, dst, sems.at[first_sem + k]).wait for k, dst in enumerate(dsts)]


def _in_proj_kernel(x, vecs, w_in_t, comm=None):
    seq = x.shape[0]
    tm = 512

    def body(x_ref, v_ref, w_ref, proj_ref, h_ref):
        xv = x_ref[...]
        rstd = lax.rsqrt(_mean_last(xv * xv) + EPS)
        n1 = (xv * rstd) * v_ref[0:1, :]
        h = n1 * (1.0 + v_ref[2:3, :]) + v_ref[1:2, :]
        hb = h.astype(MXU_DTYPE)
        h_ref[...] = hb
        proj_ref[...] = _dot_nt(hb, w_ref[...])

    return _hosted_call(
        body, comm, name="in_proj", grid=(seq // tm,),
        out_shape=[jax.ShapeDtypeStruct((seq, IN_PROJ_WIDTH), F32),
                   jax.ShapeDtypeStruct((seq, D_MODEL), MXU_DTYPE)],
        in_specs=[pl.BlockSpec((tm, D_MODEL), lambda i: (i, 0)), _full((8, D_MODEL)),
                  _full((IN_PROJ_WIDTH, D_MODEL))],
        out_specs=[pl.BlockSpec((tm, IN_PROJ_WIDTH), lambda i: (i, 0)),
                   pl.BlockSpec((tm, D_MODEL), lambda i: (i, 0))],
        semantics=("arbitrary",),
    )(x, vecs, w_in_t)


MIXER_BLOCKS_PER_STEP = 4
KV_START = 2 * GMLP_WIDTH + ATTN_WIDTH


def _mixer_fwd_kernel(proj, rope_tab, w_spatial, bias_full, sink_rows, comm=None):
    seq = proj.shape[0]
    per = MIXER_BLOCKS_PER_STEP
    steps = seq // (CHUNK * per)
    kv_col = KV_START // (2 * KV_WIDTH)

    def body(proj_ref, prev_ref, tab_ref, ptab_ref, w_ref, bias_ref, sink_ref, cat_ref):
        i = pl.program_id(0)
        wm, _, _ = _masked_spatial(w_ref)
        lo, hi = _lane_masks((CHUNK, LANES))
        lo2, _ = _lane_masks((2 * CHUNK, LANES))
        o = 2 * GMLP_WIDTH
        for s in range(per):
            rows, before = slice(CHUNK * s, CHUNK * (s + 1)), slice(CHUNK * (s - 1), CHUNK * s)
            for j in range(GMLP_GROUPS // 2):
                cols = slice(LANES * j, LANES * (j + 1))
                vcols = slice(GMLP_WIDTH + LANES * j, GMLP_WIDTH + LANES * (j + 1))
                u, _ = _gelu_tanh(proj_ref[rows, cols])
                vp, _ = _gelu_tanh(proj_ref[rows, vcols])
                sv = _sgu_forward_pair(wm, vp, j) + bias_ref[:, cols]
                cat_ref[rows, cols] = (u * sv).astype(cat_ref.dtype)
            tab = tab_ref[rows, :]
            if s == 0:
                prev_kv, prev_tab, first = prev_ref[...], ptab_ref[...], i == 0
            else:
                prev_kv, prev_tab, first = proj_ref[before, KV_START:KV_START + 2 * KV_WIDTH], tab_ref[before, :], None
            q_r = _rope_apply(proj_ref[rows, o:o + ATTN_WIDTH], tab, 1.0)
            k_cur = _rope_apply(proj_ref[rows, KV_START:KV_START + KV_WIDTH], tab, 1.0)
            k_prev = _rope_apply(prev_kv[:, 0:KV_WIDTH], prev_tab, 1.0)
            k_a = jnp.concatenate([k_prev, k_cur], axis=0)
            v_a = jnp.concatenate([prev_kv[:, KV_WIDTH:2 * KV_WIDTH],
                                   proj_ref[rows, KV_START + KV_WIDTH:KV_START + 2 * KV_WIDTH]], axis=0)
            k_b = pltpu.roll(k_a, HEAD_DIM, 1)
            v_b = pltpu.roll(v_a, HEAD_DIM, 1)
            bias_t = _attn_bias_t(first)
            for g in range(N_KV_HEADS):
                p_t, _ = _attn_probs_t(_group_dup(k_a, k_b, g, lo2), _group_rows(q_r, g, lo, hi), bias_t,
                                       _sink_row(sink_ref, g))
                o_t = _dot(_group_dup(v_a, v_b, g, lo2).T, p_t)
                for k, pair in enumerate(_pairs_from_rows(o_t.T, lo)):
                    c0 = GMLP_WIDTH + LANES * (2 * g + k)
                    cat_ref[rows, c0:c0 + LANES] = pair.astype(cat_ref.dtype)

    return _hosted_call(
        body, comm, name="mixer_fwd", grid=(steps,),
        out_shape=[jax.ShapeDtypeStruct((seq, D_MODEL), MXU_DTYPE)],
        in_specs=[pl.BlockSpec((CHUNK * per, IN_PROJ_WIDTH), lambda i: (i, 0)),
                  pl.BlockSpec((CHUNK, 2 * KV_WIDTH), lambda i: (jnp.maximum(per * i - 1, 0), kv_col)),
                  pl.BlockSpec((CHUNK * per, 3 * LANES), lambda i: (i, 0)),
                  pl.BlockSpec((CHUNK, 3 * LANES), lambda i: (jnp.maximum(per * i - 1, 0), 0)),
                  _full((GMLP_GROUPS, CHUNK, CHUNK)), _full((CHUNK, GMLP_WIDTH)),
                  _full((N_Q_HEADS, LANES))],
        out_specs=[pl.BlockSpec((CHUNK * per, D_MODEL), lambda i: (i, 0))],
        semantics=("arbitrary",),
    )(proj, proj, rope_tab, rope_tab, w_spatial, bias_full, sink_rows)


def _trunk_kernel(x, target, cat, vecs, chip_idx, gathered, local):
    seq = x.shape[0]
    tm = 256
    nj = D_FF // D_MODEL
    out_rows = D_MODEL // N_CHIPS

    def body(chip_ref, x_ref, t_ref, cat_ref, v_ref, g_out, g_w1, g_w2, l_out, l_w1, l_w2,
             dx1_ref, dcat_ref, dmix_ref, h2_ref, r_ref, da_ref, dff_ref, sums_ref,
             wout, w1, w2, a_scr, sem):
        i = pl.program_id(0)

        @pl.when(i == 0)
        def _():
            waits = _load_chip_blocks(chip_ref, g_out, l_out,
                                      [wout.at[pl.ds(out_rows * k, out_rows)] for k in range(N_CHIPS)], sem)
            waits += _load_chip_blocks(chip_ref, g_w1, l_w1, [w1.at[k] for k in range(N_CHIPS)], sem, N_CHIPS)
            waits += _load_chip_blocks(chip_ref, g_w2, l_w2, [w2.at[k] for k in range(N_CHIPS)], sem, 2 * N_CHIPS)
            for wait in waits:
                wait()
            sums_ref[...] = jnp.zeros_like(sums_ref)

        gate1, shift2, scale2 = v_ref[0:1, :], v_ref[1:2, :], v_ref[2:3, :]
        gate2, g_ffn, g_final = v_ref[3:4, :], v_ref[4:5, :], v_ref[5:6, :]

        mix = _dot(cat_ref[...], wout[...])
        x1 = x_ref[...] + gate1 * mix
        rstd2 = lax.rsqrt(_mean_last(x1 * x1) + EPS)
        xh2 = x1 * rstd2
        n2 = xh2 * g_ffn
        h2b = (n2 * (1.0 + scale2) + shift2).astype(MXU_DTYPE)
        h2_ref[...] = h2b
        ff = jnp.zeros((tm, D_MODEL), F32)
        for j in range(nj):
            a = _dot(h2b, w1[j])
            a_scr[j] = a
            relu = jnp.maximum(a, 0.0)
            rb = (relu * relu).astype(MXU_DTYPE)
            r_ref[:, D_MODEL * j:D_MODEL * (j + 1)] = rb
            ff = ff + _dot(rb, w2[j])
        x2 = x1 + gate2 * ff
        rstd3 = lax.rsqrt(_mean_last(x2 * x2) + EPS)
        xh3 = x2 * rstd3
        err = xh3 * g_final - t_ref[...]
        loss = 0.5 * _rowsum(_mean_last(err * err))
        dy = err * (1.0 / D_MODEL)
        dxh3 = dy * g_final
        dx2 = rstd3 * (dxh3 - xh3 * _mean_last(dxh3 * xh3))
        dffb = (dx2 * gate2).astype(MXU_DTYPE)
        dff_ref[...] = dffb
        dh2 = jnp.zeros((tm, D_MODEL), F32)
        for j in range(nj):
            dr = _dot_nt(dffb, w2[j])
            dab = (dr * (2.0 * jnp.maximum(a_scr[j], 0.0))).astype(MXU_DTYPE)
            da_ref[:, D_MODEL * j:D_MODEL * (j + 1)] = dab
            dh2 = dh2 + _dot_nt(dab, w1[j])
        dn2 = dh2 * (1.0 + scale2)
        dxh2 = dn2 * g_ffn
        dx1 = dx2 + rstd2 * (dxh2 - xh2 * _mean_last(dxh2 * xh2))
        dx1_ref[...] = dx1
        dmixb = (dx1 * gate1).astype(MXU_DTYPE)
        dmix_ref[...] = dmixb
        dcat_ref[...] = _dot_nt(dmixb, wout[...])

        sums_ref[0:1, :] += _rowsum(dh2)
        sums_ref[1:2, :] += _rowsum(dh2 * n2)
        sums_ref[2:3, :] += _rowsum(dx2 * ff)
        sums_ref[3:4, :] += _rowsum(dn2 * xh2)
        sums_ref[4:5, :] += _rowsum(dy * xh3)
        sums_ref[5:6, :] += _rowsum(dx1 * mix)
        sums_ref[6:7, :] += jnp.broadcast_to(loss, (1, D_MODEL))

    tok = lambda w: pl.BlockSpec((tm, w), lambda i, chip: (i, 0))
    return _hosted_call(
        body, None, name="trunk", grid=(seq // tm,), n_prefetch=1,
        out_shape=[jax.ShapeDtypeStruct((seq, D_MODEL), F32), jax.ShapeDtypeStruct((seq, D_MODEL), F32),
                   jax.ShapeDtypeStruct((seq, D_MODEL), MXU_DTYPE), jax.ShapeDtypeStruct((seq, D_MODEL), MXU_DTYPE),
                   jax.ShapeDtypeStruct((seq, D_FF), MXU_DTYPE), jax.ShapeDtypeStruct((seq, D_FF), MXU_DTYPE),
                   jax.ShapeDtypeStruct((seq, D_MODEL), MXU_DTYPE), jax.ShapeDtypeStruct((8, D_MODEL), F32)],
        in_specs=[tok(D_MODEL), tok(D_MODEL), tok(D_MODEL), _full((8, D_MODEL))] + [_any()] * 6,
        out_specs=[tok(D_MODEL), tok(D_MODEL), tok(D_MODEL), tok(D_MODEL), tok(D_FF), tok(D_FF), tok(D_MODEL),
                   _full((8, D_MODEL))],
        scratch_shapes=[pltpu.VMEM((D_MODEL, D_MODEL), MXU_DTYPE), pltpu.VMEM((nj, D_MODEL, D_MODEL), MXU_DTYPE),
                        pltpu.VMEM((nj, D_MODEL, D_MODEL), MXU_DTYPE), pltpu.VMEM((nj, tm, D_MODEL), F32),
                        pltpu.SemaphoreType.DMA((3 * N_CHIPS,))],
        semantics=("arbitrary",),
    )(chip_idx, x, target, cat, vecs, *gathered, *local)


def _mixer_bwd_kernel(proj, rope_tab, dcat, w_spatial, w_spatial_t, bias_full, sink_rows, dev_idx, comm=None):
    seq = proj.shape[0]
    per = MIXER_BLOCKS_PER_STEP
    steps = seq // (CHUNK * per)
    kv_col = KV_START // (2 * KV_WIDTH)

    def body(dev_ref, proj_ref, prev_ref, tab_ref, ptab_ref, dcat_ref, w_ref, wt_ref, bias_ref, sink_ref,
             dproj_ref, dw_out, db_ref, dsink_ref, carry, dw_ref):
        del dev_ref
        step = pl.program_id(0)

        @pl.when(step == 0)
        def _():
            carry[...] = jnp.zeros_like(carry)
            dw_ref[...] = jnp.zeros_like(dw_ref)
            db_ref[...] = jnp.zeros_like(db_ref)
            dsink_ref[...] = jnp.zeros_like(dsink_ref)

        for s in reversed(range(per)):
            rows = pl.ds(CHUNK * s, CHUNK)
            if s == 0:
                before, before_tab, first = prev_ref, ptab_ref, step == steps - 1
            else:
                before = proj_ref.at[pl.ds(CHUNK * (s - 1), CHUNK), pl.ds(KV_START, 2 * KV_WIDTH)]
                before_tab, first = tab_ref.at[pl.ds(CHUNK * (s - 1), CHUNK)], None
            one_block(proj_ref.at[rows], before, tab_ref.at[rows], before_tab, dcat_ref.at[rows], w_ref, wt_ref,
                      bias_ref, sink_ref, dproj_ref.at[rows], dw_ref, db_ref, dsink_ref, carry, first)

        @pl.when(step == steps - 1)
        def _():
            dw_out[...] = dw_ref[...].astype(dw_out.dtype)

    def one_block(proj_ref, prev_ref, tab_ref, ptab_ref, dcat_ref, w_ref, wt_ref, bias_ref, sink_ref,
                  dproj_ref, dw_ref, db_ref, dsink_ref, carry, first):
        wm, tril, triu = _masked_spatial(w_ref)
        lo, hi = _lane_masks((CHUNK, LANES))
        lane = lax.broadcasted_iota(jnp.int32, (CHUNK, LANES), 1)
        db = jnp.zeros((CHUNK, LANES), F32)
        for j in range(GMLP_GROUPS // 2):
            cols = slice(LANES * j, LANES * (j + 1))
            vcols = slice(GMLP_WIDTH + LANES * j, GMLP_WIDTH + LANES * (j + 1))
            zu, zv = proj_ref[:, cols], proj_ref[:, vcols]
            u, tu = _gelu_tanh(zu)
            vp, tv = _gelu_tanh(zv)
            sv = _sgu_forward_pair(wm, vp, j) + bias_ref[:, cols]
            dout = dcat_ref[:, cols]
            du = dout * sv
            dsv = dout * u
            dsv_lo, dsv_hi = jnp.where(lo, dsv, 0.0), jnp.where(hi, dsv, 0.0)
            lhs_t = jnp.concatenate([jnp.where(triu, wt_ref[2 * j], 0.0),
                                     jnp.where(triu, wt_ref[2 * j + 1], 0.0)], axis=1)
            dv = _dot(lhs_t, jnp.concatenate([dsv_lo, dsv_hi], axis=0))
            dw_ref[2 * j] += jnp.where(tril, _dot_nt(dsv_lo, vp), 0.0)
            dw_ref[2 * j + 1] += jnp.where(tril, _dot_nt(dsv_hi, vp), 0.0)
            db = db + (jnp.where(lane == 2 * j, jnp.sum(dsv_lo, axis=1, keepdims=True), 0.0)
                       + jnp.where(lane == 2 * j + 1, jnp.sum(dsv_hi, axis=1, keepdims=True), 0.0))
            dproj_ref[:, cols] = (du * _gelu_tanh_grad(zu, tu)).astype(dproj_ref.dtype)
            dproj_ref[:, vcols] = (dv * _gelu_tanh_grad(zv, tv)).astype(dproj_ref.dtype)
        db_ref[...] += db
        o = 2 * GMLP_WIDTH
        tab = tab_ref[...]
        q_r = _rope_apply(proj_ref[:, o:o + ATTN_WIDTH], tab, 1.0)
        k_cur = _rope_apply(proj_ref[:, o + ATTN_WIDTH:o + ATTN_WIDTH + KV_WIDTH], tab, 1.0)
        k_prev = _rope_apply(prev_ref[:, 0:KV_WIDTH], ptab_ref[...], 1.0)
        k_a = jnp.concatenate([k_prev, k_cur], axis=0)
        v_a = jnp.concatenate([prev_ref[:, KV_WIDTH:2 * KV_WIDTH],
                               proj_ref[:, o + ATTN_WIDTH + KV_WIDTH:o + ATTN_WIDTH + 2 * KV_WIDTH]], axis=0)
        k_b = pltpu.roll(k_a, HEAD_DIM, 1)
        v_b = pltpu.roll(v_a, HEAD_DIM, 1)
        bias_t = _attn_bias_t(first)
        lo2, _ = _lane_masks((2 * CHUNK, LANES))
        dout_b = dcat_ref[:, GMLP_WIDTH:GMLP_WIDTH + ATTN_WIDTH]
        dk_tot, dv_tot, dq_pairs = [], [], []
        for g in range(N_KV_HEADS):
            k_dup, v_dup = _group_dup(k_a, k_b, g, lo2), _group_dup(v_a, v_b, g, lo2)
            q_rows = _group_rows(q_r, g, lo, hi)
            do_rows = _group_rows(dout_b, g, lo, hi)
            p_t, p_sink = _attn_probs_t(k_dup, q_rows, bias_t, _sink_row(sink_ref, g))
            dp_t = _dot_nt(v_dup, do_rows)
            delta = jnp.sum(p_t * dp_t, axis=0, keepdims=True)
            ds_t = p_t * (dp_t - delta) * ATTN_SCALE
            dsink = -p_sink * delta
            for r in range(HEADS_PER_GROUP):
                h = HEADS_PER_GROUP * g + r
                dsink_ref[h:h + 1, :] += jnp.broadcast_to(
                    jnp.sum(dsink[:, LANES * r:LANES * (r + 1)], axis=1, keepdims=True), (1, LANES))
            dk_full = _dot(ds_t, q_rows)
            dv_full = _dot(p_t, do_rows)
            dk_tot.append(dk_full + pltpu.roll(dk_full, HEAD_DIM, 1))
            dv_tot.append(dv_full + pltpu.roll(dv_full, HEAD_DIM, 1))
            dq_t = _dot(k_dup.T, ds_t)
            dq_pairs += _pairs_from_rows(dq_t.T, lo)
        dk_all = jnp.where(lo2, dk_tot[0], dk_tot[1])
        dv_all = jnp.where(lo2, dv_tot[0], dv_tot[1])
        dk_cur = dk_all[CHUNK:, :] + carry[:, 0:KV_WIDTH]
        dv_cur = dv_all[CHUNK:, :] + carry[:, KV_WIDTH:2 * KV_WIDTH]
        carry[:, 0:KV_WIDTH] = dk_all[:CHUNK, :]
        carry[:, KV_WIDTH:2 * KV_WIDTH] = dv_all[:CHUNK, :]
        dq = _rope_apply(jnp.concatenate(dq_pairs, axis=1), tab, -1.0)
        dproj_ref[:, o:o + ATTN_WIDTH] = dq.astype(dproj_ref.dtype)
        dproj_ref[:, o + ATTN_WIDTH:o + ATTN_WIDTH + KV_WIDTH] = (
            _rope_apply(dk_cur, tab, -1.0).astype(dproj_ref.dtype))
        dproj_ref[:, o + ATTN_WIDTH + KV_WIDTH:o + ATTN_WIDTH + 2 * KV_WIDTH] = dv_cur.astype(dproj_ref.dtype)

    rev = lambda i: steps - 1 - i
    before = lambda i: jnp.maximum(per * rev(i) - 1, 0)
    slot = lambda shape: pl.BlockSpec((None,) + shape, lambda i, d: (d[0],) + (0,) * len(shape))
    return _hosted_call(
        body, comm, name="mixer_bwd", grid=(steps,), n_prefetch=1,
        out_shape=[jax.ShapeDtypeStruct((seq, IN_PROJ_WIDTH), MXU_DTYPE),
                   jax.ShapeDtypeStruct((N_DEV, GMLP_GROUPS, CHUNK, CHUNK), GRAD_COMM_DTYPE),
                   jax.ShapeDtypeStruct((N_DEV, CHUNK, LANES), F32),
                   jax.ShapeDtypeStruct((N_DEV, N_Q_HEADS, LANES), F32)],
        in_specs=[pl.BlockSpec((CHUNK * per, IN_PROJ_WIDTH), lambda i, d: (rev(i), 0)),
                  pl.BlockSpec((CHUNK, 2 * KV_WIDTH), lambda i, d: (before(i), kv_col)),
                  pl.BlockSpec((CHUNK * per, 3 * LANES), lambda i, d: (rev(i), 0)),
                  pl.BlockSpec((CHUNK, 3 * LANES), lambda i, d: (before(i), 0)),
                  pl.BlockSpec((CHUNK * per, D_MODEL), lambda i, d: (rev(i), 0)),
                  _full((GMLP_GROUPS, CHUNK, CHUNK)), _full((GMLP_GROUPS, CHUNK, CHUNK)),
                  _full((CHUNK, GMLP_WIDTH)), _full((N_Q_HEADS, LANES))],
        out_specs=[pl.BlockSpec((CHUNK * per, IN_PROJ_WIDTH), lambda i, d: (rev(i), 0)),
                   slot((GMLP_GROUPS, CHUNK, CHUNK)), slot((CHUNK, LANES)), slot((N_Q_HEADS, LANES))],
        scratch_shapes=[pltpu.VMEM((CHUNK, 2 * KV_WIDTH), F32), pltpu.VMEM((GMLP_GROUPS, CHUNK, CHUNK), F32)],
        semantics=("arbitrary",),
    )(dev_idx, proj, proj, rope_tab, rope_tab, dcat, w_spatial, w_spatial_t, bias_full, sink_rows)


def _in_proj_bwd_kernel(x, dx1, dproj, vecs, w_in_t, comm=None):
    seq = x.shape[0]
    tm = 512

    def body(x_ref, dx1_ref, dp_ref, v_ref, w_ref, gx_ref, sums_ref):
        @pl.when(pl.program_id(0) == 0)
        def _():
            sums_ref[...] = jnp.zeros_like(sums_ref)

        g_mix, scale1 = v_ref[0:1, :], v_ref[2:3, :]
        dh = _dot(dp_ref[...], w_ref[...])
        xv = x_ref[...]
        rstd = lax.rsqrt(_mean_last(xv * xv) + EPS)
        xh = xv * rstd
        dn1 = dh * (1.0 + scale1)
        dxh = dn1 * g_mix
        gx_ref[...] = dx1_ref[...] + rstd * (dxh - xh * _mean_last(dxh * xh))
        sums_ref[0:1, :] += _rowsum(dh)
        sums_ref[1:2, :] += _rowsum(dh * (xh * g_mix))
        sums_ref[2:3, :] += _rowsum(dn1 * xh)

    tok = lambda w: pl.BlockSpec((tm, w), lambda i: (i, 0))
    return _hosted_call(
        body, comm, name="in_proj_bwd", grid=(seq // tm,),
        out_shape=[jax.ShapeDtypeStruct((seq, D_MODEL), F32), jax.ShapeDtypeStruct((8, D_MODEL), F32)],
        in_specs=[tok(D_MODEL), tok(D_MODEL), tok(IN_PROJ_WIDTH), _full((8, D_MODEL)),
                  _full((IN_PROJ_WIDTH, D_MODEL))],
        out_specs=[tok(D_MODEL), _full((8, D_MODEL))],
        semantics=("arbitrary",),
    )(x, dx1, dproj, vecs, w_in_t)


class _GradTiles(NamedTuple):
    tm: int
    tn: int
    n_tiles: int
    chips_per_tile: int
    a_index: Callable
    b_index: Callable


def _weight_grad_kernel(a, b, c_idx, name, tiles, comm=None):
    seq = a.shape[0]
    tk = min(seq, 4096)
    nk = seq // tk
    tm, tn, n_tiles, per = tiles.tm, tiles.tn, tiles.n_tiles, tiles.chips_per_tile
    rows = tm // per

    def half(phase, c):
        return phase * c[0] + (1 - phase) * (1 - c[0])

    def body(c_ref, a_ref, b_ref, o_ref, acc, stage, landed, send_sems, recv_sems):
        del c_ref
        phase, t, kk = pl.program_id(0), pl.program_id(1), pl.program_id(2)
        x, y, c, _ = _mesh_place()

        def copy(tile):
            return pltpu.make_async_remote_copy(
                src_ref=stage.at[tile], dst_ref=landed.at[tile], send_sem=send_sems.at[tile],
                recv_sem=recv_sems.at[tile], device_id=(x, y, 1 - c), device_id_type=MESH)

        @pl.when(kk == 0)
        def _():
            acc[...] = jnp.zeros_like(acc)

        acc[...] += _dot_tn(a_ref[...], b_ref[...])

        @pl.when((kk == nk - 1) & (phase == 0))
        def _():
            stage[t] = acc[...].astype(stage.dtype)
            copy(t).start()

        @pl.when((kk == nk - 1) & (phase == 1))
        def _():
            copy(t).wait_recv()
            total = acc[...] + landed[t].astype(F32)
            for q in range(per):
                o_ref[q] = total[rows * q:rows * (q + 1)].astype(o_ref.dtype)

        @pl.when((kk == nk - 1) & (phase == 1) & (t == n_tiles - 1))
        def _():
            for tile in range(n_tiles):
                copy(tile).wait_send()

    out = _hosted_call(
        body, comm, name=name, grid=(2, n_tiles, nk), n_prefetch=1,
        out_shape=[jax.ShapeDtypeStruct((n_tiles * per, rows, tn), GRAD_COMM_DTYPE)],
        in_specs=[pl.BlockSpec((tk, tm), lambda p, t, k, c: (k, tiles.a_index(t, half(p, c)))),
                  pl.BlockSpec((tk, tn), lambda p, t, k, c: (k, tiles.b_index(t, half(p, c))))],
        out_specs=[pl.BlockSpec((per, rows, tn), lambda p, t, k, c: (p * t, 0, 0))],
        scratch_shapes=[pltpu.VMEM((tm, tn), F32), pltpu.VMEM((n_tiles, tm, tn), GRAD_COMM_DTYPE),
                        pltpu.VMEM((n_tiles, tm, tn), GRAD_COMM_DTYPE),
                        pltpu.SemaphoreType.DMA((n_tiles,)), pltpu.SemaphoreType.DMA((n_tiles,))],
        semantics=("arbitrary", "arbitrary", "arbitrary"),
    )(c_idx, a, b)
    return out[0] if comm is None else out


def _row_tile(rows, most=256, sublanes=16):
    return max(t for t in range(sublanes, most + 1, sublanes) if rows % t == 0)


def _adam_update(w, g, m, v):
    m_new = ADAM_B1 * m + (1.0 - ADAM_B1) * g
    v_new = ADAM_B2 * v + (1.0 - ADAM_B2) * (g * g)
    m_hat = m_new / (1.0 - ADAM_B1 ** ADAM_STEP)
    v_hat = v_new / (1.0 - ADAM_B2 ** ADAM_STEP)
    delta = -ADAM_LR * (m_hat / (jnp.sqrt(v_hat) + ADAM_EPS) + ADAM_WD * w)
    return delta, m_new, v_new


def _sum_chips_kernel(own, others, place, name):
    _, r, n = own.shape
    tr = _row_tile(r)

    def body(place_ref, own_ref, oth_ref, o_ref):
        del place_ref
        acc = own_ref[...].astype(F32)
        for k in range(N_CHIPS - 1):
            acc = acc + oth_ref[k].astype(F32)
        o_ref[...] = acc

    return pl.pallas_call(
        body, name=name, out_shape=jax.ShapeDtypeStruct((2, r, n), F32),
        grid_spec=pltpu.PrefetchScalarGridSpec(
            num_scalar_prefetch=1, grid=(r // tr,),
            in_specs=[pl.BlockSpec((None, tr, n), lambda i, p: (p[0], i, 0)),
                      pl.BlockSpec((N_CHIPS - 1, tr, n), lambda i, p: (0, i, 0))],
            out_specs=pl.BlockSpec((None, tr, n), lambda i, p: (p[1], i, 0))),
        compiler_params=_params("parallel"),
    )(place, own, others)


def _adam_kernel(w, g, m, v, name):
    r, n = w.shape
    by_columns = g.shape[1] == r
    tr, tn = _row_tile(g.shape[1], most=512), g.shape[2]

    def body(w_ref, g_ref, m_ref, v_ref, g_out, d_ref, mo_ref, vo_ref):
        gv = g_ref[...]
        g_out[...] = gv
        d_ref[...], mo_ref[...], vo_ref[...] = _adam_update(w_ref[...], gv, m_ref[...], v_ref[...])

    steps = g.shape[1] // tr
    spec = pl.BlockSpec((tr, tn), (lambda h, i: (i, h)) if by_columns else (lambda h, i: (h * steps + i, 0)))
    return pl.pallas_call(
        body, name=name, grid=(2, steps), out_shape=[jax.ShapeDtypeStruct((r, n), F32)] * 4,
        in_specs=[spec, pl.BlockSpec((None, tr, tn), lambda h, i: (h, i, 0)), spec, spec], out_specs=[spec] * 4,
        compiler_params=_params("parallel", "parallel"),
    )(w, g, m, v)


SMALL_PARAMS = ("b_ada", "g_mix", "g_ffn", "g_final", "b_spatial", "sinks", "w_spatial")


def _small_update_kernel(gathered, params):
    shapes = [params[nm][0].shape for nm in SMALL_PARAMS]

    def body(*refs):
        g_refs, refs = refs[:5], refs[5:]
        p_refs, refs = refs[:3 * len(SMALL_PARAMS)], refs[3 * len(SMALL_PARAMS):]
        loss_ref, o_refs = refs[0], refs[1:]

        def total(ref):
            acc = ref[0].astype(F32)
            for k in range(1, N_DEV):
                acc = acc + ref[k].astype(F32)
            return acc

        s1, s2, db, ds, dw = (total(r) for r in g_refs)
        loss_ref[...] = jnp.broadcast_to(s2[6:7, 0:1], loss_ref.shape)
        grads = {"b_ada": [s1[0:1], s1[1:2], s2[5:6], s2[0:1], s2[1:2], s2[2:3]], "g_mix": [s1[2:3]],
                 "g_ffn": [s2[3:4]], "g_final": [s2[4:5]], "b_spatial": [db.T[0:GMLP_GROUPS]],
                 "w_spatial": [dw]}
        lane = lax.broadcasted_iota(jnp.int32, (1, LANES), 1)
        sink_row = jnp.zeros((1, LANES), F32)
        for h in range(N_Q_HEADS):
            sink_row = sink_row + jnp.where(lane == h, ds[h:h + 1, :], 0.0)
        grads["sinks"] = [sink_row[:, 0:N_Q_HEADS]]
        for i, nm in enumerate(SMALL_PARAMS):
            w_ref, m_ref, v_ref = p_refs[3 * i:3 * i + 3]
            outs = o_refs[4 * i:4 * i + 4]
            width = grads[nm][0].shape[1]
            for k, g in enumerate(grads[nm]):
                cols = slice(width * k, width * (k + 1))
                upd = _adam_update(w_ref[:, cols], g, m_ref[:, cols], v_ref[:, cols])
                for o_ref, val in zip(outs, (g,) + upd):
                    o_ref[:, cols] = val

    flat = [a for nm in SMALL_PARAMS for a in params[nm]]
    out_shape = [jax.ShapeDtypeStruct((8, LANES), F32)]
    out_shape += [jax.ShapeDtypeStruct(s, F32) for s in shapes for _ in range(4)]
    outs = pl.pallas_call(
        body, name="small_update", grid=(1,), out_shape=out_shape,
        in_specs=[_full(g.shape) for g in gathered] + [_full(a.shape) for a in flat],
        out_specs=[_full(s.shape) for s in out_shape],
        compiler_params=_params("arbitrary"),
    )(*gathered, *flat)
    return {nm: outs[1 + 4 * i:5 + 4 * i] for i, nm in enumerate(SMALL_PARAMS)}, outs[0]


def _ada_update_kernel(act_t, dmod, w, m, v):
    r, n = w.shape
    tr = 256

    def body(a_ref, d_ref, w_ref, m_ref, v_ref, g_ref, dl_ref, mo_ref, vo_ref):
        g = _dot(a_ref[...], d_ref[...])
        g_ref[...] = g
        dl_ref[...], mo_ref[...], vo_ref[...] = _adam_update(w_ref[...], g, m_ref[...], v_ref[...])

    spec = pl.BlockSpec((tr, n), lambda i: (i, 0))
    return pl.pallas_call(
        body, name="ada_update", grid=(r // tr,), out_shape=[jax.ShapeDtypeStruct((r, n), F32)] * 4,
        in_specs=[pl.BlockSpec((tr, N_DEV), lambda i: (i, 0)), _full((N_DEV, n)), spec, spec, spec],
        out_specs=[spec] * 4, compiler_params=_params("parallel"),
    )(act_t, dmod, w, m, v)


def kernel(x, c, positions, w_ada, b_ada, g_mix, w_in, w_spatial, b_spatial, sinks, w_out, g_ffn, w_ff1, w_ff2, g_final, loss_target, m_w_ada, m_b_ada, m_g_mix, m_w_in, m_w_spatial, m_b_spatial, m_sinks, m_w_out, m_g_ffn, m_w_ff1, m_w_ff2, m_g_final, v_w_ada, v_b_ada, v_g_mix, v_w_in, v_w_spatial, v_b_spatial, v_sinks, v_w_out, v_g_ffn, v_w_ff1, v_w_ff2, v_g_final):
    xi, yi, ci = lax.axis_index("x"), lax.axis_index("y"), lax.axis_index("c")
    chip = 2 * xi + yi
    dev = 2 * chip + ci
    seq = x.shape[1]
    x2, tgt = x[0], loss_target[0]
    ada_cols = w_ada.shape[2]

    big = {"w_in": tuple(a[0].T for a in (w_in, m_w_in, v_w_in)),
           "w_out": (w_out[0], m_w_out[0], v_w_out[0]), "w_ff1": (w_ff1[0], m_w_ff1[0], v_w_ff1[0]),
           "w_ff2": (w_ff2[0], m_w_ff2[0], v_w_ff2[0])}

    def halves(nm):
        r, n = big[nm][0].shape
        return big[nm][0].astype(WEIGHT_COMM_DTYPE).reshape(2, r // 2, n)

    chip_idx = chip.reshape(1).astype(jnp.int32)
    c_all, w_in_t, g_out = _all_gather8([c, halves("w_in"), halves("w_out")], "gather_first",
                                        split=[False, True, True], skip_own=(2,))
    c_all, w_in_t = c_all.reshape(N_DEV, D_MODEL), w_in_t.reshape(IN_PROJ_WIDTH, D_MODEL)
    b_shard = lax.dynamic_slice(b_ada, (0, chip * ada_cols), (1, ada_cols))
    mod_part, act = _mod_kernel(c_all, w_ada[0], b_shard)
    mod_all, = _all_gather8([mod_part], "gather_mod")
    mod_me = lax.dynamic_index_in_dim(mod_all[0::2], dev, axis=1, keepdims=False)
    mod_me = mod_me.reshape(N_MOD, D_MODEL)
    shift1, scale1, gate1, shift2, scale2, gate2 = (mod_me[k:k + 1] for k in range(N_MOD))

    zeros_row = jnp.zeros((1, D_MODEL), F32)
    vecs1 = jnp.concatenate([g_mix, shift1, scale1] + [zeros_row] * 5, axis=0)
    vecs2 = jnp.concatenate([gate1, shift2, scale2, gate2, g_ffn, g_final.reshape(1, D_MODEL)]
                            + [zeros_row] * 2, axis=0)
    bias_full = jnp.repeat(b_spatial[0].T, HEAD_DIM, axis=1)
    sink_rows = jnp.broadcast_to(sinks[0][:, None], (N_Q_HEADS, LANES))
    inv_freq = ROPE_THETA ** (-jnp.arange(0, ROT_DIM, 2, dtype=F32) / ROT_DIM)
    rope_tab = _rope_lane_tables(*_rope_angle_kernel(positions, inv_freq.reshape(ROT_DIM // 2, 1)))

    trunk_weights = ["w_out", "w_ff1", "w_ff2"]
    shards = [halves(nm) for nm in trunk_weights]
    proj, hb, *staged = _in_proj_kernel(x2, vecs1, w_in_t, comm=_gather2d_first(shards[1:]))
    cat, *staged = _mixer_fwd_kernel(proj, rope_tab, w_spatial[0], bias_full, sink_rows,
                                     comm=_gather2d_second(staged, shards[1:]))
    staged = [g_out] + list(_gather_forward(staged, "gather_forward"))
    dx1, dcat, dmix, h2b, rb, dab, dffb, sums2 = _trunk_kernel(
        x2, tgt, cat, vecs2, chip_idx,
        [g.reshape((N_CHIPS,) + big[nm][0].shape) for nm, g in zip(trunk_weights, staged)],
        [s.reshape(big[nm][0].shape) for nm, s in zip(trunk_weights, shards)])

    c_idx = ci.reshape(1).astype(jnp.int32)
    place = jnp.stack([chip, ci]).astype(jnp.int32)
    half_d = D_MODEL // 2
    cs_ff2 = _weight_grad_kernel(rb, dffb, c_idx, "dw_ff2",
                                 _GradTiles(D_MODEL, half_d, N_CHIPS, 1, lambda t, h: t, lambda t, h: h))
    eighth = D_MODEL // 8
    cs_ff1, sc_ff2 = _weight_grad_kernel(
        h2b, dab, c_idx, "dw_ff1",
        _GradTiles(D_MODEL, half_d, N_CHIPS, 1, lambda t, h: 0, lambda t, h: 2 * t + h),
        comm=_scatter_job([cs_ff2], rows=(0, 6 * eighth)))
    cs_out, sc_ff2 = _weight_grad_kernel(
        cat, dmix, c_idx, "dw_out", _GradTiles(D_MODEL, half_d, 1, N_CHIPS, lambda t, h: 0, lambda t, h: h),
        comm=_scatter_job([cs_ff2], rows=(6 * eighth, eighth), into=[sc_ff2]))
    dproj, dw_spatial, db_lanes, dsink_rows, sc_ff2, sc_ff1, sc_out = _mixer_bwd_kernel(
        proj, rope_tab, dcat, w_spatial[0], w_spatial[0].transpose(0, 2, 1), bias_full, sink_rows,
        dev.reshape(1).astype(jnp.int32),
        comm=_merge_jobs(_scatter_job([cs_ff1, cs_out]),
                         _scatter_job([cs_ff2], rows=(7 * eighth, eighth), into=[sc_ff2])))
    totals = [_sum_chips_kernel(own, oth, place, "grad_sum_" + nm)
              for nm, own, oth in (("w_out", cs_out, sc_out), ("w_ff1", cs_ff1, sc_ff1), ("w_ff2", cs_ff2, sc_ff2))]
    small_slots = [db_lanes, dsink_rows, dw_spatial.reshape(N_DEV, GMLP_GROUPS * CHUNK, CHUNK)]
    cs_in, *rode = _weight_grad_kernel(
        dproj, hb, c_idx, "dw_in",
        _GradTiles(2 * W_IN_BLOCK, half_d, N_CHIPS // 2, 2, lambda t, h: t, lambda t, h: h),
        comm=_merge_jobs(_gather_job(small_slots), _share_job(totals)))
    small_stage1, shared = rode[:len(small_slots)], rode[len(small_slots):]
    grad_x, sums1 = _in_proj_bwd_kernel(x2, dx1, dproj, vecs1, w_in_t)
    *gathered, sc_in = _all_gather8([sums1, sums2], "gather_small", forward=small_stage1,
                                    riders=[_scatter_job([cs_in])])
    total_in = _sum_chips_kernel(cs_in, sc_in, place, "grad_sum_w_in")
    shared = list(_sibling_share([total_in], "grad_share_w_in")) + list(shared)
    names = ["w_in", "w_out", "w_ff1", "w_ff2"]
    big_out = {}
    for nm, g in zip(names, shared):
        w, m, v = big[nm]
        outs = _adam_kernel(w, g, m, v, "adam_" + nm)
        big_out[nm] = tuple((t.T if nm == "w_in" else t)[None] for t in outs)

    small = {"b_ada": (b_ada, m_b_ada, v_b_ada), "g_mix": (g_mix, m_g_mix, v_g_mix),
             "g_ffn": (g_ffn, m_g_ffn, v_g_ffn), "g_final": (g_final, m_g_final, v_g_final),
             "b_spatial": (b_spatial, m_b_spatial, v_b_spatial), "sinks": (sinks, m_sinks, v_sinks),
             "w_spatial": (w_spatial, m_w_spatial, v_w_spatial)}
    flat_shape = {"g_final": (1, D_MODEL), "b_spatial": (GMLP_GROUPS, CHUNK), "w_spatial": (GMLP_GROUPS * CHUNK, CHUNK)}
    small_out, loss_tile = _small_update_kernel(
        gathered, {nm: tuple(a.reshape(flat_shape.get(nm, a.shape)) for a in small[nm]) for nm in small})
    small_out = {nm: [o.reshape(small[nm][0].shape) for o in small_out[nm]] for nm in small}
    loss = loss_tile[0, 0]

    g1, g2 = gathered[0], gathered[1]
    dmod_all = jnp.concatenate([g1[:, 0], g1[:, 1], g2[:, 5], g2[:, 0], g2[:, 1], g2[:, 2]], axis=1)
    dmod_cols = lax.dynamic_slice(dmod_all, (0, chip * ada_cols), (N_DEV, ada_cols))
    ada = _ada_update_kernel(act.T, dmod_cols, w_ada[0], m_w_ada[0], v_w_ada[0])
    big_out["w_ada"] = tuple(t[None] for t in ada)

    order = ["w_ada", "b_ada", "g_mix", "w_in", "w_spatial", "b_spatial", "sinks", "w_out", "g_ffn",
             "w_ff1", "w_ff2", "g_final"]

    def leaf(nm, k):
        return big_out[nm][k] if nm in big_out else small_out[nm][k]

    outs = [loss, grad_x[None]]
    for k in range(4):
        outs += [leaf(nm, k) for nm in order]
    return tuple(outs)
```

```python
import math
from typing import Callable, NamedTuple

import jax
import jax.numpy as jnp
from jax import lax
from jax.experimental import pallas as pl
from jax.experimental.pallas import tpu as pltpu

F32 = jnp.float32
MXU_DTYPE = jnp.bfloat16
WEIGHT_COMM_DTYPE = jnp.bfloat16
GRAD_COMM_DTYPE = jnp.bfloat16

D_MODEL = 1024
D_FF = 4096
HEAD_DIM = 64
GMLP_GROUPS = 8
GMLP_WIDTH = 512
CHUNK = 128
N_Q_HEADS = 8
N_KV_HEADS = 2
ATTN_WIDTH = 512
KV_WIDTH = 128
ROT_DIM = 16
ROPE_THETA = 500000.0
IN_PROJ_WIDTH = 1792
N_MOD = 6
EPS = 1e-5
N_CHIPS = 4
N_DEV = 8
LANES = 128
W_IN_BLOCK = IN_PROJ_WIDTH // N_CHIPS

ADAM_LR = 0.001
ADAM_B1 = 0.9
ADAM_B2 = 0.999
ADAM_EPS = 1e-08
ADAM_WD = 0.01
ADAM_STEP = 10

VMEM_LIMIT_BYTES = 58 * 1024 * 1024
MESH = pl.DeviceIdType.MESH


def _params(*semantics):
    return pltpu.CompilerParams(dimension_semantics=semantics, vmem_limit_bytes=VMEM_LIMIT_BYTES)


def _dot(a, b):
    return jnp.dot(a.astype(MXU_DTYPE), b.astype(MXU_DTYPE), preferred_element_type=F32)


def _dot_nt(a, b):
    return lax.dot_general(a.astype(MXU_DTYPE), b.astype(MXU_DTYPE), (((1,), (1,)), ((), ())),
                           preferred_element_type=F32)


def _dot_tn(a, b):
    return lax.dot_general(a.astype(MXU_DTYPE), b.astype(MXU_DTYPE), (((0,), (0,)), ((), ())),
                           preferred_element_type=F32)


def _full(shape):
    return pl.BlockSpec(shape, lambda *_: (0,) * len(shape))


def _any():
    return pl.BlockSpec(memory_space=pl.ANY)


def _rowsum(v):
    return jnp.sum(v, axis=0, keepdims=True)


def _mean_last(v):
    return jnp.mean(v, axis=-1, keepdims=True)


class _Comm(NamedTuple):
    operands: tuple
    out_shapes: tuple
    n_sems: int
    make: Callable
    in_place: int = 0


def _hosted_call(body, comm, *, name, grid, in_specs, out_shape, out_specs, scratch_shapes=(), semantics,
                 n_prefetch=0):
    if comm is None:
        return pl.pallas_call(
            body, name=name, out_shape=out_shape, compiler_params=_params(*semantics),
            grid_spec=pltpu.PrefetchScalarGridSpec(
                num_scalar_prefetch=n_prefetch, grid=grid, in_specs=in_specs, out_specs=out_specs,
                scratch_shapes=list(scratch_shapes)))
    n_in, n_out, n_scr = len(in_specs), len(out_shape), len(scratch_shapes)
    k_in, k_out = len(comm.operands), len(comm.out_shapes)

    def hosted(*refs):
        prefetched, refs = refs[:n_prefetch], refs[n_prefetch:]
        ins, refs = refs[:n_in], refs[n_in:]
        c_ins, refs = refs[:k_in], refs[k_in:]
        outs, refs = refs[:n_out], refs[n_out:]
        c_outs, refs = refs[:k_out], refs[k_out:]
        scratch, (send_sems, recv_sems) = refs[:n_scr], refs[n_scr:]
        first, last = None, None
        for d, size in enumerate(grid):
            at_start, at_end = pl.program_id(d) == 0, pl.program_id(d) == size - 1
            first = at_start if first is None else first & at_start
            last = at_end if last is None else last & at_end

        @pl.when(first)
        def _():
            for cp in comm.make(c_ins, c_outs, send_sems, recv_sems)[0]:
                cp.start()

        body(*prefetched, *ins, *outs, *scratch)

        @pl.when(last)
        def _():
            for wait in comm.make(c_ins, c_outs, send_sems, recv_sems)[1]:
                wait()

    aliases = {n_prefetch + n_in + i: n_out + i for i in range(comm.in_place)}
    call = pl.pallas_call(
        hosted, name=name, out_shape=list(out_shape) + list(comm.out_shapes),
        compiler_params=_params(*semantics), input_output_aliases=aliases,
        grid_spec=pltpu.PrefetchScalarGridSpec(
            num_scalar_prefetch=n_prefetch, grid=grid, in_specs=list(in_specs) + [_any()] * k_in,
            out_specs=list(out_specs) + [_any()] * k_out,
            scratch_shapes=list(scratch_shapes) + [pltpu.SemaphoreType.DMA((comm.n_sems,)),
                                                    pltpu.SemaphoreType.DMA((comm.n_sems,))]))
    return lambda *args: call(*args, *comm.operands)


class _Shifted:
    def __init__(self, base, offset):
        self.base, self.offset = base, offset

    @property
    def at(self):
        return self

    def __getitem__(self, k):
        return self.base.at[self.offset + k]


def _merge_jobs(*jobs):
    def order(count):
        first = [(j, i) for j, job in enumerate(jobs) for i in range(job.in_place)]
        return first + [(j, i) for j, job in enumerate(jobs) for i in range(job.in_place, count(job))]

    op_order, out_order = order(lambda job: len(job.operands)), order(lambda job: len(job.out_shapes))

    def make(ins, outs, send_sems, recv_sems):
        starts, waits, sem = [], [], 0
        for j, job in enumerate(jobs):
            mine_in = [ins[k] for k, (jj, _) in enumerate(op_order) if jj == j]
            mine_out = [outs[k] for k, (jj, _) in enumerate(out_order) if jj == j]
            s, w = job.make(mine_in, mine_out, _Shifted(send_sems, sem), _Shifted(recv_sems, sem))
            starts, waits, sem = starts + s, waits + w, sem + job.n_sems
        return starts, waits

    return _Comm(tuple(jobs[j].operands[i] for j, i in op_order), tuple(jobs[j].out_shapes[i] for j, i in out_order),
                 sum(job.n_sems for job in jobs), make, in_place=sum(job.in_place for job in jobs))


def _mesh_place():
    x, y, c = lax.axis_index("x"), lax.axis_index("y"), lax.axis_index("c")
    return x, y, c, [(1 - x, y), (x, 1 - y), (1 - x, 1 - y)]


def _gather_job(bufs):
    per = 4

    def make(ins, outs, send_sems, recv_sems):
        del ins
        x, y, c, chips = _mesh_place()
        starts, waits = [], []
        for a, out in enumerate(outs):
            mine = src = out.at[4 * x + 2 * y + c]
            to = [(x, y, 1 - c)] + [(px, py, c) for px, py in chips]
            sends = [pltpu.make_async_remote_copy(
                src_ref=src, dst_ref=mine, send_sem=send_sems.at[per * a + k],
                recv_sem=recv_sems.at[per * a + k], device_id=dev, device_id_type=MESH)
                for k, dev in enumerate(to)]
            recvs = [pltpu.make_async_remote_copy(
                src_ref=src, dst_ref=out.at[4 * px + 2 * py + pc], send_sem=send_sems.at[per * a + k],
                recv_sem=recv_sems.at[per * a + k], device_id=(px, py, pc), device_id_type=MESH)
                for k, (px, py, pc) in enumerate(to)]
            starts += sends
            waits += [s.wait_send for s in sends] + [r.wait_recv for r in recvs]
        return starts, waits

    shapes = tuple(jax.ShapeDtypeStruct(b.shape, b.dtype) for b in bufs)
    return _Comm(tuple(bufs), shapes, per * len(bufs), make, in_place=len(bufs))


def _slots(x, y, c):
    return 4 * x + 2 * y + c, 4 * (1 - x) + 2 * y + c, 4 * x + 2 * (1 - y) + c, 4 * (1 - x) + 2 * (1 - y) + c


def _gather2d_first(halves):
    per = 2

    def make(ins, outs, send_sems, recv_sems):
        x, y, c, _ = _mesh_place()
        me, xn, yn, _ = _slots(x, y, c)
        starts, waits = [], []
        for a, (src, out) in enumerate(zip(ins, outs)):
            blk = src.at[c]
            rows = blk.shape[0] // 2
            upper, lower = pl.ds(0, rows), pl.ds(rows, rows)

            def copy(k, src_ref, dst_ref, dev, a=a):
                return pltpu.make_async_remote_copy(
                    src_ref=src_ref, dst_ref=dst_ref, send_sem=send_sems.at[per * a + k],
                    recv_sem=recv_sems.at[per * a + k], device_id=dev, device_id_type=MESH)

            sends = [copy(0, blk.at[upper], out.at[me, upper], (1 - x, y, c)),
                     copy(1, blk.at[lower], out.at[me, lower], (x, 1 - y, c))]
            recvs = [copy(0, blk.at[upper], out.at[xn, upper], (1 - x, y, c)),
                     copy(1, blk.at[lower], out.at[yn, lower], (x, 1 - y, c))]
            starts += sends
            waits += [s.wait_send for s in sends] + [r.wait_recv for r in recvs]
        return starts, waits

    shapes = tuple(jax.ShapeDtypeStruct((N_DEV,) + h.shape[1:], h.dtype) for h in halves)
    return _Comm(tuple(halves), shapes, per * len(halves), make)


def _gather2d_second(bufs, halves):
    per = 4
    n_arr = len(bufs)

    def make(ins, outs, send_sems, recv_sems):
        x, y, c, _ = _mesh_place()
        me, xn, yn, dg = _slots(x, y, c)
        starts, waits = [], []
        for a, buf in enumerate(outs):
            own = ins[n_arr + a].at[c]
            rows = buf.shape[1] // 2
            upper, lower = pl.ds(0, rows), pl.ds(rows, rows)
            plan = [(own.at[upper], me, upper, (x, 1 - y, c), yn), (buf.at[xn, upper], xn, upper, (x, 1 - y, c), dg),
                    (own.at[lower], me, lower, (1 - x, y, c), xn), (buf.at[yn, lower], yn, lower, (1 - x, y, c), dg)]
            for k, (src, slot, part, dev, landing) in enumerate(plan):
                sems = dict(send_sem=send_sems.at[per * a + k], recv_sem=recv_sems.at[per * a + k],
                            device_id=dev, device_id_type=MESH)
                send = pltpu.make_async_remote_copy(src_ref=src, dst_ref=buf.at[slot, part], **sems)
                arrival = pltpu.make_async_remote_copy(src_ref=src, dst_ref=buf.at[landing, part], **sems)
                starts.append(send)
                waits += [send.wait_send, arrival.wait_recv]
        return starts, waits

    shapes = tuple(jax.ShapeDtypeStruct(b.shape, b.dtype) for b in bufs)
    return _Comm(tuple(bufs) + tuple(halves), shapes, per * n_arr, make, in_place=n_arr)


def _gather_forward(bufs, name):
    n_arr = len(bufs)

    def body(*refs):
        outs = refs[n_arr:2 * n_arr]
        send_sems, recv_sems = refs[2 * n_arr:]
        x, y, c, chips = _mesh_place()
        sends, recvs = [], []
        for a, buf in enumerate(outs):
            for j, (px, py) in enumerate(chips):
                mine, theirs = buf.at[4 * px + 2 * py + c], buf.at[4 * px + 2 * py + 1 - c]
                sems = dict(send_sem=send_sems.at[3 * a + j], recv_sem=recv_sems.at[3 * a + j],
                            device_id=(x, y, 1 - c), device_id_type=MESH)
                sends.append(pltpu.make_async_remote_copy(src_ref=mine, dst_ref=mine, **sems))
                recvs.append(pltpu.make_async_remote_copy(src_ref=mine, dst_ref=theirs, **sems))
        for cp in sends:
            cp.start()
        for s, r in zip(sends, recvs):
            s.wait_send()
            r.wait_recv()

    return pl.pallas_call(
        body, name=name, out_shape=[jax.ShapeDtypeStruct(b.shape, b.dtype) for b in bufs],
        in_specs=[_any()] * n_arr, out_specs=[_any()] * n_arr,
        input_output_aliases={a: a for a in range(n_arr)},
        scratch_shapes=[pltpu.SemaphoreType.DMA((3 * n_arr,)), pltpu.SemaphoreType.DMA((3 * n_arr,))],
    )(*bufs)


def _scatter_job(chip_sums, rows=None, into=()):
    n_into = len(into)

    def part(ref):
        return ref if rows is None else ref.at[pl.ds(rows[0], rows[1])]

    def make(ins, outs, send_sems, recv_sems):
        x, y, c, chips = _mesh_place()
        copies = [pltpu.make_async_remote_copy(
            src_ref=part(src.at[2 * px + py]), dst_ref=part(out.at[j]), send_sem=send_sems.at[3 * a + j],
            recv_sem=recv_sems.at[3 * a + j], device_id=(px, py, c), device_id_type=MESH)
            for a, (src, out) in enumerate(zip(ins[n_into:], outs)) for j, (px, py) in enumerate(chips)]
        return copies, [cp.wait for cp in copies]

    shapes = tuple(jax.ShapeDtypeStruct((3,) + s.shape[1:], s.dtype) for s in chip_sums)
    return _Comm(tuple(into) + tuple(chip_sums), shapes, 3 * len(chip_sums), make, in_place=n_into)


def _all_gather8(blocks, name, split=False, forward=(), riders=(), skip_own=()):
    n_arr, n_fwd = len(blocks), len(forward)
    splits = list(split) if isinstance(split, (list, tuple)) else [split] * n_arr
    own_slots = [a not in skip_own for a in range(n_arr)]
    rider_in = sum(len(r.operands) for r in riders)
    rider_out = sum(len(r.out_shapes) for r in riders)

    def body(*refs):
        x_refs, refs = refs[:n_arr], refs[n_arr + n_fwd:]
        r_ins, refs = refs[:rider_in], refs[rider_in:]
        out_refs, refs = refs[:n_arr], refs[n_arr:]
        fwd_refs, refs = refs[:n_fwd], refs[n_fwd:]
        r_outs, refs = refs[:rider_out], refs[rider_out:]
        (send_sems, recv_sems, local_sems), rider_sems = refs[:3], refs[3:]
        x, y, c, chips = _mesh_place()
        me, sibling = (x, y, c), (x, y, 1 - c)
        passing = []
        for f, buf in enumerate(fwd_refs):
            for j, (px, py) in enumerate(chips):
                mine, theirs = buf.at[4 * px + 2 * py + c], buf.at[4 * px + 2 * py + 1 - c]
                sems = dict(send_sem=send_sems.at[7 * n_arr + 3 * f + j], recv_sem=recv_sems.at[7 * n_arr + 3 * f + j],
                            device_id=sibling, device_id_type=MESH)
                passing.append((pltpu.make_async_remote_copy(src_ref=mine, dst_ref=mine, **sems),
                                pltpu.make_async_remote_copy(src_ref=mine, dst_ref=theirs, **sems)))
        for send, _ in passing:
            send.start()
        arrays = []
        for a, (x_ref, out_ref) in enumerate(zip(x_refs, out_refs)):
            src_mine = x_ref.at[c] if splits[a] else x_ref

            def copy(k, blk, to, src=None, a=a, out_ref=out_ref):
                dst = out_ref.at[4 * blk[0] + 2 * blk[1] + blk[2]]
                return pltpu.make_async_remote_copy(
                    src_ref=dst if src is None else src, dst_ref=dst,
                    send_sem=send_sems.at[7 * a + k], recv_sem=recv_sems.at[7 * a + k],
                    device_id=to, device_id_type=MESH)

            mine = pltpu.make_async_copy(src_mine, out_ref.at[4 * x + 2 * y + c], local_sems.at[a])
            first = [copy(0, me, sibling, src=src_mine)] if own_slots[a] else []
            first += [copy(1 + j, me, (*chip, c), src=src_mine) for j, chip in enumerate(chips)]
            for cp in first + ([mine] if own_slots[a] else []):
                cp.start()
            arrays.append((copy, mine, first, own_slots[a]))
        rider_waits, i0, o0 = [], 0, 0
        for n, job in enumerate(riders):
            k_in, k_out = len(job.operands), len(job.out_shapes)
            starts, waits = job.make(r_ins[i0:i0 + k_in], r_outs[o0:o0 + k_out],
                                     rider_sems[2 * n], rider_sems[2 * n + 1])
            for cp in starts:
                cp.start()
            rider_waits += waits
            i0, o0 = i0 + k_in, o0 + k_out
        sent = []
        for copy, mine, first, own in arrays:
            passed = [copy(4 + j, (*chip, c), sibling) for j, chip in enumerate(chips)]
            for j, chip in enumerate(chips):
                copy(1 + j, (*chip, c), me).wait_recv()
                passed[j].start()
            sent += first + passed
        for copy, mine, first, own in arrays:
            if own:
                copy(0, sibling, me).wait_recv()
                mine.wait()
            for j, chip in enumerate(chips):
                copy(4 + j, (*chip, 1 - c), me).wait_recv()
        for cp in sent:
            cp.wait_send()
        for send, arrival in passing:
            send.wait_send()
            arrival.wait_recv()
        for wait in rider_waits:
            wait()

    n_sems = 7 * n_arr + 3 * n_fwd
    rider_operands = [a for r in riders for a in r.operands]
    rider_shapes = [s for r in riders for s in r.out_shapes]
    return pl.pallas_call(
        body, name=name,
        out_shape=[jax.ShapeDtypeStruct((N_DEV,) + tuple(b.shape[1:] if s else b.shape), b.dtype)
                   for b, s in zip(blocks, splits)]
        + [jax.ShapeDtypeStruct(f.shape, f.dtype) for f in forward] + rider_shapes,
        in_specs=[_any()] * (n_arr + n_fwd + rider_in), out_specs=[_any()] * (n_arr + n_fwd + rider_out),
        input_output_aliases={n_arr + f: n_arr + f for f in range(n_fwd)},
        scratch_shapes=[pltpu.SemaphoreType.DMA((n_sems,)), pltpu.SemaphoreType.DMA((n_sems,)),
                        pltpu.SemaphoreType.DMA((n_arr,))]
        + [pltpu.SemaphoreType.DMA((r.n_sems,)) for r in riders for _ in range(2)],
    )(*blocks, *forward, *rider_operands)


def _split_scatter_copies(src, land, send_sems, recv_sems):
    x, y, c, chips = _mesh_place()
    return [pltpu.make_async_remote_copy(
        src_ref=src.at[2 * px + py], dst_ref=land.at[j], send_sem=send_sems.at[j], recv_sem=recv_sems.at[j],
        device_id=(px, py, c), device_id_type=MESH) for j, (px, py) in enumerate(chips)]


def _scatter_start(chip_sums, name):
    hbm, sem = pl.BlockSpec(memory_space=pltpu.HBM), pl.BlockSpec(memory_space=pltpu.SEMAPHORE)

    def body(src, land, send_sems, recv_sems, src_thru, land_thru, token):
        del src_thru, land_thru
        for cp in _split_scatter_copies(src, land, send_sems, recv_sems):
            cp.start()
        token[...] = jnp.zeros_like(token)

    land = lax.empty((N_CHIPS - 1,) + chip_sums.shape[1:], chip_sums.dtype)
    operands = [pltpu.with_memory_space_constraint(a, pltpu.HBM) for a in (chip_sums, land)]
    return pl.pallas_call(
        body, name=name,
        out_shape=[pltpu.SemaphoreType.DMA((N_CHIPS - 1,)), pltpu.SemaphoreType.DMA((N_CHIPS - 1,))]
        + [pltpu.HBM(a.shape, a.dtype) for a in operands] + [jax.ShapeDtypeStruct((8, LANES), F32)],
        in_specs=[hbm, hbm], out_specs=[sem, sem, hbm, hbm, pl.BlockSpec(memory_space=pltpu.VMEM)],
        input_output_aliases={0: 2, 1: 3},
        compiler_params=pltpu.CompilerParams(has_side_effects=pltpu.SideEffectType.DATAFLOW_SIDE_EFFECTING),
    )(*operands)


def _scatter_wait(started, after, name):
    send_sems, recv_sems, src, land, _ = started
    hbm, sem = pl.BlockSpec(memory_space=pltpu.HBM), pl.BlockSpec(memory_space=pltpu.SEMAPHORE)

    def body(src, land, send_sems, recv_sems, after_ref, src_thru, land_thru):
        del after_ref, src_thru, land_thru
        for cp in _split_scatter_copies(src, land, send_sems, recv_sems):
            cp.wait()

    return pl.pallas_call(
        body, name=name, out_shape=[pltpu.HBM(src.shape, src.dtype), pltpu.HBM(land.shape, land.dtype)],
        in_specs=[hbm, hbm, sem, sem, _any()], out_specs=[hbm, hbm], input_output_aliases={0: 0, 1: 1},
        compiler_params=pltpu.CompilerParams(has_side_effects=pltpu.SideEffectType.DATAFLOW_SIDE_EFFECTING),
    )(src, land, send_sems, recv_sems, after)


def _share_job(bufs):
    def make(ins, outs, send_sems, recv_sems):
        del ins
        x, y, c, _ = _mesh_place()
        sems = lambda a: dict(send_sem=send_sems.at[a], recv_sem=recv_sems.at[a],
                              device_id=(x, y, 1 - c), device_id_type=MESH)
        sends = [pltpu.make_async_remote_copy(src_ref=o.at[c], dst_ref=o.at[c], **sems(a)) for a, o in enumerate(outs)]
        arrivals = [pltpu.make_async_remote_copy(src_ref=o.at[c], dst_ref=o.at[1 - c], **sems(a))
                    for a, o in enumerate(outs)]
        return sends, [s.wait_send for s in sends] + [r.wait_recv for r in arrivals]

    shapes = tuple(jax.ShapeDtypeStruct(b.shape, b.dtype) for b in bufs)
    return _Comm(tuple(bufs), shapes, len(bufs), make, in_place=len(bufs))


def _sibling_share(bufs, name):
    n_arr = len(bufs)

    def body(*refs):
        out_refs = refs[n_arr:2 * n_arr]
        send_sems, recv_sems = refs[2 * n_arr:]
        x, y, c = lax.axis_index("x"), lax.axis_index("y"), lax.axis_index("c")
        copies = [pltpu.make_async_remote_copy(
            src_ref=out_refs[a].at[c], dst_ref=out_refs[a].at[c],
            send_sem=send_sems.at[a], recv_sem=recv_sems.at[a],
            device_id=(x, y, 1 - c), device_id_type=MESH) for a in range(n_arr)]
        for cp in copies:
            cp.start()
        for a in range(n_arr):
            pltpu.make_async_remote_copy(
                src_ref=out_refs[a].at[c], dst_ref=out_refs[a].at[1 - c],
                send_sem=send_sems.at[a], recv_sem=recv_sems.at[a],
                device_id=(x, y, 1 - c), device_id_type=MESH).wait()

    return pl.pallas_call(
        body, name=name,
        out_shape=[jax.ShapeDtypeStruct(b.shape, b.dtype) for b in bufs],
        in_specs=[_any()] * n_arr, out_specs=[_any()] * n_arr,
        input_output_aliases={a: a for a in range(n_arr)},
        scratch_shapes=[pltpu.SemaphoreType.DMA((n_arr,)), pltpu.SemaphoreType.DMA((n_arr,))],
    )(*bufs)


def _gelu_tanh(z):
    k = math.sqrt(2.0 / math.pi)
    t = jnp.tanh(k * (z + 0.044715 * (z * z * z)))
    return 0.5 * z * (1.0 + t), t


def _gelu_tanh_grad(z, t):
    k = math.sqrt(2.0 / math.pi)
    return 0.5 * (1.0 + t) + 0.5 * z * (1.0 - t * t) * (k * (1.0 + 3.0 * 0.044715 * (z * z)))


def _rope_angle_kernel(pos_row, invf_col):
    seq = pos_row.shape[1]

    def body(p_ref, f_ref, cos_ref, sin_ref):
        ang = p_ref[...].astype(F32) * f_ref[...]
        cos_ref[...] = jnp.cos(ang)
        sin_ref[...] = jnp.sin(ang)

    return pl.pallas_call(
        body, name="rope_angles", grid=(1,), out_shape=[jax.ShapeDtypeStruct((ROT_DIM // 2, seq), F32)] * 2,
        in_specs=[_full((1, seq)), _full((ROT_DIM // 2, 1))], out_specs=[_full((ROT_DIM // 2, seq))] * 2,
        compiler_params=_params("arbitrary"),
    )(pos_row, invf_col)


def _rope_lane_tables(cos, sin):
    cos_t, sin_t = cos.T, sin.T
    seq, half = cos_t.shape
    ones = jnp.ones((seq, HEAD_DIM - ROT_DIM), F32)
    c64 = jnp.concatenate([cos_t, cos_t, ones], axis=1)
    s1 = jnp.concatenate([sin_t, jnp.zeros((seq, HEAD_DIM - half), F32)], axis=1)
    s2 = jnp.concatenate([jnp.zeros((seq, half), F32), sin_t, jnp.zeros((seq, HEAD_DIM - ROT_DIM), F32)], axis=1)
    return jnp.concatenate([jnp.tile(t, (1, LANES // HEAD_DIM)) for t in (c64, s1, s2)], axis=1)


def _rope_apply(t, tab, sign):
    reps = t.shape[1] // LANES
    c_tab, s1, s2 = (jnp.tile(tab[:, LANES * k:LANES * (k + 1)], (1, reps)) if reps > 1
                     else tab[:, LANES * k:LANES * (k + 1)] for k in range(3))
    half = ROT_DIM // 2
    up = pltpu.roll(t, t.shape[1] - half, 1)
    down = pltpu.roll(t, half, 1)
    return t * c_tab + sign * (down * s2 - up * s1)


def _lane_masks(shape):
    lane = lax.broadcasted_iota(jnp.int32, shape, 1)
    return lane < HEAD_DIM, lane >= HEAD_DIM


HEADS_PER_GROUP = N_Q_HEADS // N_KV_HEADS
ATTN_SCALE = 1.0 / math.sqrt(HEAD_DIM)


def _attn_bias_t(first_block):
    kj = lax.broadcasted_iota(jnp.int32, (2 * CHUNK, CHUNK), 0)
    qi = lax.broadcasted_iota(jnp.int32, (2 * CHUNK, CHUNK), 1)
    ok = (kj > qi) & (kj <= qi + CHUNK)
    if first_block is not None:
        ok = ok & (jnp.logical_not(first_block) | (kj >= CHUNK))
    return jnp.tile(jnp.where(ok, 0.0, -jnp.inf), (1, HEADS_PER_GROUP))


def _group_rows(x, g, lo, hi):
    rows = []
    for r in range(HEADS_PER_GROUP):
        h = HEADS_PER_GROUP * g + r
        pair = x[:, LANES * (h // 2):LANES * (h // 2 + 1)]
        rows.append(jnp.where(hi if h % 2 else lo, pair, 0.0))
    return jnp.concatenate(rows, axis=0)


def _pairs_from_rows(rows, lo):
    return [jnp.where(lo, rows[2 * CHUNK * k:2 * CHUNK * k + CHUNK], rows[2 * CHUNK * k + CHUNK:2 * CHUNK * (k + 1)])
            for k in range(HEADS_PER_GROUP // 2)]


def _group_dup(a, b, g, lo2):
    return jnp.where(lo2, a, b) if g == 0 else jnp.where(lo2, b, a)


def _sink_row(sink_ref, g):
    return jnp.concatenate([sink_ref[HEADS_PER_GROUP * g + r:HEADS_PER_GROUP * g + r + 1, :]
                            for r in range(HEADS_PER_GROUP)], axis=1)


def _attn_probs_t(k_dup, q_rows, bias_t, sink_row):
    s_t = _dot_nt(k_dup, q_rows) * ATTN_SCALE + bias_t
    m = jnp.maximum(jnp.max(s_t, axis=0, keepdims=True), sink_row)
    p = jnp.exp(s_t - m)
    e_sink = jnp.exp(sink_row - m)
    inv = 1.0 / (jnp.sum(p, axis=0, keepdims=True) + e_sink)
    return p * inv, e_sink * inv


def _sgu_forward_pair(wm, vp, j):
    lo, hi = _lane_masks(vp.shape)
    lhs = jnp.concatenate([wm[2 * j], wm[2 * j + 1]], axis=1)
    rhs = jnp.concatenate([jnp.where(lo, vp, 0.0), jnp.where(hi, vp, 0.0)], axis=0)
    return _dot(lhs, rhs)


def _masked_spatial(w_ref):
    t = lax.broadcasted_iota(jnp.int32, (CHUNK, CHUNK), 0)
    s = lax.broadcasted_iota(jnp.int32, (CHUNK, CHUNK), 1)
    tril = s <= t
    return [jnp.where(tril, w_ref[g], 0.0) for g in range(GMLP_GROUPS)], tril, s >= t


def _mod_kernel(c_all, w_shard, b_shard, comm=None):
    n = w_shard.shape[1]
    tn = 512

    def body(c_ref, w_ref, b_ref, mod_ref, act_ref):
        cv = c_ref[...]
        act = cv * (1.0 / (1.0 + jnp.exp(-cv)))
        act_ref[...] = act
        mod_ref[...] = _dot(act, w_ref[...]) + b_ref[...]

    return _hosted_call(
        body, comm, name="ada_mod", grid=(n // tn,),
        out_shape=[jax.ShapeDtypeStruct((N_DEV, n), F32), jax.ShapeDtypeStruct((N_DEV, D_MODEL), F32)],
        in_specs=[_full((N_DEV, D_MODEL)), pl.BlockSpec((D_MODEL, tn), lambda i: (0, i)),
                  pl.BlockSpec((1, tn), lambda i: (0, i))],
        out_specs=[pl.BlockSpec((N_DEV, tn), lambda i: (0, i)), _full((N_DEV, D_MODEL))],
        semantics=("arbitrary",),
    )(c_all, w_shard, b_shard)


def _load_chip_blocks(chip_ref, gathered, local, dsts, sems, first_sem=0):
    for k, dst in enumerate(dsts):
        @pl.when(chip_ref[0] == k)
        def _():
            pltpu.make_async_copy(local, dst, sems.at[first_sem + k]).start()

        @pl.when(chip_ref[0] != k)
        def _():
            pltpu.make_async_copy(gathered.at[k], dst, sems.at[first_sem + k]).start()
    return [pltpu.make_async_copy(local, dst, sems.at[first_sem + k]).wait for k, dst in enumerate(dsts)]


def _in_proj_kernel(x, vecs, w_in_t, comm=None):
    seq = x.shape[0]
    tm = 512

    def body(x_ref, v_ref, w_ref, proj_ref, h_ref):
        xv = x_ref[...]
        rstd = lax.rsqrt(_mean_last(xv * xv) + EPS)
        n1 = (xv * rstd) * v_ref[0:1, :]
        h = n1 * (1.0 + v_ref[2:3, :]) + v_ref[1:2, :]
        hb = h.astype(MXU_DTYPE)
        h_ref[...] = hb
        proj_ref[...] = _dot_nt(hb, w_ref[...])

    return _hosted_call(
        body, comm, name="in_proj", grid=(seq // tm,),
        out_shape=[jax.ShapeDtypeStruct((seq, IN_PROJ_WIDTH), F32),
                   jax.ShapeDtypeStruct((seq, D_MODEL), MXU_DTYPE)],
        in_specs=[pl.BlockSpec((tm, D_MODEL), lambda i: (i, 0)), _full((8, D_MODEL)),
                  _full((IN_PROJ_WIDTH, D_MODEL))],
        out_specs=[pl.BlockSpec((tm, IN_PROJ_WIDTH), lambda i: (i, 0)),
                   pl.BlockSpec((tm, D_MODEL), lambda i: (i, 0))],
        semantics=("arbitrary",),
    )(x, vecs, w_in_t)


MIXER_BLOCKS_PER_STEP = 4
KV_START = 2 * GMLP_WIDTH + ATTN_WIDTH


def _mixer_fwd_kernel(proj, rope_tab, w_spatial, bias_full, sink_rows, comm=None):
    seq = proj.shape[0]
    per = MIXER_BLOCKS_PER_STEP
    steps = seq // (CHUNK * per)
    kv_col = KV_START // (2 * KV_WIDTH)

    def body(proj_ref, prev_ref, tab_ref, ptab_ref, w_ref, bias_ref, sink_ref, cat_ref):
        i = pl.program_id(0)
        wm, _, _ = _masked_spatial(w_ref)
        lo, hi = _lane_masks((CHUNK, LANES))
        lo2, _ = _lane_masks((2 * CHUNK, LANES))
        o = 2 * GMLP_WIDTH
        for s in range(per):
            rows, before = slice(CHUNK * s, CHUNK * (s + 1)), slice(CHUNK * (s - 1), CHUNK * s)
            for j in range(GMLP_GROUPS // 2):
                cols = slice(LANES * j, LANES * (j + 1))
                vcols = slice(GMLP_WIDTH + LANES * j, GMLP_WIDTH + LANES * (j + 1))
                u, _ = _gelu_tanh(proj_ref[rows, cols])
                vp, _ = _gelu_tanh(proj_ref[rows, vcols])
                sv = _sgu_forward_pair(wm, vp, j) + bias_ref[:, cols]
                cat_ref[rows, cols] = (u * sv).astype(cat_ref.dtype)
            tab = tab_ref[rows, :]
            if s == 0:
                prev_kv, prev_tab, first = prev_ref[...], ptab_ref[...], i == 0
            else:
                prev_kv, prev_tab, first = proj_ref[before, KV_START:KV_START + 2 * KV_WIDTH], tab_ref[before, :], None
            q_r = _rope_apply(proj_ref[rows, o:o + ATTN_WIDTH], tab, 1.0)
            k_cur = _rope_apply(proj_ref[rows, KV_START:KV_START + KV_WIDTH], tab, 1.0)
            k_prev = _rope_apply(prev_kv[:, 0:KV_WIDTH], prev_tab, 1.0)
            k_a = jnp.concatenate([k_prev, k_cur], axis=0)
            v_a = jnp.concatenate([prev_kv[:, KV_WIDTH:2 * KV_WIDTH],
                                   proj_ref[rows, KV_START + KV_WIDTH:KV_START + 2 * KV_WIDTH]], axis=0)
            k_b = pltpu.roll(k_a, HEAD_DIM, 1)
            v_b = pltpu.roll(v_a, HEAD_DIM, 1)
            bias_t = _attn_bias_t(first)
            for g in range(N_KV_HEADS):
                p_t, _ = _attn_probs_t(_group_dup(k_a, k_b, g, lo2), _group_rows(q_r, g, lo, hi), bias_t,
                                       _sink_row(sink_ref, g))
                o_t = _dot(_group_dup(v_a, v_b, g, lo2).T, p_t)
                for k, pair in enumerate(_pairs_from_rows(o_t.T, lo)):
                    c0 = GMLP_WIDTH + LANES * (2 * g + k)
                    cat_ref[rows, c0:c0 + LANES] = pair.astype(cat_ref.dtype)

    return _hosted_call(
        body, comm, name="mixer_fwd", grid=(steps,),
        out_shape=[jax.ShapeDtypeStruct((seq, D_MODEL), MXU_DTYPE)],
        in_specs=[pl.BlockSpec((CHUNK * per, IN_PROJ_WIDTH), lambda i: (i, 0)),
                  pl.BlockSpec((CHUNK, 2 * KV_WIDTH), lambda i: (jnp.maximum(per * i - 1, 0), kv_col)),
                  pl.BlockSpec((CHUNK * per, 3 * LANES), lambda i: (i, 0)),
                  pl.BlockSpec((CHUNK, 3 * LANES), lambda i: (jnp.maximum(per * i - 1, 0), 0)),
                  _full((GMLP_GROUPS, CHUNK, CHUNK)), _full((CHUNK, GMLP_WIDTH)),
                  _full((N_Q_HEADS, LANES))],
        out_specs=[pl.BlockSpec((CHUNK * per, D_MODEL), lambda i: (i, 0))],
        semantics=("arbitrary",),
    )(proj, proj, rope_tab, rope_tab, w_spatial, bias_full, sink_rows)


def _trunk_kernel(x, target, cat, vecs, chip_idx, gathered, local):
    seq = x.shape[0]
    tm = 256
    nj = D_FF // D_MODEL
    out_rows = D_MODEL // N_CHIPS

    def body(chip_ref, x_ref, t_ref, cat_ref, v_ref, g_out, g_w1, g_w2, l_out, l_w1, l_w2,
             dx1_ref, dcat_ref, dmix_ref, h2_ref, r_ref, da_ref, dff_ref, sums_ref,
             wout, w1, w2, a_scr, sem):
        i = pl.program_id(0)

        @pl.when(i == 0)
        def _():
            waits = _load_chip_blocks(chip_ref, g_out, l_out,
                                      [wout.at[pl.ds(out_rows * k, out_rows)] for k in range(N_CHIPS)], sem)
            waits += _load_chip_blocks(chip_ref, g_w1, l_w1, [w1.at[k] for k in range(N_CHIPS)], sem, N_CHIPS)
            waits += _load_chip_blocks(chip_ref, g_w2, l_w2, [w2.at[k] for k in range(N_CHIPS)], sem, 2 * N_CHIPS)
            for wait in waits:
                wait()
            sums_ref[...] = jnp.zeros_like(sums_ref)

        gate1, shift2, scale2 = v_ref[0:1, :], v_ref[1:2, :], v_ref[2:3, :]
        gate2, g_ffn, g_final = v_ref[3:4, :], v_ref[4:5, :], v_ref[5:6, :]

        mix = _dot(cat_ref[...], wout[...])
        x1 = x_ref[...] + gate1 * mix
        rstd2 = lax.rsqrt(_mean_last(x1 * x1) + EPS)
        xh2 = x1 * rstd2
        n2 = xh2 * g_ffn
        h2b = (n2 * (1.0 + scale2) + shift2).astype(MXU_DTYPE)
        h2_ref[...] = h2b
        ff = jnp.zeros((tm, D_MODEL), F32)
        for j in range(nj):
            a = _dot(h2b, w1[j])
            a_scr[j] = a
            relu = jnp.maximum(a, 0.0)
            rb = (relu * relu).astype(MXU_DTYPE)
            r_ref[:, D_MODEL * j:D_MODEL * (j + 1)] = rb
            ff = ff + _dot(rb, w2[j])
        x2 = x1 + gate2 * ff
        rstd3 = lax.rsqrt(_mean_last(x2 * x2) + EPS)
        xh3 = x2 * rstd3
        err = xh3 * g_final - t_ref[...]
        loss = 0.5 * _rowsum(_mean_last(err * err))
        dy = err * (1.0 / D_MODEL)
        dxh3 = dy * g_final
        dx2 = rstd3 * (dxh3 - xh3 * _mean_last(dxh3 * xh3))
        dffb = (dx2 * gate2).astype(MXU_DTYPE)
        dff_ref[...] = dffb
        dh2 = jnp.zeros((tm, D_MODEL), F32)
        for j in range(nj):
            dr = _dot_nt(dffb, w2[j])
            dab = (dr * (2.0 * jnp.maximum(a_scr[j], 0.0))).astype(MXU_DTYPE)
            da_ref[:, D_MODEL * j:D_MODEL * (j + 1)] = dab
            dh2 = dh2 + _dot_nt(dab, w1[j])
        dn2 = dh2 * (1.0 + scale2)
        dxh2 = dn2 * g_ffn
        dx1 = dx2 + rstd2 * (dxh2 - xh2 * _mean_last(dxh2 * xh2))
        dx1_ref[...] = dx1
        dmixb = (dx1 * gate1).astype(MXU_DTYPE)
        dmix_ref[...] = dmixb
        dcat_ref[...] = _dot_nt(dmixb, wout[...])

        sums_ref[0:1, :] += _rowsum(dh2)
        sums_ref[1:2, :] += _rowsum(dh2 * n2)
        sums_ref[2:3, :] += _rowsum(dx2 * ff)
        sums_ref[3:4, :] += _rowsum(dn2 * xh2)
        sums_ref[4:5, :] += _rowsum(dy * xh3)
        sums_ref[5:6, :] += _rowsum(dx1 * mix)
        sums_ref[6:7, :] += jnp.broadcast_to(loss, (1, D_MODEL))

    tok = lambda w: pl.BlockSpec((tm, w), lambda i, chip: (i, 0))
    return _hosted_call(
        body, None, name="trunk", grid=(seq // tm,), n_prefetch=1,
        out_shape=[jax.ShapeDtypeStruct((seq, D_MODEL), F32), jax.ShapeDtypeStruct((seq, D_MODEL), F32),
                   jax.ShapeDtypeStruct((seq, D_MODEL), MXU_DTYPE), jax.ShapeDtypeStruct((seq, D_MODEL), MXU_DTYPE),
                   jax.ShapeDtypeStruct((seq, D_FF), MXU_DTYPE), jax.ShapeDtypeStruct((seq, D_FF), MXU_DTYPE),
                   jax.ShapeDtypeStruct((seq, D_MODEL), MXU_DTYPE), jax.ShapeDtypeStruct((8, D_MODEL), F32)],
        in_specs=[tok(D_MODEL), tok(D_MODEL), tok(D_MODEL), _full((8, D_MODEL))] + [_any()] * 6,
        out_specs=[tok(D_MODEL), tok(D_MODEL), tok(D_MODEL), tok(D_MODEL), tok(D_FF), tok(D_FF), tok(D_MODEL),
                   _full((8, D_MODEL))],
        scratch_shapes=[pltpu.VMEM((D_MODEL, D_MODEL), MXU_DTYPE), pltpu.VMEM((nj, D_MODEL, D_MODEL), MXU_DTYPE),
                        pltpu.VMEM((nj, D_MODEL, D_MODEL), MXU_DTYPE), pltpu.VMEM((nj, tm, D_MODEL), F32),
                        pltpu.SemaphoreType.DMA((3 * N_CHIPS,))],
        semantics=("arbitrary",),
    )(chip_idx, x, target, cat, vecs, *gathered, *local)


def _mixer_bwd_kernel(proj, rope_tab, dcat, w_spatial, w_spatial_t, bias_full, sink_rows, dev_idx, comm=None):
    seq = proj.shape[0]
    per = MIXER_BLOCKS_PER_STEP
    steps = seq // (CHUNK * per)
    kv_col = KV_START // (2 * KV_WIDTH)

    def body(dev_ref, proj_ref, prev_ref, tab_ref, ptab_ref, dcat_ref, w_ref, wt_ref, bias_ref, sink_ref,
             dproj_ref, dw_out, db_ref, dsink_ref, carry, dw_ref):
        del dev_ref
        step = pl.program_id(0)

        @pl.when(step == 0)
        def _():
            carry[...] = jnp.zeros_like(carry)
            dw_ref[...] = jnp.zeros_like(dw_ref)
            db_ref[...] = jnp.zeros_like(db_ref)
            dsink_ref[...] = jnp.zeros_like(dsink_ref)

        for s in reversed(range(per)):
            rows = pl.ds(CHUNK * s, CHUNK)
            if s == 0:
                before, before_tab, first = prev_ref, ptab_ref, step == steps - 1
            else:
                before = proj_ref.at[pl.ds(CHUNK * (s - 1), CHUNK), pl.ds(KV_START, 2 * KV_WIDTH)]
                before_tab, first = tab_ref.at[pl.ds(CHUNK * (s - 1), CHUNK)], None
            one_block(proj_ref.at[rows], before, tab_ref.at[rows], before_tab, dcat_ref.at[rows], w_ref, wt_ref,
                      bias_ref, sink_ref, dproj_ref.at[rows], dw_ref, db_ref, dsink_ref, carry, first)

        @pl.when(step == steps - 1)
        def _():
            dw_out[...] = dw_ref[...].astype(dw_out.dtype)

    def one_block(proj_ref, prev_ref, tab_ref, ptab_ref, dcat_ref, w_ref, wt_ref, bias_ref, sink_ref,
                  dproj_ref, dw_ref, db_ref, dsink_ref, carry, first):
        wm, tril, triu = _masked_spatial(w_ref)
        lo, hi = _lane_masks((CHUNK, LANES))
        lane = lax.broadcasted_iota(jnp.int32, (CHUNK, LANES), 1)
        db = jnp.zeros((CHUNK, LANES), F32)
        for j in range(GMLP_GROUPS // 2):
            cols = slice(LANES * j, LANES * (j + 1))
            vcols = slice(GMLP_WIDTH + LANES * j, GMLP_WIDTH + LANES * (j + 1))
            zu, zv = proj_ref[:, cols], proj_ref[:, vcols]
            u, tu = _gelu_tanh(zu)
            vp, tv = _gelu_tanh(zv)
            sv = _sgu_forward_pair(wm, vp, j) + bias_ref[:, cols]
            dout = dcat_ref[:, cols]
            du = dout * sv
            dsv = dout * u
            dsv_lo, dsv_hi = jnp.where(lo, dsv, 0.0), jnp.where(hi, dsv, 0.0)
            lhs_t = jnp.concatenate([jnp.where(triu, wt_ref[2 * j], 0.0),
                                     jnp.where(triu, wt_ref[2 * j + 1], 0.0)], axis=1)
            dv = _dot(lhs_t, jnp.concatenate([dsv_lo, dsv_hi], axis=0))
            dw_ref[2 * j] += jnp.where(tril, _dot_nt(dsv_lo, vp), 0.0)
            dw_ref[2 * j + 1] += jnp.where(tril, _dot_nt(dsv_hi, vp), 0.0)
            db = db + (jnp.where(lane == 2 * j, jnp.sum(dsv_lo, axis=1, keepdims=True), 0.0)
                       + jnp.where(lane == 2 * j + 1, jnp.sum(dsv_hi, axis=1, keepdims=True), 0.0))
            dproj_ref[:, cols] = (du * _gelu_tanh_grad(zu, tu)).astype(dproj_ref.dtype)
            dproj_ref[:, vcols] = (dv * _gelu_tanh_grad(zv, tv)).astype(dproj_ref.dtype)
        db_ref[...] += db
        o = 2 * GMLP_WIDTH
        tab = tab_ref[...]
        q_r = _rope_apply(proj_ref[:, o:o + ATTN_WIDTH], tab, 1.0)
        k_cur = _rope_apply(proj_ref[:, o + ATTN_WIDTH:o + ATTN_WIDTH + KV_WIDTH], tab, 1.0)
        k_prev = _rope_apply(prev_ref[:, 0:KV_WIDTH], ptab_ref[...], 1.0)
        k_a = jnp.concatenate([k_prev, k_cur], axis=0)
        v_a = jnp.concatenate([prev_ref[:, KV_WIDTH:2 * KV_WIDTH],
                               proj_ref[:, o + ATTN_WIDTH + KV_WIDTH:o + ATTN_WIDTH + 2 * KV_WIDTH]], axis=0)
        k_b = pltpu.roll(k_a, HEAD_DIM, 1)
        v_b = pltpu.roll(v_a, HEAD_DIM, 1)
        bias_t = _attn_bias_t(first)
        lo2, _ = _lane_masks((2 * CHUNK, LANES))
        dout_b = dcat_ref[:, GMLP_WIDTH:GMLP_WIDTH + ATTN_WIDTH]
        dk_tot, dv_tot, dq_pairs = [], [], []
        for g in range(N_KV_HEADS):
            k_dup, v_dup = _group_dup(k_a, k_b, g, lo2), _group_dup(v_a, v_b, g, lo2)
            q_rows = _group_rows(q_r, g, lo, hi)
            do_rows = _group_rows(dout_b, g, lo, hi)
            p_t, p_sink = _attn_probs_t(k_dup, q_rows, bias_t, _sink_row(sink_ref, g))
            dp_t = _dot_nt(v_dup, do_rows)
            delta = jnp.sum(p_t * dp_t, axis=0, keepdims=True)
            ds_t = p_t * (dp_t - delta) * ATTN_SCALE
            dsink = -p_sink * delta
            for r in range(HEADS_PER_GROUP):
                h = HEADS_PER_GROUP * g + r
                dsink_ref[h:h + 1, :] += jnp.broadcast_to(
                    jnp.sum(dsink[:, LANES * r:LANES * (r + 1)], axis=1, keepdims=True), (1, LANES))
            dk_full = _dot(ds_t, q_rows)
            dv_full = _dot(p_t, do_rows)
            dk_tot.append(dk_full + pltpu.roll(dk_full, HEAD_DIM, 1))
            dv_tot.append(dv_full + pltpu.roll(dv_full, HEAD_DIM, 1))
            dq_t = _dot(k_dup.T, ds_t)
            dq_pairs += _pairs_from_rows(dq_t.T, lo)
        dk_all = jnp.where(lo2, dk_tot[0], dk_tot[1])
        dv_all = jnp.where(lo2, dv_tot[0], dv_tot[1])
        dk_cur = dk_all[CHUNK:, :] + carry[:, 0:KV_WIDTH]
        dv_cur = dv_all[CHUNK:, :] + carry[:, KV_WIDTH:2 * KV_WIDTH]
        carry[:, 0:KV_WIDTH] = dk_all[:CHUNK, :]
        carry[:, KV_WIDTH:2 * KV_WIDTH] = dv_all[:CHUNK, :]
        dq = _rope_apply(jnp.concatenate(dq_pairs, axis=1), tab, -1.0)
        dproj_ref[:, o:o + ATTN_WIDTH] = dq.astype(dproj_ref.dtype)
        dproj_ref[:, o + ATTN_WIDTH:o + ATTN_WIDTH + KV_WIDTH] = (
            _rope_apply(dk_cur, tab, -1.0).astype(dproj_ref.dtype))
        dproj_ref[:, o + ATTN_WIDTH + KV_WIDTH:o + ATTN_WIDTH + 2 * KV_WIDTH] = dv_cur.astype(dproj_ref.dtype)

    rev = lambda i: steps - 1 - i
    before = lambda i: jnp.maximum(per * rev(i) - 1, 0)
    slot = lambda shape: pl.BlockSpec((None,) + shape, lambda i, d: (d[0],) + (0,) * len(shape))
    return _hosted_call(
        body, comm, name="mixer_bwd", grid=(steps,), n_prefetch=1,
        out_shape=[jax.ShapeDtypeStruct((seq, IN_PROJ_WIDTH), MXU_DTYPE),
                   jax.ShapeDtypeStruct((N_DEV, GMLP_GROUPS, CHUNK, CHUNK), GRAD_COMM_DTYPE),
                   jax.ShapeDtypeStruct((N_DEV, CHUNK, LANES), F32),
                   jax.ShapeDtypeStruct((N_DEV, N_Q_HEADS, LANES), F32)],
        in_specs=[pl.BlockSpec((CHUNK * per, IN_PROJ_WIDTH), lambda i, d: (rev(i), 0)),
                  pl.BlockSpec((CHUNK, 2 * KV_WIDTH), lambda i, d: (before(i), kv_col)),
                  pl.BlockSpec((CHUNK * per, 3 * LANES), lambda i, d: (rev(i), 0)),
                  pl.BlockSpec((CHUNK, 3 * LANES), lambda i, d: (before(i), 0)),
                  pl.BlockSpec((CHUNK * per, D_MODEL), lambda i, d: (rev(i), 0)),
                  _full((GMLP_GROUPS, CHUNK, CHUNK)), _full((GMLP_GROUPS, CHUNK, CHUNK)),
                  _full((CHUNK, GMLP_WIDTH)), _full((N_Q_HEADS, LANES))],
        out_specs=[pl.BlockSpec((CHUNK * per, IN_PROJ_WIDTH), lambda i, d: (rev(i), 0)),
                   slot((GMLP_GROUPS, CHUNK, CHUNK)), slot((CHUNK, LANES)), slot((N_Q_HEADS, LANES))],
        scratch_shapes=[pltpu.VMEM((CHUNK, 2 * KV_WIDTH), F32), pltpu.VMEM((GMLP_GROUPS, CHUNK, CHUNK), F32)],
        semantics=("arbitrary",),
    )(dev_idx, proj, proj, rope_tab, rope_tab, dcat, w_spatial, w_spatial_t, bias_full, sink_rows)


def _in_proj_bwd_kernel(x, dx1, dproj, vecs, w_in_t, comm=None):
    seq = x.shape[0]
    tm = 512

    def body(x_ref, dx1_ref, dp_ref, v_ref, w_ref, gx_ref, sums_ref):
        @pl.when(pl.program_id(0) == 0)
        def _():
            sums_ref[...] = jnp.zeros_like(sums_ref)

        g_mix, scale1 = v_ref[0:1, :], v_ref[2:3, :]
        dh = _dot(dp_ref[...], w_ref[...])
        xv = x_ref[...]
        rstd = lax.rsqrt(_mean_last(xv * xv) + EPS)
        xh = xv * rstd
        dn1 = dh * (1.0 + scale1)
        dxh = dn1 * g_mix
        gx_ref[...] = dx1_ref[...] + rstd * (dxh - xh * _mean_last(dxh * xh))
        sums_ref[0:1, :] += _rowsum(dh)
        sums_ref[1:2, :] += _rowsum(dh * (xh * g_mix))
        sums_ref[2:3, :] += _rowsum(dn1 * xh)

    tok = lambda w: pl.BlockSpec((tm, w), lambda i: (i, 0))
    return _hosted_call(
        body, comm, name="in_proj_bwd", grid=(seq // tm,),
        out_shape=[jax.ShapeDtypeStruct((seq, D_MODEL), F32), jax.ShapeDtypeStruct((8, D_MODEL), F32)],
        in_specs=[tok(D_MODEL), tok(D_MODEL), tok(IN_PROJ_WIDTH), _full((8, D_MODEL)),
                  _full((IN_PROJ_WIDTH, D_MODEL))],
        out_specs=[tok(D_MODEL), _full((8, D_MODEL))],
        semantics=("arbitrary",),
    )(x, dx1, dproj, vecs, w_in_t)


class _GradTiles(NamedTuple):
    tm: int
    tn: int
    n_tiles: int
    chips_per_tile: int
    a_index: Callable
    b_index: Callable


def _weight_grad_kernel(a, b, c_idx, name, tiles, comm=None):
    seq = a.shape[0]
    tk = min(seq, 4096)
    nk = seq // tk
    tm, tn, n_tiles, per = tiles.tm, tiles.tn, tiles.n_tiles, tiles.chips_per_tile
    rows = tm // per

    def half(phase, c):
        return phase * c[0] + (1 - phase) * (1 - c[0])

    def body(c_ref, a_ref, b_ref, o_ref, acc, stage, landed, send_sems, recv_sems):
        del c_ref
        phase, t, kk = pl.program_id(0), pl.program_id(1), pl.program_id(2)
        x, y, c, _ = _mesh_place()

        def copy(tile):
            return pltpu.make_async_remote_copy(
                src_ref=stage.at[tile], dst_ref=landed.at[tile], send_sem=send_sems.at[tile],
                recv_sem=recv_sems.at[tile], device_id=(x, y, 1 - c), device_id_type=MESH)

        @pl.when(kk == 0)
        def _():
            acc[...] = jnp.zeros_like(acc)

        acc[...] += _dot_tn(a_ref[...], b_ref[...])

        @pl.when((kk == nk - 1) & (phase == 0))
        def _():
            stage[t] = acc[...].astype(stage.dtype)
            copy(t).start()

        @pl.when((kk == nk - 1) & (phase == 1))
        def _():
            copy(t).wait_recv()
            total = acc[...] + landed[t].astype(F32)
            for q in range(per):
                o_ref[q] = total[rows * q:rows * (q + 1)].astype(o_ref.dtype)

        @pl.when((kk == nk - 1) & (phase == 1) & (t == n_tiles - 1))
        def _():
            for tile in range(n_tiles):
                copy(tile).wait_send()

    out = _hosted_call(
        body, comm, name=name, grid=(2, n_tiles, nk), n_prefetch=1,
        out_shape=[jax.ShapeDtypeStruct((n_tiles * per, rows, tn), GRAD_COMM_DTYPE)],
        in_specs=[pl.BlockSpec((tk, tm), lambda p, t, k, c: (k, tiles.a_index(t, half(p, c)))),
                  pl.BlockSpec((tk, tn), lambda p, t, k, c: (k, tiles.b_index(t, half(p, c))))],
        out_specs=[pl.BlockSpec((per, rows, tn), lambda p, t, k, c: (p * t, 0, 0))],
        scratch_shapes=[pltpu.VMEM((tm, tn), F32), pltpu.VMEM((n_tiles, tm, tn), GRAD_COMM_DTYPE),
                        pltpu.VMEM((n_tiles, tm, tn), GRAD_COMM_DTYPE),
                        pltpu.SemaphoreType.DMA((n_tiles,)), pltpu.SemaphoreType.DMA((n_tiles,))],
        semantics=("arbitrary", "arbitrary", "arbitrary"),
    )(c_idx, a, b)
    return out[0] if comm is None else out


def _row_tile(rows, most=256, sublanes=16):
    return max(t for t in range(sublanes, most + 1, sublanes) if rows % t == 0)


def _adam_update(w, g, m, v):
    m_new = ADAM_B1 * m + (1.0 - ADAM_B1) * g
    v_new = ADAM_B2 * v + (1.0 - ADAM_B2) * (g * g)
    m_hat = m_new / (1.0 - ADAM_B1 ** ADAM_STEP)
    v_hat = v_new / (1.0 - ADAM_B2 ** ADAM_STEP)
    delta = -ADAM_LR * (m_hat / (jnp.sqrt(v_hat) + ADAM_EPS) + ADAM_WD * w)
    return delta, m_new, v_new


def _sum_chips_kernel(own, others, place, name):
    _, r, n = own.shape
    tr = _row_tile(r)

    def body(place_ref, own_ref, oth_ref, o_ref):
        del place_ref
        acc = own_ref[...].astype(F32)
        for k in range(N_CHIPS - 1):
            acc = acc + oth_ref[k].astype(F32)
        o_ref[...] = acc

    return pl.pallas_call(
        body, name=name, out_shape=jax.ShapeDtypeStruct((2, r, n), F32),
        grid_spec=pltpu.PrefetchScalarGridSpec(
            num_scalar_prefetch=1, grid=(r // tr,),
            in_specs=[pl.BlockSpec((None, tr, n), lambda i, p: (p[0], i, 0)),
                      pl.BlockSpec((N_CHIPS - 1, tr, n), lambda i, p: (0, i, 0))],
            out_specs=pl.BlockSpec((None, tr, n), lambda i, p: (p[1], i, 0))),
        compiler_params=_params("parallel"),
    )(place, own, others)


def _adam_kernel(w, g, m, v, name):
    r, n = w.shape
    by_columns = g.shape[1] == r
    tr, tn = _row_tile(g.shape[1], most=512), g.shape[2]

    def body(w_ref, g_ref, m_ref, v_ref, g_out, d_ref, mo_ref, vo_ref):
        gv = g_ref[...]
        g_out[...] = gv
        d_ref[...], mo_ref[...], vo_ref[...] = _adam_update(w_ref[...], gv, m_ref[...], v_ref[...])

    steps = g.shape[1] // tr
    spec = pl.BlockSpec((tr, tn), (lambda h, i: (i, h)) if by_columns else (lambda h, i: (h * steps + i, 0)))
    return pl.pallas_call(
        body, name=name, grid=(2, steps), out_shape=[jax.ShapeDtypeStruct((r, n), F32)] * 4,
        in_specs=[spec, pl.BlockSpec((None, tr, tn), lambda h, i: (h, i, 0)), spec, spec], out_specs=[spec] * 4,
        compiler_params=_params("parallel", "parallel"),
    )(w, g, m, v)


SMALL_PARAMS = ("b_ada", "g_mix", "g_ffn", "g_final", "b_spatial", "sinks", "w_spatial")


def _small_update_kernel(gathered, params):
    shapes = [params[nm][0].shape for nm in SMALL_PARAMS]

    def body(*refs):
        g_refs, refs = refs[:5], refs[5:]
        p_refs, refs = refs[:3 * len(SMALL_PARAMS)], refs[3 * len(SMALL_PARAMS):]
        loss_ref, o_refs = refs[0], refs[1:]

        def total(ref):
            acc = ref[0].astype(F32)
            for k in range(1, N_DEV):
                acc = acc + ref[k].astype(F32)
            return acc

        s1, s2, db, ds, dw = (total(r) for r in g_refs)
        loss_ref[...] = jnp.broadcast_to(s2[6:7, 0:1], loss_ref.shape)
        grads = {"b_ada": [s1[0:1], s1[1:2], s2[5:6], s2[0:1], s2[1:2], s2[2:3]], "g_mix": [s1[2:3]],
                 "g_ffn": [s2[3:4]], "g_final": [s2[4:5]], "b_spatial": [db.T[0:GMLP_GROUPS]],
                 "w_spatial": [dw]}
        lane = lax.broadcasted_iota(jnp.int32, (1, LANES), 1)
        sink_row = jnp.zeros((1, LANES), F32)
        for h in range(N_Q_HEADS):
            sink_row = sink_row + jnp.where(lane == h, ds[h:h + 1, :], 0.0)
        grads["sinks"] = [sink_row[:, 0:N_Q_HEADS]]
        for i, nm in enumerate(SMALL_PARAMS):
            w_ref, m_ref, v_ref = p_refs[3 * i:3 * i + 3]
            outs = o_refs[4 * i:4 * i + 4]
            width = grads[nm][0].shape[1]
            for k, g in enumerate(grads[nm]):
                cols = slice(width * k, width * (k + 1))
                upd = _adam_update(w_ref[:, cols], g, m_ref[:, cols], v_ref[:, cols])
                for o_ref, val in zip(outs, (g,) + upd):
                    o_ref[:, cols] = val

    flat = [a for nm in SMALL_PARAMS for a in params[nm]]
    out_shape = [jax.ShapeDtypeStruct((8, LANES), F32)]
    out_shape += [jax.ShapeDtypeStruct(s, F32) for s in shapes for _ in range(4)]
    outs = pl.pallas_call(
        body, name="small_update", grid=(1,), out_shape=out_shape,
        in_specs=[_full(g.shape) for g in gathered] + [_full(a.shape) for a in flat],
        out_specs=[_full(s.shape) for s in out_shape],
        compiler_params=_params("arbitrary"),
    )(*gathered, *flat)
    return {nm: outs[1 + 4 * i:5 + 4 * i] for i, nm in enumerate(SMALL_PARAMS)}, outs[0]


def _ada_update_kernel(act_t, dmod, w, m, v):
    r, n = w.shape
    tr = 256

    def body(a_ref, d_ref, w_ref, m_ref, v_ref, g_ref, dl_ref, mo_ref, vo_ref):
        g = _dot(a_ref[...], d_ref[...])
        g_ref[...] = g
        dl_ref[...], mo_ref[...], vo_ref[...] = _adam_update(w_ref[...], g, m_ref[...], v_ref[...])

    spec = pl.BlockSpec((tr, n), lambda i: (i, 0))
    return pl.pallas_call(
        body, name="ada_update", grid=(r // tr,), out_shape=[jax.ShapeDtypeStruct((r, n), F32)] * 4,
        in_specs=[pl.BlockSpec((tr, N_DEV), lambda i: (i, 0)), _full((N_DEV, n)), spec, spec, spec],
        out_specs=[spec] * 4, compiler_params=_params("parallel"),
    )(act_t, dmod, w, m, v)


def kernel(x, c, positions, w_ada, b_ada, g_mix, w_in, w_spatial, b_spatial, sinks, w_out, g_ffn, w_ff1, w_ff2, g_final, loss_target, m_w_ada, m_b_ada, m_g_mix, m_w_in, m_w_spatial, m_b_spatial, m_sinks, m_w_out, m_g_ffn, m_w_ff1, m_w_ff2, m_g_final, v_w_ada, v_b_ada, v_g_mix, v_w_in, v_w_spatial, v_b_spatial, v_sinks, v_w_out, v_g_ffn, v_w_ff1, v_w_ff2, v_g_final):
    xi, yi, ci = lax.axis_index("x"), lax.axis_index("y"), lax.axis_index("c")
    chip = 2 * xi + yi
    dev = 2 * chip + ci
    seq = x.shape[1]
    x2, tgt = x[0], loss_target[0]
    ada_cols = w_ada.shape[2]

    big = {"w_in": tuple(a[0].T for a in (w_in, m_w_in, v_w_in)),
           "w_out": (w_out[0], m_w_out[0], v_w_out[0]), "w_ff1": (w_ff1[0], m_w_ff1[0], v_w_ff1[0]),
           "w_ff2": (w_ff2[0], m_w_ff2[0], v_w_ff2[0])}

    def halves(nm):
        r, n = big[nm][0].shape
        return big[nm][0].astype(WEIGHT_COMM_DTYPE).reshape(2, r // 2, n)

    chip_idx = chip.reshape(1).astype(jnp.int32)
    c_all, w_in_t, g_out = _all_gather8([c, halves("w_in"), halves("w_out")], "gather_first",
                                        split=[False, True, True], skip_own=(2,))
    c_all, w_in_t = c_all.reshape(N_DEV, D_MODEL), w_in_t.reshape(IN_PROJ_WIDTH, D_MODEL)
    b_shard = lax.dynamic_slice(b_ada, (0, chip * ada_cols), (1, ada_cols))
    mod_part, act = _mod_kernel(c_all, w_ada[0], b_shard)
    mod_all, = _all_gather8([mod_part], "gather_mod")
    mod_me = lax.dynamic_index_in_dim(mod_all[0::2], dev, axis=1, keepdims=False)
    mod_me = mod_me.reshape(N_MOD, D_MODEL)
    shift1, scale1, gate1, shift2, scale2, gate2 = (mod_me[k:k + 1] for k in range(N_MOD))

    zeros_row = jnp.zeros((1, D_MODEL), F32)
    vecs1 = jnp.concatenate([g_mix, shift1, scale1] + [zeros_row] * 5, axis=0)
    vecs2 = jnp.concatenate([gate1, shift2, scale2, gate2, g_ffn, g_final.reshape(1, D_MODEL)]
                            + [zeros_row] * 2, axis=0)
    bias_full = jnp.repeat(b_spatial[0].T, HEAD_DIM, axis=1)
    sink_rows = jnp.broadcast_to(sinks[0][:, None], (N_Q_HEADS, LANES))
    inv_freq = ROPE_THETA ** (-jnp.arange(0, ROT_DIM, 2, dtype=F32) / ROT_DIM)
    rope_tab = _rope_lane_tables(*_rope_angle_kernel(positions, inv_freq.reshape(ROT_DIM // 2, 1)))

    trunk_weights = ["w_out", "w_ff1", "w_ff2"]
    shards = [halves(nm) for nm in trunk_weights]
    proj, hb, *staged = _in_proj_kernel(x2, vecs1, w_in_t, comm=_gather2d_first(shards[1:]))
    cat, *staged = _mixer_fwd_kernel(proj, rope_tab, w_spatial[0], bias_full, sink_rows,
                                     comm=_gather2d_second(staged, shards[1:]))
    staged = [g_out] + list(_gather_forward(staged, "gather_forward"))
    dx1, dcat, dmix, h2b, rb, dab, dffb, sums2 = _trunk_kernel(
        x2, tgt, cat, vecs2, chip_idx,
        [g.reshape((N_CHIPS,) + big[nm][0].shape) for nm, g in zip(trunk_weights, staged)],
        [s.reshape(big[nm][0].shape) for nm, s in zip(trunk_weights, shards)])

    c_idx = ci.reshape(1).astype(jnp.int32)
    place = jnp.stack([chip, ci]).astype(jnp.int32)
    half_d = D_MODEL // 2
    cs_ff2 = _weight_grad_kernel(rb, dffb, c_idx, "dw_ff2",
                                 _GradTiles(D_MODEL, half_d, N_CHIPS, 1, lambda t, h: t, lambda t, h: h))
    eighth = D_MODEL // 8
    cs_ff1, sc_ff2 = _weight_grad_kernel(
        h2b, dab, c_idx, "dw_ff1",
        _GradTiles(D_MODEL, half_d, N_CHIPS, 1, lambda t, h: 0, lambda t, h: 2 * t + h),
        comm=_scatter_job([cs_ff2], rows=(0, 6 * eighth)))
    cs_out, sc_ff2 = _weight_grad_kernel(
        cat, dmix, c_idx, "dw_out", _GradTiles(D_MODEL, half_d, 1, N_CHIPS, lambda t, h: 0, lambda t, h: h),
        comm=_scatter_job([cs_ff2], rows=(6 * eighth, eighth), into=[sc_ff2]))
    dproj, dw_spatial, db_lanes, dsink_rows, sc_ff2, sc_ff1, sc_out = _mixer_bwd_kernel(
        proj, rope_tab, dcat, w_spatial[0], w_spatial[0].transpose(0, 2, 1), bias_full, sink_rows,
        dev.reshape(1).astype(jnp.int32),
        comm=_merge_jobs(_scatter_job([cs_ff1, cs_out]),
                         _scatter_job([cs_ff2], rows=(7 * eighth, eighth), into=[sc_ff2])))
    totals = [_sum_chips_kernel(own, oth, place, "grad_sum_" + nm)
              for nm, own, oth in (("w_out", cs_out, sc_out), ("w_ff1", cs_ff1, sc_ff1), ("w_ff2", cs_ff2, sc_ff2))]
    small_slots = [db_lanes, dsink_rows, dw_spatial.reshape(N_DEV, GMLP_GROUPS * CHUNK, CHUNK)]
    cs_in, *rode = _weight_grad_kernel(
        dproj, hb, c_idx, "dw_in",
        _GradTiles(2 * W_IN_BLOCK, half_d, N_CHIPS // 2, 2, lambda t, h: t, lambda t, h: h),
        comm=_merge_jobs(_gather_job(small_slots), _share_job(totals)))
    small_stage1, shared = rode[:len(small_slots)], rode[len(small_slots):]
    scatter_in = _scatter_start(cs_in, "grad_to_chips_w_in_start")
    grad_x, sums1 = _in_proj_bwd_kernel(x2, dx1, dproj, vecs1 + scatter_in[-1][0:1, 0:1], w_in_t)
    cs_in, sc_in = _scatter_wait(scatter_in, sums1, "grad_to_chips_w_in_wait")
    gathered = _all_gather8([sums1, sums2], "gather_small", forward=small_stage1)
    total_in = _sum_chips_kernel(cs_in, sc_in, place, "grad_sum_w_in")
    shared = list(_sibling_share([total_in], "grad_share_w_in")) + list(shared)
    names = ["w_in", "w_out", "w_ff1", "w_ff2"]
    big_out = {}
    for nm, g in zip(names, shared):
        w, m, v = big[nm]
        outs = _adam_kernel(w, g, m, v, "adam_" + nm)
        big_out[nm] = tuple((t.T if nm == "w_in" else t)[None] for t in outs)

    small = {"b_ada": (b_ada, m_b_ada, v_b_ada), "g_mix": (g_mix, m_g_mix, v_g_mix),
             "g_ffn": (g_ffn, m_g_ffn, v_g_ffn), "g_final": (g_final, m_g_final, v_g_final),
             "b_spatial": (b_spatial, m_b_spatial, v_b_spatial), "sinks": (sinks, m_sinks, v_sinks),
             "w_spatial": (w_spatial, m_w_spatial, v_w_spatial)}
    flat_shape = {"g_final": (1, D_MODEL), "b_spatial": (GMLP_GROUPS, CHUNK), "w_spatial": (GMLP_GROUPS * CHUNK, CHUNK)}
    small_out, loss_tile = _small_update_kernel(
        gathered, {nm: tuple(a.reshape(flat_shape.get(nm, a.shape)) for a in small[nm]) for nm in small})
    small_out = {nm: [o.reshape(small[nm][0].shape) for o in small_out[nm]] for nm in small}
    loss = loss_tile[0, 0]

    g1, g2 = gathered[0], gathered[1]
    dmod_all = jnp.concatenate([g1[:, 0], g1[:, 1], g2[:, 5], g2[:, 0], g2[:, 1], g2[:, 2]], axis=1)
    dmod_cols = lax.dynamic_slice(dmod_all, (0, chip * ada_cols), (N_DEV, ada_cols))
    ada = _ada_update_kernel(act.T, dmod_cols, w_ada[0], m_w_ada[0], v_w_ada[0])
    big_out["w_ada"] = tuple(t[None] for t in ada)

    order = ["w_ada", "b_ada", "g_mix", "w_in", "w_spatial", "b_spatial", "sinks", "w_out", "g_ffn",
             "w_ff1", "w_ff2", "g_final"]

    def leaf(nm, k):
        return big_out[nm][k] if nm in big_out else small_out[nm][k]

    outs = [loss, grad_x[None]]
    for k in range(4):
        outs += [leaf(nm, k) for nm in order]
    return tuple(outs)
```

```python
import math
from typing import Callable, NamedTuple

import jax
import jax.numpy as jnp
from jax import lax
from jax.experimental import pallas as pl
from jax.experimental.pallas import tpu as pltpu

F32 = jnp.float32
MXU_DTYPE = jnp.bfloat16
WEIGHT_COMM_DTYPE = jnp.bfloat16
GRAD_COMM_DTYPE = jnp.bfloat16

D_MODEL = 1024
D_FF = 4096
HEAD_DIM = 64
GMLP_GROUPS = 8
GMLP_WIDTH = 512
CHUNK = 128
N_Q_HEADS = 8
N_KV_HEADS = 2
ATTN_WIDTH = 512
KV_WIDTH = 128
ROT_DIM = 16
ROPE_THETA = 500000.0
IN_PROJ_WIDTH = 1792
N_MOD = 6
EPS = 1e-5
N_CHIPS = 4
N_DEV = 8
LANES = 128
W_IN_BLOCK = IN_PROJ_WIDTH // N_CHIPS

ADAM_LR = 0.001
ADAM_B1 = 0.9
ADAM_B2 = 0.999
ADAM_EPS = 1e-08
ADAM_WD = 0.01
ADAM_STEP = 10

VMEM_LIMIT_BYTES = 58 * 1024 * 1024
MESH = pl.DeviceIdType.MESH


def _params(*semantics):
    return pltpu.CompilerParams(dimension_semantics=semantics, vmem_limit_bytes=VMEM_LIMIT_BYTES)


def _dot(a, b):
    return jnp.dot(a.astype(MXU_DTYPE), b.astype(MXU_DTYPE), preferred_element_type=F32)


def _dot_nt(a, b):
    return lax.dot_general(a.astype(MXU_DTYPE), b.astype(MXU_DTYPE), (((1,), (1,)), ((), ())),
                           preferred_element_type=F32)


def _dot_tn(a, b):
    return lax.dot_general(a.astype(MXU_DTYPE), b.astype(MXU_DTYPE), (((0,), (0,)), ((), ())),
                           preferred_element_type=F32)


def _full(shape):
    return pl.BlockSpec(shape, lambda *_: (0,) * len(shape))


def _any():
    return pl.BlockSpec(memory_space=pl.ANY)


def _rowsum(v):
    return jnp.sum(v, axis=0, keepdims=True)


def _mean_last(v):
    return jnp.mean(v, axis=-1, keepdims=True)


class _Comm(NamedTuple):
    operands: tuple
    out_shapes: tuple
    n_sems: int
    make: Callable
    in_place: int = 0


def _hosted_call(body, comm, *, name, grid, in_specs, out_shape, out_specs, scratch_shapes=(), semantics,
                 n_prefetch=0):
    if comm is None:
        return pl.pallas_call(
            body, name=name, out_shape=out_shape, compiler_params=_params(*semantics),
            grid_spec=pltpu.PrefetchScalarGridSpec(
                num_scalar_prefetch=n_prefetch, grid=grid, in_specs=in_specs, out_specs=out_specs,
                scratch_shapes=list(scratch_shapes)))
    n_in, n_out, n_scr = len(in_specs), len(out_shape), len(scratch_shapes)
    k_in, k_out = len(comm.operands), len(comm.out_shapes)

    def hosted(*refs):
        prefetched, refs = refs[:n_prefetch], refs[n_prefetch:]
        ins, refs = refs[:n_in], refs[n_in:]
        c_ins, refs = refs[:k_in], refs[k_in:]
        outs, refs = refs[:n_out], refs[n_out:]
        c_outs, refs = refs[:k_out], refs[k_out:]
        scratch, (send_sems, recv_sems) = refs[:n_scr], refs[n_scr:]
        first, last = None, None
        for d, size in enumerate(grid):
            at_start, at_end = pl.program_id(d) == 0, pl.program_id(d) == size - 1
            first = at_start if first is None else first & at_start
            last = at_end if last is None else last & at_end

        @pl.when(first)
        def _():
            for cp in comm.make(c_ins, c_outs, send_sems, recv_sems)[0]:
                cp.start()

        body(*prefetched, *ins, *outs, *scratch)

        @pl.when(last)
        def _():
            for wait in comm.make(c_ins, c_outs, send_sems, recv_sems)[1]:
                wait()

    aliases = {n_prefetch + n_in + i: n_out + i for i in range(comm.in_place)}
    call = pl.pallas_call(
        hosted, name=name, out_shape=list(out_shape) + list(comm.out_shapes),
        compiler_params=_params(*semantics), input_output_aliases=aliases,
        grid_spec=pltpu.PrefetchScalarGridSpec(
            num_scalar_prefetch=n_prefetch, grid=grid, in_specs=list(in_specs) + [_any()] * k_in,
            out_specs=list(out_specs) + [_any()] * k_out,
            scratch_shapes=list(scratch_shapes) + [pltpu.SemaphoreType.DMA((comm.n_sems,)),
                                                    pltpu.SemaphoreType.DMA((comm.n_sems,))]))
    return lambda *args: call(*args, *comm.operands)


class _Shifted:
    def __init__(self, base, offset):
        self.base, self.offset = base, offset

    @property
    def at(self):
        return self

    def __getitem__(self, k):
        return self.base.at[self.offset + k]


def _merge_jobs(*jobs):
    def order(count):
        first = [(j, i) for j, job in enumerate(jobs) for i in range(job.in_place)]
        return first + [(j, i) for j, job in enumerate(jobs) for i in range(job.in_place, count(job))]

    op_order, out_order = order(lambda job: len(job.operands)), order(lambda job: len(job.out_shapes))

    def make(ins, outs, send_sems, recv_sems):
        starts, waits, sem = [], [], 0
        for j, job in enumerate(jobs):
            mine_in = [ins[k] for k, (jj, _) in enumerate(op_order) if jj == j]
            mine_out = [outs[k] for k, (jj, _) in enumerate(out_order) if jj == j]
            s, w = job.make(mine_in, mine_out, _Shifted(send_sems, sem), _Shifted(recv_sems, sem))
            starts, waits, sem = starts + s, waits + w, sem + job.n_sems
        return starts, waits

    return _Comm(tuple(jobs[j].operands[i] for j, i in op_order), tuple(jobs[j].out_shapes[i] for j, i in out_order),
                 sum(job.n_sems for job in jobs), make, in_place=sum(job.in_place for job in jobs))


def _mesh_place():
    x, y, c = lax.axis_index("x"), lax.axis_index("y"), lax.axis_index("c")
    return x, y, c, [(1 - x, y), (x, 1 - y), (1 - x, 1 - y)]


def _gather_job(bufs):
    per = 4

    def make(ins, outs, send_sems, recv_sems):
        del ins
        x, y, c, chips = _mesh_place()
        starts, waits = [], []
        for a, out in enumerate(outs):
            mine = src = out.at[4 * x + 2 * y + c]
            to = [(x, y, 1 - c)] + [(px, py, c) for px, py in chips]
            sends = [pltpu.make_async_remote_copy(
                src_ref=src, dst_ref=mine, send_sem=send_sems.at[per * a + k],
                recv_sem=recv_sems.at[per * a + k], device_id=dev, device_id_type=MESH)
                for k, dev in enumerate(to)]
            recvs = [pltpu.make_async_remote_copy(
                src_ref=src, dst_ref=out.at[4 * px + 2 * py + pc], send_sem=send_sems.at[per * a + k],
                recv_sem=recv_sems.at[per * a + k], device_id=(px, py, pc), device_id_type=MESH)
                for k, (px, py, pc) in enumerate(to)]
            starts += sends
            waits += [s.wait_send for s in sends] + [r.wait_recv for r in recvs]
        return starts, waits

    shapes = tuple(jax.ShapeDtypeStruct(b.shape, b.dtype) for b in bufs)
    return _Comm(tuple(bufs), shapes, per * len(bufs), make, in_place=len(bufs))


def _slots(x, y, c):
    return 4 * x + 2 * y + c, 4 * (1 - x) + 2 * y + c, 4 * x + 2 * (1 - y) + c, 4 * (1 - x) + 2 * (1 - y) + c


def _gather2d_first(halves):
    per = 2

    def make(ins, outs, send_sems, recv_sems):
        x, y, c, _ = _mesh_place()
        me, xn, yn, _ = _slots(x, y, c)
        starts, waits = [], []
        for a, (src, out) in enumerate(zip(ins, outs)):
            blk = src.at[c]
            rows = blk.shape[0] // 2
            upper, lower = pl.ds(0, rows), pl.ds(rows, rows)

            def copy(k, src_ref, dst_ref, dev, a=a):
                return pltpu.make_async_remote_copy(
                    src_ref=src_ref, dst_ref=dst_ref, send_sem=send_sems.at[per * a + k],
                    recv_sem=recv_sems.at[per * a + k], device_id=dev, device_id_type=MESH)

            sends = [copy(0, blk.at[upper], out.at[me, upper], (1 - x, y, c)),
                     copy(1, blk.at[lower], out.at[me, lower], (x, 1 - y, c))]
            recvs = [copy(0, blk.at[upper], out.at[xn, upper], (1 - x, y, c)),
                     copy(1, blk.at[lower], out.at[yn, lower], (x, 1 - y, c))]
            starts += sends
            waits += [s.wait_send for s in sends] + [r.wait_recv for r in recvs]
        return starts, waits

    shapes = tuple(jax.ShapeDtypeStruct((N_DEV,) + h.shape[1:], h.dtype) for h in halves)
    return _Comm(tuple(halves), shapes, per * len(halves), make)


def _gather2d_second(bufs, halves):
    per = 4
    n_arr = len(bufs)

    def make(ins, outs, send_sems, recv_sems):
        x, y, c, _ = _mesh_place()
        me, xn, yn, dg = _slots(x, y, c)
        starts, waits = [], []
        for a, buf in enumerate(outs):
            own = ins[n_arr + a].at[c]
            rows = buf.shape[1] // 2
            upper, lower = pl.ds(0, rows), pl.ds(rows, rows)
            plan = [(own.at[upper], me, upper, (x, 1 - y, c), yn), (buf.at[xn, upper], xn, upper, (x, 1 - y, c), dg),
                    (own.at[lower], me, lower, (1 - x, y, c), xn), (buf.at[yn, lower], yn, lower, (1 - x, y, c), dg)]
            for k, (src, slot, part, dev, landing) in enumerate(plan):
                sems = dict(send_sem=send_sems.at[per * a + k], recv_sem=recv_sems.at[per * a + k],
                            device_id=dev, device_id_type=MESH)
                send = pltpu.make_async_remote_copy(src_ref=src, dst_ref=buf.at[slot, part], **sems)
                arrival = pltpu.make_async_remote_copy(src_ref=src, dst_ref=buf.at[landing, part], **sems)
                starts.append(send)
                waits += [send.wait_send, arrival.wait_recv]
        return starts, waits

    shapes = tuple(jax.ShapeDtypeStruct(b.shape, b.dtype) for b in bufs)
    return _Comm(tuple(bufs) + tuple(halves), shapes, per * n_arr, make, in_place=n_arr)


def _gather_forward(bufs, name):
    n_arr = len(bufs)

    def body(*refs):
        outs = refs[n_arr:2 * n_arr]
        send_sems, recv_sems = refs[2 * n_arr:]
        x, y, c, chips = _mesh_place()
        sends, recvs = [], []
        for a, buf in enumerate(outs):
            for j, (px, py) in enumerate(chips):
                mine, theirs = buf.at[4 * px + 2 * py + c], buf.at[4 * px + 2 * py + 1 - c]
                sems = dict(send_sem=send_sems.at[3 * a + j], recv_sem=recv_sems.at[3 * a + j],
                            device_id=(x, y, 1 - c), device_id_type=MESH)
                sends.append(pltpu.make_async_remote_copy(src_ref=mine, dst_ref=mine, **sems))
                recvs.append(pltpu.make_async_remote_copy(src_ref=mine, dst_ref=theirs, **sems))
        for cp in sends:
            cp.start()
        for s, r in zip(sends, recvs):
            s.wait_send()
            r.wait_recv()

    return pl.pallas_call(
        body, name=name, out_shape=[jax.ShapeDtypeStruct(b.shape, b.dtype) for b in bufs],
        in_specs=[_any()] * n_arr, out_specs=[_any()] * n_arr,
        input_output_aliases={a: a for a in range(n_arr)},
        scratch_shapes=[pltpu.SemaphoreType.DMA((3 * n_arr,)), pltpu.SemaphoreType.DMA((3 * n_arr,))],
    )(*bufs)


def _scatter_job(chip_sums, rows=None, into=()):
    n_into = len(into)

    def part(ref):
        return ref if rows is None else ref.at[pl.ds(rows[0], rows[1])]

    def make(ins, outs, send_sems, recv_sems):
        x, y, c, chips = _mesh_place()
        copies = [pltpu.make_async_remote_copy(
            src_ref=part(src.at[2 * px + py]), dst_ref=part(out.at[j]), send_sem=send_sems.at[3 * a + j],
            recv_sem=recv_sems.at[3 * a + j], device_id=(px, py, c), device_id_type=MESH)
            for a, (src, out) in enumerate(zip(ins[n_into:], outs)) for j, (px, py) in enumerate(chips)]
        return copies, [cp.wait for cp in copies]

    shapes = tuple(jax.ShapeDtypeStruct((3,) + s.shape[1:], s.dtype) for s in chip_sums)
    return _Comm(tuple(into) + tuple(chip_sums), shapes, 3 * len(chip_sums), make, in_place=n_into)


def _all_gather8(blocks, name, split=False, forward=(), riders=(), skip_own=(), direct=()):
    n_arr, n_fwd = len(blocks), len(forward)
    splits = list(split) if isinstance(split, (list, tuple)) else [split] * n_arr
    own_slots = [a not in skip_own for a in range(n_arr)]
    rider_in = sum(len(r.operands) for r in riders)
    rider_out = sum(len(r.out_shapes) for r in riders)

    def body(*refs):
        x_refs, refs = refs[:n_arr], refs[n_arr + n_fwd:]
        r_ins, refs = refs[:rider_in], refs[rider_in:]
        out_refs, refs = refs[:n_arr], refs[n_arr:]
        fwd_refs, refs = refs[:n_fwd], refs[n_fwd:]
        r_outs, refs = refs[:rider_out], refs[rider_out:]
        (send_sems, recv_sems, local_sems), rider_sems = refs[:3], refs[3:]
        x, y, c, chips = _mesh_place()
        me, sibling = (x, y, c), (x, y, 1 - c)
        passing = []
        for f, buf in enumerate(fwd_refs):
            for j, (px, py) in enumerate(chips):
                mine, theirs = buf.at[4 * px + 2 * py + c], buf.at[4 * px + 2 * py + 1 - c]
                sems = dict(send_sem=send_sems.at[7 * n_arr + 3 * f + j], recv_sem=recv_sems.at[7 * n_arr + 3 * f + j],
                            device_id=sibling, device_id_type=MESH)
                passing.append((pltpu.make_async_remote_copy(src_ref=mine, dst_ref=mine, **sems),
                                pltpu.make_async_remote_copy(src_ref=mine, dst_ref=theirs, **sems)))
        for send, _ in passing:
            send.start()
        arrays = []
        for a, (x_ref, out_ref) in enumerate(zip(x_refs, out_refs)):
            src_mine = x_ref.at[c] if splits[a] else x_ref

            def copy(k, blk, to, src=None, a=a, out_ref=out_ref):
                dst = out_ref.at[4 * blk[0] + 2 * blk[1] + blk[2]]
                return pltpu.make_async_remote_copy(
                    src_ref=dst if src is None else src, dst_ref=dst,
                    send_sem=send_sems.at[7 * a + k], recv_sem=recv_sems.at[7 * a + k],
                    device_id=to, device_id_type=MESH)

            mine = pltpu.make_async_copy(src_mine, out_ref.at[4 * x + 2 * y + c], local_sems.at[a])
            first = [copy(0, me, sibling, src=src_mine)] if own_slots[a] else []
            first += [copy(1 + j, me, (*chip, c), src=src_mine) for j, chip in enumerate(chips)]
            if a in direct:
                first += [copy(4 + j, me, (*chip, 1 - c), src=src_mine) for j, chip in enumerate(chips)]
            for cp in first + ([mine] if own_slots[a] else []):
                cp.start()
            arrays.append((copy, mine, first, own_slots[a], a in direct))
        rider_waits, i0, o0 = [], 0, 0
        for n, job in enumerate(riders):
            k_in, k_out = len(job.operands), len(job.out_shapes)
            starts, waits = job.make(r_ins[i0:i0 + k_in], r_outs[o0:o0 + k_out],
                                     rider_sems[2 * n], rider_sems[2 * n + 1])
            for cp in starts:
                cp.start()
            rider_waits += waits
            i0, o0 = i0 + k_in, o0 + k_out
        sent = []
        for copy, mine, first, own, straight in arrays:
            passed = [] if straight else [copy(4 + j, (*chip, c), sibling) for j, chip in enumerate(chips)]
            for j, chip in enumerate(chips):
                copy(1 + j, (*chip, c), me).wait_recv()
                if not straight:
                    passed[j].start()
            sent += first + passed
        for copy, mine, first, own, straight in arrays:
            if own:
                copy(0, sibling, me).wait_recv()
                mine.wait()
            for j, chip in enumerate(chips):
                copy(4 + j, (*chip, 1 - c), me).wait_recv()
        for cp in sent:
            cp.wait_send()
        for send, arrival in passing:
            send.wait_send()
            arrival.wait_recv()
        for wait in rider_waits:
            wait()

    n_sems = 7 * n_arr + 3 * n_fwd
    rider_operands = [a for r in riders for a in r.operands]
    rider_shapes = [s for r in riders for s in r.out_shapes]
    return pl.pallas_call(
        body, name=name,
        out_shape=[jax.ShapeDtypeStruct((N_DEV,) + tuple(b.shape[1:] if s else b.shape), b.dtype)
                   for b, s in zip(blocks, splits)]
        + [jax.ShapeDtypeStruct(f.shape, f.dtype) for f in forward] + rider_shapes,
        in_specs=[_any()] * (n_arr + n_fwd + rider_in), out_specs=[_any()] * (n_arr + n_fwd + rider_out),
        input_output_aliases={n_arr + f: n_arr + f for f in range(n_fwd)},
        scratch_shapes=[pltpu.SemaphoreType.DMA((n_sems,)), pltpu.SemaphoreType.DMA((n_sems,)),
                        pltpu.SemaphoreType.DMA((n_arr,))]
        + [pltpu.SemaphoreType.DMA((r.n_sems,)) for r in riders for _ in range(2)],
    )(*blocks, *forward, *rider_operands)


def _split_scatter_copies(src, land, send_sems, recv_sems):
    x, y, c, chips = _mesh_place()
    return [pltpu.make_async_remote_copy(
        src_ref=src.at[2 * px + py], dst_ref=land.at[j], send_sem=send_sems.at[j], recv_sem=recv_sems.at[j],
        device_id=(px, py, c), device_id_type=MESH) for j, (px, py) in enumerate(chips)]


def _scatter_start(chip_sums, name):
    hbm, sem = pl.BlockSpec(memory_space=pltpu.HBM), pl.BlockSpec(memory_space=pltpu.SEMAPHORE)

    def body(src, land, send_sems, recv_sems, src_thru, land_thru, token):
        del src_thru, land_thru
        for cp in _split_scatter_copies(src, land, send_sems, recv_sems):
            cp.start()
        token[...] = jnp.zeros_like(token)

    land = lax.empty((N_CHIPS - 1,) + chip_sums.shape[1:], chip_sums.dtype)
    operands = [pltpu.with_memory_space_constraint(a, pltpu.HBM) for a in (chip_sums, land)]
    return pl.pallas_call(
        body, name=name,
        out_shape=[pltpu.SemaphoreType.DMA((N_CHIPS - 1,)), pltpu.SemaphoreType.DMA((N_CHIPS - 1,))]
        + [pltpu.HBM(a.shape, a.dtype) for a in operands] + [jax.ShapeDtypeStruct((8, LANES), F32)],
        in_specs=[hbm, hbm], out_specs=[sem, sem, hbm, hbm, pl.BlockSpec(memory_space=pltpu.VMEM)],
        input_output_aliases={0: 2, 1: 3},
        compiler_params=pltpu.CompilerParams(has_side_effects=pltpu.SideEffectType.DATAFLOW_SIDE_EFFECTING),
    )(*operands)


def _scatter_wait(started, after, name):
    send_sems, recv_sems, src, land, _ = started
    hbm, sem = pl.BlockSpec(memory_space=pltpu.HBM), pl.BlockSpec(memory_space=pltpu.SEMAPHORE)

    def body(src, land, send_sems, recv_sems, after_ref, src_thru, land_thru):
        del after_ref, src_thru, land_thru
        for cp in _split_scatter_copies(src, land, send_sems, recv_sems):
            cp.wait()

    return pl.pallas_call(
        body, name=name, out_shape=[pltpu.HBM(src.shape, src.dtype), pltpu.HBM(land.shape, land.dtype)],
        in_specs=[hbm, hbm, sem, sem, _any()], out_specs=[hbm, hbm], input_output_aliases={0: 0, 1: 1},
        compiler_params=pltpu.CompilerParams(has_side_effects=pltpu.SideEffectType.DATAFLOW_SIDE_EFFECTING),
    )(src, land, send_sems, recv_sems, after)


def _share_job(bufs):
    def make(ins, outs, send_sems, recv_sems):
        del ins
        x, y, c, _ = _mesh_place()
        sems = lambda a: dict(send_sem=send_sems.at[a], recv_sem=recv_sems.at[a],
                              device_id=(x, y, 1 - c), device_id_type=MESH)
        sends = [pltpu.make_async_remote_copy(src_ref=o.at[c], dst_ref=o.at[c], **sems(a)) for a, o in enumerate(outs)]
        arrivals = [pltpu.make_async_remote_copy(src_ref=o.at[c], dst_ref=o.at[1 - c], **sems(a))
                    for a, o in enumerate(outs)]
        return sends, [s.wait_send for s in sends] + [r.wait_recv for r in arrivals]

    shapes = tuple(jax.ShapeDtypeStruct(b.shape, b.dtype) for b in bufs)
    return _Comm(tuple(bufs), shapes, len(bufs), make, in_place=len(bufs))


def _sibling_share(bufs, name):
    n_arr = len(bufs)

    def body(*refs):
        out_refs = refs[n_arr:2 * n_arr]
        send_sems, recv_sems = refs[2 * n_arr:]
        x, y, c = lax.axis_index("x"), lax.axis_index("y"), lax.axis_index("c")
        copies = [pltpu.make_async_remote_copy(
            src_ref=out_refs[a].at[c], dst_ref=out_refs[a].at[c],
            send_sem=send_sems.at[a], recv_sem=recv_sems.at[a],
            device_id=(x, y, 1 - c), device_id_type=MESH) for a in range(n_arr)]
        for cp in copies:
            cp.start()
        for a in range(n_arr):
            pltpu.make_async_remote_copy(
                src_ref=out_refs[a].at[c], dst_ref=out_refs[a].at[1 - c],
                send_sem=send_sems.at[a], recv_sem=recv_sems.at[a],
                device_id=(x, y, 1 - c), device_id_type=MESH).wait()

    return pl.pallas_call(
        body, name=name,
        out_shape=[jax.ShapeDtypeStruct(b.shape, b.dtype) for b in bufs],
        in_specs=[_any()] * n_arr, out_specs=[_any()] * n_arr,
        input_output_aliases={a: a for a in range(n_arr)},
        scratch_shapes=[pltpu.SemaphoreType.DMA((n_arr,)), pltpu.SemaphoreType.DMA((n_arr,))],
    )(*bufs)


def _gelu_tanh(z):
    k = math.sqrt(2.0 / math.pi)
    t = jnp.tanh(k * (z + 0.044715 * (z * z * z)))
    return 0.5 * z * (1.0 + t), t


def _gelu_tanh_grad(z, t):
    k = math.sqrt(2.0 / math.pi)
    return 0.5 * (1.0 + t) + 0.5 * z * (1.0 - t * t) * (k * (1.0 + 3.0 * 0.044715 * (z * z)))


def _rope_angle_kernel(pos_row, invf_col):
    seq = pos_row.shape[1]

    def body(p_ref, f_ref, cos_ref, sin_ref):
        ang = p_ref[...].astype(F32) * f_ref[...]
        cos_ref[...] = jnp.cos(ang)
        sin_ref[...] = jnp.sin(ang)

    return pl.pallas_call(
        body, name="rope_angles", grid=(1,), out_shape=[jax.ShapeDtypeStruct((ROT_DIM // 2, seq), F32)] * 2,
        in_specs=[_full((1, seq)), _full((ROT_DIM // 2, 1))], out_specs=[_full((ROT_DIM // 2, seq))] * 2,
        compiler_params=_params("arbitrary"),
    )(pos_row, invf_col)


def _rope_lane_tables(cos, sin):
    cos_t, sin_t = cos.T, sin.T
    seq, half = cos_t.shape
    ones = jnp.ones((seq, HEAD_DIM - ROT_DIM), F32)
    c64 = jnp.concatenate([cos_t, cos_t, ones], axis=1)
    s1 = jnp.concatenate([sin_t, jnp.zeros((seq, HEAD_DIM - half), F32)], axis=1)
    s2 = jnp.concatenate([jnp.zeros((seq, half), F32), sin_t, jnp.zeros((seq, HEAD_DIM - ROT_DIM), F32)], axis=1)
    return jnp.concatenate([jnp.tile(t, (1, LANES // HEAD_DIM)) for t in (c64, s1, s2)], axis=1)


def _rope_apply(t, tab, sign):
    reps = t.shape[1] // LANES
    c_tab, s1, s2 = (jnp.tile(tab[:, LANES * k:LANES * (k + 1)], (1, reps)) if reps > 1
                     else tab[:, LANES * k:LANES * (k + 1)] for k in range(3))
    half = ROT_DIM // 2
    up = pltpu.roll(t, t.shape[1] - half, 1)
    down = pltpu.roll(t, half, 1)
    return t * c_tab + sign * (down * s2 - up * s1)


def _lane_masks(shape):
    lane = lax.broadcasted_iota(jnp.int32, shape, 1)
    return lane < HEAD_DIM, lane >= HEAD_DIM


HEADS_PER_GROUP = N_Q_HEADS // N_KV_HEADS
ATTN_SCALE = 1.0 / math.sqrt(HEAD_DIM)


def _attn_bias_t(first_block):
    kj = lax.broadcasted_iota(jnp.int32, (2 * CHUNK, CHUNK), 0)
    qi = lax.broadcasted_iota(jnp.int32, (2 * CHUNK, CHUNK), 1)
    ok = (kj > qi) & (kj <= qi + CHUNK)
    if first_block is not None:
        ok = ok & (jnp.logical_not(first_block) | (kj >= CHUNK))
    return jnp.tile(jnp.where(ok, 0.0, -jnp.inf), (1, HEADS_PER_GROUP))


def _group_rows(x, g, lo, hi):
    rows = []
    for r in range(HEADS_PER_GROUP):
        h = HEADS_PER_GROUP * g + r
        pair = x[:, LANES * (h // 2):LANES * (h // 2 + 1)]
        rows.append(jnp.where(hi if h % 2 else lo, pair, 0.0))
    return jnp.concatenate(rows, axis=0)


def _pairs_from_rows(rows, lo):
    return [jnp.where(lo, rows[2 * CHUNK * k:2 * CHUNK * k + CHUNK], rows[2 * CHUNK * k + CHUNK:2 * CHUNK * (k + 1)])
            for k in range(HEADS_PER_GROUP // 2)]


def _group_dup(a, b, g, lo2):
    return jnp.where(lo2, a, b) if g == 0 else jnp.where(lo2, b, a)


def _sink_row(sink_ref, g):
    return jnp.concatenate([sink_ref[HEADS_PER_GROUP * g + r:HEADS_PER_GROUP * g + r + 1, :]
                            for r in range(HEADS_PER_GROUP)], axis=1)


def _attn_probs_t(k_dup, q_rows, bias_t, sink_row):
    s_t = _dot_nt(k_dup, q_rows) * ATTN_SCALE + bias_t
    m = jnp.maximum(jnp.max(s_t, axis=0, keepdims=True), sink_row)
    p = jnp.exp(s_t - m)
    e_sink = jnp.exp(sink_row - m)
    inv = 1.0 / (jnp.sum(p, axis=0, keepdims=True) + e_sink)
    return p * inv, e_sink * inv


def _sgu_forward_pair(wm, vp, j):
    lo, hi = _lane_masks(vp.shape)
    lhs = jnp.concatenate([wm[2 * j], wm[2 * j + 1]], axis=1)
    rhs = jnp.concatenate([jnp.where(lo, vp, 0.0), jnp.where(hi, vp, 0.0)], axis=0)
    return _dot(lhs, rhs)


def _masked_spatial(w_ref):
    t = lax.broadcasted_iota(jnp.int32, (CHUNK, CHUNK), 0)
    s = lax.broadcasted_iota(jnp.int32, (CHUNK, CHUNK), 1)
    tril = s <= t
    return [jnp.where(tril, w_ref[g], 0.0) for g in range(GMLP_GROUPS)], tril, s >= t


def _mod_kernel(c_all, w_shard, b_shard, comm=None):
    n = w_shard.shape[1]
    tn = 512

    def body(c_ref, w_ref, b_ref, mod_ref, act_ref):
        cv = c_ref[...]
        act = cv * (1.0 / (1.0 + jnp.exp(-cv)))
        act_ref[...] = act
        mod_ref[...] = _dot(act, w_ref[...]) + b_ref[...]

    return _hosted_call(
        body, comm, name="ada_mod", grid=(n // tn,),
        out_shape=[jax.ShapeDtypeStruct((N_DEV, n), F32), jax.ShapeDtypeStruct((N_DEV, D_MODEL), F32)],
        in_specs=[_full((N_DEV, D_MODEL)), pl.BlockSpec((D_MODEL, tn), lambda i: (0, i)),
                  pl.BlockSpec((1, tn), lambda i: (0, i))],
        out_specs=[pl.BlockSpec((N_DEV, tn), lambda i: (0, i)), _full((N_DEV, D_MODEL))],
        semantics=("arbitrary",),
    )(c_all, w_shard, b_shard)


def _load_chip_blocks(chip_ref, gathered, local, dsts, sems, first_sem=0):
    for k, dst in enumerate(dsts):
        @pl.when(chip_ref[0] == k)
        def _():
            pltpu.make_async_copy(local, dst, sems.at[first_sem + k]).start()

        @pl.when(chip_ref[0] != k)
        def _():
            pltpu.make_async_copy(gathered.at[k], dst, sems.at[first_sem + k]).start()
    return [pltpu.make_async_copy(local, dst, sems.at[first_sem + k]).wait for k, dst in enumerate(dsts)]


def _in_proj_kernel(x, vecs, w_in_t, comm=None):
    seq = x.shape[0]
    tm = 512

    def body(x_ref, v_ref, w_ref, proj_ref, h_ref):
        xv = x_ref[...]
        rstd = lax.rsqrt(_mean_last(xv * xv) + EPS)
        n1 = (xv * rstd) * v_ref[0:1, :]
        h = n1 * (1.0 + v_ref[2:3, :]) + v_ref[1:2, :]
        hb = h.astype(MXU_DTYPE)
        h_ref[...] = hb
        proj_ref[...] = _dot_nt(hb, w_ref[...])

    return _hosted_call(
        body, comm, name="in_proj", grid=(seq // tm,),
        out_shape=[jax.ShapeDtypeStruct((seq, IN_PROJ_WIDTH), F32),
                   jax.ShapeDtypeStruct((seq, D_MODEL), MXU_DTYPE)],
        in_specs=[pl.BlockSpec((tm, D_MODEL), lambda i: (i, 0)), _full((8, D_MODEL)),
                  _full((IN_PROJ_WIDTH, D_MODEL))],
        out_specs=[pl.BlockSpec((tm, IN_PROJ_WIDTH), lambda i: (i, 0)),
                   pl.BlockSpec((tm, D_MODEL), lambda i: (i, 0))],
        semantics=("arbitrary",),
    )(x, vecs, w_in_t)


MIXER_BLOCKS_PER_STEP = 4
KV_START = 2 * GMLP_WIDTH + ATTN_WIDTH


def _mixer_fwd_kernel(proj, rope_tab, w_spatial, bias_full, sink_rows, comm=None):
    seq = proj.shape[0]
    per = MIXER_BLOCKS_PER_STEP
    steps = seq // (CHUNK * per)
    kv_col = KV_START // (2 * KV_WIDTH)

    def body(proj_ref, prev_ref, tab_ref, ptab_ref, w_ref, bias_ref, sink_ref, cat_ref):
        i = pl.program_id(0)
        wm, _, _ = _masked_spatial(w_ref)
        lo, hi = _lane_masks((CHUNK, LANES))
        lo2, _ = _lane_masks((2 * CHUNK, LANES))
        o = 2 * GMLP_WIDTH
        for s in range(per):
            rows, before = slice(CHUNK * s, CHUNK * (s + 1)), slice(CHUNK * (s - 1), CHUNK * s)
            for j in range(GMLP_GROUPS // 2):
                cols = slice(LANES * j, LANES * (j + 1))
                vcols = slice(GMLP_WIDTH + LANES * j, GMLP_WIDTH + LANES * (j + 1))
                u, _ = _gelu_tanh(proj_ref[rows, cols])
                vp, _ = _gelu_tanh(proj_ref[rows, vcols])
                sv = _sgu_forward_pair(wm, vp, j) + bias_ref[:, cols]
                cat_ref[rows, cols] = (u * sv).astype(cat_ref.dtype)
            tab = tab_ref[rows, :]
            if s == 0:
                prev_kv, prev_tab, first = prev_ref[...], ptab_ref[...], i == 0
            else:
                prev_kv, prev_tab, first = proj_ref[before, KV_START:KV_START + 2 * KV_WIDTH], tab_ref[before, :], None
            q_r = _rope_apply(proj_ref[rows, o:o + ATTN_WIDTH], tab, 1.0)
            k_cur = _rope_apply(proj_ref[rows, KV_START:KV_START + KV_WIDTH], tab, 1.0)
            k_prev = _rope_apply(prev_kv[:, 0:KV_WIDTH], prev_tab, 1.0)
            k_a = jnp.concatenate([k_prev, k_cur], axis=0)
            v_a = jnp.concatenate([prev_kv[:, KV_WIDTH:2 * KV_WIDTH],
                                   proj_ref[rows, KV_START + KV_WIDTH:KV_START + 2 * KV_WIDTH]], axis=0)
            k_b = pltpu.roll(k_a, HEAD_DIM, 1)
            v_b = pltpu.roll(v_a, HEAD_DIM, 1)
            bias_t = _attn_bias_t(first)
            for g in range(N_KV_HEADS):
                p_t, _ = _attn_probs_t(_group_dup(k_a, k_b, g, lo2), _group_rows(q_r, g, lo, hi), bias_t,
                                       _sink_row(sink_ref, g))
                o_t = _dot(_group_dup(v_a, v_b, g, lo2).T, p_t)
                for k, pair in enumerate(_pairs_from_rows(o_t.T, lo)):
                    c0 = GMLP_WIDTH + LANES * (2 * g + k)
                    cat_ref[rows, c0:c0 + LANES] = pair.astype(cat_ref.dtype)

    return _hosted_call(
        body, comm, name="mixer_fwd", grid=(steps,),
        out_shape=[jax.ShapeDtypeStruct((seq, D_MODEL), MXU_DTYPE)],
        in_specs=[pl.BlockSpec((CHUNK * per, IN_PROJ_WIDTH), lambda i: (i, 0)),
                  pl.BlockSpec((CHUNK, 2 * KV_WIDTH), lambda i: (jnp.maximum(per * i - 1, 0), kv_col)),
                  pl.BlockSpec((CHUNK * per, 3 * LANES), lambda i: (i, 0)),
                  pl.BlockSpec((CHUNK, 3 * LANES), lambda i: (jnp.maximum(per * i - 1, 0), 0)),
                  _full((GMLP_GROUPS, CHUNK, CHUNK)), _full((CHUNK, GMLP_WIDTH)),
                  _full((N_Q_HEADS, LANES))],
        out_specs=[pl.BlockSpec((CHUNK * per, D_MODEL), lambda i: (i, 0))],
        semantics=("arbitrary",),
    )(proj, proj, rope_tab, rope_tab, w_spatial, bias_full, sink_rows)


def _trunk_kernel(x, target, cat, vecs, chip_idx, gathered, local):
    seq = x.shape[0]
    tm = 256
    nj = D_FF // D_MODEL
    out_rows = D_MODEL // N_CHIPS

    def body(chip_ref, x_ref, t_ref, cat_ref, v_ref, g_out, g_w1, g_w2, l_out, l_w1, l_w2,
             dx1_ref, dcat_ref, dmix_ref, h2_ref, r_ref, da_ref, dff_ref, sums_ref,
             wout, w1, w2, a_scr, sem):
        i = pl.program_id(0)

        @pl.when(i == 0)
        def _():
            waits = _load_chip_blocks(chip_ref, g_out, l_out,
                                      [wout.at[pl.ds(out_rows * k, out_rows)] for k in range(N_CHIPS)], sem)
            waits += _load_chip_blocks(chip_ref, g_w1, l_w1, [w1.at[k] for k in range(N_CHIPS)], sem, N_CHIPS)
            waits += _load_chip_blocks(chip_ref, g_w2, l_w2, [w2.at[k] for k in range(N_CHIPS)], sem, 2 * N_CHIPS)
            for wait in waits:
                wait()
            sums_ref[...] = jnp.zeros_like(sums_ref)

        gate1, shift2, scale2 = v_ref[0:1, :], v_ref[1:2, :], v_ref[2:3, :]
        gate2, g_ffn, g_final = v_ref[3:4, :], v_ref[4:5, :], v_ref[5:6, :]

        mix = _dot(cat_ref[...], wout[...])
        x1 = x_ref[...] + gate1 * mix
        rstd2 = lax.rsqrt(_mean_last(x1 * x1) + EPS)
        xh2 = x1 * rstd2
        n2 = xh2 * g_ffn
        h2b = (n2 * (1.0 + scale2) + shift2).astype(MXU_DTYPE)
        h2_ref[...] = h2b
        ff = jnp.zeros((tm, D_MODEL), F32)
        for j in range(nj):
            a = _dot(h2b, w1[j])
            a_scr[j] = a
            relu = jnp.maximum(a, 0.0)
            rb = (relu * relu).astype(MXU_DTYPE)
            r_ref[:, D_MODEL * j:D_MODEL * (j + 1)] = rb
            ff = ff + _dot(rb, w2[j])
        x2 = x1 + gate2 * ff
        rstd3 = lax.rsqrt(_mean_last(x2 * x2) + EPS)
        xh3 = x2 * rstd3
        err = xh3 * g_final - t_ref[...]
        loss = 0.5 * _rowsum(_mean_last(err * err))
        dy = err * (1.0 / D_MODEL)
        dxh3 = dy * g_final
        dx2 = rstd3 * (dxh3 - xh3 * _mean_last(dxh3 * xh3))
        dffb = (dx2 * gate2).astype(MXU_DTYPE)
        dff_ref[...] = dffb
        dh2 = jnp.zeros((tm, D_MODEL), F32)
        for j in range(nj):
            dr = _dot_nt(dffb, w2[j])
            dab = (dr * (2.0 * jnp.maximum(a_scr[j], 0.0))).astype(MXU_DTYPE)
            da_ref[:, D_MODEL * j:D_MODEL * (j + 1)] = dab
            dh2 = dh2 + _dot_nt(dab, w1[j])
        dn2 = dh2 * (1.0 + scale2)
        dxh2 = dn2 * g_ffn
        dx1 = dx2 + rstd2 * (dxh2 - xh2 * _mean_last(dxh2 * xh2))
        dx1_ref[...] = dx1
        dmixb = (dx1 * gate1).astype(MXU_DTYPE)
        dmix_ref[...] = dmixb
        dcat_ref[...] = _dot_nt(dmixb, wout[...])

        sums_ref[0:1, :] += _rowsum(dh2)
        sums_ref[1:2, :] += _rowsum(dh2 * n2)
        sums_ref[2:3, :] += _rowsum(dx2 * ff)
        sums_ref[3:4, :] += _rowsum(dn2 * xh2)
        sums_ref[4:5, :] += _rowsum(dy * xh3)
        sums_ref[5:6, :] += _rowsum(dx1 * mix)
        sums_ref[6:7, :] += jnp.broadcast_to(loss, (1, D_MODEL))

    tok = lambda w: pl.BlockSpec((tm, w), lambda i, chip: (i, 0))
    return _hosted_call(
        body, None, name="trunk", grid=(seq // tm,), n_prefetch=1,
        out_shape=[jax.ShapeDtypeStruct((seq, D_MODEL), F32), jax.ShapeDtypeStruct((seq, D_MODEL), F32),
                   jax.ShapeDtypeStruct((seq, D_MODEL), MXU_DTYPE), jax.ShapeDtypeStruct((seq, D_MODEL), MXU_DTYPE),
                   jax.ShapeDtypeStruct((seq, D_FF), MXU_DTYPE), jax.ShapeDtypeStruct((seq, D_FF), MXU_DTYPE),
                   jax.ShapeDtypeStruct((seq, D_MODEL), MXU_DTYPE), jax.ShapeDtypeStruct((8, D_MODEL), F32)],
        in_specs=[tok(D_MODEL), tok(D_MODEL), tok(D_MODEL), _full((8, D_MODEL))] + [_any()] * 6,
        out_specs=[tok(D_MODEL), tok(D_MODEL), tok(D_MODEL), tok(D_MODEL), tok(D_FF), tok(D_FF), tok(D_MODEL),
                   _full((8, D_MODEL))],
        scratch_shapes=[pltpu.VMEM((D_MODEL, D_MODEL), MXU_DTYPE), pltpu.VMEM((nj, D_MODEL, D_MODEL), MXU_DTYPE),
                        pltpu.VMEM((nj, D_MODEL, D_MODEL), MXU_DTYPE), pltpu.VMEM((nj, tm, D_MODEL), F32),
                        pltpu.SemaphoreType.DMA((3 * N_CHIPS,))],
        semantics=("arbitrary",),
    )(chip_idx, x, target, cat, vecs, *gathered, *local)


def _mixer_bwd_kernel(proj, rope_tab, dcat, w_spatial, w_spatial_t, bias_full, sink_rows, dev_idx, comm=None):
    seq = proj.shape[0]
    per = MIXER_BLOCKS_PER_STEP
    steps = seq // (CHUNK * per)
    kv_col = KV_START // (2 * KV_WIDTH)

    def body(dev_ref, proj_ref, prev_ref, tab_ref, ptab_ref, dcat_ref, w_ref, wt_ref, bias_ref, sink_ref,
             dproj_ref, dw_out, db_ref, dsink_ref, carry, dw_ref):
        del dev_ref
        step = pl.program_id(0)

        @pl.when(step == 0)
        def _():
            carry[...] = jnp.zeros_like(carry)
            dw_ref[...] = jnp.zeros_like(dw_ref)
            db_ref[...] = jnp.zeros_like(db_ref)
            dsink_ref[...] = jnp.zeros_like(dsink_ref)

        for s in reversed(range(per)):
            rows = pl.ds(CHUNK * s, CHUNK)
            if s == 0:
                before, before_tab, first = prev_ref, ptab_ref, step == steps - 1
            else:
                before = proj_ref.at[pl.ds(CHUNK * (s - 1), CHUNK), pl.ds(KV_START, 2 * KV_WIDTH)]
                before_tab, first = tab_ref.at[pl.ds(CHUNK * (s - 1), CHUNK)], None
            one_block(proj_ref.at[rows], before, tab_ref.at[rows], before_tab, dcat_ref.at[rows], w_ref, wt_ref,
                      bias_ref, sink_ref, dproj_ref.at[rows], dw_ref, db_ref, dsink_ref, carry, first)

        @pl.when(step == steps - 1)
        def _():
            dw_out[...] = dw_ref[...].astype(dw_out.dtype)

    def one_block(proj_ref, prev_ref, tab_ref, ptab_ref, dcat_ref, w_ref, wt_ref, bias_ref, sink_ref,
                  dproj_ref, dw_ref, db_ref, dsink_ref, carry, first):
        wm, tril, triu = _masked_spatial(w_ref)
        lo, hi = _lane_masks((CHUNK, LANES))
        lane = lax.broadcasted_iota(jnp.int32, (CHUNK, LANES), 1)
        db = jnp.zeros((CHUNK, LANES), F32)
        for j in range(GMLP_GROUPS // 2):
            cols = slice(LANES * j, LANES * (j + 1))
            vcols = slice(GMLP_WIDTH + LANES * j, GMLP_WIDTH + LANES * (j + 1))
            zu, zv = proj_ref[:, cols], proj_ref[:, vcols]
            u, tu = _gelu_tanh(zu)
            vp, tv = _gelu_tanh(zv)
            sv = _sgu_forward_pair(wm, vp, j) + bias_ref[:, cols]
            dout = dcat_ref[:, cols]
            du = dout * sv
            dsv = dout * u
            dsv_lo, dsv_hi = jnp.where(lo, dsv, 0.0), jnp.where(hi, dsv, 0.0)
            lhs_t = jnp.concatenate([jnp.where(triu, wt_ref[2 * j], 0.0),
                                     jnp.where(triu, wt_ref[2 * j + 1], 0.0)], axis=1)
            dv = _dot(lhs_t, jnp.concatenate([dsv_lo, dsv_hi], axis=0))
            dw_ref[2 * j] += jnp.where(tril, _dot_nt(dsv_lo, vp), 0.0)
            dw_ref[2 * j + 1] += jnp.where(tril, _dot_nt(dsv_hi, vp), 0.0)
            db = db + (jnp.where(lane == 2 * j, jnp.sum(dsv_lo, axis=1, keepdims=True), 0.0)
                       + jnp.where(lane == 2 * j + 1, jnp.sum(dsv_hi, axis=1, keepdims=True), 0.0))
            dproj_ref[:, cols] = (du * _gelu_tanh_grad(zu, tu)).astype(dproj_ref.dtype)
            dproj_ref[:, vcols] = (dv * _gelu_tanh_grad(zv, tv)).astype(dproj_ref.dtype)
        db_ref[...] += db
        o = 2 * GMLP_WIDTH
        tab = tab_ref[...]
        q_r = _rope_apply(proj_ref[:, o:o + ATTN_WIDTH], tab, 1.0)
        k_cur = _rope_apply(proj_ref[:, o + ATTN_WIDTH:o + ATTN_WIDTH + KV_WIDTH], tab, 1.0)
        k_prev = _rope_apply(prev_ref[:, 0:KV_WIDTH], ptab_ref[...], 1.0)
        k_a = jnp.concatenate([k_prev, k_cur], axis=0)
        v_a = jnp.concatenate([prev_ref[:, KV_WIDTH:2 * KV_WIDTH],
                               proj_ref[:, o + ATTN_WIDTH + KV_WIDTH:o + ATTN_WIDTH + 2 * KV_WIDTH]], axis=0)
        k_b = pltpu.roll(k_a, HEAD_DIM, 1)
        v_b = pltpu.roll(v_a, HEAD_DIM, 1)
        bias_t = _attn_bias_t(first)
        lo2, _ = _lane_masks((2 * CHUNK, LANES))
        dout_b = dcat_ref[:, GMLP_WIDTH:GMLP_WIDTH + ATTN_WIDTH]
        dk_tot, dv_tot, dq_pairs = [], [], []
        for g in range(N_KV_HEADS):
            k_dup, v_dup = _group_dup(k_a, k_b, g, lo2), _group_dup(v_a, v_b, g, lo2)
            q_rows = _group_rows(q_r, g, lo, hi)
            do_rows = _group_rows(dout_b, g, lo, hi)
            p_t, p_sink = _attn_probs_t(k_dup, q_rows, bias_t, _sink_row(sink_ref, g))
            dp_t = _dot_nt(v_dup, do_rows)
            delta = jnp.sum(p_t * dp_t, axis=0, keepdims=True)
            ds_t = p_t * (dp_t - delta) * ATTN_SCALE
            dsink = -p_sink * delta
            for r in range(HEADS_PER_GROUP):
                h = HEADS_PER_GROUP * g + r
                dsink_ref[h:h + 1, :] += jnp.broadcast_to(
                    jnp.sum(dsink[:, LANES * r:LANES * (r + 1)], axis=1, keepdims=True), (1, LANES))
            dk_full = _dot(ds_t, q_rows)
            dv_full = _dot(p_t, do_rows)
            dk_tot.append(dk_full + pltpu.roll(dk_full, HEAD_DIM, 1))
            dv_tot.append(dv_full + pltpu.roll(dv_full, HEAD_DIM, 1))
            dq_t = _dot(k_dup.T, ds_t)
            dq_pairs += _pairs_from_rows(dq_t.T, lo)
        dk_all = jnp.where(lo2, dk_tot[0], dk_tot[1])
        dv_all = jnp.where(lo2, dv_tot[0], dv_tot[1])
        dk_cur = dk_all[CHUNK:, :] + carry[:, 0:KV_WIDTH]
        dv_cur = dv_all[CHUNK:, :] + carry[:, KV_WIDTH:2 * KV_WIDTH]
        carry[:, 0:KV_WIDTH] = dk_all[:CHUNK, :]
        carry[:, KV_WIDTH:2 * KV_WIDTH] = dv_all[:CHUNK, :]
        dq = _rope_apply(jnp.concatenate(dq_pairs, axis=1), tab, -1.0)
        dproj_ref[:, o:o + ATTN_WIDTH] = dq.astype(dproj_ref.dtype)
        dproj_ref[:, o + ATTN_WIDTH:o + ATTN_WIDTH + KV_WIDTH] = (
            _rope_apply(dk_cur, tab, -1.0).astype(dproj_ref.dtype))
        dproj_ref[:, o + ATTN_WIDTH + KV_WIDTH:o + ATTN_WIDTH + 2 * KV_WIDTH] = dv_cur.astype(dproj_ref.dtype)

    rev = lambda i: steps - 1 - i
    before = lambda i: jnp.maximum(per * rev(i) - 1, 0)
    slot = lambda shape: pl.BlockSpec((None,) + shape, lambda i, d: (d[0],) + (0,) * len(shape))
    return _hosted_call(
        body, comm, name="mixer_bwd", grid=(steps,), n_prefetch=1,
        out_shape=[jax.ShapeDtypeStruct((seq, IN_PROJ_WIDTH), MXU_DTYPE),
                   jax.ShapeDtypeStruct((N_DEV, GMLP_GROUPS, CHUNK, CHUNK), GRAD_COMM_DTYPE),
                   jax.ShapeDtypeStruct((N_DEV, CHUNK, LANES), F32),
                   jax.ShapeDtypeStruct((N_DEV, N_Q_HEADS, LANES), F32)],
        in_specs=[pl.BlockSpec((CHUNK * per, IN_PROJ_WIDTH), lambda i, d: (rev(i), 0)),
                  pl.BlockSpec((CHUNK, 2 * KV_WIDTH), lambda i, d: (before(i), kv_col)),
                  pl.BlockSpec((CHUNK * per, 3 * LANES), lambda i, d: (rev(i), 0)),
                  pl.BlockSpec((CHUNK, 3 * LANES), lambda i, d: (before(i), 0)),
                  pl.BlockSpec((CHUNK * per, D_MODEL), lambda i, d: (rev(i), 0)),
                  _full((GMLP_GROUPS, CHUNK, CHUNK)), _full((GMLP_GROUPS, CHUNK, CHUNK)),
                  _full((CHUNK, GMLP_WIDTH)), _full((N_Q_HEADS, LANES))],
        out_specs=[pl.BlockSpec((CHUNK * per, IN_PROJ_WIDTH), lambda i, d: (rev(i), 0)),
                   slot((GMLP_GROUPS, CHUNK, CHUNK)), slot((CHUNK, LANES)), slot((N_Q_HEADS, LANES))],
        scratch_shapes=[pltpu.VMEM((CHUNK, 2 * KV_WIDTH), F32), pltpu.VMEM((GMLP_GROUPS, CHUNK, CHUNK), F32)],
        semantics=("arbitrary",),
    )(dev_idx, proj, proj, rope_tab, rope_tab, dcat, w_spatial, w_spatial_t, bias_full, sink_rows)


def _in_proj_bwd_kernel(x, dx1, dproj, vecs, w_in_t, comm=None):
    seq = x.shape[0]
    tm = 512

    def body(x_ref, dx1_ref, dp_ref, v_ref, w_ref, gx_ref, sums_ref):
        @pl.when(pl.program_id(0) == 0)
        def _():
            sums_ref[...] = jnp.zeros_like(sums_ref)

        g_mix, scale1 = v_ref[0:1, :], v_ref[2:3, :]
        dh = _dot(dp_ref[...], w_ref[...])
        xv = x_ref[...]
        rstd = lax.rsqrt(_mean_last(xv * xv) + EPS)
        xh = xv * rstd
        dn1 = dh * (1.0 + scale1)
        dxh = dn1 * g_mix
        gx_ref[...] = dx1_ref[...] + rstd * (dxh - xh * _mean_last(dxh * xh))
        sums_ref[0:1, :] += _rowsum(dh)
        sums_ref[1:2, :] += _rowsum(dh * (xh * g_mix))
        sums_ref[2:3, :] += _rowsum(dn1 * xh)

    tok = lambda w: pl.BlockSpec((tm, w), lambda i: (i, 0))
    return _hosted_call(
        body, comm, name="in_proj_bwd", grid=(seq // tm,),
        out_shape=[jax.ShapeDtypeStruct((seq, D_MODEL), F32), jax.ShapeDtypeStruct((8, D_MODEL), F32)],
        in_specs=[tok(D_MODEL), tok(D_MODEL), tok(IN_PROJ_WIDTH), _full((8, D_MODEL)),
                  _full((IN_PROJ_WIDTH, D_MODEL))],
        out_specs=[tok(D_MODEL), _full((8, D_MODEL))],
        semantics=("arbitrary",),
    )(x, dx1, dproj, vecs, w_in_t)


class _GradTiles(NamedTuple):
    tm: int
    tn: int
    n_tiles: int
    chips_per_tile: int
    a_index: Callable
    b_index: Callable


def _weight_grad_kernel(a, b, c_idx, name, tiles, comm=None):
    seq = a.shape[0]
    tk = min(seq, 4096)
    nk = seq // tk
    tm, tn, n_tiles, per = tiles.tm, tiles.tn, tiles.n_tiles, tiles.chips_per_tile
    rows = tm // per

    def half(phase, c):
        return phase * c[0] + (1 - phase) * (1 - c[0])

    def body(c_ref, a_ref, b_ref, o_ref, acc, stage, landed, send_sems, recv_sems):
        del c_ref
        phase, t, kk = pl.program_id(0), pl.program_id(1), pl.program_id(2)
        x, y, c, _ = _mesh_place()

        def copy(tile):
            return pltpu.make_async_remote_copy(
                src_ref=stage.at[tile], dst_ref=landed.at[tile], send_sem=send_sems.at[tile],
                recv_sem=recv_sems.at[tile], device_id=(x, y, 1 - c), device_id_type=MESH)

        @pl.when(kk == 0)
        def _():
            acc[...] = jnp.zeros_like(acc)

        acc[...] += _dot_tn(a_ref[...], b_ref[...])

        @pl.when((kk == nk - 1) & (phase == 0))
        def _():
            stage[t] = acc[...].astype(stage.dtype)
            copy(t).start()

        @pl.when((kk == nk - 1) & (phase == 1))
        def _():
            copy(t).wait_recv()
            total = acc[...] + landed[t].astype(F32)
            for q in range(per):
                o_ref[q] = total[rows * q:rows * (q + 1)].astype(o_ref.dtype)

        @pl.when((kk == nk - 1) & (phase == 1) & (t == n_tiles - 1))
        def _():
            for tile in range(n_tiles):
                copy(tile).wait_send()

    out = _hosted_call(
        body, comm, name=name, grid=(2, n_tiles, nk), n_prefetch=1,
        out_shape=[jax.ShapeDtypeStruct((n_tiles * per, rows, tn), GRAD_COMM_DTYPE)],
        in_specs=[pl.BlockSpec((tk, tm), lambda p, t, k, c: (k, tiles.a_index(t, half(p, c)))),
                  pl.BlockSpec((tk, tn), lambda p, t, k, c: (k, tiles.b_index(t, half(p, c))))],
        out_specs=[pl.BlockSpec((per, rows, tn), lambda p, t, k, c: (p * t, 0, 0))],
        scratch_shapes=[pltpu.VMEM((tm, tn), F32), pltpu.VMEM((n_tiles, tm, tn), GRAD_COMM_DTYPE),
                        pltpu.VMEM((n_tiles, tm, tn), GRAD_COMM_DTYPE),
                        pltpu.SemaphoreType.DMA((n_tiles,)), pltpu.SemaphoreType.DMA((n_tiles,))],
        semantics=("arbitrary", "arbitrary", "arbitrary"),
    )(c_idx, a, b)
    return out[0] if comm is None else out


def _row_tile(rows, most=256, sublanes=16):
    return max(t for t in range(sublanes, most + 1, sublanes) if rows % t == 0)


def _adam_update(w, g, m, v):
    m_new = ADAM_B1 * m + (1.0 - ADAM_B1) * g
    v_new = ADAM_B2 * v + (1.0 - ADAM_B2) * (g * g)
    m_hat = m_new / (1.0 - ADAM_B1 ** ADAM_STEP)
    v_hat = v_new / (1.0 - ADAM_B2 ** ADAM_STEP)
    delta = -ADAM_LR * (m_hat / (jnp.sqrt(v_hat) + ADAM_EPS) + ADAM_WD * w)
    return delta, m_new, v_new


def _sum_chips_kernel(own, others, place, name):
    _, r, n = own.shape
    tr = _row_tile(r)

    def body(place_ref, own_ref, oth_ref, o_ref):
        del place_ref
        acc = own_ref[...].astype(F32)
        for k in range(N_CHIPS - 1):
            acc = acc + oth_ref[k].astype(F32)
        o_ref[...] = acc

    return pl.pallas_call(
        body, name=name, out_shape=jax.ShapeDtypeStruct((2, r, n), F32),
        grid_spec=pltpu.PrefetchScalarGridSpec(
            num_scalar_prefetch=1, grid=(r // tr,),
            in_specs=[pl.BlockSpec((None, tr, n), lambda i, p: (p[0], i, 0)),
                      pl.BlockSpec((N_CHIPS - 1, tr, n), lambda i, p: (0, i, 0))],
            out_specs=pl.BlockSpec((None, tr, n), lambda i, p: (p[1], i, 0))),
        compiler_params=_params("parallel"),
    )(place, own, others)


def _adam_kernel(w, g, m, v, name):
    r, n = w.shape
    by_columns = g.shape[1] == r
    tr, tn = _row_tile(g.shape[1], most=512), g.shape[2]

    def body(w_ref, g_ref, m_ref, v_ref, g_out, d_ref, mo_ref, vo_ref):
        gv = g_ref[...]
        g_out[...] = gv
        d_ref[...], mo_ref[...], vo_ref[...] = _adam_update(w_ref[...], gv, m_ref[...], v_ref[...])

    steps = g.shape[1] // tr
    spec = pl.BlockSpec((tr, tn), (lambda h, i: (i, h)) if by_columns else (lambda h, i: (h * steps + i, 0)))
    return pl.pallas_call(
        body, name=name, grid=(2, steps), out_shape=[jax.ShapeDtypeStruct((r, n), F32)] * 4,
        in_specs=[spec, pl.BlockSpec((None, tr, tn), lambda h, i: (h, i, 0)), spec, spec], out_specs=[spec] * 4,
        compiler_params=_params("parallel", "parallel"),
    )(w, g, m, v)


SMALL_PARAMS = ("b_ada", "g_mix", "g_ffn", "g_final", "b_spatial", "sinks", "w_spatial")


def _small_update_kernel(gathered, params):
    shapes = [params[nm][0].shape for nm in SMALL_PARAMS]

    def body(*refs):
        g_refs, refs = refs[:5], refs[5:]
        p_refs, refs = refs[:3 * len(SMALL_PARAMS)], refs[3 * len(SMALL_PARAMS):]
        loss_ref, o_refs = refs[0], refs[1:]

        def total(ref):
            acc = ref[0].astype(F32)
            for k in range(1, N_DEV):
                acc = acc + ref[k].astype(F32)
            return acc

        s1, s2, db, ds, dw = (total(r) for r in g_refs)
        loss_ref[...] = jnp.broadcast_to(s2[6:7, 0:1], loss_ref.shape)
        grads = {"b_ada": [s1[0:1], s1[1:2], s2[5:6], s2[0:1], s2[1:2], s2[2:3]], "g_mix": [s1[2:3]],
                 "g_ffn": [s2[3:4]], "g_final": [s2[4:5]], "b_spatial": [db.T[0:GMLP_GROUPS]],
                 "w_spatial": [dw]}
        lane = lax.broadcasted_iota(jnp.int32, (1, LANES), 1)
        sink_row = jnp.zeros((1, LANES), F32)
        for h in range(N_Q_HEADS):
            sink_row = sink_row + jnp.where(lane == h, ds[h:h + 1, :], 0.0)
        grads["sinks"] = [sink_row[:, 0:N_Q_HEADS]]
        for i, nm in enumerate(SMALL_PARAMS):
            w_ref, m_ref, v_ref = p_refs[3 * i:3 * i + 3]
            outs = o_refs[4 * i:4 * i + 4]
            width = grads[nm][0].shape[1]
            for k, g in enumerate(grads[nm]):
                cols = slice(width * k, width * (k + 1))
                upd = _adam_update(w_ref[:, cols], g, m_ref[:, cols], v_ref[:, cols])
                for o_ref, val in zip(outs, (g,) + upd):
                    o_ref[:, cols] = val

    flat = [a for nm in SMALL_PARAMS for a in params[nm]]
    out_shape = [jax.ShapeDtypeStruct((8, LANES), F32)]
    out_shape += [jax.ShapeDtypeStruct(s, F32) for s in shapes for _ in range(4)]
    outs = pl.pallas_call(
        body, name="small_update", grid=(1,), out_shape=out_shape,
        in_specs=[_full(g.shape) for g in gathered] + [_full(a.shape) for a in flat],
        out_specs=[_full(s.shape) for s in out_shape],
        compiler_params=_params("arbitrary"),
    )(*gathered, *flat)
    return {nm: outs[1 + 4 * i:5 + 4 * i] for i, nm in enumerate(SMALL_PARAMS)}, outs[0]


def _ada_update_kernel(act_t, dmod, w, m, v):
    r, n = w.shape
    tr = 256

    def body(a_ref, d_ref, w_ref, m_ref, v_ref, g_ref, dl_ref, mo_ref, vo_ref):
        g = _dot(a_ref[...], d_ref[...])
        g_ref[...] = g
        dl_ref[...], mo_ref[...], vo_ref[...] = _adam_update(w_ref[...], g, m_ref[...], v_ref[...])

    spec = pl.BlockSpec((tr, n), lambda i: (i, 0))
    return pl.pallas_call(
        body, name="ada_update", grid=(r // tr,), out_shape=[jax.ShapeDtypeStruct((r, n), F32)] * 4,
        in_specs=[pl.BlockSpec((tr, N_DEV), lambda i: (i, 0)), _full((N_DEV, n)), spec, spec, spec],
        out_specs=[spec] * 4, compiler_params=_params("parallel"),
    )(act_t, dmod, w, m, v)


def kernel(x, c, positions, w_ada, b_ada, g_mix, w_in, w_spatial, b_spatial, sinks, w_out, g_ffn, w_ff1, w_ff2, g_final, loss_target, m_w_ada, m_b_ada, m_g_mix, m_w_in, m_w_spatial, m_b_spatial, m_sinks, m_w_out, m_g_ffn, m_w_ff1, m_w_ff2, m_g_final, v_w_ada, v_b_ada, v_g_mix, v_w_in, v_w_spatial, v_b_spatial, v_sinks, v_w_out, v_g_ffn, v_w_ff1, v_w_ff2, v_g_final):
    xi, yi, ci = lax.axis_index("x"), lax.axis_index("y"), lax.axis_index("c")
    chip = 2 * xi + yi
    dev = 2 * chip + ci
    seq = x.shape[1]
    x2, tgt = x[0], loss_target[0]
    ada_cols = w_ada.shape[2]

    big = {"w_in": tuple(a[0].T for a in (w_in, m_w_in, v_w_in)),
           "w_out": (w_out[0], m_w_out[0], v_w_out[0]), "w_ff1": (w_ff1[0], m_w_ff1[0], v_w_ff1[0]),
           "w_ff2": (w_ff2[0], m_w_ff2[0], v_w_ff2[0])}

    def halves(nm):
        r, n = big[nm][0].shape
        return big[nm][0].astype(WEIGHT_COMM_DTYPE).reshape(2, r // 2, n)

    chip_idx = chip.reshape(1).astype(jnp.int32)
    c_all, w_in_t, g_out = _all_gather8([c, halves("w_in"), halves("w_out")], "gather_first",
                                        split=[False, True, True], skip_own=(2,))
    c_all, w_in_t = c_all.reshape(N_DEV, D_MODEL), w_in_t.reshape(IN_PROJ_WIDTH, D_MODEL)
    b_shard = lax.dynamic_slice(b_ada, (0, chip * ada_cols), (1, ada_cols))
    mod_part, act = _mod_kernel(c_all, w_ada[0], b_shard)
    mod_all, = _all_gather8([mod_part], "gather_mod", direct=(0,))
    mod_me = lax.dynamic_index_in_dim(mod_all[0::2], dev, axis=1, keepdims=False)
    mod_me = mod_me.reshape(N_MOD, D_MODEL)
    shift1, scale1, gate1, shift2, scale2, gate2 = (mod_me[k:k + 1] for k in range(N_MOD))

    zeros_row = jnp.zeros((1, D_MODEL), F32)
    vecs1 = jnp.concatenate([g_mix, shift1, scale1] + [zeros_row] * 5, axis=0)
    vecs2 = jnp.concatenate([gate1, shift2, scale2, gate2, g_ffn, g_final.reshape(1, D_MODEL)]
                            + [zeros_row] * 2, axis=0)
    bias_full = jnp.repeat(b_spatial[0].T, HEAD_DIM, axis=1)
    sink_rows = jnp.broadcast_to(sinks[0][:, None], (N_Q_HEADS, LANES))
    inv_freq = ROPE_THETA ** (-jnp.arange(0, ROT_DIM, 2, dtype=F32) / ROT_DIM)
    rope_tab = _rope_lane_tables(*_rope_angle_kernel(positions, inv_freq.reshape(ROT_DIM // 2, 1)))

    trunk_weights = ["w_out", "w_ff1", "w_ff2"]
    shards = [halves(nm) for nm in trunk_weights]
    proj, hb, *staged = _in_proj_kernel(x2, vecs1, w_in_t, comm=_gather2d_first(shards[1:]))
    cat, *staged = _mixer_fwd_kernel(proj, rope_tab, w_spatial[0], bias_full, sink_rows,
                                     comm=_gather2d_second(staged, shards[1:]))
    staged = [g_out] + list(_gather_forward(staged, "gather_forward"))
    dx1, dcat, dmix, h2b, rb, dab, dffb, sums2 = _trunk_kernel(
        x2, tgt, cat, vecs2, chip_idx,
        [g.reshape((N_CHIPS,) + big[nm][0].shape) for nm, g in zip(trunk_weights, staged)],
        [s.reshape(big[nm][0].shape) for nm, s in zip(trunk_weights, shards)])

    c_idx = ci.reshape(1).astype(jnp.int32)
    place = jnp.stack([chip, ci]).astype(jnp.int32)
    half_d = D_MODEL // 2
    cs_ff2 = _weight_grad_kernel(rb, dffb, c_idx, "dw_ff2",
                                 _GradTiles(D_MODEL, half_d, N_CHIPS, 1, lambda t, h: t, lambda t, h: h))
    eighth = D_MODEL // 8
    cs_ff1, sc_ff2 = _weight_grad_kernel(
        h2b, dab, c_idx, "dw_ff1",
        _GradTiles(D_MODEL, half_d, N_CHIPS, 1, lambda t, h: 0, lambda t, h: 2 * t + h),
        comm=_scatter_job([cs_ff2], rows=(0, 6 * eighth)))
    cs_out, sc_ff2 = _weight_grad_kernel(
        cat, dmix, c_idx, "dw_out", _GradTiles(D_MODEL, half_d, 1, N_CHIPS, lambda t, h: 0, lambda t, h: h),
        comm=_scatter_job([cs_ff2], rows=(6 * eighth, eighth), into=[sc_ff2]))
    dproj, dw_spatial, db_lanes, dsink_rows, sc_ff2, sc_ff1, sc_out = _mixer_bwd_kernel(
        proj, rope_tab, dcat, w_spatial[0], w_spatial[0].transpose(0, 2, 1), bias_full, sink_rows,
        dev.reshape(1).astype(jnp.int32),
        comm=_merge_jobs(_scatter_job([cs_ff1, cs_out]),
                         _scatter_job([cs_ff2], rows=(7 * eighth, eighth), into=[sc_ff2])))
    totals = [_sum_chips_kernel(own, oth, place, "grad_sum_" + nm)
              for nm, own, oth in (("w_out", cs_out, sc_out), ("w_ff1", cs_ff1, sc_ff1), ("w_ff2", cs_ff2, sc_ff2))]
    small_slots = [db_lanes, dsink_rows, dw_spatial.reshape(N_DEV, GMLP_GROUPS * CHUNK, CHUNK)]
    cs_in, *rode = _weight_grad_kernel(
        dproj, hb, c_idx, "dw_in",
        _GradTiles(2 * W_IN_BLOCK, half_d, N_CHIPS // 2, 2, lambda t, h: t, lambda t, h: h),
        comm=_merge_jobs(_gather_job(small_slots), _share_job(totals)))
    small_stage1, shared = rode[:len(small_slots)], rode[len(small_slots):]
    scatter_in = _scatter_start(cs_in, "grad_to_chips_w_in_start")
    grad_x, sums1 = _in_proj_bwd_kernel(x2, dx1, dproj, vecs1 + scatter_in[-1][0:1, 0:1], w_in_t)
    cs_in, sc_in = _scatter_wait(scatter_in, sums1, "grad_to_chips_w_in_wait")
    gathered = _all_gather8([sums1, sums2], "gather_small", forward=small_stage1, direct=(0, 1))
    total_in = _sum_chips_kernel(cs_in, sc_in, place, "grad_sum_w_in")
    shared = list(_sibling_share([total_in], "grad_share_w_in")) + list(shared)
    names = ["w_in", "w_out", "w_ff1", "w_ff2"]
    big_out = {}
    for nm, g in zip(names, shared):
        w, m, v = big[nm]
        outs = _adam_kernel(w, g, m, v, "adam_" + nm)
        big_out[nm] = tuple((t.T if nm == "w_in" else t)[None] for t in outs)

    small = {"b_ada": (b_ada, m_b_ada, v_b_ada), "g_mix": (g_mix, m_g_mix, v_g_mix),
             "g_ffn": (g_ffn, m_g_ffn, v_g_ffn), "g_final": (g_final, m_g_final, v_g_final),
             "b_spatial": (b_spatial, m_b_spatial, v_b_spatial), "sinks": (sinks, m_sinks, v_sinks),
             "w_spatial": (w_spatial, m_w_spatial, v_w_spatial)}
    flat_shape = {"g_final": (1, D_MODEL), "b_spatial": (GMLP_GROUPS, CHUNK), "w_spatial": (GMLP_GROUPS * CHUNK, CHUNK)}
    small_out, loss_tile = _small_update_kernel(
        gathered, {nm: tuple(a.reshape(flat_shape.get(nm, a.shape)) for a in small[nm]) for nm in small})
    small_out = {nm: [o.reshape(small[nm][0].shape) for o in small_out[nm]] for nm in small}
    loss = loss_tile[0, 0]

    g1, g2 = gathered[0], gathered[1]
    dmod_all = jnp.concatenate([g1[:, 0], g1[:, 1], g2[:, 5], g2[:, 0], g2[:, 1], g2[:, 2]], axis=1)
    dmod_cols = lax.dynamic_slice(dmod_all, (0, chip * ada_cols), (N_DEV, ada_cols))
    ada = _ada_update_kernel(act.T, dmod_cols, w_ada[0], m_w_ada[0], v_w_ada[0])
    big_out["w_ada"] = tuple(t[None] for t in ada)

    order = ["w_ada", "b_ada", "g_mix", "w_in", "w_spatial", "b_spatial", "sinks", "w_out", "g_ffn",
             "w_ff1", "w_ff2", "g_final"]

    def leaf(nm, k):
        return big_out[nm][k] if nm in big_out else small_out[nm][k]

    outs = [loss, grad_x[None]]
    for k in range(4):
        outs += [leaf(nm, k) for nm in order]
    return tuple(outs)
```

```python
import math
from typing import Callable, NamedTuple

import jax
import jax.numpy as jnp
from jax import lax
from jax.experimental import pallas as pl
from jax.experimental.pallas import tpu as pltpu

F32 = jnp.float32
MXU_DTYPE = jnp.bfloat16
WEIGHT_COMM_DTYPE = jnp.bfloat16
GRAD_COMM_DTYPE = jnp.bfloat16

D_MODEL = 1024
D_FF = 4096
HEAD_DIM = 64
GMLP_GROUPS = 8
GMLP_WIDTH = 512
CHUNK = 128
N_Q_HEADS = 8
N_KV_HEADS = 2
ATTN_WIDTH = 512
KV_WIDTH = 128
ROT_DIM = 16
ROPE_THETA = 500000.0
IN_PROJ_WIDTH = 1792
N_MOD = 6
EPS = 1e-5
N_CHIPS = 4
N_DEV = 8
LANES = 128
W_IN_BLOCK = IN_PROJ_WIDTH // N_CHIPS

ADAM_LR = 0.001
ADAM_B1 = 0.9
ADAM_B2 = 0.999
ADAM_EPS = 1e-08
ADAM_WD = 0.01
ADAM_STEP = 10

VMEM_LIMIT_BYTES = 58 * 1024 * 1024
MESH = pl.DeviceIdType.MESH


def _params(*semantics):
    return pltpu.CompilerParams(dimension_semantics=semantics, vmem_limit_bytes=VMEM_LIMIT_BYTES)


def _dot(a, b):
    return jnp.dot(a.astype(MXU_DTYPE), b.astype(MXU_DTYPE), preferred_element_type=F32)


def _dot_nt(a, b):
    return lax.dot_general(a.astype(MXU_DTYPE), b.astype(MXU_DTYPE), (((1,), (1,)), ((), ())),
                           preferred_element_type=F32)


def _dot_tn(a, b):
    return lax.dot_general(a.astype(MXU_DTYPE), b.astype(MXU_DTYPE), (((0,), (0,)), ((), ())),
                           preferred_element_type=F32)


def _full(shape):
    return pl.BlockSpec(shape, lambda *_: (0,) * len(shape))


def _any():
    return pl.BlockSpec(memory_space=pl.ANY)


def _rowsum(v):
    return jnp.sum(v, axis=0, keepdims=True)


def _mean_last(v):
    return jnp.mean(v, axis=-1, keepdims=True)


class _Comm(NamedTuple):
    operands: tuple
    out_shapes: tuple
    n_sems: int
    make: Callable
    in_place: int = 0


def _hosted_call(body, comm, *, name, grid, in_specs, out_shape, out_specs, scratch_shapes=(), semantics,
                 n_prefetch=0):
    if comm is None:
        return pl.pallas_call(
            body, name=name, out_shape=out_shape, compiler_params=_params(*semantics),
            grid_spec=pltpu.PrefetchScalarGridSpec(
                num_scalar_prefetch=n_prefetch, grid=grid, in_specs=in_specs, out_specs=out_specs,
                scratch_shapes=list(scratch_shapes)))
    n_in, n_out, n_scr = len(in_specs), len(out_shape), len(scratch_shapes)
    k_in, k_out = len(comm.operands), len(comm.out_shapes)

    def hosted(*refs):
        prefetched, refs = refs[:n_prefetch], refs[n_prefetch:]
        ins, refs = refs[:n_in], refs[n_in:]
        c_ins, refs = refs[:k_in], refs[k_in:]
        outs, refs = refs[:n_out], refs[n_out:]
        c_outs, refs = refs[:k_out], refs[k_out:]
        scratch, (send_sems, recv_sems) = refs[:n_scr], refs[n_scr:]
        first, last = None, None
        for d, size in enumerate(grid):
            at_start, at_end = pl.program_id(d) == 0, pl.program_id(d) == size - 1
            first = at_start if first is None else first & at_start
            last = at_end if last is None else last & at_end

        @pl.when(first)
        def _():
            for cp in comm.make(c_ins, c_outs, send_sems, recv_sems)[0]:
                cp.start()

        body(*prefetched, *ins, *outs, *scratch)

        @pl.when(last)
        def _():
            for wait in comm.make(c_ins, c_outs, send_sems, recv_sems)[1]:
                wait()

    aliases = {n_prefetch + n_in + i: n_out + i for i in range(comm.in_place)}
    call = pl.pallas_call(
        hosted, name=name, out_shape=list(out_shape) + list(comm.out_shapes),
        compiler_params=_params(*semantics), input_output_aliases=aliases,
        grid_spec=pltpu.PrefetchScalarGridSpec(
            num_scalar_prefetch=n_prefetch, grid=grid, in_specs=list(in_specs) + [_any()] * k_in,
            out_specs=list(out_specs) + [_any()] * k_out,
            scratch_shapes=list(scratch_shapes) + [pltpu.SemaphoreType.DMA((comm.n_sems,)),
                                                    pltpu.SemaphoreType.DMA((comm.n_sems,))]))
    return lambda *args: call(*args, *comm.operands)


class _Shifted:
    def __init__(self, base, offset):
        self.base, self.offset = base, offset

    @property
    def at(self):
        return self

    def __getitem__(self, k):
        return self.base.at[self.offset + k]


def _merge_jobs(*jobs):
    def order(count):
        first = [(j, i) for j, job in enumerate(jobs) for i in range(job.in_place)]
        return first + [(j, i) for j, job in enumerate(jobs) for i in range(job.in_place, count(job))]

    op_order, out_order = order(lambda job: len(job.operands)), order(lambda job: len(job.out_shapes))

    def make(ins, outs, send_sems, recv_sems):
        starts, waits, sem = [], [], 0
        for j, job in enumerate(jobs):
            mine_in = [ins[k] for k, (jj, _) in enumerate(op_order) if jj == j]
            mine_out = [outs[k] for k, (jj, _) in enumerate(out_order) if jj == j]
            s, w = job.make(mine_in, mine_out, _Shifted(send_sems, sem), _Shifted(recv_sems, sem))
            starts, waits, sem = starts + s, waits + w, sem + job.n_sems
        return starts, waits

    return _Comm(tuple(jobs[j].operands[i] for j, i in op_order), tuple(jobs[j].out_shapes[i] for j, i in out_order),
                 sum(job.n_sems for job in jobs), make, in_place=sum(job.in_place for job in jobs))


def _mesh_place():
    x, y, c = lax.axis_index("x"), lax.axis_index("y"), lax.axis_index("c")
    return x, y, c, [(1 - x, y), (x, 1 - y), (1 - x, 1 - y)]


def _gather_job(bufs):
    per = 4

    def make(ins, outs, send_sems, recv_sems):
        del ins
        x, y, c, chips = _mesh_place()
        starts, waits = [], []
        for a, out in enumerate(outs):
            mine = src = out.at[4 * x + 2 * y + c]
            to = [(x, y, 1 - c)] + [(px, py, c) for px, py in chips]
            sends = [pltpu.make_async_remote_copy(
                src_ref=src, dst_ref=mine, send_sem=send_sems.at[per * a + k],
                recv_sem=recv_sems.at[per * a + k], device_id=dev, device_id_type=MESH)
                for k, dev in enumerate(to)]
            recvs = [pltpu.make_async_remote_copy(
                src_ref=src, dst_ref=out.at[4 * px + 2 * py + pc], send_sem=send_sems.at[per * a + k],
                recv_sem=recv_sems.at[per * a + k], device_id=(px, py, pc), device_id_type=MESH)
                for k, (px, py, pc) in enumerate(to)]
            starts += sends
            waits += [s.wait_send for s in sends] + [r.wait_recv for r in recvs]
        return starts, waits

    shapes = tuple(jax.ShapeDtypeStruct(b.shape, b.dtype) for b in bufs)
    return _Comm(tuple(bufs), shapes, per * len(bufs), make, in_place=len(bufs))


def _slots(x, y, c):
    return 4 * x + 2 * y + c, 4 * (1 - x) + 2 * y + c, 4 * x + 2 * (1 - y) + c, 4 * (1 - x) + 2 * (1 - y) + c


def _gather2d_first(halves):
    per = 2

    def make(ins, outs, send_sems, recv_sems):
        x, y, c, _ = _mesh_place()
        me, xn, yn, _ = _slots(x, y, c)
        starts, waits = [], []
        for a, (src, out) in enumerate(zip(ins, outs)):
            blk = src.at[c]
            rows = blk.shape[0] // 2
            upper, lower = pl.ds(0, rows), pl.ds(rows, rows)

            def copy(k, src_ref, dst_ref, dev, a=a):
                return pltpu.make_async_remote_copy(
                    src_ref=src_ref, dst_ref=dst_ref, send_sem=send_sems.at[per * a + k],
                    recv_sem=recv_sems.at[per * a + k], device_id=dev, device_id_type=MESH)

            sends = [copy(0, blk.at[upper], out.at[me, upper], (1 - x, y, c)),
                     copy(1, blk.at[lower], out.at[me, lower], (x, 1 - y, c))]
            recvs = [copy(0, blk.at[upper], out.at[xn, upper], (1 - x, y, c)),
                     copy(1, blk.at[lower], out.at[yn, lower], (x, 1 - y, c))]
            starts += sends
            waits += [s.wait_send for s in sends] + [r.wait_recv for r in recvs]
        return starts, waits

    shapes = tuple(jax.ShapeDtypeStruct((N_DEV,) + h.shape[1:], h.dtype) for h in halves)
    return _Comm(tuple(halves), shapes, per * len(halves), make)


def _gather2d_second(bufs, halves):
    per = 4
    n_arr = len(bufs)

    def make(ins, outs, send_sems, recv_sems):
        x, y, c, _ = _mesh_place()
        me, xn, yn, dg = _slots(x, y, c)
        starts, waits = [], []
        for a, buf in enumerate(outs):
            own = ins[n_arr + a].at[c]
            rows = buf.shape[1] // 2
            upper, lower = pl.ds(0, rows), pl.ds(rows, rows)
            plan = [(own.at[upper], me, upper, (x, 1 - y, c), yn), (buf.at[xn, upper], xn, upper, (x, 1 - y, c), dg),
                    (own.at[lower], me, lower, (1 - x, y, c), xn), (buf.at[yn, lower], yn, lower, (1 - x, y, c), dg)]
            for k, (src, slot, part, dev, landing) in enumerate(plan):
                sems = dict(send_sem=send_sems.at[per * a + k], recv_sem=recv_sems.at[per * a + k],
                            device_id=dev, device_id_type=MESH)
                send = pltpu.make_async_remote_copy(src_ref=src, dst_ref=buf.at[slot, part], **sems)
                arrival = pltpu.make_async_remote_copy(src_ref=src, dst_ref=buf.at[landing, part], **sems)
                starts.append(send)
                waits += [send.wait_send, arrival.wait_recv]
        return starts, waits

    shapes = tuple(jax.ShapeDtypeStruct(b.shape, b.dtype) for b in bufs)
    return _Comm(tuple(bufs) + tuple(halves), shapes, per * n_arr, make, in_place=n_arr)


def _gather_forward(bufs, name):
    n_arr = len(bufs)

    def body(*refs):
        outs = refs[n_arr:2 * n_arr]
        send_sems, recv_sems = refs[2 * n_arr:]
        x, y, c, chips = _mesh_place()
        sends, recvs = [], []
        for a, buf in enumerate(outs):
            for j, (px, py) in enumerate(chips):
                mine, theirs = buf.at[4 * px + 2 * py + c], buf.at[4 * px + 2 * py + 1 - c]
                sems = dict(send_sem=send_sems.at[3 * a + j], recv_sem=recv_sems.at[3 * a + j],
                            device_id=(x, y, 1 - c), device_id_type=MESH)
                sends.append(pltpu.make_async_remote_copy(src_ref=mine, dst_ref=mine, **sems))
                recvs.append(pltpu.make_async_remote_copy(src_ref=mine, dst_ref=theirs, **sems))
        for cp in sends:
            cp.start()
        for s, r in zip(sends, recvs):
            s.wait_send()
            r.wait_recv()

    return pl.pallas_call(
        body, name=name, out_shape=[jax.ShapeDtypeStruct(b.shape, b.dtype) for b in bufs],
        in_specs=[_any()] * n_arr, out_specs=[_any()] * n_arr,
        input_output_aliases={a: a for a in range(n_arr)},
        scratch_shapes=[pltpu.SemaphoreType.DMA((3 * n_arr,)), pltpu.SemaphoreType.DMA((3 * n_arr,))],
    )(*bufs)


def _scatter_job(chip_sums, rows=None, into=()):
    n_into = len(into)

    def part(ref):
        return ref if rows is None else ref.at[pl.ds(rows[0], rows[1])]

    def make(ins, outs, send_sems, recv_sems):
        x, y, c, chips = _mesh_place()
        copies = [pltpu.make_async_remote_copy(
            src_ref=part(src.at[2 * px + py]), dst_ref=part(out.at[j]), send_sem=send_sems.at[3 * a + j],
            recv_sem=recv_sems.at[3 * a + j], device_id=(px, py, c), device_id_type=MESH)
            for a, (src, out) in enumerate(zip(ins[n_into:], outs)) for j, (px, py) in enumerate(chips)]
        return copies, [cp.wait for cp in copies]

    shapes = tuple(jax.ShapeDtypeStruct((3,) + s.shape[1:], s.dtype) for s in chip_sums)
    return _Comm(tuple(into) + tuple(chip_sums), shapes, 3 * len(chip_sums), make, in_place=n_into)


def _all_gather8(blocks, name, split=False, forward=(), riders=(), skip_own=()):
    n_arr, n_fwd = len(blocks), len(forward)
    splits = list(split) if isinstance(split, (list, tuple)) else [split] * n_arr
    own_slots = [a not in skip_own for a in range(n_arr)]
    rider_in = sum(len(r.operands) for r in riders)
    rider_out = sum(len(r.out_shapes) for r in riders)

    def body(*refs):
        x_refs, refs = refs[:n_arr], refs[n_arr + n_fwd:]
        r_ins, refs = refs[:rider_in], refs[rider_in:]
        out_refs, refs = refs[:n_arr], refs[n_arr:]
        fwd_refs, refs = refs[:n_fwd], refs[n_fwd:]
        r_outs, refs = refs[:rider_out], refs[rider_out:]
        (send_sems, recv_sems, local_sems), rider_sems = refs[:3], refs[3:]
        x, y, c, chips = _mesh_place()
        me, sibling = (x, y, c), (x, y, 1 - c)
        passing = []
        for f, buf in enumerate(fwd_refs):
            for j, (px, py) in enumerate(chips):
                mine, theirs = buf.at[4 * px + 2 * py + c], buf.at[4 * px + 2 * py + 1 - c]
                sems = dict(send_sem=send_sems.at[7 * n_arr + 3 * f + j], recv_sem=recv_sems.at[7 * n_arr + 3 * f + j],
                            device_id=sibling, device_id_type=MESH)
                passing.append((pltpu.make_async_remote_copy(src_ref=mine, dst_ref=mine, **sems),
                                pltpu.make_async_remote_copy(src_ref=mine, dst_ref=theirs, **sems)))
        for send, _ in passing:
            send.start()
        arrays = []
        for a, (x_ref, out_ref) in enumerate(zip(x_refs, out_refs)):
            src_mine = x_ref.at[c] if splits[a] else x_ref

            def copy(k, blk, to, src=None, a=a, out_ref=out_ref):
                dst = out_ref.at[4 * blk[0] + 2 * blk[1] + blk[2]]
                return pltpu.make_async_remote_copy(
                    src_ref=dst if src is None else src, dst_ref=dst,
                    send_sem=send_sems.at[7 * a + k], recv_sem=recv_sems.at[7 * a + k],
                    device_id=to, device_id_type=MESH)

            mine = pltpu.make_async_copy(src_mine, out_ref.at[4 * x + 2 * y + c], local_sems.at[a])
            first = [copy(0, me, sibling, src=src_mine)] if own_slots[a] else []
            first += [copy(1 + j, me, (*chip, c), src=src_mine) for j, chip in enumerate(chips)]
            for cp in first + ([mine] if own_slots[a] else []):
                cp.start()
            arrays.append((copy, mine, first, own_slots[a]))
        rider_waits, i0, o0 = [], 0, 0
        for n, job in enumerate(riders):
            k_in, k_out = len(job.operands), len(job.out_shapes)
            starts, waits = job.make(r_ins[i0:i0 + k_in], r_outs[o0:o0 + k_out],
                                     rider_sems[2 * n], rider_sems[2 * n + 1])
            for cp in starts:
                cp.start()
            rider_waits += waits
            i0, o0 = i0 + k_in, o0 + k_out
        sent = []
        for copy, mine, first, own in arrays:
            passed = [copy(4 + j, (*chip, c), sibling) for j, chip in enumerate(chips)]
            for j, chip in enumerate(chips):
                copy(1 + j, (*chip, c), me).wait_recv()
                passed[j].start()
            sent += first + passed
        for copy, mine, first, own in arrays:
            if own:
                copy(0, sibling, me).wait_recv()
                mine.wait()
            for j, chip in enumerate(chips):
                copy(4 + j, (*chip, 1 - c), me).wait_recv()
        for cp in sent:
            cp.wait_send()
        for send, arrival in passing:
            send.wait_send()
            arrival.wait_recv()
        for wait in rider_waits:
            wait()

    n_sems = 7 * n_arr + 3 * n_fwd
    rider_operands = [a for r in riders for a in r.operands]
    rider_shapes = [s for r in riders for s in r.out_shapes]
    return pl.pallas_call(
        body, name=name,
        out_shape=[jax.ShapeDtypeStruct((N_DEV,) + tuple(b.shape[1:] if s else b.shape), b.dtype)
                   for b, s in zip(blocks, splits)]
        + [jax.ShapeDtypeStruct(f.shape, f.dtype) for f in forward] + rider_shapes,
        in_specs=[_any()] * (n_arr + n_fwd + rider_in), out_specs=[_any()] * (n_arr + n_fwd + rider_out),
        input_output_aliases={n_arr + f: n_arr + f for f in range(n_fwd)},
        scratch_shapes=[pltpu.SemaphoreType.DMA((n_sems,)), pltpu.SemaphoreType.DMA((n_sems,)),
                        pltpu.SemaphoreType.DMA((n_arr,))]
        + [pltpu.SemaphoreType.DMA((r.n_sems,)) for r in riders for _ in range(2)],
    )(*blocks, *forward, *rider_operands)


def _split_scatter_copies(src, land, send_sems, recv_sems):
    x, y, c, chips = _mesh_place()
    return [pltpu.make_async_remote_copy(
        src_ref=src.at[2 * px + py], dst_ref=land.at[j], send_sem=send_sems.at[j], recv_sem=recv_sems.at[j],
        device_id=(px, py, c), device_id_type=MESH) for j, (px, py) in enumerate(chips)]


def _scatter_start(chip_sums, name):
    hbm, sem = pl.BlockSpec(memory_space=pltpu.HBM), pl.BlockSpec(memory_space=pltpu.SEMAPHORE)

    def body(src, land, send_sems, recv_sems, src_thru, land_thru, token):
        del src_thru, land_thru
        for cp in _split_scatter_copies(src, land, send_sems, recv_sems):
            cp.start()
        token[...] = jnp.zeros_like(token)

    land = lax.empty((N_CHIPS - 1,) + chip_sums.shape[1:], chip_sums.dtype)
    operands = [pltpu.with_memory_space_constraint(a, pltpu.HBM) for a in (chip_sums, land)]
    return pl.pallas_call(
        body, name=name,
        out_shape=[pltpu.SemaphoreType.DMA((N_CHIPS - 1,)), pltpu.SemaphoreType.DMA((N_CHIPS - 1,))]
        + [pltpu.HBM(a.shape, a.dtype) for a in operands] + [jax.ShapeDtypeStruct((8, LANES), F32)],
        in_specs=[hbm, hbm], out_specs=[sem, sem, hbm, hbm, pl.BlockSpec(memory_space=pltpu.VMEM)],
        input_output_aliases={0: 2, 1: 3},
        compiler_params=pltpu.CompilerParams(has_side_effects=pltpu.SideEffectType.DATAFLOW_SIDE_EFFECTING),
    )(*operands)


def _scatter_wait(started, after, name):
    send_sems, recv_sems, src, land, _ = started
    hbm, sem = pl.BlockSpec(memory_space=pltpu.HBM), pl.BlockSpec(memory_space=pltpu.SEMAPHORE)

    def body(src, land, send_sems, recv_sems, after_ref, src_thru, land_thru):
        del after_ref, src_thru, land_thru
        for cp in _split_scatter_copies(src, land, send_sems, recv_sems):
            cp.wait()

    return pl.pallas_call(
        body, name=name, out_shape=[pltpu.HBM(src.shape, src.dtype), pltpu.HBM(land.shape, land.dtype)],
        in_specs=[hbm, hbm, sem, sem, _any()], out_specs=[hbm, hbm], input_output_aliases={0: 0, 1: 1},
        compiler_params=pltpu.CompilerParams(has_side_effects=pltpu.SideEffectType.DATAFLOW_SIDE_EFFECTING),
    )(src, land, send_sems, recv_sems, after)


def _split_gather_copies(srcs, lands, fwds, send_sems, recv_sems):
    x, y, c, chips = _mesh_place()
    peers = [(x, y, 1 - c)] + [(px, py, pc) for px, py in chips for pc in (c, 1 - c)]
    pairs = []
    for a, (src, land) in enumerate(zip(srcs, lands)):
        for k, (px, py, pc) in enumerate(peers):
            sems = dict(send_sem=send_sems.at[7 * a + k], recv_sem=recv_sems.at[7 * a + k],
                        device_id=(px, py, pc), device_id_type=MESH)
            pairs.append((pltpu.make_async_remote_copy(src_ref=src, dst_ref=land.at[4 * x + 2 * y + c], **sems),
                          pltpu.make_async_remote_copy(src_ref=src, dst_ref=land.at[4 * px + 2 * py + pc], **sems)))
    for f, buf in enumerate(fwds):
        for j, (px, py) in enumerate(chips):
            mine, theirs = buf.at[4 * px + 2 * py + c], buf.at[4 * px + 2 * py + 1 - c]
            k = 7 * len(srcs) + 3 * f + j
            sems = dict(send_sem=send_sems.at[k], recv_sem=recv_sems.at[k],
                        device_id=(x, y, 1 - c), device_id_type=MESH)
            pairs.append((pltpu.make_async_remote_copy(src_ref=mine, dst_ref=mine, **sems),
                          pltpu.make_async_remote_copy(src_ref=mine, dst_ref=theirs, **sems)))
    return pairs


def _gather_start(blocks, forward, after, name):
    n, n_fwd = len(blocks), len(forward)
    n_sems = 7 * n + 3 * n_fwd
    hbm, sem = pl.BlockSpec(memory_space=pltpu.HBM), pl.BlockSpec(memory_space=pltpu.SEMAPHORE)

    def body(*refs):
        srcs, lands, fwds = refs[:n], refs[n:2 * n], refs[2 * n:2 * n + n_fwd]
        send_sems, recv_sems = refs[2 * n + n_fwd + 1:2 * n + n_fwd + 3]
        token, local_sems = refs[-2:]
        x, y, c, _ = _mesh_place()
        own = [pltpu.make_async_copy(src, land.at[4 * x + 2 * y + c], local_sems.at[a])
               for a, (src, land) in enumerate(zip(srcs, lands))]
        for cp in own:
            cp.start()
        for send, _ in _split_gather_copies(srcs, lands, fwds, send_sems, recv_sems):
            send.start()
        token[...] = jnp.zeros_like(token)
        for cp in own:
            cp.wait()

    lands = [lax.empty((N_DEV,) + b.shape, b.dtype) for b in blocks]
    operands = [pltpu.with_memory_space_constraint(a, pltpu.HBM) for a in list(blocks) + lands + list(forward)]
    return pl.pallas_call(
        body, name=name,
        out_shape=[pltpu.SemaphoreType.DMA((n_sems,)), pltpu.SemaphoreType.DMA((n_sems,))]
        + [pltpu.HBM(a.shape, a.dtype) for a in operands] + [jax.ShapeDtypeStruct((8, LANES), F32)],
        in_specs=[hbm] * len(operands) + [_any()],
        out_specs=[sem, sem] + [hbm] * len(operands) + [pl.BlockSpec(memory_space=pltpu.VMEM)],
        input_output_aliases={i: 2 + i for i in range(len(operands))},
        scratch_shapes=[pltpu.SemaphoreType.DMA((n,))],
        compiler_params=pltpu.CompilerParams(has_side_effects=pltpu.SideEffectType.DATAFLOW_SIDE_EFFECTING),
    )(*operands, after)


def _gather_wait(started, n, after, name):
    send_sems, recv_sems, *bufs, _ = started
    n_fwd = len(bufs) - 2 * n
    hbm, sem = pl.BlockSpec(memory_space=pltpu.HBM), pl.BlockSpec(memory_space=pltpu.SEMAPHORE)

    def body(*refs):
        srcs, lands, fwds = refs[:n], refs[n:2 * n], refs[2 * n:2 * n + n_fwd]
        send_sems, recv_sems = refs[2 * n + n_fwd:2 * n + n_fwd + 2]
        for send, arrival in _split_gather_copies(srcs, lands, fwds, send_sems, recv_sems):
            send.wait_send()
            arrival.wait_recv()

    out = pl.pallas_call(
        body, name=name, out_shape=[pltpu.HBM(b.shape, b.dtype) for b in bufs],
        in_specs=[hbm] * len(bufs) + [sem, sem, _any()], out_specs=[hbm] * len(bufs),
        input_output_aliases={i: i for i in range(len(bufs))},
        compiler_params=pltpu.CompilerParams(has_side_effects=pltpu.SideEffectType.DATAFLOW_SIDE_EFFECTING),
    )(*bufs, send_sems, recv_sems, after)
    return out[n:]


def _share_job(bufs):
    def make(ins, outs, send_sems, recv_sems):
        del ins
        x, y, c, _ = _mesh_place()
        sems = lambda a: dict(send_sem=send_sems.at[a], recv_sem=recv_sems.at[a],
                              device_id=(x, y, 1 - c), device_id_type=MESH)
        sends = [pltpu.make_async_remote_copy(src_ref=o.at[c], dst_ref=o.at[c], **sems(a)) for a, o in enumerate(outs)]
        arrivals = [pltpu.make_async_remote_copy(src_ref=o.at[c], dst_ref=o.at[1 - c], **sems(a))
                    for a, o in enumerate(outs)]
        return sends, [s.wait_send for s in sends] + [r.wait_recv for r in arrivals]

    shapes = tuple(jax.ShapeDtypeStruct(b.shape, b.dtype) for b in bufs)
    return _Comm(tuple(bufs), shapes, len(bufs), make, in_place=len(bufs))


def _sibling_share(bufs, name):
    n_arr = len(bufs)

    def body(*refs):
        out_refs = refs[n_arr:2 * n_arr]
        send_sems, recv_sems = refs[2 * n_arr:]
        x, y, c = lax.axis_index("x"), lax.axis_index("y"), lax.axis_index("c")
        copies = [pltpu.make_async_remote_copy(
            src_ref=out_refs[a].at[c], dst_ref=out_refs[a].at[c],
            send_sem=send_sems.at[a], recv_sem=recv_sems.at[a],
            device_id=(x, y, 1 - c), device_id_type=MESH) for a in range(n_arr)]
        for cp in copies:
            cp.start()
        for a in range(n_arr):
            pltpu.make_async_remote_copy(
                src_ref=out_refs[a].at[c], dst_ref=out_refs[a].at[1 - c],
                send_sem=send_sems.at[a], recv_sem=recv_sems.at[a],
                device_id=(x, y, 1 - c), device_id_type=MESH).wait()

    return pl.pallas_call(
        body, name=name,
        out_shape=[jax.ShapeDtypeStruct(b.shape, b.dtype) for b in bufs],
        in_specs=[_any()] * n_arr, out_specs=[_any()] * n_arr,
        input_output_aliases={a: a for a in range(n_arr)},
        scratch_shapes=[pltpu.SemaphoreType.DMA((n_arr,)), pltpu.SemaphoreType.DMA((n_arr,))],
    )(*bufs)


def _gelu_tanh(z):
    k = math.sqrt(2.0 / math.pi)
    t = jnp.tanh(k * (z + 0.044715 * (z * z * z)))
    return 0.5 * z * (1.0 + t), t


def _gelu_tanh_grad(z, t):
    k = math.sqrt(2.0 / math.pi)
    return 0.5 * (1.0 + t) + 0.5 * z * (1.0 - t * t) * (k * (1.0 + 3.0 * 0.044715 * (z * z)))


def _rope_angle_kernel(pos_row, invf_col):
    seq = pos_row.shape[1]

    def body(p_ref, f_ref, cos_ref, sin_ref):
        ang = p_ref[...].astype(F32) * f_ref[...]
        cos_ref[...] = jnp.cos(ang)
        sin_ref[...] = jnp.sin(ang)

    return pl.pallas_call(
        body, name="rope_angles", grid=(1,), out_shape=[jax.ShapeDtypeStruct((ROT_DIM // 2, seq), F32)] * 2,
        in_specs=[_full((1, seq)), _full((ROT_DIM // 2, 1))], out_specs=[_full((ROT_DIM // 2, seq))] * 2,
        compiler_params=_params("arbitrary"),
    )(pos_row, invf_col)


def _rope_lane_tables(cos, sin):
    cos_t, sin_t = cos.T, sin.T
    seq, half = cos_t.shape
    ones = jnp.ones((seq, HEAD_DIM - ROT_DIM), F32)
    c64 = jnp.concatenate([cos_t, cos_t, ones], axis=1)
    s1 = jnp.concatenate([sin_t, jnp.zeros((seq, HEAD_DIM - half), F32)], axis=1)
    s2 = jnp.concatenate([jnp.zeros((seq, half), F32), sin_t, jnp.zeros((seq, HEAD_DIM - ROT_DIM), F32)], axis=1)
    return jnp.concatenate([jnp.tile(t, (1, LANES // HEAD_DIM)) for t in (c64, s1, s2)], axis=1)


def _rope_apply(t, tab, sign):
    reps = t.shape[1] // LANES
    c_tab, s1, s2 = (jnp.tile(tab[:, LANES * k:LANES * (k + 1)], (1, reps)) if reps > 1
                     else tab[:, LANES * k:LANES * (k + 1)] for k in range(3))
    half = ROT_DIM // 2
    up = pltpu.roll(t, t.shape[1] - half, 1)
    down = pltpu.roll(t, half, 1)
    return t * c_tab + sign * (down * s2 - up * s1)


def _lane_masks(shape):
    lane = lax.broadcasted_iota(jnp.int32, shape, 1)
    return lane < HEAD_DIM, lane >= HEAD_DIM


HEADS_PER_GROUP = N_Q_HEADS // N_KV_HEADS
ATTN_SCALE = 1.0 / math.sqrt(HEAD_DIM)


def _attn_bias_t(first_block):
    kj = lax.broadcasted_iota(jnp.int32, (2 * CHUNK, CHUNK), 0)
    qi = lax.broadcasted_iota(jnp.int32, (2 * CHUNK, CHUNK), 1)
    ok = (kj > qi) & (kj <= qi + CHUNK)
    if first_block is not None:
        ok = ok & (jnp.logical_not(first_block) | (kj >= CHUNK))
    return jnp.tile(jnp.where(ok, 0.0, -jnp.inf), (1, HEADS_PER_GROUP))


def _group_rows(x, g, lo, hi):
    rows = []
    for r in range(HEADS_PER_GROUP):
        h = HEADS_PER_GROUP * g + r
        pair = x[:, LANES * (h // 2):LANES * (h // 2 + 1)]
        rows.append(jnp.where(hi if h % 2 else lo, pair, 0.0))
    return jnp.concatenate(rows, axis=0)


def _pairs_from_rows(rows, lo):
    return [jnp.where(lo, rows[2 * CHUNK * k:2 * CHUNK * k + CHUNK], rows[2 * CHUNK * k + CHUNK:2 * CHUNK * (k + 1)])
            for k in range(HEADS_PER_GROUP // 2)]


def _group_dup(a, b, g, lo2):
    return jnp.where(lo2, a, b) if g == 0 else jnp.where(lo2, b, a)


def _sink_row(sink_ref, g):
    return jnp.concatenate([sink_ref[HEADS_PER_GROUP * g + r:HEADS_PER_GROUP * g + r + 1, :]
                            for r in range(HEADS_PER_GROUP)], axis=1)


def _attn_probs_t(k_dup, q_rows, bias_t, sink_row):
    s_t = _dot_nt(k_dup, q_rows) * ATTN_SCALE + bias_t
    m = jnp.maximum(jnp.max(s_t, axis=0, keepdims=True), sink_row)
    p = jnp.exp(s_t - m)
    e_sink = jnp.exp(sink_row - m)
    inv = 1.0 / (jnp.sum(p, axis=0, keepdims=True) + e_sink)
    return p * inv, e_sink * inv


def _sgu_forward_pair(wm, vp, j):
    lo, hi = _lane_masks(vp.shape)
    lhs = jnp.concatenate([wm[2 * j], wm[2 * j + 1]], axis=1)
    rhs = jnp.concatenate([jnp.where(lo, vp, 0.0), jnp.where(hi, vp, 0.0)], axis=0)
    return _dot(lhs, rhs)


def _masked_spatial(w_ref):
    t = lax.broadcasted_iota(jnp.int32, (CHUNK, CHUNK), 0)
    s = lax.broadcasted_iota(jnp.int32, (CHUNK, CHUNK), 1)
    tril = s <= t
    return [jnp.where(tril, w_ref[g], 0.0) for g in range(GMLP_GROUPS)], tril, s >= t


def _mod_kernel(c_all, w_shard, b_shard, comm=None):
    n = w_shard.shape[1]
    tn = 512

    def body(c_ref, w_ref, b_ref, mod_ref, act_ref):
        cv = c_ref[...]
        act = cv * (1.0 / (1.0 + jnp.exp(-cv)))
        act_ref[...] = act
        mod_ref[...] = _dot(act, w_ref[...]) + b_ref[...]

    return _hosted_call(
        body, comm, name="ada_mod", grid=(n // tn,),
        out_shape=[jax.ShapeDtypeStruct((N_DEV, n), F32), jax.ShapeDtypeStruct((N_DEV, D_MODEL), F32)],
        in_specs=[_full((N_DEV, D_MODEL)), pl.BlockSpec((D_MODEL, tn), lambda i: (0, i)),
                  pl.BlockSpec((1, tn), lambda i: (0, i))],
        out_specs=[pl.BlockSpec((N_DEV, tn), lambda i: (0, i)), _full((N_DEV, D_MODEL))],
        semantics=("arbitrary",),
    )(c_all, w_shard, b_shard)


def _load_chip_blocks(chip_ref, gathered, local, dsts, sems, first_sem=0):
    for k, dst in enumerate(dsts):
        @pl.when(chip_ref[0] == k)
        def _():
            pltpu.make_async_copy(local, dst, sems.at[first_sem + k]).start()

        @pl.when(chip_ref[0] != k)
        def _():
            pltpu.make_async_copy(gathered.at[k], dst, sems.at[first_sem + k]).start()
    return [pltpu.make_async_copy(local, dst, sems.at[first_sem + k]).wait for k, dst in enumerate(dsts)]


def _in_proj_kernel(x, vecs, w_in_t, comm=None):
    seq = x.shape[0]
    tm = 512

    def body(x_ref, v_ref, w_ref, proj_ref, h_ref):
        xv = x_ref[...]
        rstd = lax.rsqrt(_mean_last(xv * xv) + EPS)
        n1 = (xv * rstd) * v_ref[0:1, :]
        h = n1 * (1.0 + v_ref[2:3, :]) + v_ref[1:2, :]
        hb = h.astype(MXU_DTYPE)
        h_ref[...] = hb
        proj_ref[...] = _dot_nt(hb, w_ref[...])

    return _hosted_call(
        body, comm, name="in_proj", grid=(seq // tm,),
        out_shape=[jax.ShapeDtypeStruct((seq, IN_PROJ_WIDTH), F32),
                   jax.ShapeDtypeStruct((seq, D_MODEL), MXU_DTYPE)],
        in_specs=[pl.BlockSpec((tm, D_MODEL), lambda i: (i, 0)), _full((8, D_MODEL)),
                  _full((IN_PROJ_WIDTH, D_MODEL))],
        out_specs=[pl.BlockSpec((tm, IN_PROJ_WIDTH), lambda i: (i, 0)),
                   pl.BlockSpec((tm, D_MODEL), lambda i: (i, 0))],
        semantics=("arbitrary",),
    )(x, vecs, w_in_t)


MIXER_BLOCKS_PER_STEP = 4
KV_START = 2 * GMLP_WIDTH + ATTN_WIDTH


def _mixer_fwd_kernel(proj, rope_tab, w_spatial, bias_full, sink_rows, comm=None):
    seq = proj.shape[0]
    per = MIXER_BLOCKS_PER_STEP
    steps = seq // (CHUNK * per)
    kv_col = KV_START // (2 * KV_WIDTH)

    def body(proj_ref, prev_ref, tab_ref, ptab_ref, w_ref, bias_ref, sink_ref, cat_ref):
        i = pl.program_id(0)
        wm, _, _ = _masked_spatial(w_ref)
        lo, hi = _lane_masks((CHUNK, LANES))
        lo2, _ = _lane_masks((2 * CHUNK, LANES))
        o = 2 * GMLP_WIDTH
        for s in range(per):
            rows, before = slice(CHUNK * s, CHUNK * (s + 1)), slice(CHUNK * (s - 1), CHUNK * s)
            for j in range(GMLP_GROUPS // 2):
                cols = slice(LANES * j, LANES * (j + 1))
                vcols = slice(GMLP_WIDTH + LANES * j, GMLP_WIDTH + LANES * (j + 1))
                u, _ = _gelu_tanh(proj_ref[rows, cols])
                vp, _ = _gelu_tanh(proj_ref[rows, vcols])
                sv = _sgu_forward_pair(wm, vp, j) + bias_ref[:, cols]
                cat_ref[rows, cols] = (u * sv).astype(cat_ref.dtype)
            tab = tab_ref[rows, :]
            if s == 0:
                prev_kv, prev_tab, first = prev_ref[...], ptab_ref[...], i == 0
            else:
                prev_kv, prev_tab, first = proj_ref[before, KV_START:KV_START + 2 * KV_WIDTH], tab_ref[before, :], None
            q_r = _rope_apply(proj_ref[rows, o:o + ATTN_WIDTH], tab, 1.0)
            k_cur = _rope_apply(proj_ref[rows, KV_START:KV_START + KV_WIDTH], tab, 1.0)
            k_prev = _rope_apply(prev_kv[:, 0:KV_WIDTH], prev_tab, 1.0)
            k_a = jnp.concatenate([k_prev, k_cur], axis=0)
            v_a = jnp.concatenate([prev_kv[:, KV_WIDTH:2 * KV_WIDTH],
                                   proj_ref[rows, KV_START + KV_WIDTH:KV_START + 2 * KV_WIDTH]], axis=0)
            k_b = pltpu.roll(k_a, HEAD_DIM, 1)
            v_b = pltpu.roll(v_a, HEAD_DIM, 1)
            bias_t = _attn_bias_t(first)
            for g in range(N_KV_HEADS):
                p_t, _ = _attn_probs_t(_group_dup(k_a, k_b, g, lo2), _group_rows(q_r, g, lo, hi), bias_t,
                                       _sink_row(sink_ref, g))
                o_t = _dot(_group_dup(v_a, v_b, g, lo2).T, p_t)
                for k, pair in enumerate(_pairs_from_rows(o_t.T, lo)):
                    c0 = GMLP_WIDTH + LANES * (2 * g + k)
                    cat_ref[rows, c0:c0 + LANES] = pair.astype(cat_ref.dtype)

    return _hosted_call(
        body, comm, name="mixer_fwd", grid=(steps,),
        out_shape=[jax.ShapeDtypeStruct((seq, D_MODEL), MXU_DTYPE)],
        in_specs=[pl.BlockSpec((CHUNK * per, IN_PROJ_WIDTH), lambda i: (i, 0)),
                  pl.BlockSpec((CHUNK, 2 * KV_WIDTH), lambda i: (jnp.maximum(per * i - 1, 0), kv_col)),
                  pl.BlockSpec((CHUNK * per, 3 * LANES), lambda i: (i, 0)),
                  pl.BlockSpec((CHUNK, 3 * LANES), lambda i: (jnp.maximum(per * i - 1, 0), 0)),
                  _full((GMLP_GROUPS, CHUNK, CHUNK)), _full((CHUNK, GMLP_WIDTH)),
                  _full((N_Q_HEADS, LANES))],
        out_specs=[pl.BlockSpec((CHUNK * per, D_MODEL), lambda i: (i, 0))],
        semantics=("arbitrary",),
    )(proj, proj, rope_tab, rope_tab, w_spatial, bias_full, sink_rows)


def _trunk_kernel(x, target, cat, vecs, chip_idx, gathered, local):
    seq = x.shape[0]
    tm = 256
    nj = D_FF // D_MODEL
    out_rows = D_MODEL // N_CHIPS

    def body(chip_ref, x_ref, t_ref, cat_ref, v_ref, g_out, g_w1, g_w2, l_out, l_w1, l_w2,
             dx1_ref, dcat_ref, dmix_ref, h2_ref, r_ref, da_ref, dff_ref, sums_ref,
             wout, w1, w2, a_scr, sem):
        i = pl.program_id(0)

        @pl.when(i == 0)
        def _():
            waits = _load_chip_blocks(chip_ref, g_out, l_out,
                                      [wout.at[pl.ds(out_rows * k, out_rows)] for k in range(N_CHIPS)], sem)
            waits += _load_chip_blocks(chip_ref, g_w1, l_w1, [w1.at[k] for k in range(N_CHIPS)], sem, N_CHIPS)
            waits += _load_chip_blocks(chip_ref, g_w2, l_w2, [w2.at[k] for k in range(N_CHIPS)], sem, 2 * N_CHIPS)
            for wait in waits:
                wait()
            sums_ref[...] = jnp.zeros_like(sums_ref)

        gate1, shift2, scale2 = v_ref[0:1, :], v_ref[1:2, :], v_ref[2:3, :]
        gate2, g_ffn, g_final = v_ref[3:4, :], v_ref[4:5, :], v_ref[5:6, :]

        mix = _dot(cat_ref[...], wout[...])
        x1 = x_ref[...] + gate1 * mix
        rstd2 = lax.rsqrt(_mean_last(x1 * x1) + EPS)
        xh2 = x1 * rstd2
        n2 = xh2 * g_ffn
        h2b = (n2 * (1.0 + scale2) + shift2).astype(MXU_DTYPE)
        h2_ref[...] = h2b
        ff = jnp.zeros((tm, D_MODEL), F32)
        for j in range(nj):
            a = _dot(h2b, w1[j])
            a_scr[j] = a
            relu = jnp.maximum(a, 0.0)
            rb = (relu * relu).astype(MXU_DTYPE)
            r_ref[:, D_MODEL * j:D_MODEL * (j + 1)] = rb
            ff = ff + _dot(rb, w2[j])
        x2 = x1 + gate2 * ff
        rstd3 = lax.rsqrt(_mean_last(x2 * x2) + EPS)
        xh3 = x2 * rstd3
        err = xh3 * g_final - t_ref[...]
        loss = 0.5 * _rowsum(_mean_last(err * err))
        dy = err * (1.0 / D_MODEL)
        dxh3 = dy * g_final
        dx2 = rstd3 * (dxh3 - xh3 * _mean_last(dxh3 * xh3))
        dffb = (dx2 * gate2).astype(MXU_DTYPE)
        dff_ref[...] = dffb
        dh2 = jnp.zeros((tm, D_MODEL), F32)
        for j in range(nj):
            dr = _dot_nt(dffb, w2[j])
            dab = (dr * (2.0 * jnp.maximum(a_scr[j], 0.0))).astype(MXU_DTYPE)
            da_ref[:, D_MODEL * j:D_MODEL * (j + 1)] = dab
            dh2 = dh2 + _dot_nt(dab, w1[j])
        dn2 = dh2 * (1.0 + scale2)
        dxh2 = dn2 * g_ffn
        dx1 = dx2 + rstd2 * (dxh2 - xh2 * _mean_last(dxh2 * xh2))
        dx1_ref[...] = dx1
        dmixb = (dx1 * gate1).astype(MXU_DTYPE)
        dmix_ref[...] = dmixb
        dcat_ref[...] = _dot_nt(dmixb, wout[...])

        sums_ref[0:1, :] += _rowsum(dh2)
        sums_ref[1:2, :] += _rowsum(dh2 * n2)
        sums_ref[2:3, :] += _rowsum(dx2 * ff)
        sums_ref[3:4, :] += _rowsum(dn2 * xh2)
        sums_ref[4:5, :] += _rowsum(dy * xh3)
        sums_ref[5:6, :] += _rowsum(dx1 * mix)
        sums_ref[6:7, :] += jnp.broadcast_to(loss, (1, D_MODEL))

    tok = lambda w: pl.BlockSpec((tm, w), lambda i, chip: (i, 0))
    return _hosted_call(
        body, None, name="trunk", grid=(seq // tm,), n_prefetch=1,
        out_shape=[jax.ShapeDtypeStruct((seq, D_MODEL), F32), jax.ShapeDtypeStruct((seq, D_MODEL), F32),
                   jax.ShapeDtypeStruct((seq, D_MODEL), MXU_DTYPE), jax.ShapeDtypeStruct((seq, D_MODEL), MXU_DTYPE),
                   jax.ShapeDtypeStruct((seq, D_FF), MXU_DTYPE), jax.ShapeDtypeStruct((seq, D_FF), MXU_DTYPE),
                   jax.ShapeDtypeStruct((seq, D_MODEL), MXU_DTYPE), jax.ShapeDtypeStruct((8, D_MODEL), F32)],
        in_specs=[tok(D_MODEL), tok(D_MODEL), tok(D_MODEL), _full((8, D_MODEL))] + [_any()] * 6,
        out_specs=[tok(D_MODEL), tok(D_MODEL), tok(D_MODEL), tok(D_MODEL), tok(D_FF), tok(D_FF), tok(D_MODEL),
                   _full((8, D_MODEL))],
        scratch_shapes=[pltpu.VMEM((D_MODEL, D_MODEL), MXU_DTYPE), pltpu.VMEM((nj, D_MODEL, D_MODEL), MXU_DTYPE),
                        pltpu.VMEM((nj, D_MODEL, D_MODEL), MXU_DTYPE), pltpu.VMEM((nj, tm, D_MODEL), F32),
                        pltpu.SemaphoreType.DMA((3 * N_CHIPS,))],
        semantics=("arbitrary",),
    )(chip_idx, x, target, cat, vecs, *gathered, *local)


def _mixer_bwd_kernel(proj, rope_tab, dcat, w_spatial, w_spatial_t, bias_full, sink_rows, dev_idx, comm=None):
    seq = proj.shape[0]
    per = MIXER_BLOCKS_PER_STEP
    steps = seq // (CHUNK * per)
    kv_col = KV_START // (2 * KV_WIDTH)

    def body(dev_ref, proj_ref, prev_ref, tab_ref, ptab_ref, dcat_ref, w_ref, wt_ref, bias_ref, sink_ref,
             dproj_ref, dw_out, db_ref, dsink_ref, carry, dw_ref):
        del dev_ref
        step = pl.program_id(0)

        @pl.when(step == 0)
        def _():
            carry[...] = jnp.zeros_like(carry)
            dw_ref[...] = jnp.zeros_like(dw_ref)
            db_ref[...] = jnp.zeros_like(db_ref)
            dsink_ref[...] = jnp.zeros_like(dsink_ref)

        for s in reversed(range(per)):
            rows = pl.ds(CHUNK * s, CHUNK)
            if s == 0:
                before, before_tab, first = prev_ref, ptab_ref, step == steps - 1
            else:
                before = proj_ref.at[pl.ds(CHUNK * (s - 1), CHUNK), pl.ds(KV_START, 2 * KV_WIDTH)]
                before_tab, first = tab_ref.at[pl.ds(CHUNK * (s - 1), CHUNK)], None
            one_block(proj_ref.at[rows], before, tab_ref.at[rows], before_tab, dcat_ref.at[rows], w_ref, wt_ref,
                      bias_ref, sink_ref, dproj_ref.at[rows], dw_ref, db_ref, dsink_ref, carry, first)

        @pl.when(step == steps - 1)
        def _():
            dw_out[...] = dw_ref[...].astype(dw_out.dtype)

    def one_block(proj_ref, prev_ref, tab_ref, ptab_ref, dcat_ref, w_ref, wt_ref, bias_ref, sink_ref,
                  dproj_ref, dw_ref, db_ref, dsink_ref, carry, first):
        wm, tril, triu = _masked_spatial(w_ref)
        lo, hi = _lane_masks((CHUNK, LANES))
        lane = lax.broadcasted_iota(jnp.int32, (CHUNK, LANES), 1)
        db = jnp.zeros((CHUNK, LANES), F32)
        for j in range(GMLP_GROUPS // 2):
            cols = slice(LANES * j, LANES * (j + 1))
            vcols = slice(GMLP_WIDTH + LANES * j, GMLP_WIDTH + LANES * (j + 1))
            zu, zv = proj_ref[:, cols], proj_ref[:, vcols]
            u, tu = _gelu_tanh(zu)
            vp, tv = _gelu_tanh(zv)
            sv = _sgu_forward_pair(wm, vp, j) + bias_ref[:, cols]
            dout = dcat_ref[:, cols]
            du = dout * sv
            dsv = dout * u
            dsv_lo, dsv_hi = jnp.where(lo, dsv, 0.0), jnp.where(hi, dsv, 0.0)
            lhs_t = jnp.concatenate([jnp.where(triu, wt_ref[2 * j], 0.0),
                                     jnp.where(triu, wt_ref[2 * j + 1], 0.0)], axis=1)
            dv = _dot(lhs_t, jnp.concatenate([dsv_lo, dsv_hi], axis=0))
            dw_ref[2 * j] += jnp.where(tril, _dot_nt(dsv_lo, vp), 0.0)
            dw_ref[2 * j + 1] += jnp.where(tril, _dot_nt(dsv_hi, vp), 0.0)
            db = db + (jnp.where(lane == 2 * j, jnp.sum(dsv_lo, axis=1, keepdims=True), 0.0)
                       + jnp.where(lane == 2 * j + 1, jnp.sum(dsv_hi, axis=1, keepdims=True), 0.0))
            dproj_ref[:, cols] = (du * _gelu_tanh_grad(zu, tu)).astype(dproj_ref.dtype)
            dproj_ref[:, vcols] = (dv * _gelu_tanh_grad(zv, tv)).astype(dproj_ref.dtype)
        db_ref[...] += db
        o = 2 * GMLP_WIDTH
        tab = tab_ref[...]
        q_r = _rope_apply(proj_ref[:, o:o + ATTN_WIDTH], tab, 1.0)
        k_cur = _rope_apply(proj_ref[:, o + ATTN_WIDTH:o + ATTN_WIDTH + KV_WIDTH], tab, 1.0)
        k_prev = _rope_apply(prev_ref[:, 0:KV_WIDTH], ptab_ref[...], 1.0)
        k_a = jnp.concatenate([k_prev, k_cur], axis=0)
        v_a = jnp.concatenate([prev_ref[:, KV_WIDTH:2 * KV_WIDTH],
                               proj_ref[:, o + ATTN_WIDTH + KV_WIDTH:o + ATTN_WIDTH + 2 * KV_WIDTH]], axis=0)
        k_b = pltpu.roll(k_a, HEAD_DIM, 1)
        v_b = pltpu.roll(v_a, HEAD_DIM, 1)
        bias_t = _attn_bias_t(first)
        lo2, _ = _lane_masks((2 * CHUNK, LANES))
        dout_b = dcat_ref[:, GMLP_WIDTH:GMLP_WIDTH + ATTN_WIDTH]
        dk_tot, dv_tot, dq_pairs = [], [], []
        for g in range(N_KV_HEADS):
            k_dup, v_dup = _group_dup(k_a, k_b, g, lo2), _group_dup(v_a, v_b, g, lo2)
            q_rows = _group_rows(q_r, g, lo, hi)
            do_rows = _group_rows(dout_b, g, lo, hi)
            p_t, p_sink = _attn_probs_t(k_dup, q_rows, bias_t, _sink_row(sink_ref, g))
            dp_t = _dot_nt(v_dup, do_rows)
            delta = jnp.sum(p_t * dp_t, axis=0, keepdims=True)
            ds_t = p_t * (dp_t - delta) * ATTN_SCALE
            dsink = -p_sink * delta
            for r in range(HEADS_PER_GROUP):
                h = HEADS_PER_GROUP * g + r
                dsink_ref[h:h + 1, :] += jnp.broadcast_to(
                    jnp.sum(dsink[:, LANES * r:LANES * (r + 1)], axis=1, keepdims=True), (1, LANES))
            dk_full = _dot(ds_t, q_rows)
            dv_full = _dot(p_t, do_rows)
            dk_tot.append(dk_full + pltpu.roll(dk_full, HEAD_DIM, 1))
            dv_tot.append(dv_full + pltpu.roll(dv_full, HEAD_DIM, 1))
            dq_t = _dot(k_dup.T, ds_t)
            dq_pairs += _pairs_from_rows(dq_t.T, lo)
        dk_all = jnp.where(lo2, dk_tot[0], dk_tot[1])
        dv_all = jnp.where(lo2, dv_tot[0], dv_tot[1])
        dk_cur = dk_all[CHUNK:, :] + carry[:, 0:KV_WIDTH]
        dv_cur = dv_all[CHUNK:, :] + carry[:, KV_WIDTH:2 * KV_WIDTH]
        carry[:, 0:KV_WIDTH] = dk_all[:CHUNK, :]
        carry[:, KV_WIDTH:2 * KV_WIDTH] = dv_all[:CHUNK, :]
        dq = _rope_apply(jnp.concatenate(dq_pairs, axis=1), tab, -1.0)
        dproj_ref[:, o:o + ATTN_WIDTH] = dq.astype(dproj_ref.dtype)
        dproj_ref[:, o + ATTN_WIDTH:o + ATTN_WIDTH + KV_WIDTH] = (
            _rope_apply(dk_cur, tab, -1.0).astype(dproj_ref.dtype))
        dproj_ref[:, o + ATTN_WIDTH + KV_WIDTH:o + ATTN_WIDTH + 2 * KV_WIDTH] = dv_cur.astype(dproj_ref.dtype)

    rev = lambda i: steps - 1 - i
    before = lambda i: jnp.maximum(per * rev(i) - 1, 0)
    slot = lambda shape: pl.BlockSpec((None,) + shape, lambda i, d: (d[0],) + (0,) * len(shape))
    return _hosted_call(
        body, comm, name="mixer_bwd", grid=(steps,), n_prefetch=1,
        out_shape=[jax.ShapeDtypeStruct((seq, IN_PROJ_WIDTH), MXU_DTYPE),
                   jax.ShapeDtypeStruct((N_DEV, GMLP_GROUPS, CHUNK, CHUNK), GRAD_COMM_DTYPE),
                   jax.ShapeDtypeStruct((N_DEV, CHUNK, LANES), F32),
                   jax.ShapeDtypeStruct((N_DEV, N_Q_HEADS, LANES), F32)],
        in_specs=[pl.BlockSpec((CHUNK * per, IN_PROJ_WIDTH), lambda i, d: (rev(i), 0)),
                  pl.BlockSpec((CHUNK, 2 * KV_WIDTH), lambda i, d: (before(i), kv_col)),
                  pl.BlockSpec((CHUNK * per, 3 * LANES), lambda i, d: (rev(i), 0)),
                  pl.BlockSpec((CHUNK, 3 * LANES), lambda i, d: (before(i), 0)),
                  pl.BlockSpec((CHUNK * per, D_MODEL), lambda i, d: (rev(i), 0)),
                  _full((GMLP_GROUPS, CHUNK, CHUNK)), _full((GMLP_GROUPS, CHUNK, CHUNK)),
                  _full((CHUNK, GMLP_WIDTH)), _full((N_Q_HEADS, LANES))],
        out_specs=[pl.BlockSpec((CHUNK * per, IN_PROJ_WIDTH), lambda i, d: (rev(i), 0)),
                   slot((GMLP_GROUPS, CHUNK, CHUNK)), slot((CHUNK, LANES)), slot((N_Q_HEADS, LANES))],
        scratch_shapes=[pltpu.VMEM((CHUNK, 2 * KV_WIDTH), F32), pltpu.VMEM((GMLP_GROUPS, CHUNK, CHUNK), F32)],
        semantics=("arbitrary",),
    )(dev_idx, proj, proj, rope_tab, rope_tab, dcat, w_spatial, w_spatial_t, bias_full, sink_rows)


def _in_proj_bwd_kernel(x, dx1, dproj, vecs, w_in_t, comm=None):
    seq = x.shape[0]
    tm = 512

    def body(x_ref, dx1_ref, dp_ref, v_ref, w_ref, gx_ref, sums_ref):
        @pl.when(pl.program_id(0) == 0)
        def _():
            sums_ref[...] = jnp.zeros_like(sums_ref)

        g_mix, scale1 = v_ref[0:1, :], v_ref[2:3, :]
        dh = _dot(dp_ref[...], w_ref[...])
        xv = x_ref[...]
        rstd = lax.rsqrt(_mean_last(xv * xv) + EPS)
        xh = xv * rstd
        dn1 = dh * (1.0 + scale1)
        dxh = dn1 * g_mix
        gx_ref[...] = dx1_ref[...] + rstd * (dxh - xh * _mean_last(dxh * xh))
        sums_ref[0:1, :] += _rowsum(dh)
        sums_ref[1:2, :] += _rowsum(dh * (xh * g_mix))
        sums_ref[2:3, :] += _rowsum(dn1 * xh)

    tok = lambda w: pl.BlockSpec((tm, w), lambda i: (i, 0))
    return _hosted_call(
        body, comm, name="in_proj_bwd", grid=(seq // tm,),
        out_shape=[jax.ShapeDtypeStruct((seq, D_MODEL), F32), jax.ShapeDtypeStruct((8, D_MODEL), F32)],
        in_specs=[tok(D_MODEL), tok(D_MODEL), tok(IN_PROJ_WIDTH), _full((8, D_MODEL)),
                  _full((IN_PROJ_WIDTH, D_MODEL))],
        out_specs=[tok(D_MODEL), _full((8, D_MODEL))],
        semantics=("arbitrary",),
    )(x, dx1, dproj, vecs, w_in_t)


class _GradTiles(NamedTuple):
    tm: int
    tn: int
    n_tiles: int
    chips_per_tile: int
    a_index: Callable
    b_index: Callable


def _weight_grad_kernel(a, b, c_idx, name, tiles, comm=None):
    seq = a.shape[0]
    tk = min(seq, 4096)
    nk = seq // tk
    tm, tn, n_tiles, per = tiles.tm, tiles.tn, tiles.n_tiles, tiles.chips_per_tile
    rows = tm // per

    def half(phase, c):
        return phase * c[0] + (1 - phase) * (1 - c[0])

    def body(c_ref, a_ref, b_ref, o_ref, acc, stage, landed, send_sems, recv_sems):
        del c_ref
        phase, t, kk = pl.program_id(0), pl.program_id(1), pl.program_id(2)
        x, y, c, _ = _mesh_place()

        def copy(tile):
            return pltpu.make_async_remote_copy(
                src_ref=stage.at[tile], dst_ref=landed.at[tile], send_sem=send_sems.at[tile],
                recv_sem=recv_sems.at[tile], device_id=(x, y, 1 - c), device_id_type=MESH)

        @pl.when(kk == 0)
        def _():
            acc[...] = jnp.zeros_like(acc)

        acc[...] += _dot_tn(a_ref[...], b_ref[...])

        @pl.when((kk == nk - 1) & (phase == 0))
        def _():
            stage[t] = acc[...].astype(stage.dtype)
            copy(t).start()

        @pl.when((kk == nk - 1) & (phase == 1))
        def _():
            copy(t).wait_recv()
            total = acc[...] + landed[t].astype(F32)
            for q in range(per):
                o_ref[q] = total[rows * q:rows * (q + 1)].astype(o_ref.dtype)

        @pl.when((kk == nk - 1) & (phase == 1) & (t == n_tiles - 1))
        def _():
            for tile in range(n_tiles):
                copy(tile).wait_send()

    out = _hosted_call(
        body, comm, name=name, grid=(2, n_tiles, nk), n_prefetch=1,
        out_shape=[jax.ShapeDtypeStruct((n_tiles * per, rows, tn), GRAD_COMM_DTYPE)],
        in_specs=[pl.BlockSpec((tk, tm), lambda p, t, k, c: (k, tiles.a_index(t, half(p, c)))),
                  pl.BlockSpec((tk, tn), lambda p, t, k, c: (k, tiles.b_index(t, half(p, c))))],
        out_specs=[pl.BlockSpec((per, rows, tn), lambda p, t, k, c: (p * t, 0, 0))],
        scratch_shapes=[pltpu.VMEM((tm, tn), F32), pltpu.VMEM((n_tiles, tm, tn), GRAD_COMM_DTYPE),
                        pltpu.VMEM((n_tiles, tm, tn), GRAD_COMM_DTYPE),
                        pltpu.SemaphoreType.DMA((n_tiles,)), pltpu.SemaphoreType.DMA((n_tiles,))],
        semantics=("arbitrary", "arbitrary", "arbitrary"),
    )(c_idx, a, b)
    return out[0] if comm is None else out


def _row_tile(rows, most=256, sublanes=16):
    return max(t for t in range(sublanes, most + 1, sublanes) if rows % t == 0)


def _adam_update(w, g, m, v):
    m_new = ADAM_B1 * m + (1.0 - ADAM_B1) * g
    v_new = ADAM_B2 * v + (1.0 - ADAM_B2) * (g * g)
    m_hat = m_new / (1.0 - ADAM_B1 ** ADAM_STEP)
    v_hat = v_new / (1.0 - ADAM_B2 ** ADAM_STEP)
    delta = -ADAM_LR * (m_hat / (jnp.sqrt(v_hat) + ADAM_EPS) + ADAM_WD * w)
    return delta, m_new, v_new


def _sum_chips_kernel(own, others, place, name, after=()):
    _, r, n = own.shape
    tr = _row_tile(r)

    def body(place_ref, own_ref, oth_ref, *rest):
        o_ref = rest[-1]
        acc = own_ref[...].astype(F32)
        for k in range(N_CHIPS - 1):
            acc = acc + oth_ref[k].astype(F32)
        o_ref[...] = acc

    return pl.pallas_call(
        body, name=name, out_shape=jax.ShapeDtypeStruct((2, r, n), F32),
        grid_spec=pltpu.PrefetchScalarGridSpec(
            num_scalar_prefetch=1, grid=(r // tr,),
            in_specs=[pl.BlockSpec((None, tr, n), lambda i, p: (p[0], i, 0)),
                      pl.BlockSpec((N_CHIPS - 1, tr, n), lambda i, p: (0, i, 0))] + [_any()] * len(after),
            out_specs=pl.BlockSpec((None, tr, n), lambda i, p: (p[1], i, 0))),
        compiler_params=_params("parallel"),
    )(place, own, others, *after)


def _adam_kernel(w, g, m, v, name):
    r, n = w.shape
    by_columns = g.shape[1] == r
    tr, tn = _row_tile(g.shape[1], most=512), g.shape[2]

    def body(w_ref, g_ref, m_ref, v_ref, g_out, d_ref, mo_ref, vo_ref):
        gv = g_ref[...]
        g_out[...] = gv
        d_ref[...], mo_ref[...], vo_ref[...] = _adam_update(w_ref[...], gv, m_ref[...], v_ref[...])

    steps = g.shape[1] // tr
    spec = pl.BlockSpec((tr, tn), (lambda h, i: (i, h)) if by_columns else (lambda h, i: (h * steps + i, 0)))
    return pl.pallas_call(
        body, name=name, grid=(2, steps), out_shape=[jax.ShapeDtypeStruct((r, n), F32)] * 4,
        in_specs=[spec, pl.BlockSpec((None, tr, tn), lambda h, i: (h, i, 0)), spec, spec], out_specs=[spec] * 4,
        compiler_params=_params("parallel", "parallel"),
    )(w, g, m, v)


SMALL_PARAMS = ("b_ada", "g_mix", "g_ffn", "g_final", "b_spatial", "sinks", "w_spatial")


def _small_update_kernel(gathered, params):
    shapes = [params[nm][0].shape for nm in SMALL_PARAMS]

    def body(*refs):
        g_refs, refs = refs[:5], refs[5:]
        p_refs, refs = refs[:3 * len(SMALL_PARAMS)], refs[3 * len(SMALL_PARAMS):]
        loss_ref, o_refs = refs[0], refs[1:]

        def total(ref):
            acc = ref[0].astype(F32)
            for k in range(1, N_DEV):
                acc = acc + ref[k].astype(F32)
            return acc

        s1, s2, db, ds, dw = (total(r) for r in g_refs)
        loss_ref[...] = jnp.broadcast_to(s2[6:7, 0:1], loss_ref.shape)
        grads = {"b_ada": [s1[0:1], s1[1:2], s2[5:6], s2[0:1], s2[1:2], s2[2:3]], "g_mix": [s1[2:3]],
                 "g_ffn": [s2[3:4]], "g_final": [s2[4:5]], "b_spatial": [db.T[0:GMLP_GROUPS]],
                 "w_spatial": [dw]}
        lane = lax.broadcasted_iota(jnp.int32, (1, LANES), 1)
        sink_row = jnp.zeros((1, LANES), F32)
        for h in range(N_Q_HEADS):
            sink_row = sink_row + jnp.where(lane == h, ds[h:h + 1, :], 0.0)
        grads["sinks"] = [sink_row[:, 0:N_Q_HEADS]]
        for i, nm in enumerate(SMALL_PARAMS):
            w_ref, m_ref, v_ref = p_refs[3 * i:3 * i + 3]
            outs = o_refs[4 * i:4 * i + 4]
            width = grads[nm][0].shape[1]
            for k, g in enumerate(grads[nm]):
                cols = slice(width * k, width * (k + 1))
                upd = _adam_update(w_ref[:, cols], g, m_ref[:, cols], v_ref[:, cols])
                for o_ref, val in zip(outs, (g,) + upd):
                    o_ref[:, cols] = val

    flat = [a for nm in SMALL_PARAMS for a in params[nm]]
    out_shape = [jax.ShapeDtypeStruct((8, LANES), F32)]
    out_shape += [jax.ShapeDtypeStruct(s, F32) for s in shapes for _ in range(4)]
    outs = pl.pallas_call(
        body, name="small_update", grid=(1,), out_shape=out_shape,
        in_specs=[_full(g.shape) for g in gathered] + [_full(a.shape) for a in flat],
        out_specs=[_full(s.shape) for s in out_shape],
        compiler_params=_params("arbitrary"),
    )(*gathered, *flat)
    return {nm: outs[1 + 4 * i:5 + 4 * i] for i, nm in enumerate(SMALL_PARAMS)}, outs[0]


def _ada_update_kernel(act_t, dmod, w, m, v):
    r, n = w.shape
    tr = 256

    def body(a_ref, d_ref, w_ref, m_ref, v_ref, g_ref, dl_ref, mo_ref, vo_ref):
        g = _dot(a_ref[...], d_ref[...])
        g_ref[...] = g
        dl_ref[...], mo_ref[...], vo_ref[...] = _adam_update(w_ref[...], g, m_ref[...], v_ref[...])

    spec = pl.BlockSpec((tr, n), lambda i: (i, 0))
    return pl.pallas_call(
        body, name="ada_update", grid=(r // tr,), out_shape=[jax.ShapeDtypeStruct((r, n), F32)] * 4,
        in_specs=[pl.BlockSpec((tr, N_DEV), lambda i: (i, 0)), _full((N_DEV, n)), spec, spec, spec],
        out_specs=[spec] * 4, compiler_params=_params("parallel"),
    )(act_t, dmod, w, m, v)


def kernel(x, c, positions, w_ada, b_ada, g_mix, w_in, w_spatial, b_spatial, sinks, w_out, g_ffn, w_ff1, w_ff2, g_final, loss_target, m_w_ada, m_b_ada, m_g_mix, m_w_in, m_w_spatial, m_b_spatial, m_sinks, m_w_out, m_g_ffn, m_w_ff1, m_w_ff2, m_g_final, v_w_ada, v_b_ada, v_g_mix, v_w_in, v_w_spatial, v_b_spatial, v_sinks, v_w_out, v_g_ffn, v_w_ff1, v_w_ff2, v_g_final):
    xi, yi, ci = lax.axis_index("x"), lax.axis_index("y"), lax.axis_index("c")
    chip = 2 * xi + yi
    dev = 2 * chip + ci
    seq = x.shape[1]
    x2, tgt = x[0], loss_target[0]
    ada_cols = w_ada.shape[2]

    big = {"w_in": tuple(a[0].T for a in (w_in, m_w_in, v_w_in)),
           "w_out": (w_out[0], m_w_out[0], v_w_out[0]), "w_ff1": (w_ff1[0], m_w_ff1[0], v_w_ff1[0]),
           "w_ff2": (w_ff2[0], m_w_ff2[0], v_w_ff2[0])}

    def halves(nm):
        r, n = big[nm][0].shape
        return big[nm][0].astype(WEIGHT_COMM_DTYPE).reshape(2, r // 2, n)

    chip_idx = chip.reshape(1).astype(jnp.int32)
    c_all, w_in_t, g_out = _all_gather8([c, halves("w_in"), halves("w_out")], "gather_first",
                                        split=[False, True, True], skip_own=(2,))
    c_all, w_in_t = c_all.reshape(N_DEV, D_MODEL), w_in_t.reshape(IN_PROJ_WIDTH, D_MODEL)
    b_shard = lax.dynamic_slice(b_ada, (0, chip * ada_cols), (1, ada_cols))
    mod_part, act = _mod_kernel(c_all, w_ada[0], b_shard)
    mod_all, = _all_gather8([mod_part], "gather_mod")
    mod_me = lax.dynamic_index_in_dim(mod_all[0::2], dev, axis=1, keepdims=False)
    mod_me = mod_me.reshape(N_MOD, D_MODEL)
    shift1, scale1, gate1, shift2, scale2, gate2 = (mod_me[k:k + 1] for k in range(N_MOD))

    zeros_row = jnp.zeros((1, D_MODEL), F32)
    vecs1 = jnp.concatenate([g_mix, shift1, scale1] + [zeros_row] * 5, axis=0)
    vecs2 = jnp.concatenate([gate1, shift2, scale2, gate2, g_ffn, g_final.reshape(1, D_MODEL)]
                            + [zeros_row] * 2, axis=0)
    bias_full = jnp.repeat(b_spatial[0].T, HEAD_DIM, axis=1)
    sink_rows = jnp.broadcast_to(sinks[0][:, None], (N_Q_HEADS, LANES))
    inv_freq = ROPE_THETA ** (-jnp.arange(0, ROT_DIM, 2, dtype=F32) / ROT_DIM)
    rope_tab = _rope_lane_tables(*_rope_angle_kernel(positions, inv_freq.reshape(ROT_DIM // 2, 1)))

    trunk_weights = ["w_out", "w_ff1", "w_ff2"]
    shards = [halves(nm) for nm in trunk_weights]
    proj, hb, *staged = _in_proj_kernel(x2, vecs1, w_in_t, comm=_gather2d_first(shards[1:]))
    cat, *staged = _mixer_fwd_kernel(proj, rope_tab, w_spatial[0], bias_full, sink_rows,
                                     comm=_gather2d_second(staged, shards[1:]))
    staged = [g_out] + list(_gather_forward(staged, "gather_forward"))
    dx1, dcat, dmix, h2b, rb, dab, dffb, sums2 = _trunk_kernel(
        x2, tgt, cat, vecs2, chip_idx,
        [g.reshape((N_CHIPS,) + big[nm][0].shape) for nm, g in zip(trunk_weights, staged)],
        [s.reshape(big[nm][0].shape) for nm, s in zip(trunk_weights, shards)])

    c_idx = ci.reshape(1).astype(jnp.int32)
    place = jnp.stack([chip, ci]).astype(jnp.int32)
    half_d = D_MODEL // 2
    cs_ff2 = _weight_grad_kernel(rb, dffb, c_idx, "dw_ff2",
                                 _GradTiles(D_MODEL, half_d, N_CHIPS, 1, lambda t, h: t, lambda t, h: h))
    eighth = D_MODEL // 8
    cs_ff1, sc_ff2 = _weight_grad_kernel(
        h2b, dab, c_idx, "dw_ff1",
        _GradTiles(D_MODEL, half_d, N_CHIPS, 1, lambda t, h: 0, lambda t, h: 2 * t + h),
        comm=_scatter_job([cs_ff2], rows=(0, 6 * eighth)))
    cs_out, sc_ff2 = _weight_grad_kernel(
        cat, dmix, c_idx, "dw_out", _GradTiles(D_MODEL, half_d, 1, N_CHIPS, lambda t, h: 0, lambda t, h: h),
        comm=_scatter_job([cs_ff2], rows=(6 * eighth, eighth), into=[sc_ff2]))
    dproj, dw_spatial, db_lanes, dsink_rows, sc_ff2, sc_ff1, sc_out = _mixer_bwd_kernel(
        proj, rope_tab, dcat, w_spatial[0], w_spatial[0].transpose(0, 2, 1), bias_full, sink_rows,
        dev.reshape(1).astype(jnp.int32),
        comm=_merge_jobs(_scatter_job([cs_ff1, cs_out]),
                         _scatter_job([cs_ff2], rows=(7 * eighth, eighth), into=[sc_ff2])))
    totals = [_sum_chips_kernel(own, oth, place, "grad_sum_" + nm)
              for nm, own, oth in (("w_out", cs_out, sc_out), ("w_ff1", cs_ff1, sc_ff1), ("w_ff2", cs_ff2, sc_ff2))]
    small_slots = [db_lanes, dsink_rows, dw_spatial.reshape(N_DEV, GMLP_GROUPS * CHUNK, CHUNK)]
    cs_in, *rode = _weight_grad_kernel(
        dproj, hb, c_idx, "dw_in",
        _GradTiles(2 * W_IN_BLOCK, half_d, N_CHIPS // 2, 2, lambda t, h: t, lambda t, h: h),
        comm=_merge_jobs(_gather_job(small_slots), _share_job(totals)))
    small_stage1, shared = rode[:len(small_slots)], rode[len(small_slots):]
    scatter_in = _scatter_start(cs_in, "grad_to_chips_w_in_start")
    grad_x, sums1 = _in_proj_bwd_kernel(x2, dx1, dproj, vecs1 + scatter_in[-1][0:1, 0:1], w_in_t)
    cs_in, sc_in = _scatter_wait(scatter_in, sums1, "grad_to_chips_w_in_wait")
    gather_small = _gather_start([sums1, sums2], small_stage1, cs_in, "gather_small_start")
    total_in = _sum_chips_kernel(cs_in, sc_in, place, "grad_sum_w_in", after=gather_small[-1:])
    shared = list(_sibling_share([total_in], "grad_share_w_in")) + list(shared)
    names = ["w_in", "w_out", "w_ff1", "w_ff2"]
    big_out = {}
    for nm, g in zip(names, shared):
        w, m, v = big[nm]
        outs = _adam_kernel(w, g, m, v, "adam_" + nm)
        big_out[nm] = tuple((t.T if nm == "w_in" else t)[None] for t in outs)
        if nm == "w_in":
            gathered = _gather_wait(gather_small, 2, outs[0], "gather_small_wait")

    small = {"b_ada": (b_ada, m_b_ada, v_b_ada), "g_mix": (g_mix, m_g_mix, v_g_mix),
             "g_ffn": (g_ffn, m_g_ffn, v_g_ffn), "g_final": (g_final, m_g_final, v_g_final),
             "b_spatial": (b_spatial, m_b_spatial, v_b_spatial), "sinks": (sinks, m_sinks, v_sinks),
             "w_spatial": (w_spatial, m_w_spatial, v_w_spatial)}
    flat_shape = {"g_final": (1, D_MODEL), "b_spatial": (GMLP_GROUPS, CHUNK), "w_spatial": (GMLP_GROUPS * CHUNK, CHUNK)}
    small_out, loss_tile = _small_update_kernel(
        gathered, {nm: tuple(a.reshape(flat_shape.get(nm, a.shape)) for a in small[nm]) for nm in small})
    small_out = {nm: [o.reshape(small[nm][0].shape) for o in small_out[nm]] for nm in small}
    loss = loss_tile[0, 0]

    g1, g2 = gathered[0], gathered[1]
    dmod_all = jnp.concatenate([g1[:, 0], g1[:, 1], g2[:, 5], g2[:, 0], g2[:, 1], g2[:, 2]], axis=1)
    dmod_cols = lax.dynamic_slice(dmod_all, (0, chip * ada_cols), (N_DEV, ada_cols))
    ada = _ada_update_kernel(act.T, dmod_cols, w_ada[0], m_w_ada[0], v_w_ada[0])
    big_out["w_ada"] = tuple(t[None] for t in ada)

    order = ["w_ada", "b_ada", "g_mix", "w_in", "w_spatial", "b_spatial", "sinks", "w_out", "g_ffn",
             "w_ff1", "w_ff2", "g_final"]

    def leaf(nm, k):
        return big_out[nm][k] if nm in big_out else small_out[nm][k]

    outs = [loss, grad_x[None]]
    for k in range(4):
        outs += [leaf(nm, k) for nm in order]
    return tuple(outs)
```

```python
import math
from typing import Callable, NamedTuple

import jax
import jax.numpy as jnp
from jax import lax
from jax.experimental import pallas as pl
from jax.experimental.pallas import tpu as pltpu

F32 = jnp.float32
MXU_DTYPE = jnp.bfloat16
WEIGHT_COMM_DTYPE = jnp.bfloat16
GRAD_COMM_DTYPE = jnp.bfloat16

D_MODEL = 1024
D_FF = 4096
HEAD_DIM = 64
GMLP_GROUPS = 8
GMLP_WIDTH = 512
CHUNK = 128
N_Q_HEADS = 8
N_KV_HEADS = 2
ATTN_WIDTH = 512
KV_WIDTH = 128
ROT_DIM = 16
ROPE_THETA = 500000.0
IN_PROJ_WIDTH = 1792
N_MOD = 6
EPS = 1e-5
N_CHIPS = 4
N_DEV = 8
LANES = 128
W_IN_BLOCK = IN_PROJ_WIDTH // N_CHIPS

ADAM_LR = 0.001
ADAM_B1 = 0.9
ADAM_B2 = 0.999
ADAM_EPS = 1e-08
ADAM_WD = 0.01
ADAM_STEP = 10

VMEM_LIMIT_BYTES = 58 * 1024 * 1024
MESH = pl.DeviceIdType.MESH


def _params(*semantics):
    return pltpu.CompilerParams(dimension_semantics=semantics, vmem_limit_bytes=VMEM_LIMIT_BYTES)


def _dot(a, b):
    return jnp.dot(a.astype(MXU_DTYPE), b.astype(MXU_DTYPE), preferred_element_type=F32)


def _dot_nt(a, b):
    return lax.dot_general(a.astype(MXU_DTYPE), b.astype(MXU_DTYPE), (((1,), (1,)), ((), ())),
                           preferred_element_type=F32)


def _dot_tn(a, b):
    return lax.dot_general(a.astype(MXU_DTYPE), b.astype(MXU_DTYPE), (((0,), (0,)), ((), ())),
                           preferred_element_type=F32)


def _full(shape):
    return pl.BlockSpec(shape, lambda *_: (0,) * len(shape))


def _any():
    return pl.BlockSpec(memory_space=pl.ANY)


def _rowsum(v):
    return jnp.sum(v, axis=0, keepdims=True)


def _mean_last(v):
    return jnp.mean(v, axis=-1, keepdims=True)


class _Comm(NamedTuple):
    operands: tuple
    out_shapes: tuple
    n_sems: int
    make: Callable
    in_place: int = 0


def _hosted_call(body, comm, *, name, grid, in_specs, out_shape, out_specs, scratch_shapes=(), semantics,
                 n_prefetch=0):
    if comm is None:
        return pl.pallas_call(
            body, name=name, out_shape=out_shape, compiler_params=_params(*semantics),
            grid_spec=pltpu.PrefetchScalarGridSpec(
                num_scalar_prefetch=n_prefetch, grid=grid, in_specs=in_specs, out_specs=out_specs,
                scratch_shapes=list(scratch_shapes)))
    n_in, n_out, n_scr = len(in_specs), len(out_shape), len(scratch_shapes)
    k_in, k_out = len(comm.operands), len(comm.out_shapes)

    def hosted(*refs):
        prefetched, refs = refs[:n_prefetch], refs[n_prefetch:]
        ins, refs = refs[:n_in], refs[n_in:]
        c_ins, refs = refs[:k_in], refs[k_in:]
        outs, refs = refs[:n_out], refs[n_out:]
        c_outs, refs = refs[:k_out], refs[k_out:]
        scratch, (send_sems, recv_sems) = refs[:n_scr], refs[n_scr:]
        first, last = None, None
        for d, size in enumerate(grid):
            at_start, at_end = pl.program_id(d) == 0, pl.program_id(d) == size - 1
            first = at_start if first is None else first & at_start
            last = at_end if last is None else last & at_end

        @pl.when(first)
        def _():
            for cp in comm.make(c_ins, c_outs, send_sems, recv_sems)[0]:
                cp.start()

        body(*prefetched, *ins, *outs, *scratch)

        @pl.when(last)
        def _():
            for wait in comm.make(c_ins, c_outs, send_sems, recv_sems)[1]:
                wait()

    aliases = {n_prefetch + n_in + i: n_out + i for i in range(comm.in_place)}
    call = pl.pallas_call(
        hosted, name=name, out_shape=list(out_shape) + list(comm.out_shapes),
        compiler_params=_params(*semantics), input_output_aliases=aliases,
        grid_spec=pltpu.PrefetchScalarGridSpec(
            num_scalar_prefetch=n_prefetch, grid=grid, in_specs=list(in_specs) + [_any()] * k_in,
            out_specs=list(out_specs) + [_any()] * k_out,
            scratch_shapes=list(scratch_shapes) + [pltpu.SemaphoreType.DMA((comm.n_sems,)),
                                                    pltpu.SemaphoreType.DMA((comm.n_sems,))]))
    return lambda *args: call(*args, *comm.operands)


class _Shifted:
    def __init__(self, base, offset):
        self.base, self.offset = base, offset

    @property
    def at(self):
        return self

    def __getitem__(self, k):
        return self.base.at[self.offset + k]


def _merge_jobs(*jobs):
    def order(count):
        first = [(j, i) for j, job in enumerate(jobs) for i in range(job.in_place)]
        return first + [(j, i) for j, job in enumerate(jobs) for i in range(job.in_place, count(job))]

    op_order, out_order = order(lambda job: len(job.operands)), order(lambda job: len(job.out_shapes))

    def make(ins, outs, send_sems, recv_sems):
        starts, waits, sem = [], [], 0
        for j, job in enumerate(jobs):
            mine_in = [ins[k] for k, (jj, _) in enumerate(op_order) if jj == j]
            mine_out = [outs[k] for k, (jj, _) in enumerate(out_order) if jj == j]
            s, w = job.make(mine_in, mine_out, _Shifted(send_sems, sem), _Shifted(recv_sems, sem))
            starts, waits, sem = starts + s, waits + w, sem + job.n_sems
        return starts, waits

    return _Comm(tuple(jobs[j].operands[i] for j, i in op_order), tuple(jobs[j].out_shapes[i] for j, i in out_order),
                 sum(job.n_sems for job in jobs), make, in_place=sum(job.in_place for job in jobs))


def _mesh_place():
    x, y, c = lax.axis_index("x"), lax.axis_index("y"), lax.axis_index("c")
    return x, y, c, [(1 - x, y), (x, 1 - y), (1 - x, 1 - y)]


def _gather_job(bufs):
    per = 4

    def make(ins, outs, send_sems, recv_sems):
        del ins
        x, y, c, chips = _mesh_place()
        starts, waits = [], []
        for a, out in enumerate(outs):
            mine = src = out.at[4 * x + 2 * y + c]
            to = [(x, y, 1 - c)] + [(px, py, c) for px, py in chips]
            sends = [pltpu.make_async_remote_copy(
                src_ref=src, dst_ref=mine, send_sem=send_sems.at[per * a + k],
                recv_sem=recv_sems.at[per * a + k], device_id=dev, device_id_type=MESH)
                for k, dev in enumerate(to)]
            recvs = [pltpu.make_async_remote_copy(
                src_ref=src, dst_ref=out.at[4 * px + 2 * py + pc], send_sem=send_sems.at[per * a + k],
                recv_sem=recv_sems.at[per * a + k], device_id=(px, py, pc), device_id_type=MESH)
                for k, (px, py, pc) in enumerate(to)]
            starts += sends
            waits += [s.wait_send for s in sends] + [r.wait_recv for r in recvs]
        return starts, waits

    shapes = tuple(jax.ShapeDtypeStruct(b.shape, b.dtype) for b in bufs)
    return _Comm(tuple(bufs), shapes, per * len(bufs), make, in_place=len(bufs))


def _slots(x, y, c):
    return 4 * x + 2 * y + c, 4 * (1 - x) + 2 * y + c, 4 * x + 2 * (1 - y) + c, 4 * (1 - x) + 2 * (1 - y) + c


def _gather2d_first(halves):
    per = 2

    def make(ins, outs, send_sems, recv_sems):
        x, y, c, _ = _mesh_place()
        me, xn, yn, _ = _slots(x, y, c)
        starts, waits = [], []
        for a, (src, out) in enumerate(zip(ins, outs)):
            blk = src.at[c]
            rows = blk.shape[0] // 2
            upper, lower = pl.ds(0, rows), pl.ds(rows, rows)

            def copy(k, src_ref, dst_ref, dev, a=a):
                return pltpu.make_async_remote_copy(
                    src_ref=src_ref, dst_ref=dst_ref, send_sem=send_sems.at[per * a + k],
                    recv_sem=recv_sems.at[per * a + k], device_id=dev, device_id_type=MESH)

            sends = [copy(0, blk.at[upper], out.at[me, upper], (1 - x, y, c)),
                     copy(1, blk.at[lower], out.at[me, lower], (x, 1 - y, c))]
            recvs = [copy(0, blk.at[upper], out.at[xn, upper], (1 - x, y, c)),
                     copy(1, blk.at[lower], out.at[yn, lower], (x, 1 - y, c))]
            starts += sends
            waits += [s.wait_send for s in sends] + [r.wait_recv for r in recvs]
        return starts, waits

    shapes = tuple(jax.ShapeDtypeStruct((N_DEV,) + h.shape[1:], h.dtype) for h in halves)
    return _Comm(tuple(halves), shapes, per * len(halves), make)


def _gather2d_second(bufs, halves):
    per = 4
    n_arr = len(bufs)

    def make(ins, outs, send_sems, recv_sems):
        x, y, c, _ = _mesh_place()
        me, xn, yn, dg = _slots(x, y, c)
        starts, waits = [], []
        for a, buf in enumerate(outs):
            own = ins[n_arr + a].at[c]
            rows = buf.shape[1] // 2
            upper, lower = pl.ds(0, rows), pl.ds(rows, rows)
            plan = [(own.at[upper], me, upper, (x, 1 - y, c), yn), (buf.at[xn, upper], xn, upper, (x, 1 - y, c), dg),
                    (own.at[lower], me, lower, (1 - x, y, c), xn), (buf.at[yn, lower], yn, lower, (1 - x, y, c), dg)]
            for k, (src, slot, part, dev, landing) in enumerate(plan):
                sems = dict(send_sem=send_sems.at[per * a + k], recv_sem=recv_sems.at[per * a + k],
                            device_id=dev, device_id_type=MESH)
                send = pltpu.make_async_remote_copy(src_ref=src, dst_ref=buf.at[slot, part], **sems)
                arrival = pltpu.make_async_remote_copy(src_ref=src, dst_ref=buf.at[landing, part], **sems)
                starts.append(send)
                waits += [send.wait_send, arrival.wait_recv]
        return starts, waits

    shapes = tuple(jax.ShapeDtypeStruct(b.shape, b.dtype) for b in bufs)
    return _Comm(tuple(bufs) + tuple(halves), shapes, per * n_arr, make, in_place=n_arr)


def _gather_forward(bufs, name):
    n_arr = len(bufs)

    def body(*refs):
        outs = refs[n_arr:2 * n_arr]
        send_sems, recv_sems = refs[2 * n_arr:]
        x, y, c, chips = _mesh_place()
        sends, recvs = [], []
        for a, buf in enumerate(outs):
            for j, (px, py) in enumerate(chips):
                mine, theirs = buf.at[4 * px + 2 * py + c], buf.at[4 * px + 2 * py + 1 - c]
                sems = dict(send_sem=send_sems.at[3 * a + j], recv_sem=recv_sems.at[3 * a + j],
                            device_id=(x, y, 1 - c), device_id_type=MESH)
                sends.append(pltpu.make_async_remote_copy(src_ref=mine, dst_ref=mine, **sems))
                recvs.append(pltpu.make_async_remote_copy(src_ref=mine, dst_ref=theirs, **sems))
        for cp in sends:
            cp.start()
        for s, r in zip(sends, recvs):
            s.wait_send()
            r.wait_recv()

    return pl.pallas_call(
        body, name=name, out_shape=[jax.ShapeDtypeStruct(b.shape, b.dtype) for b in bufs],
        in_specs=[_any()] * n_arr, out_specs=[_any()] * n_arr,
        input_output_aliases={a: a for a in range(n_arr)},
        scratch_shapes=[pltpu.SemaphoreType.DMA((3 * n_arr,)), pltpu.SemaphoreType.DMA((3 * n_arr,))],
    )(*bufs)


def _scatter_job(chip_sums, rows=None, into=()):
    n_into = len(into)

    def part(ref):
        return ref if rows is None else ref.at[pl.ds(rows[0], rows[1])]

    def make(ins, outs, send_sems, recv_sems):
        x, y, c, chips = _mesh_place()
        copies = [pltpu.make_async_remote_copy(
            src_ref=part(src.at[2 * px + py]), dst_ref=part(out.at[j]), send_sem=send_sems.at[3 * a + j],
            recv_sem=recv_sems.at[3 * a + j], device_id=(px, py, c), device_id_type=MESH)
            for a, (src, out) in enumerate(zip(ins[n_into:], outs)) for j, (px, py) in enumerate(chips)]
        return copies, [cp.wait for cp in copies]

    shapes = tuple(jax.ShapeDtypeStruct((3,) + s.shape[1:], s.dtype) for s in chip_sums)
    return _Comm(tuple(into) + tuple(chip_sums), shapes, 3 * len(chip_sums), make, in_place=n_into)


def _all_gather8(blocks, name, split=False, forward=(), riders=(), skip_own=()):
    n_arr, n_fwd = len(blocks), len(forward)
    splits = list(split) if isinstance(split, (list, tuple)) else [split] * n_arr
    own_slots = [a not in skip_own for a in range(n_arr)]
    rider_in = sum(len(r.operands) for r in riders)
    rider_out = sum(len(r.out_shapes) for r in riders)

    def body(*refs):
        x_refs, refs = refs[:n_arr], refs[n_arr + n_fwd:]
        r_ins, refs = refs[:rider_in], refs[rider_in:]
        out_refs, refs = refs[:n_arr], refs[n_arr:]
        fwd_refs, refs = refs[:n_fwd], refs[n_fwd:]
        r_outs, refs = refs[:rider_out], refs[rider_out:]
        (send_sems, recv_sems, local_sems), rider_sems = refs[:3], refs[3:]
        x, y, c, chips = _mesh_place()
        me, sibling = (x, y, c), (x, y, 1 - c)
        passing = []
        for f, buf in enumerate(fwd_refs):
            for j, (px, py) in enumerate(chips):
                mine, theirs = buf.at[4 * px + 2 * py + c], buf.at[4 * px + 2 * py + 1 - c]
                sems = dict(send_sem=send_sems.at[7 * n_arr + 3 * f + j], recv_sem=recv_sems.at[7 * n_arr + 3 * f + j],
                            device_id=sibling, device_id_type=MESH)
                passing.append((pltpu.make_async_remote_copy(src_ref=mine, dst_ref=mine, **sems),
                                pltpu.make_async_remote_copy(src_ref=mine, dst_ref=theirs, **sems)))
        for send, _ in passing:
            send.start()
        arrays = []
        for a, (x_ref, out_ref) in enumerate(zip(x_refs, out_refs)):
            src_mine = x_ref.at[c] if splits[a] else x_ref

            def copy(k, blk, to, src=None, a=a, out_ref=out_ref):
                dst = out_ref.at[4 * blk[0] + 2 * blk[1] + blk[2]]
                return pltpu.make_async_remote_copy(
                    src_ref=dst if src is None else src, dst_ref=dst,
                    send_sem=send_sems.at[7 * a + k], recv_sem=recv_sems.at[7 * a + k],
                    device_id=to, device_id_type=MESH)

            mine = pltpu.make_async_copy(src_mine, out_ref.at[4 * x + 2 * y + c], local_sems.at[a])
            first = [copy(0, me, sibling, src=src_mine)] if own_slots[a] else []
            first += [copy(1 + j, me, (*chip, c), src=src_mine) for j, chip in enumerate(chips)]
            for cp in first + ([mine] if own_slots[a] else []):
                cp.start()
            arrays.append((copy, mine, first, own_slots[a]))
        rider_waits, i0, o0 = [], 0, 0
        for n, job in enumerate(riders):
            k_in, k_out = len(job.operands), len(job.out_shapes)
            starts, waits = job.make(r_ins[i0:i0 + k_in], r_outs[o0:o0 + k_out],
                                     rider_sems[2 * n], rider_sems[2 * n + 1])
            for cp in starts:
                cp.start()
            rider_waits += waits
            i0, o0 = i0 + k_in, o0 + k_out
        sent = []
        for copy, mine, first, own in arrays:
            passed = [copy(4 + j, (*chip, c), sibling) for j, chip in enumerate(chips)]
            for j, chip in enumerate(chips):
                copy(1 + j, (*chip, c), me).wait_recv()
                passed[j].start()
            sent += first + passed
        for copy, mine, first, own in arrays:
            if own:
                copy(0, sibling, me).wait_recv()
                mine.wait()
            for j, chip in enumerate(chips):
                copy(4 + j, (*chip, 1 - c), me).wait_recv()
        for cp in sent:
            cp.wait_send()
        for send, arrival in passing:
            send.wait_send()
            arrival.wait_recv()
        for wait in rider_waits:
            wait()

    n_sems = 7 * n_arr + 3 * n_fwd
    rider_operands = [a for r in riders for a in r.operands]
    rider_shapes = [s for r in riders for s in r.out_shapes]
    return pl.pallas_call(
        body, name=name,
        out_shape=[jax.ShapeDtypeStruct((N_DEV,) + tuple(b.shape[1:] if s else b.shape), b.dtype)
                   for b, s in zip(blocks, splits)]
        + [jax.ShapeDtypeStruct(f.shape, f.dtype) for f in forward] + rider_shapes,
        in_specs=[_any()] * (n_arr + n_fwd + rider_in), out_specs=[_any()] * (n_arr + n_fwd + rider_out),
        input_output_aliases={n_arr + f: n_arr + f for f in range(n_fwd)},
        scratch_shapes=[pltpu.SemaphoreType.DMA((n_sems,)), pltpu.SemaphoreType.DMA((n_sems,)),
                        pltpu.SemaphoreType.DMA((n_arr,))]
        + [pltpu.SemaphoreType.DMA((r.n_sems,)) for r in riders for _ in range(2)],
    )(*blocks, *forward, *rider_operands)


def _split_scatter_copies(src, land, send_sems, recv_sems):
    x, y, c, chips = _mesh_place()
    return [pltpu.make_async_remote_copy(
        src_ref=src.at[2 * px + py], dst_ref=land.at[j], send_sem=send_sems.at[j], recv_sem=recv_sems.at[j],
        device_id=(px, py, c), device_id_type=MESH) for j, (px, py) in enumerate(chips)]


def _scatter_start(chip_sums, name):
    hbm, sem = pl.BlockSpec(memory_space=pltpu.HBM), pl.BlockSpec(memory_space=pltpu.SEMAPHORE)

    def body(src, land, send_sems, recv_sems, src_thru, land_thru, token):
        del src_thru, land_thru
        for cp in _split_scatter_copies(src, land, send_sems, recv_sems):
            cp.start()
        token[...] = jnp.zeros_like(token)

    land = lax.empty((N_CHIPS - 1,) + chip_sums.shape[1:], chip_sums.dtype)
    operands = [pltpu.with_memory_space_constraint(a, pltpu.HBM) for a in (chip_sums, land)]
    return pl.pallas_call(
        body, name=name,
        out_shape=[pltpu.SemaphoreType.DMA((N_CHIPS - 1,)), pltpu.SemaphoreType.DMA((N_CHIPS - 1,))]
        + [pltpu.HBM(a.shape, a.dtype) for a in operands] + [jax.ShapeDtypeStruct((8, LANES), F32)],
        in_specs=[hbm, hbm], out_specs=[sem, sem, hbm, hbm, pl.BlockSpec(memory_space=pltpu.VMEM)],
        input_output_aliases={0: 2, 1: 3},
        compiler_params=pltpu.CompilerParams(has_side_effects=pltpu.SideEffectType.DATAFLOW_SIDE_EFFECTING),
    )(*operands)


def _scatter_wait(started, after, name):
    send_sems, recv_sems, src, land, _ = started
    hbm, sem = pl.BlockSpec(memory_space=pltpu.HBM), pl.BlockSpec(memory_space=pltpu.SEMAPHORE)

    def body(src, land, send_sems, recv_sems, after_ref, src_thru, land_thru):
        del after_ref, src_thru, land_thru
        for cp in _split_scatter_copies(src, land, send_sems, recv_sems):
            cp.wait()

    return pl.pallas_call(
        body, name=name, out_shape=[pltpu.HBM(src.shape, src.dtype), pltpu.HBM(land.shape, land.dtype)],
        in_specs=[hbm, hbm, sem, sem, _any()], out_specs=[hbm, hbm], input_output_aliases={0: 0, 1: 1},
        compiler_params=pltpu.CompilerParams(has_side_effects=pltpu.SideEffectType.DATAFLOW_SIDE_EFFECTING),
    )(src, land, send_sems, recv_sems, after)


def _split_gather_copies(srcs, lands, fwds, shares, send_sems, recv_sems):
    x, y, c, chips = _mesh_place()
    peers = [(x, y, 1 - c)] + [(px, py, pc) for px, py in chips for pc in (c, 1 - c)]
    pairs = []
    for a, (src, land) in enumerate(zip(srcs, lands)):
        for k, (px, py, pc) in enumerate(peers):
            sems = dict(send_sem=send_sems.at[7 * a + k], recv_sem=recv_sems.at[7 * a + k],
                        device_id=(px, py, pc), device_id_type=MESH)
            pairs.append((pltpu.make_async_remote_copy(src_ref=src, dst_ref=land.at[4 * x + 2 * y + c], **sems),
                          pltpu.make_async_remote_copy(src_ref=src, dst_ref=land.at[4 * px + 2 * py + pc], **sems)))
    for f, buf in enumerate(fwds):
        for j, (px, py) in enumerate(chips):
            mine, theirs = buf.at[4 * px + 2 * py + c], buf.at[4 * px + 2 * py + 1 - c]
            k = 7 * len(srcs) + 3 * f + j
            sems = dict(send_sem=send_sems.at[k], recv_sem=recv_sems.at[k],
                        device_id=(x, y, 1 - c), device_id_type=MESH)
            pairs.append((pltpu.make_async_remote_copy(src_ref=mine, dst_ref=mine, **sems),
                          pltpu.make_async_remote_copy(src_ref=mine, dst_ref=theirs, **sems)))
    for s, buf in enumerate(shares):
        k = 7 * len(srcs) + 3 * len(fwds) + s
        sems = dict(send_sem=send_sems.at[k], recv_sem=recv_sems.at[k], device_id=(x, y, 1 - c), device_id_type=MESH)
        pairs.append((pltpu.make_async_remote_copy(src_ref=buf.at[c], dst_ref=buf.at[c], **sems),
                      pltpu.make_async_remote_copy(src_ref=buf.at[c], dst_ref=buf.at[1 - c], **sems)))
    return pairs


def _gather_start(blocks, forward, shares, after, name):
    n, n_fwd = len(blocks), len(forward)
    n_sems = 7 * n + 3 * n_fwd + len(shares)
    n_bufs = 2 * n + n_fwd + len(shares)
    hbm, sem = pl.BlockSpec(memory_space=pltpu.HBM), pl.BlockSpec(memory_space=pltpu.SEMAPHORE)

    def body(*refs):
        srcs, lands, fwds, swaps = refs[:n], refs[n:2 * n], refs[2 * n:2 * n + n_fwd], refs[2 * n + n_fwd:n_bufs]
        send_sems, recv_sems = refs[n_bufs + 1:n_bufs + 3]
        token, local_sems = refs[-2:]
        x, y, c, _ = _mesh_place()
        own = [pltpu.make_async_copy(src, land.at[4 * x + 2 * y + c], local_sems.at[a])
               for a, (src, land) in enumerate(zip(srcs, lands))]
        for cp in own:
            cp.start()
        for send, _ in _split_gather_copies(srcs, lands, fwds, swaps, send_sems, recv_sems):
            send.start()
        token[...] = jnp.zeros_like(token)
        for cp in own:
            cp.wait()

    lands = [lax.empty((N_DEV,) + b.shape, b.dtype) for b in blocks]
    operands = [pltpu.with_memory_space_constraint(a, pltpu.HBM)
                for a in list(blocks) + lands + list(forward) + list(shares)]
    return pl.pallas_call(
        body, name=name,
        out_shape=[pltpu.SemaphoreType.DMA((n_sems,)), pltpu.SemaphoreType.DMA((n_sems,))]
        + [pltpu.HBM(a.shape, a.dtype) for a in operands] + [jax.ShapeDtypeStruct((8, LANES), F32)],
        in_specs=[hbm] * len(operands) + [_any()],
        out_specs=[sem, sem] + [hbm] * len(operands) + [pl.BlockSpec(memory_space=pltpu.VMEM)],
        input_output_aliases={i: 2 + i for i in range(len(operands))},
        scratch_shapes=[pltpu.SemaphoreType.DMA((n,))],
        compiler_params=pltpu.CompilerParams(has_side_effects=pltpu.SideEffectType.DATAFLOW_SIDE_EFFECTING),
    )(*operands, after)


def _gather_wait(started, n, n_shares, after, name):
    send_sems, recv_sems, *bufs, _ = started
    n_bufs = len(bufs)
    n_fwd = n_bufs - 2 * n - n_shares
    hbm, sem = pl.BlockSpec(memory_space=pltpu.HBM), pl.BlockSpec(memory_space=pltpu.SEMAPHORE)

    def body(*refs):
        srcs, lands, fwds, swaps = refs[:n], refs[n:2 * n], refs[2 * n:2 * n + n_fwd], refs[2 * n + n_fwd:n_bufs]
        send_sems, recv_sems = refs[n_bufs:n_bufs + 2]
        for send, arrival in _split_gather_copies(srcs, lands, fwds, swaps, send_sems, recv_sems):
            send.wait_send()
            arrival.wait_recv()

    out = pl.pallas_call(
        body, name=name, out_shape=[pltpu.HBM(b.shape, b.dtype) for b in bufs],
        in_specs=[hbm] * len(bufs) + [sem, sem, _any()], out_specs=[hbm] * len(bufs),
        input_output_aliases={i: i for i in range(len(bufs))},
        compiler_params=pltpu.CompilerParams(has_side_effects=pltpu.SideEffectType.DATAFLOW_SIDE_EFFECTING),
    )(*bufs, send_sems, recv_sems, after)
    return out[n:]


def _share_job(bufs):
    def make(ins, outs, send_sems, recv_sems):
        del ins
        x, y, c, _ = _mesh_place()
        sems = lambda a: dict(send_sem=send_sems.at[a], recv_sem=recv_sems.at[a],
                              device_id=(x, y, 1 - c), device_id_type=MESH)
        sends = [pltpu.make_async_remote_copy(src_ref=o.at[c], dst_ref=o.at[c], **sems(a)) for a, o in enumerate(outs)]
        arrivals = [pltpu.make_async_remote_copy(src_ref=o.at[c], dst_ref=o.at[1 - c], **sems(a))
                    for a, o in enumerate(outs)]
        return sends, [s.wait_send for s in sends] + [r.wait_recv for r in arrivals]

    shapes = tuple(jax.ShapeDtypeStruct(b.shape, b.dtype) for b in bufs)
    return _Comm(tuple(bufs), shapes, len(bufs), make, in_place=len(bufs))


def _gelu_tanh(z):
    k = math.sqrt(2.0 / math.pi)
    t = jnp.tanh(k * (z + 0.044715 * (z * z * z)))
    return 0.5 * z * (1.0 + t), t


def _gelu_tanh_grad(z, t):
    k = math.sqrt(2.0 / math.pi)
    return 0.5 * (1.0 + t) + 0.5 * z * (1.0 - t * t) * (k * (1.0 + 3.0 * 0.044715 * (z * z)))


def _rope_angle_kernel(pos_row, invf_col):
    seq = pos_row.shape[1]

    def body(p_ref, f_ref, cos_ref, sin_ref):
        ang = p_ref[...].astype(F32) * f_ref[...]
        cos_ref[...] = jnp.cos(ang)
        sin_ref[...] = jnp.sin(ang)

    return pl.pallas_call(
        body, name="rope_angles", grid=(1,), out_shape=[jax.ShapeDtypeStruct((ROT_DIM // 2, seq), F32)] * 2,
        in_specs=[_full((1, seq)), _full((ROT_DIM // 2, 1))], out_specs=[_full((ROT_DIM // 2, seq))] * 2,
        compiler_params=_params("arbitrary"),
    )(pos_row, invf_col)


def _rope_lane_tables(cos, sin):
    cos_t, sin_t = cos.T, sin.T
    seq, half = cos_t.shape
    ones = jnp.ones((seq, HEAD_DIM - ROT_DIM), F32)
    c64 = jnp.concatenate([cos_t, cos_t, ones], axis=1)
    s1 = jnp.concatenate([sin_t, jnp.zeros((seq, HEAD_DIM - half), F32)], axis=1)
    s2 = jnp.concatenate([jnp.zeros((seq, half), F32), sin_t, jnp.zeros((seq, HEAD_DIM - ROT_DIM), F32)], axis=1)
    return jnp.concatenate([jnp.tile(t, (1, LANES // HEAD_DIM)) for t in (c64, s1, s2)], axis=1)


def _rope_apply(t, tab, sign):
    reps = t.shape[1] // LANES
    c_tab, s1, s2 = (jnp.tile(tab[:, LANES * k:LANES * (k + 1)], (1, reps)) if reps > 1
                     else tab[:, LANES * k:LANES * (k + 1)] for k in range(3))
    half = ROT_DIM // 2
    up = pltpu.roll(t, t.shape[1] - half, 1)
    down = pltpu.roll(t, half, 1)
    return t * c_tab + sign * (down * s2 - up * s1)


def _lane_masks(shape):
    lane = lax.broadcasted_iota(jnp.int32, shape, 1)
    return lane < HEAD_DIM, lane >= HEAD_DIM


HEADS_PER_GROUP = N_Q_HEADS // N_KV_HEADS
ATTN_SCALE = 1.0 / math.sqrt(HEAD_DIM)


def _attn_bias_t(first_block):
    kj = lax.broadcasted_iota(jnp.int32, (2 * CHUNK, CHUNK), 0)
    qi = lax.broadcasted_iota(jnp.int32, (2 * CHUNK, CHUNK), 1)
    ok = (kj > qi) & (kj <= qi + CHUNK)
    if first_block is not None:
        ok = ok & (jnp.logical_not(first_block) | (kj >= CHUNK))
    return jnp.tile(jnp.where(ok, 0.0, -jnp.inf), (1, HEADS_PER_GROUP))


def _group_rows(x, g, lo, hi):
    rows = []
    for r in range(HEADS_PER_GROUP):
        h = HEADS_PER_GROUP * g + r
        pair = x[:, LANES * (h // 2):LANES * (h // 2 + 1)]
        rows.append(jnp.where(hi if h % 2 else lo, pair, 0.0))
    return jnp.concatenate(rows, axis=0)


def _pairs_from_rows(rows, lo):
    return [jnp.where(lo, rows[2 * CHUNK * k:2 * CHUNK * k + CHUNK], rows[2 * CHUNK * k + CHUNK:2 * CHUNK * (k + 1)])
            for k in range(HEADS_PER_GROUP // 2)]


def _group_dup(a, b, g, lo2):
    return jnp.where(lo2, a, b) if g == 0 else jnp.where(lo2, b, a)


def _sink_row(sink_ref, g):
    return jnp.concatenate([sink_ref[HEADS_PER_GROUP * g + r:HEADS_PER_GROUP * g + r + 1, :]
                            for r in range(HEADS_PER_GROUP)], axis=1)


def _attn_probs_t(k_dup, q_rows, bias_t, sink_row):
    s_t = _dot_nt(k_dup, q_rows) * ATTN_SCALE + bias_t
    m = jnp.maximum(jnp.max(s_t, axis=0, keepdims=True), sink_row)
    p = jnp.exp(s_t - m)
    e_sink = jnp.exp(sink_row - m)
    inv = 1.0 / (jnp.sum(p, axis=0, keepdims=True) + e_sink)
    return p * inv, e_sink * inv


def _sgu_forward_pair(wm, vp, j):
    lo, hi = _lane_masks(vp.shape)
    lhs = jnp.concatenate([wm[2 * j], wm[2 * j + 1]], axis=1)
    rhs = jnp.concatenate([jnp.where(lo, vp, 0.0), jnp.where(hi, vp, 0.0)], axis=0)
    return _dot(lhs, rhs)


def _masked_spatial(w_ref):
    t = lax.broadcasted_iota(jnp.int32, (CHUNK, CHUNK), 0)
    s = lax.broadcasted_iota(jnp.int32, (CHUNK, CHUNK), 1)
    tril = s <= t
    return [jnp.where(tril, w_ref[g], 0.0) for g in range(GMLP_GROUPS)], tril, s >= t


def _mod_kernel(c_all, w_shard, b_shard, comm=None):
    n = w_shard.shape[1]
    tn = 512

    def body(c_ref, w_ref, b_ref, mod_ref, act_ref):
        cv = c_ref[...]
        act = cv * (1.0 / (1.0 + jnp.exp(-cv)))
        act_ref[...] = act
        mod_ref[...] = _dot(act, w_ref[...]) + b_ref[...]

    return _hosted_call(
        body, comm, name="ada_mod", grid=(n // tn,),
        out_shape=[jax.ShapeDtypeStruct((N_DEV, n), F32), jax.ShapeDtypeStruct((N_DEV, D_MODEL), F32)],
        in_specs=[_full((N_DEV, D_MODEL)), pl.BlockSpec((D_MODEL, tn), lambda i: (0, i)),
                  pl.BlockSpec((1, tn), lambda i: (0, i))],
        out_specs=[pl.BlockSpec((N_DEV, tn), lambda i: (0, i)), _full((N_DEV, D_MODEL))],
        semantics=("arbitrary",),
    )(c_all, w_shard, b_shard)


def _load_chip_blocks(chip_ref, gathered, local, dsts, sems, first_sem=0):
    for k, dst in enumerate(dsts):
        @pl.when(chip_ref[0] == k)
        def _():
            pltpu.make_async_copy(local, dst, sems.at[first_sem + k]).start()

        @pl.when(chip_ref[0] != k)
        def _():
            pltpu.make_async_copy(gathered.at[k], dst, sems.at[first_sem + k]).start()
    return [pltpu.make_async_copy(local, dst, sems.at[first_sem + k]).wait for k, dst in enumerate(dsts)]


def _in_proj_kernel(x, vecs, w_in_t, comm=None):
    seq = x.shape[0]
    tm = 512

    def body(x_ref, v_ref, w_ref, proj_ref, h_ref):
        xv = x_ref[...]
        rstd = lax.rsqrt(_mean_last(xv * xv) + EPS)
        n1 = (xv * rstd) * v_ref[0:1, :]
        h = n1 * (1.0 + v_ref[2:3, :]) + v_ref[1:2, :]
        hb = h.astype(MXU_DTYPE)
        h_ref[...] = hb
        proj_ref[...] = _dot_nt(hb, w_ref[...])

    return _hosted_call(
        body, comm, name="in_proj", grid=(seq // tm,),
        out_shape=[jax.ShapeDtypeStruct((seq, IN_PROJ_WIDTH), F32),
                   jax.ShapeDtypeStruct((seq, D_MODEL), MXU_DTYPE)],
        in_specs=[pl.BlockSpec((tm, D_MODEL), lambda i: (i, 0)), _full((8, D_MODEL)),
                  _full((IN_PROJ_WIDTH, D_MODEL))],
        out_specs=[pl.BlockSpec((tm, IN_PROJ_WIDTH), lambda i: (i, 0)),
                   pl.BlockSpec((tm, D_MODEL), lambda i: (i, 0))],
        semantics=("arbitrary",),
    )(x, vecs, w_in_t)


MIXER_BLOCKS_PER_STEP = 4
KV_START = 2 * GMLP_WIDTH + ATTN_WIDTH


def _mixer_fwd_kernel(proj, rope_tab, w_spatial, bias_full, sink_rows, comm=None):
    seq = proj.shape[0]
    per = MIXER_BLOCKS_PER_STEP
    steps = seq // (CHUNK * per)
    kv_col = KV_START // (2 * KV_WIDTH)

    def body(proj_ref, prev_ref, tab_ref, ptab_ref, w_ref, bias_ref, sink_ref, cat_ref):
        i = pl.program_id(0)
        wm, _, _ = _masked_spatial(w_ref)
        lo, hi = _lane_masks((CHUNK, LANES))
        lo2, _ = _lane_masks((2 * CHUNK, LANES))
        o = 2 * GMLP_WIDTH
        for s in range(per):
            rows, before = slice(CHUNK * s, CHUNK * (s + 1)), slice(CHUNK * (s - 1), CHUNK * s)
            for j in range(GMLP_GROUPS // 2):
                cols = slice(LANES * j, LANES * (j + 1))
                vcols = slice(GMLP_WIDTH + LANES * j, GMLP_WIDTH + LANES * (j + 1))
                u, _ = _gelu_tanh(proj_ref[rows, cols])
                vp, _ = _gelu_tanh(proj_ref[rows, vcols])
                sv = _sgu_forward_pair(wm, vp, j) + bias_ref[:, cols]
                cat_ref[rows, cols] = (u * sv).astype(cat_ref.dtype)
            tab = tab_ref[rows, :]
            if s == 0:
                prev_kv, prev_tab, first = prev_ref[...], ptab_ref[...], i == 0
            else:
                prev_kv, prev_tab, first = proj_ref[before, KV_START:KV_START + 2 * KV_WIDTH], tab_ref[before, :], None
            q_r = _rope_apply(proj_ref[rows, o:o + ATTN_WIDTH], tab, 1.0)
            k_cur = _rope_apply(proj_ref[rows, KV_START:KV_START + KV_WIDTH], tab, 1.0)
            k_prev = _rope_apply(prev_kv[:, 0:KV_WIDTH], prev_tab, 1.0)
            k_a = jnp.concatenate([k_prev, k_cur], axis=0)
            v_a = jnp.concatenate([prev_kv[:, KV_WIDTH:2 * KV_WIDTH],
                                   proj_ref[rows, KV_START + KV_WIDTH:KV_START + 2 * KV_WIDTH]], axis=0)
            k_b = pltpu.roll(k_a, HEAD_DIM, 1)
            v_b = pltpu.roll(v_a, HEAD_DIM, 1)
            bias_t = _attn_bias_t(first)
            for g in range(N_KV_HEADS):
                p_t, _ = _attn_probs_t(_group_dup(k_a, k_b, g, lo2), _group_rows(q_r, g, lo, hi), bias_t,
                                       _sink_row(sink_ref, g))
                o_t = _dot(_group_dup(v_a, v_b, g, lo2).T, p_t)
                for k, pair in enumerate(_pairs_from_rows(o_t.T, lo)):
                    c0 = GMLP_WIDTH + LANES * (2 * g + k)
                    cat_ref[rows, c0:c0 + LANES] = pair.astype(cat_ref.dtype)

    return _hosted_call(
        body, comm, name="mixer_fwd", grid=(steps,),
        out_shape=[jax.ShapeDtypeStruct((seq, D_MODEL), MXU_DTYPE)],
        in_specs=[pl.BlockSpec((CHUNK * per, IN_PROJ_WIDTH), lambda i: (i, 0)),
                  pl.BlockSpec((CHUNK, 2 * KV_WIDTH), lambda i: (jnp.maximum(per * i - 1, 0), kv_col)),
                  pl.BlockSpec((CHUNK * per, 3 * LANES), lambda i: (i, 0)),
                  pl.BlockSpec((CHUNK, 3 * LANES), lambda i: (jnp.maximum(per * i - 1, 0), 0)),
                  _full((GMLP_GROUPS, CHUNK, CHUNK)), _full((CHUNK, GMLP_WIDTH)),
                  _full((N_Q_HEADS, LANES))],
        out_specs=[pl.BlockSpec((CHUNK * per, D_MODEL), lambda i: (i, 0))],
        semantics=("arbitrary",),
    )(proj, proj, rope_tab, rope_tab, w_spatial, bias_full, sink_rows)


def _trunk_kernel(x, target, cat, vecs, chip_idx, gathered, local):
    seq = x.shape[0]
    tm = 256
    nj = D_FF // D_MODEL
    out_rows = D_MODEL // N_CHIPS

    def body(chip_ref, x_ref, t_ref, cat_ref, v_ref, g_out, g_w1, g_w2, l_out, l_w1, l_w2,
             dx1_ref, dcat_ref, dmix_ref, h2_ref, r_ref, da_ref, dff_ref, sums_ref,
             wout, w1, w2, a_scr, sem):
        i = pl.program_id(0)

        @pl.when(i == 0)
        def _():
            waits = _load_chip_blocks(chip_ref, g_out, l_out,
                                      [wout.at[pl.ds(out_rows * k, out_rows)] for k in range(N_CHIPS)], sem)
            waits += _load_chip_blocks(chip_ref, g_w1, l_w1, [w1.at[k] for k in range(N_CHIPS)], sem, N_CHIPS)
            waits += _load_chip_blocks(chip_ref, g_w2, l_w2, [w2.at[k] for k in range(N_CHIPS)], sem, 2 * N_CHIPS)
            for wait in waits:
                wait()
            sums_ref[...] = jnp.zeros_like(sums_ref)

        gate1, shift2, scale2 = v_ref[0:1, :], v_ref[1:2, :], v_ref[2:3, :]
        gate2, g_ffn, g_final = v_ref[3:4, :], v_ref[4:5, :], v_ref[5:6, :]

        mix = _dot(cat_ref[...], wout[...])
        x1 = x_ref[...] + gate1 * mix
        rstd2 = lax.rsqrt(_mean_last(x1 * x1) + EPS)
        xh2 = x1 * rstd2
        n2 = xh2 * g_ffn
        h2b = (n2 * (1.0 + scale2) + shift2).astype(MXU_DTYPE)
        h2_ref[...] = h2b
        ff = jnp.zeros((tm, D_MODEL), F32)
        for j in range(nj):
            a = _dot(h2b, w1[j])
            a_scr[j] = a
            relu = jnp.maximum(a, 0.0)
            rb = (relu * relu).astype(MXU_DTYPE)
            r_ref[:, D_MODEL * j:D_MODEL * (j + 1)] = rb
            ff = ff + _dot(rb, w2[j])
        x2 = x1 + gate2 * ff
        rstd3 = lax.rsqrt(_mean_last(x2 * x2) + EPS)
        xh3 = x2 * rstd3
        err = xh3 * g_final - t_ref[...]
        loss = 0.5 * _rowsum(_mean_last(err * err))
        dy = err * (1.0 / D_MODEL)
        dxh3 = dy * g_final
        dx2 = rstd3 * (dxh3 - xh3 * _mean_last(dxh3 * xh3))
        dffb = (dx2 * gate2).astype(MXU_DTYPE)
        dff_ref[...] = dffb
        dh2 = jnp.zeros((tm, D_MODEL), F32)
        for j in range(nj):
            dr = _dot_nt(dffb, w2[j])
            dab = (dr * (2.0 * jnp.maximum(a_scr[j], 0.0))).astype(MXU_DTYPE)
            da_ref[:, D_MODEL * j:D_MODEL * (j + 1)] = dab
            dh2 = dh2 + _dot_nt(dab, w1[j])
        dn2 = dh2 * (1.0 + scale2)
        dxh2 = dn2 * g_ffn
        dx1 = dx2 + rstd2 * (dxh2 - xh2 * _mean_last(dxh2 * xh2))
        dx1_ref[...] = dx1
        dmixb = (dx1 * gate1).astype(MXU_DTYPE)
        dmix_ref[...] = dmixb
        dcat_ref[...] = _dot_nt(dmixb, wout[...])

        sums_ref[0:1, :] += _rowsum(dh2)
        sums_ref[1:2, :] += _rowsum(dh2 * n2)
        sums_ref[2:3, :] += _rowsum(dx2 * ff)
        sums_ref[3:4, :] += _rowsum(dn2 * xh2)
        sums_ref[4:5, :] += _rowsum(dy * xh3)
        sums_ref[5:6, :] += _rowsum(dx1 * mix)
        sums_ref[6:7, :] += jnp.broadcast_to(loss, (1, D_MODEL))

    tok = lambda w: pl.BlockSpec((tm, w), lambda i, chip: (i, 0))
    return _hosted_call(
        body, None, name="trunk", grid=(seq // tm,), n_prefetch=1,
        out_shape=[jax.ShapeDtypeStruct((seq, D_MODEL), F32), jax.ShapeDtypeStruct((seq, D_MODEL), F32),
                   jax.ShapeDtypeStruct((seq, D_MODEL), MXU_DTYPE), jax.ShapeDtypeStruct((seq, D_MODEL), MXU_DTYPE),
                   jax.ShapeDtypeStruct((seq, D_FF), MXU_DTYPE), jax.ShapeDtypeStruct((seq, D_FF), MXU_DTYPE),
                   jax.ShapeDtypeStruct((seq, D_MODEL), MXU_DTYPE), jax.ShapeDtypeStruct((8, D_MODEL), F32)],
        in_specs=[tok(D_MODEL), tok(D_MODEL), tok(D_MODEL), _full((8, D_MODEL))] + [_any()] * 6,
        out_specs=[tok(D_MODEL), tok(D_MODEL), tok(D_MODEL), tok(D_MODEL), tok(D_FF), tok(D_FF), tok(D_MODEL),
                   _full((8, D_MODEL))],
        scratch_shapes=[pltpu.VMEM((D_MODEL, D_MODEL), MXU_DTYPE), pltpu.VMEM((nj, D_MODEL, D_MODEL), MXU_DTYPE),
                        pltpu.VMEM((nj, D_MODEL, D_MODEL), MXU_DTYPE), pltpu.VMEM((nj, tm, D_MODEL), F32),
                        pltpu.SemaphoreType.DMA((3 * N_CHIPS,))],
        semantics=("arbitrary",),
    )(chip_idx, x, target, cat, vecs, *gathered, *local)


def _mixer_bwd_kernel(proj, rope_tab, dcat, w_spatial, w_spatial_t, bias_full, sink_rows, dev_idx, comm=None):
    seq = proj.shape[0]
    per = MIXER_BLOCKS_PER_STEP
    steps = seq // (CHUNK * per)
    kv_col = KV_START // (2 * KV_WIDTH)

    def body(dev_ref, proj_ref, prev_ref, tab_ref, ptab_ref, dcat_ref, w_ref, wt_ref, bias_ref, sink_ref,
             dproj_ref, dw_out, db_ref, dsink_ref, carry, dw_ref):
        del dev_ref
        step = pl.program_id(0)

        @pl.when(step == 0)
        def _():
            carry[...] = jnp.zeros_like(carry)
            dw_ref[...] = jnp.zeros_like(dw_ref)
            db_ref[...] = jnp.zeros_like(db_ref)
            dsink_ref[...] = jnp.zeros_like(dsink_ref)

        for s in reversed(range(per)):
            rows = pl.ds(CHUNK * s, CHUNK)
            if s == 0:
                before, before_tab, first = prev_ref, ptab_ref, step == steps - 1
            else:
                before = proj_ref.at[pl.ds(CHUNK * (s - 1), CHUNK), pl.ds(KV_START, 2 * KV_WIDTH)]
                before_tab, first = tab_ref.at[pl.ds(CHUNK * (s - 1), CHUNK)], None
            one_block(proj_ref.at[rows], before, tab_ref.at[rows], before_tab, dcat_ref.at[rows], w_ref, wt_ref,
                      bias_ref, sink_ref, dproj_ref.at[rows], dw_ref, db_ref, dsink_ref, carry, first)

        @pl.when(step == steps - 1)
        def _():
            dw_out[...] = dw_ref[...].astype(dw_out.dtype)

    def one_block(proj_ref, prev_ref, tab_ref, ptab_ref, dcat_ref, w_ref, wt_ref, bias_ref, sink_ref,
                  dproj_ref, dw_ref, db_ref, dsink_ref, carry, first):
        wm, tril, triu = _masked_spatial(w_ref)
        lo, hi = _lane_masks((CHUNK, LANES))
        lane = lax.broadcasted_iota(jnp.int32, (CHUNK, LANES), 1)
        db = jnp.zeros((CHUNK, LANES), F32)
        for j in range(GMLP_GROUPS // 2):
            cols = slice(LANES * j, LANES * (j + 1))
            vcols = slice(GMLP_WIDTH + LANES * j, GMLP_WIDTH + LANES * (j + 1))
            zu, zv = proj_ref[:, cols], proj_ref[:, vcols]
            u, tu = _gelu_tanh(zu)
            vp, tv = _gelu_tanh(zv)
            sv = _sgu_forward_pair(wm, vp, j) + bias_ref[:, cols]
            dout = dcat_ref[:, cols]
            du = dout * sv
            dsv = dout * u
            dsv_lo, dsv_hi = jnp.where(lo, dsv, 0.0), jnp.where(hi, dsv, 0.0)
            lhs_t = jnp.concatenate([jnp.where(triu, wt_ref[2 * j], 0.0),
                                     jnp.where(triu, wt_ref[2 * j + 1], 0.0)], axis=1)
            dv = _dot(lhs_t, jnp.concatenate([dsv_lo, dsv_hi], axis=0))
            dw_ref[2 * j] += jnp.where(tril, _dot_nt(dsv_lo, vp), 0.0)
            dw_ref[2 * j + 1] += jnp.where(tril, _dot_nt(dsv_hi, vp), 0.0)
            db = db + (jnp.where(lane == 2 * j, jnp.sum(dsv_lo, axis=1, keepdims=True), 0.0)
                       + jnp.where(lane == 2 * j + 1, jnp.sum(dsv_hi, axis=1, keepdims=True), 0.0))
            dproj_ref[:, cols] = (du * _gelu_tanh_grad(zu, tu)).astype(dproj_ref.dtype)
            dproj_ref[:, vcols] = (dv * _gelu_tanh_grad(zv, tv)).astype(dproj_ref.dtype)
        db_ref[...] += db
        o = 2 * GMLP_WIDTH
        tab = tab_ref[...]
        q_r = _rope_apply(proj_ref[:, o:o + ATTN_WIDTH], tab, 1.0)
        k_cur = _rope_apply(proj_ref[:, o + ATTN_WIDTH:o + ATTN_WIDTH + KV_WIDTH], tab, 1.0)
        k_prev = _rope_apply(prev_ref[:, 0:KV_WIDTH], ptab_ref[...], 1.0)
        k_a = jnp.concatenate([k_prev, k_cur], axis=0)
        v_a = jnp.concatenate([prev_ref[:, KV_WIDTH:2 * KV_WIDTH],
                               proj_ref[:, o + ATTN_WIDTH + KV_WIDTH:o + ATTN_WIDTH + 2 * KV_WIDTH]], axis=0)
        k_b = pltpu.roll(k_a, HEAD_DIM, 1)
        v_b = pltpu.roll(v_a, HEAD_DIM, 1)
        bias_t = _attn_bias_t(first)
        lo2, _ = _lane_masks((2 * CHUNK, LANES))
        dout_b = dcat_ref[:, GMLP_WIDTH:GMLP_WIDTH + ATTN_WIDTH]
        dk_tot, dv_tot, dq_pairs = [], [], []
        for g in range(N_KV_HEADS):
            k_dup, v_dup = _group_dup(k_a, k_b, g, lo2), _group_dup(v_a, v_b, g, lo2)
            q_rows = _group_rows(q_r, g, lo, hi)
            do_rows = _group_rows(dout_b, g, lo, hi)
            p_t, p_sink = _attn_probs_t(k_dup, q_rows, bias_t, _sink_row(sink_ref, g))
            dp_t = _dot_nt(v_dup, do_rows)
            delta = jnp.sum(p_t * dp_t, axis=0, keepdims=True)
            ds_t = p_t * (dp_t - delta) * ATTN_SCALE
            dsink = -p_sink * delta
            for r in range(HEADS_PER_GROUP):
                h = HEADS_PER_GROUP * g + r
                dsink_ref[h:h + 1, :] += jnp.broadcast_to(
                    jnp.sum(dsink[:, LANES * r:LANES * (r + 1)], axis=1, keepdims=True), (1, LANES))
            dk_full = _dot(ds_t, q_rows)
            dv_full = _dot(p_t, do_rows)
            dk_tot.append(dk_full + pltpu.roll(dk_full, HEAD_DIM, 1))
            dv_tot.append(dv_full + pltpu.roll(dv_full, HEAD_DIM, 1))
            dq_t = _dot(k_dup.T, ds_t)
            dq_pairs += _pairs_from_rows(dq_t.T, lo)
        dk_all = jnp.where(lo2, dk_tot[0], dk_tot[1])
        dv_all = jnp.where(lo2, dv_tot[0], dv_tot[1])
        dk_cur = dk_all[CHUNK:, :] + carry[:, 0:KV_WIDTH]
        dv_cur = dv_all[CHUNK:, :] + carry[:, KV_WIDTH:2 * KV_WIDTH]
        carry[:, 0:KV_WIDTH] = dk_all[:CHUNK, :]
        carry[:, KV_WIDTH:2 * KV_WIDTH] = dv_all[:CHUNK, :]
        dq = _rope_apply(jnp.concatenate(dq_pairs, axis=1), tab, -1.0)
        dproj_ref[:, o:o + ATTN_WIDTH] = dq.astype(dproj_ref.dtype)
        dproj_ref[:, o + ATTN_WIDTH:o + ATTN_WIDTH + KV_WIDTH] = (
            _rope_apply(dk_cur, tab, -1.0).astype(dproj_ref.dtype))
        dproj_ref[:, o + ATTN_WIDTH + KV_WIDTH:o + ATTN_WIDTH + 2 * KV_WIDTH] = dv_cur.astype(dproj_ref.dtype)

    rev = lambda i: steps - 1 - i
    before = lambda i: jnp.maximum(per * rev(i) - 1, 0)
    slot = lambda shape: pl.BlockSpec((None,) + shape, lambda i, d: (d[0],) + (0,) * len(shape))
    return _hosted_call(
        body, comm, name="mixer_bwd", grid=(steps,), n_prefetch=1,
        out_shape=[jax.ShapeDtypeStruct((seq, IN_PROJ_WIDTH), MXU_DTYPE),
                   jax.ShapeDtypeStruct((N_DEV, GMLP_GROUPS, CHUNK, CHUNK), GRAD_COMM_DTYPE),
                   jax.ShapeDtypeStruct((N_DEV, CHUNK, LANES), F32),
                   jax.ShapeDtypeStruct((N_DEV, N_Q_HEADS, LANES), F32)],
        in_specs=[pl.BlockSpec((CHUNK * per, IN_PROJ_WIDTH), lambda i, d: (rev(i), 0)),
                  pl.BlockSpec((CHUNK, 2 * KV_WIDTH), lambda i, d: (before(i), kv_col)),
                  pl.BlockSpec((CHUNK * per, 3 * LANES), lambda i, d: (rev(i), 0)),
                  pl.BlockSpec((CHUNK, 3 * LANES), lambda i, d: (before(i), 0)),
                  pl.BlockSpec((CHUNK * per, D_MODEL), lambda i, d: (rev(i), 0)),
                  _full((GMLP_GROUPS, CHUNK, CHUNK)), _full((GMLP_GROUPS, CHUNK, CHUNK)),
                  _full((CHUNK, GMLP_WIDTH)), _full((N_Q_HEADS, LANES))],
        out_specs=[pl.BlockSpec((CHUNK * per, IN_PROJ_WIDTH), lambda i, d: (rev(i), 0)),
                   slot((GMLP_GROUPS, CHUNK, CHUNK)), slot((CHUNK, LANES)), slot((N_Q_HEADS, LANES))],
        scratch_shapes=[pltpu.VMEM((CHUNK, 2 * KV_WIDTH), F32), pltpu.VMEM((GMLP_GROUPS, CHUNK, CHUNK), F32)],
        semantics=("arbitrary",),
    )(dev_idx, proj, proj, rope_tab, rope_tab, dcat, w_spatial, w_spatial_t, bias_full, sink_rows)


def _in_proj_bwd_kernel(x, dx1, dproj, vecs, w_in_t, comm=None):
    seq = x.shape[0]
    tm = 512

    def body(x_ref, dx1_ref, dp_ref, v_ref, w_ref, gx_ref, sums_ref):
        @pl.when(pl.program_id(0) == 0)
        def _():
            sums_ref[...] = jnp.zeros_like(sums_ref)

        g_mix, scale1 = v_ref[0:1, :], v_ref[2:3, :]
        dh = _dot(dp_ref[...], w_ref[...])
        xv = x_ref[...]
        rstd = lax.rsqrt(_mean_last(xv * xv) + EPS)
        xh = xv * rstd
        dn1 = dh * (1.0 + scale1)
        dxh = dn1 * g_mix
        gx_ref[...] = dx1_ref[...] + rstd * (dxh - xh * _mean_last(dxh * xh))
        sums_ref[0:1, :] += _rowsum(dh)
        sums_ref[1:2, :] += _rowsum(dh * (xh * g_mix))
        sums_ref[2:3, :] += _rowsum(dn1 * xh)

    tok = lambda w: pl.BlockSpec((tm, w), lambda i: (i, 0))
    return _hosted_call(
        body, comm, name="in_proj_bwd", grid=(seq // tm,),
        out_shape=[jax.ShapeDtypeStruct((seq, D_MODEL), F32), jax.ShapeDtypeStruct((8, D_MODEL), F32)],
        in_specs=[tok(D_MODEL), tok(D_MODEL), tok(IN_PROJ_WIDTH), _full((8, D_MODEL)),
                  _full((IN_PROJ_WIDTH, D_MODEL))],
        out_specs=[tok(D_MODEL), _full((8, D_MODEL))],
        semantics=("arbitrary",),
    )(x, dx1, dproj, vecs, w_in_t)


class _GradTiles(NamedTuple):
    tm: int
    tn: int
    n_tiles: int
    chips_per_tile: int
    a_index: Callable
    b_index: Callable


def _weight_grad_kernel(a, b, c_idx, name, tiles, comm=None):
    seq = a.shape[0]
    tk = min(seq, 4096)
    nk = seq // tk
    tm, tn, n_tiles, per = tiles.tm, tiles.tn, tiles.n_tiles, tiles.chips_per_tile
    rows = tm // per

    def half(phase, c):
        return phase * c[0] + (1 - phase) * (1 - c[0])

    def body(c_ref, a_ref, b_ref, o_ref, acc, stage, landed, send_sems, recv_sems):
        del c_ref
        phase, t, kk = pl.program_id(0), pl.program_id(1), pl.program_id(2)
        x, y, c, _ = _mesh_place()

        def copy(tile):
            return pltpu.make_async_remote_copy(
                src_ref=stage.at[tile], dst_ref=landed.at[tile], send_sem=send_sems.at[tile],
                recv_sem=recv_sems.at[tile], device_id=(x, y, 1 - c), device_id_type=MESH)

        @pl.when(kk == 0)
        def _():
            acc[...] = jnp.zeros_like(acc)

        acc[...] += _dot_tn(a_ref[...], b_ref[...])

        @pl.when((kk == nk - 1) & (phase == 0))
        def _():
            stage[t] = acc[...].astype(stage.dtype)
            copy(t).start()

        @pl.when((kk == nk - 1) & (phase == 1))
        def _():
            copy(t).wait_recv()
            total = acc[...] + landed[t].astype(F32)
            for q in range(per):
                o_ref[q] = total[rows * q:rows * (q + 1)].astype(o_ref.dtype)

        @pl.when((kk == nk - 1) & (phase == 1) & (t == n_tiles - 1))
        def _():
            for tile in range(n_tiles):
                copy(tile).wait_send()

    out = _hosted_call(
        body, comm, name=name, grid=(2, n_tiles, nk), n_prefetch=1,
        out_shape=[jax.ShapeDtypeStruct((n_tiles * per, rows, tn), GRAD_COMM_DTYPE)],
        in_specs=[pl.BlockSpec((tk, tm), lambda p, t, k, c: (k, tiles.a_index(t, half(p, c)))),
                  pl.BlockSpec((tk, tn), lambda p, t, k, c: (k, tiles.b_index(t, half(p, c))))],
        out_specs=[pl.BlockSpec((per, rows, tn), lambda p, t, k, c: (p * t, 0, 0))],
        scratch_shapes=[pltpu.VMEM((tm, tn), F32), pltpu.VMEM((n_tiles, tm, tn), GRAD_COMM_DTYPE),
                        pltpu.VMEM((n_tiles, tm, tn), GRAD_COMM_DTYPE),
                        pltpu.SemaphoreType.DMA((n_tiles,)), pltpu.SemaphoreType.DMA((n_tiles,))],
        semantics=("arbitrary", "arbitrary", "arbitrary"),
    )(c_idx, a, b)
    return out[0] if comm is None else out


def _row_tile(rows, most=256, sublanes=16):
    return max(t for t in range(sublanes, most + 1, sublanes) if rows % t == 0)


def _adam_update(w, g, m, v):
    m_new = ADAM_B1 * m + (1.0 - ADAM_B1) * g
    v_new = ADAM_B2 * v + (1.0 - ADAM_B2) * (g * g)
    m_hat = m_new / (1.0 - ADAM_B1 ** ADAM_STEP)
    v_hat = v_new / (1.0 - ADAM_B2 ** ADAM_STEP)
    delta = -ADAM_LR * (m_hat / (jnp.sqrt(v_hat) + ADAM_EPS) + ADAM_WD * w)
    return delta, m_new, v_new


def _sum_chips_kernel(own, others, place, name):
    _, r, n = own.shape
    tr = _row_tile(r)

    def body(place_ref, own_ref, oth_ref, o_ref):
        del place_ref
        acc = own_ref[...].astype(F32)
        for k in range(N_CHIPS - 1):
            acc = acc + oth_ref[k].astype(F32)
        o_ref[...] = acc

    return pl.pallas_call(
        body, name=name, out_shape=jax.ShapeDtypeStruct((2, r, n), F32),
        grid_spec=pltpu.PrefetchScalarGridSpec(
            num_scalar_prefetch=1, grid=(r // tr,),
            in_specs=[pl.BlockSpec((None, tr, n), lambda i, p: (p[0], i, 0)),
                      pl.BlockSpec((N_CHIPS - 1, tr, n), lambda i, p: (0, i, 0))],
            out_specs=pl.BlockSpec((None, tr, n), lambda i, p: (p[1], i, 0))),
        compiler_params=_params("parallel"),
    )(place, own, others)


def _adam_kernel(w, g, m, v, name, after=()):
    r, n = w.shape
    by_columns = g.shape[1] == r
    tr, tn = _row_tile(g.shape[1], most=512), g.shape[2]

    def body(w_ref, g_ref, m_ref, v_ref, *rest):
        g_out, d_ref, mo_ref, vo_ref = rest[len(after):]
        gv = g_ref[...]
        g_out[...] = gv
        d_ref[...], mo_ref[...], vo_ref[...] = _adam_update(w_ref[...], gv, m_ref[...], v_ref[...])

    steps = g.shape[1] // tr
    spec = pl.BlockSpec((tr, tn), (lambda h, i: (i, h)) if by_columns else (lambda h, i: (h * steps + i, 0)))
    return pl.pallas_call(
        body, name=name, grid=(2, steps), out_shape=[jax.ShapeDtypeStruct((r, n), F32)] * 4,
        in_specs=[spec, pl.BlockSpec((None, tr, tn), lambda h, i: (h, i, 0)), spec, spec] + [_any()] * len(after),
        out_specs=[spec] * 4, compiler_params=_params("parallel", "parallel"),
    )(w, g, m, v, *after)


SMALL_PARAMS = ("b_ada", "g_mix", "g_ffn", "g_final", "b_spatial", "sinks", "w_spatial")


def _small_update_kernel(gathered, params):
    shapes = [params[nm][0].shape for nm in SMALL_PARAMS]

    def body(*refs):
        g_refs, refs = refs[:5], refs[5:]
        p_refs, refs = refs[:3 * len(SMALL_PARAMS)], refs[3 * len(SMALL_PARAMS):]
        loss_ref, o_refs = refs[0], refs[1:]

        def total(ref):
            acc = ref[0].astype(F32)
            for k in range(1, N_DEV):
                acc = acc + ref[k].astype(F32)
            return acc

        s1, s2, db, ds, dw = (total(r) for r in g_refs)
        loss_ref[...] = jnp.broadcast_to(s2[6:7, 0:1], loss_ref.shape)
        grads = {"b_ada": [s1[0:1], s1[1:2], s2[5:6], s2[0:1], s2[1:2], s2[2:3]], "g_mix": [s1[2:3]],
                 "g_ffn": [s2[3:4]], "g_final": [s2[4:5]], "b_spatial": [db.T[0:GMLP_GROUPS]],
                 "w_spatial": [dw]}
        lane = lax.broadcasted_iota(jnp.int32, (1, LANES), 1)
        sink_row = jnp.zeros((1, LANES), F32)
        for h in range(N_Q_HEADS):
            sink_row = sink_row + jnp.where(lane == h, ds[h:h + 1, :], 0.0)
        grads["sinks"] = [sink_row[:, 0:N_Q_HEADS]]
        for i, nm in enumerate(SMALL_PARAMS):
            w_ref, m_ref, v_ref = p_refs[3 * i:3 * i + 3]
            outs = o_refs[4 * i:4 * i + 4]
            width = grads[nm][0].shape[1]
            for k, g in enumerate(grads[nm]):
                cols = slice(width * k, width * (k + 1))
                upd = _adam_update(w_ref[:, cols], g, m_ref[:, cols], v_ref[:, cols])
                for o_ref, val in zip(outs, (g,) + upd):
                    o_ref[:, cols] = val

    flat = [a for nm in SMALL_PARAMS for a in params[nm]]
    out_shape = [jax.ShapeDtypeStruct((8, LANES), F32)]
    out_shape += [jax.ShapeDtypeStruct(s, F32) for s in shapes for _ in range(4)]
    outs = pl.pallas_call(
        body, name="small_update", grid=(1,), out_shape=out_shape,
        in_specs=[_full(g.shape) for g in gathered] + [_full(a.shape) for a in flat],
        out_specs=[_full(s.shape) for s in out_shape],
        compiler_params=_params("arbitrary"),
    )(*gathered, *flat)
    return {nm: outs[1 + 4 * i:5 + 4 * i] for i, nm in enumerate(SMALL_PARAMS)}, outs[0]


def _ada_update_kernel(act_t, dmod, w, m, v):
    r, n = w.shape
    tr = 256

    def body(a_ref, d_ref, w_ref, m_ref, v_ref, g_ref, dl_ref, mo_ref, vo_ref):
        g = _dot(a_ref[...], d_ref[...])
        g_ref[...] = g
        dl_ref[...], mo_ref[...], vo_ref[...] = _adam_update(w_ref[...], g, m_ref[...], v_ref[...])

    spec = pl.BlockSpec((tr, n), lambda i: (i, 0))
    return pl.pallas_call(
        body, name="ada_update", grid=(r // tr,), out_shape=[jax.ShapeDtypeStruct((r, n), F32)] * 4,
        in_specs=[pl.BlockSpec((tr, N_DEV), lambda i: (i, 0)), _full((N_DEV, n)), spec, spec, spec],
        out_specs=[spec] * 4, compiler_params=_params("parallel"),
    )(act_t, dmod, w, m, v)


def kernel(x, c, positions, w_ada, b_ada, g_mix, w_in, w_spatial, b_spatial, sinks, w_out, g_ffn, w_ff1, w_ff2, g_final, loss_target, m_w_ada, m_b_ada, m_g_mix, m_w_in, m_w_spatial, m_b_spatial, m_sinks, m_w_out, m_g_ffn, m_w_ff1, m_w_ff2, m_g_final, v_w_ada, v_b_ada, v_g_mix, v_w_in, v_w_spatial, v_b_spatial, v_sinks, v_w_out, v_g_ffn, v_w_ff1, v_w_ff2, v_g_final):
    xi, yi, ci = lax.axis_index("x"), lax.axis_index("y"), lax.axis_index("c")
    chip = 2 * xi + yi
    dev = 2 * chip + ci
    seq = x.shape[1]
    x2, tgt = x[0], loss_target[0]
    ada_cols = w_ada.shape[2]

    big = {"w_in": tuple(a[0].T for a in (w_in, m_w_in, v_w_in)),
           "w_out": (w_out[0], m_w_out[0], v_w_out[0]), "w_ff1": (w_ff1[0], m_w_ff1[0], v_w_ff1[0]),
           "w_ff2": (w_ff2[0], m_w_ff2[0], v_w_ff2[0])}

    def halves(nm):
        r, n = big[nm][0].shape
        return big[nm][0].astype(WEIGHT_COMM_DTYPE).reshape(2, r // 2, n)

    chip_idx = chip.reshape(1).astype(jnp.int32)
    c_all, w_in_t, g_out = _all_gather8([c, halves("w_in"), halves("w_out")], "gather_first",
                                        split=[False, True, True], skip_own=(2,))
    c_all, w_in_t = c_all.reshape(N_DEV, D_MODEL), w_in_t.reshape(IN_PROJ_WIDTH, D_MODEL)
    b_shard = lax.dynamic_slice(b_ada, (0, chip * ada_cols), (1, ada_cols))
    mod_part, act = _mod_kernel(c_all, w_ada[0], b_shard)
    mod_all, = _all_gather8([mod_part], "gather_mod")
    mod_me = lax.dynamic_index_in_dim(mod_all[0::2], dev, axis=1, keepdims=False)
    mod_me = mod_me.reshape(N_MOD, D_MODEL)
    shift1, scale1, gate1, shift2, scale2, gate2 = (mod_me[k:k + 1] for k in range(N_MOD))

    zeros_row = jnp.zeros((1, D_MODEL), F32)
    vecs1 = jnp.concatenate([g_mix, shift1, scale1] + [zeros_row] * 5, axis=0)
    vecs2 = jnp.concatenate([gate1, shift2, scale2, gate2, g_ffn, g_final.reshape(1, D_MODEL)]
                            + [zeros_row] * 2, axis=0)
    bias_full = jnp.repeat(b_spatial[0].T, HEAD_DIM, axis=1)
    sink_rows = jnp.broadcast_to(sinks[0][:, None], (N_Q_HEADS, LANES))
    inv_freq = ROPE_THETA ** (-jnp.arange(0, ROT_DIM, 2, dtype=F32) / ROT_DIM)
    rope_tab = _rope_lane_tables(*_rope_angle_kernel(positions, inv_freq.reshape(ROT_DIM // 2, 1)))

    trunk_weights = ["w_out", "w_ff1", "w_ff2"]
    shards = [halves(nm) for nm in trunk_weights]
    proj, hb, *staged = _in_proj_kernel(x2, vecs1, w_in_t, comm=_gather2d_first(shards[1:]))
    cat, *staged = _mixer_fwd_kernel(proj, rope_tab, w_spatial[0], bias_full, sink_rows,
                                     comm=_gather2d_second(staged, shards[1:]))
    staged = [g_out] + list(_gather_forward(staged, "gather_forward"))
    dx1, dcat, dmix, h2b, rb, dab, dffb, sums2 = _trunk_kernel(
        x2, tgt, cat, vecs2, chip_idx,
        [g.reshape((N_CHIPS,) + big[nm][0].shape) for nm, g in zip(trunk_weights, staged)],
        [s.reshape(big[nm][0].shape) for nm, s in zip(trunk_weights, shards)])

    c_idx = ci.reshape(1).astype(jnp.int32)
    place = jnp.stack([chip, ci]).astype(jnp.int32)
    half_d = D_MODEL // 2
    cs_ff2 = _weight_grad_kernel(rb, dffb, c_idx, "dw_ff2",
                                 _GradTiles(D_MODEL, half_d, N_CHIPS, 1, lambda t, h: t, lambda t, h: h))
    eighth = D_MODEL // 8
    cs_ff1, sc_ff2 = _weight_grad_kernel(
        h2b, dab, c_idx, "dw_ff1",
        _GradTiles(D_MODEL, half_d, N_CHIPS, 1, lambda t, h: 0, lambda t, h: 2 * t + h),
        comm=_scatter_job([cs_ff2], rows=(0, 6 * eighth)))
    cs_out, sc_ff2 = _weight_grad_kernel(
        cat, dmix, c_idx, "dw_out", _GradTiles(D_MODEL, half_d, 1, N_CHIPS, lambda t, h: 0, lambda t, h: h),
        comm=_scatter_job([cs_ff2], rows=(6 * eighth, eighth), into=[sc_ff2]))
    dproj, dw_spatial, db_lanes, dsink_rows, sc_ff2, sc_ff1, sc_out = _mixer_bwd_kernel(
        proj, rope_tab, dcat, w_spatial[0], w_spatial[0].transpose(0, 2, 1), bias_full, sink_rows,
        dev.reshape(1).astype(jnp.int32),
        comm=_merge_jobs(_scatter_job([cs_ff1, cs_out]),
                         _scatter_job([cs_ff2], rows=(7 * eighth, eighth), into=[sc_ff2])))
    totals = [_sum_chips_kernel(own, oth, place, "grad_sum_" + nm)
              for nm, own, oth in (("w_out", cs_out, sc_out), ("w_ff1", cs_ff1, sc_ff1), ("w_ff2", cs_ff2, sc_ff2))]
    small_slots = [db_lanes, dsink_rows, dw_spatial.reshape(N_DEV, GMLP_GROUPS * CHUNK, CHUNK)]
    cs_in, *rode = _weight_grad_kernel(
        dproj, hb, c_idx, "dw_in",
        _GradTiles(2 * W_IN_BLOCK, half_d, N_CHIPS // 2, 2, lambda t, h: t, lambda t, h: h),
        comm=_merge_jobs(_gather_job(small_slots), _share_job(totals)))
    small_stage1, shared = rode[:len(small_slots)], rode[len(small_slots):]
    scatter_in = _scatter_start(cs_in, "grad_to_chips_w_in_start")
    grad_x, sums1 = _in_proj_bwd_kernel(x2, dx1, dproj, vecs1 + scatter_in[-1][0:1, 0:1], w_in_t)
    cs_in, sc_in = _scatter_wait(scatter_in, sums1, "grad_to_chips_w_in_wait")
    total_in = _sum_chips_kernel(cs_in, sc_in, place, "grad_sum_w_in")
    gather_small = _gather_start([sums1, sums2], small_stage1, [total_in], cs_in, "gather_small_start")
    big_out = {}

    def update(nm, g, after=()):
        w, m, v = big[nm]
        outs = _adam_kernel(w, g, m, v, "adam_" + nm, after=after)
        big_out[nm] = tuple((t.T if nm == "w_in" else t)[None] for t in outs)
        return outs

    update("w_out", shared[0])
    covering = update("w_ff1", shared[1], after=gather_small[-1:])
    update("w_ff2", shared[2])
    *gathered, shared_in = _gather_wait(gather_small, 2, 1, covering[0], "gather_small_wait")
    update("w_in", shared_in)

    small = {"b_ada": (b_ada, m_b_ada, v_b_ada), "g_mix": (g_mix, m_g_mix, v_g_mix),
             "g_ffn": (g_ffn, m_g_ffn, v_g_ffn), "g_final": (g_final, m_g_final, v_g_final),
             "b_spatial": (b_spatial, m_b_spatial, v_b_spatial), "sinks": (sinks, m_sinks, v_sinks),
             "w_spatial": (w_spatial, m_w_spatial, v_w_spatial)}
    flat_shape = {"g_final": (1, D_MODEL), "b_spatial": (GMLP_GROUPS, CHUNK), "w_spatial": (GMLP_GROUPS * CHUNK, CHUNK)}
    small_out, loss_tile = _small_update_kernel(
        gathered, {nm: tuple(a.reshape(flat_shape.get(nm, a.shape)) for a in small[nm]) for nm in small})
    small_out = {nm: [o.reshape(small[nm][0].shape) for o in small_out[nm]] for nm in small}
    loss = loss_tile[0, 0]

    g1, g2 = gathered[0], gathered[1]
    dmod_all = jnp.concatenate([g1[:, 0], g1[:, 1], g2[:, 5], g2[:, 0], g2[:, 1], g2[:, 2]], axis=1)
    dmod_cols = lax.dynamic_slice(dmod_all, (0, chip * ada_cols), (N_DEV, ada_cols))
    ada = _ada_update_kernel(act.T, dmod_cols, w_ada[0], m_w_ada[0], v_w_ada[0])
    big_out["w_ada"] = tuple(t[None] for t in ada)

    order = ["w_ada", "b_ada", "g_mix", "w_in", "w_spatial", "b_spatial", "sinks", "w_out", "g_ffn",
             "w_ff1", "w_ff2", "g_final"]

    def leaf(nm, k):
        return big_out[nm][k] if nm in big_out else small_out[nm][k]

    outs = [loss, grad_x[None]]
    for k in range(4):
        outs += [leaf(nm, k) for nm in order]
    return tuple(outs)
```

```python
import math
from typing import Callable, NamedTuple

import jax
import jax.numpy as jnp
from jax import lax
from jax.experimental import pallas as pl
from jax.experimental.pallas import tpu as pltpu

F32 = jnp.float32
MXU_DTYPE = jnp.bfloat16
WEIGHT_COMM_DTYPE = jnp.bfloat16
GRAD_COMM_DTYPE = jnp.bfloat16

D_MODEL = 1024
D_FF = 4096
HEAD_DIM = 64
GMLP_GROUPS = 8
GMLP_WIDTH = 512
CHUNK = 128
N_Q_HEADS = 8
N_KV_HEADS = 2
ATTN_WIDTH = 512
KV_WIDTH = 128
ROT_DIM = 16
ROPE_THETA = 500000.0
IN_PROJ_WIDTH = 1792
N_MOD = 6
EPS = 1e-5
N_CHIPS = 4
N_DEV = 8
LANES = 128
W_IN_BLOCK = IN_PROJ_WIDTH // N_CHIPS

ADAM_LR = 0.001
ADAM_B1 = 0.9
ADAM_B2 = 0.999
ADAM_EPS = 1e-08
ADAM_WD = 0.01
ADAM_STEP = 10

VMEM_LIMIT_BYTES = 58 * 1024 * 1024
MESH = pl.DeviceIdType.MESH


def _params(*semantics):
    return pltpu.CompilerParams(dimension_semantics=semantics, vmem_limit_bytes=VMEM_LIMIT_BYTES)


def _dot(a, b):
    return jnp.dot(a.astype(MXU_DTYPE), b.astype(MXU_DTYPE), preferred_element_type=F32)


def _dot_nt(a, b):
    return lax.dot_general(a.astype(MXU_DTYPE), b.astype(MXU_DTYPE), (((1,), (1,)), ((), ())),
                           preferred_element_type=F32)


def _dot_tn(a, b):
    return lax.dot_general(a.astype(MXU_DTYPE), b.astype(MXU_DTYPE), (((0,), (0,)), ((), ())),
                           preferred_element_type=F32)


def _full(shape):
    return pl.BlockSpec(shape, lambda *_: (0,) * len(shape))


def _any():
    return pl.BlockSpec(memory_space=pl.ANY)


def _rowsum(v):
    return jnp.sum(v, axis=0, keepdims=True)


def _mean_last(v):
    return jnp.mean(v, axis=-1, keepdims=True)


class _Comm(NamedTuple):
    operands: tuple
    out_shapes: tuple
    n_sems: int
    make: Callable
    in_place: int = 0


def _hosted_call(body, comm, *, name, grid, in_specs, out_shape, out_specs, scratch_shapes=(), semantics,
                 n_prefetch=0):
    if comm is None:
        return pl.pallas_call(
            body, name=name, out_shape=out_shape, compiler_params=_params(*semantics),
            grid_spec=pltpu.PrefetchScalarGridSpec(
                num_scalar_prefetch=n_prefetch, grid=grid, in_specs=in_specs, out_specs=out_specs,
                scratch_shapes=list(scratch_shapes)))
    n_in, n_out, n_scr = len(in_specs), len(out_shape), len(scratch_shapes)
    k_in, k_out = len(comm.operands), len(comm.out_shapes)

    def hosted(*refs):
        prefetched, refs = refs[:n_prefetch], refs[n_prefetch:]
        ins, refs = refs[:n_in], refs[n_in:]
        c_ins, refs = refs[:k_in], refs[k_in:]
        outs, refs = refs[:n_out], refs[n_out:]
        c_outs, refs = refs[:k_out], refs[k_out:]
        scratch, (send_sems, recv_sems) = refs[:n_scr], refs[n_scr:]
        first, last = None, None
        for d, size in enumerate(grid):
            at_start, at_end = pl.program_id(d) == 0, pl.program_id(d) == size - 1
            first = at_start if first is None else first & at_start
            last = at_end if last is None else last & at_end

        @pl.when(first)
        def _():
            for cp in comm.make(c_ins, c_outs, send_sems, recv_sems)[0]:
                cp.start()

        body(*prefetched, *ins, *outs, *scratch)

        @pl.when(last)
        def _():
            for wait in comm.make(c_ins, c_outs, send_sems, recv_sems)[1]:
                wait()

    aliases = {n_prefetch + n_in + i: n_out + i for i in range(comm.in_place)}
    call = pl.pallas_call(
        hosted, name=name, out_shape=list(out_shape) + list(comm.out_shapes),
        compiler_params=_params(*semantics), input_output_aliases=aliases,
        grid_spec=pltpu.PrefetchScalarGridSpec(
            num_scalar_prefetch=n_prefetch, grid=grid, in_specs=list(in_specs) + [_any()] * k_in,
            out_specs=list(out_specs) + [_any()] * k_out,
            scratch_shapes=list(scratch_shapes) + [pltpu.SemaphoreType.DMA((comm.n_sems,)),
                                                    pltpu.SemaphoreType.DMA((comm.n_sems,))]))
    return lambda *args: call(*args, *comm.operands)


class _Shifted:
    def __init__(self, base, offset):
        self.base, self.offset = base, offset

    @property
    def at(self):
        return self

    def __getitem__(self, k):
        return self.base.at[self.offset + k]


def _merge_jobs(*jobs):
    def order(count):
        first = [(j, i) for j, job in enumerate(jobs) for i in range(job.in_place)]
        return first + [(j, i) for j, job in enumerate(jobs) for i in range(job.in_place, count(job))]

    op_order, out_order = order(lambda job: len(job.operands)), order(lambda job: len(job.out_shapes))

    def make(ins, outs, send_sems, recv_sems):
        starts, waits, sem = [], [], 0
        for j, job in enumerate(jobs):
            mine_in = [ins[k] for k, (jj, _) in enumerate(op_order) if jj == j]
            mine_out = [outs[k] for k, (jj, _) in enumerate(out_order) if jj == j]
            s, w = job.make(mine_in, mine_out, _Shifted(send_sems, sem), _Shifted(recv_sems, sem))
            starts, waits, sem = starts + s, waits + w, sem + job.n_sems
        return starts, waits

    return _Comm(tuple(jobs[j].operands[i] for j, i in op_order), tuple(jobs[j].out_shapes[i] for j, i in out_order),
                 sum(job.n_sems for job in jobs), make, in_place=sum(job.in_place for job in jobs))


def _mesh_place():
    x, y, c = lax.axis_index("x"), lax.axis_index("y"), lax.axis_index("c")
    return x, y, c, [(1 - x, y), (x, 1 - y), (1 - x, 1 - y)]


def _gather_job(bufs):
    per = 4

    def make(ins, outs, send_sems, recv_sems):
        del ins
        x, y, c, chips = _mesh_place()
        starts, waits = [], []
        for a, out in enumerate(outs):
            mine = src = out.at[4 * x + 2 * y + c]
            to = [(x, y, 1 - c)] + [(px, py, c) for px, py in chips]
            sends = [pltpu.make_async_remote_copy(
                src_ref=src, dst_ref=mine, send_sem=send_sems.at[per * a + k],
                recv_sem=recv_sems.at[per * a + k], device_id=dev, device_id_type=MESH)
                for k, dev in enumerate(to)]
            recvs = [pltpu.make_async_remote_copy(
                src_ref=src, dst_ref=out.at[4 * px + 2 * py + pc], send_sem=send_sems.at[per * a + k],
                recv_sem=recv_sems.at[per * a + k], device_id=(px, py, pc), device_id_type=MESH)
                for k, (px, py, pc) in enumerate(to)]
            starts += sends
            waits += [s.wait_send for s in sends] + [r.wait_recv for r in recvs]
        return starts, waits

    shapes = tuple(jax.ShapeDtypeStruct(b.shape, b.dtype) for b in bufs)
    return _Comm(tuple(bufs), shapes, per * len(bufs), make, in_place=len(bufs))


def _slots(x, y, c):
    return 4 * x + 2 * y + c, 4 * (1 - x) + 2 * y + c, 4 * x + 2 * (1 - y) + c, 4 * (1 - x) + 2 * (1 - y) + c


def _gather2d_first(halves):
    per = 2

    def make(ins, outs, send_sems, recv_sems):
        x, y, c, _ = _mesh_place()
        me, xn, yn, _ = _slots(x, y, c)
        starts, waits = [], []
        for a, (src, out) in enumerate(zip(ins, outs)):
            blk = src.at[c]
            rows = blk.shape[0] // 2
            upper, lower = pl.ds(0, rows), pl.ds(rows, rows)

            def copy(k, src_ref, dst_ref, dev, a=a):
                return pltpu.make_async_remote_copy(
                    src_ref=src_ref, dst_ref=dst_ref, send_sem=send_sems.at[per * a + k],
                    recv_sem=recv_sems.at[per * a + k], device_id=dev, device_id_type=MESH)

            sends = [copy(0, blk.at[upper], out.at[me, upper], (1 - x, y, c)),
                     copy(1, blk.at[lower], out.at[me, lower], (x, 1 - y, c))]
            recvs = [copy(0, blk.at[upper], out.at[xn, upper], (1 - x, y, c)),
                     copy(1, blk.at[lower], out.at[yn, lower], (x, 1 - y, c))]
            starts += sends
            waits += [s.wait_send for s in sends] + [r.wait_recv for r in recvs]
        return starts, waits

    shapes = tuple(jax.ShapeDtypeStruct((N_DEV,) + h.shape[1:], h.dtype) for h in halves)
    return _Comm(tuple(halves), shapes, per * len(halves), make)


def _gather2d_second(bufs, halves):
    per = 4
    n_arr = len(bufs)

    def make(ins, outs, send_sems, recv_sems):
        x, y, c, _ = _mesh_place()
        me, xn, yn, dg = _slots(x, y, c)
        starts, waits = [], []
        for a, buf in enumerate(outs):
            own = ins[n_arr + a].at[c]
            rows = buf.shape[1] // 2
            upper, lower = pl.ds(0, rows), pl.ds(rows, rows)
            plan = [(own.at[upper], me, upper, (x, 1 - y, c), yn), (buf.at[xn, upper], xn, upper, (x, 1 - y, c), dg),
                    (own.at[lower], me, lower, (1 - x, y, c), xn), (buf.at[yn, lower], yn, lower, (1 - x, y, c), dg)]
            for k, (src, slot, part, dev, landing) in enumerate(plan):
                sems = dict(send_sem=send_sems.at[per * a + k], recv_sem=recv_sems.at[per * a + k],
                            device_id=dev, device_id_type=MESH)
                send = pltpu.make_async_remote_copy(src_ref=src, dst_ref=buf.at[slot, part], **sems)
                arrival = pltpu.make_async_remote_copy(src_ref=src, dst_ref=buf.at[landing, part], **sems)
                starts.append(send)
                waits += [send.wait_send, arrival.wait_recv]
        return starts, waits

    shapes = tuple(jax.ShapeDtypeStruct(b.shape, b.dtype) for b in bufs)
    return _Comm(tuple(bufs) + tuple(halves), shapes, per * n_arr, make, in_place=n_arr)


def _gather_forward(bufs, name):
    n_arr = len(bufs)

    def body(*refs):
        outs = refs[n_arr:2 * n_arr]
        send_sems, recv_sems = refs[2 * n_arr:]
        x, y, c, chips = _mesh_place()
        sends, recvs = [], []
        for a, buf in enumerate(outs):
            for j, (px, py) in enumerate(chips):
                mine, theirs = buf.at[4 * px + 2 * py + c], buf.at[4 * px + 2 * py + 1 - c]
                sems = dict(send_sem=send_sems.at[3 * a + j], recv_sem=recv_sems.at[3 * a + j],
                            device_id=(x, y, 1 - c), device_id_type=MESH)
                sends.append(pltpu.make_async_remote_copy(src_ref=mine, dst_ref=mine, **sems))
                recvs.append(pltpu.make_async_remote_copy(src_ref=mine, dst_ref=theirs, **sems))
        for cp in sends:
            cp.start()
        for s, r in zip(sends, recvs):
            s.wait_send()
            r.wait_recv()

    return pl.pallas_call(
        body, name=name, out_shape=[jax.ShapeDtypeStruct(b.shape, b.dtype) for b in bufs],
        in_specs=[_any()] * n_arr, out_specs=[_any()] * n_arr,
        input_output_aliases={a: a for a in range(n_arr)},
        scratch_shapes=[pltpu.SemaphoreType.DMA((3 * n_arr,)), pltpu.SemaphoreType.DMA((3 * n_arr,))],
    )(*bufs)


def _scatter_job(chip_sums, rows=None, into=()):
    n_into = len(into)

    def part(ref):
        return ref if rows is None else ref.at[pl.ds(rows[0], rows[1])]

    def make(ins, outs, send_sems, recv_sems):
        x, y, c, chips = _mesh_place()
        copies = [pltpu.make_async_remote_copy(
            src_ref=part(src.at[2 * px + py]), dst_ref=part(out.at[j]), send_sem=send_sems.at[3 * a + j],
            recv_sem=recv_sems.at[3 * a + j], device_id=(px, py, c), device_id_type=MESH)
            for a, (src, out) in enumerate(zip(ins[n_into:], outs)) for j, (px, py) in enumerate(chips)]
        return copies, [cp.wait for cp in copies]

    shapes = tuple(jax.ShapeDtypeStruct((3,) + s.shape[1:], s.dtype) for s in chip_sums)
    return _Comm(tuple(into) + tuple(chip_sums), shapes, 3 * len(chip_sums), make, in_place=n_into)


def _all_gather8(blocks, name, split=False, forward=(), riders=(), skip_own=()):
    n_arr, n_fwd = len(blocks), len(forward)
    splits = list(split) if isinstance(split, (list, tuple)) else [split] * n_arr
    own_slots = [a not in skip_own for a in range(n_arr)]
    rider_in = sum(len(r.operands) for r in riders)
    rider_out = sum(len(r.out_shapes) for r in riders)

    def body(*refs):
        x_refs, refs = refs[:n_arr], refs[n_arr + n_fwd:]
        r_ins, refs = refs[:rider_in], refs[rider_in:]
        out_refs, refs = refs[:n_arr], refs[n_arr:]
        fwd_refs, refs = refs[:n_fwd], refs[n_fwd:]
        r_outs, refs = refs[:rider_out], refs[rider_out:]
        (send_sems, recv_sems, local_sems), rider_sems = refs[:3], refs[3:]
        x, y, c, chips = _mesh_place()
        me, sibling = (x, y, c), (x, y, 1 - c)
        passing = []
        for f, buf in enumerate(fwd_refs):
            for j, (px, py) in enumerate(chips):
                mine, theirs = buf.at[4 * px + 2 * py + c], buf.at[4 * px + 2 * py + 1 - c]
                sems = dict(send_sem=send_sems.at[7 * n_arr + 3 * f + j], recv_sem=recv_sems.at[7 * n_arr + 3 * f + j],
                            device_id=sibling, device_id_type=MESH)
                passing.append((pltpu.make_async_remote_copy(src_ref=mine, dst_ref=mine, **sems),
                                pltpu.make_async_remote_copy(src_ref=mine, dst_ref=theirs, **sems)))
        for send, _ in passing:
            send.start()
        arrays = []
        for a, (x_ref, out_ref) in enumerate(zip(x_refs, out_refs)):
            src_mine = x_ref.at[c] if splits[a] else x_ref

            def copy(k, blk, to, src=None, a=a, out_ref=out_ref):
                dst = out_ref.at[4 * blk[0] + 2 * blk[1] + blk[2]]
                return pltpu.make_async_remote_copy(
                    src_ref=dst if src is None else src, dst_ref=dst,
                    send_sem=send_sems.at[7 * a + k], recv_sem=recv_sems.at[7 * a + k],
                    device_id=to, device_id_type=MESH)

            mine = pltpu.make_async_copy(src_mine, out_ref.at[4 * x + 2 * y + c], local_sems.at[a])
            first = [copy(0, me, sibling, src=src_mine)] if own_slots[a] else []
            first += [copy(1 + j, me, (*chip, c), src=src_mine) for j, chip in enumerate(chips)]
            for cp in first + ([mine] if own_slots[a] else []):
                cp.start()
            arrays.append((copy, mine, first, own_slots[a]))
        rider_waits, i0, o0 = [], 0, 0
        for n, job in enumerate(riders):
            k_in, k_out = len(job.operands), len(job.out_shapes)
            starts, waits = job.make(r_ins[i0:i0 + k_in], r_outs[o0:o0 + k_out],
                                     rider_sems[2 * n], rider_sems[2 * n + 1])
            for cp in starts:
                cp.start()
            rider_waits += waits
            i0, o0 = i0 + k_in, o0 + k_out
        sent = []
        for copy, mine, first, own in arrays:
            passed = [copy(4 + j, (*chip, c), sibling) for j, chip in enumerate(chips)]
            for j, chip in enumerate(chips):
                copy(1 + j, (*chip, c), me).wait_recv()
                passed[j].start()
            sent += first + passed
        for copy, mine, first, own in arrays:
            if own:
                copy(0, sibling, me).wait_recv()
                mine.wait()
            for j, chip in enumerate(chips):
                copy(4 + j, (*chip, 1 - c), me).wait_recv()
        for cp in sent:
            cp.wait_send()
        for send, arrival in passing:
            send.wait_send()
            arrival.wait_recv()
        for wait in rider_waits:
            wait()

    n_sems = 7 * n_arr + 3 * n_fwd
    rider_operands = [a for r in riders for a in r.operands]
    rider_shapes = [s for r in riders for s in r.out_shapes]
    return pl.pallas_call(
        body, name=name,
        out_shape=[jax.ShapeDtypeStruct((N_DEV,) + tuple(b.shape[1:] if s else b.shape), b.dtype)
                   for b, s in zip(blocks, splits)]
        + [jax.ShapeDtypeStruct(f.shape, f.dtype) for f in forward] + rider_shapes,
        in_specs=[_any()] * (n_arr + n_fwd + rider_in), out_specs=[_any()] * (n_arr + n_fwd + rider_out),
        input_output_aliases={n_arr + f: n_arr + f for f in range(n_fwd)},
        scratch_shapes=[pltpu.SemaphoreType.DMA((n_sems,)), pltpu.SemaphoreType.DMA((n_sems,)),
                        pltpu.SemaphoreType.DMA((n_arr,))]
        + [pltpu.SemaphoreType.DMA((r.n_sems,)) for r in riders for _ in range(2)],
    )(*blocks, *forward, *rider_operands)


def _split_scatter_copies(src, land, send_sems, recv_sems):
    x, y, c, chips = _mesh_place()
    return [pltpu.make_async_remote_copy(
        src_ref=src.at[2 * px + py], dst_ref=land.at[j], send_sem=send_sems.at[j], recv_sem=recv_sems.at[j],
        device_id=(px, py, c), device_id_type=MESH) for j, (px, py) in enumerate(chips)]


def _scatter_start(chip_sums, name):
    hbm, sem = pl.BlockSpec(memory_space=pltpu.HBM), pl.BlockSpec(memory_space=pltpu.SEMAPHORE)

    def body(src, land, send_sems, recv_sems, src_thru, land_thru, token):
        del src_thru, land_thru
        for cp in _split_scatter_copies(src, land, send_sems, recv_sems):
            cp.start()
        token[...] = jnp.zeros_like(token)

    land = lax.empty((N_CHIPS - 1,) + chip_sums.shape[1:], chip_sums.dtype)
    operands = [pltpu.with_memory_space_constraint(a, pltpu.HBM) for a in (chip_sums, land)]
    return pl.pallas_call(
        body, name=name,
        out_shape=[pltpu.SemaphoreType.DMA((N_CHIPS - 1,)), pltpu.SemaphoreType.DMA((N_CHIPS - 1,))]
        + [pltpu.HBM(a.shape, a.dtype) for a in operands] + [jax.ShapeDtypeStruct((8, LANES), F32)],
        in_specs=[hbm, hbm], out_specs=[sem, sem, hbm, hbm, pl.BlockSpec(memory_space=pltpu.VMEM)],
        input_output_aliases={0: 2, 1: 3},
        compiler_params=pltpu.CompilerParams(has_side_effects=pltpu.SideEffectType.DATAFLOW_SIDE_EFFECTING),
    )(*operands)


def _scatter_wait(started, after, name):
    send_sems, recv_sems, src, land, _ = started
    hbm, sem = pl.BlockSpec(memory_space=pltpu.HBM), pl.BlockSpec(memory_space=pltpu.SEMAPHORE)

    def body(src, land, send_sems, recv_sems, after_ref, src_thru, land_thru):
        del after_ref, src_thru, land_thru
        for cp in _split_scatter_copies(src, land, send_sems, recv_sems):
            cp.wait()

    return pl.pallas_call(
        body, name=name, out_shape=[pltpu.HBM(src.shape, src.dtype), pltpu.HBM(land.shape, land.dtype)],
        in_specs=[hbm, hbm, sem, sem, _any()], out_specs=[hbm, hbm], input_output_aliases={0: 0, 1: 1},
        compiler_params=pltpu.CompilerParams(has_side_effects=pltpu.SideEffectType.DATAFLOW_SIDE_EFFECTING),
    )(src, land, send_sems, recv_sems, after)


def _split_gather_copies(srcs, lands, fwds, shares, send_sems, recv_sems):
    x, y, c, chips = _mesh_place()
    peers = [(x, y, 1 - c)] + [(px, py, pc) for px, py in chips for pc in (c, 1 - c)]
    pairs = []
    for a, (src, land) in enumerate(zip(srcs, lands)):
        for k, (px, py, pc) in enumerate(peers):
            sems = dict(send_sem=send_sems.at[7 * a + k], recv_sem=recv_sems.at[7 * a + k],
                        device_id=(px, py, pc), device_id_type=MESH)
            pairs.append((pltpu.make_async_remote_copy(src_ref=src, dst_ref=land.at[4 * x + 2 * y + c], **sems),
                          pltpu.make_async_remote_copy(src_ref=src, dst_ref=land.at[4 * px + 2 * py + pc], **sems)))
    for f, buf in enumerate(fwds):
        for j, (px, py) in enumerate(chips):
            mine, theirs = buf.at[4 * px + 2 * py + c], buf.at[4 * px + 2 * py + 1 - c]
            k = 7 * len(srcs) + 3 * f + j
            sems = dict(send_sem=send_sems.at[k], recv_sem=recv_sems.at[k],
                        device_id=(x, y, 1 - c), device_id_type=MESH)
            pairs.append((pltpu.make_async_remote_copy(src_ref=mine, dst_ref=mine, **sems),
                          pltpu.make_async_remote_copy(src_ref=mine, dst_ref=theirs, **sems)))
    for s, buf in enumerate(shares):
        k = 7 * len(srcs) + 3 * len(fwds) + s
        sems = dict(send_sem=send_sems.at[k], recv_sem=recv_sems.at[k], device_id=(x, y, 1 - c), device_id_type=MESH)
        pairs.append((pltpu.make_async_remote_copy(src_ref=buf.at[c], dst_ref=buf.at[c], **sems),
                      pltpu.make_async_remote_copy(src_ref=buf.at[c], dst_ref=buf.at[1 - c], **sems)))
    return pairs


def _gather_start(blocks, forward, shares, after, name):
    n, n_fwd = len(blocks), len(forward)
    n_sems = 7 * n + 3 * n_fwd + len(shares)
    n_bufs = 2 * n + n_fwd + len(shares)
    hbm, sem = pl.BlockSpec(memory_space=pltpu.HBM), pl.BlockSpec(memory_space=pltpu.SEMAPHORE)

    def body(*refs):
        srcs, lands, fwds, swaps = refs[:n], refs[n:2 * n], refs[2 * n:2 * n + n_fwd], refs[2 * n + n_fwd:n_bufs]
        send_sems, recv_sems = refs[n_bufs + 1:n_bufs + 3]
        token, local_sems = refs[-2:]
        x, y, c, _ = _mesh_place()
        own = [pltpu.make_async_copy(src, land.at[4 * x + 2 * y + c], local_sems.at[a])
               for a, (src, land) in enumerate(zip(srcs, lands))]
        for cp in own:
            cp.start()
        for send, _ in _split_gather_copies(srcs, lands, fwds, swaps, send_sems, recv_sems):
            send.start()
        token[...] = jnp.zeros_like(token)
        for cp in own:
            cp.wait()

    lands = [lax.empty((N_DEV,) + b.shape, b.dtype) for b in blocks]
    operands = [pltpu.with_memory_space_constraint(a, pltpu.HBM)
                for a in list(blocks) + lands + list(forward) + list(shares)]
    return pl.pallas_call(
        body, name=name,
        out_shape=[pltpu.SemaphoreType.DMA((n_sems,)), pltpu.SemaphoreType.DMA((n_sems,))]
        + [pltpu.HBM(a.shape, a.dtype) for a in operands] + [jax.ShapeDtypeStruct((8, LANES), F32)],
        in_specs=[hbm] * len(operands) + [_any()],
        out_specs=[sem, sem] + [hbm] * len(operands) + [pl.BlockSpec(memory_space=pltpu.VMEM)],
        input_output_aliases={i: 2 + i for i in range(len(operands))},
        scratch_shapes=[pltpu.SemaphoreType.DMA((n,))],
        compiler_params=pltpu.CompilerParams(has_side_effects=pltpu.SideEffectType.DATAFLOW_SIDE_EFFECTING),
    )(*operands, after)


def _gather_wait(started, n, n_shares, after, name):
    send_sems, recv_sems, *bufs, _ = started
    n_bufs = len(bufs)
    n_fwd = n_bufs - 2 * n - n_shares
    hbm, sem = pl.BlockSpec(memory_space=pltpu.HBM), pl.BlockSpec(memory_space=pltpu.SEMAPHORE)

    def body(*refs):
        srcs, lands, fwds, swaps = refs[:n], refs[n:2 * n], refs[2 * n:2 * n + n_fwd], refs[2 * n + n_fwd:n_bufs]
        send_sems, recv_sems = refs[n_bufs:n_bufs + 2]
        for send, arrival in _split_gather_copies(srcs, lands, fwds, swaps, send_sems, recv_sems):
            send.wait_send()
            arrival.wait_recv()

    out = pl.pallas_call(
        body, name=name, out_shape=[pltpu.HBM(b.shape, b.dtype) for b in bufs],
        in_specs=[hbm] * len(bufs) + [sem, sem, _any()], out_specs=[hbm] * len(bufs),
        input_output_aliases={i: i for i in range(len(bufs))},
        compiler_params=pltpu.CompilerParams(has_side_effects=pltpu.SideEffectType.DATAFLOW_SIDE_EFFECTING),
    )(*bufs, send_sems, recv_sems, after)
    return out[n:]


def _share_job(bufs):
    def make(ins, outs, send_sems, recv_sems):
        del ins
        x, y, c, _ = _mesh_place()
        sems = lambda a: dict(send_sem=send_sems.at[a], recv_sem=recv_sems.at[a],
                              device_id=(x, y, 1 - c), device_id_type=MESH)
        sends = [pltpu.make_async_remote_copy(src_ref=o.at[c], dst_ref=o.at[c], **sems(a)) for a, o in enumerate(outs)]
        arrivals = [pltpu.make_async_remote_copy(src_ref=o.at[c], dst_ref=o.at[1 - c], **sems(a))
                    for a, o in enumerate(outs)]
        return sends, [s.wait_send for s in sends] + [r.wait_recv for r in arrivals]

    shapes = tuple(jax.ShapeDtypeStruct(b.shape, b.dtype) for b in bufs)
    return _Comm(tuple(bufs), shapes, len(bufs), make, in_place=len(bufs))


def _gelu_tanh(z):
    k = math.sqrt(2.0 / math.pi)
    t = jnp.tanh(k * (z + 0.044715 * (z * z * z)))
    return 0.5 * z * (1.0 + t), t


def _gelu_tanh_grad(z, t):
    k = math.sqrt(2.0 / math.pi)
    return 0.5 * (1.0 + t) + 0.5 * z * (1.0 - t * t) * (k * (1.0 + 3.0 * 0.044715 * (z * z)))


def _rope_angle_kernel(pos_row, invf_col):
    seq = pos_row.shape[1]

    def body(p_ref, f_ref, cos_ref, sin_ref):
        ang = p_ref[...].astype(F32) * f_ref[...]
        cos_ref[...] = jnp.cos(ang)
        sin_ref[...] = jnp.sin(ang)

    return pl.pallas_call(
        body, name="rope_angles", grid=(1,), out_shape=[jax.ShapeDtypeStruct((ROT_DIM // 2, seq), F32)] * 2,
        in_specs=[_full((1, seq)), _full((ROT_DIM // 2, 1))], out_specs=[_full((ROT_DIM // 2, seq))] * 2,
        compiler_params=_params("arbitrary"),
    )(pos_row, invf_col)


def _rope_lane_tables(cos, sin):
    cos_t, sin_t = cos.T, sin.T
    seq, half = cos_t.shape
    ones = jnp.ones((seq, HEAD_DIM - ROT_DIM), F32)
    c64 = jnp.concatenate([cos_t, cos_t, ones], axis=1)
    s1 = jnp.concatenate([sin_t, jnp.zeros((seq, HEAD_DIM - half), F32)], axis=1)
    s2 = jnp.concatenate([jnp.zeros((seq, half), F32), sin_t, jnp.zeros((seq, HEAD_DIM - ROT_DIM), F32)], axis=1)
    return jnp.concatenate([jnp.tile(t, (1, LANES // HEAD_DIM)) for t in (c64, s1, s2)], axis=1)


def _rope_apply(t, tab, sign):
    reps = t.shape[1] // LANES
    c_tab, s1, s2 = (jnp.tile(tab[:, LANES * k:LANES * (k + 1)], (1, reps)) if reps > 1
                     else tab[:, LANES * k:LANES * (k + 1)] for k in range(3))
    half = ROT_DIM // 2
    up = pltpu.roll(t, t.shape[1] - half, 1)
    down = pltpu.roll(t, half, 1)
    return t * c_tab + sign * (down * s2 - up * s1)


def _lane_masks(shape):
    lane = lax.broadcasted_iota(jnp.int32, shape, 1)
    return lane < HEAD_DIM, lane >= HEAD_DIM


HEADS_PER_GROUP = N_Q_HEADS // N_KV_HEADS
ATTN_SCALE = 1.0 / math.sqrt(HEAD_DIM)


def _attn_bias_t(first_block):
    kj = lax.broadcasted_iota(jnp.int32, (2 * CHUNK, CHUNK), 0)
    qi = lax.broadcasted_iota(jnp.int32, (2 * CHUNK, CHUNK), 1)
    ok = (kj > qi) & (kj <= qi + CHUNK)
    if first_block is not None:
        ok = ok & (jnp.logical_not(first_block) | (kj >= CHUNK))
    return jnp.tile(jnp.where(ok, 0.0, -jnp.inf), (1, HEADS_PER_GROUP))


def _group_rows(x, g, lo, hi):
    rows = []
    for r in range(HEADS_PER_GROUP):
        h = HEADS_PER_GROUP * g + r
        pair = x[:, LANES * (h // 2):LANES * (h // 2 + 1)]
        rows.append(jnp.where(hi if h % 2 else lo, pair, 0.0))
    return jnp.concatenate(rows, axis=0)


def _pairs_from_rows(rows, lo):
    return [jnp.where(lo, rows[2 * CHUNK * k:2 * CHUNK * k + CHUNK], rows[2 * CHUNK * k + CHUNK:2 * CHUNK * (k + 1)])
            for k in range(HEADS_PER_GROUP // 2)]


def _group_dup(a, b, g, lo2):
    return jnp.where(lo2, a, b) if g == 0 else jnp.where(lo2, b, a)


def _sink_row(sink_ref, g):
    return jnp.concatenate([sink_ref[HEADS_PER_GROUP * g + r:HEADS_PER_GROUP * g + r + 1, :]
                            for r in range(HEADS_PER_GROUP)], axis=1)


def _attn_probs_t(k_dup, q_rows, bias_t, sink_row):
    s_t = _dot_nt(k_dup, q_rows) * ATTN_SCALE + bias_t
    m = jnp.maximum(jnp.max(s_t, axis=0, keepdims=True), sink_row)
    p = jnp.exp(s_t - m)
    e_sink = jnp.exp(sink_row - m)
    inv = 1.0 / (jnp.sum(p, axis=0, keepdims=True) + e_sink)
    return p * inv, e_sink * inv


def _sgu_forward_pair(wm, vp, j):
    lo, hi = _lane_masks(vp.shape)
    lhs = jnp.concatenate([wm[2 * j], wm[2 * j + 1]], axis=1)
    rhs = jnp.concatenate([jnp.where(lo, vp, 0.0), jnp.where(hi, vp, 0.0)], axis=0)
    return _dot(lhs, rhs)


def _masked_spatial(w_ref):
    t = lax.broadcasted_iota(jnp.int32, (CHUNK, CHUNK), 0)
    s = lax.broadcasted_iota(jnp.int32, (CHUNK, CHUNK), 1)
    tril = s <= t
    return [jnp.where(tril, w_ref[g], 0.0) for g in range(GMLP_GROUPS)], tril, s >= t


def _mod_kernel(c_all, w_shard, b_shard, comm=None):
    n = w_shard.shape[1]
    tn = 512

    def body(c_ref, w_ref, b_ref, mod_ref, act_ref):
        cv = c_ref[...]
        act = cv * (1.0 / (1.0 + jnp.exp(-cv)))
        act_ref[...] = act
        mod_ref[...] = _dot(act, w_ref[...]) + b_ref[...]

    return _hosted_call(
        body, comm, name="ada_mod", grid=(n // tn,),
        out_shape=[jax.ShapeDtypeStruct((N_DEV, n), F32), jax.ShapeDtypeStruct((N_DEV, D_MODEL), F32)],
        in_specs=[_full((N_DEV, D_MODEL)), pl.BlockSpec((D_MODEL, tn), lambda i: (0, i)),
                  pl.BlockSpec((1, tn), lambda i: (0, i))],
        out_specs=[pl.BlockSpec((N_DEV, tn), lambda i: (0, i)), _full((N_DEV, D_MODEL))],
        semantics=("arbitrary",),
    )(c_all, w_shard, b_shard)


def _load_chip_blocks(chip_ref, gathered, local, dsts, sems, first_sem=0):
    for k, dst in enumerate(dsts):
        @pl.when(chip_ref[0] == k)
        def _():
            pltpu.make_async_copy(local, dst, sems.at[first_sem + k]).start()

        @pl.when(chip_ref[0] != k)
        def _():
            pltpu.make_async_copy(gathered.at[k], dst, sems.at[first_sem + k]).start()
    return [pltpu.make_async_copy(local, dst, sems.at[first_sem + k]).wait for k, dst in enumerate(dsts)]


def _in_proj_kernel(x, vecs, w_in_t, comm=None):
    seq = x.shape[0]
    tm = 512

    def body(x_ref, v_ref, w_ref, proj_ref, h_ref):
        xv = x_ref[...]
        rstd = lax.rsqrt(_mean_last(xv * xv) + EPS)
        n1 = (xv * rstd) * v_ref[0:1, :]
        h = n1 * (1.0 + v_ref[2:3, :]) + v_ref[1:2, :]
        hb = h.astype(MXU_DTYPE)
        h_ref[...] = hb
        proj_ref[...] = _dot_nt(hb, w_ref[...])

    return _hosted_call(
        body, comm, name="in_proj", grid=(seq // tm,),
        out_shape=[jax.ShapeDtypeStruct((seq, IN_PROJ_WIDTH), F32),
                   jax.ShapeDtypeStruct((seq, D_MODEL), MXU_DTYPE)],
        in_specs=[pl.BlockSpec((tm, D_MODEL), lambda i: (i, 0)), _full((8, D_MODEL)),
                  _full((IN_PROJ_WIDTH, D_MODEL))],
        out_specs=[pl.BlockSpec((tm, IN_PROJ_WIDTH), lambda i: (i, 0)),
                   pl.BlockSpec((tm, D_MODEL), lambda i: (i, 0))],
        semantics=("arbitrary",),
    )(x, vecs, w_in_t)


MIXER_BLOCKS_PER_STEP = 4
KV_START = 2 * GMLP_WIDTH + ATTN_WIDTH


def _mixer_fwd_kernel(proj, rope_tab, w_spatial, bias_full, sink_rows, comm=None):
    seq = proj.shape[0]
    per = MIXER_BLOCKS_PER_STEP
    steps = seq // (CHUNK * per)
    kv_col = KV_START // (2 * KV_WIDTH)

    def body(proj_ref, prev_ref, tab_ref, ptab_ref, w_ref, bias_ref, sink_ref, cat_ref):
        i = pl.program_id(0)
        wm, _, _ = _masked_spatial(w_ref)
        lo, hi = _lane_masks((CHUNK, LANES))
        lo2, _ = _lane_masks((2 * CHUNK, LANES))
        o = 2 * GMLP_WIDTH
        for s in range(per):
            rows, before = slice(CHUNK * s, CHUNK * (s + 1)), slice(CHUNK * (s - 1), CHUNK * s)
            for j in range(GMLP_GROUPS // 2):
                cols = slice(LANES * j, LANES * (j + 1))
                vcols = slice(GMLP_WIDTH + LANES * j, GMLP_WIDTH + LANES * (j + 1))
                u, _ = _gelu_tanh(proj_ref[rows, cols])
                vp, _ = _gelu_tanh(proj_ref[rows, vcols])
                sv = _sgu_forward_pair(wm, vp, j) + bias_ref[:, cols]
                cat_ref[rows, cols] = (u * sv).astype(cat_ref.dtype)
            tab = tab_ref[rows, :]
            if s == 0:
                prev_kv, prev_tab, first = prev_ref[...], ptab_ref[...], i == 0
            else:
                prev_kv, prev_tab, first = proj_ref[before, KV_START:KV_START + 2 * KV_WIDTH], tab_ref[before, :], None
            q_r = _rope_apply(proj_ref[rows, o:o + ATTN_WIDTH], tab, 1.0)
            k_cur = _rope_apply(proj_ref[rows, KV_START:KV_START + KV_WIDTH], tab, 1.0)
            k_prev = _rope_apply(prev_kv[:, 0:KV_WIDTH], prev_tab, 1.0)
            k_a = jnp.concatenate([k_prev, k_cur], axis=0)
            v_a = jnp.concatenate([prev_kv[:, KV_WIDTH:2 * KV_WIDTH],
                                   proj_ref[rows, KV_START + KV_WIDTH:KV_START + 2 * KV_WIDTH]], axis=0)
            k_b = pltpu.roll(k_a, HEAD_DIM, 1)
            v_b = pltpu.roll(v_a, HEAD_DIM, 1)
            bias_t = _attn_bias_t(first)
            for g in range(N_KV_HEADS):
                p_t, _ = _attn_probs_t(_group_dup(k_a, k_b, g, lo2), _group_rows(q_r, g, lo, hi), bias_t,
                                       _sink_row(sink_ref, g))
                o_t = _dot(_group_dup(v_a, v_b, g, lo2).T, p_t)
                for k, pair in enumerate(_pairs_from_rows(o_t.T, lo)):
                    c0 = GMLP_WIDTH + LANES * (2 * g + k)
                    cat_ref[rows, c0:c0 + LANES] = pair.astype(cat_ref.dtype)

    return _hosted_call(
        body, comm, name="mixer_fwd", grid=(steps,),
        out_shape=[jax.ShapeDtypeStruct((seq, D_MODEL), MXU_DTYPE)],
        in_specs=[pl.BlockSpec((CHUNK * per, IN_PROJ_WIDTH), lambda i: (i, 0)),
                  pl.BlockSpec((CHUNK, 2 * KV_WIDTH), lambda i: (jnp.maximum(per * i - 1, 0), kv_col)),
                  pl.BlockSpec((CHUNK * per, 3 * LANES), lambda i: (i, 0)),
                  pl.BlockSpec((CHUNK, 3 * LANES), lambda i: (jnp.maximum(per * i - 1, 0), 0)),
                  _full((GMLP_GROUPS, CHUNK, CHUNK)), _full((CHUNK, GMLP_WIDTH)),
                  _full((N_Q_HEADS, LANES))],
        out_specs=[pl.BlockSpec((CHUNK * per, D_MODEL), lambda i: (i, 0))],
        semantics=("arbitrary",),
    )(proj, proj, rope_tab, rope_tab, w_spatial, bias_full, sink_rows)


def _trunk_kernel(x, target, cat, vecs, chip_idx, gathered, local):
    seq = x.shape[0]
    tm = 256
    nj = D_FF // D_MODEL
    out_rows = D_MODEL // N_CHIPS

    def body(chip_ref, x_ref, t_ref, cat_ref, v_ref, g_out, g_w1, g_w2, l_out, l_w1, l_w2,
             dx1_ref, dcat_ref, dmix_ref, h2_ref, r_ref, da_ref, dff_ref, sums_ref,
             wout, w1, w2, a_scr, sem):
        i = pl.program_id(0)

        @pl.when(i == 0)
        def _():
            waits = _load_chip_blocks(chip_ref, g_out, l_out,
                                      [wout.at[pl.ds(out_rows * k, out_rows)] for k in range(N_CHIPS)], sem)
            waits += _load_chip_blocks(chip_ref, g_w1, l_w1, [w1.at[k] for k in range(N_CHIPS)], sem, N_CHIPS)
            waits += _load_chip_blocks(chip_ref, g_w2, l_w2, [w2.at[k] for k in range(N_CHIPS)], sem, 2 * N_CHIPS)
            for wait in waits:
                wait()
            sums_ref[...] = jnp.zeros_like(sums_ref)

        gate1, shift2, scale2 = v_ref[0:1, :], v_ref[1:2, :], v_ref[2:3, :]
        gate2, g_ffn, g_final = v_ref[3:4, :], v_ref[4:5, :], v_ref[5:6, :]

        mix = _dot(cat_ref[...], wout[...])
        x1 = x_ref[...] + gate1 * mix
        rstd2 = lax.rsqrt(_mean_last(x1 * x1) + EPS)
        xh2 = x1 * rstd2
        n2 = xh2 * g_ffn
        h2b = (n2 * (1.0 + scale2) + shift2).astype(MXU_DTYPE)
        h2_ref[...] = h2b
        ff = jnp.zeros((tm, D_MODEL), F32)
        for j in range(nj):
            a = _dot(h2b, w1[j])
            a_scr[j] = a
            relu = jnp.maximum(a, 0.0)
            rb = (relu * relu).astype(MXU_DTYPE)
            r_ref[:, D_MODEL * j:D_MODEL * (j + 1)] = rb
            ff = ff + _dot(rb, w2[j])
        x2 = x1 + gate2 * ff
        rstd3 = lax.rsqrt(_mean_last(x2 * x2) + EPS)
        xh3 = x2 * rstd3
        err = xh3 * g_final - t_ref[...]
        loss = 0.5 * _rowsum(_mean_last(err * err))
        dy = err * (1.0 / D_MODEL)
        dxh3 = dy * g_final
        dx2 = rstd3 * (dxh3 - xh3 * _mean_last(dxh3 * xh3))
        dffb = (dx2 * gate2).astype(MXU_DTYPE)
        dff_ref[...] = dffb
        dh2 = jnp.zeros((tm, D_MODEL), F32)
        for j in range(nj):
            dr = _dot_nt(dffb, w2[j])
            dab = (dr * (2.0 * jnp.maximum(a_scr[j], 0.0))).astype(MXU_DTYPE)
            da_ref[:, D_MODEL * j:D_MODEL * (j + 1)] = dab
            dh2 = dh2 + _dot_nt(dab, w1[j])
        dn2 = dh2 * (1.0 + scale2)
        dxh2 = dn2 * g_ffn
        dx1 = dx2 + rstd2 * (dxh2 - xh2 * _mean_last(dxh2 * xh2))
        dx1_ref[...] = dx1
        dmixb = (dx1 * gate1).astype(MXU_DTYPE)
        dmix_ref[...] = dmixb
        dcat_ref[...] = _dot_nt(dmixb, wout[...])

        sums_ref[0:1, :] += _rowsum(dh2)
        sums_ref[1:2, :] += _rowsum(dh2 * n2)
        sums_ref[2:3, :] += _rowsum(dx2 * ff)
        sums_ref[3:4, :] += _rowsum(dn2 * xh2)
        sums_ref[4:5, :] += _rowsum(dy * xh3)
        sums_ref[5:6, :] += _rowsum(dx1 * mix)
        sums_ref[6:7, :] += jnp.broadcast_to(loss, (1, D_MODEL))

    tok = lambda w: pl.BlockSpec((tm, w), lambda i, chip: (i, 0))
    return _hosted_call(
        body, None, name="trunk", grid=(seq // tm,), n_prefetch=1,
        out_shape=[jax.ShapeDtypeStruct((seq, D_MODEL), F32), jax.ShapeDtypeStruct((seq, D_MODEL), F32),
                   jax.ShapeDtypeStruct((seq, D_MODEL), MXU_DTYPE), jax.ShapeDtypeStruct((seq, D_MODEL), MXU_DTYPE),
                   jax.ShapeDtypeStruct((seq, D_FF), MXU_DTYPE), jax.ShapeDtypeStruct((seq, D_FF), MXU_DTYPE),
                   jax.ShapeDtypeStruct((seq, D_MODEL), MXU_DTYPE), jax.ShapeDtypeStruct((8, D_MODEL), F32)],
        in_specs=[tok(D_MODEL), tok(D_MODEL), tok(D_MODEL), _full((8, D_MODEL))] + [_any()] * 6,
        out_specs=[tok(D_MODEL), tok(D_MODEL), tok(D_MODEL), tok(D_MODEL), tok(D_FF), tok(D_FF), tok(D_MODEL),
                   _full((8, D_MODEL))],
        scratch_shapes=[pltpu.VMEM((D_MODEL, D_MODEL), MXU_DTYPE), pltpu.VMEM((nj, D_MODEL, D_MODEL), MXU_DTYPE),
                        pltpu.VMEM((nj, D_MODEL, D_MODEL), MXU_DTYPE), pltpu.VMEM((nj, tm, D_MODEL), F32),
                        pltpu.SemaphoreType.DMA((3 * N_CHIPS,))],
        semantics=("arbitrary",),
    )(chip_idx, x, target, cat, vecs, *gathered, *local)


def _mixer_bwd_kernel(proj, rope_tab, dcat, w_spatial, w_spatial_t, bias_full, sink_rows, dev_idx, comm=None):
    seq = proj.shape[0]
    per = MIXER_BLOCKS_PER_STEP
    steps = seq // (CHUNK * per)
    kv_col = KV_START // (2 * KV_WIDTH)

    def body(dev_ref, proj_ref, prev_ref, tab_ref, ptab_ref, dcat_ref, w_ref, wt_ref, bias_ref, sink_ref,
             dproj_ref, dw_out, db_ref, dsink_ref, carry, dw_ref):
        del dev_ref
        step = pl.program_id(0)

        @pl.when(step == 0)
        def _():
            carry[...] = jnp.zeros_like(carry)
            dw_ref[...] = jnp.zeros_like(dw_ref)
            db_ref[...] = jnp.zeros_like(db_ref)
            dsink_ref[...] = jnp.zeros_like(dsink_ref)

        for s in reversed(range(per)):
            rows = pl.ds(CHUNK * s, CHUNK)
            if s == 0:
                before, before_tab, first = prev_ref, ptab_ref, step == steps - 1
            else:
                before = proj_ref.at[pl.ds(CHUNK * (s - 1), CHUNK), pl.ds(KV_START, 2 * KV_WIDTH)]
                before_tab, first = tab_ref.at[pl.ds(CHUNK * (s - 1), CHUNK)], None
            one_block(proj_ref.at[rows], before, tab_ref.at[rows], before_tab, dcat_ref.at[rows], w_ref, wt_ref,
                      bias_ref, sink_ref, dproj_ref.at[rows], dw_ref, db_ref, dsink_ref, carry, first)

        @pl.when(step == steps - 1)
        def _():
            dw_out[...] = dw_ref[...].astype(dw_out.dtype)

    def one_block(proj_ref, prev_ref, tab_ref, ptab_ref, dcat_ref, w_ref, wt_ref, bias_ref, sink_ref,
                  dproj_ref, dw_ref, db_ref, dsink_ref, carry, first):
        wm, tril, triu = _masked_spatial(w_ref)
        lo, hi = _lane_masks((CHUNK, LANES))
        lane = lax.broadcasted_iota(jnp.int32, (CHUNK, LANES), 1)
        db = jnp.zeros((CHUNK, LANES), F32)
        for j in range(GMLP_GROUPS // 2):
            cols = slice(LANES * j, LANES * (j + 1))
            vcols = slice(GMLP_WIDTH + LANES * j, GMLP_WIDTH + LANES * (j + 1))
            zu, zv = proj_ref[:, cols], proj_ref[:, vcols]
            u, tu = _gelu_tanh(zu)
            vp, tv = _gelu_tanh(zv)
            sv = _sgu_forward_pair(wm, vp, j) + bias_ref[:, cols]
            dout = dcat_ref[:, cols]
            du = dout * sv
            dsv = dout * u
            dsv_lo, dsv_hi = jnp.where(lo, dsv, 0.0), jnp.where(hi, dsv, 0.0)
            lhs_t = jnp.concatenate([jnp.where(triu, wt_ref[2 * j], 0.0),
                                     jnp.where(triu, wt_ref[2 * j + 1], 0.0)], axis=1)
            dv = _dot(lhs_t, jnp.concatenate([dsv_lo, dsv_hi], axis=0))
            dw_ref[2 * j] += jnp.where(tril, _dot_nt(dsv_lo, vp), 0.0)
            dw_ref[2 * j + 1] += jnp.where(tril, _dot_nt(dsv_hi, vp), 0.0)
            db = db + (jnp.where(lane == 2 * j, jnp.sum(dsv_lo, axis=1, keepdims=True), 0.0)
                       + jnp.where(lane == 2 * j + 1, jnp.sum(dsv_hi, axis=1, keepdims=True), 0.0))
            dproj_ref[:, cols] = (du * _gelu_tanh_grad(zu, tu)).astype(dproj_ref.dtype)
            dproj_ref[:, vcols] = (dv * _gelu_tanh_grad(zv, tv)).astype(dproj_ref.dtype)
        db_ref[...] += db
        o = 2 * GMLP_WIDTH
        tab = tab_ref[...]
        q_r = _rope_apply(proj_ref[:, o:o + ATTN_WIDTH], tab, 1.0)
        k_cur = _rope_apply(proj_ref[:, o + ATTN_WIDTH:o + ATTN_WIDTH + KV_WIDTH], tab, 1.0)
        k_prev = _rope_apply(prev_ref[:, 0:KV_WIDTH], ptab_ref[...], 1.0)
        k_a = jnp.concatenate([k_prev, k_cur], axis=0)
        v_a = jnp.concatenate([prev_ref[:, KV_WIDTH:2 * KV_WIDTH],
                               proj_ref[:, o + ATTN_WIDTH + KV_WIDTH:o + ATTN_WIDTH + 2 * KV_WIDTH]], axis=0)
        k_b = pltpu.roll(k_a, HEAD_DIM, 1)
        v_b = pltpu.roll(v_a, HEAD_DIM, 1)
        bias_t = _attn_bias_t(first)
        lo2, _ = _lane_masks((2 * CHUNK, LANES))
        dout_b = dcat_ref[:, GMLP_WIDTH:GMLP_WIDTH + ATTN_WIDTH]
        dk_tot, dv_tot, dq_pairs = [], [], []
        for g in range(N_KV_HEADS):
            k_dup, v_dup = _group_dup(k_a, k_b, g, lo2), _group_dup(v_a, v_b, g, lo2)
            q_rows = _group_rows(q_r, g, lo, hi)
            do_rows = _group_rows(dout_b, g, lo, hi)
            p_t, p_sink = _attn_probs_t(k_dup, q_rows, bias_t, _sink_row(sink_ref, g))
            dp_t = _dot_nt(v_dup, do_rows)
            delta = jnp.sum(p_t * dp_t, axis=0, keepdims=True)
            ds_t = p_t * (dp_t - delta) * ATTN_SCALE
            dsink = -p_sink * delta
            for r in range(HEADS_PER_GROUP):
                h = HEADS_PER_GROUP * g + r
                dsink_ref[h:h + 1, :] += jnp.broadcast_to(
                    jnp.sum(dsink[:, LANES * r:LANES * (r + 1)], axis=1, keepdims=True), (1, LANES))
            dk_full = _dot(ds_t, q_rows)
            dv_full = _dot(p_t, do_rows)
            dk_tot.append(dk_full + pltpu.roll(dk_full, HEAD_DIM, 1))
            dv_tot.append(dv_full + pltpu.roll(dv_full, HEAD_DIM, 1))
            dq_t = _dot(k_dup.T, ds_t)
            dq_pairs += _pairs_from_rows(dq_t.T, lo)
        dk_all = jnp.where(lo2, dk_tot[0], dk_tot[1])
        dv_all = jnp.where(lo2, dv_tot[0], dv_tot[1])
        dk_cur = dk_all[CHUNK:, :] + carry[:, 0:KV_WIDTH]
        dv_cur = dv_all[CHUNK:, :] + carry[:, KV_WIDTH:2 * KV_WIDTH]
        carry[:, 0:KV_WIDTH] = dk_all[:CHUNK, :]
        carry[:, KV_WIDTH:2 * KV_WIDTH] = dv_all[:CHUNK, :]
        dq = _rope_apply(jnp.concatenate(dq_pairs, axis=1), tab, -1.0)
        dproj_ref[:, o:o + ATTN_WIDTH] = dq.astype(dproj_ref.dtype)
        dproj_ref[:, o + ATTN_WIDTH:o + ATTN_WIDTH + KV_WIDTH] = (
            _rope_apply(dk_cur, tab, -1.0).astype(dproj_ref.dtype))
        dproj_ref[:, o + ATTN_WIDTH + KV_WIDTH:o + ATTN_WIDTH + 2 * KV_WIDTH] = dv_cur.astype(dproj_ref.dtype)

    rev = lambda i: steps - 1 - i
    before = lambda i: jnp.maximum(per * rev(i) - 1, 0)
    slot = lambda shape: pl.BlockSpec((None,) + shape, lambda i, d: (d[0],) + (0,) * len(shape))
    return _hosted_call(
        body, comm, name="mixer_bwd", grid=(steps,), n_prefetch=1,
        out_shape=[jax.ShapeDtypeStruct((seq, IN_PROJ_WIDTH), MXU_DTYPE),
                   jax.ShapeDtypeStruct((N_DEV, GMLP_GROUPS, CHUNK, CHUNK), GRAD_COMM_DTYPE),
                   jax.ShapeDtypeStruct((N_DEV, CHUNK, LANES), F32),
                   jax.ShapeDtypeStruct((N_DEV, N_Q_HEADS, LANES), F32)],
        in_specs=[pl.BlockSpec((CHUNK * per, IN_PROJ_WIDTH), lambda i, d: (rev(i), 0)),
                  pl.BlockSpec((CHUNK, 2 * KV_WIDTH), lambda i, d: (before(i), kv_col)),
                  pl.BlockSpec((CHUNK * per, 3 * LANES), lambda i, d: (rev(i), 0)),
                  pl.BlockSpec((CHUNK, 3 * LANES), lambda i, d: (before(i), 0)),
                  pl.BlockSpec((CHUNK * per, D_MODEL), lambda i, d: (rev(i), 0)),
                  _full((GMLP_GROUPS, CHUNK, CHUNK)), _full((GMLP_GROUPS, CHUNK, CHUNK)),
                  _full((CHUNK, GMLP_WIDTH)), _full((N_Q_HEADS, LANES))],
        out_specs=[pl.BlockSpec((CHUNK * per, IN_PROJ_WIDTH), lambda i, d: (rev(i), 0)),
                   slot((GMLP_GROUPS, CHUNK, CHUNK)), slot((CHUNK, LANES)), slot((N_Q_HEADS, LANES))],
        scratch_shapes=[pltpu.VMEM((CHUNK, 2 * KV_WIDTH), F32), pltpu.VMEM((GMLP_GROUPS, CHUNK, CHUNK), F32)],
        semantics=("arbitrary",),
    )(dev_idx, proj, proj, rope_tab, rope_tab, dcat, w_spatial, w_spatial_t, bias_full, sink_rows)


def _in_proj_bwd_kernel(x, dx1, dproj, vecs, w_in_t, comm=None):
    seq = x.shape[0]
    tm = 512

    def body(x_ref, dx1_ref, dp_ref, v_ref, w_ref, gx_ref, sums_ref):
        @pl.when(pl.program_id(0) == 0)
        def _():
            sums_ref[...] = jnp.zeros_like(sums_ref)

        g_mix, scale1 = v_ref[0:1, :], v_ref[2:3, :]
        dh = _dot(dp_ref[...], w_ref[...])
        xv = x_ref[...]
        rstd = lax.rsqrt(_mean_last(xv * xv) + EPS)
        xh = xv * rstd
        dn1 = dh * (1.0 + scale1)
        dxh = dn1 * g_mix
        gx_ref[...] = dx1_ref[...] + rstd * (dxh - xh * _mean_last(dxh * xh))
        sums_ref[0:1, :] += _rowsum(dh)
        sums_ref[1:2, :] += _rowsum(dh * (xh * g_mix))
        sums_ref[2:3, :] += _rowsum(dn1 * xh)

    tok = lambda w: pl.BlockSpec((tm, w), lambda i: (i, 0))
    return _hosted_call(
        body, comm, name="in_proj_bwd", grid=(seq // tm,),
        out_shape=[jax.ShapeDtypeStruct((seq, D_MODEL), F32), jax.ShapeDtypeStruct((8, D_MODEL), F32)],
        in_specs=[tok(D_MODEL), tok(D_MODEL), tok(IN_PROJ_WIDTH), _full((8, D_MODEL)),
                  _full((IN_PROJ_WIDTH, D_MODEL))],
        out_specs=[tok(D_MODEL), _full((8, D_MODEL))],
        semantics=("arbitrary",),
    )(x, dx1, dproj, vecs, w_in_t)


class _GradTiles(NamedTuple):
    tm: int
    tn: int
    n_tiles: int
    chips_per_tile: int
    a_index: Callable
    b_index: Callable


def _weight_grad_kernel(a, b, c_idx, name, tiles, comm=None):
    seq = a.shape[0]
    tk = min(seq, 4096)
    nk = seq // tk
    tm, tn, n_tiles, per = tiles.tm, tiles.tn, tiles.n_tiles, tiles.chips_per_tile
    rows = tm // per

    def half(phase, c):
        return phase * c[0] + (1 - phase) * (1 - c[0])

    def body(c_ref, a_ref, b_ref, o_ref, acc, stage, landed, send_sems, recv_sems):
        del c_ref
        phase, t, kk = pl.program_id(0), pl.program_id(1), pl.program_id(2)
        x, y, c, _ = _mesh_place()

        def copy(tile):
            return pltpu.make_async_remote_copy(
                src_ref=stage.at[tile], dst_ref=landed.at[tile], send_sem=send_sems.at[tile],
                recv_sem=recv_sems.at[tile], device_id=(x, y, 1 - c), device_id_type=MESH)

        @pl.when(kk == 0)
        def _():
            acc[...] = jnp.zeros_like(acc)

        acc[...] += _dot_tn(a_ref[...], b_ref[...])

        @pl.when((kk == nk - 1) & (phase == 0))
        def _():
            stage[t] = acc[...].astype(stage.dtype)
            copy(t).start()

        @pl.when((kk == nk - 1) & (phase == 1))
        def _():
            copy(t).wait_recv()
            total = acc[...] + landed[t].astype(F32)
            for q in range(per):
                o_ref[q] = total[rows * q:rows * (q + 1)].astype(o_ref.dtype)

        @pl.when((kk == nk - 1) & (phase == 1) & (t == n_tiles - 1))
        def _():
            for tile in range(n_tiles):
                copy(tile).wait_send()

    out = _hosted_call(
        body, comm, name=name, grid=(2, n_tiles, nk), n_prefetch=1,
        out_shape=[jax.ShapeDtypeStruct((n_tiles * per, rows, tn), GRAD_COMM_DTYPE)],
        in_specs=[pl.BlockSpec((tk, tm), lambda p, t, k, c: (k, tiles.a_index(t, half(p, c)))),
                  pl.BlockSpec((tk, tn), lambda p, t, k, c: (k, tiles.b_index(t, half(p, c))))],
        out_specs=[pl.BlockSpec((per, rows, tn), lambda p, t, k, c: (p * t, 0, 0))],
        scratch_shapes=[pltpu.VMEM((tm, tn), F32), pltpu.VMEM((n_tiles, tm, tn), GRAD_COMM_DTYPE),
                        pltpu.VMEM((n_tiles, tm, tn), GRAD_COMM_DTYPE),
                        pltpu.SemaphoreType.DMA((n_tiles,)), pltpu.SemaphoreType.DMA((n_tiles,))],
        semantics=("arbitrary", "arbitrary", "arbitrary"),
    )(c_idx, a, b)
    return out[0] if comm is None else out


def _row_tile(rows, most=256, sublanes=16):
    return max(t for t in range(sublanes, most + 1, sublanes) if rows % t == 0)


def _adam_update(w, g, m, v):
    m_new = ADAM_B1 * m + (1.0 - ADAM_B1) * g
    v_new = ADAM_B2 * v + (1.0 - ADAM_B2) * (g * g)
    m_hat = m_new / (1.0 - ADAM_B1 ** ADAM_STEP)
    v_hat = v_new / (1.0 - ADAM_B2 ** ADAM_STEP)
    delta = -ADAM_LR * (m_hat / (jnp.sqrt(v_hat) + ADAM_EPS) + ADAM_WD * w)
    return delta, m_new, v_new


def _sum_chips_kernel(own, others, place, name):
    _, r, n = own.shape
    tr = _row_tile(r)

    def body(place_ref, own_ref, oth_ref, o_ref):
        del place_ref
        acc = own_ref[...].astype(F32)
        for k in range(N_CHIPS - 1):
            acc = acc + oth_ref[k].astype(F32)
        o_ref[...] = acc

    return pl.pallas_call(
        body, name=name, out_shape=jax.ShapeDtypeStruct((2, r, n), F32),
        grid_spec=pltpu.PrefetchScalarGridSpec(
            num_scalar_prefetch=1, grid=(r // tr,),
            in_specs=[pl.BlockSpec((None, tr, n), lambda i, p: (p[0], i, 0)),
                      pl.BlockSpec((N_CHIPS - 1, tr, n), lambda i, p: (0, i, 0))],
            out_specs=pl.BlockSpec((None, tr, n), lambda i, p: (p[1], i, 0))),
        compiler_params=_params("parallel"),
    )(place, own, others)


def _adam_kernel(w, g, m, v, name, after=()):
    r, n = w.shape
    by_columns = g.shape[1] == r
    tr, tn = _row_tile(g.shape[1]), g.shape[2]

    def body(w_ref, g_ref, m_ref, v_ref, *rest):
        g_out, d_ref, mo_ref, vo_ref = rest[len(after):]
        gv = g_ref[...]
        g_out[...] = gv
        d_ref[...], mo_ref[...], vo_ref[...] = _adam_update(w_ref[...], gv, m_ref[...], v_ref[...])

    steps = g.shape[1] // tr
    spec = pl.BlockSpec((tr, tn), (lambda h, i: (i, h)) if by_columns else (lambda h, i: (h * steps + i, 0)))
    return pl.pallas_call(
        body, name=name, grid=(2, steps), out_shape=[jax.ShapeDtypeStruct((r, n), F32)] * 4,
        in_specs=[spec, pl.BlockSpec((None, tr, tn), lambda h, i: (h, i, 0)), spec, spec] + [_any()] * len(after),
        out_specs=[spec] * 4, compiler_params=_params("parallel", "parallel"),
    )(w, g, m, v, *after)


SMALL_PARAMS = ("b_ada", "g_mix", "g_ffn", "g_final", "b_spatial", "sinks", "w_spatial")


def _small_update_kernel(gathered, params):
    shapes = [params[nm][0].shape for nm in SMALL_PARAMS]

    def body(*refs):
        g_refs, refs = refs[:5], refs[5:]
        p_refs, refs = refs[:3 * len(SMALL_PARAMS)], refs[3 * len(SMALL_PARAMS):]
        loss_ref, o_refs = refs[0], refs[1:]

        def total(ref):
            acc = ref[0].astype(F32)
            for k in range(1, N_DEV):
                acc = acc + ref[k].astype(F32)
            return acc

        s1, s2, db, ds, dw = (total(r) for r in g_refs)
        loss_ref[...] = jnp.broadcast_to(s2[6:7, 0:1], loss_ref.shape)
        grads = {"b_ada": [s1[0:1], s1[1:2], s2[5:6], s2[0:1], s2[1:2], s2[2:3]], "g_mix": [s1[2:3]],
                 "g_ffn": [s2[3:4]], "g_final": [s2[4:5]], "b_spatial": [db.T[0:GMLP_GROUPS]],
                 "w_spatial": [dw]}
        lane = lax.broadcasted_iota(jnp.int32, (1, LANES), 1)
        sink_row = jnp.zeros((1, LANES), F32)
        for h in range(N_Q_HEADS):
            sink_row = sink_row + jnp.where(lane == h, ds[h:h + 1, :], 0.0)
        grads["sinks"] = [sink_row[:, 0:N_Q_HEADS]]
        for i, nm in enumerate(SMALL_PARAMS):
            w_ref, m_ref, v_ref = p_refs[3 * i:3 * i + 3]
            outs = o_refs[4 * i:4 * i + 4]
            width = grads[nm][0].shape[1]
            for k, g in enumerate(grads[nm]):
                cols = slice(width * k, width * (k + 1))
                upd = _adam_update(w_ref[:, cols], g, m_ref[:, cols], v_ref[:, cols])
                for o_ref, val in zip(outs, (g,) + upd):
                    o_ref[:, cols] = val

    flat = [a for nm in SMALL_PARAMS for a in params[nm]]
    out_shape = [jax.ShapeDtypeStruct((8, LANES), F32)]
    out_shape += [jax.ShapeDtypeStruct(s, F32) for s in shapes for _ in range(4)]
    outs = pl.pallas_call(
        body, name="small_update", grid=(1,), out_shape=out_shape,
        in_specs=[_full(g.shape) for g in gathered] + [_full(a.shape) for a in flat],
        out_specs=[_full(s.shape) for s in out_shape],
        compiler_params=_params("arbitrary"),
    )(*gathered, *flat)
    return {nm: outs[1 + 4 * i:5 + 4 * i] for i, nm in enumerate(SMALL_PARAMS)}, outs[0]


def _ada_update_kernel(act_t, dmod, w, m, v):
    r, n = w.shape
    tr = 128

    def body(a_ref, d_ref, w_ref, m_ref, v_ref, g_ref, dl_ref, mo_ref, vo_ref):
        g = _dot(a_ref[...], d_ref[...])
        g_ref[...] = g
        dl_ref[...], mo_ref[...], vo_ref[...] = _adam_update(w_ref[...], g, m_ref[...], v_ref[...])

    spec = pl.BlockSpec((tr, n), lambda i: (i, 0))
    return pl.pallas_call(
        body, name="ada_update", grid=(r // tr,), out_shape=[jax.ShapeDtypeStruct((r, n), F32)] * 4,
        in_specs=[pl.BlockSpec((tr, N_DEV), lambda i: (i, 0)), _full((N_DEV, n)), spec, spec, spec],
        out_specs=[spec] * 4, compiler_params=_params("parallel"),
    )(act_t, dmod, w, m, v)


def kernel(x, c, positions, w_ada, b_ada, g_mix, w_in, w_spatial, b_spatial, sinks, w_out, g_ffn, w_ff1, w_ff2, g_final, loss_target, m_w_ada, m_b_ada, m_g_mix, m_w_in, m_w_spatial, m_b_spatial, m_sinks, m_w_out, m_g_ffn, m_w_ff1, m_w_ff2, m_g_final, v_w_ada, v_b_ada, v_g_mix, v_w_in, v_w_spatial, v_b_spatial, v_sinks, v_w_out, v_g_ffn, v_w_ff1, v_w_ff2, v_g_final):
    xi, yi, ci = lax.axis_index("x"), lax.axis_index("y"), lax.axis_index("c")
    chip = 2 * xi + yi
    dev = 2 * chip + ci
    seq = x.shape[1]
    x2, tgt = x[0], loss_target[0]
    ada_cols = w_ada.shape[2]

    big = {"w_in": tuple(a[0].T for a in (w_in, m_w_in, v_w_in)),
           "w_out": (w_out[0], m_w_out[0], v_w_out[0]), "w_ff1": (w_ff1[0], m_w_ff1[0], v_w_ff1[0]),
           "w_ff2": (w_ff2[0], m_w_ff2[0], v_w_ff2[0])}

    def halves(nm):
        r, n = big[nm][0].shape
        return big[nm][0].astype(WEIGHT_COMM_DTYPE).reshape(2, r // 2, n)

    chip_idx = chip.reshape(1).astype(jnp.int32)
    c_all, w_in_t, g_out = _all_gather8([c, halves("w_in"), halves("w_out")], "gather_first",
                                        split=[False, True, True], skip_own=(2,))
    c_all, w_in_t = c_all.reshape(N_DEV, D_MODEL), w_in_t.reshape(IN_PROJ_WIDTH, D_MODEL)
    b_shard = lax.dynamic_slice(b_ada, (0, chip * ada_cols), (1, ada_cols))
    mod_part, act = _mod_kernel(c_all, w_ada[0], b_shard)
    mod_all, = _all_gather8([mod_part], "gather_mod")
    mod_me = lax.dynamic_index_in_dim(mod_all[0::2], dev, axis=1, keepdims=False)
    mod_me = mod_me.reshape(N_MOD, D_MODEL)
    shift1, scale1, gate1, shift2, scale2, gate2 = (mod_me[k:k + 1] for k in range(N_MOD))

    zeros_row = jnp.zeros((1, D_MODEL), F32)
    vecs1 = jnp.concatenate([g_mix, shift1, scale1] + [zeros_row] * 5, axis=0)
    vecs2 = jnp.concatenate([gate1, shift2, scale2, gate2, g_ffn, g_final.reshape(1, D_MODEL)]
                            + [zeros_row] * 2, axis=0)
    bias_full = jnp.repeat(b_spatial[0].T, HEAD_DIM, axis=1)
    sink_rows = jnp.broadcast_to(sinks[0][:, None], (N_Q_HEADS, LANES))
    inv_freq = ROPE_THETA ** (-jnp.arange(0, ROT_DIM, 2, dtype=F32) / ROT_DIM)
    rope_tab = _rope_lane_tables(*_rope_angle_kernel(positions, inv_freq.reshape(ROT_DIM // 2, 1)))

    trunk_weights = ["w_out", "w_ff1", "w_ff2"]
    shards = [halves(nm) for nm in trunk_weights]
    proj, hb, *staged = _in_proj_kernel(x2, vecs1, w_in_t, comm=_gather2d_first(shards[1:]))
    cat, *staged = _mixer_fwd_kernel(proj, rope_tab, w_spatial[0], bias_full, sink_rows,
                                     comm=_gather2d_second(staged, shards[1:]))
    staged = [g_out] + list(_gather_forward(staged, "gather_forward"))
    dx1, dcat, dmix, h2b, rb, dab, dffb, sums2 = _trunk_kernel(
        x2, tgt, cat, vecs2, chip_idx,
        [g.reshape((N_CHIPS,) + big[nm][0].shape) for nm, g in zip(trunk_weights, staged)],
        [s.reshape(big[nm][0].shape) for nm, s in zip(trunk_weights, shards)])

    c_idx = ci.reshape(1).astype(jnp.int32)
    place = jnp.stack([chip, ci]).astype(jnp.int32)
    half_d = D_MODEL // 2
    cs_ff2 = _weight_grad_kernel(rb, dffb, c_idx, "dw_ff2",
                                 _GradTiles(D_MODEL, half_d, N_CHIPS, 1, lambda t, h: t, lambda t, h: h))
    eighth = D_MODEL // 8
    cs_ff1, sc_ff2 = _weight_grad_kernel(
        h2b, dab, c_idx, "dw_ff1",
        _GradTiles(D_MODEL, half_d, N_CHIPS, 1, lambda t, h: 0, lambda t, h: 2 * t + h),
        comm=_scatter_job([cs_ff2], rows=(0, 6 * eighth)))
    cs_out, sc_ff2 = _weight_grad_kernel(
        cat, dmix, c_idx, "dw_out", _GradTiles(D_MODEL, half_d, 1, N_CHIPS, lambda t, h: 0, lambda t, h: h),
        comm=_scatter_job([cs_ff2], rows=(6 * eighth, eighth), into=[sc_ff2]))
    dproj, dw_spatial, db_lanes, dsink_rows, sc_ff2, sc_ff1, sc_out = _mixer_bwd_kernel(
        proj, rope_tab, dcat, w_spatial[0], w_spatial[0].transpose(0, 2, 1), bias_full, sink_rows,
        dev.reshape(1).astype(jnp.int32),
        comm=_merge_jobs(_scatter_job([cs_ff1, cs_out]),
                         _scatter_job([cs_ff2], rows=(7 * eighth, eighth), into=[sc_ff2])))
    totals = [_sum_chips_kernel(own, oth, place, "grad_sum_" + nm)
              for nm, own, oth in (("w_out", cs_out, sc_out), ("w_ff1", cs_ff1, sc_ff1), ("w_ff2", cs_ff2, sc_ff2))]
    small_slots = [db_lanes, dsink_rows, dw_spatial.reshape(N_DEV, GMLP_GROUPS * CHUNK, CHUNK)]
    cs_in, *rode = _weight_grad_kernel(
        dproj, hb, c_idx, "dw_in",
        _GradTiles(2 * W_IN_BLOCK, half_d, N_CHIPS // 2, 2, lambda t, h: t, lambda t, h: h),
        comm=_merge_jobs(_gather_job(small_slots), _share_job(totals)))
    small_stage1, shared = rode[:len(small_slots)], rode[len(small_slots):]
    scatter_in = _scatter_start(cs_in, "grad_to_chips_w_in_start")
    grad_x, sums1 = _in_proj_bwd_kernel(x2, dx1, dproj, vecs1 + scatter_in[-1][0:1, 0:1], w_in_t)
    cs_in, sc_in = _scatter_wait(scatter_in, sums1, "grad_to_chips_w_in_wait")
    total_in = _sum_chips_kernel(cs_in, sc_in, place, "grad_sum_w_in")
    gather_small = _gather_start([sums1, sums2], small_stage1, [total_in], cs_in, "gather_small_start")
    big_out = {}

    def update(nm, g, after=()):
        w, m, v = big[nm]
        outs = _adam_kernel(w, g, m, v, "adam_" + nm, after=after)
        big_out[nm] = tuple((t.T if nm == "w_in" else t)[None] for t in outs)
        return outs

    update("w_out", shared[0])
    covering = update("w_ff1", shared[1], after=gather_small[-1:])
    update("w_ff2", shared[2])
    *gathered, shared_in = _gather_wait(gather_small, 2, 1, covering[0], "gather_small_wait")
    update("w_in", shared_in)

    small = {"b_ada": (b_ada, m_b_ada, v_b_ada), "g_mix": (g_mix, m_g_mix, v_g_mix),
             "g_ffn": (g_ffn, m_g_ffn, v_g_ffn), "g_final": (g_final, m_g_final, v_g_final),
             "b_spatial": (b_spatial, m_b_spatial, v_b_spatial), "sinks": (sinks, m_sinks, v_sinks),
             "w_spatial": (w_spatial, m_w_spatial, v_w_spatial)}
    flat_shape = {"g_final": (1, D_MODEL), "b_spatial": (GMLP_GROUPS, CHUNK), "w_spatial": (GMLP_GROUPS * CHUNK, CHUNK)}
    small_out, loss_tile = _small_update_kernel(
        gathered, {nm: tuple(a.reshape(flat_shape.get(nm, a.shape)) for a in small[nm]) for nm in small})
    small_out = {nm: [o.reshape(small[nm][0].shape) for o in small_out[nm]] for nm in small}
    loss = loss_tile[0, 0]

    g1, g2 = gathered[0], gathered[1]
    dmod_all = jnp.concatenate([g1[:, 0], g1[:, 1], g2[:, 5], g2[:, 0], g2[:, 1], g2[:, 2]], axis=1)
    dmod_cols = lax.dynamic_slice(dmod_all, (0, chip * ada_cols), (N_DEV, ada_cols))
    ada = _ada_update_kernel(act.T, dmod_cols, w_ada[0], m_w_ada[0], v_w_ada[0])
    big_out["w_ada"] = tuple(t[None] for t in ada)

    order = ["w_ada", "b_ada", "g_mix", "w_in", "w_spatial", "b_spatial", "sinks", "w_out", "g_ffn",
             "w_ff1", "w_ff2", "g_final"]

    def leaf(nm, k):
        return big_out[nm][k] if nm in big_out else small_out[nm][k]

    outs = [loss, grad_x[None]]
    for k in range(4):
        outs += [leaf(nm, k) for nm in order]
    return tuple(outs)
```

```python
import math
from typing import Callable, NamedTuple

import jax
import jax.numpy as jnp
from jax import lax
from jax.experimental import pallas as pl
from jax.experimental.pallas import tpu as pltpu

F32 = jnp.float32
MXU_DTYPE = jnp.bfloat16
WEIGHT_COMM_DTYPE = jnp.bfloat16
GRAD_COMM_DTYPE = jnp.bfloat16

D_MODEL = 1024
D_FF = 4096
HEAD_DIM = 64
GMLP_GROUPS = 8
GMLP_WIDTH = 512
CHUNK = 128
N_Q_HEADS = 8
N_KV_HEADS = 2
ATTN_WIDTH = 512
KV_WIDTH = 128
ROT_DIM = 16
ROPE_THETA = 500000.0
IN_PROJ_WIDTH = 1792
N_MOD = 6
EPS = 1e-5
N_CHIPS = 4
N_DEV = 8
LANES = 128
W_IN_BLOCK = IN_PROJ_WIDTH // N_CHIPS

ADAM_LR = 0.001
ADAM_B1 = 0.9
ADAM_B2 = 0.999
ADAM_EPS = 1e-08
ADAM_WD = 0.01
ADAM_STEP = 10

VMEM_LIMIT_BYTES = 58 * 1024 * 1024
MESH = pl.DeviceIdType.MESH


def _params(*semantics):
    return pltpu.CompilerParams(dimension_semantics=semantics, vmem_limit_bytes=VMEM_LIMIT_BYTES)


def _dot(a, b):
    return jnp.dot(a.astype(MXU_DTYPE), b.astype(MXU_DTYPE), preferred_element_type=F32)


def _dot_nt(a, b):
    return lax.dot_general(a.astype(MXU_DTYPE), b.astype(MXU_DTYPE), (((1,), (1,)), ((), ())),
                           preferred_element_type=F32)


def _dot_tn(a, b):
    return lax.dot_general(a.astype(MXU_DTYPE), b.astype(MXU_DTYPE), (((0,), (0,)), ((), ())),
                           preferred_element_type=F32)


def _full(shape):
    return pl.BlockSpec(shape, lambda *_: (0,) * len(shape))


def _any():
    return pl.BlockSpec(memory_space=pl.ANY)


def _rowsum(v):
    return jnp.sum(v, axis=0, keepdims=True)


def _mean_last(v):
    return jnp.mean(v, axis=-1, keepdims=True)


class _Comm(NamedTuple):
    operands: tuple
    out_shapes: tuple
    n_sems: int
    make: Callable
    in_place: int = 0


def _hosted_call(body, comm, *, name, grid, in_specs, out_shape, out_specs, scratch_shapes=(), semantics,
                 n_prefetch=0):
    if comm is None:
        return pl.pallas_call(
            body, name=name, out_shape=out_shape, compiler_params=_params(*semantics),
            grid_spec=pltpu.PrefetchScalarGridSpec(
                num_scalar_prefetch=n_prefetch, grid=grid, in_specs=in_specs, out_specs=out_specs,
                scratch_shapes=list(scratch_shapes)))
    n_in, n_out, n_scr = len(in_specs), len(out_shape), len(scratch_shapes)
    k_in, k_out = len(comm.operands), len(comm.out_shapes)

    def hosted(*refs):
        prefetched, refs = refs[:n_prefetch], refs[n_prefetch:]
        ins, refs = refs[:n_in], refs[n_in:]
        c_ins, refs = refs[:k_in], refs[k_in:]
        outs, refs = refs[:n_out], refs[n_out:]
        c_outs, refs = refs[:k_out], refs[k_out:]
        scratch, (send_sems, recv_sems) = refs[:n_scr], refs[n_scr:]
        first, last = None, None
        for d, size in enumerate(grid):
            at_start, at_end = pl.program_id(d) == 0, pl.program_id(d) == size - 1
            first = at_start if first is None else first & at_start
            last = at_end if last is None else last & at_end

        @pl.when(first)
        def _():
            for cp in comm.make(c_ins, c_outs, send_sems, recv_sems)[0]:
                cp.start()

        body(*prefetched, *ins, *outs, *scratch)

        @pl.when(last)
        def _():
            for wait in comm.make(c_ins, c_outs, send_sems, recv_sems)[1]:
                wait()

    aliases = {n_prefetch + n_in + i: n_out + i for i in range(comm.in_place)}
    call = pl.pallas_call(
        hosted, name=name, out_shape=list(out_shape) + list(comm.out_shapes),
        compiler_params=_params(*semantics), input_output_aliases=aliases,
        grid_spec=pltpu.PrefetchScalarGridSpec(
            num_scalar_prefetch=n_prefetch, grid=grid, in_specs=list(in_specs) + [_any()] * k_in,
            out_specs=list(out_specs) + [_any()] * k_out,
            scratch_shapes=list(scratch_shapes) + [pltpu.SemaphoreType.DMA((comm.n_sems,)),
                                                    pltpu.SemaphoreType.DMA((comm.n_sems,))]))
    return lambda *args: call(*args, *comm.operands)


class _Shifted:
    def __init__(self, base, offset):
        self.base, self.offset = base, offset

    @property
    def at(self):
        return self

    def __getitem__(self, k):
        return self.base.at[self.offset + k]


def _merge_jobs(*jobs):
    def order(count):
        first = [(j, i) for j, job in enumerate(jobs) for i in range(job.in_place)]
        return first + [(j, i) for j, job in enumerate(jobs) for i in range(job.in_place, count(job))]

    op_order, out_order = order(lambda job: len(job.operands)), order(lambda job: len(job.out_shapes))

    def make(ins, outs, send_sems, recv_sems):
        starts, waits, sem = [], [], 0
        for j, job in enumerate(jobs):
            mine_in = [ins[k] for k, (jj, _) in enumerate(op_order) if jj == j]
            mine_out = [outs[k] for k, (jj, _) in enumerate(out_order) if jj == j]
            s, w = job.make(mine_in, mine_out, _Shifted(send_sems, sem), _Shifted(recv_sems, sem))
            starts, waits, sem = starts + s, waits + w, sem + job.n_sems
        return starts, waits

    return _Comm(tuple(jobs[j].operands[i] for j, i in op_order), tuple(jobs[j].out_shapes[i] for j, i in out_order),
                 sum(job.n_sems for job in jobs), make, in_place=sum(job.in_place for job in jobs))


def _mesh_place():
    x, y, c = lax.axis_index("x"), lax.axis_index("y"), lax.axis_index("c")
    return x, y, c, [(1 - x, y), (x, 1 - y), (1 - x, 1 - y)]


def _gather_job(bufs):
    per = 4

    def make(ins, outs, send_sems, recv_sems):
        del ins
        x, y, c, chips = _mesh_place()
        starts, waits = [], []
        for a, out in enumerate(outs):
            mine = src = out.at[4 * x + 2 * y + c]
            to = [(x, y, 1 - c)] + [(px, py, c) for px, py in chips]
            sends = [pltpu.make_async_remote_copy(
                src_ref=src, dst_ref=mine, send_sem=send_sems.at[per * a + k],
                recv_sem=recv_sems.at[per * a + k], device_id=dev, device_id_type=MESH)
                for k, dev in enumerate(to)]
            recvs = [pltpu.make_async_remote_copy(
                src_ref=src, dst_ref=out.at[4 * px + 2 * py + pc], send_sem=send_sems.at[per * a + k],
                recv_sem=recv_sems.at[per * a + k], device_id=(px, py, pc), device_id_type=MESH)
                for k, (px, py, pc) in enumerate(to)]
            starts += sends
            waits += [s.wait_send for s in sends] + [r.wait_recv for r in recvs]
        return starts, waits

    shapes = tuple(jax.ShapeDtypeStruct(b.shape, b.dtype) for b in bufs)
    return _Comm(tuple(bufs), shapes, per * len(bufs), make, in_place=len(bufs))


def _slots(x, y, c):
    return 4 * x + 2 * y + c, 4 * (1 - x) + 2 * y + c, 4 * x + 2 * (1 - y) + c, 4 * (1 - x) + 2 * (1 - y) + c


def _gather2d_first(halves):
    per = 2

    def make(ins, outs, send_sems, recv_sems):
        x, y, c, _ = _mesh_place()
        me, xn, yn, _ = _slots(x, y, c)
        starts, waits = [], []
        for a, (src, out) in enumerate(zip(ins, outs)):
            blk = src.at[c]
            rows = blk.shape[0] // 2
            upper, lower = pl.ds(0, rows), pl.ds(rows, rows)

            def copy(k, src_ref, dst_ref, dev, a=a):
                return pltpu.make_async_remote_copy(
                    src_ref=src_ref, dst_ref=dst_ref, send_sem=send_sems.at[per * a + k],
                    recv_sem=recv_sems.at[per * a + k], device_id=dev, device_id_type=MESH)

            sends = [copy(0, blk.at[upper], out.at[me, upper], (1 - x, y, c)),
                     copy(1, blk.at[lower], out.at[me, lower], (x, 1 - y, c))]
            recvs = [copy(0, blk.at[upper], out.at[xn, upper], (1 - x, y, c)),
                     copy(1, blk.at[lower], out.at[yn, lower], (x, 1 - y, c))]
            starts += sends
            waits += [s.wait_send for s in sends] + [r.wait_recv for r in recvs]
        return starts, waits

    shapes = tuple(jax.ShapeDtypeStruct((N_DEV,) + h.shape[1:], h.dtype) for h in halves)
    return _Comm(tuple(halves), shapes, per * len(halves), make)


def _gather2d_second(bufs, halves):
    per = 4
    n_arr = len(bufs)

    def make(ins, outs, send_sems, recv_sems):
        x, y, c, _ = _mesh_place()
        me, xn, yn, dg = _slots(x, y, c)
        starts, waits = [], []
        for a, buf in enumerate(outs):
            own = ins[n_arr + a].at[c]
            rows = buf.shape[1] // 2
            upper, lower = pl.ds(0, rows), pl.ds(rows, rows)
            plan = [(own.at[upper], me, upper, (x, 1 - y, c), yn), (buf.at[xn, upper], xn, upper, (x, 1 - y, c), dg),
                    (own.at[lower], me, lower, (1 - x, y, c), xn), (buf.at[yn, lower], yn, lower, (1 - x, y, c), dg)]
            for k, (src, slot, part, dev, landing) in enumerate(plan):
                sems = dict(send_sem=send_sems.at[per * a + k], recv_sem=recv_sems.at[per * a + k],
                            device_id=dev, device_id_type=MESH)
                send = pltpu.make_async_remote_copy(src_ref=src, dst_ref=buf.at[slot, part], **sems)
                arrival = pltpu.make_async_remote_copy(src_ref=src, dst_ref=buf.at[landing, part], **sems)
                starts.append(send)
                waits += [send.wait_send, arrival.wait_recv]
        return starts, waits

    shapes = tuple(jax.ShapeDtypeStruct(b.shape, b.dtype) for b in bufs)
    return _Comm(tuple(bufs) + tuple(halves), shapes, per * n_arr, make, in_place=n_arr)


def _gather_forward(bufs, name):
    n_arr = len(bufs)

    def body(*refs):
        outs = refs[n_arr:2 * n_arr]
        send_sems, recv_sems = refs[2 * n_arr:]
        x, y, c, chips = _mesh_place()
        sends, recvs = [], []
        for a, buf in enumerate(outs):
            for j, (px, py) in enumerate(chips):
                mine, theirs = buf.at[4 * px + 2 * py + c], buf.at[4 * px + 2 * py + 1 - c]
                sems = dict(send_sem=send_sems.at[3 * a + j], recv_sem=recv_sems.at[3 * a + j],
                            device_id=(x, y, 1 - c), device_id_type=MESH)
                sends.append(pltpu.make_async_remote_copy(src_ref=mine, dst_ref=mine, **sems))
                recvs.append(pltpu.make_async_remote_copy(src_ref=mine, dst_ref=theirs, **sems))
        for cp in sends:
            cp.start()
        for s, r in zip(sends, recvs):
            s.wait_send()
            r.wait_recv()

    return pl.pallas_call(
        body, name=name, out_shape=[jax.ShapeDtypeStruct(b.shape, b.dtype) for b in bufs],
        in_specs=[_any()] * n_arr, out_specs=[_any()] * n_arr,
        input_output_aliases={a: a for a in range(n_arr)},
        scratch_shapes=[pltpu.SemaphoreType.DMA((3 * n_arr,)), pltpu.SemaphoreType.DMA((3 * n_arr,))],
    )(*bufs)


def _scatter_job(chip_sums, rows=None, into=()):
    n_into = len(into)

    def part(ref):
        return ref if rows is None else ref.at[pl.ds(rows[0], rows[1])]

    def make(ins, outs, send_sems, recv_sems):
        x, y, c, chips = _mesh_place()
        copies = [pltpu.make_async_remote_copy(
            src_ref=part(src.at[2 * px + py]), dst_ref=part(out.at[j]), send_sem=send_sems.at[3 * a + j],
            recv_sem=recv_sems.at[3 * a + j], device_id=(px, py, c), device_id_type=MESH)
            for a, (src, out) in enumerate(zip(ins[n_into:], outs)) for j, (px, py) in enumerate(chips)]
        return copies, [cp.wait for cp in copies]

    shapes = tuple(jax.ShapeDtypeStruct((3,) + s.shape[1:], s.dtype) for s in chip_sums)
    return _Comm(tuple(into) + tuple(chip_sums), shapes, 3 * len(chip_sums), make, in_place=n_into)


def _all_gather8(blocks, name, split=False, forward=(), riders=(), skip_own=()):
    n_arr, n_fwd = len(blocks), len(forward)
    splits = list(split) if isinstance(split, (list, tuple)) else [split] * n_arr
    own_slots = [a not in skip_own for a in range(n_arr)]
    rider_in = sum(len(r.operands) for r in riders)
    rider_out = sum(len(r.out_shapes) for r in riders)

    def body(*refs):
        x_refs, refs = refs[:n_arr], refs[n_arr + n_fwd:]
        r_ins, refs = refs[:rider_in], refs[rider_in:]
        out_refs, refs = refs[:n_arr], refs[n_arr:]
        fwd_refs, refs = refs[:n_fwd], refs[n_fwd:]
        r_outs, refs = refs[:rider_out], refs[rider_out:]
        (send_sems, recv_sems, local_sems), rider_sems = refs[:3], refs[3:]
        x, y, c, chips = _mesh_place()
        me, sibling = (x, y, c), (x, y, 1 - c)
        passing = []
        for f, buf in enumerate(fwd_refs):
            for j, (px, py) in enumerate(chips):
                mine, theirs = buf.at[4 * px + 2 * py + c], buf.at[4 * px + 2 * py + 1 - c]
                sems = dict(send_sem=send_sems.at[7 * n_arr + 3 * f + j], recv_sem=recv_sems.at[7 * n_arr + 3 * f + j],
                            device_id=sibling, device_id_type=MESH)
                passing.append((pltpu.make_async_remote_copy(src_ref=mine, dst_ref=mine, **sems),
                                pltpu.make_async_remote_copy(src_ref=mine, dst_ref=theirs, **sems)))
        for send, _ in passing:
            send.start()
        arrays = []
        for a, (x_ref, out_ref) in enumerate(zip(x_refs, out_refs)):
            src_mine = x_ref.at[c] if splits[a] else x_ref

            def copy(k, blk, to, src=None, a=a, out_ref=out_ref):
                dst = out_ref.at[4 * blk[0] + 2 * blk[1] + blk[2]]
                return pltpu.make_async_remote_copy(
                    src_ref=dst if src is None else src, dst_ref=dst,
                    send_sem=send_sems.at[7 * a + k], recv_sem=recv_sems.at[7 * a + k],
                    device_id=to, device_id_type=MESH)

            mine = pltpu.make_async_copy(src_mine, out_ref.at[4 * x + 2 * y + c], local_sems.at[a])
            first = [copy(0, me, sibling, src=src_mine)] if own_slots[a] else []
            first += [copy(1 + j, me, (*chip, c), src=src_mine) for j, chip in enumerate(chips)]
            for cp in first + ([mine] if own_slots[a] else []):
                cp.start()
            arrays.append((copy, mine, first, own_slots[a]))
        rider_waits, i0, o0 = [], 0, 0
        for n, job in enumerate(riders):
            k_in, k_out = len(job.operands), len(job.out_shapes)
            starts, waits = job.make(r_ins[i0:i0 + k_in], r_outs[o0:o0 + k_out],
                                     rider_sems[2 * n], rider_sems[2 * n + 1])
            for cp in starts:
                cp.start()
            rider_waits += waits
            i0, o0 = i0 + k_in, o0 + k_out
        sent = []
        for copy, mine, first, own in arrays:
            passed = [copy(4 + j, (*chip, c), sibling) for j, chip in enumerate(chips)]
            for j, chip in enumerate(chips):
                copy(1 + j, (*chip, c), me).wait_recv()
                passed[j].start()
            sent += first + passed
        for copy, mine, first, own in arrays:
            if own:
                copy(0, sibling, me).wait_recv()
                mine.wait()
            for j, chip in enumerate(chips):
                copy(4 + j, (*chip, 1 - c), me).wait_recv()
        for cp in sent:
            cp.wait_send()
        for send, arrival in passing:
            send.wait_send()
            arrival.wait_recv()
        for wait in rider_waits:
            wait()

    n_sems = 7 * n_arr + 3 * n_fwd
    rider_operands = [a for r in riders for a in r.operands]
    rider_shapes = [s for r in riders for s in r.out_shapes]
    return pl.pallas_call(
        body, name=name,
        out_shape=[jax.ShapeDtypeStruct((N_DEV,) + tuple(b.shape[1:] if s else b.shape), b.dtype)
                   for b, s in zip(blocks, splits)]
        + [jax.ShapeDtypeStruct(f.shape, f.dtype) for f in forward] + rider_shapes,
        in_specs=[_any()] * (n_arr + n_fwd + rider_in), out_specs=[_any()] * (n_arr + n_fwd + rider_out),
        input_output_aliases={n_arr + f: n_arr + f for f in range(n_fwd)},
        scratch_shapes=[pltpu.SemaphoreType.DMA((n_sems,)), pltpu.SemaphoreType.DMA((n_sems,)),
                        pltpu.SemaphoreType.DMA((n_arr,))]
        + [pltpu.SemaphoreType.DMA((r.n_sems,)) for r in riders for _ in range(2)],
    )(*blocks, *forward, *rider_operands)


def _first_stage_copies(srcs, lands, splits, owns, send_sems, recv_sems, local_sems):
    x, y, c, chips = _mesh_place()
    same_core = [(px, py, c) for px, py in chips]
    everyone = [(x, y, 1 - c)] + [(px, py, pc) for px, py in chips for pc in (c, 1 - c)]
    per_array, local, k = [], [], 0
    for a, (src, land, split, own) in enumerate(zip(srcs, lands, splits, owns)):
        mine = src.at[c] if split else src
        peers = (([(x, y, 1 - c)] if own else []) + same_core) if split else everyone
        pairs = []
        for px, py, pc in peers:
            sems = dict(send_sem=send_sems.at[k], recv_sem=recv_sems.at[k],
                        device_id=(px, py, pc), device_id_type=MESH)
            pairs.append((pltpu.make_async_remote_copy(src_ref=mine, dst_ref=land.at[4 * x + 2 * y + c], **sems),
                          pltpu.make_async_remote_copy(src_ref=mine, dst_ref=land.at[4 * px + 2 * py + pc], **sems)))
            k += 1
        per_array.append(pairs)
        if own:
            local.append(pltpu.make_async_copy(mine, land.at[4 * x + 2 * y + c], local_sems.at[a]))
    return per_array, local


def _first_stage_start(blocks, splits, owns, name):
    n = len(blocks)
    n_sems = sum((4 if own else 3) if split else N_DEV - 1 for split, own in zip(splits, owns))
    hbm, sem = pl.BlockSpec(memory_space=pltpu.HBM), pl.BlockSpec(memory_space=pltpu.SEMAPHORE)

    def body(*refs):
        srcs, lands = refs[:n], refs[n:2 * n]
        send_sems, recv_sems, local_sems = refs[2 * n:2 * n + 3]
        token = refs[-1]
        per_array, local = _first_stage_copies(srcs, lands, splits, owns, send_sems, recv_sems, local_sems)
        for pairs in per_array:
            for send, _ in pairs:
                send.start()
        for cp in local:
            cp.start()
        token[...] = jnp.zeros_like(token)

    lands = [lax.empty((N_DEV,) + tuple(b.shape[1:] if split else b.shape), b.dtype)
             for b, split in zip(blocks, splits)]
    operands = [pltpu.with_memory_space_constraint(a, pltpu.HBM) for a in list(blocks) + lands]
    return pl.pallas_call(
        body, name=name,
        out_shape=[pltpu.SemaphoreType.DMA((n_sems,)), pltpu.SemaphoreType.DMA((n_sems,)),
                   pltpu.SemaphoreType.DMA((n,))]
        + [pltpu.HBM(a.shape, a.dtype) for a in operands] + [jax.ShapeDtypeStruct((8, LANES), F32)],
        in_specs=[hbm] * (2 * n),
        out_specs=[sem, sem, sem] + [hbm] * (2 * n) + [pl.BlockSpec(memory_space=pltpu.VMEM)],
        input_output_aliases={i: 3 + i for i in range(2 * n)},
        compiler_params=pltpu.CompilerParams(has_side_effects=pltpu.SideEffectType.DATAFLOW_SIDE_EFFECTING),
    )(*operands)


def _first_stage_finish(started, splits, owns, after, name):
    send_sems, recv_sems, local_sems, *bufs, _ = started
    n = len(bufs) // 2
    hbm, sem = pl.BlockSpec(memory_space=pltpu.HBM), pl.BlockSpec(memory_space=pltpu.SEMAPHORE)

    def body(*refs):
        srcs, lands = refs[:n], refs[n:2 * n]
        send_sems, recv_sems, local_sems = refs[2 * n:2 * n + 3]
        per_array, local = _first_stage_copies(srcs, lands, splits, owns, send_sems, recv_sems, local_sems)
        for pairs in per_array:
            for send, arrival in pairs:
                send.wait_send()
                arrival.wait_recv()
        for cp in local:
            cp.wait()

    out = pl.pallas_call(
        body, name=name, out_shape=[pltpu.HBM(b.shape, b.dtype) for b in bufs],
        in_specs=[hbm] * (2 * n) + [sem, sem, sem] + [_any()] * len(after), out_specs=[hbm] * (2 * n),
        input_output_aliases={i: i for i in range(2 * n)},
        compiler_params=pltpu.CompilerParams(has_side_effects=pltpu.SideEffectType.DATAFLOW_SIDE_EFFECTING),
    )(*bufs, send_sems, recv_sems, local_sems, *after)
    return out[n:]


def _split_scatter_copies(src, land, send_sems, recv_sems):
    x, y, c, chips = _mesh_place()
    return [pltpu.make_async_remote_copy(
        src_ref=src.at[2 * px + py], dst_ref=land.at[j], send_sem=send_sems.at[j], recv_sem=recv_sems.at[j],
        device_id=(px, py, c), device_id_type=MESH) for j, (px, py) in enumerate(chips)]


def _scatter_start(chip_sums, name):
    hbm, sem = pl.BlockSpec(memory_space=pltpu.HBM), pl.BlockSpec(memory_space=pltpu.SEMAPHORE)

    def body(src, land, send_sems, recv_sems, src_thru, land_thru, token):
        del src_thru, land_thru
        for cp in _split_scatter_copies(src, land, send_sems, recv_sems):
            cp.start()
        token[...] = jnp.zeros_like(token)

    land = lax.empty((N_CHIPS - 1,) + chip_sums.shape[1:], chip_sums.dtype)
    operands = [pltpu.with_memory_space_constraint(a, pltpu.HBM) for a in (chip_sums, land)]
    return pl.pallas_call(
        body, name=name,
        out_shape=[pltpu.SemaphoreType.DMA((N_CHIPS - 1,)), pltpu.SemaphoreType.DMA((N_CHIPS - 1,))]
        + [pltpu.HBM(a.shape, a.dtype) for a in operands] + [jax.ShapeDtypeStruct((8, LANES), F32)],
        in_specs=[hbm, hbm], out_specs=[sem, sem, hbm, hbm, pl.BlockSpec(memory_space=pltpu.VMEM)],
        input_output_aliases={0: 2, 1: 3},
        compiler_params=pltpu.CompilerParams(has_side_effects=pltpu.SideEffectType.DATAFLOW_SIDE_EFFECTING),
    )(*operands)


def _scatter_wait(started, after, name):
    send_sems, recv_sems, src, land, _ = started
    hbm, sem = pl.BlockSpec(memory_space=pltpu.HBM), pl.BlockSpec(memory_space=pltpu.SEMAPHORE)

    def body(src, land, send_sems, recv_sems, after_ref, src_thru, land_thru):
        del after_ref, src_thru, land_thru
        for cp in _split_scatter_copies(src, land, send_sems, recv_sems):
            cp.wait()

    return pl.pallas_call(
        body, name=name, out_shape=[pltpu.HBM(src.shape, src.dtype), pltpu.HBM(land.shape, land.dtype)],
        in_specs=[hbm, hbm, sem, sem, _any()], out_specs=[hbm, hbm], input_output_aliases={0: 0, 1: 1},
        compiler_params=pltpu.CompilerParams(has_side_effects=pltpu.SideEffectType.DATAFLOW_SIDE_EFFECTING),
    )(src, land, send_sems, recv_sems, after)


def _split_gather_copies(srcs, lands, fwds, shares, send_sems, recv_sems):
    x, y, c, chips = _mesh_place()
    peers = [(x, y, 1 - c)] + [(px, py, pc) for px, py in chips for pc in (c, 1 - c)]
    pairs = []
    for a, (src, land) in enumerate(zip(srcs, lands)):
        for k, (px, py, pc) in enumerate(peers):
            sems = dict(send_sem=send_sems.at[7 * a + k], recv_sem=recv_sems.at[7 * a + k],
                        device_id=(px, py, pc), device_id_type=MESH)
            pairs.append((pltpu.make_async_remote_copy(src_ref=src, dst_ref=land.at[4 * x + 2 * y + c], **sems),
                          pltpu.make_async_remote_copy(src_ref=src, dst_ref=land.at[4 * px + 2 * py + pc], **sems)))
    for f, buf in enumerate(fwds):
        for j, (px, py) in enumerate(chips):
            mine, theirs = buf.at[4 * px + 2 * py + c], buf.at[4 * px + 2 * py + 1 - c]
            k = 7 * len(srcs) + 3 * f + j
            sems = dict(send_sem=send_sems.at[k], recv_sem=recv_sems.at[k],
                        device_id=(x, y, 1 - c), device_id_type=MESH)
            pairs.append((pltpu.make_async_remote_copy(src_ref=mine, dst_ref=mine, **sems),
                          pltpu.make_async_remote_copy(src_ref=mine, dst_ref=theirs, **sems)))
    for s, buf in enumerate(shares):
        k = 7 * len(srcs) + 3 * len(fwds) + s
        sems = dict(send_sem=send_sems.at[k], recv_sem=recv_sems.at[k], device_id=(x, y, 1 - c), device_id_type=MESH)
        pairs.append((pltpu.make_async_remote_copy(src_ref=buf.at[c], dst_ref=buf.at[c], **sems),
                      pltpu.make_async_remote_copy(src_ref=buf.at[c], dst_ref=buf.at[1 - c], **sems)))
    return pairs


def _gather_start(blocks, forward, shares, after, name):
    n, n_fwd = len(blocks), len(forward)
    n_sems = 7 * n + 3 * n_fwd + len(shares)
    n_bufs = 2 * n + n_fwd + len(shares)
    hbm, sem = pl.BlockSpec(memory_space=pltpu.HBM), pl.BlockSpec(memory_space=pltpu.SEMAPHORE)

    def body(*refs):
        srcs, lands, fwds, swaps = refs[:n], refs[n:2 * n], refs[2 * n:2 * n + n_fwd], refs[2 * n + n_fwd:n_bufs]
        send_sems, recv_sems = refs[n_bufs + 1:n_bufs + 3]
        token, local_sems = refs[-2:]
        x, y, c, _ = _mesh_place()
        own = [pltpu.make_async_copy(src, land.at[4 * x + 2 * y + c], local_sems.at[a])
               for a, (src, land) in enumerate(zip(srcs, lands))]
        for cp in own:
            cp.start()
        for send, _ in _split_gather_copies(srcs, lands, fwds, swaps, send_sems, recv_sems):
            send.start()
        token[...] = jnp.zeros_like(token)
        for cp in own:
            cp.wait()

    lands = [lax.empty((N_DEV,) + b.shape, b.dtype) for b in blocks]
    operands = [pltpu.with_memory_space_constraint(a, pltpu.HBM)
                for a in list(blocks) + lands + list(forward) + list(shares)]
    return pl.pallas_call(
        body, name=name,
        out_shape=[pltpu.SemaphoreType.DMA((n_sems,)), pltpu.SemaphoreType.DMA((n_sems,))]
        + [pltpu.HBM(a.shape, a.dtype) for a in operands] + [jax.ShapeDtypeStruct((8, LANES), F32)],
        in_specs=[hbm] * len(operands) + [_any()],
        out_specs=[sem, sem] + [hbm] * len(operands) + [pl.BlockSpec(memory_space=pltpu.VMEM)],
        input_output_aliases={i: 2 + i for i in range(len(operands))},
        scratch_shapes=[pltpu.SemaphoreType.DMA((n,))],
        compiler_params=pltpu.CompilerParams(has_side_effects=pltpu.SideEffectType.DATAFLOW_SIDE_EFFECTING),
    )(*operands, after)


def _gather_wait(started, n, n_shares, after, name):
    send_sems, recv_sems, *bufs, _ = started
    n_bufs = len(bufs)
    n_fwd = n_bufs - 2 * n - n_shares
    hbm, sem = pl.BlockSpec(memory_space=pltpu.HBM), pl.BlockSpec(memory_space=pltpu.SEMAPHORE)

    def body(*refs):
        srcs, lands, fwds, swaps = refs[:n], refs[n:2 * n], refs[2 * n:2 * n + n_fwd], refs[2 * n + n_fwd:n_bufs]
        send_sems, recv_sems = refs[n_bufs:n_bufs + 2]
        for send, arrival in _split_gather_copies(srcs, lands, fwds, swaps, send_sems, recv_sems):
            send.wait_send()
            arrival.wait_recv()

    out = pl.pallas_call(
        body, name=name, out_shape=[pltpu.HBM(b.shape, b.dtype) for b in bufs],
        in_specs=[hbm] * len(bufs) + [sem, sem, _any()], out_specs=[hbm] * len(bufs),
        input_output_aliases={i: i for i in range(len(bufs))},
        compiler_params=pltpu.CompilerParams(has_side_effects=pltpu.SideEffectType.DATAFLOW_SIDE_EFFECTING),
    )(*bufs, send_sems, recv_sems, after)
    return out[n:]


def _share_job(bufs):
    def make(ins, outs, send_sems, recv_sems):
        del ins
        x, y, c, _ = _mesh_place()
        sems = lambda a: dict(send_sem=send_sems.at[a], recv_sem=recv_sems.at[a],
                              device_id=(x, y, 1 - c), device_id_type=MESH)
        sends = [pltpu.make_async_remote_copy(src_ref=o.at[c], dst_ref=o.at[c], **sems(a)) for a, o in enumerate(outs)]
        arrivals = [pltpu.make_async_remote_copy(src_ref=o.at[c], dst_ref=o.at[1 - c], **sems(a))
                    for a, o in enumerate(outs)]
        return sends, [s.wait_send for s in sends] + [r.wait_recv for r in arrivals]

    shapes = tuple(jax.ShapeDtypeStruct(b.shape, b.dtype) for b in bufs)
    return _Comm(tuple(bufs), shapes, len(bufs), make, in_place=len(bufs))


def _gelu_tanh(z):
    k = math.sqrt(2.0 / math.pi)
    t = jnp.tanh(k * (z + 0.044715 * (z * z * z)))
    return 0.5 * z * (1.0 + t), t


def _gelu_tanh_grad(z, t):
    k = math.sqrt(2.0 / math.pi)
    return 0.5 * (1.0 + t) + 0.5 * z * (1.0 - t * t) * (k * (1.0 + 3.0 * 0.044715 * (z * z)))


def _rope_angle_kernel(pos_row, invf_col):
    seq = pos_row.shape[1]

    def body(p_ref, f_ref, cos_ref, sin_ref):
        ang = p_ref[...].astype(F32) * f_ref[...]
        cos_ref[...] = jnp.cos(ang)
        sin_ref[...] = jnp.sin(ang)

    return pl.pallas_call(
        body, name="rope_angles", grid=(1,), out_shape=[jax.ShapeDtypeStruct((ROT_DIM // 2, seq), F32)] * 2,
        in_specs=[_full((1, seq)), _full((ROT_DIM // 2, 1))], out_specs=[_full((ROT_DIM // 2, seq))] * 2,
        compiler_params=_params("arbitrary"),
    )(pos_row, invf_col)


def _rope_lane_tables(cos, sin):
    cos_t, sin_t = cos.T, sin.T
    seq, half = cos_t.shape
    ones = jnp.ones((seq, HEAD_DIM - ROT_DIM), F32)
    c64 = jnp.concatenate([cos_t, cos_t, ones], axis=1)
    s1 = jnp.concatenate([sin_t, jnp.zeros((seq, HEAD_DIM - half), F32)], axis=1)
    s2 = jnp.concatenate([jnp.zeros((seq, half), F32), sin_t, jnp.zeros((seq, HEAD_DIM - ROT_DIM), F32)], axis=1)
    return jnp.concatenate([jnp.tile(t, (1, LANES // HEAD_DIM)) for t in (c64, s1, s2)], axis=1)


def _rope_apply(t, tab, sign):
    reps = t.shape[1] // LANES
    c_tab, s1, s2 = (jnp.tile(tab[:, LANES * k:LANES * (k + 1)], (1, reps)) if reps > 1
                     else tab[:, LANES * k:LANES * (k + 1)] for k in range(3))
    half = ROT_DIM // 2
    up = pltpu.roll(t, t.shape[1] - half, 1)
    down = pltpu.roll(t, half, 1)
    return t * c_tab + sign * (down * s2 - up * s1)


def _lane_masks(shape):
    lane = lax.broadcasted_iota(jnp.int32, shape, 1)
    return lane < HEAD_DIM, lane >= HEAD_DIM


HEADS_PER_GROUP = N_Q_HEADS // N_KV_HEADS
ATTN_SCALE = 1.0 / math.sqrt(HEAD_DIM)


def _attn_bias_t(first_block):
    kj = lax.broadcasted_iota(jnp.int32, (2 * CHUNK, CHUNK), 0)
    qi = lax.broadcasted_iota(jnp.int32, (2 * CHUNK, CHUNK), 1)
    ok = (kj > qi) & (kj <= qi + CHUNK)
    if first_block is not None:
        ok = ok & (jnp.logical_not(first_block) | (kj >= CHUNK))
    return jnp.tile(jnp.where(ok, 0.0, -jnp.inf), (1, HEADS_PER_GROUP))


def _group_rows(x, g, lo, hi):
    rows = []
    for r in range(HEADS_PER_GROUP):
        h = HEADS_PER_GROUP * g + r
        pair = x[:, LANES * (h // 2):LANES * (h // 2 + 1)]
        rows.append(jnp.where(hi if h % 2 else lo, pair, 0.0))
    return jnp.concatenate(rows, axis=0)


def _pairs_from_rows(rows, lo):
    return [jnp.where(lo, rows[2 * CHUNK * k:2 * CHUNK * k + CHUNK], rows[2 * CHUNK * k + CHUNK:2 * CHUNK * (k + 1)])
            for k in range(HEADS_PER_GROUP // 2)]


def _group_dup(a, b, g, lo2):
    return jnp.where(lo2, a, b) if g == 0 else jnp.where(lo2, b, a)


def _sink_row(sink_ref, g):
    return jnp.concatenate([sink_ref[HEADS_PER_GROUP * g + r:HEADS_PER_GROUP * g + r + 1, :]
                            for r in range(HEADS_PER_GROUP)], axis=1)


def _attn_probs_t(k_dup, q_rows, bias_t, sink_row):
    s_t = _dot_nt(k_dup, q_rows) * ATTN_SCALE + bias_t
    m = jnp.maximum(jnp.max(s_t, axis=0, keepdims=True), sink_row)
    p = jnp.exp(s_t - m)
    e_sink = jnp.exp(sink_row - m)
    inv = 1.0 / (jnp.sum(p, axis=0, keepdims=True) + e_sink)
    return p * inv, e_sink * inv


def _sgu_forward_pair(wm, vp, j):
    lo, hi = _lane_masks(vp.shape)
    lhs = jnp.concatenate([wm[2 * j], wm[2 * j + 1]], axis=1)
    rhs = jnp.concatenate([jnp.where(lo, vp, 0.0), jnp.where(hi, vp, 0.0)], axis=0)
    return _dot(lhs, rhs)


def _masked_spatial(w_ref):
    t = lax.broadcasted_iota(jnp.int32, (CHUNK, CHUNK), 0)
    s = lax.broadcasted_iota(jnp.int32, (CHUNK, CHUNK), 1)
    tril = s <= t
    return [jnp.where(tril, w_ref[g], 0.0) for g in range(GMLP_GROUPS)], tril, s >= t


def _mod_kernel(c_all, w_shard, b_shard, comm=None):
    n = w_shard.shape[1]
    tn = 512

    def body(c_ref, w_ref, b_ref, mod_ref, act_ref):
        cv = c_ref[...]
        act = cv * (1.0 / (1.0 + jnp.exp(-cv)))
        act_ref[...] = act
        mod_ref[...] = _dot(act, w_ref[...]) + b_ref[...]

    return _hosted_call(
        body, comm, name="ada_mod", grid=(n // tn,),
        out_shape=[jax.ShapeDtypeStruct((N_DEV, n), F32), jax.ShapeDtypeStruct((N_DEV, D_MODEL), F32)],
        in_specs=[_full((N_DEV, D_MODEL)), pl.BlockSpec((D_MODEL, tn), lambda i: (0, i)),
                  pl.BlockSpec((1, tn), lambda i: (0, i))],
        out_specs=[pl.BlockSpec((N_DEV, tn), lambda i: (0, i)), _full((N_DEV, D_MODEL))],
        semantics=("arbitrary",),
    )(c_all, w_shard, b_shard)


def _load_chip_blocks(chip_ref, gathered, local, dsts, sems, first_sem=0):
    for k, dst in enumerate(dsts):
        @pl.when(chip_ref[0] == k)
        def _():
            pltpu.make_async_copy(local, dst, sems.at[first_sem + k]).start()

        @pl.when(chip_ref[0] != k)
        def _():
            pltpu.make_async_copy(gathered.at[k], dst, sems.at[first_sem + k]).start()
    return [pltpu.make_async_copy(local, dst, sems.at[first_sem + k]).wait for k, dst in enumerate(dsts)]


def _in_proj_kernel(x, vecs, w_in_t, comm=None):
    seq = x.shape[0]
    tm = 512

    def body(x_ref, v_ref, w_ref, proj_ref, h_ref):
        xv = x_ref[...]
        rstd = lax.rsqrt(_mean_last(xv * xv) + EPS)
        n1 = (xv * rstd) * v_ref[0:1, :]
        h = n1 * (1.0 + v_ref[2:3, :]) + v_ref[1:2, :]
        hb = h.astype(MXU_DTYPE)
        h_ref[...] = hb
        proj_ref[...] = _dot_nt(hb, w_ref[...])

    return _hosted_call(
        body, comm, name="in_proj", grid=(seq // tm,),
        out_shape=[jax.ShapeDtypeStruct((seq, IN_PROJ_WIDTH), F32),
                   jax.ShapeDtypeStruct((seq, D_MODEL), MXU_DTYPE)],
        in_specs=[pl.BlockSpec((tm, D_MODEL), lambda i: (i, 0)), _full((8, D_MODEL)),
                  _full((IN_PROJ_WIDTH, D_MODEL))],
        out_specs=[pl.BlockSpec((tm, IN_PROJ_WIDTH), lambda i: (i, 0)),
                   pl.BlockSpec((tm, D_MODEL), lambda i: (i, 0))],
        semantics=("arbitrary",),
    )(x, vecs, w_in_t)


MIXER_BLOCKS_PER_STEP = 4
KV_START = 2 * GMLP_WIDTH + ATTN_WIDTH


def _mixer_fwd_kernel(proj, rope_tab, w_spatial, bias_full, sink_rows, comm=None):
    seq = proj.shape[0]
    per = MIXER_BLOCKS_PER_STEP
    steps = seq // (CHUNK * per)
    kv_col = KV_START // (2 * KV_WIDTH)

    def body(proj_ref, prev_ref, tab_ref, ptab_ref, w_ref, bias_ref, sink_ref, cat_ref):
        i = pl.program_id(0)
        wm, _, _ = _masked_spatial(w_ref)
        lo, hi = _lane_masks((CHUNK, LANES))
        lo2, _ = _lane_masks((2 * CHUNK, LANES))
        o = 2 * GMLP_WIDTH
        for s in range(per):
            rows, before = slice(CHUNK * s, CHUNK * (s + 1)), slice(CHUNK * (s - 1), CHUNK * s)
            for j in range(GMLP_GROUPS // 2):
                cols = slice(LANES * j, LANES * (j + 1))
                vcols = slice(GMLP_WIDTH + LANES * j, GMLP_WIDTH + LANES * (j + 1))
                u, _ = _gelu_tanh(proj_ref[rows, cols])
                vp, _ = _gelu_tanh(proj_ref[rows, vcols])
                sv = _sgu_forward_pair(wm, vp, j) + bias_ref[:, cols]
                cat_ref[rows, cols] = (u * sv).astype(cat_ref.dtype)
            tab = tab_ref[rows, :]
            if s == 0:
                prev_kv, prev_tab, first = prev_ref[...], ptab_ref[...], i == 0
            else:
                prev_kv, prev_tab, first = proj_ref[before, KV_START:KV_START + 2 * KV_WIDTH], tab_ref[before, :], None
            q_r = _rope_apply(proj_ref[rows, o:o + ATTN_WIDTH], tab, 1.0)
            k_cur = _rope_apply(proj_ref[rows, KV_START:KV_START + KV_WIDTH], tab, 1.0)
            k_prev = _rope_apply(prev_kv[:, 0:KV_WIDTH], prev_tab, 1.0)
            k_a = jnp.concatenate([k_prev, k_cur], axis=0)
            v_a = jnp.concatenate([prev_kv[:, KV_WIDTH:2 * KV_WIDTH],
                                   proj_ref[rows, KV_START + KV_WIDTH:KV_START + 2 * KV_WIDTH]], axis=0)
            k_b = pltpu.roll(k_a, HEAD_DIM, 1)
            v_b = pltpu.roll(v_a, HEAD_DIM, 1)
            bias_t = _attn_bias_t(first)
            for g in range(N_KV_HEADS):
                p_t, _ = _attn_probs_t(_group_dup(k_a, k_b, g, lo2), _group_rows(q_r, g, lo, hi), bias_t,
                                       _sink_row(sink_ref, g))
                o_t = _dot(_group_dup(v_a, v_b, g, lo2).T, p_t)
                for k, pair in enumerate(_pairs_from_rows(o_t.T, lo)):
                    c0 = GMLP_WIDTH + LANES * (2 * g + k)
                    cat_ref[rows, c0:c0 + LANES] = pair.astype(cat_ref.dtype)

    return _hosted_call(
        body, comm, name="mixer_fwd", grid=(steps,),
        out_shape=[jax.ShapeDtypeStruct((seq, D_MODEL), MXU_DTYPE)],
        in_specs=[pl.BlockSpec((CHUNK * per, IN_PROJ_WIDTH), lambda i: (i, 0)),
                  pl.BlockSpec((CHUNK, 2 * KV_WIDTH), lambda i: (jnp.maximum(per * i - 1, 0), kv_col)),
                  pl.BlockSpec((CHUNK * per, 3 * LANES), lambda i: (i, 0)),
                  pl.BlockSpec((CHUNK, 3 * LANES), lambda i: (jnp.maximum(per * i - 1, 0), 0)),
                  _full((GMLP_GROUPS, CHUNK, CHUNK)), _full((CHUNK, GMLP_WIDTH)),
                  _full((N_Q_HEADS, LANES))],
        out_specs=[pl.BlockSpec((CHUNK * per, D_MODEL), lambda i: (i, 0))],
        semantics=("arbitrary",),
    )(proj, proj, rope_tab, rope_tab, w_spatial, bias_full, sink_rows)


def _trunk_kernel(x, target, cat, vecs, chip_idx, gathered, local):
    seq = x.shape[0]
    tm = 256
    nj = D_FF // D_MODEL
    out_rows = D_MODEL // N_CHIPS

    def body(chip_ref, x_ref, t_ref, cat_ref, v_ref, g_out, g_w1, g_w2, l_out, l_w1, l_w2,
             dx1_ref, dcat_ref, dmix_ref, h2_ref, r_ref, da_ref, dff_ref, sums_ref,
             wout, w1, w2, a_scr, sem):
        i = pl.program_id(0)

        @pl.when(i == 0)
        def _():
            waits = _load_chip_blocks(chip_ref, g_out, l_out,
                                      [wout.at[pl.ds(out_rows * k, out_rows)] for k in range(N_CHIPS)], sem)
            waits += _load_chip_blocks(chip_ref, g_w1, l_w1, [w1.at[k] for k in range(N_CHIPS)], sem, N_CHIPS)
            waits += _load_chip_blocks(chip_ref, g_w2, l_w2, [w2.at[k] for k in range(N_CHIPS)], sem, 2 * N_CHIPS)
            for wait in waits:
                wait()
            sums_ref[...] = jnp.zeros_like(sums_ref)

        gate1, shift2, scale2 = v_ref[0:1, :], v_ref[1:2, :], v_ref[2:3, :]
        gate2, g_ffn, g_final = v_ref[3:4, :], v_ref[4:5, :], v_ref[5:6, :]

        mix = _dot(cat_ref[...], wout[...])
        x1 = x_ref[...] + gate1 * mix
        rstd2 = lax.rsqrt(_mean_last(x1 * x1) + EPS)
        xh2 = x1 * rstd2
        n2 = xh2 * g_ffn
        h2b = (n2 * (1.0 + scale2) + shift2).astype(MXU_DTYPE)
        h2_ref[...] = h2b
        ff = jnp.zeros((tm, D_MODEL), F32)
        for j in range(nj):
            a = _dot(h2b, w1[j])
            a_scr[j] = a
            relu = jnp.maximum(a, 0.0)
            rb = (relu * relu).astype(MXU_DTYPE)
            r_ref[:, D_MODEL * j:D_MODEL * (j + 1)] = rb
            ff = ff + _dot(rb, w2[j])
        x2 = x1 + gate2 * ff
        rstd3 = lax.rsqrt(_mean_last(x2 * x2) + EPS)
        xh3 = x2 * rstd3
        err = xh3 * g_final - t_ref[...]
        loss = 0.5 * _rowsum(_mean_last(err * err))
        dy = err * (1.0 / D_MODEL)
        dxh3 = dy * g_final
        dx2 = rstd3 * (dxh3 - xh3 * _mean_last(dxh3 * xh3))
        dffb = (dx2 * gate2).astype(MXU_DTYPE)
        dff_ref[...] = dffb
        dh2 = jnp.zeros((tm, D_MODEL), F32)
        for j in range(nj):
            dr = _dot_nt(dffb, w2[j])
            dab = (dr * (2.0 * jnp.maximum(a_scr[j], 0.0))).astype(MXU_DTYPE)
            da_ref[:, D_MODEL * j:D_MODEL * (j + 1)] = dab
            dh2 = dh2 + _dot_nt(dab, w1[j])
        dn2 = dh2 * (1.0 + scale2)
        dxh2 = dn2 * g_ffn
        dx1 = dx2 + rstd2 * (dxh2 - xh2 * _mean_last(dxh2 * xh2))
        dx1_ref[...] = dx1
        dmixb = (dx1 * gate1).astype(MXU_DTYPE)
        dmix_ref[...] = dmixb
        dcat_ref[...] = _dot_nt(dmixb, wout[...])

        sums_ref[0:1, :] += _rowsum(dh2)
        sums_ref[1:2, :] += _rowsum(dh2 * n2)
        sums_ref[2:3, :] += _rowsum(dx2 * ff)
        sums_ref[3:4, :] += _rowsum(dn2 * xh2)
        sums_ref[4:5, :] += _rowsum(dy * xh3)
        sums_ref[5:6, :] += _rowsum(dx1 * mix)
        sums_ref[6:7, :] += jnp.broadcast_to(loss, (1, D_MODEL))

    tok = lambda w: pl.BlockSpec((tm, w), lambda i, chip: (i, 0))
    return _hosted_call(
        body, None, name="trunk", grid=(seq // tm,), n_prefetch=1,
        out_shape=[jax.ShapeDtypeStruct((seq, D_MODEL), F32), jax.ShapeDtypeStruct((seq, D_MODEL), F32),
                   jax.ShapeDtypeStruct((seq, D_MODEL), MXU_DTYPE), jax.ShapeDtypeStruct((seq, D_MODEL), MXU_DTYPE),
                   jax.ShapeDtypeStruct((seq, D_FF), MXU_DTYPE), jax.ShapeDtypeStruct((seq, D_FF), MXU_DTYPE),
                   jax.ShapeDtypeStruct((seq, D_MODEL), MXU_DTYPE), jax.ShapeDtypeStruct((8, D_MODEL), F32)],
        in_specs=[tok(D_MODEL), tok(D_MODEL), tok(D_MODEL), _full((8, D_MODEL))] + [_any()] * 6,
        out_specs=[tok(D_MODEL), tok(D_MODEL), tok(D_MODEL), tok(D_MODEL), tok(D_FF), tok(D_FF), tok(D_MODEL),
                   _full((8, D_MODEL))],
        scratch_shapes=[pltpu.VMEM((D_MODEL, D_MODEL), MXU_DTYPE), pltpu.VMEM((nj, D_MODEL, D_MODEL), MXU_DTYPE),
                        pltpu.VMEM((nj, D_MODEL, D_MODEL), MXU_DTYPE), pltpu.VMEM((nj, tm, D_MODEL), F32),
                        pltpu.SemaphoreType.DMA((3 * N_CHIPS,))],
        semantics=("arbitrary",),
    )(chip_idx, x, target, cat, vecs, *gathered, *local)


def _mixer_bwd_kernel(proj, rope_tab, dcat, w_spatial, w_spatial_t, bias_full, sink_rows, dev_idx, comm=None):
    seq = proj.shape[0]
    per = MIXER_BLOCKS_PER_STEP
    steps = seq // (CHUNK * per)
    kv_col = KV_START // (2 * KV_WIDTH)

    def body(dev_ref, proj_ref, prev_ref, tab_ref, ptab_ref, dcat_ref, w_ref, wt_ref, bias_ref, sink_ref,
             dproj_ref, dw_out, db_ref, dsink_ref, carry, dw_ref):
        del dev_ref
        step = pl.program_id(0)

        @pl.when(step == 0)
        def _():
            carry[...] = jnp.zeros_like(carry)
            dw_ref[...] = jnp.zeros_like(dw_ref)
            db_ref[...] = jnp.zeros_like(db_ref)
            dsink_ref[...] = jnp.zeros_like(dsink_ref)

        for s in reversed(range(per)):
            rows = pl.ds(CHUNK * s, CHUNK)
            if s == 0:
                before, before_tab, first = prev_ref, ptab_ref, step == steps - 1
            else:
                before = proj_ref.at[pl.ds(CHUNK * (s - 1), CHUNK), pl.ds(KV_START, 2 * KV_WIDTH)]
                before_tab, first = tab_ref.at[pl.ds(CHUNK * (s - 1), CHUNK)], None
            one_block(proj_ref.at[rows], before, tab_ref.at[rows], before_tab, dcat_ref.at[rows], w_ref, wt_ref,
                      bias_ref, sink_ref, dproj_ref.at[rows], dw_ref, db_ref, dsink_ref, carry, first)

        @pl.when(step == steps - 1)
        def _():
            dw_out[...] = dw_ref[...].astype(dw_out.dtype)

    def one_block(proj_ref, prev_ref, tab_ref, ptab_ref, dcat_ref, w_ref, wt_ref, bias_ref, sink_ref,
                  dproj_ref, dw_ref, db_ref, dsink_ref, carry, first):
        wm, tril, triu = _masked_spatial(w_ref)
        lo, hi = _lane_masks((CHUNK, LANES))
        lane = lax.broadcasted_iota(jnp.int32, (CHUNK, LANES), 1)
        db = jnp.zeros((CHUNK, LANES), F32)
        for j in range(GMLP_GROUPS // 2):
            cols = slice(LANES * j, LANES * (j + 1))
            vcols = slice(GMLP_WIDTH + LANES * j, GMLP_WIDTH + LANES * (j + 1))
            zu, zv = proj_ref[:, cols], proj_ref[:, vcols]
            u, tu = _gelu_tanh(zu)
            vp, tv = _gelu_tanh(zv)
            sv = _sgu_forward_pair(wm, vp, j) + bias_ref[:, cols]
            dout = dcat_ref[:, cols]
            du = dout * sv
            dsv = dout * u
            dsv_lo, dsv_hi = jnp.where(lo, dsv, 0.0), jnp.where(hi, dsv, 0.0)
            lhs_t = jnp.concatenate([jnp.where(triu, wt_ref[2 * j], 0.0),
                                     jnp.where(triu, wt_ref[2 * j + 1], 0.0)], axis=1)
            dv = _dot(lhs_t, jnp.concatenate([dsv_lo, dsv_hi], axis=0))
            dw_ref[2 * j] += jnp.where(tril, _dot_nt(dsv_lo, vp), 0.0)
            dw_ref[2 * j + 1] += jnp.where(tril, _dot_nt(dsv_hi, vp), 0.0)
            db = db + (jnp.where(lane == 2 * j, jnp.sum(dsv_lo, axis=1, keepdims=True), 0.0)
                       + jnp.where(lane == 2 * j + 1, jnp.sum(dsv_hi, axis=1, keepdims=True), 0.0))
            dproj_ref[:, cols] = (du * _gelu_tanh_grad(zu, tu)).astype(dproj_ref.dtype)
            dproj_ref[:, vcols] = (dv * _gelu_tanh_grad(zv, tv)).astype(dproj_ref.dtype)
        db_ref[...] += db
        o = 2 * GMLP_WIDTH
        tab = tab_ref[...]
        q_r = _rope_apply(proj_ref[:, o:o + ATTN_WIDTH], tab, 1.0)
        k_cur = _rope_apply(proj_ref[:, o + ATTN_WIDTH:o + ATTN_WIDTH + KV_WIDTH], tab, 1.0)
        k_prev = _rope_apply(prev_ref[:, 0:KV_WIDTH], ptab_ref[...], 1.0)
        k_a = jnp.concatenate([k_prev, k_cur], axis=0)
        v_a = jnp.concatenate([prev_ref[:, KV_WIDTH:2 * KV_WIDTH],
                               proj_ref[:, o + ATTN_WIDTH + KV_WIDTH:o + ATTN_WIDTH + 2 * KV_WIDTH]], axis=0)
        k_b = pltpu.roll(k_a, HEAD_DIM, 1)
        v_b = pltpu.roll(v_a, HEAD_DIM, 1)
        bias_t = _attn_bias_t(first)
        lo2, _ = _lane_masks((2 * CHUNK, LANES))
        dout_b = dcat_ref[:, GMLP_WIDTH:GMLP_WIDTH + ATTN_WIDTH]
        dk_tot, dv_tot, dq_pairs = [], [], []
        for g in range(N_KV_HEADS):
            k_dup, v_dup = _group_dup(k_a, k_b, g, lo2), _group_dup(v_a, v_b, g, lo2)
            q_rows = _group_rows(q_r, g, lo, hi)
            do_rows = _group_rows(dout_b, g, lo, hi)
            p_t, p_sink = _attn_probs_t(k_dup, q_rows, bias_t, _sink_row(sink_ref, g))
            dp_t = _dot_nt(v_dup, do_rows)
            delta = jnp.sum(p_t * dp_t, axis=0, keepdims=True)
            ds_t = p_t * (dp_t - delta) * ATTN_SCALE
            dsink = -p_sink * delta
            for r in range(HEADS_PER_GROUP):
                h = HEADS_PER_GROUP * g + r
                dsink_ref[h:h + 1, :] += jnp.broadcast_to(
                    jnp.sum(dsink[:, LANES * r:LANES * (r + 1)], axis=1, keepdims=True), (1, LANES))
            dk_full = _dot(ds_t, q_rows)
            dv_full = _dot(p_t, do_rows)
            dk_tot.append(dk_full + pltpu.roll(dk_full, HEAD_DIM, 1))
            dv_tot.append(dv_full + pltpu.roll(dv_full, HEAD_DIM, 1))
            dq_t = _dot(k_dup.T, ds_t)
            dq_pairs += _pairs_from_rows(dq_t.T, lo)
        dk_all = jnp.where(lo2, dk_tot[0], dk_tot[1])
        dv_all = jnp.where(lo2, dv_tot[0], dv_tot[1])
        dk_cur = dk_all[CHUNK:, :] + carry[:, 0:KV_WIDTH]
        dv_cur = dv_all[CHUNK:, :] + carry[:, KV_WIDTH:2 * KV_WIDTH]
        carry[:, 0:KV_WIDTH] = dk_all[:CHUNK, :]
        carry[:, KV_WIDTH:2 * KV_WIDTH] = dv_all[:CHUNK, :]
        dq = _rope_apply(jnp.concatenate(dq_pairs, axis=1), tab, -1.0)
        dproj_ref[:, o:o + ATTN_WIDTH] = dq.astype(dproj_ref.dtype)
        dproj_ref[:, o + ATTN_WIDTH:o + ATTN_WIDTH + KV_WIDTH] = (
            _rope_apply(dk_cur, tab, -1.0).astype(dproj_ref.dtype))
        dproj_ref[:, o + ATTN_WIDTH + KV_WIDTH:o + ATTN_WIDTH + 2 * KV_WIDTH] = dv_cur.astype(dproj_ref.dtype)

    rev = lambda i: steps - 1 - i
    before = lambda i: jnp.maximum(per * rev(i) - 1, 0)
    slot = lambda shape: pl.BlockSpec((None,) + shape, lambda i, d: (d[0],) + (0,) * len(shape))
    return _hosted_call(
        body, comm, name="mixer_bwd", grid=(steps,), n_prefetch=1,
        out_shape=[jax.ShapeDtypeStruct((seq, IN_PROJ_WIDTH), MXU_DTYPE),
                   jax.ShapeDtypeStruct((N_DEV, GMLP_GROUPS, CHUNK, CHUNK), GRAD_COMM_DTYPE),
                   jax.ShapeDtypeStruct((N_DEV, CHUNK, LANES), F32),
                   jax.ShapeDtypeStruct((N_DEV, N_Q_HEADS, LANES), F32)],
        in_specs=[pl.BlockSpec((CHUNK * per, IN_PROJ_WIDTH), lambda i, d: (rev(i), 0)),
                  pl.BlockSpec((CHUNK, 2 * KV_WIDTH), lambda i, d: (before(i), kv_col)),
                  pl.BlockSpec((CHUNK * per, 3 * LANES), lambda i, d: (rev(i), 0)),
                  pl.BlockSpec((CHUNK, 3 * LANES), lambda i, d: (before(i), 0)),
                  pl.BlockSpec((CHUNK * per, D_MODEL), lambda i, d: (rev(i), 0)),
                  _full((GMLP_GROUPS, CHUNK, CHUNK)), _full((GMLP_GROUPS, CHUNK, CHUNK)),
                  _full((CHUNK, GMLP_WIDTH)), _full((N_Q_HEADS, LANES))],
        out_specs=[pl.BlockSpec((CHUNK * per, IN_PROJ_WIDTH), lambda i, d: (rev(i), 0)),
                   slot((GMLP_GROUPS, CHUNK, CHUNK)), slot((CHUNK, LANES)), slot((N_Q_HEADS, LANES))],
        scratch_shapes=[pltpu.VMEM((CHUNK, 2 * KV_WIDTH), F32), pltpu.VMEM((GMLP_GROUPS, CHUNK, CHUNK), F32)],
        semantics=("arbitrary",),
    )(dev_idx, proj, proj, rope_tab, rope_tab, dcat, w_spatial, w_spatial_t, bias_full, sink_rows)


def _in_proj_bwd_kernel(x, dx1, dproj, vecs, w_in_t, comm=None):
    seq = x.shape[0]
    tm = 512

    def body(x_ref, dx1_ref, dp_ref, v_ref, w_ref, gx_ref, sums_ref):
        @pl.when(pl.program_id(0) == 0)
        def _():
            sums_ref[...] = jnp.zeros_like(sums_ref)

        g_mix, scale1 = v_ref[0:1, :], v_ref[2:3, :]
        dh = _dot(dp_ref[...], w_ref[...])
        xv = x_ref[...]
        rstd = lax.rsqrt(_mean_last(xv * xv) + EPS)
        xh = xv * rstd
        dn1 = dh * (1.0 + scale1)
        dxh = dn1 * g_mix
        gx_ref[...] = dx1_ref[...] + rstd * (dxh - xh * _mean_last(dxh * xh))
        sums_ref[0:1, :] += _rowsum(dh)
        sums_ref[1:2, :] += _rowsum(dh * (xh * g_mix))
        sums_ref[2:3, :] += _rowsum(dn1 * xh)

    tok = lambda w: pl.BlockSpec((tm, w), lambda i: (i, 0))
    return _hosted_call(
        body, comm, name="in_proj_bwd", grid=(seq // tm,),
        out_shape=[jax.ShapeDtypeStruct((seq, D_MODEL), F32), jax.ShapeDtypeStruct((8, D_MODEL), F32)],
        in_specs=[tok(D_MODEL), tok(D_MODEL), tok(IN_PROJ_WIDTH), _full((8, D_MODEL)),
                  _full((IN_PROJ_WIDTH, D_MODEL))],
        out_specs=[tok(D_MODEL), _full((8, D_MODEL))],
        semantics=("arbitrary",),
    )(x, dx1, dproj, vecs, w_in_t)


class _GradTiles(NamedTuple):
    tm: int
    tn: int
    n_tiles: int
    chips_per_tile: int
    a_index: Callable
    b_index: Callable


def _weight_grad_kernel(a, b, c_idx, name, tiles, comm=None):
    seq = a.shape[0]
    tk = min(seq, 4096)
    nk = seq // tk
    tm, tn, n_tiles, per = tiles.tm, tiles.tn, tiles.n_tiles, tiles.chips_per_tile
    rows = tm // per

    def half(phase, c):
        return phase * c[0] + (1 - phase) * (1 - c[0])

    def body(c_ref, a_ref, b_ref, o_ref, acc, stage, landed, send_sems, recv_sems):
        del c_ref
        phase, t, kk = pl.program_id(0), pl.program_id(1), pl.program_id(2)
        x, y, c, _ = _mesh_place()

        def copy(tile):
            return pltpu.make_async_remote_copy(
                src_ref=stage.at[tile], dst_ref=landed.at[tile], send_sem=send_sems.at[tile],
                recv_sem=recv_sems.at[tile], device_id=(x, y, 1 - c), device_id_type=MESH)

        @pl.when(kk == 0)
        def _():
            acc[...] = jnp.zeros_like(acc)

        acc[...] += _dot_tn(a_ref[...], b_ref[...])

        @pl.when((kk == nk - 1) & (phase == 0))
        def _():
            stage[t] = acc[...].astype(stage.dtype)
            copy(t).start()

        @pl.when((kk == nk - 1) & (phase == 1))
        def _():
            copy(t).wait_recv()
            total = acc[...] + landed[t].astype(F32)
            for q in range(per):
                o_ref[q] = total[rows * q:rows * (q + 1)].astype(o_ref.dtype)

        @pl.when((kk == nk - 1) & (phase == 1) & (t == n_tiles - 1))
        def _():
            for tile in range(n_tiles):
                copy(tile).wait_send()

    out = _hosted_call(
        body, comm, name=name, grid=(2, n_tiles, nk), n_prefetch=1,
        out_shape=[jax.ShapeDtypeStruct((n_tiles * per, rows, tn), GRAD_COMM_DTYPE)],
        in_specs=[pl.BlockSpec((tk, tm), lambda p, t, k, c: (k, tiles.a_index(t, half(p, c)))),
                  pl.BlockSpec((tk, tn), lambda p, t, k, c: (k, tiles.b_index(t, half(p, c))))],
        out_specs=[pl.BlockSpec((per, rows, tn), lambda p, t, k, c: (p * t, 0, 0))],
        scratch_shapes=[pltpu.VMEM((tm, tn), F32), pltpu.VMEM((n_tiles, tm, tn), GRAD_COMM_DTYPE),
                        pltpu.VMEM((n_tiles, tm, tn), GRAD_COMM_DTYPE),
                        pltpu.SemaphoreType.DMA((n_tiles,)), pltpu.SemaphoreType.DMA((n_tiles,))],
        semantics=("arbitrary", "arbitrary", "arbitrary"),
    )(c_idx, a, b)
    return out[0] if comm is None else out


def _row_tile(rows, most=256, sublanes=16):
    return max(t for t in range(sublanes, most + 1, sublanes) if rows % t == 0)


def _adam_update(w, g, m, v):
    m_new = ADAM_B1 * m + (1.0 - ADAM_B1) * g
    v_new = ADAM_B2 * v + (1.0 - ADAM_B2) * (g * g)
    m_hat = m_new / (1.0 - ADAM_B1 ** ADAM_STEP)
    v_hat = v_new / (1.0 - ADAM_B2 ** ADAM_STEP)
    delta = -ADAM_LR * (m_hat / (jnp.sqrt(v_hat) + ADAM_EPS) + ADAM_WD * w)
    return delta, m_new, v_new


def _sum_chips_kernel(own, others, place, name):
    _, r, n = own.shape
    tr = _row_tile(r)

    def body(place_ref, own_ref, oth_ref, o_ref):
        del place_ref
        acc = own_ref[...].astype(F32)
        for k in range(N_CHIPS - 1):
            acc = acc + oth_ref[k].astype(F32)
        o_ref[...] = acc

    return pl.pallas_call(
        body, name=name, out_shape=jax.ShapeDtypeStruct((2, r, n), F32),
        grid_spec=pltpu.PrefetchScalarGridSpec(
            num_scalar_prefetch=1, grid=(r // tr,),
            in_specs=[pl.BlockSpec((None, tr, n), lambda i, p: (p[0], i, 0)),
                      pl.BlockSpec((N_CHIPS - 1, tr, n), lambda i, p: (0, i, 0))],
            out_specs=pl.BlockSpec((None, tr, n), lambda i, p: (p[1], i, 0))),
        compiler_params=_params("parallel"),
    )(place, own, others)


def _adam_kernel(w, g, m, v, name, after=()):
    r, n = w.shape
    by_columns = g.shape[1] == r
    tr, tn = _row_tile(g.shape[1], most=512), g.shape[2]

    def body(w_ref, g_ref, m_ref, v_ref, *rest):
        g_out, d_ref, mo_ref, vo_ref = rest[len(after):]
        gv = g_ref[...]
        g_out[...] = gv
        d_ref[...], mo_ref[...], vo_ref[...] = _adam_update(w_ref[...], gv, m_ref[...], v_ref[...])

    steps = g.shape[1] // tr
    spec = pl.BlockSpec((tr, tn), (lambda h, i: (i, h)) if by_columns else (lambda h, i: (h * steps + i, 0)))
    return pl.pallas_call(
        body, name=name, grid=(2, steps), out_shape=[jax.ShapeDtypeStruct((r, n), F32)] * 4,
        in_specs=[spec, pl.BlockSpec((None, tr, tn), lambda h, i: (h, i, 0)), spec, spec] + [_any()] * len(after),
        out_specs=[spec] * 4, compiler_params=_params("parallel", "parallel"),
    )(w, g, m, v, *after)


SMALL_PARAMS = ("b_ada", "g_mix", "g_ffn", "g_final", "b_spatial", "sinks", "w_spatial")


def _small_update_kernel(gathered, params):
    shapes = [params[nm][0].shape for nm in SMALL_PARAMS]

    def body(*refs):
        g_refs, refs = refs[:5], refs[5:]
        p_refs, refs = refs[:3 * len(SMALL_PARAMS)], refs[3 * len(SMALL_PARAMS):]
        loss_ref, o_refs = refs[0], refs[1:]

        def total(ref):
            acc = ref[0].astype(F32)
            for k in range(1, N_DEV):
                acc = acc + ref[k].astype(F32)
            return acc

        s1, s2, db, ds, dw = (total(r) for r in g_refs)
        loss_ref[...] = jnp.broadcast_to(s2[6:7, 0:1], loss_ref.shape)
        grads = {"b_ada": [s1[0:1], s1[1:2], s2[5:6], s2[0:1], s2[1:2], s2[2:3]], "g_mix": [s1[2:3]],
                 "g_ffn": [s2[3:4]], "g_final": [s2[4:5]], "b_spatial": [db.T[0:GMLP_GROUPS]],
                 "w_spatial": [dw]}
        lane = lax.broadcasted_iota(jnp.int32, (1, LANES), 1)
        sink_row = jnp.zeros((1, LANES), F32)
        for h in range(N_Q_HEADS):
            sink_row = sink_row + jnp.where(lane == h, ds[h:h + 1, :], 0.0)
        grads["sinks"] = [sink_row[:, 0:N_Q_HEADS]]
        for i, nm in enumerate(SMALL_PARAMS):
            w_ref, m_ref, v_ref = p_refs[3 * i:3 * i + 3]
            outs = o_refs[4 * i:4 * i + 4]
            width = grads[nm][0].shape[1]
            for k, g in enumerate(grads[nm]):
                cols = slice(width * k, width * (k + 1))
                upd = _adam_update(w_ref[:, cols], g, m_ref[:, cols], v_ref[:, cols])
                for o_ref, val in zip(outs, (g,) + upd):
                    o_ref[:, cols] = val

    flat = [a for nm in SMALL_PARAMS for a in params[nm]]
    out_shape = [jax.ShapeDtypeStruct((8, LANES), F32)]
    out_shape += [jax.ShapeDtypeStruct(s, F32) for s in shapes for _ in range(4)]
    outs = pl.pallas_call(
        body, name="small_update", grid=(1,), out_shape=out_shape,
        in_specs=[_full(g.shape) for g in gathered] + [_full(a.shape) for a in flat],
        out_specs=[_full(s.shape) for s in out_shape],
        compiler_params=_params("arbitrary"),
    )(*gathered, *flat)
    return {nm: outs[1 + 4 * i:5 + 4 * i] for i, nm in enumerate(SMALL_PARAMS)}, outs[0]


def _ada_update_kernel(act_t, dmod, w, m, v):
    r, n = w.shape
    tr = 256

    def body(a_ref, d_ref, w_ref, m_ref, v_ref, g_ref, dl_ref, mo_ref, vo_ref):
        g = _dot(a_ref[...], d_ref[...])
        g_ref[...] = g
        dl_ref[...], mo_ref[...], vo_ref[...] = _adam_update(w_ref[...], g, m_ref[...], v_ref[...])

    spec = pl.BlockSpec((tr, n), lambda i: (i, 0))
    return pl.pallas_call(
        body, name="ada_update", grid=(r // tr,), out_shape=[jax.ShapeDtypeStruct((r, n), F32)] * 4,
        in_specs=[pl.BlockSpec((tr, N_DEV), lambda i: (i, 0)), _full((N_DEV, n)), spec, spec, spec],
        out_specs=[spec] * 4, compiler_params=_params("parallel"),
    )(act_t, dmod, w, m, v)


def kernel(x, c, positions, w_ada, b_ada, g_mix, w_in, w_spatial, b_spatial, sinks, w_out, g_ffn, w_ff1, w_ff2, g_final, loss_target, m_w_ada, m_b_ada, m_g_mix, m_w_in, m_w_spatial, m_b_spatial, m_sinks, m_w_out, m_g_ffn, m_w_ff1, m_w_ff2, m_g_final, v_w_ada, v_b_ada, v_g_mix, v_w_in, v_w_spatial, v_b_spatial, v_sinks, v_w_out, v_g_ffn, v_w_ff1, v_w_ff2, v_g_final):
    xi, yi, ci = lax.axis_index("x"), lax.axis_index("y"), lax.axis_index("c")
    chip = 2 * xi + yi
    dev = 2 * chip + ci
    seq = x.shape[1]
    x2, tgt = x[0], loss_target[0]
    ada_cols = w_ada.shape[2]

    big = {"w_in": tuple(a[0].T for a in (w_in, m_w_in, v_w_in)),
           "w_out": (w_out[0], m_w_out[0], v_w_out[0]), "w_ff1": (w_ff1[0], m_w_ff1[0], v_w_ff1[0]),
           "w_ff2": (w_ff2[0], m_w_ff2[0], v_w_ff2[0])}

    def halves(nm, zero=None):
        r, n = big[nm][0].shape
        w = big[nm][0] if zero is None else big[nm][0] + zero
        return w.astype(WEIGHT_COMM_DTYPE).reshape(2, r // 2, n)

    chip_idx = chip.reshape(1).astype(jnp.int32)
    first_split, first_own = [False, True, True], [True, True, False]
    first = _first_stage_start([c, halves("w_in"), halves("w_out")], first_split, first_own, "gather_first_start")
    zero = first[-1][0, 0]
    trunk_weights = ["w_out", "w_ff1", "w_ff2"]
    shards = [halves("w_out"), halves("w_ff1", zero), halves("w_ff2", zero)]
    inv_freq = ROPE_THETA ** (-jnp.arange(0, ROT_DIM, 2, dtype=F32) / ROT_DIM) + zero
    rope_tab = _rope_lane_tables(*_rope_angle_kernel(positions, inv_freq.reshape(ROT_DIM // 2, 1)))
    c_all, w_in_t, g_out = _first_stage_finish(first, first_split, first_own, [rope_tab] + shards[1:],
                                               "gather_first_wait")
    w_in_t, g_out = _gather_forward([w_in_t, g_out], "gather_first_forward")
    c_all, w_in_t = c_all.reshape(N_DEV, D_MODEL), w_in_t.reshape(IN_PROJ_WIDTH, D_MODEL)
    b_shard = lax.dynamic_slice(b_ada, (0, chip * ada_cols), (1, ada_cols))
    mod_part, act = _mod_kernel(c_all, w_ada[0], b_shard)
    mod_all, = _all_gather8([mod_part], "gather_mod")
    mod_me = lax.dynamic_index_in_dim(mod_all[0::2], dev, axis=1, keepdims=False)
    mod_me = mod_me.reshape(N_MOD, D_MODEL)
    shift1, scale1, gate1, shift2, scale2, gate2 = (mod_me[k:k + 1] for k in range(N_MOD))

    zeros_row = jnp.zeros((1, D_MODEL), F32)
    vecs1 = jnp.concatenate([g_mix, shift1, scale1] + [zeros_row] * 5, axis=0)
    vecs2 = jnp.concatenate([gate1, shift2, scale2, gate2, g_ffn, g_final.reshape(1, D_MODEL)]
                            + [zeros_row] * 2, axis=0)
    bias_full = jnp.repeat(b_spatial[0].T, HEAD_DIM, axis=1)
    sink_rows = jnp.broadcast_to(sinks[0][:, None], (N_Q_HEADS, LANES))

    proj, hb, *staged = _in_proj_kernel(x2, vecs1, w_in_t, comm=_gather2d_first(shards[1:]))
    cat, *staged = _mixer_fwd_kernel(proj, rope_tab, w_spatial[0], bias_full, sink_rows,
                                     comm=_gather2d_second(staged, shards[1:]))
    staged = [g_out] + list(_gather_forward(staged, "gather_forward"))
    dx1, dcat, dmix, h2b, rb, dab, dffb, sums2 = _trunk_kernel(
        x2, tgt, cat, vecs2, chip_idx,
        [g.reshape((N_CHIPS,) + big[nm][0].shape) for nm, g in zip(trunk_weights, staged)],
        [s.reshape(big[nm][0].shape) for nm, s in zip(trunk_weights, shards)])

    c_idx = ci.reshape(1).astype(jnp.int32)
    place = jnp.stack([chip, ci]).astype(jnp.int32)
    half_d = D_MODEL // 2
    cs_ff2 = _weight_grad_kernel(rb, dffb, c_idx, "dw_ff2",
                                 _GradTiles(D_MODEL, half_d, N_CHIPS, 1, lambda t, h: t, lambda t, h: h))
    eighth = D_MODEL // 8
    cs_ff1, sc_ff2 = _weight_grad_kernel(
        h2b, dab, c_idx, "dw_ff1",
        _GradTiles(D_MODEL, half_d, N_CHIPS, 1, lambda t, h: 0, lambda t, h: 2 * t + h),
        comm=_scatter_job([cs_ff2], rows=(0, 6 * eighth)))
    cs_out, sc_ff2 = _weight_grad_kernel(
        cat, dmix, c_idx, "dw_out", _GradTiles(D_MODEL, half_d, 1, N_CHIPS, lambda t, h: 0, lambda t, h: h),
        comm=_scatter_job([cs_ff2], rows=(6 * eighth, eighth), into=[sc_ff2]))
    dproj, dw_spatial, db_lanes, dsink_rows, sc_ff2, sc_ff1, sc_out = _mixer_bwd_kernel(
        proj, rope_tab, dcat, w_spatial[0], w_spatial[0].transpose(0, 2, 1), bias_full, sink_rows,
        dev.reshape(1).astype(jnp.int32),
        comm=_merge_jobs(_scatter_job([cs_ff1, cs_out]),
                         _scatter_job([cs_ff2], rows=(7 * eighth, eighth), into=[sc_ff2])))
    totals = [_sum_chips_kernel(own, oth, place, "grad_sum_" + nm)
              for nm, own, oth in (("w_out", cs_out, sc_out), ("w_ff1", cs_ff1, sc_ff1), ("w_ff2", cs_ff2, sc_ff2))]
    small_slots = [db_lanes, dsink_rows, dw_spatial.reshape(N_DEV, GMLP_GROUPS * CHUNK, CHUNK)]
    cs_in, *rode = _weight_grad_kernel(
        dproj, hb, c_idx, "dw_in",
        _GradTiles(2 * W_IN_BLOCK, half_d, N_CHIPS // 2, 2, lambda t, h: t, lambda t, h: h),
        comm=_merge_jobs(_gather_job(small_slots), _share_job(totals)))
    small_stage1, shared = rode[:len(small_slots)], rode[len(small_slots):]
    scatter_in = _scatter_start(cs_in, "grad_to_chips_w_in_start")
    grad_x, sums1 = _in_proj_bwd_kernel(x2, dx1, dproj, vecs1 + scatter_in[-1][0:1, 0:1], w_in_t)
    cs_in, sc_in = _scatter_wait(scatter_in, sums1, "grad_to_chips_w_in_wait")
    total_in = _sum_chips_kernel(cs_in, sc_in, place, "grad_sum_w_in")
    gather_small = _gather_start([sums1, sums2], small_stage1, [total_in], cs_in, "gather_small_start")
    big_out = {}

    def update(nm, g, after=()):
        w, m, v = big[nm]
        outs = _adam_kernel(w, g, m, v, "adam_" + nm, after=after)
        big_out[nm] = tuple((t.T if nm == "w_in" else t)[None] for t in outs)
        return outs

    update("w_out", shared[0])
    covering = update("w_ff1", shared[1], after=gather_small[-1:])
    update("w_ff2", shared[2])
    *gathered, shared_in = _gather_wait(gather_small, 2, 1, covering[0], "gather_small_wait")
    update("w_in", shared_in)

    small = {"b_ada": (b_ada, m_b_ada, v_b_ada), "g_mix": (g_mix, m_g_mix, v_g_mix),
             "g_ffn": (g_ffn, m_g_ffn, v_g_ffn), "g_final": (g_final, m_g_final, v_g_final),
             "b_spatial": (b_spatial, m_b_spatial, v_b_spatial), "sinks": (sinks, m_sinks, v_sinks),
             "w_spatial": (w_spatial, m_w_spatial, v_w_spatial)}
    flat_shape = {"g_final": (1, D_MODEL), "b_spatial": (GMLP_GROUPS, CHUNK), "w_spatial": (GMLP_GROUPS * CHUNK, CHUNK)}
    small_out, loss_tile = _small_update_kernel(
        gathered, {nm: tuple(a.reshape(flat_shape.get(nm, a.shape)) for a in small[nm]) for nm in small})
    small_out = {nm: [o.reshape(small[nm][0].shape) for o in small_out[nm]] for nm in small}
    loss = loss_tile[0, 0]

    g1, g2 = gathered[0], gathered[1]
    dmod_all = jnp.concatenate([g1[:, 0], g1[:, 1], g2[:, 5], g2[:, 0], g2[:, 1], g2[:, 2]], axis=1)
    dmod_cols = lax.dynamic_slice(dmod_all, (0, chip * ada_cols), (N_DEV, ada_cols))
    ada = _ada_update_kernel(act.T, dmod_cols, w_ada[0], m_w_ada[0], v_w_ada[0])
    big_out["w_ada"] = tuple(t[None] for t in ada)

    order = ["w_ada", "b_ada", "g_mix", "w_in", "w_spatial", "b_spatial", "sinks", "w_out", "g_ffn",
             "w_ff1", "w_ff2", "g_final"]

    def leaf(nm, k):
        return big_out[nm][k] if nm in big_out else small_out[nm][k]

    outs = [loss, grad_x[None]]
    for k in range(4):
        outs += [leaf(nm, k) for nm in order]
    return tuple(outs)
```

```python
import math
from typing import Callable, NamedTuple

import jax
import jax.numpy as jnp
from jax import lax
from jax.experimental import pallas as pl
from jax.experimental.pallas import tpu as pltpu

F32 = jnp.float32
MXU_DTYPE = jnp.bfloat16
WEIGHT_COMM_DTYPE = jnp.bfloat16
GRAD_COMM_DTYPE = jnp.bfloat16

D_MODEL = 1024
D_FF = 4096
HEAD_DIM = 64
GMLP_GROUPS = 8
GMLP_WIDTH = 512
CHUNK = 128
N_Q_HEADS = 8
N_KV_HEADS = 2
ATTN_WIDTH = 512
KV_WIDTH = 128
ROT_DIM = 16
ROPE_THETA = 500000.0
IN_PROJ_WIDTH = 1792
N_MOD = 6
EPS = 1e-5
N_CHIPS = 4
N_DEV = 8
LANES = 128
W_IN_BLOCK = IN_PROJ_WIDTH // N_CHIPS

ADAM_LR = 0.001
ADAM_B1 = 0.9
ADAM_B2 = 0.999
ADAM_EPS = 1e-08
ADAM_WD = 0.01
ADAM_STEP = 10

VMEM_LIMIT_BYTES = 58 * 1024 * 1024
MESH = pl.DeviceIdType.MESH


def _params(*semantics):
    return pltpu.CompilerParams(dimension_semantics=semantics, vmem_limit_bytes=VMEM_LIMIT_BYTES)


def _dot(a, b):
    return jnp.dot(a.astype(MXU_DTYPE), b.astype(MXU_DTYPE), preferred_element_type=F32)


def _dot_nt(a, b):
    return lax.dot_general(a.astype(MXU_DTYPE), b.astype(MXU_DTYPE), (((1,), (1,)), ((), ())),
                           preferred_element_type=F32)


def _dot_tn(a, b):
    return lax.dot_general(a.astype(MXU_DTYPE), b.astype(MXU_DTYPE), (((0,), (0,)), ((), ())),
                           preferred_element_type=F32)


def _full(shape):
    return pl.BlockSpec(shape, lambda *_: (0,) * len(shape))


def _any():
    return pl.BlockSpec(memory_space=pl.ANY)


def _rowsum(v):
    return jnp.sum(v, axis=0, keepdims=True)


def _mean_last(v):
    return jnp.mean(v, axis=-1, keepdims=True)


class _Comm(NamedTuple):
    operands: tuple
    out_shapes: tuple
    n_sems: int
    make: Callable
    in_place: int = 0


def _hosted_call(body, comm, *, name, grid, in_specs, out_shape, out_specs, scratch_shapes=(), semantics,
                 n_prefetch=0):
    if comm is None:
        return pl.pallas_call(
            body, name=name, out_shape=out_shape, compiler_params=_params(*semantics),
            grid_spec=pltpu.PrefetchScalarGridSpec(
                num_scalar_prefetch=n_prefetch, grid=grid, in_specs=in_specs, out_specs=out_specs,
                scratch_shapes=list(scratch_shapes)))
    n_in, n_out, n_scr = len(in_specs), len(out_shape), len(scratch_shapes)
    k_in, k_out = len(comm.operands), len(comm.out_shapes)

    def hosted(*refs):
        prefetched, refs = refs[:n_prefetch], refs[n_prefetch:]
        ins, refs = refs[:n_in], refs[n_in:]
        c_ins, refs = refs[:k_in], refs[k_in:]
        outs, refs = refs[:n_out], refs[n_out:]
        c_outs, refs = refs[:k_out], refs[k_out:]
        scratch, (send_sems, recv_sems) = refs[:n_scr], refs[n_scr:]
        first, last = None, None
        for d, size in enumerate(grid):
            at_start, at_end = pl.program_id(d) == 0, pl.program_id(d) == size - 1
            first = at_start if first is None else first & at_start
            last = at_end if last is None else last & at_end

        @pl.when(first)
        def _():
            for cp in comm.make(c_ins, c_outs, send_sems, recv_sems)[0]:
                cp.start()

        body(*prefetched, *ins, *outs, *scratch)

        @pl.when(last)
        def _():
            for wait in comm.make(c_ins, c_outs, send_sems, recv_sems)[1]:
                wait()

    aliases = {n_prefetch + n_in + i: n_out + i for i in range(comm.in_place)}
    call = pl.pallas_call(
        hosted, name=name, out_shape=list(out_shape) + list(comm.out_shapes),
        compiler_params=_params(*semantics), input_output_aliases=aliases,
        grid_spec=pltpu.PrefetchScalarGridSpec(
            num_scalar_prefetch=n_prefetch, grid=grid, in_specs=list(in_specs) + [_any()] * k_in,
            out_specs=list(out_specs) + [_any()] * k_out,
            scratch_shapes=list(scratch_shapes) + [pltpu.SemaphoreType.DMA((comm.n_sems,)),
                                                    pltpu.SemaphoreType.DMA((comm.n_sems,))]))
    return lambda *args: call(*args, *comm.operands)


class _Shifted:
    def __init__(self, base, offset):
        self.base, self.offset = base, offset

    @property
    def at(self):
        return self

    def __getitem__(self, k):
        return self.base.at[self.offset + k]


def _merge_jobs(*jobs):
    def order(count):
        first = [(j, i) for j, job in enumerate(jobs) for i in range(job.in_place)]
        return first + [(j, i) for j, job in enumerate(jobs) for i in range(job.in_place, count(job))]

    op_order, out_order = order(lambda job: len(job.operands)), order(lambda job: len(job.out_shapes))

    def make(ins, outs, send_sems, recv_sems):
        starts, waits, sem = [], [], 0
        for j, job in enumerate(jobs):
            mine_in = [ins[k] for k, (jj, _) in enumerate(op_order) if jj == j]
            mine_out = [outs[k] for k, (jj, _) in enumerate(out_order) if jj == j]
            s, w = job.make(mine_in, mine_out, _Shifted(send_sems, sem), _Shifted(recv_sems, sem))
            starts, waits, sem = starts + s, waits + w, sem + job.n_sems
        return starts, waits

    return _Comm(tuple(jobs[j].operands[i] for j, i in op_order), tuple(jobs[j].out_shapes[i] for j, i in out_order),
                 sum(job.n_sems for job in jobs), make, in_place=sum(job.in_place for job in jobs))


def _mesh_place():
    x, y, c = lax.axis_index("x"), lax.axis_index("y"), lax.axis_index("c")
    return x, y, c, [(1 - x, y), (x, 1 - y), (1 - x, 1 - y)]


def _gather_job(bufs):
    per = 4

    def make(ins, outs, send_sems, recv_sems):
        del ins
        x, y, c, chips = _mesh_place()
        starts, waits = [], []
        for a, out in enumerate(outs):
            mine = src = out.at[4 * x + 2 * y + c]
            to = [(x, y, 1 - c)] + [(px, py, c) for px, py in chips]
            sends = [pltpu.make_async_remote_copy(
                src_ref=src, dst_ref=mine, send_sem=send_sems.at[per * a + k],
                recv_sem=recv_sems.at[per * a + k], device_id=dev, device_id_type=MESH)
                for k, dev in enumerate(to)]
            recvs = [pltpu.make_async_remote_copy(
                src_ref=src, dst_ref=out.at[4 * px + 2 * py + pc], send_sem=send_sems.at[per * a + k],
                recv_sem=recv_sems.at[per * a + k], device_id=(px, py, pc), device_id_type=MESH)
                for k, (px, py, pc) in enumerate(to)]
            starts += sends
            waits += [s.wait_send for s in sends] + [r.wait_recv for r in recvs]
        return starts, waits

    shapes = tuple(jax.ShapeDtypeStruct(b.shape, b.dtype) for b in bufs)
    return _Comm(tuple(bufs), shapes, per * len(bufs), make, in_place=len(bufs))


def _slots(x, y, c):
    return 4 * x + 2 * y + c, 4 * (1 - x) + 2 * y + c, 4 * x + 2 * (1 - y) + c, 4 * (1 - x) + 2 * (1 - y) + c


def _gather2d_first(halves):
    per = 2

    def make(ins, outs, send_sems, recv_sems):
        x, y, c, _ = _mesh_place()
        me, xn, yn, _ = _slots(x, y, c)
        starts, waits = [], []
        for a, (src, out) in enumerate(zip(ins, outs)):
            blk = src.at[c]
            rows = blk.shape[0] // 2
            upper, lower = pl.ds(0, rows), pl.ds(rows, rows)

            def copy(k, src_ref, dst_ref, dev, a=a):
                return pltpu.make_async_remote_copy(
                    src_ref=src_ref, dst_ref=dst_ref, send_sem=send_sems.at[per * a + k],
                    recv_sem=recv_sems.at[per * a + k], device_id=dev, device_id_type=MESH)

            sends = [copy(0, blk.at[upper], out.at[me, upper], (1 - x, y, c)),
                     copy(1, blk.at[lower], out.at[me, lower], (x, 1 - y, c))]
            recvs = [copy(0, blk.at[upper], out.at[xn, upper], (1 - x, y, c)),
                     copy(1, blk.at[lower], out.at[yn, lower], (x, 1 - y, c))]
            starts += sends
            waits += [s.wait_send for s in sends] + [r.wait_recv for r in recvs]
        return starts, waits

    shapes = tuple(jax.ShapeDtypeStruct((N_DEV,) + h.shape[1:], h.dtype) for h in halves)
    return _Comm(tuple(halves), shapes, per * len(halves), make)


def _gather2d_second(bufs, halves):
    per = 4
    n_arr = len(bufs)

    def make(ins, outs, send_sems, recv_sems):
        x, y, c, _ = _mesh_place()
        me, xn, yn, dg = _slots(x, y, c)
        starts, waits = [], []
        for a, buf in enumerate(outs):
            own = ins[n_arr + a].at[c]
            rows = buf.shape[1] // 2
            upper, lower = pl.ds(0, rows), pl.ds(rows, rows)
            plan = [(own.at[upper], me, upper, (x, 1 - y, c), yn), (buf.at[xn, upper], xn, upper, (x, 1 - y, c), dg),
                    (own.at[lower], me, lower, (1 - x, y, c), xn), (buf.at[yn, lower], yn, lower, (1 - x, y, c), dg)]
            for k, (src, slot, part, dev, landing) in enumerate(plan):
                sems = dict(send_sem=send_sems.at[per * a + k], recv_sem=recv_sems.at[per * a + k],
                            device_id=dev, device_id_type=MESH)
                send = pltpu.make_async_remote_copy(src_ref=src, dst_ref=buf.at[slot, part], **sems)
                arrival = pltpu.make_async_remote_copy(src_ref=src, dst_ref=buf.at[landing, part], **sems)
                starts.append(send)
                waits += [send.wait_send, arrival.wait_recv]
        return starts, waits

    shapes = tuple(jax.ShapeDtypeStruct(b.shape, b.dtype) for b in bufs)
    return _Comm(tuple(bufs) + tuple(halves), shapes, per * n_arr, make, in_place=n_arr)


def _gather_forward(bufs, name):
    n_arr = len(bufs)

    def body(*refs):
        outs = refs[n_arr:2 * n_arr]
        send_sems, recv_sems = refs[2 * n_arr:]
        x, y, c, chips = _mesh_place()
        sends, recvs = [], []
        for a, buf in enumerate(outs):
            for j, (px, py) in enumerate(chips):
                mine, theirs = buf.at[4 * px + 2 * py + c], buf.at[4 * px + 2 * py + 1 - c]
                sems = dict(send_sem=send_sems.at[3 * a + j], recv_sem=recv_sems.at[3 * a + j],
                            device_id=(x, y, 1 - c), device_id_type=MESH)
                sends.append(pltpu.make_async_remote_copy(src_ref=mine, dst_ref=mine, **sems))
                recvs.append(pltpu.make_async_remote_copy(src_ref=mine, dst_ref=theirs, **sems))
        for cp in sends:
            cp.start()
        for s, r in zip(sends, recvs):
            s.wait_send()
            r.wait_recv()

    return pl.pallas_call(
        body, name=name, out_shape=[jax.ShapeDtypeStruct(b.shape, b.dtype) for b in bufs],
        in_specs=[_any()] * n_arr, out_specs=[_any()] * n_arr,
        input_output_aliases={a: a for a in range(n_arr)},
        scratch_shapes=[pltpu.SemaphoreType.DMA((3 * n_arr,)), pltpu.SemaphoreType.DMA((3 * n_arr,))],
    )(*bufs)


def _scatter_job(chip_sums, rows=None, into=()):
    n_into = len(into)

    def part(ref):
        return ref if rows is None else ref.at[pl.ds(rows[0], rows[1])]

    def make(ins, outs, send_sems, recv_sems):
        x, y, c, chips = _mesh_place()
        copies = [pltpu.make_async_remote_copy(
            src_ref=part(src.at[2 * px + py]), dst_ref=part(out.at[j]), send_sem=send_sems.at[3 * a + j],
            recv_sem=recv_sems.at[3 * a + j], device_id=(px, py, c), device_id_type=MESH)
            for a, (src, out) in enumerate(zip(ins[n_into:], outs)) for j, (px, py) in enumerate(chips)]
        return copies, [cp.wait for cp in copies]

    shapes = tuple(jax.ShapeDtypeStruct((3,) + s.shape[1:], s.dtype) for s in chip_sums)
    return _Comm(tuple(into) + tuple(chip_sums), shapes, 3 * len(chip_sums), make, in_place=n_into)


def _all_gather8(blocks, name, split=False, forward=(), riders=(), skip_own=()):
    n_arr, n_fwd = len(blocks), len(forward)
    splits = list(split) if isinstance(split, (list, tuple)) else [split] * n_arr
    own_slots = [a not in skip_own for a in range(n_arr)]
    rider_in = sum(len(r.operands) for r in riders)
    rider_out = sum(len(r.out_shapes) for r in riders)

    def body(*refs):
        x_refs, refs = refs[:n_arr], refs[n_arr + n_fwd:]
        r_ins, refs = refs[:rider_in], refs[rider_in:]
        out_refs, refs = refs[:n_arr], refs[n_arr:]
        fwd_refs, refs = refs[:n_fwd], refs[n_fwd:]
        r_outs, refs = refs[:rider_out], refs[rider_out:]
        (send_sems, recv_sems, local_sems), rider_sems = refs[:3], refs[3:]
        x, y, c, chips = _mesh_place()
        me, sibling = (x, y, c), (x, y, 1 - c)
        passing = []
        for f, buf in enumerate(fwd_refs):
            for j, (px, py) in enumerate(chips):
                mine, theirs = buf.at[4 * px + 2 * py + c], buf.at[4 * px + 2 * py + 1 - c]
                sems = dict(send_sem=send_sems.at[7 * n_arr + 3 * f + j], recv_sem=recv_sems.at[7 * n_arr + 3 * f + j],
                            device_id=sibling, device_id_type=MESH)
                passing.append((pltpu.make_async_remote_copy(src_ref=mine, dst_ref=mine, **sems),
                                pltpu.make_async_remote_copy(src_ref=mine, dst_ref=theirs, **sems)))
        for send, _ in passing:
            send.start()
        arrays = []
        for a, (x_ref, out_ref) in enumerate(zip(x_refs, out_refs)):
            src_mine = x_ref.at[c] if splits[a] else x_ref

            def copy(k, blk, to, src=None, a=a, out_ref=out_ref):
                dst = out_ref.at[4 * blk[0] + 2 * blk[1] + blk[2]]
                return pltpu.make_async_remote_copy(
                    src_ref=dst if src is None else src, dst_ref=dst,
                    send_sem=send_sems.at[7 * a + k], recv_sem=recv_sems.at[7 * a + k],
                    device_id=to, device_id_type=MESH)

            mine = pltpu.make_async_copy(src_mine, out_ref.at[4 * x + 2 * y + c], local_sems.at[a])
            first = [copy(0, me, sibling, src=src_mine)] if own_slots[a] else []
            first += [copy(1 + j, me, (*chip, c), src=src_mine) for j, chip in enumerate(chips)]
            for cp in first + ([mine] if own_slots[a] else []):
                cp.start()
            arrays.append((copy, mine, first, own_slots[a]))
        rider_waits, i0, o0 = [], 0, 0
        for n, job in enumerate(riders):
            k_in, k_out = len(job.operands), len(job.out_shapes)
            starts, waits = job.make(r_ins[i0:i0 + k_in], r_outs[o0:o0 + k_out],
                                     rider_sems[2 * n], rider_sems[2 * n + 1])
            for cp in starts:
                cp.start()
            rider_waits += waits
            i0, o0 = i0 + k_in, o0 + k_out
        sent = []
        for copy, mine, first, own in arrays:
            passed = [copy(4 + j, (*chip, c), sibling) for j, chip in enumerate(chips)]
            for j, chip in enumerate(chips):
                copy(1 + j, (*chip, c), me).wait_recv()
                passed[j].start()
            sent += first + passed
        for copy, mine, first, own in arrays:
            if own:
                copy(0, sibling, me).wait_recv()
                mine.wait()
            for j, chip in enumerate(chips):
                copy(4 + j, (*chip, 1 - c), me).wait_recv()
        for cp in sent:
            cp.wait_send()
        for send, arrival in passing:
            send.wait_send()
            arrival.wait_recv()
        for wait in rider_waits:
            wait()

    n_sems = 7 * n_arr + 3 * n_fwd
    rider_operands = [a for r in riders for a in r.operands]
    rider_shapes = [s for r in riders for s in r.out_shapes]
    return pl.pallas_call(
        body, name=name,
        out_shape=[jax.ShapeDtypeStruct((N_DEV,) + tuple(b.shape[1:] if s else b.shape), b.dtype)
                   for b, s in zip(blocks, splits)]
        + [jax.ShapeDtypeStruct(f.shape, f.dtype) for f in forward] + rider_shapes,
        in_specs=[_any()] * (n_arr + n_fwd + rider_in), out_specs=[_any()] * (n_arr + n_fwd + rider_out),
        input_output_aliases={n_arr + f: n_arr + f for f in range(n_fwd)},
        scratch_shapes=[pltpu.SemaphoreType.DMA((n_sems,)), pltpu.SemaphoreType.DMA((n_sems,)),
                        pltpu.SemaphoreType.DMA((n_arr,))]
        + [pltpu.SemaphoreType.DMA((r.n_sems,)) for r in riders for _ in range(2)],
    )(*blocks, *forward, *rider_operands)


def _first_stage_copies(srcs, lands, splits, owns, send_sems, recv_sems, local_sems):
    x, y, c, chips = _mesh_place()
    same_core = [(px, py, c) for px, py in chips]
    everyone = [(x, y, 1 - c)] + [(px, py, pc) for px, py in chips for pc in (c, 1 - c)]
    per_array, local, k = [], [], 0
    for a, (src, land, split, own) in enumerate(zip(srcs, lands, splits, owns)):
        mine = src.at[c] if split else src
        peers = (([(x, y, 1 - c)] if own else []) + same_core) if split else everyone
        pairs = []
        for px, py, pc in peers:
            sems = dict(send_sem=send_sems.at[k], recv_sem=recv_sems.at[k],
                        device_id=(px, py, pc), device_id_type=MESH)
            pairs.append((pltpu.make_async_remote_copy(src_ref=mine, dst_ref=land.at[4 * x + 2 * y + c], **sems),
                          pltpu.make_async_remote_copy(src_ref=mine, dst_ref=land.at[4 * px + 2 * py + pc], **sems)))
            k += 1
        per_array.append(pairs)
        if own:
            local.append(pltpu.make_async_copy(mine, land.at[4 * x + 2 * y + c], local_sems.at[a]))
    return per_array, local


def _first_stage_start(groups, name):
    n_groups = len(groups)
    hbm, sem = pl.BlockSpec(memory_space=pltpu.HBM), pl.BlockSpec(memory_space=pltpu.SEMAPHORE)
    operands, sem_shapes, offsets = [], [], []
    for blocks, splits, owns in groups:
        n_sems = sum((4 if own else 3) if split else N_DEV - 1 for split, own in zip(splits, owns))
        sem_shapes += [pltpu.SemaphoreType.DMA((n_sems,)), pltpu.SemaphoreType.DMA((n_sems,)),
                       pltpu.SemaphoreType.DMA((len(blocks),))]
        offsets.append(len(operands))
        operands += list(blocks) + [lax.empty((N_DEV,) + tuple(b.shape[1:] if split else b.shape), b.dtype)
                                    for b, split in zip(blocks, splits)]
    n_ops = len(operands)

    def body(*refs):
        token = refs[-1]
        for g, (blocks, splits, owns) in enumerate(groups):
            n, at = len(blocks), offsets[g]
            per_array, local = _first_stage_copies(refs[at:at + n], refs[at + n:at + 2 * n], splits, owns,
                                                   *refs[n_ops + 3 * g:n_ops + 3 * g + 3])
            for pairs in per_array:
                for send, _ in pairs:
                    send.start()
            for cp in local:
                cp.start()
        token[...] = jnp.zeros_like(token)

    operands = [pltpu.with_memory_space_constraint(a, pltpu.HBM) for a in operands]
    out = pl.pallas_call(
        body, name=name,
        out_shape=sem_shapes + [pltpu.HBM(a.shape, a.dtype) for a in operands]
        + [jax.ShapeDtypeStruct((8, LANES), F32)],
        in_specs=[hbm] * n_ops,
        out_specs=[sem] * (3 * n_groups) + [hbm] * n_ops + [pl.BlockSpec(memory_space=pltpu.VMEM)],
        input_output_aliases={i: 3 * n_groups + i for i in range(n_ops)},
        compiler_params=pltpu.CompilerParams(has_side_effects=pltpu.SideEffectType.DATAFLOW_SIDE_EFFECTING),
    )(*operands)
    thru = out[3 * n_groups:-1]
    return [list(out[3 * g:3 * g + 3]) + list(thru[offsets[g]:offsets[g] + 2 * len(groups[g][0])]) + [out[-1]]
            for g in range(n_groups)]


def _first_stage_finish(started, splits, owns, after, name):
    send_sems, recv_sems, local_sems, *bufs, _ = started
    n = len(bufs) // 2
    hbm, sem = pl.BlockSpec(memory_space=pltpu.HBM), pl.BlockSpec(memory_space=pltpu.SEMAPHORE)

    def body(*refs):
        srcs, lands = refs[:n], refs[n:2 * n]
        send_sems, recv_sems, local_sems = refs[2 * n:2 * n + 3]
        per_array, local = _first_stage_copies(srcs, lands, splits, owns, send_sems, recv_sems, local_sems)
        for pairs in per_array:
            for send, arrival in pairs:
                send.wait_send()
                arrival.wait_recv()
        for cp in local:
            cp.wait()

    out = pl.pallas_call(
        body, name=name, out_shape=[pltpu.HBM(b.shape, b.dtype) for b in bufs],
        in_specs=[hbm] * (2 * n) + [sem, sem, sem] + [_any()] * len(after), out_specs=[hbm] * (2 * n),
        input_output_aliases={i: i for i in range(2 * n)},
        compiler_params=pltpu.CompilerParams(has_side_effects=pltpu.SideEffectType.DATAFLOW_SIDE_EFFECTING),
    )(*bufs, send_sems, recv_sems, local_sems, *after)
    return out[n:]


def _split_scatter_copies(src, land, send_sems, recv_sems):
    x, y, c, chips = _mesh_place()
    return [pltpu.make_async_remote_copy(
        src_ref=src.at[2 * px + py], dst_ref=land.at[j], send_sem=send_sems.at[j], recv_sem=recv_sems.at[j],
        device_id=(px, py, c), device_id_type=MESH) for j, (px, py) in enumerate(chips)]


def _scatter_start(chip_sums, name):
    hbm, sem = pl.BlockSpec(memory_space=pltpu.HBM), pl.BlockSpec(memory_space=pltpu.SEMAPHORE)

    def body(src, land, send_sems, recv_sems, src_thru, land_thru, token):
        del src_thru, land_thru
        for cp in _split_scatter_copies(src, land, send_sems, recv_sems):
            cp.start()
        token[...] = jnp.zeros_like(token)

    land = lax.empty((N_CHIPS - 1,) + chip_sums.shape[1:], chip_sums.dtype)
    operands = [pltpu.with_memory_space_constraint(a, pltpu.HBM) for a in (chip_sums, land)]
    return pl.pallas_call(
        body, name=name,
        out_shape=[pltpu.SemaphoreType.DMA((N_CHIPS - 1,)), pltpu.SemaphoreType.DMA((N_CHIPS - 1,))]
        + [pltpu.HBM(a.shape, a.dtype) for a in operands] + [jax.ShapeDtypeStruct((8, LANES), F32)],
        in_specs=[hbm, hbm], out_specs=[sem, sem, hbm, hbm, pl.BlockSpec(memory_space=pltpu.VMEM)],
        input_output_aliases={0: 2, 1: 3},
        compiler_params=pltpu.CompilerParams(has_side_effects=pltpu.SideEffectType.DATAFLOW_SIDE_EFFECTING),
    )(*operands)


def _scatter_wait(started, after, name):
    send_sems, recv_sems, src, land, _ = started
    hbm, sem = pl.BlockSpec(memory_space=pltpu.HBM), pl.BlockSpec(memory_space=pltpu.SEMAPHORE)

    def body(src, land, send_sems, recv_sems, after_ref, src_thru, land_thru):
        del after_ref, src_thru, land_thru
        for cp in _split_scatter_copies(src, land, send_sems, recv_sems):
            cp.wait()

    return pl.pallas_call(
        body, name=name, out_shape=[pltpu.HBM(src.shape, src.dtype), pltpu.HBM(land.shape, land.dtype)],
        in_specs=[hbm, hbm, sem, sem, _any()], out_specs=[hbm, hbm], input_output_aliases={0: 0, 1: 1},
        compiler_params=pltpu.CompilerParams(has_side_effects=pltpu.SideEffectType.DATAFLOW_SIDE_EFFECTING),
    )(src, land, send_sems, recv_sems, after)


def _split_gather_copies(srcs, lands, fwds, shares, send_sems, recv_sems):
    x, y, c, chips = _mesh_place()
    peers = [(x, y, 1 - c)] + [(px, py, pc) for px, py in chips for pc in (c, 1 - c)]
    pairs = []
    for a, (src, land) in enumerate(zip(srcs, lands)):
        for k, (px, py, pc) in enumerate(peers):
            sems = dict(send_sem=send_sems.at[7 * a + k], recv_sem=recv_sems.at[7 * a + k],
                        device_id=(px, py, pc), device_id_type=MESH)
            pairs.append((pltpu.make_async_remote_copy(src_ref=src, dst_ref=land.at[4 * x + 2 * y + c], **sems),
                          pltpu.make_async_remote_copy(src_ref=src, dst_ref=land.at[4 * px + 2 * py + pc], **sems)))
    for f, buf in enumerate(fwds):
        for j, (px, py) in enumerate(chips):
            mine, theirs = buf.at[4 * px + 2 * py + c], buf.at[4 * px + 2 * py + 1 - c]
            k = 7 * len(srcs) + 3 * f + j
            sems = dict(send_sem=send_sems.at[k], recv_sem=recv_sems.at[k],
                        device_id=(x, y, 1 - c), device_id_type=MESH)
            pairs.append((pltpu.make_async_remote_copy(src_ref=mine, dst_ref=mine, **sems),
                          pltpu.make_async_remote_copy(src_ref=mine, dst_ref=theirs, **sems)))
    for s, buf in enumerate(shares):
        k = 7 * len(srcs) + 3 * len(fwds) + s
        sems = dict(send_sem=send_sems.at[k], recv_sem=recv_sems.at[k], device_id=(x, y, 1 - c), device_id_type=MESH)
        pairs.append((pltpu.make_async_remote_copy(src_ref=buf.at[c], dst_ref=buf.at[c], **sems),
                      pltpu.make_async_remote_copy(src_ref=buf.at[c], dst_ref=buf.at[1 - c], **sems)))
    return pairs


def _gather_start(blocks, forward, shares, after, name):
    n, n_fwd = len(blocks), len(forward)
    n_sems = 7 * n + 3 * n_fwd + len(shares)
    n_bufs = 2 * n + n_fwd + len(shares)
    hbm, sem = pl.BlockSpec(memory_space=pltpu.HBM), pl.BlockSpec(memory_space=pltpu.SEMAPHORE)

    def body(*refs):
        srcs, lands, fwds, swaps = refs[:n], refs[n:2 * n], refs[2 * n:2 * n + n_fwd], refs[2 * n + n_fwd:n_bufs]
        send_sems, recv_sems = refs[n_bufs + 1:n_bufs + 3]
        token, local_sems = refs[-2:]
        x, y, c, _ = _mesh_place()
        own = [pltpu.make_async_copy(src, land.at[4 * x + 2 * y + c], local_sems.at[a])
               for a, (src, land) in enumerate(zip(srcs, lands))]
        for cp in own:
            cp.start()
        for send, _ in _split_gather_copies(srcs, lands, fwds, swaps, send_sems, recv_sems):
            send.start()
        token[...] = jnp.zeros_like(token)
        for cp in own:
            cp.wait()

    lands = [lax.empty((N_DEV,) + b.shape, b.dtype) for b in blocks]
    operands = [pltpu.with_memory_space_constraint(a, pltpu.HBM)
                for a in list(blocks) + lands + list(forward) + list(shares)]
    return pl.pallas_call(
        body, name=name,
        out_shape=[pltpu.SemaphoreType.DMA((n_sems,)), pltpu.SemaphoreType.DMA((n_sems,))]
        + [pltpu.HBM(a.shape, a.dtype) for a in operands] + [jax.ShapeDtypeStruct((8, LANES), F32)],
        in_specs=[hbm] * len(operands) + [_any()],
        out_specs=[sem, sem] + [hbm] * len(operands) + [pl.BlockSpec(memory_space=pltpu.VMEM)],
        input_output_aliases={i: 2 + i for i in range(len(operands))},
        scratch_shapes=[pltpu.SemaphoreType.DMA((n,))],
        compiler_params=pltpu.CompilerParams(has_side_effects=pltpu.SideEffectType.DATAFLOW_SIDE_EFFECTING),
    )(*operands, after)


def _gather_wait(started, n, n_shares, after, name):
    send_sems, recv_sems, *bufs, _ = started
    n_bufs = len(bufs)
    n_fwd = n_bufs - 2 * n - n_shares
    hbm, sem = pl.BlockSpec(memory_space=pltpu.HBM), pl.BlockSpec(memory_space=pltpu.SEMAPHORE)

    def body(*refs):
        srcs, lands, fwds, swaps = refs[:n], refs[n:2 * n], refs[2 * n:2 * n + n_fwd], refs[2 * n + n_fwd:n_bufs]
        send_sems, recv_sems = refs[n_bufs:n_bufs + 2]
        for send, arrival in _split_gather_copies(srcs, lands, fwds, swaps, send_sems, recv_sems):
            send.wait_send()
            arrival.wait_recv()

    out = pl.pallas_call(
        body, name=name, out_shape=[pltpu.HBM(b.shape, b.dtype) for b in bufs],
        in_specs=[hbm] * len(bufs) + [sem, sem, _any()], out_specs=[hbm] * len(bufs),
        input_output_aliases={i: i for i in range(len(bufs))},
        compiler_params=pltpu.CompilerParams(has_side_effects=pltpu.SideEffectType.DATAFLOW_SIDE_EFFECTING),
    )(*bufs, send_sems, recv_sems, after)
    return out[n:]


def _share_job(bufs):
    def make(ins, outs, send_sems, recv_sems):
        del ins
        x, y, c, _ = _mesh_place()
        sems = lambda a: dict(send_sem=send_sems.at[a], recv_sem=recv_sems.at[a],
                              device_id=(x, y, 1 - c), device_id_type=MESH)
        sends = [pltpu.make_async_remote_copy(src_ref=o.at[c], dst_ref=o.at[c], **sems(a)) for a, o in enumerate(outs)]
        arrivals = [pltpu.make_async_remote_copy(src_ref=o.at[c], dst_ref=o.at[1 - c], **sems(a))
                    for a, o in enumerate(outs)]
        return sends, [s.wait_send for s in sends] + [r.wait_recv for r in arrivals]

    shapes = tuple(jax.ShapeDtypeStruct(b.shape, b.dtype) for b in bufs)
    return _Comm(tuple(bufs), shapes, len(bufs), make, in_place=len(bufs))


def _gelu_tanh(z):
    k = math.sqrt(2.0 / math.pi)
    t = jnp.tanh(k * (z + 0.044715 * (z * z * z)))
    return 0.5 * z * (1.0 + t), t


def _gelu_tanh_grad(z, t):
    k = math.sqrt(2.0 / math.pi)
    return 0.5 * (1.0 + t) + 0.5 * z * (1.0 - t * t) * (k * (1.0 + 3.0 * 0.044715 * (z * z)))


def _rope_angle_kernel(pos_row, invf_col):
    seq = pos_row.shape[1]

    def body(p_ref, f_ref, cos_ref, sin_ref):
        ang = p_ref[...].astype(F32) * f_ref[...]
        cos_ref[...] = jnp.cos(ang)
        sin_ref[...] = jnp.sin(ang)

    return pl.pallas_call(
        body, name="rope_angles", grid=(1,), out_shape=[jax.ShapeDtypeStruct((ROT_DIM // 2, seq), F32)] * 2,
        in_specs=[_full((1, seq)), _full((ROT_DIM // 2, 1))], out_specs=[_full((ROT_DIM // 2, seq))] * 2,
        compiler_params=_params("arbitrary"),
    )(pos_row, invf_col)


def _rope_lane_tables(cos, sin):
    cos_t, sin_t = cos.T, sin.T
    seq, half = cos_t.shape
    ones = jnp.ones((seq, HEAD_DIM - ROT_DIM), F32)
    c64 = jnp.concatenate([cos_t, cos_t, ones], axis=1)
    s1 = jnp.concatenate([sin_t, jnp.zeros((seq, HEAD_DIM - half), F32)], axis=1)
    s2 = jnp.concatenate([jnp.zeros((seq, half), F32), sin_t, jnp.zeros((seq, HEAD_DIM - ROT_DIM), F32)], axis=1)
    return jnp.concatenate([jnp.tile(t, (1, LANES // HEAD_DIM)) for t in (c64, s1, s2)], axis=1)


def _rope_apply(t, tab, sign):
    reps = t.shape[1] // LANES
    c_tab, s1, s2 = (jnp.tile(tab[:, LANES * k:LANES * (k + 1)], (1, reps)) if reps > 1
                     else tab[:, LANES * k:LANES * (k + 1)] for k in range(3))
    half = ROT_DIM // 2
    up = pltpu.roll(t, t.shape[1] - half, 1)
    down = pltpu.roll(t, half, 1)
    return t * c_tab + sign * (down * s2 - up * s1)


def _lane_masks(shape):
    lane = lax.broadcasted_iota(jnp.int32, shape, 1)
    return lane < HEAD_DIM, lane >= HEAD_DIM


HEADS_PER_GROUP = N_Q_HEADS // N_KV_HEADS
ATTN_SCALE = 1.0 / math.sqrt(HEAD_DIM)


def _attn_bias_t(first_block):
    kj = lax.broadcasted_iota(jnp.int32, (2 * CHUNK, CHUNK), 0)
    qi = lax.broadcasted_iota(jnp.int32, (2 * CHUNK, CHUNK), 1)
    ok = (kj > qi) & (kj <= qi + CHUNK)
    if first_block is not None:
        ok = ok & (jnp.logical_not(first_block) | (kj >= CHUNK))
    return jnp.tile(jnp.where(ok, 0.0, -jnp.inf), (1, HEADS_PER_GROUP))


def _group_rows(x, g, lo, hi):
    rows = []
    for r in range(HEADS_PER_GROUP):
        h = HEADS_PER_GROUP * g + r
        pair = x[:, LANES * (h // 2):LANES * (h // 2 + 1)]
        rows.append(jnp.where(hi if h % 2 else lo, pair, 0.0))
    return jnp.concatenate(rows, axis=0)


def _pairs_from_rows(rows, lo):
    return [jnp.where(lo, rows[2 * CHUNK * k:2 * CHUNK * k + CHUNK], rows[2 * CHUNK * k + CHUNK:2 * CHUNK * (k + 1)])
            for k in range(HEADS_PER_GROUP // 2)]


def _group_dup(a, b, g, lo2):
    return jnp.where(lo2, a, b) if g == 0 else jnp.where(lo2, b, a)


def _sink_row(sink_ref, g):
    return jnp.concatenate([sink_ref[HEADS_PER_GROUP * g + r:HEADS_PER_GROUP * g + r + 1, :]
                            for r in range(HEADS_PER_GROUP)], axis=1)


def _attn_probs_t(k_dup, q_rows, bias_t, sink_row):
    s_t = _dot_nt(k_dup, q_rows) * ATTN_SCALE + bias_t
    m = jnp.maximum(jnp.max(s_t, axis=0, keepdims=True), sink_row)
    p = jnp.exp(s_t - m)
    e_sink = jnp.exp(sink_row - m)
    inv = 1.0 / (jnp.sum(p, axis=0, keepdims=True) + e_sink)
    return p * inv, e_sink * inv


def _sgu_forward_pair(wm, vp, j):
    lo, hi = _lane_masks(vp.shape)
    lhs = jnp.concatenate([wm[2 * j], wm[2 * j + 1]], axis=1)
    rhs = jnp.concatenate([jnp.where(lo, vp, 0.0), jnp.where(hi, vp, 0.0)], axis=0)
    return _dot(lhs, rhs)


def _masked_spatial(w_ref):
    t = lax.broadcasted_iota(jnp.int32, (CHUNK, CHUNK), 0)
    s = lax.broadcasted_iota(jnp.int32, (CHUNK, CHUNK), 1)
    tril = s <= t
    return [jnp.where(tril, w_ref[g], 0.0) for g in range(GMLP_GROUPS)], tril, s >= t


def _mod_kernel(c_all, w_shard, b_shard, comm=None):
    n = w_shard.shape[1]
    tn = 512

    def body(c_ref, w_ref, b_ref, mod_ref, act_ref):
        cv = c_ref[...]
        act = cv * (1.0 / (1.0 + jnp.exp(-cv)))
        act_ref[...] = act
        mod_ref[...] = _dot(act, w_ref[...]) + b_ref[...]

    return _hosted_call(
        body, comm, name="ada_mod", grid=(n // tn,),
        out_shape=[jax.ShapeDtypeStruct((N_DEV, n), F32), jax.ShapeDtypeStruct((N_DEV, D_MODEL), F32)],
        in_specs=[_full((N_DEV, D_MODEL)), pl.BlockSpec((D_MODEL, tn), lambda i: (0, i)),
                  pl.BlockSpec((1, tn), lambda i: (0, i))],
        out_specs=[pl.BlockSpec((N_DEV, tn), lambda i: (0, i)), _full((N_DEV, D_MODEL))],
        semantics=("arbitrary",),
    )(c_all, w_shard, b_shard)


def _load_chip_blocks(chip_ref, gathered, local, dsts, sems, first_sem=0):
    for k, dst in enumerate(dsts):
        @pl.when(chip_ref[0] == k)
        def _():
            pltpu.make_async_copy(local, dst, sems.at[first_sem + k]).start()

        @pl.when(chip_ref[0] != k)
        def _():
            pltpu.make_async_copy(gathered.at[k], dst, sems.at[first_sem + k]).start()
    return [pltpu.make_async_copy(local, dst, sems.at[first_sem + k]).wait for k, dst in enumerate(dsts)]


def _in_proj_kernel(x, vecs, w_in_t, comm=None):
    seq = x.shape[0]
    tm = 512

    def body(x_ref, v_ref, w_ref, proj_ref, h_ref):
        xv = x_ref[...]
        rstd = lax.rsqrt(_mean_last(xv * xv) + EPS)
        n1 = (xv * rstd) * v_ref[0:1, :]
        h = n1 * (1.0 + v_ref[2:3, :]) + v_ref[1:2, :]
        hb = h.astype(MXU_DTYPE)
        h_ref[...] = hb
        proj_ref[...] = _dot_nt(hb, w_ref[...])

    return _hosted_call(
        body, comm, name="in_proj", grid=(seq // tm,),
        out_shape=[jax.ShapeDtypeStruct((seq, IN_PROJ_WIDTH), F32),
                   jax.ShapeDtypeStruct((seq, D_MODEL), MXU_DTYPE)],
        in_specs=[pl.BlockSpec((tm, D_MODEL), lambda i: (i, 0)), _full((8, D_MODEL)),
                  _full((IN_PROJ_WIDTH, D_MODEL))],
        out_specs=[pl.BlockSpec((tm, IN_PROJ_WIDTH), lambda i: (i, 0)),
                   pl.BlockSpec((tm, D_MODEL), lambda i: (i, 0))],
        semantics=("arbitrary",),
    )(x, vecs, w_in_t)


MIXER_BLOCKS_PER_STEP = 4
KV_START = 2 * GMLP_WIDTH + ATTN_WIDTH


def _mixer_fwd_kernel(proj, rope_tab, w_spatial, bias_full, sink_rows, comm=None):
    seq = proj.shape[0]
    per = MIXER_BLOCKS_PER_STEP
    steps = seq // (CHUNK * per)
    kv_col = KV_START // (2 * KV_WIDTH)

    def body(proj_ref, prev_ref, tab_ref, ptab_ref, w_ref, bias_ref, sink_ref, cat_ref):
        i = pl.program_id(0)
        wm, _, _ = _masked_spatial(w_ref)
        lo, hi = _lane_masks((CHUNK, LANES))
        lo2, _ = _lane_masks((2 * CHUNK, LANES))
        o = 2 * GMLP_WIDTH
        for s in range(per):
            rows, before = slice(CHUNK * s, CHUNK * (s + 1)), slice(CHUNK * (s - 1), CHUNK * s)
            for j in range(GMLP_GROUPS // 2):
                cols = slice(LANES * j, LANES * (j + 1))
                vcols = slice(GMLP_WIDTH + LANES * j, GMLP_WIDTH + LANES * (j + 1))
                u, _ = _gelu_tanh(proj_ref[rows, cols])
                vp, _ = _gelu_tanh(proj_ref[rows, vcols])
                sv = _sgu_forward_pair(wm, vp, j) + bias_ref[:, cols]
                cat_ref[rows, cols] = (u * sv).astype(cat_ref.dtype)
            tab = tab_ref[rows, :]
            if s == 0:
                prev_kv, prev_tab, first = prev_ref[...], ptab_ref[...], i == 0
            else:
                prev_kv, prev_tab, first = proj_ref[before, KV_START:KV_START + 2 * KV_WIDTH], tab_ref[before, :], None
            q_r = _rope_apply(proj_ref[rows, o:o + ATTN_WIDTH], tab, 1.0)
            k_cur = _rope_apply(proj_ref[rows, KV_START:KV_START + KV_WIDTH], tab, 1.0)
            k_prev = _rope_apply(prev_kv[:, 0:KV_WIDTH], prev_tab, 1.0)
            k_a = jnp.concatenate([k_prev, k_cur], axis=0)
            v_a = jnp.concatenate([prev_kv[:, KV_WIDTH:2 * KV_WIDTH],
                                   proj_ref[rows, KV_START + KV_WIDTH:KV_START + 2 * KV_WIDTH]], axis=0)
            k_b = pltpu.roll(k_a, HEAD_DIM, 1)
            v_b = pltpu.roll(v_a, HEAD_DIM, 1)
            bias_t = _attn_bias_t(first)
            for g in range(N_KV_HEADS):
                p_t, _ = _attn_probs_t(_group_dup(k_a, k_b, g, lo2), _group_rows(q_r, g, lo, hi), bias_t,
                                       _sink_row(sink_ref, g))
                o_t = _dot(_group_dup(v_a, v_b, g, lo2).T, p_t)
                for k, pair in enumerate(_pairs_from_rows(o_t.T, lo)):
                    c0 = GMLP_WIDTH + LANES * (2 * g + k)
                    cat_ref[rows, c0:c0 + LANES] = pair.astype(cat_ref.dtype)

    return _hosted_call(
        body, comm, name="mixer_fwd", grid=(steps,),
        out_shape=[jax.ShapeDtypeStruct((seq, D_MODEL), MXU_DTYPE)],
        in_specs=[pl.BlockSpec((CHUNK * per, IN_PROJ_WIDTH), lambda i: (i, 0)),
                  pl.BlockSpec((CHUNK, 2 * KV_WIDTH), lambda i: (jnp.maximum(per * i - 1, 0), kv_col)),
                  pl.BlockSpec((CHUNK * per, 3 * LANES), lambda i: (i, 0)),
                  pl.BlockSpec((CHUNK, 3 * LANES), lambda i: (jnp.maximum(per * i - 1, 0), 0)),
                  _full((GMLP_GROUPS, CHUNK, CHUNK)), _full((CHUNK, GMLP_WIDTH)),
                  _full((N_Q_HEADS, LANES))],
        out_specs=[pl.BlockSpec((CHUNK * per, D_MODEL), lambda i: (i, 0))],
        semantics=("arbitrary",),
    )(proj, proj, rope_tab, rope_tab, w_spatial, bias_full, sink_rows)


def _trunk_kernel(x, target, cat, vecs, chip_idx, gathered, local):
    seq = x.shape[0]
    tm = 256
    nj = D_FF // D_MODEL
    out_rows = D_MODEL // N_CHIPS

    def body(chip_ref, x_ref, t_ref, cat_ref, v_ref, g_out, g_w1, g_w2, l_out, l_w1, l_w2,
             dx1_ref, dcat_ref, dmix_ref, h2_ref, r_ref, da_ref, dff_ref, sums_ref,
             wout, w1, w2, a_scr, sem):
        i = pl.program_id(0)

        @pl.when(i == 0)
        def _():
            waits = _load_chip_blocks(chip_ref, g_out, l_out,
                                      [wout.at[pl.ds(out_rows * k, out_rows)] for k in range(N_CHIPS)], sem)
            waits += _load_chip_blocks(chip_ref, g_w1, l_w1, [w1.at[k] for k in range(N_CHIPS)], sem, N_CHIPS)
            waits += _load_chip_blocks(chip_ref, g_w2, l_w2, [w2.at[k] for k in range(N_CHIPS)], sem, 2 * N_CHIPS)
            for wait in waits:
                wait()
            sums_ref[...] = jnp.zeros_like(sums_ref)

        gate1, shift2, scale2 = v_ref[0:1, :], v_ref[1:2, :], v_ref[2:3, :]
        gate2, g_ffn, g_final = v_ref[3:4, :], v_ref[4:5, :], v_ref[5:6, :]

        mix = _dot(cat_ref[...], wout[...])
        x1 = x_ref[...] + gate1 * mix
        rstd2 = lax.rsqrt(_mean_last(x1 * x1) + EPS)
        xh2 = x1 * rstd2
        n2 = xh2 * g_ffn
        h2b = (n2 * (1.0 + scale2) + shift2).astype(MXU_DTYPE)
        h2_ref[...] = h2b
        ff = jnp.zeros((tm, D_MODEL), F32)
        for j in range(nj):
            a = _dot(h2b, w1[j])
            a_scr[j] = a
            relu = jnp.maximum(a, 0.0)
            rb = (relu * relu).astype(MXU_DTYPE)
            r_ref[:, D_MODEL * j:D_MODEL * (j + 1)] = rb
            ff = ff + _dot(rb, w2[j])
        x2 = x1 + gate2 * ff
        rstd3 = lax.rsqrt(_mean_last(x2 * x2) + EPS)
        xh3 = x2 * rstd3
        err = xh3 * g_final - t_ref[...]
        loss = 0.5 * _rowsum(_mean_last(err * err))
        dy = err * (1.0 / D_MODEL)
        dxh3 = dy * g_final
        dx2 = rstd3 * (dxh3 - xh3 * _mean_last(dxh3 * xh3))
        dffb = (dx2 * gate2).astype(MXU_DTYPE)
        dff_ref[...] = dffb
        dh2 = jnp.zeros((tm, D_MODEL), F32)
        for j in range(nj):
            dr = _dot_nt(dffb, w2[j])
            dab = (dr * (2.0 * jnp.maximum(a_scr[j], 0.0))).astype(MXU_DTYPE)
            da_ref[:, D_MODEL * j:D_MODEL * (j + 1)] = dab
            dh2 = dh2 + _dot_nt(dab, w1[j])
        dn2 = dh2 * (1.0 + scale2)
        dxh2 = dn2 * g_ffn
        dx1 = dx2 + rstd2 * (dxh2 - xh2 * _mean_last(dxh2 * xh2))
        dx1_ref[...] = dx1
        dmixb = (dx1 * gate1).astype(MXU_DTYPE)
        dmix_ref[...] = dmixb
        dcat_ref[...] = _dot_nt(dmixb, wout[...])

        sums_ref[0:1, :] += _rowsum(dh2)
        sums_ref[1:2, :] += _rowsum(dh2 * n2)
        sums_ref[2:3, :] += _rowsum(dx2 * ff)
        sums_ref[3:4, :] += _rowsum(dn2 * xh2)
        sums_ref[4:5, :] += _rowsum(dy * xh3)
        sums_ref[5:6, :] += _rowsum(dx1 * mix)
        sums_ref[6:7, :] += jnp.broadcast_to(loss, (1, D_MODEL))

    tok = lambda w: pl.BlockSpec((tm, w), lambda i, chip: (i, 0))
    return _hosted_call(
        body, None, name="trunk", grid=(seq // tm,), n_prefetch=1,
        out_shape=[jax.ShapeDtypeStruct((seq, D_MODEL), F32), jax.ShapeDtypeStruct((seq, D_MODEL), F32),
                   jax.ShapeDtypeStruct((seq, D_MODEL), MXU_DTYPE), jax.ShapeDtypeStruct((seq, D_MODEL), MXU_DTYPE),
                   jax.ShapeDtypeStruct((seq, D_FF), MXU_DTYPE), jax.ShapeDtypeStruct((seq, D_FF), MXU_DTYPE),
                   jax.ShapeDtypeStruct((seq, D_MODEL), MXU_DTYPE), jax.ShapeDtypeStruct((8, D_MODEL), F32)],
        in_specs=[tok(D_MODEL), tok(D_MODEL), tok(D_MODEL), _full((8, D_MODEL))] + [_any()] * 6,
        out_specs=[tok(D_MODEL), tok(D_MODEL), tok(D_MODEL), tok(D_MODEL), tok(D_FF), tok(D_FF), tok(D_MODEL),
                   _full((8, D_MODEL))],
        scratch_shapes=[pltpu.VMEM((D_MODEL, D_MODEL), MXU_DTYPE), pltpu.VMEM((nj, D_MODEL, D_MODEL), MXU_DTYPE),
                        pltpu.VMEM((nj, D_MODEL, D_MODEL), MXU_DTYPE), pltpu.VMEM((nj, tm, D_MODEL), F32),
                        pltpu.SemaphoreType.DMA((3 * N_CHIPS,))],
        semantics=("arbitrary",),
    )(chip_idx, x, target, cat, vecs, *gathered, *local)


def _mixer_bwd_kernel(proj, rope_tab, dcat, w_spatial, w_spatial_t, bias_full, sink_rows, dev_idx, comm=None):
    seq = proj.shape[0]
    per = MIXER_BLOCKS_PER_STEP
    steps = seq // (CHUNK * per)
    kv_col = KV_START // (2 * KV_WIDTH)

    def body(dev_ref, proj_ref, prev_ref, tab_ref, ptab_ref, dcat_ref, w_ref, wt_ref, bias_ref, sink_ref,
             dproj_ref, dw_out, db_ref, dsink_ref, carry, dw_ref):
        del dev_ref
        step = pl.program_id(0)

        @pl.when(step == 0)
        def _():
            carry[...] = jnp.zeros_like(carry)
            dw_ref[...] = jnp.zeros_like(dw_ref)
            db_ref[...] = jnp.zeros_like(db_ref)
            dsink_ref[...] = jnp.zeros_like(dsink_ref)

        for s in reversed(range(per)):
            rows = pl.ds(CHUNK * s, CHUNK)
            if s == 0:
                before, before_tab, first = prev_ref, ptab_ref, step == steps - 1
            else:
                before = proj_ref.at[pl.ds(CHUNK * (s - 1), CHUNK), pl.ds(KV_START, 2 * KV_WIDTH)]
                before_tab, first = tab_ref.at[pl.ds(CHUNK * (s - 1), CHUNK)], None
            one_block(proj_ref.at[rows], before, tab_ref.at[rows], before_tab, dcat_ref.at[rows], w_ref, wt_ref,
                      bias_ref, sink_ref, dproj_ref.at[rows], dw_ref, db_ref, dsink_ref, carry, first)

        @pl.when(step == steps - 1)
        def _():
            dw_out[...] = dw_ref[...].astype(dw_out.dtype)

    def one_block(proj_ref, prev_ref, tab_ref, ptab_ref, dcat_ref, w_ref, wt_ref, bias_ref, sink_ref,
                  dproj_ref, dw_ref, db_ref, dsink_ref, carry, first):
        wm, tril, triu = _masked_spatial(w_ref)
        lo, hi = _lane_masks((CHUNK, LANES))
        lane = lax.broadcasted_iota(jnp.int32, (CHUNK, LANES), 1)
        db = jnp.zeros((CHUNK, LANES), F32)
        for j in range(GMLP_GROUPS // 2):
            cols = slice(LANES * j, LANES * (j + 1))
            vcols = slice(GMLP_WIDTH + LANES * j, GMLP_WIDTH + LANES * (j + 1))
            zu, zv = proj_ref[:, cols], proj_ref[:, vcols]
            u, tu = _gelu_tanh(zu)
            vp, tv = _gelu_tanh(zv)
            sv = _sgu_forward_pair(wm, vp, j) + bias_ref[:, cols]
            dout = dcat_ref[:, cols]
            du = dout * sv
            dsv = dout * u
            dsv_lo, dsv_hi = jnp.where(lo, dsv, 0.0), jnp.where(hi, dsv, 0.0)
            lhs_t = jnp.concatenate([jnp.where(triu, wt_ref[2 * j], 0.0),
                                     jnp.where(triu, wt_ref[2 * j + 1], 0.0)], axis=1)
            dv = _dot(lhs_t, jnp.concatenate([dsv_lo, dsv_hi], axis=0))
            dw_ref[2 * j] += jnp.where(tril, _dot_nt(dsv_lo, vp), 0.0)
            dw_ref[2 * j + 1] += jnp.where(tril, _dot_nt(dsv_hi, vp), 0.0)
            db = db + (jnp.where(lane == 2 * j, jnp.sum(dsv_lo, axis=1, keepdims=True), 0.0)
                       + jnp.where(lane == 2 * j + 1, jnp.sum(dsv_hi, axis=1, keepdims=True), 0.0))
            dproj_ref[:, cols] = (du * _gelu_tanh_grad(zu, tu)).astype(dproj_ref.dtype)
            dproj_ref[:, vcols] = (dv * _gelu_tanh_grad(zv, tv)).astype(dproj_ref.dtype)
        db_ref[...] += db
        o = 2 * GMLP_WIDTH
        tab = tab_ref[...]
        q_r = _rope_apply(proj_ref[:, o:o + ATTN_WIDTH], tab, 1.0)
        k_cur = _rope_apply(proj_ref[:, o + ATTN_WIDTH:o + ATTN_WIDTH + KV_WIDTH], tab, 1.0)
        k_prev = _rope_apply(prev_ref[:, 0:KV_WIDTH], ptab_ref[...], 1.0)
        k_a = jnp.concatenate([k_prev, k_cur], axis=0)
        v_a = jnp.concatenate([prev_ref[:, KV_WIDTH:2 * KV_WIDTH],
                               proj_ref[:, o + ATTN_WIDTH + KV_WIDTH:o + ATTN_WIDTH + 2 * KV_WIDTH]], axis=0)
        k_b = pltpu.roll(k_a, HEAD_DIM, 1)
        v_b = pltpu.roll(v_a, HEAD_DIM, 1)
        bias_t = _attn_bias_t(first)
        lo2, _ = _lane_masks((2 * CHUNK, LANES))
        dout_b = dcat_ref[:, GMLP_WIDTH:GMLP_WIDTH + ATTN_WIDTH]
        dk_tot, dv_tot, dq_pairs = [], [], []
        for g in range(N_KV_HEADS):
            k_dup, v_dup = _group_dup(k_a, k_b, g, lo2), _group_dup(v_a, v_b, g, lo2)
            q_rows = _group_rows(q_r, g, lo, hi)
            do_rows = _group_rows(dout_b, g, lo, hi)
            p_t, p_sink = _attn_probs_t(k_dup, q_rows, bias_t, _sink_row(sink_ref, g))
            dp_t = _dot_nt(v_dup, do_rows)
            delta = jnp.sum(p_t * dp_t, axis=0, keepdims=True)
            ds_t = p_t * (dp_t - delta) * ATTN_SCALE
            dsink = -p_sink * delta
            for r in range(HEADS_PER_GROUP):
                h = HEADS_PER_GROUP * g + r
                dsink_ref[h:h + 1, :] += jnp.broadcast_to(
                    jnp.sum(dsink[:, LANES * r:LANES * (r + 1)], axis=1, keepdims=True), (1, LANES))
            dk_full = _dot(ds_t, q_rows)
            dv_full = _dot(p_t, do_rows)
            dk_tot.append(dk_full + pltpu.roll(dk_full, HEAD_DIM, 1))
            dv_tot.append(dv_full + pltpu.roll(dv_full, HEAD_DIM, 1))
            dq_t = _dot(k_dup.T, ds_t)
            dq_pairs += _pairs_from_rows(dq_t.T, lo)
        dk_all = jnp.where(lo2, dk_tot[0], dk_tot[1])
        dv_all = jnp.where(lo2, dv_tot[0], dv_tot[1])
        dk_cur = dk_all[CHUNK:, :] + carry[:, 0:KV_WIDTH]
        dv_cur = dv_all[CHUNK:, :] + carry[:, KV_WIDTH:2 * KV_WIDTH]
        carry[:, 0:KV_WIDTH] = dk_all[:CHUNK, :]
        carry[:, KV_WIDTH:2 * KV_WIDTH] = dv_all[:CHUNK, :]
        dq = _rope_apply(jnp.concatenate(dq_pairs, axis=1), tab, -1.0)
        dproj_ref[:, o:o + ATTN_WIDTH] = dq.astype(dproj_ref.dtype)
        dproj_ref[:, o + ATTN_WIDTH:o + ATTN_WIDTH + KV_WIDTH] = (
            _rope_apply(dk_cur, tab, -1.0).astype(dproj_ref.dtype))
        dproj_ref[:, o + ATTN_WIDTH + KV_WIDTH:o + ATTN_WIDTH + 2 * KV_WIDTH] = dv_cur.astype(dproj_ref.dtype)

    rev = lambda i: steps - 1 - i
    before = lambda i: jnp.maximum(per * rev(i) - 1, 0)
    slot = lambda shape: pl.BlockSpec((None,) + shape, lambda i, d: (d[0],) + (0,) * len(shape))
    return _hosted_call(
        body, comm, name="mixer_bwd", grid=(steps,), n_prefetch=1,
        out_shape=[jax.ShapeDtypeStruct((seq, IN_PROJ_WIDTH), MXU_DTYPE),
                   jax.ShapeDtypeStruct((N_DEV, GMLP_GROUPS, CHUNK, CHUNK), GRAD_COMM_DTYPE),
                   jax.ShapeDtypeStruct((N_DEV, CHUNK, LANES), F32),
                   jax.ShapeDtypeStruct((N_DEV, N_Q_HEADS, LANES), F32)],
        in_specs=[pl.BlockSpec((CHUNK * per, IN_PROJ_WIDTH), lambda i, d: (rev(i), 0)),
                  pl.BlockSpec((CHUNK, 2 * KV_WIDTH), lambda i, d: (before(i), kv_col)),
                  pl.BlockSpec((CHUNK * per, 3 * LANES), lambda i, d: (rev(i), 0)),
                  pl.BlockSpec((CHUNK, 3 * LANES), lambda i, d: (before(i), 0)),
                  pl.BlockSpec((CHUNK * per, D_MODEL), lambda i, d: (rev(i), 0)),
                  _full((GMLP_GROUPS, CHUNK, CHUNK)), _full((GMLP_GROUPS, CHUNK, CHUNK)),
                  _full((CHUNK, GMLP_WIDTH)), _full((N_Q_HEADS, LANES))],
        out_specs=[pl.BlockSpec((CHUNK * per, IN_PROJ_WIDTH), lambda i, d: (rev(i), 0)),
                   slot((GMLP_GROUPS, CHUNK, CHUNK)), slot((CHUNK, LANES)), slot((N_Q_HEADS, LANES))],
        scratch_shapes=[pltpu.VMEM((CHUNK, 2 * KV_WIDTH), F32), pltpu.VMEM((GMLP_GROUPS, CHUNK, CHUNK), F32)],
        semantics=("arbitrary",),
    )(dev_idx, proj, proj, rope_tab, rope_tab, dcat, w_spatial, w_spatial_t, bias_full, sink_rows)


def _in_proj_bwd_kernel(x, dx1, dproj, vecs, w_in_t, comm=None):
    seq = x.shape[0]
    tm = 512

    def body(x_ref, dx1_ref, dp_ref, v_ref, w_ref, gx_ref, sums_ref):
        @pl.when(pl.program_id(0) == 0)
        def _():
            sums_ref[...] = jnp.zeros_like(sums_ref)

        g_mix, scale1 = v_ref[0:1, :], v_ref[2:3, :]
        dh = _dot(dp_ref[...], w_ref[...])
        xv = x_ref[...]
        rstd = lax.rsqrt(_mean_last(xv * xv) + EPS)
        xh = xv * rstd
        dn1 = dh * (1.0 + scale1)
        dxh = dn1 * g_mix
        gx_ref[...] = dx1_ref[...] + rstd * (dxh - xh * _mean_last(dxh * xh))
        sums_ref[0:1, :] += _rowsum(dh)
        sums_ref[1:2, :] += _rowsum(dh * (xh * g_mix))
        sums_ref[2:3, :] += _rowsum(dn1 * xh)

    tok = lambda w: pl.BlockSpec((tm, w), lambda i: (i, 0))
    return _hosted_call(
        body, comm, name="in_proj_bwd", grid=(seq // tm,),
        out_shape=[jax.ShapeDtypeStruct((seq, D_MODEL), F32), jax.ShapeDtypeStruct((8, D_MODEL), F32)],
        in_specs=[tok(D_MODEL), tok(D_MODEL), tok(IN_PROJ_WIDTH), _full((8, D_MODEL)),
                  _full((IN_PROJ_WIDTH, D_MODEL))],
        out_specs=[tok(D_MODEL), _full((8, D_MODEL))],
        semantics=("arbitrary",),
    )(x, dx1, dproj, vecs, w_in_t)


class _GradTiles(NamedTuple):
    tm: int
    tn: int
    n_tiles: int
    chips_per_tile: int
    a_index: Callable
    b_index: Callable


def _weight_grad_kernel(a, b, c_idx, name, tiles, comm=None):
    seq = a.shape[0]
    tk = min(seq, 4096)
    nk = seq // tk
    tm, tn, n_tiles, per = tiles.tm, tiles.tn, tiles.n_tiles, tiles.chips_per_tile
    rows = tm // per

    def half(phase, c):
        return phase * c[0] + (1 - phase) * (1 - c[0])

    def body(c_ref, a_ref, b_ref, o_ref, acc, stage, landed, send_sems, recv_sems):
        del c_ref
        phase, t, kk = pl.program_id(0), pl.program_id(1), pl.program_id(2)
        x, y, c, _ = _mesh_place()

        def copy(tile):
            return pltpu.make_async_remote_copy(
                src_ref=stage.at[tile], dst_ref=landed.at[tile], send_sem=send_sems.at[tile],
                recv_sem=recv_sems.at[tile], device_id=(x, y, 1 - c), device_id_type=MESH)

        @pl.when(kk == 0)
        def _():
            acc[...] = jnp.zeros_like(acc)

        acc[...] += _dot_tn(a_ref[...], b_ref[...])

        @pl.when((kk == nk - 1) & (phase == 0))
        def _():
            stage[t] = acc[...].astype(stage.dtype)
            copy(t).start()

        @pl.when((kk == nk - 1) & (phase == 1))
        def _():
            copy(t).wait_recv()
            total = acc[...] + landed[t].astype(F32)
            for q in range(per):
                o_ref[q] = total[rows * q:rows * (q + 1)].astype(o_ref.dtype)

        @pl.when((kk == nk - 1) & (phase == 1) & (t == n_tiles - 1))
        def _():
            for tile in range(n_tiles):
                copy(tile).wait_send()

    out = _hosted_call(
        body, comm, name=name, grid=(2, n_tiles, nk), n_prefetch=1,
        out_shape=[jax.ShapeDtypeStruct((n_tiles * per, rows, tn), GRAD_COMM_DTYPE)],
        in_specs=[pl.BlockSpec((tk, tm), lambda p, t, k, c: (k, tiles.a_index(t, half(p, c)))),
                  pl.BlockSpec((tk, tn), lambda p, t, k, c: (k, tiles.b_index(t, half(p, c))))],
        out_specs=[pl.BlockSpec((per, rows, tn), lambda p, t, k, c: (p * t, 0, 0))],
        scratch_shapes=[pltpu.VMEM((tm, tn), F32), pltpu.VMEM((n_tiles, tm, tn), GRAD_COMM_DTYPE),
                        pltpu.VMEM((n_tiles, tm, tn), GRAD_COMM_DTYPE),
                        pltpu.SemaphoreType.DMA((n_tiles,)), pltpu.SemaphoreType.DMA((n_tiles,))],
        semantics=("arbitrary", "arbitrary", "arbitrary"),
    )(c_idx, a, b)
    return out[0] if comm is None else out


def _row_tile(rows, most=256, sublanes=16):
    return max(t for t in range(sublanes, most + 1, sublanes) if rows % t == 0)


def _adam_update(w, g, m, v):
    m_new = ADAM_B1 * m + (1.0 - ADAM_B1) * g
    v_new = ADAM_B2 * v + (1.0 - ADAM_B2) * (g * g)
    m_hat = m_new / (1.0 - ADAM_B1 ** ADAM_STEP)
    v_hat = v_new / (1.0 - ADAM_B2 ** ADAM_STEP)
    delta = -ADAM_LR * (m_hat / (jnp.sqrt(v_hat) + ADAM_EPS) + ADAM_WD * w)
    return delta, m_new, v_new


def _sum_chips_kernel(own, others, place, name):
    _, r, n = own.shape
    tr = _row_tile(r)

    def body(place_ref, own_ref, oth_ref, o_ref):
        del place_ref
        acc = own_ref[...].astype(F32)
        for k in range(N_CHIPS - 1):
            acc = acc + oth_ref[k].astype(F32)
        o_ref[...] = acc

    return pl.pallas_call(
        body, name=name, out_shape=jax.ShapeDtypeStruct((2, r, n), F32),
        grid_spec=pltpu.PrefetchScalarGridSpec(
            num_scalar_prefetch=1, grid=(r // tr,),
            in_specs=[pl.BlockSpec((None, tr, n), lambda i, p: (p[0], i, 0)),
                      pl.BlockSpec((N_CHIPS - 1, tr, n), lambda i, p: (0, i, 0))],
            out_specs=pl.BlockSpec((None, tr, n), lambda i, p: (p[1], i, 0))),
        compiler_params=_params("parallel"),
    )(place, own, others)


def _adam_kernel(w, g, m, v, name, after=()):
    r, n = w.shape
    by_columns = g.shape[1] == r
    tr, tn = _row_tile(g.shape[1], most=512), g.shape[2]

    def body(w_ref, g_ref, m_ref, v_ref, *rest):
        g_out, d_ref, mo_ref, vo_ref = rest[len(after):]
        gv = g_ref[...]
        g_out[...] = gv
        d_ref[...], mo_ref[...], vo_ref[...] = _adam_update(w_ref[...], gv, m_ref[...], v_ref[...])

    steps = g.shape[1] // tr
    spec = pl.BlockSpec((tr, tn), (lambda h, i: (i, h)) if by_columns else (lambda h, i: (h * steps + i, 0)))
    return pl.pallas_call(
        body, name=name, grid=(2, steps), out_shape=[jax.ShapeDtypeStruct((r, n), F32)] * 4,
        in_specs=[spec, pl.BlockSpec((None, tr, tn), lambda h, i: (h, i, 0)), spec, spec] + [_any()] * len(after),
        out_specs=[spec] * 4, compiler_params=_params("parallel", "parallel"),
    )(w, g, m, v, *after)


SMALL_PARAMS = ("b_ada", "g_mix", "g_ffn", "g_final", "b_spatial", "sinks", "w_spatial")


def _small_update_kernel(gathered, params):
    shapes = [params[nm][0].shape for nm in SMALL_PARAMS]

    def body(*refs):
        g_refs, refs = refs[:5], refs[5:]
        p_refs, refs = refs[:3 * len(SMALL_PARAMS)], refs[3 * len(SMALL_PARAMS):]
        loss_ref, o_refs = refs[0], refs[1:]

        def total(ref):
            acc = ref[0].astype(F32)
            for k in range(1, N_DEV):
                acc = acc + ref[k].astype(F32)
            return acc

        s1, s2, db, ds, dw = (total(r) for r in g_refs)
        loss_ref[...] = jnp.broadcast_to(s2[6:7, 0:1], loss_ref.shape)
        grads = {"b_ada": [s1[0:1], s1[1:2], s2[5:6], s2[0:1], s2[1:2], s2[2:3]], "g_mix": [s1[2:3]],
                 "g_ffn": [s2[3:4]], "g_final": [s2[4:5]], "b_spatial": [db.T[0:GMLP_GROUPS]],
                 "w_spatial": [dw]}
        lane = lax.broadcasted_iota(jnp.int32, (1, LANES), 1)
        sink_row = jnp.zeros((1, LANES), F32)
        for h in range(N_Q_HEADS):
            sink_row = sink_row + jnp.where(lane == h, ds[h:h + 1, :], 0.0)
        grads["sinks"] = [sink_row[:, 0:N_Q_HEADS]]
        for i, nm in enumerate(SMALL_PARAMS):
            w_ref, m_ref, v_ref = p_refs[3 * i:3 * i + 3]
            outs = o_refs[4 * i:4 * i + 4]
            width = grads[nm][0].shape[1]
            for k, g in enumerate(grads[nm]):
                cols = slice(width * k, width * (k + 1))
                upd = _adam_update(w_ref[:, cols], g, m_ref[:, cols], v_ref[:, cols])
                for o_ref, val in zip(outs, (g,) + upd):
                    o_ref[:, cols] = val

    flat = [a for nm in SMALL_PARAMS for a in params[nm]]
    out_shape = [jax.ShapeDtypeStruct((8, LANES), F32)]
    out_shape += [jax.ShapeDtypeStruct(s, F32) for s in shapes for _ in range(4)]
    outs = pl.pallas_call(
        body, name="small_update", grid=(1,), out_shape=out_shape,
        in_specs=[_full(g.shape) for g in gathered] + [_full(a.shape) for a in flat],
        out_specs=[_full(s.shape) for s in out_shape],
        compiler_params=_params("arbitrary"),
    )(*gathered, *flat)
    return {nm: outs[1 + 4 * i:5 + 4 * i] for i, nm in enumerate(SMALL_PARAMS)}, outs[0]


def _ada_update_kernel(act_t, dmod, w, m, v):
    r, n = w.shape
    tr = 256

    def body(a_ref, d_ref, w_ref, m_ref, v_ref, g_ref, dl_ref, mo_ref, vo_ref):
        g = _dot(a_ref[...], d_ref[...])
        g_ref[...] = g
        dl_ref[...], mo_ref[...], vo_ref[...] = _adam_update(w_ref[...], g, m_ref[...], v_ref[...])

    spec = pl.BlockSpec((tr, n), lambda i: (i, 0))
    return pl.pallas_call(
        body, name="ada_update", grid=(r // tr,), out_shape=[jax.ShapeDtypeStruct((r, n), F32)] * 4,
        in_specs=[pl.BlockSpec((tr, N_DEV), lambda i: (i, 0)), _full((N_DEV, n)), spec, spec, spec],
        out_specs=[spec] * 4, compiler_params=_params("parallel"),
    )(act_t, dmod, w, m, v)


def kernel(x, c, positions, w_ada, b_ada, g_mix, w_in, w_spatial, b_spatial, sinks, w_out, g_ffn, w_ff1, w_ff2, g_final, loss_target, m_w_ada, m_b_ada, m_g_mix, m_w_in, m_w_spatial, m_b_spatial, m_sinks, m_w_out, m_g_ffn, m_w_ff1, m_w_ff2, m_g_final, v_w_ada, v_b_ada, v_g_mix, v_w_in, v_w_spatial, v_b_spatial, v_sinks, v_w_out, v_g_ffn, v_w_ff1, v_w_ff2, v_g_final):
    xi, yi, ci = lax.axis_index("x"), lax.axis_index("y"), lax.axis_index("c")
    chip = 2 * xi + yi
    dev = 2 * chip + ci
    seq = x.shape[1]
    x2, tgt = x[0], loss_target[0]
    ada_cols = w_ada.shape[2]

    big = {"w_in": tuple(a[0].T for a in (w_in, m_w_in, v_w_in)),
           "w_out": (w_out[0], m_w_out[0], v_w_out[0]), "w_ff1": (w_ff1[0], m_w_ff1[0], v_w_ff1[0]),
           "w_ff2": (w_ff2[0], m_w_ff2[0], v_w_ff2[0])}

    def halves(nm, zero=None):
        r, n = big[nm][0].shape
        w = big[nm][0] if zero is None else big[nm][0] + zero
        return w.astype(WEIGHT_COMM_DTYPE).reshape(2, r // 2, n)

    chip_idx = chip.reshape(1).astype(jnp.int32)
    first_c, first_w = ([c], [False], [True]), ([halves("w_in"), halves("w_out")], [True, True], [True, False])
    c_started, w_started = _first_stage_start([first_c, first_w], "gather_first_start")
    zero = c_started[-1][0, 0]
    trunk_weights = ["w_out", "w_ff1", "w_ff2"]
    shards = [halves("w_out"), halves("w_ff1", zero), halves("w_ff2", zero)]
    inv_freq = ROPE_THETA ** (-jnp.arange(0, ROT_DIM, 2, dtype=F32) / ROT_DIM) + zero
    rope_tab = _rope_lane_tables(*_rope_angle_kernel(positions, inv_freq.reshape(ROT_DIM // 2, 1)))
    c_all, = _first_stage_finish(c_started, *first_c[1:], [rope_tab], "gather_c_wait")
    b_shard = lax.dynamic_slice(b_ada, (0, chip * ada_cols), (1, ada_cols))
    mod_part, act = _mod_kernel(c_all.reshape(N_DEV, D_MODEL), w_ada[0], b_shard)
    w_in_t, g_out = _first_stage_finish(w_started, *first_w[1:], [mod_part] + shards[1:], "gather_first_wait")
    w_in_t, g_out = _gather_forward([w_in_t, g_out], "gather_first_forward")
    w_in_t = w_in_t.reshape(IN_PROJ_WIDTH, D_MODEL)
    mod_all, = _all_gather8([mod_part], "gather_mod")
    mod_me = lax.dynamic_index_in_dim(mod_all[0::2], dev, axis=1, keepdims=False)
    mod_me = mod_me.reshape(N_MOD, D_MODEL)
    shift1, scale1, gate1, shift2, scale2, gate2 = (mod_me[k:k + 1] for k in range(N_MOD))

    zeros_row = jnp.zeros((1, D_MODEL), F32)
    vecs1 = jnp.concatenate([g_mix, shift1, scale1] + [zeros_row] * 5, axis=0)
    vecs2 = jnp.concatenate([gate1, shift2, scale2, gate2, g_ffn, g_final.reshape(1, D_MODEL)]
                            + [zeros_row] * 2, axis=0)
    bias_full = jnp.repeat(b_spatial[0].T, HEAD_DIM, axis=1)
    sink_rows = jnp.broadcast_to(sinks[0][:, None], (N_Q_HEADS, LANES))

    proj, hb, *staged = _in_proj_kernel(x2, vecs1, w_in_t, comm=_gather2d_first(shards[1:]))
    cat, *staged = _mixer_fwd_kernel(proj, rope_tab, w_spatial[0], bias_full, sink_rows,
                                     comm=_gather2d_second(staged, shards[1:]))
    staged = [g_out] + list(_gather_forward(staged, "gather_forward"))
    dx1, dcat, dmix, h2b, rb, dab, dffb, sums2 = _trunk_kernel(
        x2, tgt, cat, vecs2, chip_idx,
        [g.reshape((N_CHIPS,) + big[nm][0].shape) for nm, g in zip(trunk_weights, staged)],
        [s.reshape(big[nm][0].shape) for nm, s in zip(trunk_weights, shards)])

    c_idx = ci.reshape(1).astype(jnp.int32)
    place = jnp.stack([chip, ci]).astype(jnp.int32)
    half_d = D_MODEL // 2
    cs_ff2 = _weight_grad_kernel(rb, dffb, c_idx, "dw_ff2",
                                 _GradTiles(D_MODEL, half_d, N_CHIPS, 1, lambda t, h: t, lambda t, h: h))
    eighth = D_MODEL // 8
    cs_ff1, sc_ff2 = _weight_grad_kernel(
        h2b, dab, c_idx, "dw_ff1",
        _GradTiles(D_MODEL, half_d, N_CHIPS, 1, lambda t, h: 0, lambda t, h: 2 * t + h),
        comm=_scatter_job([cs_ff2], rows=(0, 6 * eighth)))
    cs_out, sc_ff2 = _weight_grad_kernel(
        cat, dmix, c_idx, "dw_out", _GradTiles(D_MODEL, half_d, 1, N_CHIPS, lambda t, h: 0, lambda t, h: h),
        comm=_scatter_job([cs_ff2], rows=(6 * eighth, eighth), into=[sc_ff2]))
    dproj, dw_spatial, db_lanes, dsink_rows, sc_ff2, sc_ff1, sc_out = _mixer_bwd_kernel(
        proj, rope_tab, dcat, w_spatial[0], w_spatial[0].transpose(0, 2, 1), bias_full, sink_rows,
        dev.reshape(1).astype(jnp.int32),
        comm=_merge_jobs(_scatter_job([cs_ff1, cs_out]),
                         _scatter_job([cs_ff2], rows=(7 * eighth, eighth), into=[sc_ff2])))
    totals = [_sum_chips_kernel(own, oth, place, "grad_sum_" + nm)
              for nm, own, oth in (("w_out", cs_out, sc_out), ("w_ff1", cs_ff1, sc_ff1), ("w_ff2", cs_ff2, sc_ff2))]
    small_slots = [db_lanes, dsink_rows, dw_spatial.reshape(N_DEV, GMLP_GROUPS * CHUNK, CHUNK)]
    cs_in, *rode = _weight_grad_kernel(
        dproj, hb, c_idx, "dw_in",
        _GradTiles(2 * W_IN_BLOCK, half_d, N_CHIPS // 2, 2, lambda t, h: t, lambda t, h: h),
        comm=_merge_jobs(_gather_job(small_slots), _share_job(totals)))
    small_stage1, shared = rode[:len(small_slots)], rode[len(small_slots):]
    scatter_in = _scatter_start(cs_in, "grad_to_chips_w_in_start")
    grad_x, sums1 = _in_proj_bwd_kernel(x2, dx1, dproj, vecs1 + scatter_in[-1][0:1, 0:1], w_in_t)
    cs_in, sc_in = _scatter_wait(scatter_in, sums1, "grad_to_chips_w_in_wait")
    total_in = _sum_chips_kernel(cs_in, sc_in, place, "grad_sum_w_in")
    gather_small = _gather_start([sums1, sums2], small_stage1, [total_in], cs_in, "gather_small_start")
    big_out = {}

    def update(nm, g, after=()):
        w, m, v = big[nm]
        outs = _adam_kernel(w, g, m, v, "adam_" + nm, after=after)
        big_out[nm] = tuple((t.T if nm == "w_in" else t)[None] for t in outs)
        return outs

    update("w_out", shared[0])
    covering = update("w_ff1", shared[1], after=gather_small[-1:])
    update("w_ff2", shared[2])
    *gathered, shared_in = _gather_wait(gather_small, 2, 1, covering[0], "gather_small_wait")
    update("w_in", shared_in)

    small = {"b_ada": (b_ada, m_b_ada, v_b_ada), "g_mix": (g_mix, m_g_mix, v_g_mix),
             "g_ffn": (g_ffn, m_g_ffn, v_g_ffn), "g_final": (g_final, m_g_final, v_g_final),
             "b_spatial": (b_spatial, m_b_spatial, v_b_spatial), "sinks": (sinks, m_sinks, v_sinks),
             "w_spatial": (w_spatial, m_w_spatial, v_w_spatial)}
    flat_shape = {"g_final": (1, D_MODEL), "b_spatial": (GMLP_GROUPS, CHUNK), "w_spatial": (GMLP_GROUPS * CHUNK, CHUNK)}
    small_out, loss_tile = _small_update_kernel(
        gathered, {nm: tuple(a.reshape(flat_shape.get(nm, a.shape)) for a in small[nm]) for nm in small})
    small_out = {nm: [o.reshape(small[nm][0].shape) for o in small_out[nm]] for nm in small}
    loss = loss_tile[0, 0]

    g1, g2 = gathered[0], gathered[1]
    dmod_all = jnp.concatenate([g1[:, 0], g1[:, 1], g2[:, 5], g2[:, 0], g2[:, 1], g2[:, 2]], axis=1)
    dmod_cols = lax.dynamic_slice(dmod_all, (0, chip * ada_cols), (N_DEV, ada_cols))
    ada = _ada_update_kernel(act.T, dmod_cols, w_ada[0], m_w_ada[0], v_w_ada[0])
    big_out["w_ada"] = tuple(t[None] for t in ada)

    order = ["w_ada", "b_ada", "g_mix", "w_in", "w_spatial", "b_spatial", "sinks", "w_out", "g_ffn",
             "w_ff1", "w_ff2", "g_final"]

    def leaf(nm, k):
        return big_out[nm][k] if nm in big_out else small_out[nm][k]

    outs = [loss, grad_x[None]]
    for k in range(4):
        outs += [leaf(nm, k) for nm in order]
    return tuple(outs)
```

```python
import math
from typing import Callable, NamedTuple

import jax
import jax.numpy as jnp
from jax import lax
from jax.experimental import pallas as pl
from jax.experimental.pallas import tpu as pltpu

F32 = jnp.float32
MXU_DTYPE = jnp.bfloat16
WEIGHT_COMM_DTYPE = jnp.bfloat16
GRAD_COMM_DTYPE = jnp.bfloat16

D_MODEL = 1024
D_FF = 4096
HEAD_DIM = 64
GMLP_GROUPS = 8
GMLP_WIDTH = 512
CHUNK = 128
N_Q_HEADS = 8
N_KV_HEADS = 2
ATTN_WIDTH = 512
KV_WIDTH = 128
ROT_DIM = 16
ROPE_THETA = 500000.0
IN_PROJ_WIDTH = 1792
N_MOD = 6
EPS = 1e-5
N_CHIPS = 4
N_DEV = 8
LANES = 128
W_IN_BLOCK = IN_PROJ_WIDTH // N_CHIPS

ADAM_LR = 0.001
ADAM_B1 = 0.9
ADAM_B2 = 0.999
ADAM_EPS = 1e-08
ADAM_WD = 0.01
ADAM_STEP = 10

VMEM_LIMIT_BYTES = 58 * 1024 * 1024
MESH = pl.DeviceIdType.MESH


def _params(*semantics):
    return pltpu.CompilerParams(dimension_semantics=semantics, vmem_limit_bytes=VMEM_LIMIT_BYTES)


def _dot(a, b):
    return jnp.dot(a.astype(MXU_DTYPE), b.astype(MXU_DTYPE), preferred_element_type=F32)


def _dot_nt(a, b):
    return lax.dot_general(a.astype(MXU_DTYPE), b.astype(MXU_DTYPE), (((1,), (1,)), ((), ())),
                           preferred_element_type=F32)


def _dot_tn(a, b):
    return lax.dot_general(a.astype(MXU_DTYPE), b.astype(MXU_DTYPE), (((0,), (0,)), ((), ())),
                           preferred_element_type=F32)


def _full(shape):
    return pl.BlockSpec(shape, lambda *_: (0,) * len(shape))


def _any():
    return pl.BlockSpec(memory_space=pl.ANY)


def _rowsum(v):
    return jnp.sum(v, axis=0, keepdims=True)


def _mean_last(v):
    return jnp.mean(v, axis=-1, keepdims=True)


class _Comm(NamedTuple):
    operands: tuple
    out_shapes: tuple
    n_sems: int
    make: Callable
    in_place: int = 0


def _hosted_call(body, comm, *, name, grid, in_specs, out_shape, out_specs, scratch_shapes=(), semantics,
                 n_prefetch=0):
    if comm is None:
        return pl.pallas_call(
            body, name=name, out_shape=out_shape, compiler_params=_params(*semantics),
            grid_spec=pltpu.PrefetchScalarGridSpec(
                num_scalar_prefetch=n_prefetch, grid=grid, in_specs=in_specs, out_specs=out_specs,
                scratch_shapes=list(scratch_shapes)))
    n_in, n_out, n_scr = len(in_specs), len(out_shape), len(scratch_shapes)
    k_in, k_out = len(comm.operands), len(comm.out_shapes)

    def hosted(*refs):
        prefetched, refs = refs[:n_prefetch], refs[n_prefetch:]
        ins, refs = refs[:n_in], refs[n_in:]
        c_ins, refs = refs[:k_in], refs[k_in:]
        outs, refs = refs[:n_out], refs[n_out:]
        c_outs, refs = refs[:k_out], refs[k_out:]
        scratch, (send_sems, recv_sems) = refs[:n_scr], refs[n_scr:]
        first, last = None, None
        for d, size in enumerate(grid):
            at_start, at_end = pl.program_id(d) == 0, pl.program_id(d) == size - 1
            first = at_start if first is None else first & at_start
            last = at_end if last is None else last & at_end

        @pl.when(first)
        def _():
            for cp in comm.make(c_ins, c_outs, send_sems, recv_sems)[0]:
                cp.start()

        body(*prefetched, *ins, *outs, *scratch)

        @pl.when(last)
        def _():
            for wait in comm.make(c_ins, c_outs, send_sems, recv_sems)[1]:
                wait()

    aliases = {n_prefetch + n_in + i: n_out + i for i in range(comm.in_place)}
    call = pl.pallas_call(
        hosted, name=name, out_shape=list(out_shape) + list(comm.out_shapes),
        compiler_params=_params(*semantics), input_output_aliases=aliases,
        grid_spec=pltpu.PrefetchScalarGridSpec(
            num_scalar_prefetch=n_prefetch, grid=grid, in_specs=list(in_specs) + [_any()] * k_in,
            out_specs=list(out_specs) + [_any()] * k_out,
            scratch_shapes=list(scratch_shapes) + [pltpu.SemaphoreType.DMA((comm.n_sems,)),
                                                    pltpu.SemaphoreType.DMA((comm.n_sems,))]))
    return lambda *args: call(*args, *comm.operands)


class _Shifted:
    def __init__(self, base, offset):
        self.base, self.offset = base, offset

    @property
    def at(self):
        return self

    def __getitem__(self, k):
        return self.base.at[self.offset + k]


def _merge_jobs(*jobs):
    def order(count):
        first = [(j, i) for j, job in enumerate(jobs) for i in range(job.in_place)]
        return first + [(j, i) for j, job in enumerate(jobs) for i in range(job.in_place, count(job))]

    op_order, out_order = order(lambda job: len(job.operands)), order(lambda job: len(job.out_shapes))

    def make(ins, outs, send_sems, recv_sems):
        starts, waits, sem = [], [], 0
        for j, job in enumerate(jobs):
            mine_in = [ins[k] for k, (jj, _) in enumerate(op_order) if jj == j]
            mine_out = [outs[k] for k, (jj, _) in enumerate(out_order) if jj == j]
            s, w = job.make(mine_in, mine_out, _Shifted(send_sems, sem), _Shifted(recv_sems, sem))
            starts, waits, sem = starts + s, waits + w, sem + job.n_sems
        return starts, waits

    return _Comm(tuple(jobs[j].operands[i] for j, i in op_order), tuple(jobs[j].out_shapes[i] for j, i in out_order),
                 sum(job.n_sems for job in jobs), make, in_place=sum(job.in_place for job in jobs))


def _mesh_place():
    x, y, c = lax.axis_index("x"), lax.axis_index("y"), lax.axis_index("c")
    return x, y, c, [(1 - x, y), (x, 1 - y), (1 - x, 1 - y)]


def _gather_job(bufs):
    per = 4

    def make(ins, outs, send_sems, recv_sems):
        del ins
        x, y, c, chips = _mesh_place()
        starts, waits = [], []
        for a, out in enumerate(outs):
            mine = src = out.at[4 * x + 2 * y + c]
            to = [(x, y, 1 - c)] + [(px, py, c) for px, py in chips]
            sends = [pltpu.make_async_remote_copy(
                src_ref=src, dst_ref=mine, send_sem=send_sems.at[per * a + k],
                recv_sem=recv_sems.at[per * a + k], device_id=dev, device_id_type=MESH)
                for k, dev in enumerate(to)]
            recvs = [pltpu.make_async_remote_copy(
                src_ref=src, dst_ref=out.at[4 * px + 2 * py + pc], send_sem=send_sems.at[per * a + k],
                recv_sem=recv_sems.at[per * a + k], device_id=(px, py, pc), device_id_type=MESH)
                for k, (px, py, pc) in enumerate(to)]
            starts += sends
            waits += [s.wait_send for s in sends] + [r.wait_recv for r in recvs]
        return starts, waits

    shapes = tuple(jax.ShapeDtypeStruct(b.shape, b.dtype) for b in bufs)
    return _Comm(tuple(bufs), shapes, per * len(bufs), make, in_place=len(bufs))


def _slots(x, y, c):
    return 4 * x + 2 * y + c, 4 * (1 - x) + 2 * y + c, 4 * x + 2 * (1 - y) + c, 4 * (1 - x) + 2 * (1 - y) + c


def _gather2d_first(halves):
    per = 2

    def make(ins, outs, send_sems, recv_sems):
        x, y, c, _ = _mesh_place()
        me, xn, yn, _ = _slots(x, y, c)
        starts, waits = [], []
        for a, (src, out) in enumerate(zip(ins, outs)):
            blk = src.at[c]
            rows = blk.shape[0] // 2
            upper, lower = pl.ds(0, rows), pl.ds(rows, rows)

            def copy(k, src_ref, dst_ref, dev, a=a):
                return pltpu.make_async_remote_copy(
                    src_ref=src_ref, dst_ref=dst_ref, send_sem=send_sems.at[per * a + k],
                    recv_sem=recv_sems.at[per * a + k], device_id=dev, device_id_type=MESH)

            sends = [copy(0, blk.at[upper], out.at[me, upper], (1 - x, y, c)),
                     copy(1, blk.at[lower], out.at[me, lower], (x, 1 - y, c))]
            recvs = [copy(0, blk.at[upper], out.at[xn, upper], (1 - x, y, c)),
                     copy(1, blk.at[lower], out.at[yn, lower], (x, 1 - y, c))]
            starts += sends
            waits += [s.wait_send for s in sends] + [r.wait_recv for r in recvs]
        return starts, waits

    shapes = tuple(jax.ShapeDtypeStruct((N_DEV,) + h.shape[1:], h.dtype) for h in halves)
    return _Comm(tuple(halves), shapes, per * len(halves), make)


def _gather2d_second(bufs, halves):
    per = 4
    n_arr = len(bufs)

    def make(ins, outs, send_sems, recv_sems):
        x, y, c, _ = _mesh_place()
        me, xn, yn, dg = _slots(x, y, c)
        starts, waits = [], []
        for a, buf in enumerate(outs):
            own = ins[n_arr + a].at[c]
            rows = buf.shape[1] // 2
            upper, lower = pl.ds(0, rows), pl.ds(rows, rows)
            plan = [(own.at[upper], me, upper, (x, 1 - y, c), yn), (buf.at[xn, upper], xn, upper, (x, 1 - y, c), dg),
                    (own.at[lower], me, lower, (1 - x, y, c), xn), (buf.at[yn, lower], yn, lower, (1 - x, y, c), dg)]
            for k, (src, slot, part, dev, landing) in enumerate(plan):
                sems = dict(send_sem=send_sems.at[per * a + k], recv_sem=recv_sems.at[per * a + k],
                            device_id=dev, device_id_type=MESH)
                send = pltpu.make_async_remote_copy(src_ref=src, dst_ref=buf.at[slot, part], **sems)
                arrival = pltpu.make_async_remote_copy(src_ref=src, dst_ref=buf.at[landing, part], **sems)
                starts.append(send)
                waits += [send.wait_send, arrival.wait_recv]
        return starts, waits

    shapes = tuple(jax.ShapeDtypeStruct(b.shape, b.dtype) for b in bufs)
    return _Comm(tuple(bufs) + tuple(halves), shapes, per * n_arr, make, in_place=n_arr)


def _gather_forward(bufs, name):
    n_arr = len(bufs)

    def body(*refs):
        outs = refs[n_arr:2 * n_arr]
        send_sems, recv_sems = refs[2 * n_arr:]
        x, y, c, chips = _mesh_place()
        sends, recvs = [], []
        for a, buf in enumerate(outs):
            for j, (px, py) in enumerate(chips):
                mine, theirs = buf.at[4 * px + 2 * py + c], buf.at[4 * px + 2 * py + 1 - c]
                sems = dict(send_sem=send_sems.at[3 * a + j], recv_sem=recv_sems.at[3 * a + j],
                            device_id=(x, y, 1 - c), device_id_type=MESH)
                sends.append(pltpu.make_async_remote_copy(src_ref=mine, dst_ref=mine, **sems))
                recvs.append(pltpu.make_async_remote_copy(src_ref=mine, dst_ref=theirs, **sems))
        for cp in sends:
            cp.start()
        for s, r in zip(sends, recvs):
            s.wait_send()
            r.wait_recv()

    return pl.pallas_call(
        body, name=name, out_shape=[jax.ShapeDtypeStruct(b.shape, b.dtype) for b in bufs],
        in_specs=[_any()] * n_arr, out_specs=[_any()] * n_arr,
        input_output_aliases={a: a for a in range(n_arr)},
        scratch_shapes=[pltpu.SemaphoreType.DMA((3 * n_arr,)), pltpu.SemaphoreType.DMA((3 * n_arr,))],
    )(*bufs)


def _scatter_job(chip_sums, rows=None, into=()):
    n_into = len(into)

    def part(ref):
        return ref if rows is None else ref.at[pl.ds(rows[0], rows[1])]

    def make(ins, outs, send_sems, recv_sems):
        x, y, c, chips = _mesh_place()
        copies = [pltpu.make_async_remote_copy(
            src_ref=part(src.at[2 * px + py]), dst_ref=part(out.at[j]), send_sem=send_sems.at[3 * a + j],
            recv_sem=recv_sems.at[3 * a + j], device_id=(px, py, c), device_id_type=MESH)
            for a, (src, out) in enumerate(zip(ins[n_into:], outs)) for j, (px, py) in enumerate(chips)]
        return copies, [cp.wait for cp in copies]

    shapes = tuple(jax.ShapeDtypeStruct((3,) + s.shape[1:], s.dtype) for s in chip_sums)
    return _Comm(tuple(into) + tuple(chip_sums), shapes, 3 * len(chip_sums), make, in_place=n_into)


def _all_gather8(blocks, name, split=False, forward=(), riders=(), skip_own=()):
    n_arr, n_fwd = len(blocks), len(forward)
    splits = list(split) if isinstance(split, (list, tuple)) else [split] * n_arr
    own_slots = [a not in skip_own for a in range(n_arr)]
    rider_in = sum(len(r.operands) for r in riders)
    rider_out = sum(len(r.out_shapes) for r in riders)

    def body(*refs):
        x_refs, refs = refs[:n_arr], refs[n_arr + n_fwd:]
        r_ins, refs = refs[:rider_in], refs[rider_in:]
        out_refs, refs = refs[:n_arr], refs[n_arr:]
        fwd_refs, refs = refs[:n_fwd], refs[n_fwd:]
        r_outs, refs = refs[:rider_out], refs[rider_out:]
        (send_sems, recv_sems, local_sems), rider_sems = refs[:3], refs[3:]
        x, y, c, chips = _mesh_place()
        me, sibling = (x, y, c), (x, y, 1 - c)
        passing = []
        for f, buf in enumerate(fwd_refs):
            for j, (px, py) in enumerate(chips):
                mine, theirs = buf.at[4 * px + 2 * py + c], buf.at[4 * px + 2 * py + 1 - c]
                sems = dict(send_sem=send_sems.at[7 * n_arr + 3 * f + j], recv_sem=recv_sems.at[7 * n_arr + 3 * f + j],
                            device_id=sibling, device_id_type=MESH)
                passing.append((pltpu.make_async_remote_copy(src_ref=mine, dst_ref=mine, **sems),
                                pltpu.make_async_remote_copy(src_ref=mine, dst_ref=theirs, **sems)))
        for send, _ in passing:
            send.start()
        arrays = []
        for a, (x_ref, out_ref) in enumerate(zip(x_refs, out_refs)):
            src_mine = x_ref.at[c] if splits[a] else x_ref

            def copy(k, blk, to, src=None, a=a, out_ref=out_ref):
                dst = out_ref.at[4 * blk[0] + 2 * blk[1] + blk[2]]
                return pltpu.make_async_remote_copy(
                    src_ref=dst if src is None else src, dst_ref=dst,
                    send_sem=send_sems.at[7 * a + k], recv_sem=recv_sems.at[7 * a + k],
                    device_id=to, device_id_type=MESH)

            mine = pltpu.make_async_copy(src_mine, out_ref.at[4 * x + 2 * y + c], local_sems.at[a])
            first = [copy(0, me, sibling, src=src_mine)] if own_slots[a] else []
            first += [copy(1 + j, me, (*chip, c), src=src_mine) for j, chip in enumerate(chips)]
            for cp in first + ([mine] if own_slots[a] else []):
                cp.start()
            arrays.append((copy, mine, first, own_slots[a]))
        rider_waits, i0, o0 = [], 0, 0
        for n, job in enumerate(riders):
            k_in, k_out = len(job.operands), len(job.out_shapes)
            starts, waits = job.make(r_ins[i0:i0 + k_in], r_outs[o0:o0 + k_out],
                                     rider_sems[2 * n], rider_sems[2 * n + 1])
            for cp in starts:
                cp.start()
            rider_waits += waits
            i0, o0 = i0 + k_in, o0 + k_out
        sent = []
        for copy, mine, first, own in arrays:
            passed = [copy(4 + j, (*chip, c), sibling) for j, chip in enumerate(chips)]
            for j, chip in enumerate(chips):
                copy(1 + j, (*chip, c), me).wait_recv()
                passed[j].start()
            sent += first + passed
        for copy, mine, first, own in arrays:
            if own:
                copy(0, sibling, me).wait_recv()
                mine.wait()
            for j, chip in enumerate(chips):
                copy(4 + j, (*chip, 1 - c), me).wait_recv()
        for cp in sent:
            cp.wait_send()
        for send, arrival in passing:
            send.wait_send()
            arrival.wait_recv()
        for wait in rider_waits:
            wait()

    n_sems = 7 * n_arr + 3 * n_fwd
    rider_operands = [a for r in riders for a in r.operands]
    rider_shapes = [s for r in riders for s in r.out_shapes]
    return pl.pallas_call(
        body, name=name,
        out_shape=[jax.ShapeDtypeStruct((N_DEV,) + tuple(b.shape[1:] if s else b.shape), b.dtype)
                   for b, s in zip(blocks, splits)]
        + [jax.ShapeDtypeStruct(f.shape, f.dtype) for f in forward] + rider_shapes,
        in_specs=[_any()] * (n_arr + n_fwd + rider_in), out_specs=[_any()] * (n_arr + n_fwd + rider_out),
        input_output_aliases={n_arr + f: n_arr + f for f in range(n_fwd)},
        scratch_shapes=[pltpu.SemaphoreType.DMA((n_sems,)), pltpu.SemaphoreType.DMA((n_sems,)),
                        pltpu.SemaphoreType.DMA((n_arr,))]
        + [pltpu.SemaphoreType.DMA((r.n_sems,)) for r in riders for _ in range(2)],
    )(*blocks, *forward, *rider_operands)


def _first_stage_copies(srcs, lands, splits, owns, send_sems, recv_sems, local_sems):
    x, y, c, chips = _mesh_place()
    same_core = [(px, py, c) for px, py in chips]
    everyone = [(x, y, 1 - c)] + [(px, py, pc) for px, py in chips for pc in (c, 1 - c)]
    per_array, local, k = [], [], 0
    for a, (src, land, split, own) in enumerate(zip(srcs, lands, splits, owns)):
        mine = src.at[c] if split else src
        peers = (([(x, y, 1 - c)] if own else []) + same_core) if split else everyone
        pairs = []
        for px, py, pc in peers:
            sems = dict(send_sem=send_sems.at[k], recv_sem=recv_sems.at[k],
                        device_id=(px, py, pc), device_id_type=MESH)
            pairs.append((pltpu.make_async_remote_copy(src_ref=mine, dst_ref=land.at[4 * x + 2 * y + c], **sems),
                          pltpu.make_async_remote_copy(src_ref=mine, dst_ref=land.at[4 * px + 2 * py + pc], **sems)))
            k += 1
        per_array.append(pairs)
        if own:
            local.append(pltpu.make_async_copy(mine, land.at[4 * x + 2 * y + c], local_sems.at[a]))
    return per_array, local


def _first_stage_start(groups, name):
    n_groups = len(groups)
    hbm, sem = pl.BlockSpec(memory_space=pltpu.HBM), pl.BlockSpec(memory_space=pltpu.SEMAPHORE)
    operands, sem_shapes, offsets = [], [], []
    for blocks, splits, owns in groups:
        n_sems = sum((4 if own else 3) if split else N_DEV - 1 for split, own in zip(splits, owns))
        sem_shapes += [pltpu.SemaphoreType.DMA((n_sems,)), pltpu.SemaphoreType.DMA((n_sems,)),
                       pltpu.SemaphoreType.DMA((len(blocks),))]
        offsets.append(len(operands))
        operands += list(blocks) + [lax.empty((N_DEV,) + tuple(b.shape[1:] if split else b.shape), b.dtype)
                                    for b, split in zip(blocks, splits)]
    n_ops = len(operands)

    def body(*refs):
        token = refs[-1]
        for g, (blocks, splits, owns) in enumerate(groups):
            n, at = len(blocks), offsets[g]
            per_array, local = _first_stage_copies(refs[at:at + n], refs[at + n:at + 2 * n], splits, owns,
                                                   *refs[n_ops + 3 * g:n_ops + 3 * g + 3])
            for pairs in per_array:
                for send, _ in pairs:
                    send.start()
            for cp in local:
                cp.start()
        token[...] = jnp.zeros_like(token)

    operands = [pltpu.with_memory_space_constraint(a, pltpu.HBM) for a in operands]
    out = pl.pallas_call(
        body, name=name,
        out_shape=sem_shapes + [pltpu.HBM(a.shape, a.dtype) for a in operands]
        + [jax.ShapeDtypeStruct((8, LANES), F32)],
        in_specs=[hbm] * n_ops,
        out_specs=[sem] * (3 * n_groups) + [hbm] * n_ops + [pl.BlockSpec(memory_space=pltpu.VMEM)],
        input_output_aliases={i: 3 * n_groups + i for i in range(n_ops)},
        compiler_params=pltpu.CompilerParams(has_side_effects=pltpu.SideEffectType.DATAFLOW_SIDE_EFFECTING),
    )(*operands)
    thru = out[3 * n_groups:-1]
    return [list(out[3 * g:3 * g + 3]) + list(thru[offsets[g]:offsets[g] + 2 * len(groups[g][0])]) + [out[-1]]
            for g in range(n_groups)]


def _first_stage_finish(started, splits, owns, after, name):
    send_sems, recv_sems, local_sems, *bufs, _ = started
    n = len(bufs) // 2
    hbm, sem = pl.BlockSpec(memory_space=pltpu.HBM), pl.BlockSpec(memory_space=pltpu.SEMAPHORE)

    def body(*refs):
        srcs, lands = refs[:n], refs[n:2 * n]
        send_sems, recv_sems, local_sems = refs[2 * n:2 * n + 3]
        per_array, local = _first_stage_copies(srcs, lands, splits, owns, send_sems, recv_sems, local_sems)
        for pairs in per_array:
            for send, arrival in pairs:
                send.wait_send()
                arrival.wait_recv()
        for cp in local:
            cp.wait()

    out = pl.pallas_call(
        body, name=name, out_shape=[pltpu.HBM(b.shape, b.dtype) for b in bufs],
        in_specs=[hbm] * (2 * n) + [sem, sem, sem] + [_any()] * len(after), out_specs=[hbm] * (2 * n),
        input_output_aliases={i: i for i in range(2 * n)},
        compiler_params=pltpu.CompilerParams(has_side_effects=pltpu.SideEffectType.DATAFLOW_SIDE_EFFECTING),
    )(*bufs, send_sems, recv_sems, local_sems, *after)
    return out[n:]


def _split_scatter_copies(src, land, send_sems, recv_sems):
    x, y, c, chips = _mesh_place()
    return [pltpu.make_async_remote_copy(
        src_ref=src.at[2 * px + py], dst_ref=land.at[j], send_sem=send_sems.at[j], recv_sem=recv_sems.at[j],
        device_id=(px, py, c), device_id_type=MESH) for j, (px, py) in enumerate(chips)]


def _scatter_start(chip_sums, name):
    hbm, sem = pl.BlockSpec(memory_space=pltpu.HBM), pl.BlockSpec(memory_space=pltpu.SEMAPHORE)

    def body(src, land, send_sems, recv_sems, src_thru, land_thru, token):
        del src_thru, land_thru
        for cp in _split_scatter_copies(src, land, send_sems, recv_sems):
            cp.start()
        token[...] = jnp.zeros_like(token)

    land = lax.empty((N_CHIPS - 1,) + chip_sums.shape[1:], chip_sums.dtype)
    operands = [pltpu.with_memory_space_constraint(a, pltpu.HBM) for a in (chip_sums, land)]
    return pl.pallas_call(
        body, name=name,
        out_shape=[pltpu.SemaphoreType.DMA((N_CHIPS - 1,)), pltpu.SemaphoreType.DMA((N_CHIPS - 1,))]
        + [pltpu.HBM(a.shape, a.dtype) for a in operands] + [jax.ShapeDtypeStruct((8, LANES), F32)],
        in_specs=[hbm, hbm], out_specs=[sem, sem, hbm, hbm, pl.BlockSpec(memory_space=pltpu.VMEM)],
        input_output_aliases={0: 2, 1: 3},
        compiler_params=pltpu.CompilerParams(has_side_effects=pltpu.SideEffectType.DATAFLOW_SIDE_EFFECTING),
    )(*operands)


def _scatter_wait(started, after, name):
    send_sems, recv_sems, src, land, _ = started
    hbm, sem = pl.BlockSpec(memory_space=pltpu.HBM), pl.BlockSpec(memory_space=pltpu.SEMAPHORE)

    def body(src, land, send_sems, recv_sems, after_ref, src_thru, land_thru):
        del after_ref, src_thru, land_thru
        for cp in _split_scatter_copies(src, land, send_sems, recv_sems):
            cp.wait()

    return pl.pallas_call(
        body, name=name, out_shape=[pltpu.HBM(src.shape, src.dtype), pltpu.HBM(land.shape, land.dtype)],
        in_specs=[hbm, hbm, sem, sem, _any()], out_specs=[hbm, hbm], input_output_aliases={0: 0, 1: 1},
        compiler_params=pltpu.CompilerParams(has_side_effects=pltpu.SideEffectType.DATAFLOW_SIDE_EFFECTING),
    )(src, land, send_sems, recv_sems, after)


def _split_gather_copies(srcs, lands, fwds, shares, send_sems, recv_sems):
    x, y, c, chips = _mesh_place()
    peers = [(x, y, 1 - c)] + [(px, py, pc) for px, py in chips for pc in (c, 1 - c)]
    pairs = []
    for a, (src, land) in enumerate(zip(srcs, lands)):
        for k, (px, py, pc) in enumerate(peers):
            sems = dict(send_sem=send_sems.at[7 * a + k], recv_sem=recv_sems.at[7 * a + k],
                        device_id=(px, py, pc), device_id_type=MESH)
            pairs.append((pltpu.make_async_remote_copy(src_ref=src, dst_ref=land.at[4 * x + 2 * y + c], **sems),
                          pltpu.make_async_remote_copy(src_ref=src, dst_ref=land.at[4 * px + 2 * py + pc], **sems)))
    for f, buf in enumerate(fwds):
        for j, (px, py) in enumerate(chips):
            mine, theirs = buf.at[4 * px + 2 * py + c], buf.at[4 * px + 2 * py + 1 - c]
            k = 7 * len(srcs) + 3 * f + j
            sems = dict(send_sem=send_sems.at[k], recv_sem=recv_sems.at[k],
                        device_id=(x, y, 1 - c), device_id_type=MESH)
            pairs.append((pltpu.make_async_remote_copy(src_ref=mine, dst_ref=mine, **sems),
                          pltpu.make_async_remote_copy(src_ref=mine, dst_ref=theirs, **sems)))
    for s, buf in enumerate(shares):
        k = 7 * len(srcs) + 3 * len(fwds) + s
        sems = dict(send_sem=send_sems.at[k], recv_sem=recv_sems.at[k], device_id=(x, y, 1 - c), device_id_type=MESH)
        pairs.append((pltpu.make_async_remote_copy(src_ref=buf.at[c], dst_ref=buf.at[c], **sems),
                      pltpu.make_async_remote_copy(src_ref=buf.at[c], dst_ref=buf.at[1 - c], **sems)))
    return pairs


def _gather_start(blocks, forward, shares, after, name):
    n, n_fwd = len(blocks), len(forward)
    n_sems = 7 * n + 3 * n_fwd + len(shares)
    n_bufs = 2 * n + n_fwd + len(shares)
    hbm, sem = pl.BlockSpec(memory_space=pltpu.HBM), pl.BlockSpec(memory_space=pltpu.SEMAPHORE)

    def body(*refs):
        srcs, lands, fwds, swaps = refs[:n], refs[n:2 * n], refs[2 * n:2 * n + n_fwd], refs[2 * n + n_fwd:n_bufs]
        send_sems, recv_sems = refs[n_bufs + 1:n_bufs + 3]
        token, local_sems = refs[-2:]
        x, y, c, _ = _mesh_place()
        own = [pltpu.make_async_copy(src, land.at[4 * x + 2 * y + c], local_sems.at[a])
               for a, (src, land) in enumerate(zip(srcs, lands))]
        for cp in own:
            cp.start()
        for send, _ in _split_gather_copies(srcs, lands, fwds, swaps, send_sems, recv_sems):
            send.start()
        token[...] = jnp.zeros_like(token)
        for cp in own:
            cp.wait()

    lands = [lax.empty((N_DEV,) + b.shape, b.dtype) for b in blocks]
    operands = [pltpu.with_memory_space_constraint(a, pltpu.HBM)
                for a in list(blocks) + lands + list(forward) + list(shares)]
    return pl.pallas_call(
        body, name=name,
        out_shape=[pltpu.SemaphoreType.DMA((n_sems,)), pltpu.SemaphoreType.DMA((n_sems,))]
        + [pltpu.HBM(a.shape, a.dtype) for a in operands] + [jax.ShapeDtypeStruct((8, LANES), F32)],
        in_specs=[hbm] * len(operands) + [_any()],
        out_specs=[sem, sem] + [hbm] * len(operands) + [pl.BlockSpec(memory_space=pltpu.VMEM)],
        input_output_aliases={i: 2 + i for i in range(len(operands))},
        scratch_shapes=[pltpu.SemaphoreType.DMA((n,))],
        compiler_params=pltpu.CompilerParams(has_side_effects=pltpu.SideEffectType.DATAFLOW_SIDE_EFFECTING),
    )(*operands, after)


def _gather_wait(started, n, n_shares, after, name):
    send_sems, recv_sems, *bufs, _ = started
    n_bufs = len(bufs)
    n_fwd = n_bufs - 2 * n - n_shares
    hbm, sem = pl.BlockSpec(memory_space=pltpu.HBM), pl.BlockSpec(memory_space=pltpu.SEMAPHORE)

    def body(*refs):
        srcs, lands, fwds, swaps = refs[:n], refs[n:2 * n], refs[2 * n:2 * n + n_fwd], refs[2 * n + n_fwd:n_bufs]
        send_sems, recv_sems = refs[n_bufs:n_bufs + 2]
        for send, arrival in _split_gather_copies(srcs, lands, fwds, swaps, send_sems, recv_sems):
            send.wait_send()
            arrival.wait_recv()

    out = pl.pallas_call(
        body, name=name, out_shape=[pltpu.HBM(b.shape, b.dtype) for b in bufs],
        in_specs=[hbm] * len(bufs) + [sem, sem, _any()], out_specs=[hbm] * len(bufs),
        input_output_aliases={i: i for i in range(len(bufs))},
        compiler_params=pltpu.CompilerParams(has_side_effects=pltpu.SideEffectType.DATAFLOW_SIDE_EFFECTING),
    )(*bufs, send_sems, recv_sems, after)
    return out[n:]


def _share_job(bufs):
    def make(ins, outs, send_sems, recv_sems):
        del ins
        x, y, c, _ = _mesh_place()
        sems = lambda a: dict(send_sem=send_sems.at[a], recv_sem=recv_sems.at[a],
                              device_id=(x, y, 1 - c), device_id_type=MESH)
        sends = [pltpu.make_async_remote_copy(src_ref=o.at[c], dst_ref=o.at[c], **sems(a)) for a, o in enumerate(outs)]
        arrivals = [pltpu.make_async_remote_copy(src_ref=o.at[c], dst_ref=o.at[1 - c], **sems(a))
                    for a, o in enumerate(outs)]
        return sends, [s.wait_send for s in sends] + [r.wait_recv for r in arrivals]

    shapes = tuple(jax.ShapeDtypeStruct(b.shape, b.dtype) for b in bufs)
    return _Comm(tuple(bufs), shapes, len(bufs), make, in_place=len(bufs))


def _gelu_tanh(z):
    k = math.sqrt(2.0 / math.pi)
    t = jnp.tanh(k * (z + 0.044715 * (z * z * z)))
    return 0.5 * z * (1.0 + t), t


def _gelu_tanh_grad(z, t):
    k = math.sqrt(2.0 / math.pi)
    return 0.5 * (1.0 + t) + 0.5 * z * (1.0 - t * t) * (k * (1.0 + 3.0 * 0.044715 * (z * z)))


def _rope_angle_kernel(pos_row, invf_col):
    seq = pos_row.shape[1]

    def body(p_ref, f_ref, cos_ref, sin_ref):
        ang = p_ref[...].astype(F32) * f_ref[...]
        cos_ref[...] = jnp.cos(ang)
        sin_ref[...] = jnp.sin(ang)

    return pl.pallas_call(
        body, name="rope_angles", grid=(1,), out_shape=[jax.ShapeDtypeStruct((ROT_DIM // 2, seq), F32)] * 2,
        in_specs=[_full((1, seq)), _full((ROT_DIM // 2, 1))], out_specs=[_full((ROT_DIM // 2, seq))] * 2,
        compiler_params=_params("arbitrary"),
    )(pos_row, invf_col)


def _rope_lane_tables(cos, sin):
    cos_t, sin_t = cos.T, sin.T
    seq, half = cos_t.shape
    ones = jnp.ones((seq, HEAD_DIM - ROT_DIM), F32)
    c64 = jnp.concatenate([cos_t, cos_t, ones], axis=1)
    s1 = jnp.concatenate([sin_t, jnp.zeros((seq, HEAD_DIM - half), F32)], axis=1)
    s2 = jnp.concatenate([jnp.zeros((seq, half), F32), sin_t, jnp.zeros((seq, HEAD_DIM - ROT_DIM), F32)], axis=1)
    return jnp.concatenate([jnp.tile(t, (1, LANES // HEAD_DIM)) for t in (c64, s1, s2)], axis=1)


def _rope_apply(t, tab, sign):
    reps = t.shape[1] // LANES
    c_tab, s1, s2 = (jnp.tile(tab[:, LANES * k:LANES * (k + 1)], (1, reps)) if reps > 1
                     else tab[:, LANES * k:LANES * (k + 1)] for k in range(3))
    half = ROT_DIM // 2
    up = pltpu.roll(t, t.shape[1] - half, 1)
    down = pltpu.roll(t, half, 1)
    return t * c_tab + sign * (down * s2 - up * s1)


def _lane_masks(shape):
    lane = lax.broadcasted_iota(jnp.int32, shape, 1)
    return lane < HEAD_DIM, lane >= HEAD_DIM


HEADS_PER_GROUP = N_Q_HEADS // N_KV_HEADS
ATTN_SCALE = 1.0 / math.sqrt(HEAD_DIM)


def _attn_bias_t(first_block):
    kj = lax.broadcasted_iota(jnp.int32, (2 * CHUNK, CHUNK), 0)
    qi = lax.broadcasted_iota(jnp.int32, (2 * CHUNK, CHUNK), 1)
    ok = (kj > qi) & (kj <= qi + CHUNK)
    if first_block is not None:
        ok = ok & (jnp.logical_not(first_block) | (kj >= CHUNK))
    return jnp.tile(jnp.where(ok, 0.0, -jnp.inf), (1, HEADS_PER_GROUP))


def _group_rows(x, g, lo, hi):
    rows = []
    for r in range(HEADS_PER_GROUP):
        h = HEADS_PER_GROUP * g + r
        pair = x[:, LANES * (h // 2):LANES * (h // 2 + 1)]
        rows.append(jnp.where(hi if h % 2 else lo, pair, 0.0))
    return jnp.concatenate(rows, axis=0)


def _pairs_from_rows(rows, lo):
    return [jnp.where(lo, rows[2 * CHUNK * k:2 * CHUNK * k + CHUNK], rows[2 * CHUNK * k + CHUNK:2 * CHUNK * (k + 1)])
            for k in range(HEADS_PER_GROUP // 2)]


def _group_dup(a, b, g, lo2):
    return jnp.where(lo2, a, b) if g == 0 else jnp.where(lo2, b, a)


def _sink_row(sink_ref, g):
    return jnp.concatenate([sink_ref[HEADS_PER_GROUP * g + r:HEADS_PER_GROUP * g + r + 1, :]
                            for r in range(HEADS_PER_GROUP)], axis=1)


def _attn_probs_t(k_dup, q_rows, bias_t, sink_row):
    s_t = _dot_nt(k_dup, q_rows) * ATTN_SCALE + bias_t
    m = jnp.maximum(jnp.max(s_t, axis=0, keepdims=True), sink_row)
    p = jnp.exp(s_t - m)
    e_sink = jnp.exp(sink_row - m)
    inv = 1.0 / (jnp.sum(p, axis=0, keepdims=True) + e_sink)
    return p * inv, e_sink * inv


def _sgu_forward_pair(wm, vp, j):
    lo, hi = _lane_masks(vp.shape)
    lhs = jnp.concatenate([wm[2 * j], wm[2 * j + 1]], axis=1)
    rhs = jnp.concatenate([jnp.where(lo, vp, 0.0), jnp.where(hi, vp, 0.0)], axis=0)
    return _dot(lhs, rhs)


def _masked_spatial(w_ref):
    t = lax.broadcasted_iota(jnp.int32, (CHUNK, CHUNK), 0)
    s = lax.broadcasted_iota(jnp.int32, (CHUNK, CHUNK), 1)
    tril = s <= t
    return [jnp.where(tril, w_ref[g], 0.0) for g in range(GMLP_GROUPS)], tril, s >= t


def _mod_kernel(c_all, w_shard, b_shard, comm=None):
    n = w_shard.shape[1]
    tn = 512

    def body(c_ref, w_ref, b_ref, mod_ref, act_ref):
        cv = c_ref[...]
        act = cv * (1.0 / (1.0 + jnp.exp(-cv)))
        act_ref[...] = act
        mod_ref[...] = _dot(act, w_ref[...]) + b_ref[...]

    return _hosted_call(
        body, comm, name="ada_mod", grid=(n // tn,),
        out_shape=[jax.ShapeDtypeStruct((N_DEV, n), F32), jax.ShapeDtypeStruct((N_DEV, D_MODEL), F32)],
        in_specs=[_full((N_DEV, D_MODEL)), pl.BlockSpec((D_MODEL, tn), lambda i: (0, i)),
                  pl.BlockSpec((1, tn), lambda i: (0, i))],
        out_specs=[pl.BlockSpec((N_DEV, tn), lambda i: (0, i)), _full((N_DEV, D_MODEL))],
        semantics=("arbitrary",),
    )(c_all, w_shard, b_shard)


def _load_chip_blocks(chip_ref, gathered, local, dsts, sems, first_sem=0):
    for k, dst in enumerate(dsts):
        @pl.when(chip_ref[0] == k)
        def _():
            pltpu.make_async_copy(local, dst, sems.at[first_sem + k]).start()

        @pl.when(chip_ref[0] != k)
        def _():
            pltpu.make_async_copy(gathered.at[k], dst, sems.at[first_sem + k]).start()
    return [pltpu.make_async_copy(local, dst, sems.at[first_sem + k]).wait for k, dst in enumerate(dsts)]


def _in_proj_kernel(x, vecs, w_in_t, comm=None):
    seq = x.shape[0]
    tm = 512

    def body(x_ref, v_ref, w_ref, proj_ref, h_ref):
        xv = x_ref[...]
        rstd = lax.rsqrt(_mean_last(xv * xv) + EPS)
        n1 = (xv * rstd) * v_ref[0:1, :]
        h = n1 * (1.0 + v_ref[2:3, :]) + v_ref[1:2, :]
        hb = h.astype(MXU_DTYPE)
        h_ref[...] = hb
        proj_ref[...] = _dot_nt(hb, w_ref[...])

    return _hosted_call(
        body, comm, name="in_proj", grid=(seq // tm,),
        out_shape=[jax.ShapeDtypeStruct((seq, IN_PROJ_WIDTH), F32),
                   jax.ShapeDtypeStruct((seq, D_MODEL), MXU_DTYPE)],
        in_specs=[pl.BlockSpec((tm, D_MODEL), lambda i: (i, 0)), _full((8, D_MODEL)),
                  _full((IN_PROJ_WIDTH, D_MODEL))],
        out_specs=[pl.BlockSpec((tm, IN_PROJ_WIDTH), lambda i: (i, 0)),
                   pl.BlockSpec((tm, D_MODEL), lambda i: (i, 0))],
        semantics=("arbitrary",),
    )(x, vecs, w_in_t)


MIXER_BLOCKS_PER_STEP = 4
KV_START = 2 * GMLP_WIDTH + ATTN_WIDTH


def _mixer_fwd_kernel(proj, rope_tab, w_spatial, bias_full, sink_rows, comm=None):
    seq = proj.shape[0]
    per = MIXER_BLOCKS_PER_STEP
    steps = seq // (CHUNK * per)
    kv_col = KV_START // (2 * KV_WIDTH)

    def body(proj_ref, prev_ref, tab_ref, ptab_ref, w_ref, bias_ref, sink_ref, cat_ref):
        i = pl.program_id(0)
        wm, _, _ = _masked_spatial(w_ref)
        lo, hi = _lane_masks((CHUNK, LANES))
        lo2, _ = _lane_masks((2 * CHUNK, LANES))
        o = 2 * GMLP_WIDTH
        for s in range(per):
            rows, before = slice(CHUNK * s, CHUNK * (s + 1)), slice(CHUNK * (s - 1), CHUNK * s)
            for j in range(GMLP_GROUPS // 2):
                cols = slice(LANES * j, LANES * (j + 1))
                vcols = slice(GMLP_WIDTH + LANES * j, GMLP_WIDTH + LANES * (j + 1))
                u, _ = _gelu_tanh(proj_ref[rows, cols])
                vp, _ = _gelu_tanh(proj_ref[rows, vcols])
                sv = _sgu_forward_pair(wm, vp, j) + bias_ref[:, cols]
                cat_ref[rows, cols] = (u * sv).astype(cat_ref.dtype)
            tab = tab_ref[rows, :]
            if s == 0:
                prev_kv, prev_tab, first = prev_ref[...], ptab_ref[...], i == 0
            else:
                prev_kv, prev_tab, first = proj_ref[before, KV_START:KV_START + 2 * KV_WIDTH], tab_ref[before, :], None
            q_r = _rope_apply(proj_ref[rows, o:o + ATTN_WIDTH], tab, 1.0)
            k_cur = _rope_apply(proj_ref[rows, KV_START:KV_START + KV_WIDTH], tab, 1.0)
            k_prev = _rope_apply(prev_kv[:, 0:KV_WIDTH], prev_tab, 1.0)
            k_a = jnp.concatenate([k_prev, k_cur], axis=0)
            v_a = jnp.concatenate([prev_kv[:, KV_WIDTH:2 * KV_WIDTH],
                                   proj_ref[rows, KV_START + KV_WIDTH:KV_START + 2 * KV_WIDTH]], axis=0)
            k_b = pltpu.roll(k_a, HEAD_DIM, 1)
            v_b = pltpu.roll(v_a, HEAD_DIM, 1)
            bias_t = _attn_bias_t(first)
            for g in range(N_KV_HEADS):
                p_t, _ = _attn_probs_t(_group_dup(k_a, k_b, g, lo2), _group_rows(q_r, g, lo, hi), bias_t,
                                       _sink_row(sink_ref, g))
                o_t = _dot(_group_dup(v_a, v_b, g, lo2).T, p_t)
                for k, pair in enumerate(_pairs_from_rows(o_t.T, lo)):
                    c0 = GMLP_WIDTH + LANES * (2 * g + k)
                    cat_ref[rows, c0:c0 + LANES] = pair.astype(cat_ref.dtype)

    return _hosted_call(
        body, comm, name="mixer_fwd", grid=(steps,),
        out_shape=[jax.ShapeDtypeStruct((seq, D_MODEL), MXU_DTYPE)],
        in_specs=[pl.BlockSpec((CHUNK * per, IN_PROJ_WIDTH), lambda i: (i, 0)),
                  pl.BlockSpec((CHUNK, 2 * KV_WIDTH), lambda i: (jnp.maximum(per * i - 1, 0), kv_col)),
                  pl.BlockSpec((CHUNK * per, 3 * LANES), lambda i: (i, 0)),
                  pl.BlockSpec((CHUNK, 3 * LANES), lambda i: (jnp.maximum(per * i - 1, 0), 0)),
                  _full((GMLP_GROUPS, CHUNK, CHUNK)), _full((CHUNK, GMLP_WIDTH)),
                  _full((N_Q_HEADS, LANES))],
        out_specs=[pl.BlockSpec((CHUNK * per, D_MODEL), lambda i: (i, 0))],
        semantics=("arbitrary",),
    )(proj, proj, rope_tab, rope_tab, w_spatial, bias_full, sink_rows)


def _trunk_kernel(x, target, cat, vecs, chip_idx, gathered, local):
    seq = x.shape[0]
    tm = 256
    nj = D_FF // D_MODEL
    out_rows = D_MODEL // N_CHIPS

    def body(chip_ref, x_ref, t_ref, cat_ref, v_ref, g_out, g_w1, g_w2, l_out, l_w1, l_w2,
             dx1_ref, dcat_ref, dmix_ref, h2_ref, r_ref, da_ref, dff_ref, sums_ref,
             wout, w1, w2, a_scr, sem):
        i = pl.program_id(0)

        @pl.when(i == 0)
        def _():
            waits = _load_chip_blocks(chip_ref, g_out, l_out,
                                      [wout.at[pl.ds(out_rows * k, out_rows)] for k in range(N_CHIPS)], sem)
            waits += _load_chip_blocks(chip_ref, g_w1, l_w1, [w1.at[k] for k in range(N_CHIPS)], sem, N_CHIPS)
            waits += _load_chip_blocks(chip_ref, g_w2, l_w2, [w2.at[k] for k in range(N_CHIPS)], sem, 2 * N_CHIPS)
            for wait in waits:
                wait()
            sums_ref[...] = jnp.zeros_like(sums_ref)

        gate1, shift2, scale2 = v_ref[0:1, :], v_ref[1:2, :], v_ref[2:3, :]
        gate2, g_ffn, g_final = v_ref[3:4, :], v_ref[4:5, :], v_ref[5:6, :]

        mix = _dot(cat_ref[...], wout[...])
        x1 = x_ref[...] + gate1 * mix
        rstd2 = lax.rsqrt(_mean_last(x1 * x1) + EPS)
        xh2 = x1 * rstd2
        n2 = xh2 * g_ffn
        h2b = (n2 * (1.0 + scale2) + shift2).astype(MXU_DTYPE)
        h2_ref[...] = h2b
        ff = jnp.zeros((tm, D_MODEL), F32)
        for j in range(nj):
            a = _dot(h2b, w1[j])
            a_scr[j] = a
            relu = jnp.maximum(a, 0.0)
            rb = (relu * relu).astype(MXU_DTYPE)
            r_ref[:, D_MODEL * j:D_MODEL * (j + 1)] = rb
            ff = ff + _dot(rb, w2[j])
        x2 = x1 + gate2 * ff
        rstd3 = lax.rsqrt(_mean_last(x2 * x2) + EPS)
        xh3 = x2 * rstd3
        err = xh3 * g_final - t_ref[...]
        loss = 0.5 * _rowsum(_mean_last(err * err))
        dy = err * (1.0 / D_MODEL)
        dxh3 = dy * g_final
        dx2 = rstd3 * (dxh3 - xh3 * _mean_last(dxh3 * xh3))
        dffb = (dx2 * gate2).astype(MXU_DTYPE)
        dff_ref[...] = dffb
        dh2 = jnp.zeros((tm, D_MODEL), F32)
        for j in range(nj):
            dr = _dot_nt(dffb, w2[j])
            dab = (dr * (2.0 * jnp.maximum(a_scr[j], 0.0))).astype(MXU_DTYPE)
            da_ref[:, D_MODEL * j:D_MODEL * (j + 1)] = dab
            dh2 = dh2 + _dot_nt(dab, w1[j])
        dn2 = dh2 * (1.0 + scale2)
        dxh2 = dn2 * g_ffn
        dx1 = dx2 + rstd2 * (dxh2 - xh2 * _mean_last(dxh2 * xh2))
        dx1_ref[...] = dx1
        dmixb = (dx1 * gate1).astype(MXU_DTYPE)
        dmix_ref[...] = dmixb
        dcat_ref[...] = _dot_nt(dmixb, wout[...])

        sums_ref[0:1, :] += _rowsum(dh2)
        sums_ref[1:2, :] += _rowsum(dh2 * n2)
        sums_ref[2:3, :] += _rowsum(dx2 * ff)
        sums_ref[3:4, :] += _rowsum(dn2 * xh2)
        sums_ref[4:5, :] += _rowsum(dy * xh3)
        sums_ref[5:6, :] += _rowsum(dx1 * mix)
        sums_ref[6:7, :] += jnp.broadcast_to(loss, (1, D_MODEL))

    tok = lambda w: pl.BlockSpec((tm, w), lambda i, chip: (i, 0))
    return _hosted_call(
        body, None, name="trunk", grid=(seq // tm,), n_prefetch=1,
        out_shape=[jax.ShapeDtypeStruct((seq, D_MODEL), F32), jax.ShapeDtypeStruct((seq, D_MODEL), F32),
                   jax.ShapeDtypeStruct((seq, D_MODEL), MXU_DTYPE), jax.ShapeDtypeStruct((seq, D_MODEL), MXU_DTYPE),
                   jax.ShapeDtypeStruct((seq, D_FF), MXU_DTYPE), jax.ShapeDtypeStruct((seq, D_FF), MXU_DTYPE),
                   jax.ShapeDtypeStruct((seq, D_MODEL), MXU_DTYPE), jax.ShapeDtypeStruct((8, D_MODEL), F32)],
        in_specs=[tok(D_MODEL), tok(D_MODEL), tok(D_MODEL), _full((8, D_MODEL))] + [_any()] * 6,
        out_specs=[tok(D_MODEL), tok(D_MODEL), tok(D_MODEL), tok(D_MODEL), tok(D_FF), tok(D_FF), tok(D_MODEL),
                   _full((8, D_MODEL))],
        scratch_shapes=[pltpu.VMEM((D_MODEL, D_MODEL), MXU_DTYPE), pltpu.VMEM((nj, D_MODEL, D_MODEL), MXU_DTYPE),
                        pltpu.VMEM((nj, D_MODEL, D_MODEL), MXU_DTYPE), pltpu.VMEM((nj, tm, D_MODEL), F32),
                        pltpu.SemaphoreType.DMA((3 * N_CHIPS,))],
        semantics=("arbitrary",),
    )(chip_idx, x, target, cat, vecs, *gathered, *local)


def _mixer_bwd_kernel(proj, rope_tab, dcat, w_spatial, w_spatial_t, bias_full, sink_rows, dev_idx, comm=None):
    seq = proj.shape[0]
    per = MIXER_BLOCKS_PER_STEP
    steps = seq // (CHUNK * per)
    kv_col = KV_START // (2 * KV_WIDTH)

    def body(dev_ref, proj_ref, prev_ref, tab_ref, ptab_ref, dcat_ref, w_ref, wt_ref, bias_ref, sink_ref,
             dproj_ref, dw_out, db_ref, dsink_ref, carry, dw_ref):
        del dev_ref
        step = pl.program_id(0)

        @pl.when(step == 0)
        def _():
            carry[...] = jnp.zeros_like(carry)
            dw_ref[...] = jnp.zeros_like(dw_ref)
            db_ref[...] = jnp.zeros_like(db_ref)
            dsink_ref[...] = jnp.zeros_like(dsink_ref)

        for s in reversed(range(per)):
            rows = pl.ds(CHUNK * s, CHUNK)
            if s == 0:
                before, before_tab, first = prev_ref, ptab_ref, step == steps - 1
            else:
                before = proj_ref.at[pl.ds(CHUNK * (s - 1), CHUNK), pl.ds(KV_START, 2 * KV_WIDTH)]
                before_tab, first = tab_ref.at[pl.ds(CHUNK * (s - 1), CHUNK)], None
            one_block(proj_ref.at[rows], before, tab_ref.at[rows], before_tab, dcat_ref.at[rows], w_ref, wt_ref,
                      bias_ref, sink_ref, dproj_ref.at[rows], dw_ref, db_ref, dsink_ref, carry, first)

        @pl.when(step == steps - 1)
        def _():
            dw_out[...] = dw_ref[...].astype(dw_out.dtype)

    def one_block(proj_ref, prev_ref, tab_ref, ptab_ref, dcat_ref, w_ref, wt_ref, bias_ref, sink_ref,
                  dproj_ref, dw_ref, db_ref, dsink_ref, carry, first):
        wm, tril, triu = _masked_spatial(w_ref)
        lo, hi = _lane_masks((CHUNK, LANES))
        lane = lax.broadcasted_iota(jnp.int32, (CHUNK, LANES), 1)
        db = jnp.zeros((CHUNK, LANES), F32)
        for j in range(GMLP_GROUPS // 2):
            cols = slice(LANES * j, LANES * (j + 1))
            vcols = slice(GMLP_WIDTH + LANES * j, GMLP_WIDTH + LANES * (j + 1))
            zu, zv = proj_ref[:, cols], proj_ref[:, vcols]
            u, tu = _gelu_tanh(zu)
            vp, tv = _gelu_tanh(zv)
            sv = _sgu_forward_pair(wm, vp, j) + bias_ref[:, cols]
            dout = dcat_ref[:, cols]
            du = dout * sv
            dsv = dout * u
            dsv_lo, dsv_hi = jnp.where(lo, dsv, 0.0), jnp.where(hi, dsv, 0.0)
            lhs_t = jnp.concatenate([jnp.where(triu, wt_ref[2 * j], 0.0),
                                     jnp.where(triu, wt_ref[2 * j + 1], 0.0)], axis=1)
            dv = _dot(lhs_t, jnp.concatenate([dsv_lo, dsv_hi], axis=0))
            dw_ref[2 * j] += jnp.where(tril, _dot_nt(dsv_lo, vp), 0.0)
            dw_ref[2 * j + 1] += jnp.where(tril, _dot_nt(dsv_hi, vp), 0.0)
            db = db + (jnp.where(lane == 2 * j, jnp.sum(dsv_lo, axis=1, keepdims=True), 0.0)
                       + jnp.where(lane == 2 * j + 1, jnp.sum(dsv_hi, axis=1, keepdims=True), 0.0))
            dproj_ref[:, cols] = (du * _gelu_tanh_grad(zu, tu)).astype(dproj_ref.dtype)
            dproj_ref[:, vcols] = (dv * _gelu_tanh_grad(zv, tv)).astype(dproj_ref.dtype)
        db_ref[...] += db
        o = 2 * GMLP_WIDTH
        tab = tab_ref[...]
        q_r = _rope_apply(proj_ref[:, o:o + ATTN_WIDTH], tab, 1.0)
        k_cur = _rope_apply(proj_ref[:, o + ATTN_WIDTH:o + ATTN_WIDTH + KV_WIDTH], tab, 1.0)
        k_prev = _rope_apply(prev_ref[:, 0:KV_WIDTH], ptab_ref[...], 1.0)
        k_a = jnp.concatenate([k_prev, k_cur], axis=0)
        v_a = jnp.concatenate([prev_ref[:, KV_WIDTH:2 * KV_WIDTH],
                               proj_ref[:, o + ATTN_WIDTH + KV_WIDTH:o + ATTN_WIDTH + 2 * KV_WIDTH]], axis=0)
        k_b = pltpu.roll(k_a, HEAD_DIM, 1)
        v_b = pltpu.roll(v_a, HEAD_DIM, 1)
        bias_t = _attn_bias_t(first)
        lo2, _ = _lane_masks((2 * CHUNK, LANES))
        dout_b = dcat_ref[:, GMLP_WIDTH:GMLP_WIDTH + ATTN_WIDTH]
        dk_tot, dv_tot, dq_pairs = [], [], []
        for g in range(N_KV_HEADS):
            k_dup, v_dup = _group_dup(k_a, k_b, g, lo2), _group_dup(v_a, v_b, g, lo2)
            q_rows = _group_rows(q_r, g, lo, hi)
            do_rows = _group_rows(dout_b, g, lo, hi)
            p_t, p_sink = _attn_probs_t(k_dup, q_rows, bias_t, _sink_row(sink_ref, g))
            dp_t = _dot_nt(v_dup, do_rows)
            delta = jnp.sum(p_t * dp_t, axis=0, keepdims=True)
            ds_t = p_t * (dp_t - delta) * ATTN_SCALE
            dsink = -p_sink * delta
            for r in range(HEADS_PER_GROUP):
                h = HEADS_PER_GROUP * g + r
                dsink_ref[h:h + 1, :] += jnp.broadcast_to(
                    jnp.sum(dsink[:, LANES * r:LANES * (r + 1)], axis=1, keepdims=True), (1, LANES))
            dk_full = _dot(ds_t, q_rows)
            dv_full = _dot(p_t, do_rows)
            dk_tot.append(dk_full + pltpu.roll(dk_full, HEAD_DIM, 1))
            dv_tot.append(dv_full + pltpu.roll(dv_full, HEAD_DIM, 1))
            dq_t = _dot(k_dup.T, ds_t)
            dq_pairs += _pairs_from_rows(dq_t.T, lo)
        dk_all = jnp.where(lo2, dk_tot[0], dk_tot[1])
        dv_all = jnp.where(lo2, dv_tot[0], dv_tot[1])
        dk_cur = dk_all[CHUNK:, :] + carry[:, 0:KV_WIDTH]
        dv_cur = dv_all[CHUNK:, :] + carry[:, KV_WIDTH:2 * KV_WIDTH]
        carry[:, 0:KV_WIDTH] = dk_all[:CHUNK, :]
        carry[:, KV_WIDTH:2 * KV_WIDTH] = dv_all[:CHUNK, :]
        dq = _rope_apply(jnp.concatenate(dq_pairs, axis=1), tab, -1.0)
        dproj_ref[:, o:o + ATTN_WIDTH] = dq.astype(dproj_ref.dtype)
        dproj_ref[:, o + ATTN_WIDTH:o + ATTN_WIDTH + KV_WIDTH] = (
            _rope_apply(dk_cur, tab, -1.0).astype(dproj_ref.dtype))
        dproj_ref[:, o + ATTN_WIDTH + KV_WIDTH:o + ATTN_WIDTH + 2 * KV_WIDTH] = dv_cur.astype(dproj_ref.dtype)

    rev = lambda i: steps - 1 - i
    before = lambda i: jnp.maximum(per * rev(i) - 1, 0)
    slot = lambda shape: pl.BlockSpec((None,) + shape, lambda i, d: (d[0],) + (0,) * len(shape))
    return _hosted_call(
        body, comm, name="mixer_bwd", grid=(steps,), n_prefetch=1,
        out_shape=[jax.ShapeDtypeStruct((seq, IN_PROJ_WIDTH), MXU_DTYPE),
                   jax.ShapeDtypeStruct((N_DEV, GMLP_GROUPS, CHUNK, CHUNK), GRAD_COMM_DTYPE),
                   jax.ShapeDtypeStruct((N_DEV, CHUNK, LANES), F32),
                   jax.ShapeDtypeStruct((N_DEV, N_Q_HEADS, LANES), F32)],
        in_specs=[pl.BlockSpec((CHUNK * per, IN_PROJ_WIDTH), lambda i, d: (rev(i), 0)),
                  pl.BlockSpec((CHUNK, 2 * KV_WIDTH), lambda i, d: (before(i), kv_col)),
                  pl.BlockSpec((CHUNK * per, 3 * LANES), lambda i, d: (rev(i), 0)),
                  pl.BlockSpec((CHUNK, 3 * LANES), lambda i, d: (before(i), 0)),
                  pl.BlockSpec((CHUNK * per, D_MODEL), lambda i, d: (rev(i), 0)),
                  _full((GMLP_GROUPS, CHUNK, CHUNK)), _full((GMLP_GROUPS, CHUNK, CHUNK)),
                  _full((CHUNK, GMLP_WIDTH)), _full((N_Q_HEADS, LANES))],
        out_specs=[pl.BlockSpec((CHUNK * per, IN_PROJ_WIDTH), lambda i, d: (rev(i), 0)),
                   slot((GMLP_GROUPS, CHUNK, CHUNK)), slot((CHUNK, LANES)), slot((N_Q_HEADS, LANES))],
        scratch_shapes=[pltpu.VMEM((CHUNK, 2 * KV_WIDTH), F32), pltpu.VMEM((GMLP_GROUPS, CHUNK, CHUNK), F32)],
        semantics=("arbitrary",),
    )(dev_idx, proj, proj, rope_tab, rope_tab, dcat, w_spatial, w_spatial_t, bias_full, sink_rows)


def _in_proj_bwd_kernel(x, dx1, dproj, vecs, w_in_t, comm=None):
    seq = x.shape[0]
    tm = 512

    def body(x_ref, dx1_ref, dp_ref, v_ref, w_ref, gx_ref, sums_ref):
        @pl.when(pl.program_id(0) == 0)
        def _():
            sums_ref[...] = jnp.zeros_like(sums_ref)

        g_mix, scale1 = v_ref[0:1, :], v_ref[2:3, :]
        dh = _dot(dp_ref[...], w_ref[...])
        xv = x_ref[...]
        rstd = lax.rsqrt(_mean_last(xv * xv) + EPS)
        xh = xv * rstd
        dn1 = dh * (1.0 + scale1)
        dxh = dn1 * g_mix
        gx_ref[...] = dx1_ref[...] + rstd * (dxh - xh * _mean_last(dxh * xh))
        sums_ref[0:1, :] += _rowsum(dh)
        sums_ref[1:2, :] += _rowsum(dh * (xh * g_mix))
        sums_ref[2:3, :] += _rowsum(dn1 * xh)

    tok = lambda w: pl.BlockSpec((tm, w), lambda i: (i, 0))
    return _hosted_call(
        body, comm, name="in_proj_bwd", grid=(seq // tm,),
        out_shape=[jax.ShapeDtypeStruct((seq, D_MODEL), F32), jax.ShapeDtypeStruct((8, D_MODEL), F32)],
        in_specs=[tok(D_MODEL), tok(D_MODEL), tok(IN_PROJ_WIDTH), _full((8, D_MODEL)),
                  _full((IN_PROJ_WIDTH, D_MODEL))],
        out_specs=[tok(D_MODEL), _full((8, D_MODEL))],
        semantics=("arbitrary",),
    )(x, dx1, dproj, vecs, w_in_t)


class _GradTiles(NamedTuple):
    tm: int
    tn: int
    n_tiles: int
    chips_per_tile: int
    a_index: Callable
    b_index: Callable


def _weight_grad_kernel(a, b, c_idx, name, tiles, comm=None):
    seq = a.shape[0]
    tk = min(seq, 4096)
    nk = seq // tk
    tm, tn, n_tiles, per = tiles.tm, tiles.tn, tiles.n_tiles, tiles.chips_per_tile
    rows = tm // per

    def half(phase, c):
        return phase * c[0] + (1 - phase) * (1 - c[0])

    def body(c_ref, a_ref, b_ref, o_ref, acc, stage, landed, send_sems, recv_sems):
        del c_ref
        phase, t, kk = pl.program_id(0), pl.program_id(1), pl.program_id(2)
        x, y, c, _ = _mesh_place()

        def copy(tile):
            return pltpu.make_async_remote_copy(
                src_ref=stage.at[tile], dst_ref=landed.at[tile], send_sem=send_sems.at[tile],
                recv_sem=recv_sems.at[tile], device_id=(x, y, 1 - c), device_id_type=MESH)

        @pl.when(kk == 0)
        def _():
            acc[...] = jnp.zeros_like(acc)

        acc[...] += _dot_tn(a_ref[...], b_ref[...])

        @pl.when((kk == nk - 1) & (phase == 0))
        def _():
            stage[t] = acc[...].astype(stage.dtype)
            copy(t).start()

        @pl.when((kk == nk - 1) & (phase == 1))
        def _():
            copy(t).wait_recv()
            total = acc[...] + landed[t].astype(F32)
            for q in range(per):
                o_ref[q] = total[rows * q:rows * (q + 1)].astype(o_ref.dtype)

        @pl.when((kk == nk - 1) & (phase == 1) & (t == n_tiles - 1))
        def _():
            for tile in range(n_tiles):
                copy(tile).wait_send()

    out = _hosted_call(
        body, comm, name=name, grid=(2, n_tiles, nk), n_prefetch=1,
        out_shape=[jax.ShapeDtypeStruct((n_tiles * per, rows, tn), GRAD_COMM_DTYPE)],
        in_specs=[pl.BlockSpec((tk, tm), lambda p, t, k, c: (k, tiles.a_index(t, half(p, c)))),
                  pl.BlockSpec((tk, tn), lambda p, t, k, c: (k, tiles.b_index(t, half(p, c))))],
        out_specs=[pl.BlockSpec((per, rows, tn), lambda p, t, k, c: (p * t, 0, 0))],
        scratch_shapes=[pltpu.VMEM((tm, tn), F32), pltpu.VMEM((n_tiles, tm, tn), GRAD_COMM_DTYPE),
                        pltpu.VMEM((n_tiles, tm, tn), GRAD_COMM_DTYPE),
                        pltpu.SemaphoreType.DMA((n_tiles,)), pltpu.SemaphoreType.DMA((n_tiles,))],
        semantics=("arbitrary", "arbitrary", "arbitrary"),
    )(c_idx, a, b)
    return out[0] if comm is None else out


def _row_tile(rows, most=256, sublanes=16):
    return max(t for t in range(sublanes, most + 1, sublanes) if rows % t == 0)


def _adam_update(w, g, m, v):
    m_new = ADAM_B1 * m + (1.0 - ADAM_B1) * g
    v_new = ADAM_B2 * v + (1.0 - ADAM_B2) * (g * g)
    m_hat = m_new / (1.0 - ADAM_B1 ** ADAM_STEP)
    v_hat = v_new / (1.0 - ADAM_B2 ** ADAM_STEP)
    delta = -ADAM_LR * (m_hat / (jnp.sqrt(v_hat) + ADAM_EPS) + ADAM_WD * w)
    return delta, m_new, v_new


def _sum_chips_kernel(own, others, place, name):
    _, r, n = own.shape
    tr = _row_tile(r)

    def body(place_ref, own_ref, oth_ref, o_ref):
        del place_ref
        acc = own_ref[...].astype(F32)
        for k in range(N_CHIPS - 1):
            acc = acc + oth_ref[k].astype(F32)
        o_ref[...] = acc

    return pl.pallas_call(
        body, name=name, out_shape=jax.ShapeDtypeStruct((2, r, n), F32),
        grid_spec=pltpu.PrefetchScalarGridSpec(
            num_scalar_prefetch=1, grid=(r // tr,),
            in_specs=[pl.BlockSpec((None, tr, n), lambda i, p: (p[0], i, 0)),
                      pl.BlockSpec((N_CHIPS - 1, tr, n), lambda i, p: (0, i, 0))],
            out_specs=pl.BlockSpec((None, tr, n), lambda i, p: (p[1], i, 0))),
        compiler_params=_params("parallel"),
    )(place, own, others)


def _adam_kernel(w, g, m, v, name, after=()):
    r, n = w.shape
    by_columns = g.shape[1] == r
    tr, tn = _row_tile(g.shape[1], most=512), g.shape[2]

    def body(w_ref, g_ref, m_ref, v_ref, *rest):
        g_out, d_ref, mo_ref, vo_ref = rest[len(after):]
        gv = g_ref[...]
        g_out[...] = gv
        d_ref[...], mo_ref[...], vo_ref[...] = _adam_update(w_ref[...], gv, m_ref[...], v_ref[...])

    steps = g.shape[1] // tr
    spec = pl.BlockSpec((tr, tn), (lambda h, i: (i, h)) if by_columns else (lambda h, i: (h * steps + i, 0)))
    return pl.pallas_call(
        body, name=name, grid=(2, steps), out_shape=[jax.ShapeDtypeStruct((r, n), F32)] * 4,
        in_specs=[spec, pl.BlockSpec((None, tr, tn), lambda h, i: (h, i, 0)), spec, spec] + [_any()] * len(after),
        out_specs=[spec] * 4, compiler_params=_params("parallel", "parallel"),
    )(w, g, m, v, *after)


SMALL_PARAMS = ("b_ada", "g_mix", "g_ffn", "g_final", "b_spatial", "sinks", "w_spatial")


def _small_update_kernel(gathered, params):
    shapes = [params[nm][0].shape for nm in SMALL_PARAMS]

    def body(*refs):
        g_refs, refs = refs[:5], refs[5:]
        p_refs, refs = refs[:3 * len(SMALL_PARAMS)], refs[3 * len(SMALL_PARAMS):]
        loss_ref, o_refs = refs[0], refs[1:]

        def total(ref):
            acc = ref[0].astype(F32)
            for k in range(1, N_DEV):
                acc = acc + ref[k].astype(F32)
            return acc

        s1, s2, db, ds, dw = (total(r) for r in g_refs)
        loss_ref[...] = jnp.broadcast_to(s2[6:7, 0:1], loss_ref.shape)
        grads = {"b_ada": [s1[0:1], s1[1:2], s2[5:6], s2[0:1], s2[1:2], s2[2:3]], "g_mix": [s1[2:3]],
                 "g_ffn": [s2[3:4]], "g_final": [s2[4:5]], "b_spatial": [db.T[0:GMLP_GROUPS]],
                 "w_spatial": [dw]}
        lane = lax.broadcasted_iota(jnp.int32, (1, LANES), 1)
        sink_row = jnp.zeros((1, LANES), F32)
        for h in range(N_Q_HEADS):
            sink_row = sink_row + jnp.where(lane == h, ds[h:h + 1, :], 0.0)
        grads["sinks"] = [sink_row[:, 0:N_Q_HEADS]]
        for i, nm in enumerate(SMALL_PARAMS):
            w_ref, m_ref, v_ref = p_refs[3 * i:3 * i + 3]
            outs = o_refs[4 * i:4 * i + 4]
            width = grads[nm][0].shape[1]
            for k, g in enumerate(grads[nm]):
                cols = slice(width * k, width * (k + 1))
                upd = _adam_update(w_ref[:, cols], g, m_ref[:, cols], v_ref[:, cols])
                for o_ref, val in zip(outs, (g,) + upd):
                    o_ref[:, cols] = val

    flat = [a for nm in SMALL_PARAMS for a in params[nm]]
    out_shape = [jax.ShapeDtypeStruct((8, LANES), F32)]
    out_shape += [jax.ShapeDtypeStruct(s, F32) for s in shapes for _ in range(4)]
    outs = pl.pallas_call(
        body, name="small_update", grid=(1,), out_shape=out_shape,
        in_specs=[_full(g.shape) for g in gathered] + [_full(a.shape) for a in flat],
        out_specs=[_full(s.shape) for s in out_shape],
        compiler_params=_params("arbitrary"),
    )(*gathered, *flat)
    return {nm: outs[1 + 4 * i:5 + 4 * i] for i, nm in enumerate(SMALL_PARAMS)}, outs[0]


def _ada_update_kernel(act_t, dmod, w, m, v):
    r, n = w.shape
    tr = 256

    def body(a_ref, d_ref, w_ref, m_ref, v_ref, g_ref, dl_ref, mo_ref, vo_ref):
        g = _dot(a_ref[...], d_ref[...])
        g_ref[...] = g
        dl_ref[...], mo_ref[...], vo_ref[...] = _adam_update(w_ref[...], g, m_ref[...], v_ref[...])

    spec = pl.BlockSpec((tr, n), lambda i: (i, 0))
    return pl.pallas_call(
        body, name="ada_update", grid=(r // tr,), out_shape=[jax.ShapeDtypeStruct((r, n), F32)] * 4,
        in_specs=[pl.BlockSpec((tr, N_DEV), lambda i: (i, 0)), _full((N_DEV, n)), spec, spec, spec],
        out_specs=[spec] * 4, compiler_params=_params("parallel"),
    )(act_t, dmod, w, m, v)


def kernel(x, c, positions, w_ada, b_ada, g_mix, w_in, w_spatial, b_spatial, sinks, w_out, g_ffn, w_ff1, w_ff2, g_final, loss_target, m_w_ada, m_b_ada, m_g_mix, m_w_in, m_w_spatial, m_b_spatial, m_sinks, m_w_out, m_g_ffn, m_w_ff1, m_w_ff2, m_g_final, v_w_ada, v_b_ada, v_g_mix, v_w_in, v_w_spatial, v_b_spatial, v_sinks, v_w_out, v_g_ffn, v_w_ff1, v_w_ff2, v_g_final):
    xi, yi, ci = lax.axis_index("x"), lax.axis_index("y"), lax.axis_index("c")
    chip = 2 * xi + yi
    dev = 2 * chip + ci
    seq = x.shape[1]
    x2, tgt = x[0], loss_target[0]
    ada_cols = w_ada.shape[2]

    big = {"w_in": tuple(a[0].T for a in (w_in, m_w_in, v_w_in)),
           "w_out": (w_out[0], m_w_out[0], v_w_out[0]), "w_ff1": (w_ff1[0], m_w_ff1[0], v_w_ff1[0]),
           "w_ff2": (w_ff2[0], m_w_ff2[0], v_w_ff2[0])}

    def halves(nm, zero=None):
        r, n = big[nm][0].shape
        w = big[nm][0] if zero is None else big[nm][0] + zero
        return w.astype(WEIGHT_COMM_DTYPE).reshape(2, r // 2, n)

    chip_idx = chip.reshape(1).astype(jnp.int32)
    first_c, first_w = ([c], [False], [True]), ([halves("w_in"), halves("w_out")], [True, True], [True, False])
    c_started, w_started = _first_stage_start([first_c, first_w], "gather_first_start")
    zero = c_started[-1][0, 0]
    trunk_weights = ["w_out", "w_ff1", "w_ff2"]
    shards = [halves("w_out"), halves("w_ff1", zero), halves("w_ff2", zero)]
    inv_freq = ROPE_THETA ** (-jnp.arange(0, ROT_DIM, 2, dtype=F32) / ROT_DIM) + zero
    rope_tab = _rope_lane_tables(*_rope_angle_kernel(positions, inv_freq.reshape(ROT_DIM // 2, 1)))
    c_all, = _first_stage_finish(c_started, *first_c[1:], [rope_tab], "gather_c_wait")
    b_shard = lax.dynamic_slice(b_ada, (0, chip * ada_cols), (1, ada_cols))
    mod_part, act = _mod_kernel(c_all.reshape(N_DEV, D_MODEL), w_ada[0], b_shard)
    w_in_t, g_out = _first_stage_finish(w_started, *first_w[1:], [mod_part] + shards[1:], "gather_first_wait")
    mod_all, w_in_t, g_out = _all_gather8([mod_part], "gather_mod", forward=[w_in_t, g_out])
    w_in_t = w_in_t.reshape(IN_PROJ_WIDTH, D_MODEL)
    mod_me = lax.dynamic_index_in_dim(mod_all[0::2], dev, axis=1, keepdims=False)
    mod_me = mod_me.reshape(N_MOD, D_MODEL)
    shift1, scale1, gate1, shift2, scale2, gate2 = (mod_me[k:k + 1] for k in range(N_MOD))

    zeros_row = jnp.zeros((1, D_MODEL), F32)
    vecs1 = jnp.concatenate([g_mix, shift1, scale1] + [zeros_row] * 5, axis=0)
    vecs2 = jnp.concatenate([gate1, shift2, scale2, gate2, g_ffn, g_final.reshape(1, D_MODEL)]
                            + [zeros_row] * 2, axis=0)
    bias_full = jnp.repeat(b_spatial[0].T, HEAD_DIM, axis=1)
    sink_rows = jnp.broadcast_to(sinks[0][:, None], (N_Q_HEADS, LANES))

    proj, hb, *staged = _in_proj_kernel(x2, vecs1, w_in_t, comm=_gather2d_first(shards[1:]))
    cat, *staged = _mixer_fwd_kernel(proj, rope_tab, w_spatial[0], bias_full, sink_rows,
                                     comm=_gather2d_second(staged, shards[1:]))
    staged = [g_out] + list(_gather_forward(staged, "gather_forward"))
    dx1, dcat, dmix, h2b, rb, dab, dffb, sums2 = _trunk_kernel(
        x2, tgt, cat, vecs2, chip_idx,
        [g.reshape((N_CHIPS,) + big[nm][0].shape) for nm, g in zip(trunk_weights, staged)],
        [s.reshape(big[nm][0].shape) for nm, s in zip(trunk_weights, shards)])

    c_idx = ci.reshape(1).astype(jnp.int32)
    place = jnp.stack([chip, ci]).astype(jnp.int32)
    half_d = D_MODEL // 2
    cs_ff2 = _weight_grad_kernel(rb, dffb, c_idx, "dw_ff2",
                                 _GradTiles(D_MODEL, half_d, N_CHIPS, 1, lambda t, h: t, lambda t, h: h))
    eighth = D_MODEL // 8
    cs_ff1, sc_ff2 = _weight_grad_kernel(
        h2b, dab, c_idx, "dw_ff1",
        _GradTiles(D_MODEL, half_d, N_CHIPS, 1, lambda t, h: 0, lambda t, h: 2 * t + h),
        comm=_scatter_job([cs_ff2], rows=(0, 6 * eighth)))
    cs_out, sc_ff2 = _weight_grad_kernel(
        cat, dmix, c_idx, "dw_out", _GradTiles(D_MODEL, half_d, 1, N_CHIPS, lambda t, h: 0, lambda t, h: h),
        comm=_scatter_job([cs_ff2], rows=(6 * eighth, eighth), into=[sc_ff2]))
    dproj, dw_spatial, db_lanes, dsink_rows, sc_ff2, sc_ff1, sc_out = _mixer_bwd_kernel(
        proj, rope_tab, dcat, w_spatial[0], w_spatial[0].transpose(0, 2, 1), bias_full, sink_rows,
        dev.reshape(1).astype(jnp.int32),
        comm=_merge_jobs(_scatter_job([cs_ff1, cs_out]),
                         _scatter_job([cs_ff2], rows=(7 * eighth, eighth), into=[sc_ff2])))
    totals = [_sum_chips_kernel(own, oth, place, "grad_sum_" + nm)
              for nm, own, oth in (("w_out", cs_out, sc_out), ("w_ff1", cs_ff1, sc_ff1), ("w_ff2", cs_ff2, sc_ff2))]
    small_slots = [db_lanes, dsink_rows, dw_spatial.reshape(N_DEV, GMLP_GROUPS * CHUNK, CHUNK)]
    cs_in, *rode = _weight_grad_kernel(
        dproj, hb, c_idx, "dw_in",
        _GradTiles(2 * W_IN_BLOCK, half_d, N_CHIPS // 2, 2, lambda t, h: t, lambda t, h: h),
        comm=_merge_jobs(_gather_job(small_slots), _share_job(totals)))
    small_stage1, shared = rode[:len(small_slots)], rode[len(small_slots):]
    scatter_in = _scatter_start(cs_in, "grad_to_chips_w_in_start")
    grad_x, sums1 = _in_proj_bwd_kernel(x2, dx1, dproj, vecs1 + scatter_in[-1][0:1, 0:1], w_in_t)
    cs_in, sc_in = _scatter_wait(scatter_in, sums1, "grad_to_chips_w_in_wait")
    total_in = _sum_chips_kernel(cs_in, sc_in, place, "grad_sum_w_in")
    gather_small = _gather_start([sums1, sums2], small_stage1, [total_in], cs_in, "gather_small_start")
    big_out = {}

    def update(nm, g, after=()):
        w, m, v = big[nm]
        outs = _adam_kernel(w, g, m, v, "adam_" + nm, after=after)
        big_out[nm] = tuple((t.T if nm == "w_in" else t)[None] for t in outs)
        return outs

    update("w_out", shared[0])
    covering = update("w_ff1", shared[1], after=gather_small[-1:])
    update("w_ff2", shared[2])
    *gathered, shared_in = _gather_wait(gather_small, 2, 1, covering[0], "gather_small_wait")
    update("w_in", shared_in)

    small = {"b_ada": (b_ada, m_b_ada, v_b_ada), "g_mix": (g_mix, m_g_mix, v_g_mix),
             "g_ffn": (g_ffn, m_g_ffn, v_g_ffn), "g_final": (g_final, m_g_final, v_g_final),
             "b_spatial": (b_spatial, m_b_spatial, v_b_spatial), "sinks": (sinks, m_sinks, v_sinks),
             "w_spatial": (w_spatial, m_w_spatial, v_w_spatial)}
    flat_shape = {"g_final": (1, D_MODEL), "b_spatial": (GMLP_GROUPS, CHUNK), "w_spatial": (GMLP_GROUPS * CHUNK, CHUNK)}
    small_out, loss_tile = _small_update_kernel(
        gathered, {nm: tuple(a.reshape(flat_shape.get(nm, a.shape)) for a in small[nm]) for nm in small})
    small_out = {nm: [o.reshape(small[nm][0].shape) for o in small_out[nm]] for nm in small}
    loss = loss_tile[0, 0]

    g1, g2 = gathered[0], gathered[1]
    dmod_all = jnp.concatenate([g1[:, 0], g1[:, 1], g2[:, 5], g2[:, 0], g2[:, 1], g2[:, 2]], axis=1)
    dmod_cols = lax.dynamic_slice(dmod_all, (0, chip * ada_cols), (N_DEV, ada_cols))
    ada = _ada_update_kernel(act.T, dmod_cols, w_ada[0], m_w_ada[0], v_w_ada[0])
    big_out["w_ada"] = tuple(t[None] for t in ada)

    order = ["w_ada", "b_ada", "g_mix", "w_in", "w_spatial", "b_spatial", "sinks", "w_out", "g_ffn",
             "w_ff1", "w_ff2", "g_final"]

    def leaf(nm, k):
        return big_out[nm][k] if nm in big_out else small_out[nm][k]

    outs = [loss, grad_x[None]]
    for k in range(4):
        outs += [leaf(nm, k) for nm in order]
    return tuple(outs)
```

```python
import math
from typing import Callable, NamedTuple

import jax
import jax.numpy as jnp
from jax import lax
from jax.experimental import pallas as pl
from jax.experimental.pallas import tpu as pltpu

F32 = jnp.float32
MXU_DTYPE = jnp.bfloat16
WEIGHT_COMM_DTYPE = jnp.bfloat16
GRAD_COMM_DTYPE = jnp.bfloat16

D_MODEL = 1024
D_FF = 4096
HEAD_DIM = 64
GMLP_GROUPS = 8
GMLP_WIDTH = 512
CHUNK = 128
N_Q_HEADS = 8
N_KV_HEADS = 2
ATTN_WIDTH = 512
KV_WIDTH = 128
ROT_DIM = 16
ROPE_THETA = 500000.0
IN_PROJ_WIDTH = 1792
N_MOD = 6
EPS = 1e-5
N_CHIPS = 4
N_DEV = 8
LANES = 128
W_IN_BLOCK = IN_PROJ_WIDTH // N_CHIPS

ADAM_LR = 0.001
ADAM_B1 = 0.9
ADAM_B2 = 0.999
ADAM_EPS = 1e-08
ADAM_WD = 0.01
ADAM_STEP = 10

VMEM_LIMIT_BYTES = 58 * 1024 * 1024
MESH = pl.DeviceIdType.MESH


def _params(*semantics):
    return pltpu.CompilerParams(dimension_semantics=semantics, vmem_limit_bytes=VMEM_LIMIT_BYTES)


def _dot(a, b):
    return jnp.dot(a.astype(MXU_DTYPE), b.astype(MXU_DTYPE), preferred_element_type=F32)


def _dot_nt(a, b):
    return lax.dot_general(a.astype(MXU_DTYPE), b.astype(MXU_DTYPE), (((1,), (1,)), ((), ())),
                           preferred_element_type=F32)


def _dot_tn(a, b):
    return lax.dot_general(a.astype(MXU_DTYPE), b.astype(MXU_DTYPE), (((0,), (0,)), ((), ())),
                           preferred_element_type=F32)


def _full(shape):
    return pl.BlockSpec(shape, lambda *_: (0,) * len(shape))


def _any():
    return pl.BlockSpec(memory_space=pl.ANY)


def _rowsum(v):
    return jnp.sum(v, axis=0, keepdims=True)


def _mean_last(v):
    return jnp.mean(v, axis=-1, keepdims=True)


class _Comm(NamedTuple):
    operands: tuple
    out_shapes: tuple
    n_sems: int
    make: Callable
    in_place: int = 0


def _hosted_call(body, comm, *, name, grid, in_specs, out_shape, out_specs, scratch_shapes=(), semantics,
                 n_prefetch=0):
    if comm is None:
        return pl.pallas_call(
            body, name=name, out_shape=out_shape, compiler_params=_params(*semantics),
            grid_spec=pltpu.PrefetchScalarGridSpec(
                num_scalar_prefetch=n_prefetch, grid=grid, in_specs=in_specs, out_specs=out_specs,
                scratch_shapes=list(scratch_shapes)))
    n_in, n_out, n_scr = len(in_specs), len(out_shape), len(scratch_shapes)
    k_in, k_out = len(comm.operands), len(comm.out_shapes)

    def hosted(*refs):
        prefetched, refs = refs[:n_prefetch], refs[n_prefetch:]
        ins, refs = refs[:n_in], refs[n_in:]
        c_ins, refs = refs[:k_in], refs[k_in:]
        outs, refs = refs[:n_out], refs[n_out:]
        c_outs, refs = refs[:k_out], refs[k_out:]
        scratch, (send_sems, recv_sems) = refs[:n_scr], refs[n_scr:]
        first, last = None, None
        for d, size in enumerate(grid):
            at_start, at_end = pl.program_id(d) == 0, pl.program_id(d) == size - 1
            first = at_start if first is None else first & at_start
            last = at_end if last is None else last & at_end

        @pl.when(first)
        def _():
            for cp in comm.make(c_ins, c_outs, send_sems, recv_sems)[0]:
                cp.start()

        body(*prefetched, *ins, *outs, *scratch)

        @pl.when(last)
        def _():
            for wait in comm.make(c_ins, c_outs, send_sems, recv_sems)[1]:
                wait()

    aliases = {n_prefetch + n_in + i: n_out + i for i in range(comm.in_place)}
    call = pl.pallas_call(
        hosted, name=name, out_shape=list(out_shape) + list(comm.out_shapes),
        compiler_params=_params(*semantics), input_output_aliases=aliases,
        grid_spec=pltpu.PrefetchScalarGridSpec(
            num_scalar_prefetch=n_prefetch, grid=grid, in_specs=list(in_specs) + [_any()] * k_in,
            out_specs=list(out_specs) + [_any()] * k_out,
            scratch_shapes=list(scratch_shapes) + [pltpu.SemaphoreType.DMA((comm.n_sems,)),
                                                    pltpu.SemaphoreType.DMA((comm.n_sems,))]))
    return lambda *args: call(*args, *comm.operands)


class _Shifted:
    def __init__(self, base, offset):
        self.base, self.offset = base, offset

    @property
    def at(self):
        return self

    def __getitem__(self, k):
        return self.base.at[self.offset + k]


def _merge_jobs(*jobs):
    def order(count):
        first = [(j, i) for j, job in enumerate(jobs) for i in range(job.in_place)]
        return first + [(j, i) for j, job in enumerate(jobs) for i in range(job.in_place, count(job))]

    op_order, out_order = order(lambda job: len(job.operands)), order(lambda job: len(job.out_shapes))

    def make(ins, outs, send_sems, recv_sems):
        starts, waits, sem = [], [], 0
        for j, job in enumerate(jobs):
            mine_in = [ins[k] for k, (jj, _) in enumerate(op_order) if jj == j]
            mine_out = [outs[k] for k, (jj, _) in enumerate(out_order) if jj == j]
            s, w = job.make(mine_in, mine_out, _Shifted(send_sems, sem), _Shifted(recv_sems, sem))
            starts, waits, sem = starts + s, waits + w, sem + job.n_sems
        return starts, waits

    return _Comm(tuple(jobs[j].operands[i] for j, i in op_order), tuple(jobs[j].out_shapes[i] for j, i in out_order),
                 sum(job.n_sems for job in jobs), make, in_place=sum(job.in_place for job in jobs))


def _mesh_place():
    x, y, c = lax.axis_index("x"), lax.axis_index("y"), lax.axis_index("c")
    return x, y, c, [(1 - x, y), (x, 1 - y), (1 - x, 1 - y)]


def _gather_job(bufs):
    per = 4

    def make(ins, outs, send_sems, recv_sems):
        del ins
        x, y, c, chips = _mesh_place()
        starts, waits = [], []
        for a, out in enumerate(outs):
            mine = src = out.at[4 * x + 2 * y + c]
            to = [(x, y, 1 - c)] + [(px, py, c) for px, py in chips]
            sends = [pltpu.make_async_remote_copy(
                src_ref=src, dst_ref=mine, send_sem=send_sems.at[per * a + k],
                recv_sem=recv_sems.at[per * a + k], device_id=dev, device_id_type=MESH)
                for k, dev in enumerate(to)]
            recvs = [pltpu.make_async_remote_copy(
                src_ref=src, dst_ref=out.at[4 * px + 2 * py + pc], send_sem=send_sems.at[per * a + k],
                recv_sem=recv_sems.at[per * a + k], device_id=(px, py, pc), device_id_type=MESH)
                for k, (px, py, pc) in enumerate(to)]
            starts += sends
            waits += [s.wait_send for s in sends] + [r.wait_recv for r in recvs]
        return starts, waits

    shapes = tuple(jax.ShapeDtypeStruct(b.shape, b.dtype) for b in bufs)
    return _Comm(tuple(bufs), shapes, per * len(bufs), make, in_place=len(bufs))


def _slots(x, y, c):
    return 4 * x + 2 * y + c, 4 * (1 - x) + 2 * y + c, 4 * x + 2 * (1 - y) + c, 4 * (1 - x) + 2 * (1 - y) + c


def _gather2d_first(halves):
    per = 2

    def make(ins, outs, send_sems, recv_sems):
        x, y, c, _ = _mesh_place()
        me, xn, yn, _ = _slots(x, y, c)
        starts, waits = [], []
        for a, (src, out) in enumerate(zip(ins, outs)):
            blk = src.at[c]
            rows = blk.shape[0] // 2
            upper, lower = pl.ds(0, rows), pl.ds(rows, rows)

            def copy(k, src_ref, dst_ref, dev, a=a):
                return pltpu.make_async_remote_copy(
                    src_ref=src_ref, dst_ref=dst_ref, send_sem=send_sems.at[per * a + k],
                    recv_sem=recv_sems.at[per * a + k], device_id=dev, device_id_type=MESH)

            sends = [copy(0, blk.at[upper], out.at[me, upper], (1 - x, y, c)),
                     copy(1, blk.at[lower], out.at[me, lower], (x, 1 - y, c))]
            recvs = [copy(0, blk.at[upper], out.at[xn, upper], (1 - x, y, c)),
                     copy(1, blk.at[lower], out.at[yn, lower], (x, 1 - y, c))]
            starts += sends
            waits += [s.wait_send for s in sends] + [r.wait_recv for r in recvs]
        return starts, waits

    shapes = tuple(jax.ShapeDtypeStruct((N_DEV,) + h.shape[1:], h.dtype) for h in halves)
    return _Comm(tuple(halves), shapes, per * len(halves), make)


def _gather2d_second(bufs, halves):
    per = 4
    n_arr = len(bufs)

    def make(ins, outs, send_sems, recv_sems):
        x, y, c, _ = _mesh_place()
        me, xn, yn, dg = _slots(x, y, c)
        starts, waits = [], []
        for a, buf in enumerate(outs):
            own = ins[n_arr + a].at[c]
            rows = buf.shape[1] // 2
            upper, lower = pl.ds(0, rows), pl.ds(rows, rows)
            plan = [(own.at[upper], me, upper, (x, 1 - y, c), yn), (buf.at[xn, upper], xn, upper, (x, 1 - y, c), dg),
                    (own.at[lower], me, lower, (1 - x, y, c), xn), (buf.at[yn, lower], yn, lower, (1 - x, y, c), dg)]
            for k, (src, slot, part, dev, landing) in enumerate(plan):
                sems = dict(send_sem=send_sems.at[per * a + k], recv_sem=recv_sems.at[per * a + k],
                            device_id=dev, device_id_type=MESH)
                send = pltpu.make_async_remote_copy(src_ref=src, dst_ref=buf.at[slot, part], **sems)
                arrival = pltpu.make_async_remote_copy(src_ref=src, dst_ref=buf.at[landing, part], **sems)
                starts.append(send)
                waits += [send.wait_send, arrival.wait_recv]
        return starts, waits

    shapes = tuple(jax.ShapeDtypeStruct(b.shape, b.dtype) for b in bufs)
    return _Comm(tuple(bufs) + tuple(halves), shapes, per * n_arr, make, in_place=n_arr)


def _gather_forward(bufs, name):
    n_arr = len(bufs)

    def body(*refs):
        outs = refs[n_arr:2 * n_arr]
        send_sems, recv_sems = refs[2 * n_arr:]
        x, y, c, chips = _mesh_place()
        sends, recvs = [], []
        for a, buf in enumerate(outs):
            for j, (px, py) in enumerate(chips):
                mine, theirs = buf.at[4 * px + 2 * py + c], buf.at[4 * px + 2 * py + 1 - c]
                sems = dict(send_sem=send_sems.at[3 * a + j], recv_sem=recv_sems.at[3 * a + j],
                            device_id=(x, y, 1 - c), device_id_type=MESH)
                sends.append(pltpu.make_async_remote_copy(src_ref=mine, dst_ref=mine, **sems))
                recvs.append(pltpu.make_async_remote_copy(src_ref=mine, dst_ref=theirs, **sems))
        for cp in sends:
            cp.start()
        for s, r in zip(sends, recvs):
            s.wait_send()
            r.wait_recv()

    return pl.pallas_call(
        body, name=name, out_shape=[jax.ShapeDtypeStruct(b.shape, b.dtype) for b in bufs],
        in_specs=[_any()] * n_arr, out_specs=[_any()] * n_arr,
        input_output_aliases={a: a for a in range(n_arr)},
        scratch_shapes=[pltpu.SemaphoreType.DMA((3 * n_arr,)), pltpu.SemaphoreType.DMA((3 * n_arr,))],
    )(*bufs)


def _scatter_job(chip_sums, rows=None, into=()):
    n_into = len(into)

    def part(ref):
        return ref if rows is None else ref.at[pl.ds(rows[0], rows[1])]

    def make(ins, outs, send_sems, recv_sems):
        x, y, c, chips = _mesh_place()
        copies = [pltpu.make_async_remote_copy(
            src_ref=part(src.at[2 * px + py]), dst_ref=part(out.at[j]), send_sem=send_sems.at[3 * a + j],
            recv_sem=recv_sems.at[3 * a + j], device_id=(px, py, c), device_id_type=MESH)
            for a, (src, out) in enumerate(zip(ins[n_into:], outs)) for j, (px, py) in enumerate(chips)]
        return copies, [cp.wait for cp in copies]

    shapes = tuple(jax.ShapeDtypeStruct((3,) + s.shape[1:], s.dtype) for s in chip_sums)
    return _Comm(tuple(into) + tuple(chip_sums), shapes, 3 * len(chip_sums), make, in_place=n_into)


def _all_gather8(blocks, name, split=False, forward=(), riders=(), skip_own=()):
    n_arr, n_fwd = len(blocks), len(forward)
    splits = list(split) if isinstance(split, (list, tuple)) else [split] * n_arr
    own_slots = [a not in skip_own for a in range(n_arr)]
    rider_in = sum(len(r.operands) for r in riders)
    rider_out = sum(len(r.out_shapes) for r in riders)

    def body(*refs):
        x_refs, refs = refs[:n_arr], refs[n_arr + n_fwd:]
        r_ins, refs = refs[:rider_in], refs[rider_in:]
        out_refs, refs = refs[:n_arr], refs[n_arr:]
        fwd_refs, refs = refs[:n_fwd], refs[n_fwd:]
        r_outs, refs = refs[:rider_out], refs[rider_out:]
        (send_sems, recv_sems, local_sems), rider_sems = refs[:3], refs[3:]
        x, y, c, chips = _mesh_place()
        me, sibling = (x, y, c), (x, y, 1 - c)
        passing = []
        for f, buf in enumerate(fwd_refs):
            for j, (px, py) in enumerate(chips):
                mine, theirs = buf.at[4 * px + 2 * py + c], buf.at[4 * px + 2 * py + 1 - c]
                sems = dict(send_sem=send_sems.at[7 * n_arr + 3 * f + j], recv_sem=recv_sems.at[7 * n_arr + 3 * f + j],
                            device_id=sibling, device_id_type=MESH)
                passing.append((pltpu.make_async_remote_copy(src_ref=mine, dst_ref=mine, **sems),
                                pltpu.make_async_remote_copy(src_ref=mine, dst_ref=theirs, **sems)))
        for send, _ in passing:
            send.start()
        arrays = []
        for a, (x_ref, out_ref) in enumerate(zip(x_refs, out_refs)):
            src_mine = x_ref.at[c] if splits[a] else x_ref

            def copy(k, blk, to, src=None, a=a, out_ref=out_ref):
                dst = out_ref.at[4 * blk[0] + 2 * blk[1] + blk[2]]
                return pltpu.make_async_remote_copy(
                    src_ref=dst if src is None else src, dst_ref=dst,
                    send_sem=send_sems.at[7 * a + k], recv_sem=recv_sems.at[7 * a + k],
                    device_id=to, device_id_type=MESH)

            mine = pltpu.make_async_copy(src_mine, out_ref.at[4 * x + 2 * y + c], local_sems.at[a])
            first = [copy(0, me, sibling, src=src_mine)] if own_slots[a] else []
            first += [copy(1 + j, me, (*chip, c), src=src_mine) for j, chip in enumerate(chips)]
            for cp in first + ([mine] if own_slots[a] else []):
                cp.start()
            arrays.append((copy, mine, first, own_slots[a]))
        rider_waits, i0, o0 = [], 0, 0
        for n, job in enumerate(riders):
            k_in, k_out = len(job.operands), len(job.out_shapes)
            starts, waits = job.make(r_ins[i0:i0 + k_in], r_outs[o0:o0 + k_out],
                                     rider_sems[2 * n], rider_sems[2 * n + 1])
            for cp in starts:
                cp.start()
            rider_waits += waits
            i0, o0 = i0 + k_in, o0 + k_out
        sent = []
        for copy, mine, first, own in arrays:
            passed = [copy(4 + j, (*chip, c), sibling) for j, chip in enumerate(chips)]
            for j, chip in enumerate(chips):
                copy(1 + j, (*chip, c), me).wait_recv()
                passed[j].start()
            sent += first + passed
        for copy, mine, first, own in arrays:
            if own:
                copy(0, sibling, me).wait_recv()
                mine.wait()
            for j, chip in enumerate(chips):
                copy(4 + j, (*chip, 1 - c), me).wait_recv()
        for cp in sent:
            cp.wait_send()
        for send, arrival in passing:
            send.wait_send()
            arrival.wait_recv()
        for wait in rider_waits:
            wait()

    n_sems = 7 * n_arr + 3 * n_fwd
    rider_operands = [a for r in riders for a in r.operands]
    rider_shapes = [s for r in riders for s in r.out_shapes]
    return pl.pallas_call(
        body, name=name,
        out_shape=[jax.ShapeDtypeStruct((N_DEV,) + tuple(b.shape[1:] if s else b.shape), b.dtype)
                   for b, s in zip(blocks, splits)]
        + [jax.ShapeDtypeStruct(f.shape, f.dtype) for f in forward] + rider_shapes,
        in_specs=[_any()] * (n_arr + n_fwd + rider_in), out_specs=[_any()] * (n_arr + n_fwd + rider_out),
        input_output_aliases={n_arr + f: n_arr + f for f in range(n_fwd)},
        scratch_shapes=[pltpu.SemaphoreType.DMA((n_sems,)), pltpu.SemaphoreType.DMA((n_sems,)),
                        pltpu.SemaphoreType.DMA((n_arr,))]
        + [pltpu.SemaphoreType.DMA((r.n_sems,)) for r in riders for _ in range(2)],
    )(*blocks, *forward, *rider_operands)


def _first_stage_copies(srcs, lands, splits, owns, send_sems, recv_sems, local_sems):
    x, y, c, chips = _mesh_place()
    same_core = [(px, py, c) for px, py in chips]
    everyone = [(x, y, 1 - c)] + [(px, py, pc) for px, py in chips for pc in (c, 1 - c)]
    per_array, local, k = [], [], 0
    for a, (src, land, split, own) in enumerate(zip(srcs, lands, splits, owns)):
        mine = src.at[c] if split else src
        peers = (([(x, y, 1 - c)] if own else []) + same_core) if split else everyone
        pairs = []
        for px, py, pc in peers:
            sems = dict(send_sem=send_sems.at[k], recv_sem=recv_sems.at[k],
                        device_id=(px, py, pc), device_id_type=MESH)
            pairs.append((pltpu.make_async_remote_copy(src_ref=mine, dst_ref=land.at[4 * x + 2 * y + c], **sems),
                          pltpu.make_async_remote_copy(src_ref=mine, dst_ref=land.at[4 * px + 2 * py + pc], **sems)))
            k += 1
        per_array.append(pairs)
        if own:
            local.append(pltpu.make_async_copy(mine, land.at[4 * x + 2 * y + c], local_sems.at[a]))
    return per_array, local


def _first_stage_start(groups, name):
    n_groups = len(groups)
    hbm, sem = pl.BlockSpec(memory_space=pltpu.HBM), pl.BlockSpec(memory_space=pltpu.SEMAPHORE)
    operands, sem_shapes, offsets = [], [], []
    for blocks, splits, owns in groups:
        n_sems = sum((4 if own else 3) if split else N_DEV - 1 for split, own in zip(splits, owns))
        sem_shapes += [pltpu.SemaphoreType.DMA((n_sems,)), pltpu.SemaphoreType.DMA((n_sems,)),
                       pltpu.SemaphoreType.DMA((len(blocks),))]
        offsets.append(len(operands))
        operands += list(blocks) + [lax.empty((N_DEV,) + tuple(b.shape[1:] if split else b.shape), b.dtype)
                                    for b, split in zip(blocks, splits)]
    n_ops = len(operands)

    def body(*refs):
        token = refs[-1]
        for g, (blocks, splits, owns) in enumerate(groups):
            n, at = len(blocks), offsets[g]
            per_array, local = _first_stage_copies(refs[at:at + n], refs[at + n:at + 2 * n], splits, owns,
                                                   *refs[n_ops + 3 * g:n_ops + 3 * g + 3])
            for pairs in per_array:
                for send, _ in pairs:
                    send.start()
            for cp in local:
                cp.start()
        token[...] = jnp.zeros_like(token)

    operands = [pltpu.with_memory_space_constraint(a, pltpu.HBM) for a in operands]
    out = pl.pallas_call(
        body, name=name,
        out_shape=sem_shapes + [pltpu.HBM(a.shape, a.dtype) for a in operands]
        + [jax.ShapeDtypeStruct((8, LANES), F32)],
        in_specs=[hbm] * n_ops,
        out_specs=[sem] * (3 * n_groups) + [hbm] * n_ops + [pl.BlockSpec(memory_space=pltpu.VMEM)],
        input_output_aliases={i: 3 * n_groups + i for i in range(n_ops)},
        compiler_params=pltpu.CompilerParams(has_side_effects=pltpu.SideEffectType.DATAFLOW_SIDE_EFFECTING),
    )(*operands)
    thru = out[3 * n_groups:-1]
    return [list(out[3 * g:3 * g + 3]) + list(thru[offsets[g]:offsets[g] + 2 * len(groups[g][0])]) + [out[-1]]
            for g in range(n_groups)]


def _first_stage_finish(started, splits, owns, after, name, thru=()):
    send_sems, recv_sems, local_sems, *bufs, _ = started
    n = len(bufs) // 2
    hbm, sem = pl.BlockSpec(memory_space=pltpu.HBM), pl.BlockSpec(memory_space=pltpu.SEMAPHORE)

    def body(*refs):
        srcs, lands = refs[:n], refs[n:2 * n]
        send_sems, recv_sems, local_sems = refs[2 * n:2 * n + 3]
        per_array, local = _first_stage_copies(srcs, lands, splits, owns, send_sems, recv_sems, local_sems)
        for pairs in per_array:
            for send, arrival in pairs:
                send.wait_send()
                arrival.wait_recv()
        for cp in local:
            cp.wait()

    first_thru = 2 * n + 3 + len(after)
    out = pl.pallas_call(
        body, name=name,
        out_shape=[pltpu.HBM(b.shape, b.dtype) for b in bufs] + [jax.ShapeDtypeStruct(t.shape, t.dtype) for t in thru],
        in_specs=[hbm] * (2 * n) + [sem, sem, sem] + [_any()] * (len(after) + len(thru)),
        out_specs=[hbm] * (2 * n) + [_any()] * len(thru),
        input_output_aliases={**{i: i for i in range(2 * n)}, **{first_thru + j: 2 * n + j for j in range(len(thru))}},
        compiler_params=pltpu.CompilerParams(has_side_effects=pltpu.SideEffectType.DATAFLOW_SIDE_EFFECTING),
    )(*bufs, send_sems, recv_sems, local_sems, *after, *thru)
    return out[n:]


def _split_scatter_copies(src, land, send_sems, recv_sems):
    x, y, c, chips = _mesh_place()
    return [pltpu.make_async_remote_copy(
        src_ref=src.at[2 * px + py], dst_ref=land.at[j], send_sem=send_sems.at[j], recv_sem=recv_sems.at[j],
        device_id=(px, py, c), device_id_type=MESH) for j, (px, py) in enumerate(chips)]


def _scatter_start(chip_sums, name):
    hbm, sem = pl.BlockSpec(memory_space=pltpu.HBM), pl.BlockSpec(memory_space=pltpu.SEMAPHORE)

    def body(src, land, send_sems, recv_sems, src_thru, land_thru, token):
        del src_thru, land_thru
        for cp in _split_scatter_copies(src, land, send_sems, recv_sems):
            cp.start()
        token[...] = jnp.zeros_like(token)

    land = lax.empty((N_CHIPS - 1,) + chip_sums.shape[1:], chip_sums.dtype)
    operands = [pltpu.with_memory_space_constraint(a, pltpu.HBM) for a in (chip_sums, land)]
    return pl.pallas_call(
        body, name=name,
        out_shape=[pltpu.SemaphoreType.DMA((N_CHIPS - 1,)), pltpu.SemaphoreType.DMA((N_CHIPS - 1,))]
        + [pltpu.HBM(a.shape, a.dtype) for a in operands] + [jax.ShapeDtypeStruct((8, LANES), F32)],
        in_specs=[hbm, hbm], out_specs=[sem, sem, hbm, hbm, pl.BlockSpec(memory_space=pltpu.VMEM)],
        input_output_aliases={0: 2, 1: 3},
        compiler_params=pltpu.CompilerParams(has_side_effects=pltpu.SideEffectType.DATAFLOW_SIDE_EFFECTING),
    )(*operands)


def _scatter_wait(started, after, name):
    send_sems, recv_sems, src, land, _ = started
    hbm, sem = pl.BlockSpec(memory_space=pltpu.HBM), pl.BlockSpec(memory_space=pltpu.SEMAPHORE)

    def body(src, land, send_sems, recv_sems, after_ref, src_thru, land_thru):
        del after_ref, src_thru, land_thru
        for cp in _split_scatter_copies(src, land, send_sems, recv_sems):
            cp.wait()

    return pl.pallas_call(
        body, name=name, out_shape=[pltpu.HBM(src.shape, src.dtype), pltpu.HBM(land.shape, land.dtype)],
        in_specs=[hbm, hbm, sem, sem, _any()], out_specs=[hbm, hbm], input_output_aliases={0: 0, 1: 1},
        compiler_params=pltpu.CompilerParams(has_side_effects=pltpu.SideEffectType.DATAFLOW_SIDE_EFFECTING),
    )(src, land, send_sems, recv_sems, after)


def _split_gather_copies(srcs, lands, fwds, shares, send_sems, recv_sems):
    x, y, c, chips = _mesh_place()
    peers = [(x, y, 1 - c)] + [(px, py, pc) for px, py in chips for pc in (c, 1 - c)]
    pairs = []
    for a, (src, land) in enumerate(zip(srcs, lands)):
        for k, (px, py, pc) in enumerate(peers):
            sems = dict(send_sem=send_sems.at[7 * a + k], recv_sem=recv_sems.at[7 * a + k],
                        device_id=(px, py, pc), device_id_type=MESH)
            pairs.append((pltpu.make_async_remote_copy(src_ref=src, dst_ref=land.at[4 * x + 2 * y + c], **sems),
                          pltpu.make_async_remote_copy(src_ref=src, dst_ref=land.at[4 * px + 2 * py + pc], **sems)))
    for f, buf in enumerate(fwds):
        for j, (px, py) in enumerate(chips):
            mine, theirs = buf.at[4 * px + 2 * py + c], buf.at[4 * px + 2 * py + 1 - c]
            k = 7 * len(srcs) + 3 * f + j
            sems = dict(send_sem=send_sems.at[k], recv_sem=recv_sems.at[k],
                        device_id=(x, y, 1 - c), device_id_type=MESH)
            pairs.append((pltpu.make_async_remote_copy(src_ref=mine, dst_ref=mine, **sems),
                          pltpu.make_async_remote_copy(src_ref=mine, dst_ref=theirs, **sems)))
    for s, buf in enumerate(shares):
        k = 7 * len(srcs) + 3 * len(fwds) + s
        sems = dict(send_sem=send_sems.at[k], recv_sem=recv_sems.at[k], device_id=(x, y, 1 - c), device_id_type=MESH)
        pairs.append((pltpu.make_async_remote_copy(src_ref=buf.at[c], dst_ref=buf.at[c], **sems),
                      pltpu.make_async_remote_copy(src_ref=buf.at[c], dst_ref=buf.at[1 - c], **sems)))
    return pairs


def _gather_start(blocks, forward, shares, after, name):
    n, n_fwd = len(blocks), len(forward)
    n_sems = 7 * n + 3 * n_fwd + len(shares)
    n_bufs = 2 * n + n_fwd + len(shares)
    hbm, sem = pl.BlockSpec(memory_space=pltpu.HBM), pl.BlockSpec(memory_space=pltpu.SEMAPHORE)

    def body(*refs):
        srcs, lands, fwds, swaps = refs[:n], refs[n:2 * n], refs[2 * n:2 * n + n_fwd], refs[2 * n + n_fwd:n_bufs]
        send_sems, recv_sems = refs[n_bufs + 1:n_bufs + 3]
        token, local_sems = refs[-2:]
        x, y, c, _ = _mesh_place()
        own = [pltpu.make_async_copy(src, land.at[4 * x + 2 * y + c], local_sems.at[a])
               for a, (src, land) in enumerate(zip(srcs, lands))]
        for cp in own:
            cp.start()
        for send, _ in _split_gather_copies(srcs, lands, fwds, swaps, send_sems, recv_sems):
            send.start()
        token[...] = jnp.zeros_like(token)
        for cp in own:
            cp.wait()

    lands = [lax.empty((N_DEV,) + b.shape, b.dtype) for b in blocks]
    operands = [pltpu.with_memory_space_constraint(a, pltpu.HBM)
                for a in list(blocks) + lands + list(forward) + list(shares)]
    return pl.pallas_call(
        body, name=name,
        out_shape=[pltpu.SemaphoreType.DMA((n_sems,)), pltpu.SemaphoreType.DMA((n_sems,))]
        + [pltpu.HBM(a.shape, a.dtype) for a in operands] + [jax.ShapeDtypeStruct((8, LANES), F32)],
        in_specs=[hbm] * len(operands) + [_any()],
        out_specs=[sem, sem] + [hbm] * len(operands) + [pl.BlockSpec(memory_space=pltpu.VMEM)],
        input_output_aliases={i: 2 + i for i in range(len(operands))},
        scratch_shapes=[pltpu.SemaphoreType.DMA((n,))],
        compiler_params=pltpu.CompilerParams(has_side_effects=pltpu.SideEffectType.DATAFLOW_SIDE_EFFECTING),
    )(*operands, after)


def _gather_wait(started, n, n_shares, after, name):
    send_sems, recv_sems, *bufs, _ = started
    n_bufs = len(bufs)
    n_fwd = n_bufs - 2 * n - n_shares
    hbm, sem = pl.BlockSpec(memory_space=pltpu.HBM), pl.BlockSpec(memory_space=pltpu.SEMAPHORE)

    def body(*refs):
        srcs, lands, fwds, swaps = refs[:n], refs[n:2 * n], refs[2 * n:2 * n + n_fwd], refs[2 * n + n_fwd:n_bufs]
        send_sems, recv_sems = refs[n_bufs:n_bufs + 2]
        for send, arrival in _split_gather_copies(srcs, lands, fwds, swaps, send_sems, recv_sems):
            send.wait_send()
            arrival.wait_recv()

    out = pl.pallas_call(
        body, name=name, out_shape=[pltpu.HBM(b.shape, b.dtype) for b in bufs],
        in_specs=[hbm] * len(bufs) + [sem, sem, _any()], out_specs=[hbm] * len(bufs),
        input_output_aliases={i: i for i in range(len(bufs))},
        compiler_params=pltpu.CompilerParams(has_side_effects=pltpu.SideEffectType.DATAFLOW_SIDE_EFFECTING),
    )(*bufs, send_sems, recv_sems, after)
    return out[n:]


def _share_job(bufs):
    def make(ins, outs, send_sems, recv_sems):
        del ins
        x, y, c, _ = _mesh_place()
        sems = lambda a: dict(send_sem=send_sems.at[a], recv_sem=recv_sems.at[a],
                              device_id=(x, y, 1 - c), device_id_type=MESH)
        sends = [pltpu.make_async_remote_copy(src_ref=o.at[c], dst_ref=o.at[c], **sems(a)) for a, o in enumerate(outs)]
        arrivals = [pltpu.make_async_remote_copy(src_ref=o.at[c], dst_ref=o.at[1 - c], **sems(a))
                    for a, o in enumerate(outs)]
        return sends, [s.wait_send for s in sends] + [r.wait_recv for r in arrivals]

    shapes = tuple(jax.ShapeDtypeStruct(b.shape, b.dtype) for b in bufs)
    return _Comm(tuple(bufs), shapes, len(bufs), make, in_place=len(bufs))


def _gelu_tanh(z):
    k = math.sqrt(2.0 / math.pi)
    t = jnp.tanh(k * (z + 0.044715 * (z * z * z)))
    return 0.5 * z * (1.0 + t), t


def _gelu_tanh_grad(z, t):
    k = math.sqrt(2.0 / math.pi)
    return 0.5 * (1.0 + t) + 0.5 * z * (1.0 - t * t) * (k * (1.0 + 3.0 * 0.044715 * (z * z)))


def _rope_angle_kernel(pos_row, invf_col):
    seq = pos_row.shape[1]

    def body(p_ref, f_ref, cos_ref, sin_ref):
        ang = p_ref[...].astype(F32) * f_ref[...]
        cos_ref[...] = jnp.cos(ang)
        sin_ref[...] = jnp.sin(ang)

    return pl.pallas_call(
        body, name="rope_angles", grid=(1,), out_shape=[jax.ShapeDtypeStruct((ROT_DIM // 2, seq), F32)] * 2,
        in_specs=[_full((1, seq)), _full((ROT_DIM // 2, 1))], out_specs=[_full((ROT_DIM // 2, seq))] * 2,
        compiler_params=_params("arbitrary"),
    )(pos_row, invf_col)


def _rope_lane_tables(cos, sin):
    cos_t, sin_t = cos.T, sin.T
    seq, half = cos_t.shape
    ones = jnp.ones((seq, HEAD_DIM - ROT_DIM), F32)
    c64 = jnp.concatenate([cos_t, cos_t, ones], axis=1)
    s1 = jnp.concatenate([sin_t, jnp.zeros((seq, HEAD_DIM - half), F32)], axis=1)
    s2 = jnp.concatenate([jnp.zeros((seq, half), F32), sin_t, jnp.zeros((seq, HEAD_DIM - ROT_DIM), F32)], axis=1)
    return jnp.concatenate([jnp.tile(t, (1, LANES // HEAD_DIM)) for t in (c64, s1, s2)], axis=1)


def _rope_apply(t, tab, sign):
    reps = t.shape[1] // LANES
    c_tab, s1, s2 = (jnp.tile(tab[:, LANES * k:LANES * (k + 1)], (1, reps)) if reps > 1
                     else tab[:, LANES * k:LANES * (k + 1)] for k in range(3))
    half = ROT_DIM // 2
    up = pltpu.roll(t, t.shape[1] - half, 1)
    down = pltpu.roll(t, half, 1)
    return t * c_tab + sign * (down * s2 - up * s1)


def _lane_masks(shape):
    lane = lax.broadcasted_iota(jnp.int32, shape, 1)
    return lane < HEAD_DIM, lane >= HEAD_DIM


HEADS_PER_GROUP = N_Q_HEADS // N_KV_HEADS
ATTN_SCALE = 1.0 / math.sqrt(HEAD_DIM)


def _attn_bias_t(first_block):
    kj = lax.broadcasted_iota(jnp.int32, (2 * CHUNK, CHUNK), 0)
    qi = lax.broadcasted_iota(jnp.int32, (2 * CHUNK, CHUNK), 1)
    ok = (kj > qi) & (kj <= qi + CHUNK)
    if first_block is not None:
        ok = ok & (jnp.logical_not(first_block) | (kj >= CHUNK))
    return jnp.tile(jnp.where(ok, 0.0, -jnp.inf), (1, HEADS_PER_GROUP))


def _group_rows(x, g, lo, hi):
    rows = []
    for r in range(HEADS_PER_GROUP):
        h = HEADS_PER_GROUP * g + r
        pair = x[:, LANES * (h // 2):LANES * (h // 2 + 1)]
        rows.append(jnp.where(hi if h % 2 else lo, pair, 0.0))
    return jnp.concatenate(rows, axis=0)


def _pairs_from_rows(rows, lo):
    return [jnp.where(lo, rows[2 * CHUNK * k:2 * CHUNK * k + CHUNK], rows[2 * CHUNK * k + CHUNK:2 * CHUNK * (k + 1)])
            for k in range(HEADS_PER_GROUP // 2)]


def _group_dup(a, b, g, lo2):
    return jnp.where(lo2, a, b) if g == 0 else jnp.where(lo2, b, a)


def _sink_row(sink_ref, g):
    return jnp.concatenate([sink_ref[HEADS_PER_GROUP * g + r:HEADS_PER_GROUP * g + r + 1, :]
                            for r in range(HEADS_PER_GROUP)], axis=1)


def _attn_probs_t(k_dup, q_rows, bias_t, sink_row):
    s_t = _dot_nt(k_dup, q_rows) * ATTN_SCALE + bias_t
    m = jnp.maximum(jnp.max(s_t, axis=0, keepdims=True), sink_row)
    p = jnp.exp(s_t - m)
    e_sink = jnp.exp(sink_row - m)
    inv = 1.0 / (jnp.sum(p, axis=0, keepdims=True) + e_sink)
    return p * inv, e_sink * inv


def _sgu_forward_pair(wm, vp, j):
    lo, hi = _lane_masks(vp.shape)
    lhs = jnp.concatenate([wm[2 * j], wm[2 * j + 1]], axis=1)
    rhs = jnp.concatenate([jnp.where(lo, vp, 0.0), jnp.where(hi, vp, 0.0)], axis=0)
    return _dot(lhs, rhs)


def _masked_spatial(w_ref):
    t = lax.broadcasted_iota(jnp.int32, (CHUNK, CHUNK), 0)
    s = lax.broadcasted_iota(jnp.int32, (CHUNK, CHUNK), 1)
    tril = s <= t
    return [jnp.where(tril, w_ref[g], 0.0) for g in range(GMLP_GROUPS)], tril, s >= t


def _mod_kernel(c_all, w_shard, b_shard, comm=None):
    n = w_shard.shape[1]
    tn = 512

    def body(c_ref, w_ref, b_ref, mod_ref, act_ref):
        cv = c_ref[...]
        act = cv * (1.0 / (1.0 + jnp.exp(-cv)))
        act_ref[...] = act
        mod_ref[...] = _dot(act, w_ref[...]) + b_ref[...]

    return _hosted_call(
        body, comm, name="ada_mod", grid=(n // tn,),
        out_shape=[jax.ShapeDtypeStruct((N_DEV, n), F32), jax.ShapeDtypeStruct((N_DEV, D_MODEL), F32)],
        in_specs=[_full((N_DEV, D_MODEL)), pl.BlockSpec((D_MODEL, tn), lambda i: (0, i)),
                  pl.BlockSpec((1, tn), lambda i: (0, i))],
        out_specs=[pl.BlockSpec((N_DEV, tn), lambda i: (0, i)), _full((N_DEV, D_MODEL))],
        semantics=("arbitrary",),
    )(c_all, w_shard, b_shard)


def _load_chip_blocks(chip_ref, gathered, local, dsts, sems, first_sem=0):
    for k, dst in enumerate(dsts):
        @pl.when(chip_ref[0] == k)
        def _():
            pltpu.make_async_copy(local, dst, sems.at[first_sem + k]).start()

        @pl.when(chip_ref[0] != k)
        def _():
            pltpu.make_async_copy(gathered.at[k], dst, sems.at[first_sem + k]).start()
    return [pltpu.make_async_copy(local, dst, sems.at[first_sem + k]).wait for k, dst in enumerate(dsts)]


def _in_proj_kernel(x, vecs, w_in_t, comm=None):
    seq = x.shape[0]
    tm = 512

    def body(x_ref, v_ref, w_ref, proj_ref, h_ref):
        xv = x_ref[...]
        rstd = lax.rsqrt(_mean_last(xv * xv) + EPS)
        n1 = (xv * rstd) * v_ref[0:1, :]
        h = n1 * (1.0 + v_ref[2:3, :]) + v_ref[1:2, :]
        hb = h.astype(MXU_DTYPE)
        h_ref[...] = hb
        proj_ref[...] = _dot_nt(hb, w_ref[...])

    return _hosted_call(
        body, comm, name="in_proj", grid=(seq // tm,),
        out_shape=[jax.ShapeDtypeStruct((seq, IN_PROJ_WIDTH), F32),
                   jax.ShapeDtypeStruct((seq, D_MODEL), MXU_DTYPE)],
        in_specs=[pl.BlockSpec((tm, D_MODEL), lambda i: (i, 0)), _full((8, D_MODEL)),
                  _full((IN_PROJ_WIDTH, D_MODEL))],
        out_specs=[pl.BlockSpec((tm, IN_PROJ_WIDTH), lambda i: (i, 0)),
                   pl.BlockSpec((tm, D_MODEL), lambda i: (i, 0))],
        semantics=("arbitrary",),
    )(x, vecs, w_in_t)


MIXER_BLOCKS_PER_STEP = 4
KV_START = 2 * GMLP_WIDTH + ATTN_WIDTH


def _mixer_fwd_kernel(proj, rope_tab, w_spatial, bias_full, sink_rows, comm=None):
    seq = proj.shape[0]
    per = MIXER_BLOCKS_PER_STEP
    steps = seq // (CHUNK * per)
    kv_col = KV_START // (2 * KV_WIDTH)

    def body(proj_ref, prev_ref, tab_ref, ptab_ref, w_ref, bias_ref, sink_ref, cat_ref):
        i = pl.program_id(0)
        wm, _, _ = _masked_spatial(w_ref)
        lo, hi = _lane_masks((CHUNK, LANES))
        lo2, _ = _lane_masks((2 * CHUNK, LANES))
        o = 2 * GMLP_WIDTH
        for s in range(per):
            rows, before = slice(CHUNK * s, CHUNK * (s + 1)), slice(CHUNK * (s - 1), CHUNK * s)
            for j in range(GMLP_GROUPS // 2):
                cols = slice(LANES * j, LANES * (j + 1))
                vcols = slice(GMLP_WIDTH + LANES * j, GMLP_WIDTH + LANES * (j + 1))
                u, _ = _gelu_tanh(proj_ref[rows, cols])
                vp, _ = _gelu_tanh(proj_ref[rows, vcols])
                sv = _sgu_forward_pair(wm, vp, j) + bias_ref[:, cols]
                cat_ref[rows, cols] = (u * sv).astype(cat_ref.dtype)
            tab = tab_ref[rows, :]
            if s == 0:
                prev_kv, prev_tab, first = prev_ref[...], ptab_ref[...], i == 0
            else:
                prev_kv, prev_tab, first = proj_ref[before, KV_START:KV_START + 2 * KV_WIDTH], tab_ref[before, :], None
            q_r = _rope_apply(proj_ref[rows, o:o + ATTN_WIDTH], tab, 1.0)
            k_cur = _rope_apply(proj_ref[rows, KV_START:KV_START + KV_WIDTH], tab, 1.0)
            k_prev = _rope_apply(prev_kv[:, 0:KV_WIDTH], prev_tab, 1.0)
            k_a = jnp.concatenate([k_prev, k_cur], axis=0)
            v_a = jnp.concatenate([prev_kv[:, KV_WIDTH:2 * KV_WIDTH],
                                   proj_ref[rows, KV_START + KV_WIDTH:KV_START + 2 * KV_WIDTH]], axis=0)
            k_b = pltpu.roll(k_a, HEAD_DIM, 1)
            v_b = pltpu.roll(v_a, HEAD_DIM, 1)
            bias_t = _attn_bias_t(first)
            for g in range(N_KV_HEADS):
                p_t, _ = _attn_probs_t(_group_dup(k_a, k_b, g, lo2), _group_rows(q_r, g, lo, hi), bias_t,
                                       _sink_row(sink_ref, g))
                o_t = _dot(_group_dup(v_a, v_b, g, lo2).T, p_t)
                for k, pair in enumerate(_pairs_from_rows(o_t.T, lo)):
                    c0 = GMLP_WIDTH + LANES * (2 * g + k)
                    cat_ref[rows, c0:c0 + LANES] = pair.astype(cat_ref.dtype)

    return _hosted_call(
        body, comm, name="mixer_fwd", grid=(steps,),
        out_shape=[jax.ShapeDtypeStruct((seq, D_MODEL), MXU_DTYPE)],
        in_specs=[pl.BlockSpec((CHUNK * per, IN_PROJ_WIDTH), lambda i: (i, 0)),
                  pl.BlockSpec((CHUNK, 2 * KV_WIDTH), lambda i: (jnp.maximum(per * i - 1, 0), kv_col)),
                  pl.BlockSpec((CHUNK * per, 3 * LANES), lambda i: (i, 0)),
                  pl.BlockSpec((CHUNK, 3 * LANES), lambda i: (jnp.maximum(per * i - 1, 0), 0)),
                  _full((GMLP_GROUPS, CHUNK, CHUNK)), _full((CHUNK, GMLP_WIDTH)),
                  _full((N_Q_HEADS, LANES))],
        out_specs=[pl.BlockSpec((CHUNK * per, D_MODEL), lambda i: (i, 0))],
        semantics=("arbitrary",),
    )(proj, proj, rope_tab, rope_tab, w_spatial, bias_full, sink_rows)


def _trunk_kernel(x, target, cat, vecs, chip_idx, gathered, local):
    seq = x.shape[0]
    tm = 256
    nj = D_FF // D_MODEL
    out_rows = D_MODEL // N_CHIPS

    def body(chip_ref, x_ref, t_ref, cat_ref, v_ref, g_out, g_w1, g_w2, l_out, l_w1, l_w2,
             dx1_ref, dcat_ref, dmix_ref, h2_ref, r_ref, da_ref, dff_ref, sums_ref,
             wout, w1, w2, a_scr, sem):
        i = pl.program_id(0)

        @pl.when(i == 0)
        def _():
            waits = _load_chip_blocks(chip_ref, g_out, l_out,
                                      [wout.at[pl.ds(out_rows * k, out_rows)] for k in range(N_CHIPS)], sem)
            waits += _load_chip_blocks(chip_ref, g_w1, l_w1, [w1.at[k] for k in range(N_CHIPS)], sem, N_CHIPS)
            waits += _load_chip_blocks(chip_ref, g_w2, l_w2, [w2.at[k] for k in range(N_CHIPS)], sem, 2 * N_CHIPS)
            for wait in waits:
                wait()
            sums_ref[...] = jnp.zeros_like(sums_ref)

        gate1, shift2, scale2 = v_ref[0:1, :], v_ref[1:2, :], v_ref[2:3, :]
        gate2, g_ffn, g_final = v_ref[3:4, :], v_ref[4:5, :], v_ref[5:6, :]

        mix = _dot(cat_ref[...], wout[...])
        x1 = x_ref[...] + gate1 * mix
        rstd2 = lax.rsqrt(_mean_last(x1 * x1) + EPS)
        xh2 = x1 * rstd2
        n2 = xh2 * g_ffn
        h2b = (n2 * (1.0 + scale2) + shift2).astype(MXU_DTYPE)
        h2_ref[...] = h2b
        ff = jnp.zeros((tm, D_MODEL), F32)
        for j in range(nj):
            a = _dot(h2b, w1[j])
            a_scr[j] = a
            relu = jnp.maximum(a, 0.0)
            rb = (relu * relu).astype(MXU_DTYPE)
            r_ref[:, D_MODEL * j:D_MODEL * (j + 1)] = rb
            ff = ff + _dot(rb, w2[j])
        x2 = x1 + gate2 * ff
        rstd3 = lax.rsqrt(_mean_last(x2 * x2) + EPS)
        xh3 = x2 * rstd3
        err = xh3 * g_final - t_ref[...]
        loss = 0.5 * _rowsum(_mean_last(err * err))
        dy = err * (1.0 / D_MODEL)
        dxh3 = dy * g_final
        dx2 = rstd3 * (dxh3 - xh3 * _mean_last(dxh3 * xh3))
        dffb = (dx2 * gate2).astype(MXU_DTYPE)
        dff_ref[...] = dffb
        dh2 = jnp.zeros((tm, D_MODEL), F32)
        for j in range(nj):
            dr = _dot_nt(dffb, w2[j])
            dab = (dr * (2.0 * jnp.maximum(a_scr[j], 0.0))).astype(MXU_DTYPE)
            da_ref[:, D_MODEL * j:D_MODEL * (j + 1)] = dab
            dh2 = dh2 + _dot_nt(dab, w1[j])
        dn2 = dh2 * (1.0 + scale2)
        dxh2 = dn2 * g_ffn
        dx1 = dx2 + rstd2 * (dxh2 - xh2 * _mean_last(dxh2 * xh2))
        dx1_ref[...] = dx1
        dmixb = (dx1 * gate1).astype(MXU_DTYPE)
        dmix_ref[...] = dmixb
        dcat_ref[...] = _dot_nt(dmixb, wout[...])

        sums_ref[0:1, :] += _rowsum(dh2)
        sums_ref[1:2, :] += _rowsum(dh2 * n2)
        sums_ref[2:3, :] += _rowsum(dx2 * ff)
        sums_ref[3:4, :] += _rowsum(dn2 * xh2)
        sums_ref[4:5, :] += _rowsum(dy * xh3)
        sums_ref[5:6, :] += _rowsum(dx1 * mix)
        sums_ref[6:7, :] += jnp.broadcast_to(loss, (1, D_MODEL))

    tok = lambda w: pl.BlockSpec((tm, w), lambda i, chip: (i, 0))
    return _hosted_call(
        body, None, name="trunk", grid=(seq // tm,), n_prefetch=1,
        out_shape=[jax.ShapeDtypeStruct((seq, D_MODEL), F32), jax.ShapeDtypeStruct((seq, D_MODEL), F32),
                   jax.ShapeDtypeStruct((seq, D_MODEL), MXU_DTYPE), jax.ShapeDtypeStruct((seq, D_MODEL), MXU_DTYPE),
                   jax.ShapeDtypeStruct((seq, D_FF), MXU_DTYPE), jax.ShapeDtypeStruct((seq, D_FF), MXU_DTYPE),
                   jax.ShapeDtypeStruct((seq, D_MODEL), MXU_DTYPE), jax.ShapeDtypeStruct((8, D_MODEL), F32)],
        in_specs=[tok(D_MODEL), tok(D_MODEL), tok(D_MODEL), _full((8, D_MODEL))] + [_any()] * 6,
        out_specs=[tok(D_MODEL), tok(D_MODEL), tok(D_MODEL), tok(D_MODEL), tok(D_FF), tok(D_FF), tok(D_MODEL),
                   _full((8, D_MODEL))],
        scratch_shapes=[pltpu.VMEM((D_MODEL, D_MODEL), MXU_DTYPE), pltpu.VMEM((nj, D_MODEL, D_MODEL), MXU_DTYPE),
                        pltpu.VMEM((nj, D_MODEL, D_MODEL), MXU_DTYPE), pltpu.VMEM((nj, tm, D_MODEL), F32),
                        pltpu.SemaphoreType.DMA((3 * N_CHIPS,))],
        semantics=("arbitrary",),
    )(chip_idx, x, target, cat, vecs, *gathered, *local)


def _mixer_bwd_kernel(proj, rope_tab, dcat, w_spatial, w_spatial_t, bias_full, sink_rows, dev_idx, comm=None):
    seq = proj.shape[0]
    per = MIXER_BLOCKS_PER_STEP
    steps = seq // (CHUNK * per)
    kv_col = KV_START // (2 * KV_WIDTH)

    def body(dev_ref, proj_ref, prev_ref, tab_ref, ptab_ref, dcat_ref, w_ref, wt_ref, bias_ref, sink_ref,
             dproj_ref, dw_out, db_ref, dsink_ref, carry, dw_ref):
        del dev_ref
        step = pl.program_id(0)

        @pl.when(step == 0)
        def _():
            carry[...] = jnp.zeros_like(carry)
            dw_ref[...] = jnp.zeros_like(dw_ref)
            db_ref[...] = jnp.zeros_like(db_ref)
            dsink_ref[...] = jnp.zeros_like(dsink_ref)

        for s in reversed(range(per)):
            rows = pl.ds(CHUNK * s, CHUNK)
            if s == 0:
                before, before_tab, first = prev_ref, ptab_ref, step == steps - 1
            else:
                before = proj_ref.at[pl.ds(CHUNK * (s - 1), CHUNK), pl.ds(KV_START, 2 * KV_WIDTH)]
                before_tab, first = tab_ref.at[pl.ds(CHUNK * (s - 1), CHUNK)], None
            one_block(proj_ref.at[rows], before, tab_ref.at[rows], before_tab, dcat_ref.at[rows], w_ref, wt_ref,
                      bias_ref, sink_ref, dproj_ref.at[rows], dw_ref, db_ref, dsink_ref, carry, first)

        @pl.when(step == steps - 1)
        def _():
            dw_out[...] = dw_ref[...].astype(dw_out.dtype)

    def one_block(proj_ref, prev_ref, tab_ref, ptab_ref, dcat_ref, w_ref, wt_ref, bias_ref, sink_ref,
                  dproj_ref, dw_ref, db_ref, dsink_ref, carry, first):
        wm, tril, triu = _masked_spatial(w_ref)
        lo, hi = _lane_masks((CHUNK, LANES))
        lane = lax.broadcasted_iota(jnp.int32, (CHUNK, LANES), 1)
        db = jnp.zeros((CHUNK, LANES), F32)
        for j in range(GMLP_GROUPS // 2):
            cols = slice(LANES * j, LANES * (j + 1))
            vcols = slice(GMLP_WIDTH + LANES * j, GMLP_WIDTH + LANES * (j + 1))
            zu, zv = proj_ref[:, cols], proj_ref[:, vcols]
            u, tu = _gelu_tanh(zu)
            vp, tv = _gelu_tanh(zv)
            sv = _sgu_forward_pair(wm, vp, j) + bias_ref[:, cols]
            dout = dcat_ref[:, cols]
            du = dout * sv
            dsv = dout * u
            dsv_lo, dsv_hi = jnp.where(lo, dsv, 0.0), jnp.where(hi, dsv, 0.0)
            lhs_t = jnp.concatenate([jnp.where(triu, wt_ref[2 * j], 0.0),
                                     jnp.where(triu, wt_ref[2 * j + 1], 0.0)], axis=1)
            dv = _dot(lhs_t, jnp.concatenate([dsv_lo, dsv_hi], axis=0))
            dw_ref[2 * j] += jnp.where(tril, _dot_nt(dsv_lo, vp), 0.0)
            dw_ref[2 * j + 1] += jnp.where(tril, _dot_nt(dsv_hi, vp), 0.0)
            db = db + (jnp.where(lane == 2 * j, jnp.sum(dsv_lo, axis=1, keepdims=True), 0.0)
                       + jnp.where(lane == 2 * j + 1, jnp.sum(dsv_hi, axis=1, keepdims=True), 0.0))
            dproj_ref[:, cols] = (du * _gelu_tanh_grad(zu, tu)).astype(dproj_ref.dtype)
            dproj_ref[:, vcols] = (dv * _gelu_tanh_grad(zv, tv)).astype(dproj_ref.dtype)
        db_ref[...] += db
        o = 2 * GMLP_WIDTH
        tab = tab_ref[...]
        q_r = _rope_apply(proj_ref[:, o:o + ATTN_WIDTH], tab, 1.0)
        k_cur = _rope_apply(proj_ref[:, o + ATTN_WIDTH:o + ATTN_WIDTH + KV_WIDTH], tab, 1.0)
        k_prev = _rope_apply(prev_ref[:, 0:KV_WIDTH], ptab_ref[...], 1.0)
        k_a = jnp.concatenate([k_prev, k_cur], axis=0)
        v_a = jnp.concatenate([prev_ref[:, KV_WIDTH:2 * KV_WIDTH],
                               proj_ref[:, o + ATTN_WIDTH + KV_WIDTH:o + ATTN_WIDTH + 2 * KV_WIDTH]], axis=0)
        k_b = pltpu.roll(k_a, HEAD_DIM, 1)
        v_b = pltpu.roll(v_a, HEAD_DIM, 1)
        bias_t = _attn_bias_t(first)
        lo2, _ = _lane_masks((2 * CHUNK, LANES))
        dout_b = dcat_ref[:, GMLP_WIDTH:GMLP_WIDTH + ATTN_WIDTH]
        dk_tot, dv_tot, dq_pairs = [], [], []
        for g in range(N_KV_HEADS):
            k_dup, v_dup = _group_dup(k_a, k_b, g, lo2), _group_dup(v_a, v_b, g, lo2)
            q_rows = _group_rows(q_r, g, lo, hi)
            do_rows = _group_rows(dout_b, g, lo, hi)
            p_t, p_sink = _attn_probs_t(k_dup, q_rows, bias_t, _sink_row(sink_ref, g))
            dp_t = _dot_nt(v_dup, do_rows)
            delta = jnp.sum(p_t * dp_t, axis=0, keepdims=True)
            ds_t = p_t * (dp_t - delta) * ATTN_SCALE
            dsink = -p_sink * delta
            for r in range(HEADS_PER_GROUP):
                h = HEADS_PER_GROUP * g + r
                dsink_ref[h:h + 1, :] += jnp.broadcast_to(
                    jnp.sum(dsink[:, LANES * r:LANES * (r + 1)], axis=1, keepdims=True), (1, LANES))
            dk_full = _dot(ds_t, q_rows)
            dv_full = _dot(p_t, do_rows)
            dk_tot.append(dk_full + pltpu.roll(dk_full, HEAD_DIM, 1))
            dv_tot.append(dv_full + pltpu.roll(dv_full, HEAD_DIM, 1))
            dq_t = _dot(k_dup.T, ds_t)
            dq_pairs += _pairs_from_rows(dq_t.T, lo)
        dk_all = jnp.where(lo2, dk_tot[0], dk_tot[1])
        dv_all = jnp.where(lo2, dv_tot[0], dv_tot[1])
        dk_cur = dk_all[CHUNK:, :] + carry[:, 0:KV_WIDTH]
        dv_cur = dv_all[CHUNK:, :] + carry[:, KV_WIDTH:2 * KV_WIDTH]
        carry[:, 0:KV_WIDTH] = dk_all[:CHUNK, :]
        carry[:, KV_WIDTH:2 * KV_WIDTH] = dv_all[:CHUNK, :]
        dq = _rope_apply(jnp.concatenate(dq_pairs, axis=1), tab, -1.0)
        dproj_ref[:, o:o + ATTN_WIDTH] = dq.astype(dproj_ref.dtype)
        dproj_ref[:, o + ATTN_WIDTH:o + ATTN_WIDTH + KV_WIDTH] = (
            _rope_apply(dk_cur, tab, -1.0).astype(dproj_ref.dtype))
        dproj_ref[:, o + ATTN_WIDTH + KV_WIDTH:o + ATTN_WIDTH + 2 * KV_WIDTH] = dv_cur.astype(dproj_ref.dtype)

    rev = lambda i: steps - 1 - i
    before = lambda i: jnp.maximum(per * rev(i) - 1, 0)
    slot = lambda shape: pl.BlockSpec((None,) + shape, lambda i, d: (d[0],) + (0,) * len(shape))
    return _hosted_call(
        body, comm, name="mixer_bwd", grid=(steps,), n_prefetch=1,
        out_shape=[jax.ShapeDtypeStruct((seq, IN_PROJ_WIDTH), MXU_DTYPE),
                   jax.ShapeDtypeStruct((N_DEV, GMLP_GROUPS, CHUNK, CHUNK), GRAD_COMM_DTYPE),
                   jax.ShapeDtypeStruct((N_DEV, CHUNK, LANES), F32),
                   jax.ShapeDtypeStruct((N_DEV, N_Q_HEADS, LANES), F32)],
        in_specs=[pl.BlockSpec((CHUNK * per, IN_PROJ_WIDTH), lambda i, d: (rev(i), 0)),
                  pl.BlockSpec((CHUNK, 2 * KV_WIDTH), lambda i, d: (before(i), kv_col)),
                  pl.BlockSpec((CHUNK * per, 3 * LANES), lambda i, d: (rev(i), 0)),
                  pl.BlockSpec((CHUNK, 3 * LANES), lambda i, d: (before(i), 0)),
                  pl.BlockSpec((CHUNK * per, D_MODEL), lambda i, d: (rev(i), 0)),
                  _full((GMLP_GROUPS, CHUNK, CHUNK)), _full((GMLP_GROUPS, CHUNK, CHUNK)),
                  _full((CHUNK, GMLP_WIDTH)), _full((N_Q_HEADS, LANES))],
        out_specs=[pl.BlockSpec((CHUNK * per, IN_PROJ_WIDTH), lambda i, d: (rev(i), 0)),
                   slot((GMLP_GROUPS, CHUNK, CHUNK)), slot((CHUNK, LANES)), slot((N_Q_HEADS, LANES))],
        scratch_shapes=[pltpu.VMEM((CHUNK, 2 * KV_WIDTH), F32), pltpu.VMEM((GMLP_GROUPS, CHUNK, CHUNK), F32)],
        semantics=("arbitrary",),
    )(dev_idx, proj, proj, rope_tab, rope_tab, dcat, w_spatial, w_spatial_t, bias_full, sink_rows)


def _in_proj_bwd_kernel(x, dx1, dproj, vecs, w_in_t, comm=None):
    seq = x.shape[0]
    tm = 512

    def body(x_ref, dx1_ref, dp_ref, v_ref, w_ref, gx_ref, sums_ref):
        @pl.when(pl.program_id(0) == 0)
        def _():
            sums_ref[...] = jnp.zeros_like(sums_ref)

        g_mix, scale1 = v_ref[0:1, :], v_ref[2:3, :]
        dh = _dot(dp_ref[...], w_ref[...])
        xv = x_ref[...]
        rstd = lax.rsqrt(_mean_last(xv * xv) + EPS)
        xh = xv * rstd
        dn1 = dh * (1.0 + scale1)
        dxh = dn1 * g_mix
        gx_ref[...] = dx1_ref[...] + rstd * (dxh - xh * _mean_last(dxh * xh))
        sums_ref[0:1, :] += _rowsum(dh)
        sums_ref[1:2, :] += _rowsum(dh * (xh * g_mix))
        sums_ref[2:3, :] += _rowsum(dn1 * xh)

    tok = lambda w: pl.BlockSpec((tm, w), lambda i: (i, 0))
    return _hosted_call(
        body, comm, name="in_proj_bwd", grid=(seq // tm,),
        out_shape=[jax.ShapeDtypeStruct((seq, D_MODEL), F32), jax.ShapeDtypeStruct((8, D_MODEL), F32)],
        in_specs=[tok(D_MODEL), tok(D_MODEL), tok(IN_PROJ_WIDTH), _full((8, D_MODEL)),
                  _full((IN_PROJ_WIDTH, D_MODEL))],
        out_specs=[tok(D_MODEL), _full((8, D_MODEL))],
        semantics=("arbitrary",),
    )(x, dx1, dproj, vecs, w_in_t)


class _GradTiles(NamedTuple):
    tm: int
    tn: int
    n_tiles: int
    chips_per_tile: int
    a_index: Callable
    b_index: Callable


def _weight_grad_kernel(a, b, c_idx, name, tiles, comm=None):
    seq = a.shape[0]
    tk = min(seq, 4096)
    nk = seq // tk
    tm, tn, n_tiles, per = tiles.tm, tiles.tn, tiles.n_tiles, tiles.chips_per_tile
    rows = tm // per

    def half(phase, c):
        return phase * c[0] + (1 - phase) * (1 - c[0])

    def body(c_ref, a_ref, b_ref, o_ref, acc, stage, landed, send_sems, recv_sems):
        del c_ref
        phase, t, kk = pl.program_id(0), pl.program_id(1), pl.program_id(2)
        x, y, c, _ = _mesh_place()

        def copy(tile):
            return pltpu.make_async_remote_copy(
                src_ref=stage.at[tile], dst_ref=landed.at[tile], send_sem=send_sems.at[tile],
                recv_sem=recv_sems.at[tile], device_id=(x, y, 1 - c), device_id_type=MESH)

        @pl.when(kk == 0)
        def _():
            acc[...] = jnp.zeros_like(acc)

        acc[...] += _dot_tn(a_ref[...], b_ref[...])

        @pl.when((kk == nk - 1) & (phase == 0))
        def _():
            stage[t] = acc[...].astype(stage.dtype)
            copy(t).start()

        @pl.when((kk == nk - 1) & (phase == 1))
        def _():
            copy(t).wait_recv()
            total = acc[...] + landed[t].astype(F32)
            for q in range(per):
                o_ref[q] = total[rows * q:rows * (q + 1)].astype(o_ref.dtype)

        @pl.when((kk == nk - 1) & (phase == 1) & (t == n_tiles - 1))
        def _():
            for tile in range(n_tiles):
                copy(tile).wait_send()

    out = _hosted_call(
        body, comm, name=name, grid=(2, n_tiles, nk), n_prefetch=1,
        out_shape=[jax.ShapeDtypeStruct((n_tiles * per, rows, tn), GRAD_COMM_DTYPE)],
        in_specs=[pl.BlockSpec((tk, tm), lambda p, t, k, c: (k, tiles.a_index(t, half(p, c)))),
                  pl.BlockSpec((tk, tn), lambda p, t, k, c: (k, tiles.b_index(t, half(p, c))))],
        out_specs=[pl.BlockSpec((per, rows, tn), lambda p, t, k, c: (p * t, 0, 0))],
        scratch_shapes=[pltpu.VMEM((tm, tn), F32), pltpu.VMEM((n_tiles, tm, tn), GRAD_COMM_DTYPE),
                        pltpu.VMEM((n_tiles, tm, tn), GRAD_COMM_DTYPE),
                        pltpu.SemaphoreType.DMA((n_tiles,)), pltpu.SemaphoreType.DMA((n_tiles,))],
        semantics=("arbitrary", "arbitrary", "arbitrary"),
    )(c_idx, a, b)
    return out[0] if comm is None else out


def _row_tile(rows, most=256, sublanes=16):
    return max(t for t in range(sublanes, most + 1, sublanes) if rows % t == 0)


def _adam_update(w, g, m, v):
    m_new = ADAM_B1 * m + (1.0 - ADAM_B1) * g
    v_new = ADAM_B2 * v + (1.0 - ADAM_B2) * (g * g)
    m_hat = m_new / (1.0 - ADAM_B1 ** ADAM_STEP)
    v_hat = v_new / (1.0 - ADAM_B2 ** ADAM_STEP)
    delta = -ADAM_LR * (m_hat / (jnp.sqrt(v_hat) + ADAM_EPS) + ADAM_WD * w)
    return delta, m_new, v_new


def _sum_chips_kernel(own, others, place, name):
    _, r, n = own.shape
    tr = _row_tile(r)

    def body(place_ref, own_ref, oth_ref, o_ref):
        del place_ref
        acc = own_ref[...].astype(F32)
        for k in range(N_CHIPS - 1):
            acc = acc + oth_ref[k].astype(F32)
        o_ref[...] = acc

    return pl.pallas_call(
        body, name=name, out_shape=jax.ShapeDtypeStruct((2, r, n), F32),
        grid_spec=pltpu.PrefetchScalarGridSpec(
            num_scalar_prefetch=1, grid=(r // tr,),
            in_specs=[pl.BlockSpec((None, tr, n), lambda i, p: (p[0], i, 0)),
                      pl.BlockSpec((N_CHIPS - 1, tr, n), lambda i, p: (0, i, 0))],
            out_specs=pl.BlockSpec((None, tr, n), lambda i, p: (p[1], i, 0))),
        compiler_params=_params("parallel"),
    )(place, own, others)


def _adam_kernel(w, g, m, v, name, after=()):
    r, n = w.shape
    by_columns = g.shape[1] == r
    tr, tn = _row_tile(g.shape[1], most=512), g.shape[2]

    def body(w_ref, g_ref, m_ref, v_ref, *rest):
        g_out, d_ref, mo_ref, vo_ref = rest[len(after):]
        gv = g_ref[...]
        g_out[...] = gv
        d_ref[...], mo_ref[...], vo_ref[...] = _adam_update(w_ref[...], gv, m_ref[...], v_ref[...])

    steps = g.shape[1] // tr
    spec = pl.BlockSpec((tr, tn), (lambda h, i: (i, h)) if by_columns else (lambda h, i: (h * steps + i, 0)))
    return pl.pallas_call(
        body, name=name, grid=(2, steps), out_shape=[jax.ShapeDtypeStruct((r, n), F32)] * 4,
        in_specs=[spec, pl.BlockSpec((None, tr, tn), lambda h, i: (h, i, 0)), spec, spec] + [_any()] * len(after),
        out_specs=[spec] * 4, compiler_params=_params("parallel", "parallel"),
    )(w, g, m, v, *after)


SMALL_PARAMS = ("b_ada", "g_mix", "g_ffn", "g_final", "b_spatial", "sinks", "w_spatial")


def _small_update_kernel(gathered, params):
    shapes = [params[nm][0].shape for nm in SMALL_PARAMS]

    def body(*refs):
        g_refs, refs = refs[:5], refs[5:]
        p_refs, refs = refs[:3 * len(SMALL_PARAMS)], refs[3 * len(SMALL_PARAMS):]
        loss_ref, o_refs = refs[0], refs[1:]

        def total(ref):
            acc = ref[0].astype(F32)
            for k in range(1, N_DEV):
                acc = acc + ref[k].astype(F32)
            return acc

        s1, s2, db, ds, dw = (total(r) for r in g_refs)
        loss_ref[...] = jnp.broadcast_to(s2[6:7, 0:1], loss_ref.shape)
        grads = {"b_ada": [s1[0:1], s1[1:2], s2[5:6], s2[0:1], s2[1:2], s2[2:3]], "g_mix": [s1[2:3]],
                 "g_ffn": [s2[3:4]], "g_final": [s2[4:5]], "b_spatial": [db.T[0:GMLP_GROUPS]],
                 "w_spatial": [dw]}
        lane = lax.broadcasted_iota(jnp.int32, (1, LANES), 1)
        sink_row = jnp.zeros((1, LANES), F32)
        for h in range(N_Q_HEADS):
            sink_row = sink_row + jnp.where(lane == h, ds[h:h + 1, :], 0.0)
        grads["sinks"] = [sink_row[:, 0:N_Q_HEADS]]
        for i, nm in enumerate(SMALL_PARAMS):
            w_ref, m_ref, v_ref = p_refs[3 * i:3 * i + 3]
            outs = o_refs[4 * i:4 * i + 4]
            width = grads[nm][0].shape[1]
            for k, g in enumerate(grads[nm]):
                cols = slice(width * k, width * (k + 1))
                upd = _adam_update(w_ref[:, cols], g, m_ref[:, cols], v_ref[:, cols])
                for o_ref, val in zip(outs, (g,) + upd):
                    o_ref[:, cols] = val

    flat = [a for nm in SMALL_PARAMS for a in params[nm]]
    out_shape = [jax.ShapeDtypeStruct((8, LANES), F32)]
    out_shape += [jax.ShapeDtypeStruct(s, F32) for s in shapes for _ in range(4)]
    outs = pl.pallas_call(
        body, name="small_update", grid=(1,), out_shape=out_shape,
        in_specs=[_full(g.shape) for g in gathered] + [_full(a.shape) for a in flat],
        out_specs=[_full(s.shape) for s in out_shape],
        compiler_params=_params("arbitrary"),
    )(*gathered, *flat)
    return {nm: outs[1 + 4 * i:5 + 4 * i] for i, nm in enumerate(SMALL_PARAMS)}, outs[0]


def _ada_update_kernel(act_t, dmod, w, m, v):
    r, n = w.shape
    tr = 256

    def body(a_ref, d_ref, w_ref, m_ref, v_ref, g_ref, dl_ref, mo_ref, vo_ref):
        g = _dot(a_ref[...], d_ref[...])
        g_ref[...] = g
        dl_ref[...], mo_ref[...], vo_ref[...] = _adam_update(w_ref[...], g, m_ref[...], v_ref[...])

    spec = pl.BlockSpec((tr, n), lambda i: (i, 0))
    return pl.pallas_call(
        body, name="ada_update", grid=(r // tr,), out_shape=[jax.ShapeDtypeStruct((r, n), F32)] * 4,
        in_specs=[pl.BlockSpec((tr, N_DEV), lambda i: (i, 0)), _full((N_DEV, n)), spec, spec, spec],
        out_specs=[spec] * 4, compiler_params=_params("parallel"),
    )(act_t, dmod, w, m, v)


def kernel(x, c, positions, w_ada, b_ada, g_mix, w_in, w_spatial, b_spatial, sinks, w_out, g_ffn, w_ff1, w_ff2, g_final, loss_target, m_w_ada, m_b_ada, m_g_mix, m_w_in, m_w_spatial, m_b_spatial, m_sinks, m_w_out, m_g_ffn, m_w_ff1, m_w_ff2, m_g_final, v_w_ada, v_b_ada, v_g_mix, v_w_in, v_w_spatial, v_b_spatial, v_sinks, v_w_out, v_g_ffn, v_w_ff1, v_w_ff2, v_g_final):
    xi, yi, ci = lax.axis_index("x"), lax.axis_index("y"), lax.axis_index("c")
    chip = 2 * xi + yi
    dev = 2 * chip + ci
    seq = x.shape[1]
    x2, tgt = x[0], loss_target[0]
    ada_cols = w_ada.shape[2]

    big = {"w_in": tuple(a[0].T for a in (w_in, m_w_in, v_w_in)),
           "w_out": (w_out[0], m_w_out[0], v_w_out[0]), "w_ff1": (w_ff1[0], m_w_ff1[0], v_w_ff1[0]),
           "w_ff2": (w_ff2[0], m_w_ff2[0], v_w_ff2[0])}

    def halves(nm, zero=None):
        r, n = big[nm][0].shape
        w = big[nm][0] if zero is None else big[nm][0] + zero
        return w.astype(WEIGHT_COMM_DTYPE).reshape(2, r // 2, n)

    chip_idx = chip.reshape(1).astype(jnp.int32)
    first_c, first_w = ([c], [False], [True]), ([halves("w_in"), halves("w_out")], [True, True], [True, False])
    c_started, w_started = _first_stage_start([first_c, first_w], "gather_first_start")
    zero = c_started[-1][0, 0]
    trunk_weights = ["w_out", "w_ff1", "w_ff2"]
    shards = [halves("w_out"), halves("w_ff1", zero), halves("w_ff2", zero)]
    inv_freq = ROPE_THETA ** (-jnp.arange(0, ROT_DIM, 2, dtype=F32) / ROT_DIM) + zero
    rope_tab = _rope_lane_tables(*_rope_angle_kernel(positions, inv_freq.reshape(ROT_DIM // 2, 1)))
    c_all, = _first_stage_finish(c_started, *first_c[1:], [rope_tab], "gather_c_wait")
    b_shard = lax.dynamic_slice(b_ada, (0, chip * ada_cols), (1, ada_cols))
    mod_part, act = _mod_kernel(c_all.reshape(N_DEV, D_MODEL), w_ada[0], b_shard)
    w_in_t, g_out, *ff_halves = _first_stage_finish(w_started, *first_w[1:], [mod_part], "gather_first_wait",
                                                    thru=shards[1:])
    shards = shards[:1] + ff_halves
    mod_all, w_in_t, g_out = _all_gather8([mod_part], "gather_mod", forward=[w_in_t, g_out])
    w_in_t = w_in_t.reshape(IN_PROJ_WIDTH, D_MODEL)
    mod_me = lax.dynamic_index_in_dim(mod_all[0::2], dev, axis=1, keepdims=False)
    mod_me = mod_me.reshape(N_MOD, D_MODEL)
    shift1, scale1, gate1, shift2, scale2, gate2 = (mod_me[k:k + 1] for k in range(N_MOD))

    zeros_row = jnp.zeros((1, D_MODEL), F32)
    vecs1 = jnp.concatenate([g_mix, shift1, scale1] + [zeros_row] * 5, axis=0)
    vecs2 = jnp.concatenate([gate1, shift2, scale2, gate2, g_ffn, g_final.reshape(1, D_MODEL)]
                            + [zeros_row] * 2, axis=0)
    bias_full = jnp.repeat(b_spatial[0].T, HEAD_DIM, axis=1)
    sink_rows = jnp.broadcast_to(sinks[0][:, None], (N_Q_HEADS, LANES))

    proj, hb, *staged = _in_proj_kernel(x2, vecs1, w_in_t, comm=_gather2d_first(shards[1:]))
    cat, *staged = _mixer_fwd_kernel(proj, rope_tab, w_spatial[0], bias_full, sink_rows,
                                     comm=_gather2d_second(staged, shards[1:]))
    staged = [g_out] + list(_gather_forward(staged, "gather_forward"))
    dx1, dcat, dmix, h2b, rb, dab, dffb, sums2 = _trunk_kernel(
        x2, tgt, cat, vecs2, chip_idx,
        [g.reshape((N_CHIPS,) + big[nm][0].shape) for nm, g in zip(trunk_weights, staged)],
        [s.reshape(big[nm][0].shape) for nm, s in zip(trunk_weights, shards)])

    c_idx = ci.reshape(1).astype(jnp.int32)
    place = jnp.stack([chip, ci]).astype(jnp.int32)
    half_d = D_MODEL // 2
    cs_ff2 = _weight_grad_kernel(rb, dffb, c_idx, "dw_ff2",
                                 _GradTiles(D_MODEL, half_d, N_CHIPS, 1, lambda t, h: t, lambda t, h: h))
    eighth = D_MODEL // 8
    cs_ff1, sc_ff2 = _weight_grad_kernel(
        h2b, dab, c_idx, "dw_ff1",
        _GradTiles(D_MODEL, half_d, N_CHIPS, 1, lambda t, h: 0, lambda t, h: 2 * t + h),
        comm=_scatter_job([cs_ff2], rows=(0, 6 * eighth)))
    cs_out, sc_ff2 = _weight_grad_kernel(
        cat, dmix, c_idx, "dw_out", _GradTiles(D_MODEL, half_d, 1, N_CHIPS, lambda t, h: 0, lambda t, h: h),
        comm=_scatter_job([cs_ff2], rows=(6 * eighth, eighth), into=[sc_ff2]))
    dproj, dw_spatial, db_lanes, dsink_rows, sc_ff2, sc_ff1, sc_out = _mixer_bwd_kernel(
        proj, rope_tab, dcat, w_spatial[0], w_spatial[0].transpose(0, 2, 1), bias_full, sink_rows,
        dev.reshape(1).astype(jnp.int32),
        comm=_merge_jobs(_scatter_job([cs_ff1, cs_out]),
                         _scatter_job([cs_ff2], rows=(7 * eighth, eighth), into=[sc_ff2])))
    totals = [_sum_chips_kernel(own, oth, place, "grad_sum_" + nm)
              for nm, own, oth in (("w_out", cs_out, sc_out), ("w_ff1", cs_ff1, sc_ff1), ("w_ff2", cs_ff2, sc_ff2))]
    small_slots = [db_lanes, dsink_rows, dw_spatial.reshape(N_DEV, GMLP_GROUPS * CHUNK, CHUNK)]
    cs_in, *rode = _weight_grad_kernel(
        dproj, hb, c_idx, "dw_in",
        _GradTiles(2 * W_IN_BLOCK, half_d, N_CHIPS // 2, 2, lambda t, h: t, lambda t, h: h),
        comm=_merge_jobs(_gather_job(small_slots), _share_job(totals)))
    small_stage1, shared = rode[:len(small_slots)], rode[len(small_slots):]
    scatter_in = _scatter_start(cs_in, "grad_to_chips_w_in_start")
    grad_x, sums1 = _in_proj_bwd_kernel(x2, dx1, dproj, vecs1 + scatter_in[-1][0:1, 0:1], w_in_t)
    cs_in, sc_in = _scatter_wait(scatter_in, sums1, "grad_to_chips_w_in_wait")
    total_in = _sum_chips_kernel(cs_in, sc_in, place, "grad_sum_w_in")
    gather_small = _gather_start([sums1, sums2], small_stage1, [total_in], cs_in, "gather_small_start")
    big_out = {}

    def update(nm, g, after=()):
        w, m, v = big[nm]
        outs = _adam_kernel(w, g, m, v, "adam_" + nm, after=after)
        big_out[nm] = tuple((t.T if nm == "w_in" else t)[None] for t in outs)
        return outs

    update("w_out", shared[0])
    covering = update("w_ff1", shared[1], after=gather_small[-1:])
    update("w_ff2", shared[2])
    *gathered, shared_in = _gather_wait(gather_small, 2, 1, covering[0], "gather_small_wait")
    update("w_in", shared_in)

    small = {"b_ada": (b_ada, m_b_ada, v_b_ada), "g_mix": (g_mix, m_g_mix, v_g_mix),
             "g_ffn": (g_ffn, m_g_ffn, v_g_ffn), "g_final": (g_final, m_g_final, v_g_final),
             "b_spatial": (b_spatial, m_b_spatial, v_b_spatial), "sinks": (sinks, m_sinks, v_sinks),
             "w_spatial": (w_spatial, m_w_spatial, v_w_spatial)}
    flat_shape = {"g_final": (1, D_MODEL), "b_spatial": (GMLP_GROUPS, CHUNK), "w_spatial": (GMLP_GROUPS * CHUNK, CHUNK)}
    small_out, loss_tile = _small_update_kernel(
        gathered, {nm: tuple(a.reshape(flat_shape.get(nm, a.shape)) for a in small[nm]) for nm in small})
    small_out = {nm: [o.reshape(small[nm][0].shape) for o in small_out[nm]] for nm in small}
    loss = loss_tile[0, 0]

    g1, g2 = gathered[0], gathered[1]
    dmod_all = jnp.concatenate([g1[:, 0], g1[:, 1], g2[:, 5], g2[:, 0], g2[:, 1], g2[:, 2]], axis=1)
    dmod_cols = lax.dynamic_slice(dmod_all, (0, chip * ada_cols), (N_DEV, ada_cols))
    ada = _ada_update_kernel(act.T, dmod_cols, w_ada[0], m_w_ada[0], v_w_ada[0])
    big_out["w_ada"] = tuple(t[None] for t in ada)

    order = ["w_ada", "b_ada", "g_mix", "w_in", "w_spatial", "b_spatial", "sinks", "w_out", "g_ffn",
             "w_ff1", "w_ff2", "g_final"]

    def leaf(nm, k):
        return big_out[nm][k] if nm in big_out else small_out[nm][k]

    outs = [loss, grad_x[None]]
    for k in range(4):
        outs += [leaf(nm, k) for nm in order]
    return tuple(outs)
```

```python
import math
from typing import Callable, NamedTuple

import jax
import jax.numpy as jnp
from jax import lax
from jax.experimental import pallas as pl
from jax.experimental.pallas import tpu as pltpu

F32 = jnp.float32
MXU_DTYPE = jnp.bfloat16
WEIGHT_COMM_DTYPE = jnp.bfloat16
GRAD_COMM_DTYPE = jnp.bfloat16

D_MODEL = 1024
D_FF = 4096
HEAD_DIM = 64
GMLP_GROUPS = 8
GMLP_WIDTH = 512
CHUNK = 128
N_Q_HEADS = 8
N_KV_HEADS = 2
ATTN_WIDTH = 512
KV_WIDTH = 128
ROT_DIM = 16
ROPE_THETA = 500000.0
IN_PROJ_WIDTH = 1792
N_MOD = 6
EPS = 1e-5
N_CHIPS = 4
N_DEV = 8
LANES = 128
W_IN_BLOCK = IN_PROJ_WIDTH // N_CHIPS

ADAM_LR = 0.001
ADAM_B1 = 0.9
ADAM_B2 = 0.999
ADAM_EPS = 1e-08
ADAM_WD = 0.01
ADAM_STEP = 10

VMEM_LIMIT_BYTES = 58 * 1024 * 1024
MESH = pl.DeviceIdType.MESH


def _params(*semantics):
    return pltpu.CompilerParams(dimension_semantics=semantics, vmem_limit_bytes=VMEM_LIMIT_BYTES)


def _dot(a, b):
    return jnp.dot(a.astype(MXU_DTYPE), b.astype(MXU_DTYPE), preferred_element_type=F32)


def _dot_nt(a, b):
    return lax.dot_general(a.astype(MXU_DTYPE), b.astype(MXU_DTYPE), (((1,), (1,)), ((), ())),
                           preferred_element_type=F32)


def _dot_tn(a, b):
    return lax.dot_general(a.astype(MXU_DTYPE), b.astype(MXU_DTYPE), (((0,), (0,)), ((), ())),
                           preferred_element_type=F32)


def _full(shape):
    return pl.BlockSpec(shape, lambda *_: (0,) * len(shape))


def _any():
    return pl.BlockSpec(memory_space=pl.ANY)


def _rowsum(v):
    return jnp.sum(v, axis=0, keepdims=True)


def _mean_last(v):
    return jnp.mean(v, axis=-1, keepdims=True)


class _Comm(NamedTuple):
    operands: tuple
    out_shapes: tuple
    n_sems: int
    make: Callable
    in_place: int = 0


def _hosted_call(body, comm, *, name, grid, in_specs, out_shape, out_specs, scratch_shapes=(), semantics,
                 n_prefetch=0):
    if comm is None:
        return pl.pallas_call(
            body, name=name, out_shape=out_shape, compiler_params=_params(*semantics),
            grid_spec=pltpu.PrefetchScalarGridSpec(
                num_scalar_prefetch=n_prefetch, grid=grid, in_specs=in_specs, out_specs=out_specs,
                scratch_shapes=list(scratch_shapes)))
    n_in, n_out, n_scr = len(in_specs), len(out_shape), len(scratch_shapes)
    k_in, k_out = len(comm.operands), len(comm.out_shapes)

    def hosted(*refs):
        prefetched, refs = refs[:n_prefetch], refs[n_prefetch:]
        ins, refs = refs[:n_in], refs[n_in:]
        c_ins, refs = refs[:k_in], refs[k_in:]
        outs, refs = refs[:n_out], refs[n_out:]
        c_outs, refs = refs[:k_out], refs[k_out:]
        scratch, (send_sems, recv_sems) = refs[:n_scr], refs[n_scr:]
        first, last = None, None
        for d, size in enumerate(grid):
            at_start, at_end = pl.program_id(d) == 0, pl.program_id(d) == size - 1
            first = at_start if first is None else first & at_start
            last = at_end if last is None else last & at_end

        @pl.when(first)
        def _():
            for cp in comm.make(c_ins, c_outs, send_sems, recv_sems)[0]:
                cp.start()

        body(*prefetched, *ins, *outs, *scratch)

        @pl.when(last)
        def _():
            for wait in comm.make(c_ins, c_outs, send_sems, recv_sems)[1]:
                wait()

    aliases = {n_prefetch + n_in + i: n_out + i for i in range(comm.in_place)}
    call = pl.pallas_call(
        hosted, name=name, out_shape=list(out_shape) + list(comm.out_shapes),
        compiler_params=_params(*semantics), input_output_aliases=aliases,
        grid_spec=pltpu.PrefetchScalarGridSpec(
            num_scalar_prefetch=n_prefetch, grid=grid, in_specs=list(in_specs) + [_any()] * k_in,
            out_specs=list(out_specs) + [_any()] * k_out,
            scratch_shapes=list(scratch_shapes) + [pltpu.SemaphoreType.DMA((comm.n_sems,)),
                                                    pltpu.SemaphoreType.DMA((comm.n_sems,))]))
    return lambda *args: call(*args, *comm.operands)


class _Shifted:
    def __init__(self, base, offset):
        self.base, self.offset = base, offset

    @property
    def at(self):
        return self

    def __getitem__(self, k):
        return self.base.at[self.offset + k]


def _merge_jobs(*jobs):
    def order(count):
        first = [(j, i) for j, job in enumerate(jobs) for i in range(job.in_place)]
        return first + [(j, i) for j, job in enumerate(jobs) for i in range(job.in_place, count(job))]

    op_order, out_order = order(lambda job: len(job.operands)), order(lambda job: len(job.out_shapes))

    def make(ins, outs, send_sems, recv_sems):
        starts, waits, sem = [], [], 0
        for j, job in enumerate(jobs):
            mine_in = [ins[k] for k, (jj, _) in enumerate(op_order) if jj == j]
            mine_out = [outs[k] for k, (jj, _) in enumerate(out_order) if jj == j]
            s, w = job.make(mine_in, mine_out, _Shifted(send_sems, sem), _Shifted(recv_sems, sem))
            starts, waits, sem = starts + s, waits + w, sem + job.n_sems
        return starts, waits

    return _Comm(tuple(jobs[j].operands[i] for j, i in op_order), tuple(jobs[j].out_shapes[i] for j, i in out_order),
                 sum(job.n_sems for job in jobs), make, in_place=sum(job.in_place for job in jobs))


def _mesh_place():
    x, y, c = lax.axis_index("x"), lax.axis_index("y"), lax.axis_index("c")
    return x, y, c, [(1 - x, y), (x, 1 - y), (1 - x, 1 - y)]


def _gather_job(bufs):
    per = 4

    def make(ins, outs, send_sems, recv_sems):
        del ins
        x, y, c, chips = _mesh_place()
        starts, waits = [], []
        for a, out in enumerate(outs):
            mine = src = out.at[4 * x + 2 * y + c]
            to = [(x, y, 1 - c)] + [(px, py, c) for px, py in chips]
            sends = [pltpu.make_async_remote_copy(
                src_ref=src, dst_ref=mine, send_sem=send_sems.at[per * a + k],
                recv_sem=recv_sems.at[per * a + k], device_id=dev, device_id_type=MESH)
                for k, dev in enumerate(to)]
            recvs = [pltpu.make_async_remote_copy(
                src_ref=src, dst_ref=out.at[4 * px + 2 * py + pc], send_sem=send_sems.at[per * a + k],
                recv_sem=recv_sems.at[per * a + k], device_id=(px, py, pc), device_id_type=MESH)
                for k, (px, py, pc) in enumerate(to)]
            starts += sends
            waits += [s.wait_send for s in sends] + [r.wait_recv for r in recvs]
        return starts, waits

    shapes = tuple(jax.ShapeDtypeStruct(b.shape, b.dtype) for b in bufs)
    return _Comm(tuple(bufs), shapes, per * len(bufs), make, in_place=len(bufs))


def _slots(x, y, c):
    return 4 * x + 2 * y + c, 4 * (1 - x) + 2 * y + c, 4 * x + 2 * (1 - y) + c, 4 * (1 - x) + 2 * (1 - y) + c


def _gather2d_first(halves):
    per = 2

    def make(ins, outs, send_sems, recv_sems):
        x, y, c, _ = _mesh_place()
        me, xn, yn, _ = _slots(x, y, c)
        starts, waits = [], []
        for a, (src, out) in enumerate(zip(ins, outs)):
            blk = src.at[c]
            rows = blk.shape[0] // 2
            upper, lower = pl.ds(0, rows), pl.ds(rows, rows)

            def copy(k, src_ref, dst_ref, dev, a=a):
                return pltpu.make_async_remote_copy(
                    src_ref=src_ref, dst_ref=dst_ref, send_sem=send_sems.at[per * a + k],
                    recv_sem=recv_sems.at[per * a + k], device_id=dev, device_id_type=MESH)

            sends = [copy(0, blk.at[upper], out.at[me, upper], (1 - x, y, c)),
                     copy(1, blk.at[lower], out.at[me, lower], (x, 1 - y, c))]
            recvs = [copy(0, blk.at[upper], out.at[xn, upper], (1 - x, y, c)),
                     copy(1, blk.at[lower], out.at[yn, lower], (x, 1 - y, c))]
            starts += sends
            waits += [s.wait_send for s in sends] + [r.wait_recv for r in recvs]
        return starts, waits

    shapes = tuple(jax.ShapeDtypeStruct((N_DEV,) + h.shape[1:], h.dtype) for h in halves)
    return _Comm(tuple(halves), shapes, per * len(halves), make)


def _gather2d_second(bufs, halves):
    per = 4
    n_arr = len(bufs)

    def make(ins, outs, send_sems, recv_sems):
        x, y, c, _ = _mesh_place()
        me, xn, yn, dg = _slots(x, y, c)
        starts, waits = [], []
        for a, buf in enumerate(outs):
            own = ins[n_arr + a].at[c]
            rows = buf.shape[1] // 2
            upper, lower = pl.ds(0, rows), pl.ds(rows, rows)
            plan = [(own.at[upper], me, upper, (x, 1 - y, c), yn), (buf.at[xn, upper], xn, upper, (x, 1 - y, c), dg),
                    (own.at[lower], me, lower, (1 - x, y, c), xn), (buf.at[yn, lower], yn, lower, (1 - x, y, c), dg)]
            for k, (src, slot, part, dev, landing) in enumerate(plan):
                sems = dict(send_sem=send_sems.at[per * a + k], recv_sem=recv_sems.at[per * a + k],
                            device_id=dev, device_id_type=MESH)
                send = pltpu.make_async_remote_copy(src_ref=src, dst_ref=buf.at[slot, part], **sems)
                arrival = pltpu.make_async_remote_copy(src_ref=src, dst_ref=buf.at[landing, part], **sems)
                starts.append(send)
                waits += [send.wait_send, arrival.wait_recv]
        return starts, waits

    shapes = tuple(jax.ShapeDtypeStruct(b.shape, b.dtype) for b in bufs)
    return _Comm(tuple(bufs) + tuple(halves), shapes, per * n_arr, make, in_place=n_arr)


def _gather_forward(bufs, name):
    n_arr = len(bufs)

    def body(*refs):
        outs = refs[n_arr:2 * n_arr]
        send_sems, recv_sems = refs[2 * n_arr:]
        x, y, c, chips = _mesh_place()
        sends, recvs = [], []
        for a, buf in enumerate(outs):
            for j, (px, py) in enumerate(chips):
                mine, theirs = buf.at[4 * px + 2 * py + c], buf.at[4 * px + 2 * py + 1 - c]
                sems = dict(send_sem=send_sems.at[3 * a + j], recv_sem=recv_sems.at[3 * a + j],
                            device_id=(x, y, 1 - c), device_id_type=MESH)
                sends.append(pltpu.make_async_remote_copy(src_ref=mine, dst_ref=mine, **sems))
                recvs.append(pltpu.make_async_remote_copy(src_ref=mine, dst_ref=theirs, **sems))
        for cp in sends:
            cp.start()
        for s, r in zip(sends, recvs):
            s.wait_send()
            r.wait_recv()

    return pl.pallas_call(
        body, name=name, out_shape=[jax.ShapeDtypeStruct(b.shape, b.dtype) for b in bufs],
        in_specs=[_any()] * n_arr, out_specs=[_any()] * n_arr,
        input_output_aliases={a: a for a in range(n_arr)},
        scratch_shapes=[pltpu.SemaphoreType.DMA((3 * n_arr,)), pltpu.SemaphoreType.DMA((3 * n_arr,))],
    )(*bufs)


def _scatter_job(chip_sums, rows=None, into=()):
    n_into = len(into)

    def part(ref):
        return ref if rows is None else ref.at[pl.ds(rows[0], rows[1])]

    def make(ins, outs, send_sems, recv_sems):
        x, y, c, chips = _mesh_place()
        copies = [pltpu.make_async_remote_copy(
            src_ref=part(src.at[2 * px + py]), dst_ref=part(out.at[j]), send_sem=send_sems.at[3 * a + j],
            recv_sem=recv_sems.at[3 * a + j], device_id=(px, py, c), device_id_type=MESH)
            for a, (src, out) in enumerate(zip(ins[n_into:], outs)) for j, (px, py) in enumerate(chips)]
        return copies, [cp.wait for cp in copies]

    shapes = tuple(jax.ShapeDtypeStruct((3,) + s.shape[1:], s.dtype) for s in chip_sums)
    return _Comm(tuple(into) + tuple(chip_sums), shapes, 3 * len(chip_sums), make, in_place=n_into)


def _all_gather8(blocks, name, split=False, forward=(), riders=(), skip_own=()):
    n_arr, n_fwd = len(blocks), len(forward)
    splits = list(split) if isinstance(split, (list, tuple)) else [split] * n_arr
    own_slots = [a not in skip_own for a in range(n_arr)]
    rider_in = sum(len(r.operands) for r in riders)
    rider_out = sum(len(r.out_shapes) for r in riders)

    def body(*refs):
        x_refs, refs = refs[:n_arr], refs[n_arr + n_fwd:]
        r_ins, refs = refs[:rider_in], refs[rider_in:]
        out_refs, refs = refs[:n_arr], refs[n_arr:]
        fwd_refs, refs = refs[:n_fwd], refs[n_fwd:]
        r_outs, refs = refs[:rider_out], refs[rider_out:]
        (send_sems, recv_sems, local_sems), rider_sems = refs[:3], refs[3:]
        x, y, c, chips = _mesh_place()
        me, sibling = (x, y, c), (x, y, 1 - c)
        passing = []
        for f, buf in enumerate(fwd_refs):
            for j, (px, py) in enumerate(chips):
                mine, theirs = buf.at[4 * px + 2 * py + c], buf.at[4 * px + 2 * py + 1 - c]
                sems = dict(send_sem=send_sems.at[7 * n_arr + 3 * f + j], recv_sem=recv_sems.at[7 * n_arr + 3 * f + j],
                            device_id=sibling, device_id_type=MESH)
                passing.append((pltpu.make_async_remote_copy(src_ref=mine, dst_ref=mine, **sems),
                                pltpu.make_async_remote_copy(src_ref=mine, dst_ref=theirs, **sems)))
        for send, _ in passing:
            send.start()
        arrays = []
        for a, (x_ref, out_ref) in enumerate(zip(x_refs, out_refs)):
            src_mine = x_ref.at[c] if splits[a] else x_ref

            def copy(k, blk, to, src=None, a=a, out_ref=out_ref):
                dst = out_ref.at[4 * blk[0] + 2 * blk[1] + blk[2]]
                return pltpu.make_async_remote_copy(
                    src_ref=dst if src is None else src, dst_ref=dst,
                    send_sem=send_sems.at[7 * a + k], recv_sem=recv_sems.at[7 * a + k],
                    device_id=to, device_id_type=MESH)

            mine = pltpu.make_async_copy(src_mine, out_ref.at[4 * x + 2 * y + c], local_sems.at[a])
            first = [copy(0, me, sibling, src=src_mine)] if own_slots[a] else []
            first += [copy(1 + j, me, (*chip, c), src=src_mine) for j, chip in enumerate(chips)]
            for cp in first + ([mine] if own_slots[a] else []):
                cp.start()
            arrays.append((copy, mine, first, own_slots[a]))
        rider_waits, i0, o0 = [], 0, 0
        for n, job in enumerate(riders):
            k_in, k_out = len(job.operands), len(job.out_shapes)
            starts, waits = job.make(r_ins[i0:i0 + k_in], r_outs[o0:o0 + k_out],
                                     rider_sems[2 * n], rider_sems[2 * n + 1])
            for cp in starts:
                cp.start()
            rider_waits += waits
            i0, o0 = i0 + k_in, o0 + k_out
        sent = []
        for copy, mine, first, own in arrays:
            passed = [copy(4 + j, (*chip, c), sibling) for j, chip in enumerate(chips)]
            for j, chip in enumerate(chips):
                copy(1 + j, (*chip, c), me).wait_recv()
                passed[j].start()
            sent += first + passed
        for copy, mine, first, own in arrays:
            if own:
                copy(0, sibling, me).wait_recv()
                mine.wait()
            for j, chip in enumerate(chips):
                copy(4 + j, (*chip, 1 - c), me).wait_recv()
        for cp in sent:
            cp.wait_send()
        for send, arrival in passing:
            send.wait_send()
            arrival.wait_recv()
        for wait in rider_waits:
            wait()

    n_sems = 7 * n_arr + 3 * n_fwd
    rider_operands = [a for r in riders for a in r.operands]
    rider_shapes = [s for r in riders for s in r.out_shapes]
    return pl.pallas_call(
        body, name=name,
        out_shape=[jax.ShapeDtypeStruct((N_DEV,) + tuple(b.shape[1:] if s else b.shape), b.dtype)
                   for b, s in zip(blocks, splits)]
        + [jax.ShapeDtypeStruct(f.shape, f.dtype) for f in forward] + rider_shapes,
        in_specs=[_any()] * (n_arr + n_fwd + rider_in), out_specs=[_any()] * (n_arr + n_fwd + rider_out),
        input_output_aliases={n_arr + f: n_arr + f for f in range(n_fwd)},
        scratch_shapes=[pltpu.SemaphoreType.DMA((n_sems,)), pltpu.SemaphoreType.DMA((n_sems,)),
                        pltpu.SemaphoreType.DMA((n_arr,))]
        + [pltpu.SemaphoreType.DMA((r.n_sems,)) for r in riders for _ in range(2)],
    )(*blocks, *forward, *rider_operands)


def _first_stage_copies(srcs, lands, splits, owns, send_sems, recv_sems, local_sems):
    x, y, c, chips = _mesh_place()
    same_core = [(px, py, c) for px, py in chips]
    everyone = [(x, y, 1 - c)] + [(px, py, pc) for px, py in chips for pc in (c, 1 - c)]
    per_array, local, k = [], [], 0
    for a, (src, land, split, own) in enumerate(zip(srcs, lands, splits, owns)):
        mine = src.at[c] if split else src
        peers = (([(x, y, 1 - c)] if own else []) + same_core) if split else everyone
        pairs = []
        for px, py, pc in peers:
            sems = dict(send_sem=send_sems.at[k], recv_sem=recv_sems.at[k],
                        device_id=(px, py, pc), device_id_type=MESH)
            pairs.append((pltpu.make_async_remote_copy(src_ref=mine, dst_ref=land.at[4 * x + 2 * y + c], **sems),
                          pltpu.make_async_remote_copy(src_ref=mine, dst_ref=land.at[4 * px + 2 * py + pc], **sems)))
            k += 1
        per_array.append(pairs)
        if own:
            local.append(pltpu.make_async_copy(mine, land.at[4 * x + 2 * y + c], local_sems.at[a]))
    return per_array, local


def _first_stage_start(groups, name):
    n_groups = len(groups)
    hbm, sem = pl.BlockSpec(memory_space=pltpu.HBM), pl.BlockSpec(memory_space=pltpu.SEMAPHORE)
    operands, sem_shapes, offsets = [], [], []
    for blocks, splits, owns in groups:
        n_sems = sum((4 if own else 3) if split else N_DEV - 1 for split, own in zip(splits, owns))
        sem_shapes += [pltpu.SemaphoreType.DMA((n_sems,)), pltpu.SemaphoreType.DMA((n_sems,)),
                       pltpu.SemaphoreType.DMA((len(blocks),))]
        offsets.append(len(operands))
        operands += list(blocks) + [lax.empty((N_DEV,) + tuple(b.shape[1:] if split else b.shape), b.dtype)
                                    for b, split in zip(blocks, splits)]
    n_ops = len(operands)

    def body(*refs):
        token = refs[-1]
        for g, (blocks, splits, owns) in enumerate(groups):
            n, at = len(blocks), offsets[g]
            per_array, local = _first_stage_copies(refs[at:at + n], refs[at + n:at + 2 * n], splits, owns,
                                                   *refs[n_ops + 3 * g:n_ops + 3 * g + 3])
            for pairs in per_array:
                for send, _ in pairs:
                    send.start()
            for cp in local:
                cp.start()
        token[...] = jnp.zeros_like(token)

    operands = [pltpu.with_memory_space_constraint(a, pltpu.HBM) for a in operands]
    out = pl.pallas_call(
        body, name=name,
        out_shape=sem_shapes + [pltpu.HBM(a.shape, a.dtype) for a in operands]
        + [jax.ShapeDtypeStruct((8, LANES), F32)],
        in_specs=[hbm] * n_ops,
        out_specs=[sem] * (3 * n_groups) + [hbm] * n_ops + [pl.BlockSpec(memory_space=pltpu.VMEM)],
        input_output_aliases={i: 3 * n_groups + i for i in range(n_ops)},
        compiler_params=pltpu.CompilerParams(has_side_effects=pltpu.SideEffectType.DATAFLOW_SIDE_EFFECTING),
    )(*operands)
    thru = out[3 * n_groups:-1]
    return [list(out[3 * g:3 * g + 3]) + list(thru[offsets[g]:offsets[g] + 2 * len(groups[g][0])]) + [out[-1]]
            for g in range(n_groups)]


def _first_stage_finish(started, splits, owns, after, name, thru=()):
    send_sems, recv_sems, local_sems, *bufs, _ = started
    n = len(bufs) // 2
    hbm, sem = pl.BlockSpec(memory_space=pltpu.HBM), pl.BlockSpec(memory_space=pltpu.SEMAPHORE)

    def body(*refs):
        srcs, lands = refs[:n], refs[n:2 * n]
        send_sems, recv_sems, local_sems = refs[2 * n:2 * n + 3]
        per_array, local = _first_stage_copies(srcs, lands, splits, owns, send_sems, recv_sems, local_sems)
        for pairs in per_array:
            for send, arrival in pairs:
                send.wait_send()
                arrival.wait_recv()
        for cp in local:
            cp.wait()

    first_thru = 2 * n + 3 + len(after)
    out = pl.pallas_call(
        body, name=name,
        out_shape=[pltpu.HBM(b.shape, b.dtype) for b in bufs] + [jax.ShapeDtypeStruct(t.shape, t.dtype) for t in thru],
        in_specs=[hbm] * (2 * n) + [sem, sem, sem] + [_any()] * (len(after) + len(thru)),
        out_specs=[hbm] * (2 * n) + [_any()] * len(thru),
        input_output_aliases={**{i: i for i in range(2 * n)}, **{first_thru + j: 2 * n + j for j in range(len(thru))}},
        compiler_params=pltpu.CompilerParams(has_side_effects=pltpu.SideEffectType.DATAFLOW_SIDE_EFFECTING),
    )(*bufs, send_sems, recv_sems, local_sems, *after, *thru)
    return out[n:]


def _split_scatter_copies(src, land, send_sems, recv_sems):
    x, y, c, chips = _mesh_place()
    return [pltpu.make_async_remote_copy(
        src_ref=src.at[2 * px + py], dst_ref=land.at[j], send_sem=send_sems.at[j], recv_sem=recv_sems.at[j],
        device_id=(px, py, c), device_id_type=MESH) for j, (px, py) in enumerate(chips)]


def _scatter_start(chip_sums, name):
    hbm, sem = pl.BlockSpec(memory_space=pltpu.HBM), pl.BlockSpec(memory_space=pltpu.SEMAPHORE)

    def body(src, land, send_sems, recv_sems, src_thru, land_thru, token):
        del src_thru, land_thru
        for cp in _split_scatter_copies(src, land, send_sems, recv_sems):
            cp.start()
        token[...] = jnp.zeros_like(token)

    land = lax.empty((N_CHIPS - 1,) + chip_sums.shape[1:], chip_sums.dtype)
    operands = [pltpu.with_memory_space_constraint(a, pltpu.HBM) for a in (chip_sums, land)]
    return pl.pallas_call(
        body, name=name,
        out_shape=[pltpu.SemaphoreType.DMA((N_CHIPS - 1,)), pltpu.SemaphoreType.DMA((N_CHIPS - 1,))]
        + [pltpu.HBM(a.shape, a.dtype) for a in operands] + [jax.ShapeDtypeStruct((8, LANES), F32)],
        in_specs=[hbm, hbm], out_specs=[sem, sem, hbm, hbm, pl.BlockSpec(memory_space=pltpu.VMEM)],
        input_output_aliases={0: 2, 1: 3},
        compiler_params=pltpu.CompilerParams(has_side_effects=pltpu.SideEffectType.DATAFLOW_SIDE_EFFECTING),
    )(*operands)


def _scatter_wait(started, after, name):
    send_sems, recv_sems, src, land, _ = started
    hbm, sem = pl.BlockSpec(memory_space=pltpu.HBM), pl.BlockSpec(memory_space=pltpu.SEMAPHORE)

    def body(src, land, send_sems, recv_sems, after_ref, src_thru, land_thru):
        del after_ref, src_thru, land_thru
        for cp in _split_scatter_copies(src, land, send_sems, recv_sems):
            cp.wait()

    return pl.pallas_call(
        body, name=name, out_shape=[pltpu.HBM(src.shape, src.dtype), pltpu.HBM(land.shape, land.dtype)],
        in_specs=[hbm, hbm, sem, sem, _any()], out_specs=[hbm, hbm], input_output_aliases={0: 0, 1: 1},
        compiler_params=pltpu.CompilerParams(has_side_effects=pltpu.SideEffectType.DATAFLOW_SIDE_EFFECTING),
    )(src, land, send_sems, recv_sems, after)


def _split_gather_copies(srcs, lands, fwds, shares, send_sems, recv_sems):
    x, y, c, chips = _mesh_place()
    peers = [(x, y, 1 - c)] + [(px, py, pc) for px, py in chips for pc in (c, 1 - c)]
    pairs = []
    for a, (src, land) in enumerate(zip(srcs, lands)):
        for k, (px, py, pc) in enumerate(peers):
            sems = dict(send_sem=send_sems.at[7 * a + k], recv_sem=recv_sems.at[7 * a + k],
                        device_id=(px, py, pc), device_id_type=MESH)
            pairs.append((pltpu.make_async_remote_copy(src_ref=src, dst_ref=land.at[4 * x + 2 * y + c], **sems),
                          pltpu.make_async_remote_copy(src_ref=src, dst_ref=land.at[4 * px + 2 * py + pc], **sems)))
    for f, buf in enumerate(fwds):
        for j, (px, py) in enumerate(chips):
            mine, theirs = buf.at[4 * px + 2 * py + c], buf.at[4 * px + 2 * py + 1 - c]
            k = 7 * len(srcs) + 3 * f + j
            sems = dict(send_sem=send_sems.at[k], recv_sem=recv_sems.at[k],
                        device_id=(x, y, 1 - c), device_id_type=MESH)
            pairs.append((pltpu.make_async_remote_copy(src_ref=mine, dst_ref=mine, **sems),
                          pltpu.make_async_remote_copy(src_ref=mine, dst_ref=theirs, **sems)))
    for s, buf in enumerate(shares):
        k = 7 * len(srcs) + 3 * len(fwds) + s
        sems = dict(send_sem=send_sems.at[k], recv_sem=recv_sems.at[k], device_id=(x, y, 1 - c), device_id_type=MESH)
        pairs.append((pltpu.make_async_remote_copy(src_ref=buf.at[c], dst_ref=buf.at[c], **sems),
                      pltpu.make_async_remote_copy(src_ref=buf.at[c], dst_ref=buf.at[1 - c], **sems)))
    return pairs


def _gather_start(blocks, forward, shares, after, name):
    n, n_fwd = len(blocks), len(forward)
    n_sems = 7 * n + 3 * n_fwd + len(shares)
    n_bufs = 2 * n + n_fwd + len(shares)
    hbm, sem = pl.BlockSpec(memory_space=pltpu.HBM), pl.BlockSpec(memory_space=pltpu.SEMAPHORE)

    def body(*refs):
        srcs, lands, fwds, swaps = refs[:n], refs[n:2 * n], refs[2 * n:2 * n + n_fwd], refs[2 * n + n_fwd:n_bufs]
        send_sems, recv_sems = refs[n_bufs + 1:n_bufs + 3]
        token, local_sems = refs[-2:]
        x, y, c, _ = _mesh_place()
        own = [pltpu.make_async_copy(src, land.at[4 * x + 2 * y + c], local_sems.at[a])
               for a, (src, land) in enumerate(zip(srcs, lands))]
        for cp in own:
            cp.start()
        for send, _ in _split_gather_copies(srcs, lands, fwds, swaps, send_sems, recv_sems):
            send.start()
        token[...] = jnp.zeros_like(token)
        for cp in own:
            cp.wait()

    lands = [lax.empty((N_DEV,) + b.shape, b.dtype) for b in blocks]
    operands = [pltpu.with_memory_space_constraint(a, pltpu.HBM)
                for a in list(blocks) + lands + list(forward) + list(shares)]
    return pl.pallas_call(
        body, name=name,
        out_shape=[pltpu.SemaphoreType.DMA((n_sems,)), pltpu.SemaphoreType.DMA((n_sems,))]
        + [pltpu.HBM(a.shape, a.dtype) for a in operands] + [jax.ShapeDtypeStruct((8, LANES), F32)],
        in_specs=[hbm] * len(operands) + [_any()],
        out_specs=[sem, sem] + [hbm] * len(operands) + [pl.BlockSpec(memory_space=pltpu.VMEM)],
        input_output_aliases={i: 2 + i for i in range(len(operands))},
        scratch_shapes=[pltpu.SemaphoreType.DMA((n,))],
        compiler_params=pltpu.CompilerParams(has_side_effects=pltpu.SideEffectType.DATAFLOW_SIDE_EFFECTING),
    )(*operands, after)


def _gather_wait(started, n, n_shares, after, name):
    send_sems, recv_sems, *bufs, _ = started
    n_bufs = len(bufs)
    n_fwd = n_bufs - 2 * n - n_shares
    hbm, sem = pl.BlockSpec(memory_space=pltpu.HBM), pl.BlockSpec(memory_space=pltpu.SEMAPHORE)

    def body(*refs):
        srcs, lands, fwds, swaps = refs[:n], refs[n:2 * n], refs[2 * n:2 * n + n_fwd], refs[2 * n + n_fwd:n_bufs]
        send_sems, recv_sems = refs[n_bufs:n_bufs + 2]
        for send, arrival in _split_gather_copies(srcs, lands, fwds, swaps, send_sems, recv_sems):
            send.wait_send()
            arrival.wait_recv()

    out = pl.pallas_call(
        body, name=name, out_shape=[pltpu.HBM(b.shape, b.dtype) for b in bufs],
        in_specs=[hbm] * len(bufs) + [sem, sem, _any()], out_specs=[hbm] * len(bufs),
        input_output_aliases={i: i for i in range(len(bufs))},
        compiler_params=pltpu.CompilerParams(has_side_effects=pltpu.SideEffectType.DATAFLOW_SIDE_EFFECTING),
    )(*bufs, send_sems, recv_sems, after)
    return out[n:]


def _share_job(bufs):
    def make(ins, outs, send_sems, recv_sems):
        del ins
        x, y, c, _ = _mesh_place()
        sems = lambda a: dict(send_sem=send_sems.at[a], recv_sem=recv_sems.at[a],
                              device_id=(x, y, 1 - c), device_id_type=MESH)
        sends = [pltpu.make_async_remote_copy(src_ref=o.at[c], dst_ref=o.at[c], **sems(a)) for a, o in enumerate(outs)]
        arrivals = [pltpu.make_async_remote_copy(src_ref=o.at[c], dst_ref=o.at[1 - c], **sems(a))
                    for a, o in enumerate(outs)]
        return sends, [s.wait_send for s in sends] + [r.wait_recv for r in arrivals]

    shapes = tuple(jax.ShapeDtypeStruct(b.shape, b.dtype) for b in bufs)
    return _Comm(tuple(bufs), shapes, len(bufs), make, in_place=len(bufs))


def _gelu_tanh(z):
    k = math.sqrt(2.0 / math.pi)
    t = jnp.tanh(k * (z + 0.044715 * (z * z * z)))
    return 0.5 * z * (1.0 + t), t


def _gelu_tanh_grad(z, t):
    k = math.sqrt(2.0 / math.pi)
    return 0.5 * (1.0 + t) + 0.5 * z * (1.0 - t * t) * (k * (1.0 + 3.0 * 0.044715 * (z * z)))


def _rope_angle_kernel(pos_row, invf_col):
    seq = pos_row.shape[1]

    def body(p_ref, f_ref, cos_ref, sin_ref):
        ang = p_ref[...].astype(F32) * f_ref[...]
        cos_ref[...] = jnp.cos(ang)
        sin_ref[...] = jnp.sin(ang)

    return pl.pallas_call(
        body, name="rope_angles", grid=(1,), out_shape=[jax.ShapeDtypeStruct((ROT_DIM // 2, seq), F32)] * 2,
        in_specs=[_full((1, seq)), _full((ROT_DIM // 2, 1))], out_specs=[_full((ROT_DIM // 2, seq))] * 2,
        compiler_params=_params("arbitrary"),
    )(pos_row, invf_col)


def _rope_lane_tables(cos, sin):
    cos_t, sin_t = cos.T, sin.T
    seq, half = cos_t.shape
    ones = jnp.ones((seq, HEAD_DIM - ROT_DIM), F32)
    c64 = jnp.concatenate([cos_t, cos_t, ones], axis=1)
    s1 = jnp.concatenate([sin_t, jnp.zeros((seq, HEAD_DIM - half), F32)], axis=1)
    s2 = jnp.concatenate([jnp.zeros((seq, half), F32), sin_t, jnp.zeros((seq, HEAD_DIM - ROT_DIM), F32)], axis=1)
    return jnp.concatenate([jnp.tile(t, (1, LANES // HEAD_DIM)) for t in (c64, s1, s2)], axis=1)


def _rope_apply(t, tab, sign):
    reps = t.shape[1] // LANES
    c_tab, s1, s2 = (jnp.tile(tab[:, LANES * k:LANES * (k + 1)], (1, reps)) if reps > 1
                     else tab[:, LANES * k:LANES * (k + 1)] for k in range(3))
    half = ROT_DIM // 2
    up = pltpu.roll(t, t.shape[1] - half, 1)
    down = pltpu.roll(t, half, 1)
    return t * c_tab + sign * (down * s2 - up * s1)


def _lane_masks(shape):
    lane = lax.broadcasted_iota(jnp.int32, shape, 1)
    return lane < HEAD_DIM, lane >= HEAD_DIM


HEADS_PER_GROUP = N_Q_HEADS // N_KV_HEADS
ATTN_SCALE = 1.0 / math.sqrt(HEAD_DIM)


def _attn_bias_t(first_block):
    kj = lax.broadcasted_iota(jnp.int32, (2 * CHUNK, CHUNK), 0)
    qi = lax.broadcasted_iota(jnp.int32, (2 * CHUNK, CHUNK), 1)
    ok = (kj > qi) & (kj <= qi + CHUNK)
    if first_block is not None:
        ok = ok & (jnp.logical_not(first_block) | (kj >= CHUNK))
    return jnp.tile(jnp.where(ok, 0.0, -jnp.inf), (1, HEADS_PER_GROUP))


def _group_rows(x, g, lo, hi):
    rows = []
    for r in range(HEADS_PER_GROUP):
        h = HEADS_PER_GROUP * g + r
        pair = x[:, LANES * (h // 2):LANES * (h // 2 + 1)]
        rows.append(jnp.where(hi if h % 2 else lo, pair, 0.0))
    return jnp.concatenate(rows, axis=0)


def _pairs_from_rows(rows, lo):
    return [jnp.where(lo, rows[2 * CHUNK * k:2 * CHUNK * k + CHUNK], rows[2 * CHUNK * k + CHUNK:2 * CHUNK * (k + 1)])
            for k in range(HEADS_PER_GROUP // 2)]


def _group_dup(a, b, g, lo2):
    return jnp.where(lo2, a, b) if g == 0 else jnp.where(lo2, b, a)


def _sink_row(sink_ref, g):
    return jnp.concatenate([sink_ref[HEADS_PER_GROUP * g + r:HEADS_PER_GROUP * g + r + 1, :]
                            for r in range(HEADS_PER_GROUP)], axis=1)


def _attn_probs_t(k_dup, q_rows, bias_t, sink_row):
    s_t = _dot_nt(k_dup, q_rows) * ATTN_SCALE + bias_t
    m = jnp.maximum(jnp.max(s_t, axis=0, keepdims=True), sink_row)
    p = jnp.exp(s_t - m)
    e_sink = jnp.exp(sink_row - m)
    inv = 1.0 / (jnp.sum(p, axis=0, keepdims=True) + e_sink)
    return p * inv, e_sink * inv


def _sgu_forward_pair(wm, vp, j):
    lo, hi = _lane_masks(vp.shape)
    lhs = jnp.concatenate([wm[2 * j], wm[2 * j + 1]], axis=1)
    rhs = jnp.concatenate([jnp.where(lo, vp, 0.0), jnp.where(hi, vp, 0.0)], axis=0)
    return _dot(lhs, rhs)


def _masked_spatial(w_ref):
    t = lax.broadcasted_iota(jnp.int32, (CHUNK, CHUNK), 0)
    s = lax.broadcasted_iota(jnp.int32, (CHUNK, CHUNK), 1)
    tril = s <= t
    return [jnp.where(tril, w_ref[g], 0.0) for g in range(GMLP_GROUPS)], tril, s >= t


def _mod_kernel(c_all, w_shard, b_shard, comm=None):
    n = w_shard.shape[1]
    tn = 512

    def body(c_ref, w_ref, b_ref, mod_ref, act_ref):
        cv = c_ref[...]
        act = cv * (1.0 / (1.0 + jnp.exp(-cv)))
        act_ref[...] = act
        mod_ref[...] = _dot(act, w_ref[...]) + b_ref[...]

    return _hosted_call(
        body, comm, name="ada_mod", grid=(n // tn,),
        out_shape=[jax.ShapeDtypeStruct((N_DEV, n), F32), jax.ShapeDtypeStruct((N_DEV, D_MODEL), F32)],
        in_specs=[_full((N_DEV, D_MODEL)), pl.BlockSpec((D_MODEL, tn), lambda i: (0, i)),
                  pl.BlockSpec((1, tn), lambda i: (0, i))],
        out_specs=[pl.BlockSpec((N_DEV, tn), lambda i: (0, i)), _full((N_DEV, D_MODEL))],
        semantics=("arbitrary",),
    )(c_all, w_shard, b_shard)


def _load_chip_blocks(chip_ref, gathered, local, dsts, sems, first_sem=0):
    for k, dst in enumerate(dsts):
        @pl.when(chip_ref[0] == k)
        def _():
            pltpu.make_async_copy(local, dst, sems.at[first_sem + k]).start()

        @pl.when(chip_ref[0] != k)
        def _():
            pltpu.make_async_copy(gathered.at[k], dst, sems.at[first_sem + k]).start()
    return [pltpu.make_async_copy(local, dst, sems.at[first_sem + k]).wait for k, dst in enumerate(dsts)]


def _in_proj_kernel(x, vecs, w_in_t, comm=None):
    seq = x.shape[0]
    tm = 512

    def body(x_ref, v_ref, w_ref, proj_ref, h_ref):
        xv = x_ref[...]
        rstd = lax.rsqrt(_mean_last(xv * xv) + EPS)
        n1 = (xv * rstd) * v_ref[0:1, :]
        h = n1 * (1.0 + v_ref[2:3, :]) + v_ref[1:2, :]
        hb = h.astype(MXU_DTYPE)
        h_ref[...] = hb
        proj_ref[...] = _dot_nt(hb, w_ref[...])

    return _hosted_call(
        body, comm, name="in_proj", grid=(seq // tm,),
        out_shape=[jax.ShapeDtypeStruct((seq, IN_PROJ_WIDTH), F32),
                   jax.ShapeDtypeStruct((seq, D_MODEL), MXU_DTYPE)],
        in_specs=[pl.BlockSpec((tm, D_MODEL), lambda i: (i, 0)), _full((8, D_MODEL)),
                  _full((IN_PROJ_WIDTH, D_MODEL))],
        out_specs=[pl.BlockSpec((tm, IN_PROJ_WIDTH), lambda i: (i, 0)),
                   pl.BlockSpec((tm, D_MODEL), lambda i: (i, 0))],
        semantics=("arbitrary",),
    )(x, vecs, w_in_t)


MIXER_BLOCKS_PER_STEP = 4
KV_START = 2 * GMLP_WIDTH + ATTN_WIDTH


def _mixer_fwd_kernel(proj, rope_tab, w_spatial, bias_full, sink_rows, comm=None):
    seq = proj.shape[0]
    per = MIXER_BLOCKS_PER_STEP
    steps = seq // (CHUNK * per)
    kv_col = KV_START // (2 * KV_WIDTH)

    def body(proj_ref, prev_ref, tab_ref, ptab_ref, w_ref, bias_ref, sink_ref, cat_ref):
        i = pl.program_id(0)
        wm, _, _ = _masked_spatial(w_ref)
        lo, hi = _lane_masks((CHUNK, LANES))
        lo2, _ = _lane_masks((2 * CHUNK, LANES))
        o = 2 * GMLP_WIDTH
        for s in range(per):
            rows, before = slice(CHUNK * s, CHUNK * (s + 1)), slice(CHUNK * (s - 1), CHUNK * s)
            for j in range(GMLP_GROUPS // 2):
                cols = slice(LANES * j, LANES * (j + 1))
                vcols = slice(GMLP_WIDTH + LANES * j, GMLP_WIDTH + LANES * (j + 1))
                u, _ = _gelu_tanh(proj_ref[rows, cols])
                vp, _ = _gelu_tanh(proj_ref[rows, vcols])
                sv = _sgu_forward_pair(wm, vp, j) + bias_ref[:, cols]
                cat_ref[rows, cols] = (u * sv).astype(cat_ref.dtype)
            tab = tab_ref[rows, :]
            if s == 0:
                prev_kv, prev_tab, first = prev_ref[...], ptab_ref[...], i == 0
            else:
                prev_kv, prev_tab, first = proj_ref[before, KV_START:KV_START + 2 * KV_WIDTH], tab_ref[before, :], None
            q_r = _rope_apply(proj_ref[rows, o:o + ATTN_WIDTH], tab, 1.0)
            k_cur = _rope_apply(proj_ref[rows, KV_START:KV_START + KV_WIDTH], tab, 1.0)
            k_prev = _rope_apply(prev_kv[:, 0:KV_WIDTH], prev_tab, 1.0)
            k_a = jnp.concatenate([k_prev, k_cur], axis=0)
            v_a = jnp.concatenate([prev_kv[:, KV_WIDTH:2 * KV_WIDTH],
                                   proj_ref[rows, KV_START + KV_WIDTH:KV_START + 2 * KV_WIDTH]], axis=0)
            k_b = pltpu.roll(k_a, HEAD_DIM, 1)
            v_b = pltpu.roll(v_a, HEAD_DIM, 1)
            bias_t = _attn_bias_t(first)
            for g in range(N_KV_HEADS):
                p_t, _ = _attn_probs_t(_group_dup(k_a, k_b, g, lo2), _group_rows(q_r, g, lo, hi), bias_t,
                                       _sink_row(sink_ref, g))
                o_t = _dot(_group_dup(v_a, v_b, g, lo2).T, p_t)
                for k, pair in enumerate(_pairs_from_rows(o_t.T, lo)):
                    c0 = GMLP_WIDTH + LANES * (2 * g + k)
                    cat_ref[rows, c0:c0 + LANES] = pair.astype(cat_ref.dtype)

    return _hosted_call(
        body, comm, name="mixer_fwd", grid=(steps,),
        out_shape=[jax.ShapeDtypeStruct((seq, D_MODEL), MXU_DTYPE)],
        in_specs=[pl.BlockSpec((CHUNK * per, IN_PROJ_WIDTH), lambda i: (i, 0)),
                  pl.BlockSpec((CHUNK, 2 * KV_WIDTH), lambda i: (jnp.maximum(per * i - 1, 0), kv_col)),
                  pl.BlockSpec((CHUNK * per, 3 * LANES), lambda i: (i, 0)),
                  pl.BlockSpec((CHUNK, 3 * LANES), lambda i: (jnp.maximum(per * i - 1, 0), 0)),
                  _full((GMLP_GROUPS, CHUNK, CHUNK)), _full((CHUNK, GMLP_WIDTH)),
                  _full((N_Q_HEADS, LANES))],
        out_specs=[pl.BlockSpec((CHUNK * per, D_MODEL), lambda i: (i, 0))],
        semantics=("arbitrary",),
    )(proj, proj, rope_tab, rope_tab, w_spatial, bias_full, sink_rows)


def _trunk_kernel(x, target, cat, vecs, chip_idx, gathered, local):
    seq = x.shape[0]
    tm = 256
    nj = D_FF // D_MODEL
    out_rows = D_MODEL // N_CHIPS

    def body(chip_ref, x_ref, t_ref, cat_ref, v_ref, g_out, g_w1, g_w2, l_out, l_w1, l_w2,
             dx1_ref, dcat_ref, dmix_ref, h2_ref, r_ref, da_ref, dff_ref, sums_ref,
             wout, w1, w2, a_scr, sem):
        i = pl.program_id(0)

        @pl.when(i == 0)
        def _():
            _load_chip_blocks(chip_ref, g_out, l_out,
                              [wout.at[pl.ds(out_rows * k, out_rows)] for k in range(N_CHIPS)], sem)
            _load_chip_blocks(chip_ref, g_w1, l_w1, [w1.at[k] for k in range(N_CHIPS)], sem, N_CHIPS)
            _load_chip_blocks(chip_ref, g_w2, l_w2, [w2.at[k] for k in range(N_CHIPS)], sem, 2 * N_CHIPS)
            sums_ref[...] = jnp.zeros_like(sums_ref)

        def arrived(local, dsts, first_sem):
            @pl.when(i == 0)
            def _():
                for k, dst in dsts:
                    pltpu.make_async_copy(local, dst, sem.at[first_sem + k]).wait()

        gate1, shift2, scale2 = v_ref[0:1, :], v_ref[1:2, :], v_ref[2:3, :]
        gate2, g_ffn, g_final = v_ref[3:4, :], v_ref[4:5, :], v_ref[5:6, :]

        arrived(l_out, [(k, wout.at[pl.ds(out_rows * k, out_rows)]) for k in range(N_CHIPS)], 0)
        mix = _dot(cat_ref[...], wout[...])
        x1 = x_ref[...] + gate1 * mix
        rstd2 = lax.rsqrt(_mean_last(x1 * x1) + EPS)
        xh2 = x1 * rstd2
        n2 = xh2 * g_ffn
        h2b = (n2 * (1.0 + scale2) + shift2).astype(MXU_DTYPE)
        h2_ref[...] = h2b
        ff = jnp.zeros((tm, D_MODEL), F32)
        for j in range(nj):
            arrived(l_w1, [(j, w1.at[j])], N_CHIPS)
            a = _dot(h2b, w1[j])
            a_scr[j] = a
            relu = jnp.maximum(a, 0.0)
            rb = (relu * relu).astype(MXU_DTYPE)
            r_ref[:, D_MODEL * j:D_MODEL * (j + 1)] = rb
            arrived(l_w2, [(j, w2.at[j])], 2 * N_CHIPS)
            ff = ff + _dot(rb, w2[j])
        x2 = x1 + gate2 * ff
        rstd3 = lax.rsqrt(_mean_last(x2 * x2) + EPS)
        xh3 = x2 * rstd3
        err = xh3 * g_final - t_ref[...]
        loss = 0.5 * _rowsum(_mean_last(err * err))
        dy = err * (1.0 / D_MODEL)
        dxh3 = dy * g_final
        dx2 = rstd3 * (dxh3 - xh3 * _mean_last(dxh3 * xh3))
        dffb = (dx2 * gate2).astype(MXU_DTYPE)
        dff_ref[...] = dffb
        dh2 = jnp.zeros((tm, D_MODEL), F32)
        for j in range(nj):
            dr = _dot_nt(dffb, w2[j])
            dab = (dr * (2.0 * jnp.maximum(a_scr[j], 0.0))).astype(MXU_DTYPE)
            da_ref[:, D_MODEL * j:D_MODEL * (j + 1)] = dab
            dh2 = dh2 + _dot_nt(dab, w1[j])
        dn2 = dh2 * (1.0 + scale2)
        dxh2 = dn2 * g_ffn
        dx1 = dx2 + rstd2 * (dxh2 - xh2 * _mean_last(dxh2 * xh2))
        dx1_ref[...] = dx1
        dmixb = (dx1 * gate1).astype(MXU_DTYPE)
        dmix_ref[...] = dmixb
        dcat_ref[...] = _dot_nt(dmixb, wout[...])

        sums_ref[0:1, :] += _rowsum(dh2)
        sums_ref[1:2, :] += _rowsum(dh2 * n2)
        sums_ref[2:3, :] += _rowsum(dx2 * ff)
        sums_ref[3:4, :] += _rowsum(dn2 * xh2)
        sums_ref[4:5, :] += _rowsum(dy * xh3)
        sums_ref[5:6, :] += _rowsum(dx1 * mix)
        sums_ref[6:7, :] += jnp.broadcast_to(loss, (1, D_MODEL))

    tok = lambda w: pl.BlockSpec((tm, w), lambda i, chip: (i, 0))
    return _hosted_call(
        body, None, name="trunk", grid=(seq // tm,), n_prefetch=1,
        out_shape=[jax.ShapeDtypeStruct((seq, D_MODEL), F32), jax.ShapeDtypeStruct((seq, D_MODEL), F32),
                   jax.ShapeDtypeStruct((seq, D_MODEL), MXU_DTYPE), jax.ShapeDtypeStruct((seq, D_MODEL), MXU_DTYPE),
                   jax.ShapeDtypeStruct((seq, D_FF), MXU_DTYPE), jax.ShapeDtypeStruct((seq, D_FF), MXU_DTYPE),
                   jax.ShapeDtypeStruct((seq, D_MODEL), MXU_DTYPE), jax.ShapeDtypeStruct((8, D_MODEL), F32)],
        in_specs=[tok(D_MODEL), tok(D_MODEL), tok(D_MODEL), _full((8, D_MODEL))] + [_any()] * 6,
        out_specs=[tok(D_MODEL), tok(D_MODEL), tok(D_MODEL), tok(D_MODEL), tok(D_FF), tok(D_FF), tok(D_MODEL),
                   _full((8, D_MODEL))],
        scratch_shapes=[pltpu.VMEM((D_MODEL, D_MODEL), MXU_DTYPE), pltpu.VMEM((nj, D_MODEL, D_MODEL), MXU_DTYPE),
                        pltpu.VMEM((nj, D_MODEL, D_MODEL), MXU_DTYPE), pltpu.VMEM((nj, tm, D_MODEL), F32),
                        pltpu.SemaphoreType.DMA((3 * N_CHIPS,))],
        semantics=("arbitrary",),
    )(chip_idx, x, target, cat, vecs, *gathered, *local)


def _mixer_bwd_kernel(proj, rope_tab, dcat, w_spatial, w_spatial_t, bias_full, sink_rows, dev_idx, comm=None):
    seq = proj.shape[0]
    per = MIXER_BLOCKS_PER_STEP
    steps = seq // (CHUNK * per)
    kv_col = KV_START // (2 * KV_WIDTH)

    def body(dev_ref, proj_ref, prev_ref, tab_ref, ptab_ref, dcat_ref, w_ref, wt_ref, bias_ref, sink_ref,
             dproj_ref, dw_out, db_ref, dsink_ref, carry, dw_ref):
        del dev_ref
        step = pl.program_id(0)

        @pl.when(step == 0)
        def _():
            carry[...] = jnp.zeros_like(carry)
            dw_ref[...] = jnp.zeros_like(dw_ref)
            db_ref[...] = jnp.zeros_like(db_ref)
            dsink_ref[...] = jnp.zeros_like(dsink_ref)

        for s in reversed(range(per)):
            rows = pl.ds(CHUNK * s, CHUNK)
            if s == 0:
                before, before_tab, first = prev_ref, ptab_ref, step == steps - 1
            else:
                before = proj_ref.at[pl.ds(CHUNK * (s - 1), CHUNK), pl.ds(KV_START, 2 * KV_WIDTH)]
                before_tab, first = tab_ref.at[pl.ds(CHUNK * (s - 1), CHUNK)], None
            one_block(proj_ref.at[rows], before, tab_ref.at[rows], before_tab, dcat_ref.at[rows], w_ref, wt_ref,
                      bias_ref, sink_ref, dproj_ref.at[rows], dw_ref, db_ref, dsink_ref, carry, first)

        @pl.when(step == steps - 1)
        def _():
            dw_out[...] = dw_ref[...].astype(dw_out.dtype)

    def one_block(proj_ref, prev_ref, tab_ref, ptab_ref, dcat_ref, w_ref, wt_ref, bias_ref, sink_ref,
                  dproj_ref, dw_ref, db_ref, dsink_ref, carry, first):
        wm, tril, triu = _masked_spatial(w_ref)
        lo, hi = _lane_masks((CHUNK, LANES))
        lane = lax.broadcasted_iota(jnp.int32, (CHUNK, LANES), 1)
        db = jnp.zeros((CHUNK, LANES), F32)
        for j in range(GMLP_GROUPS // 2):
            cols = slice(LANES * j, LANES * (j + 1))
            vcols = slice(GMLP_WIDTH + LANES * j, GMLP_WIDTH + LANES * (j + 1))
            zu, zv = proj_ref[:, cols], proj_ref[:, vcols]
            u, tu = _gelu_tanh(zu)
            vp, tv = _gelu_tanh(zv)
            sv = _sgu_forward_pair(wm, vp, j) + bias_ref[:, cols]
            dout = dcat_ref[:, cols]
            du = dout * sv
            dsv = dout * u
            dsv_lo, dsv_hi = jnp.where(lo, dsv, 0.0), jnp.where(hi, dsv, 0.0)
            lhs_t = jnp.concatenate([jnp.where(triu, wt_ref[2 * j], 0.0),
                                     jnp.where(triu, wt_ref[2 * j + 1], 0.0)], axis=1)
            dv = _dot(lhs_t, jnp.concatenate([dsv_lo, dsv_hi], axis=0))
            dw_ref[2 * j] += jnp.where(tril, _dot_nt(dsv_lo, vp), 0.0)
            dw_ref[2 * j + 1] += jnp.where(tril, _dot_nt(dsv_hi, vp), 0.0)
            db = db + (jnp.where(lane == 2 * j, jnp.sum(dsv_lo, axis=1, keepdims=True), 0.0)
                       + jnp.where(lane == 2 * j + 1, jnp.sum(dsv_hi, axis=1, keepdims=True), 0.0))
            dproj_ref[:, cols] = (du * _gelu_tanh_grad(zu, tu)).astype(dproj_ref.dtype)
            dproj_ref[:, vcols] = (dv * _gelu_tanh_grad(zv, tv)).astype(dproj_ref.dtype)
        db_ref[...] += db
        o = 2 * GMLP_WIDTH
        tab = tab_ref[...]
        q_r = _rope_apply(proj_ref[:, o:o + ATTN_WIDTH], tab, 1.0)
        k_cur = _rope_apply(proj_ref[:, o + ATTN_WIDTH:o + ATTN_WIDTH + KV_WIDTH], tab, 1.0)
        k_prev = _rope_apply(prev_ref[:, 0:KV_WIDTH], ptab_ref[...], 1.0)
        k_a = jnp.concatenate([k_prev, k_cur], axis=0)
        v_a = jnp.concatenate([prev_ref[:, KV_WIDTH:2 * KV_WIDTH],
                               proj_ref[:, o + ATTN_WIDTH + KV_WIDTH:o + ATTN_WIDTH + 2 * KV_WIDTH]], axis=0)
        k_b = pltpu.roll(k_a, HEAD_DIM, 1)
        v_b = pltpu.roll(v_a, HEAD_DIM, 1)
        bias_t = _attn_bias_t(first)
        lo2, _ = _lane_masks((2 * CHUNK, LANES))
        dout_b = dcat_ref[:, GMLP_WIDTH:GMLP_WIDTH + ATTN_WIDTH]
        dk_tot, dv_tot, dq_pairs = [], [], []
        for g in range(N_KV_HEADS):
            k_dup, v_dup = _group_dup(k_a, k_b, g, lo2), _group_dup(v_a, v_b, g, lo2)
            q_rows = _group_rows(q_r, g, lo, hi)
            do_rows = _group_rows(dout_b, g, lo, hi)
            p_t, p_sink = _attn_probs_t(k_dup, q_rows, bias_t, _sink_row(sink_ref, g))
            dp_t = _dot_nt(v_dup, do_rows)
            delta = jnp.sum(p_t * dp_t, axis=0, keepdims=True)
            ds_t = p_t * (dp_t - delta) * ATTN_SCALE
            dsink = -p_sink * delta
            for r in range(HEADS_PER_GROUP):
                h = HEADS_PER_GROUP * g + r
                dsink_ref[h:h + 1, :] += jnp.broadcast_to(
                    jnp.sum(dsink[:, LANES * r:LANES * (r + 1)], axis=1, keepdims=True), (1, LANES))
            dk_full = _dot(ds_t, q_rows)
            dv_full = _dot(p_t, do_rows)
            dk_tot.append(dk_full + pltpu.roll(dk_full, HEAD_DIM, 1))
            dv_tot.append(dv_full + pltpu.roll(dv_full, HEAD_DIM, 1))
            dq_t = _dot(k_dup.T, ds_t)
            dq_pairs += _pairs_from_rows(dq_t.T, lo)
        dk_all = jnp.where(lo2, dk_tot[0], dk_tot[1])
        dv_all = jnp.where(lo2, dv_tot[0], dv_tot[1])
        dk_cur = dk_all[CHUNK:, :] + carry[:, 0:KV_WIDTH]
        dv_cur = dv_all[CHUNK:, :] + carry[:, KV_WIDTH:2 * KV_WIDTH]
        carry[:, 0:KV_WIDTH] = dk_all[:CHUNK, :]
        carry[:, KV_WIDTH:2 * KV_WIDTH] = dv_all[:CHUNK, :]
        dq = _rope_apply(jnp.concatenate(dq_pairs, axis=1), tab, -1.0)
        dproj_ref[:, o:o + ATTN_WIDTH] = dq.astype(dproj_ref.dtype)
        dproj_ref[:, o + ATTN_WIDTH:o + ATTN_WIDTH + KV_WIDTH] = (
            _rope_apply(dk_cur, tab, -1.0).astype(dproj_ref.dtype))
        dproj_ref[:, o + ATTN_WIDTH + KV_WIDTH:o + ATTN_WIDTH + 2 * KV_WIDTH] = dv_cur.astype(dproj_ref.dtype)

    rev = lambda i: steps - 1 - i
    before = lambda i: jnp.maximum(per * rev(i) - 1, 0)
    slot = lambda shape: pl.BlockSpec((None,) + shape, lambda i, d: (d[0],) + (0,) * len(shape))
    return _hosted_call(
        body, comm, name="mixer_bwd", grid=(steps,), n_prefetch=1,
        out_shape=[jax.ShapeDtypeStruct((seq, IN_PROJ_WIDTH), MXU_DTYPE),
                   jax.ShapeDtypeStruct((N_DEV, GMLP_GROUPS, CHUNK, CHUNK), GRAD_COMM_DTYPE),
                   jax.ShapeDtypeStruct((N_DEV, CHUNK, LANES), F32),
                   jax.ShapeDtypeStruct((N_DEV, N_Q_HEADS, LANES), F32)],
        in_specs=[pl.BlockSpec((CHUNK * per, IN_PROJ_WIDTH), lambda i, d: (rev(i), 0)),
                  pl.BlockSpec((CHUNK, 2 * KV_WIDTH), lambda i, d: (before(i), kv_col)),
                  pl.BlockSpec((CHUNK * per, 3 * LANES), lambda i, d: (rev(i), 0)),
                  pl.BlockSpec((CHUNK, 3 * LANES), lambda i, d: (before(i), 0)),
                  pl.BlockSpec((CHUNK * per, D_MODEL), lambda i, d: (rev(i), 0)),
                  _full((GMLP_GROUPS, CHUNK, CHUNK)), _full((GMLP_GROUPS, CHUNK, CHUNK)),
                  _full((CHUNK, GMLP_WIDTH)), _full((N_Q_HEADS, LANES))],
        out_specs=[pl.BlockSpec((CHUNK * per, IN_PROJ_WIDTH), lambda i, d: (rev(i), 0)),
                   slot((GMLP_GROUPS, CHUNK, CHUNK)), slot((CHUNK, LANES)), slot((N_Q_HEADS, LANES))],
        scratch_shapes=[pltpu.VMEM((CHUNK, 2 * KV_WIDTH), F32), pltpu.VMEM((GMLP_GROUPS, CHUNK, CHUNK), F32)],
        semantics=("arbitrary",),
    )(dev_idx, proj, proj, rope_tab, rope_tab, dcat, w_spatial, w_spatial_t, bias_full, sink_rows)


def _in_proj_bwd_kernel(x, dx1, dproj, vecs, w_in_t, comm=None):
    seq = x.shape[0]
    tm = 512

    def body(x_ref, dx1_ref, dp_ref, v_ref, w_ref, gx_ref, sums_ref):
        @pl.when(pl.program_id(0) == 0)
        def _():
            sums_ref[...] = jnp.zeros_like(sums_ref)

        g_mix, scale1 = v_ref[0:1, :], v_ref[2:3, :]
        dh = _dot(dp_ref[...], w_ref[...])
        xv = x_ref[...]
        rstd = lax.rsqrt(_mean_last(xv * xv) + EPS)
        xh = xv * rstd
        dn1 = dh * (1.0 + scale1)
        dxh = dn1 * g_mix
        gx_ref[...] = dx1_ref[...] + rstd * (dxh - xh * _mean_last(dxh * xh))
        sums_ref[0:1, :] += _rowsum(dh)
        sums_ref[1:2, :] += _rowsum(dh * (xh * g_mix))
        sums_ref[2:3, :] += _rowsum(dn1 * xh)

    tok = lambda w: pl.BlockSpec((tm, w), lambda i: (i, 0))
    return _hosted_call(
        body, comm, name="in_proj_bwd", grid=(seq // tm,),
        out_shape=[jax.ShapeDtypeStruct((seq, D_MODEL), F32), jax.ShapeDtypeStruct((8, D_MODEL), F32)],
        in_specs=[tok(D_MODEL), tok(D_MODEL), tok(IN_PROJ_WIDTH), _full((8, D_MODEL)),
                  _full((IN_PROJ_WIDTH, D_MODEL))],
        out_specs=[tok(D_MODEL), _full((8, D_MODEL))],
        semantics=("arbitrary",),
    )(x, dx1, dproj, vecs, w_in_t)


class _GradTiles(NamedTuple):
    tm: int
    tn: int
    n_tiles: int
    chips_per_tile: int
    a_index: Callable
    b_index: Callable


def _weight_grad_kernel(a, b, c_idx, name, tiles, comm=None):
    seq = a.shape[0]
    tk = min(seq, 4096)
    nk = seq // tk
    tm, tn, n_tiles, per = tiles.tm, tiles.tn, tiles.n_tiles, tiles.chips_per_tile
    rows = tm // per

    def half(phase, c):
        return phase * c[0] + (1 - phase) * (1 - c[0])

    def body(c_ref, a_ref, b_ref, o_ref, acc, stage, landed, send_sems, recv_sems):
        del c_ref
        phase, t, kk = pl.program_id(0), pl.program_id(1), pl.program_id(2)
        x, y, c, _ = _mesh_place()

        def copy(tile):
            return pltpu.make_async_remote_copy(
                src_ref=stage.at[tile], dst_ref=landed.at[tile], send_sem=send_sems.at[tile],
                recv_sem=recv_sems.at[tile], device_id=(x, y, 1 - c), device_id_type=MESH)

        @pl.when(kk == 0)
        def _():
            acc[...] = jnp.zeros_like(acc)

        acc[...] += _dot_tn(a_ref[...], b_ref[...])

        @pl.when((kk == nk - 1) & (phase == 0))
        def _():
            stage[t] = acc[...].astype(stage.dtype)
            copy(t).start()

        @pl.when((kk == nk - 1) & (phase == 1))
        def _():
            copy(t).wait_recv()
            total = acc[...] + landed[t].astype(F32)
            for q in range(per):
                o_ref[q] = total[rows * q:rows * (q + 1)].astype(o_ref.dtype)

        @pl.when((kk == nk - 1) & (phase == 1) & (t == n_tiles - 1))
        def _():
            for tile in range(n_tiles):
                copy(tile).wait_send()

    out = _hosted_call(
        body, comm, name=name, grid=(2, n_tiles, nk), n_prefetch=1,
        out_shape=[jax.ShapeDtypeStruct((n_tiles * per, rows, tn), GRAD_COMM_DTYPE)],
        in_specs=[pl.BlockSpec((tk, tm), lambda p, t, k, c: (k, tiles.a_index(t, half(p, c)))),
                  pl.BlockSpec((tk, tn), lambda p, t, k, c: (k, tiles.b_index(t, half(p, c))))],
        out_specs=[pl.BlockSpec((per, rows, tn), lambda p, t, k, c: (p * t, 0, 0))],
        scratch_shapes=[pltpu.VMEM((tm, tn), F32), pltpu.VMEM((n_tiles, tm, tn), GRAD_COMM_DTYPE),
                        pltpu.VMEM((n_tiles, tm, tn), GRAD_COMM_DTYPE),
                        pltpu.SemaphoreType.DMA((n_tiles,)), pltpu.SemaphoreType.DMA((n_tiles,))],
        semantics=("arbitrary", "arbitrary", "arbitrary"),
    )(c_idx, a, b)
    return out[0] if comm is None else out


def _row_tile(rows, most=256, sublanes=16):
    return max(t for t in range(sublanes, most + 1, sublanes) if rows % t == 0)


def _adam_update(w, g, m, v):
    m_new = ADAM_B1 * m + (1.0 - ADAM_B1) * g
    v_new = ADAM_B2 * v + (1.0 - ADAM_B2) * (g * g)
    m_hat = m_new / (1.0 - ADAM_B1 ** ADAM_STEP)
    v_hat = v_new / (1.0 - ADAM_B2 ** ADAM_STEP)
    delta = -ADAM_LR * (m_hat / (jnp.sqrt(v_hat) + ADAM_EPS) + ADAM_WD * w)
    return delta, m_new, v_new


def _sum_chips_kernel(own, others, place, name):
    _, r, n = own.shape
    tr = _row_tile(r)

    def body(place_ref, own_ref, oth_ref, o_ref):
        del place_ref
        acc = own_ref[...].astype(F32)
        for k in range(N_CHIPS - 1):
            acc = acc + oth_ref[k].astype(F32)
        o_ref[...] = acc

    return pl.pallas_call(
        body, name=name, out_shape=jax.ShapeDtypeStruct((2, r, n), F32),
        grid_spec=pltpu.PrefetchScalarGridSpec(
            num_scalar_prefetch=1, grid=(r // tr,),
            in_specs=[pl.BlockSpec((None, tr, n), lambda i, p: (p[0], i, 0)),
                      pl.BlockSpec((N_CHIPS - 1, tr, n), lambda i, p: (0, i, 0))],
            out_specs=pl.BlockSpec((None, tr, n), lambda i, p: (p[1], i, 0))),
        compiler_params=_params("parallel"),
    )(place, own, others)


def _adam_kernel(w, g, m, v, name, after=()):
    r, n = w.shape
    by_columns = g.shape[1] == r
    tr, tn = _row_tile(g.shape[1], most=512), g.shape[2]

    def body(w_ref, g_ref, m_ref, v_ref, *rest):
        g_out, d_ref, mo_ref, vo_ref = rest[len(after):]
        gv = g_ref[...]
        g_out[...] = gv
        d_ref[...], mo_ref[...], vo_ref[...] = _adam_update(w_ref[...], gv, m_ref[...], v_ref[...])

    steps = g.shape[1] // tr
    spec = pl.BlockSpec((tr, tn), (lambda h, i: (i, h)) if by_columns else (lambda h, i: (h * steps + i, 0)))
    return pl.pallas_call(
        body, name=name, grid=(2, steps), out_shape=[jax.ShapeDtypeStruct((r, n), F32)] * 4,
        in_specs=[spec, pl.BlockSpec((None, tr, tn), lambda h, i: (h, i, 0)), spec, spec] + [_any()] * len(after),
        out_specs=[spec] * 4, compiler_params=_params("parallel", "parallel"),
    )(w, g, m, v, *after)


SMALL_PARAMS = ("b_ada", "g_mix", "g_ffn", "g_final", "b_spatial", "sinks", "w_spatial")


def _small_update_kernel(gathered, params):
    shapes = [params[nm][0].shape for nm in SMALL_PARAMS]

    def body(*refs):
        g_refs, refs = refs[:5], refs[5:]
        p_refs, refs = refs[:3 * len(SMALL_PARAMS)], refs[3 * len(SMALL_PARAMS):]
        loss_ref, o_refs = refs[0], refs[1:]

        def total(ref):
            acc = ref[0].astype(F32)
            for k in range(1, N_DEV):
                acc = acc + ref[k].astype(F32)
            return acc

        s1, s2, db, ds, dw = (total(r) for r in g_refs)
        loss_ref[...] = jnp.broadcast_to(s2[6:7, 0:1], loss_ref.shape)
        grads = {"b_ada": [s1[0:1], s1[1:2], s2[5:6], s2[0:1], s2[1:2], s2[2:3]], "g_mix": [s1[2:3]],
                 "g_ffn": [s2[3:4]], "g_final": [s2[4:5]], "b_spatial": [db.T[0:GMLP_GROUPS]],
                 "w_spatial": [dw]}
        lane = lax.broadcasted_iota(jnp.int32, (1, LANES), 1)
        sink_row = jnp.zeros((1, LANES), F32)
        for h in range(N_Q_HEADS):
            sink_row = sink_row + jnp.where(lane == h, ds[h:h + 1, :], 0.0)
        grads["sinks"] = [sink_row[:, 0:N_Q_HEADS]]
        for i, nm in enumerate(SMALL_PARAMS):
            w_ref, m_ref, v_ref = p_refs[3 * i:3 * i + 3]
            outs = o_refs[4 * i:4 * i + 4]
            width = grads[nm][0].shape[1]
            for k, g in enumerate(grads[nm]):
                cols = slice(width * k, width * (k + 1))
                upd = _adam_update(w_ref[:, cols], g, m_ref[:, cols], v_ref[:, cols])
                for o_ref, val in zip(outs, (g,) + upd):
                    o_ref[:, cols] = val

    flat = [a for nm in SMALL_PARAMS for a in params[nm]]
    out_shape = [jax.ShapeDtypeStruct((8, LANES), F32)]
    out_shape += [jax.ShapeDtypeStruct(s, F32) for s in shapes for _ in range(4)]
    outs = pl.pallas_call(
        body, name="small_update", grid=(1,), out_shape=out_shape,
        in_specs=[_full(g.shape) for g in gathered] + [_full(a.shape) for a in flat],
        out_specs=[_full(s.shape) for s in out_shape],
        compiler_params=_params("arbitrary"),
    )(*gathered, *flat)
    return {nm: outs[1 + 4 * i:5 + 4 * i] for i, nm in enumerate(SMALL_PARAMS)}, outs[0]


def _ada_update_kernel(act_t, dmod, w, m, v):
    r, n = w.shape
    tr = 256

    def body(a_ref, d_ref, w_ref, m_ref, v_ref, g_ref, dl_ref, mo_ref, vo_ref):
        g = _dot(a_ref[...], d_ref[...])
        g_ref[...] = g
        dl_ref[...], mo_ref[...], vo_ref[...] = _adam_update(w_ref[...], g, m_ref[...], v_ref[...])

    spec = pl.BlockSpec((tr, n), lambda i: (i, 0))
    return pl.pallas_call(
        body, name="ada_update", grid=(r // tr,), out_shape=[jax.ShapeDtypeStruct((r, n), F32)] * 4,
        in_specs=[pl.BlockSpec((tr, N_DEV), lambda i: (i, 0)), _full((N_DEV, n)), spec, spec, spec],
        out_specs=[spec] * 4, compiler_params=_params("parallel"),
    )(act_t, dmod, w, m, v)


def kernel(x, c, positions, w_ada, b_ada, g_mix, w_in, w_spatial, b_spatial, sinks, w_out, g_ffn, w_ff1, w_ff2, g_final, loss_target, m_w_ada, m_b_ada, m_g_mix, m_w_in, m_w_spatial, m_b_spatial, m_sinks, m_w_out, m_g_ffn, m_w_ff1, m_w_ff2, m_g_final, v_w_ada, v_b_ada, v_g_mix, v_w_in, v_w_spatial, v_b_spatial, v_sinks, v_w_out, v_g_ffn, v_w_ff1, v_w_ff2, v_g_final):
    xi, yi, ci = lax.axis_index("x"), lax.axis_index("y"), lax.axis_index("c")
    chip = 2 * xi + yi
    dev = 2 * chip + ci
    seq = x.shape[1]
    x2, tgt = x[0], loss_target[0]
    ada_cols = w_ada.shape[2]

    big = {"w_in": tuple(a[0].T for a in (w_in, m_w_in, v_w_in)),
           "w_out": (w_out[0], m_w_out[0], v_w_out[0]), "w_ff1": (w_ff1[0], m_w_ff1[0], v_w_ff1[0]),
           "w_ff2": (w_ff2[0], m_w_ff2[0], v_w_ff2[0])}

    def halves(nm, zero=None):
        r, n = big[nm][0].shape
        w = big[nm][0] if zero is None else big[nm][0] + zero
        return w.astype(WEIGHT_COMM_DTYPE).reshape(2, r // 2, n)

    chip_idx = chip.reshape(1).astype(jnp.int32)
    first_c, first_w = ([c], [False], [True]), ([halves("w_in"), halves("w_out")], [True, True], [True, False])
    c_started, w_started = _first_stage_start([first_c, first_w], "gather_first_start")
    zero = c_started[-1][0, 0]
    trunk_weights = ["w_out", "w_ff1", "w_ff2"]
    shards = [halves("w_out"), halves("w_ff1", zero), halves("w_ff2", zero)]
    inv_freq = ROPE_THETA ** (-jnp.arange(0, ROT_DIM, 2, dtype=F32) / ROT_DIM) + zero
    rope_tab = _rope_lane_tables(*_rope_angle_kernel(positions, inv_freq.reshape(ROT_DIM // 2, 1)))
    c_all, = _first_stage_finish(c_started, *first_c[1:], [rope_tab], "gather_c_wait")
    b_shard = lax.dynamic_slice(b_ada, (0, chip * ada_cols), (1, ada_cols))
    mod_part, act = _mod_kernel(c_all.reshape(N_DEV, D_MODEL), w_ada[0], b_shard)
    w_in_t, g_out, *ff_halves = _first_stage_finish(w_started, *first_w[1:], [mod_part], "gather_first_wait",
                                                    thru=shards[1:])
    shards = shards[:1] + ff_halves
    mod_all, w_in_t, g_out = _all_gather8([mod_part], "gather_mod", forward=[w_in_t, g_out])
    w_in_t = w_in_t.reshape(IN_PROJ_WIDTH, D_MODEL)
    mod_me = lax.dynamic_index_in_dim(mod_all[0::2], dev, axis=1, keepdims=False)
    mod_me = mod_me.reshape(N_MOD, D_MODEL)
    shift1, scale1, gate1, shift2, scale2, gate2 = (mod_me[k:k + 1] for k in range(N_MOD))

    zeros_row = jnp.zeros((1, D_MODEL), F32)
    vecs1 = jnp.concatenate([g_mix, shift1, scale1] + [zeros_row] * 5, axis=0)
    vecs2 = jnp.concatenate([gate1, shift2, scale2, gate2, g_ffn, g_final.reshape(1, D_MODEL)]
                            + [zeros_row] * 2, axis=0)
    bias_full = jnp.repeat(b_spatial[0].T, HEAD_DIM, axis=1)
    sink_rows = jnp.broadcast_to(sinks[0][:, None], (N_Q_HEADS, LANES))

    proj, hb, *staged = _in_proj_kernel(x2, vecs1, w_in_t, comm=_gather2d_first(shards[1:]))
    cat, *staged = _mixer_fwd_kernel(proj, rope_tab, w_spatial[0], bias_full, sink_rows,
                                     comm=_gather2d_second(staged, shards[1:]))
    staged = [g_out] + list(_gather_forward(staged, "gather_forward"))
    dx1, dcat, dmix, h2b, rb, dab, dffb, sums2 = _trunk_kernel(
        x2, tgt, cat, vecs2, chip_idx,
        [g.reshape((N_CHIPS,) + big[nm][0].shape) for nm, g in zip(trunk_weights, staged)],
        [s.reshape(big[nm][0].shape) for nm, s in zip(trunk_weights, shards)])

    c_idx = ci.reshape(1).astype(jnp.int32)
    place = jnp.stack([chip, ci]).astype(jnp.int32)
    half_d = D_MODEL // 2
    cs_ff2 = _weight_grad_kernel(rb, dffb, c_idx, "dw_ff2",
                                 _GradTiles(D_MODEL, half_d, N_CHIPS, 1, lambda t, h: t, lambda t, h: h))
    eighth = D_MODEL // 8
    cs_ff1, sc_ff2 = _weight_grad_kernel(
        h2b, dab, c_idx, "dw_ff1",
        _GradTiles(D_MODEL, half_d, N_CHIPS, 1, lambda t, h: 0, lambda t, h: 2 * t + h),
        comm=_scatter_job([cs_ff2], rows=(0, 6 * eighth)))
    cs_out, sc_ff2 = _weight_grad_kernel(
        cat, dmix, c_idx, "dw_out", _GradTiles(D_MODEL, half_d, 1, N_CHIPS, lambda t, h: 0, lambda t, h: h),
        comm=_scatter_job([cs_ff2], rows=(6 * eighth, eighth), into=[sc_ff2]))
    dproj, dw_spatial, db_lanes, dsink_rows, sc_ff2, sc_ff1, sc_out = _mixer_bwd_kernel(
        proj, rope_tab, dcat, w_spatial[0], w_spatial[0].transpose(0, 2, 1), bias_full, sink_rows,
        dev.reshape(1).astype(jnp.int32),
        comm=_merge_jobs(_scatter_job([cs_ff1, cs_out]),
                         _scatter_job([cs_ff2], rows=(7 * eighth, eighth), into=[sc_ff2])))
    totals = [_sum_chips_kernel(own, oth, place, "grad_sum_" + nm)
              for nm, own, oth in (("w_out", cs_out, sc_out), ("w_ff1", cs_ff1, sc_ff1), ("w_ff2", cs_ff2, sc_ff2))]
    small_slots = [db_lanes, dsink_rows, dw_spatial.reshape(N_DEV, GMLP_GROUPS * CHUNK, CHUNK)]
    cs_in, *rode = _weight_grad_kernel(
        dproj, hb, c_idx, "dw_in",
        _GradTiles(2 * W_IN_BLOCK, half_d, N_CHIPS // 2, 2, lambda t, h: t, lambda t, h: h),
        comm=_merge_jobs(_gather_job(small_slots), _share_job(totals)))
    small_stage1, shared = rode[:len(small_slots)], rode[len(small_slots):]
    scatter_in = _scatter_start(cs_in, "grad_to_chips_w_in_start")
    grad_x, sums1 = _in_proj_bwd_kernel(x2, dx1, dproj, vecs1 + scatter_in[-1][0:1, 0:1], w_in_t)
    cs_in, sc_in = _scatter_wait(scatter_in, sums1, "grad_to_chips_w_in_wait")
    total_in = _sum_chips_kernel(cs_in, sc_in, place, "grad_sum_w_in")
    gather_small = _gather_start([sums1, sums2], small_stage1, [total_in], cs_in, "gather_small_start")
    big_out = {}

    def update(nm, g, after=()):
        w, m, v = big[nm]
        outs = _adam_kernel(w, g, m, v, "adam_" + nm, after=after)
        big_out[nm] = tuple((t.T if nm == "w_in" else t)[None] for t in outs)
        return outs

    update("w_out", shared[0])
    covering = update("w_ff1", shared[1], after=gather_small[-1:])
    update("w_ff2", shared[2])
    *gathered, shared_in = _gather_wait(gather_small, 2, 1, covering[0], "gather_small_wait")
    update("w_in", shared_in)

    small = {"b_ada": (b_ada, m_b_ada, v_b_ada), "g_mix": (g_mix, m_g_mix, v_g_mix),
             "g_ffn": (g_ffn, m_g_ffn, v_g_ffn), "g_final": (g_final, m_g_final, v_g_final),
             "b_spatial": (b_spatial, m_b_spatial, v_b_spatial), "sinks": (sinks, m_sinks, v_sinks),
             "w_spatial": (w_spatial, m_w_spatial, v_w_spatial)}
    flat_shape = {"g_final": (1, D_MODEL), "b_spatial": (GMLP_GROUPS, CHUNK), "w_spatial": (GMLP_GROUPS * CHUNK, CHUNK)}
    small_out, loss_tile = _small_update_kernel(
        gathered, {nm: tuple(a.reshape(flat_shape.get(nm, a.shape)) for a in small[nm]) for nm in small})
    small_out = {nm: [o.reshape(small[nm][0].shape) for o in small_out[nm]] for nm in small}
    loss = loss_tile[0, 0]

    g1, g2 = gathered[0], gathered[1]
    dmod_all = jnp.concatenate([g1[:, 0], g1[:, 1], g2[:, 5], g2[:, 0], g2[:, 1], g2[:, 2]], axis=1)
    dmod_cols = lax.dynamic_slice(dmod_all, (0, chip * ada_cols), (N_DEV, ada_cols))
    ada = _ada_update_kernel(act.T, dmod_cols, w_ada[0], m_w_ada[0], v_w_ada[0])
    big_out["w_ada"] = tuple(t[None] for t in ada)

    order = ["w_ada", "b_ada", "g_mix", "w_in", "w_spatial", "b_spatial", "sinks", "w_out", "g_ffn",
             "w_ff1", "w_ff2", "g_final"]

    def leaf(nm, k):
        return big_out[nm][k] if nm in big_out else small_out[nm][k]

    outs = [loss, grad_x[None]]
    for k in range(4):
        outs += [leaf(nm, k) for nm in order]
    return tuple(outs)
```

```python
import math
from typing import Callable, NamedTuple

import jax
import jax.numpy as jnp
from jax import lax
from jax.experimental import pallas as pl
from jax.experimental.pallas import tpu as pltpu

F32 = jnp.float32
MXU_DTYPE = jnp.bfloat16
WEIGHT_COMM_DTYPE = jnp.bfloat16
GRAD_COMM_DTYPE = jnp.bfloat16

D_MODEL = 1024
D_FF = 4096
HEAD_DIM = 64
GMLP_GROUPS = 8
GMLP_WIDTH = 512
CHUNK = 128
N_Q_HEADS = 8
N_KV_HEADS = 2
ATTN_WIDTH = 512
KV_WIDTH = 128
ROT_DIM = 16
ROPE_THETA = 500000.0
IN_PROJ_WIDTH = 1792
N_MOD = 6
EPS = 1e-5
N_CHIPS = 4
N_DEV = 8
LANES = 128
W_IN_BLOCK = IN_PROJ_WIDTH // N_CHIPS

ADAM_LR = 0.001
ADAM_B1 = 0.9
ADAM_B2 = 0.999
ADAM_EPS = 1e-08
ADAM_WD = 0.01
ADAM_STEP = 10

VMEM_LIMIT_BYTES = 58 * 1024 * 1024
MESH = pl.DeviceIdType.MESH


def _params(*semantics):
    return pltpu.CompilerParams(dimension_semantics=semantics, vmem_limit_bytes=VMEM_LIMIT_BYTES)


def _dot(a, b):
    return jnp.dot(a.astype(MXU_DTYPE), b.astype(MXU_DTYPE), preferred_element_type=F32)


def _dot_nt(a, b):
    return lax.dot_general(a.astype(MXU_DTYPE), b.astype(MXU_DTYPE), (((1,), (1,)), ((), ())),
                           preferred_element_type=F32)


def _dot_tn(a, b):
    return lax.dot_general(a.astype(MXU_DTYPE), b.astype(MXU_DTYPE), (((0,), (0,)), ((), ())),
                           preferred_element_type=F32)


def _full(shape):
    return pl.BlockSpec(shape, lambda *_: (0,) * len(shape))


def _any():
    return pl.BlockSpec(memory_space=pl.ANY)


def _rowsum(v):
    return jnp.sum(v, axis=0, keepdims=True)


def _mean_last(v):
    return jnp.mean(v, axis=-1, keepdims=True)


class _Comm(NamedTuple):
    operands: tuple
    out_shapes: tuple
    n_sems: int
    make: Callable
    in_place: int = 0


def _hosted_call(body, comm, *, name, grid, in_specs, out_shape, out_specs, scratch_shapes=(), semantics,
                 n_prefetch=0):
    if comm is None:
        return pl.pallas_call(
            body, name=name, out_shape=out_shape, compiler_params=_params(*semantics),
            grid_spec=pltpu.PrefetchScalarGridSpec(
                num_scalar_prefetch=n_prefetch, grid=grid, in_specs=in_specs, out_specs=out_specs,
                scratch_shapes=list(scratch_shapes)))
    n_in, n_out, n_scr = len(in_specs), len(out_shape), len(scratch_shapes)
    k_in, k_out = len(comm.operands), len(comm.out_shapes)

    def hosted(*refs):
        prefetched, refs = refs[:n_prefetch], refs[n_prefetch:]
        ins, refs = refs[:n_in], refs[n_in:]
        c_ins, refs = refs[:k_in], refs[k_in:]
        outs, refs = refs[:n_out], refs[n_out:]
        c_outs, refs = refs[:k_out], refs[k_out:]
        scratch, (send_sems, recv_sems) = refs[:n_scr], refs[n_scr:]
        first, last = None, None
        for d, size in enumerate(grid):
            at_start, at_end = pl.program_id(d) == 0, pl.program_id(d) == size - 1
            first = at_start if first is None else first & at_start
            last = at_end if last is None else last & at_end

        @pl.when(first)
        def _():
            for cp in comm.make(c_ins, c_outs, send_sems, recv_sems)[0]:
                cp.start()

        body(*prefetched, *ins, *outs, *scratch)

        @pl.when(last)
        def _():
            for wait in comm.make(c_ins, c_outs, send_sems, recv_sems)[1]:
                wait()

    aliases = {n_prefetch + n_in + i: n_out + i for i in range(comm.in_place)}
    call = pl.pallas_call(
        hosted, name=name, out_shape=list(out_shape) + list(comm.out_shapes),
        compiler_params=_params(*semantics), input_output_aliases=aliases,
        grid_spec=pltpu.PrefetchScalarGridSpec(
            num_scalar_prefetch=n_prefetch, grid=grid, in_specs=list(in_specs) + [_any()] * k_in,
            out_specs=list(out_specs) + [_any()] * k_out,
            scratch_shapes=list(scratch_shapes) + [pltpu.SemaphoreType.DMA((comm.n_sems,)),
                                                    pltpu.SemaphoreType.DMA((comm.n_sems,))]))
    return lambda *args: call(*args, *comm.operands)


class _Shifted:
    def __init__(self, base, offset):
        self.base, self.offset = base, offset

    @property
    def at(self):
        return self

    def __getitem__(self, k):
        return self.base.at[self.offset + k]


def _merge_jobs(*jobs):
    def order(count):
        first = [(j, i) for j, job in enumerate(jobs) for i in range(job.in_place)]
        return first + [(j, i) for j, job in enumerate(jobs) for i in range(job.in_place, count(job))]

    op_order, out_order = order(lambda job: len(job.operands)), order(lambda job: len(job.out_shapes))

    def make(ins, outs, send_sems, recv_sems):
        starts, waits, sem = [], [], 0
        for j, job in enumerate(jobs):
            mine_in = [ins[k] for k, (jj, _) in enumerate(op_order) if jj == j]
            mine_out = [outs[k] for k, (jj, _) in enumerate(out_order) if jj == j]
            s, w = job.make(mine_in, mine_out, _Shifted(send_sems, sem), _Shifted(recv_sems, sem))
            starts, waits, sem = starts + s, waits + w, sem + job.n_sems
        return starts, waits

    return _Comm(tuple(jobs[j].operands[i] for j, i in op_order), tuple(jobs[j].out_shapes[i] for j, i in out_order),
                 sum(job.n_sems for job in jobs), make, in_place=sum(job.in_place for job in jobs))


def _mesh_place():
    x, y, c = lax.axis_index("x"), lax.axis_index("y"), lax.axis_index("c")
    return x, y, c, [(1 - x, y), (x, 1 - y), (1 - x, 1 - y)]


def _gather_job(bufs):
    per = 4

    def make(ins, outs, send_sems, recv_sems):
        del ins
        x, y, c, chips = _mesh_place()
        starts, waits = [], []
        for a, out in enumerate(outs):
            mine = src = out.at[4 * x + 2 * y + c]
            to = [(x, y, 1 - c)] + [(px, py, c) for px, py in chips]
            sends = [pltpu.make_async_remote_copy(
                src_ref=src, dst_ref=mine, send_sem=send_sems.at[per * a + k],
                recv_sem=recv_sems.at[per * a + k], device_id=dev, device_id_type=MESH)
                for k, dev in enumerate(to)]
            recvs = [pltpu.make_async_remote_copy(
                src_ref=src, dst_ref=out.at[4 * px + 2 * py + pc], send_sem=send_sems.at[per * a + k],
                recv_sem=recv_sems.at[per * a + k], device_id=(px, py, pc), device_id_type=MESH)
                for k, (px, py, pc) in enumerate(to)]
            starts += sends
            waits += [s.wait_send for s in sends] + [r.wait_recv for r in recvs]
        return starts, waits

    shapes = tuple(jax.ShapeDtypeStruct(b.shape, b.dtype) for b in bufs)
    return _Comm(tuple(bufs), shapes, per * len(bufs), make, in_place=len(bufs))


def _slots(x, y, c):
    return 4 * x + 2 * y + c, 4 * (1 - x) + 2 * y + c, 4 * x + 2 * (1 - y) + c, 4 * (1 - x) + 2 * (1 - y) + c


def _gather2d_first(halves):
    per = 2

    def make(ins, outs, send_sems, recv_sems):
        x, y, c, _ = _mesh_place()
        me, xn, yn, _ = _slots(x, y, c)
        starts, waits = [], []
        for a, (src, out) in enumerate(zip(ins, outs)):
            blk = src.at[c]
            rows = blk.shape[0] // 2
            upper, lower = pl.ds(0, rows), pl.ds(rows, rows)

            def copy(k, src_ref, dst_ref, dev, a=a):
                return pltpu.make_async_remote_copy(
                    src_ref=src_ref, dst_ref=dst_ref, send_sem=send_sems.at[per * a + k],
                    recv_sem=recv_sems.at[per * a + k], device_id=dev, device_id_type=MESH)

            sends = [copy(0, blk.at[upper], out.at[me, upper], (1 - x, y, c)),
                     copy(1, blk.at[lower], out.at[me, lower], (x, 1 - y, c))]
            recvs = [copy(0, blk.at[upper], out.at[xn, upper], (1 - x, y, c)),
                     copy(1, blk.at[lower], out.at[yn, lower], (x, 1 - y, c))]
            starts += sends
            waits += [s.wait_send for s in sends] + [r.wait_recv for r in recvs]
        return starts, waits

    shapes = tuple(jax.ShapeDtypeStruct((N_DEV,) + h.shape[1:], h.dtype) for h in halves)
    return _Comm(tuple(halves), shapes, per * len(halves), make)


def _gather2d_second(bufs, halves):
    per = 4
    n_arr = len(bufs)

    def make(ins, outs, send_sems, recv_sems):
        x, y, c, _ = _mesh_place()
        me, xn, yn, dg = _slots(x, y, c)
        starts, waits = [], []
        for a, buf in enumerate(outs):
            own = ins[n_arr + a].at[c]
            rows = buf.shape[1] // 2
            upper, lower = pl.ds(0, rows), pl.ds(rows, rows)
            plan = [(own.at[upper], me, upper, (x, 1 - y, c), yn), (buf.at[xn, upper], xn, upper, (x, 1 - y, c), dg),
                    (own.at[lower], me, lower, (1 - x, y, c), xn), (buf.at[yn, lower], yn, lower, (1 - x, y, c), dg)]
            for k, (src, slot, part, dev, landing) in enumerate(plan):
                sems = dict(send_sem=send_sems.at[per * a + k], recv_sem=recv_sems.at[per * a + k],
                            device_id=dev, device_id_type=MESH)
                send = pltpu.make_async_remote_copy(src_ref=src, dst_ref=buf.at[slot, part], **sems)
                arrival = pltpu.make_async_remote_copy(src_ref=src, dst_ref=buf.at[landing, part], **sems)
                starts.append(send)
                waits += [send.wait_send, arrival.wait_recv]
        return starts, waits

    shapes = tuple(jax.ShapeDtypeStruct(b.shape, b.dtype) for b in bufs)
    return _Comm(tuple(bufs) + tuple(halves), shapes, per * n_arr, make, in_place=n_arr)


def _gather_forward(bufs, name):
    n_arr = len(bufs)

    def body(*refs):
        outs = refs[n_arr:2 * n_arr]
        send_sems, recv_sems = refs[2 * n_arr:]
        x, y, c, chips = _mesh_place()
        sends, recvs = [], []
        for a, buf in enumerate(outs):
            for j, (px, py) in enumerate(chips):
                mine, theirs = buf.at[4 * px + 2 * py + c], buf.at[4 * px + 2 * py + 1 - c]
                sems = dict(send_sem=send_sems.at[3 * a + j], recv_sem=recv_sems.at[3 * a + j],
                            device_id=(x, y, 1 - c), device_id_type=MESH)
                sends.append(pltpu.make_async_remote_copy(src_ref=mine, dst_ref=mine, **sems))
                recvs.append(pltpu.make_async_remote_copy(src_ref=mine, dst_ref=theirs, **sems))
        for cp in sends:
            cp.start()
        for s, r in zip(sends, recvs):
            s.wait_send()
            r.wait_recv()

    return pl.pallas_call(
        body, name=name, out_shape=[jax.ShapeDtypeStruct(b.shape, b.dtype) for b in bufs],
        in_specs=[_any()] * n_arr, out_specs=[_any()] * n_arr,
        input_output_aliases={a: a for a in range(n_arr)},
        scratch_shapes=[pltpu.SemaphoreType.DMA((3 * n_arr,)), pltpu.SemaphoreType.DMA((3 * n_arr,))],
    )(*bufs)


def _scatter_job(chip_sums, rows=None, into=()):
    n_into = len(into)

    def part(ref):
        return ref if rows is None else ref.at[pl.ds(rows[0], rows[1])]

    def make(ins, outs, send_sems, recv_sems):
        x, y, c, chips = _mesh_place()
        copies = [pltpu.make_async_remote_copy(
            src_ref=part(src.at[2 * px + py]), dst_ref=part(out.at[j]), send_sem=send_sems.at[3 * a + j],
            recv_sem=recv_sems.at[3 * a + j], device_id=(px, py, c), device_id_type=MESH)
            for a, (src, out) in enumerate(zip(ins[n_into:], outs)) for j, (px, py) in enumerate(chips)]
        return copies, [cp.wait for cp in copies]

    shapes = tuple(jax.ShapeDtypeStruct((3,) + s.shape[1:], s.dtype) for s in chip_sums)
    return _Comm(tuple(into) + tuple(chip_sums), shapes, 3 * len(chip_sums), make, in_place=n_into)


def _all_gather8(blocks, name, split=False, forward=(), riders=(), skip_own=()):
    n_arr, n_fwd = len(blocks), len(forward)
    splits = list(split) if isinstance(split, (list, tuple)) else [split] * n_arr
    own_slots = [a not in skip_own for a in range(n_arr)]
    rider_in = sum(len(r.operands) for r in riders)
    rider_out = sum(len(r.out_shapes) for r in riders)

    def body(*refs):
        x_refs, refs = refs[:n_arr], refs[n_arr + n_fwd:]
        r_ins, refs = refs[:rider_in], refs[rider_in:]
        out_refs, refs = refs[:n_arr], refs[n_arr:]
        fwd_refs, refs = refs[:n_fwd], refs[n_fwd:]
        r_outs, refs = refs[:rider_out], refs[rider_out:]
        (send_sems, recv_sems, local_sems), rider_sems = refs[:3], refs[3:]
        x, y, c, chips = _mesh_place()
        me, sibling = (x, y, c), (x, y, 1 - c)
        passing = []
        for f, buf in enumerate(fwd_refs):
            for j, (px, py) in enumerate(chips):
                mine, theirs = buf.at[4 * px + 2 * py + c], buf.at[4 * px + 2 * py + 1 - c]
                sems = dict(send_sem=send_sems.at[7 * n_arr + 3 * f + j], recv_sem=recv_sems.at[7 * n_arr + 3 * f + j],
                            device_id=sibling, device_id_type=MESH)
                passing.append((pltpu.make_async_remote_copy(src_ref=mine, dst_ref=mine, **sems),
                                pltpu.make_async_remote_copy(src_ref=mine, dst_ref=theirs, **sems)))
        for send, _ in passing:
            send.start()
        arrays = []
        for a, (x_ref, out_ref) in enumerate(zip(x_refs, out_refs)):
            src_mine = x_ref.at[c] if splits[a] else x_ref

            def copy(k, blk, to, src=None, a=a, out_ref=out_ref):
                dst = out_ref.at[4 * blk[0] + 2 * blk[1] + blk[2]]
                return pltpu.make_async_remote_copy(
                    src_ref=dst if src is None else src, dst_ref=dst,
                    send_sem=send_sems.at[7 * a + k], recv_sem=recv_sems.at[7 * a + k],
                    device_id=to, device_id_type=MESH)

            mine = pltpu.make_async_copy(src_mine, out_ref.at[4 * x + 2 * y + c], local_sems.at[a])
            first = [copy(0, me, sibling, src=src_mine)] if own_slots[a] else []
            first += [copy(1 + j, me, (*chip, c), src=src_mine) for j, chip in enumerate(chips)]
            for cp in first + ([mine] if own_slots[a] else []):
                cp.start()
            arrays.append((copy, mine, first, own_slots[a]))
        rider_waits, i0, o0 = [], 0, 0
        for n, job in enumerate(riders):
            k_in, k_out = len(job.operands), len(job.out_shapes)
            starts, waits = job.make(r_ins[i0:i0 + k_in], r_outs[o0:o0 + k_out],
                                     rider_sems[2 * n], rider_sems[2 * n + 1])
            for cp in starts:
                cp.start()
            rider_waits += waits
            i0, o0 = i0 + k_in, o0 + k_out
        sent = []
        for copy, mine, first, own in arrays:
            passed = [copy(4 + j, (*chip, c), sibling) for j, chip in enumerate(chips)]
            for j, chip in enumerate(chips):
                copy(1 + j, (*chip, c), me).wait_recv()
                passed[j].start()
            sent += first + passed
        for copy, mine, first, own in arrays:
            if own:
                copy(0, sibling, me).wait_recv()
                mine.wait()
            for j, chip in enumerate(chips):
                copy(4 + j, (*chip, 1 - c), me).wait_recv()
        for cp in sent:
            cp.wait_send()
        for send, arrival in passing:
            send.wait_send()
            arrival.wait_recv()
        for wait in rider_waits:
            wait()

    n_sems = 7 * n_arr + 3 * n_fwd
    rider_operands = [a for r in riders for a in r.operands]
    rider_shapes = [s for r in riders for s in r.out_shapes]
    return pl.pallas_call(
        body, name=name,
        out_shape=[jax.ShapeDtypeStruct((N_DEV,) + tuple(b.shape[1:] if s else b.shape), b.dtype)
                   for b, s in zip(blocks, splits)]
        + [jax.ShapeDtypeStruct(f.shape, f.dtype) for f in forward] + rider_shapes,
        in_specs=[_any()] * (n_arr + n_fwd + rider_in), out_specs=[_any()] * (n_arr + n_fwd + rider_out),
        input_output_aliases={n_arr + f: n_arr + f for f in range(n_fwd)},
        scratch_shapes=[pltpu.SemaphoreType.DMA((n_sems,)), pltpu.SemaphoreType.DMA((n_sems,)),
                        pltpu.SemaphoreType.DMA((n_arr,))]
        + [pltpu.SemaphoreType.DMA((r.n_sems,)) for r in riders for _ in range(2)],
    )(*blocks, *forward, *rider_operands)


def _first_stage_copies(srcs, lands, splits, owns, send_sems, recv_sems, local_sems):
    x, y, c, chips = _mesh_place()
    same_core = [(px, py, c) for px, py in chips]
    everyone = [(x, y, 1 - c)] + [(px, py, pc) for px, py in chips for pc in (c, 1 - c)]
    per_array, local, k = [], [], 0
    for a, (src, land, split, own) in enumerate(zip(srcs, lands, splits, owns)):
        mine = src.at[c] if split else src
        peers = (([(x, y, 1 - c)] if own else []) + same_core) if split else everyone
        pairs = []
        for px, py, pc in peers:
            sems = dict(send_sem=send_sems.at[k], recv_sem=recv_sems.at[k],
                        device_id=(px, py, pc), device_id_type=MESH)
            pairs.append((pltpu.make_async_remote_copy(src_ref=mine, dst_ref=land.at[4 * x + 2 * y + c], **sems),
                          pltpu.make_async_remote_copy(src_ref=mine, dst_ref=land.at[4 * px + 2 * py + pc], **sems)))
            k += 1
        per_array.append(pairs)
        if own:
            local.append(pltpu.make_async_copy(mine, land.at[4 * x + 2 * y + c], local_sems.at[a]))
    return per_array, local


def _first_stage_start(groups, name):
    n_groups = len(groups)
    hbm, sem = pl.BlockSpec(memory_space=pltpu.HBM), pl.BlockSpec(memory_space=pltpu.SEMAPHORE)
    operands, sem_shapes, offsets = [], [], []
    for blocks, splits, owns in groups:
        n_sems = sum((4 if own else 3) if split else N_DEV - 1 for split, own in zip(splits, owns))
        sem_shapes += [pltpu.SemaphoreType.DMA((n_sems,)), pltpu.SemaphoreType.DMA((n_sems,)),
                       pltpu.SemaphoreType.DMA((len(blocks),))]
        offsets.append(len(operands))
        operands += list(blocks) + [lax.empty((N_DEV,) + tuple(b.shape[1:] if split else b.shape), b.dtype)
                                    for b, split in zip(blocks, splits)]
    n_ops = len(operands)

    def body(*refs):
        token = refs[-1]
        for g, (blocks, splits, owns) in enumerate(groups):
            n, at = len(blocks), offsets[g]
            per_array, local = _first_stage_copies(refs[at:at + n], refs[at + n:at + 2 * n], splits, owns,
                                                   *refs[n_ops + 3 * g:n_ops + 3 * g + 3])
            for pairs in per_array:
                for send, _ in pairs:
                    send.start()
            for cp in local:
                cp.start()
        token[...] = jnp.zeros_like(token)

    operands = [pltpu.with_memory_space_constraint(a, pltpu.HBM) for a in operands]
    out = pl.pallas_call(
        body, name=name,
        out_shape=sem_shapes + [pltpu.HBM(a.shape, a.dtype) for a in operands]
        + [jax.ShapeDtypeStruct((8, LANES), F32)],
        in_specs=[hbm] * n_ops,
        out_specs=[sem] * (3 * n_groups) + [hbm] * n_ops + [pl.BlockSpec(memory_space=pltpu.VMEM)],
        input_output_aliases={i: 3 * n_groups + i for i in range(n_ops)},
        compiler_params=pltpu.CompilerParams(has_side_effects=pltpu.SideEffectType.DATAFLOW_SIDE_EFFECTING),
    )(*operands)
    thru = out[3 * n_groups:-1]
    return [list(out[3 * g:3 * g + 3]) + list(thru[offsets[g]:offsets[g] + 2 * len(groups[g][0])]) + [out[-1]]
            for g in range(n_groups)]


def _first_stage_finish(started, splits, owns, after, name, thru=()):
    send_sems, recv_sems, local_sems, *bufs, _ = started
    n = len(bufs) // 2
    hbm, sem = pl.BlockSpec(memory_space=pltpu.HBM), pl.BlockSpec(memory_space=pltpu.SEMAPHORE)

    def body(*refs):
        srcs, lands = refs[:n], refs[n:2 * n]
        send_sems, recv_sems, local_sems = refs[2 * n:2 * n + 3]
        per_array, local = _first_stage_copies(srcs, lands, splits, owns, send_sems, recv_sems, local_sems)
        for pairs in per_array:
            for send, arrival in pairs:
                send.wait_send()
                arrival.wait_recv()
        for cp in local:
            cp.wait()

    first_thru = 2 * n + 3 + len(after)
    out = pl.pallas_call(
        body, name=name,
        out_shape=[pltpu.HBM(b.shape, b.dtype) for b in bufs] + [jax.ShapeDtypeStruct(t.shape, t.dtype) for t in thru],
        in_specs=[hbm] * (2 * n) + [sem, sem, sem] + [_any()] * (len(after) + len(thru)),
        out_specs=[hbm] * (2 * n) + [_any()] * len(thru),
        input_output_aliases={**{i: i for i in range(2 * n)}, **{first_thru + j: 2 * n + j for j in range(len(thru))}},
        compiler_params=pltpu.CompilerParams(has_side_effects=pltpu.SideEffectType.DATAFLOW_SIDE_EFFECTING),
    )(*bufs, send_sems, recv_sems, local_sems, *after, *thru)
    return out[n:]


def _split_scatter_copies(src, land, send_sems, recv_sems):
    x, y, c, chips = _mesh_place()
    return [pltpu.make_async_remote_copy(
        src_ref=src.at[2 * px + py], dst_ref=land.at[j], send_sem=send_sems.at[j], recv_sem=recv_sems.at[j],
        device_id=(px, py, c), device_id_type=MESH) for j, (px, py) in enumerate(chips)]


def _scatter_start(chip_sums, name):
    hbm, sem = pl.BlockSpec(memory_space=pltpu.HBM), pl.BlockSpec(memory_space=pltpu.SEMAPHORE)

    def body(src, land, send_sems, recv_sems, src_thru, land_thru, token):
        del src_thru, land_thru
        for cp in _split_scatter_copies(src, land, send_sems, recv_sems):
            cp.start()
        token[...] = jnp.zeros_like(token)

    land = lax.empty((N_CHIPS - 1,) + chip_sums.shape[1:], chip_sums.dtype)
    operands = [pltpu.with_memory_space_constraint(a, pltpu.HBM) for a in (chip_sums, land)]
    return pl.pallas_call(
        body, name=name,
        out_shape=[pltpu.SemaphoreType.DMA((N_CHIPS - 1,)), pltpu.SemaphoreType.DMA((N_CHIPS - 1,))]
        + [pltpu.HBM(a.shape, a.dtype) for a in operands] + [jax.ShapeDtypeStruct((8, LANES), F32)],
        in_specs=[hbm, hbm], out_specs=[sem, sem, hbm, hbm, pl.BlockSpec(memory_space=pltpu.VMEM)],
        input_output_aliases={0: 2, 1: 3},
        compiler_params=pltpu.CompilerParams(has_side_effects=pltpu.SideEffectType.DATAFLOW_SIDE_EFFECTING),
    )(*operands)


def _scatter_wait(started, after, name):
    send_sems, recv_sems, src, land, _ = started
    hbm, sem = pl.BlockSpec(memory_space=pltpu.HBM), pl.BlockSpec(memory_space=pltpu.SEMAPHORE)

    def body(src, land, send_sems, recv_sems, after_ref, src_thru, land_thru):
        del after_ref, src_thru, land_thru
        for cp in _split_scatter_copies(src, land, send_sems, recv_sems):
            cp.wait()

    return pl.pallas_call(
        body, name=name, out_shape=[pltpu.HBM(src.shape, src.dtype), pltpu.HBM(land.shape, land.dtype)],
        in_specs=[hbm, hbm, sem, sem, _any()], out_specs=[hbm, hbm], input_output_aliases={0: 0, 1: 1},
        compiler_params=pltpu.CompilerParams(has_side_effects=pltpu.SideEffectType.DATAFLOW_SIDE_EFFECTING),
    )(src, land, send_sems, recv_sems, after)


def _split_gather_copies(srcs, lands, fwds, shares, send_sems, recv_sems):
    x, y, c, chips = _mesh_place()
    peers = [(x, y, 1 - c)] + [(px, py, pc) for px, py in chips for pc in (c, 1 - c)]
    pairs = []
    for a, (src, land) in enumerate(zip(srcs, lands)):
        for k, (px, py, pc) in enumerate(peers):
            sems = dict(send_sem=send_sems.at[7 * a + k], recv_sem=recv_sems.at[7 * a + k],
                        device_id=(px, py, pc), device_id_type=MESH)
            pairs.append((pltpu.make_async_remote_copy(src_ref=src, dst_ref=land.at[4 * x + 2 * y + c], **sems),
                          pltpu.make_async_remote_copy(src_ref=src, dst_ref=land.at[4 * px + 2 * py + pc], **sems)))
    for f, buf in enumerate(fwds):
        for j, (px, py) in enumerate(chips):
            mine, theirs = buf.at[4 * px + 2 * py + c], buf.at[4 * px + 2 * py + 1 - c]
            k = 7 * len(srcs) + 3 * f + j
            sems = dict(send_sem=send_sems.at[k], recv_sem=recv_sems.at[k],
                        device_id=(x, y, 1 - c), device_id_type=MESH)
            pairs.append((pltpu.make_async_remote_copy(src_ref=mine, dst_ref=mine, **sems),
                          pltpu.make_async_remote_copy(src_ref=mine, dst_ref=theirs, **sems)))
    for s, buf in enumerate(shares):
        k = 7 * len(srcs) + 3 * len(fwds) + s
        sems = dict(send_sem=send_sems.at[k], recv_sem=recv_sems.at[k], device_id=(x, y, 1 - c), device_id_type=MESH)
        pairs.append((pltpu.make_async_remote_copy(src_ref=buf.at[c], dst_ref=buf.at[c], **sems),
                      pltpu.make_async_remote_copy(src_ref=buf.at[c], dst_ref=buf.at[1 - c], **sems)))
    return pairs


def _gather_start(blocks, forward, shares, after, name):
    n, n_fwd = len(blocks), len(forward)
    n_sems = 7 * n + 3 * n_fwd + len(shares)
    n_bufs = 2 * n + n_fwd + len(shares)
    hbm, sem = pl.BlockSpec(memory_space=pltpu.HBM), pl.BlockSpec(memory_space=pltpu.SEMAPHORE)

    def body(*refs):
        srcs, lands, fwds, swaps = refs[:n], refs[n:2 * n], refs[2 * n:2 * n + n_fwd], refs[2 * n + n_fwd:n_bufs]
        send_sems, recv_sems = refs[n_bufs + 1:n_bufs + 3]
        token, local_sems = refs[-2:]
        x, y, c, _ = _mesh_place()
        own = [pltpu.make_async_copy(src, land.at[4 * x + 2 * y + c], local_sems.at[a])
               for a, (src, land) in enumerate(zip(srcs, lands))]
        for cp in own:
            cp.start()
        for send, _ in _split_gather_copies(srcs, lands, fwds, swaps, send_sems, recv_sems):
            send.start()
        token[...] = jnp.zeros_like(token)
        for cp in own:
            cp.wait()

    lands = [lax.empty((N_DEV,) + b.shape, b.dtype) for b in blocks]
    operands = [pltpu.with_memory_space_constraint(a, pltpu.HBM)
                for a in list(blocks) + lands + list(forward) + list(shares)]
    return pl.pallas_call(
        body, name=name,
        out_shape=[pltpu.SemaphoreType.DMA((n_sems,)), pltpu.SemaphoreType.DMA((n_sems,))]
        + [pltpu.HBM(a.shape, a.dtype) for a in operands] + [jax.ShapeDtypeStruct((8, LANES), F32)],
        in_specs=[hbm] * len(operands) + [_any()],
        out_specs=[sem, sem] + [hbm] * len(operands) + [pl.BlockSpec(memory_space=pltpu.VMEM)],
        input_output_aliases={i: 2 + i for i in range(len(operands))},
        scratch_shapes=[pltpu.SemaphoreType.DMA((n,))],
        compiler_params=pltpu.CompilerParams(has_side_effects=pltpu.SideEffectType.DATAFLOW_SIDE_EFFECTING),
    )(*operands, after)


def _gather_wait(started, n, n_shares, after, name):
    send_sems, recv_sems, *bufs, _ = started
    n_bufs = len(bufs)
    n_fwd = n_bufs - 2 * n - n_shares
    hbm, sem = pl.BlockSpec(memory_space=pltpu.HBM), pl.BlockSpec(memory_space=pltpu.SEMAPHORE)

    def body(*refs):
        srcs, lands, fwds, swaps = refs[:n], refs[n:2 * n], refs[2 * n:2 * n + n_fwd], refs[2 * n + n_fwd:n_bufs]
        send_sems, recv_sems = refs[n_bufs:n_bufs + 2]
        for send, arrival in _split_gather_copies(srcs, lands, fwds, swaps, send_sems, recv_sems):
            send.wait_send()
            arrival.wait_recv()

    out = pl.pallas_call(
        body, name=name, out_shape=[pltpu.HBM(b.shape, b.dtype) for b in bufs],
        in_specs=[hbm] * len(bufs) + [sem, sem, _any()], out_specs=[hbm] * len(bufs),
        input_output_aliases={i: i for i in range(len(bufs))},
        compiler_params=pltpu.CompilerParams(has_side_effects=pltpu.SideEffectType.DATAFLOW_SIDE_EFFECTING),
    )(*bufs, send_sems, recv_sems, after)
    return out[n:]


def _share_job(bufs):
    def make(ins, outs, send_sems, recv_sems):
        del ins
        x, y, c, _ = _mesh_place()
        sems = lambda a: dict(send_sem=send_sems.at[a], recv_sem=recv_sems.at[a],
                              device_id=(x, y, 1 - c), device_id_type=MESH)
        sends = [pltpu.make_async_remote_copy(src_ref=o.at[c], dst_ref=o.at[c], **sems(a)) for a, o in enumerate(outs)]
        arrivals = [pltpu.make_async_remote_copy(src_ref=o.at[c], dst_ref=o.at[1 - c], **sems(a))
                    for a, o in enumerate(outs)]
        return sends, [s.wait_send for s in sends] + [r.wait_recv for r in arrivals]

    shapes = tuple(jax.ShapeDtypeStruct(b.shape, b.dtype) for b in bufs)
    return _Comm(tuple(bufs), shapes, len(bufs), make, in_place=len(bufs))


def _gelu_tanh(z):
    k = math.sqrt(2.0 / math.pi)
    t = jnp.tanh(k * (z + 0.044715 * (z * z * z)))
    return 0.5 * z * (1.0 + t), t


def _gelu_tanh_grad(z, t):
    k = math.sqrt(2.0 / math.pi)
    return 0.5 * (1.0 + t) + 0.5 * z * (1.0 - t * t) * (k * (1.0 + 3.0 * 0.044715 * (z * z)))


def _rope_angle_kernel(pos_row, invf_col):
    seq = pos_row.shape[1]

    def body(p_ref, f_ref, cos_ref, sin_ref):
        ang = p_ref[...].astype(F32) * f_ref[...]
        cos_ref[...] = jnp.cos(ang)
        sin_ref[...] = jnp.sin(ang)

    return pl.pallas_call(
        body, name="rope_angles", grid=(1,), out_shape=[jax.ShapeDtypeStruct((ROT_DIM // 2, seq), F32)] * 2,
        in_specs=[_full((1, seq)), _full((ROT_DIM // 2, 1))], out_specs=[_full((ROT_DIM // 2, seq))] * 2,
        compiler_params=_params("arbitrary"),
    )(pos_row, invf_col)


def _rope_lane_tables(cos, sin):
    cos_t, sin_t = cos.T, sin.T
    seq, half = cos_t.shape
    ones = jnp.ones((seq, HEAD_DIM - ROT_DIM), F32)
    c64 = jnp.concatenate([cos_t, cos_t, ones], axis=1)
    s1 = jnp.concatenate([sin_t, jnp.zeros((seq, HEAD_DIM - half), F32)], axis=1)
    s2 = jnp.concatenate([jnp.zeros((seq, half), F32), sin_t, jnp.zeros((seq, HEAD_DIM - ROT_DIM), F32)], axis=1)
    return jnp.concatenate([jnp.tile(t, (1, LANES // HEAD_DIM)) for t in (c64, s1, s2)], axis=1)


def _rope_apply(t, tab, sign):
    reps = t.shape[1] // LANES
    c_tab, s1, s2 = (jnp.tile(tab[:, LANES * k:LANES * (k + 1)], (1, reps)) if reps > 1
                     else tab[:, LANES * k:LANES * (k + 1)] for k in range(3))
    half = ROT_DIM // 2
    up = pltpu.roll(t, t.shape[1] - half, 1)
    down = pltpu.roll(t, half, 1)
    return t * c_tab + sign * (down * s2 - up * s1)


def _lane_masks(shape):
    lane = lax.broadcasted_iota(jnp.int32, shape, 1)
    return lane < HEAD_DIM, lane >= HEAD_DIM


HEADS_PER_GROUP = N_Q_HEADS // N_KV_HEADS
ATTN_SCALE = 1.0 / math.sqrt(HEAD_DIM)


def _attn_bias_t(first_block):
    kj = lax.broadcasted_iota(jnp.int32, (2 * CHUNK, CHUNK), 0)
    qi = lax.broadcasted_iota(jnp.int32, (2 * CHUNK, CHUNK), 1)
    ok = (kj > qi) & (kj <= qi + CHUNK)
    if first_block is not None:
        ok = ok & (jnp.logical_not(first_block) | (kj >= CHUNK))
    return jnp.tile(jnp.where(ok, 0.0, -jnp.inf), (1, HEADS_PER_GROUP))


def _group_rows(x, g, lo, hi):
    rows = []
    for r in range(HEADS_PER_GROUP):
        h = HEADS_PER_GROUP * g + r
        pair = x[:, LANES * (h // 2):LANES * (h // 2 + 1)]
        rows.append(jnp.where(hi if h % 2 else lo, pair, 0.0))
    return jnp.concatenate(rows, axis=0)


def _pairs_from_rows(rows, lo):
    return [jnp.where(lo, rows[2 * CHUNK * k:2 * CHUNK * k + CHUNK], rows[2 * CHUNK * k + CHUNK:2 * CHUNK * (k + 1)])
            for k in range(HEADS_PER_GROUP // 2)]


def _group_dup(a, b, g, lo2):
    return jnp.where(lo2, a, b) if g == 0 else jnp.where(lo2, b, a)


def _sink_row(sink_ref, g):
    return jnp.concatenate([sink_ref[HEADS_PER_GROUP * g + r:HEADS_PER_GROUP * g + r + 1, :]
                            for r in range(HEADS_PER_GROUP)], axis=1)


def _attn_probs_t(k_dup, q_rows, bias_t, sink_row):
    s_t = _dot_nt(k_dup, q_rows) * ATTN_SCALE + bias_t
    m = jnp.maximum(jnp.max(s_t, axis=0, keepdims=True), sink_row)
    p = jnp.exp(s_t - m)
    e_sink = jnp.exp(sink_row - m)
    inv = 1.0 / (jnp.sum(p, axis=0, keepdims=True) + e_sink)
    return p * inv, e_sink * inv


def _sgu_forward_pair(wm, vp, j):
    lo, hi = _lane_masks(vp.shape)
    lhs = jnp.concatenate([wm[2 * j], wm[2 * j + 1]], axis=1)
    rhs = jnp.concatenate([jnp.where(lo, vp, 0.0), jnp.where(hi, vp, 0.0)], axis=0)
    return _dot(lhs, rhs)


def _masked_spatial(w_ref):
    t = lax.broadcasted_iota(jnp.int32, (CHUNK, CHUNK), 0)
    s = lax.broadcasted_iota(jnp.int32, (CHUNK, CHUNK), 1)
    tril = s <= t
    return [jnp.where(tril, w_ref[g], 0.0) for g in range(GMLP_GROUPS)], tril, s >= t


def _mod_kernel(c_all, w_shard, b_shard, comm=None):
    n = w_shard.shape[1]
    tn = 512

    def body(c_ref, w_ref, b_ref, mod_ref, act_ref):
        cv = c_ref[...]
        act = cv * (1.0 / (1.0 + jnp.exp(-cv)))
        act_ref[...] = act
        mod_ref[...] = _dot(act, w_ref[...]) + b_ref[...]

    return _hosted_call(
        body, comm, name="ada_mod", grid=(n // tn,),
        out_shape=[jax.ShapeDtypeStruct((N_DEV, n), F32), jax.ShapeDtypeStruct((N_DEV, D_MODEL), F32)],
        in_specs=[_full((N_DEV, D_MODEL)), pl.BlockSpec((D_MODEL, tn), lambda i: (0, i)),
                  pl.BlockSpec((1, tn), lambda i: (0, i))],
        out_specs=[pl.BlockSpec((N_DEV, tn), lambda i: (0, i)), _full((N_DEV, D_MODEL))],
        semantics=("arbitrary",),
    )(c_all, w_shard, b_shard)


def _load_chip_blocks(chip_ref, gathered, local, dsts, sems, first_sem=0):
    for k, dst in enumerate(dsts):
        @pl.when(chip_ref[0] == k)
        def _():
            pltpu.make_async_copy(local, dst, sems.at[first_sem + k]).start(priority=k % 2)

        @pl.when(chip_ref[0] != k)
        def _():
            pltpu.make_async_copy(gathered.at[k], dst, sems.at[first_sem + k]).start(priority=k % 2)
    return [pltpu.make_async_copy(local, dst, sems.at[first_sem + k]).wait for k, dst in enumerate(dsts)]


def _in_proj_kernel(x, vecs, w_in_t, comm=None):
    seq = x.shape[0]
    tm = 512

    def body(x_ref, v_ref, w_ref, proj_ref, h_ref):
        xv = x_ref[...]
        rstd = lax.rsqrt(_mean_last(xv * xv) + EPS)
        n1 = (xv * rstd) * v_ref[0:1, :]
        h = n1 * (1.0 + v_ref[2:3, :]) + v_ref[1:2, :]
        hb = h.astype(MXU_DTYPE)
        h_ref[...] = hb
        proj_ref[...] = _dot_nt(hb, w_ref[...])

    return _hosted_call(
        body, comm, name="in_proj", grid=(seq // tm,),
        out_shape=[jax.ShapeDtypeStruct((seq, IN_PROJ_WIDTH), F32),
                   jax.ShapeDtypeStruct((seq, D_MODEL), MXU_DTYPE)],
        in_specs=[pl.BlockSpec((tm, D_MODEL), lambda i: (i, 0)), _full((8, D_MODEL)),
                  _full((IN_PROJ_WIDTH, D_MODEL))],
        out_specs=[pl.BlockSpec((tm, IN_PROJ_WIDTH), lambda i: (i, 0)),
                   pl.BlockSpec((tm, D_MODEL), lambda i: (i, 0))],
        semantics=("arbitrary",),
    )(x, vecs, w_in_t)


MIXER_BLOCKS_PER_STEP = 4
KV_START = 2 * GMLP_WIDTH + ATTN_WIDTH


def _mixer_fwd_kernel(proj, rope_tab, w_spatial, bias_full, sink_rows, comm=None):
    seq = proj.shape[0]
    per = MIXER_BLOCKS_PER_STEP
    steps = seq // (CHUNK * per)
    kv_col = KV_START // (2 * KV_WIDTH)

    def body(proj_ref, prev_ref, tab_ref, ptab_ref, w_ref, bias_ref, sink_ref, cat_ref):
        i = pl.program_id(0)
        wm, _, _ = _masked_spatial(w_ref)
        lo, hi = _lane_masks((CHUNK, LANES))
        lo2, _ = _lane_masks((2 * CHUNK, LANES))
        o = 2 * GMLP_WIDTH
        for s in range(per):
            rows, before = slice(CHUNK * s, CHUNK * (s + 1)), slice(CHUNK * (s - 1), CHUNK * s)
            for j in range(GMLP_GROUPS // 2):
                cols = slice(LANES * j, LANES * (j + 1))
                vcols = slice(GMLP_WIDTH + LANES * j, GMLP_WIDTH + LANES * (j + 1))
                u, _ = _gelu_tanh(proj_ref[rows, cols])
                vp, _ = _gelu_tanh(proj_ref[rows, vcols])
                sv = _sgu_forward_pair(wm, vp, j) + bias_ref[:, cols]
                cat_ref[rows, cols] = (u * sv).astype(cat_ref.dtype)
            tab = tab_ref[rows, :]
            if s == 0:
                prev_kv, prev_tab, first = prev_ref[...], ptab_ref[...], i == 0
            else:
                prev_kv, prev_tab, first = proj_ref[before, KV_START:KV_START + 2 * KV_WIDTH], tab_ref[before, :], None
            q_r = _rope_apply(proj_ref[rows, o:o + ATTN_WIDTH], tab, 1.0)
            k_cur = _rope_apply(proj_ref[rows, KV_START:KV_START + KV_WIDTH], tab, 1.0)
            k_prev = _rope_apply(prev_kv[:, 0:KV_WIDTH], prev_tab, 1.0)
            k_a = jnp.concatenate([k_prev, k_cur], axis=0)
            v_a = jnp.concatenate([prev_kv[:, KV_WIDTH:2 * KV_WIDTH],
                                   proj_ref[rows, KV_START + KV_WIDTH:KV_START + 2 * KV_WIDTH]], axis=0)
            k_b = pltpu.roll(k_a, HEAD_DIM, 1)
            v_b = pltpu.roll(v_a, HEAD_DIM, 1)
            bias_t = _attn_bias_t(first)
            for g in range(N_KV_HEADS):
                p_t, _ = _attn_probs_t(_group_dup(k_a, k_b, g, lo2), _group_rows(q_r, g, lo, hi), bias_t,
                                       _sink_row(sink_ref, g))
                o_t = _dot(_group_dup(v_a, v_b, g, lo2).T, p_t)
                for k, pair in enumerate(_pairs_from_rows(o_t.T, lo)):
                    c0 = GMLP_WIDTH + LANES * (2 * g + k)
                    cat_ref[rows, c0:c0 + LANES] = pair.astype(cat_ref.dtype)

    return _hosted_call(
        body, comm, name="mixer_fwd", grid=(steps,),
        out_shape=[jax.ShapeDtypeStruct((seq, D_MODEL), MXU_DTYPE)],
        in_specs=[pl.BlockSpec((CHUNK * per, IN_PROJ_WIDTH), lambda i: (i, 0)),
                  pl.BlockSpec((CHUNK, 2 * KV_WIDTH), lambda i: (jnp.maximum(per * i - 1, 0), kv_col)),
                  pl.BlockSpec((CHUNK * per, 3 * LANES), lambda i: (i, 0)),
                  pl.BlockSpec((CHUNK, 3 * LANES), lambda i: (jnp.maximum(per * i - 1, 0), 0)),
                  _full((GMLP_GROUPS, CHUNK, CHUNK)), _full((CHUNK, GMLP_WIDTH)),
                  _full((N_Q_HEADS, LANES))],
        out_specs=[pl.BlockSpec((CHUNK * per, D_MODEL), lambda i: (i, 0))],
        semantics=("arbitrary",),
    )(proj, proj, rope_tab, rope_tab, w_spatial, bias_full, sink_rows)


def _trunk_kernel(x, target, cat, vecs, chip_idx, gathered, local):
    seq = x.shape[0]
    tm = 256
    nj = D_FF // D_MODEL
    out_rows = D_MODEL // N_CHIPS

    def body(chip_ref, x_ref, t_ref, cat_ref, v_ref, g_out, g_w1, g_w2, l_out, l_w1, l_w2,
             dx1_ref, dcat_ref, dmix_ref, h2_ref, r_ref, da_ref, dff_ref, sums_ref,
             wout, w1, w2, a_scr, sem):
        i = pl.program_id(0)

        @pl.when(i == 0)
        def _():
            waits = _load_chip_blocks(chip_ref, g_out, l_out,
                                      [wout.at[pl.ds(out_rows * k, out_rows)] for k in range(N_CHIPS)], sem)
            waits += _load_chip_blocks(chip_ref, g_w1, l_w1, [w1.at[k] for k in range(N_CHIPS)], sem, N_CHIPS)
            waits += _load_chip_blocks(chip_ref, g_w2, l_w2, [w2.at[k] for k in range(N_CHIPS)], sem, 2 * N_CHIPS)
            for wait in waits:
                wait()
            sums_ref[...] = jnp.zeros_like(sums_ref)

        gate1, shift2, scale2 = v_ref[0:1, :], v_ref[1:2, :], v_ref[2:3, :]
        gate2, g_ffn, g_final = v_ref[3:4, :], v_ref[4:5, :], v_ref[5:6, :]

        mix = _dot(cat_ref[...], wout[...])
        x1 = x_ref[...] + gate1 * mix
        rstd2 = lax.rsqrt(_mean_last(x1 * x1) + EPS)
        xh2 = x1 * rstd2
        n2 = xh2 * g_ffn
        h2b = (n2 * (1.0 + scale2) + shift2).astype(MXU_DTYPE)
        h2_ref[...] = h2b
        ff = jnp.zeros((tm, D_MODEL), F32)
        for j in range(nj):
            a = _dot(h2b, w1[j])
            a_scr[j] = a
            relu = jnp.maximum(a, 0.0)
            rb = (relu * relu).astype(MXU_DTYPE)
            r_ref[:, D_MODEL * j:D_MODEL * (j + 1)] = rb
            ff = ff + _dot(rb, w2[j])
        x2 = x1 + gate2 * ff
        rstd3 = lax.rsqrt(_mean_last(x2 * x2) + EPS)
        xh3 = x2 * rstd3
        err = xh3 * g_final - t_ref[...]
        loss = 0.5 * _rowsum(_mean_last(err * err))
        dy = err * (1.0 / D_MODEL)
        dxh3 = dy * g_final
        dx2 = rstd3 * (dxh3 - xh3 * _mean_last(dxh3 * xh3))
        dffb = (dx2 * gate2).astype(MXU_DTYPE)
        dff_ref[...] = dffb
        dh2 = jnp.zeros((tm, D_MODEL), F32)
        for j in range(nj):
            dr = _dot_nt(dffb, w2[j])
            dab = (dr * (2.0 * jnp.maximum(a_scr[j], 0.0))).astype(MXU_DTYPE)
            da_ref[:, D_MODEL * j:D_MODEL * (j + 1)] = dab
            dh2 = dh2 + _dot_nt(dab, w1[j])
        dn2 = dh2 * (1.0 + scale2)
        dxh2 = dn2 * g_ffn
        dx1 = dx2 + rstd2 * (dxh2 - xh2 * _mean_last(dxh2 * xh2))
        dx1_ref[...] = dx1
        dmixb = (dx1 * gate1).astype(MXU_DTYPE)
        dmix_ref[...] = dmixb
        dcat_ref[...] = _dot_nt(dmixb, wout[...])

        sums_ref[0:1, :] += _rowsum(dh2)
        sums_ref[1:2, :] += _rowsum(dh2 * n2)
        sums_ref[2:3, :] += _rowsum(dx2 * ff)
        sums_ref[3:4, :] += _rowsum(dn2 * xh2)
        sums_ref[4:5, :] += _rowsum(dy * xh3)
        sums_ref[5:6, :] += _rowsum(dx1 * mix)
        sums_ref[6:7, :] += jnp.broadcast_to(loss, (1, D_MODEL))

    tok = lambda w: pl.BlockSpec((tm, w), lambda i, chip: (i, 0))
    return _hosted_call(
        body, None, name="trunk", grid=(seq // tm,), n_prefetch=1,
        out_shape=[jax.ShapeDtypeStruct((seq, D_MODEL), F32), jax.ShapeDtypeStruct((seq, D_MODEL), F32),
                   jax.ShapeDtypeStruct((seq, D_MODEL), MXU_DTYPE), jax.ShapeDtypeStruct((seq, D_MODEL), MXU_DTYPE),
                   jax.ShapeDtypeStruct((seq, D_FF), MXU_DTYPE), jax.ShapeDtypeStruct((seq, D_FF), MXU_DTYPE),
                   jax.ShapeDtypeStruct((seq, D_MODEL), MXU_DTYPE), jax.ShapeDtypeStruct((8, D_MODEL), F32)],
        in_specs=[tok(D_MODEL), tok(D_MODEL), tok(D_MODEL), _full((8, D_MODEL))] + [_any()] * 6,
        out_specs=[tok(D_MODEL), tok(D_MODEL), tok(D_MODEL), tok(D_MODEL), tok(D_FF), tok(D_FF), tok(D_MODEL),
                   _full((8, D_MODEL))],
        scratch_shapes=[pltpu.VMEM((D_MODEL, D_MODEL), MXU_DTYPE), pltpu.VMEM((nj, D_MODEL, D_MODEL), MXU_DTYPE),
                        pltpu.VMEM((nj, D_MODEL, D_MODEL), MXU_DTYPE), pltpu.VMEM((nj, tm, D_MODEL), F32),
                        pltpu.SemaphoreType.DMA((3 * N_CHIPS,))],
        semantics=("arbitrary",),
    )(chip_idx, x, target, cat, vecs, *gathered, *local)


def _mixer_bwd_kernel(proj, rope_tab, dcat, w_spatial, w_spatial_t, bias_full, sink_rows, dev_idx, comm=None):
    seq = proj.shape[0]
    per = MIXER_BLOCKS_PER_STEP
    steps = seq // (CHUNK * per)
    kv_col = KV_START // (2 * KV_WIDTH)

    def body(dev_ref, proj_ref, prev_ref, tab_ref, ptab_ref, dcat_ref, w_ref, wt_ref, bias_ref, sink_ref,
             dproj_ref, dw_out, db_ref, dsink_ref, carry, dw_ref):
        del dev_ref
        step = pl.program_id(0)

        @pl.when(step == 0)
        def _():
            carry[...] = jnp.zeros_like(carry)
            dw_ref[...] = jnp.zeros_like(dw_ref)
            db_ref[...] = jnp.zeros_like(db_ref)
            dsink_ref[...] = jnp.zeros_like(dsink_ref)

        for s in reversed(range(per)):
            rows = pl.ds(CHUNK * s, CHUNK)
            if s == 0:
                before, before_tab, first = prev_ref, ptab_ref, step == steps - 1
            else:
                before = proj_ref.at[pl.ds(CHUNK * (s - 1), CHUNK), pl.ds(KV_START, 2 * KV_WIDTH)]
                before_tab, first = tab_ref.at[pl.ds(CHUNK * (s - 1), CHUNK)], None
            one_block(proj_ref.at[rows], before, tab_ref.at[rows], before_tab, dcat_ref.at[rows], w_ref, wt_ref,
                      bias_ref, sink_ref, dproj_ref.at[rows], dw_ref, db_ref, dsink_ref, carry, first)

        @pl.when(step == steps - 1)
        def _():
            dw_out[...] = dw_ref[...].astype(dw_out.dtype)

    def one_block(proj_ref, prev_ref, tab_ref, ptab_ref, dcat_ref, w_ref, wt_ref, bias_ref, sink_ref,
                  dproj_ref, dw_ref, db_ref, dsink_ref, carry, first):
        wm, tril, triu = _masked_spatial(w_ref)
        lo, hi = _lane_masks((CHUNK, LANES))
        lane = lax.broadcasted_iota(jnp.int32, (CHUNK, LANES), 1)
        db = jnp.zeros((CHUNK, LANES), F32)
        for j in range(GMLP_GROUPS // 2):
            cols = slice(LANES * j, LANES * (j + 1))
            vcols = slice(GMLP_WIDTH + LANES * j, GMLP_WIDTH + LANES * (j + 1))
            zu, zv = proj_ref[:, cols], proj_ref[:, vcols]
            u, tu = _gelu_tanh(zu)
            vp, tv = _gelu_tanh(zv)
            sv = _sgu_forward_pair(wm, vp, j) + bias_ref[:, cols]
            dout = dcat_ref[:, cols]
            du = dout * sv
            dsv = dout * u
            dsv_lo, dsv_hi = jnp.where(lo, dsv, 0.0), jnp.where(hi, dsv, 0.0)
            lhs_t = jnp.concatenate([jnp.where(triu, wt_ref[2 * j], 0.0),
                                     jnp.where(triu, wt_ref[2 * j + 1], 0.0)], axis=1)
            dv = _dot(lhs_t, jnp.concatenate([dsv_lo, dsv_hi], axis=0))
            dw_ref[2 * j] += jnp.where(tril, _dot_nt(dsv_lo, vp), 0.0)
            dw_ref[2 * j + 1] += jnp.where(tril, _dot_nt(dsv_hi, vp), 0.0)
            db = db + (jnp.where(lane == 2 * j, jnp.sum(dsv_lo, axis=1, keepdims=True), 0.0)
                       + jnp.where(lane == 2 * j + 1, jnp.sum(dsv_hi, axis=1, keepdims=True), 0.0))
            dproj_ref[:, cols] = (du * _gelu_tanh_grad(zu, tu)).astype(dproj_ref.dtype)
            dproj_ref[:, vcols] = (dv * _gelu_tanh_grad(zv, tv)).astype(dproj_ref.dtype)
        db_ref[...] += db
        o = 2 * GMLP_WIDTH
        tab = tab_ref[...]
        q_r = _rope_apply(proj_ref[:, o:o + ATTN_WIDTH], tab, 1.0)
        k_cur = _rope_apply(proj_ref[:, o + ATTN_WIDTH:o + ATTN_WIDTH + KV_WIDTH], tab, 1.0)
        k_prev = _rope_apply(prev_ref[:, 0:KV_WIDTH], ptab_ref[...], 1.0)
        k_a = jnp.concatenate([k_prev, k_cur], axis=0)
        v_a = jnp.concatenate([prev_ref[:, KV_WIDTH:2 * KV_WIDTH],
                               proj_ref[:, o + ATTN_WIDTH + KV_WIDTH:o + ATTN_WIDTH + 2 * KV_WIDTH]], axis=0)
        k_b = pltpu.roll(k_a, HEAD_DIM, 1)
        v_b = pltpu.roll(v_a, HEAD_DIM, 1)
        bias_t = _attn_bias_t(first)
        lo2, _ = _lane_masks((2 * CHUNK, LANES))
        dout_b = dcat_ref[:, GMLP_WIDTH:GMLP_WIDTH + ATTN_WIDTH]
        dk_tot, dv_tot, dq_pairs = [], [], []
        for g in range(N_KV_HEADS):
            k_dup, v_dup = _group_dup(k_a, k_b, g, lo2), _group_dup(v_a, v_b, g, lo2)
            q_rows = _group_rows(q_r, g, lo, hi)
            do_rows = _group_rows(dout_b, g, lo, hi)
            p_t, p_sink = _attn_probs_t(k_dup, q_rows, bias_t, _sink_row(sink_ref, g))
            dp_t = _dot_nt(v_dup, do_rows)
            delta = jnp.sum(p_t * dp_t, axis=0, keepdims=True)
            ds_t = p_t * (dp_t - delta) * ATTN_SCALE
            dsink = -p_sink * delta
            for r in range(HEADS_PER_GROUP):
                h = HEADS_PER_GROUP * g + r
                dsink_ref[h:h + 1, :] += jnp.broadcast_to(
                    jnp.sum(dsink[:, LANES * r:LANES * (r + 1)], axis=1, keepdims=True), (1, LANES))
            dk_full = _dot(ds_t, q_rows)
            dv_full = _dot(p_t, do_rows)
            dk_tot.append(dk_full + pltpu.roll(dk_full, HEAD_DIM, 1))
            dv_tot.append(dv_full + pltpu.roll(dv_full, HEAD_DIM, 1))
            dq_t = _dot(k_dup.T, ds_t)
            dq_pairs += _pairs_from_rows(dq_t.T, lo)
        dk_all = jnp.where(lo2, dk_tot[0], dk_tot[1])
        dv_all = jnp.where(lo2, dv_tot[0], dv_tot[1])
        dk_cur = dk_all[CHUNK:, :] + carry[:, 0:KV_WIDTH]
        dv_cur = dv_all[CHUNK:, :] + carry[:, KV_WIDTH:2 * KV_WIDTH]
        carry[:, 0:KV_WIDTH] = dk_all[:CHUNK, :]
        carry[:, KV_WIDTH:2 * KV_WIDTH] = dv_all[:CHUNK, :]
        dq = _rope_apply(jnp.concatenate(dq_pairs, axis=1), tab, -1.0)
        dproj_ref[:, o:o + ATTN_WIDTH] = dq.astype(dproj_ref.dtype)
        dproj_ref[:, o + ATTN_WIDTH:o + ATTN_WIDTH + KV_WIDTH] = (
            _rope_apply(dk_cur, tab, -1.0).astype(dproj_ref.dtype))
        dproj_ref[:, o + ATTN_WIDTH + KV_WIDTH:o + ATTN_WIDTH + 2 * KV_WIDTH] = dv_cur.astype(dproj_ref.dtype)

    rev = lambda i: steps - 1 - i
    before = lambda i: jnp.maximum(per * rev(i) - 1, 0)
    slot = lambda shape: pl.BlockSpec((None,) + shape, lambda i, d: (d[0],) + (0,) * len(shape))
    return _hosted_call(
        body, comm, name="mixer_bwd", grid=(steps,), n_prefetch=1,
        out_shape=[jax.ShapeDtypeStruct((seq, IN_PROJ_WIDTH), MXU_DTYPE),
                   jax.ShapeDtypeStruct((N_DEV, GMLP_GROUPS, CHUNK, CHUNK), GRAD_COMM_DTYPE),
                   jax.ShapeDtypeStruct((N_DEV, CHUNK, LANES), F32),
                   jax.ShapeDtypeStruct((N_DEV, N_Q_HEADS, LANES), F32)],
        in_specs=[pl.BlockSpec((CHUNK * per, IN_PROJ_WIDTH), lambda i, d: (rev(i), 0)),
                  pl.BlockSpec((CHUNK, 2 * KV_WIDTH), lambda i, d: (before(i), kv_col)),
                  pl.BlockSpec((CHUNK * per, 3 * LANES), lambda i, d: (rev(i), 0)),
                  pl.BlockSpec((CHUNK, 3 * LANES), lambda i, d: (before(i), 0)),
                  pl.BlockSpec((CHUNK * per, D_MODEL), lambda i, d: (rev(i), 0)),
                  _full((GMLP_GROUPS, CHUNK, CHUNK)), _full((GMLP_GROUPS, CHUNK, CHUNK)),
                  _full((CHUNK, GMLP_WIDTH)), _full((N_Q_HEADS, LANES))],
        out_specs=[pl.BlockSpec((CHUNK * per, IN_PROJ_WIDTH), lambda i, d: (rev(i), 0)),
                   slot((GMLP_GROUPS, CHUNK, CHUNK)), slot((CHUNK, LANES)), slot((N_Q_HEADS, LANES))],
        scratch_shapes=[pltpu.VMEM((CHUNK, 2 * KV_WIDTH), F32), pltpu.VMEM((GMLP_GROUPS, CHUNK, CHUNK), F32)],
        semantics=("arbitrary",),
    )(dev_idx, proj, proj, rope_tab, rope_tab, dcat, w_spatial, w_spatial_t, bias_full, sink_rows)


def _in_proj_bwd_kernel(x, dx1, dproj, vecs, w_in_t, comm=None):
    seq = x.shape[0]
    tm = 512

    def body(x_ref, dx1_ref, dp_ref, v_ref, w_ref, gx_ref, sums_ref):
        @pl.when(pl.program_id(0) == 0)
        def _():
            sums_ref[...] = jnp.zeros_like(sums_ref)

        g_mix, scale1 = v_ref[0:1, :], v_ref[2:3, :]
        dh = _dot(dp_ref[...], w_ref[...])
        xv = x_ref[...]
        rstd = lax.rsqrt(_mean_last(xv * xv) + EPS)
        xh = xv * rstd
        dn1 = dh * (1.0 + scale1)
        dxh = dn1 * g_mix
        gx_ref[...] = dx1_ref[...] + rstd * (dxh - xh * _mean_last(dxh * xh))
        sums_ref[0:1, :] += _rowsum(dh)
        sums_ref[1:2, :] += _rowsum(dh * (xh * g_mix))
        sums_ref[2:3, :] += _rowsum(dn1 * xh)

    tok = lambda w: pl.BlockSpec((tm, w), lambda i: (i, 0))
    return _hosted_call(
        body, comm, name="in_proj_bwd", grid=(seq // tm,),
        out_shape=[jax.ShapeDtypeStruct((seq, D_MODEL), F32), jax.ShapeDtypeStruct((8, D_MODEL), F32)],
        in_specs=[tok(D_MODEL), tok(D_MODEL), tok(IN_PROJ_WIDTH), _full((8, D_MODEL)),
                  _full((IN_PROJ_WIDTH, D_MODEL))],
        out_specs=[tok(D_MODEL), _full((8, D_MODEL))],
        semantics=("arbitrary",),
    )(x, dx1, dproj, vecs, w_in_t)


class _GradTiles(NamedTuple):
    tm: int
    tn: int
    n_tiles: int
    chips_per_tile: int
    a_index: Callable
    b_index: Callable


def _weight_grad_kernel(a, b, c_idx, name, tiles, comm=None):
    seq = a.shape[0]
    tk = min(seq, 4096)
    nk = seq // tk
    tm, tn, n_tiles, per = tiles.tm, tiles.tn, tiles.n_tiles, tiles.chips_per_tile
    rows = tm // per

    def half(phase, c):
        return phase * c[0] + (1 - phase) * (1 - c[0])

    def body(c_ref, a_ref, b_ref, o_ref, acc, stage, landed, send_sems, recv_sems):
        del c_ref
        phase, t, kk = pl.program_id(0), pl.program_id(1), pl.program_id(2)
        x, y, c, _ = _mesh_place()

        def copy(tile):
            return pltpu.make_async_remote_copy(
                src_ref=stage.at[tile], dst_ref=landed.at[tile], send_sem=send_sems.at[tile],
                recv_sem=recv_sems.at[tile], device_id=(x, y, 1 - c), device_id_type=MESH)

        @pl.when(kk == 0)
        def _():
            acc[...] = jnp.zeros_like(acc)

        acc[...] += _dot_tn(a_ref[...], b_ref[...])

        @pl.when((kk == nk - 1) & (phase == 0))
        def _():
            stage[t] = acc[...].astype(stage.dtype)
            copy(t).start()

        @pl.when((kk == nk - 1) & (phase == 1))
        def _():
            copy(t).wait_recv()
            total = acc[...] + landed[t].astype(F32)
            for q in range(per):
                o_ref[q] = total[rows * q:rows * (q + 1)].astype(o_ref.dtype)

        @pl.when((kk == nk - 1) & (phase == 1) & (t == n_tiles - 1))
        def _():
            for tile in range(n_tiles):
                copy(tile).wait_send()

    out = _hosted_call(
        body, comm, name=name, grid=(2, n_tiles, nk), n_prefetch=1,
        out_shape=[jax.ShapeDtypeStruct((n_tiles * per, rows, tn), GRAD_COMM_DTYPE)],
        in_specs=[pl.BlockSpec((tk, tm), lambda p, t, k, c: (k, tiles.a_index(t, half(p, c)))),
                  pl.BlockSpec((tk, tn), lambda p, t, k, c: (k, tiles.b_index(t, half(p, c))))],
        out_specs=[pl.BlockSpec((per, rows, tn), lambda p, t, k, c: (p * t, 0, 0))],
        scratch_shapes=[pltpu.VMEM((tm, tn), F32), pltpu.VMEM((n_tiles, tm, tn), GRAD_COMM_DTYPE),
                        pltpu.VMEM((n_tiles, tm, tn), GRAD_COMM_DTYPE),
                        pltpu.SemaphoreType.DMA((n_tiles,)), pltpu.SemaphoreType.DMA((n_tiles,))],
        semantics=("arbitrary", "arbitrary", "arbitrary"),
    )(c_idx, a, b)
    return out[0] if comm is None else out


def _row_tile(rows, most=256, sublanes=16):
    return max(t for t in range(sublanes, most + 1, sublanes) if rows % t == 0)


def _adam_update(w, g, m, v):
    m_new = ADAM_B1 * m + (1.0 - ADAM_B1) * g
    v_new = ADAM_B2 * v + (1.0 - ADAM_B2) * (g * g)
    m_hat = m_new / (1.0 - ADAM_B1 ** ADAM_STEP)
    v_hat = v_new / (1.0 - ADAM_B2 ** ADAM_STEP)
    delta = -ADAM_LR * (m_hat / (jnp.sqrt(v_hat) + ADAM_EPS) + ADAM_WD * w)
    return delta, m_new, v_new


def _sum_chips_kernel(own, others, place, name):
    _, r, n = own.shape
    tr = _row_tile(r)

    def body(place_ref, own_ref, oth_ref, o_ref):
        del place_ref
        acc = own_ref[...].astype(F32)
        for k in range(N_CHIPS - 1):
            acc = acc + oth_ref[k].astype(F32)
        o_ref[...] = acc

    return pl.pallas_call(
        body, name=name, out_shape=jax.ShapeDtypeStruct((2, r, n), F32),
        grid_spec=pltpu.PrefetchScalarGridSpec(
            num_scalar_prefetch=1, grid=(r // tr,),
            in_specs=[pl.BlockSpec((None, tr, n), lambda i, p: (p[0], i, 0)),
                      pl.BlockSpec((N_CHIPS - 1, tr, n), lambda i, p: (0, i, 0))],
            out_specs=pl.BlockSpec((None, tr, n), lambda i, p: (p[1], i, 0))),
        compiler_params=_params("parallel"),
    )(place, own, others)


def _adam_kernel(w, g, m, v, name, after=()):
    r, n = w.shape
    by_columns = g.shape[1] == r
    tr, tn = _row_tile(g.shape[1], most=512), g.shape[2]

    def body(w_ref, g_ref, m_ref, v_ref, *rest):
        g_out, d_ref, mo_ref, vo_ref = rest[len(after):]
        gv = g_ref[...]
        g_out[...] = gv
        d_ref[...], mo_ref[...], vo_ref[...] = _adam_update(w_ref[...], gv, m_ref[...], v_ref[...])

    steps = g.shape[1] // tr
    spec = pl.BlockSpec((tr, tn), (lambda h, i: (i, h)) if by_columns else (lambda h, i: (h * steps + i, 0)))
    return pl.pallas_call(
        body, name=name, grid=(2, steps), out_shape=[jax.ShapeDtypeStruct((r, n), F32)] * 4,
        in_specs=[spec, pl.BlockSpec((None, tr, tn), lambda h, i: (h, i, 0)), spec, spec] + [_any()] * len(after),
        out_specs=[spec] * 4, compiler_params=_params("parallel", "parallel"),
    )(w, g, m, v, *after)


SMALL_PARAMS = ("b_ada", "g_mix", "g_ffn", "g_final", "b_spatial", "sinks", "w_spatial")


def _small_update_kernel(gathered, params):
    shapes = [params[nm][0].shape for nm in SMALL_PARAMS]

    def body(*refs):
        g_refs, refs = refs[:5], refs[5:]
        p_refs, refs = refs[:3 * len(SMALL_PARAMS)], refs[3 * len(SMALL_PARAMS):]
        loss_ref, o_refs = refs[0], refs[1:]

        def total(ref):
            acc = ref[0].astype(F32)
            for k in range(1, N_DEV):
                acc = acc + ref[k].astype(F32)
            return acc

        s1, s2, db, ds, dw = (total(r) for r in g_refs)
        loss_ref[...] = jnp.broadcast_to(s2[6:7, 0:1], loss_ref.shape)
        grads = {"b_ada": [s1[0:1], s1[1:2], s2[5:6], s2[0:1], s2[1:2], s2[2:3]], "g_mix": [s1[2:3]],
                 "g_ffn": [s2[3:4]], "g_final": [s2[4:5]], "b_spatial": [db.T[0:GMLP_GROUPS]],
                 "w_spatial": [dw]}
        lane = lax.broadcasted_iota(jnp.int32, (1, LANES), 1)
        sink_row = jnp.zeros((1, LANES), F32)
        for h in range(N_Q_HEADS):
            sink_row = sink_row + jnp.where(lane == h, ds[h:h + 1, :], 0.0)
        grads["sinks"] = [sink_row[:, 0:N_Q_HEADS]]
        for i, nm in enumerate(SMALL_PARAMS):
            w_ref, m_ref, v_ref = p_refs[3 * i:3 * i + 3]
            outs = o_refs[4 * i:4 * i + 4]
            width = grads[nm][0].shape[1]
            for k, g in enumerate(grads[nm]):
                cols = slice(width * k, width * (k + 1))
                upd = _adam_update(w_ref[:, cols], g, m_ref[:, cols], v_ref[:, cols])
                for o_ref, val in zip(outs, (g,) + upd):
                    o_ref[:, cols] = val

    flat = [a for nm in SMALL_PARAMS for a in params[nm]]
    out_shape = [jax.ShapeDtypeStruct((8, LANES), F32)]
    out_shape += [jax.ShapeDtypeStruct(s, F32) for s in shapes for _ in range(4)]
    outs = pl.pallas_call(
        body, name="small_update", grid=(1,), out_shape=out_shape,
        in_specs=[_full(g.shape) for g in gathered] + [_full(a.shape) for a in flat],
        out_specs=[_full(s.shape) for s in out_shape],
        compiler_params=_params("arbitrary"),
    )(*gathered, *flat)
    return {nm: outs[1 + 4 * i:5 + 4 * i] for i, nm in enumerate(SMALL_PARAMS)}, outs[0]


def _ada_update_kernel(act_t, dmod, w, m, v):
    r, n = w.shape
    tr = 256

    def body(a_ref, d_ref, w_ref, m_ref, v_ref, g_ref, dl_ref, mo_ref, vo_ref):
        g = _dot(a_ref[...], d_ref[...])
        g_ref[...] = g
        dl_ref[...], mo_ref[...], vo_ref[...] = _adam_update(w_ref[...], g, m_ref[...], v_ref[...])

    spec = pl.BlockSpec((tr, n), lambda i: (i, 0))
    return pl.pallas_call(
        body, name="ada_update", grid=(r // tr,), out_shape=[jax.ShapeDtypeStruct((r, n), F32)] * 4,
        in_specs=[pl.BlockSpec((tr, N_DEV), lambda i: (i, 0)), _full((N_DEV, n)), spec, spec, spec],
        out_specs=[spec] * 4, compiler_params=_params("parallel"),
    )(act_t, dmod, w, m, v)


def kernel(x, c, positions, w_ada, b_ada, g_mix, w_in, w_spatial, b_spatial, sinks, w_out, g_ffn, w_ff1, w_ff2, g_final, loss_target, m_w_ada, m_b_ada, m_g_mix, m_w_in, m_w_spatial, m_b_spatial, m_sinks, m_w_out, m_g_ffn, m_w_ff1, m_w_ff2, m_g_final, v_w_ada, v_b_ada, v_g_mix, v_w_in, v_w_spatial, v_b_spatial, v_sinks, v_w_out, v_g_ffn, v_w_ff1, v_w_ff2, v_g_final):
    xi, yi, ci = lax.axis_index("x"), lax.axis_index("y"), lax.axis_index("c")
    chip = 2 * xi + yi
    dev = 2 * chip + ci
    seq = x.shape[1]
    x2, tgt = x[0], loss_target[0]
    ada_cols = w_ada.shape[2]

    big = {"w_in": tuple(a[0].T for a in (w_in, m_w_in, v_w_in)),
           "w_out": (w_out[0], m_w_out[0], v_w_out[0]), "w_ff1": (w_ff1[0], m_w_ff1[0], v_w_ff1[0]),
           "w_ff2": (w_ff2[0], m_w_ff2[0], v_w_ff2[0])}

    def halves(nm, zero=None):
        r, n = big[nm][0].shape
        w = big[nm][0] if zero is None else big[nm][0] + zero
        return w.astype(WEIGHT_COMM_DTYPE).reshape(2, r // 2, n)

    chip_idx = chip.reshape(1).astype(jnp.int32)
    first_c, first_w = ([c], [False], [True]), ([halves("w_in"), halves("w_out")], [True, True], [True, False])
    c_started, w_started = _first_stage_start([first_c, first_w], "gather_first_start")
    zero = c_started[-1][0, 0]
    trunk_weights = ["w_out", "w_ff1", "w_ff2"]
    shards = [halves("w_out"), halves("w_ff1", zero), halves("w_ff2", zero)]
    inv_freq = ROPE_THETA ** (-jnp.arange(0, ROT_DIM, 2, dtype=F32) / ROT_DIM) + zero
    rope_tab = _rope_lane_tables(*_rope_angle_kernel(positions, inv_freq.reshape(ROT_DIM // 2, 1)))
    c_all, = _first_stage_finish(c_started, *first_c[1:], [rope_tab], "gather_c_wait")
    b_shard = lax.dynamic_slice(b_ada, (0, chip * ada_cols), (1, ada_cols))
    mod_part, act = _mod_kernel(c_all.reshape(N_DEV, D_MODEL), w_ada[0], b_shard)
    w_in_t, g_out, *ff_halves = _first_stage_finish(w_started, *first_w[1:], [mod_part], "gather_first_wait",
                                                    thru=shards[1:])
    shards = shards[:1] + ff_halves
    mod_all, w_in_t, g_out = _all_gather8([mod_part], "gather_mod", forward=[w_in_t, g_out])
    w_in_t = w_in_t.reshape(IN_PROJ_WIDTH, D_MODEL)
    mod_me = lax.dynamic_index_in_dim(mod_all[0::2], dev, axis=1, keepdims=False)
    mod_me = mod_me.reshape(N_MOD, D_MODEL)
    shift1, scale1, gate1, shift2, scale2, gate2 = (mod_me[k:k + 1] for k in range(N_MOD))

    zeros_row = jnp.zeros((1, D_MODEL), F32)
    vecs1 = jnp.concatenate([g_mix, shift1, scale1] + [zeros_row] * 5, axis=0)
    vecs2 = jnp.concatenate([gate1, shift2, scale2, gate2, g_ffn, g_final.reshape(1, D_MODEL)]
                            + [zeros_row] * 2, axis=0)
    bias_full = jnp.repeat(b_spatial[0].T, HEAD_DIM, axis=1)
    sink_rows = jnp.broadcast_to(sinks[0][:, None], (N_Q_HEADS, LANES))

    proj, hb, *staged = _in_proj_kernel(x2, vecs1, w_in_t, comm=_gather2d_first(shards[1:]))
    cat, *staged = _mixer_fwd_kernel(proj, rope_tab, w_spatial[0], bias_full, sink_rows,
                                     comm=_gather2d_second(staged, shards[1:]))
    staged = [g_out] + list(_gather_forward(staged, "gather_forward"))
    dx1, dcat, dmix, h2b, rb, dab, dffb, sums2 = _trunk_kernel(
        x2, tgt, cat, vecs2, chip_idx,
        [g.reshape((N_CHIPS,) + big[nm][0].shape) for nm, g in zip(trunk_weights, staged)],
        [s.reshape(big[nm][0].shape) for nm, s in zip(trunk_weights, shards)])

    c_idx = ci.reshape(1).astype(jnp.int32)
    place = jnp.stack([chip, ci]).astype(jnp.int32)
    half_d = D_MODEL // 2
    cs_ff2 = _weight_grad_kernel(rb, dffb, c_idx, "dw_ff2",
                                 _GradTiles(D_MODEL, half_d, N_CHIPS, 1, lambda t, h: t, lambda t, h: h))
    eighth = D_MODEL // 8
    cs_ff1, sc_ff2 = _weight_grad_kernel(
        h2b, dab, c_idx, "dw_ff1",
        _GradTiles(D_MODEL, half_d, N_CHIPS, 1, lambda t, h: 0, lambda t, h: 2 * t + h),
        comm=_scatter_job([cs_ff2], rows=(0, 6 * eighth)))
    cs_out, sc_ff2 = _weight_grad_kernel(
        cat, dmix, c_idx, "dw_out", _GradTiles(D_MODEL, half_d, 1, N_CHIPS, lambda t, h: 0, lambda t, h: h),
        comm=_scatter_job([cs_ff2], rows=(6 * eighth, eighth), into=[sc_ff2]))
    dproj, dw_spatial, db_lanes, dsink_rows, sc_ff2, sc_ff1, sc_out = _mixer_bwd_kernel(
        proj, rope_tab, dcat, w_spatial[0], w_spatial[0].transpose(0, 2, 1), bias_full, sink_rows,
        dev.reshape(1).astype(jnp.int32),
        comm=_merge_jobs(_scatter_job([cs_ff1, cs_out]),
                         _scatter_job([cs_ff2], rows=(7 * eighth, eighth), into=[sc_ff2])))
    totals = [_sum_chips_kernel(own, oth, place, "grad_sum_" + nm)
              for nm, own, oth in (("w_out", cs_out, sc_out), ("w_ff1", cs_ff1, sc_ff1), ("w_ff2", cs_ff2, sc_ff2))]
    small_slots = [db_lanes, dsink_rows, dw_spatial.reshape(N_DEV, GMLP_GROUPS * CHUNK, CHUNK)]
    cs_in, *rode = _weight_grad_kernel(
        dproj, hb, c_idx, "dw_in",
        _GradTiles(2 * W_IN_BLOCK, half_d, N_CHIPS // 2, 2, lambda t, h: t, lambda t, h: h),
        comm=_merge_jobs(_gather_job(small_slots), _share_job(totals)))
    small_stage1, shared = rode[:len(small_slots)], rode[len(small_slots):]
    scatter_in = _scatter_start(cs_in, "grad_to_chips_w_in_start")
    grad_x, sums1 = _in_proj_bwd_kernel(x2, dx1, dproj, vecs1 + scatter_in[-1][0:1, 0:1], w_in_t)
    cs_in, sc_in = _scatter_wait(scatter_in, sums1, "grad_to_chips_w_in_wait")
    total_in = _sum_chips_kernel(cs_in, sc_in, place, "grad_sum_w_in")
    gather_small = _gather_start([sums1, sums2], small_stage1, [total_in], cs_in, "gather_small_start")
    big_out = {}

    def update(nm, g, after=()):
        w, m, v = big[nm]
        outs = _adam_kernel(w, g, m, v, "adam_" + nm, after=after)
        big_out[nm] = tuple((t.T if nm == "w_in" else t)[None] for t in outs)
        return outs

    update("w_out", shared[0])
    covering = update("w_ff1", shared[1], after=gather_small[-1:])
    update("w_ff2", shared[2])
    *gathered, shared_in = _gather_wait(gather_small, 2, 1, covering[0], "gather_small_wait")
    update("w_in", shared_in)

    small = {"b_ada": (b_ada, m_b_ada, v_b_ada), "g_mix": (g_mix, m_g_mix, v_g_mix),
             "g_ffn": (g_ffn, m_g_ffn, v_g_ffn), "g_final": (g_final, m_g_final, v_g_final),
             "b_spatial": (b_spatial, m_b_spatial, v_b_spatial), "sinks": (sinks, m_sinks, v_sinks),
             "w_spatial": (w_spatial, m_w_spatial, v_w_spatial)}
    flat_shape = {"g_final": (1, D_MODEL), "b_spatial": (GMLP_GROUPS, CHUNK), "w_spatial": (GMLP_GROUPS * CHUNK, CHUNK)}
    small_out, loss_tile = _small_update_kernel(
        gathered, {nm: tuple(a.reshape(flat_shape.get(nm, a.shape)) for a in small[nm]) for nm in small})
    small_out = {nm: [o.reshape(small[nm][0].shape) for o in small_out[nm]] for nm in small}
    loss = loss_tile[0, 0]

    g1, g2 = gathered[0], gathered[1]
    dmod_all = jnp.concatenate([g1[:, 0], g1[:, 1], g2[:, 5], g2[:, 0], g2[:, 1], g2[:, 2]], axis=1)
    dmod_cols = lax.dynamic_slice(dmod_all, (0, chip * ada_cols), (N_DEV, ada_cols))
    ada = _ada_update_kernel(act.T, dmod_cols, w_ada[0], m_w_ada[0], v_w_ada[0])
    big_out["w_ada"] = tuple(t[None] for t in ada)

    order = ["w_ada", "b_ada", "g_mix", "w_in", "w_spatial", "b_spatial", "sinks", "w_out", "g_ffn",
             "w_ff1", "w_ff2", "g_final"]

    def leaf(nm, k):
        return big_out[nm][k] if nm in big_out else small_out[nm][k]

    outs = [loss, grad_x[None]]
    for k in range(4):
        outs += [leaf(nm, k) for nm in order]
    return tuple(outs)
```
